```python
import math
import jax, jax.numpy as jnp
from jax import lax
import numpy as np

D_MODEL = 1024
BATCH = 8
SEQ = 4096
DEPTH = 2

D_MIX = D_MODEL
A_HEADS = 4
A_HEAD_DIM = 128
A_WIDTH = A_HEADS * A_HEAD_DIM
CONV_K = 4
CHUNK = 64
B_Q_HEADS = 8
B_KV_HEADS = 2
B_HEAD_DIM = 64
B_GROUP = B_Q_HEADS // B_KV_HEADS
B_WIDTH = B_Q_HEADS * B_HEAD_DIM
B_KV_WIDTH = B_KV_HEADS * B_HEAD_DIM
WINDOW = 128
BLOCK = 128
IN_SIZES = (A_WIDTH, A_WIDTH, A_WIDTH, A_WIDTH, A_HEADS, A_HEADS,
            B_WIDTH, B_KV_WIDTH, B_KV_WIDTH, B_WIDTH)
IN_COLS = sum(IN_SIZES)
DEEPNORM_ALPHA = (2 * DEPTH) ** 0.25
DEEPNORM_BETA = (8 * DEPTH) ** -0.25
LN_EPS = 1e-5
RMS_EPS = 1e-6
L2_EPS = 1e-6

kernel_name = "hybrid_deltanet_swa_sink_alibi_deepnorm"


def _alibi_slopes(n_heads):
    return jnp.asarray([2.0 ** (-8.0 * (h + 1) / n_heads) for h in range(n_heads)], dtype=jnp.float32)


def _layernorm(x, g, b):
    xf = x.astype(jnp.float32)
    mu = jnp.mean(xf, axis=-1, keepdims=True)
    var = jnp.mean(jnp.square(xf - mu), axis=-1, keepdims=True)
    y = (xf - mu) * lax.rsqrt(var + LN_EPS) * g.astype(jnp.float32) + b.astype(jnp.float32)
    return y.astype(x.dtype)


def _l2norm(t):
    return t * lax.rsqrt(jnp.sum(jnp.square(t), axis=-1, keepdims=True) + L2_EPS)


def _short_conv(x, w):
    c = x.shape[-1]
    return lax.conv_general_dilated(
        x, w[:, None, :].astype(x.dtype), window_strides=(1,), padding=[(CONV_K - 1, 0)],
        dimension_numbers=('NWC', 'WIO', 'NWC'), feature_group_count=c)


def _gated_delta_rule(q, k, v, g, beta):
    bsz, t_len, h, dk = q.shape
    dv = v.shape[-1]
    n = t_len // CHUNK

    def chunks(t):
        t = t.reshape((bsz, n, CHUNK, h) + t.shape[3:])
        return jnp.moveaxis(t, 3, 1)

    q, k, v, g, beta = chunks(q), chunks(k), chunks(v), chunks(g), chunks(beta)
    g = jnp.cumsum(g, axis=-1)
    causal = jnp.tril(jnp.ones((CHUNK, CHUNK), dtype=bool))
    strict = jnp.tril(jnp.ones((CHUNK, CHUNK), dtype=bool), -1)
    decay = jnp.exp(jnp.where(causal, g[..., :, None] - g[..., None, :], -jnp.inf))
    k_beta = k * beta[..., None]
    a_mat = jnp.where(strict, jnp.einsum('bhncd,bhnsd->bhncs', k_beta, k) * decay, 0.0)
    eye = jnp.eye(CHUNK, dtype=a_mat.dtype)
    t_mat = lax.linalg.triangular_solve(eye + a_mat, jnp.broadcast_to(eye, a_mat.shape),
                                        left_side=True, lower=True)
    u = jnp.einsum('bhncs,bhnse->bhnce', t_mat, v * beta[..., None])
    w = jnp.einsum('bhncs,bhnsd->bhncd', t_mat, k_beta * jnp.exp(g)[..., None])
    qk = jnp.where(causal, jnp.einsum('bhncd,bhnsd->bhncs', q, k) * decay, 0.0)
    q_dec = q * jnp.exp(g)[..., None]
    k_dec = k * jnp.exp(g[..., -1:] - g)[..., None]
    g_tot = jnp.exp(g[..., -1])
    xs = tuple(jnp.moveaxis(t, 2, 0) for t in (q_dec, k_dec, u, w, qk, g_tot))

    def step(s, inp):
        q_c, k_c, u_c, w_c, qk_c, gt = inp
        v_new = u_c - jnp.einsum('bhcd,bhde->bhce', w_c, s)
        o = jnp.einsum('bhcd,bhde->bhce', q_c, s) + jnp.einsum('bhcs,bhse->bhce', qk_c, v_new)
        s = s * gt[..., None, None] + jnp.einsum('bhcd,bhce->bhde', k_c, v_new)
        return s, o

    s0 = jnp.zeros((bsz, h, dk, dv), jnp.float32)
    _, o = lax.scan(step, s0, xs)
    o = jnp.moveaxis(o, 0, 2)
    return jnp.moveaxis(o, 1, 3).reshape(bsz, t_len, h, dv)


def _deltanet_group(q, k, v, z, b, a, conv_w, a_log, dt_bias, norm_w):
    bsz, t_len, _ = q.shape
    qkv = jax.nn.silu(_short_conv(jnp.concatenate([q, k, v], axis=-1), conv_w))
    q, k, v = jnp.split(qkv, 3, axis=-1)
    heads = lambda t: t.reshape(bsz, t_len, A_HEADS, A_HEAD_DIM).astype(jnp.float32)
    q = _l2norm(heads(q)) * (A_HEAD_DIM ** -0.5)
    k = _l2norm(heads(k))
    v = heads(v)
    beta = jax.nn.sigmoid(b.astype(jnp.float32))
    g = -jnp.exp(a_log.astype(jnp.float32)) * jax.nn.softplus(
        a.astype(jnp.float32) + dt_bias.astype(jnp.float32))
    o = _gated_delta_rule(q, k, v, g, beta)
    o = o * lax.rsqrt(jnp.mean(jnp.square(o), axis=-1, keepdims=True) + RMS_EPS) * norm_w.astype(jnp.float32)
    return o.reshape(bsz, t_len, A_WIDTH).astype(z.dtype) * jax.nn.silu(z)


def _swa_group(q, k, v, z, sinks):
    bsz, t_len, _ = q.shape
    n = t_len // BLOCK
    q = q.reshape(bsz, n, BLOCK, B_KV_HEADS, B_GROUP, B_HEAD_DIM)
    k = k.reshape(bsz, t_len, B_KV_HEADS, B_HEAD_DIM)
    v = v.reshape(bsz, t_len, B_KV_HEADS, B_HEAD_DIM)

    def band(t):
        prev = jnp.pad(t, ((0, 0), (BLOCK, 0), (0, 0), (0, 0)))[:, :t_len]
        shp = (bsz, n, BLOCK, B_KV_HEADS, B_HEAD_DIM)
        return jnp.concatenate([prev.reshape(shp), t.reshape(shp)], axis=2)

    kb, vb = band(k), band(v)
    s = jnp.einsum('bnqhgd,bnshd->bhgnqs', q, kb).astype(jnp.float32) * (B_HEAD_DIM ** -0.5)
    q_idx = jnp.arange(BLOCK)[:, None]
    s_idx = jnp.arange(2 * BLOCK)[None, :]
    dist = q_idx + BLOCK - s_idx
    key_pos = jnp.arange(n)[:, None] * BLOCK - BLOCK + jnp.arange(2 * BLOCK)[None, :]
    mask = ((dist >= 0) & (dist < WINDOW))[None] & (key_pos >= 0)[:, None, :]
    slopes = _alibi_slopes(B_Q_HEADS).reshape(B_KV_HEADS, B_GROUP)
    s = s - slopes[:, :, None, None, None] * dist.astype(jnp.float32)
    s = jnp.where(mask, s, -jnp.inf)
    sink = sinks.astype(jnp.float32).reshape(B_KV_HEADS, B_GROUP)[:, :, None, None]
    m = jnp.maximum(jnp.max(s, axis=-1), sink)
    p = jnp.exp(s - m[..., None])
    p = p / (jnp.sum(p, axis=-1, keepdims=True) + jnp.exp(sink - m)[..., None])
    o = jnp.einsum('bhgnqs,bnshd->bnqhgd', p.astype(vb.dtype), vb)
    return o.reshape(bsz, t_len, B_WIDTH) * jax.nn.silu(z)


def _layer(x, w_in, conv_w, a_log, dt_bias, norm_w, sinks, w_out, ln_g, ln_b):
    h = jnp.einsum('btd,dc->btc', x, w_in)
    offsets = [sum(IN_SIZES[:i]) for i in range(1, len(IN_SIZES))]
    qa, ka, va, za, ba, aa, qb, kb, vb, zb = jnp.split(h, offsets, axis=-1)
    ya = _deltanet_group(qa, ka, va, za, ba, aa, conv_w, a_log, dt_bias, norm_w)
    yb = _swa_group(qb, kb, vb, zb, sinks)
    y = jnp.einsum('btc,cd->btd', jnp.concatenate([ya, yb], axis=-1), w_out)
    return _layernorm(DEEPNORM_ALPHA * x + y, ln_g, ln_b)


def _fwd_setup_inputs(seed: int = 0) -> dict:
    key = jax.random.key(seed)
    ks = jax.random.split(key, 10)
    x = jax.random.normal(ks[0], (BATCH, SEQ, D_MODEL), jnp.float32)
    col_scale = np.concatenate([
        np.ones(2 * A_WIDTH), np.full(A_WIDTH, DEEPNORM_BETA), np.ones(A_WIDTH + 2 * A_HEADS),
        np.ones(B_WIDTH + B_KV_WIDTH), np.full(B_KV_WIDTH, DEEPNORM_BETA), np.ones(B_WIDTH)]).astype(np.float32)
    w_in = jax.random.normal(ks[1], (DEPTH, D_MODEL, IN_COLS), jnp.float32) * (D_MODEL ** -0.5) * jnp.asarray(col_scale)
    conv_w = jax.random.normal(ks[2], (DEPTH, CONV_K, 3 * A_WIDTH), jnp.float32) * (CONV_K ** -0.5)
    a_log = jnp.log(jax.random.uniform(ks[3], (DEPTH, A_HEADS), jnp.float32, 1.0, 16.0))
    dt = jnp.exp(jax.random.uniform(ks[4], (DEPTH, A_HEADS), jnp.float32, math.log(1e-3), math.log(1e-1)))
    dt_bias = dt + jnp.log(-jnp.expm1(-dt))
    norm_w = 1.0 + 0.02 * jax.random.normal(ks[5], (DEPTH, A_HEAD_DIM), jnp.float32)
    sinks = 0.5 * jax.random.normal(ks[6], (DEPTH, B_Q_HEADS), jnp.float32)
    w_out = jax.random.normal(ks[7], (DEPTH, D_MIX, D_MODEL), jnp.float32) * (D_MIX ** -0.5) * DEEPNORM_BETA
    ln_g = 1.0 + 0.02 * jax.random.normal(ks[8], (DEPTH, D_MODEL), jnp.float32)
    ln_b = 0.02 * jax.random.normal(ks[9], (DEPTH, D_MODEL), jnp.float32)
    return {"x": x, "w_in": w_in, "conv_w": conv_w, "a_log": a_log, "dt_bias": dt_bias,
            "norm_w": norm_w, "sinks": sinks, "w_out": w_out, "ln_g": ln_g, "ln_b": ln_b}


def _fwd_reference(x, w_in, conv_w, a_log, dt_bias, norm_w, sinks, w_out, ln_g, ln_b):
    for l in range(DEPTH):
        x = _layer(x, w_in[l], conv_w[l], a_log[l], dt_bias[l], norm_w[l], sinks[l],
                   w_out[l], ln_g[l], ln_b[l])
    return x


import jax as _jax
import jax.numpy as _jnp

TWIN_FORMAT = 'train_step'
FWD_PARAMS = ['x', 'w_in', 'conv_w', 'a_log', 'dt_bias', 'norm_w', 'sinks', 'w_out', 'ln_g', 'ln_b']
TWIN_WEIGHTS = ['w_in', 'conv_w', 'a_log', 'dt_bias', 'norm_w', 'sinks', 'w_out', 'ln_g', 'ln_b']
TWIN_DIFF_INPUT = 'x'
TWIN_INPUTS = ['x', 'w_in', 'conv_w', 'a_log', 'dt_bias', 'norm_w', 'sinks', 'w_out', 'ln_g', 'ln_b', 'loss_target', 'm_w_in', 'm_conv_w', 'm_a_log', 'm_dt_bias', 'm_norm_w', 'm_sinks', 'm_w_out', 'm_ln_g', 'm_ln_b', 'v_w_in', 'v_conv_w', 'v_a_log', 'v_dt_bias', 'v_norm_w', 'v_sinks', 'v_w_out', 'v_ln_g', 'v_ln_b']
TWIN_OUTPUTS = ['loss', 'grad_x', 'grad_w_in', 'grad_conv_w', 'grad_a_log', 'grad_dt_bias', 'grad_norm_w', 'grad_sinks', 'grad_w_out', 'grad_ln_g', 'grad_ln_b', 'delta_w_in', 'delta_conv_w', 'delta_a_log', 'delta_dt_bias', 'delta_norm_w', 'delta_sinks', 'delta_w_out', 'delta_ln_g', 'delta_ln_b', 'new_m_w_in', 'new_m_conv_w', 'new_m_a_log', 'new_m_dt_bias', 'new_m_norm_w', 'new_m_sinks', 'new_m_w_out', 'new_m_ln_g', 'new_m_ln_b', 'new_v_w_in', 'new_v_conv_w', 'new_v_a_log', 'new_v_dt_bias', 'new_v_norm_w', 'new_v_sinks', 'new_v_w_out', 'new_v_ln_g', 'new_v_ln_b']
TWIN_LEAF_KINDS = {'loss': 'loss', 'grad_x': 'grad_x', 'grad_w_in': 'grad_w', 'grad_conv_w': 'grad_w', 'grad_a_log': 'grad_w', 'grad_dt_bias': 'grad_w', 'grad_norm_w': 'grad_w', 'grad_sinks': 'grad_w', 'grad_w_out': 'grad_w', 'grad_ln_g': 'grad_w', 'grad_ln_b': 'grad_w', 'delta_w_in': 'delta_w', 'delta_conv_w': 'delta_w', 'delta_a_log': 'delta_w', 'delta_dt_bias': 'delta_w', 'delta_norm_w': 'delta_w', 'delta_sinks': 'delta_w', 'delta_w_out': 'delta_w', 'delta_ln_g': 'delta_w', 'delta_ln_b': 'delta_w', 'new_m_w_in': 'new_m', 'new_m_conv_w': 'new_m', 'new_m_a_log': 'new_m', 'new_m_dt_bias': 'new_m', 'new_m_norm_w': 'new_m', 'new_m_sinks': 'new_m', 'new_m_w_out': 'new_m', 'new_m_ln_g': 'new_m', 'new_m_ln_b': 'new_m', 'new_v_w_in': 'new_v', 'new_v_conv_w': 'new_v', 'new_v_a_log': 'new_v', 'new_v_dt_bias': 'new_v', 'new_v_norm_w': 'new_v', 'new_v_sinks': 'new_v', 'new_v_w_out': 'new_v', 'new_v_ln_g': 'new_v', 'new_v_ln_b': 'new_v'}


def _forward(args):
    return _fwd_reference(*[args[k] for k in FWD_PARAMS])


def _output_shape():
    def fwd():
        inp = _fwd_setup_inputs(0)
        return _fwd_reference(*[inp[k] for k in FWD_PARAMS])
    out = _jax.eval_shape(fwd)
    return out.shape, out.dtype

N_MICROBATCH = 1
ADAM_LR = 0.001
ADAM_B1 = 0.9
ADAM_B2 = 0.999
ADAM_EPS = 1e-08
ADAM_WD = 0.01
ADAM_STEP = 10
PER_EXAMPLE_BATCH_AXIS = {'x': 0, 'loss_target': 0}
SHARED_INPUTS = []
_WEIGHT_DTYPES = {'w_in': _jnp.float32, 'conv_w': _jnp.float32, 'a_log': _jnp.float32, 'dt_bias': _jnp.float32, 'norm_w': _jnp.float32, 'sinks': _jnp.float32, 'w_out': _jnp.float32, 'ln_g': _jnp.float32, 'ln_b': _jnp.float32}
MOMENT_SCALE = {'w_in': 3.365811e-02, 'conv_w': 3.007763e-02, 'a_log': 1.763462e-01, 'dt_bias': 1.739213e-01, 'norm_w': 7.624680e-02, 'sinks': 5.927547e-03, 'w_out': 5.421767e-02, 'ln_g': 2.266173e+01, 'ln_b': 9.784442e-01}


def _to_microbatches(a, axis):
    t = _jnp.moveaxis(a, axis, 0)
    t = t.reshape((N_MICROBATCH, t.shape[0] // N_MICROBATCH) + t.shape[1:])
    return _jnp.moveaxis(t, 1, axis + 1)


def setup_inputs(seed: int = 0) -> dict:
    inp = _fwd_setup_inputs(seed)
    key = _jax.random.fold_in(_jax.random.key(seed), 7919)
    shape, _ = _output_shape()
    out = dict(inp)
    out["loss_target"] = _jax.random.normal(_jax.random.fold_in(key, 0), shape, _jnp.float32)
    for i, name in enumerate(TWIN_WEIGHTS):
        w = inp[name].astype(_jnp.float32)
        if MOMENT_SCALE is None:
            s = _jnp.sqrt(_jnp.mean(_jnp.square(w)) + 1e-30)
        else:
            s = MOMENT_SCALE[name]
        km, kv = _jax.random.split(_jax.random.fold_in(key, i + 1))
        out[name] = w
        out["m_" + name] = s * _jax.random.normal(km, w.shape, _jnp.float32)
        out["v_" + name] = (s * s) * _jax.random.uniform(kv, w.shape, _jnp.float32, 0.5, 1.5)
    if N_MICROBATCH > 1:
        for name, axis in PER_EXAMPLE_BATCH_AXIS.items():
            out[name] = _to_microbatches(out[name], axis)
    return {'x': out['x'], 'w_in': out['w_in'], 'conv_w': out['conv_w'], 'a_log': out['a_log'], 'dt_bias': out['dt_bias'], 'norm_w': out['norm_w'], 'sinks': out['sinks'], 'w_out': out['w_out'], 'ln_g': out['ln_g'], 'ln_b': out['ln_b'], 'loss_target': out['loss_target'], 'm_w_in': out['m_w_in'], 'm_conv_w': out['m_conv_w'], 'm_a_log': out['m_a_log'], 'm_dt_bias': out['m_dt_bias'], 'm_norm_w': out['m_norm_w'], 'm_sinks': out['m_sinks'], 'm_w_out': out['m_w_out'], 'm_ln_g': out['m_ln_g'], 'm_ln_b': out['m_ln_b'], 'v_w_in': out['v_w_in'], 'v_conv_w': out['v_conv_w'], 'v_a_log': out['v_a_log'], 'v_dt_bias': out['v_dt_bias'], 'v_norm_w': out['v_norm_w'], 'v_sinks': out['v_sinks'], 'v_w_out': out['v_w_out'], 'v_ln_g': out['v_ln_g'], 'v_ln_b': out['v_ln_b']}


def _loss(weights, diff, rest, loss_target):
    with _jax.named_scope("forward"):
        args = {**rest, TWIN_DIFF_INPUT: diff, **{k: w.astype(_WEIGHT_DTYPES[k]) for k, w in weights.items()}}
        y = _forward(args)
    with _jax.named_scope("loss_head"):
        err = _jnp.square(y.astype(_jnp.float32) - loss_target)
        return 0.5 * _jnp.sum(_jnp.mean(err, axis=-1)) if err.ndim else 0.5 * err


def _adamw(w, g, m, v):
    m = ADAM_B1 * m + (1.0 - ADAM_B1) * g
    v = ADAM_B2 * v + (1.0 - ADAM_B2) * _jnp.square(g)
    m_hat = m / (1.0 - ADAM_B1 ** ADAM_STEP)
    v_hat = v / (1.0 - ADAM_B2 ** ADAM_STEP)
    delta = -ADAM_LR * (m_hat / (_jnp.sqrt(v_hat) + ADAM_EPS) + ADAM_WD * w)
    return delta, m, v


def reference(x, w_in, conv_w, a_log, dt_bias, norm_w, sinks, w_out, ln_g, ln_b, loss_target, m_w_in, m_conv_w, m_a_log, m_dt_bias, m_norm_w, m_sinks, m_w_out, m_ln_g, m_ln_b, v_w_in, v_conv_w, v_a_log, v_dt_bias, v_norm_w, v_sinks, v_w_out, v_ln_g, v_ln_b):
    given = dict(x=x, w_in=w_in, conv_w=conv_w, a_log=a_log, dt_bias=dt_bias, norm_w=norm_w, sinks=sinks, w_out=w_out, ln_g=ln_g, ln_b=ln_b, loss_target=loss_target, m_w_in=m_w_in, m_conv_w=m_conv_w, m_a_log=m_a_log, m_dt_bias=m_dt_bias, m_norm_w=m_norm_w, m_sinks=m_sinks, m_w_out=m_w_out, m_ln_g=m_ln_g, m_ln_b=m_ln_b, v_w_in=v_w_in, v_conv_w=v_conv_w, v_a_log=v_a_log, v_dt_bias=v_dt_bias, v_norm_w=v_norm_w, v_sinks=v_sinks, v_w_out=v_w_out, v_ln_g=v_ln_g, v_ln_b=v_ln_b)
    weights = {n: given[n] for n in TWIN_WEIGHTS}
    shared = {n: given[n] for n in SHARED_INPUTS}
    per_example = {n: given[n] for n in ['x']}
    grad_fn = _jax.value_and_grad(_loss, argnums=(0, 1))

    def one_microbatch(ex, loss_target):
        ex = dict(ex)
        diff = ex.pop(TWIN_DIFF_INPUT)
        return grad_fn(weights, diff, {**shared, **ex}, loss_target)

    if N_MICROBATCH == 1:
        loss, (grad_w, grad_x) = one_microbatch(per_example, given["loss_target"])
    else:
        def body(carry, xs):
            loss_sum, grad_sum = carry
            l_k, (gw_k, gx_k) = one_microbatch(xs[0], xs[1])
            with _jax.named_scope("update"):
                return (loss_sum + l_k, _jax.tree.map(_jnp.add, grad_sum, gw_k)), gx_k

        init = (_jnp.zeros((), _jnp.float32), _jax.tree.map(_jnp.zeros_like, weights))
        (loss, grad_w), grad_x = _jax.lax.scan(body, init, (per_example, given["loss_target"]))
    with _jax.named_scope("update"):
        delta_w, new_m, new_v = {}, {}, {}
        for n in TWIN_WEIGHTS:
            delta_w[n], new_m[n], new_v[n] = _adamw(weights[n], grad_w[n], given["m_" + n], given["v_" + n])
    return (loss, grad_x, *[grad_w[n] for n in TWIN_WEIGHTS], *[delta_w[n] for n in TWIN_WEIGHTS],
            *[new_m[n] for n in TWIN_WEIGHTS], *[new_v[n] for n in TWIN_WEIGHTS])
```

```python
import functools
import math

import jax
import jax.numpy as jnp
from jax import lax
from jax.experimental import pallas as pl
from jax.experimental.pallas import tpu as pltpu

F32 = jnp.float32
BF16 = jnp.bfloat16
MM_DTYPE = BF16

N_DEV = 8
D_MODEL = 1024
DEPTH = 2
A_HEADS = 4
A_HEAD_DIM = 128
A_WIDTH = 512
CONV_K = 4
CHUNK = 64
SUPER = 256
B_Q_HEADS = 8
B_KV_HEADS = 2
B_HEAD_DIM = 64
B_GROUP = 4
B_WIDTH = 512
B_KV_WIDTH = 128
WINDOW = 128
BLOCK = 128
IN_COLS = 3336
SHARD_COLS = IN_COLS // N_DEV
DEEPNORM_ALPHA = (2 * DEPTH) ** 0.25
LN_EPS = 1e-5
RMS_EPS = 1e-6
L2_EPS = 1e-6
ADAM_LR, ADAM_B1, ADAM_B2, ADAM_EPS, ADAM_WD, ADAM_STEP = 0.001, 0.9, 0.999, 1e-08, 0.01, 10

LANE = 128
L_QKV, L_ZA, L_QB, L_ZB, L_KB, L_VB, L_BA = 0, 1536, 2048, 2560, 3072, 3200, 3328
L_COLS = 3456
R_WIN = DEPTH * D_MODEL * SHARD_COLS // LANE
R_WOUT = DEPTH * 128 * D_MODEL // LANE
R_CONV = 16
R_SMALL = 40
R_GRAD = 9216
R_AG = R_WIN + R_WOUT + 2 * R_CONV
SMALL_SIZES = (("a_log", 4), ("dt_bias", 4), ("norm_w", 128), ("sinks", 8), ("ln_g", 1024), ("ln_b", 1024))
VMEM_LIMIT = 48 * 1024 * 1024


def _cparams(sem=None):
    return pltpu.CompilerParams(dimension_semantics=sem, vmem_limit_bytes=VMEM_LIMIT)


def _mm(a, b):
    return jnp.dot(a.astype(MM_DTYPE), b.astype(MM_DTYPE), preferred_element_type=F32)


def _mm_nt(a, b):
    return lax.dot_general(a.astype(MM_DTYPE), b.astype(MM_DTYPE), (((1,), (1,)), ((), ())),
                           preferred_element_type=F32)


def _mm_tn(a, b):
    return lax.dot_general(a.astype(MM_DTYPE), b.astype(MM_DTYPE), (((0,), (0,)), ((), ())),
                           preferred_element_type=F32)


def _hp(a, b):
    return jnp.dot(a, b, precision=lax.Precision.HIGHEST, preferred_element_type=F32)


def _silu(x):
    return x * jax.nn.sigmoid(x)


def _softplus(x):
    return jnp.maximum(x, 0.0) + jnp.log1p(jnp.exp(-jnp.abs(x)))


def _matmul(a, b, *, form, tm, tn, tk, name, add=None, add_scale=1.0):
    if form == "nn":
        (m, kk), (_, n) = a.shape, b.shape
        a_spec = pl.BlockSpec((tm, tk), lambda i, j, k: (i, k))
        b_spec = pl.BlockSpec((tk, tn), lambda i, j, k: (k, j))
        dn = (((1,), (0,)), ((), ()))
    elif form == "nt":
        (m, kk), (n, _) = a.shape, b.shape
        a_spec = pl.BlockSpec((tm, tk), lambda i, j, k: (i, k))
        b_spec = pl.BlockSpec((tn, tk), lambda i, j, k: (j, k))
        dn = (((1,), (1,)), ((), ()))
    else:
        (kk, m), (_, n) = a.shape, b.shape
        a_spec = pl.BlockSpec((tk, tm), lambda i, j, k: (k, i))
        b_spec = pl.BlockSpec((tk, tn), lambda i, j, k: (k, j))
        dn = (((0,), (0,)), ((), ()))
    assert m % tm == 0 and n % tn == 0 and kk % tk == 0, (name, m, n, kk)
    has_add = add is not None

    def body(*refs):
        if has_add:
            a_ref, b_ref, add_ref, o_ref = refs
        else:
            a_ref, b_ref, o_ref = refs
        k = pl.program_id(2)
        p = lax.dot_general(a_ref[...].astype(MM_DTYPE), b_ref[...].astype(MM_DTYPE), dn,
                            preferred_element_type=F32)

        @pl.when(k == 0)
        def _():
            if has_add:
                o_ref[...] = p + add_scale * add_ref[...]
            else:
                o_ref[...] = p

        @pl.when(k > 0)
        def _():
            o_ref[...] += p

    in_specs = [a_spec, b_spec]
    args = [a, b]
    if has_add:
        in_specs.append(pl.BlockSpec((tm, tn), lambda i, j, k: (i, j)))
        args.append(add)
    return pl.pallas_call(
        body, name=name, grid=(m // tm, n // tn, kk // tk), in_specs=in_specs,
        out_specs=pl.BlockSpec((tm, tn), lambda i, j, k: (i, j)),
        out_shape=jax.ShapeDtypeStruct((m, n), F32),
        compiler_params=_cparams(("parallel", "parallel", "arbitrary")),
    )(*args)


def _shift_down(x, k, row):
    return jnp.where(row >= k, pltpu.roll(x, k, 0), 0.0)


def _shift_up(x, k, row, t_len):
    return jnp.where(row < t_len - k, pltpu.roll(x, t_len - k, 0), 0.0)


def _conv_slab(x, w, row):
    return (w[3:4] * x + w[2:3] * _shift_down(x, 1, row) + w[1:2] * _shift_down(x, 2, row)
            + w[0:1] * _shift_down(x, 3, row))


def _prep_fwd(h, conv_w, *, name):
    t_len = h.shape[0]

    def body(x_ref, w_ref, o_ref):
        s = pl.program_id(0)
        row = lax.broadcasted_iota(jnp.int32, (t_len, LANE), 0)
        y = _silu(_conv_slab(x_ref[...], w_ref[...], row))
        rs = lax.rsqrt(jnp.sum(y * y, axis=-1, keepdims=True) + L2_EPS)
        scale = jnp.where(s < A_HEADS, A_HEAD_DIM ** -0.5, 1.0)
        o_ref[...] = jnp.where(s < 2 * A_HEADS, y * rs * scale, y)

    return pl.pallas_call(
        body, name=name, grid=(12,),
        in_specs=[pl.BlockSpec((t_len, LANE), lambda s: (0, s)), pl.BlockSpec((8, LANE), lambda s: (0, s))],
        out_specs=pl.BlockSpec((t_len, LANE), lambda s: (0, s)),
        out_shape=jax.ShapeDtypeStruct((t_len, 3 * A_WIDTH), F32),
        compiler_params=_cparams(("parallel",)),
    )(h, conv_w)


def _prep_bwd(h, conv_w, d_out, *, name):
    t_len = h.shape[0]

    def body(x_ref, w_ref, g_ref, dx_ref, dw_ref):
        s = pl.program_id(0)
        row = lax.broadcasted_iota(jnp.int32, (t_len, LANE), 0)
        x = x_ref[...]
        w = w_ref[...]
        c = _conv_slab(x, w, row)
        sg = jax.nn.sigmoid(c)
        y = c * sg
        g = g_ref[...]
        rs = lax.rsqrt(jnp.sum(y * y, axis=-1, keepdims=True) + L2_EPS)
        scale = jnp.where(s < A_HEADS, A_HEAD_DIM ** -0.5, 1.0)
        dy_n = scale * (rs * g - y * (rs * rs * rs) * jnp.sum(g * y, axis=-1, keepdims=True))
        dy = jnp.where(s < 2 * A_HEADS, dy_n, g)
        dc = dy * (sg * (1.0 + c * (1.0 - sg)))
        dx_ref[...] = (w[3:4] * dc + w[2:3] * _shift_up(dc, 1, row, t_len)
                       + w[1:2] * _shift_up(dc, 2, row, t_len) + w[0:1] * _shift_up(dc, 3, row, t_len))
        dws = [jnp.sum(dc * _shift_down(x, 3 - j, row), axis=0, keepdims=True) if j < 3
               else jnp.sum(dc * x, axis=0, keepdims=True) for j in range(CONV_K)]
        dw_ref[...] = jnp.concatenate(dws + [jnp.zeros((8 - CONV_K, LANE), F32)], axis=0)

    return pl.pallas_call(
        body, name=name, grid=(12,),
        in_specs=[pl.BlockSpec((t_len, LANE), lambda s: (0, s)), pl.BlockSpec((8, LANE), lambda s: (0, s)),
                  pl.BlockSpec((t_len, LANE), lambda s: (0, s))],
        out_specs=[pl.BlockSpec((t_len, LANE), lambda s: (0, s)), pl.BlockSpec((8, LANE), lambda s: (0, s))],
        out_shape=[jax.ShapeDtypeStruct((t_len, 3 * A_WIDTH), F32), jax.ShapeDtypeStruct((8, 3 * A_WIDTH), F32)],
        compiler_params=_cparams(("parallel",)),
    )(h, conv_w, d_out)


def _tri_inv_impl(a):
    n = a.shape[0]
    r = lax.broadcasted_iota(jnp.int32, (n, n), 0)
    c = lax.broadcasted_iota(jnp.int32, (n, n), 1)

    def same(shift):
        return (r >> shift) == (c >> shift)

    eye = (r == c).astype(F32)
    a0 = jnp.where(same(3), a, 0.0)
    a2 = _hp(a0, a0)
    a4 = _hp(a2, a2)
    t = _hp(_hp(eye - a0, eye + a2), eye + a4)
    for shift in (4, 5, 6):
        low = jnp.where(same(shift) & jnp.logical_not(same(shift - 1)), a, 0.0)
        t = t - _hp(_hp(t, low), t)
    return t


@jax.custom_vjp
def _tri_inv(a):
    return _tri_inv_impl(a)


def _tri_inv_fwd(a):
    t = _tri_inv_impl(a)
    return t, t


def _tri_inv_bwd(t, dt):
    tt = t.T
    return (-_hp(_hp(tt, dt), tt),)


_tri_inv.defvjp(_tri_inv_fwd, _tri_inv_bwd)


def _gdn_block(s, q, k, v, z, ba, alog, dtb, nw, h):
    n = q.shape[0]
    lane = lax.broadcasted_iota(jnp.int32, (1, LANE), 1)

    def pick(x, idx):
        return jnp.sum(jnp.where(lane == idx, x, 0.0), axis=1, keepdims=True)

    beta = jax.nn.sigmoid(pick(ba, h))
    g = -jnp.exp(pick(alog, h)) * _softplus(pick(ba, h + A_HEADS) + pick(dtb, h))
    r = lax.broadcasted_iota(jnp.int32, (n, n), 0)
    c = lax.broadcasted_iota(jnp.int32, (n, n), 1)
    same = (r >> 6) == (c >> 6)
    tril = same & (r >= c)
    stril = same & (r > c)
    g_b = jnp.broadcast_to(g, (n, LANE))
    gc = _hp(tril.astype(F32), g_b)
    gl = _hp(same.astype(F32), g_b)
    gc_col = gc[:, :1]
    decay = jnp.exp(jnp.where(tril, gc_col - gc_col.T, -jnp.inf))
    kb = k * beta
    a_mat = jnp.where(stril, _mm_nt(kb, k) * decay, 0.0)
    t_mat = _tri_inv(a_mat)
    eg = jnp.exp(gc)
    u = _mm(t_mat, v * beta)
    w = _mm(t_mat, kb * eg)
    qk = jnp.where(tril, _mm_nt(q, k) * decay, 0.0)
    q_dec = q * eg
    k_dec = k * jnp.exp(gl - gc)
    g_tot = jnp.exp(gl)
    outs = []
    for ci in range(n // CHUNK):
        lo, hi = ci * CHUNK, (ci + 1) * CHUNK
        v_new = u[lo:hi] - _mm(w[lo:hi], s)
        pieces = []
        if lo:
            pieces.append(jnp.zeros((lo, LANE), F32))
        pieces.append(v_new)
        if n - hi:
            pieces.append(jnp.zeros((n - hi, LANE), F32))
        v_pad = jnp.concatenate(pieces, axis=0) if len(pieces) > 1 else v_new
        outs.append(_mm(q_dec[lo:hi], s) + _mm(qk[lo:hi], v_pad))
        s = s * g_tot[lo:lo + 1, :1] + _mm_tn(k_dec[lo:hi], v_new)
    o = jnp.concatenate(outs, axis=0)
    o = o * lax.rsqrt(jnp.mean(o * o, axis=-1, keepdims=True) + RMS_EPS) * nw
    return o * _silu(z), s


def _gdn_specs(t_len):
    nsc = t_len // SUPER
    return nsc


def _gdn_fwd(qkv, h, alog, dtb, nw, *, name):
    t_len = qkv.shape[0]
    nsc = t_len // SUPER

    def body(q_ref, k_ref, v_ref, z_ref, ba_ref, al_ref, dt_ref, nw_ref, y_ref, sin_ref, s_scr):
        sc, hd = pl.program_id(0), pl.program_id(1)

        @pl.when(sc == 0)
        def _():
            s_scr[hd] = jnp.zeros((A_HEAD_DIM, A_HEAD_DIM), F32)

        s = s_scr[hd]
        sin_ref[0, 0] = s
        y, s_new = _gdn_block(s, q_ref[...], k_ref[...], v_ref[...], z_ref[...], ba_ref[...],
                              al_ref[...], dt_ref[...], nw_ref[...], hd)
        y_ref[...] = y
        s_scr[hd] = s_new

    blk = lambda off: pl.BlockSpec((SUPER, LANE), lambda sc, hd: (sc, off + hd))
    row = pl.BlockSpec((1, LANE), lambda sc, hd: (0, 0))
    return pl.pallas_call(
        body, name=name, grid=(nsc, A_HEADS),
        in_specs=[blk(0), blk(4), blk(8), blk(L_ZA // LANE),
                  pl.BlockSpec((SUPER, LANE), lambda sc, hd: (sc, L_BA // LANE)), row, row, row],
        out_specs=[pl.BlockSpec((SUPER, LANE), lambda sc, hd: (sc, hd)),
                   pl.BlockSpec((1, 1, A_HEAD_DIM, A_HEAD_DIM), lambda sc, hd: (sc, hd, 0, 0))],
        out_shape=[jax.ShapeDtypeStruct((t_len, A_WIDTH), F32),
                   jax.ShapeDtypeStruct((nsc, A_HEADS, A_HEAD_DIM, A_HEAD_DIM), F32)],
        scratch_shapes=[pltpu.VMEM((A_HEADS, A_HEAD_DIM, A_HEAD_DIM), F32)],
        compiler_params=_cparams(("arbitrary", "arbitrary")),
    )(qkv, qkv, qkv, h, h, alog, dtb, nw)


def _gdn_bwd(qkv, h, alog, dtb, nw, s_in, dycat, *, name):
    t_len = qkv.shape[0]
    nsc = t_len // SUPER

    def body(q_ref, k_ref, v_ref, z_ref, ba_ref, al_ref, dt_ref, nw_ref, sin_ref, dy_ref,
             dq_ref, dk_ref, dv_ref, dz_ref, dba_ref, dal_ref, ddt_ref, dnw_ref, ds_scr):
        i, hd = pl.program_id(0), pl.program_id(1)

        @pl.when(i == 0)
        def _():
            ds_scr[hd] = jnp.zeros((A_HEAD_DIM, A_HEAD_DIM), F32)

        @pl.when((i == 0) & (hd == 0))
        def _():
            dal_ref[...] = jnp.zeros_like(dal_ref)
            ddt_ref[...] = jnp.zeros_like(ddt_ref)
            dnw_ref[...] = jnp.zeros_like(dnw_ref)

        fn = functools.partial(_gdn_block, h=hd)
        _, vjp = jax.vjp(fn, sin_ref[0, 0], q_ref[...], k_ref[...], v_ref[...], z_ref[...], ba_ref[...],
                         al_ref[...], dt_ref[...], nw_ref[...])
        ds, dq, dk, dv, dz, dba, dal, ddt, dnw = vjp((dy_ref[...], ds_scr[hd]))
        ds_scr[hd] = ds
        dq_ref[...] = dq
        dk_ref[...] = dk
        dv_ref[...] = dv
        dz_ref[...] = dz

        @pl.when(hd == 0)
        def _():
            dba_ref[...] = dba

        @pl.when(hd > 0)
        def _():
            dba_ref[...] += dba

        dal_ref[...] += dal
        ddt_ref[...] += ddt
        dnw_ref[...] += dnw

    rev = lambda i: nsc - 1 - i
    blk = lambda off: pl.BlockSpec((SUPER, LANE), lambda i, hd: (rev(i), off + hd))
    row = pl.BlockSpec((1, LANE), lambda i, hd: (0, 0))
    ba_blk = lambda off: pl.BlockSpec((SUPER, LANE), lambda i, hd: (rev(i), off))
    return pl.pallas_call(
        body, name=name, grid=(nsc, A_HEADS),
        in_specs=[blk(0), blk(4), blk(8), blk(L_ZA // LANE), ba_blk(L_BA // LANE), row, row, row,
                  pl.BlockSpec((1, 1, A_HEAD_DIM, A_HEAD_DIM), lambda i, hd: (rev(i), hd, 0, 0)),
                  blk(0)],
        out_specs=[blk(0), blk(0), blk(0), blk(0), ba_blk(0), row, row, row],
        out_shape=[jax.ShapeDtypeStruct((t_len, A_WIDTH), F32)] * 4 + [jax.ShapeDtypeStruct((t_len, LANE), F32)]
        + [jax.ShapeDtypeStruct((1, LANE), F32)] * 3,
        scratch_shapes=[pltpu.VMEM((A_HEADS, A_HEAD_DIM, A_HEAD_DIM), F32)],
        compiler_params=_cparams(("arbitrary", "arbitrary")),
    )(qkv, qkv, qkv, h, h, alog, dtb, nw, s_in, dycat)


def _swa_block(q, kp, kc, vp, vc, z, sinks, first):
    qi = lax.broadcasted_iota(jnp.int32, (BLOCK, 2 * BLOCK), 0)
    si = lax.broadcasted_iota(jnp.int32, (BLOCK, 2 * BLOCK), 1)
    dist = qi + BLOCK - si
    mask = (dist >= 0) & (dist < WINDOW) & ((si >= BLOCK) | jnp.logical_not(first))
    dist_f = dist.astype(F32)
    outs = []
    for j in range(B_KV_HEADS):
        cs = slice(j * B_HEAD_DIM, (j + 1) * B_HEAD_DIM)
        kk = jnp.concatenate([kp[:, cs], kc[:, cs]], axis=0)
        vv = jnp.concatenate([vp[:, cs], vc[:, cs]], axis=0)
        for gi in range(B_GROUP):
            hq = j * B_GROUP + gi
            slope = 2.0 ** (-8.0 * (hq + 1) / B_Q_HEADS)
            sc = _mm_nt(q[:, hq * B_HEAD_DIM:(hq + 1) * B_HEAD_DIM], kk) * (B_HEAD_DIM ** -0.5)
            sc = jnp.where(mask, sc - slope * dist_f, -jnp.inf)
            sink = sinks[:, hq:hq + 1]
            m = lax.stop_gradient(jnp.maximum(jnp.max(sc, axis=-1, keepdims=True), sink))
            p = jnp.exp(sc - m)
            p = p / (jnp.sum(p, axis=-1, keepdims=True) + jnp.exp(sink - m))
            outs.append(_mm(p, vv))
    return jnp.concatenate(outs, axis=1) * _silu(z)


def _swa_fwd(h, sinks, *, name):
    t_len = h.shape[0]
    nb = t_len // BLOCK

    def body(q_ref, kp_ref, kc_ref, vp_ref, vc_ref, z_ref, s_ref, o_ref):
        o_ref[...] = _swa_block(q_ref[...], kp_ref[...], kc_ref[...], vp_ref[...], vc_ref[...], z_ref[...],
                                s_ref[...], pl.program_id(0) == 0)

    wide = lambda off: pl.BlockSpec((BLOCK, B_WIDTH), lambda n: (n, off))
    cur = lambda off: pl.BlockSpec((BLOCK, LANE), lambda n: (n, off))
    prev = lambda off: pl.BlockSpec((BLOCK, LANE), lambda n: (jnp.maximum(n - 1, 0), off))
    return pl.pallas_call(
        body, name=name, grid=(nb,),
        in_specs=[wide(L_QB // B_WIDTH), prev(L_KB // LANE), cur(L_KB // LANE), prev(L_VB // LANE),
                  cur(L_VB // LANE), wide(L_ZB // B_WIDTH), pl.BlockSpec((1, LANE), lambda n: (0, 0))],
        out_specs=pl.BlockSpec((BLOCK, B_WIDTH), lambda n: (n, 0)),
        out_shape=jax.ShapeDtypeStruct((t_len, B_WIDTH), F32),
        compiler_params=_cparams(("parallel",)),
    )(h, h, h, h, h, h, sinks)


def _swa_bwd(h, sinks, dycat, *, name):
    t_len = h.shape[0]
    nb = t_len // BLOCK

    def body(q_ref, kp_ref, kc_ref, vp_ref, vc_ref, z_ref, s_ref, dy_ref,
             dq_ref, dz_ref, dk_ref, dv_ref, dsk_ref, ck_scr, cv_scr):
        i = pl.program_id(0)
        n = nb - 1 - i

        @pl.when(i == 0)
        def _():
            ck_scr[...] = jnp.zeros_like(ck_scr)
            cv_scr[...] = jnp.zeros_like(cv_scr)
            dsk_ref[...] = jnp.zeros_like(dsk_ref)

        fn = functools.partial(_swa_block, first=(n == 0))
        _, vjp = jax.vjp(fn, q_ref[...], kp_ref[...], kc_ref[...], vp_ref[...], vc_ref[...], z_ref[...], s_ref[...])
        dq, dkp, dkc, dvp, dvc, dz, dsk = vjp(dy_ref[...])
        dq_ref[...] = dq
        dz_ref[...] = dz
        dk_ref[...] = dkc + ck_scr[...]
        dv_ref[...] = dvc + cv_scr[...]
        ck_scr[...] = dkp
        cv_scr[...] = dvp
        dsk_ref[...] += dsk

    rev = lambda i: nb - 1 - i
    wide = lambda off: pl.BlockSpec((BLOCK, B_WIDTH), lambda i: (rev(i), off))
    cur = lambda off: pl.BlockSpec((BLOCK, LANE), lambda i: (rev(i), off))
    prev = lambda off: pl.BlockSpec((BLOCK, LANE), lambda i: (jnp.maximum(rev(i) - 1, 0), off))
    return pl.pallas_call(
        body, name=name, grid=(nb,),
        in_specs=[wide(L_QB // B_WIDTH), prev(L_KB // LANE), cur(L_KB // LANE), prev(L_VB // LANE),
                  cur(L_VB // LANE), wide(L_ZB // B_WIDTH), pl.BlockSpec((1, LANE), lambda i: (0, 0)), wide(1)],
        out_specs=[wide(0), wide(0), cur(0), cur(0), pl.BlockSpec((1, LANE), lambda i: (0, 0))],
        out_shape=[jax.ShapeDtypeStruct((t_len, B_WIDTH), F32)] * 2
        + [jax.ShapeDtypeStruct((t_len, LANE), F32)] * 2 + [jax.ShapeDtypeStruct((1, LANE), F32)],
        scratch_shapes=[pltpu.VMEM((BLOCK, LANE), F32), pltpu.VMEM((BLOCK, LANE), F32)],
        compiler_params=_cparams(("arbitrary",)),
    )(h, h, h, h, h, h, sinks, dycat)


def _out_ln_fwd(ycat, w_out, x, ln_g, ln_b, *, name, tm=256):
    t_len = x.shape[0]

    def body(y_ref, w_ref, x_ref, g_ref, b_ref, r_ref, o_ref):
        r = DEEPNORM_ALPHA * x_ref[...] + _mm(y_ref[...], w_ref[...])
        r_ref[...] = r
        mu = jnp.mean(r, axis=-1, keepdims=True)
        d = r - mu
        var = jnp.mean(d * d, axis=-1, keepdims=True)
        o_ref[...] = d * lax.rsqrt(var + LN_EPS) * g_ref[...] + b_ref[...]

    tile = pl.BlockSpec((tm, D_MODEL), lambda i: (i, 0))
    vec = pl.BlockSpec((1, D_MODEL), lambda i: (0, 0))
    return pl.pallas_call(
        body, name=name, grid=(t_len // tm,),
        in_specs=[tile, pl.BlockSpec((D_MODEL, D_MODEL), lambda i: (0, 0)), tile, vec, vec],
        out_specs=[tile, tile],
        out_shape=[jax.ShapeDtypeStruct((t_len, D_MODEL), F32)] * 2,
        compiler_params=_cparams(("parallel",)),
    )(ycat, w_out, x, ln_g, ln_b)


def _ln_bwd(dxn, r, ln_g, *, name, tm=256):
    t_len = r.shape[0]

    def body(dx_ref, r_ref, g_ref, dr_ref, dg_ref, db_ref):
        @pl.when(pl.program_id(0) == 0)
        def _():
            dg_ref[...] = jnp.zeros_like(dg_ref)
            db_ref[...] = jnp.zeros_like(db_ref)

        rr = r_ref[...]
        dx = dx_ref[...]
        mu = jnp.mean(rr, axis=-1, keepdims=True)
        d = rr - mu
        rstd = lax.rsqrt(jnp.mean(d * d, axis=-1, keepdims=True) + LN_EPS)
        xh = d * rstd
        dxh = dx * g_ref[...]
        dr_ref[...] = rstd * (dxh - jnp.mean(dxh, axis=-1, keepdims=True)
                              - xh * jnp.mean(dxh * xh, axis=-1, keepdims=True))
        dg_ref[...] += jnp.sum(dx * xh, axis=0, keepdims=True)
        db_ref[...] += jnp.sum(dx, axis=0, keepdims=True)

    tile = pl.BlockSpec((tm, D_MODEL), lambda i: (i, 0))
    vec = pl.BlockSpec((1, D_MODEL), lambda i: (0, 0))
    return pl.pallas_call(
        body, name=name, grid=(t_len // tm,),
        in_specs=[tile, tile, vec], out_specs=[tile, vec, vec],
        out_shape=[jax.ShapeDtypeStruct((t_len, D_MODEL), F32), jax.ShapeDtypeStruct((1, D_MODEL), F32),
                   jax.ShapeDtypeStruct((1, D_MODEL), F32)],
        compiler_params=_cparams(("arbitrary",)),
    )(dxn, r, ln_g)


def _loss_head(y, target, *, name, tm=256):
    t_len = y.shape[0]

    def body(y_ref, t_ref, d_ref, l_ref):
        @pl.when(pl.program_id(0) == 0)
        def _():
            l_ref[...] = jnp.zeros_like(l_ref)

        e = y_ref[...] - t_ref[...]
        d_ref[...] = e * (1.0 / D_MODEL)
        l_ref[...] += jnp.sum(e * e, axis=0, keepdims=True)

    tile = pl.BlockSpec((tm, D_MODEL), lambda i: (i, 0))
    vec = pl.BlockSpec((1, D_MODEL), lambda i: (0, 0))
    return pl.pallas_call(
        body, name=name, grid=(t_len // tm,), in_specs=[tile, tile], out_specs=[tile, vec],
        out_shape=[jax.ShapeDtypeStruct((t_len, D_MODEL), F32), jax.ShapeDtypeStruct((1, D_MODEL), F32)],
        compiler_params=_cparams(("arbitrary",)),
    )(y, target)


def _pad_row(v):
    return jnp.zeros((1, LANE), F32).at[0, :v.shape[0]].set(v)


def _local_step(x, target, w_in_l, w_out_l, conv_l, a_log, dt_bias, norm_w, sinks, ln_g, ln_b):
    t_len = x.shape[0]
    tm = min(512, t_len)
    saved = []
    for l in range(DEPTH):
        h = _matmul(x, w_in_l[l], form="nn", tm=tm, tn=1152, tk=D_MODEL, name=f"in_proj_{l}")
        qkv = _prep_fwd(h, conv_l[l], name=f"prep_fwd_{l}")
        al, dt, nw, sk = _pad_row(a_log[l]), _pad_row(dt_bias[l]), norm_w[l][None, :], _pad_row(sinks[l])
        ya, s_in = _gdn_fwd(qkv, h, al, dt, nw, name=f"gdn_fwd_{l}")
        yb = _swa_fwd(h, sk, name=f"swa_fwd_{l}")
        ycat = jnp.concatenate([ya, yb], axis=1)
        r, xn = _out_ln_fwd(ycat, w_out_l[l], x, ln_g[l][None, :], ln_b[l][None, :], name=f"out_ln_{l}")
        saved.append((x, h, qkv, s_in, ycat, r, al, dt, nw, sk))
        x = xn
    dx, loss_lanes = _loss_head(x, target, name="loss_head")
    grads = [None] * DEPTH
    for l in reversed(range(DEPTH)):
        x_in, h, qkv, s_in, ycat, r, al, dt, nw, sk = saved[l]
        dr, d_lng, d_lnb = _ln_bwd(dx, r, ln_g[l][None, :], name=f"ln_bwd_{l}")
        dycat = _matmul(dr, w_out_l[l], form="nt", tm=tm, tn=D_MODEL, tk=D_MODEL, name=f"out_proj_dx_{l}")
        d_wout = _matmul(ycat, dr, form="tn", tm=512, tn=D_MODEL, tk=tm, name=f"out_proj_dw_{l}")
        dqn, dkn, dvn, dza, dba, d_al, d_dt, d_nw = _gdn_bwd(qkv, h, al, dt, nw, s_in, dycat, name=f"gdn_bwd_{l}")
        dqb, dzb, dkb, dvb, d_sk = _swa_bwd(h, sk, dycat, name=f"swa_bwd_{l}")
        dqkv_n = jnp.concatenate([dqn, dkn, dvn], axis=1)
        dqkv, d_conv = _prep_bwd(h, conv_l[l], dqkv_n, name=f"prep_bwd_{l}")
        dh = jnp.concatenate([dqkv, dza, dqb, dzb, dkb, dvb, dba], axis=1)
        d_win = _matmul(x_in, dh, form="tn", tm=512, tn=1152, tk=tm, name=f"in_proj_dw_{l}")
        dx = _matmul(dh, w_in_l[l], form="nt", tm=tm, tn=D_MODEL, tk=1152, name=f"in_proj_dx_{l}",
                     add=dr, add_scale=DEEPNORM_ALPHA)
        grads[l] = dict(w_in=d_win, w_out=d_wout, conv_w=d_conv[:CONV_K], a_log=d_al[0, :A_HEADS],
                        dt_bias=d_dt[0, :A_HEADS], norm_w=d_nw[0], sinks=d_sk[0, :B_Q_HEADS],
                        ln_g=d_lng[0], ln_b=d_lnb[0])
    return loss_lanes, dx, grads


def _me():
    return lax.axis_index("x"), lax.axis_index("y"), lax.axis_index("c")


def _peer(rel):
    x, y, c = _me()
    return (x ^ ((rel >> 2) & 1), y ^ ((rel >> 1) & 1), c ^ (rel & 1))


def _flat_id(pos):
    return 4 * pos[0] + 2 * pos[1] + pos[2]


def _all_gather(shard, *, name):
    rows = shard.shape[0]

    def body(x_ref, out_ref, send_sems, recv_sems, local_sem):
        me = _flat_id(_me())
        own = pltpu.make_async_copy(x_ref, out_ref.at[me], local_sem)
        own.start()
        copies = []
        for rel in range(1, N_DEV):
            cp = pltpu.make_async_remote_copy(
                src_ref=x_ref, dst_ref=out_ref.at[me], send_sem=send_sems.at[rel - 1], recv_sem=recv_sems.at[rel - 1],
                device_id=_peer(rel), device_id_type=pl.DeviceIdType.MESH)
            cp.start()
            copies.append(cp)
        for rel in range(1, N_DEV):
            src = _flat_id(_peer(rel))
            pltpu.make_async_remote_copy(
                src_ref=x_ref, dst_ref=out_ref.at[src], send_sem=send_sems.at[rel - 1],
                recv_sem=recv_sems.at[rel - 1], device_id=_peer(rel), device_id_type=pl.DeviceIdType.MESH).wait_recv()
        for cp in copies:
            cp.wait_send()
        own.wait()

    return pl.pallas_call(
        body, name=name,
        in_specs=[pl.BlockSpec(memory_space=pl.ANY)], out_specs=pl.BlockSpec(memory_space=pl.ANY),
        out_shape=jax.ShapeDtypeStruct((N_DEV, rows, LANE), shard.dtype),
        scratch_shapes=[pltpu.SemaphoreType.DMA((N_DEV - 1,)), pltpu.SemaphoreType.DMA((N_DEV - 1,)),
                        pltpu.SemaphoreType.DMA],
    )(shard)


def _exchange(contrib, *, name):
    rows = contrib.shape[1]

    def body(x_ref, out_ref, send_sems, recv_sems, local_sem):
        me = _flat_id(_me())
        own = pltpu.make_async_copy(x_ref.at[me], out_ref.at[me], local_sem)
        own.start()
        copies = []
        for rel in range(1, N_DEV):
            dst = _flat_id(_peer(rel))
            cp = pltpu.make_async_remote_copy(
                src_ref=x_ref.at[dst], dst_ref=out_ref.at[me], send_sem=send_sems.at[rel - 1],
                recv_sem=recv_sems.at[rel - 1], device_id=_peer(rel), device_id_type=pl.DeviceIdType.MESH)
            cp.start()
            copies.append(cp)
        for rel in range(1, N_DEV):
            src = _flat_id(_peer(rel))
            pltpu.make_async_remote_copy(
                src_ref=x_ref.at[src], dst_ref=out_ref.at[src], send_sem=send_sems.at[rel - 1],
                recv_sem=recv_sems.at[rel - 1], device_id=_peer(rel), device_id_type=pl.DeviceIdType.MESH).wait_recv()
        for cp in copies:
            cp.wait_send()
        own.wait()

    return pl.pallas_call(
        body, name=name,
        in_specs=[pl.BlockSpec(memory_space=pl.ANY)], out_specs=pl.BlockSpec(memory_space=pl.ANY),
        out_shape=jax.ShapeDtypeStruct((N_DEV, rows, LANE), contrib.dtype),
        scratch_shapes=[pltpu.SemaphoreType.DMA((N_DEV - 1,)), pltpu.SemaphoreType.DMA((N_DEV - 1,)),
                        pltpu.SemaphoreType.DMA],
    )(contrib)


def _adamw(gparts, w, m, v, *, name, tr=1024):
    rows = w.shape[0]
    c1 = 1.0 / (1.0 - ADAM_B1 ** ADAM_STEP)
    c2 = 1.0 / (1.0 - ADAM_B2 ** ADAM_STEP)

    def body(g_ref, w_ref, m_ref, v_ref, go_ref, d_ref, mo_ref, vo_ref):
        g = g_ref[0]
        for s in range(1, N_DEV):
            g = g + g_ref[s]
        m_new = ADAM_B1 * m_ref[...] + (1.0 - ADAM_B1) * g
        v_new = ADAM_B2 * v_ref[...] + (1.0 - ADAM_B2) * (g * g)
        go_ref[...] = g
        mo_ref[...] = m_new
        vo_ref[...] = v_new
        d_ref[...] = -ADAM_LR * ((m_new * c1) / (jnp.sqrt(v_new * c2) + ADAM_EPS) + ADAM_WD * w_ref[...])

    tile = pl.BlockSpec((tr, LANE), lambda i: (i, 0))
    return pl.pallas_call(
        body, name=name, grid=(rows // tr,),
        in_specs=[pl.BlockSpec((N_DEV, tr, LANE), lambda i: (0, i, 0)), tile, tile, tile],
        out_specs=[tile] * 4, out_shape=[jax.ShapeDtypeStruct((rows, LANE), F32)] * 4,
        compiler_params=_cparams(("parallel",)),
    )(gparts, w, m, v)


def _pack_shard(w_in, w_out, conv_w, small):
    parts = [w_in.reshape(R_WIN, LANE), w_out.reshape(R_WOUT, LANE),
             jnp.pad(conv_w.reshape(-1), (0, R_CONV * LANE - conv_w.size)).reshape(R_CONV, LANE)]
    flat = jnp.concatenate([s.reshape(-1) for s in small])
    parts.append(jnp.pad(flat, (0, R_SMALL * LANE - flat.shape[0])).reshape(R_SMALL, LANE))
    rows = R_WIN + R_WOUT + R_CONV + R_SMALL
    parts.append(jnp.zeros((R_GRAD - rows, LANE), F32))
    return jnp.concatenate(parts, axis=0)


def _unpack_shard(p):
    w_in = p[:R_WIN].reshape(DEPTH, D_MODEL, SHARD_COLS)
    w_out = p[R_WIN:R_WIN + R_WOUT].reshape(DEPTH, 128, D_MODEL)
    o = R_WIN + R_WOUT
    conv = p[o:o + R_CONV].reshape(-1)[:DEPTH * CONV_K * 192].reshape(DEPTH, CONV_K, 192)
    flat = p[o + R_CONV:o + R_CONV + R_SMALL].reshape(-1)
    small, off = [], 0
    for _, n in SMALL_SIZES:
        small.append(flat[off:off + DEPTH * n].reshape(DEPTH, n))
        off += DEPTH * n
    return w_in, w_out, conv, small


def _to_layout(w_full):
    s = lambda a, b: w_full[..., a:b]
    pad = jnp.zeros(w_full.shape[:-1] + (LANE - 2 * A_HEADS,), w_full.dtype)
    return jnp.concatenate([s(0, 1536), s(1536, 2048), s(2056, 2568), s(2824, 3336), s(2568, 2696), s(2696, 2824),
                            s(2048, 2056), pad], axis=-1)


def _from_layout(g):
    s = lambda a, b: g[..., a:b]
    return jnp.concatenate([s(0, 1536), s(L_ZA, L_ZA + 512), s(L_BA, L_BA + 8), s(L_QB, L_QB + 512),
                            s(L_KB, L_KB + 128), s(L_VB, L_VB + 128), s(L_ZB, L_ZB + 512)], axis=-1)


def kernel(x, w_in, conv_w, a_log, dt_bias, norm_w, sinks, w_out, ln_g, ln_b, loss_target, m_w_in, m_conv_w, m_a_log, m_dt_bias, m_norm_w, m_sinks, m_w_out, m_ln_g, m_ln_b, v_w_in, v_conv_w, v_a_log, v_dt_bias, v_norm_w, v_sinks, v_w_out, v_ln_g, v_ln_b):
    small = [a_log, dt_bias, norm_w, sinks, ln_g, ln_b]
    conv_bits = lax.bitcast_convert_type(conv_w.reshape(-1), BF16).reshape(-1)
    conv_rows = jnp.pad(conv_bits, (0, 2 * R_CONV * LANE - conv_bits.shape[0])).reshape(2 * R_CONV, LANE)
    shard = jnp.concatenate([w_in.astype(BF16).reshape(R_WIN, LANE), w_out.astype(BF16).reshape(R_WOUT, LANE),
                             conv_rows], axis=0)
    gathered = _all_gather(shard, name="weights_all_gather")
    w_in_full = gathered[:, :R_WIN].reshape(N_DEV, DEPTH, D_MODEL, SHARD_COLS).transpose(1, 2, 0, 3)
    w_in_l = _to_layout(w_in_full.reshape(DEPTH, D_MODEL, IN_COLS))
    w_out_l = gathered[:, R_WIN:R_WIN + R_WOUT].reshape(N_DEV, DEPTH, 128, D_MODEL).transpose(1, 0, 2, 3)
    w_out_l = w_out_l.reshape(DEPTH, D_MODEL, D_MODEL)
    conv_full = lax.bitcast_convert_type(
        gathered[:, R_WIN + R_WOUT:].reshape(N_DEV, -1)[:, :2 * DEPTH * CONV_K * 192].reshape(N_DEV, -1, 2), F32)
    conv_full = conv_full.reshape(N_DEV, DEPTH, CONV_K, 192).transpose(1, 2, 0, 3).reshape(DEPTH, CONV_K, 1536)
    conv_l = jnp.pad(conv_full, ((0, 0), (0, 8 - CONV_K), (0, 0)))

    loss_lanes, dx, grads = _local_step(x[0], loss_target[0], w_in_l, w_out_l, conv_l, a_log, dt_bias, norm_w,
                                        sinks, ln_g, ln_b)
    loss = lax.psum(0.5 * jnp.sum(loss_lanes) * (1.0 / D_MODEL), ("x", "y", "c"))

    g_win = jnp.stack([_from_layout(grads[l]["w_in"]) for l in range(DEPTH)])
    g_win = g_win.reshape(DEPTH, D_MODEL, N_DEV, SHARD_COLS).transpose(2, 0, 1, 3).reshape(N_DEV, R_WIN, LANE)
    g_wout = jnp.stack([grads[l]["w_out"] for l in range(DEPTH)])
    g_wout = g_wout.reshape(DEPTH, N_DEV, 128, D_MODEL).transpose(1, 0, 2, 3).reshape(N_DEV, R_WOUT, LANE)
    g_conv = jnp.stack([grads[l]["conv_w"] for l in range(DEPTH)])
    g_conv = g_conv.reshape(DEPTH, CONV_K, N_DEV, 192).transpose(2, 0, 1, 3).reshape(N_DEV, -1)
    g_conv = jnp.pad(g_conv, ((0, 0), (0, R_CONV * LANE - g_conv.shape[1]))).reshape(N_DEV, R_CONV, LANE)
    g_small = jnp.concatenate([jnp.stack([grads[l][n] for l in range(DEPTH)]).reshape(-1) for n, _ in SMALL_SIZES])
    g_small = jnp.pad(g_small, (0, R_SMALL * LANE - g_small.shape[0])).reshape(1, R_SMALL, LANE)
    g_small = jnp.broadcast_to(g_small, (N_DEV, R_SMALL, LANE))
    rows = R_WIN + R_WOUT + R_CONV + R_SMALL
    contrib = jnp.concatenate([g_win, g_wout, g_conv, g_small, jnp.zeros((N_DEV, R_GRAD - rows, LANE), F32)], axis=1)
    parts = _exchange(contrib, name="grad_exchange")

    w_p = _pack_shard(w_in, w_out, conv_w, small)
    m_p = _pack_shard(m_w_in, m_w_out, m_conv_w, [m_a_log, m_dt_bias, m_norm_w, m_sinks, m_ln_g, m_ln_b])
    v_p = _pack_shard(v_w_in, v_w_out, v_conv_w, [v_a_log, v_dt_bias, v_norm_w, v_sinks, v_ln_g, v_ln_b])
    outs = [_unpack_shard(p) for p in _adamw(parts, w_p, m_p, v_p, name="adamw")]

    def ordered(u):
        wi, wo, cv, sm = u
        return [wi, cv, sm[0], sm[1], sm[2], sm[3], wo, sm[4], sm[5]]

    g_o, d_o, m_o, v_o = (ordered(u) for u in outs)
    return (loss, dx[None], *g_o, *d_o, *m_o, *v_o)
```

```python
import functools
import math

import jax
import jax.numpy as jnp
from jax import lax
from jax.experimental import pallas as pl
from jax.experimental.pallas import tpu as pltpu

F32 = jnp.float32
BF16 = jnp.bfloat16
MM_DTYPE = BF16

N_DEV = 8
D_MODEL = 1024
DEPTH = 2
A_HEADS = 4
A_HEAD_DIM = 128
A_WIDTH = 512
CONV_K = 4
CHUNK = 64
SUPER = 256
B_Q_HEADS = 8
B_KV_HEADS = 2
B_HEAD_DIM = 64
B_GROUP = 4
B_WIDTH = 512
B_KV_WIDTH = 128
WINDOW = 128
BLOCK = 128
IN_COLS = 3336
SHARD_COLS = IN_COLS // N_DEV
DEEPNORM_ALPHA = (2 * DEPTH) ** 0.25
LN_EPS = 1e-5
RMS_EPS = 1e-6
L2_EPS = 1e-6
ADAM_LR, ADAM_B1, ADAM_B2, ADAM_EPS, ADAM_WD, ADAM_STEP = 0.001, 0.9, 0.999, 1e-08, 0.01, 10

LANE = 128
L_QKV, L_ZA, L_QB, L_ZB, L_KB, L_VB, L_BA = 0, 1536, 2048, 2560, 3072, 3200, 3328
L_COLS = 3456
R_WIN = DEPTH * D_MODEL * SHARD_COLS // LANE
R_WOUT = DEPTH * 128 * D_MODEL // LANE
R_CONV = 16
R_SMALL = 40
R_GRAD = 9216
R_AG = R_WIN + R_WOUT + 2 * R_CONV
SMALL_SIZES = (("a_log", 4), ("dt_bias", 4), ("norm_w", 128), ("sinks", 8), ("ln_g", 1024), ("ln_b", 1024))
VMEM_LIMIT = 48 * 1024 * 1024


def _cparams(sem=None):
    return pltpu.CompilerParams(dimension_semantics=sem, vmem_limit_bytes=VMEM_LIMIT)


def _mm(a, b):
    return jnp.dot(a.astype(MM_DTYPE), b.astype(MM_DTYPE), preferred_element_type=F32)


def _mm_nt(a, b):
    return lax.dot_general(a.astype(MM_DTYPE), b.astype(MM_DTYPE), (((1,), (1,)), ((), ())),
                           preferred_element_type=F32)


def _mm_tn(a, b):
    return lax.dot_general(a.astype(MM_DTYPE), b.astype(MM_DTYPE), (((0,), (0,)), ((), ())),
                           preferred_element_type=F32)


def _split(a):
    hi = a.astype(BF16)
    return hi, (a - hi.astype(F32)).astype(BF16)


def _hp(a2, b2):
    d = lambda p, q: jnp.dot(p, q, preferred_element_type=F32)
    return d(a2[0], b2[0]) + (d(a2[0], b2[1]) + d(a2[1], b2[0]))


def _silu(x):
    return x * jax.nn.sigmoid(x)


def _softplus(x):
    return jnp.maximum(x, 0.0) + jnp.log1p(jnp.exp(-jnp.abs(x)))


def _matmul(a, b, *, form, tm, tn, tk, name, add=None, add_scale=1.0):
    if form == "nn":
        (m, kk), (_, n) = a.shape, b.shape
        a_spec = pl.BlockSpec((tm, tk), lambda i, j, k: (i, k))
        b_spec = pl.BlockSpec((tk, tn), lambda i, j, k: (k, j))
        dn = (((1,), (0,)), ((), ()))
    elif form == "nt":
        (m, kk), (n, _) = a.shape, b.shape
        a_spec = pl.BlockSpec((tm, tk), lambda i, j, k: (i, k))
        b_spec = pl.BlockSpec((tn, tk), lambda i, j, k: (j, k))
        dn = (((1,), (1,)), ((), ()))
    else:
        (kk, m), (_, n) = a.shape, b.shape
        a_spec = pl.BlockSpec((tk, tm), lambda i, j, k: (k, i))
        b_spec = pl.BlockSpec((tk, tn), lambda i, j, k: (k, j))
        dn = (((0,), (0,)), ((), ()))
    assert m % tm == 0 and n % tn == 0 and kk % tk == 0, (name, m, n, kk)
    has_add = add is not None

    def body(*refs):
        if has_add:
            a_ref, b_ref, add_ref, o_ref = refs
        else:
            a_ref, b_ref, o_ref = refs
        k = pl.program_id(2)
        p = lax.dot_general(a_ref[...].astype(MM_DTYPE), b_ref[...].astype(MM_DTYPE), dn,
                            preferred_element_type=F32)

        @pl.when(k == 0)
        def _():
            if has_add:
                o_ref[...] = p + add_scale * add_ref[...]
            else:
                o_ref[...] = p

        @pl.when(k > 0)
        def _():
            o_ref[...] += p

    in_specs = [a_spec, b_spec]
    args = [a, b]
    if has_add:
        in_specs.append(pl.BlockSpec((tm, tn), lambda i, j, k: (i, j)))
        args.append(add)
    return pl.pallas_call(
        body, name=name, grid=(m // tm, n // tn, kk // tk), in_specs=in_specs,
        out_specs=pl.BlockSpec((tm, tn), lambda i, j, k: (i, j)),
        out_shape=jax.ShapeDtypeStruct((m, n), F32),
        compiler_params=_cparams(("parallel", "parallel", "arbitrary")),
    )(*args)


def _shift_down(x, k, row):
    return jnp.where(row >= k, pltpu.roll(x, k, 0), 0.0)


def _shift_up(x, k, row, t_len):
    return jnp.where(row < t_len - k, pltpu.roll(x, t_len - k, 0), 0.0)


def _conv_slab(x, w, row):
    return (w[3:4] * x + w[2:3] * _shift_down(x, 1, row) + w[1:2] * _shift_down(x, 2, row)
            + w[0:1] * _shift_down(x, 3, row))


def _prep_fwd(h, conv_w, *, name):
    t_len = h.shape[0]

    def body(x_ref, w_ref, o_ref):
        s = pl.program_id(0)
        row = lax.broadcasted_iota(jnp.int32, (t_len, LANE), 0)
        y = _silu(_conv_slab(x_ref[...], w_ref[...], row))
        rs = lax.rsqrt(jnp.sum(y * y, axis=-1, keepdims=True) + L2_EPS)
        scale = jnp.where(s < A_HEADS, A_HEAD_DIM ** -0.5, 1.0)
        o_ref[...] = jnp.where(s < 2 * A_HEADS, y * rs * scale, y)

    return pl.pallas_call(
        body, name=name, grid=(12,),
        in_specs=[pl.BlockSpec((t_len, LANE), lambda s: (0, s)), pl.BlockSpec((8, LANE), lambda s: (0, s))],
        out_specs=pl.BlockSpec((t_len, LANE), lambda s: (0, s)),
        out_shape=jax.ShapeDtypeStruct((t_len, 3 * A_WIDTH), F32),
        compiler_params=_cparams(("parallel",)),
    )(h, conv_w)


def _prep_bwd(h, conv_w, d_out, *, name):
    t_len = h.shape[0]

    def body(x_ref, w_ref, g_ref, dx_ref, dw_ref):
        s = pl.program_id(0)
        row = lax.broadcasted_iota(jnp.int32, (t_len, LANE), 0)
        x = x_ref[...]
        w = w_ref[...]
        c = _conv_slab(x, w, row)
        sg = jax.nn.sigmoid(c)
        y = c * sg
        g = g_ref[...]
        rs = lax.rsqrt(jnp.sum(y * y, axis=-1, keepdims=True) + L2_EPS)
        scale = jnp.where(s < A_HEADS, A_HEAD_DIM ** -0.5, 1.0)
        dy_n = scale * (rs * g - y * (rs * rs * rs) * jnp.sum(g * y, axis=-1, keepdims=True))
        dy = jnp.where(s < 2 * A_HEADS, dy_n, g)
        dc = dy * (sg * (1.0 + c * (1.0 - sg)))
        dx_ref[...] = (w[3:4] * dc + w[2:3] * _shift_up(dc, 1, row, t_len)
                       + w[1:2] * _shift_up(dc, 2, row, t_len) + w[0:1] * _shift_up(dc, 3, row, t_len))
        dws = [jnp.sum(dc * _shift_down(x, 3 - j, row), axis=0, keepdims=True) if j < 3
               else jnp.sum(dc * x, axis=0, keepdims=True) for j in range(CONV_K)]
        dw_ref[...] = jnp.concatenate(dws + [jnp.zeros((8 - CONV_K, LANE), F32)], axis=0)

    return pl.pallas_call(
        body, name=name, grid=(12,),
        in_specs=[pl.BlockSpec((t_len, LANE), lambda s: (0, s)), pl.BlockSpec((8, LANE), lambda s: (0, s)),
                  pl.BlockSpec((t_len, LANE), lambda s: (0, s))],
        out_specs=[pl.BlockSpec((t_len, LANE), lambda s: (0, s)), pl.BlockSpec((8, LANE), lambda s: (0, s))],
        out_shape=[jax.ShapeDtypeStruct((t_len, 3 * A_WIDTH), F32), jax.ShapeDtypeStruct((8, 3 * A_WIDTH), F32)],
        compiler_params=_cparams(("parallel",)),
    )(h, conv_w, d_out)


def _tri_inv_impl(a):
    n = a.shape[0]
    r = lax.broadcasted_iota(jnp.int32, (n, n), 0)
    c = lax.broadcasted_iota(jnp.int32, (n, n), 1)

    def same(shift):
        return (r >> shift) == (c >> shift)

    eye = (r == c).astype(F32)
    a0 = jnp.where(same(3), a, 0.0)
    a0s = _split(a0)
    a2 = _hp(a0s, a0s)
    a2s = _split(a2)
    a4 = _hp(a2s, a2s)
    t = _hp(_split(_hp(_split(eye - a0), _split(eye + a2))), _split(eye + a4))
    for shift in (4, 5, 6):
        low = jnp.where(same(shift) & jnp.logical_not(same(shift - 1)), a, 0.0)
        ts = _split(t)
        t = t - _hp(_split(_hp(ts, _split(low))), ts)
    return t


def _tri_inv_cotangent(t, dt):
    tts = _split(t.T)
    return -_hp(_split(_hp(tts, _split(dt))), tts)


@jax.custom_vjp
def _tri_inv(a):
    return _tri_inv_impl(a)


def _tri_inv_fwd(a):
    t = _tri_inv_impl(a)
    return t, t


_tri_inv.defvjp(_tri_inv_fwd, lambda t, dt: (_tri_inv_cotangent(t, dt),))


@jax.custom_vjp
def _tri_inv_known(a, t):
    return t


def _tri_inv_known_fwd(a, t):
    return t, t


def _tri_inv_known_bwd(t, dt):
    return _tri_inv_cotangent(t, dt), jnp.zeros_like(t)


_tri_inv_known.defvjp(_tri_inv_known_fwd, _tri_inv_known_bwd)


def _mask_times_col(x, kind):
    n = x.shape[0]
    r = lax.broadcasted_iota(jnp.int32, (n, n), 0)
    c = lax.broadcasted_iota(jnp.int32, (n, n), 1)
    same = (r >> 6) == (c >> 6)
    mask = {"cum": same & (r >= c), "cum_t": same & (r <= c), "tot": same}[kind]
    x1 = x.astype(BF16).astype(F32)
    x2 = (x - x1).astype(BF16).astype(F32)
    x3 = (x - x1 - x2).astype(BF16).astype(F32)
    lane = lax.broadcasted_iota(jnp.int32, (1, LANE), 1)
    pieces = jnp.where(lane == 0, x1, jnp.where(lane == 1, x2, jnp.where(lane == 2, x3, 0.0)))
    res = jnp.dot(mask.astype(BF16), pieces.astype(BF16), preferred_element_type=F32)
    return res[:, 0:1] + res[:, 1:2] + res[:, 2:3]


@jax.custom_vjp
def _chunk_sums(g):
    return _mask_times_col(g, "cum"), _mask_times_col(g, "tot")


def _chunk_sums_fwd(g):
    return _chunk_sums(g), None


def _chunk_sums_bwd(_, d):
    return (_mask_times_col(d[0], "cum_t") + _mask_times_col(d[1], "tot"),)


_chunk_sums.defvjp(_chunk_sums_fwd, _chunk_sums_bwd)


def _gdn_block(s, q, k, v, z, ba, alog, dtb, nw, h, t_known=None):
    n = q.shape[0]
    lane = lax.broadcasted_iota(jnp.int32, (1, LANE), 1)

    def pick(x, idx):
        return jnp.sum(jnp.where(lane == idx, x, 0.0), axis=1, keepdims=True)

    beta = jax.nn.sigmoid(pick(ba, h))
    g = -jnp.exp(pick(alog, h)) * _softplus(pick(ba, h + A_HEADS) + pick(dtb, h))
    r = lax.broadcasted_iota(jnp.int32, (n, n), 0)
    c = lax.broadcasted_iota(jnp.int32, (n, n), 1)
    same = (r >> 6) == (c >> 6)
    tril = same & (r >= c)
    stril = same & (r > c)
    gc, gl = _chunk_sums(g)
    decay = jnp.exp(jnp.where(tril, gc - gc.T, -jnp.inf))
    kb = k * beta
    a_mat = jnp.where(stril, _mm_nt(kb, k) * decay, 0.0)
    t_mat = _tri_inv(a_mat) if t_known is None else _tri_inv_known(a_mat, t_known)
    eg = jnp.exp(gc)
    u = _mm(t_mat, v * beta)
    w = _mm(t_mat, kb * eg)
    qk = jnp.where(tril, _mm_nt(q, k) * decay, 0.0)
    q_dec = q * eg
    k_dec = k * jnp.exp(gl - gc)
    g_tot = jnp.exp(gl)
    outs = []
    for ci in range(n // CHUNK):
        lo, hi = ci * CHUNK, (ci + 1) * CHUNK
        v_new = u[lo:hi] - _mm(w[lo:hi], s)
        pieces = []
        if lo:
            pieces.append(jnp.zeros((lo, LANE), F32))
        pieces.append(v_new)
        if n - hi:
            pieces.append(jnp.zeros((n - hi, LANE), F32))
        v_pad = jnp.concatenate(pieces, axis=0) if len(pieces) > 1 else v_new
        outs.append(_mm(q_dec[lo:hi], s) + _mm(qk[lo:hi], v_pad))
        s = s * g_tot[lo:lo + 1] + _mm_tn(k_dec[lo:hi], v_new)
    o = jnp.concatenate(outs, axis=0)
    o = o * lax.rsqrt(jnp.mean(o * o, axis=-1, keepdims=True) + RMS_EPS) * nw
    return o * _silu(z), s, t_mat


def _gdn_fwd(qkv, h, alog, dtb, nw, *, name):
    t_len = qkv.shape[0]
    nsc = t_len // SUPER

    def body(q_ref, k_ref, v_ref, z_ref, ba_ref, al_ref, dt_ref, nw_ref, y_ref, sin_ref, t_ref, s_scr):
        sc, hd = pl.program_id(0), pl.program_id(1)

        @pl.when(sc == 0)
        def _():
            s_scr[hd] = jnp.zeros((A_HEAD_DIM, A_HEAD_DIM), F32)

        s = s_scr[hd]
        sin_ref[0, 0] = s
        y, s_new, t_mat = _gdn_block(s, q_ref[...], k_ref[...], v_ref[...], z_ref[...], ba_ref[...],
                                     al_ref[...], dt_ref[...], nw_ref[...], hd)
        y_ref[...] = y
        t_ref[0, 0] = t_mat
        s_scr[hd] = s_new

    blk = lambda off: pl.BlockSpec((SUPER, LANE), lambda sc, hd: (sc, off + hd))
    row = pl.BlockSpec((1, LANE), lambda sc, hd: (0, 0))
    return pl.pallas_call(
        body, name=name, grid=(nsc, A_HEADS),
        in_specs=[blk(0), blk(4), blk(8), blk(L_ZA // LANE),
                  pl.BlockSpec((SUPER, LANE), lambda sc, hd: (sc, L_BA // LANE)), row, row, row],
        out_specs=[pl.BlockSpec((SUPER, LANE), lambda sc, hd: (sc, hd)),
                   pl.BlockSpec((1, 1, A_HEAD_DIM, A_HEAD_DIM), lambda sc, hd: (sc, hd, 0, 0)),
                   pl.BlockSpec((1, 1, SUPER, SUPER), lambda sc, hd: (sc, hd, 0, 0))],
        out_shape=[jax.ShapeDtypeStruct((t_len, A_WIDTH), F32),
                   jax.ShapeDtypeStruct((nsc, A_HEADS, A_HEAD_DIM, A_HEAD_DIM), F32),
                   jax.ShapeDtypeStruct((nsc, A_HEADS, SUPER, SUPER), F32)],
        scratch_shapes=[pltpu.VMEM((A_HEADS, A_HEAD_DIM, A_HEAD_DIM), F32)],
        compiler_params=_cparams(("arbitrary", "arbitrary")),
    )(qkv, qkv, qkv, h, h, alog, dtb, nw)


def _gdn_bwd(qkv, h, alog, dtb, nw, s_in, t_in, dycat, *, name):
    t_len = qkv.shape[0]
    nsc = t_len // SUPER

    def body(q_ref, k_ref, v_ref, z_ref, ba_ref, al_ref, dt_ref, nw_ref, sin_ref, t_ref, dy_ref,
             dq_ref, dk_ref, dv_ref, dz_ref, dba_ref, dal_ref, ddt_ref, dnw_ref, ds_scr):
        i, hd = pl.program_id(0), pl.program_id(1)

        @pl.when(i == 0)
        def _():
            ds_scr[hd] = jnp.zeros((A_HEAD_DIM, A_HEAD_DIM), F32)

        @pl.when((i == 0) & (hd == 0))
        def _():
            dal_ref[...] = jnp.zeros_like(dal_ref)
            ddt_ref[...] = jnp.zeros_like(ddt_ref)
            dnw_ref[...] = jnp.zeros_like(dnw_ref)

        t_known = t_ref[0, 0]
        fn = lambda *a: _gdn_block(*a, h=hd, t_known=t_known)[:2]
        _, vjp = jax.vjp(fn, sin_ref[0, 0], q_ref[...], k_ref[...], v_ref[...], z_ref[...], ba_ref[...],
                         al_ref[...], dt_ref[...], nw_ref[...])
        ds, dq, dk, dv, dz, dba, dal, ddt, dnw = vjp((dy_ref[...], ds_scr[hd]))
        ds_scr[hd] = ds
        dq_ref[...] = dq
        dk_ref[...] = dk
        dv_ref[...] = dv
        dz_ref[...] = dz

        @pl.when(hd == 0)
        def _():
            dba_ref[...] = dba

        @pl.when(hd > 0)
        def _():
            dba_ref[...] += dba

        dal_ref[...] += dal
        ddt_ref[...] += ddt
        dnw_ref[...] += dnw

    rev = lambda i: nsc - 1 - i
    blk = lambda off: pl.BlockSpec((SUPER, LANE), lambda i, hd: (rev(i), off + hd))
    row = pl.BlockSpec((1, LANE), lambda i, hd: (0, 0))
    ba_blk = lambda off: pl.BlockSpec((SUPER, LANE), lambda i, hd: (rev(i), off))
    return pl.pallas_call(
        body, name=name, grid=(nsc, A_HEADS),
        in_specs=[blk(0), blk(4), blk(8), blk(L_ZA // LANE), ba_blk(L_BA // LANE), row, row, row,
                  pl.BlockSpec((1, 1, A_HEAD_DIM, A_HEAD_DIM), lambda i, hd: (rev(i), hd, 0, 0)),
                  pl.BlockSpec((1, 1, SUPER, SUPER), lambda i, hd: (rev(i), hd, 0, 0)),
                  blk(0)],
        out_specs=[blk(0), blk(0), blk(0), blk(0), ba_blk(0), row, row, row],
        out_shape=[jax.ShapeDtypeStruct((t_len, A_WIDTH), F32)] * 4 + [jax.ShapeDtypeStruct((t_len, LANE), F32)]
        + [jax.ShapeDtypeStruct((1, LANE), F32)] * 3,
        scratch_shapes=[pltpu.VMEM((A_HEADS, A_HEAD_DIM, A_HEAD_DIM), F32)],
        compiler_params=_cparams(("arbitrary", "arbitrary")),
    )(qkv, qkv, qkv, h, h, alog, dtb, nw, s_in, t_in, dycat)


def _swa_block(q, kp, kc, vp, vc, z, sinks, first):
    qi = lax.broadcasted_iota(jnp.int32, (BLOCK, 2 * BLOCK), 0)
    si = lax.broadcasted_iota(jnp.int32, (BLOCK, 2 * BLOCK), 1)
    dist = qi + BLOCK - si
    mask = (dist >= 0) & (dist < WINDOW) & ((si >= BLOCK) | jnp.logical_not(first))
    dist_f = dist.astype(F32)
    outs = []
    for j in range(B_KV_HEADS):
        cs = slice(j * B_HEAD_DIM, (j + 1) * B_HEAD_DIM)
        kk = jnp.concatenate([kp[:, cs], kc[:, cs]], axis=0)
        vv = jnp.concatenate([vp[:, cs], vc[:, cs]], axis=0)
        for gi in range(B_GROUP):
            hq = j * B_GROUP + gi
            slope = 2.0 ** (-8.0 * (hq + 1) / B_Q_HEADS)
            sc = _mm_nt(q[:, hq * B_HEAD_DIM:(hq + 1) * B_HEAD_DIM], kk) * (B_HEAD_DIM ** -0.5)
            sc = jnp.where(mask, sc - slope * dist_f, -jnp.inf)
            sink = sinks[:, hq:hq + 1]
            m = lax.stop_gradient(jnp.maximum(jnp.max(sc, axis=-1, keepdims=True), sink))
            p = jnp.exp(sc - m)
            p = p / (jnp.sum(p, axis=-1, keepdims=True) + jnp.exp(sink - m))
            outs.append(_mm(p, vv))
    return jnp.concatenate(outs, axis=1) * _silu(z)


def _swa_fwd(h, sinks, *, name):
    t_len = h.shape[0]
    nb = t_len // BLOCK

    def body(q_ref, kp_ref, kc_ref, vp_ref, vc_ref, z_ref, s_ref, o_ref):
        o_ref[...] = _swa_block(q_ref[...], kp_ref[...], kc_ref[...], vp_ref[...], vc_ref[...], z_ref[...],
                                s_ref[...], pl.program_id(0) == 0)

    wide = lambda off: pl.BlockSpec((BLOCK, B_WIDTH), lambda n: (n, off))
    cur = lambda off: pl.BlockSpec((BLOCK, LANE), lambda n: (n, off))
    prev = lambda off: pl.BlockSpec((BLOCK, LANE), lambda n: (jnp.maximum(n - 1, 0), off))
    return pl.pallas_call(
        body, name=name, grid=(nb,),
        in_specs=[wide(L_QB // B_WIDTH), prev(L_KB // LANE), cur(L_KB // LANE), prev(L_VB // LANE),
                  cur(L_VB // LANE), wide(L_ZB // B_WIDTH), pl.BlockSpec((1, LANE), lambda n: (0, 0))],
        out_specs=pl.BlockSpec((BLOCK, B_WIDTH), lambda n: (n, 0)),
        out_shape=jax.ShapeDtypeStruct((t_len, B_WIDTH), F32),
        compiler_params=_cparams(("parallel",)),
    )(h, h, h, h, h, h, sinks)


def _swa_bwd(h, sinks, dycat, *, name):
    t_len = h.shape[0]
    nb = t_len // BLOCK

    def body(q_ref, kp_ref, kc_ref, vp_ref, vc_ref, z_ref, s_ref, dy_ref,
             dq_ref, dz_ref, dk_ref, dv_ref, dsk_ref, ck_scr, cv_scr):
        i = pl.program_id(0)
        n = nb - 1 - i

        @pl.when(i == 0)
        def _():
            ck_scr[...] = jnp.zeros_like(ck_scr)
            cv_scr[...] = jnp.zeros_like(cv_scr)
            dsk_ref[...] = jnp.zeros_like(dsk_ref)

        fn = functools.partial(_swa_block, first=(n == 0))
        _, vjp = jax.vjp(fn, q_ref[...], kp_ref[...], kc_ref[...], vp_ref[...], vc_ref[...], z_ref[...], s_ref[...])
        dq, dkp, dkc, dvp, dvc, dz, dsk = vjp(dy_ref[...])
        dq_ref[...] = dq
        dz_ref[...] = dz
        dk_ref[...] = dkc + ck_scr[...]
        dv_ref[...] = dvc + cv_scr[...]
        ck_scr[...] = dkp
        cv_scr[...] = dvp
        dsk_ref[...] += dsk

    rev = lambda i: nb - 1 - i
    wide = lambda off: pl.BlockSpec((BLOCK, B_WIDTH), lambda i: (rev(i), off))
    cur = lambda off: pl.BlockSpec((BLOCK, LANE), lambda i: (rev(i), off))
    prev = lambda off: pl.BlockSpec((BLOCK, LANE), lambda i: (jnp.maximum(rev(i) - 1, 0), off))
    return pl.pallas_call(
        body, name=name, grid=(nb,),
        in_specs=[wide(L_QB // B_WIDTH), prev(L_KB // LANE), cur(L_KB // LANE), prev(L_VB // LANE),
                  cur(L_VB // LANE), wide(L_ZB // B_WIDTH), pl.BlockSpec((1, LANE), lambda i: (0, 0)), wide(1)],
        out_specs=[wide(0), wide(0), cur(0), cur(0), pl.BlockSpec((1, LANE), lambda i: (0, 0))],
        out_shape=[jax.ShapeDtypeStruct((t_len, B_WIDTH), F32)] * 2
        + [jax.ShapeDtypeStruct((t_len, LANE), F32)] * 2 + [jax.ShapeDtypeStruct((1, LANE), F32)],
        scratch_shapes=[pltpu.VMEM((BLOCK, LANE), F32), pltpu.VMEM((BLOCK, LANE), F32)],
        compiler_params=_cparams(("arbitrary",)),
    )(h, h, h, h, h, h, sinks, dycat)


def _out_ln_fwd(ycat, w_out, x, ln_g, ln_b, *, name, tm=256):
    t_len = x.shape[0]

    def body(y_ref, w_ref, x_ref, g_ref, b_ref, r_ref, o_ref):
        r = DEEPNORM_ALPHA * x_ref[...] + _mm(y_ref[...], w_ref[...])
        r_ref[...] = r
        mu = jnp.mean(r, axis=-1, keepdims=True)
        d = r - mu
        var = jnp.mean(d * d, axis=-1, keepdims=True)
        o_ref[...] = d * lax.rsqrt(var + LN_EPS) * g_ref[...] + b_ref[...]

    tile = pl.BlockSpec((tm, D_MODEL), lambda i: (i, 0))
    vec = pl.BlockSpec((1, D_MODEL), lambda i: (0, 0))
    return pl.pallas_call(
        body, name=name, grid=(t_len // tm,),
        in_specs=[tile, pl.BlockSpec((D_MODEL, D_MODEL), lambda i: (0, 0)), tile, vec, vec],
        out_specs=[tile, tile],
        out_shape=[jax.ShapeDtypeStruct((t_len, D_MODEL), F32)] * 2,
        compiler_params=_cparams(("parallel",)),
    )(ycat, w_out, x, ln_g, ln_b)


def _ln_bwd(dxn, r, ln_g, *, name, tm=256):
    t_len = r.shape[0]

    def body(dx_ref, r_ref, g_ref, dr_ref, dg_ref, db_ref):
        @pl.when(pl.program_id(0) == 0)
        def _():
            dg_ref[...] = jnp.zeros_like(dg_ref)
            db_ref[...] = jnp.zeros_like(db_ref)

        rr = r_ref[...]
        dx = dx_ref[...]
        mu = jnp.mean(rr, axis=-1, keepdims=True)
        d = rr - mu
        rstd = lax.rsqrt(jnp.mean(d * d, axis=-1, keepdims=True) + LN_EPS)
        xh = d * rstd
        dxh = dx * g_ref[...]
        dr_ref[...] = rstd * (dxh - jnp.mean(dxh, axis=-1, keepdims=True)
                              - xh * jnp.mean(dxh * xh, axis=-1, keepdims=True))
        dg_ref[...] += jnp.sum(dx * xh, axis=0, keepdims=True)
        db_ref[...] += jnp.sum(dx, axis=0, keepdims=True)

    tile = pl.BlockSpec((tm, D_MODEL), lambda i: (i, 0))
    vec = pl.BlockSpec((1, D_MODEL), lambda i: (0, 0))
    return pl.pallas_call(
        body, name=name, grid=(t_len // tm,),
        in_specs=[tile, tile, vec], out_specs=[tile, vec, vec],
        out_shape=[jax.ShapeDtypeStruct((t_len, D_MODEL), F32), jax.ShapeDtypeStruct((1, D_MODEL), F32),
                   jax.ShapeDtypeStruct((1, D_MODEL), F32)],
        compiler_params=_cparams(("arbitrary",)),
    )(dxn, r, ln_g)


def _loss_head(y, target, *, name, tm=256):
    t_len = y.shape[0]

    def body(y_ref, t_ref, d_ref, l_ref):
        @pl.when(pl.program_id(0) == 0)
        def _():
            l_ref[...] = jnp.zeros_like(l_ref)

        e = y_ref[...] - t_ref[...]
        d_ref[...] = e * (1.0 / D_MODEL)
        l_ref[...] += jnp.sum(e * e, axis=0, keepdims=True)

    tile = pl.BlockSpec((tm, D_MODEL), lambda i: (i, 0))
    vec = pl.BlockSpec((1, D_MODEL), lambda i: (0, 0))
    return pl.pallas_call(
        body, name=name, grid=(t_len // tm,), in_specs=[tile, tile], out_specs=[tile, vec],
        out_shape=[jax.ShapeDtypeStruct((t_len, D_MODEL), F32), jax.ShapeDtypeStruct((1, D_MODEL), F32)],
        compiler_params=_cparams(("arbitrary",)),
    )(y, target)


def _pad_row(v):
    return jnp.zeros((1, LANE), F32).at[0, :v.shape[0]].set(v)


def _local_step(x, target, w_in_l, w_out_l, conv_l, a_log, dt_bias, norm_w, sinks, ln_g, ln_b):
    t_len = x.shape[0]
    tm = min(512, t_len)
    saved = []
    for l in range(DEPTH):
        h = _matmul(x, w_in_l[l], form="nn", tm=tm, tn=1152, tk=D_MODEL, name=f"in_proj_{l}")
        qkv = _prep_fwd(h, conv_l[l], name=f"prep_fwd_{l}")
        al, dt, nw, sk = _pad_row(a_log[l]), _pad_row(dt_bias[l]), norm_w[l][None, :], _pad_row(sinks[l])
        ya, s_in, t_in = _gdn_fwd(qkv, h, al, dt, nw, name=f"gdn_fwd_{l}")
        yb = _swa_fwd(h, sk, name=f"swa_fwd_{l}")
        ycat = jnp.concatenate([ya, yb], axis=1)
        r, xn = _out_ln_fwd(ycat, w_out_l[l], x, ln_g[l][None, :], ln_b[l][None, :], name=f"out_ln_{l}")
        saved.append((x, h, qkv, (s_in, t_in), ycat, r, al, dt, nw, sk))
        x = xn
    dx, loss_lanes = _loss_head(x, target, name="loss_head")
    grads = [None] * DEPTH
    for l in reversed(range(DEPTH)):
        x_in, h, qkv, (s_in, t_in), ycat, r, al, dt, nw, sk = saved[l]
        dr, d_lng, d_lnb = _ln_bwd(dx, r, ln_g[l][None, :], name=f"ln_bwd_{l}")
        dycat = _matmul(dr, w_out_l[l], form="nt", tm=tm, tn=D_MODEL, tk=D_MODEL, name=f"out_proj_dx_{l}")
        d_wout = _matmul(ycat, dr, form="tn", tm=512, tn=D_MODEL, tk=tm, name=f"out_proj_dw_{l}")
        dqn, dkn, dvn, dza, dba, d_al, d_dt, d_nw = _gdn_bwd(qkv, h, al, dt, nw, s_in, t_in, dycat,
                                                             name=f"gdn_bwd_{l}")
        dqb, dzb, dkb, dvb, d_sk = _swa_bwd(h, sk, dycat, name=f"swa_bwd_{l}")
        dqkv_n = jnp.concatenate([dqn, dkn, dvn], axis=1)
        dqkv, d_conv = _prep_bwd(h, conv_l[l], dqkv_n, name=f"prep_bwd_{l}")
        dh = jnp.concatenate([dqkv, dza, dqb, dzb, dkb, dvb, dba], axis=1)
        d_win = _matmul(x_in, dh, form="tn", tm=512, tn=1152, tk=tm, name=f"in_proj_dw_{l}")
        dx = _matmul(dh, w_in_l[l], form="nt", tm=tm, tn=D_MODEL, tk=1152, name=f"in_proj_dx_{l}",
                     add=dr, add_scale=DEEPNORM_ALPHA)
        grads[l] = dict(w_in=d_win, w_out=d_wout, conv_w=d_conv[:CONV_K], a_log=d_al[0, :A_HEADS],
                        dt_bias=d_dt[0, :A_HEADS], norm_w=d_nw[0], sinks=d_sk[0, :B_Q_HEADS],
                        ln_g=d_lng[0], ln_b=d_lnb[0])
    return loss_lanes, dx, grads


def _me():
    return lax.axis_index("x"), lax.axis_index("y"), lax.axis_index("c")


def _peer(rel):
    x, y, c = _me()
    return (x ^ ((rel >> 2) & 1), y ^ ((rel >> 1) & 1), c ^ (rel & 1))


def _flat_id(pos):
    return 4 * pos[0] + 2 * pos[1] + pos[2]


def _all_gather(shard, *, name):
    rows = shard.shape[0]

    def body(x_ref, out_ref, send_sems, recv_sems, local_sem):
        me = _flat_id(_me())
        own = pltpu.make_async_copy(x_ref, out_ref.at[me], local_sem)
        own.start()
        copies = []
        for rel in range(1, N_DEV):
            cp = pltpu.make_async_remote_copy(
                src_ref=x_ref, dst_ref=out_ref.at[me], send_sem=send_sems.at[rel - 1], recv_sem=recv_sems.at[rel - 1],
                device_id=_peer(rel), device_id_type=pl.DeviceIdType.MESH)
            cp.start()
            copies.append(cp)
        for rel in range(1, N_DEV):
            src = _flat_id(_peer(rel))
            pltpu.make_async_remote_copy(
                src_ref=x_ref, dst_ref=out_ref.at[src], send_sem=send_sems.at[rel - 1],
                recv_sem=recv_sems.at[rel - 1], device_id=_peer(rel), device_id_type=pl.DeviceIdType.MESH).wait_recv()
        for cp in copies:
            cp.wait_send()
        own.wait()

    return pl.pallas_call(
        body, name=name,
        in_specs=[pl.BlockSpec(memory_space=pl.ANY)], out_specs=pl.BlockSpec(memory_space=pl.ANY),
        out_shape=jax.ShapeDtypeStruct((N_DEV, rows, LANE), shard.dtype),
        scratch_shapes=[pltpu.SemaphoreType.DMA((N_DEV - 1,)), pltpu.SemaphoreType.DMA((N_DEV - 1,)),
                        pltpu.SemaphoreType.DMA],
    )(shard)


def _exchange(contrib, *, name):
    rows = contrib.shape[1]

    def body(x_ref, out_ref, send_sems, recv_sems, local_sem):
        me = _flat_id(_me())
        own = pltpu.make_async_copy(x_ref.at[me], out_ref.at[me], local_sem)
        own.start()
        copies = []
        for rel in range(1, N_DEV):
            dst = _flat_id(_peer(rel))
            cp = pltpu.make_async_remote_copy(
                src_ref=x_ref.at[dst], dst_ref=out_ref.at[me], send_sem=send_sems.at[rel - 1],
                recv_sem=recv_sems.at[rel - 1], device_id=_peer(rel), device_id_type=pl.DeviceIdType.MESH)
            cp.start()
            copies.append(cp)
        for rel in range(1, N_DEV):
            src = _flat_id(_peer(rel))
            pltpu.make_async_remote_copy(
                src_ref=x_ref.at[src], dst_ref=out_ref.at[src], send_sem=send_sems.at[rel - 1],
                recv_sem=recv_sems.at[rel - 1], device_id=_peer(rel), device_id_type=pl.DeviceIdType.MESH).wait_recv()
        for cp in copies:
            cp.wait_send()
        own.wait()

    return pl.pallas_call(
        body, name=name,
        in_specs=[pl.BlockSpec(memory_space=pl.ANY)], out_specs=pl.BlockSpec(memory_space=pl.ANY),
        out_shape=jax.ShapeDtypeStruct((N_DEV, rows, LANE), contrib.dtype),
        scratch_shapes=[pltpu.SemaphoreType.DMA((N_DEV - 1,)), pltpu.SemaphoreType.DMA((N_DEV - 1,)),
                        pltpu.SemaphoreType.DMA],
    )(contrib)


def _adamw(gparts, w, m, v, *, name, tr=1024):
    rows = w.shape[0]
    c1 = 1.0 / (1.0 - ADAM_B1 ** ADAM_STEP)
    c2 = 1.0 / (1.0 - ADAM_B2 ** ADAM_STEP)

    def body(g_ref, w_ref, m_ref, v_ref, go_ref, d_ref, mo_ref, vo_ref):
        g = g_ref[0]
        for s in range(1, N_DEV):
            g = g + g_ref[s]
        m_new = ADAM_B1 * m_ref[...] + (1.0 - ADAM_B1) * g
        v_new = ADAM_B2 * v_ref[...] + (1.0 - ADAM_B2) * (g * g)
        go_ref[...] = g
        mo_ref[...] = m_new
        vo_ref[...] = v_new
        d_ref[...] = -ADAM_LR * ((m_new * c1) / (jnp.sqrt(v_new * c2) + ADAM_EPS) + ADAM_WD * w_ref[...])

    tile = pl.BlockSpec((tr, LANE), lambda i: (i, 0))
    return pl.pallas_call(
        body, name=name, grid=(rows // tr,),
        in_specs=[pl.BlockSpec((N_DEV, tr, LANE), lambda i: (0, i, 0)), tile, tile, tile],
        out_specs=[tile] * 4, out_shape=[jax.ShapeDtypeStruct((rows, LANE), F32)] * 4,
        compiler_params=_cparams(("parallel",)),
    )(gparts, w, m, v)


def _pack_shard(w_in, w_out, conv_w, small):
    parts = [w_in.reshape(R_WIN, LANE), w_out.reshape(R_WOUT, LANE),
             jnp.pad(conv_w.reshape(-1), (0, R_CONV * LANE - conv_w.size)).reshape(R_CONV, LANE)]
    flat = jnp.concatenate([s.reshape(-1) for s in small])
    parts.append(jnp.pad(flat, (0, R_SMALL * LANE - flat.shape[0])).reshape(R_SMALL, LANE))
    rows = R_WIN + R_WOUT + R_CONV + R_SMALL
    parts.append(jnp.zeros((R_GRAD - rows, LANE), F32))
    return jnp.concatenate(parts, axis=0)


def _unpack_shard(p):
    w_in = p[:R_WIN].reshape(DEPTH, D_MODEL, SHARD_COLS)
    w_out = p[R_WIN:R_WIN + R_WOUT].reshape(DEPTH, 128, D_MODEL)
    o = R_WIN + R_WOUT
    conv = p[o:o + R_CONV].reshape(-1)[:DEPTH * CONV_K * 192].reshape(DEPTH, CONV_K, 192)
    flat = p[o + R_CONV:o + R_CONV + R_SMALL].reshape(-1)
    small, off = [], 0
    for _, n in SMALL_SIZES:
        small.append(flat[off:off + DEPTH * n].reshape(DEPTH, n))
        off += DEPTH * n
    return w_in, w_out, conv, small


def _to_layout(w_full):
    s = lambda a, b: w_full[..., a:b]
    pad = jnp.zeros(w_full.shape[:-1] + (LANE - 2 * A_HEADS,), w_full.dtype)
    return jnp.concatenate([s(0, 1536), s(1536, 2048), s(2056, 2568), s(2824, 3336), s(2568, 2696), s(2696, 2824),
                            s(2048, 2056), pad], axis=-1)


def _from_layout(g):
    s = lambda a, b: g[..., a:b]
    return jnp.concatenate([s(0, 1536), s(L_ZA, L_ZA + 512), s(L_BA, L_BA + 8), s(L_QB, L_QB + 512),
                            s(L_KB, L_KB + 128), s(L_VB, L_VB + 128), s(L_ZB, L_ZB + 512)], axis=-1)


def kernel(x, w_in, conv_w, a_log, dt_bias, norm_w, sinks, w_out, ln_g, ln_b, loss_target, m_w_in, m_conv_w, m_a_log, m_dt_bias, m_norm_w, m_sinks, m_w_out, m_ln_g, m_ln_b, v_w_in, v_conv_w, v_a_log, v_dt_bias, v_norm_w, v_sinks, v_w_out, v_ln_g, v_ln_b):
    small = [a_log, dt_bias, norm_w, sinks, ln_g, ln_b]
    conv_bits = lax.bitcast_convert_type(conv_w.reshape(-1), BF16).reshape(-1)
    conv_rows = jnp.pad(conv_bits, (0, 2 * R_CONV * LANE - conv_bits.shape[0])).reshape(2 * R_CONV, LANE)
    shard = jnp.concatenate([w_in.astype(BF16).reshape(R_WIN, LANE), w_out.astype(BF16).reshape(R_WOUT, LANE),
                             conv_rows], axis=0)
    gathered = _all_gather(shard, name="weights_all_gather")
    w_in_full = gathered[:, :R_WIN].reshape(N_DEV, DEPTH, D_MODEL, SHARD_COLS).transpose(1, 2, 0, 3)
    w_in_l = _to_layout(w_in_full.reshape(DEPTH, D_MODEL, IN_COLS))
    w_out_l = gathered[:, R_WIN:R_WIN + R_WOUT].reshape(N_DEV, DEPTH, 128, D_MODEL).transpose(1, 0, 2, 3)
    w_out_l = w_out_l.reshape(DEPTH, D_MODEL, D_MODEL)
    conv_full = lax.bitcast_convert_type(
        gathered[:, R_WIN + R_WOUT:].reshape(N_DEV, -1)[:, :2 * DEPTH * CONV_K * 192].reshape(N_DEV, -1, 2), F32)
    conv_full = conv_full.reshape(N_DEV, DEPTH, CONV_K, 192).transpose(1, 2, 0, 3).reshape(DEPTH, CONV_K, 1536)
    conv_l = jnp.pad(conv_full, ((0, 0), (0, 8 - CONV_K), (0, 0)))

    loss_lanes, dx, grads = _local_step(x[0], loss_target[0], w_in_l, w_out_l, conv_l, a_log, dt_bias, norm_w,
                                        sinks, ln_g, ln_b)
    loss = lax.psum(0.5 * jnp.sum(loss_lanes) * (1.0 / D_MODEL), ("x", "y", "c"))

    g_win = jnp.stack([_from_layout(grads[l]["w_in"]) for l in range(DEPTH)])
    g_win = g_win.reshape(DEPTH, D_MODEL, N_DEV, SHARD_COLS).transpose(2, 0, 1, 3).reshape(N_DEV, R_WIN, LANE)
    g_wout = jnp.stack([grads[l]["w_out"] for l in range(DEPTH)])
    g_wout = g_wout.reshape(DEPTH, N_DEV, 128, D_MODEL).transpose(1, 0, 2, 3).reshape(N_DEV, R_WOUT, LANE)
    g_conv = jnp.stack([grads[l]["conv_w"] for l in range(DEPTH)])
    g_conv = g_conv.reshape(DEPTH, CONV_K, N_DEV, 192).transpose(2, 0, 1, 3).reshape(N_DEV, -1)
    g_conv = jnp.pad(g_conv, ((0, 0), (0, R_CONV * LANE - g_conv.shape[1]))).reshape(N_DEV, R_CONV, LANE)
    g_small = jnp.concatenate([jnp.stack([grads[l][n] for l in range(DEPTH)]).reshape(-1) for n, _ in SMALL_SIZES])
    g_small = jnp.pad(g_small, (0, R_SMALL * LANE - g_small.shape[0])).reshape(1, R_SMALL, LANE)
    g_small = jnp.broadcast_to(g_small, (N_DEV, R_SMALL, LANE))
    rows = R_WIN + R_WOUT + R_CONV + R_SMALL
    contrib = jnp.concatenate([g_win, g_wout, g_conv, g_small, jnp.zeros((N_DEV, R_GRAD - rows, LANE), F32)], axis=1)
    parts = _exchange(contrib, name="grad_exchange")

    w_p = _pack_shard(w_in, w_out, conv_w, small)
    m_p = _pack_shard(m_w_in, m_w_out, m_conv_w, [m_a_log, m_dt_bias, m_norm_w, m_sinks, m_ln_g, m_ln_b])
    v_p = _pack_shard(v_w_in, v_w_out, v_conv_w, [v_a_log, v_dt_bias, v_norm_w, v_sinks, v_ln_g, v_ln_b])
    outs = [_unpack_shard(p) for p in _adamw(parts, w_p, m_p, v_p, name="adamw")]

    def ordered(u):
        wi, wo, cv, sm = u
        return [wi, cv, sm[0], sm[1], sm[2], sm[3], wo, sm[4], sm[5]]

    g_o, d_o, m_o, v_o = (ordered(u) for u in outs)
    return (loss, dx[None], *g_o, *d_o, *m_o, *v_o)
```

```python
import functools

import jax
import jax.numpy as jnp
from jax import lax
from jax.experimental import pallas as pl
from jax.experimental.pallas import tpu as pltpu

F32 = jnp.float32
BF16 = jnp.bfloat16
MM_DTYPE = BF16

N_DEV = 8
D_MODEL = 1024
DEPTH = 2
A_HEADS = 4
A_HEAD_DIM = 128
A_WIDTH = 512
CONV_K = 4
CHUNK = 64
SUPER = 256
B_Q_HEADS = 8
B_KV_HEADS = 2
B_HEAD_DIM = 64
B_GROUP = 4
B_WIDTH = 512
WINDOW = 128
BLOCK = 128
IN_COLS = 3336
SHARD_COLS = IN_COLS // N_DEV
OUT_SHARD_ROWS = D_MODEL // N_DEV
CONV_SHARD_COLS = 3 * A_WIDTH // N_DEV
DEEPNORM_ALPHA = (2 * DEPTH) ** 0.25
LN_EPS = 1e-5
RMS_EPS = 1e-6
L2_EPS = 1e-6
ADAM_LR, ADAM_B1, ADAM_B2, ADAM_EPS, ADAM_WD, ADAM_STEP = 0.001, 0.9, 0.999, 1e-08, 0.01, 10

LANE = 128
L_QB, L_ZB, L_KB, L_VB, L_QKV, L_ZA, L_BA = 0, 512, 1024, 1152, 1280, 2816, 3328
L_SWA = 1280
L_MAIN = 3328
L_COLS = 3456
SMALL_SIZES = (("a_log", 4), ("dt_bias", 4), ("norm_w", 128), ("sinks", 8), ("ln_g", 1024), ("ln_b", 1024))
CS_CONV = CONV_K * CONV_SHARD_COLS
CS_ROWS = 24
VMEM_LIMIT = 48 * 1024 * 1024


def _cparams(sem=None):
    return pltpu.CompilerParams(dimension_semantics=sem, vmem_limit_bytes=VMEM_LIMIT)


def _mm(a, b):
    return jnp.dot(a.astype(MM_DTYPE), b.astype(MM_DTYPE), preferred_element_type=F32)


def _mm_nt(a, b):
    return lax.dot_general(a.astype(MM_DTYPE), b.astype(MM_DTYPE), (((1,), (1,)), ((), ())),
                           preferred_element_type=F32)


def _mm_tn(a, b):
    return lax.dot_general(a.astype(MM_DTYPE), b.astype(MM_DTYPE), (((0,), (0,)), ((), ())),
                           preferred_element_type=F32)


def _split(a):
    hi = a.astype(BF16)
    return hi, (a - hi.astype(F32)).astype(BF16)


def _hp(a2, b2):
    d = lambda p, q: jnp.dot(p, q, preferred_element_type=F32)
    return d(a2[0], b2[0]) + (d(a2[0], b2[1]) + d(a2[1], b2[0]))


def _silu(x):
    return x * jax.nn.sigmoid(x)


def _softplus(x):
    return jnp.maximum(x, 0.0) + jnp.log1p(jnp.exp(-jnp.abs(x)))


_ANY = pl.BlockSpec(memory_space=pl.ANY)


def _matmul(a, b, *, form, tm, tn, tk, name, add=None, add_scale=1.0, extra=None):
    if form == "nn":
        (m, kk), n = a.shape, b.shape[1]
        a_spec = pl.BlockSpec((tm, tk), lambda i, j, k: (i, k))
        b_spec = pl.BlockSpec((tk, tn), lambda i, j, k: (k, j))
        dn = (((1,), (0,)), ((), ()))
    elif form == "nt":
        (m, kk), n = a.shape, b.shape[0]
        a_spec = pl.BlockSpec((tm, tk), lambda i, j, k: (i, k))
        b_spec = pl.BlockSpec((tn, tk), lambda i, j, k: (j, k))
        dn = (((1,), (1,)), ((), ()))
    else:
        (kk, m), n = a.shape, b.shape[1]
        a_spec = pl.BlockSpec((tk, tm), lambda i, j, k: (k, i))
        b_spec = pl.BlockSpec((tk, tn), lambda i, j, k: (k, j))
        dn = (((0,), (0,)), ((), ()))
    assert m % tm == 0 and n % tn == 0 and kk % tk == 0, (name, m, n, kk)
    has_add, has_extra = add is not None, extra is not None

    def body(*refs):
        refs = list(refs)
        a_ref, b_ref = refs[:2]
        o_ref = refs[-1]
        rest = refs[2:-1]
        k = pl.program_id(2)
        p = lax.dot_general(a_ref[...].astype(MM_DTYPE), b_ref[...].astype(MM_DTYPE), dn,
                            preferred_element_type=F32)

        @pl.when(k == 0)
        def _():
            first = p
            pos = 0
            if has_extra:
                first = first + _mm_nt(rest[0][...], rest[1][...])
                pos = 2
            if has_add:
                first = first + add_scale * rest[pos][...]
            o_ref[...] = first

        @pl.when(k > 0)
        def _():
            o_ref[...] += p

    in_specs = [a_spec, b_spec]
    args = [a, b]
    if has_extra:
        a2, b2, idx = extra
        in_specs += [pl.BlockSpec((tm, LANE), lambda i, j, k: (i, 0)),
                     pl.BlockSpec((tn, LANE), lambda i, j, k: (j, idx))]
        args += [a2, b2]
    if has_add:
        in_specs.append(pl.BlockSpec((tm, tn), lambda i, j, k: (i, j)))
        args.append(add)
    return pl.pallas_call(
        body, name=name, grid=(m // tm, n // tn, kk // tk), in_specs=in_specs,
        out_specs=pl.BlockSpec((tm, tn), lambda i, j, k: (i, j)),
        out_shape=jax.ShapeDtypeStruct((m, n), F32),
        compiler_params=_cparams(("parallel", "parallel", "arbitrary")),
    )(*args)


def _shift_down(x, k, row):
    return jnp.where(row >= k, pltpu.roll(x, k, 0), 0.0)


def _shift_up(x, k, row, t_len):
    return jnp.where(row < t_len - k, pltpu.roll(x, t_len - k, 0), 0.0)


def _conv_slab(x, w, row):
    return (w[3:4] * x + w[2:3] * _shift_down(x, 1, row) + w[1:2] * _shift_down(x, 2, row)
            + w[0:1] * _shift_down(x, 3, row))


def _prep_fwd(h, conv_w, *, name):
    t_len = h.shape[0]

    def body(x_ref, w_ref, o_ref):
        s = pl.program_id(0)
        row = lax.broadcasted_iota(jnp.int32, (t_len, LANE), 0)
        y = _silu(_conv_slab(x_ref[...], w_ref[...], row))
        rs = lax.rsqrt(jnp.sum(y * y, axis=-1, keepdims=True) + L2_EPS)
        scale = jnp.where(s < A_HEADS, A_HEAD_DIM ** -0.5, 1.0)
        o_ref[...] = jnp.where(s < 2 * A_HEADS, y * rs * scale, y)

    return pl.pallas_call(
        body, name=name, grid=(12,),
        in_specs=[pl.BlockSpec((t_len, LANE), lambda s: (0, L_QKV // LANE + s)),
                  pl.BlockSpec((8, LANE), lambda s: (0, s))],
        out_specs=pl.BlockSpec((t_len, LANE), lambda s: (0, s)),
        out_shape=jax.ShapeDtypeStruct((t_len, 3 * A_WIDTH), F32),
        compiler_params=_cparams(("parallel",)),
    )(h, conv_w)


def _prep_bwd(h, conv_w, d_out, dh, *, name):
    t_len = h.shape[0]

    def body(x_ref, w_ref, g_ref, dh_in, dx_ref, dw_ref):
        del dh_in
        s = pl.program_id(0)
        row = lax.broadcasted_iota(jnp.int32, (t_len, LANE), 0)
        x = x_ref[...]
        w = w_ref[...]
        c = _conv_slab(x, w, row)
        sg = jax.nn.sigmoid(c)
        y = c * sg
        g = g_ref[0]
        rs = lax.rsqrt(jnp.sum(y * y, axis=-1, keepdims=True) + L2_EPS)
        scale = jnp.where(s < A_HEADS, A_HEAD_DIM ** -0.5, 1.0)
        dy_n = scale * (rs * g - y * (rs * rs * rs) * jnp.sum(g * y, axis=-1, keepdims=True))
        dy = jnp.where(s < 2 * A_HEADS, dy_n, g)
        dc = dy * (sg * (1.0 + c * (1.0 - sg)))
        dx_ref[...] = (w[3:4] * dc + w[2:3] * _shift_up(dc, 1, row, t_len)
                       + w[1:2] * _shift_up(dc, 2, row, t_len) + w[0:1] * _shift_up(dc, 3, row, t_len))
        dws = [jnp.sum(dc * _shift_down(x, 3 - j, row), axis=0, keepdims=True) if j < 3
               else jnp.sum(dc * x, axis=0, keepdims=True) for j in range(CONV_K)]
        dw_ref[...] = jnp.concatenate(dws + [jnp.zeros((8 - CONV_K, LANE), F32)], axis=0)

    slab = pl.BlockSpec((t_len, LANE), lambda s: (0, L_QKV // LANE + s))
    return pl.pallas_call(
        body, name=name, grid=(12,),
        in_specs=[slab, pl.BlockSpec((8, LANE), lambda s: (0, s)),
                  pl.BlockSpec((1, t_len, LANE), lambda s: (s // A_HEADS, 0, s % A_HEADS)), _ANY],
        out_specs=[slab, pl.BlockSpec((8, LANE), lambda s: (0, s))],
        out_shape=[jax.ShapeDtypeStruct((t_len, L_MAIN), F32), jax.ShapeDtypeStruct((8, 3 * A_WIDTH), F32)],
        input_output_aliases={3: 0},
        compiler_params=_cparams(("parallel",)),
    )(h, conv_w, d_out, dh)


def _tri_inv_impl(a):
    n = a.shape[0]
    r = lax.broadcasted_iota(jnp.int32, (n, n), 0)
    c = lax.broadcasted_iota(jnp.int32, (n, n), 1)

    def same(shift):
        return (r >> shift) == (c >> shift)

    eye = (r == c).astype(F32)
    a0 = jnp.where(same(3), a, 0.0)
    a0s = _split(a0)
    a2 = _hp(a0s, a0s)
    a2s = _split(a2)
    a4 = _hp(a2s, a2s)
    t = _hp(_split(_hp(_split(eye - a0), _split(eye + a2))), _split(eye + a4))
    for shift in (4, 5, 6):
        low = jnp.where(same(shift) & jnp.logical_not(same(shift - 1)), a, 0.0)
        ts = _split(t)
        t = t - _hp(_split(_hp(ts, _split(low))), ts)
    return t


def _tri_inv_cotangent(t, dt):
    tts = _split(t.T)
    return -_hp(_split(_hp(tts, _split(dt))), tts)


@jax.custom_vjp
def _tri_inv(a):
    return _tri_inv_impl(a)


def _tri_inv_fwd(a):
    t = _tri_inv_impl(a)
    return t, t


_tri_inv.defvjp(_tri_inv_fwd, lambda t, dt: (_tri_inv_cotangent(t, dt),))


@jax.custom_vjp
def _tri_inv_known(a, t):
    return t


def _tri_inv_known_fwd(a, t):
    return t, t


def _tri_inv_known_bwd(t, dt):
    return _tri_inv_cotangent(t, dt), jnp.zeros_like(t)


_tri_inv_known.defvjp(_tri_inv_known_fwd, _tri_inv_known_bwd)


def _mask_times_col(x, kind):
    n = x.shape[0]
    r = lax.broadcasted_iota(jnp.int32, (n, n), 0)
    c = lax.broadcasted_iota(jnp.int32, (n, n), 1)
    same = (r >> 6) == (c >> 6)
    mask = {"cum": same & (r >= c), "cum_t": same & (r <= c), "tot": same}[kind]
    x1 = x.astype(BF16).astype(F32)
    x2 = (x - x1).astype(BF16).astype(F32)
    x3 = (x - x1 - x2).astype(BF16).astype(F32)
    lane = lax.broadcasted_iota(jnp.int32, (1, LANE), 1)
    pieces = jnp.where(lane == 0, x1, jnp.where(lane == 1, x2, jnp.where(lane == 2, x3, 0.0)))
    res = jnp.dot(mask.astype(BF16), pieces.astype(BF16), preferred_element_type=F32)
    return res[:, 0:1] + res[:, 1:2] + res[:, 2:3]


@jax.custom_vjp
def _chunk_sums(g):
    return _mask_times_col(g, "cum"), _mask_times_col(g, "tot")


def _chunk_sums_fwd(g):
    return _chunk_sums(g), None


def _chunk_sums_bwd(_, d):
    return (_mask_times_col(d[0], "cum_t") + _mask_times_col(d[1], "tot"),)


_chunk_sums.defvjp(_chunk_sums_fwd, _chunk_sums_bwd)


def _gdn_block(s, q, k, v, z, ba, alog, dtb, nw, h, t_known=None):
    n = q.shape[0]
    lane = lax.broadcasted_iota(jnp.int32, (1, LANE), 1)

    def pick(x, idx):
        return jnp.sum(jnp.where(lane == idx, x, 0.0), axis=1, keepdims=True)

    beta = jax.nn.sigmoid(pick(ba, h))
    g = -jnp.exp(pick(alog, h)) * _softplus(pick(ba, h + A_HEADS) + pick(dtb, h))
    r = lax.broadcasted_iota(jnp.int32, (n, n), 0)
    c = lax.broadcasted_iota(jnp.int32, (n, n), 1)
    same = (r >> 6) == (c >> 6)
    tril = same & (r >= c)
    stril = same & (r > c)
    gc, gl = _chunk_sums(g)
    decay = jnp.exp(jnp.where(tril, gc - gc.T, -jnp.inf))
    kb = k * beta
    a_mat = jnp.where(stril, _mm_nt(kb, k) * decay, 0.0)
    t_mat = _tri_inv(a_mat) if t_known is None else _tri_inv_known(a_mat, t_known)
    eg = jnp.exp(gc)
    u = _mm(t_mat, v * beta)
    w = _mm(t_mat, kb * eg)
    qk = jnp.where(tril, _mm_nt(q, k) * decay, 0.0)
    q_dec = q * eg
    k_dec = k * jnp.exp(gl - gc)
    g_tot = jnp.exp(gl)
    outs = []
    for ci in range(n // CHUNK):
        lo, hi = ci * CHUNK, (ci + 1) * CHUNK
        v_new = u[lo:hi] - _mm(w[lo:hi], s)
        pieces = []
        if lo:
            pieces.append(jnp.zeros((lo, LANE), F32))
        pieces.append(v_new)
        if n - hi:
            pieces.append(jnp.zeros((n - hi, LANE), F32))
        v_pad = jnp.concatenate(pieces, axis=0) if len(pieces) > 1 else v_new
        outs.append(_mm(q_dec[lo:hi], s) + _mm(qk[lo:hi], v_pad))
        s = s * g_tot[lo:lo + 1] + _mm_tn(k_dec[lo:hi], v_new)
    o = jnp.concatenate(outs, axis=0)
    o = o * lax.rsqrt(jnp.mean(o * o, axis=-1, keepdims=True) + RMS_EPS) * nw
    return o * _silu(z), s, t_mat


def _gdn_fwd(qkv, h, alog, dtb, nw, ycat, *, name):
    t_len = qkv.shape[0]
    nsc = t_len // SUPER

    def body(q_ref, k_ref, v_ref, z_ref, ba_ref, al_ref, dt_ref, nw_ref, y_in, y_ref, sin_ref, t_ref, s_scr):
        del y_in
        sc, hd = pl.program_id(0), pl.program_id(1)

        @pl.when(sc == 0)
        def _():
            s_scr[hd] = jnp.zeros((A_HEAD_DIM, A_HEAD_DIM), F32)

        s = s_scr[hd]
        sin_ref[0, 0] = s
        y, s_new, t_mat = _gdn_block(s, q_ref[...], k_ref[...], v_ref[...], z_ref[...], ba_ref[...],
                                     al_ref[...], dt_ref[...], nw_ref[...], hd)
        y_ref[...] = y
        t_ref[0, 0] = t_mat
        s_scr[hd] = s_new

    blk = lambda off: pl.BlockSpec((SUPER, LANE), lambda sc, hd: (sc, off + hd))
    row = pl.BlockSpec((1, LANE), lambda sc, hd: (0, 0))
    return pl.pallas_call(
        body, name=name, grid=(nsc, A_HEADS),
        in_specs=[blk(0), blk(4), blk(8), blk(L_ZA // LANE),
                  pl.BlockSpec((SUPER, LANE), lambda sc, hd: (sc, L_BA // LANE)), row, row, row, _ANY],
        out_specs=[blk(0),
                   pl.BlockSpec((1, 1, A_HEAD_DIM, A_HEAD_DIM), lambda sc, hd: (sc, hd, 0, 0)),
                   pl.BlockSpec((1, 1, SUPER, SUPER), lambda sc, hd: (sc, hd, 0, 0))],
        out_shape=[jax.ShapeDtypeStruct((t_len, D_MODEL), F32),
                   jax.ShapeDtypeStruct((nsc, A_HEADS, A_HEAD_DIM, A_HEAD_DIM), F32),
                   jax.ShapeDtypeStruct((nsc, A_HEADS, SUPER, SUPER), F32)],
        scratch_shapes=[pltpu.VMEM((A_HEADS, A_HEAD_DIM, A_HEAD_DIM), F32)],
        input_output_aliases={8: 0},
        compiler_params=_cparams(("arbitrary", "arbitrary")),
    )(qkv, qkv, qkv, h, h, alog, dtb, nw, ycat)


def _gdn_bwd(qkv, h, alog, dtb, nw, s_in, t_in, dycat, dh, *, name):
    t_len = qkv.shape[0]
    nsc = t_len // SUPER

    def body(q_ref, k_ref, v_ref, z_ref, ba_ref, al_ref, dt_ref, nw_ref, sin_ref, t_ref, dy_ref, dh_in,
             dz_ref, dqkv_ref, dba_ref, dal_ref, ddt_ref, dnw_ref, ds_scr):
        del dh_in
        i, hd = pl.program_id(0), pl.program_id(1)

        @pl.when(i == 0)
        def _():
            ds_scr[hd] = jnp.zeros((A_HEAD_DIM, A_HEAD_DIM), F32)

        @pl.when((i == 0) & (hd == 0))
        def _():
            dal_ref[...] = jnp.zeros_like(dal_ref)
            ddt_ref[...] = jnp.zeros_like(ddt_ref)
            dnw_ref[...] = jnp.zeros_like(dnw_ref)

        t_known = t_ref[0, 0]
        fn = lambda *a: _gdn_block(*a, h=hd, t_known=t_known)[:2]
        _, vjp = jax.vjp(fn, sin_ref[0, 0], q_ref[...], k_ref[...], v_ref[...], z_ref[...], ba_ref[...],
                         al_ref[...], dt_ref[...], nw_ref[...])
        ds, dq, dk, dv, dz, dba, dal, ddt, dnw = vjp((dy_ref[...], ds_scr[hd]))
        ds_scr[hd] = ds
        dqkv_ref[0] = dq
        dqkv_ref[1] = dk
        dqkv_ref[2] = dv
        dz_ref[...] = dz

        @pl.when(hd == 0)
        def _():
            dba_ref[...] = dba

        @pl.when(hd > 0)
        def _():
            dba_ref[...] += dba

        dal_ref[...] += dal
        ddt_ref[...] += ddt
        dnw_ref[...] += dnw

    rev = lambda i: nsc - 1 - i
    blk = lambda off: pl.BlockSpec((SUPER, LANE), lambda i, hd: (rev(i), off + hd))
    row = pl.BlockSpec((1, LANE), lambda i, hd: (0, 0))
    ba_blk = lambda off: pl.BlockSpec((SUPER, LANE), lambda i, hd: (rev(i), off))
    return pl.pallas_call(
        body, name=name, grid=(nsc, A_HEADS),
        in_specs=[blk(0), blk(4), blk(8), blk(L_ZA // LANE), ba_blk(L_BA // LANE), row, row, row,
                  pl.BlockSpec((1, 1, A_HEAD_DIM, A_HEAD_DIM), lambda i, hd: (rev(i), hd, 0, 0)),
                  pl.BlockSpec((1, 1, SUPER, SUPER), lambda i, hd: (rev(i), hd, 0, 0)),
                  blk(0), _ANY],
        out_specs=[blk(L_ZA // LANE),
                   pl.BlockSpec((3, SUPER, LANE), lambda i, hd: (0, rev(i), hd)),
                   ba_blk(0), row, row, row],
        out_shape=[jax.ShapeDtypeStruct((t_len, L_MAIN), F32), jax.ShapeDtypeStruct((3, t_len, A_WIDTH), F32),
                   jax.ShapeDtypeStruct((t_len, LANE), F32)] + [jax.ShapeDtypeStruct((1, LANE), F32)] * 3,
        scratch_shapes=[pltpu.VMEM((A_HEADS, A_HEAD_DIM, A_HEAD_DIM), F32)],
        input_output_aliases={11: 0},
        compiler_params=_cparams(("arbitrary", "arbitrary")),
    )(qkv, qkv, qkv, h, h, alog, dtb, nw, s_in, t_in, dycat, dh)


def _swa_block(q, kp, kc, vp, vc, z, sinks, first):
    qi = lax.broadcasted_iota(jnp.int32, (BLOCK, 2 * BLOCK), 0)
    si = lax.broadcasted_iota(jnp.int32, (BLOCK, 2 * BLOCK), 1)
    dist = qi + BLOCK - si
    mask = (dist >= 0) & (dist < WINDOW) & ((si >= BLOCK) | jnp.logical_not(first))
    dist_f = dist.astype(F32)
    outs = []
    for j in range(B_KV_HEADS):
        cs = slice(j * B_HEAD_DIM, (j + 1) * B_HEAD_DIM)
        kk = jnp.concatenate([kp[:, cs], kc[:, cs]], axis=0)
        vv = jnp.concatenate([vp[:, cs], vc[:, cs]], axis=0)
        for gi in range(B_GROUP):
            hq = j * B_GROUP + gi
            slope = 2.0 ** (-8.0 * (hq + 1) / B_Q_HEADS)
            sc = _mm_nt(q[:, hq * B_HEAD_DIM:(hq + 1) * B_HEAD_DIM], kk) * (B_HEAD_DIM ** -0.5)
            sc = jnp.where(mask, sc - slope * dist_f, -jnp.inf)
            sink = sinks[:, hq:hq + 1]
            m = lax.stop_gradient(jnp.maximum(jnp.max(sc, axis=-1, keepdims=True), sink))
            p = jnp.exp(sc - m)
            p = p / (jnp.sum(p, axis=-1, keepdims=True) + jnp.exp(sink - m))
            outs.append(_mm(p, vv))
    return jnp.concatenate(outs, axis=1) * _silu(z)


def _swa_specs(idx):
    wide = lambda off: pl.BlockSpec((BLOCK, B_WIDTH), lambda n: (idx(n), off))
    cur = lambda off: pl.BlockSpec((BLOCK, LANE), lambda n: (idx(n), off))
    prev = lambda off: pl.BlockSpec((BLOCK, LANE), lambda n: (jnp.maximum(idx(n) - 1, 0), off))
    return [wide(L_QB // B_WIDTH), prev(L_KB // LANE), cur(L_KB // LANE), prev(L_VB // LANE), cur(L_VB // LANE),
            wide(L_ZB // B_WIDTH), pl.BlockSpec((1, LANE), lambda n: (0, 0))]


def _swa_fwd(h, sinks, *, name):
    t_len = h.shape[0]
    nb = t_len // BLOCK

    def body(q_ref, kp_ref, kc_ref, vp_ref, vc_ref, z_ref, s_ref, o_ref):
        o_ref[...] = _swa_block(q_ref[...], kp_ref[...], kc_ref[...], vp_ref[...], vc_ref[...], z_ref[...],
                                s_ref[...], pl.program_id(0) == 0)

    return pl.pallas_call(
        body, name=name, grid=(nb,), in_specs=_swa_specs(lambda n: n),
        out_specs=pl.BlockSpec((BLOCK, B_WIDTH), lambda n: (n, 1)),
        out_shape=jax.ShapeDtypeStruct((t_len, D_MODEL), F32),
        compiler_params=_cparams(("parallel",)),
    )(h, h, h, h, h, h, sinks)


def _swa_bwd(h, sinks, dycat, *, name):
    t_len = h.shape[0]
    nb = t_len // BLOCK

    def body(q_ref, kp_ref, kc_ref, vp_ref, vc_ref, z_ref, s_ref, dy_ref, dh_ref, dsk_ref, ck_scr, cv_scr):
        i = pl.program_id(0)
        n = nb - 1 - i

        @pl.when(i == 0)
        def _():
            ck_scr[...] = jnp.zeros_like(ck_scr)
            cv_scr[...] = jnp.zeros_like(cv_scr)
            dsk_ref[...] = jnp.zeros_like(dsk_ref)

        fn = functools.partial(_swa_block, first=(n == 0))
        _, vjp = jax.vjp(fn, q_ref[...], kp_ref[...], kc_ref[...], vp_ref[...], vc_ref[...], z_ref[...], s_ref[...])
        dq, dkp, dkc, dvp, dvc, dz, dsk = vjp(dy_ref[...])
        dh_ref[:, L_QB:L_QB + B_WIDTH] = dq
        dh_ref[:, L_ZB:L_ZB + B_WIDTH] = dz
        dh_ref[:, L_KB:L_KB + LANE] = dkc + ck_scr[...]
        dh_ref[:, L_VB:L_VB + LANE] = dvc + cv_scr[...]
        ck_scr[...] = dkp
        cv_scr[...] = dvp
        dsk_ref[...] += dsk

    rev = lambda i: nb - 1 - i
    return pl.pallas_call(
        body, name=name, grid=(nb,),
        in_specs=_swa_specs(rev) + [pl.BlockSpec((BLOCK, B_WIDTH), lambda i: (rev(i), 1))],
        out_specs=[pl.BlockSpec((BLOCK, L_SWA), lambda i: (rev(i), 0)), pl.BlockSpec((1, LANE), lambda i: (0, 0))],
        out_shape=[jax.ShapeDtypeStruct((t_len, L_MAIN), F32), jax.ShapeDtypeStruct((1, LANE), F32)],
        scratch_shapes=[pltpu.VMEM((BLOCK, LANE), F32), pltpu.VMEM((BLOCK, LANE), F32)],
        compiler_params=_cparams(("arbitrary",)),
    )(h, h, h, h, h, h, sinks, dycat)


def _out_ln_fwd(ycat, w_out, x, ln_g, ln_b, *, name, tm=256):
    t_len = x.shape[0]

    def body(y_ref, w_ref, x_ref, g_ref, b_ref, r_ref, o_ref):
        r = DEEPNORM_ALPHA * x_ref[...] + _mm(y_ref[...], w_ref[...])
        r_ref[...] = r
        mu = jnp.mean(r, axis=-1, keepdims=True)
        d = r - mu
        var = jnp.mean(d * d, axis=-1, keepdims=True)
        o_ref[...] = d * lax.rsqrt(var + LN_EPS) * g_ref[...] + b_ref[...]

    tile = pl.BlockSpec((tm, D_MODEL), lambda i: (i, 0))
    vec = pl.BlockSpec((1, D_MODEL), lambda i: (0, 0))
    return pl.pallas_call(
        body, name=name, grid=(t_len // tm,),
        in_specs=[tile, pl.BlockSpec((D_MODEL, D_MODEL), lambda i: (0, 0)), tile, vec, vec],
        out_specs=[tile, tile],
        out_shape=[jax.ShapeDtypeStruct((t_len, D_MODEL), F32)] * 2,
        compiler_params=_cparams(("parallel",)),
    )(ycat, w_out, x, ln_g, ln_b)


def _ln_bwd(dxn, r, ln_g, *, name, tm=256):
    t_len = r.shape[0]

    def body(dx_ref, r_ref, g_ref, dr_ref, dg_ref, db_ref):
        @pl.when(pl.program_id(0) == 0)
        def _():
            dg_ref[...] = jnp.zeros_like(dg_ref)
            db_ref[...] = jnp.zeros_like(db_ref)

        rr = r_ref[...]
        dx = dx_ref[...]
        mu = jnp.mean(rr, axis=-1, keepdims=True)
        d = rr - mu
        rstd = lax.rsqrt(jnp.mean(d * d, axis=-1, keepdims=True) + LN_EPS)
        xh = d * rstd
        dxh = dx * g_ref[...]
        dr_ref[...] = rstd * (dxh - jnp.mean(dxh, axis=-1, keepdims=True)
                              - xh * jnp.mean(dxh * xh, axis=-1, keepdims=True))
        dg_ref[...] += jnp.sum(dx * xh, axis=0, keepdims=True)
        db_ref[...] += jnp.sum(dx, axis=0, keepdims=True)

    tile = pl.BlockSpec((tm, D_MODEL), lambda i: (i, 0))
    vec = pl.BlockSpec((1, D_MODEL), lambda i: (0, 0))
    return pl.pallas_call(
        body, name=name, grid=(t_len // tm,),
        in_specs=[tile, tile, vec], out_specs=[tile, vec, vec],
        out_shape=[jax.ShapeDtypeStruct((t_len, D_MODEL), F32), jax.ShapeDtypeStruct((1, D_MODEL), F32),
                   jax.ShapeDtypeStruct((1, D_MODEL), F32)],
        compiler_params=_cparams(("arbitrary",)),
    )(dxn, r, ln_g)


def _loss_head(y, target, *, name, tm=256):
    t_len = y.shape[0]

    def body(y_ref, t_ref, d_ref, l_ref):
        @pl.when(pl.program_id(0) == 0)
        def _():
            l_ref[...] = jnp.zeros_like(l_ref)

        e = y_ref[...] - t_ref[...]
        d_ref[...] = e * (1.0 / D_MODEL)
        l_ref[...] += jnp.sum(e * e, axis=0, keepdims=True)

    tile = pl.BlockSpec((tm, D_MODEL), lambda i: (i, 0))
    vec = pl.BlockSpec((1, D_MODEL), lambda i: (0, 0))
    return pl.pallas_call(
        body, name=name, grid=(t_len // tm,), in_specs=[tile, tile], out_specs=[tile, vec],
        out_shape=[jax.ShapeDtypeStruct((t_len, D_MODEL), F32), jax.ShapeDtypeStruct((1, D_MODEL), F32)],
        compiler_params=_cparams(("arbitrary",)),
    )(y, target)


def _pad_row(v):
    return jnp.zeros((1, LANE), F32).at[0, :v.shape[0]].set(v)


def _to_layout(w_full):
    s = lambda a, b: w_full[..., a:b]
    pad = jnp.zeros(w_full.shape[:-1] + (LANE - 2 * A_HEADS,), w_full.dtype)
    return jnp.concatenate([s(2056, 2568), s(2824, 3336), s(2568, 2696), s(2696, 2824), s(0, 1536), s(1536, 2048),
                            s(2048, 2056), pad], axis=-1)


def _from_layout(g_main, g_ba):
    s = lambda a, b: g_main[..., a:b]
    return jnp.concatenate([s(L_QKV, L_QKV + 1536), s(L_ZA, L_ZA + 512), g_ba[..., :2 * A_HEADS],
                            s(L_QB, L_QB + 512), s(L_KB, L_KB + 128), s(L_VB, L_VB + 128), s(L_ZB, L_ZB + 512)],
                           axis=-1)


def _forward(x, weights, small):
    a_log, dt_bias, norm_w, sinks, ln_g, ln_b = small
    tm = min(512, x.shape[0])
    saved = []
    for l in range(DEPTH):
        w_in_l, w_out_l, conv_l = weights[l]
        h = _matmul(x, w_in_l, form="nn", tm=tm, tn=1152, tk=D_MODEL, name=f"in_proj_{l}")
        qkv = _prep_fwd(h, conv_l, name=f"prep_fwd_{l}")
        al, dt, nw, sk = _pad_row(a_log[l]), _pad_row(dt_bias[l]), norm_w[l][None, :], _pad_row(sinks[l])
        ycat = _swa_fwd(h, sk, name=f"swa_fwd_{l}")
        ycat, s_in, t_in = _gdn_fwd(qkv, h, al, dt, nw, ycat, name=f"gdn_fwd_{l}")
        r, xn = _out_ln_fwd(ycat, w_out_l, x, ln_g[l][None, :], ln_b[l][None, :], name=f"out_ln_{l}")
        saved.append((x, h, qkv, s_in, t_in, ycat, r, al, dt, nw, sk))
        x = xn
    return x, saved


def _backward_layer(l, dx, saved_l, weights_l, ln_g_l):
    x_in, h, qkv, s_in, t_in, ycat, r, al, dt, nw, sk = saved_l
    w_in_l, w_out_l, conv_l = weights_l
    tm = min(512, x_in.shape[0])
    dr, d_lng, d_lnb = _ln_bwd(dx, r, ln_g_l[None, :], name=f"ln_bwd_{l}")
    dycat = _matmul(dr, w_out_l, form="nt", tm=tm, tn=D_MODEL, tk=D_MODEL, name=f"out_proj_dx_{l}")
    d_wout = _matmul(ycat, dr, form="tn", tm=512, tn=D_MODEL, tk=tm, name=f"out_proj_dw_{l}")
    dh, d_sk = _swa_bwd(h, sk, dycat, name=f"swa_bwd_{l}")
    dh, dqkv_n, dba, d_al, d_dt, d_nw = _gdn_bwd(qkv, h, al, dt, nw, s_in, t_in, dycat, dh, name=f"gdn_bwd_{l}")
    dh, d_conv = _prep_bwd(h, conv_l, dqkv_n, dh, name=f"prep_bwd_{l}")
    d_win_main = _matmul(x_in, dh, form="tn", tm=512, tn=L_MAIN // 2, tk=tm, name=f"in_proj_dw_{l}")
    d_win_ba = _matmul(x_in, dba, form="tn", tm=D_MODEL, tn=LANE, tk=tm, name=f"in_proj_dw_ba_{l}")
    dx = _matmul(dh, w_in_l, form="nt", tm=tm, tn=D_MODEL, tk=L_MAIN // 2, name=f"in_proj_dx_{l}",
                 add=dr, add_scale=DEEPNORM_ALPHA, extra=(dba, w_in_l, L_BA // LANE))
    grads = dict(w_in=_from_layout(d_win_main, d_win_ba), w_out=d_wout, conv_w=d_conv[:CONV_K],
                 a_log=d_al[0, :A_HEADS], dt_bias=d_dt[0, :A_HEADS], norm_w=d_nw[0], sinks=d_sk[0, :B_Q_HEADS],
                 ln_g=d_lng[0], ln_b=d_lnb[0])
    return dx, grads


def _me():
    return lax.axis_index("x"), lax.axis_index("y"), lax.axis_index("c")


def _flat_id(pos):
    return 4 * pos[0] + 2 * pos[1] + pos[2]


def _remote(src, dst, send_sem, recv_sem, to):
    return pltpu.make_async_remote_copy(src_ref=src, dst_ref=dst, send_sem=send_sem, recv_sem=recv_sem,
                                        device_id=to, device_id_type=pl.DeviceIdType.MESH)


def _all_gather(shards, *, name):
    n_arr = len(shards)

    def body(*refs):
        x_refs, out_refs = refs[:n_arr], refs[n_arr:2 * n_arr]
        send_sems, recv_sems, local_sems = refs[2 * n_arr:]
        x, y, c = _me()
        me, sibling = (x, y, c), (x, y, 1 - c)
        chips = [(1 - x, y), (x, 1 - y), (1 - x, 1 - y)]

        def copy(a, k, block, to, src=None):
            dst = out_refs[a].at[_flat_id(block)]
            return _remote(dst if src is None else src, dst, send_sems.at[a, k], recv_sems.at[a, k], to)

        mine = [pltpu.make_async_copy(x_refs[a], out_refs[a].at[_flat_id(me)], local_sems.at[a])
                for a in range(n_arr)]
        for cp in mine:
            cp.start()
        first = []
        for a in range(n_arr):
            first.append(copy(a, 0, me, sibling, src=x_refs[a]))
            first += [copy(a, 1 + j, me, (*chip, c), src=x_refs[a]) for j, chip in enumerate(chips)]
        for cp in first:
            cp.start()
        passed = []
        for j, chip in enumerate(chips):
            for a in range(n_arr):
                copy(a, 1 + j, (*chip, c), me).wait_recv()
                fwd = copy(a, 4 + j, (*chip, c), sibling)
                fwd.start()
                passed.append(fwd)
        for a in range(n_arr):
            copy(a, 0, sibling, me).wait_recv()
            for j, chip in enumerate(chips):
                copy(a, 4 + j, (*chip, 1 - c), me).wait_recv()
        for cp in first + passed:
            cp.wait_send()
        for cp in mine:
            cp.wait()

    return pl.pallas_call(
        body, name=name, in_specs=[_ANY] * n_arr, out_specs=[_ANY] * n_arr,
        out_shape=[jax.ShapeDtypeStruct((N_DEV,) + s.shape, s.dtype) for s in shards],
        scratch_shapes=[pltpu.SemaphoreType.DMA((n_arr, N_DEV - 1)), pltpu.SemaphoreType.DMA((n_arr, N_DEV - 1)),
                        pltpu.SemaphoreType.DMA((n_arr,))],
    )(*shards)


def _exchange(contribs, bufs, layer, *, name):
    n_arr = len(contribs)
    create = bufs is None

    def body(*refs):
        x_refs = refs[:n_arr]
        out_refs = refs[n_arr:2 * n_arr] if create else refs[2 * n_arr:3 * n_arr]
        send_sems, recv_sems, local_sems = refs[-3:]
        x, y, c = _me()
        me = _flat_id((x, y, c))
        peers = [(x ^ ((rel >> 2) & 1), y ^ ((rel >> 1) & 1), c ^ (rel & 1)) for rel in range(1, N_DEV)]
        mine = [pltpu.make_async_copy(x_refs[a].at[me], out_refs[a].at[layer, me], local_sems.at[a])
                for a in range(n_arr)]
        for cp in mine:
            cp.start()
        copies = []
        for a in range(n_arr):
            for k, peer in enumerate(peers):
                cp = _remote(x_refs[a].at[_flat_id(peer)], out_refs[a].at[layer, me], send_sems.at[a, k],
                             recv_sems.at[a, k], peer)
                cp.start()
                copies.append(cp)
        for a in range(n_arr):
            for k, peer in enumerate(peers):
                src = _flat_id(peer)
                _remote(x_refs[a].at[src], out_refs[a].at[layer, src], send_sems.at[a, k], recv_sems.at[a, k],
                        peer).wait_recv()
        for cp in copies:
            cp.wait_send()
        for cp in mine:
            cp.wait()

    out_shape = [jax.ShapeDtypeStruct((DEPTH,) + c.shape, c.dtype) for c in contribs]
    args = list(contribs) + ([] if create else list(bufs))
    return pl.pallas_call(
        body, name=name, in_specs=[_ANY] * len(args), out_specs=[_ANY] * n_arr, out_shape=out_shape,
        input_output_aliases={} if create else {n_arr + a: a for a in range(n_arr)},
        scratch_shapes=[pltpu.SemaphoreType.DMA((n_arr, N_DEV - 1)), pltpu.SemaphoreType.DMA((n_arr, N_DEV - 1)),
                        pltpu.SemaphoreType.DMA((n_arr,))],
    )(*args)


def _adamw(parts, w, m, v, *, tr, name):
    depth, rows, cols = w.shape
    c1 = 1.0 - ADAM_B1 ** ADAM_STEP
    c2 = 1.0 - ADAM_B2 ** ADAM_STEP

    def body(g_ref, w_ref, m_ref, v_ref, go_ref, d_ref, mo_ref, vo_ref):
        g = g_ref[0, 0].astype(F32)
        for s in range(1, N_DEV):
            g = g + g_ref[0, s].astype(F32)
        m_new = ADAM_B1 * m_ref[0] + (1.0 - ADAM_B1) * g
        v_new = ADAM_B2 * v_ref[0] + (1.0 - ADAM_B2) * (g * g)
        go_ref[0] = g
        mo_ref[0] = m_new
        vo_ref[0] = v_new
        d_ref[0] = -ADAM_LR * ((m_new / c1) / (jnp.sqrt(v_new / c2) + ADAM_EPS) + ADAM_WD * w_ref[0])

    tile = pl.BlockSpec((1, tr, cols), lambda l, i: (l, i, 0))
    return pl.pallas_call(
        body, name=name, grid=(depth, rows // tr),
        in_specs=[pl.BlockSpec((1, N_DEV, tr, cols), lambda l, i: (l, 0, i, 0)), tile, tile, tile],
        out_specs=[tile] * 4, out_shape=[jax.ShapeDtypeStruct(w.shape, F32)] * 4,
        compiler_params=_cparams(("parallel", "parallel")),
    )(parts, w, m, v)


def _pack_small(conv, small):
    lead = conv.shape[:-2]
    flat = jnp.concatenate([conv.reshape(lead + (CS_CONV,))] + list(small), axis=-1)
    pad = CS_ROWS * LANE - flat.shape[-1]
    flat = jnp.concatenate([flat, jnp.zeros(lead + (pad,), F32)], axis=-1)
    return flat.reshape(lead + (CS_ROWS, LANE))


def _unpack_small(p):
    flat = p.reshape(DEPTH, CS_ROWS * LANE)
    conv = flat[:, :CS_CONV].reshape(DEPTH, CONV_K, CONV_SHARD_COLS)
    small, off = [], CS_CONV
    for _, n in SMALL_SIZES:
        small.append(flat[:, off:off + n])
        off += n
    return conv, small


def kernel(x, w_in, conv_w, a_log, dt_bias, norm_w, sinks, w_out, ln_g, ln_b, loss_target, m_w_in, m_conv_w, m_a_log, m_dt_bias, m_norm_w, m_sinks, m_w_out, m_ln_g, m_ln_b, v_w_in, v_conv_w, v_a_log, v_dt_bias, v_norm_w, v_sinks, v_w_out, v_ln_g, v_ln_b):
    small = [a_log, dt_bias, norm_w, sinks, ln_g, ln_b]
    weights = []
    for l in range(DEPTH):
        g_in, g_out, g_conv = _all_gather([w_in[l].astype(BF16), w_out[l].astype(BF16), conv_w[l]],
                                          name=f"weights_all_gather_{l}")
        w_in_l = _to_layout(g_in.transpose(1, 0, 2).reshape(D_MODEL, IN_COLS))
        w_out_l = g_out.reshape(D_MODEL, D_MODEL)
        conv_l = jnp.pad(g_conv.transpose(1, 0, 2).reshape(CONV_K, 3 * A_WIDTH), ((0, 8 - CONV_K), (0, 0)))
        weights.append((w_in_l, w_out_l, conv_l))

    y, saved = _forward(x[0], weights, small)
    dx, loss_lanes = _loss_head(y, loss_target[0], name="loss_head")
    loss = lax.psum(0.5 * jnp.sum(loss_lanes) * (1.0 / D_MODEL), ("x", "y", "c"))
    bufs = None
    for l in reversed(range(DEPTH)):
        dx, g = _backward_layer(l, dx, saved[l], weights[l], ln_g[l])
        c_in = g["w_in"].reshape(D_MODEL, N_DEV, SHARD_COLS).transpose(1, 0, 2).astype(BF16)
        c_out = g["w_out"].astype(BF16).reshape(N_DEV, OUT_SHARD_ROWS, D_MODEL)
        c_conv = g["conv_w"].reshape(CONV_K, N_DEV, CONV_SHARD_COLS).transpose(1, 0, 2)
        c_small = [jnp.broadcast_to(g[n][None], (N_DEV,) + g[n].shape) for n, _ in SMALL_SIZES]
        bufs = _exchange([c_in, c_out, _pack_small(c_conv, c_small)], bufs, l, name=f"grad_exchange_{l}")

    p_in, p_out, p_small = bufs
    o_in = _adamw(p_in, w_in, m_w_in, v_w_in, tr=256, name="adamw_w_in")
    o_out = _adamw(p_out, w_out, m_w_out, v_w_out, tr=OUT_SHARD_ROWS, name="adamw_w_out")
    o_small = _adamw(p_small, _pack_small(conv_w, small),
                     _pack_small(m_conv_w, [m_a_log, m_dt_bias, m_norm_w, m_sinks, m_ln_g, m_ln_b]),
                     _pack_small(v_conv_w, [v_a_log, v_dt_bias, v_norm_w, v_sinks, v_ln_g, v_ln_b]),
                     tr=CS_ROWS, name="adamw_small")
    outs = []
    for k in range(4):
        cv, sm = _unpack_small(o_small[k])
        outs += [o_in[k], cv, sm[0], sm[1], sm[2], sm[3], o_out[k], sm[4], sm[5]]
    return (loss, dx[None], *outs)
```

```python
import functools

import jax
import jax.numpy as jnp
from jax import lax
from jax.experimental import pallas as pl
from jax.experimental.pallas import tpu as pltpu

F32 = jnp.float32
BF16 = jnp.bfloat16
MM_DTYPE = BF16

N_DEV = 8
D_MODEL = 1024
DEPTH = 2
A_HEADS = 4
A_HEAD_DIM = 128
A_WIDTH = 512
CONV_K = 4
CHUNK = 64
SUPER = 256
G_HEADS = 2
B_Q_HEADS = 8
B_KV_HEADS = 2
B_HEAD_DIM = 64
B_GROUP = 4
B_WIDTH = 512
WINDOW = 128
BLOCK = 128
IN_COLS = 3336
SHARD_COLS = IN_COLS // N_DEV
OUT_SHARD_ROWS = D_MODEL // N_DEV
CONV_SHARD_COLS = 3 * A_WIDTH // N_DEV
DEEPNORM_ALPHA = (2 * DEPTH) ** 0.25
LN_EPS = 1e-5
RMS_EPS = 1e-6
L2_EPS = 1e-6
ADAM_LR, ADAM_B1, ADAM_B2, ADAM_EPS, ADAM_WD, ADAM_STEP = 0.001, 0.9, 0.999, 1e-08, 0.01, 10

LANE = 128
L_QB, L_ZB, L_KB, L_VB, L_QKV, L_ZA, L_BA = 0, 512, 1024, 1152, 1280, 2816, 3328
L_SWA = 1280
L_MAIN = 3328
L_COLS = 3456
SMALL_SIZES = (("a_log", 4), ("dt_bias", 4), ("norm_w", 128), ("sinks", 8), ("ln_g", 1024), ("ln_b", 1024))
CS_CONV = CONV_K * CONV_SHARD_COLS
CS_ROWS = 24
VMEM_LIMIT = 48 * 1024 * 1024


def _cparams(sem=None):
    return pltpu.CompilerParams(dimension_semantics=sem, vmem_limit_bytes=VMEM_LIMIT)


def _mm(a, b):
    return jnp.dot(a.astype(MM_DTYPE), b.astype(MM_DTYPE), preferred_element_type=F32)


def _mm_nt(a, b):
    return lax.dot_general(a.astype(MM_DTYPE), b.astype(MM_DTYPE), (((1,), (1,)), ((), ())),
                           preferred_element_type=F32)


def _mm_tn(a, b):
    return lax.dot_general(a.astype(MM_DTYPE), b.astype(MM_DTYPE), (((0,), (0,)), ((), ())),
                           preferred_element_type=F32)


def _split(a):
    hi = a.astype(BF16)
    return hi, (a - hi.astype(F32)).astype(BF16)


def _hp(a2, b2):
    d = lambda p, q: jnp.dot(p, q, preferred_element_type=F32)
    return d(a2[0], b2[0]) + (d(a2[0], b2[1]) + d(a2[1], b2[0]))


def _silu(x):
    return x * jax.nn.sigmoid(x)


def _softplus(x):
    return jnp.maximum(x, 0.0) + jnp.log1p(jnp.exp(-jnp.abs(x)))


_ANY = pl.BlockSpec(memory_space=pl.ANY)


def _matmul(a, b, *, form, tm, tn, tk, name, add=None, add_scale=1.0, extra=None):
    if form == "nn":
        (m, kk), n = a.shape, b.shape[1]
        a_spec = pl.BlockSpec((tm, tk), lambda i, j, k: (i, k))
        b_spec = pl.BlockSpec((tk, tn), lambda i, j, k: (k, j))
        dn = (((1,), (0,)), ((), ()))
    elif form == "nt":
        (m, kk), n = a.shape, b.shape[0]
        a_spec = pl.BlockSpec((tm, tk), lambda i, j, k: (i, k))
        b_spec = pl.BlockSpec((tn, tk), lambda i, j, k: (j, k))
        dn = (((1,), (1,)), ((), ()))
    else:
        (kk, m), n = a.shape, b.shape[1]
        a_spec = pl.BlockSpec((tk, tm), lambda i, j, k: (k, i))
        b_spec = pl.BlockSpec((tk, tn), lambda i, j, k: (k, j))
        dn = (((0,), (0,)), ((), ()))
    assert m % tm == 0 and n % tn == 0 and kk % tk == 0, (name, m, n, kk)
    has_add, has_extra = add is not None, extra is not None

    def body(*refs):
        refs = list(refs)
        a_ref, b_ref = refs[:2]
        o_ref = refs[-1]
        rest = refs[2:-1]
        k = pl.program_id(2)
        p = lax.dot_general(a_ref[...].astype(MM_DTYPE), b_ref[...].astype(MM_DTYPE), dn,
                            preferred_element_type=F32)

        @pl.when(k == 0)
        def _():
            first = p
            pos = 0
            if has_extra:
                first = first + _mm_nt(rest[0][...], rest[1][...])
                pos = 2
            if has_add:
                first = first + add_scale * rest[pos][...]
            o_ref[...] = first

        @pl.when(k > 0)
        def _():
            o_ref[...] += p

    in_specs = [a_spec, b_spec]
    args = [a, b]
    if has_extra:
        a2, b2, idx = extra
        in_specs += [pl.BlockSpec((tm, LANE), lambda i, j, k: (i, 0)),
                     pl.BlockSpec((tn, LANE), lambda i, j, k: (j, idx))]
        args += [a2, b2]
    if has_add:
        in_specs.append(pl.BlockSpec((tm, tn), lambda i, j, k: (i, j)))
        args.append(add)
    return pl.pallas_call(
        body, name=name, grid=(m // tm, n // tn, kk // tk), in_specs=in_specs,
        out_specs=pl.BlockSpec((tm, tn), lambda i, j, k: (i, j)),
        out_shape=jax.ShapeDtypeStruct((m, n), F32),
        compiler_params=_cparams(("parallel", "parallel", "arbitrary")),
    )(*args)


def _shift_down(x, k, row):
    return jnp.where(row >= k, pltpu.roll(x, k, 0), 0.0)


def _shift_up(x, k, row, t_len):
    return jnp.where(row < t_len - k, pltpu.roll(x, t_len - k, 0), 0.0)


def _conv_slab(x, w, row):
    return (w[3:4] * x + w[2:3] * _shift_down(x, 1, row) + w[1:2] * _shift_down(x, 2, row)
            + w[0:1] * _shift_down(x, 3, row))


def _prep_fwd(h, conv_w, *, name):
    t_len = h.shape[0]

    def body(x_ref, w_ref, o_ref):
        s = pl.program_id(0)
        row = lax.broadcasted_iota(jnp.int32, (t_len, LANE), 0)
        y = _silu(_conv_slab(x_ref[...], w_ref[...], row))
        rs = lax.rsqrt(jnp.sum(y * y, axis=-1, keepdims=True) + L2_EPS)
        scale = jnp.where(s < A_HEADS, A_HEAD_DIM ** -0.5, 1.0)
        o_ref[...] = jnp.where(s < 2 * A_HEADS, y * rs * scale, y)

    return pl.pallas_call(
        body, name=name, grid=(12,),
        in_specs=[pl.BlockSpec((t_len, LANE), lambda s: (0, L_QKV // LANE + s)),
                  pl.BlockSpec((8, LANE), lambda s: (0, s))],
        out_specs=pl.BlockSpec((t_len, LANE), lambda s: (0, s)),
        out_shape=jax.ShapeDtypeStruct((t_len, 3 * A_WIDTH), F32),
        compiler_params=_cparams(("parallel",)),
    )(h, conv_w)


def _prep_bwd(h, conv_w, d_out, dh, *, name):
    t_len = h.shape[0]

    def body(x_ref, w_ref, g_ref, dh_in, dx_ref, dw_ref):
        del dh_in
        s = pl.program_id(0)
        row = lax.broadcasted_iota(jnp.int32, (t_len, LANE), 0)
        x = x_ref[...]
        w = w_ref[...]
        c = _conv_slab(x, w, row)
        sg = jax.nn.sigmoid(c)
        y = c * sg
        g = g_ref[0]
        rs = lax.rsqrt(jnp.sum(y * y, axis=-1, keepdims=True) + L2_EPS)
        scale = jnp.where(s < A_HEADS, A_HEAD_DIM ** -0.5, 1.0)
        dy_n = scale * (rs * g - y * (rs * rs * rs) * jnp.sum(g * y, axis=-1, keepdims=True))
        dy = jnp.where(s < 2 * A_HEADS, dy_n, g)
        dc = dy * (sg * (1.0 + c * (1.0 - sg)))
        dx_ref[...] = (w[3:4] * dc + w[2:3] * _shift_up(dc, 1, row, t_len)
                       + w[1:2] * _shift_up(dc, 2, row, t_len) + w[0:1] * _shift_up(dc, 3, row, t_len))
        dws = [jnp.sum(dc * _shift_down(x, 3 - j, row), axis=0, keepdims=True) if j < 3
               else jnp.sum(dc * x, axis=0, keepdims=True) for j in range(CONV_K)]
        dw_ref[...] = jnp.concatenate(dws + [jnp.zeros((8 - CONV_K, LANE), F32)], axis=0)

    slab = pl.BlockSpec((t_len, LANE), lambda s: (0, L_QKV // LANE + s))
    return pl.pallas_call(
        body, name=name, grid=(12,),
        in_specs=[slab, pl.BlockSpec((8, LANE), lambda s: (0, s)),
                  pl.BlockSpec((1, t_len, LANE), lambda s: (s // A_HEADS, 0, s % A_HEADS)), _ANY],
        out_specs=[slab, pl.BlockSpec((8, LANE), lambda s: (0, s))],
        out_shape=[jax.ShapeDtypeStruct((t_len, L_MAIN), F32), jax.ShapeDtypeStruct((8, 3 * A_WIDTH), F32)],
        input_output_aliases={3: 0},
        compiler_params=_cparams(("parallel",)),
    )(h, conv_w, d_out, dh)


MF_TRIL, MF_STRIL, MF_DIAG8, MF_LOW16, MF_EYE = 0, 1, 2, 3, 6
MB_CUM, MB_CUM_T, MB_TOT = 0, 1, 2


def _gdn_masks():
    r = lax.broadcasted_iota(jnp.int32, (SUPER, SUPER), 0)
    c = lax.broadcasted_iota(jnp.int32, (SUPER, SUPER), 1)
    same = lambda shift: (r >> shift) == (c >> shift)
    chunk = same(6)
    ninf = lambda m: jnp.where(m, 0.0, -jnp.inf).astype(F32)
    one = lambda m: m.astype(F32)
    mf = jnp.stack([ninf(chunk & (r >= c)), ninf(chunk & (r > c)), one(same(3))]
                   + [one(same(sh) & jnp.logical_not(same(sh - 1))) for sh in (4, 5, 6)] + [one(r == c)])
    mb = jnp.stack([one(chunk & (r >= c)), one(chunk & (r <= c)), one(chunk)]).astype(BF16)
    return mf, mb


def _tri_inv_impl(a, mf):
    eye = mf[MF_EYE]
    a0 = a * mf[MF_DIAG8]
    a0s = _split(a0)
    a2 = _hp(a0s, a0s)
    a2s = _split(a2)
    a4 = _hp(a2s, a2s)
    t = _hp(_split(_hp(_split(eye - a0), _split(eye + a2))), _split(eye + a4))
    for level in range(3):
        ts = _split(t)
        t = t - _hp(_split(_hp(ts, _split(a * mf[MF_LOW16 + level]))), ts)
    return t


def _tri_inv_cotangent(t, dt):
    tts = _split(t.T)
    return -_hp(_split(_hp(tts, _split(dt))), tts)


@jax.custom_vjp
def _tri_inv(a, mf):
    return _tri_inv_impl(a, mf)


def _tri_inv_fwd(a, mf):
    t = _tri_inv_impl(a, mf)
    return t, (t, mf)


_tri_inv.defvjp(_tri_inv_fwd, lambda res, dt: (_tri_inv_cotangent(res[0], dt), jnp.zeros_like(res[1])))


@jax.custom_vjp
def _tri_inv_known(a, t):
    return t


def _tri_inv_known_fwd(a, t):
    return t, t


def _tri_inv_known_bwd(t, dt):
    return _tri_inv_cotangent(t, dt), jnp.zeros_like(t)


_tri_inv_known.defvjp(_tri_inv_known_fwd, _tri_inv_known_bwd)


def _mask_times_col(x, mask):
    x1 = x.astype(BF16).astype(F32)
    x2 = (x - x1).astype(BF16).astype(F32)
    x3 = (x - x1 - x2).astype(BF16).astype(F32)
    lane = lax.broadcasted_iota(jnp.int32, (1, LANE), 1)
    pieces = jnp.where(lane == 0, x1, jnp.where(lane == 1, x2, jnp.where(lane == 2, x3, 0.0)))
    res = jnp.dot(mask, pieces.astype(BF16), preferred_element_type=F32)
    return res[:, 0:1] + res[:, 1:2] + res[:, 2:3]


@jax.custom_vjp
def _chunk_sums(g, mb):
    return _mask_times_col(g, mb[MB_CUM]), _mask_times_col(g, mb[MB_TOT])


def _chunk_sums_fwd(g, mb):
    return _chunk_sums(g, mb), mb


def _chunk_sums_bwd(mb, d):
    return _mask_times_col(d[0], mb[MB_CUM_T]) + _mask_times_col(d[1], mb[MB_TOT]), jnp.zeros_like(mb)


_chunk_sums.defvjp(_chunk_sums_fwd, _chunk_sums_bwd)


def _gdn_block(s, q, k, v, z, ba, alog, dtb, nw, h, t_known, mf, mb):
    n = q.shape[0]
    lane = lax.broadcasted_iota(jnp.int32, (1, LANE), 1)

    def pick(x, idx):
        return jnp.sum(jnp.where(lane == idx, x, 0.0), axis=1, keepdims=True)

    beta = jax.nn.sigmoid(pick(ba, h))
    g = -jnp.exp(pick(alog, h)) * _softplus(pick(ba, h + A_HEADS) + pick(dtb, h))
    gc, gl = _chunk_sums(g, mb)
    diff = gc - gc.T
    decay = jnp.exp(diff + mf[MF_TRIL])
    kb = k * beta
    a_mat = _mm_nt(kb, k) * jnp.exp(diff + mf[MF_STRIL])
    t_mat = _tri_inv(a_mat, mf) if t_known is None else _tri_inv_known(a_mat, t_known)
    eg = jnp.exp(gc)
    u = _mm(t_mat, v * beta)
    w = _mm(t_mat, kb * eg)
    qk = _mm_nt(q, k) * decay
    q_dec = q * eg
    k_dec = k * jnp.exp(gl - gc)
    g_tot = jnp.exp(gl)
    outs = []
    for ci in range(n // CHUNK):
        lo, hi = ci * CHUNK, (ci + 1) * CHUNK
        v_new = u[lo:hi] - _mm(w[lo:hi], s)
        pieces = []
        if lo:
            pieces.append(jnp.zeros((lo, LANE), F32))
        pieces.append(v_new)
        if n - hi:
            pieces.append(jnp.zeros((n - hi, LANE), F32))
        v_pad = jnp.concatenate(pieces, axis=0) if len(pieces) > 1 else v_new
        outs.append(_mm(q_dec[lo:hi], s) + _mm(qk[lo:hi], v_pad))
        s = s * g_tot[lo:lo + 1] + _mm_tn(k_dec[lo:hi], v_new)
    o = jnp.concatenate(outs, axis=0)
    o = o * lax.rsqrt(jnp.mean(o * o, axis=-1, keepdims=True) + RMS_EPS) * nw
    return o * _silu(z), s, t_mat


def _gdn_fwd(qkv, h, alog, dtb, nw, ycat, *, name):
    t_len = qkv.shape[0]
    nsc = t_len // SUPER

    def body(q_ref, k_ref, v_ref, z_ref, ba_ref, al_ref, dt_ref, nw_ref, mf_ref, mb_ref, y_in,
             y_ref, sin_ref, t_ref, s_scr):
        del y_in
        sc, pg = pl.program_id(0), pl.program_id(1)
        heads = [pg * G_HEADS + hh for hh in range(G_HEADS)]

        @pl.when(sc == 0)
        def _():
            for hd in heads:
                s_scr[hd] = jnp.zeros((A_HEAD_DIM, A_HEAD_DIM), F32)

        per_head = lambda ref: jnp.stack([ref[:, hh * LANE:(hh + 1) * LANE] for hh in range(G_HEADS)])
        states = jnp.stack([s_scr[hd] for hd in heads])
        fn = jax.vmap(_gdn_block, in_axes=(0, 0, 0, 0, 0, None, None, None, None, 0, None, None, None))
        y, s_new, t_mat = fn(states, per_head(q_ref), per_head(k_ref), per_head(v_ref), per_head(z_ref),
                             ba_ref[...], al_ref[...], dt_ref[...], nw_ref[...], jnp.stack(heads), None,
                             mf_ref[...], mb_ref[...])
        sin_ref[0] = states
        t_ref[0] = t_mat
        for hh, hd in enumerate(heads):
            y_ref[:, hh * LANE:(hh + 1) * LANE] = y[hh]
            s_scr[hd] = s_new[hh]

    gw = G_HEADS * LANE
    blk = lambda off: pl.BlockSpec((SUPER, gw), lambda sc, pg: (sc, off // G_HEADS + pg))
    row = pl.BlockSpec((1, LANE), lambda sc, pg: (0, 0))
    mf, mb = _gdn_masks()
    whole = lambda a: pl.BlockSpec(a.shape, lambda sc, pg: (0, 0, 0))
    return pl.pallas_call(
        body, name=name, grid=(nsc, A_HEADS // G_HEADS),
        in_specs=[blk(0), blk(4), blk(8), blk(L_ZA // LANE),
                  pl.BlockSpec((SUPER, LANE), lambda sc, pg: (sc, L_BA // LANE)), row, row, row,
                  whole(mf), whole(mb), _ANY],
        out_specs=[blk(0),
                   pl.BlockSpec((1, G_HEADS, A_HEAD_DIM, A_HEAD_DIM), lambda sc, pg: (sc, pg, 0, 0)),
                   pl.BlockSpec((1, G_HEADS, SUPER, SUPER), lambda sc, pg: (sc, pg, 0, 0))],
        out_shape=[jax.ShapeDtypeStruct((t_len, D_MODEL), F32),
                   jax.ShapeDtypeStruct((nsc, A_HEADS, A_HEAD_DIM, A_HEAD_DIM), F32),
                   jax.ShapeDtypeStruct((nsc, A_HEADS, SUPER, SUPER), F32)],
        scratch_shapes=[pltpu.VMEM((A_HEADS, A_HEAD_DIM, A_HEAD_DIM), F32)],
        input_output_aliases={10: 0},
        compiler_params=_cparams(("arbitrary", "arbitrary")),
    )(qkv, qkv, qkv, h, h, alog, dtb, nw, mf, mb, ycat)


def _gdn_bwd(qkv, h, alog, dtb, nw, s_in, t_in, dycat, dh, *, name):
    t_len = qkv.shape[0]
    nsc = t_len // SUPER

    def body(q_ref, k_ref, v_ref, z_ref, ba_ref, al_ref, dt_ref, nw_ref, sin_ref, t_ref, dy_ref, mf_ref, mb_ref,
             dh_in, dz_ref, dqkv_ref, dba_ref, dal_ref, ddt_ref, dnw_ref, ds_scr):
        del dh_in
        i, pg = pl.program_id(0), pl.program_id(1)

        @pl.when((i == 0) & (pg == 0))
        def _():
            dal_ref[...] = jnp.zeros_like(dal_ref)
            ddt_ref[...] = jnp.zeros_like(ddt_ref)
            dnw_ref[...] = jnp.zeros_like(dnw_ref)

        @pl.when(pg == 0)
        def _():
            dba_ref[...] = jnp.zeros_like(dba_ref)

        heads = [pg * G_HEADS + hh for hh in range(G_HEADS)]

        @pl.when(i == 0)
        def _():
            for hd in heads:
                ds_scr[hd] = jnp.zeros((A_HEAD_DIM, A_HEAD_DIM), F32)

        per_head = lambda ref: jnp.stack([ref[:, hh * LANE:(hh + 1) * LANE] for hh in range(G_HEADS)])
        d_states = jnp.stack([ds_scr[hd] for hd in heads])
        head_ids = jnp.stack(heads)
        t_known, mf, mb = t_ref[0], mf_ref[...], mb_ref[...]

        def fn(s, q, k, v, z, ba, alog, dtb, nw):
            one = lambda s, q, k, v, z, t, h: _gdn_block(s, q, k, v, z, ba, alog, dtb, nw, h, t, mf, mb)[:2]
            return jax.vmap(one)(s, q, k, v, z, t_known, head_ids)

        _, vjp = jax.vjp(fn, sin_ref[0], per_head(q_ref), per_head(k_ref), per_head(v_ref), per_head(z_ref),
                         ba_ref[...], al_ref[...], dt_ref[...], nw_ref[...])
        ds, dq, dk, dv, dz, dba, dal, ddt, dnw = vjp((per_head(dy_ref), d_states))
        for hh, hd in enumerate(heads):
            cols = slice(hh * LANE, (hh + 1) * LANE)
            ds_scr[hd] = ds[hh]
            dqkv_ref[0, :, cols] = dq[hh]
            dqkv_ref[1, :, cols] = dk[hh]
            dqkv_ref[2, :, cols] = dv[hh]
            dz_ref[:, cols] = dz[hh]
        dba_ref[...] += dba
        dal_ref[...] += dal
        ddt_ref[...] += ddt
        dnw_ref[...] += dnw

    rev = lambda i: nsc - 1 - i
    gw = G_HEADS * LANE
    blk = lambda off: pl.BlockSpec((SUPER, gw), lambda i, pg: (rev(i), off // G_HEADS + pg))
    row = pl.BlockSpec((1, LANE), lambda i, pg: (0, 0))
    ba_blk = lambda off: pl.BlockSpec((SUPER, LANE), lambda i, pg: (rev(i), off))
    mf, mb = _gdn_masks()
    whole = lambda a: pl.BlockSpec(a.shape, lambda i, pg: (0, 0, 0))
    return pl.pallas_call(
        body, name=name, grid=(nsc, A_HEADS // G_HEADS),
        in_specs=[blk(0), blk(4), blk(8), blk(L_ZA // LANE), ba_blk(L_BA // LANE), row, row, row,
                  pl.BlockSpec((1, G_HEADS, A_HEAD_DIM, A_HEAD_DIM), lambda i, pg: (rev(i), pg, 0, 0)),
                  pl.BlockSpec((1, G_HEADS, SUPER, SUPER), lambda i, pg: (rev(i), pg, 0, 0)),
                  blk(0), whole(mf), whole(mb), _ANY],
        out_specs=[blk(L_ZA // LANE),
                   pl.BlockSpec((3, SUPER, gw), lambda i, pg: (0, rev(i), pg)),
                   ba_blk(0), row, row, row],
        out_shape=[jax.ShapeDtypeStruct((t_len, L_MAIN), F32), jax.ShapeDtypeStruct((3, t_len, A_WIDTH), F32),
                   jax.ShapeDtypeStruct((t_len, LANE), F32)] + [jax.ShapeDtypeStruct((1, LANE), F32)] * 3,
        scratch_shapes=[pltpu.VMEM((A_HEADS, A_HEAD_DIM, A_HEAD_DIM), F32)],
        input_output_aliases={13: 0},
        compiler_params=_cparams(("arbitrary", "arbitrary")),
    )(qkv, qkv, qkv, h, h, alog, dtb, nw, s_in, t_in, dycat, mf, mb, dh)


def _swa_block(q, kp, kc, vp, vc, z, sinks, first):
    qi = lax.broadcasted_iota(jnp.int32, (BLOCK, 2 * BLOCK), 0)
    si = lax.broadcasted_iota(jnp.int32, (BLOCK, 2 * BLOCK), 1)
    dist = qi + BLOCK - si
    mask = (dist >= 0) & (dist < WINDOW) & ((si >= BLOCK) | jnp.logical_not(first))
    dist_f = dist.astype(F32)
    outs = []
    for j in range(B_KV_HEADS):
        cs = slice(j * B_HEAD_DIM, (j + 1) * B_HEAD_DIM)
        kk = jnp.concatenate([kp[:, cs], kc[:, cs]], axis=0)
        vv = jnp.concatenate([vp[:, cs], vc[:, cs]], axis=0)
        for gi in range(B_GROUP):
            hq = j * B_GROUP + gi
            slope = 2.0 ** (-8.0 * (hq + 1) / B_Q_HEADS)
            sc = _mm_nt(q[:, hq * B_HEAD_DIM:(hq + 1) * B_HEAD_DIM], kk) * (B_HEAD_DIM ** -0.5)
            sc = jnp.where(mask, sc - slope * dist_f, -jnp.inf)
            sink = sinks[:, hq:hq + 1]
            m = lax.stop_gradient(jnp.maximum(jnp.max(sc, axis=-1, keepdims=True), sink))
            p = jnp.exp(sc - m)
            p = p / (jnp.sum(p, axis=-1, keepdims=True) + jnp.exp(sink - m))
            outs.append(_mm(p, vv))
    return jnp.concatenate(outs, axis=1) * _silu(z)


def _swa_specs(idx):
    wide = lambda off: pl.BlockSpec((BLOCK, B_WIDTH), lambda n: (idx(n), off))
    cur = lambda off: pl.BlockSpec((BLOCK, LANE), lambda n: (idx(n), off))
    prev = lambda off: pl.BlockSpec((BLOCK, LANE), lambda n: (jnp.maximum(idx(n) - 1, 0), off))
    return [wide(L_QB // B_WIDTH), prev(L_KB // LANE), cur(L_KB // LANE), prev(L_VB // LANE), cur(L_VB // LANE),
            wide(L_ZB // B_WIDTH), pl.BlockSpec((1, LANE), lambda n: (0, 0))]


def _swa_fwd(h, sinks, *, name):
    t_len = h.shape[0]
    nb = t_len // BLOCK

    def body(q_ref, kp_ref, kc_ref, vp_ref, vc_ref, z_ref, s_ref, o_ref):
        o_ref[...] = _swa_block(q_ref[...], kp_ref[...], kc_ref[...], vp_ref[...], vc_ref[...], z_ref[...],
                                s_ref[...], pl.program_id(0) == 0)

    return pl.pallas_call(
        body, name=name, grid=(nb,), in_specs=_swa_specs(lambda n: n),
        out_specs=pl.BlockSpec((BLOCK, B_WIDTH), lambda n: (n, 1)),
        out_shape=jax.ShapeDtypeStruct((t_len, D_MODEL), F32),
        compiler_params=_cparams(("parallel",)),
    )(h, h, h, h, h, h, sinks)


def _swa_bwd(h, sinks, dycat, *, name):
    t_len = h.shape[0]
    nb = t_len // BLOCK

    def body(q_ref, kp_ref, kc_ref, vp_ref, vc_ref, z_ref, s_ref, dy_ref, dh_ref, dsk_ref, ck_scr, cv_scr):
        i = pl.program_id(0)
        n = nb - 1 - i

        @pl.when(i == 0)
        def _():
            ck_scr[...] = jnp.zeros_like(ck_scr)
            cv_scr[...] = jnp.zeros_like(cv_scr)
            dsk_ref[...] = jnp.zeros_like(dsk_ref)

        fn = functools.partial(_swa_block, first=(n == 0))
        _, vjp = jax.vjp(fn, q_ref[...], kp_ref[...], kc_ref[...], vp_ref[...], vc_ref[...], z_ref[...], s_ref[...])
        dq, dkp, dkc, dvp, dvc, dz, dsk = vjp(dy_ref[...])
        dh_ref[:, L_QB:L_QB + B_WIDTH] = dq
        dh_ref[:, L_ZB:L_ZB + B_WIDTH] = dz
        dh_ref[:, L_KB:L_KB + LANE] = dkc + ck_scr[...]
        dh_ref[:, L_VB:L_VB + LANE] = dvc + cv_scr[...]
        ck_scr[...] = dkp
        cv_scr[...] = dvp
        dsk_ref[...] += dsk

    rev = lambda i: nb - 1 - i
    return pl.pallas_call(
        body, name=name, grid=(nb,),
        in_specs=_swa_specs(rev) + [pl.BlockSpec((BLOCK, B_WIDTH), lambda i: (rev(i), 1))],
        out_specs=[pl.BlockSpec((BLOCK, L_SWA), lambda i: (rev(i), 0)), pl.BlockSpec((1, LANE), lambda i: (0, 0))],
        out_shape=[jax.ShapeDtypeStruct((t_len, L_MAIN), F32), jax.ShapeDtypeStruct((1, LANE), F32)],
        scratch_shapes=[pltpu.VMEM((BLOCK, LANE), F32), pltpu.VMEM((BLOCK, LANE), F32)],
        compiler_params=_cparams(("arbitrary",)),
    )(h, h, h, h, h, h, sinks, dycat)


def _out_ln_fwd(ycat, w_out, x, ln_g, ln_b, *, name, tm=256):
    t_len = x.shape[0]

    def body(y_ref, w_ref, x_ref, g_ref, b_ref, r_ref, o_ref):
        r = DEEPNORM_ALPHA * x_ref[...] + _mm(y_ref[...], w_ref[...])
        r_ref[...] = r
        mu = jnp.mean(r, axis=-1, keepdims=True)
        d = r - mu
        var = jnp.mean(d * d, axis=-1, keepdims=True)
        o_ref[...] = d * lax.rsqrt(var + LN_EPS) * g_ref[...] + b_ref[...]

    tile = pl.BlockSpec((tm, D_MODEL), lambda i: (i, 0))
    vec = pl.BlockSpec((1, D_MODEL), lambda i: (0, 0))
    return pl.pallas_call(
        body, name=name, grid=(t_len // tm,),
        in_specs=[tile, pl.BlockSpec((D_MODEL, D_MODEL), lambda i: (0, 0)), tile, vec, vec],
        out_specs=[tile, tile],
        out_shape=[jax.ShapeDtypeStruct((t_len, D_MODEL), F32)] * 2,
        compiler_params=_cparams(("parallel",)),
    )(ycat, w_out, x, ln_g, ln_b)


def _ln_bwd(dxn, r, ln_g, *, name, tm=256):
    t_len = r.shape[0]

    def body(dx_ref, r_ref, g_ref, dr_ref, dg_ref, db_ref):
        @pl.when(pl.program_id(0) == 0)
        def _():
            dg_ref[...] = jnp.zeros_like(dg_ref)
            db_ref[...] = jnp.zeros_like(db_ref)

        rr = r_ref[...]
        dx = dx_ref[...]
        mu = jnp.mean(rr, axis=-1, keepdims=True)
        d = rr - mu
        rstd = lax.rsqrt(jnp.mean(d * d, axis=-1, keepdims=True) + LN_EPS)
        xh = d * rstd
        dxh = dx * g_ref[...]
        dr_ref[...] = rstd * (dxh - jnp.mean(dxh, axis=-1, keepdims=True)
                              - xh * jnp.mean(dxh * xh, axis=-1, keepdims=True))
        dg_ref[...] += jnp.sum(dx * xh, axis=0, keepdims=True)
        db_ref[...] += jnp.sum(dx, axis=0, keepdims=True)

    tile = pl.BlockSpec((tm, D_MODEL), lambda i: (i, 0))
    vec = pl.BlockSpec((1, D_MODEL), lambda i: (0, 0))
    return pl.pallas_call(
        body, name=name, grid=(t_len // tm,),
        in_specs=[tile, tile, vec], out_specs=[tile, vec, vec],
        out_shape=[jax.ShapeDtypeStruct((t_len, D_MODEL), F32), jax.ShapeDtypeStruct((1, D_MODEL), F32),
                   jax.ShapeDtypeStruct((1, D_MODEL), F32)],
        compiler_params=_cparams(("arbitrary",)),
    )(dxn, r, ln_g)


def _loss_head(y, target, *, name, tm=256):
    t_len = y.shape[0]

    def body(y_ref, t_ref, d_ref, l_ref):
        @pl.when(pl.program_id(0) == 0)
        def _():
            l_ref[...] = jnp.zeros_like(l_ref)

        e = y_ref[...] - t_ref[...]
        d_ref[...] = e * (1.0 / D_MODEL)
        l_ref[...] += jnp.sum(e * e, axis=0, keepdims=True)

    tile = pl.BlockSpec((tm, D_MODEL), lambda i: (i, 0))
    vec = pl.BlockSpec((1, D_MODEL), lambda i: (0, 0))
    return pl.pallas_call(
        body, name=name, grid=(t_len // tm,), in_specs=[tile, tile], out_specs=[tile, vec],
        out_shape=[jax.ShapeDtypeStruct((t_len, D_MODEL), F32), jax.ShapeDtypeStruct((1, D_MODEL), F32)],
        compiler_params=_cparams(("arbitrary",)),
    )(y, target)


def _pad_row(v):
    return jnp.zeros((1, LANE), F32).at[0, :v.shape[0]].set(v)


def _to_layout(w_full):
    s = lambda a, b: w_full[..., a:b]
    pad = jnp.zeros(w_full.shape[:-1] + (LANE - 2 * A_HEADS,), w_full.dtype)
    return jnp.concatenate([s(2056, 2568), s(2824, 3336), s(2568, 2696), s(2696, 2824), s(0, 1536), s(1536, 2048),
                            s(2048, 2056), pad], axis=-1)


def _from_layout(g_main, g_ba):
    s = lambda a, b: g_main[..., a:b]
    return jnp.concatenate([s(L_QKV, L_QKV + 1536), s(L_ZA, L_ZA + 512), g_ba[..., :2 * A_HEADS],
                            s(L_QB, L_QB + 512), s(L_KB, L_KB + 128), s(L_VB, L_VB + 128), s(L_ZB, L_ZB + 512)],
                           axis=-1)


def _forward(x, weights, small):
    a_log, dt_bias, norm_w, sinks, ln_g, ln_b = small
    tm = min(512, x.shape[0])
    saved = []
    for l in range(DEPTH):
        w_in_l, w_out_l, conv_l = weights[l]
        h = _matmul(x, w_in_l, form="nn", tm=tm, tn=1152, tk=D_MODEL, name=f"in_proj_{l}")
        qkv = _prep_fwd(h, conv_l, name=f"prep_fwd_{l}")
        al, dt, nw, sk = _pad_row(a_log[l]), _pad_row(dt_bias[l]), norm_w[l][None, :], _pad_row(sinks[l])
        ycat = _swa_fwd(h, sk, name=f"swa_fwd_{l}")
        ycat, s_in, t_in = _gdn_fwd(qkv, h, al, dt, nw, ycat, name=f"gdn_fwd_{l}")
        r, xn = _out_ln_fwd(ycat, w_out_l, x, ln_g[l][None, :], ln_b[l][None, :], name=f"out_ln_{l}")
        saved.append((x, h, qkv, s_in, t_in, ycat, r, al, dt, nw, sk))
        x = xn
    return x, saved


def _backward_layer(l, dx, saved_l, weights_l, ln_g_l):
    x_in, h, qkv, s_in, t_in, ycat, r, al, dt, nw, sk = saved_l
    w_in_l, w_out_l, conv_l = weights_l
    tm = min(512, x_in.shape[0])
    dr, d_lng, d_lnb = _ln_bwd(dx, r, ln_g_l[None, :], name=f"ln_bwd_{l}")
    dycat = _matmul(dr, w_out_l, form="nt", tm=tm, tn=D_MODEL, tk=D_MODEL, name=f"out_proj_dx_{l}")
    d_wout = _matmul(ycat, dr, form="tn", tm=512, tn=D_MODEL, tk=tm, name=f"out_proj_dw_{l}")
    dh, d_sk = _swa_bwd(h, sk, dycat, name=f"swa_bwd_{l}")
    dh, dqkv_n, dba, d_al, d_dt, d_nw = _gdn_bwd(qkv, h, al, dt, nw, s_in, t_in, dycat, dh, name=f"gdn_bwd_{l}")
    dh, d_conv = _prep_bwd(h, conv_l, dqkv_n, dh, name=f"prep_bwd_{l}")
    d_win_main = _matmul(x_in, dh, form="tn", tm=512, tn=L_MAIN // 2, tk=tm, name=f"in_proj_dw_{l}")
    d_win_ba = _matmul(x_in, dba, form="tn", tm=D_MODEL, tn=LANE, tk=tm, name=f"in_proj_dw_ba_{l}")
    dx = _matmul(dh, w_in_l, form="nt", tm=tm, tn=D_MODEL, tk=L_MAIN // 2, name=f"in_proj_dx_{l}",
                 add=dr, add_scale=DEEPNORM_ALPHA, extra=(dba, w_in_l, L_BA // LANE))
    grads = dict(w_in=_from_layout(d_win_main, d_win_ba), w_out=d_wout, conv_w=d_conv[:CONV_K],
                 a_log=d_al[0, :A_HEADS], dt_bias=d_dt[0, :A_HEADS], norm_w=d_nw[0], sinks=d_sk[0, :B_Q_HEADS],
                 ln_g=d_lng[0], ln_b=d_lnb[0])
    return dx, grads


def _me():
    return lax.axis_index("x"), lax.axis_index("y"), lax.axis_index("c")


def _flat_id(pos):
    return 4 * pos[0] + 2 * pos[1] + pos[2]


def _remote(src, dst, send_sem, recv_sem, to):
    return pltpu.make_async_remote_copy(src_ref=src, dst_ref=dst, send_sem=send_sem, recv_sem=recv_sem,
                                        device_id=to, device_id_type=pl.DeviceIdType.MESH)


def _all_gather(shards, *, name):
    n_arr = len(shards)

    def body(*refs):
        x_refs, out_refs = refs[:n_arr], refs[n_arr:2 * n_arr]
        send_sems, recv_sems, local_sems = refs[2 * n_arr:]
        x, y, c = _me()
        me, sibling = (x, y, c), (x, y, 1 - c)
        chips = [(1 - x, y), (x, 1 - y), (1 - x, 1 - y)]

        def copy(a, k, block, to, src=None):
            dst = out_refs[a].at[_flat_id(block)]
            return _remote(dst if src is None else src, dst, send_sems.at[a, k], recv_sems.at[a, k], to)

        mine = [pltpu.make_async_copy(x_refs[a], out_refs[a].at[_flat_id(me)], local_sems.at[a])
                for a in range(n_arr)]
        for cp in mine:
            cp.start()
        first = []
        for a in range(n_arr):
            first.append(copy(a, 0, me, sibling, src=x_refs[a]))
            first += [copy(a, 1 + j, me, (*chip, c), src=x_refs[a]) for j, chip in enumerate(chips)]
        for cp in first:
            cp.start()
        passed = []
        for j, chip in enumerate(chips):
            for a in range(n_arr):
                copy(a, 1 + j, (*chip, c), me).wait_recv()
                fwd = copy(a, 4 + j, (*chip, c), sibling)
                fwd.start()
                passed.append(fwd)
        for a in range(n_arr):
            copy(a, 0, sibling, me).wait_recv()
            for j, chip in enumerate(chips):
                copy(a, 4 + j, (*chip, 1 - c), me).wait_recv()
        for cp in first + passed:
            cp.wait_send()
        for cp in mine:
            cp.wait()

    return pl.pallas_call(
        body, name=name, in_specs=[_ANY] * n_arr, out_specs=[_ANY] * n_arr,
        out_shape=[jax.ShapeDtypeStruct((N_DEV,) + s.shape, s.dtype) for s in shards],
        scratch_shapes=[pltpu.SemaphoreType.DMA((n_arr, N_DEV - 1)), pltpu.SemaphoreType.DMA((n_arr, N_DEV - 1)),
                        pltpu.SemaphoreType.DMA((n_arr,))],
    )(*shards)


def _exchange(contribs, bufs, layer, *, name):
    n_arr = len(contribs)
    create = bufs is None

    def body(*refs):
        x_refs = refs[:n_arr]
        out_refs = refs[n_arr:2 * n_arr] if create else refs[2 * n_arr:3 * n_arr]
        send_sems, recv_sems, local_sems = refs[-3:]
        x, y, c = _me()
        me = _flat_id((x, y, c))
        peers = [(x ^ ((rel >> 2) & 1), y ^ ((rel >> 1) & 1), c ^ (rel & 1)) for rel in range(1, N_DEV)]
        mine = [pltpu.make_async_copy(x_refs[a].at[me], out_refs[a].at[layer, me], local_sems.at[a])
                for a in range(n_arr)]
        for cp in mine:
            cp.start()
        copies = []
        for a in range(n_arr):
            for k, peer in enumerate(peers):
                cp = _remote(x_refs[a].at[_flat_id(peer)], out_refs[a].at[layer, me], send_sems.at[a, k],
                             recv_sems.at[a, k], peer)
                cp.start()
                copies.append(cp)
        for a in range(n_arr):
            for k, peer in enumerate(peers):
                src = _flat_id(peer)
                _remote(x_refs[a].at[src], out_refs[a].at[layer, src], send_sems.at[a, k], recv_sems.at[a, k],
                        peer).wait_recv()
        for cp in copies:
            cp.wait_send()
        for cp in mine:
            cp.wait()

    out_shape = [jax.ShapeDtypeStruct((DEPTH,) + c.shape, c.dtype) for c in contribs]
    args = list(contribs) + ([] if create else list(bufs))
    return pl.pallas_call(
        body, name=name, in_specs=[_ANY] * len(args), out_specs=[_ANY] * n_arr, out_shape=out_shape,
        input_output_aliases={} if create else {n_arr + a: a for a in range(n_arr)},
        scratch_shapes=[pltpu.SemaphoreType.DMA((n_arr, N_DEV - 1)), pltpu.SemaphoreType.DMA((n_arr, N_DEV - 1)),
                        pltpu.SemaphoreType.DMA((n_arr,))],
    )(*args)


def _adamw(parts, w, m, v, *, tr, name):
    depth, rows, cols = w.shape
    c1 = 1.0 - ADAM_B1 ** ADAM_STEP
    c2 = 1.0 - ADAM_B2 ** ADAM_STEP

    def body(g_ref, w_ref, m_ref, v_ref, go_ref, d_ref, mo_ref, vo_ref):
        g = g_ref[0, 0].astype(F32)
        for s in range(1, N_DEV):
            g = g + g_ref[0, s].astype(F32)
        m_new = ADAM_B1 * m_ref[0] + (1.0 - ADAM_B1) * g
        v_new = ADAM_B2 * v_ref[0] + (1.0 - ADAM_B2) * (g * g)
        go_ref[0] = g
        mo_ref[0] = m_new
        vo_ref[0] = v_new
        d_ref[0] = -ADAM_LR * ((m_new / c1) / (jnp.sqrt(v_new / c2) + ADAM_EPS) + ADAM_WD * w_ref[0])

    tile = pl.BlockSpec((1, tr, cols), lambda l, i: (l, i, 0))
    return pl.pallas_call(
        body, name=name, grid=(depth, rows // tr),
        in_specs=[pl.BlockSpec((1, N_DEV, tr, cols), lambda l, i: (l, 0, i, 0)), tile, tile, tile],
        out_specs=[tile] * 4, out_shape=[jax.ShapeDtypeStruct(w.shape, F32)] * 4,
        compiler_params=_cparams(("parallel", "parallel")),
    )(parts, w, m, v)


def _pack_small(conv, small):
    lead = conv.shape[:-2]
    flat = jnp.concatenate([conv.reshape(lead + (CS_CONV,))] + list(small), axis=-1)
    pad = CS_ROWS * LANE - flat.shape[-1]
    flat = jnp.concatenate([flat, jnp.zeros(lead + (pad,), F32)], axis=-1)
    return flat.reshape(lead + (CS_ROWS, LANE))


def _unpack_small(p):
    flat = p.reshape(DEPTH, CS_ROWS * LANE)
    conv = flat[:, :CS_CONV].reshape(DEPTH, CONV_K, CONV_SHARD_COLS)
    small, off = [], CS_CONV
    for _, n in SMALL_SIZES:
        small.append(flat[:, off:off + n])
        off += n
    return conv, small


def kernel(x, w_in, conv_w, a_log, dt_bias, norm_w, sinks, w_out, ln_g, ln_b, loss_target, m_w_in, m_conv_w, m_a_log, m_dt_bias, m_norm_w, m_sinks, m_w_out, m_ln_g, m_ln_b, v_w_in, v_conv_w, v_a_log, v_dt_bias, v_norm_w, v_sinks, v_w_out, v_ln_g, v_ln_b):
    small = [a_log, dt_bias, norm_w, sinks, ln_g, ln_b]
    weights = []
    for l in range(DEPTH):
        g_in, g_out, g_conv = _all_gather([w_in[l].astype(BF16), w_out[l].astype(BF16), conv_w[l]],
                                          name=f"weights_all_gather_{l}")
        w_in_l = _to_layout(g_in.transpose(1, 0, 2).reshape(D_MODEL, IN_COLS))
        w_out_l = g_out.reshape(D_MODEL, D_MODEL)
        conv_l = jnp.pad(g_conv.transpose(1, 0, 2).reshape(CONV_K, 3 * A_WIDTH), ((0, 8 - CONV_K), (0, 0)))
        weights.append((w_in_l, w_out_l, conv_l))

    y, saved = _forward(x[0], weights, small)
    dx, loss_lanes = _loss_head(y, loss_target[0], name="loss_head")
    loss = lax.psum(0.5 * jnp.sum(loss_lanes) * (1.0 / D_MODEL), ("x", "y", "c"))
    bufs = None
    for l in reversed(range(DEPTH)):
        dx, g = _backward_layer(l, dx, saved[l], weights[l], ln_g[l])
        c_in = g["w_in"].reshape(D_MODEL, N_DEV, SHARD_COLS).transpose(1, 0, 2).astype(BF16)
        c_out = g["w_out"].astype(BF16).reshape(N_DEV, OUT_SHARD_ROWS, D_MODEL)
        c_conv = g["conv_w"].reshape(CONV_K, N_DEV, CONV_SHARD_COLS).transpose(1, 0, 2)
        c_small = [jnp.broadcast_to(g[n][None], (N_DEV,) + g[n].shape) for n, _ in SMALL_SIZES]
        bufs = _exchange([c_in, c_out, _pack_small(c_conv, c_small)], bufs, l, name=f"grad_exchange_{l}")

    p_in, p_out, p_small = bufs
    o_in = _adamw(p_in, w_in, m_w_in, v_w_in, tr=256, name="adamw_w_in")
    o_out = _adamw(p_out, w_out, m_w_out, v_w_out, tr=OUT_SHARD_ROWS, name="adamw_w_out")
    o_small = _adamw(p_small, _pack_small(conv_w, small),
                     _pack_small(m_conv_w, [m_a_log, m_dt_bias, m_norm_w, m_sinks, m_ln_g, m_ln_b]),
                     _pack_small(v_conv_w, [v_a_log, v_dt_bias, v_norm_w, v_sinks, v_ln_g, v_ln_b]),
                     tr=CS_ROWS, name="adamw_small")
    outs = []
    for k in range(4):
        cv, sm = _unpack_small(o_small[k])
        outs += [o_in[k], cv, sm[0], sm[1], sm[2], sm[3], o_out[k], sm[4], sm[5]]
    return (loss, dx[None], *outs)
```

```python
import functools

import jax
import jax.numpy as jnp
from jax import lax
from jax.experimental import pallas as pl
from jax.experimental.pallas import tpu as pltpu

F32 = jnp.float32
BF16 = jnp.bfloat16
MM_DTYPE = BF16

N_DEV = 8
D_MODEL = 1024
DEPTH = 2
A_HEADS = 4
A_HEAD_DIM = 128
A_WIDTH = 512
CONV_K = 4
CHUNK = 64
SUPER = 256
G_HEADS = 2
B_Q_HEADS = 8
B_KV_HEADS = 2
B_HEAD_DIM = 64
B_GROUP = 4
B_WIDTH = 512
WINDOW = 128
BLOCK = 128
IN_COLS = 3336
SHARD_COLS = IN_COLS // N_DEV
OUT_SHARD_ROWS = D_MODEL // N_DEV
CONV_SHARD_COLS = 3 * A_WIDTH // N_DEV
DEEPNORM_ALPHA = (2 * DEPTH) ** 0.25
LN_EPS = 1e-5
RMS_EPS = 1e-6
L2_EPS = 1e-6
ADAM_LR, ADAM_B1, ADAM_B2, ADAM_EPS, ADAM_WD, ADAM_STEP = 0.001, 0.9, 0.999, 1e-08, 0.01, 10

LANE = 128
L_QB, L_ZB, L_KB, L_VB, L_QKV, L_ZA, L_BA = 0, 512, 1024, 1152, 1280, 2816, 3328
L_SWA = 1280
L_MAIN = 3328
L_COLS = 3456
SMALL_SIZES = (("a_log", 4), ("dt_bias", 4), ("norm_w", 128), ("sinks", 8), ("ln_g", 1024), ("ln_b", 1024))
CS_CONV = CONV_K * CONV_SHARD_COLS
CS_ROWS = 24
VMEM_LIMIT = 48 * 1024 * 1024


def _cparams(sem=None):
    return pltpu.CompilerParams(dimension_semantics=sem, vmem_limit_bytes=VMEM_LIMIT)


def _mm(a, b):
    return jnp.dot(a.astype(MM_DTYPE), b.astype(MM_DTYPE), preferred_element_type=F32)


def _mm_nt(a, b):
    return lax.dot_general(a.astype(MM_DTYPE), b.astype(MM_DTYPE), (((1,), (1,)), ((), ())),
                           preferred_element_type=F32)


def _mm_tn(a, b):
    return lax.dot_general(a.astype(MM_DTYPE), b.astype(MM_DTYPE), (((0,), (0,)), ((), ())),
                           preferred_element_type=F32)


def _split(a):
    hi = a.astype(BF16)
    return hi, (a - hi.astype(F32)).astype(BF16)


def _hp(a2, b2):
    d = lambda p, q: jnp.dot(p, q, preferred_element_type=F32)
    return d(a2[0], b2[0]) + (d(a2[0], b2[1]) + d(a2[1], b2[0]))


def _silu(x):
    return x * jax.nn.sigmoid(x)


def _softplus(x):
    return jnp.maximum(x, 0.0) + jnp.log1p(jnp.exp(-jnp.abs(x)))


_ANY = pl.BlockSpec(memory_space=pl.ANY)


def _matmul(a, b, *, form, tm, tn, tk, name, add=None, add_scale=1.0, extra=None):
    if form == "nn":
        (m, kk), n = a.shape, b.shape[1]
        a_spec = pl.BlockSpec((tm, tk), lambda i, j, k: (i, k))
        b_spec = pl.BlockSpec((tk, tn), lambda i, j, k: (k, j))
        dn = (((1,), (0,)), ((), ()))
    elif form == "nt":
        (m, kk), n = a.shape, b.shape[0]
        a_spec = pl.BlockSpec((tm, tk), lambda i, j, k: (i, k))
        b_spec = pl.BlockSpec((tn, tk), lambda i, j, k: (j, k))
        dn = (((1,), (1,)), ((), ()))
    else:
        (kk, m), n = a.shape, b.shape[1]
        a_spec = pl.BlockSpec((tk, tm), lambda i, j, k: (k, i))
        b_spec = pl.BlockSpec((tk, tn), lambda i, j, k: (k, j))
        dn = (((0,), (0,)), ((), ()))
    assert m % tm == 0 and n % tn == 0 and kk % tk == 0, (name, m, n, kk)
    has_add, has_extra = add is not None, extra is not None

    def body(*refs):
        refs = list(refs)
        a_ref, b_ref = refs[:2]
        o_ref = refs[-1]
        rest = refs[2:-1]
        k = pl.program_id(2)
        p = lax.dot_general(a_ref[...].astype(MM_DTYPE), b_ref[...].astype(MM_DTYPE), dn,
                            preferred_element_type=F32)

        @pl.when(k == 0)
        def _():
            first = p
            pos = 0
            if has_extra:
                first = first + _mm_nt(rest[0][...], rest[1][...])
                pos = 2
            if has_add:
                first = first + add_scale * rest[pos][...]
            o_ref[...] = first

        @pl.when(k > 0)
        def _():
            o_ref[...] += p

    in_specs = [a_spec, b_spec]
    args = [a, b]
    if has_extra:
        a2, b2, idx = extra
        in_specs += [pl.BlockSpec((tm, LANE), lambda i, j, k: (i, 0)),
                     pl.BlockSpec((tn, LANE), lambda i, j, k: (j, idx))]
        args += [a2, b2]
    if has_add:
        in_specs.append(pl.BlockSpec((tm, tn), lambda i, j, k: (i, j)))
        args.append(add)
    return pl.pallas_call(
        body, name=name, grid=(m // tm, n // tn, kk // tk), in_specs=in_specs,
        out_specs=pl.BlockSpec((tm, tn), lambda i, j, k: (i, j)),
        out_shape=jax.ShapeDtypeStruct((m, n), F32),
        compiler_params=_cparams(("parallel", "parallel", "arbitrary")),
    )(*args)


def _shift_down(x, k, row):
    return jnp.where(row >= k, pltpu.roll(x, k, 0), 0.0)


def _shift_up(x, k, row, t_len):
    return jnp.where(row < t_len - k, pltpu.roll(x, t_len - k, 0), 0.0)


def _conv_slab(x, w, row):
    return (w[3:4] * x + w[2:3] * _shift_down(x, 1, row) + w[1:2] * _shift_down(x, 2, row)
            + w[0:1] * _shift_down(x, 3, row))


def _prep_fwd(h, conv_w, *, name):
    t_len = h.shape[0]

    def body(x_ref, w_ref, o_ref):
        s = pl.program_id(0)
        row = lax.broadcasted_iota(jnp.int32, (t_len, LANE), 0)
        y = _silu(_conv_slab(x_ref[...], w_ref[...], row))
        rs = lax.rsqrt(jnp.sum(y * y, axis=-1, keepdims=True) + L2_EPS)
        scale = jnp.where(s < A_HEADS, A_HEAD_DIM ** -0.5, 1.0)
        o_ref[...] = jnp.where(s < 2 * A_HEADS, y * rs * scale, y)

    return pl.pallas_call(
        body, name=name, grid=(12,),
        in_specs=[pl.BlockSpec((t_len, LANE), lambda s: (0, L_QKV // LANE + s)),
                  pl.BlockSpec((8, LANE), lambda s: (0, s))],
        out_specs=pl.BlockSpec((t_len, LANE), lambda s: (0, s)),
        out_shape=jax.ShapeDtypeStruct((t_len, 3 * A_WIDTH), F32),
        compiler_params=_cparams(("parallel",)),
    )(h, conv_w)


def _prep_bwd(h, conv_w, d_out, dh, *, name):
    t_len = h.shape[0]

    def body(x_ref, w_ref, g_ref, dh_in, dx_ref, dw_ref):
        del dh_in
        s = pl.program_id(0)
        row = lax.broadcasted_iota(jnp.int32, (t_len, LANE), 0)
        x = x_ref[...]
        w = w_ref[...]
        c = _conv_slab(x, w, row)
        sg = jax.nn.sigmoid(c)
        y = c * sg
        g = g_ref[0]
        rs = lax.rsqrt(jnp.sum(y * y, axis=-1, keepdims=True) + L2_EPS)
        scale = jnp.where(s < A_HEADS, A_HEAD_DIM ** -0.5, 1.0)
        dy_n = scale * (rs * g - y * (rs * rs * rs) * jnp.sum(g * y, axis=-1, keepdims=True))
        dy = jnp.where(s < 2 * A_HEADS, dy_n, g)
        dc = dy * (sg * (1.0 + c * (1.0 - sg)))
        dx_ref[...] = (w[3:4] * dc + w[2:3] * _shift_up(dc, 1, row, t_len)
                       + w[1:2] * _shift_up(dc, 2, row, t_len) + w[0:1] * _shift_up(dc, 3, row, t_len))
        dws = [jnp.sum(dc * _shift_down(x, 3 - j, row), axis=0, keepdims=True) if j < 3
               else jnp.sum(dc * x, axis=0, keepdims=True) for j in range(CONV_K)]
        dw_ref[...] = jnp.concatenate(dws + [jnp.zeros((8 - CONV_K, LANE), F32)], axis=0)

    slab = pl.BlockSpec((t_len, LANE), lambda s: (0, L_QKV // LANE + s))
    return pl.pallas_call(
        body, name=name, grid=(12,),
        in_specs=[slab, pl.BlockSpec((8, LANE), lambda s: (0, s)),
                  pl.BlockSpec((1, t_len, LANE), lambda s: (s // A_HEADS, 0, s % A_HEADS)), _ANY],
        out_specs=[slab, pl.BlockSpec((8, LANE), lambda s: (0, s))],
        out_shape=[jax.ShapeDtypeStruct((t_len, L_MAIN), F32), jax.ShapeDtypeStruct((8, 3 * A_WIDTH), F32)],
        input_output_aliases={3: 0},
        compiler_params=_cparams(("parallel",)),
    )(h, conv_w, d_out, dh)


MF_TRIL, MF_STRIL, MF_DIAG8, MF_LOW16, MF_EYE = 0, 1, 2, 3, 6
MB_CUM, MB_CUM_T, MB_TOT = 0, 1, 2


def _gdn_masks():
    r = lax.broadcasted_iota(jnp.int32, (SUPER, SUPER), 0)
    c = lax.broadcasted_iota(jnp.int32, (SUPER, SUPER), 1)
    same = lambda shift: (r >> shift) == (c >> shift)
    chunk = same(6)
    ninf = lambda m: jnp.where(m, 0.0, -jnp.inf).astype(F32)
    one = lambda m: m.astype(F32)
    mf = jnp.stack([ninf(chunk & (r >= c)), ninf(chunk & (r > c)), one(same(3))]
                   + [one(same(sh) & jnp.logical_not(same(sh - 1))) for sh in (4, 5, 6)] + [one(r == c)])
    mb = jnp.stack([one(chunk & (r >= c)), one(chunk & (r <= c)), one(chunk)]).astype(BF16)
    return mf, mb


def _tri_inv_impl(a, mf):
    eye = mf[MF_EYE]
    a0 = a * mf[MF_DIAG8]
    a0s = _split(a0)
    a2 = _hp(a0s, a0s)
    a2s = _split(a2)
    a4 = _hp(a2s, a2s)
    t = _hp(_split(_hp(_split(eye - a0), _split(eye + a2))), _split(eye + a4))
    for level in range(3):
        ts = _split(t)
        t = t - _hp(_split(_hp(ts, _split(a * mf[MF_LOW16 + level]))), ts)
    return t


def _tri_inv_cotangent(t, dt):
    tts = _split(t.T)
    return -_hp(_split(_hp(tts, _split(dt))), tts)


@jax.custom_vjp
def _tri_inv(a, mf):
    return _tri_inv_impl(a, mf)


def _tri_inv_fwd(a, mf):
    t = _tri_inv_impl(a, mf)
    return t, (t, mf)


_tri_inv.defvjp(_tri_inv_fwd, lambda res, dt: (_tri_inv_cotangent(res[0], dt), jnp.zeros_like(res[1])))


@jax.custom_vjp
def _tri_inv_known(a, t):
    return t


def _tri_inv_known_fwd(a, t):
    return t, t


def _tri_inv_known_bwd(t, dt):
    return _tri_inv_cotangent(t, dt), jnp.zeros_like(t)


_tri_inv_known.defvjp(_tri_inv_known_fwd, _tri_inv_known_bwd)


@functools.partial(jax.custom_vjp, nondiff_argnums=(1,))
def _lane_roll(x, shift):
    return pltpu.roll(x, shift % LANE, 1)


_lane_roll.defvjp(lambda x, shift: (_lane_roll(x, shift), None), lambda shift, _, g: (_lane_roll(g, -shift),))


def _mask_times_lanes(x, mask):
    lane = lax.broadcasted_iota(jnp.int32, (1, LANE), 1)
    x = jnp.where(lane < A_HEADS, x, 0.0)
    x1 = x.astype(BF16).astype(F32)
    x2 = (x - x1).astype(BF16).astype(F32)
    x3 = (x - x1 - x2).astype(BF16).astype(F32)
    pieces = x1 + pltpu.roll(x2, A_HEADS, 1) + pltpu.roll(x3, 2 * A_HEADS, 1)
    res = jnp.dot(mask, pieces.astype(BF16), preferred_element_type=F32)
    return res + pltpu.roll(res, LANE - A_HEADS, 1) + pltpu.roll(res, LANE - 2 * A_HEADS, 1)


@jax.custom_vjp
def _chunk_sums(g, mb):
    return _mask_times_lanes(g, mb[MB_CUM]), _mask_times_lanes(g, mb[MB_TOT])


def _chunk_sums_fwd(g, mb):
    return _chunk_sums(g, mb), mb


def _chunk_sums_bwd(mb, d):
    lane = lax.broadcasted_iota(jnp.int32, (1, LANE), 1)
    dg = _mask_times_lanes(d[0], mb[MB_CUM_T]) + _mask_times_lanes(d[1], mb[MB_TOT])
    return jnp.where(lane < A_HEADS, dg, 0.0), jnp.zeros_like(mb)


_chunk_sums.defvjp(_chunk_sums_fwd, _chunk_sums_bwd)


def _gdn_gates(ba, alog, dtb, mb):
    beta = jax.nn.sigmoid(ba)
    g = -jnp.exp(alog) * _softplus(_lane_roll(ba, -A_HEADS) + dtb)
    gc, gl = _chunk_sums(g, mb)
    return beta, gc, gl, gc.T


def _gdn_block(s, q, k, v, z, gates, nw, h, t_known, mf):
    n = q.shape[0]
    beta_all, gc_all, gl_all, gct_all = gates
    lane = lax.broadcasted_iota(jnp.int32, (1, LANE), 1)
    sub = lax.broadcasted_iota(jnp.int32, (LANE, 1), 0)
    col = lambda x: jnp.sum(jnp.where(lane == h, x, 0.0), axis=1, keepdims=True)
    wide = lambda c: jnp.broadcast_to(c, (n, LANE))
    gc, gl = col(gc_all), col(gl_all)
    gc_row = jnp.sum(jnp.where(sub == h, gct_all, 0.0), axis=0, keepdims=True)
    beta_w, eg_w = wide(col(beta_all)), wide(jnp.exp(gc))
    diff = gc - gc_row
    decay = jnp.exp(diff + mf[MF_TRIL])
    kb = k * beta_w
    a_mat = _mm_nt(kb, k) * jnp.exp(diff + mf[MF_STRIL])
    t_mat = _tri_inv(a_mat, mf) if t_known is None else _tri_inv_known(a_mat, t_known)
    u = _mm(t_mat, v * beta_w)
    w = _mm(t_mat, kb * eg_w)
    qk = _mm_nt(q, k) * decay
    q_dec = q * eg_w
    k_dec = k * wide(jnp.exp(gl - gc))
    g_tot = jnp.exp(gl)
    outs = []
    for ci in range(n // CHUNK):
        lo, hi = ci * CHUNK, (ci + 1) * CHUNK
        v_new = u[lo:hi] - _mm(w[lo:hi], s)
        pieces = []
        if lo:
            pieces.append(jnp.zeros((lo, LANE), F32))
        pieces.append(v_new)
        if n - hi:
            pieces.append(jnp.zeros((n - hi, LANE), F32))
        v_pad = jnp.concatenate(pieces, axis=0) if len(pieces) > 1 else v_new
        outs.append(_mm(q_dec[lo:hi], s) + _mm(qk[lo:hi], v_pad))
        s = s * g_tot[lo:lo + 1] + _mm_tn(k_dec[lo:hi], v_new)
    o = jnp.concatenate(outs, axis=0)
    o = o * lax.rsqrt(jnp.mean(o * o, axis=-1, keepdims=True) + RMS_EPS) * nw
    return o * _silu(z), s, t_mat


def _gdn_fwd(qkv, h, alog, dtb, nw, ycat, *, name):
    t_len = qkv.shape[0]
    nsc = t_len // SUPER

    def body(q_ref, k_ref, v_ref, z_ref, ba_ref, al_ref, dt_ref, nw_ref, mf_ref, mb_ref, y_in,
             y_ref, sin_ref, t_ref, s_scr):
        del y_in
        sc, pg = pl.program_id(0), pl.program_id(1)
        heads = [pg * G_HEADS + hh for hh in range(G_HEADS)]

        @pl.when(sc == 0)
        def _():
            for hd in heads:
                s_scr[hd] = jnp.zeros((A_HEAD_DIM, A_HEAD_DIM), F32)

        per_head = lambda ref: jnp.stack([ref[:, hh * LANE:(hh + 1) * LANE] for hh in range(G_HEADS)])
        states = jnp.stack([s_scr[hd] for hd in heads])
        gates = _gdn_gates(ba_ref[...], al_ref[...], dt_ref[...], mb_ref[...])
        fn = jax.vmap(_gdn_block, in_axes=(0, 0, 0, 0, 0, None, None, 0, None, None))
        y, s_new, t_mat = fn(states, per_head(q_ref), per_head(k_ref), per_head(v_ref), per_head(z_ref),
                             gates, nw_ref[...], jnp.stack(heads), None, mf_ref[...])
        sin_ref[0] = states
        t_ref[0] = t_mat
        for hh, hd in enumerate(heads):
            y_ref[:, hh * LANE:(hh + 1) * LANE] = y[hh]
            s_scr[hd] = s_new[hh]

    gw = G_HEADS * LANE
    blk = lambda off: pl.BlockSpec((SUPER, gw), lambda sc, pg: (sc, off // G_HEADS + pg))
    row = pl.BlockSpec((1, LANE), lambda sc, pg: (0, 0))
    mf, mb = _gdn_masks()
    whole = lambda a: pl.BlockSpec(a.shape, lambda sc, pg: (0, 0, 0))
    return pl.pallas_call(
        body, name=name, grid=(nsc, A_HEADS // G_HEADS),
        in_specs=[blk(0), blk(4), blk(8), blk(L_ZA // LANE),
                  pl.BlockSpec((SUPER, LANE), lambda sc, pg: (sc, L_BA // LANE)), row, row, row,
                  whole(mf), whole(mb), _ANY],
        out_specs=[blk(0),
                   pl.BlockSpec((1, G_HEADS, A_HEAD_DIM, A_HEAD_DIM), lambda sc, pg: (sc, pg, 0, 0)),
                   pl.BlockSpec((1, G_HEADS, SUPER, SUPER), lambda sc, pg: (sc, pg, 0, 0))],
        out_shape=[jax.ShapeDtypeStruct((t_len, D_MODEL), F32),
                   jax.ShapeDtypeStruct((nsc, A_HEADS, A_HEAD_DIM, A_HEAD_DIM), F32),
                   jax.ShapeDtypeStruct((nsc, A_HEADS, SUPER, SUPER), F32)],
        scratch_shapes=[pltpu.VMEM((A_HEADS, A_HEAD_DIM, A_HEAD_DIM), F32)],
        input_output_aliases={10: 0},
        compiler_params=_cparams(("arbitrary", "arbitrary")),
    )(qkv, qkv, qkv, h, h, alog, dtb, nw, mf, mb, ycat)


def _gdn_bwd(qkv, h, alog, dtb, nw, s_in, t_in, dycat, dh, *, name):
    t_len = qkv.shape[0]
    nsc = t_len // SUPER

    def body(q_ref, k_ref, v_ref, z_ref, ba_ref, al_ref, dt_ref, nw_ref, sin_ref, t_ref, dy_ref, mf_ref, mb_ref,
             dh_in, dz_ref, dqkv_ref, dba_ref, dal_ref, ddt_ref, dnw_ref, ds_scr):
        del dh_in
        i, pg = pl.program_id(0), pl.program_id(1)

        @pl.when((i == 0) & (pg == 0))
        def _():
            dal_ref[...] = jnp.zeros_like(dal_ref)
            ddt_ref[...] = jnp.zeros_like(ddt_ref)
            dnw_ref[...] = jnp.zeros_like(dnw_ref)

        @pl.when(pg == 0)
        def _():
            dba_ref[...] = jnp.zeros_like(dba_ref)

        heads = [pg * G_HEADS + hh for hh in range(G_HEADS)]

        @pl.when(i == 0)
        def _():
            for hd in heads:
                ds_scr[hd] = jnp.zeros((A_HEAD_DIM, A_HEAD_DIM), F32)

        per_head = lambda ref: jnp.stack([ref[:, hh * LANE:(hh + 1) * LANE] for hh in range(G_HEADS)])
        d_states = jnp.stack([ds_scr[hd] for hd in heads])
        head_ids = jnp.stack(heads)
        t_known, mf, mb = t_ref[0], mf_ref[...], mb_ref[...]

        def fn(s, q, k, v, z, ba, alog, dtb, nw):
            gates = _gdn_gates(ba, alog, dtb, mb)
            one = lambda s, q, k, v, z, t, h: _gdn_block(s, q, k, v, z, gates, nw, h, t, mf)[:2]
            return jax.vmap(one)(s, q, k, v, z, t_known, head_ids)

        _, vjp = jax.vjp(fn, sin_ref[0], per_head(q_ref), per_head(k_ref), per_head(v_ref), per_head(z_ref),
                         ba_ref[...], al_ref[...], dt_ref[...], nw_ref[...])
        ds, dq, dk, dv, dz, dba, dal, ddt, dnw = vjp((per_head(dy_ref), d_states))
        for hh, hd in enumerate(heads):
            cols = slice(hh * LANE, (hh + 1) * LANE)
            ds_scr[hd] = ds[hh]
            dqkv_ref[0, :, cols] = dq[hh]
            dqkv_ref[1, :, cols] = dk[hh]
            dqkv_ref[2, :, cols] = dv[hh]
            dz_ref[:, cols] = dz[hh]
        dba_ref[...] += dba
        dal_ref[...] += dal
        ddt_ref[...] += ddt
        dnw_ref[...] += dnw

    rev = lambda i: nsc - 1 - i
    gw = G_HEADS * LANE
    blk = lambda off: pl.BlockSpec((SUPER, gw), lambda i, pg: (rev(i), off // G_HEADS + pg))
    row = pl.BlockSpec((1, LANE), lambda i, pg: (0, 0))
    ba_blk = lambda off: pl.BlockSpec((SUPER, LANE), lambda i, pg: (rev(i), off))
    mf, mb = _gdn_masks()
    whole = lambda a: pl.BlockSpec(a.shape, lambda i, pg: (0, 0, 0))
    return pl.pallas_call(
        body, name=name, grid=(nsc, A_HEADS // G_HEADS),
        in_specs=[blk(0), blk(4), blk(8), blk(L_ZA // LANE), ba_blk(L_BA // LANE), row, row, row,
                  pl.BlockSpec((1, G_HEADS, A_HEAD_DIM, A_HEAD_DIM), lambda i, pg: (rev(i), pg, 0, 0)),
                  pl.BlockSpec((1, G_HEADS, SUPER, SUPER), lambda i, pg: (rev(i), pg, 0, 0)),
                  blk(0), whole(mf), whole(mb), _ANY],
        out_specs=[blk(L_ZA // LANE),
                   pl.BlockSpec((3, SUPER, gw), lambda i, pg: (0, rev(i), pg)),
                   ba_blk(0), row, row, row],
        out_shape=[jax.ShapeDtypeStruct((t_len, L_MAIN), F32), jax.ShapeDtypeStruct((3, t_len, A_WIDTH), F32),
                   jax.ShapeDtypeStruct((t_len, LANE), F32)] + [jax.ShapeDtypeStruct((1, LANE), F32)] * 3,
        scratch_shapes=[pltpu.VMEM((A_HEADS, A_HEAD_DIM, A_HEAD_DIM), F32)],
        input_output_aliases={13: 0},
        compiler_params=_cparams(("arbitrary", "arbitrary")),
    )(qkv, qkv, qkv, h, h, alog, dtb, nw, s_in, t_in, dycat, mf, mb, dh)


def _swa_block(q, kp, kc, vp, vc, z, sinks, first):
    qi = lax.broadcasted_iota(jnp.int32, (BLOCK, 2 * BLOCK), 0)
    si = lax.broadcasted_iota(jnp.int32, (BLOCK, 2 * BLOCK), 1)
    dist = qi + BLOCK - si
    mask = (dist >= 0) & (dist < WINDOW) & ((si >= BLOCK) | jnp.logical_not(first))
    dist_f = dist.astype(F32)
    outs = []
    for j in range(B_KV_HEADS):
        cs = slice(j * B_HEAD_DIM, (j + 1) * B_HEAD_DIM)
        kk = jnp.concatenate([kp[:, cs], kc[:, cs]], axis=0)
        vv = jnp.concatenate([vp[:, cs], vc[:, cs]], axis=0)
        for gi in range(B_GROUP):
            hq = j * B_GROUP + gi
            slope = 2.0 ** (-8.0 * (hq + 1) / B_Q_HEADS)
            sc = _mm_nt(q[:, hq * B_HEAD_DIM:(hq + 1) * B_HEAD_DIM], kk) * (B_HEAD_DIM ** -0.5)
            sc = jnp.where(mask, sc - slope * dist_f, -jnp.inf)
            sink = sinks[:, hq:hq + 1]
            m = lax.stop_gradient(jnp.maximum(jnp.max(sc, axis=-1, keepdims=True), sink))
            p = jnp.exp(sc - m)
            p = p / (jnp.sum(p, axis=-1, keepdims=True) + jnp.exp(sink - m))
            outs.append(_mm(p, vv))
    return jnp.concatenate(outs, axis=1) * _silu(z)


def _swa_specs(idx):
    wide = lambda off: pl.BlockSpec((BLOCK, B_WIDTH), lambda n: (idx(n), off))
    cur = lambda off: pl.BlockSpec((BLOCK, LANE), lambda n: (idx(n), off))
    prev = lambda off: pl.BlockSpec((BLOCK, LANE), lambda n: (jnp.maximum(idx(n) - 1, 0), off))
    return [wide(L_QB // B_WIDTH), prev(L_KB // LANE), cur(L_KB // LANE), prev(L_VB // LANE), cur(L_VB // LANE),
            wide(L_ZB // B_WIDTH), pl.BlockSpec((1, LANE), lambda n: (0, 0))]


def _swa_fwd(h, sinks, *, name):
    t_len = h.shape[0]
    nb = t_len // BLOCK

    def body(q_ref, kp_ref, kc_ref, vp_ref, vc_ref, z_ref, s_ref, o_ref):
        o_ref[...] = _swa_block(q_ref[...], kp_ref[...], kc_ref[...], vp_ref[...], vc_ref[...], z_ref[...],
                                s_ref[...], pl.program_id(0) == 0)

    return pl.pallas_call(
        body, name=name, grid=(nb,), in_specs=_swa_specs(lambda n: n),
        out_specs=pl.BlockSpec((BLOCK, B_WIDTH), lambda n: (n, 1)),
        out_shape=jax.ShapeDtypeStruct((t_len, D_MODEL), F32),
        compiler_params=_cparams(("parallel",)),
    )(h, h, h, h, h, h, sinks)


def _swa_bwd(h, sinks, dycat, *, name):
    t_len = h.shape[0]
    nb = t_len // BLOCK

    def body(q_ref, kp_ref, kc_ref, vp_ref, vc_ref, z_ref, s_ref, dy_ref, dh_ref, dsk_ref, ck_scr, cv_scr):
        i = pl.program_id(0)
        n = nb - 1 - i

        @pl.when(i == 0)
        def _():
            ck_scr[...] = jnp.zeros_like(ck_scr)
            cv_scr[...] = jnp.zeros_like(cv_scr)
            dsk_ref[...] = jnp.zeros_like(dsk_ref)

        fn = functools.partial(_swa_block, first=(n == 0))
        _, vjp = jax.vjp(fn, q_ref[...], kp_ref[...], kc_ref[...], vp_ref[...], vc_ref[...], z_ref[...], s_ref[...])
        dq, dkp, dkc, dvp, dvc, dz, dsk = vjp(dy_ref[...])
        dh_ref[:, L_QB:L_QB + B_WIDTH] = dq
        dh_ref[:, L_ZB:L_ZB + B_WIDTH] = dz
        dh_ref[:, L_KB:L_KB + LANE] = dkc + ck_scr[...]
        dh_ref[:, L_VB:L_VB + LANE] = dvc + cv_scr[...]
        ck_scr[...] = dkp
        cv_scr[...] = dvp
        dsk_ref[...] += dsk

    rev = lambda i: nb - 1 - i
    return pl.pallas_call(
        body, name=name, grid=(nb,),
        in_specs=_swa_specs(rev) + [pl.BlockSpec((BLOCK, B_WIDTH), lambda i: (rev(i), 1))],
        out_specs=[pl.BlockSpec((BLOCK, L_SWA), lambda i: (rev(i), 0)), pl.BlockSpec((1, LANE), lambda i: (0, 0))],
        out_shape=[jax.ShapeDtypeStruct((t_len, L_MAIN), F32), jax.ShapeDtypeStruct((1, LANE), F32)],
        scratch_shapes=[pltpu.VMEM((BLOCK, LANE), F32), pltpu.VMEM((BLOCK, LANE), F32)],
        compiler_params=_cparams(("arbitrary",)),
    )(h, h, h, h, h, h, sinks, dycat)


def _out_ln_fwd(ycat, w_out, x, ln_g, ln_b, *, name, tm=256):
    t_len = x.shape[0]

    def body(y_ref, w_ref, x_ref, g_ref, b_ref, r_ref, o_ref):
        r = DEEPNORM_ALPHA * x_ref[...] + _mm(y_ref[...], w_ref[...])
        r_ref[...] = r
        mu = jnp.mean(r, axis=-1, keepdims=True)
        d = r - mu
        var = jnp.mean(d * d, axis=-1, keepdims=True)
        o_ref[...] = d * lax.rsqrt(var + LN_EPS) * g_ref[...] + b_ref[...]

    tile = pl.BlockSpec((tm, D_MODEL), lambda i: (i, 0))
    vec = pl.BlockSpec((1, D_MODEL), lambda i: (0, 0))
    return pl.pallas_call(
        body, name=name, grid=(t_len // tm,),
        in_specs=[tile, pl.BlockSpec((D_MODEL, D_MODEL), lambda i: (0, 0)), tile, vec, vec],
        out_specs=[tile, tile],
        out_shape=[jax.ShapeDtypeStruct((t_len, D_MODEL), F32)] * 2,
        compiler_params=_cparams(("parallel",)),
    )(ycat, w_out, x, ln_g, ln_b)


def _ln_bwd(dxn, r, ln_g, *, name, tm=256):
    t_len = r.shape[0]

    def body(dx_ref, r_ref, g_ref, dr_ref, dg_ref, db_ref):
        @pl.when(pl.program_id(0) == 0)
        def _():
            dg_ref[...] = jnp.zeros_like(dg_ref)
            db_ref[...] = jnp.zeros_like(db_ref)

        rr = r_ref[...]
        dx = dx_ref[...]
        mu = jnp.mean(rr, axis=-1, keepdims=True)
        d = rr - mu
        rstd = lax.rsqrt(jnp.mean(d * d, axis=-1, keepdims=True) + LN_EPS)
        xh = d * rstd
        dxh = dx * g_ref[...]
        dr_ref[...] = rstd * (dxh - jnp.mean(dxh, axis=-1, keepdims=True)
                              - xh * jnp.mean(dxh * xh, axis=-1, keepdims=True))
        dg_ref[...] += jnp.sum(dx * xh, axis=0, keepdims=True)
        db_ref[...] += jnp.sum(dx, axis=0, keepdims=True)

    tile = pl.BlockSpec((tm, D_MODEL), lambda i: (i, 0))
    vec = pl.BlockSpec((1, D_MODEL), lambda i: (0, 0))
    return pl.pallas_call(
        body, name=name, grid=(t_len // tm,),
        in_specs=[tile, tile, vec], out_specs=[tile, vec, vec],
        out_shape=[jax.ShapeDtypeStruct((t_len, D_MODEL), F32), jax.ShapeDtypeStruct((1, D_MODEL), F32),
                   jax.ShapeDtypeStruct((1, D_MODEL), F32)],
        compiler_params=_cparams(("arbitrary",)),
    )(dxn, r, ln_g)


def _loss_head(y, target, *, name, tm=256):
    t_len = y.shape[0]

    def body(y_ref, t_ref, d_ref, l_ref):
        @pl.when(pl.program_id(0) == 0)
        def _():
            l_ref[...] = jnp.zeros_like(l_ref)

        e = y_ref[...] - t_ref[...]
        d_ref[...] = e * (1.0 / D_MODEL)
        l_ref[...] += jnp.sum(e * e, axis=0, keepdims=True)

    tile = pl.BlockSpec((tm, D_MODEL), lambda i: (i, 0))
    vec = pl.BlockSpec((1, D_MODEL), lambda i: (0, 0))
    return pl.pallas_call(
        body, name=name, grid=(t_len // tm,), in_specs=[tile, tile], out_specs=[tile, vec],
        out_shape=[jax.ShapeDtypeStruct((t_len, D_MODEL), F32), jax.ShapeDtypeStruct((1, D_MODEL), F32)],
        compiler_params=_cparams(("arbitrary",)),
    )(y, target)


def _pad_row(v):
    return jnp.zeros((1, LANE), F32).at[0, :v.shape[0]].set(v)


def _to_layout(w_full):
    s = lambda a, b: w_full[..., a:b]
    pad = jnp.zeros(w_full.shape[:-1] + (LANE - 2 * A_HEADS,), w_full.dtype)
    return jnp.concatenate([s(2056, 2568), s(2824, 3336), s(2568, 2696), s(2696, 2824), s(0, 1536), s(1536, 2048),
                            s(2048, 2056), pad], axis=-1)


def _from_layout(g_main, g_ba):
    s = lambda a, b: g_main[..., a:b]
    return jnp.concatenate([s(L_QKV, L_QKV + 1536), s(L_ZA, L_ZA + 512), g_ba[..., :2 * A_HEADS],
                            s(L_QB, L_QB + 512), s(L_KB, L_KB + 128), s(L_VB, L_VB + 128), s(L_ZB, L_ZB + 512)],
                           axis=-1)


def _forward(x, weights, small):
    a_log, dt_bias, norm_w, sinks, ln_g, ln_b = small
    tm = min(512, x.shape[0])
    saved = []
    for l in range(DEPTH):
        w_in_l, w_out_l, conv_l = weights[l]
        h = _matmul(x, w_in_l, form="nn", tm=tm, tn=1152, tk=D_MODEL, name=f"in_proj_{l}")
        qkv = _prep_fwd(h, conv_l, name=f"prep_fwd_{l}")
        al, dt, nw, sk = _pad_row(a_log[l]), _pad_row(dt_bias[l]), norm_w[l][None, :], _pad_row(sinks[l])
        ycat = _swa_fwd(h, sk, name=f"swa_fwd_{l}")
        ycat, s_in, t_in = _gdn_fwd(qkv, h, al, dt, nw, ycat, name=f"gdn_fwd_{l}")
        r, xn = _out_ln_fwd(ycat, w_out_l, x, ln_g[l][None, :], ln_b[l][None, :], name=f"out_ln_{l}")
        saved.append((x, h, qkv, s_in, t_in, ycat, r, al, dt, nw, sk))
        x = xn
    return x, saved


def _backward_layer(l, dx, saved_l, weights_l, ln_g_l):
    x_in, h, qkv, s_in, t_in, ycat, r, al, dt, nw, sk = saved_l
    w_in_l, w_out_l, conv_l = weights_l
    tm = min(512, x_in.shape[0])
    dr, d_lng, d_lnb = _ln_bwd(dx, r, ln_g_l[None, :], name=f"ln_bwd_{l}")
    dycat = _matmul(dr, w_out_l, form="nt", tm=tm, tn=D_MODEL, tk=D_MODEL, name=f"out_proj_dx_{l}")
    d_wout = _matmul(ycat, dr, form="tn", tm=512, tn=D_MODEL, tk=tm, name=f"out_proj_dw_{l}")
    dh, d_sk = _swa_bwd(h, sk, dycat, name=f"swa_bwd_{l}")
    dh, dqkv_n, dba, d_al, d_dt, d_nw = _gdn_bwd(qkv, h, al, dt, nw, s_in, t_in, dycat, dh, name=f"gdn_bwd_{l}")
    dh, d_conv = _prep_bwd(h, conv_l, dqkv_n, dh, name=f"prep_bwd_{l}")
    d_win_main = _matmul(x_in, dh, form="tn", tm=512, tn=L_MAIN // 2, tk=tm, name=f"in_proj_dw_{l}")
    d_win_ba = _matmul(x_in, dba, form="tn", tm=D_MODEL, tn=LANE, tk=tm, name=f"in_proj_dw_ba_{l}")
    dx = _matmul(dh, w_in_l, form="nt", tm=tm, tn=D_MODEL, tk=L_MAIN // 2, name=f"in_proj_dx_{l}",
                 add=dr, add_scale=DEEPNORM_ALPHA, extra=(dba, w_in_l, L_BA // LANE))
    grads = dict(w_in=_from_layout(d_win_main, d_win_ba), w_out=d_wout, conv_w=d_conv[:CONV_K],
                 a_log=d_al[0, :A_HEADS], dt_bias=d_dt[0, :A_HEADS], norm_w=d_nw[0], sinks=d_sk[0, :B_Q_HEADS],
                 ln_g=d_lng[0], ln_b=d_lnb[0])
    return dx, grads


def _me():
    return lax.axis_index("x"), lax.axis_index("y"), lax.axis_index("c")


def _flat_id(pos):
    return 4 * pos[0] + 2 * pos[1] + pos[2]


def _remote(src, dst, send_sem, recv_sem, to):
    return pltpu.make_async_remote_copy(src_ref=src, dst_ref=dst, send_sem=send_sem, recv_sem=recv_sem,
                                        device_id=to, device_id_type=pl.DeviceIdType.MESH)


def _all_gather(shards, *, name):
    n_arr = len(shards)

    def body(*refs):
        x_refs, out_refs = refs[:n_arr], refs[n_arr:2 * n_arr]
        send_sems, recv_sems, local_sems = refs[2 * n_arr:]
        x, y, c = _me()
        me, sibling = (x, y, c), (x, y, 1 - c)
        chips = [(1 - x, y), (x, 1 - y), (1 - x, 1 - y)]

        def copy(a, k, block, to, src=None):
            dst = out_refs[a].at[_flat_id(block)]
            return _remote(dst if src is None else src, dst, send_sems.at[a, k], recv_sems.at[a, k], to)

        mine = [pltpu.make_async_copy(x_refs[a], out_refs[a].at[_flat_id(me)], local_sems.at[a])
                for a in range(n_arr)]
        for cp in mine:
            cp.start()
        first = []
        for a in range(n_arr):
            first.append(copy(a, 0, me, sibling, src=x_refs[a]))
            first += [copy(a, 1 + j, me, (*chip, c), src=x_refs[a]) for j, chip in enumerate(chips)]
        for cp in first:
            cp.start()
        passed = []
        for j, chip in enumerate(chips):
            for a in range(n_arr):
                copy(a, 1 + j, (*chip, c), me).wait_recv()
                fwd = copy(a, 4 + j, (*chip, c), sibling)
                fwd.start()
                passed.append(fwd)
        for a in range(n_arr):
            copy(a, 0, sibling, me).wait_recv()
            for j, chip in enumerate(chips):
                copy(a, 4 + j, (*chip, 1 - c), me).wait_recv()
        for cp in first + passed:
            cp.wait_send()
        for cp in mine:
            cp.wait()

    return pl.pallas_call(
        body, name=name, in_specs=[_ANY] * n_arr, out_specs=[_ANY] * n_arr,
        out_shape=[jax.ShapeDtypeStruct((N_DEV,) + s.shape, s.dtype) for s in shards],
        scratch_shapes=[pltpu.SemaphoreType.DMA((n_arr, N_DEV - 1)), pltpu.SemaphoreType.DMA((n_arr, N_DEV - 1)),
                        pltpu.SemaphoreType.DMA((n_arr,))],
    )(*shards)


def _exchange(contribs, bufs, layer, *, name):
    n_arr = len(contribs)
    create = bufs is None

    def body(*refs):
        x_refs = refs[:n_arr]
        out_refs = refs[n_arr:2 * n_arr] if create else refs[2 * n_arr:3 * n_arr]
        send_sems, recv_sems, local_sems = refs[-3:]
        x, y, c = _me()
        me = _flat_id((x, y, c))
        peers = [(x ^ ((rel >> 2) & 1), y ^ ((rel >> 1) & 1), c ^ (rel & 1)) for rel in range(1, N_DEV)]
        mine = [pltpu.make_async_copy(x_refs[a].at[me], out_refs[a].at[layer, me], local_sems.at[a])
                for a in range(n_arr)]
        for cp in mine:
            cp.start()
        copies = []
        for a in range(n_arr):
            for k, peer in enumerate(peers):
                cp = _remote(x_refs[a].at[_flat_id(peer)], out_refs[a].at[layer, me], send_sems.at[a, k],
                             recv_sems.at[a, k], peer)
                cp.start()
                copies.append(cp)
        for a in range(n_arr):
            for k, peer in enumerate(peers):
                src = _flat_id(peer)
                _remote(x_refs[a].at[src], out_refs[a].at[layer, src], send_sems.at[a, k], recv_sems.at[a, k],
                        peer).wait_recv()
        for cp in copies:
            cp.wait_send()
        for cp in mine:
            cp.wait()

    out_shape = [jax.ShapeDtypeStruct((DEPTH,) + c.shape, c.dtype) for c in contribs]
    args = list(contribs) + ([] if create else list(bufs))
    return pl.pallas_call(
        body, name=name, in_specs=[_ANY] * len(args), out_specs=[_ANY] * n_arr, out_shape=out_shape,
        input_output_aliases={} if create else {n_arr + a: a for a in range(n_arr)},
        scratch_shapes=[pltpu.SemaphoreType.DMA((n_arr, N_DEV - 1)), pltpu.SemaphoreType.DMA((n_arr, N_DEV - 1)),
                        pltpu.SemaphoreType.DMA((n_arr,))],
    )(*args)


def _adamw(parts, w, m, v, *, tr, name):
    depth, rows, cols = w.shape
    c1 = 1.0 - ADAM_B1 ** ADAM_STEP
    c2 = 1.0 - ADAM_B2 ** ADAM_STEP

    def body(g_ref, w_ref, m_ref, v_ref, go_ref, d_ref, mo_ref, vo_ref):
        g = g_ref[0, 0].astype(F32)
        for s in range(1, N_DEV):
            g = g + g_ref[0, s].astype(F32)
        m_new = ADAM_B1 * m_ref[0] + (1.0 - ADAM_B1) * g
        v_new = ADAM_B2 * v_ref[0] + (1.0 - ADAM_B2) * (g * g)
        go_ref[0] = g
        mo_ref[0] = m_new
        vo_ref[0] = v_new
        d_ref[0] = -ADAM_LR * ((m_new / c1) / (jnp.sqrt(v_new / c2) + ADAM_EPS) + ADAM_WD * w_ref[0])

    tile = pl.BlockSpec((1, tr, cols), lambda l, i: (l, i, 0))
    return pl.pallas_call(
        body, name=name, grid=(depth, rows // tr),
        in_specs=[pl.BlockSpec((1, N_DEV, tr, cols), lambda l, i: (l, 0, i, 0)), tile, tile, tile],
        out_specs=[tile] * 4, out_shape=[jax.ShapeDtypeStruct(w.shape, F32)] * 4,
        compiler_params=_cparams(("parallel", "parallel")),
    )(parts, w, m, v)


def _pack_small(conv, small):
    lead = conv.shape[:-2]
    flat = jnp.concatenate([conv.reshape(lead + (CS_CONV,))] + list(small), axis=-1)
    pad = CS_ROWS * LANE - flat.shape[-1]
    flat = jnp.concatenate([flat, jnp.zeros(lead + (pad,), F32)], axis=-1)
    return flat.reshape(lead + (CS_ROWS, LANE))


def _unpack_small(p):
    flat = p.reshape(DEPTH, CS_ROWS * LANE)
    conv = flat[:, :CS_CONV].reshape(DEPTH, CONV_K, CONV_SHARD_COLS)
    small, off = [], CS_CONV
    for _, n in SMALL_SIZES:
        small.append(flat[:, off:off + n])
        off += n
    return conv, small


def kernel(x, w_in, conv_w, a_log, dt_bias, norm_w, sinks, w_out, ln_g, ln_b, loss_target, m_w_in, m_conv_w, m_a_log, m_dt_bias, m_norm_w, m_sinks, m_w_out, m_ln_g, m_ln_b, v_w_in, v_conv_w, v_a_log, v_dt_bias, v_norm_w, v_sinks, v_w_out, v_ln_g, v_ln_b):
    small = [a_log, dt_bias, norm_w, sinks, ln_g, ln_b]
    weights = []
    for l in range(DEPTH):
        g_in, g_out, g_conv = _all_gather([w_in[l].astype(BF16), w_out[l].astype(BF16), conv_w[l]],
                                          name=f"weights_all_gather_{l}")
        w_in_l = _to_layout(g_in.transpose(1, 0, 2).reshape(D_MODEL, IN_COLS))
        w_out_l = g_out.reshape(D_MODEL, D_MODEL)
        conv_l = jnp.pad(g_conv.transpose(1, 0, 2).reshape(CONV_K, 3 * A_WIDTH), ((0, 8 - CONV_K), (0, 0)))
        weights.append((w_in_l, w_out_l, conv_l))

    y, saved = _forward(x[0], weights, small)
    dx, loss_lanes = _loss_head(y, loss_target[0], name="loss_head")
    loss = lax.psum(0.5 * jnp.sum(loss_lanes) * (1.0 / D_MODEL), ("x", "y", "c"))
    bufs = None
    for l in reversed(range(DEPTH)):
        dx, g = _backward_layer(l, dx, saved[l], weights[l], ln_g[l])
        c_in = g["w_in"].reshape(D_MODEL, N_DEV, SHARD_COLS).transpose(1, 0, 2).astype(BF16)
        c_out = g["w_out"].astype(BF16).reshape(N_DEV, OUT_SHARD_ROWS, D_MODEL)
        c_conv = g["conv_w"].reshape(CONV_K, N_DEV, CONV_SHARD_COLS).transpose(1, 0, 2)
        c_small = [jnp.broadcast_to(g[n][None], (N_DEV,) + g[n].shape) for n, _ in SMALL_SIZES]
        bufs = _exchange([c_in, c_out, _pack_small(c_conv, c_small)], bufs, l, name=f"grad_exchange_{l}")

    p_in, p_out, p_small = bufs
    o_in = _adamw(p_in, w_in, m_w_in, v_w_in, tr=256, name="adamw_w_in")
    o_out = _adamw(p_out, w_out, m_w_out, v_w_out, tr=OUT_SHARD_ROWS, name="adamw_w_out")
    o_small = _adamw(p_small, _pack_small(conv_w, small),
                     _pack_small(m_conv_w, [m_a_log, m_dt_bias, m_norm_w, m_sinks, m_ln_g, m_ln_b]),
                     _pack_small(v_conv_w, [v_a_log, v_dt_bias, v_norm_w, v_sinks, v_ln_g, v_ln_b]),
                     tr=CS_ROWS, name="adamw_small")
    outs = []
    for k in range(4):
        cv, sm = _unpack_small(o_small[k])
        outs += [o_in[k], cv, sm[0], sm[1], sm[2], sm[3], o_out[k], sm[4], sm[5]]
    return (loss, dx[None], *outs)
```

```python
import functools

import jax
import jax.numpy as jnp
from jax import lax
from jax.experimental import pallas as pl
from jax.experimental.pallas import tpu as pltpu

F32 = jnp.float32
BF16 = jnp.bfloat16
MM_DTYPE = BF16

N_DEV = 8
D_MODEL = 1024
DEPTH = 2
A_HEADS = 4
A_HEAD_DIM = 128
A_WIDTH = 512
CONV_K = 4
CHUNK = 64
SUPER = 256
G_HEADS = 2
B_Q_HEADS = 8
B_KV_HEADS = 2
B_HEAD_DIM = 64
B_GROUP = 4
B_WIDTH = 512
WINDOW = 128
BLOCK = 128
IN_COLS = 3336
SHARD_COLS = IN_COLS // N_DEV
OUT_SHARD_ROWS = D_MODEL // N_DEV
CONV_SHARD_COLS = 3 * A_WIDTH // N_DEV
DEEPNORM_ALPHA = (2 * DEPTH) ** 0.25
LN_EPS = 1e-5
RMS_EPS = 1e-6
L2_EPS = 1e-6
ADAM_LR, ADAM_B1, ADAM_B2, ADAM_EPS, ADAM_WD, ADAM_STEP = 0.001, 0.9, 0.999, 1e-08, 0.01, 10

LANE = 128
L_QB, L_ZB, L_KB, L_VB, L_QKV, L_ZA, L_BA = 0, 512, 1024, 1152, 1280, 2816, 3328
L_SWA = 1280
L_MAIN = 3328
L_COLS = 3456
SMALL_SIZES = (("a_log", 4), ("dt_bias", 4), ("norm_w", 128), ("sinks", 8), ("ln_g", 1024), ("ln_b", 1024))
CS_CONV = CONV_K * CONV_SHARD_COLS
CS_ROWS = 24
VMEM_LIMIT = 48 * 1024 * 1024


def _cparams(sem=None):
    return pltpu.CompilerParams(dimension_semantics=sem, vmem_limit_bytes=VMEM_LIMIT)


def _mm(a, b):
    return jnp.dot(a.astype(MM_DTYPE), b.astype(MM_DTYPE), preferred_element_type=F32)


def _mm_nt(a, b):
    return lax.dot_general(a.astype(MM_DTYPE), b.astype(MM_DTYPE), (((1,), (1,)), ((), ())),
                           preferred_element_type=F32)


def _mm_tn(a, b):
    return lax.dot_general(a.astype(MM_DTYPE), b.astype(MM_DTYPE), (((0,), (0,)), ((), ())),
                           preferred_element_type=F32)


def _split(a):
    hi = a.astype(BF16)
    return hi, (a - hi.astype(F32)).astype(BF16)


def _hp(a2, b2):
    d = lambda p, q: jnp.dot(p, q, preferred_element_type=F32)
    return d(a2[0], b2[0]) + (d(a2[0], b2[1]) + d(a2[1], b2[0]))


def _silu(x):
    return x * jax.nn.sigmoid(x)


def _softplus(x):
    return jnp.maximum(x, 0.0) + jnp.log1p(jnp.exp(-jnp.abs(x)))


_ANY = pl.BlockSpec(memory_space=pl.ANY)


def _matmul(a, b, *, form, tm, tn, tk, name, add=None, add_scale=1.0, extra=None):
    if form == "nn":
        (m, kk), n = a.shape, b.shape[1]
        a_spec = pl.BlockSpec((tm, tk), lambda i, j, k: (i, k))
        b_spec = pl.BlockSpec((tk, tn), lambda i, j, k: (k, j))
        dn = (((1,), (0,)), ((), ()))
    elif form == "nt":
        (m, kk), n = a.shape, b.shape[0]
        a_spec = pl.BlockSpec((tm, tk), lambda i, j, k: (i, k))
        b_spec = pl.BlockSpec((tn, tk), lambda i, j, k: (j, k))
        dn = (((1,), (1,)), ((), ()))
    else:
        (kk, m), n = a.shape, b.shape[1]
        a_spec = pl.BlockSpec((tk, tm), lambda i, j, k: (k, i))
        b_spec = pl.BlockSpec((tk, tn), lambda i, j, k: (k, j))
        dn = (((0,), (0,)), ((), ()))
    assert m % tm == 0 and n % tn == 0 and kk % tk == 0, (name, m, n, kk)
    has_add, has_extra = add is not None, extra is not None

    def body(*refs):
        refs = list(refs)
        a_ref, b_ref = refs[:2]
        o_ref = refs[-1]
        rest = refs[2:-1]
        k = pl.program_id(2)
        p = lax.dot_general(a_ref[...].astype(MM_DTYPE), b_ref[...].astype(MM_DTYPE), dn,
                            preferred_element_type=F32)

        @pl.when(k == 0)
        def _():
            first = p
            pos = 0
            if has_extra:
                first = first + _mm_nt(rest[0][...], rest[1][...])
                pos = 2
            if has_add:
                first = first + add_scale * rest[pos][...]
            o_ref[...] = first

        @pl.when(k > 0)
        def _():
            o_ref[...] += p

    in_specs = [a_spec, b_spec]
    args = [a, b]
    if has_extra:
        a2, b2, idx = extra
        in_specs += [pl.BlockSpec((tm, LANE), lambda i, j, k: (i, 0)),
                     pl.BlockSpec((tn, LANE), lambda i, j, k: (j, idx))]
        args += [a2, b2]
    if has_add:
        in_specs.append(pl.BlockSpec((tm, tn), lambda i, j, k: (i, j)))
        args.append(add)
    return pl.pallas_call(
        body, name=name, grid=(m // tm, n // tn, kk // tk), in_specs=in_specs,
        out_specs=pl.BlockSpec((tm, tn), lambda i, j, k: (i, j)),
        out_shape=jax.ShapeDtypeStruct((m, n), F32),
        compiler_params=_cparams(("parallel", "parallel", "arbitrary")),
    )(*args)


def _shift_down(x, k, row):
    return jnp.where(row >= k, pltpu.roll(x, k, 0), 0.0)


def _shift_up(x, k, row, t_len):
    return jnp.where(row < t_len - k, pltpu.roll(x, t_len - k, 0), 0.0)


def _conv_slab(x, w, row):
    return (w[3:4] * x + w[2:3] * _shift_down(x, 1, row) + w[1:2] * _shift_down(x, 2, row)
            + w[0:1] * _shift_down(x, 3, row))


def _prep_fwd(h, conv_w, *, name):
    t_len = h.shape[0]

    def body(x_ref, w_ref, o_ref):
        s = pl.program_id(0)
        row = lax.broadcasted_iota(jnp.int32, (t_len, LANE), 0)
        y = _silu(_conv_slab(x_ref[...], w_ref[...], row))
        rs = lax.rsqrt(jnp.sum(y * y, axis=-1, keepdims=True) + L2_EPS)
        scale = jnp.where(s < A_HEADS, A_HEAD_DIM ** -0.5, 1.0)
        o_ref[...] = jnp.where(s < 2 * A_HEADS, y * rs * scale, y)

    return pl.pallas_call(
        body, name=name, grid=(12,),
        in_specs=[pl.BlockSpec((t_len, LANE), lambda s: (0, L_QKV // LANE + s)),
                  pl.BlockSpec((8, LANE), lambda s: (0, s))],
        out_specs=pl.BlockSpec((t_len, LANE), lambda s: (0, s)),
        out_shape=jax.ShapeDtypeStruct((t_len, 3 * A_WIDTH), F32),
        compiler_params=_cparams(("parallel",)),
    )(h, conv_w)


def _prep_bwd(h, conv_w, d_out, dh, *, name):
    t_len = h.shape[0]

    def body(x_ref, w_ref, g_ref, dh_in, dx_ref, dw_ref):
        del dh_in
        s = pl.program_id(0)
        row = lax.broadcasted_iota(jnp.int32, (t_len, LANE), 0)
        x = x_ref[...]
        w = w_ref[...]
        c = _conv_slab(x, w, row)
        sg = jax.nn.sigmoid(c)
        y = c * sg
        g = g_ref[0]
        rs = lax.rsqrt(jnp.sum(y * y, axis=-1, keepdims=True) + L2_EPS)
        scale = jnp.where(s < A_HEADS, A_HEAD_DIM ** -0.5, 1.0)
        dy_n = scale * (rs * g - y * (rs * rs * rs) * jnp.sum(g * y, axis=-1, keepdims=True))
        dy = jnp.where(s < 2 * A_HEADS, dy_n, g)
        dc = dy * (sg * (1.0 + c * (1.0 - sg)))
        dx_ref[...] = (w[3:4] * dc + w[2:3] * _shift_up(dc, 1, row, t_len)
                       + w[1:2] * _shift_up(dc, 2, row, t_len) + w[0:1] * _shift_up(dc, 3, row, t_len))
        dws = [jnp.sum(dc * _shift_down(x, 3 - j, row), axis=0, keepdims=True) if j < 3
               else jnp.sum(dc * x, axis=0, keepdims=True) for j in range(CONV_K)]
        dw_ref[...] = jnp.concatenate(dws + [jnp.zeros((8 - CONV_K, LANE), F32)], axis=0)

    slab = pl.BlockSpec((t_len, LANE), lambda s: (0, L_QKV // LANE + s))
    return pl.pallas_call(
        body, name=name, grid=(12,),
        in_specs=[slab, pl.BlockSpec((8, LANE), lambda s: (0, s)),
                  pl.BlockSpec((1, t_len, LANE), lambda s: (s // A_HEADS, 0, s % A_HEADS)), _ANY],
        out_specs=[slab, pl.BlockSpec((8, LANE), lambda s: (0, s))],
        out_shape=[jax.ShapeDtypeStruct((t_len, L_MAIN), F32), jax.ShapeDtypeStruct((8, 3 * A_WIDTH), F32)],
        input_output_aliases={3: 0},
        compiler_params=_cparams(("parallel",)),
    )(h, conv_w, d_out, dh)


MF_TRIL, MF_STRIL, MF_DIAG8, MF_LOW16, MF_EYE = 0, 1, 2, 3, 6
MB_CUM, MB_CUM_T, MB_TOT = 0, 1, 2


def _gdn_masks():
    r = lax.broadcasted_iota(jnp.int32, (SUPER, SUPER), 0)
    c = lax.broadcasted_iota(jnp.int32, (SUPER, SUPER), 1)
    same = lambda shift: (r >> shift) == (c >> shift)
    chunk = same(6)
    ninf = lambda m: jnp.where(m, 0.0, -jnp.inf).astype(F32)
    one = lambda m: m.astype(F32)
    mf = jnp.stack([ninf(chunk & (r >= c)), ninf(chunk & (r > c)), one(same(3))]
                   + [one(same(sh) & jnp.logical_not(same(sh - 1))) for sh in (4, 5, 6)] + [one(r == c)])
    mb = jnp.stack([one(chunk & (r >= c)), one(chunk & (r <= c)), one(chunk)]).astype(BF16)
    return mf, mb


def _tri_inv_impl(a, mf):
    eye = mf[MF_EYE]
    a0 = a * mf[MF_DIAG8]
    a0s = _split(a0)
    a2 = _hp(a0s, a0s)
    a2s = _split(a2)
    a4 = _hp(a2s, a2s)
    t = _hp(_split(_hp(_split(eye - a0), _split(eye + a2))), _split(eye + a4))
    for level in range(3):
        ts = _split(t)
        t = t - _hp(_split(_hp(ts, _split(a * mf[MF_LOW16 + level]))), ts)
    return t


def _tri_inv_cotangent(t, dt):
    tts = _split(t.T)
    return -_hp(_split(_hp(tts, _split(dt))), tts)


@jax.custom_vjp
def _tri_inv(a, mf):
    return _tri_inv_impl(a, mf)


def _tri_inv_fwd(a, mf):
    t = _tri_inv_impl(a, mf)
    return t, (t, mf)


_tri_inv.defvjp(_tri_inv_fwd, lambda res, dt: (_tri_inv_cotangent(res[0], dt), jnp.zeros_like(res[1])))


@jax.custom_vjp
def _tri_inv_known(a, t):
    return t


def _tri_inv_known_fwd(a, t):
    return t, t


def _tri_inv_known_bwd(t, dt):
    return _tri_inv_cotangent(t, dt), jnp.zeros_like(t)


_tri_inv_known.defvjp(_tri_inv_known_fwd, _tri_inv_known_bwd)


@functools.partial(jax.custom_vjp, nondiff_argnums=(1,))
def _lane_roll(x, shift):
    return pltpu.roll(x, shift % LANE, 1)


_lane_roll.defvjp(lambda x, shift: (_lane_roll(x, shift), None), lambda shift, _, g: (_lane_roll(g, -shift),))


def _mask_times_lanes(x, mask):
    lane = lax.broadcasted_iota(jnp.int32, (1, LANE), 1)
    x = jnp.where(lane < A_HEADS, x, 0.0)
    x1 = x.astype(BF16).astype(F32)
    x2 = (x - x1).astype(BF16).astype(F32)
    x3 = (x - x1 - x2).astype(BF16).astype(F32)
    pieces = x1 + pltpu.roll(x2, A_HEADS, 1) + pltpu.roll(x3, 2 * A_HEADS, 1)
    res = jnp.dot(mask, pieces.astype(BF16), preferred_element_type=F32)
    return res + pltpu.roll(res, LANE - A_HEADS, 1) + pltpu.roll(res, LANE - 2 * A_HEADS, 1)


@jax.custom_vjp
def _chunk_sums(g, mb):
    return _mask_times_lanes(g, mb[MB_CUM]), _mask_times_lanes(g, mb[MB_TOT])


def _chunk_sums_fwd(g, mb):
    return _chunk_sums(g, mb), mb


def _chunk_sums_bwd(mb, d):
    lane = lax.broadcasted_iota(jnp.int32, (1, LANE), 1)
    dg = _mask_times_lanes(d[0], mb[MB_CUM_T]) + _mask_times_lanes(d[1], mb[MB_TOT])
    return jnp.where(lane < A_HEADS, dg, 0.0), jnp.zeros_like(mb)


_chunk_sums.defvjp(_chunk_sums_fwd, _chunk_sums_bwd)


def _gdn_gates(ba, alog, dtb, mb):
    beta = jax.nn.sigmoid(ba)
    g = -jnp.exp(alog) * _softplus(_lane_roll(ba, -A_HEADS) + dtb)
    gc, gl = _chunk_sums(g, mb)
    return beta, gc, gl, gc.T


def _gdn_block(s, q, k, v, z, gates, nw, h, t_known, mf):
    n = q.shape[0]
    beta_all, gc_all, gl_all, gct_all = gates
    lane = lax.broadcasted_iota(jnp.int32, (1, LANE), 1)
    sub = lax.broadcasted_iota(jnp.int32, (LANE, 1), 0)
    col = lambda x: jnp.sum(jnp.where(lane == h, x, 0.0), axis=1, keepdims=True)
    wide = lambda c: jnp.broadcast_to(c, (n, LANE))
    gc, gl = col(gc_all), col(gl_all)
    gc_row = jnp.sum(jnp.where(sub == h, gct_all, 0.0), axis=0, keepdims=True)
    beta_w, eg_w = wide(col(beta_all)), wide(jnp.exp(gc))
    diff = gc - gc_row
    decay = jnp.exp(diff + mf[MF_TRIL])
    kb = k * beta_w
    a_mat = _mm_nt(kb, k) * jnp.exp(diff + mf[MF_STRIL])
    t_mat = _tri_inv(a_mat, mf) if t_known is None else _tri_inv_known(a_mat, t_known)
    u = _mm(t_mat, v * beta_w)
    w = _mm(t_mat, kb * eg_w)
    qk = _mm_nt(q, k) * decay
    q_dec = q * eg_w
    k_dec = k * wide(jnp.exp(gl - gc))
    g_tot = jnp.exp(gl)
    outs = []
    for ci in range(n // CHUNK):
        lo, hi = ci * CHUNK, (ci + 1) * CHUNK
        v_new = u[lo:hi] - _mm(w[lo:hi], s)
        pieces = []
        if lo:
            pieces.append(jnp.zeros((lo, LANE), F32))
        pieces.append(v_new)
        if n - hi:
            pieces.append(jnp.zeros((n - hi, LANE), F32))
        v_pad = jnp.concatenate(pieces, axis=0) if len(pieces) > 1 else v_new
        outs.append(_mm(q_dec[lo:hi], s) + _mm(qk[lo:hi], v_pad))
        s = s * g_tot[lo:lo + 1] + _mm_tn(k_dec[lo:hi], v_new)
    o = jnp.concatenate(outs, axis=0)
    o = o * lax.rsqrt(jnp.mean(o * o, axis=-1, keepdims=True) + RMS_EPS) * nw
    return o * _silu(z), s, t_mat


def _gdn_fwd(qkv, h, alog, dtb, nw, ycat, *, name, rider=None):
    t_len = qkv.shape[0]
    nsc = t_len // SUPER

    n_rin, n_rout = (len(rider.args), rider.n) if rider else (0, 0)

    def body(*refs):
        q_ref, k_ref, v_ref, z_ref, ba_ref, al_ref, dt_ref, nw_ref, mf_ref, mb_ref, _ = refs[:11]
        y_ref, sin_ref, t_ref = refs[11 + n_rin:14 + n_rin]
        s_scr = refs[14 + n_rin + n_rout]
        sc, pg = pl.program_id(0), pl.program_id(1)
        heads = [pg * G_HEADS + hh for hh in range(G_HEADS)]
        if rider:
            r_refs = (refs[11:11 + rider.n], refs[14 + n_rin:14 + n_rin + n_rout], refs[15 + n_rin + n_rout:])
            pl.when((sc == 0) & (pg == 0))(lambda: rider.start(*r_refs))

        @pl.when(sc == 0)
        def _():
            for hd in heads:
                s_scr[hd] = jnp.zeros((A_HEAD_DIM, A_HEAD_DIM), F32)

        per_head = lambda ref: jnp.stack([ref[:, hh * LANE:(hh + 1) * LANE] for hh in range(G_HEADS)])
        states = jnp.stack([s_scr[hd] for hd in heads])
        gates = _gdn_gates(ba_ref[...], al_ref[...], dt_ref[...], mb_ref[...])
        fn = jax.vmap(_gdn_block, in_axes=(0, 0, 0, 0, 0, None, None, 0, None, None))
        y, s_new, t_mat = fn(states, per_head(q_ref), per_head(k_ref), per_head(v_ref), per_head(z_ref),
                             gates, nw_ref[...], jnp.stack(heads), None, mf_ref[...])
        sin_ref[0] = states
        t_ref[0] = t_mat
        for hh, hd in enumerate(heads):
            y_ref[:, hh * LANE:(hh + 1) * LANE] = y[hh]
            s_scr[hd] = s_new[hh]
        if rider:
            pl.when((sc == nsc - 1) & (pg == A_HEADS // G_HEADS - 1))(lambda: rider.wait(*r_refs))

    gw = G_HEADS * LANE
    blk = lambda off: pl.BlockSpec((SUPER, gw), lambda sc, pg: (sc, off // G_HEADS + pg))
    row = pl.BlockSpec((1, LANE), lambda sc, pg: (0, 0))
    mf, mb = _gdn_masks()
    whole = lambda a: pl.BlockSpec(a.shape, lambda sc, pg: (0, 0, 0))
    aliases = {10: 0}
    if rider:
        aliases.update(rider.aliases(11, 3))
    return pl.pallas_call(
        body, name=name, grid=(nsc, A_HEADS // G_HEADS),
        in_specs=[blk(0), blk(4), blk(8), blk(L_ZA // LANE),
                  pl.BlockSpec((SUPER, LANE), lambda sc, pg: (sc, L_BA // LANE)), row, row, row,
                  whole(mf), whole(mb), _ANY] + [_ANY] * n_rin,
        out_specs=[blk(0),
                   pl.BlockSpec((1, G_HEADS, A_HEAD_DIM, A_HEAD_DIM), lambda sc, pg: (sc, pg, 0, 0)),
                   pl.BlockSpec((1, G_HEADS, SUPER, SUPER), lambda sc, pg: (sc, pg, 0, 0))] + [_ANY] * n_rout,
        out_shape=[jax.ShapeDtypeStruct((t_len, D_MODEL), F32),
                   jax.ShapeDtypeStruct((nsc, A_HEADS, A_HEAD_DIM, A_HEAD_DIM), F32),
                   jax.ShapeDtypeStruct((nsc, A_HEADS, SUPER, SUPER), F32)] + (rider.out_shape if rider else []),
        scratch_shapes=[pltpu.VMEM((A_HEADS, A_HEAD_DIM, A_HEAD_DIM), F32)] + (rider.scratch if rider else []),
        input_output_aliases=aliases,
        compiler_params=_cparams(("arbitrary", "arbitrary")),
    )(qkv, qkv, qkv, h, h, alog, dtb, nw, mf, mb, ycat, *(rider.args if rider else []))


def _gdn_bwd(qkv, h, alog, dtb, nw, s_in, t_in, dycat, dh, *, name, rider=None):
    t_len = qkv.shape[0]
    nsc = t_len // SUPER
    n_rin, n_rout = (len(rider.args), rider.n) if rider else (0, 0)

    def body(*refs):
        (q_ref, k_ref, v_ref, z_ref, ba_ref, al_ref, dt_ref, nw_ref, sin_ref, t_ref, dy_ref, mf_ref, mb_ref,
         _) = refs[:14]
        dz_ref, dqkv_ref, dba_ref, dal_ref, ddt_ref, dnw_ref = refs[14 + n_rin:20 + n_rin]
        ds_scr = refs[20 + n_rin + n_rout]
        i, pg = pl.program_id(0), pl.program_id(1)
        if rider:
            r_refs = (refs[14:14 + rider.n], refs[20 + n_rin:20 + n_rin + n_rout], refs[21 + n_rin + n_rout:])
            pl.when((i == 0) & (pg == 0))(lambda: rider.start(*r_refs))

        @pl.when((i == 0) & (pg == 0))
        def _():
            dal_ref[...] = jnp.zeros_like(dal_ref)
            ddt_ref[...] = jnp.zeros_like(ddt_ref)
            dnw_ref[...] = jnp.zeros_like(dnw_ref)

        @pl.when(pg == 0)
        def _():
            dba_ref[...] = jnp.zeros_like(dba_ref)

        heads = [pg * G_HEADS + hh for hh in range(G_HEADS)]

        @pl.when(i == 0)
        def _():
            for hd in heads:
                ds_scr[hd] = jnp.zeros((A_HEAD_DIM, A_HEAD_DIM), F32)

        per_head = lambda ref: jnp.stack([ref[:, hh * LANE:(hh + 1) * LANE] for hh in range(G_HEADS)])
        d_states = jnp.stack([ds_scr[hd] for hd in heads])
        head_ids = jnp.stack(heads)
        t_known, mf, mb = t_ref[0], mf_ref[...], mb_ref[...]

        def fn(s, q, k, v, z, ba, alog, dtb, nw):
            gates = _gdn_gates(ba, alog, dtb, mb)
            one = lambda s, q, k, v, z, t, h: _gdn_block(s, q, k, v, z, gates, nw, h, t, mf)[:2]
            return jax.vmap(one)(s, q, k, v, z, t_known, head_ids)

        _, vjp = jax.vjp(fn, sin_ref[0], per_head(q_ref), per_head(k_ref), per_head(v_ref), per_head(z_ref),
                         ba_ref[...], al_ref[...], dt_ref[...], nw_ref[...])
        ds, dq, dk, dv, dz, dba, dal, ddt, dnw = vjp((per_head(dy_ref), d_states))
        for hh, hd in enumerate(heads):
            cols = slice(hh * LANE, (hh + 1) * LANE)
            ds_scr[hd] = ds[hh]
            dqkv_ref[0, :, cols] = dq[hh]
            dqkv_ref[1, :, cols] = dk[hh]
            dqkv_ref[2, :, cols] = dv[hh]
            dz_ref[:, cols] = dz[hh]
        dba_ref[...] += dba
        dal_ref[...] += dal
        ddt_ref[...] += ddt
        dnw_ref[...] += dnw
        if rider:
            pl.when((i == nsc - 1) & (pg == A_HEADS // G_HEADS - 1))(lambda: rider.wait(*r_refs))

    rev = lambda i: nsc - 1 - i
    gw = G_HEADS * LANE
    blk = lambda off: pl.BlockSpec((SUPER, gw), lambda i, pg: (rev(i), off // G_HEADS + pg))
    row = pl.BlockSpec((1, LANE), lambda i, pg: (0, 0))
    ba_blk = lambda off: pl.BlockSpec((SUPER, LANE), lambda i, pg: (rev(i), off))
    mf, mb = _gdn_masks()
    whole = lambda a: pl.BlockSpec(a.shape, lambda i, pg: (0, 0, 0))
    aliases = {13: 0}
    if rider:
        aliases.update(rider.aliases(14, 6))
    return pl.pallas_call(
        body, name=name, grid=(nsc, A_HEADS // G_HEADS),
        in_specs=[blk(0), blk(4), blk(8), blk(L_ZA // LANE), ba_blk(L_BA // LANE), row, row, row,
                  pl.BlockSpec((1, G_HEADS, A_HEAD_DIM, A_HEAD_DIM), lambda i, pg: (rev(i), pg, 0, 0)),
                  pl.BlockSpec((1, G_HEADS, SUPER, SUPER), lambda i, pg: (rev(i), pg, 0, 0)),
                  blk(0), whole(mf), whole(mb), _ANY] + [_ANY] * n_rin,
        out_specs=[blk(L_ZA // LANE),
                   pl.BlockSpec((3, SUPER, gw), lambda i, pg: (0, rev(i), pg)),
                   ba_blk(0), row, row, row] + [_ANY] * n_rout,
        out_shape=[jax.ShapeDtypeStruct((t_len, L_MAIN), F32), jax.ShapeDtypeStruct((3, t_len, A_WIDTH), F32),
                   jax.ShapeDtypeStruct((t_len, LANE), F32)] + [jax.ShapeDtypeStruct((1, LANE), F32)] * 3
        + (rider.out_shape if rider else []),
        scratch_shapes=[pltpu.VMEM((A_HEADS, A_HEAD_DIM, A_HEAD_DIM), F32)] + (rider.scratch if rider else []),
        input_output_aliases=aliases,
        compiler_params=_cparams(("arbitrary", "arbitrary")),
    )(qkv, qkv, qkv, h, h, alog, dtb, nw, s_in, t_in, dycat, mf, mb, dh, *(rider.args if rider else []))


def _swa_block(q, kp, kc, vp, vc, z, sinks, first):
    rows = B_GROUP * BLOCK
    ri = lax.broadcasted_iota(jnp.int32, (rows, 2 * BLOCK), 0)
    si = lax.broadcasted_iota(jnp.int32, (rows, 2 * BLOCK), 1)
    dist = (ri & (BLOCK - 1)) + BLOCK - si
    bias = jnp.where((dist >= 0) & (dist < WINDOW) & ((si >= BLOCK) | jnp.logical_not(first)), 0.0, -jnp.inf)
    dist_f = dist.astype(F32)
    head_of_row = lax.broadcasted_iota(jnp.int32, (rows, 1), 0) >> 7

    def group(j):
        cs = slice(j * B_HEAD_DIM, (j + 1) * B_HEAD_DIM)
        heads = range(j * B_GROUP, (j + 1) * B_GROUP)
        qs = jnp.concatenate([q[:, hq * B_HEAD_DIM:(hq + 1) * B_HEAD_DIM] for hq in heads], axis=0)
        kk = jnp.concatenate([kp[:, cs], kc[:, cs]], axis=0)
        vv = jnp.concatenate([vp[:, cs], vc[:, cs]], axis=0)
        sink = jnp.concatenate([jnp.broadcast_to(sinks[:, hq:hq + 1], (BLOCK, 1)) for hq in heads], axis=0)
        slope = sum(jnp.where(head_of_row == gi, 2.0 ** (-8.0 * (hq + 1) / B_Q_HEADS), 0.0)
                    for gi, hq in enumerate(heads))
        return qs, kk, vv, sink, slope

    def attend(qs, kk, vv, sink, slope):
        sc = _mm_nt(qs, kk) * (B_HEAD_DIM ** -0.5) - slope * dist_f + bias
        m = lax.stop_gradient(jnp.maximum(jnp.max(sc, axis=-1, keepdims=True), sink))
        p = jnp.exp(sc - m)
        inv = 1.0 / (jnp.sum(p, axis=-1, keepdims=True) + jnp.exp(sink - m))
        return _mm(p * inv, vv)

    o = jax.vmap(attend)(*[jnp.stack(t) for t in zip(*[group(j) for j in range(B_KV_HEADS)])])
    outs = [o[j, gi * BLOCK:(gi + 1) * BLOCK] for j in range(B_KV_HEADS) for gi in range(B_GROUP)]
    return jnp.concatenate(outs, axis=1) * _silu(z)


def _swa_specs(idx):
    wide = lambda off: pl.BlockSpec((BLOCK, B_WIDTH), lambda n: (idx(n), off))
    cur = lambda off: pl.BlockSpec((BLOCK, LANE), lambda n: (idx(n), off))
    prev = lambda off: pl.BlockSpec((BLOCK, LANE), lambda n: (jnp.maximum(idx(n) - 1, 0), off))
    return [wide(L_QB // B_WIDTH), prev(L_KB // LANE), cur(L_KB // LANE), prev(L_VB // LANE), cur(L_VB // LANE),
            wide(L_ZB // B_WIDTH), pl.BlockSpec((1, LANE), lambda n: (0, 0))]


def _swa_fwd(h, sinks, *, name):
    t_len = h.shape[0]
    nb = t_len // BLOCK

    def body(q_ref, kp_ref, kc_ref, vp_ref, vc_ref, z_ref, s_ref, o_ref):
        o_ref[...] = _swa_block(q_ref[...], kp_ref[...], kc_ref[...], vp_ref[...], vc_ref[...], z_ref[...],
                                s_ref[...], pl.program_id(0) == 0)

    return pl.pallas_call(
        body, name=name, grid=(nb,), in_specs=_swa_specs(lambda n: n),
        out_specs=pl.BlockSpec((BLOCK, B_WIDTH), lambda n: (n, 1)),
        out_shape=jax.ShapeDtypeStruct((t_len, D_MODEL), F32),
        compiler_params=_cparams(("parallel",)),
    )(h, h, h, h, h, h, sinks)


def _swa_bwd(h, sinks, dycat, *, name):
    t_len = h.shape[0]
    nb = t_len // BLOCK

    def body(q_ref, kp_ref, kc_ref, vp_ref, vc_ref, z_ref, s_ref, dy_ref, dh_ref, dsk_ref, ck_scr, cv_scr):
        i = pl.program_id(0)
        n = nb - 1 - i

        @pl.when(i == 0)
        def _():
            ck_scr[...] = jnp.zeros_like(ck_scr)
            cv_scr[...] = jnp.zeros_like(cv_scr)
            dsk_ref[...] = jnp.zeros_like(dsk_ref)

        fn = functools.partial(_swa_block, first=(n == 0))
        _, vjp = jax.vjp(fn, q_ref[...], kp_ref[...], kc_ref[...], vp_ref[...], vc_ref[...], z_ref[...], s_ref[...])
        dq, dkp, dkc, dvp, dvc, dz, dsk = vjp(dy_ref[...])
        dh_ref[:, L_QB:L_QB + B_WIDTH] = dq
        dh_ref[:, L_ZB:L_ZB + B_WIDTH] = dz
        dh_ref[:, L_KB:L_KB + LANE] = dkc + ck_scr[...]
        dh_ref[:, L_VB:L_VB + LANE] = dvc + cv_scr[...]
        ck_scr[...] = dkp
        cv_scr[...] = dvp
        dsk_ref[...] += dsk

    rev = lambda i: nb - 1 - i
    return pl.pallas_call(
        body, name=name, grid=(nb,),
        in_specs=_swa_specs(rev) + [pl.BlockSpec((BLOCK, B_WIDTH), lambda i: (rev(i), 1))],
        out_specs=[pl.BlockSpec((BLOCK, L_SWA), lambda i: (rev(i), 0)), pl.BlockSpec((1, LANE), lambda i: (0, 0))],
        out_shape=[jax.ShapeDtypeStruct((t_len, L_MAIN), F32), jax.ShapeDtypeStruct((1, LANE), F32)],
        scratch_shapes=[pltpu.VMEM((BLOCK, LANE), F32), pltpu.VMEM((BLOCK, LANE), F32)],
        compiler_params=_cparams(("arbitrary",)),
    )(h, h, h, h, h, h, sinks, dycat)


def _out_ln_fwd(ycat, w_out, x, ln_g, ln_b, *, name, tm=256):
    t_len = x.shape[0]

    def body(y_ref, w_ref, x_ref, g_ref, b_ref, r_ref, o_ref):
        r = DEEPNORM_ALPHA * x_ref[...] + _mm(y_ref[...], w_ref[...])
        r_ref[...] = r
        mu = jnp.mean(r, axis=-1, keepdims=True)
        d = r - mu
        var = jnp.mean(d * d, axis=-1, keepdims=True)
        o_ref[...] = d * lax.rsqrt(var + LN_EPS) * g_ref[...] + b_ref[...]

    tile = pl.BlockSpec((tm, D_MODEL), lambda i: (i, 0))
    vec = pl.BlockSpec((1, D_MODEL), lambda i: (0, 0))
    return pl.pallas_call(
        body, name=name, grid=(t_len // tm,),
        in_specs=[tile, pl.BlockSpec((D_MODEL, D_MODEL), lambda i: (0, 0)), tile, vec, vec],
        out_specs=[tile, tile],
        out_shape=[jax.ShapeDtypeStruct((t_len, D_MODEL), F32)] * 2,
        compiler_params=_cparams(("parallel",)),
    )(ycat, w_out, x, ln_g, ln_b)


def _ln_bwd(dxn, r, ln_g, *, name, tm=256):
    t_len = r.shape[0]

    def body(dx_ref, r_ref, g_ref, dr_ref, dg_ref, db_ref):
        @pl.when(pl.program_id(0) == 0)
        def _():
            dg_ref[...] = jnp.zeros_like(dg_ref)
            db_ref[...] = jnp.zeros_like(db_ref)

        rr = r_ref[...]
        dx = dx_ref[...]
        mu = jnp.mean(rr, axis=-1, keepdims=True)
        d = rr - mu
        rstd = lax.rsqrt(jnp.mean(d * d, axis=-1, keepdims=True) + LN_EPS)
        xh = d * rstd
        dxh = dx * g_ref[...]
        dr_ref[...] = rstd * (dxh - jnp.mean(dxh, axis=-1, keepdims=True)
                              - xh * jnp.mean(dxh * xh, axis=-1, keepdims=True))
        dg_ref[...] += jnp.sum(dx * xh, axis=0, keepdims=True)
        db_ref[...] += jnp.sum(dx, axis=0, keepdims=True)

    tile = pl.BlockSpec((tm, D_MODEL), lambda i: (i, 0))
    vec = pl.BlockSpec((1, D_MODEL), lambda i: (0, 0))
    return pl.pallas_call(
        body, name=name, grid=(t_len // tm,),
        in_specs=[tile, tile, vec], out_specs=[tile, vec, vec],
        out_shape=[jax.ShapeDtypeStruct((t_len, D_MODEL), F32), jax.ShapeDtypeStruct((1, D_MODEL), F32),
                   jax.ShapeDtypeStruct((1, D_MODEL), F32)],
        compiler_params=_cparams(("arbitrary",)),
    )(dxn, r, ln_g)


def _loss_head(y, target, *, name, tm=256):
    t_len = y.shape[0]

    def body(y_ref, t_ref, d_ref, l_ref):
        @pl.when(pl.program_id(0) == 0)
        def _():
            l_ref[...] = jnp.zeros_like(l_ref)

        e = y_ref[...] - t_ref[...]
        d_ref[...] = e * (1.0 / D_MODEL)
        l_ref[...] += jnp.sum(e * e, axis=0, keepdims=True)

    tile = pl.BlockSpec((tm, D_MODEL), lambda i: (i, 0))
    vec = pl.BlockSpec((1, D_MODEL), lambda i: (0, 0))
    return pl.pallas_call(
        body, name=name, grid=(t_len // tm,), in_specs=[tile, tile], out_specs=[tile, vec],
        out_shape=[jax.ShapeDtypeStruct((t_len, D_MODEL), F32), jax.ShapeDtypeStruct((1, D_MODEL), F32)],
        compiler_params=_cparams(("arbitrary",)),
    )(y, target)


def _pad_row(v):
    return jnp.zeros((1, LANE), F32).at[0, :v.shape[0]].set(v)


def _to_layout(w_full):
    s = lambda a, b: w_full[..., a:b]
    pad = jnp.zeros(w_full.shape[:-1] + (LANE - 2 * A_HEADS,), w_full.dtype)
    return jnp.concatenate([s(2056, 2568), s(2824, 3336), s(2568, 2696), s(2696, 2824), s(0, 1536), s(1536, 2048),
                            s(2048, 2056), pad], axis=-1)


def _from_layout(g_main, g_ba):
    s = lambda a, b: g_main[..., a:b]
    return jnp.concatenate([s(L_QKV, L_QKV + 1536), s(L_ZA, L_ZA + 512), g_ba[..., :2 * A_HEADS],
                            s(L_QB, L_QB + 512), s(L_KB, L_KB + 128), s(L_VB, L_VB + 128), s(L_ZB, L_ZB + 512)],
                           axis=-1)


def _unpack_weights(g_in, g_out, g_conv):
    w_in_l = _to_layout(g_in.transpose(1, 0, 2).reshape(D_MODEL, IN_COLS))
    conv_l = jnp.pad(g_conv.transpose(1, 0, 2).reshape(CONV_K, 3 * A_WIDTH), ((0, 8 - CONV_K), (0, 0)))
    return w_in_l, g_out.reshape(D_MODEL, D_MODEL), conv_l


def _forward(x, weights, shards1, small):
    a_log, dt_bias, norm_w, sinks, ln_g, ln_b = small
    tm = min(512, x.shape[0])
    saved, weights = [], list(weights)
    for l in range(DEPTH):
        w_in_l, w_out_l, conv_l = weights[l]
        h = _matmul(x, w_in_l, form="nn", tm=tm, tn=1152, tk=D_MODEL, name=f"in_proj_{l}")
        qkv = _prep_fwd(h, conv_l, name=f"prep_fwd_{l}")
        al, dt, nw, sk = _pad_row(a_log[l]), _pad_row(dt_bias[l]), norm_w[l][None, :], _pad_row(sinks[l])
        ycat = _swa_fwd(h, sk, name=f"swa_fwd_{l}")
        rider = None
        if l == 0 and len(weights) == 1:
            rider = _Direct(shards1, None, (), False, [(N_DEV,) + s.shape for s in shards1])
        ycat, s_in, t_in, *gathered = _gdn_fwd(qkv, h, al, dt, nw, ycat, name=f"gdn_fwd_{l}", rider=rider)
        if rider:
            weights.append(_unpack_weights(*gathered))
        r, xn = _out_ln_fwd(ycat, w_out_l, x, ln_g[l][None, :], ln_b[l][None, :], name=f"out_ln_{l}")
        saved.append((x, h, qkv, s_in, t_in, ycat, r, al, dt, nw, sk))
        x = xn
    return x, saved, weights


def _backward_layer(l, dx, saved_l, weights_l, ln_g_l, rider=None):
    x_in, h, qkv, s_in, t_in, ycat, r, al, dt, nw, sk = saved_l
    w_in_l, w_out_l, conv_l = weights_l
    tm = min(512, x_in.shape[0])
    dr, d_lng, d_lnb = _ln_bwd(dx, r, ln_g_l[None, :], name=f"ln_bwd_{l}")
    dycat = _matmul(dr, w_out_l, form="nt", tm=tm, tn=D_MODEL, tk=D_MODEL, name=f"out_proj_dx_{l}")
    d_wout = _matmul(ycat, dr, form="tn", tm=512, tn=D_MODEL, tk=tm, name=f"out_proj_dw_{l}")
    dh, d_sk = _swa_bwd(h, sk, dycat, name=f"swa_bwd_{l}")
    dh, dqkv_n, dba, d_al, d_dt, d_nw, *rider_bufs = _gdn_bwd(qkv, h, al, dt, nw, s_in, t_in, dycat, dh,
                                                              name=f"gdn_bwd_{l}", rider=rider)
    dh, d_conv = _prep_bwd(h, conv_l, dqkv_n, dh, name=f"prep_bwd_{l}")
    d_win_main = _matmul(x_in, dh, form="tn", tm=512, tn=L_MAIN // 2, tk=tm, name=f"in_proj_dw_{l}")
    d_win_ba = _matmul(x_in, dba, form="tn", tm=D_MODEL, tn=LANE, tk=tm, name=f"in_proj_dw_ba_{l}")
    dx = _matmul(dh, w_in_l, form="nt", tm=tm, tn=D_MODEL, tk=L_MAIN // 2, name=f"in_proj_dx_{l}",
                 add=dr, add_scale=DEEPNORM_ALPHA, extra=(dba, w_in_l, L_BA // LANE))
    grads = dict(w_in=_from_layout(d_win_main, d_win_ba), w_out=d_wout, conv_w=d_conv[:CONV_K],
                 a_log=d_al[0, :A_HEADS], dt_bias=d_dt[0, :A_HEADS], norm_w=d_nw[0], sinks=d_sk[0, :B_Q_HEADS],
                 ln_g=d_lng[0], ln_b=d_lnb[0])
    return dx, grads, rider_bufs


def _me():
    return lax.axis_index("x"), lax.axis_index("y"), lax.axis_index("c")


def _flat_id(pos):
    return 4 * pos[0] + 2 * pos[1] + pos[2]


def _remote(src, dst, send_sem, recv_sem, to):
    return pltpu.make_async_remote_copy(src_ref=src, dst_ref=dst, send_sem=send_sem, recv_sem=recv_sem,
                                        device_id=to, device_id_type=pl.DeviceIdType.MESH)


def _all_gather(shards, *, name):
    n_arr = len(shards)

    def body(*refs):
        x_refs, out_refs = refs[:n_arr], refs[n_arr:2 * n_arr]
        send_sems, recv_sems, local_sems = refs[2 * n_arr:]
        x, y, c = _me()
        me, sibling = (x, y, c), (x, y, 1 - c)
        chips = [(1 - x, y), (x, 1 - y), (1 - x, 1 - y)]

        def copy(a, k, block, to, src=None):
            dst = out_refs[a].at[_flat_id(block)]
            return _remote(dst if src is None else src, dst, send_sems.at[a, k], recv_sems.at[a, k], to)

        mine = [pltpu.make_async_copy(x_refs[a], out_refs[a].at[_flat_id(me)], local_sems.at[a])
                for a in range(n_arr)]
        for cp in mine:
            cp.start()
        first = []
        for a in range(n_arr):
            first.append(copy(a, 0, me, sibling, src=x_refs[a]))
            first += [copy(a, 1 + j, me, (*chip, c), src=x_refs[a]) for j, chip in enumerate(chips)]
        for cp in first:
            cp.start()
        passed = []
        for j, chip in enumerate(chips):
            for a in range(n_arr):
                copy(a, 1 + j, (*chip, c), me).wait_recv()
                fwd = copy(a, 4 + j, (*chip, c), sibling)
                fwd.start()
                passed.append(fwd)
        for a in range(n_arr):
            copy(a, 0, sibling, me).wait_recv()
            for j, chip in enumerate(chips):
                copy(a, 4 + j, (*chip, 1 - c), me).wait_recv()
        for cp in first + passed:
            cp.wait_send()
        for cp in mine:
            cp.wait()

    return pl.pallas_call(
        body, name=name, in_specs=[_ANY] * n_arr, out_specs=[_ANY] * n_arr,
        out_shape=[jax.ShapeDtypeStruct((N_DEV,) + s.shape, s.dtype) for s in shards],
        scratch_shapes=[pltpu.SemaphoreType.DMA((n_arr, N_DEV - 1)), pltpu.SemaphoreType.DMA((n_arr, N_DEV - 1)),
                        pltpu.SemaphoreType.DMA((n_arr,))],
    )(*shards)


class _Direct:
    def __init__(self, srcs, bufs, prefix, per_dest, buf_shapes):
        self.srcs, self.bufs, self.prefix, self.per_dest = list(srcs), bufs, tuple(prefix), per_dest
        self.n = len(self.srcs)
        self.out_shape = [jax.ShapeDtypeStruct(s, a.dtype) for s, a in zip(buf_shapes, self.srcs)]
        self.args = self.srcs + (list(bufs) if bufs is not None else [])
        self.scratch = [pltpu.SemaphoreType.DMA((self.n, N_DEV - 1)), pltpu.SemaphoreType.DMA((self.n, N_DEV - 1)),
                        pltpu.SemaphoreType.DMA((self.n,))]

    def aliases(self, in_base, out_base):
        return {} if self.bufs is None else {in_base + self.n + a: out_base + a for a in range(self.n)}

    def copies(self, in_refs, out_refs, sems):
        send_sems, recv_sems, local_sems = sems
        x, y, c = _me()
        me = _flat_id((x, y, c))
        peers = [(x ^ ((rel >> 2) & 1), y ^ ((rel >> 1) & 1), c ^ (rel & 1)) for rel in range(1, N_DEV)]
        src = lambda a, d: in_refs[a].at[d] if self.per_dest else in_refs[a]
        dst = lambda a, s: out_refs[a].at[self.prefix + (s,)]
        local = [pltpu.make_async_copy(src(a, me), dst(a, me), local_sems.at[a]) for a in range(self.n)]
        sends, recvs = [], []
        for a in range(self.n):
            for k, peer in enumerate(peers):
                pid = _flat_id(peer)
                sends.append(_remote(src(a, pid), dst(a, me), send_sems.at[a, k], recv_sems.at[a, k], peer))
                recvs.append(_remote(src(a, pid), dst(a, pid), send_sems.at[a, k], recv_sems.at[a, k], peer))
        return local, sends, recvs

    def start(self, in_refs, out_refs, sems):
        local, sends, _ = self.copies(in_refs, out_refs, sems)
        for cp in local + sends:
            cp.start()

    def wait(self, in_refs, out_refs, sems):
        local, sends, recvs = self.copies(in_refs, out_refs, sems)
        for cp in recvs:
            cp.wait_recv()
        for cp in sends:
            cp.wait_send()
        for cp in local:
            cp.wait()


def _exchange(direct, *, name):
    n, n_in = direct.n, len(direct.args)

    def body(*refs):
        in_refs, out_refs, sems = refs[:n], refs[n_in:n_in + n], refs[n_in + n:]
        direct.start(in_refs, out_refs, sems)
        direct.wait(in_refs, out_refs, sems)

    return pl.pallas_call(
        body, name=name, in_specs=[_ANY] * n_in, out_specs=[_ANY] * n, out_shape=direct.out_shape,
        input_output_aliases=direct.aliases(0, 0), scratch_shapes=direct.scratch,
    )(*direct.args)


def _grad_exchange(contribs, bufs, layer):
    return _Direct(contribs, bufs, (layer,), True, [(DEPTH,) + c.shape for c in contribs])


def _adamw(parts, w, m, v, *, tr, name):
    depth, rows, cols = w.shape
    c1 = 1.0 - ADAM_B1 ** ADAM_STEP
    c2 = 1.0 - ADAM_B2 ** ADAM_STEP

    def body(g_ref, w_ref, m_ref, v_ref, go_ref, d_ref, mo_ref, vo_ref):
        g = g_ref[0, 0].astype(F32)
        for s in range(1, N_DEV):
            g = g + g_ref[0, s].astype(F32)
        m_new = ADAM_B1 * m_ref[0] + (1.0 - ADAM_B1) * g
        v_new = ADAM_B2 * v_ref[0] + (1.0 - ADAM_B2) * (g * g)
        go_ref[0] = g
        mo_ref[0] = m_new
        vo_ref[0] = v_new
        d_ref[0] = -ADAM_LR * ((m_new / c1) / (jnp.sqrt(v_new / c2) + ADAM_EPS) + ADAM_WD * w_ref[0])

    tile = pl.BlockSpec((1, tr, cols), lambda l, i: (l, i, 0))
    return pl.pallas_call(
        body, name=name, grid=(depth, rows // tr),
        in_specs=[pl.BlockSpec((1, N_DEV, tr, cols), lambda l, i: (l, 0, i, 0)), tile, tile, tile],
        out_specs=[tile] * 4, out_shape=[jax.ShapeDtypeStruct(w.shape, F32)] * 4,
        compiler_params=_cparams(("parallel", "parallel")),
    )(parts, w, m, v)


def _pack_small(conv, small):
    lead = conv.shape[:-2]
    flat = jnp.concatenate([conv.reshape(lead + (CS_CONV,))] + list(small), axis=-1)
    pad = CS_ROWS * LANE - flat.shape[-1]
    flat = jnp.concatenate([flat, jnp.zeros(lead + (pad,), F32)], axis=-1)
    return flat.reshape(lead + (CS_ROWS, LANE))


def _unpack_small(p):
    flat = p.reshape(DEPTH, CS_ROWS * LANE)
    conv = flat[:, :CS_CONV].reshape(DEPTH, CONV_K, CONV_SHARD_COLS)
    small, off = [], CS_CONV
    for _, n in SMALL_SIZES:
        small.append(flat[:, off:off + n])
        off += n
    return conv, small


def kernel(x, w_in, conv_w, a_log, dt_bias, norm_w, sinks, w_out, ln_g, ln_b, loss_target, m_w_in, m_conv_w, m_a_log, m_dt_bias, m_norm_w, m_sinks, m_w_out, m_ln_g, m_ln_b, v_w_in, v_conv_w, v_a_log, v_dt_bias, v_norm_w, v_sinks, v_w_out, v_ln_g, v_ln_b):
    small = [a_log, dt_bias, norm_w, sinks, ln_g, ln_b]
    shards = [[w_in[l].astype(BF16), w_out[l].astype(BF16), conv_w[l]] for l in range(DEPTH)]
    weights0 = _unpack_weights(*_all_gather(shards[0], name="weights_all_gather_0"))

    y, saved, weights = _forward(x[0], [weights0], shards[1], small)
    dx, loss_lanes = _loss_head(y, loss_target[0], name="loss_head")
    loss = lax.psum(0.5 * jnp.sum(loss_lanes) * (1.0 / D_MODEL), ("x", "y", "c"))

    def contributions(g):
        c_in = g["w_in"].reshape(D_MODEL, N_DEV, SHARD_COLS).transpose(1, 0, 2).astype(BF16)
        c_out = g["w_out"].astype(BF16).reshape(N_DEV, OUT_SHARD_ROWS, D_MODEL)
        c_conv = g["conv_w"].reshape(CONV_K, N_DEV, CONV_SHARD_COLS).transpose(1, 0, 2)
        c_small = [jnp.broadcast_to(g[n][None], (N_DEV,) + g[n].shape) for n, _ in SMALL_SIZES]
        return [c_in, c_out, _pack_small(c_conv, c_small)]

    dx, g1, _ = _backward_layer(1, dx, saved[1], weights[1], ln_g[1])
    dx, g0, bufs = _backward_layer(0, dx, saved[0], weights[0], ln_g[0],
                                   rider=_grad_exchange(contributions(g1), None, 1))
    bufs = _exchange(_grad_exchange(contributions(g0), bufs, 0), name="grad_exchange_0")

    p_in, p_out, p_small = bufs
    o_in = _adamw(p_in, w_in, m_w_in, v_w_in, tr=256, name="adamw_w_in")
    o_out = _adamw(p_out, w_out, m_w_out, v_w_out, tr=OUT_SHARD_ROWS, name="adamw_w_out")
    o_small = _adamw(p_small, _pack_small(conv_w, small),
                     _pack_small(m_conv_w, [m_a_log, m_dt_bias, m_norm_w, m_sinks, m_ln_g, m_ln_b]),
                     _pack_small(v_conv_w, [v_a_log, v_dt_bias, v_norm_w, v_sinks, v_ln_g, v_ln_b]),
                     tr=CS_ROWS, name="adamw_small")
    outs = []
    for k in range(4):
        cv, sm = _unpack_small(o_small[k])
        outs += [o_in[k], cv, sm[0], sm[1], sm[2], sm[3], o_out[k], sm[4], sm[5]]
    return (loss, dx[None], *outs)
```

```python
import functools

import jax
import jax.numpy as jnp
from jax import lax
from jax.experimental import pallas as pl
from jax.experimental.pallas import tpu as pltpu

F32 = jnp.float32
BF16 = jnp.bfloat16
MM_DTYPE = BF16

N_DEV = 8
D_MODEL = 1024
DEPTH = 2
A_HEADS = 4
A_HEAD_DIM = 128
A_WIDTH = 512
CONV_K = 4
CHUNK = 64
SUPER = 256
G_HEADS = 4
B_Q_HEADS = 8
B_KV_HEADS = 2
B_HEAD_DIM = 64
B_GROUP = 4
B_WIDTH = 512
WINDOW = 128
BLOCK = 128
IN_COLS = 3336
SHARD_COLS = IN_COLS // N_DEV
OUT_SHARD_ROWS = D_MODEL // N_DEV
CONV_SHARD_COLS = 3 * A_WIDTH // N_DEV
DEEPNORM_ALPHA = (2 * DEPTH) ** 0.25
LN_EPS = 1e-5
RMS_EPS = 1e-6
L2_EPS = 1e-6
ADAM_LR, ADAM_B1, ADAM_B2, ADAM_EPS, ADAM_WD, ADAM_STEP = 0.001, 0.9, 0.999, 1e-08, 0.01, 10

LANE = 128
L_QB, L_ZB, L_ZA, L_KB, L_VB, L_QKV, L_BA = 0, 512, 1024, 1536, 1664, 1792, 3328
L_SWA = 1792
L_MAIN = 3328
L_COLS = 3456
SMALL_SIZES = (("a_log", 4), ("dt_bias", 4), ("norm_w", 128), ("sinks", 8), ("ln_g", 1024), ("ln_b", 1024))
CS_CONV = CONV_K * CONV_SHARD_COLS
CS_ROWS = 24
VMEM_LIMIT = 48 * 1024 * 1024


def _cparams(sem=None):
    return pltpu.CompilerParams(dimension_semantics=sem, vmem_limit_bytes=VMEM_LIMIT)


def _mm(a, b):
    return jnp.dot(a.astype(MM_DTYPE), b.astype(MM_DTYPE), preferred_element_type=F32)


def _mm_nt(a, b):
    return lax.dot_general(a.astype(MM_DTYPE), b.astype(MM_DTYPE), (((1,), (1,)), ((), ())),
                           preferred_element_type=F32)


def _mm_tn(a, b):
    return lax.dot_general(a.astype(MM_DTYPE), b.astype(MM_DTYPE), (((0,), (0,)), ((), ())),
                           preferred_element_type=F32)


def _split(a):
    hi = a.astype(BF16)
    return hi, (a - hi.astype(F32)).astype(BF16)


def _hp(a2, b2):
    d = lambda p, q: jnp.dot(p, q, preferred_element_type=F32)
    return d(a2[0], b2[0]) + (d(a2[0], b2[1]) + d(a2[1], b2[0]))


def _silu(x):
    return x * jax.nn.sigmoid(x)


def _softplus(x):
    return jnp.maximum(x, 0.0) + jnp.log1p(jnp.exp(-jnp.abs(x)))


_ANY = pl.BlockSpec(memory_space=pl.ANY)


def _matmul(a, b, *, form, tm, tn, tk, name, add=None, add_scale=1.0, extra=None):
    if form == "nn":
        (m, kk), n = a.shape, b.shape[1]
        a_spec = pl.BlockSpec((tm, tk), lambda i, j, k: (i, k))
        b_spec = pl.BlockSpec((tk, tn), lambda i, j, k: (k, j))
        dn = (((1,), (0,)), ((), ()))
    elif form == "nt":
        (m, kk), n = a.shape, b.shape[0]
        a_spec = pl.BlockSpec((tm, tk), lambda i, j, k: (i, k))
        b_spec = pl.BlockSpec((tn, tk), lambda i, j, k: (j, k))
        dn = (((1,), (1,)), ((), ()))
    else:
        (kk, m), n = a.shape, b.shape[1]
        a_spec = pl.BlockSpec((tk, tm), lambda i, j, k: (k, i))
        b_spec = pl.BlockSpec((tk, tn), lambda i, j, k: (k, j))
        dn = (((0,), (0,)), ((), ()))
    assert m % tm == 0 and n % tn == 0 and kk % tk == 0, (name, m, n, kk)
    has_add, has_extra = add is not None, extra is not None

    def body(*refs):
        refs = list(refs)
        a_ref, b_ref = refs[:2]
        o_ref = refs[-1]
        rest = refs[2:-1]
        k = pl.program_id(2)
        p = lax.dot_general(a_ref[...].astype(MM_DTYPE), b_ref[...].astype(MM_DTYPE), dn,
                            preferred_element_type=F32)

        @pl.when(k == 0)
        def _():
            first = p
            pos = 0
            if has_extra:
                first = first + _mm_nt(rest[0][...], rest[1][...])
                pos = 2
            if has_add:
                first = first + add_scale * rest[pos][...]
            o_ref[...] = first

        @pl.when(k > 0)
        def _():
            o_ref[...] += p

    in_specs = [a_spec, b_spec]
    args = [a, b]
    if has_extra:
        a2, b2, idx = extra
        in_specs += [pl.BlockSpec((tm, LANE), lambda i, j, k: (i, 0)),
                     pl.BlockSpec((tn, LANE), lambda i, j, k: (j, idx))]
        args += [a2, b2]
    if has_add:
        in_specs.append(pl.BlockSpec((tm, tn), lambda i, j, k: (i, j)))
        args.append(add)
    return pl.pallas_call(
        body, name=name, grid=(m // tm, n // tn, kk // tk), in_specs=in_specs,
        out_specs=pl.BlockSpec((tm, tn), lambda i, j, k: (i, j)),
        out_shape=jax.ShapeDtypeStruct((m, n), F32),
        compiler_params=_cparams(("parallel", "parallel", "arbitrary")),
    )(*args)


def _shift_down(x, k, row):
    return jnp.where(row >= k, pltpu.roll(x, k, 0), 0.0)


def _shift_up(x, k, row, t_len):
    return jnp.where(row < t_len - k, pltpu.roll(x, t_len - k, 0), 0.0)


def _conv_slab(x, w, row):
    return (w[3:4] * x + w[2:3] * _shift_down(x, 1, row) + w[1:2] * _shift_down(x, 2, row)
            + w[0:1] * _shift_down(x, 3, row))


def _prep_fwd(h, conv_w, *, name):
    t_len = h.shape[0]

    def body(x_ref, w_ref, o_ref):
        s = pl.program_id(0)
        row = lax.broadcasted_iota(jnp.int32, (t_len, LANE), 0)
        y = _silu(_conv_slab(x_ref[...], w_ref[...], row))
        rs = lax.rsqrt(jnp.sum(y * y, axis=-1, keepdims=True) + L2_EPS)
        scale = jnp.where(s < A_HEADS, A_HEAD_DIM ** -0.5, 1.0)
        o_ref[...] = jnp.where(s < 2 * A_HEADS, y * rs * scale, y)

    return pl.pallas_call(
        body, name=name, grid=(12,),
        in_specs=[pl.BlockSpec((t_len, LANE), lambda s: (0, L_QKV // LANE + s)),
                  pl.BlockSpec((8, LANE), lambda s: (0, s))],
        out_specs=pl.BlockSpec((t_len, LANE), lambda s: (0, s)),
        out_shape=jax.ShapeDtypeStruct((t_len, 3 * A_WIDTH), F32),
        compiler_params=_cparams(("parallel",)),
    )(h, conv_w)


def _prep_bwd(h, conv_w, d_out, dh, *, name):
    t_len = h.shape[0]

    def body(x_ref, w_ref, g_ref, dh_in, dx_ref, dw_ref):
        del dh_in
        s = pl.program_id(0)
        row = lax.broadcasted_iota(jnp.int32, (t_len, LANE), 0)
        x = x_ref[...]
        w = w_ref[...]
        c = _conv_slab(x, w, row)
        sg = jax.nn.sigmoid(c)
        y = c * sg
        g = g_ref[0]
        rs = lax.rsqrt(jnp.sum(y * y, axis=-1, keepdims=True) + L2_EPS)
        scale = jnp.where(s < A_HEADS, A_HEAD_DIM ** -0.5, 1.0)
        dy_n = scale * (rs * g - y * (rs * rs * rs) * jnp.sum(g * y, axis=-1, keepdims=True))
        dy = jnp.where(s < 2 * A_HEADS, dy_n, g)
        dc = dy * (sg * (1.0 + c * (1.0 - sg)))
        dx_ref[...] = (w[3:4] * dc + w[2:3] * _shift_up(dc, 1, row, t_len)
                       + w[1:2] * _shift_up(dc, 2, row, t_len) + w[0:1] * _shift_up(dc, 3, row, t_len))
        dws = [jnp.sum(dc * _shift_down(x, 3 - j, row), axis=0, keepdims=True) if j < 3
               else jnp.sum(dc * x, axis=0, keepdims=True) for j in range(CONV_K)]
        dw_ref[...] = jnp.concatenate(dws + [jnp.zeros((8 - CONV_K, LANE), F32)], axis=0)

    slab = pl.BlockSpec((t_len, LANE), lambda s: (0, L_QKV // LANE + s))
    return pl.pallas_call(
        body, name=name, grid=(12,),
        in_specs=[slab, pl.BlockSpec((8, LANE), lambda s: (0, s)),
                  pl.BlockSpec((1, t_len, LANE), lambda s: (s // A_HEADS, 0, s % A_HEADS)), _ANY],
        out_specs=[slab, pl.BlockSpec((8, LANE), lambda s: (0, s))],
        out_shape=[jax.ShapeDtypeStruct((t_len, L_MAIN), F32), jax.ShapeDtypeStruct((8, 3 * A_WIDTH), F32)],
        input_output_aliases={3: 0},
        compiler_params=_cparams(("parallel",)),
    )(h, conv_w, d_out, dh)


MF_TRIL, MF_STRIL, MF_DIAG8, MF_LOW16, MF_EYE = 0, 1, 2, 3, 6
MB_CUM, MB_CUM_T, MB_TOT = 0, 1, 2


def _gdn_masks():
    r = lax.broadcasted_iota(jnp.int32, (SUPER, SUPER), 0)
    c = lax.broadcasted_iota(jnp.int32, (SUPER, SUPER), 1)
    same = lambda shift: (r >> shift) == (c >> shift)
    chunk = same(6)
    ninf = lambda m: jnp.where(m, 0.0, -jnp.inf).astype(F32)
    one = lambda m: m.astype(F32)
    mf = jnp.stack([ninf(chunk & (r >= c)), ninf(chunk & (r > c)), one(same(3))]
                   + [one(same(sh) & jnp.logical_not(same(sh - 1))) for sh in (4, 5, 6)] + [one(r == c)])
    mb = jnp.stack([one(chunk & (r >= c)), one(chunk & (r <= c)), one(chunk)]).astype(BF16)
    return mf, mb


def _tri_inv_impl(a, mf):
    d = lambda p, q: jnp.dot(p.astype(BF16), q.astype(BF16), preferred_element_type=F32)
    eye = mf[MF_EYE]
    a0 = a * mf[MF_DIAG8]
    a2 = d(a0, a0)
    a4 = d(a2, a2)
    t = d(d(eye - a0, eye + a2), eye + a4)
    for level in range(3):
        t = t - d(d(t, a * mf[MF_LOW16 + level]), t)
    t0 = t.astype(BF16)
    a_hi, a_lo = _split(a)
    dd = lambda p, q: jnp.dot(p, q, preferred_element_type=F32)
    t0f = t0.astype(F32)
    resid = (eye - t0f) - (dd(a_hi, t0) + dd(a_lo, t0))
    r_hi, r_lo = _split(resid)
    return t0f + (dd(t0, r_hi) + dd(t0, r_lo))


def _tri_inv_cotangent(t, dt):
    tts = _split(t.T)
    return -_hp(_split(_hp(tts, _split(dt))), tts)


@jax.custom_vjp
def _tri_inv(a, mf):
    return _tri_inv_impl(a, mf)


def _tri_inv_fwd(a, mf):
    t = _tri_inv_impl(a, mf)
    return t, (t, mf)


_tri_inv.defvjp(_tri_inv_fwd, lambda res, dt: (_tri_inv_cotangent(res[0], dt), jnp.zeros_like(res[1])))


@jax.custom_vjp
def _tri_inv_known(a, t):
    return t


def _tri_inv_known_fwd(a, t):
    return t, t


def _tri_inv_known_bwd(t, dt):
    return _tri_inv_cotangent(t, dt), jnp.zeros_like(t)


_tri_inv_known.defvjp(_tri_inv_known_fwd, _tri_inv_known_bwd)


@functools.partial(jax.custom_vjp, nondiff_argnums=(1,))
def _lane_roll(x, shift):
    return pltpu.roll(x, shift % LANE, 1)


_lane_roll.defvjp(lambda x, shift: (_lane_roll(x, shift), None), lambda shift, _, g: (_lane_roll(g, -shift),))


def _mask_times_lanes(x, mask):
    lane = lax.broadcasted_iota(jnp.int32, (1, LANE), 1)
    x = jnp.where(lane < A_HEADS, x, 0.0)
    x1 = x.astype(BF16).astype(F32)
    x2 = (x - x1).astype(BF16).astype(F32)
    x3 = (x - x1 - x2).astype(BF16).astype(F32)
    pieces = x1 + pltpu.roll(x2, A_HEADS, 1) + pltpu.roll(x3, 2 * A_HEADS, 1)
    res = jnp.dot(mask, pieces.astype(BF16), preferred_element_type=F32)
    return res + pltpu.roll(res, LANE - A_HEADS, 1) + pltpu.roll(res, LANE - 2 * A_HEADS, 1)


@jax.custom_vjp
def _chunk_sums(g, mb):
    return _mask_times_lanes(g, mb[MB_CUM]), _mask_times_lanes(g, mb[MB_TOT])


def _chunk_sums_fwd(g, mb):
    return _chunk_sums(g, mb), mb


def _chunk_sums_bwd(mb, d):
    lane = lax.broadcasted_iota(jnp.int32, (1, LANE), 1)
    dg = _mask_times_lanes(d[0], mb[MB_CUM_T]) + _mask_times_lanes(d[1], mb[MB_TOT])
    return jnp.where(lane < A_HEADS, dg, 0.0), jnp.zeros_like(mb)


_chunk_sums.defvjp(_chunk_sums_fwd, _chunk_sums_bwd)


def _gdn_gates(ba, alog, dtb, mb):
    beta = jax.nn.sigmoid(ba)
    g = -jnp.exp(alog) * _softplus(_lane_roll(ba, -A_HEADS) + dtb)
    gc, gl = _chunk_sums(g, mb)
    return beta, gc, gl, gc.T


def _gdn_block(s, q, k, v, z, gates, nw, h, t_known, mf):
    n = q.shape[0]
    beta_all, gc_all, gl_all, gct_all = gates
    lane = lax.broadcasted_iota(jnp.int32, (1, LANE), 1)
    sub = lax.broadcasted_iota(jnp.int32, (LANE, 1), 0)
    col = lambda x: jnp.sum(jnp.where(lane == h, x, 0.0), axis=1, keepdims=True)
    wide = lambda c: jnp.broadcast_to(c, (n, LANE))
    gc, gl = col(gc_all), col(gl_all)
    gc_row = jnp.sum(jnp.where(sub == h, gct_all, 0.0), axis=0, keepdims=True)
    beta_w, eg_w = wide(col(beta_all)), wide(jnp.exp(gc))
    diff = gc - gc_row
    decay = jnp.exp(diff + mf[MF_TRIL])
    kb = k * beta_w
    a_mat = _mm_nt(kb, k) * jnp.exp(diff + mf[MF_STRIL])
    t_mat = _tri_inv(a_mat, mf) if t_known is None else _tri_inv_known(a_mat, t_known)
    u = _mm(t_mat, v * beta_w)
    w = _mm(t_mat, kb * eg_w)
    qk = _mm_nt(q, k) * decay
    q_dec = q * eg_w
    k_dec = k * wide(jnp.exp(gl - gc))
    g_tot = jnp.exp(gl)
    outs = []
    for ci in range(n // CHUNK):
        lo, hi = ci * CHUNK, (ci + 1) * CHUNK
        v_new = u[lo:hi] - _mm(w[lo:hi], s)
        pieces = []
        if lo:
            pieces.append(jnp.zeros((lo, LANE), F32))
        pieces.append(v_new)
        if n - hi:
            pieces.append(jnp.zeros((n - hi, LANE), F32))
        v_pad = jnp.concatenate(pieces, axis=0) if len(pieces) > 1 else v_new
        outs.append(_mm(q_dec[lo:hi], s) + _mm(qk[lo:hi], v_pad))
        s = s * g_tot[lo:lo + 1] + _mm_tn(k_dec[lo:hi], v_new)
    o = jnp.concatenate(outs, axis=0)
    o = o * lax.rsqrt(jnp.mean(o * o, axis=-1, keepdims=True) + RMS_EPS) * nw
    return o * _silu(z), s, t_mat


def _gdn_fwd(qkv, h, alog, dtb, nw, ycat, *, name, rider=None):
    t_len = qkv.shape[0]
    nsc = t_len // SUPER

    n_rin, n_rout = (len(rider.args), rider.n) if rider else (0, 0)

    def body(*refs):
        q_ref, k_ref, v_ref, z_ref, ba_ref, al_ref, dt_ref, nw_ref, mf_ref, mb_ref, _ = refs[:11]
        y_ref, sin_ref, t_ref = refs[11 + n_rin:14 + n_rin]
        s_scr = refs[14 + n_rin + n_rout]
        sc, pg = pl.program_id(0), pl.program_id(1)
        heads = [pg * G_HEADS + hh for hh in range(G_HEADS)]
        if rider:
            r_refs = (refs[11:11 + rider.n], refs[14 + n_rin:14 + n_rin + n_rout], refs[15 + n_rin + n_rout:])
            pl.when((sc == 0) & (pg == 0))(lambda: rider.start(*r_refs))

        @pl.when(sc == 0)
        def _():
            for hd in heads:
                s_scr[hd] = jnp.zeros((A_HEAD_DIM, A_HEAD_DIM), F32)

        per_head = lambda ref: jnp.stack([ref[:, hh * LANE:(hh + 1) * LANE] for hh in range(G_HEADS)])
        states = jnp.stack([s_scr[hd] for hd in heads])
        gates = _gdn_gates(ba_ref[...], al_ref[...], dt_ref[...], mb_ref[...])
        fn = jax.vmap(_gdn_block, in_axes=(0, 0, 0, 0, 0, None, None, 0, None, None))
        y, s_new, t_mat = fn(states, per_head(q_ref), per_head(k_ref), per_head(v_ref), per_head(z_ref),
                             gates, nw_ref[...], jnp.stack(heads), None, mf_ref[...])
        sin_ref[0] = states
        t_ref[0] = t_mat
        for hh, hd in enumerate(heads):
            y_ref[:, hh * LANE:(hh + 1) * LANE] = y[hh]
            s_scr[hd] = s_new[hh]
        if rider:
            pl.when((sc == nsc - 1) & (pg == A_HEADS // G_HEADS - 1))(lambda: rider.wait(*r_refs))

    gw = G_HEADS * LANE
    blk = lambda off: pl.BlockSpec((SUPER, gw), lambda sc, pg: (sc, off // G_HEADS + pg))
    row = pl.BlockSpec((1, LANE), lambda sc, pg: (0, 0))
    mf, mb = _gdn_masks()
    whole = lambda a: pl.BlockSpec(a.shape, lambda sc, pg: (0, 0, 0))
    aliases = {10: 0}
    if rider:
        aliases.update(rider.aliases(11, 3))
    return pl.pallas_call(
        body, name=name, grid=(nsc, A_HEADS // G_HEADS),
        in_specs=[blk(0), blk(4), blk(8), blk(L_ZA // LANE),
                  pl.BlockSpec((SUPER, LANE), lambda sc, pg: (sc, L_BA // LANE)), row, row, row,
                  whole(mf), whole(mb), _ANY] + [_ANY] * n_rin,
        out_specs=[blk(0),
                   pl.BlockSpec((1, G_HEADS, A_HEAD_DIM, A_HEAD_DIM), lambda sc, pg: (sc, pg, 0, 0)),
                   pl.BlockSpec((1, G_HEADS, SUPER, SUPER), lambda sc, pg: (sc, pg, 0, 0))] + [_ANY] * n_rout,
        out_shape=[jax.ShapeDtypeStruct((t_len, D_MODEL), F32),
                   jax.ShapeDtypeStruct((nsc, A_HEADS, A_HEAD_DIM, A_HEAD_DIM), F32),
                   jax.ShapeDtypeStruct((nsc, A_HEADS, SUPER, SUPER), F32)] + (rider.out_shape if rider else []),
        scratch_shapes=[pltpu.VMEM((A_HEADS, A_HEAD_DIM, A_HEAD_DIM), F32)] + (rider.scratch if rider else []),
        input_output_aliases=aliases,
        compiler_params=_cparams(("arbitrary", "arbitrary")),
    )(qkv, qkv, qkv, h, h, alog, dtb, nw, mf, mb, ycat, *(rider.args if rider else []))


def _gdn_bwd(qkv, h, alog, dtb, nw, s_in, t_in, dycat, dh, *, name, rider=None):
    t_len = qkv.shape[0]
    nsc = t_len // SUPER
    n_rin, n_rout = (len(rider.args), rider.n) if rider else (0, 0)

    def body(*refs):
        (q_ref, k_ref, v_ref, z_ref, ba_ref, al_ref, dt_ref, nw_ref, sin_ref, t_ref, dy_ref, mf_ref, mb_ref,
         _) = refs[:14]
        dz_ref, dqkv_ref, dba_ref, dal_ref, ddt_ref, dnw_ref = refs[14 + n_rin:20 + n_rin]
        ds_scr = refs[20 + n_rin + n_rout]
        i, pg = pl.program_id(0), pl.program_id(1)
        if rider:
            r_refs = (refs[14:14 + rider.n], refs[20 + n_rin:20 + n_rin + n_rout], refs[21 + n_rin + n_rout:])
            pl.when((i == 0) & (pg == 0))(lambda: rider.start(*r_refs))

        @pl.when((i == 0) & (pg == 0))
        def _():
            dal_ref[...] = jnp.zeros_like(dal_ref)
            ddt_ref[...] = jnp.zeros_like(ddt_ref)
            dnw_ref[...] = jnp.zeros_like(dnw_ref)

        @pl.when(pg == 0)
        def _():
            dba_ref[...] = jnp.zeros_like(dba_ref)

        heads = [pg * G_HEADS + hh for hh in range(G_HEADS)]

        @pl.when(i == 0)
        def _():
            for hd in heads:
                ds_scr[hd] = jnp.zeros((A_HEAD_DIM, A_HEAD_DIM), F32)

        per_head = lambda ref: jnp.stack([ref[:, hh * LANE:(hh + 1) * LANE] for hh in range(G_HEADS)])
        d_states = jnp.stack([ds_scr[hd] for hd in heads])
        head_ids = jnp.stack(heads)
        t_known, mf, mb = t_ref[0], mf_ref[...], mb_ref[...]

        def fn(s, q, k, v, z, ba, alog, dtb, nw):
            gates = _gdn_gates(ba, alog, dtb, mb)
            one = lambda s, q, k, v, z, t, h: _gdn_block(s, q, k, v, z, gates, nw, h, t, mf)[:2]
            return jax.vmap(one)(s, q, k, v, z, t_known, head_ids)

        _, vjp = jax.vjp(fn, sin_ref[0], per_head(q_ref), per_head(k_ref), per_head(v_ref), per_head(z_ref),
                         ba_ref[...], al_ref[...], dt_ref[...], nw_ref[...])
        ds, dq, dk, dv, dz, dba, dal, ddt, dnw = vjp((per_head(dy_ref), d_states))
        for hh, hd in enumerate(heads):
            cols = slice(hh * LANE, (hh + 1) * LANE)
            ds_scr[hd] = ds[hh]
            dqkv_ref[0, :, cols] = dq[hh]
            dqkv_ref[1, :, cols] = dk[hh]
            dqkv_ref[2, :, cols] = dv[hh]
            dz_ref[:, cols] = dz[hh]
        dba_ref[...] += dba
        dal_ref[...] += dal
        ddt_ref[...] += ddt
        dnw_ref[...] += dnw
        if rider:
            pl.when((i == nsc - 1) & (pg == A_HEADS // G_HEADS - 1))(lambda: rider.wait(*r_refs))

    rev = lambda i: nsc - 1 - i
    gw = G_HEADS * LANE
    blk = lambda off: pl.BlockSpec((SUPER, gw), lambda i, pg: (rev(i), off // G_HEADS + pg))
    row = pl.BlockSpec((1, LANE), lambda i, pg: (0, 0))
    ba_blk = lambda off: pl.BlockSpec((SUPER, LANE), lambda i, pg: (rev(i), off))
    mf, mb = _gdn_masks()
    whole = lambda a: pl.BlockSpec(a.shape, lambda i, pg: (0, 0, 0))
    aliases = {13: 0}
    if rider:
        aliases.update(rider.aliases(14, 6))
    return pl.pallas_call(
        body, name=name, grid=(nsc, A_HEADS // G_HEADS),
        in_specs=[blk(0), blk(4), blk(8), blk(L_ZA // LANE), ba_blk(L_BA // LANE), row, row, row,
                  pl.BlockSpec((1, G_HEADS, A_HEAD_DIM, A_HEAD_DIM), lambda i, pg: (rev(i), pg, 0, 0)),
                  pl.BlockSpec((1, G_HEADS, SUPER, SUPER), lambda i, pg: (rev(i), pg, 0, 0)),
                  blk(0), whole(mf), whole(mb), _ANY] + [_ANY] * n_rin,
        out_specs=[blk(L_ZA // LANE),
                   pl.BlockSpec((3, SUPER, gw), lambda i, pg: (0, rev(i), pg)),
                   ba_blk(0), row, row, row] + [_ANY] * n_rout,
        out_shape=[jax.ShapeDtypeStruct((t_len, L_MAIN), F32), jax.ShapeDtypeStruct((3, t_len, A_WIDTH), F32),
                   jax.ShapeDtypeStruct((t_len, LANE), F32)] + [jax.ShapeDtypeStruct((1, LANE), F32)] * 3
        + (rider.out_shape if rider else []),
        scratch_shapes=[pltpu.VMEM((A_HEADS, A_HEAD_DIM, A_HEAD_DIM), F32)] + (rider.scratch if rider else []),
        input_output_aliases=aliases,
        compiler_params=_cparams(("arbitrary", "arbitrary")),
    )(qkv, qkv, qkv, h, h, alog, dtb, nw, s_in, t_in, dycat, mf, mb, dh, *(rider.args if rider else []))


def _swa_block(q, kp, kc, vp, vc, z, sinks, first):
    rows = B_GROUP * BLOCK
    ri = lax.broadcasted_iota(jnp.int32, (rows, 2 * BLOCK), 0)
    si = lax.broadcasted_iota(jnp.int32, (rows, 2 * BLOCK), 1)
    dist = (ri & (BLOCK - 1)) + BLOCK - si
    bias = jnp.where((dist >= 0) & (dist < WINDOW) & ((si >= BLOCK) | jnp.logical_not(first)), 0.0, -jnp.inf)
    dist_f = dist.astype(F32)
    head_of_row = lax.broadcasted_iota(jnp.int32, (rows, 1), 0) >> 7

    def group(j):
        cs = slice(j * B_HEAD_DIM, (j + 1) * B_HEAD_DIM)
        heads = range(j * B_GROUP, (j + 1) * B_GROUP)
        qs = jnp.concatenate([q[:, hq * B_HEAD_DIM:(hq + 1) * B_HEAD_DIM] for hq in heads], axis=0)
        kk = jnp.concatenate([kp[:, cs], kc[:, cs]], axis=0)
        vv = jnp.concatenate([vp[:, cs], vc[:, cs]], axis=0)
        sink = jnp.concatenate([jnp.broadcast_to(sinks[:, hq:hq + 1], (BLOCK, 1)) for hq in heads], axis=0)
        slope = sum(jnp.where(head_of_row == gi, 2.0 ** (-8.0 * (hq + 1) / B_Q_HEADS), 0.0)
                    for gi, hq in enumerate(heads))
        return qs, kk, vv, sink, slope

    def attend(qs, kk, vv, sink, slope):
        sc = _mm_nt(qs, kk) * (B_HEAD_DIM ** -0.5) - slope * dist_f + bias
        m = lax.stop_gradient(jnp.maximum(jnp.max(sc, axis=-1, keepdims=True), sink))
        p = jnp.exp(sc - m)
        inv = 1.0 / (jnp.sum(p, axis=-1, keepdims=True) + jnp.exp(sink - m))
        return _mm(p * inv, vv)

    o = jax.vmap(attend)(*[jnp.stack(t) for t in zip(*[group(j) for j in range(B_KV_HEADS)])])
    outs = [o[j, gi * BLOCK:(gi + 1) * BLOCK] for j in range(B_KV_HEADS) for gi in range(B_GROUP)]
    return jnp.concatenate(outs, axis=1) * _silu(z)


def _swa_specs(idx):
    wide = lambda off: pl.BlockSpec((BLOCK, B_WIDTH), lambda n: (idx(n), off))
    cur = lambda off: pl.BlockSpec((BLOCK, LANE), lambda n: (idx(n), off))
    prev = lambda off: pl.BlockSpec((BLOCK, LANE), lambda n: (jnp.maximum(idx(n) - 1, 0), off))
    return [wide(L_QB // B_WIDTH), prev(L_KB // LANE), cur(L_KB // LANE), prev(L_VB // LANE), cur(L_VB // LANE),
            wide(L_ZB // B_WIDTH), pl.BlockSpec((1, LANE), lambda n: (0, 0))]


def _swa_fwd(h, sinks, *, name):
    t_len = h.shape[0]
    nb = t_len // BLOCK

    def body(q_ref, kp_ref, kc_ref, vp_ref, vc_ref, z_ref, s_ref, o_ref):
        o_ref[...] = _swa_block(q_ref[...], kp_ref[...], kc_ref[...], vp_ref[...], vc_ref[...], z_ref[...],
                                s_ref[...], pl.program_id(0) == 0)

    return pl.pallas_call(
        body, name=name, grid=(nb,), in_specs=_swa_specs(lambda n: n),
        out_specs=pl.BlockSpec((BLOCK, B_WIDTH), lambda n: (n, 1)),
        out_shape=jax.ShapeDtypeStruct((t_len, D_MODEL), F32),
        compiler_params=_cparams(("parallel",)),
    )(h, h, h, h, h, h, sinks)


def _swa_bwd(h, sinks, dycat, *, name):
    t_len = h.shape[0]
    nb = t_len // BLOCK

    def body(q_ref, kp_ref, kc_ref, vp_ref, vc_ref, z_ref, s_ref, dy_ref, dh_ref, dsk_ref, ck_scr, cv_scr):
        i = pl.program_id(0)
        n = nb - 1 - i

        @pl.when(i == 0)
        def _():
            ck_scr[...] = jnp.zeros_like(ck_scr)
            cv_scr[...] = jnp.zeros_like(cv_scr)
            dsk_ref[...] = jnp.zeros_like(dsk_ref)

        fn = functools.partial(_swa_block, first=(n == 0))
        _, vjp = jax.vjp(fn, q_ref[...], kp_ref[...], kc_ref[...], vp_ref[...], vc_ref[...], z_ref[...], s_ref[...])
        dq, dkp, dkc, dvp, dvc, dz, dsk = vjp(dy_ref[...])
        dh_ref[:, L_QB:L_QB + B_WIDTH] = dq
        dh_ref[:, L_ZB:L_ZB + B_WIDTH] = dz
        dh_ref[:, L_ZA:L_ZA + A_WIDTH] = jnp.zeros((BLOCK, A_WIDTH), F32)
        dh_ref[:, L_KB:L_KB + LANE] = dkc + ck_scr[...]
        dh_ref[:, L_VB:L_VB + LANE] = dvc + cv_scr[...]
        ck_scr[...] = dkp
        cv_scr[...] = dvp
        dsk_ref[...] += dsk

    rev = lambda i: nb - 1 - i
    return pl.pallas_call(
        body, name=name, grid=(nb,),
        in_specs=_swa_specs(rev) + [pl.BlockSpec((BLOCK, B_WIDTH), lambda i: (rev(i), 1))],
        out_specs=[pl.BlockSpec((BLOCK, L_SWA), lambda i: (rev(i), 0)), pl.BlockSpec((1, LANE), lambda i: (0, 0))],
        out_shape=[jax.ShapeDtypeStruct((t_len, L_MAIN), F32), jax.ShapeDtypeStruct((1, LANE), F32)],
        scratch_shapes=[pltpu.VMEM((BLOCK, LANE), F32), pltpu.VMEM((BLOCK, LANE), F32)],
        compiler_params=_cparams(("arbitrary",)),
    )(h, h, h, h, h, h, sinks, dycat)


def _out_ln_fwd(ycat, w_out, x, ln_g, ln_b, *, name, tm=256):
    t_len = x.shape[0]

    def body(y_ref, w_ref, x_ref, g_ref, b_ref, r_ref, o_ref):
        r = DEEPNORM_ALPHA * x_ref[...] + _mm(y_ref[...], w_ref[...])
        r_ref[...] = r
        mu = jnp.mean(r, axis=-1, keepdims=True)
        d = r - mu
        var = jnp.mean(d * d, axis=-1, keepdims=True)
        o_ref[...] = d * lax.rsqrt(var + LN_EPS) * g_ref[...] + b_ref[...]

    tile = pl.BlockSpec((tm, D_MODEL), lambda i: (i, 0))
    vec = pl.BlockSpec((1, D_MODEL), lambda i: (0, 0))
    return pl.pallas_call(
        body, name=name, grid=(t_len // tm,),
        in_specs=[tile, pl.BlockSpec((D_MODEL, D_MODEL), lambda i: (0, 0)), tile, vec, vec],
        out_specs=[tile, tile],
        out_shape=[jax.ShapeDtypeStruct((t_len, D_MODEL), F32)] * 2,
        compiler_params=_cparams(("parallel",)),
    )(ycat, w_out, x, ln_g, ln_b)


def _ln_bwd(dxn, r, ln_g, *, name, tm=256):
    t_len = r.shape[0]

    def body(dx_ref, r_ref, g_ref, dr_ref, dg_ref, db_ref):
        @pl.when(pl.program_id(0) == 0)
        def _():
            dg_ref[...] = jnp.zeros_like(dg_ref)
            db_ref[...] = jnp.zeros_like(db_ref)

        rr = r_ref[...]
        dx = dx_ref[...]
        mu = jnp.mean(rr, axis=-1, keepdims=True)
        d = rr - mu
        rstd = lax.rsqrt(jnp.mean(d * d, axis=-1, keepdims=True) + LN_EPS)
        xh = d * rstd
        dxh = dx * g_ref[...]
        dr_ref[...] = rstd * (dxh - jnp.mean(dxh, axis=-1, keepdims=True)
                              - xh * jnp.mean(dxh * xh, axis=-1, keepdims=True))
        dg_ref[...] += jnp.sum(dx * xh, axis=0, keepdims=True)
        db_ref[...] += jnp.sum(dx, axis=0, keepdims=True)

    tile = pl.BlockSpec((tm, D_MODEL), lambda i: (i, 0))
    vec = pl.BlockSpec((1, D_MODEL), lambda i: (0, 0))
    return pl.pallas_call(
        body, name=name, grid=(t_len // tm,),
        in_specs=[tile, tile, vec], out_specs=[tile, vec, vec],
        out_shape=[jax.ShapeDtypeStruct((t_len, D_MODEL), F32), jax.ShapeDtypeStruct((1, D_MODEL), F32),
                   jax.ShapeDtypeStruct((1, D_MODEL), F32)],
        compiler_params=_cparams(("arbitrary",)),
    )(dxn, r, ln_g)


def _loss_head(y, target, *, name, tm=256):
    t_len = y.shape[0]

    def body(y_ref, t_ref, d_ref, l_ref):
        @pl.when(pl.program_id(0) == 0)
        def _():
            l_ref[...] = jnp.zeros_like(l_ref)

        e = y_ref[...] - t_ref[...]
        d_ref[...] = e * (1.0 / D_MODEL)
        l_ref[...] += jnp.sum(e * e, axis=0, keepdims=True)

    tile = pl.BlockSpec((tm, D_MODEL), lambda i: (i, 0))
    vec = pl.BlockSpec((1, D_MODEL), lambda i: (0, 0))
    return pl.pallas_call(
        body, name=name, grid=(t_len // tm,), in_specs=[tile, tile], out_specs=[tile, vec],
        out_shape=[jax.ShapeDtypeStruct((t_len, D_MODEL), F32), jax.ShapeDtypeStruct((1, D_MODEL), F32)],
        compiler_params=_cparams(("arbitrary",)),
    )(y, target)


def _pad_row(v):
    return jnp.zeros((1, LANE), F32).at[0, :v.shape[0]].set(v)


def _to_layout(w_full):
    s = lambda a, b: w_full[..., a:b]
    pad = jnp.zeros(w_full.shape[:-1] + (LANE - 2 * A_HEADS,), w_full.dtype)
    return jnp.concatenate([s(2056, 2568), s(2824, 3336), s(1536, 2048), s(2568, 2696), s(2696, 2824), s(0, 1536),
                            s(2048, 2056), pad], axis=-1)


def _from_layout(g_main, g_ba):
    s = lambda a, b: g_main[..., a:b]
    return jnp.concatenate([s(L_QKV, L_QKV + 1536), s(L_ZA, L_ZA + 512), g_ba[..., :2 * A_HEADS],
                            s(L_QB, L_QB + 512), s(L_KB, L_KB + 128), s(L_VB, L_VB + 128), s(L_ZB, L_ZB + 512)],
                           axis=-1)


def _unpack_weights(g_in, g_out, g_conv):
    w_in_l = _to_layout(g_in.transpose(1, 0, 2).reshape(D_MODEL, IN_COLS))
    conv_l = jnp.pad(g_conv.transpose(1, 0, 2).reshape(CONV_K, 3 * A_WIDTH), ((0, 8 - CONV_K), (0, 0)))
    return w_in_l, g_out.reshape(D_MODEL, D_MODEL), conv_l


def _forward(x, weights, shards1, small):
    a_log, dt_bias, norm_w, sinks, ln_g, ln_b = small
    tm = min(512, x.shape[0])
    saved, weights = [], list(weights)
    for l in range(DEPTH):
        w_in_l, w_out_l, conv_l = weights[l]
        h = _matmul(x, w_in_l, form="nn", tm=tm, tn=1152, tk=D_MODEL, name=f"in_proj_{l}")
        qkv = _prep_fwd(h, conv_l, name=f"prep_fwd_{l}")
        al, dt, nw, sk = _pad_row(a_log[l]), _pad_row(dt_bias[l]), norm_w[l][None, :], _pad_row(sinks[l])
        ycat = _swa_fwd(h, sk, name=f"swa_fwd_{l}")
        rider = None
        if l == 0 and len(weights) == 1:
            rider = _Direct(shards1, None, (), False, [(N_DEV,) + s.shape for s in shards1])
        ycat, s_in, t_in, *gathered = _gdn_fwd(qkv, h, al, dt, nw, ycat, name=f"gdn_fwd_{l}", rider=rider)
        if rider:
            weights.append(_unpack_weights(*gathered))
        r, xn = _out_ln_fwd(ycat, w_out_l, x, ln_g[l][None, :], ln_b[l][None, :], name=f"out_ln_{l}")
        saved.append((x, h, qkv, s_in, t_in, ycat, r, al, dt, nw, sk))
        x = xn
    return x, saved, weights


def _backward_layer(l, dx, saved_l, weights_l, ln_g_l, rider=None):
    x_in, h, qkv, s_in, t_in, ycat, r, al, dt, nw, sk = saved_l
    w_in_l, w_out_l, conv_l = weights_l
    tm = min(512, x_in.shape[0])
    dr, d_lng, d_lnb = _ln_bwd(dx, r, ln_g_l[None, :], name=f"ln_bwd_{l}")
    dycat = _matmul(dr, w_out_l, form="nt", tm=tm, tn=D_MODEL, tk=D_MODEL, name=f"out_proj_dx_{l}")
    d_wout = _matmul(ycat, dr, form="tn", tm=512, tn=D_MODEL, tk=tm, name=f"out_proj_dw_{l}")
    dh, d_sk = _swa_bwd(h, sk, dycat, name=f"swa_bwd_{l}")
    dh, dqkv_n, dba, d_al, d_dt, d_nw, *rider_bufs = _gdn_bwd(qkv, h, al, dt, nw, s_in, t_in, dycat, dh,
                                                              name=f"gdn_bwd_{l}", rider=rider)
    dh, d_conv = _prep_bwd(h, conv_l, dqkv_n, dh, name=f"prep_bwd_{l}")
    d_win_main = _matmul(x_in, dh, form="tn", tm=512, tn=L_MAIN // 2, tk=tm, name=f"in_proj_dw_{l}")
    d_win_ba = _matmul(x_in, dba, form="tn", tm=D_MODEL, tn=LANE, tk=tm, name=f"in_proj_dw_ba_{l}")
    dx = _matmul(dh, w_in_l, form="nt", tm=tm, tn=D_MODEL, tk=L_MAIN // 2, name=f"in_proj_dx_{l}",
                 add=dr, add_scale=DEEPNORM_ALPHA, extra=(dba, w_in_l, L_BA // LANE))
    grads = dict(w_in=_from_layout(d_win_main, d_win_ba), w_out=d_wout, conv_w=d_conv[:CONV_K],
                 a_log=d_al[0, :A_HEADS], dt_bias=d_dt[0, :A_HEADS], norm_w=d_nw[0], sinks=d_sk[0, :B_Q_HEADS],
                 ln_g=d_lng[0], ln_b=d_lnb[0])
    return dx, grads, rider_bufs


def _me():
    return lax.axis_index("x"), lax.axis_index("y"), lax.axis_index("c")


def _flat_id(pos):
    return 4 * pos[0] + 2 * pos[1] + pos[2]


def _remote(src, dst, send_sem, recv_sem, to):
    return pltpu.make_async_remote_copy(src_ref=src, dst_ref=dst, send_sem=send_sem, recv_sem=recv_sem,
                                        device_id=to, device_id_type=pl.DeviceIdType.MESH)


def _all_gather(shards, *, name):
    n_arr = len(shards)

    def body(*refs):
        x_refs, out_refs = refs[:n_arr], refs[n_arr:2 * n_arr]
        send_sems, recv_sems, local_sems = refs[2 * n_arr:]
        x, y, c = _me()
        me, sibling = (x, y, c), (x, y, 1 - c)
        chips = [(1 - x, y), (x, 1 - y), (1 - x, 1 - y)]

        def copy(a, k, block, to, src=None):
            dst = out_refs[a].at[_flat_id(block)]
            return _remote(dst if src is None else src, dst, send_sems.at[a, k], recv_sems.at[a, k], to)

        mine = [pltpu.make_async_copy(x_refs[a], out_refs[a].at[_flat_id(me)], local_sems.at[a])
                for a in range(n_arr)]
        for cp in mine:
            cp.start()
        first = []
        for a in range(n_arr):
            first.append(copy(a, 0, me, sibling, src=x_refs[a]))
            first += [copy(a, 1 + j, me, (*chip, c), src=x_refs[a]) for j, chip in enumerate(chips)]
        for cp in first:
            cp.start()
        passed = []
        for j, chip in enumerate(chips):
            for a in range(n_arr):
                copy(a, 1 + j, (*chip, c), me).wait_recv()
                fwd = copy(a, 4 + j, (*chip, c), sibling)
                fwd.start()
                passed.append(fwd)
        for a in range(n_arr):
            copy(a, 0, sibling, me).wait_recv()
            for j, chip in enumerate(chips):
                copy(a, 4 + j, (*chip, 1 - c), me).wait_recv()
        for cp in first + passed:
            cp.wait_send()
        for cp in mine:
            cp.wait()

    return pl.pallas_call(
        body, name=name, in_specs=[_ANY] * n_arr, out_specs=[_ANY] * n_arr,
        out_shape=[jax.ShapeDtypeStruct((N_DEV,) + s.shape, s.dtype) for s in shards],
        scratch_shapes=[pltpu.SemaphoreType.DMA((n_arr, N_DEV - 1)), pltpu.SemaphoreType.DMA((n_arr, N_DEV - 1)),
                        pltpu.SemaphoreType.DMA((n_arr,))],
    )(*shards)


class _Direct:
    def __init__(self, srcs, bufs, prefix, per_dest, buf_shapes):
        self.srcs, self.bufs, self.prefix, self.per_dest = list(srcs), bufs, tuple(prefix), per_dest
        self.n = len(self.srcs)
        self.out_shape = [jax.ShapeDtypeStruct(s, a.dtype) for s, a in zip(buf_shapes, self.srcs)]
        self.args = self.srcs + (list(bufs) if bufs is not None else [])
        self.scratch = [pltpu.SemaphoreType.DMA((self.n, N_DEV - 1)), pltpu.SemaphoreType.DMA((self.n, N_DEV - 1)),
                        pltpu.SemaphoreType.DMA((self.n,))]

    def aliases(self, in_base, out_base):
        return {} if self.bufs is None else {in_base + self.n + a: out_base + a for a in range(self.n)}

    def copies(self, in_refs, out_refs, sems):
        send_sems, recv_sems, local_sems = sems
        x, y, c = _me()
        me = _flat_id((x, y, c))
        peers = [(x ^ ((rel >> 2) & 1), y ^ ((rel >> 1) & 1), c ^ (rel & 1)) for rel in range(1, N_DEV)]
        src = lambda a, d: in_refs[a].at[d] if self.per_dest else in_refs[a]
        dst = lambda a, s: out_refs[a].at[self.prefix + (s,)]
        local = [pltpu.make_async_copy(src(a, me), dst(a, me), local_sems.at[a]) for a in range(self.n)]
        sends, recvs = [], []
        for a in range(self.n):
            for k, peer in enumerate(peers):
                pid = _flat_id(peer)
                sends.append(_remote(src(a, pid), dst(a, me), send_sems.at[a, k], recv_sems.at[a, k], peer))
                recvs.append(_remote(src(a, pid), dst(a, pid), send_sems.at[a, k], recv_sems.at[a, k], peer))
        return local, sends, recvs

    def start(self, in_refs, out_refs, sems):
        local, sends, _ = self.copies(in_refs, out_refs, sems)
        for cp in local + sends:
            cp.start()

    def wait(self, in_refs, out_refs, sems):
        local, sends, recvs = self.copies(in_refs, out_refs, sems)
        for cp in recvs:
            cp.wait_recv()
        for cp in sends:
            cp.wait_send()
        for cp in local:
            cp.wait()


def _exchange(direct, *, name):
    n, n_in = direct.n, len(direct.args)

    def body(*refs):
        in_refs, out_refs, sems = refs[:n], refs[n_in:n_in + n], refs[n_in + n:]
        direct.start(in_refs, out_refs, sems)
        direct.wait(in_refs, out_refs, sems)

    return pl.pallas_call(
        body, name=name, in_specs=[_ANY] * n_in, out_specs=[_ANY] * n, out_shape=direct.out_shape,
        input_output_aliases=direct.aliases(0, 0), scratch_shapes=direct.scratch,
    )(*direct.args)


def _grad_exchange(contribs, bufs, layer):
    return _Direct(contribs, bufs, (layer,), True, [(DEPTH,) + c.shape for c in contribs])


def _adamw(parts, w, m, v, *, tr, name):
    depth, rows, cols = w.shape
    c1 = 1.0 - ADAM_B1 ** ADAM_STEP
    c2 = 1.0 - ADAM_B2 ** ADAM_STEP

    def body(g_ref, w_ref, m_ref, v_ref, go_ref, d_ref, mo_ref, vo_ref):
        g = g_ref[0, 0].astype(F32)
        for s in range(1, N_DEV):
            g = g + g_ref[0, s].astype(F32)
        m_new = ADAM_B1 * m_ref[0] + (1.0 - ADAM_B1) * g
        v_new = ADAM_B2 * v_ref[0] + (1.0 - ADAM_B2) * (g * g)
        go_ref[0] = g
        mo_ref[0] = m_new
        vo_ref[0] = v_new
        d_ref[0] = -ADAM_LR * ((m_new / c1) / (jnp.sqrt(v_new / c2) + ADAM_EPS) + ADAM_WD * w_ref[0])

    tile = pl.BlockSpec((1, tr, cols), lambda l, i: (l, i, 0))
    return pl.pallas_call(
        body, name=name, grid=(depth, rows // tr),
        in_specs=[pl.BlockSpec((1, N_DEV, tr, cols), lambda l, i: (l, 0, i, 0)), tile, tile, tile],
        out_specs=[tile] * 4, out_shape=[jax.ShapeDtypeStruct(w.shape, F32)] * 4,
        compiler_params=_cparams(("parallel", "parallel")),
    )(parts, w, m, v)


def _pack_small(conv, small):
    lead = conv.shape[:-2]
    flat = jnp.concatenate([conv.reshape(lead + (CS_CONV,))] + list(small), axis=-1)
    pad = CS_ROWS * LANE - flat.shape[-1]
    flat = jnp.concatenate([flat, jnp.zeros(lead + (pad,), F32)], axis=-1)
    return flat.reshape(lead + (CS_ROWS, LANE))


def _unpack_small(p):
    flat = p.reshape(DEPTH, CS_ROWS * LANE)
    conv = flat[:, :CS_CONV].reshape(DEPTH, CONV_K, CONV_SHARD_COLS)
    small, off = [], CS_CONV
    for _, n in SMALL_SIZES:
        small.append(flat[:, off:off + n])
        off += n
    return conv, small


def kernel(x, w_in, conv_w, a_log, dt_bias, norm_w, sinks, w_out, ln_g, ln_b, loss_target, m_w_in, m_conv_w, m_a_log, m_dt_bias, m_norm_w, m_sinks, m_w_out, m_ln_g, m_ln_b, v_w_in, v_conv_w, v_a_log, v_dt_bias, v_norm_w, v_sinks, v_w_out, v_ln_g, v_ln_b):
    small = [a_log, dt_bias, norm_w, sinks, ln_g, ln_b]
    shards = [[w_in[l].astype(BF16), w_out[l].astype(BF16), conv_w[l]] for l in range(DEPTH)]
    weights0 = _unpack_weights(*_all_gather(shards[0], name="weights_all_gather_0"))

    y, saved, weights = _forward(x[0], [weights0], shards[1], small)
    dx, loss_lanes = _loss_head(y, loss_target[0], name="loss_head")
    loss = lax.psum(0.5 * jnp.sum(loss_lanes) * (1.0 / D_MODEL), ("x", "y", "c"))

    def contributions(g):
        c_in = g["w_in"].reshape(D_MODEL, N_DEV, SHARD_COLS).transpose(1, 0, 2).astype(BF16)
        c_out = g["w_out"].astype(BF16).reshape(N_DEV, OUT_SHARD_ROWS, D_MODEL)
        c_conv = g["conv_w"].reshape(CONV_K, N_DEV, CONV_SHARD_COLS).transpose(1, 0, 2)
        c_small = [jnp.broadcast_to(g[n][None], (N_DEV,) + g[n].shape) for n, _ in SMALL_SIZES]
        return [c_in, c_out, _pack_small(c_conv, c_small)]

    dx, g1, _ = _backward_layer(1, dx, saved[1], weights[1], ln_g[1])
    dx, g0, bufs = _backward_layer(0, dx, saved[0], weights[0], ln_g[0],
                                   rider=_grad_exchange(contributions(g1), None, 1))
    bufs = _exchange(_grad_exchange(contributions(g0), bufs, 0), name="grad_exchange_0")

    p_in, p_out, p_small = bufs
    o_in = _adamw(p_in, w_in, m_w_in, v_w_in, tr=256, name="adamw_w_in")
    o_out = _adamw(p_out, w_out, m_w_out, v_w_out, tr=OUT_SHARD_ROWS, name="adamw_w_out")
    o_small = _adamw(p_small, _pack_small(conv_w, small),
                     _pack_small(m_conv_w, [m_a_log, m_dt_bias, m_norm_w, m_sinks, m_ln_g, m_ln_b]),
                     _pack_small(v_conv_w, [v_a_log, v_dt_bias, v_norm_w, v_sinks, v_ln_g, v_ln_b]),
                     tr=CS_ROWS, name="adamw_small")
    outs = []
    for k in range(4):
        cv, sm = _unpack_small(o_small[k])
        outs += [o_in[k], cv, sm[0], sm[1], sm[2], sm[3], o_out[k], sm[4], sm[5]]
    return (loss, dx[None], *outs)
```

```python
import functools

import jax
import jax.numpy as jnp
from jax import lax
from jax.experimental import pallas as pl
from jax.experimental.pallas import tpu as pltpu

F32 = jnp.float32
BF16 = jnp.bfloat16
MM_DTYPE = BF16

N_DEV = 8
D_MODEL = 1024
DEPTH = 2
A_HEADS = 4
A_HEAD_DIM = 128
A_WIDTH = 512
CONV_K = 4
CHUNK = 64
SUPER = 256
G_HEADS = 4
NEWTON_STEPS = 1
B_Q_HEADS = 8
B_KV_HEADS = 2
B_HEAD_DIM = 64
B_GROUP = 4
B_WIDTH = 512
WINDOW = 128
BLOCK = 128
IN_COLS = 3336
SHARD_COLS = IN_COLS // N_DEV
OUT_SHARD_ROWS = D_MODEL // N_DEV
CONV_SHARD_COLS = 3 * A_WIDTH // N_DEV
DEEPNORM_ALPHA = (2 * DEPTH) ** 0.25
LN_EPS = 1e-5
RMS_EPS = 1e-6
L2_EPS = 1e-6
ADAM_LR, ADAM_B1, ADAM_B2, ADAM_EPS, ADAM_WD, ADAM_STEP = 0.001, 0.9, 0.999, 1e-08, 0.01, 10

LANE = 128
L_QB, L_ZB, L_ZA, L_KB, L_VB, L_QKV, L_BA = 0, 512, 1024, 1536, 1664, 1792, 3328
L_SWA = 1792
L_MAIN = 3328
L_COLS = 3456
SMALL_SIZES = (("a_log", 4), ("dt_bias", 4), ("norm_w", 128), ("sinks", 8), ("ln_g", 1024), ("ln_b", 1024))
CS_CONV = CONV_K * CONV_SHARD_COLS
CS_ROWS = 24
VMEM_LIMIT = 48 * 1024 * 1024


def _cparams(sem=None):
    return pltpu.CompilerParams(dimension_semantics=sem, vmem_limit_bytes=VMEM_LIMIT)


def _mm(a, b):
    return jnp.dot(a.astype(MM_DTYPE), b.astype(MM_DTYPE), preferred_element_type=F32)


def _mm_nt(a, b):
    return lax.dot_general(a.astype(MM_DTYPE), b.astype(MM_DTYPE), (((1,), (1,)), ((), ())),
                           preferred_element_type=F32)


def _mm_tn(a, b):
    return lax.dot_general(a.astype(MM_DTYPE), b.astype(MM_DTYPE), (((0,), (0,)), ((), ())),
                           preferred_element_type=F32)


def _split(a):
    hi = a.astype(BF16)
    return hi, (a - hi.astype(F32)).astype(BF16)


def _hp(a2, b2):
    d = lambda p, q: jnp.dot(p, q, preferred_element_type=F32)
    return d(a2[0], b2[0]) + (d(a2[0], b2[1]) + d(a2[1], b2[0]))


def _silu(x):
    return x * jax.nn.sigmoid(x)


def _softplus(x):
    return jnp.maximum(x, 0.0) + jnp.log1p(jnp.exp(-jnp.abs(x)))


_ANY = pl.BlockSpec(memory_space=pl.ANY)


def _matmul(a, b, *, form, tm, tn, tk, name, add=None, add_scale=1.0, extra=None):
    if form == "nn":
        (m, kk), n = a.shape, b.shape[1]
        a_spec = pl.BlockSpec((tm, tk), lambda i, j, k: (i, k))
        b_spec = pl.BlockSpec((tk, tn), lambda i, j, k: (k, j))
        dn = (((1,), (0,)), ((), ()))
    elif form == "nt":
        (m, kk), n = a.shape, b.shape[0]
        a_spec = pl.BlockSpec((tm, tk), lambda i, j, k: (i, k))
        b_spec = pl.BlockSpec((tn, tk), lambda i, j, k: (j, k))
        dn = (((1,), (1,)), ((), ()))
    else:
        (kk, m), n = a.shape, b.shape[1]
        a_spec = pl.BlockSpec((tk, tm), lambda i, j, k: (k, i))
        b_spec = pl.BlockSpec((tk, tn), lambda i, j, k: (k, j))
        dn = (((0,), (0,)), ((), ()))
    assert m % tm == 0 and n % tn == 0 and kk % tk == 0, (name, m, n, kk)
    has_add, has_extra = add is not None, extra is not None

    def body(*refs):
        refs = list(refs)
        a_ref, b_ref = refs[:2]
        o_ref = refs[-1]
        rest = refs[2:-1]
        k = pl.program_id(2)
        p = lax.dot_general(a_ref[...].astype(MM_DTYPE), b_ref[...].astype(MM_DTYPE), dn,
                            preferred_element_type=F32)

        @pl.when(k == 0)
        def _():
            first = p
            pos = 0
            if has_extra:
                first = first + _mm_nt(rest[0][...], rest[1][...])
                pos = 2
            if has_add:
                first = first + add_scale * rest[pos][...]
            o_ref[...] = first

        @pl.when(k > 0)
        def _():
            o_ref[...] += p

    in_specs = [a_spec, b_spec]
    args = [a, b]
    if has_extra:
        a2, b2, idx = extra
        in_specs += [pl.BlockSpec((tm, LANE), lambda i, j, k: (i, 0)),
                     pl.BlockSpec((tn, LANE), lambda i, j, k: (j, idx))]
        args += [a2, b2]
    if has_add:
        in_specs.append(pl.BlockSpec((tm, tn), lambda i, j, k: (i, j)))
        args.append(add)
    return pl.pallas_call(
        body, name=name, grid=(m // tm, n // tn, kk // tk), in_specs=in_specs,
        out_specs=pl.BlockSpec((tm, tn), lambda i, j, k: (i, j)),
        out_shape=jax.ShapeDtypeStruct((m, n), F32),
        compiler_params=_cparams(("parallel", "parallel", "arbitrary")),
    )(*args)


def _shift_down(x, k, row):
    return jnp.where(row >= k, pltpu.roll(x, k, 0), 0.0)


def _shift_up(x, k, row, t_len):
    return jnp.where(row < t_len - k, pltpu.roll(x, t_len - k, 0), 0.0)


def _conv_slab(x, w, row):
    return (w[3:4] * x + w[2:3] * _shift_down(x, 1, row) + w[1:2] * _shift_down(x, 2, row)
            + w[0:1] * _shift_down(x, 3, row))


def _prep_fwd(h, conv_w, *, name):
    t_len = h.shape[0]

    def body(x_ref, w_ref, o_ref):
        s = pl.program_id(0)
        row = lax.broadcasted_iota(jnp.int32, (t_len, LANE), 0)
        y = _silu(_conv_slab(x_ref[...], w_ref[...], row))
        rs = lax.rsqrt(jnp.sum(y * y, axis=-1, keepdims=True) + L2_EPS)
        scale = jnp.where(s < A_HEADS, A_HEAD_DIM ** -0.5, 1.0)
        o_ref[...] = jnp.where(s < 2 * A_HEADS, y * rs * scale, y)

    return pl.pallas_call(
        body, name=name, grid=(12,),
        in_specs=[pl.BlockSpec((t_len, LANE), lambda s: (0, L_QKV // LANE + s)),
                  pl.BlockSpec((8, LANE), lambda s: (0, s))],
        out_specs=pl.BlockSpec((t_len, LANE), lambda s: (0, s)),
        out_shape=jax.ShapeDtypeStruct((t_len, 3 * A_WIDTH), F32),
        compiler_params=_cparams(("parallel",)),
    )(h, conv_w)


def _prep_bwd(h, conv_w, d_out, dh, *, name):
    t_len = h.shape[0]

    def body(x_ref, w_ref, g_ref, dh_in, dx_ref, dw_ref):
        del dh_in
        s = pl.program_id(0)
        row = lax.broadcasted_iota(jnp.int32, (t_len, LANE), 0)
        x = x_ref[...]
        w = w_ref[...]
        c = _conv_slab(x, w, row)
        sg = jax.nn.sigmoid(c)
        y = c * sg
        g = g_ref[0]
        rs = lax.rsqrt(jnp.sum(y * y, axis=-1, keepdims=True) + L2_EPS)
        scale = jnp.where(s < A_HEADS, A_HEAD_DIM ** -0.5, 1.0)
        dy_n = scale * (rs * g - y * (rs * rs * rs) * jnp.sum(g * y, axis=-1, keepdims=True))
        dy = jnp.where(s < 2 * A_HEADS, dy_n, g)
        dc = dy * (sg * (1.0 + c * (1.0 - sg)))
        dx_ref[...] = (w[3:4] * dc + w[2:3] * _shift_up(dc, 1, row, t_len)
                       + w[1:2] * _shift_up(dc, 2, row, t_len) + w[0:1] * _shift_up(dc, 3, row, t_len))
        dws = [jnp.sum(dc * _shift_down(x, 3 - j, row), axis=0, keepdims=True) if j < 3
               else jnp.sum(dc * x, axis=0, keepdims=True) for j in range(CONV_K)]
        dw_ref[...] = jnp.concatenate(dws + [jnp.zeros((8 - CONV_K, LANE), F32)], axis=0)

    slab = pl.BlockSpec((t_len, LANE), lambda s: (0, L_QKV // LANE + s))
    return pl.pallas_call(
        body, name=name, grid=(12,),
        in_specs=[slab, pl.BlockSpec((8, LANE), lambda s: (0, s)),
                  pl.BlockSpec((1, t_len, LANE), lambda s: (s // A_HEADS, 0, s % A_HEADS)), _ANY],
        out_specs=[slab, pl.BlockSpec((8, LANE), lambda s: (0, s))],
        out_shape=[jax.ShapeDtypeStruct((t_len, L_MAIN), F32), jax.ShapeDtypeStruct((8, 3 * A_WIDTH), F32)],
        input_output_aliases={3: 0},
        compiler_params=_cparams(("parallel",)),
    )(h, conv_w, d_out, dh)


N_LEVELS = 5
MF_TRIL, MF_STRIL, MF_DIAG8, MF_LOW16, MF_EYE = 0, 1, 2, 3, 3 + N_LEVELS
MB_CUM, MB_CUM_T, MB_TOT = 0, 1, 2


def _gdn_masks():
    r = lax.broadcasted_iota(jnp.int32, (SUPER, SUPER), 0)
    c = lax.broadcasted_iota(jnp.int32, (SUPER, SUPER), 1)
    same = lambda shift: (r >> shift) == (c >> shift)
    ninf = lambda m: jnp.where(m, 0.0, -jnp.inf).astype(F32)
    one = lambda m: m.astype(F32)
    mf = jnp.stack([ninf(r >= c), ninf(r > c), one(same(3))]
                   + [one(same(4 + lv) & jnp.logical_not(same(3 + lv))) for lv in range(N_LEVELS)] + [one(r == c)])
    mb = jnp.stack([one(r >= c), one(r <= c), jnp.ones((SUPER, SUPER), F32)]).astype(BF16)
    return mf, mb


def _tri_inv_impl(a, mf):
    d = lambda p, q: jnp.dot(p.astype(BF16), q.astype(BF16), preferred_element_type=F32)
    dd = lambda p, q: jnp.dot(p, q, preferred_element_type=F32)
    eye = mf[MF_EYE]
    a0 = a * mf[MF_DIAG8]
    a2 = d(a0, a0)
    a4 = d(a2, a2)
    t = d(d(eye - a0, eye + a2), eye + a4)
    for level in range(N_LEVELS):
        t = t - d(d(t, a * mf[MF_LOW16 + level]), t)
    a_hi, a_lo = _split(a)
    for _ in range(NEWTON_STEPS):
        t_hi, t_lo = _split(t)
        resid = (eye - t) - (dd(a_hi, t_hi) + (dd(a_hi, t_lo) + dd(a_lo, t_hi)))
        r_hi, r_lo = _split(resid)
        t = t + (dd(t_hi, r_hi) + dd(t_hi, r_lo))
    return t


@jax.custom_vjp
def _wy_apply(a, rhs, t):
    return _mm(t, rhs)


def _wy_apply_fwd(a, rhs, t):
    x = _mm(t, rhs)
    return x, (t, x)


def _wy_apply_bwd(res, dx):
    t, x = res
    d_rhs = _mm_tn(t, dx)
    return -_mm_nt(d_rhs, x), d_rhs, jnp.zeros_like(t)


_wy_apply.defvjp(_wy_apply_fwd, _wy_apply_bwd)


@functools.partial(jax.custom_vjp, nondiff_argnums=(1,))
def _lane_roll(x, shift):
    return pltpu.roll(x, shift % LANE, 1)


_lane_roll.defvjp(lambda x, shift: (_lane_roll(x, shift), None), lambda shift, _, g: (_lane_roll(g, -shift),))


def _mask_times_lanes(x, mask):
    lane = lax.broadcasted_iota(jnp.int32, (1, LANE), 1)
    x = jnp.where(lane < A_HEADS, x, 0.0)
    x1 = x.astype(BF16).astype(F32)
    x2 = (x - x1).astype(BF16).astype(F32)
    x3 = (x - x1 - x2).astype(BF16).astype(F32)
    pieces = x1 + pltpu.roll(x2, A_HEADS, 1) + pltpu.roll(x3, 2 * A_HEADS, 1)
    res = jnp.dot(mask, pieces.astype(BF16), preferred_element_type=F32)
    return res + pltpu.roll(res, LANE - A_HEADS, 1) + pltpu.roll(res, LANE - 2 * A_HEADS, 1)


@jax.custom_vjp
def _chunk_sums(g, mb):
    return _mask_times_lanes(g, mb[MB_CUM]), _mask_times_lanes(g, mb[MB_TOT])


def _chunk_sums_fwd(g, mb):
    return _chunk_sums(g, mb), mb


def _chunk_sums_bwd(mb, d):
    lane = lax.broadcasted_iota(jnp.int32, (1, LANE), 1)
    dg = _mask_times_lanes(d[0], mb[MB_CUM_T]) + _mask_times_lanes(d[1], mb[MB_TOT])
    return jnp.where(lane < A_HEADS, dg, 0.0), jnp.zeros_like(mb)


_chunk_sums.defvjp(_chunk_sums_fwd, _chunk_sums_bwd)


def _gdn_gates(ba, alog, dtb, mb):
    beta = jax.nn.sigmoid(ba)
    g = -jnp.exp(alog) * _softplus(_lane_roll(ba, -A_HEADS) + dtb)
    gc, gl = _chunk_sums(g, mb)
    return beta, gc, gl, gc.T


def _gdn_block(s, q, k, v, z, gates, nw, h, t_known, mf):
    n = q.shape[0]
    beta_all, gc_all, gl_all, gct_all = gates
    lane = lax.broadcasted_iota(jnp.int32, (1, LANE), 1)
    sub = lax.broadcasted_iota(jnp.int32, (LANE, 1), 0)
    col = lambda x: jnp.sum(jnp.where(lane == h, x, 0.0), axis=1, keepdims=True)
    wide = lambda c: jnp.broadcast_to(c, (n, LANE))
    gc, gl = col(gc_all), col(gl_all)
    gc_row = jnp.sum(jnp.where(sub == h, gct_all, 0.0), axis=0, keepdims=True)
    beta_w, eg_w = wide(col(beta_all)), wide(jnp.exp(gc))
    diff = gc - gc_row
    decay = jnp.exp(diff + mf[MF_TRIL])
    kb = k * beta_w
    a_mat = _mm_nt(kb, k) * jnp.exp(diff + mf[MF_STRIL])
    rhs = jnp.concatenate([v * beta_w, kb * eg_w], axis=1)
    if t_known is None:
        t_mat = _tri_inv_impl(a_mat, mf)
        uw = _mm(t_mat, rhs)
    else:
        t_mat = t_known
        uw = _wy_apply(a_mat, rhs, t_known)
    u, w = uw[:, :LANE], uw[:, LANE:]
    qk = _mm_nt(q, k) * decay
    q_dec = q * eg_w
    k_dec = k * wide(jnp.exp(gl - gc))
    v_new = u - _mm(w, s)
    o = _mm(q_dec, s) + _mm(qk, v_new)
    s = s * jnp.exp(gl[0:1]) + _mm_tn(k_dec, v_new)
    o = o * lax.rsqrt(jnp.mean(o * o, axis=-1, keepdims=True) + RMS_EPS) * nw
    return o * _silu(z), s, t_mat


def _gdn_fwd(qkv, h, alog, dtb, nw, ycat, *, name, rider=None):
    t_len = qkv.shape[0]
    nsc = t_len // SUPER

    n_rin, n_rout = (len(rider.args), rider.n) if rider else (0, 0)

    def body(*refs):
        q_ref, k_ref, v_ref, z_ref, ba_ref, al_ref, dt_ref, nw_ref, mf_ref, mb_ref, _ = refs[:11]
        y_ref, sin_ref, t_ref = refs[11 + n_rin:14 + n_rin]
        s_scr = refs[14 + n_rin + n_rout]
        sc, pg = pl.program_id(0), pl.program_id(1)
        heads = [pg * G_HEADS + hh for hh in range(G_HEADS)]
        if rider:
            r_refs = (refs[11:11 + rider.n], refs[14 + n_rin:14 + n_rin + n_rout], refs[15 + n_rin + n_rout:])
            pl.when((sc == 0) & (pg == 0))(lambda: rider.start(*r_refs))

        @pl.when(sc == 0)
        def _():
            for hd in heads:
                s_scr[hd] = jnp.zeros((A_HEAD_DIM, A_HEAD_DIM), F32)

        per_head = lambda ref: jnp.stack([ref[:, hh * LANE:(hh + 1) * LANE] for hh in range(G_HEADS)])
        states = jnp.stack([s_scr[hd] for hd in heads])
        gates = _gdn_gates(ba_ref[...], al_ref[...], dt_ref[...], mb_ref[...])
        fn = jax.vmap(_gdn_block, in_axes=(0, 0, 0, 0, 0, None, None, 0, None, None))
        y, s_new, t_mat = fn(states, per_head(q_ref), per_head(k_ref), per_head(v_ref), per_head(z_ref),
                             gates, nw_ref[...], jnp.stack(heads), None, mf_ref[...])
        sin_ref[0] = states
        t_ref[0] = t_mat
        for hh, hd in enumerate(heads):
            y_ref[:, hh * LANE:(hh + 1) * LANE] = y[hh]
            s_scr[hd] = s_new[hh]
        if rider:
            pl.when((sc == nsc - 1) & (pg == A_HEADS // G_HEADS - 1))(lambda: rider.wait(*r_refs))

    gw = G_HEADS * LANE
    blk = lambda off: pl.BlockSpec((SUPER, gw), lambda sc, pg: (sc, off // G_HEADS + pg))
    row = pl.BlockSpec((1, LANE), lambda sc, pg: (0, 0))
    mf, mb = _gdn_masks()
    whole = lambda a: pl.BlockSpec(a.shape, lambda sc, pg: (0, 0, 0))
    aliases = {10: 0}
    if rider:
        aliases.update(rider.aliases(11, 3))
    return pl.pallas_call(
        body, name=name, grid=(nsc, A_HEADS // G_HEADS),
        in_specs=[blk(0), blk(4), blk(8), blk(L_ZA // LANE),
                  pl.BlockSpec((SUPER, LANE), lambda sc, pg: (sc, L_BA // LANE)), row, row, row,
                  whole(mf), whole(mb), _ANY] + [_ANY] * n_rin,
        out_specs=[blk(0),
                   pl.BlockSpec((1, G_HEADS, A_HEAD_DIM, A_HEAD_DIM), lambda sc, pg: (sc, pg, 0, 0)),
                   pl.BlockSpec((1, G_HEADS, SUPER, SUPER), lambda sc, pg: (sc, pg, 0, 0))] + [_ANY] * n_rout,
        out_shape=[jax.ShapeDtypeStruct((t_len, D_MODEL), F32),
                   jax.ShapeDtypeStruct((nsc, A_HEADS, A_HEAD_DIM, A_HEAD_DIM), F32),
                   jax.ShapeDtypeStruct((nsc, A_HEADS, SUPER, SUPER), F32)] + (rider.out_shape if rider else []),
        scratch_shapes=[pltpu.VMEM((A_HEADS, A_HEAD_DIM, A_HEAD_DIM), F32)] + (rider.scratch if rider else []),
        input_output_aliases=aliases,
        compiler_params=_cparams(("arbitrary", "arbitrary")),
    )(qkv, qkv, qkv, h, h, alog, dtb, nw, mf, mb, ycat, *(rider.args if rider else []))


def _gdn_bwd(qkv, h, alog, dtb, nw, s_in, t_in, dycat, dh, *, name, rider=None):
    t_len = qkv.shape[0]
    nsc = t_len // SUPER
    n_rin, n_rout = (len(rider.args), rider.n) if rider else (0, 0)

    def body(*refs):
        (q_ref, k_ref, v_ref, z_ref, ba_ref, al_ref, dt_ref, nw_ref, sin_ref, t_ref, dy_ref, mf_ref, mb_ref,
         _) = refs[:14]
        dz_ref, dqkv_ref, dba_ref, dal_ref, ddt_ref, dnw_ref = refs[14 + n_rin:20 + n_rin]
        ds_scr = refs[20 + n_rin + n_rout]
        i, pg = pl.program_id(0), pl.program_id(1)
        if rider:
            r_refs = (refs[14:14 + rider.n], refs[20 + n_rin:20 + n_rin + n_rout], refs[21 + n_rin + n_rout:])
            pl.when((i == 0) & (pg == 0))(lambda: rider.start(*r_refs))

        @pl.when((i == 0) & (pg == 0))
        def _():
            dal_ref[...] = jnp.zeros_like(dal_ref)
            ddt_ref[...] = jnp.zeros_like(ddt_ref)
            dnw_ref[...] = jnp.zeros_like(dnw_ref)

        @pl.when(pg == 0)
        def _():
            dba_ref[...] = jnp.zeros_like(dba_ref)

        heads = [pg * G_HEADS + hh for hh in range(G_HEADS)]

        @pl.when(i == 0)
        def _():
            for hd in heads:
                ds_scr[hd] = jnp.zeros((A_HEAD_DIM, A_HEAD_DIM), F32)

        per_head = lambda ref: jnp.stack([ref[:, hh * LANE:(hh + 1) * LANE] for hh in range(G_HEADS)])
        d_states = jnp.stack([ds_scr[hd] for hd in heads])
        head_ids = jnp.stack(heads)
        t_known, mf, mb = t_ref[0], mf_ref[...], mb_ref[...]

        def fn(s, q, k, v, z, ba, alog, dtb, nw):
            gates = _gdn_gates(ba, alog, dtb, mb)
            one = lambda s, q, k, v, z, t, h: _gdn_block(s, q, k, v, z, gates, nw, h, t, mf)[:2]
            return jax.vmap(one)(s, q, k, v, z, t_known, head_ids)

        _, vjp = jax.vjp(fn, sin_ref[0], per_head(q_ref), per_head(k_ref), per_head(v_ref), per_head(z_ref),
                         ba_ref[...], al_ref[...], dt_ref[...], nw_ref[...])
        ds, dq, dk, dv, dz, dba, dal, ddt, dnw = vjp((per_head(dy_ref), d_states))
        for hh, hd in enumerate(heads):
            cols = slice(hh * LANE, (hh + 1) * LANE)
            ds_scr[hd] = ds[hh]
            dqkv_ref[0, :, cols] = dq[hh]
            dqkv_ref[1, :, cols] = dk[hh]
            dqkv_ref[2, :, cols] = dv[hh]
            dz_ref[:, cols] = dz[hh]
        dba_ref[...] += dba
        dal_ref[...] += dal
        ddt_ref[...] += ddt
        dnw_ref[...] += dnw
        if rider:
            pl.when((i == nsc - 1) & (pg == A_HEADS // G_HEADS - 1))(lambda: rider.wait(*r_refs))

    rev = lambda i: nsc - 1 - i
    gw = G_HEADS * LANE
    blk = lambda off: pl.BlockSpec((SUPER, gw), lambda i, pg: (rev(i), off // G_HEADS + pg))
    row = pl.BlockSpec((1, LANE), lambda i, pg: (0, 0))
    ba_blk = lambda off: pl.BlockSpec((SUPER, LANE), lambda i, pg: (rev(i), off))
    mf, mb = _gdn_masks()
    whole = lambda a: pl.BlockSpec(a.shape, lambda i, pg: (0, 0, 0))
    aliases = {13: 0}
    if rider:
        aliases.update(rider.aliases(14, 6))
    return pl.pallas_call(
        body, name=name, grid=(nsc, A_HEADS // G_HEADS),
        in_specs=[blk(0), blk(4), blk(8), blk(L_ZA // LANE), ba_blk(L_BA // LANE), row, row, row,
                  pl.BlockSpec((1, G_HEADS, A_HEAD_DIM, A_HEAD_DIM), lambda i, pg: (rev(i), pg, 0, 0)),
                  pl.BlockSpec((1, G_HEADS, SUPER, SUPER), lambda i, pg: (rev(i), pg, 0, 0)),
                  blk(0), whole(mf), whole(mb), _ANY] + [_ANY] * n_rin,
        out_specs=[blk(L_ZA // LANE),
                   pl.BlockSpec((3, SUPER, gw), lambda i, pg: (0, rev(i), pg)),
                   ba_blk(0), row, row, row] + [_ANY] * n_rout,
        out_shape=[jax.ShapeDtypeStruct((t_len, L_MAIN), F32), jax.ShapeDtypeStruct((3, t_len, A_WIDTH), F32),
                   jax.ShapeDtypeStruct((t_len, LANE), F32)] + [jax.ShapeDtypeStruct((1, LANE), F32)] * 3
        + (rider.out_shape if rider else []),
        scratch_shapes=[pltpu.VMEM((A_HEADS, A_HEAD_DIM, A_HEAD_DIM), F32)] + (rider.scratch if rider else []),
        input_output_aliases=aliases,
        compiler_params=_cparams(("arbitrary", "arbitrary")),
    )(qkv, qkv, qkv, h, h, alog, dtb, nw, s_in, t_in, dycat, mf, mb, dh, *(rider.args if rider else []))


def _swa_block(q, kp, kc, vp, vc, z, sinks, first):
    rows = B_GROUP * BLOCK
    ri = lax.broadcasted_iota(jnp.int32, (rows, 2 * BLOCK), 0)
    si = lax.broadcasted_iota(jnp.int32, (rows, 2 * BLOCK), 1)
    dist = (ri & (BLOCK - 1)) + BLOCK - si
    bias = jnp.where((dist >= 0) & (dist < WINDOW) & ((si >= BLOCK) | jnp.logical_not(first)), 0.0, -jnp.inf)
    dist_f = dist.astype(F32)
    head_of_row = lax.broadcasted_iota(jnp.int32, (rows, 1), 0) >> 7

    def group(j):
        cs = slice(j * B_HEAD_DIM, (j + 1) * B_HEAD_DIM)
        heads = range(j * B_GROUP, (j + 1) * B_GROUP)
        qs = jnp.concatenate([q[:, hq * B_HEAD_DIM:(hq + 1) * B_HEAD_DIM] for hq in heads], axis=0)
        kk = jnp.concatenate([kp[:, cs], kc[:, cs]], axis=0)
        vv = jnp.concatenate([vp[:, cs], vc[:, cs]], axis=0)
        sink = jnp.concatenate([jnp.broadcast_to(sinks[:, hq:hq + 1], (BLOCK, 1)) for hq in heads], axis=0)
        slope = sum(jnp.where(head_of_row == gi, 2.0 ** (-8.0 * (hq + 1) / B_Q_HEADS), 0.0)
                    for gi, hq in enumerate(heads))
        return qs, kk, vv, sink, slope

    def attend(qs, kk, vv, sink, slope):
        sc = _mm_nt(qs, kk) * (B_HEAD_DIM ** -0.5) - slope * dist_f + bias
        m = lax.stop_gradient(jnp.maximum(jnp.max(sc, axis=-1, keepdims=True), sink))
        p = jnp.exp(sc - m)
        inv = 1.0 / (jnp.sum(p, axis=-1, keepdims=True) + jnp.exp(sink - m))
        return _mm(p * inv, vv)

    o = jax.vmap(attend)(*[jnp.stack(t) for t in zip(*[group(j) for j in range(B_KV_HEADS)])])
    outs = [o[j, gi * BLOCK:(gi + 1) * BLOCK] for j in range(B_KV_HEADS) for gi in range(B_GROUP)]
    return jnp.concatenate(outs, axis=1) * _silu(z)


def _swa_specs(idx):
    wide = lambda off: pl.BlockSpec((BLOCK, B_WIDTH), lambda n: (idx(n), off))
    cur = lambda off: pl.BlockSpec((BLOCK, LANE), lambda n: (idx(n), off))
    prev = lambda off: pl.BlockSpec((BLOCK, LANE), lambda n: (jnp.maximum(idx(n) - 1, 0), off))
    return [wide(L_QB // B_WIDTH), prev(L_KB // LANE), cur(L_KB // LANE), prev(L_VB // LANE), cur(L_VB // LANE),
            wide(L_ZB // B_WIDTH), pl.BlockSpec((1, LANE), lambda n: (0, 0))]


def _swa_fwd(h, sinks, *, name):
    t_len = h.shape[0]
    nb = t_len // BLOCK

    def body(q_ref, kp_ref, kc_ref, vp_ref, vc_ref, z_ref, s_ref, o_ref):
        o_ref[...] = _swa_block(q_ref[...], kp_ref[...], kc_ref[...], vp_ref[...], vc_ref[...], z_ref[...],
                                s_ref[...], pl.program_id(0) == 0)

    return pl.pallas_call(
        body, name=name, grid=(nb,), in_specs=_swa_specs(lambda n: n),
        out_specs=pl.BlockSpec((BLOCK, B_WIDTH), lambda n: (n, 1)),
        out_shape=jax.ShapeDtypeStruct((t_len, D_MODEL), F32),
        compiler_params=_cparams(("parallel",)),
    )(h, h, h, h, h, h, sinks)


def _swa_bwd(h, sinks, dycat, *, name):
    t_len = h.shape[0]
    nb = t_len // BLOCK

    def body(q_ref, kp_ref, kc_ref, vp_ref, vc_ref, z_ref, s_ref, dy_ref, dh_ref, dsk_ref, ck_scr, cv_scr):
        i = pl.program_id(0)
        n = nb - 1 - i

        @pl.when(i == 0)
        def _():
            ck_scr[...] = jnp.zeros_like(ck_scr)
            cv_scr[...] = jnp.zeros_like(cv_scr)
            dsk_ref[...] = jnp.zeros_like(dsk_ref)

        fn = functools.partial(_swa_block, first=(n == 0))
        _, vjp = jax.vjp(fn, q_ref[...], kp_ref[...], kc_ref[...], vp_ref[...], vc_ref[...], z_ref[...], s_ref[...])
        dq, dkp, dkc, dvp, dvc, dz, dsk = vjp(dy_ref[...])
        dh_ref[:, L_QB:L_QB + B_WIDTH] = dq
        dh_ref[:, L_ZB:L_ZB + B_WIDTH] = dz
        dh_ref[:, L_ZA:L_ZA + A_WIDTH] = jnp.zeros((BLOCK, A_WIDTH), F32)
        dh_ref[:, L_KB:L_KB + LANE] = dkc + ck_scr[...]
        dh_ref[:, L_VB:L_VB + LANE] = dvc + cv_scr[...]
        ck_scr[...] = dkp
        cv_scr[...] = dvp
        dsk_ref[...] += dsk

    rev = lambda i: nb - 1 - i
    return pl.pallas_call(
        body, name=name, grid=(nb,),
        in_specs=_swa_specs(rev) + [pl.BlockSpec((BLOCK, B_WIDTH), lambda i: (rev(i), 1))],
        out_specs=[pl.BlockSpec((BLOCK, L_SWA), lambda i: (rev(i), 0)), pl.BlockSpec((1, LANE), lambda i: (0, 0))],
        out_shape=[jax.ShapeDtypeStruct((t_len, L_MAIN), F32), jax.ShapeDtypeStruct((1, LANE), F32)],
        scratch_shapes=[pltpu.VMEM((BLOCK, LANE), F32), pltpu.VMEM((BLOCK, LANE), F32)],
        compiler_params=_cparams(("arbitrary",)),
    )(h, h, h, h, h, h, sinks, dycat)


def _out_ln_fwd(ycat, w_out, x, ln_g, ln_b, *, name, tm=256):
    t_len = x.shape[0]

    def body(y_ref, w_ref, x_ref, g_ref, b_ref, r_ref, o_ref):
        r = DEEPNORM_ALPHA * x_ref[...] + _mm(y_ref[...], w_ref[...])
        r_ref[...] = r
        mu = jnp.mean(r, axis=-1, keepdims=True)
        d = r - mu
        var = jnp.mean(d * d, axis=-1, keepdims=True)
        o_ref[...] = d * lax.rsqrt(var + LN_EPS) * g_ref[...] + b_ref[...]

    tile = pl.BlockSpec((tm, D_MODEL), lambda i: (i, 0))
    vec = pl.BlockSpec((1, D_MODEL), lambda i: (0, 0))
    return pl.pallas_call(
        body, name=name, grid=(t_len // tm,),
        in_specs=[tile, pl.BlockSpec((D_MODEL, D_MODEL), lambda i: (0, 0)), tile, vec, vec],
        out_specs=[tile, tile],
        out_shape=[jax.ShapeDtypeStruct((t_len, D_MODEL), F32)] * 2,
        compiler_params=_cparams(("parallel",)),
    )(ycat, w_out, x, ln_g, ln_b)


def _ln_bwd(dxn, r, ln_g, *, name, tm=256):
    t_len = r.shape[0]

    def body(dx_ref, r_ref, g_ref, dr_ref, dg_ref, db_ref):
        @pl.when(pl.program_id(0) == 0)
        def _():
            dg_ref[...] = jnp.zeros_like(dg_ref)
            db_ref[...] = jnp.zeros_like(db_ref)

        rr = r_ref[...]
        dx = dx_ref[...]
        mu = jnp.mean(rr, axis=-1, keepdims=True)
        d = rr - mu
        rstd = lax.rsqrt(jnp.mean(d * d, axis=-1, keepdims=True) + LN_EPS)
        xh = d * rstd
        dxh = dx * g_ref[...]
        dr_ref[...] = rstd * (dxh - jnp.mean(dxh, axis=-1, keepdims=True)
                              - xh * jnp.mean(dxh * xh, axis=-1, keepdims=True))
        dg_ref[...] += jnp.sum(dx * xh, axis=0, keepdims=True)
        db_ref[...] += jnp.sum(dx, axis=0, keepdims=True)

    tile = pl.BlockSpec((tm, D_MODEL), lambda i: (i, 0))
    vec = pl.BlockSpec((1, D_MODEL), lambda i: (0, 0))
    return pl.pallas_call(
        body, name=name, grid=(t_len // tm,),
        in_specs=[tile, tile, vec], out_specs=[tile, vec, vec],
        out_shape=[jax.ShapeDtypeStruct((t_len, D_MODEL), F32), jax.ShapeDtypeStruct((1, D_MODEL), F32),
                   jax.ShapeDtypeStruct((1, D_MODEL), F32)],
        compiler_params=_cparams(("arbitrary",)),
    )(dxn, r, ln_g)


def _loss_head(y, target, *, name, tm=256):
    t_len = y.shape[0]

    def body(y_ref, t_ref, d_ref, l_ref):
        @pl.when(pl.program_id(0) == 0)
        def _():
            l_ref[...] = jnp.zeros_like(l_ref)

        e = y_ref[...] - t_ref[...]
        d_ref[...] = e * (1.0 / D_MODEL)
        l_ref[...] += jnp.sum(e * e, axis=0, keepdims=True)

    tile = pl.BlockSpec((tm, D_MODEL), lambda i: (i, 0))
    vec = pl.BlockSpec((1, D_MODEL), lambda i: (0, 0))
    return pl.pallas_call(
        body, name=name, grid=(t_len // tm,), in_specs=[tile, tile], out_specs=[tile, vec],
        out_shape=[jax.ShapeDtypeStruct((t_len, D_MODEL), F32), jax.ShapeDtypeStruct((1, D_MODEL), F32)],
        compiler_params=_cparams(("arbitrary",)),
    )(y, target)


def _pad_row(v):
    return jnp.zeros((1, LANE), F32).at[0, :v.shape[0]].set(v)


def _to_layout(w_full):
    s = lambda a, b: w_full[..., a:b]
    pad = jnp.zeros(w_full.shape[:-1] + (LANE - 2 * A_HEADS,), w_full.dtype)
    return jnp.concatenate([s(2056, 2568), s(2824, 3336), s(1536, 2048), s(2568, 2696), s(2696, 2824), s(0, 1536),
                            s(2048, 2056), pad], axis=-1)


def _from_layout(g_main, g_ba):
    s = lambda a, b: g_main[..., a:b]
    return jnp.concatenate([s(L_QKV, L_QKV + 1536), s(L_ZA, L_ZA + 512), g_ba[..., :2 * A_HEADS],
                            s(L_QB, L_QB + 512), s(L_KB, L_KB + 128), s(L_VB, L_VB + 128), s(L_ZB, L_ZB + 512)],
                           axis=-1)


def _unpack_weights(g_in, g_out, g_conv):
    w_in_l = _to_layout(g_in.transpose(1, 0, 2).reshape(D_MODEL, IN_COLS))
    conv_l = jnp.pad(g_conv.transpose(1, 0, 2).reshape(CONV_K, 3 * A_WIDTH), ((0, 8 - CONV_K), (0, 0)))
    return w_in_l, g_out.reshape(D_MODEL, D_MODEL), conv_l


def _forward(x, weights, shards1, small):
    a_log, dt_bias, norm_w, sinks, ln_g, ln_b = small
    tm = min(512, x.shape[0])
    saved, weights = [], list(weights)
    for l in range(DEPTH):
        w_in_l, w_out_l, conv_l = weights[l]
        h = _matmul(x, w_in_l, form="nn", tm=tm, tn=1152, tk=D_MODEL, name=f"in_proj_{l}")
        qkv = _prep_fwd(h, conv_l, name=f"prep_fwd_{l}")
        al, dt, nw, sk = _pad_row(a_log[l]), _pad_row(dt_bias[l]), norm_w[l][None, :], _pad_row(sinks[l])
        ycat = _swa_fwd(h, sk, name=f"swa_fwd_{l}")
        rider = None
        if l == 0 and len(weights) == 1:
            rider = _Direct(shards1, None, (), False, [(N_DEV,) + s.shape for s in shards1])
        ycat, s_in, t_in, *gathered = _gdn_fwd(qkv, h, al, dt, nw, ycat, name=f"gdn_fwd_{l}", rider=rider)
        if rider:
            weights.append(_unpack_weights(*gathered))
        r, xn = _out_ln_fwd(ycat, w_out_l, x, ln_g[l][None, :], ln_b[l][None, :], name=f"out_ln_{l}")
        saved.append((x, h, qkv, s_in, t_in, ycat, r, al, dt, nw, sk))
        x = xn
    return x, saved, weights


def _backward_layer(l, dx, saved_l, weights_l, ln_g_l, rider=None):
    x_in, h, qkv, s_in, t_in, ycat, r, al, dt, nw, sk = saved_l
    w_in_l, w_out_l, conv_l = weights_l
    tm = min(512, x_in.shape[0])
    dr, d_lng, d_lnb = _ln_bwd(dx, r, ln_g_l[None, :], name=f"ln_bwd_{l}")
    dycat = _matmul(dr, w_out_l, form="nt", tm=tm, tn=D_MODEL, tk=D_MODEL, name=f"out_proj_dx_{l}")
    d_wout = _matmul(ycat, dr, form="tn", tm=512, tn=D_MODEL, tk=tm, name=f"out_proj_dw_{l}")
    dh, d_sk = _swa_bwd(h, sk, dycat, name=f"swa_bwd_{l}")
    dh, dqkv_n, dba, d_al, d_dt, d_nw, *rider_bufs = _gdn_bwd(qkv, h, al, dt, nw, s_in, t_in, dycat, dh,
                                                              name=f"gdn_bwd_{l}", rider=rider)
    dh, d_conv = _prep_bwd(h, conv_l, dqkv_n, dh, name=f"prep_bwd_{l}")
    d_win_main = _matmul(x_in, dh, form="tn", tm=512, tn=L_MAIN // 2, tk=tm, name=f"in_proj_dw_{l}")
    d_win_ba = _matmul(x_in, dba, form="tn", tm=D_MODEL, tn=LANE, tk=tm, name=f"in_proj_dw_ba_{l}")
    dx = _matmul(dh, w_in_l, form="nt", tm=tm, tn=D_MODEL, tk=L_MAIN // 2, name=f"in_proj_dx_{l}",
                 add=dr, add_scale=DEEPNORM_ALPHA, extra=(dba, w_in_l, L_BA // LANE))
    grads = dict(w_in=_from_layout(d_win_main, d_win_ba), w_out=d_wout, conv_w=d_conv[:CONV_K],
                 a_log=d_al[0, :A_HEADS], dt_bias=d_dt[0, :A_HEADS], norm_w=d_nw[0], sinks=d_sk[0, :B_Q_HEADS],
                 ln_g=d_lng[0], ln_b=d_lnb[0])
    return dx, grads, rider_bufs


def _me():
    return lax.axis_index("x"), lax.axis_index("y"), lax.axis_index("c")


def _flat_id(pos):
    return 4 * pos[0] + 2 * pos[1] + pos[2]


def _remote(src, dst, send_sem, recv_sem, to):
    return pltpu.make_async_remote_copy(src_ref=src, dst_ref=dst, send_sem=send_sem, recv_sem=recv_sem,
                                        device_id=to, device_id_type=pl.DeviceIdType.MESH)


def _all_gather(shards, *, name):
    n_arr = len(shards)

    def body(*refs):
        x_refs, out_refs = refs[:n_arr], refs[n_arr:2 * n_arr]
        send_sems, recv_sems, local_sems = refs[2 * n_arr:]
        x, y, c = _me()
        me, sibling = (x, y, c), (x, y, 1 - c)
        chips = [(1 - x, y), (x, 1 - y), (1 - x, 1 - y)]

        def copy(a, k, block, to, src=None):
            dst = out_refs[a].at[_flat_id(block)]
            return _remote(dst if src is None else src, dst, send_sems.at[a, k], recv_sems.at[a, k], to)

        mine = [pltpu.make_async_copy(x_refs[a], out_refs[a].at[_flat_id(me)], local_sems.at[a])
                for a in range(n_arr)]
        for cp in mine:
            cp.start()
        first = []
        for a in range(n_arr):
            first.append(copy(a, 0, me, sibling, src=x_refs[a]))
            first += [copy(a, 1 + j, me, (*chip, c), src=x_refs[a]) for j, chip in enumerate(chips)]
        for cp in first:
            cp.start()
        passed = []
        for j, chip in enumerate(chips):
            for a in range(n_arr):
                copy(a, 1 + j, (*chip, c), me).wait_recv()
                fwd = copy(a, 4 + j, (*chip, c), sibling)
                fwd.start()
                passed.append(fwd)
        for a in range(n_arr):
            copy(a, 0, sibling, me).wait_recv()
            for j, chip in enumerate(chips):
                copy(a, 4 + j, (*chip, 1 - c), me).wait_recv()
        for cp in first + passed:
            cp.wait_send()
        for cp in mine:
            cp.wait()

    return pl.pallas_call(
        body, name=name, in_specs=[_ANY] * n_arr, out_specs=[_ANY] * n_arr,
        out_shape=[jax.ShapeDtypeStruct((N_DEV,) + s.shape, s.dtype) for s in shards],
        scratch_shapes=[pltpu.SemaphoreType.DMA((n_arr, N_DEV - 1)), pltpu.SemaphoreType.DMA((n_arr, N_DEV - 1)),
                        pltpu.SemaphoreType.DMA((n_arr,))],
    )(*shards)


class _Direct:
    def __init__(self, srcs, bufs, prefix, per_dest, buf_shapes):
        self.srcs, self.bufs, self.prefix, self.per_dest = list(srcs), bufs, tuple(prefix), per_dest
        self.n = len(self.srcs)
        self.out_shape = [jax.ShapeDtypeStruct(s, a.dtype) for s, a in zip(buf_shapes, self.srcs)]
        self.args = self.srcs + (list(bufs) if bufs is not None else [])
        self.scratch = [pltpu.SemaphoreType.DMA((self.n, N_DEV - 1)), pltpu.SemaphoreType.DMA((self.n, N_DEV - 1)),
                        pltpu.SemaphoreType.DMA((self.n,))]

    def aliases(self, in_base, out_base):
        return {} if self.bufs is None else {in_base + self.n + a: out_base + a for a in range(self.n)}

    def copies(self, in_refs, out_refs, sems):
        send_sems, recv_sems, local_sems = sems
        x, y, c = _me()
        me = _flat_id((x, y, c))
        peers = [(x ^ ((rel >> 2) & 1), y ^ ((rel >> 1) & 1), c ^ (rel & 1)) for rel in range(1, N_DEV)]
        src = lambda a, d: in_refs[a].at[d] if self.per_dest else in_refs[a]
        dst = lambda a, s: out_refs[a].at[self.prefix + (s,)]
        local = [pltpu.make_async_copy(src(a, me), dst(a, me), local_sems.at[a]) for a in range(self.n)]
        sends, recvs = [], []
        for a in range(self.n):
            for k, peer in enumerate(peers):
                pid = _flat_id(peer)
                sends.append(_remote(src(a, pid), dst(a, me), send_sems.at[a, k], recv_sems.at[a, k], peer))
                recvs.append(_remote(src(a, pid), dst(a, pid), send_sems.at[a, k], recv_sems.at[a, k], peer))
        return local, sends, recvs

    def start(self, in_refs, out_refs, sems):
        local, sends, _ = self.copies(in_refs, out_refs, sems)
        for cp in local + sends:
            cp.start()

    def wait(self, in_refs, out_refs, sems):
        local, sends, recvs = self.copies(in_refs, out_refs, sems)
        for cp in recvs:
            cp.wait_recv()
        for cp in sends:
            cp.wait_send()
        for cp in local:
            cp.wait()


def _exchange(direct, *, name):
    n, n_in = direct.n, len(direct.args)

    def body(*refs):
        in_refs, out_refs, sems = refs[:n], refs[n_in:n_in + n], refs[n_in + n:]
        direct.start(in_refs, out_refs, sems)
        direct.wait(in_refs, out_refs, sems)

    return pl.pallas_call(
        body, name=name, in_specs=[_ANY] * n_in, out_specs=[_ANY] * n, out_shape=direct.out_shape,
        input_output_aliases=direct.aliases(0, 0), scratch_shapes=direct.scratch,
    )(*direct.args)


def _grad_exchange(contribs, bufs, layer):
    return _Direct(contribs, bufs, (layer,), True, [(DEPTH,) + c.shape for c in contribs])


def _adamw(parts, w, m, v, *, tr, name):
    depth, rows, cols = w.shape
    c1 = 1.0 - ADAM_B1 ** ADAM_STEP
    c2 = 1.0 - ADAM_B2 ** ADAM_STEP

    def body(g_ref, w_ref, m_ref, v_ref, go_ref, d_ref, mo_ref, vo_ref):
        g = g_ref[0, 0].astype(F32)
        for s in range(1, N_DEV):
            g = g + g_ref[0, s].astype(F32)
        m_new = ADAM_B1 * m_ref[0] + (1.0 - ADAM_B1) * g
        v_new = ADAM_B2 * v_ref[0] + (1.0 - ADAM_B2) * (g * g)
        go_ref[0] = g
        mo_ref[0] = m_new
        vo_ref[0] = v_new
        d_ref[0] = -ADAM_LR * ((m_new / c1) / (jnp.sqrt(v_new / c2) + ADAM_EPS) + ADAM_WD * w_ref[0])

    tile = pl.BlockSpec((1, tr, cols), lambda l, i: (l, i, 0))
    return pl.pallas_call(
        body, name=name, grid=(depth, rows // tr),
        in_specs=[pl.BlockSpec((1, N_DEV, tr, cols), lambda l, i: (l, 0, i, 0)), tile, tile, tile],
        out_specs=[tile] * 4, out_shape=[jax.ShapeDtypeStruct(w.shape, F32)] * 4,
        compiler_params=_cparams(("parallel", "parallel")),
    )(parts, w, m, v)


def _pack_small(conv, small):
    lead = conv.shape[:-2]
    flat = jnp.concatenate([conv.reshape(lead + (CS_CONV,))] + list(small), axis=-1)
    pad = CS_ROWS * LANE - flat.shape[-1]
    flat = jnp.concatenate([flat, jnp.zeros(lead + (pad,), F32)], axis=-1)
    return flat.reshape(lead + (CS_ROWS, LANE))


def _unpack_small(p):
    flat = p.reshape(DEPTH, CS_ROWS * LANE)
    conv = flat[:, :CS_CONV].reshape(DEPTH, CONV_K, CONV_SHARD_COLS)
    small, off = [], CS_CONV
    for _, n in SMALL_SIZES:
        small.append(flat[:, off:off + n])
        off += n
    return conv, small


def kernel(x, w_in, conv_w, a_log, dt_bias, norm_w, sinks, w_out, ln_g, ln_b, loss_target, m_w_in, m_conv_w, m_a_log, m_dt_bias, m_norm_w, m_sinks, m_w_out, m_ln_g, m_ln_b, v_w_in, v_conv_w, v_a_log, v_dt_bias, v_norm_w, v_sinks, v_w_out, v_ln_g, v_ln_b):
    small = [a_log, dt_bias, norm_w, sinks, ln_g, ln_b]
    shards = [[w_in[l].astype(BF16), w_out[l].astype(BF16), conv_w[l]] for l in range(DEPTH)]
    weights0 = _unpack_weights(*_all_gather(shards[0], name="weights_all_gather_0"))

    y, saved, weights = _forward(x[0], [weights0], shards[1], small)
    dx, loss_lanes = _loss_head(y, loss_target[0], name="loss_head")
    loss = lax.psum(0.5 * jnp.sum(loss_lanes) * (1.0 / D_MODEL), ("x", "y", "c"))

    def contributions(g):
        c_in = g["w_in"].reshape(D_MODEL, N_DEV, SHARD_COLS).transpose(1, 0, 2).astype(BF16)
        c_out = g["w_out"].astype(BF16).reshape(N_DEV, OUT_SHARD_ROWS, D_MODEL)
        c_conv = g["conv_w"].reshape(CONV_K, N_DEV, CONV_SHARD_COLS).transpose(1, 0, 2)
        c_small = [jnp.broadcast_to(g[n][None], (N_DEV,) + g[n].shape) for n, _ in SMALL_SIZES]
        return [c_in, c_out, _pack_small(c_conv, c_small)]

    dx, g1, _ = _backward_layer(1, dx, saved[1], weights[1], ln_g[1])
    dx, g0, bufs = _backward_layer(0, dx, saved[0], weights[0], ln_g[0],
                                   rider=_grad_exchange(contributions(g1), None, 1))
    bufs = _exchange(_grad_exchange(contributions(g0), bufs, 0), name="grad_exchange_0")

    p_in, p_out, p_small = bufs
    o_in = _adamw(p_in, w_in, m_w_in, v_w_in, tr=256, name="adamw_w_in")
    o_out = _adamw(p_out, w_out, m_w_out, v_w_out, tr=OUT_SHARD_ROWS, name="adamw_w_out")
    o_small = _adamw(p_small, _pack_small(conv_w, small),
                     _pack_small(m_conv_w, [m_a_log, m_dt_bias, m_norm_w, m_sinks, m_ln_g, m_ln_b]),
                     _pack_small(v_conv_w, [v_a_log, v_dt_bias, v_norm_w, v_sinks, v_ln_g, v_ln_b]),
                     tr=CS_ROWS, name="adamw_small")
    outs = []
    for k in range(4):
        cv, sm = _unpack_small(o_small[k])
        outs += [o_in[k], cv, sm[0], sm[1], sm[2], sm[3], o_out[k], sm[4], sm[5]]
    return (loss, dx[None], *outs)
```

```python
import functools

import jax
import jax.numpy as jnp
from jax import lax
from jax.experimental import pallas as pl
from jax.experimental.pallas import tpu as pltpu

F32 = jnp.float32
BF16 = jnp.bfloat16
MM_DTYPE = BF16

N_DEV = 8
D_MODEL = 1024
DEPTH = 2
A_HEADS = 4
A_HEAD_DIM = 128
A_WIDTH = 512
CONV_K = 4
CHUNK = 64
SUPER = 256
G_HEADS = 4
NEWTON_STEPS = 1
B_Q_HEADS = 8
B_KV_HEADS = 2
B_HEAD_DIM = 64
B_GROUP = 4
B_WIDTH = 512
WINDOW = 128
BLOCK = 128
IN_COLS = 3336
SHARD_COLS = IN_COLS // N_DEV
OUT_SHARD_ROWS = D_MODEL // N_DEV
CONV_SHARD_COLS = 3 * A_WIDTH // N_DEV
DEEPNORM_ALPHA = (2 * DEPTH) ** 0.25
LN_EPS = 1e-5
RMS_EPS = 1e-6
L2_EPS = 1e-6
ADAM_LR, ADAM_B1, ADAM_B2, ADAM_EPS, ADAM_WD, ADAM_STEP = 0.001, 0.9, 0.999, 1e-08, 0.01, 10

LANE = 128
L_QB, L_ZB, L_ZA, L_KB, L_VB, L_QKV, L_BA = 0, 512, 1024, 1536, 1664, 1792, 3328
L_SWA = 1792
L_MAIN = 3328
L_COLS = 3456
SMALL_SIZES = (("a_log", 4), ("dt_bias", 4), ("norm_w", 128), ("sinks", 8), ("ln_g", 1024), ("ln_b", 1024))
CS_CONV = CONV_K * CONV_SHARD_COLS
CS_ROWS = 24
VMEM_LIMIT = 48 * 1024 * 1024


def _cparams(sem=None):
    return pltpu.CompilerParams(dimension_semantics=sem, vmem_limit_bytes=VMEM_LIMIT)


def _mm(a, b):
    return jnp.dot(a.astype(MM_DTYPE), b.astype(MM_DTYPE), preferred_element_type=F32)


def _mm_nt(a, b):
    return lax.dot_general(a.astype(MM_DTYPE), b.astype(MM_DTYPE), (((1,), (1,)), ((), ())),
                           preferred_element_type=F32)


def _mm_tn(a, b):
    return lax.dot_general(a.astype(MM_DTYPE), b.astype(MM_DTYPE), (((0,), (0,)), ((), ())),
                           preferred_element_type=F32)


def _split(a):
    hi = a.astype(BF16)
    return hi, (a - hi.astype(F32)).astype(BF16)


def _hp(a2, b2):
    d = lambda p, q: jnp.dot(p, q, preferred_element_type=F32)
    return d(a2[0], b2[0]) + (d(a2[0], b2[1]) + d(a2[1], b2[0]))


def _silu(x):
    return x * jax.nn.sigmoid(x)


@jax.custom_vjp
def _stack(parts):
    return jnp.stack(parts)


_stack.defvjp(lambda parts: (jnp.stack(parts), None), lambda _, g: (tuple(g[i] for i in range(g.shape[0])),))


def _softplus(x):
    return jnp.maximum(x, 0.0) + jnp.log1p(jnp.exp(-jnp.abs(x)))


_ANY = pl.BlockSpec(memory_space=pl.ANY)


def _me():
    return lax.axis_index("x"), lax.axis_index("y"), lax.axis_index("c")


def _flat_id(pos):
    return 4 * pos[0] + 2 * pos[1] + pos[2]


def _remote(src, dst, send_sem, recv_sem, to):
    return pltpu.make_async_remote_copy(src_ref=src, dst_ref=dst, send_sem=send_sem, recv_sem=recv_sem,
                                        device_id=to, device_id_type=pl.DeviceIdType.MESH)


class _Direct:
    def __init__(self, items, bufs):
        self.items, self.bufs = list(items), list(bufs)
        self.n_src, self.n_buf = len(self.items), len(self.bufs)
        self.old = [j for j, b in enumerate(self.bufs) if not isinstance(b, jax.ShapeDtypeStruct)]
        self.args = [it[0] for it in self.items] + [self.bufs[j] for j in self.old]
        self.out_shape = [jax.ShapeDtypeStruct(b.shape, b.dtype) for b in self.bufs]
        self.scratch = [pltpu.SemaphoreType.DMA((self.n_src, N_DEV - 1)),
                        pltpu.SemaphoreType.DMA((self.n_src, N_DEV - 1)), pltpu.SemaphoreType.DMA((self.n_src,))]

    def aliases(self, in_base, out_base):
        return {in_base + self.n_src + pos: out_base + j for pos, j in enumerate(self.old)}

    def copies(self, in_refs, out_refs, sems):
        send_sems, recv_sems, local_sems = sems
        x, y, c = _me()
        me = _flat_id((x, y, c))
        peers = [(x ^ ((rel >> 2) & 1), y ^ ((rel >> 1) & 1), c ^ (rel & 1)) for rel in range(1, N_DEV)]
        local, sends, recvs = [], [], []
        for a, (_, per_dest, j, prefix) in enumerate(self.items):
            src = lambda d: in_refs[a].at[d] if per_dest else in_refs[a]
            dst = lambda s: out_refs[j].at[tuple(prefix) + (s,)]
            local.append(pltpu.make_async_copy(src(me), dst(me), local_sems.at[a]))
            for k, peer in enumerate(peers):
                pid = _flat_id(peer)
                sends.append(_remote(src(pid), dst(me), send_sems.at[a, k], recv_sems.at[a, k], peer))
                recvs.append(_remote(src(pid), dst(pid), send_sems.at[a, k], recv_sems.at[a, k], peer))
        return local, sends, recvs

    def start(self, in_refs, out_refs, sems):
        local, sends, _ = self.copies(in_refs, out_refs, sems)
        for cp in local + sends:
            cp.start()

    def wait(self, in_refs, out_refs, sems):
        local, sends, recvs = self.copies(in_refs, out_refs, sems)
        for cp in recvs:
            cp.wait_recv()
        for cp in sends:
            cp.wait_send()
        for cp in local:
            cp.wait()


def _pcall(core, *, name, grid, in_specs, out_specs, out_shape, args, sem, scratch_shapes=(), aliases=None,
           rider=None):
    n_in, n_out, n_scr = len(in_specs), len(out_specs), len(scratch_shapes)
    n_rin, n_rout = (len(rider.args), rider.n_buf) if rider else (0, 0)

    def body(*refs):
        ins, outs = refs[:n_in], refs[n_in + n_rin:n_in + n_rin + n_out]
        scr = refs[n_in + n_rin + n_out + n_rout:n_in + n_rin + n_out + n_rout + n_scr]
        if rider:
            r_refs = (refs[n_in:n_in + rider.n_src], refs[n_in + n_rin + n_out:n_in + n_rin + n_out + n_rout],
                      refs[n_in + n_rin + n_out + n_rout + n_scr:])
            ids = [pl.program_id(d) for d in range(len(grid))]
            first = functools.reduce(lambda p, q: p & q, [i == 0 for i in ids])
            last = functools.reduce(lambda p, q: p & q, [i == g - 1 for i, g in zip(ids, grid)])
            pl.when(first)(lambda: rider.start(*r_refs))
        core(ins, outs, scr)
        if rider:
            pl.when(last)(lambda: rider.wait(*r_refs))

    aliases = dict(aliases or {})
    if rider:
        sem = ("arbitrary",) * len(grid)
        aliases.update(rider.aliases(n_in, n_out))
    return pl.pallas_call(
        body, name=name, grid=grid, in_specs=list(in_specs) + [_ANY] * n_rin,
        out_specs=list(out_specs) + [_ANY] * n_rout,
        out_shape=list(out_shape) + (rider.out_shape if rider else []),
        scratch_shapes=list(scratch_shapes) + (rider.scratch if rider else []),
        input_output_aliases=aliases, compiler_params=_cparams(sem),
    )(*args, *(rider.args if rider else []))


def _exchange(direct, *, name):
    n_in = len(direct.args)

    def body(*refs):
        r_refs = refs[:direct.n_src], refs[n_in:n_in + direct.n_buf], refs[n_in + direct.n_buf:]
        direct.start(*r_refs)
        direct.wait(*r_refs)

    return pl.pallas_call(
        body, name=name, in_specs=[_ANY] * n_in, out_specs=[_ANY] * direct.n_buf, out_shape=direct.out_shape,
        input_output_aliases=direct.aliases(0, 0), scratch_shapes=direct.scratch,
    )(*direct.args)


def _matmul(a, b, *, form, tm, tn, tk, name, add=None, add_scale=1.0, extra=None, rider=None):
    if form == "nn":
        (m, kk), n = a.shape, b.shape[1]
        a_spec = pl.BlockSpec((tm, tk), lambda i, j, k: (i, k))
        b_spec = pl.BlockSpec((tk, tn), lambda i, j, k: (k, j))
        dn = (((1,), (0,)), ((), ()))
    elif form == "nt":
        (m, kk), n = a.shape, b.shape[0]
        a_spec = pl.BlockSpec((tm, tk), lambda i, j, k: (i, k))
        b_spec = pl.BlockSpec((tn, tk), lambda i, j, k: (j, k))
        dn = (((1,), (1,)), ((), ()))
    else:
        (kk, m), n = a.shape, b.shape[1]
        a_spec = pl.BlockSpec((tk, tm), lambda i, j, k: (k, i))
        b_spec = pl.BlockSpec((tk, tn), lambda i, j, k: (k, j))
        dn = (((0,), (0,)), ((), ()))
    assert m % tm == 0 and n % tn == 0 and kk % tk == 0, (name, m, n, kk)
    has_add, has_extra = add is not None, extra is not None

    def core(ins, outs, _):
        a_ref, b_ref = ins[:2]
        o_ref = outs[0]
        rest = ins[2:]
        k = pl.program_id(2)
        p = lax.dot_general(a_ref[...].astype(MM_DTYPE), b_ref[...].astype(MM_DTYPE), dn,
                            preferred_element_type=F32)

        @pl.when(k == 0)
        def _():
            first = p
            pos = 0
            if has_extra:
                first = first + _mm_nt(rest[0][...], rest[1][...])
                pos = 2
            if has_add:
                first = first + add_scale * rest[pos][...]
            o_ref[...] = first

        @pl.when(k > 0)
        def _():
            o_ref[...] += p

    in_specs = [a_spec, b_spec]
    args = [a, b]
    if has_extra:
        a2, b2, idx = extra
        in_specs += [pl.BlockSpec((tm, LANE), lambda i, j, k: (i, 0)),
                     pl.BlockSpec((tn, LANE), lambda i, j, k: (j, idx))]
        args += [a2, b2]
    if has_add:
        in_specs.append(pl.BlockSpec((tm, tn), lambda i, j, k: (i, j)))
        args.append(add)
    res = _pcall(core, name=name, grid=(m // tm, n // tn, kk // tk), in_specs=in_specs,
                 out_specs=[pl.BlockSpec((tm, tn), lambda i, j, k: (i, j))],
                 out_shape=[jax.ShapeDtypeStruct((m, n), F32)], args=args,
                 sem=("parallel", "parallel", "arbitrary"), rider=rider)
    return res if rider else res[0]


def _shift_down(x, k, row):
    return jnp.where(row >= k, pltpu.roll(x, k, 0), 0.0)


def _shift_up(x, k, row, t_len):
    return jnp.where(row < t_len - k, pltpu.roll(x, t_len - k, 0), 0.0)


def _conv_slab(x, w, row):
    return (w[3:4] * x + w[2:3] * _shift_down(x, 1, row) + w[1:2] * _shift_down(x, 2, row)
            + w[0:1] * _shift_down(x, 3, row))


def _prep_fwd(h, conv_w, *, name):
    t_len = h.shape[0]

    def body(x_ref, w_ref, o_ref):
        s = pl.program_id(0)
        row = lax.broadcasted_iota(jnp.int32, (t_len, LANE), 0)
        y = _silu(_conv_slab(x_ref[...], w_ref[...], row))
        rs = lax.rsqrt(jnp.sum(y * y, axis=-1, keepdims=True) + L2_EPS)
        scale = jnp.where(s < A_HEADS, A_HEAD_DIM ** -0.5, 1.0)
        o_ref[...] = jnp.where(s < 2 * A_HEADS, y * rs * scale, y)

    return pl.pallas_call(
        body, name=name, grid=(12,),
        in_specs=[pl.BlockSpec((t_len, LANE), lambda s: (0, L_QKV // LANE + s)),
                  pl.BlockSpec((8, LANE), lambda s: (0, s))],
        out_specs=pl.BlockSpec((t_len, LANE), lambda s: (0, s)),
        out_shape=jax.ShapeDtypeStruct((t_len, 3 * A_WIDTH), F32),
        compiler_params=_cparams(("parallel",)),
    )(h, conv_w)


def _prep_bwd(h, conv_w, d_out, dh, *, name):
    t_len = h.shape[0]

    def body(x_ref, w_ref, g_ref, dh_in, dx_ref, dw_ref):
        del dh_in
        s = pl.program_id(0)
        row = lax.broadcasted_iota(jnp.int32, (t_len, LANE), 0)
        x = x_ref[...]
        w = w_ref[...]
        c = _conv_slab(x, w, row)
        sg = jax.nn.sigmoid(c)
        y = c * sg
        g = g_ref[0]
        rs = lax.rsqrt(jnp.sum(y * y, axis=-1, keepdims=True) + L2_EPS)
        scale = jnp.where(s < A_HEADS, A_HEAD_DIM ** -0.5, 1.0)
        dy_n = scale * (rs * g - y * (rs * rs * rs) * jnp.sum(g * y, axis=-1, keepdims=True))
        dy = jnp.where(s < 2 * A_HEADS, dy_n, g)
        dc = dy * (sg * (1.0 + c * (1.0 - sg)))
        dx_ref[...] = (w[3:4] * dc + w[2:3] * _shift_up(dc, 1, row, t_len)
                       + w[1:2] * _shift_up(dc, 2, row, t_len) + w[0:1] * _shift_up(dc, 3, row, t_len))
        dws = [jnp.sum(dc * _shift_down(x, 3 - j, row), axis=0, keepdims=True) if j < 3
               else jnp.sum(dc * x, axis=0, keepdims=True) for j in range(CONV_K)]
        dw_ref[...] = jnp.concatenate(dws + [jnp.zeros((8 - CONV_K, LANE), F32)], axis=0)

    slab = pl.BlockSpec((t_len, LANE), lambda s: (0, L_QKV // LANE + s))
    return pl.pallas_call(
        body, name=name, grid=(12,),
        in_specs=[slab, pl.BlockSpec((8, LANE), lambda s: (0, s)),
                  pl.BlockSpec((1, t_len, LANE), lambda s: (s // A_HEADS, 0, s % A_HEADS)), _ANY],
        out_specs=[slab, pl.BlockSpec((8, LANE), lambda s: (0, s))],
        out_shape=[jax.ShapeDtypeStruct((t_len, L_MAIN), F32), jax.ShapeDtypeStruct((8, 3 * A_WIDTH), F32)],
        input_output_aliases={3: 0},
        compiler_params=_cparams(("parallel",)),
    )(h, conv_w, d_out, dh)


N_LEVELS = 5
MF_TRIL, MF_STRIL, MF_DIAG8, MF_LOW16, MF_EYE = 0, 1, 2, 3, 3 + N_LEVELS
MB_CUM, MB_CUM_T, MB_TOT = 0, 1, 2


def _gdn_masks():
    r = lax.broadcasted_iota(jnp.int32, (SUPER, SUPER), 0)
    c = lax.broadcasted_iota(jnp.int32, (SUPER, SUPER), 1)
    same = lambda shift: (r >> shift) == (c >> shift)
    ninf = lambda m: jnp.where(m, 0.0, -jnp.inf).astype(F32)
    one = lambda m: m.astype(F32)
    mf = jnp.stack([ninf(r >= c), ninf(r > c), one(same(3))]
                   + [one(same(4 + lv) & jnp.logical_not(same(3 + lv))) for lv in range(N_LEVELS)] + [one(r == c)])
    mb = jnp.stack([one(r >= c), one(r <= c), jnp.ones((SUPER, SUPER), F32)]).astype(BF16)
    return mf, mb


def _tri_inv_impl(a, mf):
    d = lambda p, q: jnp.dot(p.astype(BF16), q.astype(BF16), preferred_element_type=F32)
    dd = lambda p, q: jnp.dot(p, q, preferred_element_type=F32)
    eye = mf[MF_EYE]
    a0 = a * mf[MF_DIAG8]
    a2 = d(a0, a0)
    a4 = d(a2, a2)
    t = d(d(eye - a0, eye + a2), eye + a4)
    for level in range(N_LEVELS):
        t = t - d(d(t, a * mf[MF_LOW16 + level]), t)
    a_hi, a_lo = _split(a)
    for _ in range(NEWTON_STEPS):
        t_hi, t_lo = _split(t)
        resid = (eye - t) - (dd(a_hi, t_hi) + (dd(a_hi, t_lo) + dd(a_lo, t_hi)))
        r_hi, r_lo = _split(resid)
        t = t + (dd(t_hi, r_hi) + dd(t_hi, r_lo))
    return t


@jax.custom_vjp
def _wy_apply(a, rhs, t):
    return _mm(t, rhs)


def _wy_apply_fwd(a, rhs, t):
    x = _mm(t, rhs)
    return x, (t, x)


def _wy_apply_bwd(res, dx):
    t, x = res
    d_rhs = _mm_tn(t, dx)
    return -_mm_nt(d_rhs, x), d_rhs, jnp.zeros_like(t)


_wy_apply.defvjp(_wy_apply_fwd, _wy_apply_bwd)


@functools.partial(jax.custom_vjp, nondiff_argnums=(1,))
def _lane_roll(x, shift):
    return pltpu.roll(x, shift % LANE, 1)


_lane_roll.defvjp(lambda x, shift: (_lane_roll(x, shift), None), lambda shift, _, g: (_lane_roll(g, -shift),))


def _mask_times_lanes(x, mask):
    lane = lax.broadcasted_iota(jnp.int32, (1, LANE), 1)
    x = jnp.where(lane < A_HEADS, x, 0.0)
    x1 = x.astype(BF16).astype(F32)
    x2 = (x - x1).astype(BF16).astype(F32)
    x3 = (x - x1 - x2).astype(BF16).astype(F32)
    pieces = x1 + pltpu.roll(x2, A_HEADS, 1) + pltpu.roll(x3, 2 * A_HEADS, 1)
    res = jnp.dot(mask, pieces.astype(BF16), preferred_element_type=F32)
    return res + pltpu.roll(res, LANE - A_HEADS, 1) + pltpu.roll(res, LANE - 2 * A_HEADS, 1)


@jax.custom_vjp
def _chunk_sums(g, mb):
    return _mask_times_lanes(g, mb[MB_CUM]), _mask_times_lanes(g, mb[MB_TOT])


def _chunk_sums_fwd(g, mb):
    return _chunk_sums(g, mb), mb


def _chunk_sums_bwd(mb, d):
    lane = lax.broadcasted_iota(jnp.int32, (1, LANE), 1)
    dg = _mask_times_lanes(d[0], mb[MB_CUM_T]) + _mask_times_lanes(d[1], mb[MB_TOT])
    return jnp.where(lane < A_HEADS, dg, 0.0), jnp.zeros_like(mb)


_chunk_sums.defvjp(_chunk_sums_fwd, _chunk_sums_bwd)


def _gdn_gates(ba, alog, dtb, mb):
    beta = jax.nn.sigmoid(ba)
    g = -jnp.exp(alog) * _softplus(_lane_roll(ba, -A_HEADS) + dtb)
    gc, gl = _chunk_sums(g, mb)
    return beta, gc, gl, gc.T


def _gdn_block(s, q, k, v, z, gates, nw, h, t_known, mf):
    n = q.shape[0]
    beta_all, gc_all, gl_all, gct_all = gates
    lane = lax.broadcasted_iota(jnp.int32, (1, LANE), 1)
    sub = lax.broadcasted_iota(jnp.int32, (LANE, 1), 0)
    col = lambda x: jnp.sum(jnp.where(lane == h, x, 0.0), axis=1, keepdims=True)
    wide = lambda c: jnp.broadcast_to(c, (n, LANE))
    gc, gl = col(gc_all), col(gl_all)
    gc_row = jnp.sum(jnp.where(sub == h, gct_all, 0.0), axis=0, keepdims=True)
    beta_w, eg_w = wide(col(beta_all)), wide(jnp.exp(gc))
    diff = gc - gc_row
    decay = jnp.exp(diff + mf[MF_TRIL])
    kb = k * beta_w
    a_mat = _mm_nt(kb, k) * jnp.exp(diff + mf[MF_STRIL])
    rhs = jnp.concatenate([v * beta_w, kb * eg_w], axis=1)
    if t_known is None:
        t_mat = _tri_inv_impl(a_mat, mf)
        uw = _mm(t_mat, rhs)
    else:
        t_mat = t_known
        uw = _wy_apply(a_mat, rhs, t_known)
    u, w = uw[:, :LANE], uw[:, LANE:]
    qk = _mm_nt(q, k) * decay
    q_dec = q * eg_w
    k_dec = k * wide(jnp.exp(gl - gc))
    v_new = u - _mm(w, s)
    o = _mm(q_dec, s) + _mm(qk, v_new)
    s = s * jnp.exp(gl[0:1]) + _mm_tn(k_dec, v_new)
    o = o * lax.rsqrt(jnp.mean(o * o, axis=-1, keepdims=True) + RMS_EPS) * nw
    return o * _silu(z), s, t_mat


def _gdn_fwd(qkv, h, alog, dtb, nw, ycat, *, name, rider=None):
    t_len = qkv.shape[0]
    nsc = t_len // SUPER

    def core(ins, outs, scr):
        q_ref, k_ref, v_ref, z_ref, ba_ref, al_ref, dt_ref, nw_ref, mf_ref, mb_ref, _ = ins
        y_ref, sin_ref, t_ref = outs
        s_scr, = scr
        sc, pg = pl.program_id(0), pl.program_id(1)
        heads = [pg * G_HEADS + hh for hh in range(G_HEADS)]

        @pl.when(sc == 0)
        def _():
            for hd in heads:
                s_scr[hd] = jnp.zeros((A_HEAD_DIM, A_HEAD_DIM), F32)

        per_head = lambda ref: jnp.stack([ref[:, hh * LANE:(hh + 1) * LANE] for hh in range(G_HEADS)])
        states = jnp.stack([s_scr[hd] for hd in heads])
        gates = _gdn_gates(ba_ref[...], al_ref[...], dt_ref[...], mb_ref[...])
        fn = jax.vmap(_gdn_block, in_axes=(0, 0, 0, 0, 0, None, None, 0, None, None))
        y, s_new, t_mat = fn(states, per_head(q_ref), per_head(k_ref), per_head(v_ref), per_head(z_ref),
                             gates, nw_ref[...], jnp.stack(heads), None, mf_ref[...])
        sin_ref[0] = states
        t_ref[0] = t_mat
        for hh, hd in enumerate(heads):
            y_ref[:, hh * LANE:(hh + 1) * LANE] = y[hh]
            s_scr[hd] = s_new[hh]

    gw = G_HEADS * LANE
    blk = lambda off: pl.BlockSpec((SUPER, gw), lambda sc, pg: (sc, off // G_HEADS + pg))
    row = pl.BlockSpec((1, LANE), lambda sc, pg: (0, 0))
    mf, mb = _gdn_masks()
    whole = lambda a: pl.BlockSpec(a.shape, lambda sc, pg: (0, 0, 0))
    return _pcall(
        core, name=name, grid=(nsc, A_HEADS // G_HEADS),
        in_specs=[blk(0), blk(4), blk(8), blk(L_ZA // LANE),
                  pl.BlockSpec((SUPER, LANE), lambda sc, pg: (sc, L_BA // LANE)), row, row, row,
                  whole(mf), whole(mb), _ANY],
        out_specs=[blk(0),
                   pl.BlockSpec((1, G_HEADS, A_HEAD_DIM, A_HEAD_DIM), lambda sc, pg: (sc, pg, 0, 0)),
                   pl.BlockSpec((1, G_HEADS, SUPER, SUPER), lambda sc, pg: (sc, pg, 0, 0))],
        out_shape=[jax.ShapeDtypeStruct((t_len, D_MODEL), F32),
                   jax.ShapeDtypeStruct((nsc, A_HEADS, A_HEAD_DIM, A_HEAD_DIM), F32),
                   jax.ShapeDtypeStruct((nsc, A_HEADS, SUPER, SUPER), F32)],
        scratch_shapes=[pltpu.VMEM((A_HEADS, A_HEAD_DIM, A_HEAD_DIM), F32)],
        aliases={10: 0}, sem=("arbitrary", "arbitrary"), rider=rider,
        args=(qkv, qkv, qkv, h, h, alog, dtb, nw, mf, mb, ycat))


def _gdn_bwd(qkv, h, alog, dtb, nw, s_in, t_in, dycat, dh, *, name, rider=None):
    t_len = qkv.shape[0]
    nsc = t_len // SUPER

    def core(ins, outs, scr):
        (q_ref, k_ref, v_ref, z_ref, ba_ref, al_ref, dt_ref, nw_ref, sin_ref, t_ref, dy_ref, mf_ref, mb_ref,
         _) = ins
        dz_ref, dqkv_ref, dba_ref, dal_ref, ddt_ref, dnw_ref = outs
        ds_scr, = scr
        i, pg = pl.program_id(0), pl.program_id(1)

        @pl.when((i == 0) & (pg == 0))
        def _():
            dal_ref[...] = jnp.zeros_like(dal_ref)
            ddt_ref[...] = jnp.zeros_like(ddt_ref)
            dnw_ref[...] = jnp.zeros_like(dnw_ref)

        @pl.when(pg == 0)
        def _():
            dba_ref[...] = jnp.zeros_like(dba_ref)

        heads = [pg * G_HEADS + hh for hh in range(G_HEADS)]

        @pl.when(i == 0)
        def _():
            for hd in heads:
                ds_scr[hd] = jnp.zeros((A_HEAD_DIM, A_HEAD_DIM), F32)

        per_head = lambda ref: jnp.stack([ref[:, hh * LANE:(hh + 1) * LANE] for hh in range(G_HEADS)])
        d_states = jnp.stack([ds_scr[hd] for hd in heads])
        head_ids = jnp.stack(heads)
        t_known, mf, mb = t_ref[0], mf_ref[...], mb_ref[...]

        def fn(s, q, k, v, z, ba, alog, dtb, nw):
            gates = _gdn_gates(ba, alog, dtb, mb)
            one = lambda s, q, k, v, z, t, h: _gdn_block(s, q, k, v, z, gates, nw, h, t, mf)[:2]
            return jax.vmap(one)(s, q, k, v, z, t_known, head_ids)

        _, vjp = jax.vjp(fn, sin_ref[0], per_head(q_ref), per_head(k_ref), per_head(v_ref), per_head(z_ref),
                         ba_ref[...], al_ref[...], dt_ref[...], nw_ref[...])
        ds, dq, dk, dv, dz, dba, dal, ddt, dnw = vjp((per_head(dy_ref), d_states))
        for hh, hd in enumerate(heads):
            cols = slice(hh * LANE, (hh + 1) * LANE)
            ds_scr[hd] = ds[hh]
            dqkv_ref[0, :, cols] = dq[hh]
            dqkv_ref[1, :, cols] = dk[hh]
            dqkv_ref[2, :, cols] = dv[hh]
            dz_ref[:, cols] = dz[hh]
        dba_ref[...] += dba
        dal_ref[...] += dal
        ddt_ref[...] += ddt
        dnw_ref[...] += dnw

    rev = lambda i: nsc - 1 - i
    gw = G_HEADS * LANE
    blk = lambda off: pl.BlockSpec((SUPER, gw), lambda i, pg: (rev(i), off // G_HEADS + pg))
    row = pl.BlockSpec((1, LANE), lambda i, pg: (0, 0))
    ba_blk = lambda off: pl.BlockSpec((SUPER, LANE), lambda i, pg: (rev(i), off))
    mf, mb = _gdn_masks()
    whole = lambda a: pl.BlockSpec(a.shape, lambda i, pg: (0, 0, 0))
    return _pcall(
        core, name=name, grid=(nsc, A_HEADS // G_HEADS),
        in_specs=[blk(0), blk(4), blk(8), blk(L_ZA // LANE), ba_blk(L_BA // LANE), row, row, row,
                  pl.BlockSpec((1, G_HEADS, A_HEAD_DIM, A_HEAD_DIM), lambda i, pg: (rev(i), pg, 0, 0)),
                  pl.BlockSpec((1, G_HEADS, SUPER, SUPER), lambda i, pg: (rev(i), pg, 0, 0)),
                  blk(0), whole(mf), whole(mb), _ANY],
        out_specs=[blk(L_ZA // LANE),
                   pl.BlockSpec((3, SUPER, gw), lambda i, pg: (0, rev(i), pg)),
                   ba_blk(0), row, row, row],
        out_shape=[jax.ShapeDtypeStruct((t_len, L_MAIN), F32), jax.ShapeDtypeStruct((3, t_len, A_WIDTH), F32),
                   jax.ShapeDtypeStruct((t_len, LANE), F32)] + [jax.ShapeDtypeStruct((1, LANE), F32)] * 3,
        scratch_shapes=[pltpu.VMEM((A_HEADS, A_HEAD_DIM, A_HEAD_DIM), F32)],
        aliases={13: 0}, sem=("arbitrary", "arbitrary"), rider=rider,
        args=(qkv, qkv, qkv, h, h, alog, dtb, nw, s_in, t_in, dycat, mf, mb, dh))


def _swa_block(q, kp, kc, vp, vc, z, sinks, first):
    rows = B_GROUP * BLOCK
    ri = lax.broadcasted_iota(jnp.int32, (rows, 2 * BLOCK), 0)
    si = lax.broadcasted_iota(jnp.int32, (rows, 2 * BLOCK), 1)
    dist = (ri & (BLOCK - 1)) + BLOCK - si
    bias = jnp.where((dist >= 0) & (dist < WINDOW) & ((si >= BLOCK) | jnp.logical_not(first)), 0.0, -jnp.inf)
    dist_f = dist.astype(F32)
    head_of_row = lax.broadcasted_iota(jnp.int32, (rows, 1), 0) >> 7

    def group(j):
        cs = slice(j * B_HEAD_DIM, (j + 1) * B_HEAD_DIM)
        heads = range(j * B_GROUP, (j + 1) * B_GROUP)
        qs = jnp.concatenate([q[:, hq * B_HEAD_DIM:(hq + 1) * B_HEAD_DIM] for hq in heads], axis=0)
        kk = jnp.concatenate([kp[:, cs], kc[:, cs]], axis=0)
        vv = jnp.concatenate([vp[:, cs], vc[:, cs]], axis=0)
        sink = jnp.concatenate([jnp.broadcast_to(sinks[:, hq:hq + 1], (BLOCK, 1)) for hq in heads], axis=0)
        slope = sum(jnp.where(head_of_row == gi, 2.0 ** (-8.0 * (hq + 1) / B_Q_HEADS), 0.0)
                    for gi, hq in enumerate(heads))
        return qs, kk, vv, sink, slope

    def attend(qs, kk, vv, sink, slope):
        sc = _mm_nt(qs, kk) * (B_HEAD_DIM ** -0.5) - slope * dist_f + bias
        m = lax.stop_gradient(jnp.maximum(jnp.max(sc, axis=-1, keepdims=True), sink))
        p = jnp.exp(sc - m)
        inv = 1.0 / (jnp.sum(p, axis=-1, keepdims=True) + jnp.exp(sink - m))
        return _mm(p * inv, vv)

    o = jax.vmap(attend)(*[_stack(t) for t in zip(*[group(j) for j in range(B_KV_HEADS)])])
    outs = [o[j, gi * BLOCK:(gi + 1) * BLOCK] for j in range(B_KV_HEADS) for gi in range(B_GROUP)]
    return jnp.concatenate(outs, axis=1) * _silu(z)


def _swa_specs(idx):
    wide = lambda off: pl.BlockSpec((BLOCK, B_WIDTH), lambda n: (idx(n), off))
    cur = lambda off: pl.BlockSpec((BLOCK, LANE), lambda n: (idx(n), off))
    prev = lambda off: pl.BlockSpec((BLOCK, LANE), lambda n: (jnp.maximum(idx(n) - 1, 0), off))
    return [wide(L_QB // B_WIDTH), prev(L_KB // LANE), cur(L_KB // LANE), prev(L_VB // LANE), cur(L_VB // LANE),
            wide(L_ZB // B_WIDTH), pl.BlockSpec((1, LANE), lambda n: (0, 0))]


def _swa_fwd(h, sinks, *, name, rider=None):
    t_len = h.shape[0]
    nb = t_len // BLOCK

    def core(ins, outs, _):
        q_ref, kp_ref, kc_ref, vp_ref, vc_ref, z_ref, s_ref = ins
        outs[0][...] = _swa_block(q_ref[...], kp_ref[...], kc_ref[...], vp_ref[...], vc_ref[...], z_ref[...],
                                  s_ref[...], pl.program_id(0) == 0)

    res = _pcall(core, name=name, grid=(nb,), in_specs=_swa_specs(lambda n: n),
                 out_specs=[pl.BlockSpec((BLOCK, B_WIDTH), lambda n: (n, 1))],
                 out_shape=[jax.ShapeDtypeStruct((t_len, D_MODEL), F32)], sem=("parallel",), rider=rider,
                 args=(h, h, h, h, h, h, sinks))
    return res if rider else res[0]


def _swa_bwd(h, sinks, dycat, *, name, rider=None):
    t_len = h.shape[0]
    nb = t_len // BLOCK

    def core(ins, outs, scr):
        q_ref, kp_ref, kc_ref, vp_ref, vc_ref, z_ref, s_ref, dy_ref = ins
        dh_ref, dsk_ref = outs
        ck_scr, cv_scr = scr
        i = pl.program_id(0)
        n = nb - 1 - i

        @pl.when(i == 0)
        def _():
            ck_scr[...] = jnp.zeros_like(ck_scr)
            cv_scr[...] = jnp.zeros_like(cv_scr)
            dsk_ref[...] = jnp.zeros_like(dsk_ref)

        fn = functools.partial(_swa_block, first=(n == 0))
        _, vjp = jax.vjp(fn, q_ref[...], kp_ref[...], kc_ref[...], vp_ref[...], vc_ref[...], z_ref[...], s_ref[...])
        dq, dkp, dkc, dvp, dvc, dz, dsk = vjp(dy_ref[...])
        dh_ref[:, L_QB:L_QB + B_WIDTH] = dq
        dh_ref[:, L_ZB:L_ZB + B_WIDTH] = dz
        dh_ref[:, L_ZA:L_ZA + A_WIDTH] = jnp.zeros((BLOCK, A_WIDTH), F32)
        dh_ref[:, L_KB:L_KB + LANE] = dkc + ck_scr[...]
        dh_ref[:, L_VB:L_VB + LANE] = dvc + cv_scr[...]
        ck_scr[...] = dkp
        cv_scr[...] = dvp
        dsk_ref[...] += dsk

    rev = lambda i: nb - 1 - i
    return _pcall(
        core, name=name, grid=(nb,),
        in_specs=_swa_specs(rev) + [pl.BlockSpec((BLOCK, B_WIDTH), lambda i: (rev(i), 1))],
        out_specs=[pl.BlockSpec((BLOCK, L_SWA), lambda i: (rev(i), 0)), pl.BlockSpec((1, LANE), lambda i: (0, 0))],
        out_shape=[jax.ShapeDtypeStruct((t_len, L_MAIN), F32), jax.ShapeDtypeStruct((1, LANE), F32)],
        scratch_shapes=[pltpu.VMEM((BLOCK, LANE), F32), pltpu.VMEM((BLOCK, LANE), F32)],
        sem=("arbitrary",), rider=rider, args=(h, h, h, h, h, h, sinks, dycat))


def _out_ln_fwd(ycat, w_out, x, ln_g, ln_b, *, name, tm=256):
    t_len = x.shape[0]

    def body(y_ref, w_ref, x_ref, g_ref, b_ref, r_ref, o_ref):
        r = DEEPNORM_ALPHA * x_ref[...] + _mm(y_ref[...], w_ref[...])
        r_ref[...] = r
        mu = jnp.mean(r, axis=-1, keepdims=True)
        d = r - mu
        var = jnp.mean(d * d, axis=-1, keepdims=True)
        o_ref[...] = d * lax.rsqrt(var + LN_EPS) * g_ref[...] + b_ref[...]

    tile = pl.BlockSpec((tm, D_MODEL), lambda i: (i, 0))
    vec = pl.BlockSpec((1, D_MODEL), lambda i: (0, 0))
    return pl.pallas_call(
        body, name=name, grid=(t_len // tm,),
        in_specs=[tile, pl.BlockSpec((D_MODEL, D_MODEL), lambda i: (0, 0)), tile, vec, vec],
        out_specs=[tile, tile],
        out_shape=[jax.ShapeDtypeStruct((t_len, D_MODEL), F32)] * 2,
        compiler_params=_cparams(("parallel",)),
    )(ycat, w_out, x, ln_g, ln_b)


def _ln_bwd(dxn, r, ln_g, *, name, tm=256):
    t_len = r.shape[0]

    def body(dx_ref, r_ref, g_ref, dr_ref, dg_ref, db_ref):
        @pl.when(pl.program_id(0) == 0)
        def _():
            dg_ref[...] = jnp.zeros_like(dg_ref)
            db_ref[...] = jnp.zeros_like(db_ref)

        rr = r_ref[...]
        dx = dx_ref[...]
        mu = jnp.mean(rr, axis=-1, keepdims=True)
        d = rr - mu
        rstd = lax.rsqrt(jnp.mean(d * d, axis=-1, keepdims=True) + LN_EPS)
        xh = d * rstd
        dxh = dx * g_ref[...]
        dr_ref[...] = rstd * (dxh - jnp.mean(dxh, axis=-1, keepdims=True)
                              - xh * jnp.mean(dxh * xh, axis=-1, keepdims=True))
        dg_ref[...] += jnp.sum(dx * xh, axis=0, keepdims=True)
        db_ref[...] += jnp.sum(dx, axis=0, keepdims=True)

    tile = pl.BlockSpec((tm, D_MODEL), lambda i: (i, 0))
    vec = pl.BlockSpec((1, D_MODEL), lambda i: (0, 0))
    return pl.pallas_call(
        body, name=name, grid=(t_len // tm,),
        in_specs=[tile, tile, vec], out_specs=[tile, vec, vec],
        out_shape=[jax.ShapeDtypeStruct((t_len, D_MODEL), F32), jax.ShapeDtypeStruct((1, D_MODEL), F32),
                   jax.ShapeDtypeStruct((1, D_MODEL), F32)],
        compiler_params=_cparams(("arbitrary",)),
    )(dxn, r, ln_g)


def _loss_head(y, target, *, name, tm=256):
    t_len = y.shape[0]

    def body(y_ref, t_ref, d_ref, l_ref):
        @pl.when(pl.program_id(0) == 0)
        def _():
            l_ref[...] = jnp.zeros_like(l_ref)

        e = y_ref[...] - t_ref[...]
        d_ref[...] = e * (1.0 / D_MODEL)
        l_ref[...] += jnp.sum(e * e, axis=0, keepdims=True)

    tile = pl.BlockSpec((tm, D_MODEL), lambda i: (i, 0))
    vec = pl.BlockSpec((1, D_MODEL), lambda i: (0, 0))
    return pl.pallas_call(
        body, name=name, grid=(t_len // tm,), in_specs=[tile, tile], out_specs=[tile, vec],
        out_shape=[jax.ShapeDtypeStruct((t_len, D_MODEL), F32), jax.ShapeDtypeStruct((1, D_MODEL), F32)],
        compiler_params=_cparams(("arbitrary",)),
    )(y, target)


def _pad_row(v):
    return jnp.zeros((1, LANE), F32).at[0, :v.shape[0]].set(v)


def _to_layout(w_full):
    s = lambda a, b: w_full[..., a:b]
    pad = jnp.zeros(w_full.shape[:-1] + (LANE - 2 * A_HEADS,), w_full.dtype)
    return jnp.concatenate([s(2056, 2568), s(2824, 3336), s(1536, 2048), s(2568, 2696), s(2696, 2824), s(0, 1536),
                            s(2048, 2056), pad], axis=-1)


def _from_layout(g_main, g_ba):
    s = lambda a, b: g_main[..., a:b]
    return jnp.concatenate([s(L_QKV, L_QKV + 1536), s(L_ZA, L_ZA + 512), g_ba[..., :2 * A_HEADS],
                            s(L_QB, L_QB + 512), s(L_KB, L_KB + 128), s(L_VB, L_VB + 128), s(L_ZB, L_ZB + 512)],
                           axis=-1)


def _as_list(r):
    return list(r) if isinstance(r, (list, tuple)) else [r]


def _gathered(shard):
    return jax.ShapeDtypeStruct((N_DEV,) + shard.shape, shard.dtype)


def _full_w_in(g_in):
    return _to_layout(g_in.transpose(1, 0, 2).reshape(D_MODEL, IN_COLS))


def _full_conv(g_conv):
    return jnp.pad(g_conv.transpose(1, 0, 2).reshape(CONV_K, 3 * A_WIDTH), ((0, 8 - CONV_K), (0, 0)))


def _forward(x, weights, shards, small):
    a_log, dt_bias, norm_w, sinks, ln_g, ln_b = small
    tm = min(512, x.shape[0])
    saved, weights = [], [list(w) for w in weights]
    whole = lambda arrs: _Direct([(a, False, j, ()) for j, a in enumerate(arrs)], [_gathered(a) for a in arrs])
    for l in range(DEPTH):
        rider = whole(shards[l][1:]) if weights[l][1] is None else None
        h, *got = _as_list(_matmul(x, weights[l][0], form="nn", tm=tm, tn=1152, tk=D_MODEL, name=f"in_proj_{l}",
                                   rider=rider))
        if rider:
            weights[l][1:] = [got[0].reshape(D_MODEL, D_MODEL), _full_conv(got[1])]
        w_in_l, w_out_l, conv_l = weights[l]
        qkv = _prep_fwd(h, conv_l, name=f"prep_fwd_{l}")
        al, dt, nw, sk = _pad_row(a_log[l]), _pad_row(dt_bias[l]), norm_w[l][None, :], _pad_row(sinks[l])
        ahead = l + 1 < DEPTH and weights[l + 1][0] is None
        rider = whole(shards[l + 1][1:]) if ahead else None
        ycat, *got = _as_list(_swa_fwd(h, sk, name=f"swa_fwd_{l}", rider=rider))
        if ahead:
            weights[l + 1][1:] = [got[0].reshape(D_MODEL, D_MODEL), _full_conv(got[1])]
        rider = whole(shards[l + 1][:1]) if ahead else None
        ycat, s_in, t_in, *got = _gdn_fwd(qkv, h, al, dt, nw, ycat, name=f"gdn_fwd_{l}", rider=rider)
        if ahead:
            weights[l + 1][0] = _full_w_in(got[0])
        r, xn = _out_ln_fwd(ycat, w_out_l, x, ln_g[l][None, :], ln_b[l][None, :], name=f"out_ln_{l}")
        saved.append((x, h, qkv, s_in, t_in, ycat, r, al, dt, nw, sk))
        x = xn
    return x, saved, weights


def _contributions(g):
    c_in = g["w_in"].reshape(D_MODEL, N_DEV, SHARD_COLS).transpose(1, 0, 2).astype(BF16)
    c_out = g["w_out"].astype(BF16).reshape(N_DEV, OUT_SHARD_ROWS, D_MODEL)
    c_conv = g["conv_w"].reshape(CONV_K, N_DEV, CONV_SHARD_COLS).transpose(1, 0, 2)
    c_small = [jnp.broadcast_to(g[n][None], (N_DEV,) + g[n].shape) for n, _ in SMALL_SIZES]
    return c_in, c_out, _pack_small(c_conv, c_small)


def _backward_layer(l, dx, saved_l, weights_l, ln_g_l, above=None):
    x_in, h, qkv, s_in, t_in, ycat, r, al, dt, nw, sk = saved_l
    w_in_l, w_out_l, conv_l = weights_l
    tm = min(512, x_in.shape[0])
    dr, d_lng, d_lnb = _ln_bwd(dx, r, ln_g_l[None, :], name=f"ln_bwd_{l}")
    dycat = _matmul(dr, w_out_l, form="nt", tm=tm, tn=D_MODEL, tk=D_MODEL, name=f"out_proj_dx_{l}")
    d_wout = _matmul(ycat, dr, form="tn", tm=512, tn=D_MODEL, tk=tm, name=f"out_proj_dw_{l}")
    rider, p_in, p_out, p_small = None, None, None, None
    recv = lambda c: jax.ShapeDtypeStruct((DEPTH,) + c.shape, c.dtype)
    if above:
        c_out = d_wout.astype(BF16).reshape(N_DEV, OUT_SHARD_ROWS, D_MODEL)
        rider = _Direct([(above[1], True, 0, (l + 1,)), (above[2], True, 1, (l + 1,)), (c_out, True, 0, (l,))],
                        [recv(above[1]), recv(above[2])])
    dh, d_sk, *got = _swa_bwd(h, sk, dycat, name=f"swa_bwd_{l}", rider=rider)
    if above:
        p_out, p_small = got
        rider = _Direct([(above[0], True, 0, (l + 1,))], [recv(above[0])])
    dh, dqkv_n, dba, d_al, d_dt, d_nw, *got = _gdn_bwd(qkv, h, al, dt, nw, s_in, t_in, dycat, dh,
                                                       name=f"gdn_bwd_{l}", rider=rider)
    dh, d_conv = _prep_bwd(h, conv_l, dqkv_n, dh, name=f"prep_bwd_{l}")
    d_win_main = _matmul(x_in, dh, form="tn", tm=512, tn=L_MAIN // 2, tk=tm, name=f"in_proj_dw_{l}")
    d_win_ba = _matmul(x_in, dba, form="tn", tm=D_MODEL, tn=LANE, tk=tm, name=f"in_proj_dw_ba_{l}")
    grads = dict(w_in=_from_layout(d_win_main, d_win_ba), w_out=d_wout, conv_w=d_conv[:CONV_K],
                 a_log=d_al[0, :A_HEADS], dt_bias=d_dt[0, :A_HEADS], norm_w=d_nw[0], sinks=d_sk[0, :B_Q_HEADS],
                 ln_g=d_lng[0], ln_b=d_lnb[0])
    if above:
        p_in, = got
        c_in, _, c_small = _contributions(grads)
        rider = _Direct([(c_in, True, 0, (l,)), (c_small, True, 1, (l,))], [p_in, p_small])
    dx, *got = _as_list(_matmul(dh, w_in_l, form="nt", tm=tm, tn=D_MODEL, tk=L_MAIN // 2, name=f"in_proj_dx_{l}",
                                add=dr, add_scale=DEEPNORM_ALPHA, extra=(dba, w_in_l, L_BA // LANE), rider=rider))
    bufs = (got[0], p_out, got[1]) if above else None
    return dx, grads, bufs


def _all_gather(shards, *, name):
    n_arr = len(shards)

    def body(*refs):
        x_refs, out_refs = refs[:n_arr], refs[n_arr:2 * n_arr]
        send_sems, recv_sems, local_sems = refs[2 * n_arr:]
        x, y, c = _me()
        me, sibling = (x, y, c), (x, y, 1 - c)
        chips = [(1 - x, y), (x, 1 - y), (1 - x, 1 - y)]

        def copy(a, k, block, to, src=None):
            dst = out_refs[a].at[_flat_id(block)]
            return _remote(dst if src is None else src, dst, send_sems.at[a, k], recv_sems.at[a, k], to)

        mine = [pltpu.make_async_copy(x_refs[a], out_refs[a].at[_flat_id(me)], local_sems.at[a])
                for a in range(n_arr)]
        for cp in mine:
            cp.start()
        first = []
        for a in range(n_arr):
            first.append(copy(a, 0, me, sibling, src=x_refs[a]))
            first += [copy(a, 1 + j, me, (*chip, c), src=x_refs[a]) for j, chip in enumerate(chips)]
        for cp in first:
            cp.start()
        passed = []
        for j, chip in enumerate(chips):
            for a in range(n_arr):
                copy(a, 1 + j, (*chip, c), me).wait_recv()
                fwd = copy(a, 4 + j, (*chip, c), sibling)
                fwd.start()
                passed.append(fwd)
        for a in range(n_arr):
            copy(a, 0, sibling, me).wait_recv()
            for j, chip in enumerate(chips):
                copy(a, 4 + j, (*chip, 1 - c), me).wait_recv()
        for cp in first + passed:
            cp.wait_send()
        for cp in mine:
            cp.wait()

    return pl.pallas_call(
        body, name=name, in_specs=[_ANY] * n_arr, out_specs=[_ANY] * n_arr,
        out_shape=[jax.ShapeDtypeStruct((N_DEV,) + s.shape, s.dtype) for s in shards],
        scratch_shapes=[pltpu.SemaphoreType.DMA((n_arr, N_DEV - 1)), pltpu.SemaphoreType.DMA((n_arr, N_DEV - 1)),
                        pltpu.SemaphoreType.DMA((n_arr,))],
    )(*shards)


def _adamw(parts, w, m, v, *, tr, name):
    depth, rows, cols = w.shape
    c1 = 1.0 - ADAM_B1 ** ADAM_STEP
    c2 = 1.0 - ADAM_B2 ** ADAM_STEP

    def body(g_ref, w_ref, m_ref, v_ref, go_ref, d_ref, mo_ref, vo_ref):
        g = g_ref[0, 0].astype(F32)
        for s in range(1, N_DEV):
            g = g + g_ref[0, s].astype(F32)
        m_new = ADAM_B1 * m_ref[0] + (1.0 - ADAM_B1) * g
        v_new = ADAM_B2 * v_ref[0] + (1.0 - ADAM_B2) * (g * g)
        go_ref[0] = g
        mo_ref[0] = m_new
        vo_ref[0] = v_new
        d_ref[0] = -ADAM_LR * ((m_new / c1) / (jnp.sqrt(v_new / c2) + ADAM_EPS) + ADAM_WD * w_ref[0])

    tile = pl.BlockSpec((1, tr, cols), lambda l, i: (l, i, 0))
    return pl.pallas_call(
        body, name=name, grid=(depth, rows // tr),
        in_specs=[pl.BlockSpec((1, N_DEV, tr, cols), lambda l, i: (l, 0, i, 0)), tile, tile, tile],
        out_specs=[tile] * 4, out_shape=[jax.ShapeDtypeStruct(w.shape, F32)] * 4,
        compiler_params=_cparams(("parallel", "parallel")),
    )(parts, w, m, v)


def _pack_small(conv, small):
    lead = conv.shape[:-2]
    flat = jnp.concatenate([conv.reshape(lead + (CS_CONV,))] + list(small), axis=-1)
    pad = CS_ROWS * LANE - flat.shape[-1]
    flat = jnp.concatenate([flat, jnp.zeros(lead + (pad,), F32)], axis=-1)
    return flat.reshape(lead + (CS_ROWS, LANE))


def _unpack_small(p):
    flat = p.reshape(DEPTH, CS_ROWS * LANE)
    conv = flat[:, :CS_CONV].reshape(DEPTH, CONV_K, CONV_SHARD_COLS)
    small, off = [], CS_CONV
    for _, n in SMALL_SIZES:
        small.append(flat[:, off:off + n])
        off += n
    return conv, small


def kernel(x, w_in, conv_w, a_log, dt_bias, norm_w, sinks, w_out, ln_g, ln_b, loss_target, m_w_in, m_conv_w, m_a_log, m_dt_bias, m_norm_w, m_sinks, m_w_out, m_ln_g, m_ln_b, v_w_in, v_conv_w, v_a_log, v_dt_bias, v_norm_w, v_sinks, v_w_out, v_ln_g, v_ln_b):
    small = [a_log, dt_bias, norm_w, sinks, ln_g, ln_b]
    shards = [[w_in[l].astype(BF16), w_out[l].astype(BF16), conv_w[l]] for l in range(DEPTH)]
    g_in0, = _all_gather(shards[0][:1], name="weights_all_gather_0")
    weights = [[_full_w_in(g_in0), None, None]] + [[None, None, None]] * (DEPTH - 1)

    y, saved, weights = _forward(x[0], weights, shards, small)
    dx, loss_lanes = _loss_head(y, loss_target[0], name="loss_head")
    loss = lax.psum(0.5 * jnp.sum(loss_lanes) * (1.0 / D_MODEL), ("x", "y", "c"))
    dx, g1, _ = _backward_layer(1, dx, saved[1], weights[1], ln_g[1])
    dx, _, (p_in, p_out, p_small) = _backward_layer(0, dx, saved[0], weights[0], ln_g[0], above=_contributions(g1))

    o_in = _adamw(p_in, w_in, m_w_in, v_w_in, tr=256, name="adamw_w_in")
    o_out = _adamw(p_out, w_out, m_w_out, v_w_out, tr=OUT_SHARD_ROWS, name="adamw_w_out")
    o_small = _adamw(p_small, _pack_small(conv_w, small),
                     _pack_small(m_conv_w, [m_a_log, m_dt_bias, m_norm_w, m_sinks, m_ln_g, m_ln_b]),
                     _pack_small(v_conv_w, [v_a_log, v_dt_bias, v_norm_w, v_sinks, v_ln_g, v_ln_b]),
                     tr=CS_ROWS, name="adamw_small")
    outs = []
    for k in range(4):
        cv, sm = _unpack_small(o_small[k])
        outs += [o_in[k], cv, sm[0], sm[1], sm[2], sm[3], o_out[k], sm[4], sm[5]]
    return (loss, dx[None], *outs)
```

```python
import functools

import jax
import jax.numpy as jnp
from jax import lax
from jax.experimental import pallas as pl
from jax.experimental.pallas import tpu as pltpu

F32 = jnp.float32
BF16 = jnp.bfloat16
MM_DTYPE = BF16

N_DEV = 8
D_MODEL = 1024
DEPTH = 2
A_HEADS = 4
A_HEAD_DIM = 128
A_WIDTH = 512
CONV_K = 4
CHUNK = 64
SUPER = 256
G_HEADS = 4
NEWTON_STEPS = 1
B_Q_HEADS = 8
B_KV_HEADS = 2
B_HEAD_DIM = 64
B_GROUP = 4
B_WIDTH = 512
WINDOW = 128
BLOCK = 128
IN_COLS = 3336
SHARD_COLS = IN_COLS // N_DEV
OUT_SHARD_ROWS = D_MODEL // N_DEV
CONV_SHARD_COLS = 3 * A_WIDTH // N_DEV
DEEPNORM_ALPHA = (2 * DEPTH) ** 0.25
LN_EPS = 1e-5
RMS_EPS = 1e-6
L2_EPS = 1e-6
ADAM_LR, ADAM_B1, ADAM_B2, ADAM_EPS, ADAM_WD, ADAM_STEP = 0.001, 0.9, 0.999, 1e-08, 0.01, 10

LANE = 128
L_QB, L_ZB, L_ZA, L_KB, L_VB, L_QKV, L_BA = 0, 512, 1024, 1536, 1664, 1792, 3328
L_SWA = 1792
L_MAIN = 3328
L_COLS = 3456
SMALL_SIZES = (("a_log", 4), ("dt_bias", 4), ("norm_w", 128), ("sinks", 8), ("ln_g", 1024), ("ln_b", 1024))
CS_CONV = CONV_K * CONV_SHARD_COLS
CS_ROWS = 24
VMEM_LIMIT = 48 * 1024 * 1024


def _cparams(sem=None):
    return pltpu.CompilerParams(dimension_semantics=sem, vmem_limit_bytes=VMEM_LIMIT)


def _mm(a, b):
    return jnp.dot(a.astype(MM_DTYPE), b.astype(MM_DTYPE), preferred_element_type=F32)


def _mm_nt(a, b):
    return lax.dot_general(a.astype(MM_DTYPE), b.astype(MM_DTYPE), (((1,), (1,)), ((), ())),
                           preferred_element_type=F32)


def _mm_tn(a, b):
    return lax.dot_general(a.astype(MM_DTYPE), b.astype(MM_DTYPE), (((0,), (0,)), ((), ())),
                           preferred_element_type=F32)


def _split(a):
    hi = a.astype(BF16)
    return hi, (a - hi.astype(F32)).astype(BF16)


def _hp(a2, b2):
    d = lambda p, q: jnp.dot(p, q, preferred_element_type=F32)
    return d(a2[0], b2[0]) + (d(a2[0], b2[1]) + d(a2[1], b2[0]))


def _silu(x):
    return x * jax.nn.sigmoid(x)


@jax.custom_vjp
def _stack(parts):
    return jnp.stack(parts)


_stack.defvjp(lambda parts: (jnp.stack(parts), None), lambda _, g: (tuple(g[i] for i in range(g.shape[0])),))


def _softplus(x):
    return jnp.maximum(x, 0.0) + jnp.log1p(jnp.exp(-jnp.abs(x)))


_ANY = pl.BlockSpec(memory_space=pl.ANY)


def _me():
    return lax.axis_index("x"), lax.axis_index("y"), lax.axis_index("c")


def _flat_id(pos):
    return 4 * pos[0] + 2 * pos[1] + pos[2]


def _remote(src, dst, send_sem, recv_sem, to):
    return pltpu.make_async_remote_copy(src_ref=src, dst_ref=dst, send_sem=send_sem, recv_sem=recv_sem,
                                        device_id=to, device_id_type=pl.DeviceIdType.MESH)


class _Direct:
    def __init__(self, items, bufs):
        self.items, self.bufs = list(items), list(bufs)
        self.n_src, self.n_buf = len(self.items), len(self.bufs)
        self.old = [j for j, b in enumerate(self.bufs) if not isinstance(b, jax.ShapeDtypeStruct)]
        self.args = [it[0] for it in self.items] + [self.bufs[j] for j in self.old]
        self.out_shape = [jax.ShapeDtypeStruct(b.shape, b.dtype) for b in self.bufs]
        self.scratch = [pltpu.SemaphoreType.DMA((self.n_src, N_DEV - 1)),
                        pltpu.SemaphoreType.DMA((self.n_src, N_DEV - 1)), pltpu.SemaphoreType.DMA((self.n_src,))]

    def aliases(self, in_base, out_base):
        return {in_base + self.n_src + pos: out_base + j for pos, j in enumerate(self.old)}

    def copies(self, in_refs, out_refs, sems):
        send_sems, recv_sems, local_sems = sems
        x, y, c = _me()
        me = _flat_id((x, y, c))
        peers = [(x ^ ((rel >> 2) & 1), y ^ ((rel >> 1) & 1), c ^ (rel & 1)) for rel in range(1, N_DEV)]
        local, sends, recvs = [], [], []
        for a, (_, per_dest, j, prefix) in enumerate(self.items):
            src = lambda d: in_refs[a].at[d] if per_dest else in_refs[a]
            dst = lambda s: out_refs[j].at[tuple(prefix) + (s,)]
            local.append(pltpu.make_async_copy(src(me), dst(me), local_sems.at[a]))
            for k, peer in enumerate(peers):
                pid = _flat_id(peer)
                sends.append(_remote(src(pid), dst(me), send_sems.at[a, k], recv_sems.at[a, k], peer))
                recvs.append(_remote(src(pid), dst(pid), send_sems.at[a, k], recv_sems.at[a, k], peer))
        return local, sends, recvs

    def start(self, in_refs, out_refs, sems):
        local, sends, _ = self.copies(in_refs, out_refs, sems)
        for cp in local + sends:
            cp.start()

    def wait(self, in_refs, out_refs, sems):
        local, sends, recvs = self.copies(in_refs, out_refs, sems)
        for cp in recvs:
            cp.wait_recv()
        for cp in sends:
            cp.wait_send()
        for cp in local:
            cp.wait()


def _pcall(core, *, name, grid, in_specs, out_specs, out_shape, args, sem, scratch_shapes=(), aliases=None,
           rider=None):
    n_in, n_out, n_scr = len(in_specs), len(out_specs), len(scratch_shapes)
    n_rin, n_rout = (len(rider.args), rider.n_buf) if rider else (0, 0)

    def body(*refs):
        ins, outs = refs[:n_in], refs[n_in + n_rin:n_in + n_rin + n_out]
        scr = refs[n_in + n_rin + n_out + n_rout:n_in + n_rin + n_out + n_rout + n_scr]
        if rider:
            r_refs = (refs[n_in:n_in + rider.n_src], refs[n_in + n_rin + n_out:n_in + n_rin + n_out + n_rout],
                      refs[n_in + n_rin + n_out + n_rout + n_scr:])
            ids = [pl.program_id(d) for d in range(len(grid))]
            first = functools.reduce(lambda p, q: p & q, [i == 0 for i in ids])
            last = functools.reduce(lambda p, q: p & q, [i == g - 1 for i, g in zip(ids, grid)])
            pl.when(first)(lambda: rider.start(*r_refs))
        core(ins, outs, scr)
        if rider:
            pl.when(last)(lambda: rider.wait(*r_refs))

    aliases = dict(aliases or {})
    if rider:
        sem = ("arbitrary",) * len(grid)
        aliases.update(rider.aliases(n_in, n_out))
    return pl.pallas_call(
        body, name=name, grid=grid, in_specs=list(in_specs) + [_ANY] * n_rin,
        out_specs=list(out_specs) + [_ANY] * n_rout,
        out_shape=list(out_shape) + (rider.out_shape if rider else []),
        scratch_shapes=list(scratch_shapes) + (rider.scratch if rider else []),
        input_output_aliases=aliases, compiler_params=_cparams(sem),
    )(*args, *(rider.args if rider else []))


def _exchange(direct, *, name):
    n_in = len(direct.args)

    def body(*refs):
        r_refs = refs[:direct.n_src], refs[n_in:n_in + direct.n_buf], refs[n_in + direct.n_buf:]
        direct.start(*r_refs)
        direct.wait(*r_refs)

    return pl.pallas_call(
        body, name=name, in_specs=[_ANY] * n_in, out_specs=[_ANY] * direct.n_buf, out_shape=direct.out_shape,
        input_output_aliases=direct.aliases(0, 0), scratch_shapes=direct.scratch,
    )(*direct.args)


def _matmul(a, b, *, form, tm, tn, tk, name, add=None, add_scale=1.0, extra=None, rider=None):
    if form == "nn":
        (m, kk), n = a.shape, b.shape[1]
        a_spec = pl.BlockSpec((tm, tk), lambda i, j, k: (i, k))
        b_spec = pl.BlockSpec((tk, tn), lambda i, j, k: (k, j))
        dn = (((1,), (0,)), ((), ()))
    elif form == "nt":
        (m, kk), n = a.shape, b.shape[0]
        a_spec = pl.BlockSpec((tm, tk), lambda i, j, k: (i, k))
        b_spec = pl.BlockSpec((tn, tk), lambda i, j, k: (j, k))
        dn = (((1,), (1,)), ((), ()))
    else:
        (kk, m), n = a.shape, b.shape[1]
        a_spec = pl.BlockSpec((tk, tm), lambda i, j, k: (k, i))
        b_spec = pl.BlockSpec((tk, tn), lambda i, j, k: (k, j))
        dn = (((0,), (0,)), ((), ()))
    assert m % tm == 0 and n % tn == 0 and kk % tk == 0, (name, m, n, kk)
    has_add, has_extra = add is not None, extra is not None

    def core(ins, outs, _):
        a_ref, b_ref = ins[:2]
        o_ref = outs[0]
        rest = ins[2:]
        k = pl.program_id(2)
        p = lax.dot_general(a_ref[...].astype(MM_DTYPE), b_ref[...].astype(MM_DTYPE), dn,
                            preferred_element_type=F32)

        @pl.when(k == 0)
        def _():
            first = p
            pos = 0
            if has_extra:
                first = first + _mm_nt(rest[0][...], rest[1][...])
                pos = 2
            if has_add:
                first = first + add_scale * rest[pos][...]
            o_ref[...] = first

        @pl.when(k > 0)
        def _():
            o_ref[...] += p

    in_specs = [a_spec, b_spec]
    args = [a, b]
    if has_extra:
        a2, b2, idx = extra
        in_specs += [pl.BlockSpec((tm, LANE), lambda i, j, k: (i, 0)),
                     pl.BlockSpec((tn, LANE), lambda i, j, k: (j, idx))]
        args += [a2, b2]
    if has_add:
        in_specs.append(pl.BlockSpec((tm, tn), lambda i, j, k: (i, j)))
        args.append(add)
    res = _pcall(core, name=name, grid=(m // tm, n // tn, kk // tk), in_specs=in_specs,
                 out_specs=[pl.BlockSpec((tm, tn), lambda i, j, k: (i, j))],
                 out_shape=[jax.ShapeDtypeStruct((m, n), F32)], args=args,
                 sem=("parallel", "parallel", "arbitrary"), rider=rider)
    return res if rider else res[0]


def _shift_down(x, k, row):
    return jnp.where(row >= k, pltpu.roll(x, k, 0), 0.0)


def _shift_up(x, k, row, t_len):
    return jnp.where(row < t_len - k, pltpu.roll(x, t_len - k, 0), 0.0)


def _conv_slab(x, w, row):
    return (w[3:4] * x + w[2:3] * _shift_down(x, 1, row) + w[1:2] * _shift_down(x, 2, row)
            + w[0:1] * _shift_down(x, 3, row))


def _prep_fwd(h, conv_w, *, name):
    t_len = h.shape[0]

    def body(x_ref, w_ref, o_ref):
        s = pl.program_id(0)
        row = lax.broadcasted_iota(jnp.int32, (t_len, LANE), 0)
        y = _silu(_conv_slab(x_ref[...], w_ref[...], row))
        rs = lax.rsqrt(jnp.sum(y * y, axis=-1, keepdims=True) + L2_EPS)
        scale = jnp.where(s < A_HEADS, A_HEAD_DIM ** -0.5, 1.0)
        o_ref[...] = jnp.where(s < 2 * A_HEADS, y * rs * scale, y)

    return pl.pallas_call(
        body, name=name, grid=(12,),
        in_specs=[pl.BlockSpec((t_len, LANE), lambda s: (0, L_QKV // LANE + s)),
                  pl.BlockSpec((8, LANE), lambda s: (0, s))],
        out_specs=pl.BlockSpec((t_len, LANE), lambda s: (0, s)),
        out_shape=jax.ShapeDtypeStruct((t_len, 3 * A_WIDTH), F32),
        compiler_params=_cparams(("parallel",)),
    )(h, conv_w)


def _prep_bwd(h, conv_w, d_out, dh, *, name):
    t_len = h.shape[0]

    def body(x_ref, w_ref, g_ref, dh_in, dx_ref, dw_ref):
        del dh_in
        s = pl.program_id(0)
        row = lax.broadcasted_iota(jnp.int32, (t_len, LANE), 0)
        x = x_ref[...]
        w = w_ref[...]
        c = _conv_slab(x, w, row)
        sg = jax.nn.sigmoid(c)
        y = c * sg
        g = g_ref[0]
        rs = lax.rsqrt(jnp.sum(y * y, axis=-1, keepdims=True) + L2_EPS)
        scale = jnp.where(s < A_HEADS, A_HEAD_DIM ** -0.5, 1.0)
        dy_n = scale * (rs * g - y * (rs * rs * rs) * jnp.sum(g * y, axis=-1, keepdims=True))
        dy = jnp.where(s < 2 * A_HEADS, dy_n, g)
        dc = dy * (sg * (1.0 + c * (1.0 - sg)))
        dx_ref[...] = (w[3:4] * dc + w[2:3] * _shift_up(dc, 1, row, t_len)
                       + w[1:2] * _shift_up(dc, 2, row, t_len) + w[0:1] * _shift_up(dc, 3, row, t_len))
        dws = [jnp.sum(dc * _shift_down(x, 3 - j, row), axis=0, keepdims=True) if j < 3
               else jnp.sum(dc * x, axis=0, keepdims=True) for j in range(CONV_K)]
        dw_ref[...] = jnp.concatenate(dws + [jnp.zeros((8 - CONV_K, LANE), F32)], axis=0)

    slab = pl.BlockSpec((t_len, LANE), lambda s: (0, L_QKV // LANE + s))
    return pl.pallas_call(
        body, name=name, grid=(12,),
        in_specs=[slab, pl.BlockSpec((8, LANE), lambda s: (0, s)),
                  pl.BlockSpec((1, t_len, LANE), lambda s: (s // A_HEADS, 0, s % A_HEADS)), _ANY],
        out_specs=[slab, pl.BlockSpec((8, LANE), lambda s: (0, s))],
        out_shape=[jax.ShapeDtypeStruct((t_len, L_MAIN), F32), jax.ShapeDtypeStruct((8, 3 * A_WIDTH), F32)],
        input_output_aliases={3: 0},
        compiler_params=_cparams(("parallel",)),
    )(h, conv_w, d_out, dh)


N_LEVELS = 5
MF_TRIL, MF_STRIL, MF_DIAG8, MF_LOW16, MF_EYE = 0, 1, 2, 3, 3 + N_LEVELS
MB_CUM, MB_CUM_T, MB_TOT = 0, 1, 2


def _gdn_masks():
    r = lax.broadcasted_iota(jnp.int32, (SUPER, SUPER), 0)
    c = lax.broadcasted_iota(jnp.int32, (SUPER, SUPER), 1)
    same = lambda shift: (r >> shift) == (c >> shift)
    ninf = lambda m: jnp.where(m, 0.0, -jnp.inf).astype(F32)
    one = lambda m: m.astype(F32)
    mf = jnp.stack([ninf(r >= c), ninf(r > c), one(same(3))]
                   + [one(same(4 + lv) & jnp.logical_not(same(3 + lv))) for lv in range(N_LEVELS)] + [one(r == c)])
    mb = jnp.stack([one(r >= c), one(r <= c), jnp.ones((SUPER, SUPER), F32)]).astype(BF16)
    return mf, mb


def _tri_inv_impl(a, mf):
    d = lambda p, q: jnp.dot(p.astype(BF16), q.astype(BF16), preferred_element_type=F32)
    dd = lambda p, q: jnp.dot(p, q, preferred_element_type=F32)
    eye = mf[MF_EYE]
    a0 = a * mf[MF_DIAG8]
    a2 = d(a0, a0)
    a4 = d(a2, a2)
    t = d(d(eye - a0, eye + a2), eye + a4)
    for level in range(N_LEVELS):
        t = t - d(d(t, a * mf[MF_LOW16 + level]), t)
    a_hi, a_lo = _split(a)
    for _ in range(NEWTON_STEPS):
        t_hi, t_lo = _split(t)
        resid = (eye - t) - (dd(a_hi, t_hi) + (dd(a_hi, t_lo) + dd(a_lo, t_hi)))
        r_hi, r_lo = _split(resid)
        t = t + (dd(t_hi, r_hi) + dd(t_hi, r_lo))
    return t


@jax.custom_vjp
def _wy_apply(a, rhs, t):
    return _mm(t, rhs)


def _wy_apply_fwd(a, rhs, t):
    x = _mm(t, rhs)
    return x, (t, x)


def _wy_apply_bwd(res, dx):
    t, x = res
    d_rhs = _mm_tn(t, dx)
    return -_mm_nt(d_rhs, x), d_rhs, jnp.zeros_like(t)


_wy_apply.defvjp(_wy_apply_fwd, _wy_apply_bwd)


@functools.partial(jax.custom_vjp, nondiff_argnums=(1,))
def _lane_roll(x, shift):
    return pltpu.roll(x, shift % LANE, 1)


_lane_roll.defvjp(lambda x, shift: (_lane_roll(x, shift), None), lambda shift, _, g: (_lane_roll(g, -shift),))


def _mask_times_lanes(x, mask):
    lane = lax.broadcasted_iota(jnp.int32, (1, LANE), 1)
    x = jnp.where(lane < A_HEADS, x, 0.0)
    x1 = x.astype(BF16).astype(F32)
    x2 = (x - x1).astype(BF16).astype(F32)
    x3 = (x - x1 - x2).astype(BF16).astype(F32)
    pieces = x1 + pltpu.roll(x2, A_HEADS, 1) + pltpu.roll(x3, 2 * A_HEADS, 1)
    res = jnp.dot(mask, pieces.astype(BF16), preferred_element_type=F32)
    return res + pltpu.roll(res, LANE - A_HEADS, 1) + pltpu.roll(res, LANE - 2 * A_HEADS, 1)


@jax.custom_vjp
def _chunk_sums(g, mb):
    return _mask_times_lanes(g, mb[MB_CUM]), _mask_times_lanes(g, mb[MB_TOT])


def _chunk_sums_fwd(g, mb):
    return _chunk_sums(g, mb), mb


def _chunk_sums_bwd(mb, d):
    lane = lax.broadcasted_iota(jnp.int32, (1, LANE), 1)
    dg = _mask_times_lanes(d[0], mb[MB_CUM_T]) + _mask_times_lanes(d[1], mb[MB_TOT])
    return jnp.where(lane < A_HEADS, dg, 0.0), jnp.zeros_like(mb)


_chunk_sums.defvjp(_chunk_sums_fwd, _chunk_sums_bwd)


def _gdn_gates(ba, alog, dtb, mb):
    beta = jax.nn.sigmoid(ba)
    g = -jnp.exp(alog) * _softplus(_lane_roll(ba, -A_HEADS) + dtb)
    gc, gl = _chunk_sums(g, mb)
    return beta, gc, gl, gc.T


def _gdn_block(s, q, k, v, z, gates, nw, h, t_known, mf):
    n = q.shape[0]
    beta_all, gc_all, gl_all, gct_all = gates
    lane = lax.broadcasted_iota(jnp.int32, (1, LANE), 1)
    sub = lax.broadcasted_iota(jnp.int32, (LANE, 1), 0)
    col = lambda x: jnp.sum(jnp.where(lane == h, x, 0.0), axis=1, keepdims=True)
    wide = lambda c: jnp.broadcast_to(c, (n, LANE))
    gc, gl = col(gc_all), col(gl_all)
    gc_row = jnp.sum(jnp.where(sub == h, gct_all, 0.0), axis=0, keepdims=True)
    beta_w, eg_w = wide(col(beta_all)), wide(jnp.exp(gc))
    diff = gc - gc_row
    decay = jnp.exp(diff + mf[MF_TRIL])
    kb = k * beta_w
    a_mat = _mm_nt(kb, k) * jnp.exp(diff + mf[MF_STRIL])
    rhs = jnp.concatenate([v * beta_w, kb * eg_w], axis=1)
    if t_known is None:
        t_mat = _tri_inv_impl(a_mat, mf)
        uw = _mm(t_mat, rhs)
    else:
        t_mat = t_known
        uw = _wy_apply(a_mat, rhs, t_known)
    u, w = uw[:, :LANE], uw[:, LANE:]
    qk = _mm_nt(q, k) * decay
    q_dec = q * eg_w
    k_dec = k * wide(jnp.exp(gl - gc))
    v_new = u - _mm(w, s)
    o = _mm(q_dec, s) + _mm(qk, v_new)
    s = s * jnp.exp(gl[0:1]) + _mm_tn(k_dec, v_new)
    o = o * lax.rsqrt(jnp.mean(o * o, axis=-1, keepdims=True) + RMS_EPS) * nw
    return o * _silu(z), s, t_mat


def _gdn_fwd(qkv, h, alog, dtb, nw, ycat, *, name, rider=None):
    t_len = qkv.shape[0]
    nsc = t_len // SUPER

    def core(ins, outs, scr):
        q_ref, k_ref, v_ref, z_ref, ba_ref, al_ref, dt_ref, nw_ref, mf_ref, mb_ref, _ = ins
        y_ref, sin_ref, t_ref = outs
        s_scr, = scr
        sc, pg = pl.program_id(0), pl.program_id(1)
        heads = [pg * G_HEADS + hh for hh in range(G_HEADS)]

        @pl.when(sc == 0)
        def _():
            for hd in heads:
                s_scr[hd] = jnp.zeros((A_HEAD_DIM, A_HEAD_DIM), F32)

        per_head = lambda ref: jnp.stack([ref[:, hh * LANE:(hh + 1) * LANE] for hh in range(G_HEADS)])
        states = jnp.stack([s_scr[hd] for hd in heads])
        gates = _gdn_gates(ba_ref[...], al_ref[...], dt_ref[...], mb_ref[...])
        fn = jax.vmap(_gdn_block, in_axes=(0, 0, 0, 0, 0, None, None, 0, None, None))
        y, s_new, t_mat = fn(states, per_head(q_ref), per_head(k_ref), per_head(v_ref), per_head(z_ref),
                             gates, nw_ref[...], jnp.stack(heads), None, mf_ref[...])
        sin_ref[0] = states
        t_ref[0] = t_mat
        for hh, hd in enumerate(heads):
            y_ref[:, hh * LANE:(hh + 1) * LANE] = y[hh]
            s_scr[hd] = s_new[hh]

    gw = G_HEADS * LANE
    blk = lambda off: pl.BlockSpec((SUPER, gw), lambda sc, pg: (sc, off // G_HEADS + pg))
    row = pl.BlockSpec((1, LANE), lambda sc, pg: (0, 0))
    mf, mb = _gdn_masks()
    whole = lambda a: pl.BlockSpec(a.shape, lambda sc, pg: (0, 0, 0))
    return _pcall(
        core, name=name, grid=(nsc, A_HEADS // G_HEADS),
        in_specs=[blk(0), blk(4), blk(8), blk(L_ZA // LANE),
                  pl.BlockSpec((SUPER, LANE), lambda sc, pg: (sc, L_BA // LANE)), row, row, row,
                  whole(mf), whole(mb), _ANY],
        out_specs=[blk(0),
                   pl.BlockSpec((1, G_HEADS, A_HEAD_DIM, A_HEAD_DIM), lambda sc, pg: (sc, pg, 0, 0)),
                   pl.BlockSpec((1, G_HEADS, SUPER, SUPER), lambda sc, pg: (sc, pg, 0, 0))],
        out_shape=[jax.ShapeDtypeStruct((t_len, D_MODEL), F32),
                   jax.ShapeDtypeStruct((nsc, A_HEADS, A_HEAD_DIM, A_HEAD_DIM), F32),
                   jax.ShapeDtypeStruct((nsc, A_HEADS, SUPER, SUPER), F32)],
        scratch_shapes=[pltpu.VMEM((A_HEADS, A_HEAD_DIM, A_HEAD_DIM), F32)],
        aliases={10: 0}, sem=("arbitrary", "arbitrary"), rider=rider,
        args=(qkv, qkv, qkv, h, h, alog, dtb, nw, mf, mb, ycat))


def _gdn_bwd(qkv, h, alog, dtb, nw, s_in, t_in, dycat, dh, *, name, rider=None):
    t_len = qkv.shape[0]
    nsc = t_len // SUPER

    def core(ins, outs, scr):
        (q_ref, k_ref, v_ref, z_ref, ba_ref, al_ref, dt_ref, nw_ref, sin_ref, t_ref, dy_ref, mf_ref, mb_ref,
         _) = ins
        dz_ref, dqkv_ref, dba_ref, dal_ref, ddt_ref, dnw_ref = outs
        ds_scr, = scr
        i, pg = pl.program_id(0), pl.program_id(1)

        @pl.when((i == 0) & (pg == 0))
        def _():
            dal_ref[...] = jnp.zeros_like(dal_ref)
            ddt_ref[...] = jnp.zeros_like(ddt_ref)
            dnw_ref[...] = jnp.zeros_like(dnw_ref)

        @pl.when(pg == 0)
        def _():
            dba_ref[...] = jnp.zeros_like(dba_ref)

        heads = [pg * G_HEADS + hh for hh in range(G_HEADS)]

        @pl.when(i == 0)
        def _():
            for hd in heads:
                ds_scr[hd] = jnp.zeros((A_HEAD_DIM, A_HEAD_DIM), F32)

        per_head = lambda ref: jnp.stack([ref[:, hh * LANE:(hh + 1) * LANE] for hh in range(G_HEADS)])
        d_states = jnp.stack([ds_scr[hd] for hd in heads])
        head_ids = jnp.stack(heads)
        t_known, mf, mb = t_ref[0], mf_ref[...], mb_ref[...]

        def fn(s, q, k, v, z, ba, alog, dtb, nw):
            gates = _gdn_gates(ba, alog, dtb, mb)
            one = lambda s, q, k, v, z, t, h: _gdn_block(s, q, k, v, z, gates, nw, h, t, mf)[:2]
            return jax.vmap(one)(s, q, k, v, z, t_known, head_ids)

        _, vjp = jax.vjp(fn, sin_ref[0], per_head(q_ref), per_head(k_ref), per_head(v_ref), per_head(z_ref),
                         ba_ref[...], al_ref[...], dt_ref[...], nw_ref[...])
        ds, dq, dk, dv, dz, dba, dal, ddt, dnw = vjp((per_head(dy_ref), d_states))
        for hh, hd in enumerate(heads):
            cols = slice(hh * LANE, (hh + 1) * LANE)
            ds_scr[hd] = ds[hh]
            dqkv_ref[0, :, cols] = dq[hh]
            dqkv_ref[1, :, cols] = dk[hh]
            dqkv_ref[2, :, cols] = dv[hh]
            dz_ref[:, cols] = dz[hh]
        dba_ref[...] += dba
        dal_ref[...] += dal
        ddt_ref[...] += ddt
        dnw_ref[...] += dnw

    rev = lambda i: nsc - 1 - i
    gw = G_HEADS * LANE
    blk = lambda off: pl.BlockSpec((SUPER, gw), lambda i, pg: (rev(i), off // G_HEADS + pg))
    row = pl.BlockSpec((1, LANE), lambda i, pg: (0, 0))
    ba_blk = lambda off: pl.BlockSpec((SUPER, LANE), lambda i, pg: (rev(i), off))
    mf, mb = _gdn_masks()
    whole = lambda a: pl.BlockSpec(a.shape, lambda i, pg: (0, 0, 0))
    return _pcall(
        core, name=name, grid=(nsc, A_HEADS // G_HEADS),
        in_specs=[blk(0), blk(4), blk(8), blk(L_ZA // LANE), ba_blk(L_BA // LANE), row, row, row,
                  pl.BlockSpec((1, G_HEADS, A_HEAD_DIM, A_HEAD_DIM), lambda i, pg: (rev(i), pg, 0, 0)),
                  pl.BlockSpec((1, G_HEADS, SUPER, SUPER), lambda i, pg: (rev(i), pg, 0, 0)),
                  blk(0), whole(mf), whole(mb), _ANY],
        out_specs=[blk(L_ZA // LANE),
                   pl.BlockSpec((3, SUPER, gw), lambda i, pg: (0, rev(i), pg)),
                   ba_blk(0), row, row, row],
        out_shape=[jax.ShapeDtypeStruct((t_len, L_MAIN), F32), jax.ShapeDtypeStruct((3, t_len, A_WIDTH), F32),
                   jax.ShapeDtypeStruct((t_len, LANE), F32)] + [jax.ShapeDtypeStruct((1, LANE), F32)] * 3,
        scratch_shapes=[pltpu.VMEM((A_HEADS, A_HEAD_DIM, A_HEAD_DIM), F32)],
        aliases={13: 0}, sem=("arbitrary", "arbitrary"), rider=rider,
        args=(qkv, qkv, qkv, h, h, alog, dtb, nw, s_in, t_in, dycat, mf, mb, dh))


def _swa_block(q, kp, kc, vp, vc, z, sinks, first):
    rows = B_GROUP * BLOCK
    ri = lax.broadcasted_iota(jnp.int32, (rows, 2 * BLOCK), 0)
    si = lax.broadcasted_iota(jnp.int32, (rows, 2 * BLOCK), 1)
    dist = (ri & (BLOCK - 1)) + BLOCK - si
    bias = jnp.where((dist >= 0) & (dist < WINDOW) & ((si >= BLOCK) | jnp.logical_not(first)), 0.0, -jnp.inf)
    dist_f = dist.astype(F32)
    head_of_row = lax.broadcasted_iota(jnp.int32, (rows, 1), 0) >> 7

    def group(j):
        cs = slice(j * B_HEAD_DIM, (j + 1) * B_HEAD_DIM)
        heads = range(j * B_GROUP, (j + 1) * B_GROUP)
        qs = jnp.concatenate([q[:, hq * B_HEAD_DIM:(hq + 1) * B_HEAD_DIM] for hq in heads], axis=0)
        kk = jnp.concatenate([kp[:, cs], kc[:, cs]], axis=0)
        vv = jnp.concatenate([vp[:, cs], vc[:, cs]], axis=0)
        sink = jnp.concatenate([jnp.broadcast_to(sinks[:, hq:hq + 1], (BLOCK, 1)) for hq in heads], axis=0)
        slope = sum(jnp.where(head_of_row == gi, 2.0 ** (-8.0 * (hq + 1) / B_Q_HEADS), 0.0)
                    for gi, hq in enumerate(heads))
        return qs, kk, vv, sink, slope

    def attend(qs, kk, vv, sink, slope):
        sc = _mm_nt(qs, kk) * (B_HEAD_DIM ** -0.5) - slope * dist_f + bias
        m = lax.stop_gradient(jnp.maximum(jnp.max(sc, axis=-1, keepdims=True), sink))
        p = jnp.exp(sc - m)
        inv = 1.0 / (jnp.sum(p, axis=-1, keepdims=True) + jnp.exp(sink - m))
        return _mm(p * inv, vv)

    o = jax.vmap(attend)(*[_stack(t) for t in zip(*[group(j) for j in range(B_KV_HEADS)])])
    outs = [o[j, gi * BLOCK:(gi + 1) * BLOCK] for j in range(B_KV_HEADS) for gi in range(B_GROUP)]
    return jnp.concatenate(outs, axis=1) * _silu(z)


def _swa_specs(idx):
    wide = lambda off: pl.BlockSpec((BLOCK, B_WIDTH), lambda n: (idx(n), off))
    cur = lambda off: pl.BlockSpec((BLOCK, LANE), lambda n: (idx(n), off))
    prev = lambda off: pl.BlockSpec((BLOCK, LANE), lambda n: (jnp.maximum(idx(n) - 1, 0), off))
    return [wide(L_QB // B_WIDTH), prev(L_KB // LANE), cur(L_KB // LANE), prev(L_VB // LANE), cur(L_VB // LANE),
            wide(L_ZB // B_WIDTH), pl.BlockSpec((1, LANE), lambda n: (0, 0))]


def _swa_fwd(h, sinks, *, name, rider=None):
    t_len = h.shape[0]
    nb = t_len // BLOCK

    def core(ins, outs, _):
        q_ref, kp_ref, kc_ref, vp_ref, vc_ref, z_ref, s_ref = ins
        outs[0][...] = _swa_block(q_ref[...], kp_ref[...], kc_ref[...], vp_ref[...], vc_ref[...], z_ref[...],
                                  s_ref[...], pl.program_id(0) == 0)

    res = _pcall(core, name=name, grid=(nb,), in_specs=_swa_specs(lambda n: n),
                 out_specs=[pl.BlockSpec((BLOCK, B_WIDTH), lambda n: (n, 1))],
                 out_shape=[jax.ShapeDtypeStruct((t_len, D_MODEL), F32)], sem=("parallel",), rider=rider,
                 args=(h, h, h, h, h, h, sinks))
    return res if rider else res[0]


def _swa_bwd(h, sinks, dycat, *, name, rider=None):
    t_len = h.shape[0]
    nb = t_len // BLOCK

    def core(ins, outs, scr):
        q_ref, kp_ref, kc_ref, vp_ref, vc_ref, z_ref, s_ref, dy_ref = ins
        dh_ref, dsk_ref = outs
        ck_scr, cv_scr = scr
        i = pl.program_id(0)
        n = nb - 1 - i

        @pl.when(i == 0)
        def _():
            ck_scr[...] = jnp.zeros_like(ck_scr)
            cv_scr[...] = jnp.zeros_like(cv_scr)
            dsk_ref[...] = jnp.zeros_like(dsk_ref)

        fn = functools.partial(_swa_block, first=(n == 0))
        _, vjp = jax.vjp(fn, q_ref[...], kp_ref[...], kc_ref[...], vp_ref[...], vc_ref[...], z_ref[...], s_ref[...])
        dq, dkp, dkc, dvp, dvc, dz, dsk = vjp(dy_ref[...])
        dh_ref[:, L_QB:L_QB + B_WIDTH] = dq
        dh_ref[:, L_ZB:L_ZB + B_WIDTH] = dz
        dh_ref[:, L_ZA:L_ZA + A_WIDTH] = jnp.zeros((BLOCK, A_WIDTH), F32)
        dh_ref[:, L_KB:L_KB + LANE] = dkc + ck_scr[...]
        dh_ref[:, L_VB:L_VB + LANE] = dvc + cv_scr[...]
        ck_scr[...] = dkp
        cv_scr[...] = dvp
        dsk_ref[...] += dsk

    rev = lambda i: nb - 1 - i
    return _pcall(
        core, name=name, grid=(nb,),
        in_specs=_swa_specs(rev) + [pl.BlockSpec((BLOCK, B_WIDTH), lambda i: (rev(i), 1))],
        out_specs=[pl.BlockSpec((BLOCK, L_SWA), lambda i: (rev(i), 0)), pl.BlockSpec((1, LANE), lambda i: (0, 0))],
        out_shape=[jax.ShapeDtypeStruct((t_len, L_MAIN), F32), jax.ShapeDtypeStruct((1, LANE), F32)],
        scratch_shapes=[pltpu.VMEM((BLOCK, LANE), F32), pltpu.VMEM((BLOCK, LANE), F32)],
        sem=("arbitrary",), rider=rider, args=(h, h, h, h, h, h, sinks, dycat))


def _out_ln_fwd(ycat, w_out, x, ln_g, ln_b, *, name, tm=256):
    t_len = x.shape[0]

    def body(y_ref, w_ref, x_ref, g_ref, b_ref, r_ref, o_ref):
        r = DEEPNORM_ALPHA * x_ref[...] + _mm(y_ref[...], w_ref[...])
        r_ref[...] = r
        mu = jnp.mean(r, axis=-1, keepdims=True)
        d = r - mu
        var = jnp.mean(d * d, axis=-1, keepdims=True)
        o_ref[...] = d * lax.rsqrt(var + LN_EPS) * g_ref[...] + b_ref[...]

    tile = pl.BlockSpec((tm, D_MODEL), lambda i: (i, 0))
    vec = pl.BlockSpec((1, D_MODEL), lambda i: (0, 0))
    return pl.pallas_call(
        body, name=name, grid=(t_len // tm,),
        in_specs=[tile, pl.BlockSpec((D_MODEL, D_MODEL), lambda i: (0, 0)), tile, vec, vec],
        out_specs=[tile, tile],
        out_shape=[jax.ShapeDtypeStruct((t_len, D_MODEL), F32)] * 2,
        compiler_params=_cparams(("parallel",)),
    )(ycat, w_out, x, ln_g, ln_b)


def _ln_bwd(dxn, r, ln_g, *, name, tm=256):
    t_len = r.shape[0]

    def body(dx_ref, r_ref, g_ref, dr_ref, dg_ref, db_ref):
        @pl.when(pl.program_id(0) == 0)
        def _():
            dg_ref[...] = jnp.zeros_like(dg_ref)
            db_ref[...] = jnp.zeros_like(db_ref)

        rr = r_ref[...]
        dx = dx_ref[...]
        mu = jnp.mean(rr, axis=-1, keepdims=True)
        d = rr - mu
        rstd = lax.rsqrt(jnp.mean(d * d, axis=-1, keepdims=True) + LN_EPS)
        xh = d * rstd
        dxh = dx * g_ref[...]
        dr_ref[...] = rstd * (dxh - jnp.mean(dxh, axis=-1, keepdims=True)
                              - xh * jnp.mean(dxh * xh, axis=-1, keepdims=True))
        dg_ref[...] += jnp.sum(dx * xh, axis=0, keepdims=True)
        db_ref[...] += jnp.sum(dx, axis=0, keepdims=True)

    tile = pl.BlockSpec((tm, D_MODEL), lambda i: (i, 0))
    vec = pl.BlockSpec((1, D_MODEL), lambda i: (0, 0))
    return pl.pallas_call(
        body, name=name, grid=(t_len // tm,),
        in_specs=[tile, tile, vec], out_specs=[tile, vec, vec],
        out_shape=[jax.ShapeDtypeStruct((t_len, D_MODEL), F32), jax.ShapeDtypeStruct((1, D_MODEL), F32),
                   jax.ShapeDtypeStruct((1, D_MODEL), F32)],
        compiler_params=_cparams(("arbitrary",)),
    )(dxn, r, ln_g)


def _loss_head(y, target, *, name, tm=256):
    t_len = y.shape[0]

    def body(y_ref, t_ref, d_ref, l_ref):
        @pl.when(pl.program_id(0) == 0)
        def _():
            l_ref[...] = jnp.zeros_like(l_ref)

        e = y_ref[...] - t_ref[...]
        d_ref[...] = e * (1.0 / D_MODEL)
        l_ref[...] += jnp.sum(e * e, axis=0, keepdims=True)

    tile = pl.BlockSpec((tm, D_MODEL), lambda i: (i, 0))
    vec = pl.BlockSpec((1, D_MODEL), lambda i: (0, 0))
    return pl.pallas_call(
        body, name=name, grid=(t_len // tm,), in_specs=[tile, tile], out_specs=[tile, vec],
        out_shape=[jax.ShapeDtypeStruct((t_len, D_MODEL), F32), jax.ShapeDtypeStruct((1, D_MODEL), F32)],
        compiler_params=_cparams(("arbitrary",)),
    )(y, target)


def _pad_row(v):
    return jnp.zeros((1, LANE), F32).at[0, :v.shape[0]].set(v)


def _to_layout(w_full):
    s = lambda a, b: w_full[..., a:b]
    pad = jnp.zeros(w_full.shape[:-1] + (LANE - 2 * A_HEADS,), w_full.dtype)
    return jnp.concatenate([s(2056, 2568), s(2824, 3336), s(1536, 2048), s(2568, 2696), s(2696, 2824), s(0, 1536),
                            s(2048, 2056), pad], axis=-1)


def _from_layout(g_main, g_ba):
    s = lambda a, b: g_main[..., a:b]
    return jnp.concatenate([s(L_QKV, L_QKV + 1536), s(L_ZA, L_ZA + 512), g_ba[..., :2 * A_HEADS],
                            s(L_QB, L_QB + 512), s(L_KB, L_KB + 128), s(L_VB, L_VB + 128), s(L_ZB, L_ZB + 512)],
                           axis=-1)


_REGIONS = ((0, 1536, L_QKV), (1536, 2048, L_ZA), (2048, 2056, L_BA), (2056, 2568, L_QB), (2568, 2696, L_KB),
            (2696, 2824, L_VB), (2824, 3336, L_ZB))


def _shard_pieces(regions):
    for a, b, off in regions:
        for d in range(N_DEV):
            lo, hi = max(a, d * SHARD_COLS), min(b, (d + 1) * SHARD_COLS)
            if lo < hi:
                yield d, lo - d * SHARD_COLS, hi - d * SHARD_COLS, off + lo - a


def _as_list(r):
    return list(r) if isinstance(r, (list, tuple)) else [r]


def _gathered(shard):
    return jax.ShapeDtypeStruct((N_DEV,) + shard.shape, shard.dtype)


def _full_w_in(g_in):
    by_offset = sorted(_shard_pieces(_REGIONS), key=lambda p: p[3])
    pieces = [g_in[d, :, lo:hi] for d, lo, hi, _ in by_offset]
    pad = jnp.zeros((D_MODEL, L_COLS - L_BA - 2 * A_HEADS), g_in.dtype)
    return jnp.concatenate(pieces + [pad], axis=1)


def _full_conv(g_conv):
    return jnp.pad(g_conv.transpose(1, 0, 2).reshape(CONV_K, 3 * A_WIDTH), ((0, 8 - CONV_K), (0, 0)))


def _forward(x, weights, shards, small):
    a_log, dt_bias, norm_w, sinks, ln_g, ln_b = small
    tm = min(512, x.shape[0])
    saved, weights = [], [list(w) for w in weights]
    whole = lambda arrs: _Direct([(a, False, j, ()) for j, a in enumerate(arrs)], [_gathered(a) for a in arrs])
    for l in range(DEPTH):
        rider = whole(shards[l][1:]) if weights[l][1] is None else None
        h, *got = _as_list(_matmul(x, weights[l][0], form="nn", tm=tm, tn=1152, tk=D_MODEL, name=f"in_proj_{l}",
                                   rider=rider))
        if rider:
            weights[l][1:] = [got[0].reshape(D_MODEL, D_MODEL), _full_conv(got[1])]
        w_in_l, w_out_l, conv_l = weights[l]
        qkv = _prep_fwd(h, conv_l, name=f"prep_fwd_{l}")
        al, dt, nw, sk = _pad_row(a_log[l]), _pad_row(dt_bias[l]), norm_w[l][None, :], _pad_row(sinks[l])
        ahead = l + 1 < DEPTH and weights[l + 1][0] is None
        rider = whole(shards[l + 1][1:]) if ahead else None
        ycat, *got = _as_list(_swa_fwd(h, sk, name=f"swa_fwd_{l}", rider=rider))
        if ahead:
            weights[l + 1][1:] = [got[0].reshape(D_MODEL, D_MODEL), _full_conv(got[1])]
        rider = whole(shards[l + 1][:1]) if ahead else None
        ycat, s_in, t_in, *got = _gdn_fwd(qkv, h, al, dt, nw, ycat, name=f"gdn_fwd_{l}", rider=rider)
        if ahead:
            weights[l + 1][0] = _full_w_in(got[0])
        r, xn = _out_ln_fwd(ycat, w_out_l, x, ln_g[l][None, :], ln_b[l][None, :], name=f"out_ln_{l}")
        saved.append((x, h, qkv, s_in, t_in, ycat, r, al, dt, nw, sk))
        x = xn
    return x, saved, weights


def _contributions(g):
    g_main, g_ba = g["w_in_parts"]
    src = lambda off: (g_ba, off - L_BA) if off >= L_BA else (g_main, off)
    blocks = [[] for _ in range(N_DEV)]
    for d, lo, hi, off in _shard_pieces(_REGIONS):
        a, o = src(off)
        blocks[d].append(a[:, o:o + hi - lo])
    c_in = jnp.stack([jnp.concatenate(b, axis=1) for b in blocks]).astype(BF16)
    c_out = g["w_out"].astype(BF16).reshape(N_DEV, OUT_SHARD_ROWS, D_MODEL)
    c_conv = g["conv_w"].reshape(CONV_K, N_DEV, CONV_SHARD_COLS).transpose(1, 0, 2)
    c_small = [jnp.broadcast_to(g[n][None], (N_DEV,) + g[n].shape) for n, _ in SMALL_SIZES]
    return c_in, c_out, _pack_small(c_conv, c_small)


def _backward_layer(l, dx, saved_l, weights_l, ln_g_l, above=None):
    x_in, h, qkv, s_in, t_in, ycat, r, al, dt, nw, sk = saved_l
    w_in_l, w_out_l, conv_l = weights_l
    tm = min(512, x_in.shape[0])
    dr, d_lng, d_lnb = _ln_bwd(dx, r, ln_g_l[None, :], name=f"ln_bwd_{l}")
    dycat = _matmul(dr, w_out_l, form="nt", tm=tm, tn=D_MODEL, tk=D_MODEL, name=f"out_proj_dx_{l}")
    d_wout = _matmul(ycat, dr, form="tn", tm=512, tn=D_MODEL, tk=tm, name=f"out_proj_dw_{l}")
    rider, p_in, p_out, p_small = None, None, None, None
    recv = lambda c: jax.ShapeDtypeStruct((DEPTH,) + c.shape, c.dtype)
    if above:
        c_out = d_wout.astype(BF16).reshape(N_DEV, OUT_SHARD_ROWS, D_MODEL)
        rider = _Direct([(above[1], True, 0, (l + 1,)), (above[2], True, 1, (l + 1,)), (c_out, True, 0, (l,))],
                        [recv(above[1]), recv(above[2])])
    dh, d_sk, *got = _swa_bwd(h, sk, dycat, name=f"swa_bwd_{l}", rider=rider)
    if above:
        p_out, p_small = got
        rider = _Direct([(above[0], True, 0, (l + 1,))], [recv(above[0])])
    dh, dqkv_n, dba, d_al, d_dt, d_nw, *got = _gdn_bwd(qkv, h, al, dt, nw, s_in, t_in, dycat, dh,
                                                       name=f"gdn_bwd_{l}", rider=rider)
    dh, d_conv = _prep_bwd(h, conv_l, dqkv_n, dh, name=f"prep_bwd_{l}")
    d_win_main = _matmul(x_in, dh, form="tn", tm=512, tn=L_MAIN // 2, tk=tm, name=f"in_proj_dw_{l}")
    d_win_ba = _matmul(x_in, dba, form="tn", tm=D_MODEL, tn=LANE, tk=tm, name=f"in_proj_dw_ba_{l}")
    grads = dict(w_in_parts=(d_win_main, d_win_ba), w_out=d_wout, conv_w=d_conv[:CONV_K],
                 a_log=d_al[0, :A_HEADS], dt_bias=d_dt[0, :A_HEADS], norm_w=d_nw[0], sinks=d_sk[0, :B_Q_HEADS],
                 ln_g=d_lng[0], ln_b=d_lnb[0])
    if above:
        p_in, = got
        c_in, _, c_small = _contributions(grads)
        rider = _Direct([(c_in, True, 0, (l,)), (c_small, True, 1, (l,))], [p_in, p_small])
    dx, *got = _as_list(_matmul(dh, w_in_l, form="nt", tm=tm, tn=D_MODEL, tk=L_MAIN // 2, name=f"in_proj_dx_{l}",
                                add=dr, add_scale=DEEPNORM_ALPHA, extra=(dba, w_in_l, L_BA // LANE), rider=rider))
    bufs = (got[0], p_out, got[1]) if above else None
    return dx, grads, bufs


def _all_gather(shards, *, name):
    n_arr = len(shards)

    def body(*refs):
        x_refs, out_refs = refs[:n_arr], refs[n_arr:2 * n_arr]
        send_sems, recv_sems, local_sems = refs[2 * n_arr:]
        x, y, c = _me()
        me, sibling = (x, y, c), (x, y, 1 - c)
        chips = [(1 - x, y), (x, 1 - y), (1 - x, 1 - y)]

        def copy(a, k, block, to, src=None):
            dst = out_refs[a].at[_flat_id(block)]
            return _remote(dst if src is None else src, dst, send_sems.at[a, k], recv_sems.at[a, k], to)

        mine = [pltpu.make_async_copy(x_refs[a], out_refs[a].at[_flat_id(me)], local_sems.at[a])
                for a in range(n_arr)]
        for cp in mine:
            cp.start()
        first = []
        for a in range(n_arr):
            first.append(copy(a, 0, me, sibling, src=x_refs[a]))
            first += [copy(a, 1 + j, me, (*chip, c), src=x_refs[a]) for j, chip in enumerate(chips)]
        for cp in first:
            cp.start()
        passed = []
        for j, chip in enumerate(chips):
            for a in range(n_arr):
                copy(a, 1 + j, (*chip, c), me).wait_recv()
                fwd = copy(a, 4 + j, (*chip, c), sibling)
                fwd.start()
                passed.append(fwd)
        for a in range(n_arr):
            copy(a, 0, sibling, me).wait_recv()
            for j, chip in enumerate(chips):
                copy(a, 4 + j, (*chip, 1 - c), me).wait_recv()
        for cp in first + passed:
            cp.wait_send()
        for cp in mine:
            cp.wait()

    return pl.pallas_call(
        body, name=name, in_specs=[_ANY] * n_arr, out_specs=[_ANY] * n_arr,
        out_shape=[jax.ShapeDtypeStruct((N_DEV,) + s.shape, s.dtype) for s in shards],
        scratch_shapes=[pltpu.SemaphoreType.DMA((n_arr, N_DEV - 1)), pltpu.SemaphoreType.DMA((n_arr, N_DEV - 1)),
                        pltpu.SemaphoreType.DMA((n_arr,))],
    )(*shards)


def _adamw(parts, w, m, v, *, tr, name):
    depth, rows, cols = w.shape
    c1 = 1.0 - ADAM_B1 ** ADAM_STEP
    c2 = 1.0 - ADAM_B2 ** ADAM_STEP

    def body(g_ref, w_ref, m_ref, v_ref, go_ref, d_ref, mo_ref, vo_ref):
        g = g_ref[0, 0].astype(F32)
        for s in range(1, N_DEV):
            g = g + g_ref[0, s].astype(F32)
        m_new = ADAM_B1 * m_ref[0] + (1.0 - ADAM_B1) * g
        v_new = ADAM_B2 * v_ref[0] + (1.0 - ADAM_B2) * (g * g)
        go_ref[0] = g
        mo_ref[0] = m_new
        vo_ref[0] = v_new
        d_ref[0] = -ADAM_LR * ((m_new / c1) / (jnp.sqrt(v_new / c2) + ADAM_EPS) + ADAM_WD * w_ref[0])

    tile = pl.BlockSpec((1, tr, cols), lambda l, i: (l, i, 0))
    return pl.pallas_call(
        body, name=name, grid=(depth, rows // tr),
        in_specs=[pl.BlockSpec((1, N_DEV, tr, cols), lambda l, i: (l, 0, i, 0)), tile, tile, tile],
        out_specs=[tile] * 4, out_shape=[jax.ShapeDtypeStruct(w.shape, F32)] * 4,
        compiler_params=_cparams(("parallel", "parallel")),
    )(parts, w, m, v)


def _pack_small(conv, small):
    lead = conv.shape[:-2]
    flat = jnp.concatenate([conv.reshape(lead + (CS_CONV,))] + list(small), axis=-1)
    pad = CS_ROWS * LANE - flat.shape[-1]
    flat = jnp.concatenate([flat, jnp.zeros(lead + (pad,), F32)], axis=-1)
    return flat.reshape(lead + (CS_ROWS, LANE))


def _unpack_small(p):
    flat = p.reshape(DEPTH, CS_ROWS * LANE)
    conv = flat[:, :CS_CONV].reshape(DEPTH, CONV_K, CONV_SHARD_COLS)
    small, off = [], CS_CONV
    for _, n in SMALL_SIZES:
        small.append(flat[:, off:off + n])
        off += n
    return conv, small


def kernel(x, w_in, conv_w, a_log, dt_bias, norm_w, sinks, w_out, ln_g, ln_b, loss_target, m_w_in, m_conv_w, m_a_log, m_dt_bias, m_norm_w, m_sinks, m_w_out, m_ln_g, m_ln_b, v_w_in, v_conv_w, v_a_log, v_dt_bias, v_norm_w, v_sinks, v_w_out, v_ln_g, v_ln_b):
    small = [a_log, dt_bias, norm_w, sinks, ln_g, ln_b]
    shards = [[w_in[l].astype(BF16), w_out[l].astype(BF16), conv_w[l]] for l in range(DEPTH)]
    g_in0, = _all_gather(shards[0][:1], name="weights_all_gather_0")
    weights = [[_full_w_in(g_in0), None, None]] + [[None, None, None]] * (DEPTH - 1)

    y, saved, weights = _forward(x[0], weights, shards, small)
    dx, loss_lanes = _loss_head(y, loss_target[0], name="loss_head")
    loss = lax.psum(0.5 * jnp.sum(loss_lanes) * (1.0 / D_MODEL), ("x", "y", "c"))
    dx, g1, _ = _backward_layer(1, dx, saved[1], weights[1], ln_g[1])
    dx, _, (p_in, p_out, p_small) = _backward_layer(0, dx, saved[0], weights[0], ln_g[0], above=_contributions(g1))

    o_in = _adamw(p_in, w_in, m_w_in, v_w_in, tr=256, name="adamw_w_in")
    o_out = _adamw(p_out, w_out, m_w_out, v_w_out, tr=OUT_SHARD_ROWS, name="adamw_w_out")
    o_small = _adamw(p_small, _pack_small(conv_w, small),
                     _pack_small(m_conv_w, [m_a_log, m_dt_bias, m_norm_w, m_sinks, m_ln_g, m_ln_b]),
                     _pack_small(v_conv_w, [v_a_log, v_dt_bias, v_norm_w, v_sinks, v_ln_g, v_ln_b]),
                     tr=CS_ROWS, name="adamw_small")
    outs = []
    for k in range(4):
        cv, sm = _unpack_small(o_small[k])
        outs += [o_in[k], cv, sm[0], sm[1], sm[2], sm[3], o_out[k], sm[4], sm[5]]
    return (loss, dx[None], *outs)
```

```python
import functools

import jax
import jax.numpy as jnp
from jax import lax
from jax.experimental import pallas as pl
from jax.experimental.pallas import tpu as pltpu

F32 = jnp.float32
BF16 = jnp.bfloat16
MM_DTYPE = BF16

N_DEV = 8
D_MODEL = 1024
DEPTH = 2
A_HEADS = 4
A_HEAD_DIM = 128
A_WIDTH = 512
CONV_K = 4
CHUNK = 64
SUPER = 256
G_HEADS = 4
NEWTON_STEPS = 1
B_Q_HEADS = 8
B_KV_HEADS = 2
B_HEAD_DIM = 64
B_GROUP = 4
B_WIDTH = 512
WINDOW = 128
BLOCK = 128
IN_COLS = 3336
SHARD_COLS = IN_COLS // N_DEV
OUT_SHARD_ROWS = D_MODEL // N_DEV
CONV_SHARD_COLS = 3 * A_WIDTH // N_DEV
DEEPNORM_ALPHA = (2 * DEPTH) ** 0.25
LN_EPS = 1e-5
RMS_EPS = 1e-6
L2_EPS = 1e-6
ADAM_LR, ADAM_B1, ADAM_B2, ADAM_EPS, ADAM_WD, ADAM_STEP = 0.001, 0.9, 0.999, 1e-08, 0.01, 10

LANE = 128
L_QB, L_ZB, L_ZA, L_KB, L_VB, L_QKV, L_BA = 0, 512, 1024, 1536, 1664, 1792, 3328
L_SWA = 1792
L_MAIN = 3328
L_COLS = 3456
SMALL_SIZES = (("a_log", 4), ("dt_bias", 4), ("norm_w", 128), ("sinks", 8), ("ln_g", 1024), ("ln_b", 1024))
CS_CONV = CONV_K * CONV_SHARD_COLS
CS_ROWS = 24
VMEM_LIMIT = 48 * 1024 * 1024


def _cparams(sem=None):
    return pltpu.CompilerParams(dimension_semantics=sem, vmem_limit_bytes=VMEM_LIMIT)


def _mm(a, b):
    return jnp.dot(a.astype(MM_DTYPE), b.astype(MM_DTYPE), preferred_element_type=F32)


def _mm_nt(a, b):
    return lax.dot_general(a.astype(MM_DTYPE), b.astype(MM_DTYPE), (((1,), (1,)), ((), ())),
                           preferred_element_type=F32)


def _mm_tn(a, b):
    return lax.dot_general(a.astype(MM_DTYPE), b.astype(MM_DTYPE), (((0,), (0,)), ((), ())),
                           preferred_element_type=F32)


def _split(a):
    hi = a.astype(BF16)
    return hi, (a - hi.astype(F32)).astype(BF16)


def _hp(a2, b2):
    d = lambda p, q: jnp.dot(p, q, preferred_element_type=F32)
    return d(a2[0], b2[0]) + (d(a2[0], b2[1]) + d(a2[1], b2[0]))


def _silu(x):
    return x * jax.nn.sigmoid(x)


@jax.custom_vjp
def _stack(parts):
    return jnp.stack(parts)


_stack.defvjp(lambda parts: (jnp.stack(parts), None), lambda _, g: (tuple(g[i] for i in range(g.shape[0])),))


def _softplus(x):
    return jnp.maximum(x, 0.0) + jnp.log1p(jnp.exp(-jnp.abs(x)))


_ANY = pl.BlockSpec(memory_space=pl.ANY)


def _me():
    return lax.axis_index("x"), lax.axis_index("y"), lax.axis_index("c")


def _flat_id(pos):
    return 4 * pos[0] + 2 * pos[1] + pos[2]


def _remote(src, dst, send_sem, recv_sem, to):
    return pltpu.make_async_remote_copy(src_ref=src, dst_ref=dst, send_sem=send_sem, recv_sem=recv_sem,
                                        device_id=to, device_id_type=pl.DeviceIdType.MESH)


class _Direct:
    def __init__(self, items, bufs):
        self.items, self.bufs = list(items), list(bufs)
        self.n_src, self.n_buf = len(self.items), len(self.bufs)
        self.old = [j for j, b in enumerate(self.bufs) if not isinstance(b, jax.ShapeDtypeStruct)]
        self.args = [it[0] for it in self.items] + [self.bufs[j] for j in self.old]
        self.out_shape = [jax.ShapeDtypeStruct(b.shape, b.dtype) for b in self.bufs]
        self.scratch = [pltpu.SemaphoreType.DMA((self.n_src, N_DEV - 1)),
                        pltpu.SemaphoreType.DMA((self.n_src, N_DEV - 1)), pltpu.SemaphoreType.DMA((self.n_src,))]

    def aliases(self, in_base, out_base):
        return {in_base + self.n_src + pos: out_base + j for pos, j in enumerate(self.old)}

    def copies(self, in_refs, out_refs, sems):
        send_sems, recv_sems, local_sems = sems
        x, y, c = _me()
        me = _flat_id((x, y, c))
        peers = [(x ^ ((rel >> 2) & 1), y ^ ((rel >> 1) & 1), c ^ (rel & 1)) for rel in range(1, N_DEV)]
        local, sends, recvs = [], [], []
        for a, (_, per_dest, j, prefix, *rest) in enumerate(self.items):
            src = lambda d: in_refs[a].at[d] if per_dest else in_refs[a]
            dst = lambda s: out_refs[j].at[tuple(prefix) + (s,) + tuple(rest[0] if rest else ())]
            local.append(pltpu.make_async_copy(src(me), dst(me), local_sems.at[a]))
            for k, peer in enumerate(peers):
                pid = _flat_id(peer)
                sends.append(_remote(src(pid), dst(me), send_sems.at[a, k], recv_sems.at[a, k], peer))
                recvs.append(_remote(src(pid), dst(pid), send_sems.at[a, k], recv_sems.at[a, k], peer))
        return local, sends, recvs

    def start(self, in_refs, out_refs, sems):
        local, sends, _ = self.copies(in_refs, out_refs, sems)
        for cp in local + sends:
            cp.start()

    def wait(self, in_refs, out_refs, sems):
        local, sends, recvs = self.copies(in_refs, out_refs, sems)
        for cp in recvs:
            cp.wait_recv()
        for cp in sends:
            cp.wait_send()
        for cp in local:
            cp.wait()


def _pcall(core, *, name, grid, in_specs, out_specs, out_shape, args, sem, scratch_shapes=(), aliases=None,
           rider=None):
    n_in, n_out, n_scr = len(in_specs), len(out_specs), len(scratch_shapes)
    n_rin, n_rout = (len(rider.args), rider.n_buf) if rider else (0, 0)

    def body(*refs):
        ins, outs = refs[:n_in], refs[n_in + n_rin:n_in + n_rin + n_out]
        scr = refs[n_in + n_rin + n_out + n_rout:n_in + n_rin + n_out + n_rout + n_scr]
        if rider:
            r_refs = (refs[n_in:n_in + rider.n_src], refs[n_in + n_rin + n_out:n_in + n_rin + n_out + n_rout],
                      refs[n_in + n_rin + n_out + n_rout + n_scr:])
            ids = [pl.program_id(d) for d in range(len(grid))]
            first = functools.reduce(lambda p, q: p & q, [i == 0 for i in ids])
            last = functools.reduce(lambda p, q: p & q, [i == g - 1 for i, g in zip(ids, grid)])
            pl.when(first)(lambda: rider.start(*r_refs))
        core(ins, outs, scr)
        if rider:
            pl.when(last)(lambda: rider.wait(*r_refs))

    aliases = dict(aliases or {})
    if rider:
        sem = ("arbitrary",) * len(grid)
        aliases.update(rider.aliases(n_in, n_out))
    return pl.pallas_call(
        body, name=name, grid=grid, in_specs=list(in_specs) + [_ANY] * n_rin,
        out_specs=list(out_specs) + [_ANY] * n_rout,
        out_shape=list(out_shape) + (rider.out_shape if rider else []),
        scratch_shapes=list(scratch_shapes) + (rider.scratch if rider else []),
        input_output_aliases=aliases, compiler_params=_cparams(sem),
    )(*args, *(rider.args if rider else []))


def _exchange(direct, *, name):
    n_in = len(direct.args)

    def body(*refs):
        r_refs = refs[:direct.n_src], refs[n_in:n_in + direct.n_buf], refs[n_in + direct.n_buf:]
        direct.start(*r_refs)
        direct.wait(*r_refs)

    return pl.pallas_call(
        body, name=name, in_specs=[_ANY] * n_in, out_specs=[_ANY] * direct.n_buf, out_shape=direct.out_shape,
        input_output_aliases=direct.aliases(0, 0), scratch_shapes=direct.scratch,
    )(*direct.args)


def _matmul(a, b, *, form, tm, tn, tk, name, add=None, add_scale=1.0, extra=None, rider=None, a_cols=None):
    if form == "nn":
        (m, kk), n = a.shape, b.shape[1]
        a_spec = pl.BlockSpec((tm, tk), lambda i, j, k: (i, k))
        b_spec = pl.BlockSpec((tk, tn), lambda i, j, k: (k, j))
        dn = (((1,), (0,)), ((), ()))
    elif form == "nt":
        (m, kk), n = a.shape, b.shape[0]
        a_spec = pl.BlockSpec((tm, tk), lambda i, j, k: (i, k))
        b_spec = pl.BlockSpec((tn, tk), lambda i, j, k: (j, k))
        dn = (((1,), (1,)), ((), ()))
    else:
        kk, n = a.shape[0], b.shape[1]
        m0, m = a_cols or (0, a.shape[1])
        assert m0 % tm == 0
        a_spec = pl.BlockSpec((tk, tm), lambda i, j, k: (k, i + m0 // tm))
        b_spec = pl.BlockSpec((tk, tn), lambda i, j, k: (k, j))
        dn = (((0,), (0,)), ((), ()))
    assert m % tm == 0 and n % tn == 0 and kk % tk == 0, (name, m, n, kk)
    has_add, has_extra = add is not None, extra is not None

    def core(ins, outs, _):
        a_ref, b_ref = ins[:2]
        o_ref = outs[0]
        rest = ins[2:]
        k = pl.program_id(2)
        p = lax.dot_general(a_ref[...].astype(MM_DTYPE), b_ref[...].astype(MM_DTYPE), dn,
                            preferred_element_type=F32)

        @pl.when(k == 0)
        def _():
            first = p
            pos = 0
            if has_extra:
                first = first + _mm_nt(rest[0][...], rest[1][...])
                pos = 2
            if has_add:
                first = first + add_scale * rest[pos][...]
            o_ref[...] = first

        @pl.when(k > 0)
        def _():
            o_ref[...] += p

    in_specs = [a_spec, b_spec]
    args = [a, b]
    if has_extra:
        a2, b2, idx = extra
        in_specs += [pl.BlockSpec((tm, LANE), lambda i, j, k: (i, 0)),
                     pl.BlockSpec((tn, LANE), lambda i, j, k: (j, idx))]
        args += [a2, b2]
    if has_add:
        in_specs.append(pl.BlockSpec((tm, tn), lambda i, j, k: (i, j)))
        args.append(add)
    res = _pcall(core, name=name, grid=(m // tm, n // tn, kk // tk), in_specs=in_specs,
                 out_specs=[pl.BlockSpec((tm, tn), lambda i, j, k: (i, j))],
                 out_shape=[jax.ShapeDtypeStruct((m, n), F32)], args=args,
                 sem=("parallel", "parallel", "arbitrary"), rider=rider)
    return res if rider else res[0]


def _shift_down(x, k, row):
    return jnp.where(row >= k, pltpu.roll(x, k, 0), 0.0)


def _shift_up(x, k, row, t_len):
    return jnp.where(row < t_len - k, pltpu.roll(x, t_len - k, 0), 0.0)


def _conv_slab(x, w, row):
    return (w[3:4] * x + w[2:3] * _shift_down(x, 1, row) + w[1:2] * _shift_down(x, 2, row)
            + w[0:1] * _shift_down(x, 3, row))


def _prep_fwd(h, conv_w, *, name):
    t_len = h.shape[0]

    def body(x_ref, w_ref, o_ref):
        s = pl.program_id(0)
        row = lax.broadcasted_iota(jnp.int32, (t_len, LANE), 0)
        y = _silu(_conv_slab(x_ref[...], w_ref[...], row))
        rs = lax.rsqrt(jnp.sum(y * y, axis=-1, keepdims=True) + L2_EPS)
        scale = jnp.where(s < A_HEADS, A_HEAD_DIM ** -0.5, 1.0)
        o_ref[...] = jnp.where(s < 2 * A_HEADS, y * rs * scale, y)

    return pl.pallas_call(
        body, name=name, grid=(12,),
        in_specs=[pl.BlockSpec((t_len, LANE), lambda s: (0, L_QKV // LANE + s)),
                  pl.BlockSpec((8, LANE), lambda s: (0, s))],
        out_specs=pl.BlockSpec((t_len, LANE), lambda s: (0, s)),
        out_shape=jax.ShapeDtypeStruct((t_len, 3 * A_WIDTH), F32),
        compiler_params=_cparams(("parallel",)),
    )(h, conv_w)


def _prep_bwd(h, conv_w, d_out, dh, *, name):
    t_len = h.shape[0]

    def body(x_ref, w_ref, g_ref, dh_in, dx_ref, dw_ref):
        del dh_in
        s = pl.program_id(0)
        row = lax.broadcasted_iota(jnp.int32, (t_len, LANE), 0)
        x = x_ref[...]
        w = w_ref[...]
        c = _conv_slab(x, w, row)
        sg = jax.nn.sigmoid(c)
        y = c * sg
        g = g_ref[0]
        rs = lax.rsqrt(jnp.sum(y * y, axis=-1, keepdims=True) + L2_EPS)
        scale = jnp.where(s < A_HEADS, A_HEAD_DIM ** -0.5, 1.0)
        dy_n = scale * (rs * g - y * (rs * rs * rs) * jnp.sum(g * y, axis=-1, keepdims=True))
        dy = jnp.where(s < 2 * A_HEADS, dy_n, g)
        dc = dy * (sg * (1.0 + c * (1.0 - sg)))
        dx_ref[...] = (w[3:4] * dc + w[2:3] * _shift_up(dc, 1, row, t_len)
                       + w[1:2] * _shift_up(dc, 2, row, t_len) + w[0:1] * _shift_up(dc, 3, row, t_len))
        dws = [jnp.sum(dc * _shift_down(x, 3 - j, row), axis=0, keepdims=True) if j < 3
               else jnp.sum(dc * x, axis=0, keepdims=True) for j in range(CONV_K)]
        dw_ref[...] = jnp.concatenate(dws + [jnp.zeros((8 - CONV_K, LANE), F32)], axis=0)

    slab = pl.BlockSpec((t_len, LANE), lambda s: (0, L_QKV // LANE + s))
    return pl.pallas_call(
        body, name=name, grid=(12,),
        in_specs=[slab, pl.BlockSpec((8, LANE), lambda s: (0, s)),
                  pl.BlockSpec((1, t_len, LANE), lambda s: (s // A_HEADS, 0, s % A_HEADS)), _ANY],
        out_specs=[slab, pl.BlockSpec((8, LANE), lambda s: (0, s))],
        out_shape=[jax.ShapeDtypeStruct((t_len, L_MAIN), F32), jax.ShapeDtypeStruct((8, 3 * A_WIDTH), F32)],
        input_output_aliases={3: 0},
        compiler_params=_cparams(("parallel",)),
    )(h, conv_w, d_out, dh)


N_LEVELS = 5
MF_TRIL, MF_STRIL, MF_DIAG8, MF_LOW16, MF_EYE = 0, 1, 2, 3, 3 + N_LEVELS
MB_CUM, MB_CUM_T, MB_TOT = 0, 1, 2


def _gdn_masks():
    r = lax.broadcasted_iota(jnp.int32, (SUPER, SUPER), 0)
    c = lax.broadcasted_iota(jnp.int32, (SUPER, SUPER), 1)
    same = lambda shift: (r >> shift) == (c >> shift)
    ninf = lambda m: jnp.where(m, 0.0, -jnp.inf).astype(F32)
    one = lambda m: m.astype(F32)
    mf = jnp.stack([ninf(r >= c), ninf(r > c), one(same(3))]
                   + [one(same(4 + lv) & jnp.logical_not(same(3 + lv))) for lv in range(N_LEVELS)] + [one(r == c)])
    mb = jnp.stack([one(r >= c), one(r <= c), jnp.ones((SUPER, SUPER), F32)]).astype(BF16)
    return mf, mb


def _tri_inv_impl(a, mf):
    d = lambda p, q: jnp.dot(p.astype(BF16), q.astype(BF16), preferred_element_type=F32)
    dd = lambda p, q: jnp.dot(p, q, preferred_element_type=F32)
    eye = mf[MF_EYE]
    a0 = a * mf[MF_DIAG8]
    a2 = d(a0, a0)
    a4 = d(a2, a2)
    t = d(d(eye - a0, eye + a2), eye + a4)
    for level in range(N_LEVELS):
        t = t - d(d(t, a * mf[MF_LOW16 + level]), t)
    a_hi, a_lo = _split(a)
    for _ in range(NEWTON_STEPS):
        t_hi, t_lo = _split(t)
        resid = (eye - t) - (dd(a_hi, t_hi) + (dd(a_hi, t_lo) + dd(a_lo, t_hi)))
        r_hi, r_lo = _split(resid)
        t = t + (dd(t_hi, r_hi) + dd(t_hi, r_lo))
    return t


@jax.custom_vjp
def _wy_apply(a, rhs, t):
    return _mm(t, rhs)


def _wy_apply_fwd(a, rhs, t):
    x = _mm(t, rhs)
    return x, (t, x)


def _wy_apply_bwd(res, dx):
    t, x = res
    d_rhs = _mm_tn(t, dx)
    return -_mm_nt(d_rhs, x), d_rhs, jnp.zeros_like(t)


_wy_apply.defvjp(_wy_apply_fwd, _wy_apply_bwd)


@functools.partial(jax.custom_vjp, nondiff_argnums=(1,))
def _lane_roll(x, shift):
    return pltpu.roll(x, shift % LANE, 1)


_lane_roll.defvjp(lambda x, shift: (_lane_roll(x, shift), None), lambda shift, _, g: (_lane_roll(g, -shift),))


def _mask_times_lanes(x, mask):
    lane = lax.broadcasted_iota(jnp.int32, (1, LANE), 1)
    x = jnp.where(lane < A_HEADS, x, 0.0)
    x1 = x.astype(BF16).astype(F32)
    x2 = (x - x1).astype(BF16).astype(F32)
    x3 = (x - x1 - x2).astype(BF16).astype(F32)
    pieces = x1 + pltpu.roll(x2, A_HEADS, 1) + pltpu.roll(x3, 2 * A_HEADS, 1)
    res = jnp.dot(mask, pieces.astype(BF16), preferred_element_type=F32)
    return res + pltpu.roll(res, LANE - A_HEADS, 1) + pltpu.roll(res, LANE - 2 * A_HEADS, 1)


@jax.custom_vjp
def _chunk_sums(g, mb):
    return _mask_times_lanes(g, mb[MB_CUM]), _mask_times_lanes(g, mb[MB_TOT])


def _chunk_sums_fwd(g, mb):
    return _chunk_sums(g, mb), mb


def _chunk_sums_bwd(mb, d):
    lane = lax.broadcasted_iota(jnp.int32, (1, LANE), 1)
    dg = _mask_times_lanes(d[0], mb[MB_CUM_T]) + _mask_times_lanes(d[1], mb[MB_TOT])
    return jnp.where(lane < A_HEADS, dg, 0.0), jnp.zeros_like(mb)


_chunk_sums.defvjp(_chunk_sums_fwd, _chunk_sums_bwd)


def _gdn_gates(ba, alog, dtb, mb):
    beta = jax.nn.sigmoid(ba)
    g = -jnp.exp(alog) * _softplus(_lane_roll(ba, -A_HEADS) + dtb)
    gc, gl = _chunk_sums(g, mb)
    return beta, gc, gl, gc.T


def _gdn_block(s, q, k, v, z, gates, nw, h, t_known, mf):
    n = q.shape[0]
    beta_all, gc_all, gl_all, gct_all = gates
    lane = lax.broadcasted_iota(jnp.int32, (1, LANE), 1)
    sub = lax.broadcasted_iota(jnp.int32, (LANE, 1), 0)
    col = lambda x: jnp.sum(jnp.where(lane == h, x, 0.0), axis=1, keepdims=True)
    wide = lambda c: jnp.broadcast_to(c, (n, LANE))
    gc, gl = col(gc_all), col(gl_all)
    gc_row = jnp.sum(jnp.where(sub == h, gct_all, 0.0), axis=0, keepdims=True)
    beta_w, eg_w = wide(col(beta_all)), wide(jnp.exp(gc))
    diff = gc - gc_row
    decay = jnp.exp(diff + mf[MF_TRIL])
    kb = k * beta_w
    a_mat = _mm_nt(kb, k) * jnp.exp(diff + mf[MF_STRIL])
    rhs = jnp.concatenate([v * beta_w, kb * eg_w], axis=1)
    if t_known is None:
        t_mat = _tri_inv_impl(a_mat, mf)
        uw = _mm(t_mat, rhs)
    else:
        t_mat = t_known
        uw = _wy_apply(a_mat, rhs, t_known)
    u, w = uw[:, :LANE], uw[:, LANE:]
    qk = _mm_nt(q, k) * decay
    q_dec = q * eg_w
    k_dec = k * wide(jnp.exp(gl - gc))
    v_new = u - _mm(w, s)
    o = _mm(q_dec, s) + _mm(qk, v_new)
    s = s * jnp.exp(gl[0:1]) + _mm_tn(k_dec, v_new)
    o = o * lax.rsqrt(jnp.mean(o * o, axis=-1, keepdims=True) + RMS_EPS) * nw
    return o * _silu(z), s, t_mat


def _gdn_fwd(qkv, h, alog, dtb, nw, ycat, *, name, rider=None):
    t_len = qkv.shape[0]
    nsc = t_len // SUPER

    def core(ins, outs, scr):
        q_ref, k_ref, v_ref, z_ref, ba_ref, al_ref, dt_ref, nw_ref, mf_ref, mb_ref, _ = ins
        y_ref, sin_ref, t_ref = outs
        s_scr, = scr
        sc, pg = pl.program_id(0), pl.program_id(1)
        heads = [pg * G_HEADS + hh for hh in range(G_HEADS)]

        @pl.when(sc == 0)
        def _():
            for hd in heads:
                s_scr[hd] = jnp.zeros((A_HEAD_DIM, A_HEAD_DIM), F32)

        per_head = lambda ref: jnp.stack([ref[:, hh * LANE:(hh + 1) * LANE] for hh in range(G_HEADS)])
        states = jnp.stack([s_scr[hd] for hd in heads])
        gates = _gdn_gates(ba_ref[...], al_ref[...], dt_ref[...], mb_ref[...])
        fn = jax.vmap(_gdn_block, in_axes=(0, 0, 0, 0, 0, None, None, 0, None, None))
        y, s_new, t_mat = fn(states, per_head(q_ref), per_head(k_ref), per_head(v_ref), per_head(z_ref),
                             gates, nw_ref[...], jnp.stack(heads), None, mf_ref[...])
        sin_ref[0] = states
        t_ref[0] = t_mat
        for hh, hd in enumerate(heads):
            y_ref[:, hh * LANE:(hh + 1) * LANE] = y[hh]
            s_scr[hd] = s_new[hh]

    gw = G_HEADS * LANE
    blk = lambda off: pl.BlockSpec((SUPER, gw), lambda sc, pg: (sc, off // G_HEADS + pg))
    row = pl.BlockSpec((1, LANE), lambda sc, pg: (0, 0))
    mf, mb = _gdn_masks()
    whole = lambda a: pl.BlockSpec(a.shape, lambda sc, pg: (0, 0, 0))
    return _pcall(
        core, name=name, grid=(nsc, A_HEADS // G_HEADS),
        in_specs=[blk(0), blk(4), blk(8), blk(L_ZA // LANE),
                  pl.BlockSpec((SUPER, LANE), lambda sc, pg: (sc, L_BA // LANE)), row, row, row,
                  whole(mf), whole(mb), _ANY],
        out_specs=[blk(0),
                   pl.BlockSpec((1, G_HEADS, A_HEAD_DIM, A_HEAD_DIM), lambda sc, pg: (sc, pg, 0, 0)),
                   pl.BlockSpec((1, G_HEADS, SUPER, SUPER), lambda sc, pg: (sc, pg, 0, 0))],
        out_shape=[jax.ShapeDtypeStruct((t_len, D_MODEL), F32),
                   jax.ShapeDtypeStruct((nsc, A_HEADS, A_HEAD_DIM, A_HEAD_DIM), F32),
                   jax.ShapeDtypeStruct((nsc, A_HEADS, SUPER, SUPER), F32)],
        scratch_shapes=[pltpu.VMEM((A_HEADS, A_HEAD_DIM, A_HEAD_DIM), F32)],
        aliases={10: 0}, sem=("arbitrary", "arbitrary"), rider=rider,
        args=(qkv, qkv, qkv, h, h, alog, dtb, nw, mf, mb, ycat))


def _gdn_bwd(qkv, h, alog, dtb, nw, s_in, t_in, dycat, dh, *, name, rider=None):
    t_len = qkv.shape[0]
    nsc = t_len // SUPER

    def core(ins, outs, scr):
        (q_ref, k_ref, v_ref, z_ref, ba_ref, al_ref, dt_ref, nw_ref, sin_ref, t_ref, dy_ref, mf_ref, mb_ref,
         _) = ins
        dz_ref, dqkv_ref, dba_ref, dal_ref, ddt_ref, dnw_ref = outs
        ds_scr, = scr
        i, pg = pl.program_id(0), pl.program_id(1)

        @pl.when((i == 0) & (pg == 0))
        def _():
            dal_ref[...] = jnp.zeros_like(dal_ref)
            ddt_ref[...] = jnp.zeros_like(ddt_ref)
            dnw_ref[...] = jnp.zeros_like(dnw_ref)

        @pl.when(pg == 0)
        def _():
            dba_ref[...] = jnp.zeros_like(dba_ref)

        heads = [pg * G_HEADS + hh for hh in range(G_HEADS)]

        @pl.when(i == 0)
        def _():
            for hd in heads:
                ds_scr[hd] = jnp.zeros((A_HEAD_DIM, A_HEAD_DIM), F32)

        per_head = lambda ref: jnp.stack([ref[:, hh * LANE:(hh + 1) * LANE] for hh in range(G_HEADS)])
        d_states = jnp.stack([ds_scr[hd] for hd in heads])
        head_ids = jnp.stack(heads)
        t_known, mf, mb = t_ref[0], mf_ref[...], mb_ref[...]

        def fn(s, q, k, v, z, ba, alog, dtb, nw):
            gates = _gdn_gates(ba, alog, dtb, mb)
            one = lambda s, q, k, v, z, t, h: _gdn_block(s, q, k, v, z, gates, nw, h, t, mf)[:2]
            return jax.vmap(one)(s, q, k, v, z, t_known, head_ids)

        _, vjp = jax.vjp(fn, sin_ref[0], per_head(q_ref), per_head(k_ref), per_head(v_ref), per_head(z_ref),
                         ba_ref[...], al_ref[...], dt_ref[...], nw_ref[...])
        ds, dq, dk, dv, dz, dba, dal, ddt, dnw = vjp((per_head(dy_ref), d_states))
        for hh, hd in enumerate(heads):
            cols = slice(hh * LANE, (hh + 1) * LANE)
            ds_scr[hd] = ds[hh]
            dqkv_ref[0, :, cols] = dq[hh]
            dqkv_ref[1, :, cols] = dk[hh]
            dqkv_ref[2, :, cols] = dv[hh]
            dz_ref[:, cols] = dz[hh]
        dba_ref[...] += dba
        dal_ref[...] += dal
        ddt_ref[...] += ddt
        dnw_ref[...] += dnw

    rev = lambda i: nsc - 1 - i
    gw = G_HEADS * LANE
    blk = lambda off: pl.BlockSpec((SUPER, gw), lambda i, pg: (rev(i), off // G_HEADS + pg))
    row = pl.BlockSpec((1, LANE), lambda i, pg: (0, 0))
    ba_blk = lambda off: pl.BlockSpec((SUPER, LANE), lambda i, pg: (rev(i), off))
    mf, mb = _gdn_masks()
    whole = lambda a: pl.BlockSpec(a.shape, lambda i, pg: (0, 0, 0))
    return _pcall(
        core, name=name, grid=(nsc, A_HEADS // G_HEADS),
        in_specs=[blk(0), blk(4), blk(8), blk(L_ZA // LANE), ba_blk(L_BA // LANE), row, row, row,
                  pl.BlockSpec((1, G_HEADS, A_HEAD_DIM, A_HEAD_DIM), lambda i, pg: (rev(i), pg, 0, 0)),
                  pl.BlockSpec((1, G_HEADS, SUPER, SUPER), lambda i, pg: (rev(i), pg, 0, 0)),
                  blk(0), whole(mf), whole(mb), _ANY],
        out_specs=[blk(L_ZA // LANE),
                   pl.BlockSpec((3, SUPER, gw), lambda i, pg: (0, rev(i), pg)),
                   ba_blk(0), row, row, row],
        out_shape=[jax.ShapeDtypeStruct((t_len, L_MAIN), F32), jax.ShapeDtypeStruct((3, t_len, A_WIDTH), F32),
                   jax.ShapeDtypeStruct((t_len, LANE), F32)] + [jax.ShapeDtypeStruct((1, LANE), F32)] * 3,
        scratch_shapes=[pltpu.VMEM((A_HEADS, A_HEAD_DIM, A_HEAD_DIM), F32)],
        aliases={13: 0}, sem=("arbitrary", "arbitrary"), rider=rider,
        args=(qkv, qkv, qkv, h, h, alog, dtb, nw, s_in, t_in, dycat, mf, mb, dh))


def _swa_block(q, kp, kc, vp, vc, z, sinks, first):
    rows = B_GROUP * BLOCK
    ri = lax.broadcasted_iota(jnp.int32, (rows, 2 * BLOCK), 0)
    si = lax.broadcasted_iota(jnp.int32, (rows, 2 * BLOCK), 1)
    dist = (ri & (BLOCK - 1)) + BLOCK - si
    bias = jnp.where((dist >= 0) & (dist < WINDOW) & ((si >= BLOCK) | jnp.logical_not(first)), 0.0, -jnp.inf)
    dist_f = dist.astype(F32)
    head_of_row = lax.broadcasted_iota(jnp.int32, (rows, 1), 0) >> 7

    def group(j):
        cs = slice(j * B_HEAD_DIM, (j + 1) * B_HEAD_DIM)
        heads = range(j * B_GROUP, (j + 1) * B_GROUP)
        qs = jnp.concatenate([q[:, hq * B_HEAD_DIM:(hq + 1) * B_HEAD_DIM] for hq in heads], axis=0)
        kk = jnp.concatenate([kp[:, cs], kc[:, cs]], axis=0)
        vv = jnp.concatenate([vp[:, cs], vc[:, cs]], axis=0)
        sink = jnp.concatenate([jnp.broadcast_to(sinks[:, hq:hq + 1], (BLOCK, 1)) for hq in heads], axis=0)
        slope = sum(jnp.where(head_of_row == gi, 2.0 ** (-8.0 * (hq + 1) / B_Q_HEADS), 0.0)
                    for gi, hq in enumerate(heads))
        return qs, kk, vv, sink, slope

    def attend(qs, kk, vv, sink, slope):
        sc = _mm_nt(qs, kk) * (B_HEAD_DIM ** -0.5) - slope * dist_f + bias
        m = lax.stop_gradient(jnp.maximum(jnp.max(sc, axis=-1, keepdims=True), sink))
        p = jnp.exp(sc - m)
        inv = 1.0 / (jnp.sum(p, axis=-1, keepdims=True) + jnp.exp(sink - m))
        return _mm(p * inv, vv)

    o = jax.vmap(attend)(*[_stack(t) for t in zip(*[group(j) for j in range(B_KV_HEADS)])])
    outs = [o[j, gi * BLOCK:(gi + 1) * BLOCK] for j in range(B_KV_HEADS) for gi in range(B_GROUP)]
    return jnp.concatenate(outs, axis=1) * _silu(z)


def _swa_specs(idx):
    wide = lambda off: pl.BlockSpec((BLOCK, B_WIDTH), lambda n: (idx(n), off))
    cur = lambda off: pl.BlockSpec((BLOCK, LANE), lambda n: (idx(n), off))
    prev = lambda off: pl.BlockSpec((BLOCK, LANE), lambda n: (jnp.maximum(idx(n) - 1, 0), off))
    return [wide(L_QB // B_WIDTH), prev(L_KB // LANE), cur(L_KB // LANE), prev(L_VB // LANE), cur(L_VB // LANE),
            wide(L_ZB // B_WIDTH), pl.BlockSpec((1, LANE), lambda n: (0, 0))]


def _swa_fwd(h, sinks, *, name, rider=None):
    t_len = h.shape[0]
    nb = t_len // BLOCK

    def core(ins, outs, _):
        q_ref, kp_ref, kc_ref, vp_ref, vc_ref, z_ref, s_ref = ins
        outs[0][...] = _swa_block(q_ref[...], kp_ref[...], kc_ref[...], vp_ref[...], vc_ref[...], z_ref[...],
                                  s_ref[...], pl.program_id(0) == 0)

    res = _pcall(core, name=name, grid=(nb,), in_specs=_swa_specs(lambda n: n),
                 out_specs=[pl.BlockSpec((BLOCK, B_WIDTH), lambda n: (n, 1))],
                 out_shape=[jax.ShapeDtypeStruct((t_len, D_MODEL), F32)], sem=("parallel",), rider=rider,
                 args=(h, h, h, h, h, h, sinks))
    return res if rider else res[0]


def _swa_bwd(h, sinks, dycat, *, name, rider=None):
    t_len = h.shape[0]
    nb = t_len // BLOCK

    def core(ins, outs, scr):
        q_ref, kp_ref, kc_ref, vp_ref, vc_ref, z_ref, s_ref, dy_ref = ins
        dh_ref, dsk_ref = outs
        ck_scr, cv_scr = scr
        i = pl.program_id(0)
        n = nb - 1 - i

        @pl.when(i == 0)
        def _():
            ck_scr[...] = jnp.zeros_like(ck_scr)
            cv_scr[...] = jnp.zeros_like(cv_scr)
            dsk_ref[...] = jnp.zeros_like(dsk_ref)

        fn = functools.partial(_swa_block, first=(n == 0))
        _, vjp = jax.vjp(fn, q_ref[...], kp_ref[...], kc_ref[...], vp_ref[...], vc_ref[...], z_ref[...], s_ref[...])
        dq, dkp, dkc, dvp, dvc, dz, dsk = vjp(dy_ref[...])
        dh_ref[:, L_QB:L_QB + B_WIDTH] = dq
        dh_ref[:, L_ZB:L_ZB + B_WIDTH] = dz
        dh_ref[:, L_ZA:L_ZA + A_WIDTH] = jnp.zeros((BLOCK, A_WIDTH), F32)
        dh_ref[:, L_KB:L_KB + LANE] = dkc + ck_scr[...]
        dh_ref[:, L_VB:L_VB + LANE] = dvc + cv_scr[...]
        ck_scr[...] = dkp
        cv_scr[...] = dvp
        dsk_ref[...] += dsk

    rev = lambda i: nb - 1 - i
    return _pcall(
        core, name=name, grid=(nb,),
        in_specs=_swa_specs(rev) + [pl.BlockSpec((BLOCK, B_WIDTH), lambda i: (rev(i), 1))],
        out_specs=[pl.BlockSpec((BLOCK, L_SWA), lambda i: (rev(i), 0)), pl.BlockSpec((1, LANE), lambda i: (0, 0))],
        out_shape=[jax.ShapeDtypeStruct((t_len, L_MAIN), F32), jax.ShapeDtypeStruct((1, LANE), F32)],
        scratch_shapes=[pltpu.VMEM((BLOCK, LANE), F32), pltpu.VMEM((BLOCK, LANE), F32)],
        sem=("arbitrary",), rider=rider, args=(h, h, h, h, h, h, sinks, dycat))


def _out_ln_fwd(ycat, w_out, x, ln_g, ln_b, *, name, tm=256):
    t_len = x.shape[0]

    def body(y_ref, w_ref, x_ref, g_ref, b_ref, r_ref, o_ref):
        r = DEEPNORM_ALPHA * x_ref[...] + _mm(y_ref[...], w_ref[...])
        r_ref[...] = r
        mu = jnp.mean(r, axis=-1, keepdims=True)
        d = r - mu
        var = jnp.mean(d * d, axis=-1, keepdims=True)
        o_ref[...] = d * lax.rsqrt(var + LN_EPS) * g_ref[...] + b_ref[...]

    tile = pl.BlockSpec((tm, D_MODEL), lambda i: (i, 0))
    vec = pl.BlockSpec((1, D_MODEL), lambda i: (0, 0))
    return pl.pallas_call(
        body, name=name, grid=(t_len // tm,),
        in_specs=[tile, pl.BlockSpec((D_MODEL, D_MODEL), lambda i: (0, 0)), tile, vec, vec],
        out_specs=[tile, tile],
        out_shape=[jax.ShapeDtypeStruct((t_len, D_MODEL), F32)] * 2,
        compiler_params=_cparams(("parallel",)),
    )(ycat, w_out, x, ln_g, ln_b)


def _ln_bwd(dxn, r, ln_g, *, name, tm=256):
    t_len = r.shape[0]

    def body(dx_ref, r_ref, g_ref, dr_ref, dg_ref, db_ref):
        @pl.when(pl.program_id(0) == 0)
        def _():
            dg_ref[...] = jnp.zeros_like(dg_ref)
            db_ref[...] = jnp.zeros_like(db_ref)

        rr = r_ref[...]
        dx = dx_ref[...]
        mu = jnp.mean(rr, axis=-1, keepdims=True)
        d = rr - mu
        rstd = lax.rsqrt(jnp.mean(d * d, axis=-1, keepdims=True) + LN_EPS)
        xh = d * rstd
        dxh = dx * g_ref[...]
        dr_ref[...] = rstd * (dxh - jnp.mean(dxh, axis=-1, keepdims=True)
                              - xh * jnp.mean(dxh * xh, axis=-1, keepdims=True))
        dg_ref[...] += jnp.sum(dx * xh, axis=0, keepdims=True)
        db_ref[...] += jnp.sum(dx, axis=0, keepdims=True)

    tile = pl.BlockSpec((tm, D_MODEL), lambda i: (i, 0))
    vec = pl.BlockSpec((1, D_MODEL), lambda i: (0, 0))
    return pl.pallas_call(
        body, name=name, grid=(t_len // tm,),
        in_specs=[tile, tile, vec], out_specs=[tile, vec, vec],
        out_shape=[jax.ShapeDtypeStruct((t_len, D_MODEL), F32), jax.ShapeDtypeStruct((1, D_MODEL), F32),
                   jax.ShapeDtypeStruct((1, D_MODEL), F32)],
        compiler_params=_cparams(("arbitrary",)),
    )(dxn, r, ln_g)


def _loss_head(y, target, *, name, tm=256):
    t_len = y.shape[0]

    def body(y_ref, t_ref, d_ref, l_ref):
        @pl.when(pl.program_id(0) == 0)
        def _():
            l_ref[...] = jnp.zeros_like(l_ref)

        e = y_ref[...] - t_ref[...]
        d_ref[...] = e * (1.0 / D_MODEL)
        l_ref[...] += jnp.sum(e * e, axis=0, keepdims=True)

    tile = pl.BlockSpec((tm, D_MODEL), lambda i: (i, 0))
    vec = pl.BlockSpec((1, D_MODEL), lambda i: (0, 0))
    return pl.pallas_call(
        body, name=name, grid=(t_len // tm,), in_specs=[tile, tile], out_specs=[tile, vec],
        out_shape=[jax.ShapeDtypeStruct((t_len, D_MODEL), F32), jax.ShapeDtypeStruct((1, D_MODEL), F32)],
        compiler_params=_cparams(("arbitrary",)),
    )(y, target)


def _pad_row(v):
    return jnp.zeros((1, LANE), F32).at[0, :v.shape[0]].set(v)


def _to_layout(w_full):
    s = lambda a, b: w_full[..., a:b]
    pad = jnp.zeros(w_full.shape[:-1] + (LANE - 2 * A_HEADS,), w_full.dtype)
    return jnp.concatenate([s(2056, 2568), s(2824, 3336), s(1536, 2048), s(2568, 2696), s(2696, 2824), s(0, 1536),
                            s(2048, 2056), pad], axis=-1)


def _from_layout(g_main, g_ba):
    s = lambda a, b: g_main[..., a:b]
    return jnp.concatenate([s(L_QKV, L_QKV + 1536), s(L_ZA, L_ZA + 512), g_ba[..., :2 * A_HEADS],
                            s(L_QB, L_QB + 512), s(L_KB, L_KB + 128), s(L_VB, L_VB + 128), s(L_ZB, L_ZB + 512)],
                           axis=-1)


_REGIONS = ((0, 1536, L_QKV), (1536, 2048, L_ZA), (2048, 2056, L_BA), (2056, 2568, L_QB), (2568, 2696, L_KB),
            (2696, 2824, L_VB), (2824, 3336, L_ZB))


def _shard_pieces(regions):
    for a, b, off in regions:
        for d in range(N_DEV):
            lo, hi = max(a, d * SHARD_COLS), min(b, (d + 1) * SHARD_COLS)
            if lo < hi:
                yield d, lo - d * SHARD_COLS, hi - d * SHARD_COLS, off + lo - a


def _as_list(r):
    return list(r) if isinstance(r, (list, tuple)) else [r]


def _gathered(shard):
    return jax.ShapeDtypeStruct((N_DEV,) + shard.shape, shard.dtype)


def _full_w_in(g_in):
    by_offset = sorted(_shard_pieces(_REGIONS), key=lambda p: p[3])
    pieces = [g_in[d, :, lo:hi] for d, lo, hi, _ in by_offset]
    pad = jnp.zeros((D_MODEL, L_COLS - L_BA - 2 * A_HEADS), g_in.dtype)
    return jnp.concatenate(pieces + [pad], axis=1)


def _full_conv(g_conv):
    return jnp.pad(g_conv.transpose(1, 0, 2).reshape(CONV_K, 3 * A_WIDTH), ((0, 8 - CONV_K), (0, 0)))


def _forward(x, weights, shards, small):
    a_log, dt_bias, norm_w, sinks, ln_g, ln_b = small
    tm = min(512, x.shape[0])
    saved, weights = [], [list(w) for w in weights]
    whole = lambda arrs: _Direct([(a, False, j, ()) for j, a in enumerate(arrs)], [_gathered(a) for a in arrs])
    for l in range(DEPTH):
        rider = whole(shards[l][1:]) if weights[l][1] is None else None
        h, *got = _as_list(_matmul(x, weights[l][0], form="nn", tm=tm, tn=1152, tk=D_MODEL, name=f"in_proj_{l}",
                                   rider=rider))
        if rider:
            weights[l][1:] = [got[0].reshape(D_MODEL, D_MODEL), _full_conv(got[1])]
        w_in_l, w_out_l, conv_l = weights[l]
        qkv = _prep_fwd(h, conv_l, name=f"prep_fwd_{l}")
        al, dt, nw, sk = _pad_row(a_log[l]), _pad_row(dt_bias[l]), norm_w[l][None, :], _pad_row(sinks[l])
        ahead = l + 1 < DEPTH and weights[l + 1][0] is None
        rider = whole(shards[l + 1][1:]) if ahead else None
        ycat, *got = _as_list(_swa_fwd(h, sk, name=f"swa_fwd_{l}", rider=rider))
        if ahead:
            weights[l + 1][1:] = [got[0].reshape(D_MODEL, D_MODEL), _full_conv(got[1])]
        rider = whole(shards[l + 1][:1]) if ahead else None
        ycat, s_in, t_in, *got = _gdn_fwd(qkv, h, al, dt, nw, ycat, name=f"gdn_fwd_{l}", rider=rider)
        if ahead:
            weights[l + 1][0] = _full_w_in(got[0])
        r, xn = _out_ln_fwd(ycat, w_out_l, x, ln_g[l][None, :], ln_b[l][None, :], name=f"out_ln_{l}")
        saved.append((x, h, qkv, s_in, t_in, ycat, r, al, dt, nw, sk))
        x = xn
    return x, saved, weights


def _w_in_blocks(g_main, g_ba):
    src = lambda off: (g_ba, off - L_BA) if off >= L_BA else (g_main, off)
    blocks = [[] for _ in range(N_DEV)]
    for d, lo, hi, off in _shard_pieces(_REGIONS):
        a, o = src(off)
        blocks[d].append(a[:, o:o + hi - lo])
    return jnp.stack([jnp.concatenate(b, axis=1) for b in blocks]).astype(BF16)


def _small_blocks(g):
    c_conv = g["conv_w"].reshape(CONV_K, N_DEV, CONV_SHARD_COLS).transpose(1, 0, 2)
    c_small = [jnp.broadcast_to(g[n][None], (N_DEV,) + g[n].shape) for n, _ in SMALL_SIZES]
    return _pack_small(c_conv, c_small)


def _contributions(g):
    c_out = g["w_out"].astype(BF16).reshape(N_DEV, OUT_SHARD_ROWS, D_MODEL)
    return _w_in_blocks(*g["w_in_parts"]), c_out, _small_blocks(g)


def _backward_layer(l, dx, saved_l, weights_l, ln_g_l, above=None):
    x_in, h, qkv, s_in, t_in, ycat, r, al, dt, nw, sk = saved_l
    w_in_l, w_out_l, conv_l = weights_l
    tm = min(512, x_in.shape[0])
    dr, d_lng, d_lnb = _ln_bwd(dx, r, ln_g_l[None, :], name=f"ln_bwd_{l}")
    dycat = _matmul(dr, w_out_l, form="nt", tm=tm, tn=D_MODEL, tk=D_MODEL, name=f"out_proj_dx_{l}")
    d_wout = _matmul(ycat, dr, form="tn", tm=512, tn=D_MODEL, tk=tm, name=f"out_proj_dw_{l}")
    rider, p_in, p_out, p_small = None, None, None, None
    recv = lambda c: jax.ShapeDtypeStruct((DEPTH,) + c.shape, c.dtype)
    if above:
        c_out = d_wout.astype(BF16).reshape(N_DEV, OUT_SHARD_ROWS, D_MODEL)
        rider = _Direct([(above[1], True, 0, (l + 1,)), (above[2], True, 1, (l + 1,)), (c_out, True, 0, (l,))],
                        [recv(above[1]), recv(above[2])])
    dh, d_sk, *got = _swa_bwd(h, sk, dycat, name=f"swa_bwd_{l}", rider=rider)
    if above:
        p_out, p_small = got
        rider = _Direct([(above[0], True, 0, (l + 1,))], [recv(above[0])])
    dh, dqkv_n, dba, d_al, d_dt, d_nw, *got = _gdn_bwd(qkv, h, al, dt, nw, s_in, t_in, dycat, dh,
                                                       name=f"gdn_bwd_{l}", rider=rider)
    dh, d_conv = _prep_bwd(h, conv_l, dqkv_n, dh, name=f"prep_bwd_{l}")
    d_win_ba = _matmul(x_in, dba, form="tn", tm=D_MODEL, tn=LANE, tk=tm, name=f"in_proj_dw_ba_{l}")
    grads = dict(w_out=d_wout, conv_w=d_conv[:CONV_K], a_log=d_al[0, :A_HEADS], dt_bias=d_dt[0, :A_HEADS],
                 norm_w=d_nw[0], sinks=d_sk[0, :B_Q_HEADS], ln_g=d_lng[0], ln_b=d_lnb[0])
    dw = functools.partial(_matmul, x_in, dh, form="tn", tm=512, tn=L_MAIN // 2, tk=tm)
    if not above:
        grads["w_in_parts"] = (dw(name=f"in_proj_dw_{l}"), d_win_ba)
    else:
        p_in, = got
        half = D_MODEL // 2
        top = dw(name=f"in_proj_dw_top_{l}", a_cols=(0, half))
        rider = _Direct([(_w_in_blocks(top, d_win_ba[:half]), True, 0, (l,), (pl.ds(0, half),))], [p_in])
        bottom, p_in = dw(name=f"in_proj_dw_bottom_{l}", a_cols=(half, half), rider=rider)
        rider = _Direct([(_w_in_blocks(bottom, d_win_ba[half:]), True, 0, (l,), (pl.ds(half, half),)),
                         (_small_blocks(grads), True, 1, (l,))], [p_in, p_small])
    dx, *got = _as_list(_matmul(dh, w_in_l, form="nt", tm=tm, tn=D_MODEL, tk=L_MAIN // 2, name=f"in_proj_dx_{l}",
                                add=dr, add_scale=DEEPNORM_ALPHA, extra=(dba, w_in_l, L_BA // LANE), rider=rider))
    bufs = (got[0], p_out, got[1]) if above else None
    return dx, grads, bufs


def _all_gather(shards, *, name):
    n_arr = len(shards)

    def body(*refs):
        x_refs, out_refs = refs[:n_arr], refs[n_arr:2 * n_arr]
        send_sems, recv_sems, local_sems = refs[2 * n_arr:]
        x, y, c = _me()
        me, sibling = (x, y, c), (x, y, 1 - c)
        chips = [(1 - x, y), (x, 1 - y), (1 - x, 1 - y)]

        def copy(a, k, block, to, src=None):
            dst = out_refs[a].at[_flat_id(block)]
            return _remote(dst if src is None else src, dst, send_sems.at[a, k], recv_sems.at[a, k], to)

        mine = [pltpu.make_async_copy(x_refs[a], out_refs[a].at[_flat_id(me)], local_sems.at[a])
                for a in range(n_arr)]
        for cp in mine:
            cp.start()
        first = []
        for a in range(n_arr):
            first.append(copy(a, 0, me, sibling, src=x_refs[a]))
            first += [copy(a, 1 + j, me, (*chip, c), src=x_refs[a]) for j, chip in enumerate(chips)]
        for cp in first:
            cp.start()
        passed = []
        for j, chip in enumerate(chips):
            for a in range(n_arr):
                copy(a, 1 + j, (*chip, c), me).wait_recv()
                fwd = copy(a, 4 + j, (*chip, c), sibling)
                fwd.start()
                passed.append(fwd)
        for a in range(n_arr):
            copy(a, 0, sibling, me).wait_recv()
            for j, chip in enumerate(chips):
                copy(a, 4 + j, (*chip, 1 - c), me).wait_recv()
        for cp in first + passed:
            cp.wait_send()
        for cp in mine:
            cp.wait()

    return pl.pallas_call(
        body, name=name, in_specs=[_ANY] * n_arr, out_specs=[_ANY] * n_arr,
        out_shape=[jax.ShapeDtypeStruct((N_DEV,) + s.shape, s.dtype) for s in shards],
        scratch_shapes=[pltpu.SemaphoreType.DMA((n_arr, N_DEV - 1)), pltpu.SemaphoreType.DMA((n_arr, N_DEV - 1)),
                        pltpu.SemaphoreType.DMA((n_arr,))],
    )(*shards)


def _adamw(parts, w, m, v, *, tr, name):
    depth, rows, cols = w.shape
    c1 = 1.0 - ADAM_B1 ** ADAM_STEP
    c2 = 1.0 - ADAM_B2 ** ADAM_STEP

    def body(g_ref, w_ref, m_ref, v_ref, go_ref, d_ref, mo_ref, vo_ref):
        g = g_ref[0, 0].astype(F32)
        for s in range(1, N_DEV):
            g = g + g_ref[0, s].astype(F32)
        m_new = ADAM_B1 * m_ref[0] + (1.0 - ADAM_B1) * g
        v_new = ADAM_B2 * v_ref[0] + (1.0 - ADAM_B2) * (g * g)
        go_ref[0] = g
        mo_ref[0] = m_new
        vo_ref[0] = v_new
        d_ref[0] = -ADAM_LR * ((m_new / c1) / (jnp.sqrt(v_new / c2) + ADAM_EPS) + ADAM_WD * w_ref[0])

    tile = pl.BlockSpec((1, tr, cols), lambda l, i: (l, i, 0))
    return pl.pallas_call(
        body, name=name, grid=(depth, rows // tr),
        in_specs=[pl.BlockSpec((1, N_DEV, tr, cols), lambda l, i: (l, 0, i, 0)), tile, tile, tile],
        out_specs=[tile] * 4, out_shape=[jax.ShapeDtypeStruct(w.shape, F32)] * 4,
        compiler_params=_cparams(("parallel", "parallel")),
    )(parts, w, m, v)


def _pack_small(conv, small):
    lead = conv.shape[:-2]
    flat = jnp.concatenate([conv.reshape(lead + (CS_CONV,))] + list(small), axis=-1)
    pad = CS_ROWS * LANE - flat.shape[-1]
    flat = jnp.concatenate([flat, jnp.zeros(lead + (pad,), F32)], axis=-1)
    return flat.reshape(lead + (CS_ROWS, LANE))


def _unpack_small(p):
    flat = p.reshape(DEPTH, CS_ROWS * LANE)
    conv = flat[:, :CS_CONV].reshape(DEPTH, CONV_K, CONV_SHARD_COLS)
    small, off = [], CS_CONV
    for _, n in SMALL_SIZES:
        small.append(flat[:, off:off + n])
        off += n
    return conv, small


def kernel(x, w_in, conv_w, a_log, dt_bias, norm_w, sinks, w_out, ln_g, ln_b, loss_target, m_w_in, m_conv_w, m_a_log, m_dt_bias, m_norm_w, m_sinks, m_w_out, m_ln_g, m_ln_b, v_w_in, v_conv_w, v_a_log, v_dt_bias, v_norm_w, v_sinks, v_w_out, v_ln_g, v_ln_b):
    small = [a_log, dt_bias, norm_w, sinks, ln_g, ln_b]
    shards = [[w_in[l].astype(BF16), w_out[l].astype(BF16), conv_w[l]] for l in range(DEPTH)]
    g_in0, = _all_gather(shards[0][:1], name="weights_all_gather_0")
    weights = [[_full_w_in(g_in0), None, None]] + [[None, None, None]] * (DEPTH - 1)

    y, saved, weights = _forward(x[0], weights, shards, small)
    dx, loss_lanes = _loss_head(y, loss_target[0], name="loss_head")
    loss = lax.psum(0.5 * jnp.sum(loss_lanes) * (1.0 / D_MODEL), ("x", "y", "c"))
    dx, g1, _ = _backward_layer(1, dx, saved[1], weights[1], ln_g[1])
    dx, _, (p_in, p_out, p_small) = _backward_layer(0, dx, saved[0], weights[0], ln_g[0], above=_contributions(g1))

    o_in = _adamw(p_in, w_in, m_w_in, v_w_in, tr=256, name="adamw_w_in")
    o_out = _adamw(p_out, w_out, m_w_out, v_w_out, tr=OUT_SHARD_ROWS, name="adamw_w_out")
    o_small = _adamw(p_small, _pack_small(conv_w, small),
                     _pack_small(m_conv_w, [m_a_log, m_dt_bias, m_norm_w, m_sinks, m_ln_g, m_ln_b]),
                     _pack_small(v_conv_w, [v_a_log, v_dt_bias, v_norm_w, v_sinks, v_ln_g, v_ln_b]),
                     tr=CS_ROWS, name="adamw_small")
    outs = []
    for k in range(4):
        cv, sm = _unpack_small(o_small[k])
        outs += [o_in[k], cv, sm[0], sm[1], sm[2], sm[3], o_out[k], sm[4], sm[5]]
    return (loss, dx[None], *outs)
```

```python
import functools

import jax
import jax.numpy as jnp
from jax import lax
from jax.experimental import pallas as pl
from jax.experimental.pallas import tpu as pltpu

F32 = jnp.float32
BF16 = jnp.bfloat16
MM_DTYPE = BF16

N_DEV = 8
D_MODEL = 1024
DEPTH = 2
A_HEADS = 4
A_HEAD_DIM = 128
A_WIDTH = 512
CONV_K = 4
CHUNK = 64
SUPER = 256
G_HEADS = 4
NEWTON_STEPS = 1
B_Q_HEADS = 8
B_KV_HEADS = 2
B_HEAD_DIM = 64
B_GROUP = 4
B_WIDTH = 512
WINDOW = 128
BLOCK = 128
IN_COLS = 3336
SHARD_COLS = IN_COLS // N_DEV
OUT_SHARD_ROWS = D_MODEL // N_DEV
CONV_SHARD_COLS = 3 * A_WIDTH // N_DEV
DEEPNORM_ALPHA = (2 * DEPTH) ** 0.25
LN_EPS = 1e-5
RMS_EPS = 1e-6
L2_EPS = 1e-6
ADAM_LR, ADAM_B1, ADAM_B2, ADAM_EPS, ADAM_WD, ADAM_STEP = 0.001, 0.9, 0.999, 1e-08, 0.01, 10

LANE = 128
L_QB, L_ZB, L_ZA, L_KB, L_VB, L_QKV, L_BA = 0, 512, 1024, 1536, 1664, 1792, 3328
L_SWA = 1792
L_MAIN = 3328
L_COLS = 3456
SMALL_SIZES = (("a_log", 4), ("dt_bias", 4), ("norm_w", 128), ("sinks", 8), ("ln_g", 1024), ("ln_b", 1024))
CS_CONV = CONV_K * CONV_SHARD_COLS
CS_ROWS = 24
VMEM_LIMIT = 48 * 1024 * 1024


def _cparams(sem=None):
    return pltpu.CompilerParams(dimension_semantics=sem, vmem_limit_bytes=VMEM_LIMIT)


def _mm(a, b):
    return jnp.dot(a.astype(MM_DTYPE), b.astype(MM_DTYPE), preferred_element_type=F32)


def _mm_nt(a, b):
    return lax.dot_general(a.astype(MM_DTYPE), b.astype(MM_DTYPE), (((1,), (1,)), ((), ())),
                           preferred_element_type=F32)


def _mm_tn(a, b):
    return lax.dot_general(a.astype(MM_DTYPE), b.astype(MM_DTYPE), (((0,), (0,)), ((), ())),
                           preferred_element_type=F32)


def _split(a):
    hi = a.astype(BF16)
    return hi, (a - hi.astype(F32)).astype(BF16)


def _hp(a2, b2):
    d = lambda p, q: jnp.dot(p, q, preferred_element_type=F32)
    return d(a2[0], b2[0]) + (d(a2[0], b2[1]) + d(a2[1], b2[0]))


def _silu(x):
    return x * jax.nn.sigmoid(x)


@jax.custom_vjp
def _stack(parts):
    return jnp.stack(parts)


_stack.defvjp(lambda parts: (jnp.stack(parts), None), lambda _, g: (tuple(g[i] for i in range(g.shape[0])),))


def _softplus(x):
    return jnp.maximum(x, 0.0) + jnp.log1p(jnp.exp(-jnp.abs(x)))


_ANY = pl.BlockSpec(memory_space=pl.ANY)


def _me():
    return lax.axis_index("x"), lax.axis_index("y"), lax.axis_index("c")


def _flat_id(pos):
    return 4 * pos[0] + 2 * pos[1] + pos[2]


def _remote(src, dst, send_sem, recv_sem, to):
    return pltpu.make_async_remote_copy(src_ref=src, dst_ref=dst, send_sem=send_sem, recv_sem=recv_sem,
                                        device_id=to, device_id_type=pl.DeviceIdType.MESH)


class _Direct:
    def __init__(self, items, bufs):
        self.items, self.bufs = list(items), list(bufs)
        self.n_src, self.n_buf = len(self.items), len(self.bufs)
        self.old = [j for j, b in enumerate(self.bufs) if not isinstance(b, jax.ShapeDtypeStruct)]
        self.args = [it[0] for it in self.items] + [self.bufs[j] for j in self.old]
        self.out_shape = [jax.ShapeDtypeStruct(b.shape, b.dtype) for b in self.bufs]
        self.scratch = [pltpu.SemaphoreType.DMA((self.n_src, N_DEV - 1)),
                        pltpu.SemaphoreType.DMA((self.n_src, N_DEV - 1)), pltpu.SemaphoreType.DMA((self.n_src,))]

    def aliases(self, in_base, out_base):
        return {in_base + self.n_src + pos: out_base + j for pos, j in enumerate(self.old)}

    def copies(self, in_refs, out_refs, sems):
        send_sems, recv_sems, local_sems = sems
        x, y, c = _me()
        me = _flat_id((x, y, c))
        peers = [(x ^ ((rel >> 2) & 1), y ^ ((rel >> 1) & 1), c ^ (rel & 1)) for rel in range(1, N_DEV)]
        local, sends, recvs = [], [], []
        for a, (_, per_dest, j, prefix, *rest) in enumerate(self.items):
            src = lambda d: in_refs[a].at[d] if per_dest else in_refs[a]
            dst = lambda s: out_refs[j].at[tuple(prefix) + (s,) + tuple(rest[0] if rest else ())]
            local.append(pltpu.make_async_copy(src(me), dst(me), local_sems.at[a]))
            for k, peer in enumerate(peers):
                pid = _flat_id(peer)
                sends.append(_remote(src(pid), dst(me), send_sems.at[a, k], recv_sems.at[a, k], peer))
                recvs.append(_remote(src(pid), dst(pid), send_sems.at[a, k], recv_sems.at[a, k], peer))
        return local, sends, recvs

    def start(self, in_refs, out_refs, sems):
        local, sends, _ = self.copies(in_refs, out_refs, sems)
        for cp in local + sends:
            cp.start()

    def wait(self, in_refs, out_refs, sems):
        local, sends, recvs = self.copies(in_refs, out_refs, sems)
        for cp in recvs:
            cp.wait_recv()
        for cp in sends:
            cp.wait_send()
        for cp in local:
            cp.wait()


def _pcall(core, *, name, grid, in_specs, out_specs, out_shape, args, sem, scratch_shapes=(), aliases=None,
           rider=None):
    n_in, n_out, n_scr = len(in_specs), len(out_specs), len(scratch_shapes)
    n_rin, n_rout = (len(rider.args), rider.n_buf) if rider else (0, 0)

    def body(*refs):
        ins, outs = refs[:n_in], refs[n_in + n_rin:n_in + n_rin + n_out]
        scr = refs[n_in + n_rin + n_out + n_rout:n_in + n_rin + n_out + n_rout + n_scr]
        if rider:
            r_refs = (refs[n_in:n_in + rider.n_src], refs[n_in + n_rin + n_out:n_in + n_rin + n_out + n_rout],
                      refs[n_in + n_rin + n_out + n_rout + n_scr:])
            ids = [pl.program_id(d) for d in range(len(grid))]
            first = functools.reduce(lambda p, q: p & q, [i == 0 for i in ids])
            last = functools.reduce(lambda p, q: p & q, [i == g - 1 for i, g in zip(ids, grid)])
            pl.when(first)(lambda: rider.start(*r_refs))
        core(ins, outs, scr)
        if rider:
            pl.when(last)(lambda: rider.wait(*r_refs))

    aliases = dict(aliases or {})
    if rider:
        sem = ("arbitrary",) * len(grid)
        aliases.update(rider.aliases(n_in, n_out))
    return pl.pallas_call(
        body, name=name, grid=grid, in_specs=list(in_specs) + [_ANY] * n_rin,
        out_specs=list(out_specs) + [_ANY] * n_rout,
        out_shape=list(out_shape) + (rider.out_shape if rider else []),
        scratch_shapes=list(scratch_shapes) + (rider.scratch if rider else []),
        input_output_aliases=aliases, compiler_params=_cparams(sem),
    )(*args, *(rider.args if rider else []))


def _exchange(direct, *, name):
    n_in = len(direct.args)

    def body(*refs):
        r_refs = refs[:direct.n_src], refs[n_in:n_in + direct.n_buf], refs[n_in + direct.n_buf:]
        direct.start(*r_refs)
        direct.wait(*r_refs)

    return pl.pallas_call(
        body, name=name, in_specs=[_ANY] * n_in, out_specs=[_ANY] * direct.n_buf, out_shape=direct.out_shape,
        input_output_aliases=direct.aliases(0, 0), scratch_shapes=direct.scratch,
    )(*direct.args)


def _matmul(a, b, *, form, tm, tn, tk, name, add=None, add_scale=1.0, extra=None, rider=None, a_cols=None):
    if form == "nn":
        (m, kk), n = a.shape, b.shape[1]
        a_spec = pl.BlockSpec((tm, tk), lambda i, j, k: (i, k))
        b_spec = pl.BlockSpec((tk, tn), lambda i, j, k: (k, j))
        dn = (((1,), (0,)), ((), ()))
    elif form == "nt":
        (m, kk), n = a.shape, b.shape[0]
        a_spec = pl.BlockSpec((tm, tk), lambda i, j, k: (i, k))
        b_spec = pl.BlockSpec((tn, tk), lambda i, j, k: (j, k))
        dn = (((1,), (1,)), ((), ()))
    else:
        kk, n = a.shape[0], b.shape[1]
        m0, m = a_cols or (0, a.shape[1])
        assert m0 % tm == 0
        a_spec = pl.BlockSpec((tk, tm), lambda i, j, k: (k, i + m0 // tm))
        b_spec = pl.BlockSpec((tk, tn), lambda i, j, k: (k, j))
        dn = (((0,), (0,)), ((), ()))
    assert m % tm == 0 and n % tn == 0 and kk % tk == 0, (name, m, n, kk)
    has_add, has_extra = add is not None, extra is not None

    def core(ins, outs, _):
        a_ref, b_ref = ins[:2]
        o_ref = outs[0]
        rest = ins[2:]
        k = pl.program_id(2)
        p = lax.dot_general(a_ref[...].astype(MM_DTYPE), b_ref[...].astype(MM_DTYPE), dn,
                            preferred_element_type=F32)

        @pl.when(k == 0)
        def _():
            first = p
            pos = 0
            if has_extra:
                first = first + _mm_nt(rest[0][...], rest[1][...])
                pos = 2
            if has_add:
                first = first + add_scale * rest[pos][...]
            o_ref[...] = first

        @pl.when(k > 0)
        def _():
            o_ref[...] += p

    in_specs = [a_spec, b_spec]
    args = [a, b]
    if has_extra:
        a2, b2, idx = extra
        in_specs += [pl.BlockSpec((tm, LANE), lambda i, j, k: (i, 0)),
                     pl.BlockSpec((tn, LANE), lambda i, j, k: (j, idx))]
        args += [a2, b2]
    if has_add:
        in_specs.append(pl.BlockSpec((tm, tn), lambda i, j, k: (i, j)))
        args.append(add)
    res = _pcall(core, name=name, grid=(m // tm, n // tn, kk // tk), in_specs=in_specs,
                 out_specs=[pl.BlockSpec((tm, tn), lambda i, j, k: (i, j))],
                 out_shape=[jax.ShapeDtypeStruct((m, n), F32)], args=args,
                 sem=("parallel", "parallel", "arbitrary"), rider=rider)
    return res if rider else res[0]


ZERO_TAIL = 8


def _with_tail(x):
    return jnp.concatenate([x, jnp.zeros((ZERO_TAIL,) + x.shape[1:], x.dtype)], axis=0)


def _shift_down(x, k):
    return pltpu.roll(x, k, 0)


def _shift_up(x, k):
    return pltpu.roll(x, x.shape[0] - k, 0)


def _conv_slab(x, w):
    return w[3:4] * x + w[2:3] * _shift_down(x, 1) + w[1:2] * _shift_down(x, 2) + w[0:1] * _shift_down(x, 3)


def _prep_fwd(h, conv_w, *, name):
    t_len = h.shape[0]

    def body(x_ref, w_ref, o_ref):
        s = pl.program_id(0)
        y = _silu(_conv_slab(_with_tail(x_ref[...]), w_ref[...])[:t_len])
        rs = lax.rsqrt(jnp.sum(y * y, axis=-1, keepdims=True) + L2_EPS)
        scale = jnp.where(s < A_HEADS, A_HEAD_DIM ** -0.5, 1.0)
        o_ref[...] = jnp.where(s < 2 * A_HEADS, y * rs * scale, y)

    return pl.pallas_call(
        body, name=name, grid=(12,),
        in_specs=[pl.BlockSpec((t_len, LANE), lambda s: (0, L_QKV // LANE + s)),
                  pl.BlockSpec((8, LANE), lambda s: (0, s))],
        out_specs=pl.BlockSpec((t_len, LANE), lambda s: (0, s)),
        out_shape=jax.ShapeDtypeStruct((t_len, 3 * A_WIDTH), F32),
        compiler_params=_cparams(("parallel",)),
    )(h, conv_w)


def _prep_bwd(h, conv_w, d_out, dh, *, name):
    t_len = h.shape[0]

    def body(x_ref, w_ref, g_ref, dh_in, dx_ref, dw_ref):
        del dh_in
        s = pl.program_id(0)
        x = _with_tail(x_ref[...])
        g = _with_tail(g_ref[0])
        w = w_ref[...]
        c = _conv_slab(x, w)
        sg = jax.nn.sigmoid(c)
        y = c * sg
        rs = lax.rsqrt(jnp.sum(y * y, axis=-1, keepdims=True) + L2_EPS)
        scale = jnp.where(s < A_HEADS, A_HEAD_DIM ** -0.5, 1.0)
        dy_n = scale * (rs * g - y * (rs * rs * rs) * jnp.sum(g * y, axis=-1, keepdims=True))
        dy = jnp.where(s < 2 * A_HEADS, dy_n, g)
        dc = dy * (sg * (1.0 + c * (1.0 - sg)))
        dx = w[3:4] * dc + w[2:3] * _shift_up(dc, 1) + w[1:2] * _shift_up(dc, 2) + w[0:1] * _shift_up(dc, 3)
        dx_ref[...] = dx[:t_len]
        dws = [jnp.sum(dc * _shift_down(x, 3 - j), axis=0, keepdims=True) if j < 3
               else jnp.sum(dc * x, axis=0, keepdims=True) for j in range(CONV_K)]
        dw_ref[...] = jnp.concatenate(dws + [jnp.zeros((8 - CONV_K, LANE), F32)], axis=0)

    slab = pl.BlockSpec((t_len, LANE), lambda s: (0, L_QKV // LANE + s))
    return pl.pallas_call(
        body, name=name, grid=(12,),
        in_specs=[slab, pl.BlockSpec((8, LANE), lambda s: (0, s)),
                  pl.BlockSpec((1, t_len, LANE), lambda s: (s // A_HEADS, 0, s % A_HEADS)), _ANY],
        out_specs=[slab, pl.BlockSpec((8, LANE), lambda s: (0, s))],
        out_shape=[jax.ShapeDtypeStruct((t_len, L_MAIN), F32), jax.ShapeDtypeStruct((8, 3 * A_WIDTH), F32)],
        input_output_aliases={3: 0},
        compiler_params=_cparams(("parallel",)),
    )(h, conv_w, d_out, dh)


N_LEVELS = 5
MF_TRIL, MF_STRIL, MF_DIAG8, MF_LOW16, MF_EYE = 0, 1, 2, 3, 3 + N_LEVELS
MB_CUM, MB_CUM_T, MB_TOT = 0, 1, 2


def _gdn_masks():
    r = lax.broadcasted_iota(jnp.int32, (SUPER, SUPER), 0)
    c = lax.broadcasted_iota(jnp.int32, (SUPER, SUPER), 1)
    same = lambda shift: (r >> shift) == (c >> shift)
    ninf = lambda m: jnp.where(m, 0.0, -jnp.inf).astype(F32)
    one = lambda m: m.astype(F32)
    mf = jnp.stack([ninf(r >= c), ninf(r > c), one(same(3))]
                   + [one(same(4 + lv) & jnp.logical_not(same(3 + lv))) for lv in range(N_LEVELS)] + [one(r == c)])
    mb = jnp.stack([one(r >= c), one(r <= c), jnp.ones((SUPER, SUPER), F32)]).astype(BF16)
    return mf, mb


def _tri_inv_impl(a, mf):
    d = lambda p, q: jnp.dot(p.astype(BF16), q.astype(BF16), preferred_element_type=F32)
    dd = lambda p, q: jnp.dot(p, q, preferred_element_type=F32)
    eye = mf[MF_EYE]
    a0 = a * mf[MF_DIAG8]
    a2 = d(a0, a0)
    a4 = d(a2, a2)
    t = d(d(eye - a0, eye + a2), eye + a4)
    for level in range(N_LEVELS):
        t = t - d(d(t, a * mf[MF_LOW16 + level]), t)
    a_hi, a_lo = _split(a)
    for _ in range(NEWTON_STEPS):
        t_hi, t_lo = _split(t)
        resid = (eye - t) - (dd(a_hi, t_hi) + (dd(a_hi, t_lo) + dd(a_lo, t_hi)))
        r_hi, r_lo = _split(resid)
        t = t + (dd(t_hi, r_hi) + dd(t_hi, r_lo))
    return t


@jax.custom_vjp
def _wy_apply(a, rhs, t):
    return _mm(t, rhs)


def _wy_apply_fwd(a, rhs, t):
    x = _mm(t, rhs)
    return x, (t, x)


def _wy_apply_bwd(res, dx):
    t, x = res
    d_rhs = _mm_tn(t, dx)
    return -_mm_nt(d_rhs, x), d_rhs, jnp.zeros_like(t)


_wy_apply.defvjp(_wy_apply_fwd, _wy_apply_bwd)


@functools.partial(jax.custom_vjp, nondiff_argnums=(1,))
def _lane_roll(x, shift):
    return pltpu.roll(x, shift % LANE, 1)


_lane_roll.defvjp(lambda x, shift: (_lane_roll(x, shift), None), lambda shift, _, g: (_lane_roll(g, -shift),))


def _mask_times_lanes(x, mask):
    lane = lax.broadcasted_iota(jnp.int32, (1, LANE), 1)
    x = jnp.where(lane < A_HEADS, x, 0.0)
    x1 = x.astype(BF16).astype(F32)
    x2 = (x - x1).astype(BF16).astype(F32)
    x3 = (x - x1 - x2).astype(BF16).astype(F32)
    pieces = x1 + pltpu.roll(x2, A_HEADS, 1) + pltpu.roll(x3, 2 * A_HEADS, 1)
    res = jnp.dot(mask, pieces.astype(BF16), preferred_element_type=F32)
    return res + pltpu.roll(res, LANE - A_HEADS, 1) + pltpu.roll(res, LANE - 2 * A_HEADS, 1)


@jax.custom_vjp
def _chunk_sums(g, mb):
    return _mask_times_lanes(g, mb[MB_CUM]), _mask_times_lanes(g, mb[MB_TOT])


def _chunk_sums_fwd(g, mb):
    return _chunk_sums(g, mb), mb


def _chunk_sums_bwd(mb, d):
    lane = lax.broadcasted_iota(jnp.int32, (1, LANE), 1)
    dg = _mask_times_lanes(d[0], mb[MB_CUM_T]) + _mask_times_lanes(d[1], mb[MB_TOT])
    return jnp.where(lane < A_HEADS, dg, 0.0), jnp.zeros_like(mb)


_chunk_sums.defvjp(_chunk_sums_fwd, _chunk_sums_bwd)


def _gdn_gates(ba, alog, dtb, mb):
    beta = jax.nn.sigmoid(ba)
    g = -jnp.exp(alog) * _softplus(_lane_roll(ba, -A_HEADS) + dtb)
    gc, gl = _chunk_sums(g, mb)
    return beta, gc, gl, gc.T


def _gdn_block(s, q, k, v, z, gates, nw, h, t_known, mf):
    n = q.shape[0]
    beta_all, gc_all, gl_all, gct_all = gates
    lane = lax.broadcasted_iota(jnp.int32, (1, LANE), 1)
    sub = lax.broadcasted_iota(jnp.int32, (LANE, 1), 0)
    col = lambda x: jnp.sum(jnp.where(lane == h, x, 0.0), axis=1, keepdims=True)
    wide = lambda c: jnp.broadcast_to(c, (n, LANE))
    gc, gl = col(gc_all), col(gl_all)
    gc_row = jnp.sum(jnp.where(sub == h, gct_all, 0.0), axis=0, keepdims=True)
    beta_w, eg_w = wide(col(beta_all)), wide(jnp.exp(gc))
    diff = gc - gc_row
    decay = jnp.exp(diff + mf[MF_TRIL])
    kb = k * beta_w
    a_mat = _mm_nt(kb, k) * jnp.exp(diff + mf[MF_STRIL])
    rhs = jnp.concatenate([v * beta_w, kb * eg_w], axis=1)
    if t_known is None:
        t_mat = _tri_inv_impl(a_mat, mf)
        uw = _mm(t_mat, rhs)
    else:
        t_mat = t_known
        uw = _wy_apply(a_mat, rhs, t_known)
    u, w = uw[:, :LANE], uw[:, LANE:]
    qk = _mm_nt(q, k) * decay
    q_dec = q * eg_w
    k_dec = k * wide(jnp.exp(gl - gc))
    v_new = u - _mm(w, s)
    o = _mm(q_dec, s) + _mm(qk, v_new)
    s = s * jnp.exp(gl[0:1]) + _mm_tn(k_dec, v_new)
    o = o * lax.rsqrt(jnp.mean(o * o, axis=-1, keepdims=True) + RMS_EPS) * nw
    return o * _silu(z), s, t_mat


def _gdn_fwd(qkv, h, alog, dtb, nw, ycat, *, name, rider=None):
    t_len = qkv.shape[0]
    nsc = t_len // SUPER

    def core(ins, outs, scr):
        q_ref, k_ref, v_ref, z_ref, ba_ref, al_ref, dt_ref, nw_ref, mf_ref, mb_ref, _ = ins
        y_ref, sin_ref, t_ref = outs
        s_scr, = scr
        sc, pg = pl.program_id(0), pl.program_id(1)
        heads = [pg * G_HEADS + hh for hh in range(G_HEADS)]

        @pl.when(sc == 0)
        def _():
            for hd in heads:
                s_scr[hd] = jnp.zeros((A_HEAD_DIM, A_HEAD_DIM), F32)

        per_head = lambda ref: jnp.stack([ref[:, hh * LANE:(hh + 1) * LANE] for hh in range(G_HEADS)])
        states = jnp.stack([s_scr[hd] for hd in heads])
        gates = _gdn_gates(ba_ref[...], al_ref[...], dt_ref[...], mb_ref[...])
        fn = jax.vmap(_gdn_block, in_axes=(0, 0, 0, 0, 0, None, None, 0, None, None))
        y, s_new, t_mat = fn(states, per_head(q_ref), per_head(k_ref), per_head(v_ref), per_head(z_ref),
                             gates, nw_ref[...], jnp.stack(heads), None, mf_ref[...])
        sin_ref[0] = states
        t_ref[0] = t_mat
        for hh, hd in enumerate(heads):
            y_ref[:, hh * LANE:(hh + 1) * LANE] = y[hh]
            s_scr[hd] = s_new[hh]

    gw = G_HEADS * LANE
    blk = lambda off: pl.BlockSpec((SUPER, gw), lambda sc, pg: (sc, off // G_HEADS + pg))
    row = pl.BlockSpec((1, LANE), lambda sc, pg: (0, 0))
    mf, mb = _gdn_masks()
    whole = lambda a: pl.BlockSpec(a.shape, lambda sc, pg: (0, 0, 0))
    return _pcall(
        core, name=name, grid=(nsc, A_HEADS // G_HEADS),
        in_specs=[blk(0), blk(4), blk(8), blk(L_ZA // LANE),
                  pl.BlockSpec((SUPER, LANE), lambda sc, pg: (sc, L_BA // LANE)), row, row, row,
                  whole(mf), whole(mb), _ANY],
        out_specs=[blk(0),
                   pl.BlockSpec((1, G_HEADS, A_HEAD_DIM, A_HEAD_DIM), lambda sc, pg: (sc, pg, 0, 0)),
                   pl.BlockSpec((1, G_HEADS, SUPER, SUPER), lambda sc, pg: (sc, pg, 0, 0))],
        out_shape=[jax.ShapeDtypeStruct((t_len, D_MODEL), F32),
                   jax.ShapeDtypeStruct((nsc, A_HEADS, A_HEAD_DIM, A_HEAD_DIM), F32),
                   jax.ShapeDtypeStruct((nsc, A_HEADS, SUPER, SUPER), F32)],
        scratch_shapes=[pltpu.VMEM((A_HEADS, A_HEAD_DIM, A_HEAD_DIM), F32)],
        aliases={10: 0}, sem=("arbitrary", "arbitrary"), rider=rider,
        args=(qkv, qkv, qkv, h, h, alog, dtb, nw, mf, mb, ycat))


def _gdn_bwd(qkv, h, alog, dtb, nw, s_in, t_in, dycat, dh, *, name, rider=None):
    t_len = qkv.shape[0]
    nsc = t_len // SUPER

    def core(ins, outs, scr):
        (q_ref, k_ref, v_ref, z_ref, ba_ref, al_ref, dt_ref, nw_ref, sin_ref, t_ref, dy_ref, mf_ref, mb_ref,
         _) = ins
        dz_ref, dqkv_ref, dba_ref, dal_ref, ddt_ref, dnw_ref = outs
        ds_scr, = scr
        i, pg = pl.program_id(0), pl.program_id(1)

        @pl.when((i == 0) & (pg == 0))
        def _():
            dal_ref[...] = jnp.zeros_like(dal_ref)
            ddt_ref[...] = jnp.zeros_like(ddt_ref)
            dnw_ref[...] = jnp.zeros_like(dnw_ref)

        @pl.when(pg == 0)
        def _():
            dba_ref[...] = jnp.zeros_like(dba_ref)

        heads = [pg * G_HEADS + hh for hh in range(G_HEADS)]

        @pl.when(i == 0)
        def _():
            for hd in heads:
                ds_scr[hd] = jnp.zeros((A_HEAD_DIM, A_HEAD_DIM), F32)

        per_head = lambda ref: jnp.stack([ref[:, hh * LANE:(hh + 1) * LANE] for hh in range(G_HEADS)])
        d_states = jnp.stack([ds_scr[hd] for hd in heads])
        head_ids = jnp.stack(heads)
        t_known, mf, mb = t_ref[0], mf_ref[...], mb_ref[...]

        def fn(s, q, k, v, z, ba, alog, dtb, nw):
            gates = _gdn_gates(ba, alog, dtb, mb)
            one = lambda s, q, k, v, z, t, h: _gdn_block(s, q, k, v, z, gates, nw, h, t, mf)[:2]
            return jax.vmap(one)(s, q, k, v, z, t_known, head_ids)

        _, vjp = jax.vjp(fn, sin_ref[0], per_head(q_ref), per_head(k_ref), per_head(v_ref), per_head(z_ref),
                         ba_ref[...], al_ref[...], dt_ref[...], nw_ref[...])
        ds, dq, dk, dv, dz, dba, dal, ddt, dnw = vjp((per_head(dy_ref), d_states))
        for hh, hd in enumerate(heads):
            cols = slice(hh * LANE, (hh + 1) * LANE)
            ds_scr[hd] = ds[hh]
            dqkv_ref[0, :, cols] = dq[hh]
            dqkv_ref[1, :, cols] = dk[hh]
            dqkv_ref[2, :, cols] = dv[hh]
            dz_ref[:, cols] = dz[hh]
        dba_ref[...] += dba
        dal_ref[...] += dal
        ddt_ref[...] += ddt
        dnw_ref[...] += dnw

    rev = lambda i: nsc - 1 - i
    gw = G_HEADS * LANE
    blk = lambda off: pl.BlockSpec((SUPER, gw), lambda i, pg: (rev(i), off // G_HEADS + pg))
    row = pl.BlockSpec((1, LANE), lambda i, pg: (0, 0))
    ba_blk = lambda off: pl.BlockSpec((SUPER, LANE), lambda i, pg: (rev(i), off))
    mf, mb = _gdn_masks()
    whole = lambda a: pl.BlockSpec(a.shape, lambda i, pg: (0, 0, 0))
    return _pcall(
        core, name=name, grid=(nsc, A_HEADS // G_HEADS),
        in_specs=[blk(0), blk(4), blk(8), blk(L_ZA // LANE), ba_blk(L_BA // LANE), row, row, row,
                  pl.BlockSpec((1, G_HEADS, A_HEAD_DIM, A_HEAD_DIM), lambda i, pg: (rev(i), pg, 0, 0)),
                  pl.BlockSpec((1, G_HEADS, SUPER, SUPER), lambda i, pg: (rev(i), pg, 0, 0)),
                  blk(0), whole(mf), whole(mb), _ANY],
        out_specs=[blk(L_ZA // LANE),
                   pl.BlockSpec((3, SUPER, gw), lambda i, pg: (0, rev(i), pg)),
                   ba_blk(0), row, row, row],
        out_shape=[jax.ShapeDtypeStruct((t_len, L_MAIN), F32), jax.ShapeDtypeStruct((3, t_len, A_WIDTH), F32),
                   jax.ShapeDtypeStruct((t_len, LANE), F32)] + [jax.ShapeDtypeStruct((1, LANE), F32)] * 3,
        scratch_shapes=[pltpu.VMEM((A_HEADS, A_HEAD_DIM, A_HEAD_DIM), F32)],
        aliases={13: 0}, sem=("arbitrary", "arbitrary"), rider=rider,
        args=(qkv, qkv, qkv, h, h, alog, dtb, nw, s_in, t_in, dycat, mf, mb, dh))


def _swa_block(q, kp, kc, vp, vc, z, sinks, first):
    rows = B_GROUP * BLOCK
    ri = lax.broadcasted_iota(jnp.int32, (rows, 2 * BLOCK), 0)
    si = lax.broadcasted_iota(jnp.int32, (rows, 2 * BLOCK), 1)
    dist = (ri & (BLOCK - 1)) + BLOCK - si
    bias = jnp.where((dist >= 0) & (dist < WINDOW) & ((si >= BLOCK) | jnp.logical_not(first)), 0.0, -jnp.inf)
    dist_f = dist.astype(F32)
    head_of_row = lax.broadcasted_iota(jnp.int32, (rows, 1), 0) >> 7

    def group(j):
        cs = slice(j * B_HEAD_DIM, (j + 1) * B_HEAD_DIM)
        heads = range(j * B_GROUP, (j + 1) * B_GROUP)
        qs = jnp.concatenate([q[:, hq * B_HEAD_DIM:(hq + 1) * B_HEAD_DIM] for hq in heads], axis=0) * (
            B_HEAD_DIM ** -0.5)
        kk = jnp.concatenate([kp[:, cs], kc[:, cs]], axis=0)
        vv = jnp.concatenate([vp[:, cs], vc[:, cs]], axis=0)
        sink = jnp.concatenate([jnp.broadcast_to(sinks[:, hq:hq + 1], (BLOCK, 1)) for hq in heads], axis=0)
        slope = sum(jnp.where(head_of_row == gi, 2.0 ** (-8.0 * (hq + 1) / B_Q_HEADS), 0.0)
                    for gi, hq in enumerate(heads))
        return qs, kk, vv, sink, slope

    def attend(qs, kk, vv, sink, slope):
        sc = _mm_nt(qs, kk) - slope * dist_f + bias
        m = lax.stop_gradient(jnp.maximum(jnp.max(sc, axis=-1, keepdims=True), sink))
        p = jnp.exp(sc - m)
        inv = 1.0 / (jnp.sum(p, axis=-1, keepdims=True) + jnp.exp(sink - m))
        return _mm(p * inv, vv)

    o = jax.vmap(attend)(*[_stack(t) for t in zip(*[group(j) for j in range(B_KV_HEADS)])])
    outs = [o[j, gi * BLOCK:(gi + 1) * BLOCK] for j in range(B_KV_HEADS) for gi in range(B_GROUP)]
    return jnp.concatenate(outs, axis=1) * _silu(z)


def _swa_specs(idx):
    wide = lambda off: pl.BlockSpec((BLOCK, B_WIDTH), lambda n: (idx(n), off))
    cur = lambda off: pl.BlockSpec((BLOCK, LANE), lambda n: (idx(n), off))
    prev = lambda off: pl.BlockSpec((BLOCK, LANE), lambda n: (jnp.maximum(idx(n) - 1, 0), off))
    return [wide(L_QB // B_WIDTH), prev(L_KB // LANE), cur(L_KB // LANE), prev(L_VB // LANE), cur(L_VB // LANE),
            wide(L_ZB // B_WIDTH), pl.BlockSpec((1, LANE), lambda n: (0, 0))]


def _swa_fwd(h, sinks, *, name, rider=None):
    t_len = h.shape[0]
    nb = t_len // BLOCK

    def core(ins, outs, _):
        q_ref, kp_ref, kc_ref, vp_ref, vc_ref, z_ref, s_ref = ins
        outs[0][...] = _swa_block(q_ref[...], kp_ref[...], kc_ref[...], vp_ref[...], vc_ref[...], z_ref[...],
                                  s_ref[...], pl.program_id(0) == 0)

    res = _pcall(core, name=name, grid=(nb,), in_specs=_swa_specs(lambda n: n),
                 out_specs=[pl.BlockSpec((BLOCK, B_WIDTH), lambda n: (n, 1))],
                 out_shape=[jax.ShapeDtypeStruct((t_len, D_MODEL), F32)], sem=("parallel",), rider=rider,
                 args=(h, h, h, h, h, h, sinks))
    return res if rider else res[0]


def _swa_bwd(h, sinks, dycat, *, name, rider=None):
    t_len = h.shape[0]
    nb = t_len // BLOCK

    def core(ins, outs, scr):
        q_ref, kp_ref, kc_ref, vp_ref, vc_ref, z_ref, s_ref, dy_ref = ins
        dh_ref, dsk_ref = outs
        ck_scr, cv_scr = scr
        i = pl.program_id(0)
        n = nb - 1 - i

        @pl.when(i == 0)
        def _():
            ck_scr[...] = jnp.zeros_like(ck_scr)
            cv_scr[...] = jnp.zeros_like(cv_scr)
            dsk_ref[...] = jnp.zeros_like(dsk_ref)

        fn = functools.partial(_swa_block, first=(n == 0))
        _, vjp = jax.vjp(fn, q_ref[...], kp_ref[...], kc_ref[...], vp_ref[...], vc_ref[...], z_ref[...], s_ref[...])
        dq, dkp, dkc, dvp, dvc, dz, dsk = vjp(dy_ref[...])
        dh_ref[:, L_QB:L_QB + B_WIDTH] = dq
        dh_ref[:, L_ZB:L_ZB + B_WIDTH] = dz
        dh_ref[:, L_ZA:L_ZA + A_WIDTH] = jnp.zeros((BLOCK, A_WIDTH), F32)
        dh_ref[:, L_KB:L_KB + LANE] = dkc + ck_scr[...]
        dh_ref[:, L_VB:L_VB + LANE] = dvc + cv_scr[...]
        ck_scr[...] = dkp
        cv_scr[...] = dvp
        dsk_ref[...] += dsk

    rev = lambda i: nb - 1 - i
    return _pcall(
        core, name=name, grid=(nb,),
        in_specs=_swa_specs(rev) + [pl.BlockSpec((BLOCK, B_WIDTH), lambda i: (rev(i), 1))],
        out_specs=[pl.BlockSpec((BLOCK, L_SWA), lambda i: (rev(i), 0)), pl.BlockSpec((1, LANE), lambda i: (0, 0))],
        out_shape=[jax.ShapeDtypeStruct((t_len, L_MAIN), F32), jax.ShapeDtypeStruct((1, LANE), F32)],
        scratch_shapes=[pltpu.VMEM((BLOCK, LANE), F32), pltpu.VMEM((BLOCK, LANE), F32)],
        sem=("arbitrary",), rider=rider, args=(h, h, h, h, h, h, sinks, dycat))


def _out_ln_fwd(ycat, w_out, x, ln_g, ln_b, *, name, tm=256):
    t_len = x.shape[0]

    def body(y_ref, w_ref, x_ref, g_ref, b_ref, r_ref, o_ref):
        r = DEEPNORM_ALPHA * x_ref[...] + _mm(y_ref[...], w_ref[...])
        r_ref[...] = r
        mu = jnp.mean(r, axis=-1, keepdims=True)
        d = r - mu
        var = jnp.mean(d * d, axis=-1, keepdims=True)
        o_ref[...] = d * lax.rsqrt(var + LN_EPS) * g_ref[...] + b_ref[...]

    tile = pl.BlockSpec((tm, D_MODEL), lambda i: (i, 0))
    vec = pl.BlockSpec((1, D_MODEL), lambda i: (0, 0))
    return pl.pallas_call(
        body, name=name, grid=(t_len // tm,),
        in_specs=[tile, pl.BlockSpec((D_MODEL, D_MODEL), lambda i: (0, 0)), tile, vec, vec],
        out_specs=[tile, tile],
        out_shape=[jax.ShapeDtypeStruct((t_len, D_MODEL), F32)] * 2,
        compiler_params=_cparams(("parallel",)),
    )(ycat, w_out, x, ln_g, ln_b)


def _ln_bwd(dxn, r, ln_g, *, name, tm=256):
    t_len = r.shape[0]

    def body(dx_ref, r_ref, g_ref, dr_ref, dg_ref, db_ref):
        @pl.when(pl.program_id(0) == 0)
        def _():
            dg_ref[...] = jnp.zeros_like(dg_ref)
            db_ref[...] = jnp.zeros_like(db_ref)

        rr = r_ref[...]
        dx = dx_ref[...]
        mu = jnp.mean(rr, axis=-1, keepdims=True)
        d = rr - mu
        rstd = lax.rsqrt(jnp.mean(d * d, axis=-1, keepdims=True) + LN_EPS)
        xh = d * rstd
        dxh = dx * g_ref[...]
        dr_ref[...] = rstd * (dxh - jnp.mean(dxh, axis=-1, keepdims=True)
                              - xh * jnp.mean(dxh * xh, axis=-1, keepdims=True))
        dg_ref[...] += jnp.sum(dx * xh, axis=0, keepdims=True)
        db_ref[...] += jnp.sum(dx, axis=0, keepdims=True)

    tile = pl.BlockSpec((tm, D_MODEL), lambda i: (i, 0))
    vec = pl.BlockSpec((1, D_MODEL), lambda i: (0, 0))
    return pl.pallas_call(
        body, name=name, grid=(t_len // tm,),
        in_specs=[tile, tile, vec], out_specs=[tile, vec, vec],
        out_shape=[jax.ShapeDtypeStruct((t_len, D_MODEL), F32), jax.ShapeDtypeStruct((1, D_MODEL), F32),
                   jax.ShapeDtypeStruct((1, D_MODEL), F32)],
        compiler_params=_cparams(("arbitrary",)),
    )(dxn, r, ln_g)


def _loss_head(y, target, *, name, tm=256):
    t_len = y.shape[0]

    def body(y_ref, t_ref, d_ref, l_ref):
        @pl.when(pl.program_id(0) == 0)
        def _():
            l_ref[...] = jnp.zeros_like(l_ref)

        e = y_ref[...] - t_ref[...]
        d_ref[...] = e * (1.0 / D_MODEL)
        l_ref[...] += jnp.sum(e * e, axis=0, keepdims=True)

    tile = pl.BlockSpec((tm, D_MODEL), lambda i: (i, 0))
    vec = pl.BlockSpec((1, D_MODEL), lambda i: (0, 0))
    return pl.pallas_call(
        body, name=name, grid=(t_len // tm,), in_specs=[tile, tile], out_specs=[tile, vec],
        out_shape=[jax.ShapeDtypeStruct((t_len, D_MODEL), F32), jax.ShapeDtypeStruct((1, D_MODEL), F32)],
        compiler_params=_cparams(("arbitrary",)),
    )(y, target)


def _pad_row(v):
    return jnp.zeros((1, LANE), F32).at[0, :v.shape[0]].set(v)


def _to_layout(w_full):
    s = lambda a, b: w_full[..., a:b]
    pad = jnp.zeros(w_full.shape[:-1] + (LANE - 2 * A_HEADS,), w_full.dtype)
    return jnp.concatenate([s(2056, 2568), s(2824, 3336), s(1536, 2048), s(2568, 2696), s(2696, 2824), s(0, 1536),
                            s(2048, 2056), pad], axis=-1)


def _from_layout(g_main, g_ba):
    s = lambda a, b: g_main[..., a:b]
    return jnp.concatenate([s(L_QKV, L_QKV + 1536), s(L_ZA, L_ZA + 512), g_ba[..., :2 * A_HEADS],
                            s(L_QB, L_QB + 512), s(L_KB, L_KB + 128), s(L_VB, L_VB + 128), s(L_ZB, L_ZB + 512)],
                           axis=-1)


_REGIONS = ((0, 1536, L_QKV), (1536, 2048, L_ZA), (2048, 2056, L_BA), (2056, 2568, L_QB), (2568, 2696, L_KB),
            (2696, 2824, L_VB), (2824, 3336, L_ZB))


def _shard_pieces(regions):
    for a, b, off in regions:
        for d in range(N_DEV):
            lo, hi = max(a, d * SHARD_COLS), min(b, (d + 1) * SHARD_COLS)
            if lo < hi:
                yield d, lo - d * SHARD_COLS, hi - d * SHARD_COLS, off + lo - a


def _as_list(r):
    return list(r) if isinstance(r, (list, tuple)) else [r]


def _gathered(shard):
    return jax.ShapeDtypeStruct((N_DEV,) + shard.shape, shard.dtype)


def _full_w_in(g_in, name):
    by_offset = sorted(_shard_pieces(_REGIONS), key=lambda p: p[3])
    tr = 256

    def body(g_ref, o_ref):
        pieces = [g_ref[d, :, lo:hi] for d, lo, hi, _ in by_offset]
        pad = jnp.zeros((tr, L_COLS - L_BA - 2 * A_HEADS), g_ref.dtype)
        o_ref[...] = jnp.concatenate(pieces + [pad], axis=1)

    return pl.pallas_call(
        body, name=name, grid=(D_MODEL // tr,),
        in_specs=[pl.BlockSpec((N_DEV, tr, SHARD_COLS), lambda i: (0, i, 0))],
        out_specs=pl.BlockSpec((tr, L_COLS), lambda i: (i, 0)),
        out_shape=jax.ShapeDtypeStruct((D_MODEL, L_COLS), g_in.dtype),
        compiler_params=_cparams(("parallel",)),
    )(g_in)


def _full_conv(g_conv):
    return jnp.pad(g_conv.transpose(1, 0, 2).reshape(CONV_K, 3 * A_WIDTH), ((0, 8 - CONV_K), (0, 0)))


def _forward(x, weights, shards, small):
    a_log, dt_bias, norm_w, sinks, ln_g, ln_b = small
    tm = min(512, x.shape[0])
    saved, weights = [], [list(w) for w in weights]
    whole = lambda arrs: _Direct([(a, False, j, ()) for j, a in enumerate(arrs)], [_gathered(a) for a in arrs])
    for l in range(DEPTH):
        rider = whole(shards[l][1:]) if weights[l][1] is None else None
        h, *got = _as_list(_matmul(x, weights[l][0], form="nn", tm=tm, tn=1152, tk=D_MODEL, name=f"in_proj_{l}",
                                   rider=rider))
        if rider:
            weights[l][1:] = [got[0].reshape(D_MODEL, D_MODEL), _full_conv(got[1])]
        w_in_l, w_out_l, conv_l = weights[l]
        qkv = _prep_fwd(h, conv_l, name=f"prep_fwd_{l}")
        al, dt, nw, sk = _pad_row(a_log[l]), _pad_row(dt_bias[l]), norm_w[l][None, :], _pad_row(sinks[l])
        ahead = l + 1 < DEPTH and weights[l + 1][0] is None
        rider = whole(shards[l + 1][1:]) if ahead else None
        ycat, *got = _as_list(_swa_fwd(h, sk, name=f"swa_fwd_{l}", rider=rider))
        if ahead:
            weights[l + 1][1:] = [got[0].reshape(D_MODEL, D_MODEL), _full_conv(got[1])]
        rider = whole(shards[l + 1][:1]) if ahead else None
        ycat, s_in, t_in, *got = _gdn_fwd(qkv, h, al, dt, nw, ycat, name=f"gdn_fwd_{l}", rider=rider)
        if ahead:
            weights[l + 1][0] = _full_w_in(got[0], f"w_in_columns_{l + 1}")
        r, xn = _out_ln_fwd(ycat, w_out_l, x, ln_g[l][None, :], ln_b[l][None, :], name=f"out_ln_{l}")
        saved.append((x, h, qkv, s_in, t_in, ycat, r, al, dt, nw, sk))
        x = xn
    return x, saved, weights


def _w_in_blocks(g_main, g_ba, name):
    rows, tr = g_main.shape[0], 256
    pieces = list(_shard_pieces(_REGIONS))

    def body(m_ref, b_ref, o_ref):
        src = lambda off: (b_ref, off - L_BA) if off >= L_BA else (m_ref, off)
        blocks = [[] for _ in range(N_DEV)]
        for d, lo, hi, off in pieces:
            a, o = src(off)
            blocks[d].append(a[:, o:o + hi - lo])
        for d in range(N_DEV):
            o_ref[d] = jnp.concatenate(blocks[d], axis=1).astype(BF16)

    return pl.pallas_call(
        body, name=name, grid=(rows // tr,),
        in_specs=[pl.BlockSpec((tr, L_MAIN), lambda i: (i, 0)), pl.BlockSpec((tr, LANE), lambda i: (i, 0))],
        out_specs=pl.BlockSpec((N_DEV, tr, SHARD_COLS), lambda i: (0, i, 0)),
        out_shape=jax.ShapeDtypeStruct((N_DEV, rows, SHARD_COLS), BF16),
        compiler_params=_cparams(("parallel",)),
    )(g_main, g_ba)


def _small_blocks(g):
    c_conv = g["conv_w"].reshape(CONV_K, N_DEV, CONV_SHARD_COLS).transpose(1, 0, 2)
    c_small = [jnp.broadcast_to(g[n][None], (N_DEV,) + g[n].shape) for n, _ in SMALL_SIZES]
    return _pack_small(c_conv, c_small)


def _contributions(g):
    c_out = g["w_out"].astype(BF16).reshape(N_DEV, OUT_SHARD_ROWS, D_MODEL)
    return _w_in_blocks(*g["w_in_parts"], name="w_in_grad_blocks_above"), c_out, _small_blocks(g)


def _backward_layer(l, dx, saved_l, weights_l, ln_g_l, above=None):
    x_in, h, qkv, s_in, t_in, ycat, r, al, dt, nw, sk = saved_l
    w_in_l, w_out_l, conv_l = weights_l
    tm = min(512, x_in.shape[0])
    dr, d_lng, d_lnb = _ln_bwd(dx, r, ln_g_l[None, :], name=f"ln_bwd_{l}")
    dycat = _matmul(dr, w_out_l, form="nt", tm=tm, tn=D_MODEL, tk=D_MODEL, name=f"out_proj_dx_{l}")
    d_wout = _matmul(ycat, dr, form="tn", tm=512, tn=D_MODEL, tk=tm, name=f"out_proj_dw_{l}")
    rider, p_in, p_out, p_small = None, None, None, None
    recv = lambda c: jax.ShapeDtypeStruct((DEPTH,) + c.shape, c.dtype)
    if above:
        c_out = d_wout.astype(BF16).reshape(N_DEV, OUT_SHARD_ROWS, D_MODEL)
        rider = _Direct([(above[1], True, 0, (l + 1,)), (above[2], True, 1, (l + 1,)), (c_out, True, 0, (l,))],
                        [recv(above[1]), recv(above[2])])
    dh, d_sk, *got = _swa_bwd(h, sk, dycat, name=f"swa_bwd_{l}", rider=rider)
    if above:
        p_out, p_small = got
        rider = _Direct([(above[0], True, 0, (l + 1,))], [recv(above[0])])
    dh, dqkv_n, dba, d_al, d_dt, d_nw, *got = _gdn_bwd(qkv, h, al, dt, nw, s_in, t_in, dycat, dh,
                                                       name=f"gdn_bwd_{l}", rider=rider)
    dh, d_conv = _prep_bwd(h, conv_l, dqkv_n, dh, name=f"prep_bwd_{l}")
    d_win_ba = _matmul(x_in, dba, form="tn", tm=D_MODEL, tn=LANE, tk=tm, name=f"in_proj_dw_ba_{l}")
    grads = dict(w_out=d_wout, conv_w=d_conv[:CONV_K], a_log=d_al[0, :A_HEADS], dt_bias=d_dt[0, :A_HEADS],
                 norm_w=d_nw[0], sinks=d_sk[0, :B_Q_HEADS], ln_g=d_lng[0], ln_b=d_lnb[0])
    dw = functools.partial(_matmul, x_in, dh, form="tn", tm=512, tn=L_MAIN // 2, tk=tm)
    if not above:
        grads["w_in_parts"] = (dw(name=f"in_proj_dw_{l}"), d_win_ba)
    else:
        p_in, = got
        half = D_MODEL // 2
        top = dw(name=f"in_proj_dw_top_{l}", a_cols=(0, half))
        blocks = _w_in_blocks(top, d_win_ba[:half], name=f"w_in_grad_blocks_top_{l}")
        rider = _Direct([(blocks, True, 0, (l,), (pl.ds(0, half),))], [p_in])
        bottom, p_in = dw(name=f"in_proj_dw_bottom_{l}", a_cols=(half, half), rider=rider)
        blocks = _w_in_blocks(bottom, d_win_ba[half:], name=f"w_in_grad_blocks_bottom_{l}")
        rider = _Direct([(blocks, True, 0, (l,), (pl.ds(half, half),)),
                         (_small_blocks(grads), True, 1, (l,))], [p_in, p_small])
    dx, *got = _as_list(_matmul(dh, w_in_l, form="nt", tm=tm, tn=D_MODEL, tk=L_MAIN // 2, name=f"in_proj_dx_{l}",
                                add=dr, add_scale=DEEPNORM_ALPHA, extra=(dba, w_in_l, L_BA // LANE), rider=rider))
    bufs = (got[0], p_out, got[1]) if above else None
    return dx, grads, bufs


def _all_gather(shards, *, name):
    n_arr = len(shards)

    def body(*refs):
        x_refs, out_refs = refs[:n_arr], refs[n_arr:2 * n_arr]
        send_sems, recv_sems, local_sems = refs[2 * n_arr:]
        x, y, c = _me()
        me, sibling = (x, y, c), (x, y, 1 - c)
        chips = [(1 - x, y), (x, 1 - y), (1 - x, 1 - y)]

        def copy(a, k, block, to, src=None):
            dst = out_refs[a].at[_flat_id(block)]
            return _remote(dst if src is None else src, dst, send_sems.at[a, k], recv_sems.at[a, k], to)

        mine = [pltpu.make_async_copy(x_refs[a], out_refs[a].at[_flat_id(me)], local_sems.at[a])
                for a in range(n_arr)]
        for cp in mine:
            cp.start()
        first = []
        for a in range(n_arr):
            first.append(copy(a, 0, me, sibling, src=x_refs[a]))
            first += [copy(a, 1 + j, me, (*chip, c), src=x_refs[a]) for j, chip in enumerate(chips)]
        for cp in first:
            cp.start()
        passed = []
        for j, chip in enumerate(chips):
            for a in range(n_arr):
                copy(a, 1 + j, (*chip, c), me).wait_recv()
                fwd = copy(a, 4 + j, (*chip, c), sibling)
                fwd.start()
                passed.append(fwd)
        for a in range(n_arr):
            copy(a, 0, sibling, me).wait_recv()
            for j, chip in enumerate(chips):
                copy(a, 4 + j, (*chip, 1 - c), me).wait_recv()
        for cp in first + passed:
            cp.wait_send()
        for cp in mine:
            cp.wait()

    return pl.pallas_call(
        body, name=name, in_specs=[_ANY] * n_arr, out_specs=[_ANY] * n_arr,
        out_shape=[jax.ShapeDtypeStruct((N_DEV,) + s.shape, s.dtype) for s in shards],
        scratch_shapes=[pltpu.SemaphoreType.DMA((n_arr, N_DEV - 1)), pltpu.SemaphoreType.DMA((n_arr, N_DEV - 1)),
                        pltpu.SemaphoreType.DMA((n_arr,))],
    )(*shards)


def _adamw(parts, w, m, v, *, tr, name):
    depth, rows, cols = w.shape
    c1 = 1.0 - ADAM_B1 ** ADAM_STEP
    c2 = 1.0 - ADAM_B2 ** ADAM_STEP

    def body(g_ref, w_ref, m_ref, v_ref, go_ref, d_ref, mo_ref, vo_ref):
        g = g_ref[0, 0].astype(F32)
        for s in range(1, N_DEV):
            g = g + g_ref[0, s].astype(F32)
        m_new = ADAM_B1 * m_ref[0] + (1.0 - ADAM_B1) * g
        v_new = ADAM_B2 * v_ref[0] + (1.0 - ADAM_B2) * (g * g)
        go_ref[0] = g
        mo_ref[0] = m_new
        vo_ref[0] = v_new
        d_ref[0] = -ADAM_LR * ((m_new / c1) / (jnp.sqrt(v_new / c2) + ADAM_EPS) + ADAM_WD * w_ref[0])

    tile = pl.BlockSpec((1, tr, cols), lambda l, i: (l, i, 0))
    return pl.pallas_call(
        body, name=name, grid=(depth, rows // tr),
        in_specs=[pl.BlockSpec((1, N_DEV, tr, cols), lambda l, i: (l, 0, i, 0)), tile, tile, tile],
        out_specs=[tile] * 4, out_shape=[jax.ShapeDtypeStruct(w.shape, F32)] * 4,
        compiler_params=_cparams(("parallel", "parallel")),
    )(parts, w, m, v)


def _pack_small(conv, small):
    lead = conv.shape[:-2]
    flat = jnp.concatenate([conv.reshape(lead + (CS_CONV,))] + list(small), axis=-1)
    pad = CS_ROWS * LANE - flat.shape[-1]
    flat = jnp.concatenate([flat, jnp.zeros(lead + (pad,), F32)], axis=-1)
    return flat.reshape(lead + (CS_ROWS, LANE))


def _unpack_small(p):
    flat = p.reshape(DEPTH, CS_ROWS * LANE)
    conv = flat[:, :CS_CONV].reshape(DEPTH, CONV_K, CONV_SHARD_COLS)
    small, off = [], CS_CONV
    for _, n in SMALL_SIZES:
        small.append(flat[:, off:off + n])
        off += n
    return conv, small


def kernel(x, w_in, conv_w, a_log, dt_bias, norm_w, sinks, w_out, ln_g, ln_b, loss_target, m_w_in, m_conv_w, m_a_log, m_dt_bias, m_norm_w, m_sinks, m_w_out, m_ln_g, m_ln_b, v_w_in, v_conv_w, v_a_log, v_dt_bias, v_norm_w, v_sinks, v_w_out, v_ln_g, v_ln_b):
    small = [a_log, dt_bias, norm_w, sinks, ln_g, ln_b]
    shards = [[w_in[l].astype(BF16), w_out[l].astype(BF16), conv_w[l]] for l in range(DEPTH)]
    g_in0, = _all_gather(shards[0][:1], name="weights_all_gather_0")
    weights = [[_full_w_in(g_in0, "w_in_columns_0"), None, None]] + [[None, None, None]] * (DEPTH - 1)

    y, saved, weights = _forward(x[0], weights, shards, small)
    dx, loss_lanes = _loss_head(y, loss_target[0], name="loss_head")
    loss = lax.psum(0.5 * jnp.sum(loss_lanes) * (1.0 / D_MODEL), ("x", "y", "c"))
    dx, g1, _ = _backward_layer(1, dx, saved[1], weights[1], ln_g[1])
    dx, _, (p_in, p_out, p_small) = _backward_layer(0, dx, saved[0], weights[0], ln_g[0], above=_contributions(g1))

    o_in = _adamw(p_in, w_in, m_w_in, v_w_in, tr=256, name="adamw_w_in")
    o_out = _adamw(p_out, w_out, m_w_out, v_w_out, tr=OUT_SHARD_ROWS, name="adamw_w_out")
    o_small = _adamw(p_small, _pack_small(conv_w, small),
                     _pack_small(m_conv_w, [m_a_log, m_dt_bias, m_norm_w, m_sinks, m_ln_g, m_ln_b]),
                     _pack_small(v_conv_w, [v_a_log, v_dt_bias, v_norm_w, v_sinks, v_ln_g, v_ln_b]),
                     tr=CS_ROWS, name="adamw_small")
    outs = []
    for k in range(4):
        cv, sm = _unpack_small(o_small[k])
        outs += [o_in[k], cv, sm[0], sm[1], sm[2], sm[3], o_out[k], sm[4], sm[5]]
    return (loss, dx[None], *outs)
```

```python
import functools

import jax
import jax.numpy as jnp
from jax import lax
from jax.experimental import pallas as pl
from jax.experimental.pallas import tpu as pltpu

F32 = jnp.float32
BF16 = jnp.bfloat16
MM_DTYPE = BF16

N_DEV = 8
D_MODEL = 1024
DEPTH = 2
A_HEADS = 4
A_HEAD_DIM = 128
A_WIDTH = 512
CONV_K = 4
CHUNK = 64
SUPER = 256
G_HEADS = 4
NEWTON_STEPS = 1
B_Q_HEADS = 8
B_KV_HEADS = 2
B_HEAD_DIM = 64
B_GROUP = 4
B_WIDTH = 512
WINDOW = 128
BLOCK = 128
IN_COLS = 3336
SHARD_COLS = IN_COLS // N_DEV
OUT_SHARD_ROWS = D_MODEL // N_DEV
CONV_SHARD_COLS = 3 * A_WIDTH // N_DEV
DEEPNORM_ALPHA = (2 * DEPTH) ** 0.25
LN_EPS = 1e-5
RMS_EPS = 1e-6
L2_EPS = 1e-6
ADAM_LR, ADAM_B1, ADAM_B2, ADAM_EPS, ADAM_WD, ADAM_STEP = 0.001, 0.9, 0.999, 1e-08, 0.01, 10

LANE = 128
L_QB, L_ZB, L_ZA, L_KB, L_VB, L_QKV, L_BA = 0, 512, 1024, 1536, 1664, 1792, 3328
L_SWA = 1792
L_MAIN = 3328
L_COLS = 3456
SMALL_SIZES = (("a_log", 4), ("dt_bias", 4), ("norm_w", 128), ("sinks", 8), ("ln_g", 1024), ("ln_b", 1024))
CS_CONV = CONV_K * CONV_SHARD_COLS
CS_ROWS = 24
VMEM_LIMIT = 48 * 1024 * 1024


def _cparams(sem=None):
    return pltpu.CompilerParams(dimension_semantics=sem, vmem_limit_bytes=VMEM_LIMIT)


def _mm(a, b):
    return jnp.dot(a.astype(MM_DTYPE), b.astype(MM_DTYPE), preferred_element_type=F32)


def _mm_nt(a, b):
    return lax.dot_general(a.astype(MM_DTYPE), b.astype(MM_DTYPE), (((1,), (1,)), ((), ())),
                           preferred_element_type=F32)


def _mm_tn(a, b):
    return lax.dot_general(a.astype(MM_DTYPE), b.astype(MM_DTYPE), (((0,), (0,)), ((), ())),
                           preferred_element_type=F32)


def _split(a):
    hi = a.astype(BF16)
    return hi, (a - hi.astype(F32)).astype(BF16)


def _hp(a2, b2):
    d = lambda p, q: jnp.dot(p, q, preferred_element_type=F32)
    return d(a2[0], b2[0]) + (d(a2[0], b2[1]) + d(a2[1], b2[0]))


def _silu(x):
    return x * jax.nn.sigmoid(x)


@jax.custom_vjp
def _stack(parts):
    return jnp.stack(parts)


_stack.defvjp(lambda parts: (jnp.stack(parts), None), lambda _, g: (tuple(g[i] for i in range(g.shape[0])),))


def _softplus(x):
    return jnp.maximum(x, 0.0) + jnp.log1p(jnp.exp(-jnp.abs(x)))


_ANY = pl.BlockSpec(memory_space=pl.ANY)


def _me():
    return lax.axis_index("x"), lax.axis_index("y"), lax.axis_index("c")


def _flat_id(pos):
    return 4 * pos[0] + 2 * pos[1] + pos[2]


def _remote(src, dst, send_sem, recv_sem, to):
    return pltpu.make_async_remote_copy(src_ref=src, dst_ref=dst, send_sem=send_sem, recv_sem=recv_sem,
                                        device_id=to, device_id_type=pl.DeviceIdType.MESH)


class _Direct:
    def __init__(self, items, bufs):
        self.items, self.bufs = list(items), list(bufs)
        self.n_src, self.n_buf = len(self.items), len(self.bufs)
        self.old = [j for j, b in enumerate(self.bufs) if not isinstance(b, jax.ShapeDtypeStruct)]
        self.args = [it[0] for it in self.items] + [self.bufs[j] for j in self.old]
        self.out_shape = [jax.ShapeDtypeStruct(b.shape, b.dtype) for b in self.bufs]
        self.scratch = [pltpu.SemaphoreType.DMA((self.n_src, N_DEV - 1)),
                        pltpu.SemaphoreType.DMA((self.n_src, N_DEV - 1)), pltpu.SemaphoreType.DMA((self.n_src,))]

    def aliases(self, in_base, out_base):
        return {in_base + self.n_src + pos: out_base + j for pos, j in enumerate(self.old)}

    def copies(self, in_refs, out_refs, sems):
        send_sems, recv_sems, local_sems = sems
        x, y, c = _me()
        me = _flat_id((x, y, c))
        peers = [(x ^ ((rel >> 2) & 1), y ^ ((rel >> 1) & 1), c ^ (rel & 1)) for rel in range(1, N_DEV)]
        local, sends, recvs = [], [], []
        for a, (_, per_dest, j, prefix, *rest) in enumerate(self.items):
            src = lambda d: in_refs[a].at[d] if per_dest else in_refs[a]
            dst = lambda s: out_refs[j].at[tuple(prefix) + (s,) + tuple(rest[0] if rest else ())]
            local.append(pltpu.make_async_copy(src(me), dst(me), local_sems.at[a]))
            for k, peer in enumerate(peers):
                pid = _flat_id(peer)
                sends.append(_remote(src(pid), dst(me), send_sems.at[a, k], recv_sems.at[a, k], peer))
                recvs.append(_remote(src(pid), dst(pid), send_sems.at[a, k], recv_sems.at[a, k], peer))
        return local, sends, recvs

    def start(self, in_refs, out_refs, sems):
        local, sends, _ = self.copies(in_refs, out_refs, sems)
        for cp in local + sends:
            cp.start()

    def wait(self, in_refs, out_refs, sems):
        local, sends, recvs = self.copies(in_refs, out_refs, sems)
        for cp in recvs:
            cp.wait_recv()
        for cp in sends:
            cp.wait_send()
        for cp in local:
            cp.wait()


def _pcall(core, *, name, grid, in_specs, out_specs, out_shape, args, sem, scratch_shapes=(), aliases=None,
           rider=None):
    n_in, n_out, n_scr = len(in_specs), len(out_specs), len(scratch_shapes)
    n_rin, n_rout = (len(rider.args), rider.n_buf) if rider else (0, 0)

    def body(*refs):
        ins, outs = refs[:n_in], refs[n_in + n_rin:n_in + n_rin + n_out]
        scr = refs[n_in + n_rin + n_out + n_rout:n_in + n_rin + n_out + n_rout + n_scr]
        if rider:
            r_refs = (refs[n_in:n_in + rider.n_src], refs[n_in + n_rin + n_out:n_in + n_rin + n_out + n_rout],
                      refs[n_in + n_rin + n_out + n_rout + n_scr:])
            ids = [pl.program_id(d) for d in range(len(grid))]
            first = functools.reduce(lambda p, q: p & q, [i == 0 for i in ids])
            last = functools.reduce(lambda p, q: p & q, [i == g - 1 for i, g in zip(ids, grid)])
            pl.when(first)(lambda: rider.start(*r_refs))
        core(ins, outs, scr)
        if rider:
            pl.when(last)(lambda: rider.wait(*r_refs))

    aliases = dict(aliases or {})
    if rider:
        sem = ("arbitrary",) * len(grid)
        aliases.update(rider.aliases(n_in, n_out))
    return pl.pallas_call(
        body, name=name, grid=grid, in_specs=list(in_specs) + [_ANY] * n_rin,
        out_specs=list(out_specs) + [_ANY] * n_rout,
        out_shape=list(out_shape) + (rider.out_shape if rider else []),
        scratch_shapes=list(scratch_shapes) + (rider.scratch if rider else []),
        input_output_aliases=aliases, compiler_params=_cparams(sem),
    )(*args, *(rider.args if rider else []))


def _exchange(direct, *, name):
    n_in = len(direct.args)

    def body(*refs):
        r_refs = refs[:direct.n_src], refs[n_in:n_in + direct.n_buf], refs[n_in + direct.n_buf:]
        direct.start(*r_refs)
        direct.wait(*r_refs)

    return pl.pallas_call(
        body, name=name, in_specs=[_ANY] * n_in, out_specs=[_ANY] * direct.n_buf, out_shape=direct.out_shape,
        input_output_aliases=direct.aliases(0, 0), scratch_shapes=direct.scratch,
    )(*direct.args)


def _matmul(a, b, *, form, tm, tn, tk, name, add=None, add_scale=1.0, extra=None, rider=None, a_cols=None):
    if form == "nn":
        (m, kk), n = a.shape, b.shape[1]
        a_spec = pl.BlockSpec((tm, tk), lambda i, j, k: (i, k))
        b_spec = pl.BlockSpec((tk, tn), lambda i, j, k: (k, j))
        dn = (((1,), (0,)), ((), ()))
    elif form == "nt":
        (m, kk), n = a.shape, b.shape[0]
        a_spec = pl.BlockSpec((tm, tk), lambda i, j, k: (i, k))
        b_spec = pl.BlockSpec((tn, tk), lambda i, j, k: (j, k))
        dn = (((1,), (1,)), ((), ()))
    else:
        kk, n = a.shape[0], b.shape[1]
        m0, m = a_cols or (0, a.shape[1])
        assert m0 % tm == 0
        a_spec = pl.BlockSpec((tk, tm), lambda i, j, k: (k, i + m0 // tm))
        b_spec = pl.BlockSpec((tk, tn), lambda i, j, k: (k, j))
        dn = (((0,), (0,)), ((), ()))
    assert m % tm == 0 and n % tn == 0 and kk % tk == 0, (name, m, n, kk)
    has_add, has_extra = add is not None, extra is not None

    def core(ins, outs, _):
        a_ref, b_ref = ins[:2]
        o_ref = outs[0]
        rest = ins[2:]
        k = pl.program_id(2)
        p = lax.dot_general(a_ref[...].astype(MM_DTYPE), b_ref[...].astype(MM_DTYPE), dn,
                            preferred_element_type=F32)

        @pl.when(k == 0)
        def _():
            first = p
            pos = 0
            if has_extra:
                first = first + _mm_nt(rest[0][...], rest[1][...])
                pos = 2
            if has_add:
                first = first + add_scale * rest[pos][...]
            o_ref[...] = first

        @pl.when(k > 0)
        def _():
            o_ref[...] += p

    in_specs = [a_spec, b_spec]
    args = [a, b]
    if has_extra:
        a2, b2, idx = extra
        in_specs += [pl.BlockSpec((tm, LANE), lambda i, j, k: (i, 0)),
                     pl.BlockSpec((tn, LANE), lambda i, j, k: (j, idx))]
        args += [a2, b2]
    if has_add:
        in_specs.append(pl.BlockSpec((tm, tn), lambda i, j, k: (i, j)))
        args.append(add)
    res = _pcall(core, name=name, grid=(m // tm, n // tn, kk // tk), in_specs=in_specs,
                 out_specs=[pl.BlockSpec((tm, tn), lambda i, j, k: (i, j))],
                 out_shape=[jax.ShapeDtypeStruct((m, n), F32)], args=args,
                 sem=("parallel", "parallel", "arbitrary"), rider=rider)
    return res if rider else res[0]


ZERO_TAIL = 8


def _with_tail(x):
    return jnp.concatenate([x, jnp.zeros((ZERO_TAIL,) + x.shape[1:], x.dtype)], axis=0)


def _shift_down(x, k):
    return pltpu.roll(x, k, 0)


def _shift_up(x, k):
    return pltpu.roll(x, x.shape[0] - k, 0)


def _conv_slab(x, w):
    return w[3:4] * x + w[2:3] * _shift_down(x, 1) + w[1:2] * _shift_down(x, 2) + w[0:1] * _shift_down(x, 3)


def _prep_fwd(h, conv_w, *, name):
    t_len = h.shape[0]

    def body(x_ref, w_ref, o_ref):
        s = pl.program_id(0)
        y = _silu(_conv_slab(_with_tail(x_ref[...]), w_ref[...])[:t_len])
        rs = lax.rsqrt(jnp.sum(y * y, axis=-1, keepdims=True) + L2_EPS)
        scale = jnp.where(s < A_HEADS, A_HEAD_DIM ** -0.5, 1.0)
        o_ref[...] = jnp.where(s < 2 * A_HEADS, y * rs * scale, y)

    return pl.pallas_call(
        body, name=name, grid=(12,),
        in_specs=[pl.BlockSpec((t_len, LANE), lambda s: (0, L_QKV // LANE + s)),
                  pl.BlockSpec((8, LANE), lambda s: (0, s))],
        out_specs=pl.BlockSpec((t_len, LANE), lambda s: (0, s)),
        out_shape=jax.ShapeDtypeStruct((t_len, 3 * A_WIDTH), F32),
        compiler_params=_cparams(("parallel",)),
    )(h, conv_w)


def _prep_bwd(h, conv_w, d_out, dh, *, name):
    t_len = h.shape[0]

    def body(x_ref, w_ref, g_ref, dh_in, dx_ref, dw_ref):
        del dh_in
        s = pl.program_id(0)
        x = _with_tail(x_ref[...])
        g = _with_tail(g_ref[0])
        w = w_ref[...]
        c = _conv_slab(x, w)
        sg = jax.nn.sigmoid(c)
        y = c * sg
        rs = lax.rsqrt(jnp.sum(y * y, axis=-1, keepdims=True) + L2_EPS)
        scale = jnp.where(s < A_HEADS, A_HEAD_DIM ** -0.5, 1.0)
        dy_n = scale * (rs * g - y * (rs * rs * rs) * jnp.sum(g * y, axis=-1, keepdims=True))
        dy = jnp.where(s < 2 * A_HEADS, dy_n, g)
        dc = dy * (sg * (1.0 + c * (1.0 - sg)))
        dx = w[3:4] * dc + w[2:3] * _shift_up(dc, 1) + w[1:2] * _shift_up(dc, 2) + w[0:1] * _shift_up(dc, 3)
        dx_ref[...] = dx[:t_len]
        dws = [jnp.sum(dc * _shift_down(x, 3 - j), axis=0, keepdims=True) if j < 3
               else jnp.sum(dc * x, axis=0, keepdims=True) for j in range(CONV_K)]
        dw_ref[...] = jnp.concatenate(dws + [jnp.zeros((8 - CONV_K, LANE), F32)], axis=0)

    slab = pl.BlockSpec((t_len, LANE), lambda s: (0, L_QKV // LANE + s))
    return pl.pallas_call(
        body, name=name, grid=(12,),
        in_specs=[slab, pl.BlockSpec((8, LANE), lambda s: (0, s)),
                  pl.BlockSpec((1, t_len, LANE), lambda s: (s // A_HEADS, 0, s % A_HEADS)), _ANY],
        out_specs=[slab, pl.BlockSpec((8, LANE), lambda s: (0, s))],
        out_shape=[jax.ShapeDtypeStruct((t_len, L_MAIN), F32), jax.ShapeDtypeStruct((8, 3 * A_WIDTH), F32)],
        input_output_aliases={3: 0},
        compiler_params=_cparams(("parallel",)),
    )(h, conv_w, d_out, dh)


N_LEVELS = 5
MF_TRIL, MF_STRIL, MF_DIAG8, MF_LOW16, MF_EYE = 0, 1, 2, 3, 3 + N_LEVELS
MB_CUM, MB_CUM_T, MB_TOT = 0, 1, 2


def _gdn_masks():
    r = lax.broadcasted_iota(jnp.int32, (SUPER, SUPER), 0)
    c = lax.broadcasted_iota(jnp.int32, (SUPER, SUPER), 1)
    same = lambda shift: (r >> shift) == (c >> shift)
    ninf = lambda m: jnp.where(m, 0.0, -jnp.inf).astype(F32)
    one = lambda m: m.astype(F32)
    mf = jnp.stack([ninf(r >= c), ninf(r > c), one(same(3))]
                   + [one(same(4 + lv) & jnp.logical_not(same(3 + lv))) for lv in range(N_LEVELS)] + [one(r == c)])
    mb = jnp.stack([one(r >= c), one(r <= c), jnp.ones((SUPER, SUPER), F32)]).astype(BF16)
    return mf, mb


def _tri_inv_impl(a, mf):
    d = lambda p, q: jnp.dot(p.astype(BF16), q.astype(BF16), preferred_element_type=F32)
    dd = lambda p, q: jnp.dot(p, q, preferred_element_type=F32)
    eye = mf[MF_EYE]
    a0 = a * mf[MF_DIAG8]
    a2 = d(a0, a0)
    a4 = d(a2, a2)
    t = d(d(eye - a0, eye + a2), eye + a4)
    for level in range(N_LEVELS):
        t = t - d(d(t, a * mf[MF_LOW16 + level]), t)
    a_hi, a_lo = _split(a)
    for _ in range(NEWTON_STEPS):
        t_hi, t_lo = _split(t)
        resid = (eye - t) - (dd(a_hi, t_hi) + (dd(a_hi, t_lo) + dd(a_lo, t_hi)))
        r_hi, r_lo = _split(resid)
        t = t + (dd(t_hi, r_hi) + dd(t_hi, r_lo))
    return t


@jax.custom_vjp
def _wy_apply(a, rhs, t):
    return _mm(t, rhs)


def _wy_apply_fwd(a, rhs, t):
    x = _mm(t, rhs)
    return x, (t, x)


def _wy_apply_bwd(res, dx):
    t, x = res
    d_rhs = _mm_tn(t, dx)
    return -_mm_nt(d_rhs, x), d_rhs, jnp.zeros_like(t)


_wy_apply.defvjp(_wy_apply_fwd, _wy_apply_bwd)


@functools.partial(jax.custom_vjp, nondiff_argnums=(1,))
def _lane_roll(x, shift):
    return pltpu.roll(x, shift % LANE, 1)


_lane_roll.defvjp(lambda x, shift: (_lane_roll(x, shift), None), lambda shift, _, g: (_lane_roll(g, -shift),))


def _mask_times_lanes(x, mask):
    lane = lax.broadcasted_iota(jnp.int32, (1, LANE), 1)
    x = jnp.where(lane < A_HEADS, x, 0.0)
    x1 = x.astype(BF16).astype(F32)
    x2 = (x - x1).astype(BF16).astype(F32)
    x3 = (x - x1 - x2).astype(BF16).astype(F32)
    pieces = x1 + pltpu.roll(x2, A_HEADS, 1) + pltpu.roll(x3, 2 * A_HEADS, 1)
    res = jnp.dot(mask, pieces.astype(BF16), preferred_element_type=F32)
    return res + pltpu.roll(res, LANE - A_HEADS, 1) + pltpu.roll(res, LANE - 2 * A_HEADS, 1)


@jax.custom_vjp
def _chunk_sums(g, mb):
    return _mask_times_lanes(g, mb[MB_CUM]), _mask_times_lanes(g, mb[MB_TOT])


def _chunk_sums_fwd(g, mb):
    return _chunk_sums(g, mb), mb


def _chunk_sums_bwd(mb, d):
    lane = lax.broadcasted_iota(jnp.int32, (1, LANE), 1)
    dg = _mask_times_lanes(d[0], mb[MB_CUM_T]) + _mask_times_lanes(d[1], mb[MB_TOT])
    return jnp.where(lane < A_HEADS, dg, 0.0), jnp.zeros_like(mb)


_chunk_sums.defvjp(_chunk_sums_fwd, _chunk_sums_bwd)


def _gdn_gates(ba, alog, dtb, mb):
    beta = jax.nn.sigmoid(ba)
    g = -jnp.exp(alog) * _softplus(_lane_roll(ba, -A_HEADS) + dtb)
    gc, gl = _chunk_sums(g, mb)
    return beta, gc, gl, gc.T


def _gdn_block(s, q, k, v, z, gates, nw, h, t_known, mf):
    n = q.shape[0]
    beta_all, gc_all, gl_all, gct_all = gates
    lane = lax.broadcasted_iota(jnp.int32, (1, LANE), 1)
    sub = lax.broadcasted_iota(jnp.int32, (LANE, 1), 0)
    col = lambda x: jnp.sum(jnp.where(lane == h, x, 0.0), axis=1, keepdims=True)
    wide = lambda c: jnp.broadcast_to(c, (n, LANE))
    gc, gl = col(gc_all), col(gl_all)
    gc_row = jnp.sum(jnp.where(sub == h, gct_all, 0.0), axis=0, keepdims=True)
    beta_w, eg_w = wide(col(beta_all)), wide(jnp.exp(gc))
    diff = gc - gc_row
    decay = jnp.exp(diff + mf[MF_TRIL])
    kb = k * beta_w
    a_mat = _mm_nt(kb, k) * jnp.exp(diff + mf[MF_STRIL])
    rhs = jnp.concatenate([v * beta_w, kb * eg_w], axis=1)
    if t_known is None:
        t_mat = _tri_inv_impl(a_mat, mf)
        uw = _mm(t_mat, rhs)
    else:
        t_mat = t_known
        uw = _wy_apply(a_mat, rhs, t_known)
    u, w = uw[:, :LANE], uw[:, LANE:]
    qk = _mm_nt(q, k) * decay
    q_dec = q * eg_w
    k_dec = k * wide(jnp.exp(gl - gc))
    v_new = u - _mm(w, s)
    o = _mm(q_dec, s) + _mm(qk, v_new)
    s = s * jnp.exp(gl[0:1]) + _mm_tn(k_dec, v_new)
    o = o * lax.rsqrt(jnp.mean(o * o, axis=-1, keepdims=True) + RMS_EPS) * nw
    return o * _silu(z), s, t_mat


def _gdn_fwd(qkv, h, alog, dtb, nw, ycat, *, name, rider=None):
    t_len = qkv.shape[0]
    nsc = t_len // SUPER

    def core(ins, outs, scr):
        q_ref, k_ref, v_ref, z_ref, ba_ref, al_ref, dt_ref, nw_ref, mf_ref, mb_ref, _ = ins
        y_ref, sin_ref, t_ref = outs
        s_scr, = scr
        sc, pg = pl.program_id(0), pl.program_id(1)
        heads = [pg * G_HEADS + hh for hh in range(G_HEADS)]

        @pl.when(sc == 0)
        def _():
            for hd in heads:
                s_scr[hd] = jnp.zeros((A_HEAD_DIM, A_HEAD_DIM), F32)

        per_head = lambda ref: jnp.stack([ref[:, hh * LANE:(hh + 1) * LANE] for hh in range(G_HEADS)])
        states = jnp.stack([s_scr[hd] for hd in heads])
        gates = _gdn_gates(ba_ref[...], al_ref[...], dt_ref[...], mb_ref[...])
        fn = jax.vmap(_gdn_block, in_axes=(0, 0, 0, 0, 0, None, None, 0, None, None))
        y, s_new, t_mat = fn(states, per_head(q_ref), per_head(k_ref), per_head(v_ref), per_head(z_ref),
                             gates, nw_ref[...], jnp.stack(heads), None, mf_ref[...])
        sin_ref[0] = states
        t_ref[0] = t_mat
        for hh, hd in enumerate(heads):
            y_ref[:, hh * LANE:(hh + 1) * LANE] = y[hh]
            s_scr[hd] = s_new[hh]

    gw = G_HEADS * LANE
    blk = lambda off: pl.BlockSpec((SUPER, gw), lambda sc, pg: (sc, off // G_HEADS + pg))
    row = pl.BlockSpec((1, LANE), lambda sc, pg: (0, 0))
    mf, mb = _gdn_masks()
    whole = lambda a: pl.BlockSpec(a.shape, lambda sc, pg: (0, 0, 0))
    return _pcall(
        core, name=name, grid=(nsc, A_HEADS // G_HEADS),
        in_specs=[blk(0), blk(4), blk(8), blk(L_ZA // LANE),
                  pl.BlockSpec((SUPER, LANE), lambda sc, pg: (sc, L_BA // LANE)), row, row, row,
                  whole(mf), whole(mb), _ANY],
        out_specs=[blk(0),
                   pl.BlockSpec((1, G_HEADS, A_HEAD_DIM, A_HEAD_DIM), lambda sc, pg: (sc, pg, 0, 0)),
                   pl.BlockSpec((1, G_HEADS, SUPER, SUPER), lambda sc, pg: (sc, pg, 0, 0))],
        out_shape=[jax.ShapeDtypeStruct((t_len, D_MODEL), F32),
                   jax.ShapeDtypeStruct((nsc, A_HEADS, A_HEAD_DIM, A_HEAD_DIM), F32),
                   jax.ShapeDtypeStruct((nsc, A_HEADS, SUPER, SUPER), F32)],
        scratch_shapes=[pltpu.VMEM((A_HEADS, A_HEAD_DIM, A_HEAD_DIM), F32)],
        aliases={10: 0}, sem=("arbitrary", "arbitrary"), rider=rider,
        args=(qkv, qkv, qkv, h, h, alog, dtb, nw, mf, mb, ycat))


def _gdn_bwd(qkv, h, alog, dtb, nw, s_in, t_in, dycat, dh, *, name, rider=None):
    t_len = qkv.shape[0]
    nsc = t_len // SUPER

    def core(ins, outs, scr):
        (q_ref, k_ref, v_ref, z_ref, ba_ref, al_ref, dt_ref, nw_ref, sin_ref, t_ref, dy_ref, mf_ref, mb_ref,
         _) = ins
        dz_ref, dqkv_ref, dba_ref, dal_ref, ddt_ref, dnw_ref = outs
        ds_scr, = scr
        i, pg = pl.program_id(0), pl.program_id(1)

        @pl.when((i == 0) & (pg == 0))
        def _():
            dal_ref[...] = jnp.zeros_like(dal_ref)
            ddt_ref[...] = jnp.zeros_like(ddt_ref)
            dnw_ref[...] = jnp.zeros_like(dnw_ref)

        @pl.when(pg == 0)
        def _():
            dba_ref[...] = jnp.zeros_like(dba_ref)

        heads = [pg * G_HEADS + hh for hh in range(G_HEADS)]

        @pl.when(i == 0)
        def _():
            for hd in heads:
                ds_scr[hd] = jnp.zeros((A_HEAD_DIM, A_HEAD_DIM), F32)

        per_head = lambda ref: jnp.stack([ref[:, hh * LANE:(hh + 1) * LANE] for hh in range(G_HEADS)])
        d_states = jnp.stack([ds_scr[hd] for hd in heads])
        head_ids = jnp.stack(heads)
        t_known, mf, mb = t_ref[0], mf_ref[...], mb_ref[...]

        def fn(s, q, k, v, z, ba, alog, dtb, nw):
            gates = _gdn_gates(ba, alog, dtb, mb)
            one = lambda s, q, k, v, z, t, h: _gdn_block(s, q, k, v, z, gates, nw, h, t, mf)[:2]
            return jax.vmap(one)(s, q, k, v, z, t_known, head_ids)

        _, vjp = jax.vjp(fn, sin_ref[0], per_head(q_ref), per_head(k_ref), per_head(v_ref), per_head(z_ref),
                         ba_ref[...], al_ref[...], dt_ref[...], nw_ref[...])
        ds, dq, dk, dv, dz, dba, dal, ddt, dnw = vjp((per_head(dy_ref), d_states))
        for hh, hd in enumerate(heads):
            cols = slice(hh * LANE, (hh + 1) * LANE)
            ds_scr[hd] = ds[hh]
            dqkv_ref[0, :, cols] = dq[hh]
            dqkv_ref[1, :, cols] = dk[hh]
            dqkv_ref[2, :, cols] = dv[hh]
            dz_ref[:, cols] = dz[hh]
        dba_ref[...] += dba
        dal_ref[...] += dal
        ddt_ref[...] += ddt
        dnw_ref[...] += dnw

    rev = lambda i: nsc - 1 - i
    gw = G_HEADS * LANE
    blk = lambda off: pl.BlockSpec((SUPER, gw), lambda i, pg: (rev(i), off // G_HEADS + pg))
    row = pl.BlockSpec((1, LANE), lambda i, pg: (0, 0))
    ba_blk = lambda off: pl.BlockSpec((SUPER, LANE), lambda i, pg: (rev(i), off))
    mf, mb = _gdn_masks()
    whole = lambda a: pl.BlockSpec(a.shape, lambda i, pg: (0, 0, 0))
    return _pcall(
        core, name=name, grid=(nsc, A_HEADS // G_HEADS),
        in_specs=[blk(0), blk(4), blk(8), blk(L_ZA // LANE), ba_blk(L_BA // LANE), row, row, row,
                  pl.BlockSpec((1, G_HEADS, A_HEAD_DIM, A_HEAD_DIM), lambda i, pg: (rev(i), pg, 0, 0)),
                  pl.BlockSpec((1, G_HEADS, SUPER, SUPER), lambda i, pg: (rev(i), pg, 0, 0)),
                  blk(0), whole(mf), whole(mb), _ANY],
        out_specs=[blk(L_ZA // LANE),
                   pl.BlockSpec((3, SUPER, gw), lambda i, pg: (0, rev(i), pg)),
                   ba_blk(0), row, row, row],
        out_shape=[jax.ShapeDtypeStruct((t_len, L_MAIN), F32), jax.ShapeDtypeStruct((3, t_len, A_WIDTH), F32),
                   jax.ShapeDtypeStruct((t_len, LANE), F32)] + [jax.ShapeDtypeStruct((1, LANE), F32)] * 3,
        scratch_shapes=[pltpu.VMEM((A_HEADS, A_HEAD_DIM, A_HEAD_DIM), F32)],
        aliases={13: 0}, sem=("arbitrary", "arbitrary"), rider=rider,
        args=(qkv, qkv, qkv, h, h, alog, dtb, nw, s_in, t_in, dycat, mf, mb, dh))


Q_BLOCKS = 4
Q_ROWS = Q_BLOCKS * BLOCK


def _swa_block(q, kp, kc, vp, vc, z, sinks, first):
    rows = B_GROUP * BLOCK
    ri = lax.broadcasted_iota(jnp.int32, (rows, 2 * BLOCK), 0)
    si = lax.broadcasted_iota(jnp.int32, (rows, 2 * BLOCK), 1)
    dist = (ri & (BLOCK - 1)) + BLOCK - si
    bias = jnp.where((dist >= 0) & (dist < WINDOW), 0.0, -jnp.inf)
    no_prev = jnp.where(first & (si[:1] < BLOCK), -jnp.inf, 0.0)
    dist_f = dist.astype(F32)
    head_of_row = lax.broadcasted_iota(jnp.int32, (rows, 1), 0) >> 7
    keys = jnp.concatenate([kp, kc], axis=0)
    vals = jnp.concatenate([vp, vc], axis=0)

    def item(b, j):
        cs = slice(j * B_HEAD_DIM, (j + 1) * B_HEAD_DIM)
        rs = slice(b * BLOCK, (b + 1) * BLOCK)
        heads = range(j * B_GROUP, (j + 1) * B_GROUP)
        qs = jnp.concatenate([q[rs, hq * B_HEAD_DIM:(hq + 1) * B_HEAD_DIM] for hq in heads], axis=0) * (
            B_HEAD_DIM ** -0.5)
        kk = keys[b * BLOCK:(b + 2) * BLOCK, cs]
        vv = vals[b * BLOCK:(b + 2) * BLOCK, cs]
        sink = jnp.concatenate([jnp.broadcast_to(sinks[:, hq:hq + 1], (BLOCK, 1)) for hq in heads], axis=0)
        slope = sum(jnp.where(head_of_row == gi, 2.0 ** (-8.0 * (hq + 1) / B_Q_HEADS), 0.0)
                    for gi, hq in enumerate(heads))
        return qs, kk, vv, sink, slope, (no_prev if b == 0 else jnp.zeros_like(no_prev))

    def attend(qs, kk, vv, sink, slope, hide):
        sc = _mm_nt(qs, kk) - slope * dist_f + (bias + hide)
        m = lax.stop_gradient(jnp.maximum(jnp.max(sc, axis=-1, keepdims=True), sink))
        p = jnp.exp(sc - m)
        inv = 1.0 / (jnp.sum(p, axis=-1, keepdims=True) + jnp.exp(sink - m))
        return _mm(p * inv, vv)

    items = [(b, j) for b in range(Q_BLOCKS) for j in range(B_KV_HEADS)]
    o = jax.vmap(attend)(*[_stack(t) for t in zip(*[item(b, j) for b, j in items])])
    rows_out = [jnp.concatenate([o[b * B_KV_HEADS + j, gi * BLOCK:(gi + 1) * BLOCK]
                                 for j in range(B_KV_HEADS) for gi in range(B_GROUP)], axis=1)
                for b in range(Q_BLOCKS)]
    return jnp.concatenate(rows_out, axis=0) * _silu(z)


def _swa_specs(idx):
    wide = lambda off: pl.BlockSpec((Q_ROWS, B_WIDTH), lambda n: (idx(n), off))
    cur = lambda off: pl.BlockSpec((Q_ROWS, LANE), lambda n: (idx(n), off))
    prev = lambda off: pl.BlockSpec((BLOCK, LANE), lambda n: (jnp.maximum(idx(n) * Q_BLOCKS - 1, 0), off))
    return [wide(L_QB // B_WIDTH), prev(L_KB // LANE), cur(L_KB // LANE), prev(L_VB // LANE), cur(L_VB // LANE),
            wide(L_ZB // B_WIDTH), pl.BlockSpec((1, LANE), lambda n: (0, 0))]


def _swa_fwd(h, sinks, *, name, rider=None):
    t_len = h.shape[0]
    nb = t_len // Q_ROWS

    def core(ins, outs, _):
        q_ref, kp_ref, kc_ref, vp_ref, vc_ref, z_ref, s_ref = ins
        outs[0][...] = _swa_block(q_ref[...], kp_ref[...], kc_ref[...], vp_ref[...], vc_ref[...], z_ref[...],
                                  s_ref[...], pl.program_id(0) == 0)

    res = _pcall(core, name=name, grid=(nb,), in_specs=_swa_specs(lambda n: n),
                 out_specs=[pl.BlockSpec((Q_ROWS, B_WIDTH), lambda n: (n, 1))],
                 out_shape=[jax.ShapeDtypeStruct((t_len, D_MODEL), F32)], sem=("parallel",), rider=rider,
                 args=(h, h, h, h, h, h, sinks))
    return res if rider else res[0]


def _swa_bwd(h, sinks, dycat, *, name, rider=None):
    t_len = h.shape[0]
    nb = t_len // Q_ROWS
    last = slice(Q_ROWS - BLOCK, Q_ROWS)

    def core(ins, outs, scr):
        q_ref, kp_ref, kc_ref, vp_ref, vc_ref, z_ref, s_ref, dy_ref = ins
        dh_ref, dsk_ref = outs
        ck_scr, cv_scr = scr
        i = pl.program_id(0)
        n = nb - 1 - i

        @pl.when(i == 0)
        def _():
            ck_scr[...] = jnp.zeros_like(ck_scr)
            cv_scr[...] = jnp.zeros_like(cv_scr)
            dsk_ref[...] = jnp.zeros_like(dsk_ref)

        fn = functools.partial(_swa_block, first=(n == 0))
        _, vjp = jax.vjp(fn, q_ref[...], kp_ref[...], kc_ref[...], vp_ref[...], vc_ref[...], z_ref[...], s_ref[...])
        dq, dkp, dkc, dvp, dvc, dz, dsk = vjp(dy_ref[...])
        dh_ref[:, L_QB:L_QB + B_WIDTH] = dq
        dh_ref[:, L_ZB:L_ZB + B_WIDTH] = dz
        dh_ref[:, L_ZA:L_ZA + A_WIDTH] = jnp.zeros((Q_ROWS, A_WIDTH), F32)
        dh_ref[:, L_KB:L_KB + LANE] = dkc
        dh_ref[:, L_VB:L_VB + LANE] = dvc
        dh_ref[last, L_KB:L_KB + LANE] += ck_scr[...]
        dh_ref[last, L_VB:L_VB + LANE] += cv_scr[...]
        ck_scr[...] = dkp
        cv_scr[...] = dvp
        dsk_ref[...] += dsk

    rev = lambda i: nb - 1 - i
    return _pcall(
        core, name=name, grid=(nb,),
        in_specs=_swa_specs(rev) + [pl.BlockSpec((Q_ROWS, B_WIDTH), lambda i: (rev(i), 1))],
        out_specs=[pl.BlockSpec((Q_ROWS, L_SWA), lambda i: (rev(i), 0)), pl.BlockSpec((1, LANE), lambda i: (0, 0))],
        out_shape=[jax.ShapeDtypeStruct((t_len, L_MAIN), F32), jax.ShapeDtypeStruct((1, LANE), F32)],
        scratch_shapes=[pltpu.VMEM((BLOCK, LANE), F32), pltpu.VMEM((BLOCK, LANE), F32)],
        sem=("arbitrary",), rider=rider, args=(h, h, h, h, h, h, sinks, dycat))


def _out_ln_fwd(ycat, w_out, x, ln_g, ln_b, *, name, tm=256):
    t_len = x.shape[0]

    def body(y_ref, w_ref, x_ref, g_ref, b_ref, r_ref, o_ref):
        r = DEEPNORM_ALPHA * x_ref[...] + _mm(y_ref[...], w_ref[...])
        r_ref[...] = r
        mu = jnp.mean(r, axis=-1, keepdims=True)
        d = r - mu
        var = jnp.mean(d * d, axis=-1, keepdims=True)
        o_ref[...] = d * lax.rsqrt(var + LN_EPS) * g_ref[...] + b_ref[...]

    tile = pl.BlockSpec((tm, D_MODEL), lambda i: (i, 0))
    vec = pl.BlockSpec((1, D_MODEL), lambda i: (0, 0))
    return pl.pallas_call(
        body, name=name, grid=(t_len // tm,),
        in_specs=[tile, pl.BlockSpec((D_MODEL, D_MODEL), lambda i: (0, 0)), tile, vec, vec],
        out_specs=[tile, tile],
        out_shape=[jax.ShapeDtypeStruct((t_len, D_MODEL), F32)] * 2,
        compiler_params=_cparams(("parallel",)),
    )(ycat, w_out, x, ln_g, ln_b)


def _ln_bwd(dxn, r, ln_g, *, name, tm=256):
    t_len = r.shape[0]

    def body(dx_ref, r_ref, g_ref, dr_ref, dg_ref, db_ref):
        @pl.when(pl.program_id(0) == 0)
        def _():
            dg_ref[...] = jnp.zeros_like(dg_ref)
            db_ref[...] = jnp.zeros_like(db_ref)

        rr = r_ref[...]
        dx = dx_ref[...]
        mu = jnp.mean(rr, axis=-1, keepdims=True)
        d = rr - mu
        rstd = lax.rsqrt(jnp.mean(d * d, axis=-1, keepdims=True) + LN_EPS)
        xh = d * rstd
        dxh = dx * g_ref[...]
        dr_ref[...] = rstd * (dxh - jnp.mean(dxh, axis=-1, keepdims=True)
                              - xh * jnp.mean(dxh * xh, axis=-1, keepdims=True))
        dg_ref[...] += jnp.sum(dx * xh, axis=0, keepdims=True)
        db_ref[...] += jnp.sum(dx, axis=0, keepdims=True)

    tile = pl.BlockSpec((tm, D_MODEL), lambda i: (i, 0))
    vec = pl.BlockSpec((1, D_MODEL), lambda i: (0, 0))
    return pl.pallas_call(
        body, name=name, grid=(t_len // tm,),
        in_specs=[tile, tile, vec], out_specs=[tile, vec, vec],
        out_shape=[jax.ShapeDtypeStruct((t_len, D_MODEL), F32), jax.ShapeDtypeStruct((1, D_MODEL), F32),
                   jax.ShapeDtypeStruct((1, D_MODEL), F32)],
        compiler_params=_cparams(("arbitrary",)),
    )(dxn, r, ln_g)


def _loss_head(y, target, *, name, tm=256):
    t_len = y.shape[0]

    def body(y_ref, t_ref, d_ref, l_ref):
        @pl.when(pl.program_id(0) == 0)
        def _():
            l_ref[...] = jnp.zeros_like(l_ref)

        e = y_ref[...] - t_ref[...]
        d_ref[...] = e * (1.0 / D_MODEL)
        l_ref[...] += jnp.sum(e * e, axis=0, keepdims=True)

    tile = pl.BlockSpec((tm, D_MODEL), lambda i: (i, 0))
    vec = pl.BlockSpec((1, D_MODEL), lambda i: (0, 0))
    return pl.pallas_call(
        body, name=name, grid=(t_len // tm,), in_specs=[tile, tile], out_specs=[tile, vec],
        out_shape=[jax.ShapeDtypeStruct((t_len, D_MODEL), F32), jax.ShapeDtypeStruct((1, D_MODEL), F32)],
        compiler_params=_cparams(("arbitrary",)),
    )(y, target)


def _pad_row(v):
    return jnp.zeros((1, LANE), F32).at[0, :v.shape[0]].set(v)


def _to_layout(w_full):
    s = lambda a, b: w_full[..., a:b]
    pad = jnp.zeros(w_full.shape[:-1] + (LANE - 2 * A_HEADS,), w_full.dtype)
    return jnp.concatenate([s(2056, 2568), s(2824, 3336), s(1536, 2048), s(2568, 2696), s(2696, 2824), s(0, 1536),
                            s(2048, 2056), pad], axis=-1)


def _from_layout(g_main, g_ba):
    s = lambda a, b: g_main[..., a:b]
    return jnp.concatenate([s(L_QKV, L_QKV + 1536), s(L_ZA, L_ZA + 512), g_ba[..., :2 * A_HEADS],
                            s(L_QB, L_QB + 512), s(L_KB, L_KB + 128), s(L_VB, L_VB + 128), s(L_ZB, L_ZB + 512)],
                           axis=-1)


_REGIONS = ((0, 1536, L_QKV), (1536, 2048, L_ZA), (2048, 2056, L_BA), (2056, 2568, L_QB), (2568, 2696, L_KB),
            (2696, 2824, L_VB), (2824, 3336, L_ZB))


def _shard_pieces(regions):
    for a, b, off in regions:
        for d in range(N_DEV):
            lo, hi = max(a, d * SHARD_COLS), min(b, (d + 1) * SHARD_COLS)
            if lo < hi:
                yield d, lo - d * SHARD_COLS, hi - d * SHARD_COLS, off + lo - a


def _as_list(r):
    return list(r) if isinstance(r, (list, tuple)) else [r]


def _gathered(shard):
    return jax.ShapeDtypeStruct((N_DEV,) + shard.shape, shard.dtype)


def _full_w_in(g_in, name):
    by_offset = sorted(_shard_pieces(_REGIONS), key=lambda p: p[3])
    tr = 256

    def body(g_ref, o_ref):
        pieces = [g_ref[d, :, lo:hi] for d, lo, hi, _ in by_offset]
        pad = jnp.zeros((tr, L_COLS - L_BA - 2 * A_HEADS), g_ref.dtype)
        o_ref[...] = jnp.concatenate(pieces + [pad], axis=1)

    return pl.pallas_call(
        body, name=name, grid=(D_MODEL // tr,),
        in_specs=[pl.BlockSpec((N_DEV, tr, SHARD_COLS), lambda i: (0, i, 0))],
        out_specs=pl.BlockSpec((tr, L_COLS), lambda i: (i, 0)),
        out_shape=jax.ShapeDtypeStruct((D_MODEL, L_COLS), g_in.dtype),
        compiler_params=_cparams(("parallel",)),
    )(g_in)


def _full_conv(g_conv):
    return jnp.pad(g_conv.transpose(1, 0, 2).reshape(CONV_K, 3 * A_WIDTH), ((0, 8 - CONV_K), (0, 0)))


def _forward(x, weights, shards, small):
    a_log, dt_bias, norm_w, sinks, ln_g, ln_b = small
    tm = min(512, x.shape[0])
    saved, weights = [], [list(w) for w in weights]
    whole = lambda arrs: _Direct([(a, False, j, ()) for j, a in enumerate(arrs)], [_gathered(a) for a in arrs])
    for l in range(DEPTH):
        rider = whole(shards[l][1:]) if weights[l][1] is None else None
        h, *got = _as_list(_matmul(x, weights[l][0], form="nn", tm=tm, tn=1152, tk=D_MODEL, name=f"in_proj_{l}",
                                   rider=rider))
        if rider:
            weights[l][1:] = [got[0].reshape(D_MODEL, D_MODEL), _full_conv(got[1])]
        w_in_l, w_out_l, conv_l = weights[l]
        qkv = _prep_fwd(h, conv_l, name=f"prep_fwd_{l}")
        al, dt, nw, sk = _pad_row(a_log[l]), _pad_row(dt_bias[l]), norm_w[l][None, :], _pad_row(sinks[l])
        ahead = l + 1 < DEPTH and weights[l + 1][0] is None
        rider = whole(shards[l + 1][1:]) if ahead else None
        ycat, *got = _as_list(_swa_fwd(h, sk, name=f"swa_fwd_{l}", rider=rider))
        if ahead:
            weights[l + 1][1:] = [got[0].reshape(D_MODEL, D_MODEL), _full_conv(got[1])]
        rider = whole(shards[l + 1][:1]) if ahead else None
        ycat, s_in, t_in, *got = _gdn_fwd(qkv, h, al, dt, nw, ycat, name=f"gdn_fwd_{l}", rider=rider)
        if ahead:
            weights[l + 1][0] = _full_w_in(got[0], f"w_in_columns_{l + 1}")
        r, xn = _out_ln_fwd(ycat, w_out_l, x, ln_g[l][None, :], ln_b[l][None, :], name=f"out_ln_{l}")
        saved.append((x, h, qkv, s_in, t_in, ycat, r, al, dt, nw, sk))
        x = xn
    return x, saved, weights


def _w_in_blocks(g_main, g_ba, name):
    rows, tr = g_main.shape[0], 256
    pieces = list(_shard_pieces(_REGIONS))

    def body(m_ref, b_ref, o_ref):
        src = lambda off: (b_ref, off - L_BA) if off >= L_BA else (m_ref, off)
        blocks = [[] for _ in range(N_DEV)]
        for d, lo, hi, off in pieces:
            a, o = src(off)
            blocks[d].append(a[:, o:o + hi - lo])
        for d in range(N_DEV):
            o_ref[d] = jnp.concatenate(blocks[d], axis=1).astype(BF16)

    return pl.pallas_call(
        body, name=name, grid=(rows // tr,),
        in_specs=[pl.BlockSpec((tr, L_MAIN), lambda i: (i, 0)), pl.BlockSpec((tr, LANE), lambda i: (i, 0))],
        out_specs=pl.BlockSpec((N_DEV, tr, SHARD_COLS), lambda i: (0, i, 0)),
        out_shape=jax.ShapeDtypeStruct((N_DEV, rows, SHARD_COLS), BF16),
        compiler_params=_cparams(("parallel",)),
    )(g_main, g_ba)


def _small_blocks(g):
    c_conv = g["conv_w"].reshape(CONV_K, N_DEV, CONV_SHARD_COLS).transpose(1, 0, 2)
    c_small = [jnp.broadcast_to(g[n][None], (N_DEV,) + g[n].shape) for n, _ in SMALL_SIZES]
    return _pack_small(c_conv, c_small)


def _contributions(g):
    c_out = g["w_out"].astype(BF16).reshape(N_DEV, OUT_SHARD_ROWS, D_MODEL)
    return _w_in_blocks(*g["w_in_parts"], name="w_in_grad_blocks_above"), c_out, _small_blocks(g)


def _backward_layer(l, dx, saved_l, weights_l, ln_g_l, above=None):
    x_in, h, qkv, s_in, t_in, ycat, r, al, dt, nw, sk = saved_l
    w_in_l, w_out_l, conv_l = weights_l
    tm = min(512, x_in.shape[0])
    dr, d_lng, d_lnb = _ln_bwd(dx, r, ln_g_l[None, :], name=f"ln_bwd_{l}")
    dycat = _matmul(dr, w_out_l, form="nt", tm=tm, tn=D_MODEL, tk=D_MODEL, name=f"out_proj_dx_{l}")
    d_wout = _matmul(ycat, dr, form="tn", tm=512, tn=D_MODEL, tk=tm, name=f"out_proj_dw_{l}")
    rider, p_in, p_out, p_small = None, None, None, None
    recv = lambda c: jax.ShapeDtypeStruct((DEPTH,) + c.shape, c.dtype)
    if above:
        c_out = d_wout.astype(BF16).reshape(N_DEV, OUT_SHARD_ROWS, D_MODEL)
        rider = _Direct([(above[1], True, 0, (l + 1,)), (above[2], True, 1, (l + 1,)), (c_out, True, 0, (l,))],
                        [recv(above[1]), recv(above[2])])
    dh, d_sk, *got = _swa_bwd(h, sk, dycat, name=f"swa_bwd_{l}", rider=rider)
    if above:
        p_out, p_small = got
        rider = _Direct([(above[0], True, 0, (l + 1,))], [recv(above[0])])
    dh, dqkv_n, dba, d_al, d_dt, d_nw, *got = _gdn_bwd(qkv, h, al, dt, nw, s_in, t_in, dycat, dh,
                                                       name=f"gdn_bwd_{l}", rider=rider)
    dh, d_conv = _prep_bwd(h, conv_l, dqkv_n, dh, name=f"prep_bwd_{l}")
    d_win_ba = _matmul(x_in, dba, form="tn", tm=D_MODEL, tn=LANE, tk=tm, name=f"in_proj_dw_ba_{l}")
    grads = dict(w_out=d_wout, conv_w=d_conv[:CONV_K], a_log=d_al[0, :A_HEADS], dt_bias=d_dt[0, :A_HEADS],
                 norm_w=d_nw[0], sinks=d_sk[0, :B_Q_HEADS], ln_g=d_lng[0], ln_b=d_lnb[0])
    dw = functools.partial(_matmul, x_in, dh, form="tn", tm=512, tn=L_MAIN // 2, tk=tm)
    if not above:
        grads["w_in_parts"] = (dw(name=f"in_proj_dw_{l}"), d_win_ba)
    else:
        p_in, = got
        half = D_MODEL // 2
        top = dw(name=f"in_proj_dw_top_{l}", a_cols=(0, half))
        blocks = _w_in_blocks(top, d_win_ba[:half], name=f"w_in_grad_blocks_top_{l}")
        rider = _Direct([(blocks, True, 0, (l,), (pl.ds(0, half),))], [p_in])
        bottom, p_in = dw(name=f"in_proj_dw_bottom_{l}", a_cols=(half, half), rider=rider)
        blocks = _w_in_blocks(bottom, d_win_ba[half:], name=f"w_in_grad_blocks_bottom_{l}")
        rider = _Direct([(blocks, True, 0, (l,), (pl.ds(half, half),)),
                         (_small_blocks(grads), True, 1, (l,))], [p_in, p_small])
    dx, *got = _as_list(_matmul(dh, w_in_l, form="nt", tm=tm, tn=D_MODEL, tk=L_MAIN // 2, name=f"in_proj_dx_{l}",
                                add=dr, add_scale=DEEPNORM_ALPHA, extra=(dba, w_in_l, L_BA // LANE), rider=rider))
    bufs = (got[0], p_out, got[1]) if above else None
    return dx, grads, bufs


def _all_gather(shards, *, name):
    n_arr = len(shards)

    def body(*refs):
        x_refs, out_refs = refs[:n_arr], refs[n_arr:2 * n_arr]
        send_sems, recv_sems, local_sems = refs[2 * n_arr:]
        x, y, c = _me()
        me, sibling = (x, y, c), (x, y, 1 - c)
        chips = [(1 - x, y), (x, 1 - y), (1 - x, 1 - y)]

        def copy(a, k, block, to, src=None):
            dst = out_refs[a].at[_flat_id(block)]
            return _remote(dst if src is None else src, dst, send_sems.at[a, k], recv_sems.at[a, k], to)

        mine = [pltpu.make_async_copy(x_refs[a], out_refs[a].at[_flat_id(me)], local_sems.at[a])
                for a in range(n_arr)]
        for cp in mine:
            cp.start()
        first = []
        for a in range(n_arr):
            first.append(copy(a, 0, me, sibling, src=x_refs[a]))
            first += [copy(a, 1 + j, me, (*chip, c), src=x_refs[a]) for j, chip in enumerate(chips)]
        for cp in first:
            cp.start()
        passed = []
        for j, chip in enumerate(chips):
            for a in range(n_arr):
                copy(a, 1 + j, (*chip, c), me).wait_recv()
                fwd = copy(a, 4 + j, (*chip, c), sibling)
                fwd.start()
                passed.append(fwd)
        for a in range(n_arr):
            copy(a, 0, sibling, me).wait_recv()
            for j, chip in enumerate(chips):
                copy(a, 4 + j, (*chip, 1 - c), me).wait_recv()
        for cp in first + passed:
            cp.wait_send()
        for cp in mine:
            cp.wait()

    return pl.pallas_call(
        body, name=name, in_specs=[_ANY] * n_arr, out_specs=[_ANY] * n_arr,
        out_shape=[jax.ShapeDtypeStruct((N_DEV,) + s.shape, s.dtype) for s in shards],
        scratch_shapes=[pltpu.SemaphoreType.DMA((n_arr, N_DEV - 1)), pltpu.SemaphoreType.DMA((n_arr, N_DEV - 1)),
                        pltpu.SemaphoreType.DMA((n_arr,))],
    )(*shards)


def _adamw(parts, w, m, v, *, tr, name):
    depth, rows, cols = w.shape
    c1 = 1.0 - ADAM_B1 ** ADAM_STEP
    c2 = 1.0 - ADAM_B2 ** ADAM_STEP

    def body(g_ref, w_ref, m_ref, v_ref, go_ref, d_ref, mo_ref, vo_ref):
        g = g_ref[0, 0].astype(F32)
        for s in range(1, N_DEV):
            g = g + g_ref[0, s].astype(F32)
        m_new = ADAM_B1 * m_ref[0] + (1.0 - ADAM_B1) * g
        v_new = ADAM_B2 * v_ref[0] + (1.0 - ADAM_B2) * (g * g)
        go_ref[0] = g
        mo_ref[0] = m_new
        vo_ref[0] = v_new
        d_ref[0] = -ADAM_LR * ((m_new / c1) / (jnp.sqrt(v_new / c2) + ADAM_EPS) + ADAM_WD * w_ref[0])

    tile = pl.BlockSpec((1, tr, cols), lambda l, i: (l, i, 0))
    return pl.pallas_call(
        body, name=name, grid=(depth, rows // tr),
        in_specs=[pl.BlockSpec((1, N_DEV, tr, cols), lambda l, i: (l, 0, i, 0)), tile, tile, tile],
        out_specs=[tile] * 4, out_shape=[jax.ShapeDtypeStruct(w.shape, F32)] * 4,
        compiler_params=_cparams(("parallel", "parallel")),
    )(parts, w, m, v)


def _pack_small(conv, small):
    lead = conv.shape[:-2]
    flat = jnp.concatenate([conv.reshape(lead + (CS_CONV,))] + list(small), axis=-1)
    pad = CS_ROWS * LANE - flat.shape[-1]
    flat = jnp.concatenate([flat, jnp.zeros(lead + (pad,), F32)], axis=-1)
    return flat.reshape(lead + (CS_ROWS, LANE))


def _unpack_small(p):
    flat = p.reshape(DEPTH, CS_ROWS * LANE)
    conv = flat[:, :CS_CONV].reshape(DEPTH, CONV_K, CONV_SHARD_COLS)
    small, off = [], CS_CONV
    for _, n in SMALL_SIZES:
        small.append(flat[:, off:off + n])
        off += n
    return conv, small


def kernel(x, w_in, conv_w, a_log, dt_bias, norm_w, sinks, w_out, ln_g, ln_b, loss_target, m_w_in, m_conv_w, m_a_log, m_dt_bias, m_norm_w, m_sinks, m_w_out, m_ln_g, m_ln_b, v_w_in, v_conv_w, v_a_log, v_dt_bias, v_norm_w, v_sinks, v_w_out, v_ln_g, v_ln_b):
    small = [a_log, dt_bias, norm_w, sinks, ln_g, ln_b]
    shards = [[w_in[l].astype(BF16), w_out[l].astype(BF16), conv_w[l]] for l in range(DEPTH)]
    g_in0, = _all_gather(shards[0][:1], name="weights_all_gather_0")
    weights = [[_full_w_in(g_in0, "w_in_columns_0"), None, None]] + [[None, None, None]] * (DEPTH - 1)

    y, saved, weights = _forward(x[0], weights, shards, small)
    dx, loss_lanes = _loss_head(y, loss_target[0], name="loss_head")
    loss = lax.psum(0.5 * jnp.sum(loss_lanes) * (1.0 / D_MODEL), ("x", "y", "c"))
    dx, g1, _ = _backward_layer(1, dx, saved[1], weights[1], ln_g[1])
    dx, _, (p_in, p_out, p_small) = _backward_layer(0, dx, saved[0], weights[0], ln_g[0], above=_contributions(g1))

    o_in = _adamw(p_in, w_in, m_w_in, v_w_in, tr=256, name="adamw_w_in")
    o_out = _adamw(p_out, w_out, m_w_out, v_w_out, tr=OUT_SHARD_ROWS, name="adamw_w_out")
    o_small = _adamw(p_small, _pack_small(conv_w, small),
                     _pack_small(m_conv_w, [m_a_log, m_dt_bias, m_norm_w, m_sinks, m_ln_g, m_ln_b]),
                     _pack_small(v_conv_w, [v_a_log, v_dt_bias, v_norm_w, v_sinks, v_ln_g, v_ln_b]),
                     tr=CS_ROWS, name="adamw_small")
    outs = []
    for k in range(4):
        cv, sm = _unpack_small(o_small[k])
        outs += [o_in[k], cv, sm[0], sm[1], sm[2], sm[3], o_out[k], sm[4], sm[5]]
    return (loss, dx[None], *outs)
```

```python
import functools

import jax
import jax.numpy as jnp
from jax import lax
from jax.experimental import pallas as pl
from jax.experimental.pallas import tpu as pltpu

F32 = jnp.float32
BF16 = jnp.bfloat16
MM_DTYPE = BF16

N_DEV = 8
D_MODEL = 1024
DEPTH = 2
A_HEADS = 4
A_HEAD_DIM = 128
A_WIDTH = 512
CONV_K = 4
CHUNK = 64
SUPER = 256
NEWTON_STEPS = 1
B_Q_HEADS = 8
B_KV_HEADS = 2
B_HEAD_DIM = 64
B_GROUP = 4
B_WIDTH = 512
WINDOW = 128
BLOCK = 128
IN_COLS = 3336
SHARD_COLS = IN_COLS // N_DEV
OUT_SHARD_ROWS = D_MODEL // N_DEV
CONV_SHARD_COLS = 3 * A_WIDTH // N_DEV
DEEPNORM_ALPHA = (2 * DEPTH) ** 0.25
LN_EPS = 1e-5
RMS_EPS = 1e-6
L2_EPS = 1e-6
ADAM_LR, ADAM_B1, ADAM_B2, ADAM_EPS, ADAM_WD, ADAM_STEP = 0.001, 0.9, 0.999, 1e-08, 0.01, 10

LANE = 128
L_QB, L_ZB, L_KB, L_VB, L_ZA, L_BA, L_QKV = 0, 512, 1024, 1152, 1280, 1792, 1920
L_SWA = 1280
L_GATE = 640
L_COLS = 3456
SMALL_SIZES = (("a_log", 4), ("dt_bias", 4), ("norm_w", 128), ("sinks", 8), ("ln_g", 1024), ("ln_b", 1024))
CS_CONV = CONV_K * CONV_SHARD_COLS
CS_ROWS = 24
VMEM_LIMIT = 48 * 1024 * 1024


def _cparams(sem=None):
    return pltpu.CompilerParams(dimension_semantics=sem, vmem_limit_bytes=VMEM_LIMIT)


def _mm(a, b):
    return jnp.dot(a.astype(MM_DTYPE), b.astype(MM_DTYPE), preferred_element_type=F32)


def _mm_nt(a, b):
    return lax.dot_general(a.astype(MM_DTYPE), b.astype(MM_DTYPE), (((1,), (1,)), ((), ())),
                           preferred_element_type=F32)


def _mm_tn(a, b):
    return lax.dot_general(a.astype(MM_DTYPE), b.astype(MM_DTYPE), (((0,), (0,)), ((), ())),
                           preferred_element_type=F32)


def _split(a):
    hi = a.astype(BF16)
    return hi, (a - hi.astype(F32)).astype(BF16)


def _hp(a2, b2):
    d = lambda p, q: jnp.dot(p, q, preferred_element_type=F32)
    return d(a2[0], b2[0]) + (d(a2[0], b2[1]) + d(a2[1], b2[0]))


def _silu(x):
    return x * jax.nn.sigmoid(x)


@jax.custom_vjp
def _stack(parts):
    return jnp.stack(parts)


_stack.defvjp(lambda parts: (jnp.stack(parts), None), lambda _, g: (tuple(g[i] for i in range(g.shape[0])),))


def _softplus(x):
    return jnp.maximum(x, 0.0) + jnp.log1p(jnp.exp(-jnp.abs(x)))


_ANY = pl.BlockSpec(memory_space=pl.ANY)


def _me():
    return lax.axis_index("x"), lax.axis_index("y"), lax.axis_index("c")


def _flat_id(pos):
    return 4 * pos[0] + 2 * pos[1] + pos[2]


def _remote(src, dst, send_sem, recv_sem, to):
    return pltpu.make_async_remote_copy(src_ref=src, dst_ref=dst, send_sem=send_sem, recv_sem=recv_sem,
                                        device_id=to, device_id_type=pl.DeviceIdType.MESH)


class _Direct:
    def __init__(self, items, bufs):
        self.items, self.bufs = list(items), list(bufs)
        self.n_src, self.n_buf = len(self.items), len(self.bufs)
        self.old = [j for j, b in enumerate(self.bufs) if not isinstance(b, jax.ShapeDtypeStruct)]
        self.args = [it[0] for it in self.items] + [self.bufs[j] for j in self.old]
        self.out_shape = [jax.ShapeDtypeStruct(b.shape, b.dtype) for b in self.bufs]
        self.scratch = [pltpu.SemaphoreType.DMA((self.n_src, N_DEV - 1)),
                        pltpu.SemaphoreType.DMA((self.n_src, N_DEV - 1)), pltpu.SemaphoreType.DMA((self.n_src,))]

    def aliases(self, in_base, out_base):
        return {in_base + self.n_src + pos: out_base + j for pos, j in enumerate(self.old)}

    def copies(self, in_refs, out_refs, sems):
        send_sems, recv_sems, local_sems = sems
        x, y, c = _me()
        me = _flat_id((x, y, c))
        peers = [(x ^ ((rel >> 2) & 1), y ^ ((rel >> 1) & 1), c ^ (rel & 1)) for rel in range(1, N_DEV)]
        local, sends, recvs = [], [], []
        for a, (_, per_dest, j, prefix, *rest) in enumerate(self.items):
            src = lambda d: in_refs[a].at[d] if per_dest else in_refs[a]
            dst = lambda s: out_refs[j].at[tuple(prefix) + (s,) + tuple(rest[0] if rest else ())]
            local.append(pltpu.make_async_copy(src(me), dst(me), local_sems.at[a]))
            for k, peer in enumerate(peers):
                pid = _flat_id(peer)
                sends.append(_remote(src(pid), dst(me), send_sems.at[a, k], recv_sems.at[a, k], peer))
                recvs.append(_remote(src(pid), dst(pid), send_sems.at[a, k], recv_sems.at[a, k], peer))
        return local, sends, recvs

    def start(self, in_refs, out_refs, sems):
        local, sends, _ = self.copies(in_refs, out_refs, sems)
        for cp in local + sends:
            cp.start()

    def wait(self, in_refs, out_refs, sems):
        local, sends, recvs = self.copies(in_refs, out_refs, sems)
        for cp in recvs:
            cp.wait_recv()
        for cp in sends:
            cp.wait_send()
        for cp in local:
            cp.wait()


def _pcall(core, *, name, grid, in_specs, out_specs, out_shape, args, sem, scratch_shapes=(), aliases=None,
           rider=None):
    n_in, n_out, n_scr = len(in_specs), len(out_specs), len(scratch_shapes)
    n_rin, n_rout = (len(rider.args), rider.n_buf) if rider else (0, 0)

    def body(*refs):
        ins, outs = refs[:n_in], refs[n_in + n_rin:n_in + n_rin + n_out]
        scr = refs[n_in + n_rin + n_out + n_rout:n_in + n_rin + n_out + n_rout + n_scr]
        if rider:
            r_refs = (refs[n_in:n_in + rider.n_src], refs[n_in + n_rin + n_out:n_in + n_rin + n_out + n_rout],
                      refs[n_in + n_rin + n_out + n_rout + n_scr:])
            ids = [pl.program_id(d) for d in range(len(grid))]
            first = functools.reduce(lambda p, q: p & q, [i == 0 for i in ids])
            last = functools.reduce(lambda p, q: p & q, [i == g - 1 for i, g in zip(ids, grid)])
            pl.when(first)(lambda: rider.start(*r_refs))
        core(ins, outs, scr)
        if rider:
            pl.when(last)(lambda: rider.wait(*r_refs))

    aliases = dict(aliases or {})
    if rider:
        sem = ("arbitrary",) * len(grid)
        aliases.update(rider.aliases(n_in, n_out))
    return pl.pallas_call(
        body, name=name, grid=grid, in_specs=list(in_specs) + [_ANY] * n_rin,
        out_specs=list(out_specs) + [_ANY] * n_rout,
        out_shape=list(out_shape) + (rider.out_shape if rider else []),
        scratch_shapes=list(scratch_shapes) + (rider.scratch if rider else []),
        input_output_aliases=aliases, compiler_params=_cparams(sem),
    )(*args, *(rider.args if rider else []))


def _exchange(direct, *, name):
    n_in = len(direct.args)

    def body(*refs):
        r_refs = refs[:direct.n_src], refs[n_in:n_in + direct.n_buf], refs[n_in + direct.n_buf:]
        direct.start(*r_refs)
        direct.wait(*r_refs)

    return pl.pallas_call(
        body, name=name, in_specs=[_ANY] * n_in, out_specs=[_ANY] * direct.n_buf, out_shape=direct.out_shape,
        input_output_aliases=direct.aliases(0, 0), scratch_shapes=direct.scratch,
    )(*direct.args)


def _matmul(a, b, *, form, tm, tn, tk, name, add=None, add_scale=1.0, extra=None, rider=None, a_cols=None):
    if form == "nn":
        (m, kk), n = a.shape, b.shape[1]
        a_spec = pl.BlockSpec((tm, tk), lambda i, j, k: (i, k))
        b_spec = pl.BlockSpec((tk, tn), lambda i, j, k: (k, j))
        dn = (((1,), (0,)), ((), ()))
    elif form == "nt":
        (m, kk), n = a.shape, b.shape[0]
        a_spec = pl.BlockSpec((tm, tk), lambda i, j, k: (i, k))
        b_spec = pl.BlockSpec((tn, tk), lambda i, j, k: (j, k))
        dn = (((1,), (1,)), ((), ()))
    else:
        kk, n = a.shape[0], b.shape[1]
        m0, m = a_cols or (0, a.shape[1])
        assert m0 % tm == 0
        a_spec = pl.BlockSpec((tk, tm), lambda i, j, k: (k, i + m0 // tm))
        b_spec = pl.BlockSpec((tk, tn), lambda i, j, k: (k, j))
        dn = (((0,), (0,)), ((), ()))
    assert m % tm == 0 and n % tn == 0 and kk % tk == 0, (name, m, n, kk)
    has_add, has_extra = add is not None, extra is not None

    def core(ins, outs, _):
        a_ref, b_ref = ins[:2]
        o_ref = outs[0]
        rest = ins[2:]
        k = pl.program_id(2)
        p = lax.dot_general(a_ref[...].astype(MM_DTYPE), b_ref[...].astype(MM_DTYPE), dn,
                            preferred_element_type=F32)

        @pl.when(k == 0)
        def _():
            first = p
            pos = 0
            if has_extra:
                first = first + _mm_nt(rest[0][...], rest[1][...])
                pos = 2
            if has_add:
                first = first + add_scale * rest[pos][...]
            o_ref[...] = first

        @pl.when(k > 0)
        def _():
            o_ref[...] += p

    in_specs = [a_spec, b_spec]
    args = [a, b]
    if has_extra:
        a2, b2, idx = extra
        in_specs += [pl.BlockSpec((tm, LANE), lambda i, j, k: (i, 0)),
                     pl.BlockSpec((tn, LANE), lambda i, j, k: (j, idx))]
        args += [a2, b2]
    if has_add:
        in_specs.append(pl.BlockSpec((tm, tn), lambda i, j, k: (i, j)))
        args.append(add)
    res = _pcall(core, name=name, grid=(m // tm, n // tn, kk // tk), in_specs=in_specs,
                 out_specs=[pl.BlockSpec((tm, tn), lambda i, j, k: (i, j))],
                 out_shape=[jax.ShapeDtypeStruct((m, n), F32)], args=args,
                 sem=("parallel", "parallel", "arbitrary"), rider=rider)
    return res if rider else res[0]


ZERO_TAIL = 8


def _with_tail(x):
    return jnp.concatenate([x, jnp.zeros((ZERO_TAIL,) + x.shape[1:], x.dtype)], axis=0)


def _shift_down(x, k):
    return pltpu.roll(x, k, 0)


def _shift_up(x, k):
    return pltpu.roll(x, x.shape[0] - k, 0)


def _conv_slab(x, w):
    return w[3:4] * x + w[2:3] * _shift_down(x, 1) + w[1:2] * _shift_down(x, 2) + w[0:1] * _shift_down(x, 3)


def _prep_fwd(h, conv_w, *, name):
    t_len = h.shape[0]

    def body(x_ref, w_ref, o_ref):
        s = pl.program_id(0)
        y = _silu(_conv_slab(_with_tail(x_ref[...]), w_ref[...])[:t_len])
        rs = lax.rsqrt(jnp.sum(y * y, axis=-1, keepdims=True) + L2_EPS)
        scale = jnp.where(s < A_HEADS, A_HEAD_DIM ** -0.5, 1.0)
        o_ref[...] = jnp.where(s < 2 * A_HEADS, y * rs * scale, y)

    return pl.pallas_call(
        body, name=name, grid=(12,),
        in_specs=[pl.BlockSpec((t_len, LANE), lambda s: (0, L_QKV // LANE + s)),
                  pl.BlockSpec((8, LANE), lambda s: (0, s))],
        out_specs=pl.BlockSpec((t_len, LANE), lambda s: (0, s)),
        out_shape=jax.ShapeDtypeStruct((t_len, 3 * A_WIDTH), F32),
        compiler_params=_cparams(("parallel",)),
    )(h, conv_w)


def _prep_bwd(h, conv_w, d_out, dh, *, name):
    t_len = h.shape[0]

    def body(x_ref, w_ref, g_ref, dh_in, dx_ref, dw_ref):
        del dh_in
        s = pl.program_id(0)
        x = _with_tail(x_ref[...])
        g = _with_tail(g_ref[0])
        w = w_ref[...]
        c = _conv_slab(x, w)
        sg = jax.nn.sigmoid(c)
        y = c * sg
        rs = lax.rsqrt(jnp.sum(y * y, axis=-1, keepdims=True) + L2_EPS)
        scale = jnp.where(s < A_HEADS, A_HEAD_DIM ** -0.5, 1.0)
        dy_n = scale * (rs * g - y * (rs * rs * rs) * jnp.sum(g * y, axis=-1, keepdims=True))
        dy = jnp.where(s < 2 * A_HEADS, dy_n, g)
        dc = dy * (sg * (1.0 + c * (1.0 - sg)))
        dx = w[3:4] * dc + w[2:3] * _shift_up(dc, 1) + w[1:2] * _shift_up(dc, 2) + w[0:1] * _shift_up(dc, 3)
        dx_ref[...] = dx[:t_len]
        dws = [jnp.sum(dc * _shift_down(x, 3 - j), axis=0, keepdims=True) if j < 3
               else jnp.sum(dc * x, axis=0, keepdims=True) for j in range(CONV_K)]
        dw_ref[...] = jnp.concatenate(dws + [jnp.zeros((8 - CONV_K, LANE), F32)], axis=0)

    slab = pl.BlockSpec((t_len, LANE), lambda s: (0, L_QKV // LANE + s))
    return pl.pallas_call(
        body, name=name, grid=(12,),
        in_specs=[slab, pl.BlockSpec((8, LANE), lambda s: (0, s)),
                  pl.BlockSpec((1, t_len, LANE), lambda s: (s // A_HEADS, 0, s % A_HEADS)), _ANY],
        out_specs=[slab, pl.BlockSpec((8, LANE), lambda s: (0, s))],
        out_shape=[jax.ShapeDtypeStruct((t_len, L_COLS), F32), jax.ShapeDtypeStruct((8, 3 * A_WIDTH), F32)],
        input_output_aliases={3: 0},
        compiler_params=_cparams(("parallel",)),
    )(h, conv_w, d_out, dh)


N_LEVELS = 5
MF_TRIL, MF_STRIL, MF_DIAG8, MF_LOW16, MF_EYE = 0, 1, 2, 3, 3 + N_LEVELS
MB_CUM, MB_CUM_T, MB_TOT = 0, 1, 2


def _gdn_masks():
    r = lax.broadcasted_iota(jnp.int32, (SUPER, SUPER), 0)
    c = lax.broadcasted_iota(jnp.int32, (SUPER, SUPER), 1)
    same = lambda shift: (r >> shift) == (c >> shift)
    ninf = lambda m: jnp.where(m, 0.0, -jnp.inf).astype(F32)
    one = lambda m: m.astype(F32)
    mf = jnp.stack([ninf(r >= c), ninf(r > c), one(same(3))]
                   + [one(same(4 + lv) & jnp.logical_not(same(3 + lv))) for lv in range(N_LEVELS)] + [one(r == c)])
    mb = jnp.stack([one(r >= c), one(r <= c), jnp.ones((SUPER, SUPER), F32)]).astype(BF16)
    return mf, mb


def _tri_inv_impl(a, mf):
    d = lambda p, q: jnp.dot(p.astype(BF16), q.astype(BF16), preferred_element_type=F32)
    dd = lambda p, q: jnp.dot(p, q, preferred_element_type=F32)
    eye = mf[MF_EYE]
    a0 = a * mf[MF_DIAG8]
    a2 = d(a0, a0)
    a4 = d(a2, a2)
    t = d(d(eye - a0, eye + a2), eye + a4)
    for level in range(N_LEVELS):
        t = t - d(d(t, a * mf[MF_LOW16 + level]), t)
    a_hi, a_lo = _split(a)
    for _ in range(NEWTON_STEPS):
        t_hi, t_lo = _split(t)
        resid = (eye - t) - (dd(a_hi, t_hi) + (dd(a_hi, t_lo) + dd(a_lo, t_hi)))
        r_hi, r_lo = _split(resid)
        t = t + (dd(t_hi, r_hi) + dd(t_hi, r_lo))
    return t


@jax.custom_vjp
def _wy_apply(a, rhs, t):
    return _mm(t, rhs)


def _wy_apply_fwd(a, rhs, t):
    x = _mm(t, rhs)
    return x, (t, x)


def _wy_apply_bwd(res, dx):
    t, x = res
    d_rhs = _mm_tn(t, dx)
    return -_mm_nt(d_rhs, x), d_rhs, jnp.zeros_like(t)


_wy_apply.defvjp(_wy_apply_fwd, _wy_apply_bwd)


@functools.partial(jax.custom_vjp, nondiff_argnums=(1,))
def _lane_roll(x, shift):
    return pltpu.roll(x, shift % LANE, 1)


_lane_roll.defvjp(lambda x, shift: (_lane_roll(x, shift), None), lambda shift, _, g: (_lane_roll(g, -shift),))


def _mask_times_lanes(x, mask):
    lane = lax.broadcasted_iota(jnp.int32, (1, LANE), 1)
    x = jnp.where(lane < A_HEADS, x, 0.0)
    x1 = x.astype(BF16).astype(F32)
    x2 = (x - x1).astype(BF16).astype(F32)
    x3 = (x - x1 - x2).astype(BF16).astype(F32)
    pieces = x1 + pltpu.roll(x2, A_HEADS, 1) + pltpu.roll(x3, 2 * A_HEADS, 1)
    res = jnp.dot(mask, pieces.astype(BF16), preferred_element_type=F32)
    return res + pltpu.roll(res, LANE - A_HEADS, 1) + pltpu.roll(res, LANE - 2 * A_HEADS, 1)


@jax.custom_vjp
def _chunk_sums(g, mb):
    return _mask_times_lanes(g, mb[MB_CUM]), _mask_times_lanes(g, mb[MB_TOT])


def _chunk_sums_fwd(g, mb):
    return _chunk_sums(g, mb), mb


def _chunk_sums_bwd(mb, d):
    lane = lax.broadcasted_iota(jnp.int32, (1, LANE), 1)
    dg = _mask_times_lanes(d[0], mb[MB_CUM_T]) + _mask_times_lanes(d[1], mb[MB_TOT])
    return jnp.where(lane < A_HEADS, dg, 0.0), jnp.zeros_like(mb)


_chunk_sums.defvjp(_chunk_sums_fwd, _chunk_sums_bwd)


def _gdn_gates(ba, alog, dtb, mb):
    beta = jax.nn.sigmoid(ba)
    g = -jnp.exp(alog) * _softplus(_lane_roll(ba, -A_HEADS) + dtb)
    gc, gl = _chunk_sums(g, mb)
    return beta, gc, gl, gc.T


def _gdn_block(s, q, k, v, z, gates, nw, h, t_known, mf):
    n = q.shape[0]
    beta_all, gc_all, gl_all, gct_all = gates
    lane = lax.broadcasted_iota(jnp.int32, (1, LANE), 1)
    sub = lax.broadcasted_iota(jnp.int32, (LANE, 1), 0)
    col = lambda x: jnp.sum(jnp.where(lane == h, x, 0.0), axis=1, keepdims=True)
    wide = lambda c: jnp.broadcast_to(c, (n, LANE))
    gc, gl = col(gc_all), col(gl_all)
    gc_row = jnp.sum(jnp.where(sub == h, gct_all, 0.0), axis=0, keepdims=True)
    beta_w, eg_w = wide(col(beta_all)), wide(jnp.exp(gc))
    diff = gc - gc_row
    decay = jnp.exp(diff + mf[MF_TRIL])
    kb = k * beta_w
    a_mat = _mm_nt(kb, k) * jnp.exp(diff + mf[MF_STRIL])
    rhs = jnp.concatenate([v * beta_w, kb * eg_w], axis=1)
    if t_known is None:
        t_mat = _tri_inv_impl(a_mat, mf)
        uw = _mm(t_mat, rhs)
    else:
        t_mat = t_known
        uw = _wy_apply(a_mat, rhs, t_known)
    u, w = uw[:, :LANE], uw[:, LANE:]
    qk = _mm_nt(q, k) * decay
    q_dec = q * eg_w
    k_dec = k * wide(jnp.exp(gl - gc))
    v_new = u - _mm(w, s)
    o = _mm(q_dec, s) + _mm(qk, v_new)
    s = s * jnp.exp(gl[0:1]) + _mm_tn(k_dec, v_new)
    o = o * lax.rsqrt(jnp.mean(o * o, axis=-1, keepdims=True) + RMS_EPS) * nw
    return o * _silu(z), s, t_mat


def _gdn_fwd(qkv, h, alog, dtb, nw, ycat, *, name, rider=None):
    t_len = qkv.shape[0]
    nsc = t_len // SUPER

    def core(ins, outs, scr):
        q_ref, k_ref, v_ref, gate_ref, al_ref, dt_ref, nw_ref, mf_ref, mb_ref, _ = ins
        y_ref, sin_ref, t_ref = outs
        s_scr, = scr

        @pl.when(pl.program_id(0) == 0)
        def _():
            s_scr[...] = jnp.zeros_like(s_scr)

        per_head = lambda ref: jnp.stack([ref[:, hh * LANE:(hh + 1) * LANE] for hh in range(A_HEADS)])
        states = s_scr[...]
        gates = _gdn_gates(gate_ref[:, A_WIDTH:], al_ref[...], dt_ref[...], mb_ref[...])
        fn = jax.vmap(_gdn_block, in_axes=(0, 0, 0, 0, 0, None, None, 0, None, None))
        y, s_new, t_mat = fn(states, per_head(q_ref), per_head(k_ref), per_head(v_ref), per_head(gate_ref),
                             gates, nw_ref[...], jnp.arange(A_HEADS), None, mf_ref[...])
        sin_ref[0] = states
        t_ref[0] = t_mat
        s_scr[...] = s_new
        for hh in range(A_HEADS):
            y_ref[:, hh * LANE:(hh + 1) * LANE] = y[hh]

    blk = lambda j: pl.BlockSpec((SUPER, A_WIDTH), lambda sc: (sc, j))
    row = pl.BlockSpec((1, LANE), lambda sc: (0, 0))
    mf, mb = _gdn_masks()
    whole = lambda a: pl.BlockSpec(a.shape, lambda sc: (0, 0, 0))
    return _pcall(
        core, name=name, grid=(nsc,),
        in_specs=[blk(0), blk(1), blk(2), pl.BlockSpec((SUPER, L_GATE), lambda sc: (sc, L_ZA // L_GATE)),
                  row, row, row, whole(mf), whole(mb), _ANY],
        out_specs=[blk(0),
                   pl.BlockSpec((1, A_HEADS, A_HEAD_DIM, A_HEAD_DIM), lambda sc: (sc, 0, 0, 0)),
                   pl.BlockSpec((1, A_HEADS, SUPER, SUPER), lambda sc: (sc, 0, 0, 0))],
        out_shape=[jax.ShapeDtypeStruct((t_len, D_MODEL), F32),
                   jax.ShapeDtypeStruct((nsc, A_HEADS, A_HEAD_DIM, A_HEAD_DIM), F32),
                   jax.ShapeDtypeStruct((nsc, A_HEADS, SUPER, SUPER), F32)],
        scratch_shapes=[pltpu.VMEM((A_HEADS, A_HEAD_DIM, A_HEAD_DIM), F32)],
        aliases={9: 0}, sem=("arbitrary",), rider=rider,
        args=(qkv, qkv, qkv, h, alog, dtb, nw, mf, mb, ycat))


def _gdn_bwd(qkv, h, alog, dtb, nw, s_in, t_in, dycat, dh, *, name, rider=None):
    t_len = qkv.shape[0]
    nsc = t_len // SUPER

    def core(ins, outs, scr):
        q_ref, k_ref, v_ref, gate_ref, al_ref, dt_ref, nw_ref, sin_ref, t_ref, dy_ref, mf_ref, mb_ref, _ = ins
        dgate_ref, dqkv_ref, dal_ref, ddt_ref, dnw_ref = outs
        ds_scr, = scr

        @pl.when(pl.program_id(0) == 0)
        def _():
            ds_scr[...] = jnp.zeros_like(ds_scr)
            dal_ref[...] = jnp.zeros_like(dal_ref)
            ddt_ref[...] = jnp.zeros_like(ddt_ref)
            dnw_ref[...] = jnp.zeros_like(dnw_ref)

        per_head = lambda ref: jnp.stack([ref[:, hh * LANE:(hh + 1) * LANE] for hh in range(A_HEADS)])
        head_ids = jnp.arange(A_HEADS)
        t_known, mf, mb = t_ref[0], mf_ref[...], mb_ref[...]

        def fn(s, q, k, v, z, ba, alog, dtb, nw):
            gates = _gdn_gates(ba, alog, dtb, mb)
            one = lambda s, q, k, v, z, t, h: _gdn_block(s, q, k, v, z, gates, nw, h, t, mf)[:2]
            return jax.vmap(one)(s, q, k, v, z, t_known, head_ids)

        _, vjp = jax.vjp(fn, sin_ref[0], per_head(q_ref), per_head(k_ref), per_head(v_ref), per_head(gate_ref),
                         gate_ref[:, A_WIDTH:], al_ref[...], dt_ref[...], nw_ref[...])
        ds, dq, dk, dv, dz, dba, dal, ddt, dnw = vjp((per_head(dy_ref), ds_scr[...]))
        ds_scr[...] = ds
        for hh in range(A_HEADS):
            cols = slice(hh * LANE, (hh + 1) * LANE)
            dqkv_ref[0, :, cols] = dq[hh]
            dqkv_ref[1, :, cols] = dk[hh]
            dqkv_ref[2, :, cols] = dv[hh]
            dgate_ref[:, cols] = dz[hh]
        dgate_ref[:, A_WIDTH:] = dba
        dal_ref[...] += dal
        ddt_ref[...] += ddt
        dnw_ref[...] += dnw

    rev = lambda i: nsc - 1 - i
    blk = lambda j: pl.BlockSpec((SUPER, A_WIDTH), lambda i: (rev(i), j))
    gate = pl.BlockSpec((SUPER, L_GATE), lambda i: (rev(i), L_ZA // L_GATE))
    row = pl.BlockSpec((1, LANE), lambda i: (0, 0))
    mf, mb = _gdn_masks()
    whole = lambda a: pl.BlockSpec(a.shape, lambda i: (0, 0, 0))
    return _pcall(
        core, name=name, grid=(nsc,),
        in_specs=[blk(0), blk(1), blk(2), gate, row, row, row,
                  pl.BlockSpec((1, A_HEADS, A_HEAD_DIM, A_HEAD_DIM), lambda i: (rev(i), 0, 0, 0)),
                  pl.BlockSpec((1, A_HEADS, SUPER, SUPER), lambda i: (rev(i), 0, 0, 0)),
                  blk(0), whole(mf), whole(mb), _ANY],
        out_specs=[gate, pl.BlockSpec((3, SUPER, A_WIDTH), lambda i: (0, rev(i), 0)), row, row, row],
        out_shape=[jax.ShapeDtypeStruct((t_len, L_COLS), F32), jax.ShapeDtypeStruct((3, t_len, A_WIDTH), F32)]
        + [jax.ShapeDtypeStruct((1, LANE), F32)] * 3,
        scratch_shapes=[pltpu.VMEM((A_HEADS, A_HEAD_DIM, A_HEAD_DIM), F32)],
        aliases={12: 0}, sem=("arbitrary",), rider=rider,
        args=(qkv, qkv, qkv, h, alog, dtb, nw, s_in, t_in, dycat, mf, mb, dh))


Q_BLOCKS = 4
Q_ROWS = Q_BLOCKS * BLOCK


def _swa_block(q, kp, kc, vp, vc, z, sinks, first):
    rows = B_GROUP * BLOCK
    ri = lax.broadcasted_iota(jnp.int32, (rows, 2 * BLOCK), 0)
    si = lax.broadcasted_iota(jnp.int32, (rows, 2 * BLOCK), 1)
    dist = (ri & (BLOCK - 1)) + BLOCK - si
    bias = jnp.where((dist >= 0) & (dist < WINDOW), 0.0, -jnp.inf)
    no_prev = jnp.where(first & (si[:1] < BLOCK), -jnp.inf, 0.0)
    dist_f = dist.astype(F32)
    head_of_row = lax.broadcasted_iota(jnp.int32, (rows, 1), 0) >> 7
    keys = jnp.concatenate([kp, kc], axis=0)
    vals = jnp.concatenate([vp, vc], axis=0)

    def item(b, j):
        cs = slice(j * B_HEAD_DIM, (j + 1) * B_HEAD_DIM)
        rs = slice(b * BLOCK, (b + 1) * BLOCK)
        heads = range(j * B_GROUP, (j + 1) * B_GROUP)
        qs = jnp.concatenate([q[rs, hq * B_HEAD_DIM:(hq + 1) * B_HEAD_DIM] for hq in heads], axis=0) * (
            B_HEAD_DIM ** -0.5)
        kk = keys[b * BLOCK:(b + 2) * BLOCK, cs]
        vv = vals[b * BLOCK:(b + 2) * BLOCK, cs]
        sink = jnp.concatenate([jnp.broadcast_to(sinks[:, hq:hq + 1], (BLOCK, 1)) for hq in heads], axis=0)
        slope = sum(jnp.where(head_of_row == gi, 2.0 ** (-8.0 * (hq + 1) / B_Q_HEADS), 0.0)
                    for gi, hq in enumerate(heads))
        return qs, kk, vv, sink, slope, (no_prev if b == 0 else jnp.zeros_like(no_prev))

    def attend(qs, kk, vv, sink, slope, hide):
        sc = _mm_nt(qs, kk) - slope * dist_f + (bias + hide)
        m = lax.stop_gradient(jnp.maximum(jnp.max(sc, axis=-1, keepdims=True), sink))
        p = jnp.exp(sc - m)
        inv = 1.0 / (jnp.sum(p, axis=-1, keepdims=True) + jnp.exp(sink - m))
        return _mm(p * inv, vv)

    items = [(b, j) for b in range(Q_BLOCKS) for j in range(B_KV_HEADS)]
    o = jax.vmap(attend)(*[_stack(t) for t in zip(*[item(b, j) for b, j in items])])
    rows_out = [jnp.concatenate([o[b * B_KV_HEADS + j, gi * BLOCK:(gi + 1) * BLOCK]
                                 for j in range(B_KV_HEADS) for gi in range(B_GROUP)], axis=1)
                for b in range(Q_BLOCKS)]
    return jnp.concatenate(rows_out, axis=0) * _silu(z)


def _swa_specs(idx):
    wide = lambda off: pl.BlockSpec((Q_ROWS, B_WIDTH), lambda n: (idx(n), off))
    cur = lambda off: pl.BlockSpec((Q_ROWS, LANE), lambda n: (idx(n), off))
    prev = lambda off: pl.BlockSpec((BLOCK, LANE), lambda n: (jnp.maximum(idx(n) * Q_BLOCKS - 1, 0), off))
    return [wide(L_QB // B_WIDTH), prev(L_KB // LANE), cur(L_KB // LANE), prev(L_VB // LANE), cur(L_VB // LANE),
            wide(L_ZB // B_WIDTH), pl.BlockSpec((1, LANE), lambda n: (0, 0))]


def _swa_fwd(h, sinks, *, name, rider=None):
    t_len = h.shape[0]
    nb = t_len // Q_ROWS

    def core(ins, outs, _):
        q_ref, kp_ref, kc_ref, vp_ref, vc_ref, z_ref, s_ref = ins
        outs[0][...] = _swa_block(q_ref[...], kp_ref[...], kc_ref[...], vp_ref[...], vc_ref[...], z_ref[...],
                                  s_ref[...], pl.program_id(0) == 0)

    res = _pcall(core, name=name, grid=(nb,), in_specs=_swa_specs(lambda n: n),
                 out_specs=[pl.BlockSpec((Q_ROWS, B_WIDTH), lambda n: (n, 1))],
                 out_shape=[jax.ShapeDtypeStruct((t_len, D_MODEL), F32)], sem=("parallel",), rider=rider,
                 args=(h, h, h, h, h, h, sinks))
    return res if rider else res[0]


def _swa_bwd(h, sinks, dycat, *, name, rider=None):
    t_len = h.shape[0]
    nb = t_len // Q_ROWS
    last = slice(Q_ROWS - BLOCK, Q_ROWS)

    def core(ins, outs, scr):
        q_ref, kp_ref, kc_ref, vp_ref, vc_ref, z_ref, s_ref, dy_ref = ins
        dh_ref, dsk_ref = outs
        ck_scr, cv_scr = scr
        i = pl.program_id(0)
        n = nb - 1 - i

        @pl.when(i == 0)
        def _():
            ck_scr[...] = jnp.zeros_like(ck_scr)
            cv_scr[...] = jnp.zeros_like(cv_scr)
            dsk_ref[...] = jnp.zeros_like(dsk_ref)

        fn = functools.partial(_swa_block, first=(n == 0))
        _, vjp = jax.vjp(fn, q_ref[...], kp_ref[...], kc_ref[...], vp_ref[...], vc_ref[...], z_ref[...], s_ref[...])
        dq, dkp, dkc, dvp, dvc, dz, dsk = vjp(dy_ref[...])
        dh_ref[:, L_QB:L_QB + B_WIDTH] = dq
        dh_ref[:, L_ZB:L_ZB + B_WIDTH] = dz
        dh_ref[:, L_KB:L_KB + LANE] = dkc
        dh_ref[:, L_VB:L_VB + LANE] = dvc
        dh_ref[last, L_KB:L_KB + LANE] += ck_scr[...]
        dh_ref[last, L_VB:L_VB + LANE] += cv_scr[...]
        ck_scr[...] = dkp
        cv_scr[...] = dvp
        dsk_ref[...] += dsk

    rev = lambda i: nb - 1 - i
    return _pcall(
        core, name=name, grid=(nb,),
        in_specs=_swa_specs(rev) + [pl.BlockSpec((Q_ROWS, B_WIDTH), lambda i: (rev(i), 1))],
        out_specs=[pl.BlockSpec((Q_ROWS, L_SWA), lambda i: (rev(i), 0)), pl.BlockSpec((1, LANE), lambda i: (0, 0))],
        out_shape=[jax.ShapeDtypeStruct((t_len, L_COLS), F32), jax.ShapeDtypeStruct((1, LANE), F32)],
        scratch_shapes=[pltpu.VMEM((BLOCK, LANE), F32), pltpu.VMEM((BLOCK, LANE), F32)],
        sem=("arbitrary",), rider=rider, args=(h, h, h, h, h, h, sinks, dycat))


def _out_ln_fwd(ycat, w_out, x, ln_g, ln_b, *, name, tm=256, last=False):
    t_len = x.shape[0]

    def body(y_ref, w_ref, x_ref, g_ref, b_ref, r_ref, *o_ref):
        r = DEEPNORM_ALPHA * x_ref[...] + _mm(y_ref[...], w_ref[...])
        r_ref[...] = r
        if not last:
            mu = jnp.mean(r, axis=-1, keepdims=True)
            d = r - mu
            var = jnp.mean(d * d, axis=-1, keepdims=True)
            o_ref[0][...] = d * lax.rsqrt(var + LN_EPS) * g_ref[...] + b_ref[...]

    tile = pl.BlockSpec((tm, D_MODEL), lambda i: (i, 0))
    vec = pl.BlockSpec((1, D_MODEL), lambda i: (0, 0))
    n_out = 1 if last else 2
    res = pl.pallas_call(
        body, name=name, grid=(t_len // tm,),
        in_specs=[tile, pl.BlockSpec((D_MODEL, D_MODEL), lambda i: (0, 0)), tile, vec, vec],
        out_specs=[tile] * n_out,
        out_shape=[jax.ShapeDtypeStruct((t_len, D_MODEL), F32)] * n_out,
        compiler_params=_cparams(("parallel",)),
    )(ycat, w_out, x, ln_g, ln_b)
    return (res[0], None) if last else res


def _ln_bwd(dxn, r, ln_g, *, name, tm=256, loss=None):
    t_len = r.shape[0]

    def body(*refs):
        if loss:
            t_ref, r_ref, g_ref, b_ref, dr_ref, dg_ref, db_ref, l_ref = refs
        else:
            dx_ref, r_ref, g_ref, dr_ref, dg_ref, db_ref = refs

        @pl.when(pl.program_id(0) == 0)
        def _():
            dg_ref[...] = jnp.zeros_like(dg_ref)
            db_ref[...] = jnp.zeros_like(db_ref)
            if loss:
                l_ref[...] = jnp.zeros_like(l_ref)

        rr = r_ref[...]
        mu = jnp.mean(rr, axis=-1, keepdims=True)
        d = rr - mu
        rstd = lax.rsqrt(jnp.mean(d * d, axis=-1, keepdims=True) + LN_EPS)
        xh = d * rstd
        if loss:
            e = (xh * g_ref[...] + b_ref[...]) - t_ref[...]
            dx = e * (1.0 / D_MODEL)
            l_ref[...] += jnp.sum(e * e, axis=0, keepdims=True)
        else:
            dx = dx_ref[...]
        dxh = dx * g_ref[...]
        dr_ref[...] = rstd * (dxh - jnp.mean(dxh, axis=-1, keepdims=True)
                              - xh * jnp.mean(dxh * xh, axis=-1, keepdims=True))
        dg_ref[...] += jnp.sum(dx * xh, axis=0, keepdims=True)
        db_ref[...] += jnp.sum(dx, axis=0, keepdims=True)

    tile = pl.BlockSpec((tm, D_MODEL), lambda i: (i, 0))
    vec = pl.BlockSpec((1, D_MODEL), lambda i: (0, 0))
    vec_shape = jax.ShapeDtypeStruct((1, D_MODEL), F32)
    args = (loss[0], r, ln_g, loss[1]) if loss else (dxn, r, ln_g)
    return pl.pallas_call(
        body, name=name, grid=(t_len // tm,),
        in_specs=[tile, tile, vec] + ([vec] if loss else []), out_specs=[tile, vec, vec] + ([vec] if loss else []),
        out_shape=[jax.ShapeDtypeStruct((t_len, D_MODEL), F32), vec_shape, vec_shape] + ([vec_shape] if loss else []),
        compiler_params=_cparams(("arbitrary",)),
    )(*args)


def _pad_row(v):
    return jnp.zeros((1, LANE), F32).at[0, :v.shape[0]].set(v)


_REGIONS = ((0, 1536, L_QKV), (1536, 2048, L_ZA), (2048, 2056, L_BA), (2056, 2568, L_QB), (2568, 2696, L_KB),
            (2696, 2824, L_VB), (2824, 3336, L_ZB))


def _to_layout(w_full):
    out = jnp.zeros(w_full.shape[:-1] + (L_COLS,), w_full.dtype)
    for a, b, off in _REGIONS:
        out = out.at[..., off:off + b - a].set(w_full[..., a:b])
    return out


def _from_layout(g):
    return jnp.concatenate([g[..., off:off + b - a] for a, b, off in _REGIONS], axis=-1)


def _shard_pieces(regions):
    for a, b, off in regions:
        for d in range(N_DEV):
            lo, hi = max(a, d * SHARD_COLS), min(b, (d + 1) * SHARD_COLS)
            if lo < hi:
                yield d, lo - d * SHARD_COLS, hi - d * SHARD_COLS, off + lo - a


def _as_list(r):
    return list(r) if isinstance(r, (list, tuple)) else [r]


def _gathered(shard):
    return jax.ShapeDtypeStruct((N_DEV,) + shard.shape, shard.dtype)


def _full_w_in(g_in, name):
    by_offset = sorted(_shard_pieces(_REGIONS), key=lambda p: p[3])
    tr = 256

    def body(g_ref, o_ref):
        pieces, col = [], 0
        for d, lo, hi, off in by_offset + [(None, 0, 0, L_COLS)]:
            if off > col:
                pieces.append(jnp.zeros((tr, off - col), g_ref.dtype))
            if d is not None:
                pieces.append(g_ref[d, :, lo:hi])
            col = off + hi - lo
        o_ref[...] = jnp.concatenate(pieces, axis=1)

    return pl.pallas_call(
        body, name=name, grid=(D_MODEL // tr,),
        in_specs=[pl.BlockSpec((N_DEV, tr, SHARD_COLS), lambda i: (0, i, 0))],
        out_specs=pl.BlockSpec((tr, L_COLS), lambda i: (i, 0)),
        out_shape=jax.ShapeDtypeStruct((D_MODEL, L_COLS), g_in.dtype),
        compiler_params=_cparams(("parallel",)),
    )(g_in)


def _full_conv(g_conv):
    return jnp.pad(g_conv.transpose(1, 0, 2).reshape(CONV_K, 3 * A_WIDTH), ((0, 8 - CONV_K), (0, 0)))


def _forward(x, weights, shards, small):
    a_log, dt_bias, norm_w, sinks, ln_g, ln_b = small
    tm = min(512, x.shape[0])
    saved, weights = [], [list(w) for w in weights]
    whole = lambda arrs: _Direct([(a, False, j, ()) for j, a in enumerate(arrs)], [_gathered(a) for a in arrs])
    for l in range(DEPTH):
        rider = whole(shards[l][1:]) if weights[l][1] is None else None
        h, *got = _as_list(_matmul(x, weights[l][0], form="nn", tm=tm, tn=1152, tk=D_MODEL, name=f"in_proj_{l}",
                                   rider=rider))
        if rider:
            weights[l][1:] = [got[0].reshape(D_MODEL, D_MODEL), _full_conv(got[1])]
        w_in_l, w_out_l, conv_l = weights[l]
        qkv = _prep_fwd(h, conv_l, name=f"prep_fwd_{l}")
        al, dt, nw, sk = _pad_row(a_log[l]), _pad_row(dt_bias[l]), norm_w[l][None, :], _pad_row(sinks[l])
        ahead = l + 1 < DEPTH and weights[l + 1][0] is None
        rider = whole(shards[l + 1][1:]) if ahead else None
        ycat, *got = _as_list(_swa_fwd(h, sk, name=f"swa_fwd_{l}", rider=rider))
        if ahead:
            weights[l + 1][1:] = [got[0].reshape(D_MODEL, D_MODEL), _full_conv(got[1])]
        rider = whole(shards[l + 1][:1]) if ahead else None
        ycat, s_in, t_in, *got = _gdn_fwd(qkv, h, al, dt, nw, ycat, name=f"gdn_fwd_{l}", rider=rider)
        if ahead:
            weights[l + 1][0] = _full_w_in(got[0], f"w_in_columns_{l + 1}")
        r, xn = _out_ln_fwd(ycat, w_out_l, x, ln_g[l][None, :], ln_b[l][None, :], name=f"out_ln_{l}",
                            last=(l == DEPTH - 1))
        saved.append((x, h, qkv, s_in, t_in, ycat, r, al, dt, nw, sk))
        x = xn
    return x, saved, weights


def _w_in_blocks(g, name):
    rows, tr = g.shape[0], 256
    pieces = list(_shard_pieces(_REGIONS))

    def body(g_ref, o_ref):
        blocks = [[] for _ in range(N_DEV)]
        for d, lo, hi, off in pieces:
            blocks[d].append(g_ref[:, off:off + hi - lo])
        for d in range(N_DEV):
            o_ref[d] = jnp.concatenate(blocks[d], axis=1).astype(BF16)

    return pl.pallas_call(
        body, name=name, grid=(rows // tr,),
        in_specs=[pl.BlockSpec((tr, L_COLS), lambda i: (i, 0))],
        out_specs=pl.BlockSpec((N_DEV, tr, SHARD_COLS), lambda i: (0, i, 0)),
        out_shape=jax.ShapeDtypeStruct((N_DEV, rows, SHARD_COLS), BF16),
        compiler_params=_cparams(("parallel",)),
    )(g)


def _small_blocks(g):
    c_conv = g["conv_w"].reshape(CONV_K, N_DEV, CONV_SHARD_COLS).transpose(1, 0, 2)
    c_small = [jnp.broadcast_to(g[n][None], (N_DEV,) + g[n].shape) for n, _ in SMALL_SIZES]
    return _pack_small(c_conv, c_small)


def _contributions(g):
    c_out = g["w_out"].astype(BF16).reshape(N_DEV, OUT_SHARD_ROWS, D_MODEL)
    return _w_in_blocks(g["w_in_cols"], name="w_in_grad_blocks_above"), c_out, _small_blocks(g)


def _backward_layer(l, dx, saved_l, weights_l, ln_g_l, above=None, loss=None):
    x_in, h, qkv, s_in, t_in, ycat, r, al, dt, nw, sk = saved_l
    w_in_l, w_out_l, conv_l = weights_l
    tm = min(512, x_in.shape[0])
    dr, d_lng, d_lnb, *loss_lanes = _ln_bwd(dx, r, ln_g_l[None, :], name=f"ln_bwd_{l}", loss=loss)
    dycat = _matmul(dr, w_out_l, form="nt", tm=tm, tn=D_MODEL, tk=D_MODEL, name=f"out_proj_dx_{l}")
    d_wout = _matmul(ycat, dr, form="tn", tm=512, tn=D_MODEL, tk=tm, name=f"out_proj_dw_{l}")
    rider, p_in, p_out, p_small = None, None, None, None
    recv = lambda c: jax.ShapeDtypeStruct((DEPTH,) + c.shape, c.dtype)
    if above:
        c_out = d_wout.astype(BF16).reshape(N_DEV, OUT_SHARD_ROWS, D_MODEL)
        rider = _Direct([(above[1], True, 0, (l + 1,)), (above[2], True, 1, (l + 1,)), (c_out, True, 0, (l,))],
                        [recv(above[1]), recv(above[2])])
    dh, d_sk, *got = _swa_bwd(h, sk, dycat, name=f"swa_bwd_{l}", rider=rider)
    if above:
        p_out, p_small = got
        rider = _Direct([(above[0], True, 0, (l + 1,))], [recv(above[0])])
    dh, dqkv_n, d_al, d_dt, d_nw, *got = _gdn_bwd(qkv, h, al, dt, nw, s_in, t_in, dycat, dh,
                                                  name=f"gdn_bwd_{l}", rider=rider)
    dh, d_conv = _prep_bwd(h, conv_l, dqkv_n, dh, name=f"prep_bwd_{l}")
    grads = dict(w_out=d_wout, conv_w=d_conv[:CONV_K], a_log=d_al[0, :A_HEADS], dt_bias=d_dt[0, :A_HEADS],
                 norm_w=d_nw[0], sinks=d_sk[0, :B_Q_HEADS], ln_g=d_lng[0], ln_b=d_lnb[0])
    dw = functools.partial(_matmul, x_in, dh, form="tn", tm=512, tn=L_COLS // 3, tk=tm)
    if not above:
        grads["w_in_cols"] = dw(name=f"in_proj_dw_{l}")
    else:
        p_in, = got
        half = D_MODEL // 2
        top = dw(name=f"in_proj_dw_top_{l}", a_cols=(0, half))
        blocks = _w_in_blocks(top, name=f"w_in_grad_blocks_top_{l}")
        rider = _Direct([(blocks, True, 0, (l,), (pl.ds(0, half),))], [p_in])
        bottom, p_in = dw(name=f"in_proj_dw_bottom_{l}", a_cols=(half, half), rider=rider)
        blocks = _w_in_blocks(bottom, name=f"w_in_grad_blocks_bottom_{l}")
        rider = _Direct([(blocks, True, 0, (l,), (pl.ds(half, half),)),
                         (_small_blocks(grads), True, 1, (l,))], [p_in, p_small])
    dx, *got = _as_list(_matmul(dh, w_in_l, form="nt", tm=tm, tn=D_MODEL, tk=L_COLS // 3, name=f"in_proj_dx_{l}",
                                add=dr, add_scale=DEEPNORM_ALPHA, rider=rider))
    bufs = (got[0], p_out, got[1]) if above else None
    return dx, grads, bufs, (loss_lanes[0] if loss else None)


def _all_gather(shards, *, name):
    n_arr = len(shards)

    def body(*refs):
        x_refs, out_refs = refs[:n_arr], refs[n_arr:2 * n_arr]
        send_sems, recv_sems, local_sems = refs[2 * n_arr:]
        x, y, c = _me()
        me, sibling = (x, y, c), (x, y, 1 - c)
        chips = [(1 - x, y), (x, 1 - y), (1 - x, 1 - y)]

        def copy(a, k, block, to, src=None):
            dst = out_refs[a].at[_flat_id(block)]
            return _remote(dst if src is None else src, dst, send_sems.at[a, k], recv_sems.at[a, k], to)

        mine = [pltpu.make_async_copy(x_refs[a], out_refs[a].at[_flat_id(me)], local_sems.at[a])
                for a in range(n_arr)]
        for cp in mine:
            cp.start()
        first = []
        for a in range(n_arr):
            first.append(copy(a, 0, me, sibling, src=x_refs[a]))
            first += [copy(a, 1 + j, me, (*chip, c), src=x_refs[a]) for j, chip in enumerate(chips)]
        for cp in first:
            cp.start()
        passed = []
        for j, chip in enumerate(chips):
            for a in range(n_arr):
                copy(a, 1 + j, (*chip, c), me).wait_recv()
                fwd = copy(a, 4 + j, (*chip, c), sibling)
                fwd.start()
                passed.append(fwd)
        for a in range(n_arr):
            copy(a, 0, sibling, me).wait_recv()
            for j, chip in enumerate(chips):
                copy(a, 4 + j, (*chip, 1 - c), me).wait_recv()
        for cp in first + passed:
            cp.wait_send()
        for cp in mine:
            cp.wait()

    return pl.pallas_call(
        body, name=name, in_specs=[_ANY] * n_arr, out_specs=[_ANY] * n_arr,
        out_shape=[jax.ShapeDtypeStruct((N_DEV,) + s.shape, s.dtype) for s in shards],
        scratch_shapes=[pltpu.SemaphoreType.DMA((n_arr, N_DEV - 1)), pltpu.SemaphoreType.DMA((n_arr, N_DEV - 1)),
                        pltpu.SemaphoreType.DMA((n_arr,))],
    )(*shards)


def _adamw(parts, w, m, v, *, tr, name):
    depth, rows, cols = w.shape
    c1 = 1.0 - ADAM_B1 ** ADAM_STEP
    c2 = 1.0 - ADAM_B2 ** ADAM_STEP

    def body(g_ref, w_ref, m_ref, v_ref, go_ref, d_ref, mo_ref, vo_ref):
        g = g_ref[0, 0].astype(F32)
        for s in range(1, N_DEV):
            g = g + g_ref[0, s].astype(F32)
        m_new = ADAM_B1 * m_ref[0] + (1.0 - ADAM_B1) * g
        v_new = ADAM_B2 * v_ref[0] + (1.0 - ADAM_B2) * (g * g)
        go_ref[0] = g
        mo_ref[0] = m_new
        vo_ref[0] = v_new
        d_ref[0] = -ADAM_LR * ((m_new / c1) / (jnp.sqrt(v_new / c2) + ADAM_EPS) + ADAM_WD * w_ref[0])

    tile = pl.BlockSpec((1, tr, cols), lambda l, i: (l, i, 0))
    return pl.pallas_call(
        body, name=name, grid=(depth, rows // tr),
        in_specs=[pl.BlockSpec((1, N_DEV, tr, cols), lambda l, i: (l, 0, i, 0)), tile, tile, tile],
        out_specs=[tile] * 4, out_shape=[jax.ShapeDtypeStruct(w.shape, F32)] * 4,
        compiler_params=_cparams(("parallel", "parallel")),
    )(parts, w, m, v)


def _pack_small(conv, small):
    lead = conv.shape[:-2]
    flat = jnp.concatenate([conv.reshape(lead + (CS_CONV,))] + list(small), axis=-1)
    pad = CS_ROWS * LANE - flat.shape[-1]
    flat = jnp.concatenate([flat, jnp.zeros(lead + (pad,), F32)], axis=-1)
    return flat.reshape(lead + (CS_ROWS, LANE))


def _unpack_small(p):
    flat = p.reshape(DEPTH, CS_ROWS * LANE)
    conv = flat[:, :CS_CONV].reshape(DEPTH, CONV_K, CONV_SHARD_COLS)
    small, off = [], CS_CONV
    for _, n in SMALL_SIZES:
        small.append(flat[:, off:off + n])
        off += n
    return conv, small


def kernel(x, w_in, conv_w, a_log, dt_bias, norm_w, sinks, w_out, ln_g, ln_b, loss_target, m_w_in, m_conv_w, m_a_log, m_dt_bias, m_norm_w, m_sinks, m_w_out, m_ln_g, m_ln_b, v_w_in, v_conv_w, v_a_log, v_dt_bias, v_norm_w, v_sinks, v_w_out, v_ln_g, v_ln_b):
    small = [a_log, dt_bias, norm_w, sinks, ln_g, ln_b]
    shards = [[w_in[l].astype(BF16), w_out[l].astype(BF16), conv_w[l]] for l in range(DEPTH)]
    g_in0, = _all_gather(shards[0][:1], name="weights_all_gather_0")
    weights = [[_full_w_in(g_in0, "w_in_columns_0"), None, None]] + [[None, None, None]] * (DEPTH - 1)

    _, saved, weights = _forward(x[0], weights, shards, small)
    dx, g1, _, loss_lanes = _backward_layer(1, None, saved[1], weights[1], ln_g[1],
                                            loss=(loss_target[0], ln_b[1][None, :]))
    loss = lax.psum(0.5 * jnp.sum(loss_lanes) * (1.0 / D_MODEL), ("x", "y", "c"))
    dx, _, (p_in, p_out, p_small), _ = _backward_layer(0, dx, saved[0], weights[0], ln_g[0],
                                                       above=_contributions(g1))

    o_in = _adamw(p_in, w_in, m_w_in, v_w_in, tr=256, name="adamw_w_in")
    o_out = _adamw(p_out, w_out, m_w_out, v_w_out, tr=OUT_SHARD_ROWS, name="adamw_w_out")
    o_small = _adamw(p_small, _pack_small(conv_w, small),
                     _pack_small(m_conv_w, [m_a_log, m_dt_bias, m_norm_w, m_sinks, m_ln_g, m_ln_b]),
                     _pack_small(v_conv_w, [v_a_log, v_dt_bias, v_norm_w, v_sinks, v_ln_g, v_ln_b]),
                     tr=CS_ROWS, name="adamw_small")
    outs = []
    for k in range(4):
        cv, sm = _unpack_small(o_small[k])
        outs += [o_in[k], cv, sm[0], sm[1], sm[2], sm[3], o_out[k], sm[4], sm[5]]
    return (loss, dx[None], *outs)
```

```python
import functools

import jax
import jax.numpy as jnp
from jax import lax
from jax.experimental import pallas as pl
from jax.experimental.pallas import tpu as pltpu

F32 = jnp.float32
BF16 = jnp.bfloat16
MM_DTYPE = BF16

N_DEV = 8
D_MODEL = 1024
DEPTH = 2
A_HEADS = 4
A_HEAD_DIM = 128
A_WIDTH = 512
CONV_K = 4
CHUNK = 64
SUPER = 256
NEWTON_STEPS = 1
B_Q_HEADS = 8
B_KV_HEADS = 2
B_HEAD_DIM = 64
B_GROUP = 4
B_WIDTH = 512
WINDOW = 128
BLOCK = 128
IN_COLS = 3336
SHARD_COLS = IN_COLS // N_DEV
OUT_SHARD_ROWS = D_MODEL // N_DEV
CONV_SHARD_COLS = 3 * A_WIDTH // N_DEV
DEEPNORM_ALPHA = (2 * DEPTH) ** 0.25
LN_EPS = 1e-5
RMS_EPS = 1e-6
L2_EPS = 1e-6
ADAM_LR, ADAM_B1, ADAM_B2, ADAM_EPS, ADAM_WD, ADAM_STEP = 0.001, 0.9, 0.999, 1e-08, 0.01, 10

LANE = 128
L_QB, L_ZB, L_KB, L_VB, L_ZA, L_BA, L_QKV = 0, 512, 1024, 1152, 1280, 1792, 1920
L_SWA = 1280
L_GATE = 640
L_COLS = 3456
SMALL_SIZES = (("a_log", 4), ("dt_bias", 4), ("norm_w", 128), ("sinks", 8), ("ln_g", 1024), ("ln_b", 1024))
CS_CONV = CONV_K * CONV_SHARD_COLS
CS_ROWS = 24
VMEM_LIMIT = 48 * 1024 * 1024


def _cparams(sem=None):
    return pltpu.CompilerParams(dimension_semantics=sem, vmem_limit_bytes=VMEM_LIMIT)


def _mm(a, b):
    return jnp.dot(a.astype(MM_DTYPE), b.astype(MM_DTYPE), preferred_element_type=F32)


def _mm_nt(a, b):
    return lax.dot_general(a.astype(MM_DTYPE), b.astype(MM_DTYPE), (((1,), (1,)), ((), ())),
                           preferred_element_type=F32)


def _mm_tn(a, b):
    return lax.dot_general(a.astype(MM_DTYPE), b.astype(MM_DTYPE), (((0,), (0,)), ((), ())),
                           preferred_element_type=F32)


def _split(a):
    hi = a.astype(BF16)
    return hi, (a - hi.astype(F32)).astype(BF16)


def _hp(a2, b2):
    d = lambda p, q: jnp.dot(p, q, preferred_element_type=F32)
    return d(a2[0], b2[0]) + (d(a2[0], b2[1]) + d(a2[1], b2[0]))


def _silu(x):
    return x * jax.nn.sigmoid(x)


@jax.custom_vjp
def _stack(parts):
    return jnp.stack(parts)


_stack.defvjp(lambda parts: (jnp.stack(parts), None), lambda _, g: (tuple(g[i] for i in range(g.shape[0])),))


def _softplus(x):
    return jnp.maximum(x, 0.0) + jnp.log1p(jnp.exp(-jnp.abs(x)))


_ANY = pl.BlockSpec(memory_space=pl.ANY)


def _me():
    return lax.axis_index("x"), lax.axis_index("y"), lax.axis_index("c")


def _flat_id(pos):
    return 4 * pos[0] + 2 * pos[1] + pos[2]


def _remote(src, dst, send_sem, recv_sem, to):
    return pltpu.make_async_remote_copy(src_ref=src, dst_ref=dst, send_sem=send_sem, recv_sem=recv_sem,
                                        device_id=to, device_id_type=pl.DeviceIdType.MESH)


class _Direct:
    def __init__(self, items, bufs):
        self.items, self.bufs = list(items), list(bufs)
        self.n_src, self.n_buf = len(self.items), len(self.bufs)
        self.old = [j for j, b in enumerate(self.bufs) if not isinstance(b, jax.ShapeDtypeStruct)]
        self.args = [it[0] for it in self.items] + [self.bufs[j] for j in self.old]
        self.out_shape = [jax.ShapeDtypeStruct(b.shape, b.dtype) for b in self.bufs]
        self.scratch = [pltpu.SemaphoreType.DMA((self.n_src, N_DEV - 1)),
                        pltpu.SemaphoreType.DMA((self.n_src, N_DEV - 1)), pltpu.SemaphoreType.DMA((self.n_src,))]

    def aliases(self, in_base, out_base):
        return {in_base + self.n_src + pos: out_base + j for pos, j in enumerate(self.old)}

    def copies(self, in_refs, out_refs, sems):
        send_sems, recv_sems, local_sems = sems
        x, y, c = _me()
        me = _flat_id((x, y, c))
        peers = [(x ^ ((rel >> 2) & 1), y ^ ((rel >> 1) & 1), c ^ (rel & 1)) for rel in range(1, N_DEV)]
        local, sends, recvs = [], [], []
        for a, (_, per_dest, j, prefix, *rest) in enumerate(self.items):
            src = lambda d: in_refs[a].at[d] if per_dest else in_refs[a]
            dst = lambda s: out_refs[j].at[tuple(prefix) + (s,) + tuple(rest[0] if rest else ())]
            local.append(pltpu.make_async_copy(src(me), dst(me), local_sems.at[a]))
            for k, peer in enumerate(peers):
                pid = _flat_id(peer)
                sends.append(_remote(src(pid), dst(me), send_sems.at[a, k], recv_sems.at[a, k], peer))
                recvs.append(_remote(src(pid), dst(pid), send_sems.at[a, k], recv_sems.at[a, k], peer))
        return local, sends, recvs

    def start(self, in_refs, out_refs, sems):
        local, sends, _ = self.copies(in_refs, out_refs, sems)
        for cp in local + sends:
            cp.start()

    def wait(self, in_refs, out_refs, sems):
        local, sends, recvs = self.copies(in_refs, out_refs, sems)
        for cp in recvs:
            cp.wait_recv()
        for cp in sends:
            cp.wait_send()
        for cp in local:
            cp.wait()


def _pcall(core, *, name, grid, in_specs, out_specs, out_shape, args, sem, scratch_shapes=(), aliases=None,
           rider=None):
    n_in, n_out, n_scr = len(in_specs), len(out_specs), len(scratch_shapes)
    n_rin, n_rout = (len(rider.args), rider.n_buf) if rider else (0, 0)

    def body(*refs):
        ins, outs = refs[:n_in], refs[n_in + n_rin:n_in + n_rin + n_out]
        scr = refs[n_in + n_rin + n_out + n_rout:n_in + n_rin + n_out + n_rout + n_scr]
        if rider:
            r_refs = (refs[n_in:n_in + rider.n_src], refs[n_in + n_rin + n_out:n_in + n_rin + n_out + n_rout],
                      refs[n_in + n_rin + n_out + n_rout + n_scr:])
            ids = [pl.program_id(d) for d in range(len(grid))]
            first = functools.reduce(lambda p, q: p & q, [i == 0 for i in ids])
            last = functools.reduce(lambda p, q: p & q, [i == g - 1 for i, g in zip(ids, grid)])
            pl.when(first)(lambda: rider.start(*r_refs))
        core(ins, outs, scr)
        if rider:
            pl.when(last)(lambda: rider.wait(*r_refs))

    aliases = dict(aliases or {})
    if rider:
        sem = ("arbitrary",) * len(grid)
        aliases.update(rider.aliases(n_in, n_out))
    return pl.pallas_call(
        body, name=name, grid=grid, in_specs=list(in_specs) + [_ANY] * n_rin,
        out_specs=list(out_specs) + [_ANY] * n_rout,
        out_shape=list(out_shape) + (rider.out_shape if rider else []),
        scratch_shapes=list(scratch_shapes) + (rider.scratch if rider else []),
        input_output_aliases=aliases, compiler_params=_cparams(sem),
    )(*args, *(rider.args if rider else []))


def _exchange(direct, *, name):
    n_in = len(direct.args)

    def body(*refs):
        r_refs = refs[:direct.n_src], refs[n_in:n_in + direct.n_buf], refs[n_in + direct.n_buf:]
        direct.start(*r_refs)
        direct.wait(*r_refs)

    return pl.pallas_call(
        body, name=name, in_specs=[_ANY] * n_in, out_specs=[_ANY] * direct.n_buf, out_shape=direct.out_shape,
        input_output_aliases=direct.aliases(0, 0), scratch_shapes=direct.scratch,
    )(*direct.args)


def _matmul(a, b, *, form, tm, tn, tk, name, add=None, add_scale=1.0, extra=None, rider=None, a_cols=None):
    if form == "nn":
        (m, kk), n = a.shape, b.shape[1]
        a_spec = pl.BlockSpec((tm, tk), lambda i, j, k: (i, k))
        b_spec = pl.BlockSpec((tk, tn), lambda i, j, k: (k, j))
        dn = (((1,), (0,)), ((), ()))
    elif form == "nt":
        (m, kk), n = a.shape, b.shape[0]
        a_spec = pl.BlockSpec((tm, tk), lambda i, j, k: (i, k))
        b_spec = pl.BlockSpec((tn, tk), lambda i, j, k: (j, k))
        dn = (((1,), (1,)), ((), ()))
    else:
        kk, n = a.shape[0], b.shape[1]
        m0, m = a_cols or (0, a.shape[1])
        assert m0 % tm == 0
        a_spec = pl.BlockSpec((tk, tm), lambda i, j, k: (k, i + m0 // tm))
        b_spec = pl.BlockSpec((tk, tn), lambda i, j, k: (k, j))
        dn = (((0,), (0,)), ((), ()))
    assert m % tm == 0 and n % tn == 0 and kk % tk == 0, (name, m, n, kk)
    has_add, has_extra = add is not None, extra is not None

    def core(ins, outs, _):
        a_ref, b_ref = ins[:2]
        o_ref = outs[0]
        rest = ins[2:]
        k = pl.program_id(2)
        p = lax.dot_general(a_ref[...].astype(MM_DTYPE), b_ref[...].astype(MM_DTYPE), dn,
                            preferred_element_type=F32)

        @pl.when(k == 0)
        def _():
            first = p
            pos = 0
            if has_extra:
                first = first + _mm_nt(rest[0][...], rest[1][...])
                pos = 2
            if has_add:
                first = first + add_scale * rest[pos][...]
            o_ref[...] = first

        @pl.when(k > 0)
        def _():
            o_ref[...] += p

    in_specs = [a_spec, b_spec]
    args = [a, b]
    if has_extra:
        a2, b2, idx = extra
        in_specs += [pl.BlockSpec((tm, LANE), lambda i, j, k: (i, 0)),
                     pl.BlockSpec((tn, LANE), lambda i, j, k: (j, idx))]
        args += [a2, b2]
    if has_add:
        in_specs.append(pl.BlockSpec((tm, tn), lambda i, j, k: (i, j)))
        args.append(add)
    res = _pcall(core, name=name, grid=(m // tm, n // tn, kk // tk), in_specs=in_specs,
                 out_specs=[pl.BlockSpec((tm, tn), lambda i, j, k: (i, j))],
                 out_shape=[jax.ShapeDtypeStruct((m, n), F32)], args=args,
                 sem=("parallel", "parallel", "arbitrary"), rider=rider)
    return res if rider else res[0]


ZERO_TAIL = 8


def _with_tail(x):
    return jnp.concatenate([x, jnp.zeros((ZERO_TAIL,) + x.shape[1:], x.dtype)], axis=0)


def _shift_down(x, k):
    return pltpu.roll(x, k, 0)


def _shift_up(x, k):
    return pltpu.roll(x, x.shape[0] - k, 0)


def _conv_slab(x, w):
    return w[3:4] * x + w[2:3] * _shift_down(x, 1) + w[1:2] * _shift_down(x, 2) + w[0:1] * _shift_down(x, 3)


def _prep_fwd(h, conv_w, *, name):
    t_len = h.shape[0]

    def body(x_ref, w_ref, o_ref):
        s = pl.program_id(0)
        y = _silu(_conv_slab(_with_tail(x_ref[...]), w_ref[...])[:t_len])
        rs = lax.rsqrt(jnp.sum(y * y, axis=-1, keepdims=True) + L2_EPS)
        scale = jnp.where(s < A_HEADS, A_HEAD_DIM ** -0.5, 1.0)
        o_ref[...] = jnp.where(s < 2 * A_HEADS, y * rs * scale, y)

    return pl.pallas_call(
        body, name=name, grid=(12,),
        in_specs=[pl.BlockSpec((t_len, LANE), lambda s: (0, L_QKV // LANE + s)),
                  pl.BlockSpec((8, LANE), lambda s: (0, s))],
        out_specs=pl.BlockSpec((t_len, LANE), lambda s: (0, s)),
        out_shape=jax.ShapeDtypeStruct((t_len, 3 * A_WIDTH), F32),
        compiler_params=_cparams(("parallel",)),
    )(h, conv_w)


def _prep_bwd(h, conv_w, d_out, dh, *, name):
    t_len = h.shape[0]

    def body(x_ref, w_ref, g_ref, dh_in, dx_ref, dw_ref):
        del dh_in
        s = pl.program_id(0)
        x = _with_tail(x_ref[...])
        g = _with_tail(g_ref[0])
        w = w_ref[...]
        c = _conv_slab(x, w)
        sg = jax.nn.sigmoid(c)
        y = c * sg
        rs = lax.rsqrt(jnp.sum(y * y, axis=-1, keepdims=True) + L2_EPS)
        scale = jnp.where(s < A_HEADS, A_HEAD_DIM ** -0.5, 1.0)
        dy_n = scale * (rs * g - y * (rs * rs * rs) * jnp.sum(g * y, axis=-1, keepdims=True))
        dy = jnp.where(s < 2 * A_HEADS, dy_n, g)
        dc = dy * (sg * (1.0 + c * (1.0 - sg)))
        dx = w[3:4] * dc + w[2:3] * _shift_up(dc, 1) + w[1:2] * _shift_up(dc, 2) + w[0:1] * _shift_up(dc, 3)
        dx_ref[...] = dx[:t_len]
        dws = [jnp.sum(dc * _shift_down(x, 3 - j), axis=0, keepdims=True) if j < 3
               else jnp.sum(dc * x, axis=0, keepdims=True) for j in range(CONV_K)]
        dw_ref[...] = jnp.concatenate(dws + [jnp.zeros((8 - CONV_K, LANE), F32)], axis=0)

    slab = pl.BlockSpec((t_len, LANE), lambda s: (0, L_QKV // LANE + s))
    return pl.pallas_call(
        body, name=name, grid=(12,),
        in_specs=[slab, pl.BlockSpec((8, LANE), lambda s: (0, s)),
                  pl.BlockSpec((1, t_len, LANE), lambda s: (s // A_HEADS, 0, s % A_HEADS)), _ANY],
        out_specs=[slab, pl.BlockSpec((8, LANE), lambda s: (0, s))],
        out_shape=[jax.ShapeDtypeStruct((t_len, L_COLS), F32), jax.ShapeDtypeStruct((8, 3 * A_WIDTH), F32)],
        input_output_aliases={3: 0},
        compiler_params=_cparams(("parallel",)),
    )(h, conv_w, d_out, dh)


N_LEVELS = 5
MF_TRIL, MF_STRIL, MF_DIAG8, MF_LOW16, MF_EYE = 0, 1, 2, 3, 3 + N_LEVELS
MB_CUM, MB_CUM_T, MB_TOT = 0, 1, 2


def _gdn_masks():
    r = lax.broadcasted_iota(jnp.int32, (SUPER, SUPER), 0)
    c = lax.broadcasted_iota(jnp.int32, (SUPER, SUPER), 1)
    same = lambda shift: (r >> shift) == (c >> shift)
    ninf = lambda m: jnp.where(m, 0.0, -jnp.inf).astype(F32)
    one = lambda m: m.astype(F32)
    mf = jnp.stack([ninf(r >= c), ninf(r > c), one(same(3))]
                   + [one(same(4 + lv) & jnp.logical_not(same(3 + lv))) for lv in range(N_LEVELS)] + [one(r == c)])
    mb = jnp.stack([one(r >= c), one(r <= c), jnp.ones((SUPER, SUPER), F32)]).astype(BF16)
    return mf, mb


def _tri_inv_impl(a, mf):
    d = lambda p, q: jnp.dot(p.astype(BF16), q.astype(BF16), preferred_element_type=F32)
    dd = lambda p, q: jnp.dot(p, q, preferred_element_type=F32)
    eye = mf[MF_EYE]
    a0 = a * mf[MF_DIAG8]
    a2 = d(a0, a0)
    a4 = d(a2, a2)
    t = d(d(eye - a0, eye + a2), eye + a4)
    for level in range(N_LEVELS):
        t = t - d(d(t, a * mf[MF_LOW16 + level]), t)
    a_hi, a_lo = _split(a)
    for _ in range(NEWTON_STEPS):
        t_hi, t_lo = _split(t)
        resid = (eye - t) - (dd(a_hi, t_hi) + (dd(a_hi, t_lo) + dd(a_lo, t_hi)))
        r_hi, r_lo = _split(resid)
        t = t + (dd(t_hi, r_hi) + dd(t_hi, r_lo))
    return t


@jax.custom_vjp
def _wy_apply(a, rhs, t):
    return _mm(t, rhs)


def _wy_apply_fwd(a, rhs, t):
    x = _mm(t, rhs)
    return x, (t, x)


def _wy_apply_bwd(res, dx):
    t, x = res
    d_rhs = _mm_tn(t, dx)
    return -_mm_nt(d_rhs, x), d_rhs, jnp.zeros_like(t)


_wy_apply.defvjp(_wy_apply_fwd, _wy_apply_bwd)


@functools.partial(jax.custom_vjp, nondiff_argnums=(1,))
def _lane_roll(x, shift):
    return pltpu.roll(x, shift % LANE, 1)


_lane_roll.defvjp(lambda x, shift: (_lane_roll(x, shift), None), lambda shift, _, g: (_lane_roll(g, -shift),))


def _mask_times_lanes(x, mask):
    lane = lax.broadcasted_iota(jnp.int32, (1, LANE), 1)
    x = jnp.where(lane < A_HEADS, x, 0.0)
    x1 = x.astype(BF16).astype(F32)
    x2 = (x - x1).astype(BF16).astype(F32)
    x3 = (x - x1 - x2).astype(BF16).astype(F32)
    pieces = x1 + pltpu.roll(x2, A_HEADS, 1) + pltpu.roll(x3, 2 * A_HEADS, 1)
    res = jnp.dot(mask, pieces.astype(BF16), preferred_element_type=F32)
    return res + pltpu.roll(res, LANE - A_HEADS, 1) + pltpu.roll(res, LANE - 2 * A_HEADS, 1)


@jax.custom_vjp
def _chunk_sums(g, mb):
    return _mask_times_lanes(g, mb[MB_CUM]), _mask_times_lanes(g, mb[MB_TOT])


def _chunk_sums_fwd(g, mb):
    return _chunk_sums(g, mb), mb


def _chunk_sums_bwd(mb, d):
    lane = lax.broadcasted_iota(jnp.int32, (1, LANE), 1)
    dg = _mask_times_lanes(d[0], mb[MB_CUM_T]) + _mask_times_lanes(d[1], mb[MB_TOT])
    return jnp.where(lane < A_HEADS, dg, 0.0), jnp.zeros_like(mb)


_chunk_sums.defvjp(_chunk_sums_fwd, _chunk_sums_bwd)


def _gdn_gates(ba, alog, dtb, mb):
    beta = jax.nn.sigmoid(ba)
    g = -jnp.exp(alog) * _softplus(_lane_roll(ba, -A_HEADS) + dtb)
    gc, gl = _chunk_sums(g, mb)
    return beta, gc, gl, gc.T


def _gdn_block(s, q, k, v, z, gates, nw, h, t_known, mf):
    n = q.shape[0]
    beta_all, gc_all, gl_all, gct_all = gates
    lane = lax.broadcasted_iota(jnp.int32, (1, LANE), 1)
    sub = lax.broadcasted_iota(jnp.int32, (LANE, 1), 0)
    col = lambda x: jnp.sum(jnp.where(lane == h, x, 0.0), axis=1, keepdims=True)
    wide = lambda c: jnp.broadcast_to(c, (n, LANE))
    gc, gl = col(gc_all), col(gl_all)
    gc_row = jnp.sum(jnp.where(sub == h, gct_all, 0.0), axis=0, keepdims=True)
    beta_w, eg_w = wide(col(beta_all)), wide(jnp.exp(gc))
    diff = gc - gc_row
    decay = jnp.exp(diff + mf[MF_TRIL])
    kb = k * beta_w
    a_mat = _mm_nt(kb, k) * jnp.exp(diff + mf[MF_STRIL])
    rhs = jnp.concatenate([v * beta_w, kb * eg_w], axis=1)
    if t_known is None:
        t_mat = _tri_inv_impl(a_mat, mf)
        uw = _mm(t_mat, rhs)
    else:
        t_mat = t_known
        uw = _wy_apply(a_mat, rhs, t_known)
    u, w = uw[:, :LANE], uw[:, LANE:]
    qk = _mm_nt(q, k) * decay
    q_dec = q * eg_w
    k_dec = k * wide(jnp.exp(gl - gc))
    v_new = u - _mm(w, s)
    o = _mm(q_dec, s) + _mm(qk, v_new)
    s = s * jnp.exp(gl[0:1]) + _mm_tn(k_dec, v_new)
    o = o * lax.rsqrt(jnp.mean(o * o, axis=-1, keepdims=True) + RMS_EPS) * nw
    return o * _silu(z), s, t_mat


def _gdn_fwd(qkv, h, alog, dtb, nw, ycat, *, name, rider=None):
    t_len = qkv.shape[0]
    nsc = t_len // SUPER

    def core(ins, outs, scr):
        q_ref, k_ref, v_ref, gate_ref, al_ref, dt_ref, nw_ref, mf_ref, mb_ref, _ = ins
        y_ref, sin_ref, t_ref = outs
        s_scr, = scr

        @pl.when(pl.program_id(0) == 0)
        def _():
            s_scr[...] = jnp.zeros_like(s_scr)

        per_head = lambda ref: jnp.stack([ref[:, hh * LANE:(hh + 1) * LANE] for hh in range(A_HEADS)])
        states = s_scr[...]
        gates = _gdn_gates(gate_ref[:, A_WIDTH:], al_ref[...], dt_ref[...], mb_ref[...])
        fn = jax.vmap(_gdn_block, in_axes=(0, 0, 0, 0, 0, None, None, 0, None, None))
        y, s_new, t_mat = fn(states, per_head(q_ref), per_head(k_ref), per_head(v_ref), per_head(gate_ref),
                             gates, nw_ref[...], jnp.arange(A_HEADS), None, mf_ref[...])
        sin_ref[0] = states
        t_ref[0] = t_mat
        s_scr[...] = s_new
        for hh in range(A_HEADS):
            y_ref[:, hh * LANE:(hh + 1) * LANE] = y[hh]

    blk = lambda j: pl.BlockSpec((SUPER, A_WIDTH), lambda sc: (sc, j))
    row = pl.BlockSpec((1, LANE), lambda sc: (0, 0))
    mf, mb = _gdn_masks()
    whole = lambda a: pl.BlockSpec(a.shape, lambda sc: (0, 0, 0))
    return _pcall(
        core, name=name, grid=(nsc,),
        in_specs=[blk(0), blk(1), blk(2), pl.BlockSpec((SUPER, L_GATE), lambda sc: (sc, L_ZA // L_GATE)),
                  row, row, row, whole(mf), whole(mb), _ANY],
        out_specs=[blk(0),
                   pl.BlockSpec((1, A_HEADS, A_HEAD_DIM, A_HEAD_DIM), lambda sc: (sc, 0, 0, 0)),
                   pl.BlockSpec((1, A_HEADS, SUPER, SUPER), lambda sc: (sc, 0, 0, 0))],
        out_shape=[jax.ShapeDtypeStruct((t_len, D_MODEL), F32),
                   jax.ShapeDtypeStruct((nsc, A_HEADS, A_HEAD_DIM, A_HEAD_DIM), F32),
                   jax.ShapeDtypeStruct((nsc, A_HEADS, SUPER, SUPER), F32)],
        scratch_shapes=[pltpu.VMEM((A_HEADS, A_HEAD_DIM, A_HEAD_DIM), F32)],
        aliases={9: 0}, sem=("arbitrary",), rider=rider,
        args=(qkv, qkv, qkv, h, alog, dtb, nw, mf, mb, ycat))


def _gdn_bwd(qkv, h, alog, dtb, nw, s_in, t_in, dycat, dh, *, name, rider=None):
    t_len = qkv.shape[0]
    nsc = t_len // SUPER

    def core(ins, outs, scr):
        q_ref, k_ref, v_ref, gate_ref, al_ref, dt_ref, nw_ref, sin_ref, t_ref, dy_ref, mf_ref, mb_ref, _ = ins
        dgate_ref, dqkv_ref, dal_ref, ddt_ref, dnw_ref = outs
        ds_scr, = scr

        @pl.when(pl.program_id(0) == 0)
        def _():
            ds_scr[...] = jnp.zeros_like(ds_scr)
            dal_ref[...] = jnp.zeros_like(dal_ref)
            ddt_ref[...] = jnp.zeros_like(ddt_ref)
            dnw_ref[...] = jnp.zeros_like(dnw_ref)

        per_head = lambda ref: jnp.stack([ref[:, hh * LANE:(hh + 1) * LANE] for hh in range(A_HEADS)])
        head_ids = jnp.arange(A_HEADS)
        t_known, mf, mb = t_ref[0], mf_ref[...], mb_ref[...]

        def fn(s, q, k, v, z, ba, alog, dtb, nw):
            gates = _gdn_gates(ba, alog, dtb, mb)
            one = lambda s, q, k, v, z, t, h: _gdn_block(s, q, k, v, z, gates, nw, h, t, mf)[:2]
            return jax.vmap(one)(s, q, k, v, z, t_known, head_ids)

        _, vjp = jax.vjp(fn, sin_ref[0], per_head(q_ref), per_head(k_ref), per_head(v_ref), per_head(gate_ref),
                         gate_ref[:, A_WIDTH:], al_ref[...], dt_ref[...], nw_ref[...])
        ds, dq, dk, dv, dz, dba, dal, ddt, dnw = vjp((per_head(dy_ref), ds_scr[...]))
        ds_scr[...] = ds
        for hh in range(A_HEADS):
            cols = slice(hh * LANE, (hh + 1) * LANE)
            dqkv_ref[0, :, cols] = dq[hh]
            dqkv_ref[1, :, cols] = dk[hh]
            dqkv_ref[2, :, cols] = dv[hh]
            dgate_ref[:, cols] = dz[hh]
        dgate_ref[:, A_WIDTH:] = dba
        dal_ref[...] += dal
        ddt_ref[...] += ddt
        dnw_ref[...] += dnw

    rev = lambda i: nsc - 1 - i
    blk = lambda j: pl.BlockSpec((SUPER, A_WIDTH), lambda i: (rev(i), j))
    gate = pl.BlockSpec((SUPER, L_GATE), lambda i: (rev(i), L_ZA // L_GATE))
    row = pl.BlockSpec((1, LANE), lambda i: (0, 0))
    mf, mb = _gdn_masks()
    whole = lambda a: pl.BlockSpec(a.shape, lambda i: (0, 0, 0))
    return _pcall(
        core, name=name, grid=(nsc,),
        in_specs=[blk(0), blk(1), blk(2), gate, row, row, row,
                  pl.BlockSpec((1, A_HEADS, A_HEAD_DIM, A_HEAD_DIM), lambda i: (rev(i), 0, 0, 0)),
                  pl.BlockSpec((1, A_HEADS, SUPER, SUPER), lambda i: (rev(i), 0, 0, 0)),
                  blk(0), whole(mf), whole(mb), _ANY],
        out_specs=[gate, pl.BlockSpec((3, SUPER, A_WIDTH), lambda i: (0, rev(i), 0)), row, row, row],
        out_shape=[jax.ShapeDtypeStruct((t_len, L_COLS), F32), jax.ShapeDtypeStruct((3, t_len, A_WIDTH), F32)]
        + [jax.ShapeDtypeStruct((1, LANE), F32)] * 3,
        scratch_shapes=[pltpu.VMEM((A_HEADS, A_HEAD_DIM, A_HEAD_DIM), F32)],
        aliases={12: 0}, sem=("arbitrary",), rider=rider,
        args=(qkv, qkv, qkv, h, alog, dtb, nw, s_in, t_in, dycat, mf, mb, dh))


Q_BLOCKS = 4
Q_ROWS = Q_BLOCKS * BLOCK


def _swa_block(q, kp, kc, vp, vc, z, sinks, first):
    rows = B_GROUP * BLOCK
    ri = lax.broadcasted_iota(jnp.int32, (rows, 2 * BLOCK), 0)
    si = lax.broadcasted_iota(jnp.int32, (rows, 2 * BLOCK), 1)
    dist = (ri & (BLOCK - 1)) + BLOCK - si
    bias = jnp.where((dist >= 0) & (dist < WINDOW), 0.0, -jnp.inf)
    no_prev = jnp.where(first & (si[:1] < BLOCK), -jnp.inf, 0.0)
    dist_f = dist.astype(F32)
    head_of_row = lax.broadcasted_iota(jnp.int32, (rows, 1), 0) >> 7
    keys = jnp.concatenate([kp, kc], axis=0)
    vals = jnp.concatenate([vp, vc], axis=0)

    def item(b, j):
        cs = slice(j * B_HEAD_DIM, (j + 1) * B_HEAD_DIM)
        rs = slice(b * BLOCK, (b + 1) * BLOCK)
        heads = range(j * B_GROUP, (j + 1) * B_GROUP)
        qs = jnp.concatenate([q[rs, hq * B_HEAD_DIM:(hq + 1) * B_HEAD_DIM] for hq in heads], axis=0) * (
            B_HEAD_DIM ** -0.5)
        kk = keys[b * BLOCK:(b + 2) * BLOCK, cs]
        vv = vals[b * BLOCK:(b + 2) * BLOCK, cs]
        sink = jnp.concatenate([jnp.broadcast_to(sinks[:, hq:hq + 1], (BLOCK, 1)) for hq in heads], axis=0)
        slope = sum(jnp.where(head_of_row == gi, 2.0 ** (-8.0 * (hq + 1) / B_Q_HEADS), 0.0)
                    for gi, hq in enumerate(heads))
        return qs, kk, vv, sink, slope, (no_prev if b == 0 else jnp.zeros_like(no_prev))

    def attend(qs, kk, vv, sink, slope, hide):
        sc = _mm_nt(qs, kk) - slope * dist_f + (bias + hide)
        m = lax.stop_gradient(jnp.maximum(jnp.max(sc, axis=-1, keepdims=True), sink))
        p = jnp.exp(sc - m)
        inv = 1.0 / (jnp.sum(p, axis=-1, keepdims=True) + jnp.exp(sink - m))
        return _mm(p * inv, vv)

    items = [(b, j) for b in range(Q_BLOCKS) for j in range(B_KV_HEADS)]
    o = jax.vmap(attend)(*[_stack(t) for t in zip(*[item(b, j) for b, j in items])])
    rows_out = [jnp.concatenate([o[b * B_KV_HEADS + j, gi * BLOCK:(gi + 1) * BLOCK]
                                 for j in range(B_KV_HEADS) for gi in range(B_GROUP)], axis=1)
                for b in range(Q_BLOCKS)]
    return jnp.concatenate(rows_out, axis=0) * _silu(z)


def _swa_specs(idx):
    wide = lambda off: pl.BlockSpec((Q_ROWS, B_WIDTH), lambda n: (idx(n), off))
    cur = lambda off: pl.BlockSpec((Q_ROWS, LANE), lambda n: (idx(n), off))
    prev = lambda off: pl.BlockSpec((BLOCK, LANE), lambda n: (jnp.maximum(idx(n) * Q_BLOCKS - 1, 0), off))
    return [wide(L_QB // B_WIDTH), prev(L_KB // LANE), cur(L_KB // LANE), prev(L_VB // LANE), cur(L_VB // LANE),
            wide(L_ZB // B_WIDTH), pl.BlockSpec((1, LANE), lambda n: (0, 0))]


def _swa_fwd(h, sinks, *, name, rider=None):
    t_len = h.shape[0]
    nb = t_len // Q_ROWS

    def core(ins, outs, _):
        q_ref, kp_ref, kc_ref, vp_ref, vc_ref, z_ref, s_ref = ins
        outs[0][...] = _swa_block(q_ref[...], kp_ref[...], kc_ref[...], vp_ref[...], vc_ref[...], z_ref[...],
                                  s_ref[...], pl.program_id(0) == 0)

    res = _pcall(core, name=name, grid=(nb,), in_specs=_swa_specs(lambda n: n),
                 out_specs=[pl.BlockSpec((Q_ROWS, B_WIDTH), lambda n: (n, 1))],
                 out_shape=[jax.ShapeDtypeStruct((t_len, D_MODEL), F32)], sem=("parallel",), rider=rider,
                 args=(h, h, h, h, h, h, sinks))
    return res if rider else res[0]


def _swa_bwd(h, sinks, dycat, *, name, rider=None):
    t_len = h.shape[0]
    nb = t_len // Q_ROWS
    last = slice(Q_ROWS - BLOCK, Q_ROWS)

    def core(ins, outs, scr):
        q_ref, kp_ref, kc_ref, vp_ref, vc_ref, z_ref, s_ref, dy_ref = ins
        dh_ref, dsk_ref = outs
        ck_scr, cv_scr = scr
        i = pl.program_id(0)
        n = nb - 1 - i

        @pl.when(i == 0)
        def _():
            ck_scr[...] = jnp.zeros_like(ck_scr)
            cv_scr[...] = jnp.zeros_like(cv_scr)
            dsk_ref[...] = jnp.zeros_like(dsk_ref)

        fn = functools.partial(_swa_block, first=(n == 0))
        _, vjp = jax.vjp(fn, q_ref[...], kp_ref[...], kc_ref[...], vp_ref[...], vc_ref[...], z_ref[...], s_ref[...])
        dq, dkp, dkc, dvp, dvc, dz, dsk = vjp(dy_ref[...])
        dh_ref[:, L_QB:L_QB + B_WIDTH] = dq
        dh_ref[:, L_ZB:L_ZB + B_WIDTH] = dz
        dh_ref[:, L_KB:L_KB + LANE] = dkc
        dh_ref[:, L_VB:L_VB + LANE] = dvc
        dh_ref[last, L_KB:L_KB + LANE] += ck_scr[...]
        dh_ref[last, L_VB:L_VB + LANE] += cv_scr[...]
        ck_scr[...] = dkp
        cv_scr[...] = dvp
        dsk_ref[...] += dsk

    rev = lambda i: nb - 1 - i
    return _pcall(
        core, name=name, grid=(nb,),
        in_specs=_swa_specs(rev) + [pl.BlockSpec((Q_ROWS, B_WIDTH), lambda i: (rev(i), 1))],
        out_specs=[pl.BlockSpec((Q_ROWS, L_SWA), lambda i: (rev(i), 0)), pl.BlockSpec((1, LANE), lambda i: (0, 0))],
        out_shape=[jax.ShapeDtypeStruct((t_len, L_COLS), F32), jax.ShapeDtypeStruct((1, LANE), F32)],
        scratch_shapes=[pltpu.VMEM((BLOCK, LANE), F32), pltpu.VMEM((BLOCK, LANE), F32)],
        sem=("arbitrary",), rider=rider, args=(h, h, h, h, h, h, sinks, dycat))


def _out_ln_fwd(ycat, w_out, x, ln_g, ln_b, *, name, tm=256, last=False):
    t_len = x.shape[0]

    def body(y_ref, w_ref, x_ref, g_ref, b_ref, r_ref, *o_ref):
        r = DEEPNORM_ALPHA * x_ref[...] + _mm(y_ref[...], w_ref[...])
        r_ref[...] = r
        if not last:
            mu = jnp.mean(r, axis=-1, keepdims=True)
            d = r - mu
            var = jnp.mean(d * d, axis=-1, keepdims=True)
            o_ref[0][...] = d * lax.rsqrt(var + LN_EPS) * g_ref[...] + b_ref[...]

    tile = pl.BlockSpec((tm, D_MODEL), lambda i: (i, 0))
    vec = pl.BlockSpec((1, D_MODEL), lambda i: (0, 0))
    n_out = 1 if last else 2
    res = pl.pallas_call(
        body, name=name, grid=(t_len // tm,),
        in_specs=[tile, pl.BlockSpec((D_MODEL, D_MODEL), lambda i: (0, 0)), tile, vec, vec],
        out_specs=[tile] * n_out,
        out_shape=[jax.ShapeDtypeStruct((t_len, D_MODEL), F32)] * n_out,
        compiler_params=_cparams(("parallel",)),
    )(ycat, w_out, x, ln_g, ln_b)
    return (res[0], None) if last else res


def _ln_bwd(dxn, r, ln_g, *, name, tm=256, loss=None):
    t_len = r.shape[0]

    def body(*refs):
        if loss:
            t_ref, r_ref, g_ref, b_ref, dr_ref, dg_ref, db_ref, l_ref = refs
        else:
            dx_ref, r_ref, g_ref, dr_ref, dg_ref, db_ref = refs

        @pl.when(pl.program_id(0) == 0)
        def _():
            dg_ref[...] = jnp.zeros_like(dg_ref)
            db_ref[...] = jnp.zeros_like(db_ref)
            if loss:
                l_ref[...] = jnp.zeros_like(l_ref)

        rr = r_ref[...]
        mu = jnp.mean(rr, axis=-1, keepdims=True)
        d = rr - mu
        rstd = lax.rsqrt(jnp.mean(d * d, axis=-1, keepdims=True) + LN_EPS)
        xh = d * rstd
        if loss:
            e = (xh * g_ref[...] + b_ref[...]) - t_ref[...]
            dx = e * (1.0 / D_MODEL)
            l_ref[...] += jnp.sum(e * e, axis=0, keepdims=True)
        else:
            dx = dx_ref[...]
        dxh = dx * g_ref[...]
        dr_ref[...] = rstd * (dxh - jnp.mean(dxh, axis=-1, keepdims=True)
                              - xh * jnp.mean(dxh * xh, axis=-1, keepdims=True))
        dg_ref[...] += jnp.sum(dx * xh, axis=0, keepdims=True)
        db_ref[...] += jnp.sum(dx, axis=0, keepdims=True)

    tile = pl.BlockSpec((tm, D_MODEL), lambda i: (i, 0))
    vec = pl.BlockSpec((1, D_MODEL), lambda i: (0, 0))
    vec_shape = jax.ShapeDtypeStruct((1, D_MODEL), F32)
    args = (loss[0], r, ln_g, loss[1]) if loss else (dxn, r, ln_g)
    return pl.pallas_call(
        body, name=name, grid=(t_len // tm,),
        in_specs=[tile, tile, vec] + ([vec] if loss else []), out_specs=[tile, vec, vec] + ([vec] if loss else []),
        out_shape=[jax.ShapeDtypeStruct((t_len, D_MODEL), F32), vec_shape, vec_shape] + ([vec_shape] if loss else []),
        compiler_params=_cparams(("arbitrary",)),
    )(*args)


def _pad_row(v):
    return jnp.zeros((1, LANE), F32).at[0, :v.shape[0]].set(v)


_REGIONS = ((0, 1536, L_QKV), (1536, 2048, L_ZA), (2048, 2056, L_BA), (2056, 2568, L_QB), (2568, 2696, L_KB),
            (2696, 2824, L_VB), (2824, 3336, L_ZB))


def _to_layout(w_full):
    out = jnp.zeros(w_full.shape[:-1] + (L_COLS,), w_full.dtype)
    for a, b, off in _REGIONS:
        out = out.at[..., off:off + b - a].set(w_full[..., a:b])
    return out


def _from_layout(g):
    return jnp.concatenate([g[..., off:off + b - a] for a, b, off in _REGIONS], axis=-1)


def _shard_pieces(regions):
    for a, b, off in regions:
        for d in range(N_DEV):
            lo, hi = max(a, d * SHARD_COLS), min(b, (d + 1) * SHARD_COLS)
            if lo < hi:
                yield d, lo - d * SHARD_COLS, hi - d * SHARD_COLS, off + lo - a


def _as_list(r):
    return list(r) if isinstance(r, (list, tuple)) else [r]


def _gathered(shard):
    return jax.ShapeDtypeStruct((N_DEV,) + shard.shape, shard.dtype)


def _full_w_in(g_in, name):
    by_offset = sorted(_shard_pieces(_REGIONS), key=lambda p: p[3])
    tr = 256

    def body(g_ref, o_ref):
        pieces, col = [], 0
        for d, lo, hi, off in by_offset + [(None, 0, 0, L_COLS)]:
            if off > col:
                pieces.append(jnp.zeros((tr, off - col), g_ref.dtype))
            if d is not None:
                pieces.append(g_ref[d, :, lo:hi])
            col = off + hi - lo
        o_ref[...] = jnp.concatenate(pieces, axis=1)

    return pl.pallas_call(
        body, name=name, grid=(D_MODEL // tr,),
        in_specs=[pl.BlockSpec((N_DEV, tr, SHARD_COLS), lambda i: (0, i, 0))],
        out_specs=pl.BlockSpec((tr, L_COLS), lambda i: (i, 0)),
        out_shape=jax.ShapeDtypeStruct((D_MODEL, L_COLS), g_in.dtype),
        compiler_params=_cparams(("parallel",)),
    )(g_in)


def _full_conv(g_conv):
    return jnp.pad(g_conv.transpose(1, 0, 2).reshape(CONV_K, 3 * A_WIDTH), ((0, 8 - CONV_K), (0, 0)))


def _forward(x, weights, shards, small):
    a_log, dt_bias, norm_w, sinks, ln_g, ln_b = small
    tm = min(512, x.shape[0])
    saved, weights = [], [list(w) for w in weights]
    whole = lambda arrs: _Direct([(a, False, j, ()) for j, a in enumerate(arrs)], [_gathered(a) for a in arrs])
    for l in range(DEPTH):
        rider = whole(shards[l][1:]) if weights[l][1] is None else None
        h, *got = _as_list(_matmul(x, weights[l][0], form="nn", tm=tm, tn=L_COLS, tk=D_MODEL, name=f"in_proj_{l}",
                                   rider=rider))
        if rider:
            weights[l][1:] = [got[0].reshape(D_MODEL, D_MODEL), _full_conv(got[1])]
        w_in_l, w_out_l, conv_l = weights[l]
        qkv = _prep_fwd(h, conv_l, name=f"prep_fwd_{l}")
        al, dt, nw, sk = _pad_row(a_log[l]), _pad_row(dt_bias[l]), norm_w[l][None, :], _pad_row(sinks[l])
        ahead = l + 1 < DEPTH and weights[l + 1][0] is None
        rider = whole(shards[l + 1][1:]) if ahead else None
        ycat, *got = _as_list(_swa_fwd(h, sk, name=f"swa_fwd_{l}", rider=rider))
        if ahead:
            weights[l + 1][1:] = [got[0].reshape(D_MODEL, D_MODEL), _full_conv(got[1])]
        rider = whole(shards[l + 1][:1]) if ahead else None
        ycat, s_in, t_in, *got = _gdn_fwd(qkv, h, al, dt, nw, ycat, name=f"gdn_fwd_{l}", rider=rider)
        if ahead:
            weights[l + 1][0] = _full_w_in(got[0], f"w_in_columns_{l + 1}")
        r, xn = _out_ln_fwd(ycat, w_out_l, x, ln_g[l][None, :], ln_b[l][None, :], name=f"out_ln_{l}",
                            last=(l == DEPTH - 1))
        saved.append((x, h, qkv, s_in, t_in, ycat, r, al, dt, nw, sk))
        x = xn
    return x, saved, weights


def _w_in_blocks(g, name):
    rows, tr = g.shape[0], 256
    pieces = list(_shard_pieces(_REGIONS))

    def body(g_ref, o_ref):
        blocks = [[] for _ in range(N_DEV)]
        for d, lo, hi, off in pieces:
            blocks[d].append(g_ref[:, off:off + hi - lo])
        for d in range(N_DEV):
            o_ref[d] = jnp.concatenate(blocks[d], axis=1).astype(BF16)

    return pl.pallas_call(
        body, name=name, grid=(rows // tr,),
        in_specs=[pl.BlockSpec((tr, L_COLS), lambda i: (i, 0))],
        out_specs=pl.BlockSpec((N_DEV, tr, SHARD_COLS), lambda i: (0, i, 0)),
        out_shape=jax.ShapeDtypeStruct((N_DEV, rows, SHARD_COLS), BF16),
        compiler_params=_cparams(("parallel",)),
    )(g)


def _small_blocks(g):
    c_conv = g["conv_w"].reshape(CONV_K, N_DEV, CONV_SHARD_COLS).transpose(1, 0, 2)
    c_small = [jnp.broadcast_to(g[n][None], (N_DEV,) + g[n].shape) for n, _ in SMALL_SIZES]
    return _pack_small(c_conv, c_small)


def _contributions(g):
    c_out = g["w_out"].astype(BF16).reshape(N_DEV, OUT_SHARD_ROWS, D_MODEL)
    return _w_in_blocks(g["w_in_cols"], name="w_in_grad_blocks_above"), c_out, _small_blocks(g)


def _backward_layer(l, dx, saved_l, weights_l, ln_g_l, above=None, loss=None):
    x_in, h, qkv, s_in, t_in, ycat, r, al, dt, nw, sk = saved_l
    w_in_l, w_out_l, conv_l = weights_l
    tm = min(512, x_in.shape[0])
    dr, d_lng, d_lnb, *loss_lanes = _ln_bwd(dx, r, ln_g_l[None, :], name=f"ln_bwd_{l}", loss=loss)
    dycat = _matmul(dr, w_out_l, form="nt", tm=tm, tn=D_MODEL, tk=D_MODEL, name=f"out_proj_dx_{l}")
    d_wout = _matmul(ycat, dr, form="tn", tm=512, tn=D_MODEL, tk=tm, name=f"out_proj_dw_{l}")
    rider, p_in, p_out, p_small = None, None, None, None
    recv = lambda c: jax.ShapeDtypeStruct((DEPTH,) + c.shape, c.dtype)
    if above:
        c_out = d_wout.astype(BF16).reshape(N_DEV, OUT_SHARD_ROWS, D_MODEL)
        rider = _Direct([(above[1], True, 0, (l + 1,)), (above[2], True, 1, (l + 1,)), (c_out, True, 0, (l,))],
                        [recv(above[1]), recv(above[2])])
    dh, d_sk, *got = _swa_bwd(h, sk, dycat, name=f"swa_bwd_{l}", rider=rider)
    if above:
        p_out, p_small = got
        rider = _Direct([(above[0], True, 0, (l + 1,))], [recv(above[0])])
    dh, dqkv_n, d_al, d_dt, d_nw, *got = _gdn_bwd(qkv, h, al, dt, nw, s_in, t_in, dycat, dh,
                                                  name=f"gdn_bwd_{l}", rider=rider)
    dh, d_conv = _prep_bwd(h, conv_l, dqkv_n, dh, name=f"prep_bwd_{l}")
    grads = dict(w_out=d_wout, conv_w=d_conv[:CONV_K], a_log=d_al[0, :A_HEADS], dt_bias=d_dt[0, :A_HEADS],
                 norm_w=d_nw[0], sinks=d_sk[0, :B_Q_HEADS], ln_g=d_lng[0], ln_b=d_lnb[0])
    dw = functools.partial(_matmul, x_in, dh, form="tn", tm=512, tn=L_COLS, tk=tm)
    if not above:
        grads["w_in_cols"] = dw(name=f"in_proj_dw_{l}")
    else:
        p_in, = got
        half = D_MODEL // 2
        top = dw(name=f"in_proj_dw_top_{l}", a_cols=(0, half))
        blocks = _w_in_blocks(top, name=f"w_in_grad_blocks_top_{l}")
        rider = _Direct([(blocks, True, 0, (l,), (pl.ds(0, half),))], [p_in])
        bottom, p_in = dw(name=f"in_proj_dw_bottom_{l}", a_cols=(half, half), rider=rider)
        blocks = _w_in_blocks(bottom, name=f"w_in_grad_blocks_bottom_{l}")
        rider = _Direct([(blocks, True, 0, (l,), (pl.ds(half, half),)),
                         (_small_blocks(grads), True, 1, (l,))], [p_in, p_small])
    dx, *got = _as_list(_matmul(dh, w_in_l, form="nt", tm=tm, tn=D_MODEL, tk=L_COLS, name=f"in_proj_dx_{l}",
                                add=dr, add_scale=DEEPNORM_ALPHA, rider=rider))
    bufs = (got[0], p_out, got[1]) if above else None
    return dx, grads, bufs, (loss_lanes[0] if loss else None)


def _all_gather(shards, *, name):
    n_arr = len(shards)

    def body(*refs):
        x_refs, out_refs = refs[:n_arr], refs[n_arr:2 * n_arr]
        send_sems, recv_sems, local_sems = refs[2 * n_arr:]
        x, y, c = _me()
        me, sibling = (x, y, c), (x, y, 1 - c)
        chips = [(1 - x, y), (x, 1 - y), (1 - x, 1 - y)]

        def copy(a, k, block, to, src=None):
            dst = out_refs[a].at[_flat_id(block)]
            return _remote(dst if src is None else src, dst, send_sems.at[a, k], recv_sems.at[a, k], to)

        mine = [pltpu.make_async_copy(x_refs[a], out_refs[a].at[_flat_id(me)], local_sems.at[a])
                for a in range(n_arr)]
        for cp in mine:
            cp.start()
        first = []
        for a in range(n_arr):
            first.append(copy(a, 0, me, sibling, src=x_refs[a]))
            first += [copy(a, 1 + j, me, (*chip, c), src=x_refs[a]) for j, chip in enumerate(chips)]
        for cp in first:
            cp.start()
        passed = []
        for j, chip in enumerate(chips):
            for a in range(n_arr):
                copy(a, 1 + j, (*chip, c), me).wait_recv()
                fwd = copy(a, 4 + j, (*chip, c), sibling)
                fwd.start()
                passed.append(fwd)
        for a in range(n_arr):
            copy(a, 0, sibling, me).wait_recv()
            for j, chip in enumerate(chips):
                copy(a, 4 + j, (*chip, 1 - c), me).wait_recv()
        for cp in first + passed:
            cp.wait_send()
        for cp in mine:
            cp.wait()

    return pl.pallas_call(
        body, name=name, in_specs=[_ANY] * n_arr, out_specs=[_ANY] * n_arr,
        out_shape=[jax.ShapeDtypeStruct((N_DEV,) + s.shape, s.dtype) for s in shards],
        scratch_shapes=[pltpu.SemaphoreType.DMA((n_arr, N_DEV - 1)), pltpu.SemaphoreType.DMA((n_arr, N_DEV - 1)),
                        pltpu.SemaphoreType.DMA((n_arr,))],
    )(*shards)


def _adamw(parts, w, m, v, *, tr, name):
    depth, rows, cols = w.shape
    c1 = 1.0 - ADAM_B1 ** ADAM_STEP
    c2 = 1.0 - ADAM_B2 ** ADAM_STEP

    def body(g_ref, w_ref, m_ref, v_ref, go_ref, d_ref, mo_ref, vo_ref):
        g = g_ref[0, 0].astype(F32)
        for s in range(1, N_DEV):
            g = g + g_ref[0, s].astype(F32)
        m_new = ADAM_B1 * m_ref[0] + (1.0 - ADAM_B1) * g
        v_new = ADAM_B2 * v_ref[0] + (1.0 - ADAM_B2) * (g * g)
        go_ref[0] = g
        mo_ref[0] = m_new
        vo_ref[0] = v_new
        d_ref[0] = -ADAM_LR * ((m_new / c1) / (jnp.sqrt(v_new / c2) + ADAM_EPS) + ADAM_WD * w_ref[0])

    tile = pl.BlockSpec((1, tr, cols), lambda l, i: (l, i, 0))
    return pl.pallas_call(
        body, name=name, grid=(depth, rows // tr),
        in_specs=[pl.BlockSpec((1, N_DEV, tr, cols), lambda l, i: (l, 0, i, 0)), tile, tile, tile],
        out_specs=[tile] * 4, out_shape=[jax.ShapeDtypeStruct(w.shape, F32)] * 4,
        compiler_params=_cparams(("parallel", "parallel")),
    )(parts, w, m, v)


def _pack_small(conv, small):
    lead = conv.shape[:-2]
    flat = jnp.concatenate([conv.reshape(lead + (CS_CONV,))] + list(small), axis=-1)
    pad = CS_ROWS * LANE - flat.shape[-1]
    flat = jnp.concatenate([flat, jnp.zeros(lead + (pad,), F32)], axis=-1)
    return flat.reshape(lead + (CS_ROWS, LANE))


def _unpack_small(p):
    flat = p.reshape(DEPTH, CS_ROWS * LANE)
    conv = flat[:, :CS_CONV].reshape(DEPTH, CONV_K, CONV_SHARD_COLS)
    small, off = [], CS_CONV
    for _, n in SMALL_SIZES:
        small.append(flat[:, off:off + n])
        off += n
    return conv, small


def kernel(x, w_in, conv_w, a_log, dt_bias, norm_w, sinks, w_out, ln_g, ln_b, loss_target, m_w_in, m_conv_w, m_a_log, m_dt_bias, m_norm_w, m_sinks, m_w_out, m_ln_g, m_ln_b, v_w_in, v_conv_w, v_a_log, v_dt_bias, v_norm_w, v_sinks, v_w_out, v_ln_g, v_ln_b):
    small = [a_log, dt_bias, norm_w, sinks, ln_g, ln_b]
    shards = [[w_in[l].astype(BF16), w_out[l].astype(BF16), conv_w[l]] for l in range(DEPTH)]
    g_in0, = _all_gather(shards[0][:1], name="weights_all_gather_0")
    weights = [[_full_w_in(g_in0, "w_in_columns_0"), None, None]] + [[None, None, None]] * (DEPTH - 1)

    _, saved, weights = _forward(x[0], weights, shards, small)
    dx, g1, _, loss_lanes = _backward_layer(1, None, saved[1], weights[1], ln_g[1],
                                            loss=(loss_target[0], ln_b[1][None, :]))
    loss = lax.psum(0.5 * jnp.sum(loss_lanes) * (1.0 / D_MODEL), ("x", "y", "c"))
    dx, _, (p_in, p_out, p_small), _ = _backward_layer(0, dx, saved[0], weights[0], ln_g[0],
                                                       above=_contributions(g1))

    o_in = _adamw(p_in, w_in, m_w_in, v_w_in, tr=256, name="adamw_w_in")
    o_out = _adamw(p_out, w_out, m_w_out, v_w_out, tr=OUT_SHARD_ROWS, name="adamw_w_out")
    o_small = _adamw(p_small, _pack_small(conv_w, small),
                     _pack_small(m_conv_w, [m_a_log, m_dt_bias, m_norm_w, m_sinks, m_ln_g, m_ln_b]),
                     _pack_small(v_conv_w, [v_a_log, v_dt_bias, v_norm_w, v_sinks, v_ln_g, v_ln_b]),
                     tr=CS_ROWS, name="adamw_small")
    outs = []
    for k in range(4):
        cv, sm = _unpack_small(o_small[k])
        outs += [o_in[k], cv, sm[0], sm[1], sm[2], sm[3], o_out[k], sm[4], sm[5]]
    return (loss, dx[None], *outs)
```

```python
import functools

import jax
import jax.numpy as jnp
from jax import lax
from jax.experimental import pallas as pl
from jax.experimental.pallas import tpu as pltpu

F32 = jnp.float32
BF16 = jnp.bfloat16
MM_DTYPE = BF16

N_DEV = 8
D_MODEL = 1024
DEPTH = 2
A_HEADS = 4
A_HEAD_DIM = 128
A_WIDTH = 512
CONV_K = 4
CHUNK = 64
SUPER = 256
NEWTON_STEPS = 1
B_Q_HEADS = 8
B_KV_HEADS = 2
B_HEAD_DIM = 64
B_GROUP = 4
B_WIDTH = 512
WINDOW = 128
BLOCK = 128
IN_COLS = 3336
SHARD_COLS = IN_COLS // N_DEV
OUT_SHARD_ROWS = D_MODEL // N_DEV
CONV_SHARD_COLS = 3 * A_WIDTH // N_DEV
DEEPNORM_ALPHA = (2 * DEPTH) ** 0.25
LN_EPS = 1e-5
RMS_EPS = 1e-6
L2_EPS = 1e-6
ADAM_LR, ADAM_B1, ADAM_B2, ADAM_EPS, ADAM_WD, ADAM_STEP = 0.001, 0.9, 0.999, 1e-08, 0.01, 10

LANE = 128
L_QB, L_ZB, L_KB, L_VB, L_ZA, L_BA, L_QKV = 0, 512, 1024, 1152, 1280, 1792, 1920
L_SWA = 1280
L_GATE = 640
L_COLS = 3456
SMALL_SIZES = (("a_log", 4), ("dt_bias", 4), ("norm_w", 128), ("sinks", 8), ("ln_g", 1024), ("ln_b", 1024))
CS_CONV = CONV_K * CONV_SHARD_COLS
CS_ROWS = 24
VMEM_LIMIT = 48 * 1024 * 1024


def _cparams(sem=None):
    return pltpu.CompilerParams(dimension_semantics=sem, vmem_limit_bytes=VMEM_LIMIT)


def _mm(a, b):
    return jnp.dot(a.astype(MM_DTYPE), b.astype(MM_DTYPE), preferred_element_type=F32)


def _mm_nt(a, b):
    return lax.dot_general(a.astype(MM_DTYPE), b.astype(MM_DTYPE), (((1,), (1,)), ((), ())),
                           preferred_element_type=F32)


def _mm_tn(a, b):
    return lax.dot_general(a.astype(MM_DTYPE), b.astype(MM_DTYPE), (((0,), (0,)), ((), ())),
                           preferred_element_type=F32)


def _split(a):
    hi = a.astype(BF16)
    return hi, (a - hi.astype(F32)).astype(BF16)


def _hp(a2, b2):
    d = lambda p, q: jnp.dot(p, q, preferred_element_type=F32)
    return d(a2[0], b2[0]) + (d(a2[0], b2[1]) + d(a2[1], b2[0]))


def _silu(x):
    return x * jax.nn.sigmoid(x)


@jax.custom_vjp
def _stack(parts):
    return jnp.stack(parts)


_stack.defvjp(lambda parts: (jnp.stack(parts), None), lambda _, g: (tuple(g[i] for i in range(g.shape[0])),))


def _softplus(x):
    return jnp.maximum(x, 0.0) + jnp.log1p(jnp.exp(-jnp.abs(x)))


_ANY = pl.BlockSpec(memory_space=pl.ANY)


def _me():
    return lax.axis_index("x"), lax.axis_index("y"), lax.axis_index("c")


def _flat_id(pos):
    return 4 * pos[0] + 2 * pos[1] + pos[2]


def _remote(src, dst, send_sem, recv_sem, to):
    return pltpu.make_async_remote_copy(src_ref=src, dst_ref=dst, send_sem=send_sem, recv_sem=recv_sem,
                                        device_id=to, device_id_type=pl.DeviceIdType.MESH)


class _Direct:
    def __init__(self, items, bufs):
        self.items, self.bufs = list(items), list(bufs)
        self.n_src, self.n_buf = len(self.items), len(self.bufs)
        self.old = [j for j, b in enumerate(self.bufs) if not isinstance(b, jax.ShapeDtypeStruct)]
        self.args = [it[0] for it in self.items] + [self.bufs[j] for j in self.old]
        self.out_shape = [jax.ShapeDtypeStruct(b.shape, b.dtype) for b in self.bufs]
        self.scratch = [pltpu.SemaphoreType.DMA((self.n_src, N_DEV - 1)),
                        pltpu.SemaphoreType.DMA((self.n_src, N_DEV - 1)), pltpu.SemaphoreType.DMA((self.n_src,))]

    def aliases(self, in_base, out_base):
        return {in_base + self.n_src + pos: out_base + j for pos, j in enumerate(self.old)}

    def copies(self, in_refs, out_refs, sems):
        send_sems, recv_sems, local_sems = sems
        x, y, c = _me()
        me = _flat_id((x, y, c))
        peers = [(x ^ ((rel >> 2) & 1), y ^ ((rel >> 1) & 1), c ^ (rel & 1)) for rel in range(1, N_DEV)]
        local, sends, recvs = [], [], []
        for a, (_, per_dest, j, prefix, *rest) in enumerate(self.items):
            src = lambda d: in_refs[a].at[d] if per_dest else in_refs[a]
            dst = lambda s: out_refs[j].at[tuple(prefix) + (s,) + tuple(rest[0] if rest else ())]
            local.append(pltpu.make_async_copy(src(me), dst(me), local_sems.at[a]))
            for k, peer in enumerate(peers):
                pid = _flat_id(peer)
                sends.append(_remote(src(pid), dst(me), send_sems.at[a, k], recv_sems.at[a, k], peer))
                recvs.append(_remote(src(pid), dst(pid), send_sems.at[a, k], recv_sems.at[a, k], peer))
        return local, sends, recvs

    def start(self, in_refs, out_refs, sems):
        local, sends, _ = self.copies(in_refs, out_refs, sems)
        for cp in local + sends:
            cp.start()

    def wait(self, in_refs, out_refs, sems):
        local, sends, recvs = self.copies(in_refs, out_refs, sems)
        for cp in recvs:
            cp.wait_recv()
        for cp in sends:
            cp.wait_send()
        for cp in local:
            cp.wait()


def _pcall(core, *, name, grid, in_specs, out_specs, out_shape, args, sem, scratch_shapes=(), aliases=None,
           rider=None):
    n_in, n_out, n_scr = len(in_specs), len(out_specs), len(scratch_shapes)
    n_rin, n_rout = (len(rider.args), rider.n_buf) if rider else (0, 0)

    def body(*refs):
        ins, outs = refs[:n_in], refs[n_in + n_rin:n_in + n_rin + n_out]
        scr = refs[n_in + n_rin + n_out + n_rout:n_in + n_rin + n_out + n_rout + n_scr]
        if rider:
            r_refs = (refs[n_in:n_in + rider.n_src], refs[n_in + n_rin + n_out:n_in + n_rin + n_out + n_rout],
                      refs[n_in + n_rin + n_out + n_rout + n_scr:])
            ids = [pl.program_id(d) for d in range(len(grid))]
            first = functools.reduce(lambda p, q: p & q, [i == 0 for i in ids])
            last = functools.reduce(lambda p, q: p & q, [i == g - 1 for i, g in zip(ids, grid)])
            pl.when(first)(lambda: rider.start(*r_refs))
        core(ins, outs, scr)
        if rider:
            pl.when(last)(lambda: rider.wait(*r_refs))

    aliases = dict(aliases or {})
    if rider:
        sem = ("arbitrary",) * len(grid)
        aliases.update(rider.aliases(n_in, n_out))
    return pl.pallas_call(
        body, name=name, grid=grid, in_specs=list(in_specs) + [_ANY] * n_rin,
        out_specs=list(out_specs) + [_ANY] * n_rout,
        out_shape=list(out_shape) + (rider.out_shape if rider else []),
        scratch_shapes=list(scratch_shapes) + (rider.scratch if rider else []),
        input_output_aliases=aliases, compiler_params=_cparams(sem),
    )(*args, *(rider.args if rider else []))


def _exchange(direct, *, name):
    n_in = len(direct.args)

    def body(*refs):
        r_refs = refs[:direct.n_src], refs[n_in:n_in + direct.n_buf], refs[n_in + direct.n_buf:]
        direct.start(*r_refs)
        direct.wait(*r_refs)

    return pl.pallas_call(
        body, name=name, in_specs=[_ANY] * n_in, out_specs=[_ANY] * direct.n_buf, out_shape=direct.out_shape,
        input_output_aliases=direct.aliases(0, 0), scratch_shapes=direct.scratch,
    )(*direct.args)


def _matmul(a, b, *, form, tm, tn, tk, name, add=None, add_scale=1.0, extra=None, rider=None, a_cols=None):
    if form == "nn":
        (m, kk), n = a.shape, b.shape[1]
        a_spec = pl.BlockSpec((tm, tk), lambda i, j, k: (i, k))
        b_spec = pl.BlockSpec((tk, tn), lambda i, j, k: (k, j))
        dn = (((1,), (0,)), ((), ()))
    elif form == "nt":
        (m, kk), n = a.shape, b.shape[0]
        a_spec = pl.BlockSpec((tm, tk), lambda i, j, k: (i, k))
        b_spec = pl.BlockSpec((tn, tk), lambda i, j, k: (j, k))
        dn = (((1,), (1,)), ((), ()))
    else:
        kk, n = a.shape[0], b.shape[1]
        m0, m = a_cols or (0, a.shape[1])
        assert m0 % tm == 0
        a_spec = pl.BlockSpec((tk, tm), lambda i, j, k: (k, i + m0 // tm))
        b_spec = pl.BlockSpec((tk, tn), lambda i, j, k: (k, j))
        dn = (((0,), (0,)), ((), ()))
    assert m % tm == 0 and n % tn == 0 and kk % tk == 0, (name, m, n, kk)
    has_add, has_extra = add is not None, extra is not None

    def core(ins, outs, _):
        a_ref, b_ref = ins[:2]
        o_ref = outs[0]
        rest = ins[2:]
        k = pl.program_id(2)
        p = lax.dot_general(a_ref[...].astype(MM_DTYPE), b_ref[...].astype(MM_DTYPE), dn,
                            preferred_element_type=F32)

        @pl.when(k == 0)
        def _():
            first = p
            pos = 0
            if has_extra:
                first = first + _mm_nt(rest[0][...], rest[1][...])
                pos = 2
            if has_add:
                first = first + add_scale * rest[pos][...]
            o_ref[...] = first

        @pl.when(k > 0)
        def _():
            o_ref[...] += p

    in_specs = [a_spec, b_spec]
    args = [a, b]
    if has_extra:
        a2, b2, idx = extra
        in_specs += [pl.BlockSpec((tm, LANE), lambda i, j, k: (i, 0)),
                     pl.BlockSpec((tn, LANE), lambda i, j, k: (j, idx))]
        args += [a2, b2]
    if has_add:
        in_specs.append(pl.BlockSpec((tm, tn), lambda i, j, k: (i, j)))
        args.append(add)
    res = _pcall(core, name=name, grid=(m // tm, n // tn, kk // tk), in_specs=in_specs,
                 out_specs=[pl.BlockSpec((tm, tn), lambda i, j, k: (i, j))],
                 out_shape=[jax.ShapeDtypeStruct((m, n), F32)], args=args,
                 sem=("parallel", "parallel", "arbitrary"), rider=rider)
    return res if rider else res[0]


ZERO_TAIL = 8


def _with_tail(x):
    return jnp.concatenate([x, jnp.zeros((ZERO_TAIL,) + x.shape[1:], x.dtype)], axis=0)


def _shift_down(x, k):
    return pltpu.roll(x, k, 0)


def _shift_up(x, k):
    return pltpu.roll(x, x.shape[0] - k, 0)


def _conv_slab(x, w):
    return w[3:4] * x + w[2:3] * _shift_down(x, 1) + w[1:2] * _shift_down(x, 2) + w[0:1] * _shift_down(x, 3)


def _prep_fwd(h, conv_w, *, name):
    t_len = h.shape[0]

    def body(x_ref, w_ref, o_ref):
        s = pl.program_id(0)
        y = _silu(_conv_slab(_with_tail(x_ref[...]), w_ref[...])[:t_len])
        rs = lax.rsqrt(jnp.sum(y * y, axis=-1, keepdims=True) + L2_EPS)
        scale = jnp.where(s < A_HEADS, A_HEAD_DIM ** -0.5, 1.0)
        o_ref[...] = jnp.where(s < 2 * A_HEADS, y * rs * scale, y)

    return pl.pallas_call(
        body, name=name, grid=(12,),
        in_specs=[pl.BlockSpec((t_len, LANE), lambda s: (0, L_QKV // LANE + s)),
                  pl.BlockSpec((8, LANE), lambda s: (0, s))],
        out_specs=pl.BlockSpec((t_len, LANE), lambda s: (0, s)),
        out_shape=jax.ShapeDtypeStruct((t_len, 3 * A_WIDTH), F32),
        compiler_params=_cparams(("parallel",)),
    )(h, conv_w)


def _prep_bwd(h, conv_w, d_out, dh, *, name):
    t_len = h.shape[0]

    def body(x_ref, w_ref, g_ref, dh_in, dx_ref, dw_ref):
        del dh_in
        s = pl.program_id(0)
        x = _with_tail(x_ref[...])
        g = _with_tail(g_ref[0])
        w = w_ref[...]
        c = _conv_slab(x, w)
        sg = jax.nn.sigmoid(c)
        y = c * sg
        rs = lax.rsqrt(jnp.sum(y * y, axis=-1, keepdims=True) + L2_EPS)
        scale = jnp.where(s < A_HEADS, A_HEAD_DIM ** -0.5, 1.0)
        dy_n = scale * (rs * g - y * (rs * rs * rs) * jnp.sum(g * y, axis=-1, keepdims=True))
        dy = jnp.where(s < 2 * A_HEADS, dy_n, g)
        dc = dy * (sg * (1.0 + c * (1.0 - sg)))
        dx = w[3:4] * dc + w[2:3] * _shift_up(dc, 1) + w[1:2] * _shift_up(dc, 2) + w[0:1] * _shift_up(dc, 3)
        dx_ref[...] = dx[:t_len]
        dws = [jnp.sum(dc * _shift_down(x, 3 - j), axis=0, keepdims=True) if j < 3
               else jnp.sum(dc * x, axis=0, keepdims=True) for j in range(CONV_K)]
        dw_ref[...] = jnp.concatenate(dws + [jnp.zeros((8 - CONV_K, LANE), F32)], axis=0)

    slab = pl.BlockSpec((t_len, LANE), lambda s: (0, L_QKV // LANE + s))
    return pl.pallas_call(
        body, name=name, grid=(12,),
        in_specs=[slab, pl.BlockSpec((8, LANE), lambda s: (0, s)),
                  pl.BlockSpec((1, t_len, LANE), lambda s: (s // A_HEADS, 0, s % A_HEADS)), _ANY],
        out_specs=[slab, pl.BlockSpec((8, LANE), lambda s: (0, s))],
        out_shape=[jax.ShapeDtypeStruct((t_len, L_COLS), F32), jax.ShapeDtypeStruct((8, 3 * A_WIDTH), F32)],
        input_output_aliases={3: 0},
        compiler_params=_cparams(("parallel",)),
    )(h, conv_w, d_out, dh)


N_LEVELS = 5
MF_TRIL, MF_STRIL, MF_DIAG8, MF_LOW16, MF_EYE = 0, 1, 2, 3, 3 + N_LEVELS
MB_CUM, MB_CUM_T, MB_TOT = 0, 1, 2


def _gdn_masks():
    r = lax.broadcasted_iota(jnp.int32, (SUPER, SUPER), 0)
    c = lax.broadcasted_iota(jnp.int32, (SUPER, SUPER), 1)
    same = lambda shift: (r >> shift) == (c >> shift)
    ninf = lambda m: jnp.where(m, 0.0, -jnp.inf).astype(F32)
    one = lambda m: m.astype(F32)
    mf = jnp.stack([ninf(r >= c), ninf(r > c), one(same(3))]
                   + [one(same(4 + lv) & jnp.logical_not(same(3 + lv))) for lv in range(N_LEVELS)] + [one(r == c)])
    mb = jnp.stack([one(r >= c), one(r <= c), jnp.ones((SUPER, SUPER), F32)]).astype(BF16)
    return mf, mb


def _tri_inv_impl(a, mf):
    d = lambda p, q: jnp.dot(p.astype(BF16), q.astype(BF16), preferred_element_type=F32)
    dd = lambda p, q: jnp.dot(p, q, preferred_element_type=F32)
    eye = mf[MF_EYE]
    a0 = a * mf[MF_DIAG8]
    a2 = d(a0, a0)
    a4 = d(a2, a2)
    t = d(d(eye - a0, eye + a2), eye + a4)
    for level in range(N_LEVELS):
        t = t - d(d(t, a * mf[MF_LOW16 + level]), t)
    a_hi, a_lo = _split(a)
    for _ in range(NEWTON_STEPS):
        t_hi, t_lo = _split(t)
        resid = (eye - t) - (dd(a_hi, t_hi) + (dd(a_hi, t_lo) + dd(a_lo, t_hi)))
        r_hi, r_lo = _split(resid)
        t = t + (dd(t_hi, r_hi) + dd(t_hi, r_lo))
    return t


@jax.custom_vjp
def _wy_apply(a, rhs, t):
    return _mm(t, rhs)


def _wy_apply_fwd(a, rhs, t):
    x = _mm(t, rhs)
    return x, (t, x)


def _wy_apply_bwd(res, dx):
    t, x = res
    d_rhs = _mm_tn(t, dx)
    return -_mm_nt(d_rhs, x), d_rhs, jnp.zeros_like(t)


_wy_apply.defvjp(_wy_apply_fwd, _wy_apply_bwd)


@functools.partial(jax.custom_vjp, nondiff_argnums=(1,))
def _lane_roll(x, shift):
    return pltpu.roll(x, shift % LANE, 1)


_lane_roll.defvjp(lambda x, shift: (_lane_roll(x, shift), None), lambda shift, _, g: (_lane_roll(g, -shift),))


def _mask_times_lanes(x, mask):
    lane = lax.broadcasted_iota(jnp.int32, (1, LANE), 1)
    x = jnp.where(lane < A_HEADS, x, 0.0)
    x1 = x.astype(BF16).astype(F32)
    x2 = (x - x1).astype(BF16).astype(F32)
    x3 = (x - x1 - x2).astype(BF16).astype(F32)
    pieces = x1 + pltpu.roll(x2, A_HEADS, 1) + pltpu.roll(x3, 2 * A_HEADS, 1)
    res = jnp.dot(mask, pieces.astype(BF16), preferred_element_type=F32)
    return res + pltpu.roll(res, LANE - A_HEADS, 1) + pltpu.roll(res, LANE - 2 * A_HEADS, 1)


@jax.custom_vjp
def _chunk_sums(g, mb):
    return _mask_times_lanes(g, mb[MB_CUM]), _mask_times_lanes(g, mb[MB_TOT])


def _chunk_sums_fwd(g, mb):
    return _chunk_sums(g, mb), mb


def _chunk_sums_bwd(mb, d):
    lane = lax.broadcasted_iota(jnp.int32, (1, LANE), 1)
    dg = _mask_times_lanes(d[0], mb[MB_CUM_T]) + _mask_times_lanes(d[1], mb[MB_TOT])
    return jnp.where(lane < A_HEADS, dg, 0.0), jnp.zeros_like(mb)


_chunk_sums.defvjp(_chunk_sums_fwd, _chunk_sums_bwd)


def _gdn_gates(ba, alog, dtb, mb):
    beta = jax.nn.sigmoid(ba)
    g = -jnp.exp(alog) * _softplus(_lane_roll(ba, -A_HEADS) + dtb)
    gc, gl = _chunk_sums(g, mb)
    return beta, gc, gl, gc.T


def _gdn_block(s, q, k, v, z, gates, nw, h, t_known, mf):
    n = q.shape[0]
    beta_all, gc_all, gl_all, gct_all = gates
    lane = lax.broadcasted_iota(jnp.int32, (1, LANE), 1)
    sub = lax.broadcasted_iota(jnp.int32, (LANE, 1), 0)
    col = lambda x: jnp.sum(jnp.where(lane == h, x, 0.0), axis=1, keepdims=True)
    wide = lambda c: jnp.broadcast_to(c, (n, LANE))
    gc, gl = col(gc_all), col(gl_all)
    gc_row = jnp.sum(jnp.where(sub == h, gct_all, 0.0), axis=0, keepdims=True)
    beta_w, eg_w = wide(col(beta_all)), wide(jnp.exp(gc))
    diff = gc - gc_row
    decay = jnp.exp(diff + mf[MF_TRIL])
    kb = k * beta_w
    a_mat = _mm_nt(kb, k) * jnp.exp(diff + mf[MF_STRIL])
    rhs = jnp.concatenate([v * beta_w, kb * eg_w], axis=1)
    if t_known is None:
        t_mat = _tri_inv_impl(a_mat, mf)
        uw = _mm(t_mat, rhs)
    else:
        t_mat = t_known
        uw = _wy_apply(a_mat, rhs, t_known)
    u, w = uw[:, :LANE], uw[:, LANE:]
    qk = _mm_nt(q, k) * decay
    q_dec = q * eg_w
    k_dec = k * wide(jnp.exp(gl - gc))
    v_new = u - _mm(w, s)
    o = _mm(q_dec, s) + _mm(qk, v_new)
    s = s * jnp.exp(gl[0:1]) + _mm_tn(k_dec, v_new)
    o = o * lax.rsqrt(jnp.mean(o * o, axis=-1, keepdims=True) + RMS_EPS) * nw
    return o * _silu(z), s, t_mat


def _gdn_fwd(qkv, h, alog, dtb, nw, ycat, *, name, rider=None):
    t_len = qkv.shape[0]
    nsc = t_len // SUPER

    def core(ins, outs, scr):
        q_ref, k_ref, v_ref, gate_ref, al_ref, dt_ref, nw_ref, mf_ref, mb_ref, _ = ins
        y_ref, sin_ref, t_ref = outs
        s_scr, = scr

        @pl.when(pl.program_id(0) == 0)
        def _():
            s_scr[...] = jnp.zeros_like(s_scr)

        per_head = lambda ref: jnp.stack([ref[:, hh * LANE:(hh + 1) * LANE] for hh in range(A_HEADS)])
        states = s_scr[...]
        gates = _gdn_gates(gate_ref[:, A_WIDTH:], al_ref[...], dt_ref[...], mb_ref[...])
        fn = jax.vmap(_gdn_block, in_axes=(0, 0, 0, 0, 0, None, None, 0, None, None))
        y, s_new, t_mat = fn(states, per_head(q_ref), per_head(k_ref), per_head(v_ref), per_head(gate_ref),
                             gates, nw_ref[...], jnp.arange(A_HEADS), None, mf_ref[...])
        sin_ref[0] = states
        t_ref[0] = t_mat
        s_scr[...] = s_new
        for hh in range(A_HEADS):
            y_ref[:, hh * LANE:(hh + 1) * LANE] = y[hh]

    blk = lambda j: pl.BlockSpec((SUPER, A_WIDTH), lambda sc: (sc, j))
    row = pl.BlockSpec((1, LANE), lambda sc: (0, 0))
    mf, mb = _gdn_masks()
    whole = lambda a: pl.BlockSpec(a.shape, lambda sc: (0, 0, 0))
    return _pcall(
        core, name=name, grid=(nsc,),
        in_specs=[blk(0), blk(1), blk(2), pl.BlockSpec((SUPER, L_GATE), lambda sc: (sc, L_ZA // L_GATE)),
                  row, row, row, whole(mf), whole(mb), _ANY],
        out_specs=[blk(0),
                   pl.BlockSpec((1, A_HEADS, A_HEAD_DIM, A_HEAD_DIM), lambda sc: (sc, 0, 0, 0)),
                   pl.BlockSpec((1, A_HEADS, SUPER, SUPER), lambda sc: (sc, 0, 0, 0))],
        out_shape=[jax.ShapeDtypeStruct((t_len, D_MODEL), F32),
                   jax.ShapeDtypeStruct((nsc, A_HEADS, A_HEAD_DIM, A_HEAD_DIM), F32),
                   jax.ShapeDtypeStruct((nsc, A_HEADS, SUPER, SUPER), F32)],
        scratch_shapes=[pltpu.VMEM((A_HEADS, A_HEAD_DIM, A_HEAD_DIM), F32)],
        aliases={9: 0}, sem=("arbitrary",), rider=rider,
        args=(qkv, qkv, qkv, h, alog, dtb, nw, mf, mb, ycat))


def _gdn_bwd(qkv, h, alog, dtb, nw, s_in, t_in, dycat, dh, *, name, rider=None):
    t_len = qkv.shape[0]
    nsc = t_len // SUPER

    def core(ins, outs, scr):
        q_ref, k_ref, v_ref, gate_ref, al_ref, dt_ref, nw_ref, sin_ref, t_ref, dy_ref, mf_ref, mb_ref, _ = ins
        dgate_ref, dqkv_ref, dal_ref, ddt_ref, dnw_ref = outs
        ds_scr, = scr

        @pl.when(pl.program_id(0) == 0)
        def _():
            ds_scr[...] = jnp.zeros_like(ds_scr)
            dal_ref[...] = jnp.zeros_like(dal_ref)
            ddt_ref[...] = jnp.zeros_like(ddt_ref)
            dnw_ref[...] = jnp.zeros_like(dnw_ref)

        per_head = lambda ref: jnp.stack([ref[:, hh * LANE:(hh + 1) * LANE] for hh in range(A_HEADS)])
        head_ids = jnp.arange(A_HEADS)
        t_known, mf, mb = t_ref[0], mf_ref[...], mb_ref[...]

        def fn(s, q, k, v, z, ba, alog, dtb, nw):
            gates = _gdn_gates(ba, alog, dtb, mb)
            one = lambda s, q, k, v, z, t, h: _gdn_block(s, q, k, v, z, gates, nw, h, t, mf)[:2]
            return jax.vmap(one)(s, q, k, v, z, t_known, head_ids)

        _, vjp = jax.vjp(fn, sin_ref[0], per_head(q_ref), per_head(k_ref), per_head(v_ref), per_head(gate_ref),
                         gate_ref[:, A_WIDTH:], al_ref[...], dt_ref[...], nw_ref[...])
        ds, dq, dk, dv, dz, dba, dal, ddt, dnw = vjp((per_head(dy_ref), ds_scr[...]))
        ds_scr[...] = ds
        for hh in range(A_HEADS):
            cols = slice(hh * LANE, (hh + 1) * LANE)
            dqkv_ref[0, :, cols] = dq[hh]
            dqkv_ref[1, :, cols] = dk[hh]
            dqkv_ref[2, :, cols] = dv[hh]
            dgate_ref[:, cols] = dz[hh]
        dgate_ref[:, A_WIDTH:] = dba
        dal_ref[...] += dal
        ddt_ref[...] += ddt
        dnw_ref[...] += dnw

    rev = lambda i: nsc - 1 - i
    blk = lambda j: pl.BlockSpec((SUPER, A_WIDTH), lambda i: (rev(i), j))
    gate = pl.BlockSpec((SUPER, L_GATE), lambda i: (rev(i), L_ZA // L_GATE))
    row = pl.BlockSpec((1, LANE), lambda i: (0, 0))
    mf, mb = _gdn_masks()
    whole = lambda a: pl.BlockSpec(a.shape, lambda i: (0, 0, 0))
    return _pcall(
        core, name=name, grid=(nsc,),
        in_specs=[blk(0), blk(1), blk(2), gate, row, row, row,
                  pl.BlockSpec((1, A_HEADS, A_HEAD_DIM, A_HEAD_DIM), lambda i: (rev(i), 0, 0, 0)),
                  pl.BlockSpec((1, A_HEADS, SUPER, SUPER), lambda i: (rev(i), 0, 0, 0)),
                  blk(0), whole(mf), whole(mb), _ANY],
        out_specs=[gate, pl.BlockSpec((3, SUPER, A_WIDTH), lambda i: (0, rev(i), 0)), row, row, row],
        out_shape=[jax.ShapeDtypeStruct((t_len, L_COLS), F32), jax.ShapeDtypeStruct((3, t_len, A_WIDTH), F32)]
        + [jax.ShapeDtypeStruct((1, LANE), F32)] * 3,
        scratch_shapes=[pltpu.VMEM((A_HEADS, A_HEAD_DIM, A_HEAD_DIM), F32)],
        aliases={12: 0}, sem=("arbitrary",), rider=rider,
        args=(qkv, qkv, qkv, h, alog, dtb, nw, s_in, t_in, dycat, mf, mb, dh))


Q_BLOCKS = 4
Q_ROWS = Q_BLOCKS * BLOCK


def _swa_block(q, kp, kc, vp, vc, z, sinks, first):
    rows = B_GROUP * BLOCK
    ri = lax.broadcasted_iota(jnp.int32, (rows, 2 * BLOCK), 0)
    si = lax.broadcasted_iota(jnp.int32, (rows, 2 * BLOCK), 1)
    dist = (ri & (BLOCK - 1)) + BLOCK - si
    bias = jnp.where((dist >= 0) & (dist < WINDOW), 0.0, -jnp.inf)
    no_prev = jnp.where(first & (si[:1] < BLOCK), -jnp.inf, 0.0)
    dist_f = dist.astype(F32)
    head_of_row = lax.broadcasted_iota(jnp.int32, (rows, 1), 0) >> 7
    keys = jnp.concatenate([kp, kc], axis=0)
    vals = jnp.concatenate([vp, vc], axis=0)

    def item(b, j):
        cs = slice(j * B_HEAD_DIM, (j + 1) * B_HEAD_DIM)
        rs = slice(b * BLOCK, (b + 1) * BLOCK)
        heads = range(j * B_GROUP, (j + 1) * B_GROUP)
        qs = jnp.concatenate([q[rs, hq * B_HEAD_DIM:(hq + 1) * B_HEAD_DIM] for hq in heads], axis=0) * (
            B_HEAD_DIM ** -0.5)
        kk = keys[b * BLOCK:(b + 2) * BLOCK, cs]
        vv = vals[b * BLOCK:(b + 2) * BLOCK, cs]
        sink = jnp.concatenate([jnp.broadcast_to(sinks[:, hq:hq + 1], (BLOCK, 1)) for hq in heads], axis=0)
        slope = sum(jnp.where(head_of_row == gi, 2.0 ** (-8.0 * (hq + 1) / B_Q_HEADS), 0.0)
                    for gi, hq in enumerate(heads))
        return qs, kk, vv, sink, slope, (no_prev if b == 0 else jnp.zeros_like(no_prev))

    def attend(qs, kk, vv, sink, slope, hide):
        sc = _mm_nt(qs, kk) - slope * dist_f + (bias + hide)
        m = lax.stop_gradient(jnp.maximum(jnp.max(sc, axis=-1, keepdims=True), sink))
        p = jnp.exp(sc - m)
        inv = 1.0 / (jnp.sum(p, axis=-1, keepdims=True) + jnp.exp(sink - m))
        return _mm(p * inv, vv)

    items = [(b, j) for b in range(Q_BLOCKS) for j in range(B_KV_HEADS)]
    o = jax.vmap(attend)(*[_stack(t) for t in zip(*[item(b, j) for b, j in items])])
    rows_out = [jnp.concatenate([o[b * B_KV_HEADS + j, gi * BLOCK:(gi + 1) * BLOCK]
                                 for j in range(B_KV_HEADS) for gi in range(B_GROUP)], axis=1)
                for b in range(Q_BLOCKS)]
    return jnp.concatenate(rows_out, axis=0) * _silu(z)


def _swa_specs(idx):
    wide = lambda off: pl.BlockSpec((Q_ROWS, B_WIDTH), lambda n: (idx(n), off))
    cur = lambda off: pl.BlockSpec((Q_ROWS, LANE), lambda n: (idx(n), off))
    prev = lambda off: pl.BlockSpec((BLOCK, LANE), lambda n: (jnp.maximum(idx(n) * Q_BLOCKS - 1, 0), off))
    return [wide(L_QB // B_WIDTH), prev(L_KB // LANE), cur(L_KB // LANE), prev(L_VB // LANE), cur(L_VB // LANE),
            wide(L_ZB // B_WIDTH), pl.BlockSpec((1, LANE), lambda n: (0, 0))]


def _swa_fwd(h, sinks, *, name, rider=None):
    t_len = h.shape[0]
    nb = t_len // Q_ROWS

    def core(ins, outs, _):
        q_ref, kp_ref, kc_ref, vp_ref, vc_ref, z_ref, s_ref = ins
        outs[0][...] = _swa_block(q_ref[...], kp_ref[...], kc_ref[...], vp_ref[...], vc_ref[...], z_ref[...],
                                  s_ref[...], pl.program_id(0) == 0)

    res = _pcall(core, name=name, grid=(nb,), in_specs=_swa_specs(lambda n: n),
                 out_specs=[pl.BlockSpec((Q_ROWS, B_WIDTH), lambda n: (n, 1))],
                 out_shape=[jax.ShapeDtypeStruct((t_len, D_MODEL), F32)], sem=("parallel",), rider=rider,
                 args=(h, h, h, h, h, h, sinks))
    return res if rider else res[0]


def _swa_bwd(h, sinks, dycat, *, name, rider=None):
    t_len = h.shape[0]
    nb = t_len // Q_ROWS
    last = slice(Q_ROWS - BLOCK, Q_ROWS)

    def core(ins, outs, scr):
        q_ref, kp_ref, kc_ref, vp_ref, vc_ref, z_ref, s_ref, dy_ref = ins
        dh_ref, dsk_ref = outs
        ck_scr, cv_scr = scr
        i = pl.program_id(0)
        n = nb - 1 - i

        @pl.when(i == 0)
        def _():
            ck_scr[...] = jnp.zeros_like(ck_scr)
            cv_scr[...] = jnp.zeros_like(cv_scr)
            dsk_ref[...] = jnp.zeros_like(dsk_ref)

        fn = functools.partial(_swa_block, first=(n == 0))
        _, vjp = jax.vjp(fn, q_ref[...], kp_ref[...], kc_ref[...], vp_ref[...], vc_ref[...], z_ref[...], s_ref[...])
        dq, dkp, dkc, dvp, dvc, dz, dsk = vjp(dy_ref[...])
        dh_ref[:, L_QB:L_QB + B_WIDTH] = dq
        dh_ref[:, L_ZB:L_ZB + B_WIDTH] = dz
        dh_ref[:, L_KB:L_KB + LANE] = dkc
        dh_ref[:, L_VB:L_VB + LANE] = dvc
        dh_ref[last, L_KB:L_KB + LANE] += ck_scr[...]
        dh_ref[last, L_VB:L_VB + LANE] += cv_scr[...]
        ck_scr[...] = dkp
        cv_scr[...] = dvp
        dsk_ref[...] += dsk

    rev = lambda i: nb - 1 - i
    return _pcall(
        core, name=name, grid=(nb,),
        in_specs=_swa_specs(rev) + [pl.BlockSpec((Q_ROWS, B_WIDTH), lambda i: (rev(i), 1))],
        out_specs=[pl.BlockSpec((Q_ROWS, L_SWA), lambda i: (rev(i), 0)), pl.BlockSpec((1, LANE), lambda i: (0, 0))],
        out_shape=[jax.ShapeDtypeStruct((t_len, L_COLS), F32), jax.ShapeDtypeStruct((1, LANE), F32)],
        scratch_shapes=[pltpu.VMEM((BLOCK, LANE), F32), pltpu.VMEM((BLOCK, LANE), F32)],
        sem=("arbitrary",), rider=rider, args=(h, h, h, h, h, h, sinks, dycat))


def _out_ln_fwd(ycat, w_out, x, ln_g, ln_b, *, name, tm=512, last=False):
    t_len = x.shape[0]

    def body(y_ref, w_ref, x_ref, g_ref, b_ref, r_ref, *o_ref):
        r = DEEPNORM_ALPHA * x_ref[...] + _mm(y_ref[...], w_ref[...])
        r_ref[...] = r
        if not last:
            mu = jnp.mean(r, axis=-1, keepdims=True)
            d = r - mu
            var = jnp.mean(d * d, axis=-1, keepdims=True)
            o_ref[0][...] = d * lax.rsqrt(var + LN_EPS) * g_ref[...] + b_ref[...]

    tile = pl.BlockSpec((tm, D_MODEL), lambda i: (i, 0))
    vec = pl.BlockSpec((1, D_MODEL), lambda i: (0, 0))
    n_out = 1 if last else 2
    res = pl.pallas_call(
        body, name=name, grid=(t_len // tm,),
        in_specs=[tile, pl.BlockSpec((D_MODEL, D_MODEL), lambda i: (0, 0)), tile, vec, vec],
        out_specs=[tile] * n_out,
        out_shape=[jax.ShapeDtypeStruct((t_len, D_MODEL), F32)] * n_out,
        compiler_params=_cparams(("parallel",)),
    )(ycat, w_out, x, ln_g, ln_b)
    return (res[0], None) if last else res


def _ln_bwd(dxn, r, ln_g, *, name, tm=512, loss=None):
    t_len = r.shape[0]

    def body(*refs):
        if loss:
            t_ref, r_ref, g_ref, b_ref, dr_ref, dg_ref, db_ref, l_ref = refs
        else:
            dx_ref, r_ref, g_ref, dr_ref, dg_ref, db_ref = refs

        @pl.when(pl.program_id(0) == 0)
        def _():
            dg_ref[...] = jnp.zeros_like(dg_ref)
            db_ref[...] = jnp.zeros_like(db_ref)
            if loss:
                l_ref[...] = jnp.zeros_like(l_ref)

        rr = r_ref[...]
        mu = jnp.mean(rr, axis=-1, keepdims=True)
        d = rr - mu
        rstd = lax.rsqrt(jnp.mean(d * d, axis=-1, keepdims=True) + LN_EPS)
        xh = d * rstd
        if loss:
            e = (xh * g_ref[...] + b_ref[...]) - t_ref[...]
            dx = e * (1.0 / D_MODEL)
            l_ref[...] += jnp.sum(e * e, axis=0, keepdims=True)
        else:
            dx = dx_ref[...]
        dxh = dx * g_ref[...]
        dr_ref[...] = rstd * (dxh - jnp.mean(dxh, axis=-1, keepdims=True)
                              - xh * jnp.mean(dxh * xh, axis=-1, keepdims=True))
        dg_ref[...] += jnp.sum(dx * xh, axis=0, keepdims=True)
        db_ref[...] += jnp.sum(dx, axis=0, keepdims=True)

    tile = pl.BlockSpec((tm, D_MODEL), lambda i: (i, 0))
    vec = pl.BlockSpec((1, D_MODEL), lambda i: (0, 0))
    vec_shape = jax.ShapeDtypeStruct((1, D_MODEL), F32)
    args = (loss[0], r, ln_g, loss[1]) if loss else (dxn, r, ln_g)
    return pl.pallas_call(
        body, name=name, grid=(t_len // tm,),
        in_specs=[tile, tile, vec] + ([vec] if loss else []), out_specs=[tile, vec, vec] + ([vec] if loss else []),
        out_shape=[jax.ShapeDtypeStruct((t_len, D_MODEL), F32), vec_shape, vec_shape] + ([vec_shape] if loss else []),
        compiler_params=_cparams(("arbitrary",)),
    )(*args)


def _pad_row(v):
    return jnp.zeros((1, LANE), F32).at[0, :v.shape[0]].set(v)


_REGIONS = ((0, 1536, L_QKV), (1536, 2048, L_ZA), (2048, 2056, L_BA), (2056, 2568, L_QB), (2568, 2696, L_KB),
            (2696, 2824, L_VB), (2824, 3336, L_ZB))


def _to_layout(w_full):
    out = jnp.zeros(w_full.shape[:-1] + (L_COLS,), w_full.dtype)
    for a, b, off in _REGIONS:
        out = out.at[..., off:off + b - a].set(w_full[..., a:b])
    return out


def _from_layout(g):
    return jnp.concatenate([g[..., off:off + b - a] for a, b, off in _REGIONS], axis=-1)


def _shard_pieces(regions):
    for a, b, off in regions:
        for d in range(N_DEV):
            lo, hi = max(a, d * SHARD_COLS), min(b, (d + 1) * SHARD_COLS)
            if lo < hi:
                yield d, lo - d * SHARD_COLS, hi - d * SHARD_COLS, off + lo - a


def _as_list(r):
    return list(r) if isinstance(r, (list, tuple)) else [r]


def _gathered(shard):
    return jax.ShapeDtypeStruct((N_DEV,) + shard.shape, shard.dtype)


def _full_w_in(g_in, name):
    by_offset = sorted(_shard_pieces(_REGIONS), key=lambda p: p[3])
    tr = 256

    def body(g_ref, o_ref):
        pieces, col = [], 0
        for d, lo, hi, off in by_offset + [(None, 0, 0, L_COLS)]:
            if off > col:
                pieces.append(jnp.zeros((tr, off - col), g_ref.dtype))
            if d is not None:
                pieces.append(g_ref[d, :, lo:hi])
            col = off + hi - lo
        o_ref[...] = jnp.concatenate(pieces, axis=1)

    return pl.pallas_call(
        body, name=name, grid=(D_MODEL // tr,),
        in_specs=[pl.BlockSpec((N_DEV, tr, SHARD_COLS), lambda i: (0, i, 0))],
        out_specs=pl.BlockSpec((tr, L_COLS), lambda i: (i, 0)),
        out_shape=jax.ShapeDtypeStruct((D_MODEL, L_COLS), g_in.dtype),
        compiler_params=_cparams(("parallel",)),
    )(g_in)


def _full_conv(g_conv):
    return jnp.pad(g_conv.transpose(1, 0, 2).reshape(CONV_K, 3 * A_WIDTH), ((0, 8 - CONV_K), (0, 0)))


def _forward(x, weights, shards, small):
    a_log, dt_bias, norm_w, sinks, ln_g, ln_b = small
    tm = min(512, x.shape[0])
    saved, weights = [], [list(w) for w in weights]
    whole = lambda arrs: _Direct([(a, False, j, ()) for j, a in enumerate(arrs)], [_gathered(a) for a in arrs])
    for l in range(DEPTH):
        rider = whole(shards[l][1:]) if weights[l][1] is None else None
        h, *got = _as_list(_matmul(x, weights[l][0], form="nn", tm=tm, tn=L_COLS, tk=D_MODEL, name=f"in_proj_{l}",
                                   rider=rider))
        if rider:
            weights[l][1:] = [got[0].reshape(D_MODEL, D_MODEL), _full_conv(got[1])]
        w_in_l, w_out_l, conv_l = weights[l]
        qkv = _prep_fwd(h, conv_l, name=f"prep_fwd_{l}")
        al, dt, nw, sk = _pad_row(a_log[l]), _pad_row(dt_bias[l]), norm_w[l][None, :], _pad_row(sinks[l])
        ahead = l + 1 < DEPTH and weights[l + 1][0] is None
        rider = whole(shards[l + 1][1:]) if ahead else None
        ycat, *got = _as_list(_swa_fwd(h, sk, name=f"swa_fwd_{l}", rider=rider))
        if ahead:
            weights[l + 1][1:] = [got[0].reshape(D_MODEL, D_MODEL), _full_conv(got[1])]
        rider = whole(shards[l + 1][:1]) if ahead else None
        ycat, s_in, t_in, *got = _gdn_fwd(qkv, h, al, dt, nw, ycat, name=f"gdn_fwd_{l}", rider=rider)
        if ahead:
            weights[l + 1][0] = _full_w_in(got[0], f"w_in_columns_{l + 1}")
        r, xn = _out_ln_fwd(ycat, w_out_l, x, ln_g[l][None, :], ln_b[l][None, :], name=f"out_ln_{l}",
                            last=(l == DEPTH - 1))
        saved.append((x, h, qkv, s_in, t_in, ycat, r, al, dt, nw, sk))
        x = xn
    return x, saved, weights


def _w_in_blocks(g, name):
    rows, tr = g.shape[0], 256
    pieces = list(_shard_pieces(_REGIONS))

    def body(g_ref, o_ref):
        blocks = [[] for _ in range(N_DEV)]
        for d, lo, hi, off in pieces:
            blocks[d].append(g_ref[:, off:off + hi - lo])
        for d in range(N_DEV):
            o_ref[d] = jnp.concatenate(blocks[d], axis=1).astype(BF16)

    return pl.pallas_call(
        body, name=name, grid=(rows // tr,),
        in_specs=[pl.BlockSpec((tr, L_COLS), lambda i: (i, 0))],
        out_specs=pl.BlockSpec((N_DEV, tr, SHARD_COLS), lambda i: (0, i, 0)),
        out_shape=jax.ShapeDtypeStruct((N_DEV, rows, SHARD_COLS), BF16),
        compiler_params=_cparams(("parallel",)),
    )(g)


def _small_blocks(g):
    c_conv = g["conv_w"].reshape(CONV_K, N_DEV, CONV_SHARD_COLS).transpose(1, 0, 2)
    c_small = [jnp.broadcast_to(g[n][None], (N_DEV,) + g[n].shape) for n, _ in SMALL_SIZES]
    return _pack_small(c_conv, c_small)


def _contributions(g):
    c_out = g["w_out"].astype(BF16).reshape(N_DEV, OUT_SHARD_ROWS, D_MODEL)
    return _w_in_blocks(g["w_in_cols"], name="w_in_grad_blocks_above"), c_out, _small_blocks(g)


def _backward_layer(l, dx, saved_l, weights_l, ln_g_l, above=None, loss=None):
    x_in, h, qkv, s_in, t_in, ycat, r, al, dt, nw, sk = saved_l
    w_in_l, w_out_l, conv_l = weights_l
    tm = min(512, x_in.shape[0])
    dr, d_lng, d_lnb, *loss_lanes = _ln_bwd(dx, r, ln_g_l[None, :], name=f"ln_bwd_{l}", loss=loss)
    big = min(1024, x_in.shape[0])
    dycat = _matmul(dr, w_out_l, form="nt", tm=big, tn=D_MODEL, tk=D_MODEL, name=f"out_proj_dx_{l}")
    d_wout = _matmul(ycat, dr, form="tn", tm=D_MODEL, tn=D_MODEL, tk=big, name=f"out_proj_dw_{l}")
    rider, p_in, p_out, p_small = None, None, None, None
    recv = lambda c: jax.ShapeDtypeStruct((DEPTH,) + c.shape, c.dtype)
    if above:
        c_out = d_wout.astype(BF16).reshape(N_DEV, OUT_SHARD_ROWS, D_MODEL)
        rider = _Direct([(above[1], True, 0, (l + 1,)), (above[2], True, 1, (l + 1,)), (c_out, True, 0, (l,))],
                        [recv(above[1]), recv(above[2])])
    dh, d_sk, *got = _swa_bwd(h, sk, dycat, name=f"swa_bwd_{l}", rider=rider)
    if above:
        p_out, p_small = got
        rider = _Direct([(above[0], True, 0, (l + 1,))], [recv(above[0])])
    dh, dqkv_n, d_al, d_dt, d_nw, *got = _gdn_bwd(qkv, h, al, dt, nw, s_in, t_in, dycat, dh,
                                                  name=f"gdn_bwd_{l}", rider=rider)
    dh, d_conv = _prep_bwd(h, conv_l, dqkv_n, dh, name=f"prep_bwd_{l}")
    grads = dict(w_out=d_wout, conv_w=d_conv[:CONV_K], a_log=d_al[0, :A_HEADS], dt_bias=d_dt[0, :A_HEADS],
                 norm_w=d_nw[0], sinks=d_sk[0, :B_Q_HEADS], ln_g=d_lng[0], ln_b=d_lnb[0])
    dw = functools.partial(_matmul, x_in, dh, form="tn", tm=512, tn=L_COLS, tk=tm)
    if not above:
        grads["w_in_cols"] = dw(name=f"in_proj_dw_{l}")
    else:
        p_in, = got
        half = D_MODEL // 2
        top = dw(name=f"in_proj_dw_top_{l}", a_cols=(0, half))
        blocks = _w_in_blocks(top, name=f"w_in_grad_blocks_top_{l}")
        rider = _Direct([(blocks, True, 0, (l,), (pl.ds(0, half),))], [p_in])
        bottom, p_in = dw(name=f"in_proj_dw_bottom_{l}", a_cols=(half, half), rider=rider)
        blocks = _w_in_blocks(bottom, name=f"w_in_grad_blocks_bottom_{l}")
        rider = _Direct([(blocks, True, 0, (l,), (pl.ds(half, half),)),
                         (_small_blocks(grads), True, 1, (l,))], [p_in, p_small])
    dx, *got = _as_list(_matmul(dh, w_in_l, form="nt", tm=tm, tn=D_MODEL, tk=L_COLS, name=f"in_proj_dx_{l}",
                                add=dr, add_scale=DEEPNORM_ALPHA, rider=rider))
    bufs = (got[0], p_out, got[1]) if above else None
    return dx, grads, bufs, (loss_lanes[0] if loss else None)


def _all_gather(shards, *, name):
    n_arr = len(shards)

    def body(*refs):
        x_refs, out_refs = refs[:n_arr], refs[n_arr:2 * n_arr]
        send_sems, recv_sems, local_sems = refs[2 * n_arr:]
        x, y, c = _me()
        me, sibling = (x, y, c), (x, y, 1 - c)
        chips = [(1 - x, y), (x, 1 - y), (1 - x, 1 - y)]

        def copy(a, k, block, to, src=None):
            dst = out_refs[a].at[_flat_id(block)]
            return _remote(dst if src is None else src, dst, send_sems.at[a, k], recv_sems.at[a, k], to)

        mine = [pltpu.make_async_copy(x_refs[a], out_refs[a].at[_flat_id(me)], local_sems.at[a])
                for a in range(n_arr)]
        for cp in mine:
            cp.start()
        first = []
        for a in range(n_arr):
            first.append(copy(a, 0, me, sibling, src=x_refs[a]))
            first += [copy(a, 1 + j, me, (*chip, c), src=x_refs[a]) for j, chip in enumerate(chips)]
        for cp in first:
            cp.start()
        passed = []
        for j, chip in enumerate(chips):
            for a in range(n_arr):
                copy(a, 1 + j, (*chip, c), me).wait_recv()
                fwd = copy(a, 4 + j, (*chip, c), sibling)
                fwd.start()
                passed.append(fwd)
        for a in range(n_arr):
            copy(a, 0, sibling, me).wait_recv()
            for j, chip in enumerate(chips):
                copy(a, 4 + j, (*chip, 1 - c), me).wait_recv()
        for cp in first + passed:
            cp.wait_send()
        for cp in mine:
            cp.wait()

    return pl.pallas_call(
        body, name=name, in_specs=[_ANY] * n_arr, out_specs=[_ANY] * n_arr,
        out_shape=[jax.ShapeDtypeStruct((N_DEV,) + s.shape, s.dtype) for s in shards],
        scratch_shapes=[pltpu.SemaphoreType.DMA((n_arr, N_DEV - 1)), pltpu.SemaphoreType.DMA((n_arr, N_DEV - 1)),
                        pltpu.SemaphoreType.DMA((n_arr,))],
    )(*shards)


def _adamw(parts, w, m, v, *, tr, name):
    depth, rows, cols = w.shape
    c1 = 1.0 - ADAM_B1 ** ADAM_STEP
    c2 = 1.0 - ADAM_B2 ** ADAM_STEP

    def body(g_ref, w_ref, m_ref, v_ref, go_ref, d_ref, mo_ref, vo_ref):
        g = g_ref[0, 0].astype(F32)
        for s in range(1, N_DEV):
            g = g + g_ref[0, s].astype(F32)
        m_new = ADAM_B1 * m_ref[0] + (1.0 - ADAM_B1) * g
        v_new = ADAM_B2 * v_ref[0] + (1.0 - ADAM_B2) * (g * g)
        go_ref[0] = g
        mo_ref[0] = m_new
        vo_ref[0] = v_new
        d_ref[0] = -ADAM_LR * ((m_new / c1) / (jnp.sqrt(v_new / c2) + ADAM_EPS) + ADAM_WD * w_ref[0])

    tile = pl.BlockSpec((1, tr, cols), lambda l, i: (l, i, 0))
    return pl.pallas_call(
        body, name=name, grid=(depth, rows // tr),
        in_specs=[pl.BlockSpec((1, N_DEV, tr, cols), lambda l, i: (l, 0, i, 0)), tile, tile, tile],
        out_specs=[tile] * 4, out_shape=[jax.ShapeDtypeStruct(w.shape, F32)] * 4,
        compiler_params=_cparams(("parallel", "parallel")),
    )(parts, w, m, v)


def _pack_small(conv, small):
    lead = conv.shape[:-2]
    flat = jnp.concatenate([conv.reshape(lead + (CS_CONV,))] + list(small), axis=-1)
    pad = CS_ROWS * LANE - flat.shape[-1]
    flat = jnp.concatenate([flat, jnp.zeros(lead + (pad,), F32)], axis=-1)
    return flat.reshape(lead + (CS_ROWS, LANE))


def _unpack_small(p):
    flat = p.reshape(DEPTH, CS_ROWS * LANE)
    conv = flat[:, :CS_CONV].reshape(DEPTH, CONV_K, CONV_SHARD_COLS)
    small, off = [], CS_CONV
    for _, n in SMALL_SIZES:
        small.append(flat[:, off:off + n])
        off += n
    return conv, small


def kernel(x, w_in, conv_w, a_log, dt_bias, norm_w, sinks, w_out, ln_g, ln_b, loss_target, m_w_in, m_conv_w, m_a_log, m_dt_bias, m_norm_w, m_sinks, m_w_out, m_ln_g, m_ln_b, v_w_in, v_conv_w, v_a_log, v_dt_bias, v_norm_w, v_sinks, v_w_out, v_ln_g, v_ln_b):
    small = [a_log, dt_bias, norm_w, sinks, ln_g, ln_b]
    shards = [[w_in[l].astype(BF16), w_out[l].astype(BF16), conv_w[l]] for l in range(DEPTH)]
    g_in0, = _all_gather(shards[0][:1], name="weights_all_gather_0")
    weights = [[_full_w_in(g_in0, "w_in_columns_0"), None, None]] + [[None, None, None]] * (DEPTH - 1)

    _, saved, weights = _forward(x[0], weights, shards, small)
    dx, g1, _, loss_lanes = _backward_layer(1, None, saved[1], weights[1], ln_g[1],
                                            loss=(loss_target[0], ln_b[1][None, :]))
    loss = lax.psum(0.5 * jnp.sum(loss_lanes) * (1.0 / D_MODEL), ("x", "y", "c"))
    dx, _, (p_in, p_out, p_small), _ = _backward_layer(0, dx, saved[0], weights[0], ln_g[0],
                                                       above=_contributions(g1))

    o_in = _adamw(p_in, w_in, m_w_in, v_w_in, tr=256, name="adamw_w_in")
    o_out = _adamw(p_out, w_out, m_w_out, v_w_out, tr=OUT_SHARD_ROWS, name="adamw_w_out")
    o_small = _adamw(p_small, _pack_small(conv_w, small),
                     _pack_small(m_conv_w, [m_a_log, m_dt_bias, m_norm_w, m_sinks, m_ln_g, m_ln_b]),
                     _pack_small(v_conv_w, [v_a_log, v_dt_bias, v_norm_w, v_sinks, v_ln_g, v_ln_b]),
                     tr=CS_ROWS, name="adamw_small")
    outs = []
    for k in range(4):
        cv, sm = _unpack_small(o_small[k])
        outs += [o_in[k], cv, sm[0], sm[1], sm[2], sm[3], o_out[k], sm[4], sm[5]]
    return (loss, dx[None], *outs)
```

```python
import functools

import jax
import jax.numpy as jnp
from jax import lax
from jax.experimental import pallas as pl
from jax.experimental.pallas import tpu as pltpu

F32 = jnp.float32
BF16 = jnp.bfloat16
MM_DTYPE = BF16

N_DEV = 8
D_MODEL = 1024
DEPTH = 2
A_HEADS = 4
A_HEAD_DIM = 128
A_WIDTH = 512
CONV_K = 4
CHUNK = 64
SUPER = 256
NEWTON_STEPS = 1
B_Q_HEADS = 8
B_KV_HEADS = 2
B_HEAD_DIM = 64
B_GROUP = 4
B_WIDTH = 512
WINDOW = 128
BLOCK = 128
IN_COLS = 3336
SHARD_COLS = IN_COLS // N_DEV
OUT_SHARD_ROWS = D_MODEL // N_DEV
CONV_SHARD_COLS = 3 * A_WIDTH // N_DEV
DEEPNORM_ALPHA = (2 * DEPTH) ** 0.25
LN_EPS = 1e-5
RMS_EPS = 1e-6
L2_EPS = 1e-6
ADAM_LR, ADAM_B1, ADAM_B2, ADAM_EPS, ADAM_WD, ADAM_STEP = 0.001, 0.9, 0.999, 1e-08, 0.01, 10

LANE = 128
L_QB, L_ZB, L_KB, L_VB, L_ZA, L_BA, L_QKV = 0, 512, 1024, 1152, 1280, 1792, 1920
L_SWA = 1280
L_GATE = 640
L_COLS = 3456
SMALL_SIZES = (("a_log", 4), ("dt_bias", 4), ("norm_w", 128), ("sinks", 8), ("ln_g", 1024), ("ln_b", 1024))
CS_CONV = CONV_K * CONV_SHARD_COLS
CS_ROWS = 24
VMEM_LIMIT = 48 * 1024 * 1024


def _cparams(sem=None):
    return pltpu.CompilerParams(dimension_semantics=sem, vmem_limit_bytes=VMEM_LIMIT)


def _mm(a, b):
    return jnp.dot(a.astype(MM_DTYPE), b.astype(MM_DTYPE), preferred_element_type=F32)


def _mm_nt(a, b):
    return lax.dot_general(a.astype(MM_DTYPE), b.astype(MM_DTYPE), (((1,), (1,)), ((), ())),
                           preferred_element_type=F32)


def _mm_tn(a, b):
    return lax.dot_general(a.astype(MM_DTYPE), b.astype(MM_DTYPE), (((0,), (0,)), ((), ())),
                           preferred_element_type=F32)


def _split(a):
    hi = a.astype(BF16)
    return hi, (a - hi.astype(F32)).astype(BF16)


def _hp(a2, b2):
    d = lambda p, q: jnp.dot(p, q, preferred_element_type=F32)
    return d(a2[0], b2[0]) + (d(a2[0], b2[1]) + d(a2[1], b2[0]))


def _silu(x):
    return x * jax.nn.sigmoid(x)


@jax.custom_vjp
def _stack(parts):
    return jnp.stack(parts)


_stack.defvjp(lambda parts: (jnp.stack(parts), None), lambda _, g: (tuple(g[i] for i in range(g.shape[0])),))


def _softplus(x):
    return jnp.maximum(x, 0.0) + jnp.log1p(jnp.exp(-jnp.abs(x)))


_ANY = pl.BlockSpec(memory_space=pl.ANY)


def _me():
    return lax.axis_index("x"), lax.axis_index("y"), lax.axis_index("c")


def _flat_id(pos):
    return 4 * pos[0] + 2 * pos[1] + pos[2]


def _remote(src, dst, send_sem, recv_sem, to):
    return pltpu.make_async_remote_copy(src_ref=src, dst_ref=dst, send_sem=send_sem, recv_sem=recv_sem,
                                        device_id=to, device_id_type=pl.DeviceIdType.MESH)


class _Direct:
    def __init__(self, items, bufs):
        self.items, self.bufs = list(items), list(bufs)
        self.n_src, self.n_buf = len(self.items), len(self.bufs)
        self.old = [j for j, b in enumerate(self.bufs) if not isinstance(b, jax.ShapeDtypeStruct)]
        self.args = [it[0] for it in self.items] + [self.bufs[j] for j in self.old]
        self.out_shape = [jax.ShapeDtypeStruct(b.shape, b.dtype) for b in self.bufs]
        self.scratch = [pltpu.SemaphoreType.DMA((self.n_src, N_DEV - 1)),
                        pltpu.SemaphoreType.DMA((self.n_src, N_DEV - 1)), pltpu.SemaphoreType.DMA((self.n_src,))]

    def aliases(self, in_base, out_base):
        return {in_base + self.n_src + pos: out_base + j for pos, j in enumerate(self.old)}

    def copies(self, in_refs, out_refs, sems):
        send_sems, recv_sems, local_sems = sems
        x, y, c = _me()
        me = _flat_id((x, y, c))
        peers = [(x ^ ((rel >> 2) & 1), y ^ ((rel >> 1) & 1), c ^ (rel & 1)) for rel in range(1, N_DEV)]
        local, sends, recvs = [], [], []
        for a, (_, per_dest, j, prefix, *rest) in enumerate(self.items):
            src = lambda d: in_refs[a].at[d] if per_dest else in_refs[a]
            dst = lambda s: out_refs[j].at[tuple(prefix) + (s,) + tuple(rest[0] if rest else ())]
            local.append(pltpu.make_async_copy(src(me), dst(me), local_sems.at[a]))
            for k, peer in enumerate(peers):
                pid = _flat_id(peer)
                sends.append(_remote(src(pid), dst(me), send_sems.at[a, k], recv_sems.at[a, k], peer))
                recvs.append(_remote(src(pid), dst(pid), send_sems.at[a, k], recv_sems.at[a, k], peer))
        return local, sends, recvs

    def start(self, in_refs, out_refs, sems):
        local, sends, _ = self.copies(in_refs, out_refs, sems)
        for cp in local + sends:
            cp.start()

    def wait(self, in_refs, out_refs, sems):
        local, sends, recvs = self.copies(in_refs, out_refs, sems)
        for cp in recvs:
            cp.wait_recv()
        for cp in sends:
            cp.wait_send()
        for cp in local:
            cp.wait()


def _pcall(core, *, name, grid, in_specs, out_specs, out_shape, args, sem, scratch_shapes=(), aliases=None,
           rider=None):
    n_in, n_out, n_scr = len(in_specs), len(out_specs), len(scratch_shapes)
    n_rin, n_rout = (len(rider.args), rider.n_buf) if rider else (0, 0)

    def body(*refs):
        ins, outs = refs[:n_in], refs[n_in + n_rin:n_in + n_rin + n_out]
        scr = refs[n_in + n_rin + n_out + n_rout:n_in + n_rin + n_out + n_rout + n_scr]
        if rider:
            r_refs = (refs[n_in:n_in + rider.n_src], refs[n_in + n_rin + n_out:n_in + n_rin + n_out + n_rout],
                      refs[n_in + n_rin + n_out + n_rout + n_scr:])
            ids = [pl.program_id(d) for d in range(len(grid))]
            first = functools.reduce(lambda p, q: p & q, [i == 0 for i in ids])
            last = functools.reduce(lambda p, q: p & q, [i == g - 1 for i, g in zip(ids, grid)])
            pl.when(first)(lambda: rider.start(*r_refs))
        core(ins, outs, scr)
        if rider:
            pl.when(last)(lambda: rider.wait(*r_refs))

    aliases = dict(aliases or {})
    if rider:
        sem = ("arbitrary",) * len(grid)
        aliases.update(rider.aliases(n_in, n_out))
    return pl.pallas_call(
        body, name=name, grid=grid, in_specs=list(in_specs) + [_ANY] * n_rin,
        out_specs=list(out_specs) + [_ANY] * n_rout,
        out_shape=list(out_shape) + (rider.out_shape if rider else []),
        scratch_shapes=list(scratch_shapes) + (rider.scratch if rider else []),
        input_output_aliases=aliases, compiler_params=_cparams(sem),
    )(*args, *(rider.args if rider else []))


def _exchange(direct, *, name):
    n_in = len(direct.args)

    def body(*refs):
        r_refs = refs[:direct.n_src], refs[n_in:n_in + direct.n_buf], refs[n_in + direct.n_buf:]
        direct.start(*r_refs)
        direct.wait(*r_refs)

    return pl.pallas_call(
        body, name=name, in_specs=[_ANY] * n_in, out_specs=[_ANY] * direct.n_buf, out_shape=direct.out_shape,
        input_output_aliases=direct.aliases(0, 0), scratch_shapes=direct.scratch,
    )(*direct.args)


def _matmul(a, b, *, form, tm, tn, tk, name, add=None, add_scale=1.0, extra=None, rider=None, a_cols=None):
    if form == "nn":
        (m, kk), n = a.shape, b.shape[1]
        a_spec = pl.BlockSpec((tm, tk), lambda i, j, k: (i, k))
        b_spec = pl.BlockSpec((tk, tn), lambda i, j, k: (k, j))
        dn = (((1,), (0,)), ((), ()))
    elif form == "nt":
        (m, kk), n = a.shape, b.shape[0]
        a_spec = pl.BlockSpec((tm, tk), lambda i, j, k: (i, k))
        b_spec = pl.BlockSpec((tn, tk), lambda i, j, k: (j, k))
        dn = (((1,), (1,)), ((), ()))
    else:
        kk, n = a.shape[0], b.shape[1]
        m0, m = a_cols or (0, a.shape[1])
        assert m0 % tm == 0
        a_spec = pl.BlockSpec((tk, tm), lambda i, j, k: (k, i + m0 // tm))
        b_spec = pl.BlockSpec((tk, tn), lambda i, j, k: (k, j))
        dn = (((0,), (0,)), ((), ()))
    assert m % tm == 0 and n % tn == 0 and kk % tk == 0, (name, m, n, kk)
    has_add, has_extra = add is not None, extra is not None

    def core(ins, outs, _):
        a_ref, b_ref = ins[:2]
        o_ref = outs[0]
        rest = ins[2:]
        k = pl.program_id(2)
        p = lax.dot_general(a_ref[...].astype(MM_DTYPE), b_ref[...].astype(MM_DTYPE), dn,
                            preferred_element_type=F32)

        @pl.when(k == 0)
        def _():
            first = p
            pos = 0
            if has_extra:
                first = first + _mm_nt(rest[0][...], rest[1][...])
                pos = 2
            if has_add:
                first = first + add_scale * rest[pos][...]
            o_ref[...] = first

        @pl.when(k > 0)
        def _():
            o_ref[...] += p

    in_specs = [a_spec, b_spec]
    args = [a, b]
    if has_extra:
        a2, b2, idx = extra
        in_specs += [pl.BlockSpec((tm, LANE), lambda i, j, k: (i, 0)),
                     pl.BlockSpec((tn, LANE), lambda i, j, k: (j, idx))]
        args += [a2, b2]
    if has_add:
        in_specs.append(pl.BlockSpec((tm, tn), lambda i, j, k: (i, j)))
        args.append(add)
    res = _pcall(core, name=name, grid=(m // tm, n // tn, kk // tk), in_specs=in_specs,
                 out_specs=[pl.BlockSpec((tm, tn), lambda i, j, k: (i, j))],
                 out_shape=[jax.ShapeDtypeStruct((m, n), F32)], args=args,
                 sem=("parallel", "parallel", "arbitrary"), rider=rider)
    return res if rider else res[0]


ZERO_TAIL = 8


def _with_tail(x):
    return jnp.concatenate([x, jnp.zeros((ZERO_TAIL,) + x.shape[1:], x.dtype)], axis=0)


def _shift_down(x, k):
    return pltpu.roll(x, k, 0)


def _shift_up(x, k):
    return pltpu.roll(x, x.shape[0] - k, 0)


def _conv_slab(x, w):
    return w[3:4] * x + w[2:3] * _shift_down(x, 1) + w[1:2] * _shift_down(x, 2) + w[0:1] * _shift_down(x, 3)


def _prep_fwd(h, conv_w, *, name):
    t_len = h.shape[0]

    def body(x_ref, w_ref, o_ref):
        s = pl.program_id(0)
        y = _silu(_conv_slab(_with_tail(x_ref[...]), w_ref[...])[:t_len])
        rs = lax.rsqrt(jnp.sum(y * y, axis=-1, keepdims=True) + L2_EPS)
        scale = jnp.where(s < A_HEADS, A_HEAD_DIM ** -0.5, 1.0)
        o_ref[...] = jnp.where(s < 2 * A_HEADS, y * rs * scale, y)

    return pl.pallas_call(
        body, name=name, grid=(12,),
        in_specs=[pl.BlockSpec((t_len, LANE), lambda s: (0, L_QKV // LANE + s)),
                  pl.BlockSpec((8, LANE), lambda s: (0, s))],
        out_specs=pl.BlockSpec((t_len, LANE), lambda s: (0, s)),
        out_shape=jax.ShapeDtypeStruct((t_len, 3 * A_WIDTH), F32),
        compiler_params=_cparams(("parallel",)),
    )(h, conv_w)


def _prep_bwd(h, conv_w, d_out, dh, *, name):
    t_len = h.shape[0]

    def body(x_ref, w_ref, g_ref, dh_in, dx_ref, dw_ref):
        del dh_in
        s = pl.program_id(0)
        x = _with_tail(x_ref[...])
        g = _with_tail(g_ref[0])
        w = w_ref[...]
        c = _conv_slab(x, w)
        sg = jax.nn.sigmoid(c)
        y = c * sg
        rs = lax.rsqrt(jnp.sum(y * y, axis=-1, keepdims=True) + L2_EPS)
        scale = jnp.where(s < A_HEADS, A_HEAD_DIM ** -0.5, 1.0)
        dy_n = scale * (rs * g - y * (rs * rs * rs) * jnp.sum(g * y, axis=-1, keepdims=True))
        dy = jnp.where(s < 2 * A_HEADS, dy_n, g)
        dc = dy * (sg * (1.0 + c * (1.0 - sg)))
        dx = w[3:4] * dc + w[2:3] * _shift_up(dc, 1) + w[1:2] * _shift_up(dc, 2) + w[0:1] * _shift_up(dc, 3)
        dx_ref[...] = dx[:t_len]
        dws = [jnp.sum(dc * _shift_down(x, 3 - j), axis=0, keepdims=True) if j < 3
               else jnp.sum(dc * x, axis=0, keepdims=True) for j in range(CONV_K)]
        dw_ref[...] = jnp.concatenate(dws + [jnp.zeros((8 - CONV_K, LANE), F32)], axis=0)

    slab = pl.BlockSpec((t_len, LANE), lambda s: (0, L_QKV // LANE + s))
    return pl.pallas_call(
        body, name=name, grid=(12,),
        in_specs=[slab, pl.BlockSpec((8, LANE), lambda s: (0, s)),
                  pl.BlockSpec((1, t_len, LANE), lambda s: (s // A_HEADS, 0, s % A_HEADS)), _ANY],
        out_specs=[slab, pl.BlockSpec((8, LANE), lambda s: (0, s))],
        out_shape=[jax.ShapeDtypeStruct((t_len, L_COLS), F32), jax.ShapeDtypeStruct((8, 3 * A_WIDTH), F32)],
        input_output_aliases={3: 0},
        compiler_params=_cparams(("parallel",)),
    )(h, conv_w, d_out, dh)


N_LEVELS = 5
MF_TRIL, MF_STRIL, MF_DIAG8, MF_LOW16, MF_EYE = 0, 1, 2, 3, 3 + N_LEVELS
MB_CUM, MB_CUM_T, MB_TOT = 0, 1, 2


def _gdn_masks():
    r = lax.broadcasted_iota(jnp.int32, (SUPER, SUPER), 0)
    c = lax.broadcasted_iota(jnp.int32, (SUPER, SUPER), 1)
    same = lambda shift: (r >> shift) == (c >> shift)
    ninf = lambda m: jnp.where(m, 0.0, -jnp.inf).astype(F32)
    one = lambda m: m.astype(F32)
    mf = jnp.stack([ninf(r >= c), ninf(r > c), one(same(3))]
                   + [one(same(4 + lv) & jnp.logical_not(same(3 + lv))) for lv in range(N_LEVELS)] + [one(r == c)])
    mb = jnp.stack([one(r >= c), one(r <= c), jnp.ones((SUPER, SUPER), F32)]).astype(BF16)
    return mf, mb


def _tri_inv_impl(a, mf):
    d = lambda p, q: jnp.dot(p.astype(BF16), q.astype(BF16), preferred_element_type=F32)
    dd = lambda p, q: jnp.dot(p, q, preferred_element_type=F32)
    eye = mf[MF_EYE]
    a0 = a * mf[MF_DIAG8]
    a2 = d(a0, a0)
    a4 = d(a2, a2)
    t = d(d(eye - a0, eye + a2), eye + a4)
    for level in range(N_LEVELS):
        t = t - d(d(t, a * mf[MF_LOW16 + level]), t)
    a_hi, a_lo = _split(a)
    for _ in range(NEWTON_STEPS):
        t_hi, t_lo = _split(t)
        resid = (eye - t) - (dd(a_hi, t_hi) + (dd(a_hi, t_lo) + dd(a_lo, t_hi)))
        r_hi, r_lo = _split(resid)
        t = t + (dd(t_hi, r_hi) + dd(t_hi, r_lo))
    return t


@jax.custom_vjp
def _wy_apply(a, rhs, t):
    return _mm(t, rhs)


def _wy_apply_fwd(a, rhs, t):
    x = _mm(t, rhs)
    return x, (t, x)


def _wy_apply_bwd(res, dx):
    t, x = res
    d_rhs = _mm_tn(t, dx)
    return -_mm_nt(d_rhs, x), d_rhs, jnp.zeros_like(t)


_wy_apply.defvjp(_wy_apply_fwd, _wy_apply_bwd)


@functools.partial(jax.custom_vjp, nondiff_argnums=(1,))
def _lane_roll(x, shift):
    return pltpu.roll(x, shift % LANE, 1)


_lane_roll.defvjp(lambda x, shift: (_lane_roll(x, shift), None), lambda shift, _, g: (_lane_roll(g, -shift),))


def _mask_times_lanes(x, mask):
    lane = lax.broadcasted_iota(jnp.int32, (1, LANE), 1)
    x = jnp.where(lane < A_HEADS, x, 0.0)
    x1 = x.astype(BF16).astype(F32)
    x2 = (x - x1).astype(BF16).astype(F32)
    x3 = (x - x1 - x2).astype(BF16).astype(F32)
    pieces = x1 + pltpu.roll(x2, A_HEADS, 1) + pltpu.roll(x3, 2 * A_HEADS, 1)
    res = jnp.dot(mask, pieces.astype(BF16), preferred_element_type=F32)
    return res + pltpu.roll(res, LANE - A_HEADS, 1) + pltpu.roll(res, LANE - 2 * A_HEADS, 1)


@jax.custom_vjp
def _chunk_sums(g, mb):
    return _mask_times_lanes(g, mb[MB_CUM]), _mask_times_lanes(g, mb[MB_TOT])


def _chunk_sums_fwd(g, mb):
    return _chunk_sums(g, mb), mb


def _chunk_sums_bwd(mb, d):
    lane = lax.broadcasted_iota(jnp.int32, (1, LANE), 1)
    dg = _mask_times_lanes(d[0], mb[MB_CUM_T]) + _mask_times_lanes(d[1], mb[MB_TOT])
    return jnp.where(lane < A_HEADS, dg, 0.0), jnp.zeros_like(mb)


_chunk_sums.defvjp(_chunk_sums_fwd, _chunk_sums_bwd)


def _gdn_gates(ba, alog, dtb, mb):
    beta = jax.nn.sigmoid(ba)
    g = -jnp.exp(alog) * _softplus(_lane_roll(ba, -A_HEADS) + dtb)
    gc, gl = _chunk_sums(g, mb)
    return beta, gc, gl, gc.T


def _gdn_block(s, q, k, v, z, gates, nw, h, t_known, mf):
    n = q.shape[0]
    beta_all, gc_all, gl_all, gct_all = gates
    lane = lax.broadcasted_iota(jnp.int32, (1, LANE), 1)
    sub = lax.broadcasted_iota(jnp.int32, (LANE, 1), 0)
    col = lambda x: jnp.sum(jnp.where(lane == h, x, 0.0), axis=1, keepdims=True)
    wide = lambda c: jnp.broadcast_to(c, (n, LANE))
    gc, gl = col(gc_all), col(gl_all)
    gc_row = jnp.sum(jnp.where(sub == h, gct_all, 0.0), axis=0, keepdims=True)
    beta_w, eg_w = wide(col(beta_all)), wide(jnp.exp(gc))
    diff = gc - gc_row
    decay = jnp.exp(diff + mf[MF_TRIL])
    kb = k * beta_w
    a_mat = _mm_nt(kb, k) * jnp.exp(diff + mf[MF_STRIL])
    rhs = jnp.concatenate([v * beta_w, kb * eg_w], axis=1)
    if t_known is None:
        t_mat = _tri_inv_impl(a_mat, mf)
        uw = _mm(t_mat, rhs)
    else:
        t_mat = t_known
        uw = _wy_apply(a_mat, rhs, t_known)
    u, w = uw[:, :LANE], uw[:, LANE:]
    qk = _mm_nt(q, k) * decay
    q_dec = q * eg_w
    k_dec = k * wide(jnp.exp(gl - gc))
    v_new = u - _mm(w, s)
    o = _mm(q_dec, s) + _mm(qk, v_new)
    s = s * jnp.exp(gl[0:1]) + _mm_tn(k_dec, v_new)
    o = o * lax.rsqrt(jnp.mean(o * o, axis=-1, keepdims=True) + RMS_EPS) * nw
    return o * _silu(z), s, t_mat


def _gdn_fwd(qkv, h, alog, dtb, nw, ycat, *, name, rider=None):
    t_len = qkv.shape[0]
    nsc = t_len // SUPER

    def core(ins, outs, scr):
        q_ref, k_ref, v_ref, gate_ref, al_ref, dt_ref, nw_ref, mf_ref, mb_ref, _ = ins
        y_ref, sin_ref, t_ref = outs
        s_scr, = scr

        @pl.when(pl.program_id(0) == 0)
        def _():
            s_scr[...] = jnp.zeros_like(s_scr)

        per_head = lambda ref: jnp.stack([ref[:, hh * LANE:(hh + 1) * LANE] for hh in range(A_HEADS)])
        states = s_scr[...]
        gates = _gdn_gates(gate_ref[:, A_WIDTH:], al_ref[...], dt_ref[...], mb_ref[...])
        fn = jax.vmap(_gdn_block, in_axes=(0, 0, 0, 0, 0, None, None, 0, None, None))
        y, s_new, t_mat = fn(states, per_head(q_ref), per_head(k_ref), per_head(v_ref), per_head(gate_ref),
                             gates, nw_ref[...], jnp.arange(A_HEADS), None, mf_ref[...])
        sin_ref[0] = states
        t_ref[0] = t_mat
        s_scr[...] = s_new
        for hh in range(A_HEADS):
            y_ref[:, hh * LANE:(hh + 1) * LANE] = y[hh]

    blk = lambda j: pl.BlockSpec((SUPER, A_WIDTH), lambda sc: (sc, j))
    row = pl.BlockSpec((1, LANE), lambda sc: (0, 0))
    mf, mb = _gdn_masks()
    whole = lambda a: pl.BlockSpec(a.shape, lambda sc: (0, 0, 0))
    return _pcall(
        core, name=name, grid=(nsc,),
        in_specs=[blk(0), blk(1), blk(2), pl.BlockSpec((SUPER, L_GATE), lambda sc: (sc, L_ZA // L_GATE)),
                  row, row, row, whole(mf), whole(mb), _ANY],
        out_specs=[blk(0),
                   pl.BlockSpec((1, A_HEADS, A_HEAD_DIM, A_HEAD_DIM), lambda sc: (sc, 0, 0, 0)),
                   pl.BlockSpec((1, A_HEADS, SUPER, SUPER), lambda sc: (sc, 0, 0, 0))],
        out_shape=[jax.ShapeDtypeStruct((t_len, D_MODEL), F32),
                   jax.ShapeDtypeStruct((nsc, A_HEADS, A_HEAD_DIM, A_HEAD_DIM), F32),
                   jax.ShapeDtypeStruct((nsc, A_HEADS, SUPER, SUPER), F32)],
        scratch_shapes=[pltpu.VMEM((A_HEADS, A_HEAD_DIM, A_HEAD_DIM), F32)],
        aliases={9: 0}, sem=("arbitrary",), rider=rider,
        args=(qkv, qkv, qkv, h, alog, dtb, nw, mf, mb, ycat))


def _gdn_bwd(qkv, h, alog, dtb, nw, s_in, t_in, dycat, dh, *, name, rider=None):
    t_len = qkv.shape[0]
    nsc = t_len // SUPER

    def core(ins, outs, scr):
        q_ref, k_ref, v_ref, gate_ref, al_ref, dt_ref, nw_ref, sin_ref, t_ref, dy_ref, mf_ref, mb_ref, _ = ins
        dgate_ref, dqkv_ref, dal_ref, ddt_ref, dnw_ref = outs
        ds_scr, = scr

        @pl.when(pl.program_id(0) == 0)
        def _():
            ds_scr[...] = jnp.zeros_like(ds_scr)
            dal_ref[...] = jnp.zeros_like(dal_ref)
            ddt_ref[...] = jnp.zeros_like(ddt_ref)
            dnw_ref[...] = jnp.zeros_like(dnw_ref)

        per_head = lambda ref: jnp.stack([ref[:, hh * LANE:(hh + 1) * LANE] for hh in range(A_HEADS)])
        head_ids = jnp.arange(A_HEADS)
        t_known, mf, mb = t_ref[0], mf_ref[...], mb_ref[...]

        def fn(s, q, k, v, z, ba, alog, dtb, nw):
            gates = _gdn_gates(ba, alog, dtb, mb)
            one = lambda s, q, k, v, z, t, h: _gdn_block(s, q, k, v, z, gates, nw, h, t, mf)[:2]
            return jax.vmap(one)(s, q, k, v, z, t_known, head_ids)

        _, vjp = jax.vjp(fn, sin_ref[0], per_head(q_ref), per_head(k_ref), per_head(v_ref), per_head(gate_ref),
                         gate_ref[:, A_WIDTH:], al_ref[...], dt_ref[...], nw_ref[...])
        ds, dq, dk, dv, dz, dba, dal, ddt, dnw = vjp((per_head(dy_ref), ds_scr[...]))
        ds_scr[...] = ds
        for hh in range(A_HEADS):
            cols = slice(hh * LANE, (hh + 1) * LANE)
            dqkv_ref[0, :, cols] = dq[hh]
            dqkv_ref[1, :, cols] = dk[hh]
            dqkv_ref[2, :, cols] = dv[hh]
            dgate_ref[:, cols] = dz[hh]
        dgate_ref[:, A_WIDTH:] = dba
        dal_ref[...] += dal
        ddt_ref[...] += ddt
        dnw_ref[...] += dnw

    rev = lambda i: nsc - 1 - i
    blk = lambda j: pl.BlockSpec((SUPER, A_WIDTH), lambda i: (rev(i), j))
    gate = pl.BlockSpec((SUPER, L_GATE), lambda i: (rev(i), L_ZA // L_GATE))
    row = pl.BlockSpec((1, LANE), lambda i: (0, 0))
    mf, mb = _gdn_masks()
    whole = lambda a: pl.BlockSpec(a.shape, lambda i: (0, 0, 0))
    return _pcall(
        core, name=name, grid=(nsc,),
        in_specs=[blk(0), blk(1), blk(2), gate, row, row, row,
                  pl.BlockSpec((1, A_HEADS, A_HEAD_DIM, A_HEAD_DIM), lambda i: (rev(i), 0, 0, 0)),
                  pl.BlockSpec((1, A_HEADS, SUPER, SUPER), lambda i: (rev(i), 0, 0, 0)),
                  blk(0), whole(mf), whole(mb), _ANY],
        out_specs=[gate, pl.BlockSpec((3, SUPER, A_WIDTH), lambda i: (0, rev(i), 0)), row, row, row],
        out_shape=[jax.ShapeDtypeStruct((t_len, L_COLS), F32), jax.ShapeDtypeStruct((3, t_len, A_WIDTH), F32)]
        + [jax.ShapeDtypeStruct((1, LANE), F32)] * 3,
        scratch_shapes=[pltpu.VMEM((A_HEADS, A_HEAD_DIM, A_HEAD_DIM), F32)],
        aliases={12: 0}, sem=("arbitrary",), rider=rider,
        args=(qkv, qkv, qkv, h, alog, dtb, nw, s_in, t_in, dycat, mf, mb, dh))


Q_BLOCKS = 4
Q_ROWS = Q_BLOCKS * BLOCK


def _swa_block(q, kp, kc, vp, vc, z, sinks, first):
    rows = B_GROUP * BLOCK
    ri = lax.broadcasted_iota(jnp.int32, (rows, 2 * BLOCK), 0)
    si = lax.broadcasted_iota(jnp.int32, (rows, 2 * BLOCK), 1)
    dist = (ri & (BLOCK - 1)) + BLOCK - si
    bias = jnp.where((dist >= 0) & (dist < WINDOW), 0.0, -jnp.inf)
    no_prev = jnp.where(first & (si[:1] < BLOCK), -jnp.inf, 0.0)
    dist_f = dist.astype(F32)
    head_of_row = lax.broadcasted_iota(jnp.int32, (rows, 1), 0) >> 7
    keys = jnp.concatenate([kp, kc], axis=0)
    vals = jnp.concatenate([vp, vc], axis=0)

    def item(b, j):
        cs = slice(j * B_HEAD_DIM, (j + 1) * B_HEAD_DIM)
        rs = slice(b * BLOCK, (b + 1) * BLOCK)
        heads = range(j * B_GROUP, (j + 1) * B_GROUP)
        qs = jnp.concatenate([q[rs, hq * B_HEAD_DIM:(hq + 1) * B_HEAD_DIM] for hq in heads], axis=0) * (
            B_HEAD_DIM ** -0.5)
        kk = keys[b * BLOCK:(b + 2) * BLOCK, cs]
        vv = vals[b * BLOCK:(b + 2) * BLOCK, cs]
        sink = jnp.concatenate([jnp.broadcast_to(sinks[:, hq:hq + 1], (BLOCK, 1)) for hq in heads], axis=0)
        slope = sum(jnp.where(head_of_row == gi, 2.0 ** (-8.0 * (hq + 1) / B_Q_HEADS), 0.0)
                    for gi, hq in enumerate(heads))
        return qs, kk, vv, sink, slope, (no_prev if b == 0 else jnp.zeros_like(no_prev))

    def attend(qs, kk, vv, sink, slope, hide):
        sc = _mm_nt(qs, kk) - slope * dist_f + (bias + hide)
        m = lax.stop_gradient(jnp.maximum(jnp.max(sc, axis=-1, keepdims=True), sink))
        p = jnp.exp(sc - m)
        inv = 1.0 / (jnp.sum(p, axis=-1, keepdims=True) + jnp.exp(sink - m))
        return _mm(p * inv, vv)

    items = [(b, j) for b in range(Q_BLOCKS) for j in range(B_KV_HEADS)]
    o = jax.vmap(attend)(*[_stack(t) for t in zip(*[item(b, j) for b, j in items])])
    rows_out = [jnp.concatenate([o[b * B_KV_HEADS + j, gi * BLOCK:(gi + 1) * BLOCK]
                                 for j in range(B_KV_HEADS) for gi in range(B_GROUP)], axis=1)
                for b in range(Q_BLOCKS)]
    return jnp.concatenate(rows_out, axis=0) * _silu(z)


def _swa_specs(idx):
    wide = lambda off: pl.BlockSpec((Q_ROWS, B_WIDTH), lambda n: (idx(n), off))
    cur = lambda off: pl.BlockSpec((Q_ROWS, LANE), lambda n: (idx(n), off))
    prev = lambda off: pl.BlockSpec((BLOCK, LANE), lambda n: (jnp.maximum(idx(n) * Q_BLOCKS - 1, 0), off))
    return [wide(L_QB // B_WIDTH), prev(L_KB // LANE), cur(L_KB // LANE), prev(L_VB // LANE), cur(L_VB // LANE),
            wide(L_ZB // B_WIDTH), pl.BlockSpec((1, LANE), lambda n: (0, 0))]


def _swa_fwd(h, sinks, *, name, rider=None):
    t_len = h.shape[0]
    nb = t_len // Q_ROWS

    def core(ins, outs, _):
        q_ref, kp_ref, kc_ref, vp_ref, vc_ref, z_ref, s_ref = ins
        outs[0][...] = _swa_block(q_ref[...], kp_ref[...], kc_ref[...], vp_ref[...], vc_ref[...], z_ref[...],
                                  s_ref[...], pl.program_id(0) == 0)

    res = _pcall(core, name=name, grid=(nb,), in_specs=_swa_specs(lambda n: n),
                 out_specs=[pl.BlockSpec((Q_ROWS, B_WIDTH), lambda n: (n, 1))],
                 out_shape=[jax.ShapeDtypeStruct((t_len, D_MODEL), F32)], sem=("parallel",), rider=rider,
                 args=(h, h, h, h, h, h, sinks))
    return res if rider else res[0]


def _swa_bwd(h, sinks, dycat, *, name, rider=None):
    t_len = h.shape[0]
    nb = t_len // Q_ROWS
    last = slice(Q_ROWS - BLOCK, Q_ROWS)

    def core(ins, outs, scr):
        q_ref, kp_ref, kc_ref, vp_ref, vc_ref, z_ref, s_ref, dy_ref = ins
        dh_ref, dsk_ref = outs
        ck_scr, cv_scr = scr
        i = pl.program_id(0)
        n = nb - 1 - i

        @pl.when(i == 0)
        def _():
            ck_scr[...] = jnp.zeros_like(ck_scr)
            cv_scr[...] = jnp.zeros_like(cv_scr)
            dsk_ref[...] = jnp.zeros_like(dsk_ref)

        fn = functools.partial(_swa_block, first=(n == 0))
        _, vjp = jax.vjp(fn, q_ref[...], kp_ref[...], kc_ref[...], vp_ref[...], vc_ref[...], z_ref[...], s_ref[...])
        dq, dkp, dkc, dvp, dvc, dz, dsk = vjp(dy_ref[...])
        dh_ref[:, L_QB:L_QB + B_WIDTH] = dq
        dh_ref[:, L_ZB:L_ZB + B_WIDTH] = dz
        dh_ref[:, L_KB:L_KB + LANE] = dkc
        dh_ref[:, L_VB:L_VB + LANE] = dvc
        dh_ref[last, L_KB:L_KB + LANE] += ck_scr[...]
        dh_ref[last, L_VB:L_VB + LANE] += cv_scr[...]
        ck_scr[...] = dkp
        cv_scr[...] = dvp
        dsk_ref[...] += dsk

    rev = lambda i: nb - 1 - i
    return _pcall(
        core, name=name, grid=(nb,),
        in_specs=_swa_specs(rev) + [pl.BlockSpec((Q_ROWS, B_WIDTH), lambda i: (rev(i), 1))],
        out_specs=[pl.BlockSpec((Q_ROWS, L_SWA), lambda i: (rev(i), 0)), pl.BlockSpec((1, LANE), lambda i: (0, 0))],
        out_shape=[jax.ShapeDtypeStruct((t_len, L_COLS), F32), jax.ShapeDtypeStruct((1, LANE), F32)],
        scratch_shapes=[pltpu.VMEM((BLOCK, LANE), F32), pltpu.VMEM((BLOCK, LANE), F32)],
        sem=("arbitrary",), rider=rider, args=(h, h, h, h, h, h, sinks, dycat))


def _out_ln_fwd(ycat, w_out, x, ln_g, ln_b, *, name, tm=512, last=False):
    t_len = x.shape[0]

    def body(y_ref, w_ref, x_ref, g_ref, b_ref, r_ref, *o_ref):
        r = DEEPNORM_ALPHA * x_ref[...] + _mm(y_ref[...], w_ref[...])
        r_ref[...] = r
        if not last:
            mu = jnp.mean(r, axis=-1, keepdims=True)
            d = r - mu
            var = jnp.mean(d * d, axis=-1, keepdims=True)
            o_ref[0][...] = d * lax.rsqrt(var + LN_EPS) * g_ref[...] + b_ref[...]

    tile = pl.BlockSpec((tm, D_MODEL), lambda i: (i, 0))
    vec = pl.BlockSpec((1, D_MODEL), lambda i: (0, 0))
    n_out = 1 if last else 2
    res = pl.pallas_call(
        body, name=name, grid=(t_len // tm,),
        in_specs=[tile, pl.BlockSpec((D_MODEL, D_MODEL), lambda i: (0, 0)), tile, vec, vec],
        out_specs=[tile] * n_out,
        out_shape=[jax.ShapeDtypeStruct((t_len, D_MODEL), F32)] * n_out,
        compiler_params=_cparams(("parallel",)),
    )(ycat, w_out, x, ln_g, ln_b)
    return (res[0], None) if last else res


def _ln_bwd(dxn, r, ln_g, *, name, tm=512, loss=None):
    t_len = r.shape[0]

    def body(*refs):
        if loss:
            t_ref, r_ref, g_ref, b_ref, dr_ref, dg_ref, db_ref, l_ref = refs
        else:
            dx_ref, r_ref, g_ref, dr_ref, dg_ref, db_ref = refs

        @pl.when(pl.program_id(0) == 0)
        def _():
            dg_ref[...] = jnp.zeros_like(dg_ref)
            db_ref[...] = jnp.zeros_like(db_ref)
            if loss:
                l_ref[...] = jnp.zeros_like(l_ref)

        rr = r_ref[...]
        mu = jnp.mean(rr, axis=-1, keepdims=True)
        d = rr - mu
        rstd = lax.rsqrt(jnp.mean(d * d, axis=-1, keepdims=True) + LN_EPS)
        xh = d * rstd
        if loss:
            e = (xh * g_ref[...] + b_ref[...]) - t_ref[...]
            dx = e * (1.0 / D_MODEL)
            l_ref[...] += jnp.sum(e * e, axis=0, keepdims=True)
        else:
            dx = dx_ref[...]
        dxh = dx * g_ref[...]
        dr_ref[...] = rstd * (dxh - jnp.mean(dxh, axis=-1, keepdims=True)
                              - xh * jnp.mean(dxh * xh, axis=-1, keepdims=True))
        dg_ref[...] += jnp.sum(dx * xh, axis=0, keepdims=True)
        db_ref[...] += jnp.sum(dx, axis=0, keepdims=True)

    tile = pl.BlockSpec((tm, D_MODEL), lambda i: (i, 0))
    vec = pl.BlockSpec((1, D_MODEL), lambda i: (0, 0))
    vec_shape = jax.ShapeDtypeStruct((1, D_MODEL), F32)
    args = (loss[0], r, ln_g, loss[1]) if loss else (dxn, r, ln_g)
    return pl.pallas_call(
        body, name=name, grid=(t_len // tm,),
        in_specs=[tile, tile, vec] + ([vec] if loss else []), out_specs=[tile, vec, vec] + ([vec] if loss else []),
        out_shape=[jax.ShapeDtypeStruct((t_len, D_MODEL), F32), vec_shape, vec_shape] + ([vec_shape] if loss else []),
        compiler_params=_cparams(("arbitrary",)),
    )(*args)


def _pad_row(v):
    return jnp.zeros((1, LANE), F32).at[0, :v.shape[0]].set(v)


_REGIONS = ((0, 1536, L_QKV), (1536, 2048, L_ZA), (2048, 2056, L_BA), (2056, 2568, L_QB), (2568, 2696, L_KB),
            (2696, 2824, L_VB), (2824, 3336, L_ZB))


def _to_layout(w_full):
    out = jnp.zeros(w_full.shape[:-1] + (L_COLS,), w_full.dtype)
    for a, b, off in _REGIONS:
        out = out.at[..., off:off + b - a].set(w_full[..., a:b])
    return out


def _from_layout(g):
    return jnp.concatenate([g[..., off:off + b - a] for a, b, off in _REGIONS], axis=-1)


def _shard_pieces(regions):
    for a, b, off in regions:
        for d in range(N_DEV):
            lo, hi = max(a, d * SHARD_COLS), min(b, (d + 1) * SHARD_COLS)
            if lo < hi:
                yield d, lo - d * SHARD_COLS, hi - d * SHARD_COLS, off + lo - a


def _as_list(r):
    return list(r) if isinstance(r, (list, tuple)) else [r]


def _gathered(shard):
    return jax.ShapeDtypeStruct((N_DEV,) + shard.shape, shard.dtype)


def _full_w_in(g_in, name):
    by_offset = sorted(_shard_pieces(_REGIONS), key=lambda p: p[3])
    tr = 256

    def body(g_ref, o_ref):
        pieces, col = [], 0
        for d, lo, hi, off in by_offset + [(None, 0, 0, L_COLS)]:
            if off > col:
                pieces.append(jnp.zeros((tr, off - col), g_ref.dtype))
            if d is not None:
                pieces.append(g_ref[d, :, lo:hi])
            col = off + hi - lo
        o_ref[...] = jnp.concatenate(pieces, axis=1)

    return pl.pallas_call(
        body, name=name, grid=(D_MODEL // tr,),
        in_specs=[pl.BlockSpec((N_DEV, tr, SHARD_COLS), lambda i: (0, i, 0))],
        out_specs=pl.BlockSpec((tr, L_COLS), lambda i: (i, 0)),
        out_shape=jax.ShapeDtypeStruct((D_MODEL, L_COLS), g_in.dtype),
        compiler_params=_cparams(("parallel",)),
    )(g_in)


def _full_conv(g_conv):
    return jnp.pad(g_conv.transpose(1, 0, 2).reshape(CONV_K, 3 * A_WIDTH), ((0, 8 - CONV_K), (0, 0)))


def _forward(x, weights, shards, small):
    a_log, dt_bias, norm_w, sinks, ln_g, ln_b = small
    tm = min(512, x.shape[0])
    saved, weights = [], [list(w) for w in weights]
    whole = lambda arrs: _Direct([(a, False, j, ()) for j, a in enumerate(arrs)], [_gathered(a) for a in arrs])
    for l in range(DEPTH):
        rider = whole(shards[l][1:]) if weights[l][1] is None else None
        h, *got = _as_list(_matmul(x, weights[l][0], form="nn", tm=tm, tn=L_COLS, tk=D_MODEL, name=f"in_proj_{l}",
                                   rider=rider))
        if rider:
            weights[l][1:] = [got[0].reshape(D_MODEL, D_MODEL), _full_conv(got[1])]
        w_in_l, w_out_l, conv_l = weights[l]
        qkv = _prep_fwd(h, conv_l, name=f"prep_fwd_{l}")
        al, dt, nw, sk = _pad_row(a_log[l]), _pad_row(dt_bias[l]), norm_w[l][None, :], _pad_row(sinks[l])
        ahead = l + 1 < DEPTH and weights[l + 1][0] is None
        rider = whole(shards[l + 1][1:]) if ahead else None
        ycat, *got = _as_list(_swa_fwd(h, sk, name=f"swa_fwd_{l}", rider=rider))
        if ahead:
            weights[l + 1][1:] = [got[0].reshape(D_MODEL, D_MODEL), _full_conv(got[1])]
        rider = whole(shards[l + 1][:1]) if ahead else None
        ycat, s_in, t_in, *got = _gdn_fwd(qkv, h, al, dt, nw, ycat, name=f"gdn_fwd_{l}", rider=rider)
        if ahead:
            weights[l + 1][0] = _full_w_in(got[0], f"w_in_columns_{l + 1}")
        r, xn = _out_ln_fwd(ycat, w_out_l, x, ln_g[l][None, :], ln_b[l][None, :], name=f"out_ln_{l}",
                            last=(l == DEPTH - 1))
        saved.append((x, h, qkv, s_in, t_in, ycat, r, al, dt, nw, sk))
        x = xn
    return x, saved, weights


def _w_in_blocks(g, name):
    rows, tr = g.shape[0], 128
    pieces = list(_shard_pieces(_REGIONS))

    def body(g_ref, o_ref):
        blocks = [[] for _ in range(N_DEV)]
        for d, lo, hi, off in pieces:
            blocks[d].append(g_ref[:, off:off + hi - lo])
        for d in range(N_DEV):
            o_ref[d] = jnp.concatenate(blocks[d], axis=1).astype(BF16)

    return pl.pallas_call(
        body, name=name, grid=(rows // tr,),
        in_specs=[pl.BlockSpec((tr, L_COLS), lambda i: (i, 0))],
        out_specs=pl.BlockSpec((N_DEV, tr, SHARD_COLS), lambda i: (0, i, 0)),
        out_shape=jax.ShapeDtypeStruct((N_DEV, rows, SHARD_COLS), BF16),
        compiler_params=_cparams(("parallel",)),
    )(g)


def _small_blocks(g):
    c_conv = g["conv_w"].reshape(CONV_K, N_DEV, CONV_SHARD_COLS).transpose(1, 0, 2)
    c_small = [jnp.broadcast_to(g[n][None], (N_DEV,) + g[n].shape) for n, _ in SMALL_SIZES]
    return _pack_small(c_conv, c_small)


def _contributions(g):
    c_out = g["w_out"].astype(BF16).reshape(N_DEV, OUT_SHARD_ROWS, D_MODEL)
    return _w_in_blocks(g["w_in_cols"], name="w_in_grad_blocks_above"), c_out, _small_blocks(g)


def _backward_layer(l, dx, saved_l, weights_l, ln_g_l, above=None, loss=None):
    x_in, h, qkv, s_in, t_in, ycat, r, al, dt, nw, sk = saved_l
    w_in_l, w_out_l, conv_l = weights_l
    tm = min(512, x_in.shape[0])
    dr, d_lng, d_lnb, *loss_lanes = _ln_bwd(dx, r, ln_g_l[None, :], name=f"ln_bwd_{l}", loss=loss)
    big = min(1024, x_in.shape[0])
    dycat = _matmul(dr, w_out_l, form="nt", tm=big, tn=D_MODEL, tk=D_MODEL, name=f"out_proj_dx_{l}")
    d_wout = _matmul(ycat, dr, form="tn", tm=D_MODEL, tn=D_MODEL, tk=big, name=f"out_proj_dw_{l}")
    rider, p_in, p_out, p_small = None, None, None, None
    recv = lambda c: jax.ShapeDtypeStruct((DEPTH,) + c.shape, c.dtype)
    if above:
        c_out = d_wout.astype(BF16).reshape(N_DEV, OUT_SHARD_ROWS, D_MODEL)
        rider = _Direct([(above[1], True, 0, (l + 1,)), (above[2], True, 1, (l + 1,)), (c_out, True, 0, (l,))],
                        [recv(above[1]), recv(above[2])])
    dh, d_sk, *got = _swa_bwd(h, sk, dycat, name=f"swa_bwd_{l}", rider=rider)
    if above:
        p_out, p_small = got
        rider = _Direct([(above[0], True, 0, (l + 1,))], [recv(above[0])])
    dh, dqkv_n, d_al, d_dt, d_nw, *got = _gdn_bwd(qkv, h, al, dt, nw, s_in, t_in, dycat, dh,
                                                  name=f"gdn_bwd_{l}", rider=rider)
    dh, d_conv = _prep_bwd(h, conv_l, dqkv_n, dh, name=f"prep_bwd_{l}")
    grads = dict(w_out=d_wout, conv_w=d_conv[:CONV_K], a_log=d_al[0, :A_HEADS], dt_bias=d_dt[0, :A_HEADS],
                 norm_w=d_nw[0], sinks=d_sk[0, :B_Q_HEADS], ln_g=d_lng[0], ln_b=d_lnb[0])
    dw = functools.partial(_matmul, x_in, dh, form="tn", tn=L_COLS, tk=tm)
    if not above:
        grads["w_in_cols"] = dw(name=f"in_proj_dw_{l}", tm=512)
    else:
        p_in, = got
        cut = D_MODEL // 4
        rest = D_MODEL - cut
        top = dw(name=f"in_proj_dw_top_{l}", tm=cut, a_cols=(0, cut))
        blocks = _w_in_blocks(top, name=f"w_in_grad_blocks_top_{l}")
        rider = _Direct([(blocks, True, 0, (l,), (pl.ds(0, cut),))], [p_in])
        bottom, p_in = dw(name=f"in_proj_dw_bottom_{l}", tm=cut, a_cols=(cut, rest), rider=rider)
        blocks = _w_in_blocks(bottom, name=f"w_in_grad_blocks_bottom_{l}")
        rider = _Direct([(blocks, True, 0, (l,), (pl.ds(cut, rest),)),
                         (_small_blocks(grads), True, 1, (l,))], [p_in, p_small])
    dx, *got = _as_list(_matmul(dh, w_in_l, form="nt", tm=tm, tn=D_MODEL, tk=L_COLS, name=f"in_proj_dx_{l}",
                                add=dr, add_scale=DEEPNORM_ALPHA, rider=rider))
    bufs = (got[0], p_out, got[1]) if above else None
    return dx, grads, bufs, (loss_lanes[0] if loss else None)


def _all_gather(shards, *, name):
    n_arr = len(shards)

    def body(*refs):
        x_refs, out_refs = refs[:n_arr], refs[n_arr:2 * n_arr]
        send_sems, recv_sems, local_sems = refs[2 * n_arr:]
        x, y, c = _me()
        me, sibling = (x, y, c), (x, y, 1 - c)
        chips = [(1 - x, y), (x, 1 - y), (1 - x, 1 - y)]

        def copy(a, k, block, to, src=None):
            dst = out_refs[a].at[_flat_id(block)]
            return _remote(dst if src is None else src, dst, send_sems.at[a, k], recv_sems.at[a, k], to)

        mine = [pltpu.make_async_copy(x_refs[a], out_refs[a].at[_flat_id(me)], local_sems.at[a])
                for a in range(n_arr)]
        for cp in mine:
            cp.start()
        first = []
        for a in range(n_arr):
            first.append(copy(a, 0, me, sibling, src=x_refs[a]))
            first += [copy(a, 1 + j, me, (*chip, c), src=x_refs[a]) for j, chip in enumerate(chips)]
        for cp in first:
            cp.start()
        passed = []
        for j, chip in enumerate(chips):
            for a in range(n_arr):
                copy(a, 1 + j, (*chip, c), me).wait_recv()
                fwd = copy(a, 4 + j, (*chip, c), sibling)
                fwd.start()
                passed.append(fwd)
        for a in range(n_arr):
            copy(a, 0, sibling, me).wait_recv()
            for j, chip in enumerate(chips):
                copy(a, 4 + j, (*chip, 1 - c), me).wait_recv()
        for cp in first + passed:
            cp.wait_send()
        for cp in mine:
            cp.wait()

    return pl.pallas_call(
        body, name=name, in_specs=[_ANY] * n_arr, out_specs=[_ANY] * n_arr,
        out_shape=[jax.ShapeDtypeStruct((N_DEV,) + s.shape, s.dtype) for s in shards],
        scratch_shapes=[pltpu.SemaphoreType.DMA((n_arr, N_DEV - 1)), pltpu.SemaphoreType.DMA((n_arr, N_DEV - 1)),
                        pltpu.SemaphoreType.DMA((n_arr,))],
    )(*shards)


def _adamw(parts, w, m, v, *, tr, name):
    depth, rows, cols = w.shape
    c1 = 1.0 - ADAM_B1 ** ADAM_STEP
    c2 = 1.0 - ADAM_B2 ** ADAM_STEP

    def body(g_ref, w_ref, m_ref, v_ref, go_ref, d_ref, mo_ref, vo_ref):
        g = g_ref[0, 0].astype(F32)
        for s in range(1, N_DEV):
            g = g + g_ref[0, s].astype(F32)
        m_new = ADAM_B1 * m_ref[0] + (1.0 - ADAM_B1) * g
        v_new = ADAM_B2 * v_ref[0] + (1.0 - ADAM_B2) * (g * g)
        go_ref[0] = g
        mo_ref[0] = m_new
        vo_ref[0] = v_new
        d_ref[0] = -ADAM_LR * ((m_new / c1) / (jnp.sqrt(v_new / c2) + ADAM_EPS) + ADAM_WD * w_ref[0])

    tile = pl.BlockSpec((1, tr, cols), lambda l, i: (l, i, 0))
    return pl.pallas_call(
        body, name=name, grid=(depth, rows // tr),
        in_specs=[pl.BlockSpec((1, N_DEV, tr, cols), lambda l, i: (l, 0, i, 0)), tile, tile, tile],
        out_specs=[tile] * 4, out_shape=[jax.ShapeDtypeStruct(w.shape, F32)] * 4,
        compiler_params=_cparams(("parallel", "parallel")),
    )(parts, w, m, v)


def _pack_small(conv, small):
    lead = conv.shape[:-2]
    flat = jnp.concatenate([conv.reshape(lead + (CS_CONV,))] + list(small), axis=-1)
    pad = CS_ROWS * LANE - flat.shape[-1]
    flat = jnp.concatenate([flat, jnp.zeros(lead + (pad,), F32)], axis=-1)
    return flat.reshape(lead + (CS_ROWS, LANE))


def _unpack_small(p):
    flat = p.reshape(DEPTH, CS_ROWS * LANE)
    conv = flat[:, :CS_CONV].reshape(DEPTH, CONV_K, CONV_SHARD_COLS)
    small, off = [], CS_CONV
    for _, n in SMALL_SIZES:
        small.append(flat[:, off:off + n])
        off += n
    return conv, small


def kernel(x, w_in, conv_w, a_log, dt_bias, norm_w, sinks, w_out, ln_g, ln_b, loss_target, m_w_in, m_conv_w, m_a_log, m_dt_bias, m_norm_w, m_sinks, m_w_out, m_ln_g, m_ln_b, v_w_in, v_conv_w, v_a_log, v_dt_bias, v_norm_w, v_sinks, v_w_out, v_ln_g, v_ln_b):
    small = [a_log, dt_bias, norm_w, sinks, ln_g, ln_b]
    shards = [[w_in[l].astype(BF16), w_out[l].astype(BF16), conv_w[l]] for l in range(DEPTH)]
    g_in0, = _all_gather(shards[0][:1], name="weights_all_gather_0")
    weights = [[_full_w_in(g_in0, "w_in_columns_0"), None, None]] + [[None, None, None]] * (DEPTH - 1)

    _, saved, weights = _forward(x[0], weights, shards, small)
    dx, g1, _, loss_lanes = _backward_layer(1, None, saved[1], weights[1], ln_g[1],
                                            loss=(loss_target[0], ln_b[1][None, :]))
    loss = lax.psum(0.5 * jnp.sum(loss_lanes) * (1.0 / D_MODEL), ("x", "y", "c"))
    dx, _, (p_in, p_out, p_small), _ = _backward_layer(0, dx, saved[0], weights[0], ln_g[0],
                                                       above=_contributions(g1))

    o_in = _adamw(p_in, w_in, m_w_in, v_w_in, tr=256, name="adamw_w_in")
    o_out = _adamw(p_out, w_out, m_w_out, v_w_out, tr=OUT_SHARD_ROWS, name="adamw_w_out")
    o_small = _adamw(p_small, _pack_small(conv_w, small),
                     _pack_small(m_conv_w, [m_a_log, m_dt_bias, m_norm_w, m_sinks, m_ln_g, m_ln_b]),
                     _pack_small(v_conv_w, [v_a_log, v_dt_bias, v_norm_w, v_sinks, v_ln_g, v_ln_b]),
                     tr=CS_ROWS, name="adamw_small")
    outs = []
    for k in range(4):
        cv, sm = _unpack_small(o_small[k])
        outs += [o_in[k], cv, sm[0], sm[1], sm[2], sm[3], o_out[k], sm[4], sm[5]]
    return (loss, dx[None], *outs)
```

```python
import functools

import jax
import jax.numpy as jnp
from jax import lax
from jax.experimental import pallas as pl
from jax.experimental.pallas import tpu as pltpu

F32 = jnp.float32
BF16 = jnp.bfloat16
MM_DTYPE = BF16

N_DEV = 8
D_MODEL = 1024
DEPTH = 2
A_HEADS = 4
A_HEAD_DIM = 128
A_WIDTH = 512
CONV_K = 4
CHUNK = 64
SUPER = 256
NEWTON_STEPS = 1
B_Q_HEADS = 8
B_KV_HEADS = 2
B_HEAD_DIM = 64
B_GROUP = 4
B_WIDTH = 512
WINDOW = 128
BLOCK = 128
IN_COLS = 3336
SHARD_COLS = IN_COLS // N_DEV
OUT_SHARD_ROWS = D_MODEL // N_DEV
CONV_SHARD_COLS = 3 * A_WIDTH // N_DEV
DEEPNORM_ALPHA = (2 * DEPTH) ** 0.25
LN_EPS = 1e-5
RMS_EPS = 1e-6
L2_EPS = 1e-6
ADAM_LR, ADAM_B1, ADAM_B2, ADAM_EPS, ADAM_WD, ADAM_STEP = 0.001, 0.9, 0.999, 1e-08, 0.01, 10

LANE = 128
L_QB, L_ZB, L_KB, L_VB, L_ZA, L_BA, L_QKV = 0, 512, 1024, 1152, 1280, 1792, 1920
L_SWA = 1280
L_GATE = 640
L_COLS = 3456
SMALL_SIZES = (("a_log", 4), ("dt_bias", 4), ("norm_w", 128), ("sinks", 8), ("ln_g", 1024), ("ln_b", 1024))
CS_CONV = CONV_K * CONV_SHARD_COLS
CS_ROWS = 24
VMEM_LIMIT = 56 * 1024 * 1024


def _cparams(sem=None):
    return pltpu.CompilerParams(dimension_semantics=sem, vmem_limit_bytes=VMEM_LIMIT)


def _mm(a, b):
    return jnp.dot(a.astype(MM_DTYPE), b.astype(MM_DTYPE), preferred_element_type=F32)


def _mm_nt(a, b):
    return lax.dot_general(a.astype(MM_DTYPE), b.astype(MM_DTYPE), (((1,), (1,)), ((), ())),
                           preferred_element_type=F32)


def _mm_tn(a, b):
    return lax.dot_general(a.astype(MM_DTYPE), b.astype(MM_DTYPE), (((0,), (0,)), ((), ())),
                           preferred_element_type=F32)


def _split(a):
    hi = a.astype(BF16)
    return hi, (a - hi.astype(F32)).astype(BF16)


def _hp(a2, b2):
    d = lambda p, q: jnp.dot(p, q, preferred_element_type=F32)
    return d(a2[0], b2[0]) + (d(a2[0], b2[1]) + d(a2[1], b2[0]))


def _silu(x):
    return x * jax.nn.sigmoid(x)


@jax.custom_vjp
def _stack(parts):
    return jnp.stack(parts)


_stack.defvjp(lambda parts: (jnp.stack(parts), None), lambda _, g: (tuple(g[i] for i in range(g.shape[0])),))


def _softplus(x):
    return jnp.maximum(x, 0.0) + jnp.log1p(jnp.exp(-jnp.abs(x)))


_ANY = pl.BlockSpec(memory_space=pl.ANY)


def _me():
    return lax.axis_index("x"), lax.axis_index("y"), lax.axis_index("c")


def _flat_id(pos):
    return 4 * pos[0] + 2 * pos[1] + pos[2]


def _remote(src, dst, send_sem, recv_sem, to):
    return pltpu.make_async_remote_copy(src_ref=src, dst_ref=dst, send_sem=send_sem, recv_sem=recv_sem,
                                        device_id=to, device_id_type=pl.DeviceIdType.MESH)


class _Direct:
    def __init__(self, items, bufs):
        self.items, self.bufs = list(items), list(bufs)
        self.n_src, self.n_buf = len(self.items), len(self.bufs)
        self.old = [j for j, b in enumerate(self.bufs) if not isinstance(b, jax.ShapeDtypeStruct)]
        self.args = [it[0] for it in self.items] + [self.bufs[j] for j in self.old]
        self.out_shape = [jax.ShapeDtypeStruct(b.shape, b.dtype) for b in self.bufs]
        self.scratch = [pltpu.SemaphoreType.DMA((self.n_src, N_DEV - 1)),
                        pltpu.SemaphoreType.DMA((self.n_src, N_DEV - 1)), pltpu.SemaphoreType.DMA((self.n_src,))]

    def aliases(self, in_base, out_base):
        return {in_base + self.n_src + pos: out_base + j for pos, j in enumerate(self.old)}

    def copies(self, in_refs, out_refs, sems):
        send_sems, recv_sems, local_sems = sems
        x, y, c = _me()
        me = _flat_id((x, y, c))
        peers = [(x ^ ((rel >> 2) & 1), y ^ ((rel >> 1) & 1), c ^ (rel & 1)) for rel in range(1, N_DEV)]
        local, sends, recvs = [], [], []
        for a, (_, per_dest, j, prefix, *rest) in enumerate(self.items):
            src = lambda d: in_refs[a].at[d] if per_dest else in_refs[a]
            dst = lambda s: out_refs[j].at[tuple(prefix) + (s,) + tuple(rest[0] if rest else ())]
            local.append(pltpu.make_async_copy(src(me), dst(me), local_sems.at[a]))
            for k, peer in enumerate(peers):
                pid = _flat_id(peer)
                sends.append(_remote(src(pid), dst(me), send_sems.at[a, k], recv_sems.at[a, k], peer))
                recvs.append(_remote(src(pid), dst(pid), send_sems.at[a, k], recv_sems.at[a, k], peer))
        return local, sends, recvs

    def start(self, in_refs, out_refs, sems):
        local, sends, _ = self.copies(in_refs, out_refs, sems)
        for cp in local + sends:
            cp.start()

    def wait(self, in_refs, out_refs, sems):
        local, sends, recvs = self.copies(in_refs, out_refs, sems)
        for cp in recvs:
            cp.wait_recv()
        for cp in sends:
            cp.wait_send()
        for cp in local:
            cp.wait()


def _pcall(core, *, name, grid, in_specs, out_specs, out_shape, args, sem, scratch_shapes=(), aliases=None,
           rider=None):
    n_in, n_out, n_scr = len(in_specs), len(out_specs), len(scratch_shapes)
    n_rin, n_rout = (len(rider.args), rider.n_buf) if rider else (0, 0)

    def body(*refs):
        ins, outs = refs[:n_in], refs[n_in + n_rin:n_in + n_rin + n_out]
        scr = refs[n_in + n_rin + n_out + n_rout:n_in + n_rin + n_out + n_rout + n_scr]
        if rider:
            r_refs = (refs[n_in:n_in + rider.n_src], refs[n_in + n_rin + n_out:n_in + n_rin + n_out + n_rout],
                      refs[n_in + n_rin + n_out + n_rout + n_scr:])
            ids = [pl.program_id(d) for d in range(len(grid))]
            first = functools.reduce(lambda p, q: p & q, [i == 0 for i in ids])
            last = functools.reduce(lambda p, q: p & q, [i == g - 1 for i, g in zip(ids, grid)])
            pl.when(first)(lambda: rider.start(*r_refs))
        core(ins, outs, scr)
        if rider:
            pl.when(last)(lambda: rider.wait(*r_refs))

    aliases = dict(aliases or {})
    if rider:
        sem = ("arbitrary",) * len(grid)
        aliases.update(rider.aliases(n_in, n_out))
    return pl.pallas_call(
        body, name=name, grid=grid, in_specs=list(in_specs) + [_ANY] * n_rin,
        out_specs=list(out_specs) + [_ANY] * n_rout,
        out_shape=list(out_shape) + (rider.out_shape if rider else []),
        scratch_shapes=list(scratch_shapes) + (rider.scratch if rider else []),
        input_output_aliases=aliases, compiler_params=_cparams(sem),
    )(*args, *(rider.args if rider else []))


def _exchange(direct, *, name):
    n_in = len(direct.args)

    def body(*refs):
        r_refs = refs[:direct.n_src], refs[n_in:n_in + direct.n_buf], refs[n_in + direct.n_buf:]
        direct.start(*r_refs)
        direct.wait(*r_refs)

    return pl.pallas_call(
        body, name=name, in_specs=[_ANY] * n_in, out_specs=[_ANY] * direct.n_buf, out_shape=direct.out_shape,
        input_output_aliases=direct.aliases(0, 0), scratch_shapes=direct.scratch,
    )(*direct.args)


def _matmul(a, b, *, form, tm, tn, tk, name, add=None, add_scale=1.0, extra=None, rider=None, a_cols=None):
    if form == "nn":
        (m, kk), n = a.shape, b.shape[1]
        a_spec = pl.BlockSpec((tm, tk), lambda i, j, k: (i, k))
        b_spec = pl.BlockSpec((tk, tn), lambda i, j, k: (k, j))
        dn = (((1,), (0,)), ((), ()))
    elif form == "nt":
        (m, kk), n = a.shape, b.shape[0]
        a_spec = pl.BlockSpec((tm, tk), lambda i, j, k: (i, k))
        b_spec = pl.BlockSpec((tn, tk), lambda i, j, k: (j, k))
        dn = (((1,), (1,)), ((), ()))
    else:
        kk, n = a.shape[0], b.shape[1]
        m0, m = a_cols or (0, a.shape[1])
        assert m0 % tm == 0
        a_spec = pl.BlockSpec((tk, tm), lambda i, j, k: (k, i + m0 // tm))
        b_spec = pl.BlockSpec((tk, tn), lambda i, j, k: (k, j))
        dn = (((0,), (0,)), ((), ()))
    assert m % tm == 0 and n % tn == 0 and kk % tk == 0, (name, m, n, kk)
    has_add, has_extra = add is not None, extra is not None

    def core(ins, outs, _):
        a_ref, b_ref = ins[:2]
        o_ref = outs[0]
        rest = ins[2:]
        k = pl.program_id(2)
        p = lax.dot_general(a_ref[...].astype(MM_DTYPE), b_ref[...].astype(MM_DTYPE), dn,
                            preferred_element_type=F32)

        @pl.when(k == 0)
        def _():
            first = p
            pos = 0
            if has_extra:
                first = first + _mm_nt(rest[0][...], rest[1][...])
                pos = 2
            if has_add:
                first = first + add_scale * rest[pos][...]
            o_ref[...] = first

        @pl.when(k > 0)
        def _():
            o_ref[...] += p

    in_specs = [a_spec, b_spec]
    args = [a, b]
    if has_extra:
        a2, b2, idx = extra
        in_specs += [pl.BlockSpec((tm, LANE), lambda i, j, k: (i, 0)),
                     pl.BlockSpec((tn, LANE), lambda i, j, k: (j, idx))]
        args += [a2, b2]
    if has_add:
        in_specs.append(pl.BlockSpec((tm, tn), lambda i, j, k: (i, j)))
        args.append(add)
    res = _pcall(core, name=name, grid=(m // tm, n // tn, kk // tk), in_specs=in_specs,
                 out_specs=[pl.BlockSpec((tm, tn), lambda i, j, k: (i, j))],
                 out_shape=[jax.ShapeDtypeStruct((m, n), F32)], args=args,
                 sem=("parallel", "parallel", "arbitrary"), rider=rider)
    return res if rider else res[0]


ZERO_TAIL = 8


def _with_tail(x):
    return jnp.concatenate([x, jnp.zeros((ZERO_TAIL,) + x.shape[1:], x.dtype)], axis=0)


def _shift_down(x, k):
    return pltpu.roll(x, k, 0)


def _shift_up(x, k):
    return pltpu.roll(x, x.shape[0] - k, 0)


def _conv_slab(x, w):
    return w[3:4] * x + w[2:3] * _shift_down(x, 1) + w[1:2] * _shift_down(x, 2) + w[0:1] * _shift_down(x, 3)


def _prep_fwd(h, conv_w, *, name):
    t_len = h.shape[0]

    def body(x_ref, w_ref, o_ref):
        s = pl.program_id(0)
        y = _silu(_conv_slab(_with_tail(x_ref[...]), w_ref[...])[:t_len])
        rs = lax.rsqrt(jnp.sum(y * y, axis=-1, keepdims=True) + L2_EPS)
        scale = jnp.where(s < A_HEADS, A_HEAD_DIM ** -0.5, 1.0)
        o_ref[...] = jnp.where(s < 2 * A_HEADS, y * rs * scale, y)

    return pl.pallas_call(
        body, name=name, grid=(12,),
        in_specs=[pl.BlockSpec((t_len, LANE), lambda s: (0, L_QKV // LANE + s)),
                  pl.BlockSpec((8, LANE), lambda s: (0, s))],
        out_specs=pl.BlockSpec((t_len, LANE), lambda s: (0, s)),
        out_shape=jax.ShapeDtypeStruct((t_len, 3 * A_WIDTH), F32),
        compiler_params=_cparams(("parallel",)),
    )(h, conv_w)


def _prep_bwd(h, conv_w, d_out, dh, *, name):
    t_len = h.shape[0]

    def body(x_ref, w_ref, g_ref, dh_in, dx_ref, dw_ref):
        del dh_in
        s = pl.program_id(0)
        x = _with_tail(x_ref[...])
        g = _with_tail(g_ref[0])
        w = w_ref[...]
        xs = [_shift_down(x, 3), _shift_down(x, 2), _shift_down(x, 1), x]
        c = w[0:1] * xs[0] + w[1:2] * xs[1] + w[2:3] * xs[2] + w[3:4] * xs[3]
        sg = jax.nn.sigmoid(c)
        y = c * sg
        rs = lax.rsqrt(jnp.sum(y * y, axis=-1, keepdims=True) + L2_EPS)
        scale = jnp.where(s < A_HEADS, A_HEAD_DIM ** -0.5, 1.0)
        dy_n = scale * (rs * g - y * (rs * rs * rs) * jnp.sum(g * y, axis=-1, keepdims=True))
        dy = jnp.where(s < 2 * A_HEADS, dy_n, g)
        dc = dy * (sg * (1.0 + c * (1.0 - sg)))
        dx = w[3:4] * dc + w[2:3] * _shift_up(dc, 1) + w[1:2] * _shift_up(dc, 2) + w[0:1] * _shift_up(dc, 3)
        dx_ref[...] = dx[:t_len]
        dws = [jnp.sum(dc * xs[j], axis=0, keepdims=True) for j in range(CONV_K)]
        dw_ref[...] = jnp.concatenate(dws + [jnp.zeros((8 - CONV_K, LANE), F32)], axis=0)

    slab = pl.BlockSpec((t_len, LANE), lambda s: (0, L_QKV // LANE + s))
    return pl.pallas_call(
        body, name=name, grid=(12,),
        in_specs=[slab, pl.BlockSpec((8, LANE), lambda s: (0, s)),
                  pl.BlockSpec((1, t_len, LANE), lambda s: (s // A_HEADS, 0, s % A_HEADS)), _ANY],
        out_specs=[slab, pl.BlockSpec((8, LANE), lambda s: (0, s))],
        out_shape=[jax.ShapeDtypeStruct((t_len, L_COLS), F32), jax.ShapeDtypeStruct((8, 3 * A_WIDTH), F32)],
        input_output_aliases={3: 0},
        compiler_params=_cparams(("parallel",)),
    )(h, conv_w, d_out, dh)


N_LEVELS = 5
MF_TRIL, MF_STRIL, MF_DIAG8, MF_LOW16, MF_EYE = 0, 1, 2, 3, 3 + N_LEVELS
MB_CUM, MB_CUM_T, MB_TOT = 0, 1, 2


def _gdn_masks():
    r = lax.broadcasted_iota(jnp.int32, (SUPER, SUPER), 0)
    c = lax.broadcasted_iota(jnp.int32, (SUPER, SUPER), 1)
    same = lambda shift: (r >> shift) == (c >> shift)
    ninf = lambda m: jnp.where(m, 0.0, -jnp.inf).astype(F32)
    one = lambda m: m.astype(F32)
    mf = jnp.stack([ninf(r >= c), ninf(r > c), one(same(3))]
                   + [one(same(4 + lv) & jnp.logical_not(same(3 + lv))) for lv in range(N_LEVELS)] + [one(r == c)])
    mb = jnp.stack([one(r >= c), one(r <= c), jnp.ones((SUPER, SUPER), F32)]).astype(BF16)
    return mf, mb


def _tri_inv_impl(a, mf):
    d = lambda p, q: jnp.dot(p.astype(BF16), q.astype(BF16), preferred_element_type=F32)
    dd = lambda p, q: jnp.dot(p, q, preferred_element_type=F32)
    eye = mf[MF_EYE]
    a0 = a * mf[MF_DIAG8]
    a2 = d(a0, a0)
    a4 = d(a2, a2)
    t = d(d(eye - a0, eye + a2), eye + a4)
    for level in range(N_LEVELS):
        t = t - d(d(t, a * mf[MF_LOW16 + level]), t)
    a_hi, a_lo = _split(a)
    for _ in range(NEWTON_STEPS):
        t0 = t.astype(BF16)
        t0f = t0.astype(F32)
        resid = (eye - t0f) - (dd(a_hi, t0) + dd(a_lo, t0))
        r_hi, r_lo = _split(resid)
        t = t0f + (dd(t0, r_hi) + dd(t0, r_lo))
    return t


@jax.custom_vjp
def _wy_apply(a, rhs, t):
    return _mm(t, rhs)


def _wy_apply_fwd(a, rhs, t):
    x = _mm(t, rhs)
    return x, (t, x)


def _wy_apply_bwd(res, dx):
    t, x = res
    d_rhs = _mm_tn(t, dx)
    return -_mm_nt(d_rhs, x), d_rhs, jnp.zeros_like(t)


_wy_apply.defvjp(_wy_apply_fwd, _wy_apply_bwd)


@functools.partial(jax.custom_vjp, nondiff_argnums=(1,))
def _lane_roll(x, shift):
    return pltpu.roll(x, shift % LANE, 1)


_lane_roll.defvjp(lambda x, shift: (_lane_roll(x, shift), None), lambda shift, _, g: (_lane_roll(g, -shift),))


def _mask_times_lanes(x, mask):
    lane = lax.broadcasted_iota(jnp.int32, (1, LANE), 1)
    x = jnp.where(lane < A_HEADS, x, 0.0)
    x1 = x.astype(BF16).astype(F32)
    x2 = (x - x1).astype(BF16).astype(F32)
    x3 = (x - x1 - x2).astype(BF16).astype(F32)
    pieces = x1 + pltpu.roll(x2, A_HEADS, 1) + pltpu.roll(x3, 2 * A_HEADS, 1)
    res = jnp.dot(mask, pieces.astype(BF16), preferred_element_type=F32)
    return res + pltpu.roll(res, LANE - A_HEADS, 1) + pltpu.roll(res, LANE - 2 * A_HEADS, 1)


@jax.custom_vjp
def _chunk_sums(g, mb):
    return _mask_times_lanes(g, mb[MB_CUM]), _mask_times_lanes(g, mb[MB_TOT])


def _chunk_sums_fwd(g, mb):
    return _chunk_sums(g, mb), mb


def _chunk_sums_bwd(mb, d):
    lane = lax.broadcasted_iota(jnp.int32, (1, LANE), 1)
    dg = _mask_times_lanes(d[0], mb[MB_CUM_T]) + _mask_times_lanes(d[1], mb[MB_TOT])
    return jnp.where(lane < A_HEADS, dg, 0.0), jnp.zeros_like(mb)


_chunk_sums.defvjp(_chunk_sums_fwd, _chunk_sums_bwd)


def _gdn_gates(ba, alog, dtb, mb):
    beta = jax.nn.sigmoid(ba)
    g = -jnp.exp(alog) * _softplus(_lane_roll(ba, -A_HEADS) + dtb)
    gc, gl = _chunk_sums(g, mb)
    return beta, gc, gl, gc.T


def _gdn_block(s, q, k, v, z, gates, nw, h, t_known, mf):
    n = q.shape[0]
    beta_all, gc_all, gl_all, gct_all = gates
    lane = lax.broadcasted_iota(jnp.int32, (1, LANE), 1)
    sub = lax.broadcasted_iota(jnp.int32, (LANE, 1), 0)
    col = lambda x: jnp.sum(jnp.where(lane == h, x, 0.0), axis=1, keepdims=True)
    wide = lambda c: jnp.broadcast_to(c, (n, LANE))
    gc, gl = col(gc_all), col(gl_all)
    gc_row = jnp.sum(jnp.where(sub == h, gct_all, 0.0), axis=0, keepdims=True)
    beta_w, eg_w = wide(col(beta_all)), wide(jnp.exp(gc))
    diff = gc - gc_row
    decay = jnp.exp(diff + mf[MF_TRIL])
    kb = k * beta_w
    a_mat = _mm_nt(kb, k) * jnp.exp(diff + mf[MF_STRIL])
    rhs = jnp.concatenate([v * beta_w, kb * eg_w], axis=1)
    if t_known is None:
        t_mat = _tri_inv_impl(a_mat, mf)
        uw = _mm(t_mat, rhs)
    else:
        t_mat = t_known
        uw = _wy_apply(a_mat, rhs, t_known)
    u, w = uw[:, :LANE], uw[:, LANE:]
    qk = _mm_nt(q, k) * decay
    q_dec = q * eg_w
    k_dec = k * wide(jnp.exp(gl - gc))
    v_new = u - _mm(w, s)
    o = _mm(q_dec, s) + _mm(qk, v_new)
    s = s * jnp.exp(gl[0:1]) + _mm_tn(k_dec, v_new)
    o = o * lax.rsqrt(jnp.mean(o * o, axis=-1, keepdims=True) + RMS_EPS) * nw
    return o * _silu(z), s, t_mat


def _gdn_fwd(qkv, h, alog, dtb, nw, ycat, *, name, rider=None):
    t_len = qkv.shape[0]
    nsc = t_len // SUPER

    def core(ins, outs, scr):
        q_ref, k_ref, v_ref, gate_ref, al_ref, dt_ref, nw_ref, mf_ref, mb_ref, _ = ins
        y_ref, sin_ref, t_ref = outs
        s_scr, = scr

        @pl.when(pl.program_id(0) == 0)
        def _():
            s_scr[...] = jnp.zeros_like(s_scr)

        per_head = lambda ref: jnp.stack([ref[:, hh * LANE:(hh + 1) * LANE] for hh in range(A_HEADS)])
        states = s_scr[...]
        gates = _gdn_gates(gate_ref[:, A_WIDTH:], al_ref[...], dt_ref[...], mb_ref[...])
        fn = jax.vmap(_gdn_block, in_axes=(0, 0, 0, 0, 0, None, None, 0, None, None))
        y, s_new, t_mat = fn(states, per_head(q_ref), per_head(k_ref), per_head(v_ref), per_head(gate_ref),
                             gates, nw_ref[...], jnp.arange(A_HEADS), None, mf_ref[...])
        sin_ref[0] = states
        t_ref[0] = t_mat
        s_scr[...] = s_new
        for hh in range(A_HEADS):
            y_ref[:, hh * LANE:(hh + 1) * LANE] = y[hh]

    blk = lambda j: pl.BlockSpec((SUPER, A_WIDTH), lambda sc: (sc, j))
    row = pl.BlockSpec((1, LANE), lambda sc: (0, 0))
    mf, mb = _gdn_masks()
    whole = lambda a: pl.BlockSpec(a.shape, lambda sc: (0, 0, 0))
    return _pcall(
        core, name=name, grid=(nsc,),
        in_specs=[blk(0), blk(1), blk(2), pl.BlockSpec((SUPER, L_GATE), lambda sc: (sc, L_ZA // L_GATE)),
                  row, row, row, whole(mf), whole(mb), _ANY],
        out_specs=[blk(0),
                   pl.BlockSpec((1, A_HEADS, A_HEAD_DIM, A_HEAD_DIM), lambda sc: (sc, 0, 0, 0)),
                   pl.BlockSpec((1, A_HEADS, SUPER, SUPER), lambda sc: (sc, 0, 0, 0))],
        out_shape=[jax.ShapeDtypeStruct((t_len, D_MODEL), F32),
                   jax.ShapeDtypeStruct((nsc, A_HEADS, A_HEAD_DIM, A_HEAD_DIM), F32),
                   jax.ShapeDtypeStruct((nsc, A_HEADS, SUPER, SUPER), F32)],
        scratch_shapes=[pltpu.VMEM((A_HEADS, A_HEAD_DIM, A_HEAD_DIM), F32)],
        aliases={9: 0}, sem=("arbitrary",), rider=rider,
        args=(qkv, qkv, qkv, h, alog, dtb, nw, mf, mb, ycat))


def _gdn_bwd(qkv, h, alog, dtb, nw, s_in, t_in, dycat, dh, *, name, rider=None):
    t_len = qkv.shape[0]
    nsc = t_len // SUPER

    def core(ins, outs, scr):
        q_ref, k_ref, v_ref, gate_ref, al_ref, dt_ref, nw_ref, sin_ref, t_ref, dy_ref, mf_ref, mb_ref, _ = ins
        dgate_ref, dqkv_ref, dal_ref, ddt_ref, dnw_ref = outs
        ds_scr, = scr

        @pl.when(pl.program_id(0) == 0)
        def _():
            ds_scr[...] = jnp.zeros_like(ds_scr)
            dal_ref[...] = jnp.zeros_like(dal_ref)
            ddt_ref[...] = jnp.zeros_like(ddt_ref)
            dnw_ref[...] = jnp.zeros_like(dnw_ref)

        per_head = lambda ref: jnp.stack([ref[:, hh * LANE:(hh + 1) * LANE] for hh in range(A_HEADS)])
        head_ids = jnp.arange(A_HEADS)
        t_known, mf, mb = t_ref[0], mf_ref[...], mb_ref[...]

        def fn(s, q, k, v, z, ba, alog, dtb, nw):
            gates = _gdn_gates(ba, alog, dtb, mb)
            one = lambda s, q, k, v, z, t, h: _gdn_block(s, q, k, v, z, gates, nw, h, t, mf)[:2]
            return jax.vmap(one)(s, q, k, v, z, t_known, head_ids)

        _, vjp = jax.vjp(fn, sin_ref[0], per_head(q_ref), per_head(k_ref), per_head(v_ref), per_head(gate_ref),
                         gate_ref[:, A_WIDTH:], al_ref[...], dt_ref[...], nw_ref[...])
        ds, dq, dk, dv, dz, dba, dal, ddt, dnw = vjp((per_head(dy_ref), ds_scr[...]))
        ds_scr[...] = ds
        for hh in range(A_HEADS):
            cols = slice(hh * LANE, (hh + 1) * LANE)
            dqkv_ref[0, :, cols] = dq[hh]
            dqkv_ref[1, :, cols] = dk[hh]
            dqkv_ref[2, :, cols] = dv[hh]
            dgate_ref[:, cols] = dz[hh]
        dgate_ref[:, A_WIDTH:] = dba
        dal_ref[...] += dal
        ddt_ref[...] += ddt
        dnw_ref[...] += dnw

    rev = lambda i: nsc - 1 - i
    blk = lambda j: pl.BlockSpec((SUPER, A_WIDTH), lambda i: (rev(i), j))
    gate = pl.BlockSpec((SUPER, L_GATE), lambda i: (rev(i), L_ZA // L_GATE))
    row = pl.BlockSpec((1, LANE), lambda i: (0, 0))
    mf, mb = _gdn_masks()
    whole = lambda a: pl.BlockSpec(a.shape, lambda i: (0, 0, 0))
    return _pcall(
        core, name=name, grid=(nsc,),
        in_specs=[blk(0), blk(1), blk(2), gate, row, row, row,
                  pl.BlockSpec((1, A_HEADS, A_HEAD_DIM, A_HEAD_DIM), lambda i: (rev(i), 0, 0, 0)),
                  pl.BlockSpec((1, A_HEADS, SUPER, SUPER), lambda i: (rev(i), 0, 0, 0)),
                  blk(0), whole(mf), whole(mb), _ANY],
        out_specs=[gate, pl.BlockSpec((3, SUPER, A_WIDTH), lambda i: (0, rev(i), 0)), row, row, row],
        out_shape=[jax.ShapeDtypeStruct((t_len, L_COLS), F32), jax.ShapeDtypeStruct((3, t_len, A_WIDTH), F32)]
        + [jax.ShapeDtypeStruct((1, LANE), F32)] * 3,
        scratch_shapes=[pltpu.VMEM((A_HEADS, A_HEAD_DIM, A_HEAD_DIM), F32)],
        aliases={12: 0}, sem=("arbitrary",), rider=rider,
        args=(qkv, qkv, qkv, h, alog, dtb, nw, s_in, t_in, dycat, mf, mb, dh))


Q_BLOCKS = 4
Q_ROWS = Q_BLOCKS * BLOCK


def _swa_block(q, kp, kc, vp, vc, z, sinks, first):
    rows = B_GROUP * BLOCK
    ri = lax.broadcasted_iota(jnp.int32, (rows, 2 * BLOCK), 0)
    si = lax.broadcasted_iota(jnp.int32, (rows, 2 * BLOCK), 1)
    dist = (ri & (BLOCK - 1)) + BLOCK - si
    bias = jnp.where((dist >= 0) & (dist < WINDOW), 0.0, -jnp.inf)
    no_prev = jnp.where(first & (si[:1] < BLOCK), -jnp.inf, 0.0)
    dist_f = dist.astype(F32)
    head_of_row = lax.broadcasted_iota(jnp.int32, (rows, 1), 0) >> 7
    keys = jnp.concatenate([kp, kc], axis=0)
    vals = jnp.concatenate([vp, vc], axis=0)

    def item(b, j):
        cs = slice(j * B_HEAD_DIM, (j + 1) * B_HEAD_DIM)
        rs = slice(b * BLOCK, (b + 1) * BLOCK)
        heads = range(j * B_GROUP, (j + 1) * B_GROUP)
        qs = jnp.concatenate([q[rs, hq * B_HEAD_DIM:(hq + 1) * B_HEAD_DIM] for hq in heads], axis=0) * (
            B_HEAD_DIM ** -0.5)
        kk = keys[b * BLOCK:(b + 2) * BLOCK, cs]
        vv = vals[b * BLOCK:(b + 2) * BLOCK, cs]
        sink = jnp.concatenate([jnp.broadcast_to(sinks[:, hq:hq + 1], (BLOCK, 1)) for hq in heads], axis=0)
        slope = sum(jnp.where(head_of_row == gi, 2.0 ** (-8.0 * (hq + 1) / B_Q_HEADS), 0.0)
                    for gi, hq in enumerate(heads))
        return qs, kk, vv, sink, slope, (no_prev if b == 0 else jnp.zeros_like(no_prev))

    def attend(qs, kk, vv, sink, slope, hide):
        sc = _mm_nt(qs, kk) - slope * dist_f + (bias + hide)
        m = lax.stop_gradient(jnp.maximum(jnp.max(sc, axis=-1, keepdims=True), sink))
        p = jnp.exp(sc - m)
        inv = 1.0 / (jnp.sum(p, axis=-1, keepdims=True) + jnp.exp(sink - m))
        return _mm(p * inv, vv)

    items = [(b, j) for b in range(Q_BLOCKS) for j in range(B_KV_HEADS)]
    o = jax.vmap(attend)(*[_stack(t) for t in zip(*[item(b, j) for b, j in items])])
    rows_out = [jnp.concatenate([o[b * B_KV_HEADS + j, gi * BLOCK:(gi + 1) * BLOCK]
                                 for j in range(B_KV_HEADS) for gi in range(B_GROUP)], axis=1)
                for b in range(Q_BLOCKS)]
    return jnp.concatenate(rows_out, axis=0) * _silu(z)


def _swa_specs(idx):
    wide = lambda off: pl.BlockSpec((Q_ROWS, B_WIDTH), lambda n: (idx(n), off))
    cur = lambda off: pl.BlockSpec((Q_ROWS, LANE), lambda n: (idx(n), off))
    prev = lambda off: pl.BlockSpec((BLOCK, LANE), lambda n: (jnp.maximum(idx(n) * Q_BLOCKS - 1, 0), off))
    return [wide(L_QB // B_WIDTH), prev(L_KB // LANE), cur(L_KB // LANE), prev(L_VB // LANE), cur(L_VB // LANE),
            wide(L_ZB // B_WIDTH), pl.BlockSpec((1, LANE), lambda n: (0, 0))]


def _swa_fwd(h, sinks, *, name, rider=None):
    t_len = h.shape[0]
    nb = t_len // Q_ROWS

    def core(ins, outs, _):
        q_ref, kp_ref, kc_ref, vp_ref, vc_ref, z_ref, s_ref = ins
        outs[0][...] = _swa_block(q_ref[...], kp_ref[...], kc_ref[...], vp_ref[...], vc_ref[...], z_ref[...],
                                  s_ref[...], pl.program_id(0) == 0)

    res = _pcall(core, name=name, grid=(nb,), in_specs=_swa_specs(lambda n: n),
                 out_specs=[pl.BlockSpec((Q_ROWS, B_WIDTH), lambda n: (n, 1))],
                 out_shape=[jax.ShapeDtypeStruct((t_len, D_MODEL), F32)], sem=("parallel",), rider=rider,
                 args=(h, h, h, h, h, h, sinks))
    return res if rider else res[0]


def _swa_bwd(h, sinks, dycat, *, name, rider=None):
    t_len = h.shape[0]
    nb = t_len // Q_ROWS
    last = slice(Q_ROWS - BLOCK, Q_ROWS)

    def core(ins, outs, scr):
        q_ref, kp_ref, kc_ref, vp_ref, vc_ref, z_ref, s_ref, dy_ref = ins
        dh_ref, dsk_ref = outs
        ck_scr, cv_scr = scr
        i = pl.program_id(0)
        n = nb - 1 - i

        @pl.when(i == 0)
        def _():
            ck_scr[...] = jnp.zeros_like(ck_scr)
            cv_scr[...] = jnp.zeros_like(cv_scr)
            dsk_ref[...] = jnp.zeros_like(dsk_ref)

        fn = functools.partial(_swa_block, first=(n == 0))
        _, vjp = jax.vjp(fn, q_ref[...], kp_ref[...], kc_ref[...], vp_ref[...], vc_ref[...], z_ref[...], s_ref[...])
        dq, dkp, dkc, dvp, dvc, dz, dsk = vjp(dy_ref[...])
        dh_ref[:, L_QB:L_QB + B_WIDTH] = dq
        dh_ref[:, L_ZB:L_ZB + B_WIDTH] = dz
        dh_ref[:, L_KB:L_KB + LANE] = dkc
        dh_ref[:, L_VB:L_VB + LANE] = dvc
        dh_ref[last, L_KB:L_KB + LANE] += ck_scr[...]
        dh_ref[last, L_VB:L_VB + LANE] += cv_scr[...]
        ck_scr[...] = dkp
        cv_scr[...] = dvp
        dsk_ref[...] += dsk

    rev = lambda i: nb - 1 - i
    return _pcall(
        core, name=name, grid=(nb,),
        in_specs=_swa_specs(rev) + [pl.BlockSpec((Q_ROWS, B_WIDTH), lambda i: (rev(i), 1))],
        out_specs=[pl.BlockSpec((Q_ROWS, L_SWA), lambda i: (rev(i), 0)), pl.BlockSpec((1, LANE), lambda i: (0, 0))],
        out_shape=[jax.ShapeDtypeStruct((t_len, L_COLS), F32), jax.ShapeDtypeStruct((1, LANE), F32)],
        scratch_shapes=[pltpu.VMEM((BLOCK, LANE), F32), pltpu.VMEM((BLOCK, LANE), F32)],
        sem=("arbitrary",), rider=rider, args=(h, h, h, h, h, h, sinks, dycat))


def _out_ln_fwd(ycat, w_out, x, ln_g, ln_b, *, name, tm=512, last=False):
    t_len = x.shape[0]

    def body(y_ref, w_ref, x_ref, g_ref, b_ref, r_ref, *o_ref):
        r = DEEPNORM_ALPHA * x_ref[...] + _mm(y_ref[...], w_ref[...])
        r_ref[...] = r
        if not last:
            mu = jnp.mean(r, axis=-1, keepdims=True)
            d = r - mu
            var = jnp.mean(d * d, axis=-1, keepdims=True)
            o_ref[0][...] = d * lax.rsqrt(var + LN_EPS) * g_ref[...] + b_ref[...]

    tile = pl.BlockSpec((tm, D_MODEL), lambda i: (i, 0))
    vec = pl.BlockSpec((1, D_MODEL), lambda i: (0, 0))
    n_out = 1 if last else 2
    res = pl.pallas_call(
        body, name=name, grid=(t_len // tm,),
        in_specs=[tile, pl.BlockSpec((D_MODEL, D_MODEL), lambda i: (0, 0)), tile, vec, vec],
        out_specs=[tile] * n_out,
        out_shape=[jax.ShapeDtypeStruct((t_len, D_MODEL), F32)] * n_out,
        compiler_params=_cparams(("parallel",)),
    )(ycat, w_out, x, ln_g, ln_b)
    return (res[0], None) if last else res


def _ln_bwd(dxn, r, ln_g, *, name, tm=512, loss=None):
    t_len = r.shape[0]

    def body(*refs):
        if loss:
            t_ref, r_ref, g_ref, b_ref, dr_ref, dg_ref, db_ref, l_ref = refs
        else:
            dx_ref, r_ref, g_ref, dr_ref, dg_ref, db_ref = refs

        @pl.when(pl.program_id(0) == 0)
        def _():
            dg_ref[...] = jnp.zeros_like(dg_ref)
            db_ref[...] = jnp.zeros_like(db_ref)
            if loss:
                l_ref[...] = jnp.zeros_like(l_ref)

        rr = r_ref[...]
        mu = jnp.mean(rr, axis=-1, keepdims=True)
        d = rr - mu
        rstd = lax.rsqrt(jnp.mean(d * d, axis=-1, keepdims=True) + LN_EPS)
        xh = d * rstd
        if loss:
            e = (xh * g_ref[...] + b_ref[...]) - t_ref[...]
            dx = e * (1.0 / D_MODEL)
            l_ref[...] += jnp.sum(e * e, axis=0, keepdims=True)
        else:
            dx = dx_ref[...]
        dxh = dx * g_ref[...]
        dr_ref[...] = rstd * (dxh - jnp.mean(dxh, axis=-1, keepdims=True)
                              - xh * jnp.mean(dxh * xh, axis=-1, keepdims=True))
        dg_ref[...] += jnp.sum(dx * xh, axis=0, keepdims=True)
        db_ref[...] += jnp.sum(dx, axis=0, keepdims=True)

    tile = pl.BlockSpec((tm, D_MODEL), lambda i: (i, 0))
    vec = pl.BlockSpec((1, D_MODEL), lambda i: (0, 0))
    vec_shape = jax.ShapeDtypeStruct((1, D_MODEL), F32)
    args = (loss[0], r, ln_g, loss[1]) if loss else (dxn, r, ln_g)
    return pl.pallas_call(
        body, name=name, grid=(t_len // tm,),
        in_specs=[tile, tile, vec] + ([vec] if loss else []), out_specs=[tile, vec, vec] + ([vec] if loss else []),
        out_shape=[jax.ShapeDtypeStruct((t_len, D_MODEL), F32), vec_shape, vec_shape] + ([vec_shape] if loss else []),
        compiler_params=_cparams(("arbitrary",)),
    )(*args)


def _pad_row(v):
    return jnp.zeros((1, LANE), F32).at[0, :v.shape[0]].set(v)


_REGIONS = ((0, 1536, L_QKV), (1536, 2048, L_ZA), (2048, 2056, L_BA), (2056, 2568, L_QB), (2568, 2696, L_KB),
            (2696, 2824, L_VB), (2824, 3336, L_ZB))


def _to_layout(w_full):
    out = jnp.zeros(w_full.shape[:-1] + (L_COLS,), w_full.dtype)
    for a, b, off in _REGIONS:
        out = out.at[..., off:off + b - a].set(w_full[..., a:b])
    return out


def _from_layout(g):
    return jnp.concatenate([g[..., off:off + b - a] for a, b, off in _REGIONS], axis=-1)


def _shard_pieces(regions):
    for a, b, off in regions:
        for d in range(N_DEV):
            lo, hi = max(a, d * SHARD_COLS), min(b, (d + 1) * SHARD_COLS)
            if lo < hi:
                yield d, lo - d * SHARD_COLS, hi - d * SHARD_COLS, off + lo - a


def _as_list(r):
    return list(r) if isinstance(r, (list, tuple)) else [r]


def _gathered(shard):
    return jax.ShapeDtypeStruct((N_DEV,) + shard.shape, shard.dtype)


def _full_w_in(g_in, name):
    by_offset = sorted(_shard_pieces(_REGIONS), key=lambda p: p[3])
    tr = 256

    def body(g_ref, o_ref):
        pieces, col = [], 0
        for d, lo, hi, off in by_offset + [(None, 0, 0, L_COLS)]:
            if off > col:
                pieces.append(jnp.zeros((tr, off - col), g_ref.dtype))
            if d is not None:
                pieces.append(g_ref[d, :, lo:hi])
            col = off + hi - lo
        o_ref[...] = jnp.concatenate(pieces, axis=1)

    return pl.pallas_call(
        body, name=name, grid=(D_MODEL // tr,),
        in_specs=[pl.BlockSpec((N_DEV, tr, SHARD_COLS), lambda i: (0, i, 0))],
        out_specs=pl.BlockSpec((tr, L_COLS), lambda i: (i, 0)),
        out_shape=jax.ShapeDtypeStruct((D_MODEL, L_COLS), g_in.dtype),
        compiler_params=_cparams(("parallel",)),
    )(g_in)


def _full_conv(g_conv):
    return jnp.pad(g_conv.transpose(1, 0, 2).reshape(CONV_K, 3 * A_WIDTH), ((0, 8 - CONV_K), (0, 0)))


def _forward(x, weights, shards, small):
    a_log, dt_bias, norm_w, sinks, ln_g, ln_b = small
    tm = min(512, x.shape[0])
    saved, weights = [], [list(w) for w in weights]
    whole = lambda arrs: _Direct([(a, False, j, ()) for j, a in enumerate(arrs)], [_gathered(a) for a in arrs])
    for l in range(DEPTH):
        rider = whole(shards[l][1:]) if weights[l][1] is None else None
        h, *got = _as_list(_matmul(x, weights[l][0], form="nn", tm=min(1024, x.shape[0]), tn=L_COLS, tk=D_MODEL,
                                   name=f"in_proj_{l}",
                                   rider=rider))
        if rider:
            weights[l][1:] = [got[0].reshape(D_MODEL, D_MODEL), _full_conv(got[1])]
        w_in_l, w_out_l, conv_l = weights[l]
        qkv = _prep_fwd(h, conv_l, name=f"prep_fwd_{l}")
        al, dt, nw, sk = _pad_row(a_log[l]), _pad_row(dt_bias[l]), norm_w[l][None, :], _pad_row(sinks[l])
        ahead = l + 1 < DEPTH and weights[l + 1][0] is None
        rider = whole(shards[l + 1][1:]) if ahead else None
        ycat, *got = _as_list(_swa_fwd(h, sk, name=f"swa_fwd_{l}", rider=rider))
        if ahead:
            weights[l + 1][1:] = [got[0].reshape(D_MODEL, D_MODEL), _full_conv(got[1])]
        rider = whole(shards[l + 1][:1]) if ahead else None
        ycat, s_in, t_in, *got = _gdn_fwd(qkv, h, al, dt, nw, ycat, name=f"gdn_fwd_{l}", rider=rider)
        if ahead:
            weights[l + 1][0] = _full_w_in(got[0], f"w_in_columns_{l + 1}")
        r, xn = _out_ln_fwd(ycat, w_out_l, x, ln_g[l][None, :], ln_b[l][None, :], name=f"out_ln_{l}",
                            last=(l == DEPTH - 1))
        saved.append((x, h, qkv, s_in, t_in, ycat, r, al, dt, nw, sk))
        x = xn
    return x, saved, weights


def _w_in_blocks(g, name):
    rows, tr = g.shape[0], 128
    pieces = list(_shard_pieces(_REGIONS))

    def body(g_ref, o_ref):
        blocks = [[] for _ in range(N_DEV)]
        for d, lo, hi, off in pieces:
            blocks[d].append(g_ref[:, off:off + hi - lo])
        for d in range(N_DEV):
            o_ref[d] = jnp.concatenate(blocks[d], axis=1).astype(BF16)

    return pl.pallas_call(
        body, name=name, grid=(rows // tr,),
        in_specs=[pl.BlockSpec((tr, L_COLS), lambda i: (i, 0))],
        out_specs=pl.BlockSpec((N_DEV, tr, SHARD_COLS), lambda i: (0, i, 0)),
        out_shape=jax.ShapeDtypeStruct((N_DEV, rows, SHARD_COLS), BF16),
        compiler_params=_cparams(("parallel",)),
    )(g)


def _small_blocks(g):
    c_conv = g["conv_w"].reshape(CONV_K, N_DEV, CONV_SHARD_COLS).transpose(1, 0, 2)
    c_small = [jnp.broadcast_to(g[n][None], (N_DEV,) + g[n].shape) for n, _ in SMALL_SIZES]
    return _pack_small(c_conv, c_small)


def _contributions(g):
    c_out = g["w_out"].astype(BF16).reshape(N_DEV, OUT_SHARD_ROWS, D_MODEL)
    return _w_in_blocks(g["w_in_cols"], name="w_in_grad_blocks_above"), c_out, _small_blocks(g)


def _backward_layer(l, dx, saved_l, weights_l, ln_g_l, above=None, loss=None):
    x_in, h, qkv, s_in, t_in, ycat, r, al, dt, nw, sk = saved_l
    w_in_l, w_out_l, conv_l = weights_l
    tm = min(512, x_in.shape[0])
    dr, d_lng, d_lnb, *loss_lanes = _ln_bwd(dx, r, ln_g_l[None, :], name=f"ln_bwd_{l}", loss=loss)
    big = min(1024, x_in.shape[0])
    dycat = _matmul(dr, w_out_l, form="nt", tm=big, tn=D_MODEL, tk=D_MODEL, name=f"out_proj_dx_{l}")
    d_wout = _matmul(ycat, dr, form="tn", tm=D_MODEL, tn=D_MODEL, tk=big, name=f"out_proj_dw_{l}")
    rider, p_in, p_out, p_small = None, None, None, None
    recv = lambda c: jax.ShapeDtypeStruct((DEPTH,) + c.shape, c.dtype)
    if above:
        c_out = d_wout.astype(BF16).reshape(N_DEV, OUT_SHARD_ROWS, D_MODEL)
        rider = _Direct([(above[1], True, 0, (l + 1,)), (above[2], True, 1, (l + 1,)), (c_out, True, 0, (l,))],
                        [recv(above[1]), recv(above[2])])
    dh, d_sk, *got = _swa_bwd(h, sk, dycat, name=f"swa_bwd_{l}", rider=rider)
    if above:
        p_out, p_small = got
        rider = _Direct([(above[0], True, 0, (l + 1,))], [recv(above[0])])
    dh, dqkv_n, d_al, d_dt, d_nw, *got = _gdn_bwd(qkv, h, al, dt, nw, s_in, t_in, dycat, dh,
                                                  name=f"gdn_bwd_{l}", rider=rider)
    dh, d_conv = _prep_bwd(h, conv_l, dqkv_n, dh, name=f"prep_bwd_{l}")
    grads = dict(w_out=d_wout, conv_w=d_conv[:CONV_K], a_log=d_al[0, :A_HEADS], dt_bias=d_dt[0, :A_HEADS],
                 norm_w=d_nw[0], sinks=d_sk[0, :B_Q_HEADS], ln_g=d_lng[0], ln_b=d_lnb[0])
    dw = functools.partial(_matmul, x_in, dh, form="tn", tn=L_COLS, tk=tm)
    if not above:
        grads["w_in_cols"] = dw(name=f"in_proj_dw_{l}", tm=512)
    else:
        p_in, = got
        cut = D_MODEL // 2
        rest = D_MODEL - cut
        top = dw(name=f"in_proj_dw_top_{l}", tm=cut, a_cols=(0, cut))
        blocks = _w_in_blocks(top, name=f"w_in_grad_blocks_top_{l}")
        rider = _Direct([(blocks, True, 0, (l,), (pl.ds(0, cut),))], [p_in])
        bottom, p_in = dw(name=f"in_proj_dw_bottom_{l}", tm=cut, a_cols=(cut, rest), rider=rider)
        blocks = _w_in_blocks(bottom, name=f"w_in_grad_blocks_bottom_{l}")
        rider = _Direct([(blocks, True, 0, (l,), (pl.ds(cut, rest),)),
                         (_small_blocks(grads), True, 1, (l,))], [p_in, p_small])
    dx, *got = _as_list(_matmul(dh, w_in_l, form="nt", tm=tm, tn=D_MODEL, tk=L_COLS, name=f"in_proj_dx_{l}",
                                add=dr, add_scale=DEEPNORM_ALPHA, rider=rider))
    bufs = (got[0], p_out, got[1]) if above else None
    return dx, grads, bufs, (loss_lanes[0] if loss else None)


def _all_gather(shards, *, name):
    n_arr = len(shards)

    def body(*refs):
        x_refs, out_refs = refs[:n_arr], refs[n_arr:2 * n_arr]
        send_sems, recv_sems, local_sems = refs[2 * n_arr:]
        x, y, c = _me()
        me, sibling = (x, y, c), (x, y, 1 - c)
        chips = [(1 - x, y), (x, 1 - y), (1 - x, 1 - y)]

        def copy(a, k, block, to, src=None):
            dst = out_refs[a].at[_flat_id(block)]
            return _remote(dst if src is None else src, dst, send_sems.at[a, k], recv_sems.at[a, k], to)

        mine = [pltpu.make_async_copy(x_refs[a], out_refs[a].at[_flat_id(me)], local_sems.at[a])
                for a in range(n_arr)]
        for cp in mine:
            cp.start()
        first = []
        for a in range(n_arr):
            first.append(copy(a, 0, me, sibling, src=x_refs[a]))
            first += [copy(a, 1 + j, me, (*chip, c), src=x_refs[a]) for j, chip in enumerate(chips)]
        for cp in first:
            cp.start()
        passed = []
        for j, chip in enumerate(chips):
            for a in range(n_arr):
                copy(a, 1 + j, (*chip, c), me).wait_recv()
                fwd = copy(a, 4 + j, (*chip, c), sibling)
                fwd.start()
                passed.append(fwd)
        for a in range(n_arr):
            copy(a, 0, sibling, me).wait_recv()
            for j, chip in enumerate(chips):
                copy(a, 4 + j, (*chip, 1 - c), me).wait_recv()
        for cp in first + passed:
            cp.wait_send()
        for cp in mine:
            cp.wait()

    return pl.pallas_call(
        body, name=name, in_specs=[_ANY] * n_arr, out_specs=[_ANY] * n_arr,
        out_shape=[jax.ShapeDtypeStruct((N_DEV,) + s.shape, s.dtype) for s in shards],
        scratch_shapes=[pltpu.SemaphoreType.DMA((n_arr, N_DEV - 1)), pltpu.SemaphoreType.DMA((n_arr, N_DEV - 1)),
                        pltpu.SemaphoreType.DMA((n_arr,))],
    )(*shards)


def _adamw(parts, w, m, v, *, tr, name):
    depth, rows, cols = w.shape
    c1 = 1.0 - ADAM_B1 ** ADAM_STEP
    c2 = 1.0 - ADAM_B2 ** ADAM_STEP

    def body(g_ref, w_ref, m_ref, v_ref, go_ref, d_ref, mo_ref, vo_ref):
        g = g_ref[0, 0].astype(F32)
        for s in range(1, N_DEV):
            g = g + g_ref[0, s].astype(F32)
        m_new = ADAM_B1 * m_ref[0] + (1.0 - ADAM_B1) * g
        v_new = ADAM_B2 * v_ref[0] + (1.0 - ADAM_B2) * (g * g)
        go_ref[0] = g
        mo_ref[0] = m_new
        vo_ref[0] = v_new
        d_ref[0] = -ADAM_LR * ((m_new / c1) / (jnp.sqrt(v_new / c2) + ADAM_EPS) + ADAM_WD * w_ref[0])

    tile = pl.BlockSpec((1, tr, cols), lambda l, i: (l, i, 0))
    return pl.pallas_call(
        body, name=name, grid=(depth, rows // tr),
        in_specs=[pl.BlockSpec((1, N_DEV, tr, cols), lambda l, i: (l, 0, i, 0)), tile, tile, tile],
        out_specs=[tile] * 4, out_shape=[jax.ShapeDtypeStruct(w.shape, F32)] * 4,
        compiler_params=_cparams(("parallel", "parallel")),
    )(parts, w, m, v)


def _pack_small(conv, small):
    lead = conv.shape[:-2]
    flat = jnp.concatenate([conv.reshape(lead + (CS_CONV,))] + list(small), axis=-1)
    pad = CS_ROWS * LANE - flat.shape[-1]
    flat = jnp.concatenate([flat, jnp.zeros(lead + (pad,), F32)], axis=-1)
    return flat.reshape(lead + (CS_ROWS, LANE))


def _unpack_small(p):
    flat = p.reshape(DEPTH, CS_ROWS * LANE)
    conv = flat[:, :CS_CONV].reshape(DEPTH, CONV_K, CONV_SHARD_COLS)
    small, off = [], CS_CONV
    for _, n in SMALL_SIZES:
        small.append(flat[:, off:off + n])
        off += n
    return conv, small


def kernel(x, w_in, conv_w, a_log, dt_bias, norm_w, sinks, w_out, ln_g, ln_b, loss_target, m_w_in, m_conv_w, m_a_log, m_dt_bias, m_norm_w, m_sinks, m_w_out, m_ln_g, m_ln_b, v_w_in, v_conv_w, v_a_log, v_dt_bias, v_norm_w, v_sinks, v_w_out, v_ln_g, v_ln_b):
    small = [a_log, dt_bias, norm_w, sinks, ln_g, ln_b]
    shards = [[w_in[l].astype(BF16), w_out[l].astype(BF16), conv_w[l]] for l in range(DEPTH)]
    g_in0, = _all_gather(shards[0][:1], name="weights_all_gather_0")
    weights = [[_full_w_in(g_in0, "w_in_columns_0"), None, None]] + [[None, None, None]] * (DEPTH - 1)

    _, saved, weights = _forward(x[0], weights, shards, small)
    dx, g1, _, loss_lanes = _backward_layer(1, None, saved[1], weights[1], ln_g[1],
                                            loss=(loss_target[0], ln_b[1][None, :]))
    loss = lax.psum(0.5 * jnp.sum(loss_lanes) * (1.0 / D_MODEL), ("x", "y", "c"))
    dx, _, (p_in, p_out, p_small), _ = _backward_layer(0, dx, saved[0], weights[0], ln_g[0],
                                                       above=_contributions(g1))

    o_in = _adamw(p_in, w_in, m_w_in, v_w_in, tr=256, name="adamw_w_in")
    o_out = _adamw(p_out, w_out, m_w_out, v_w_out, tr=OUT_SHARD_ROWS, name="adamw_w_out")
    o_small = _adamw(p_small, _pack_small(conv_w, small),
                     _pack_small(m_conv_w, [m_a_log, m_dt_bias, m_norm_w, m_sinks, m_ln_g, m_ln_b]),
                     _pack_small(v_conv_w, [v_a_log, v_dt_bias, v_norm_w, v_sinks, v_ln_g, v_ln_b]),
                     tr=CS_ROWS, name="adamw_small")
    outs = []
    for k in range(4):
        cv, sm = _unpack_small(o_small[k])
        outs += [o_in[k], cv, sm[0], sm[1], sm[2], sm[3], o_out[k], sm[4], sm[5]]
    return (loss, dx[None], *outs)
```

```python
import functools

import jax
import jax.numpy as jnp
from jax import lax
from jax.experimental import pallas as pl
from jax.experimental.pallas import tpu as pltpu

F32 = jnp.float32
BF16 = jnp.bfloat16
MM_DTYPE = BF16

N_DEV = 8
D_MODEL = 1024
DEPTH = 2
A_HEADS = 4
A_HEAD_DIM = 128
A_WIDTH = 512
CONV_K = 4
SUPER = 256
NEWTON_STEPS = 1
B_Q_HEADS = 8
B_KV_HEADS = 2
B_HEAD_DIM = 64
B_GROUP = 4
B_WIDTH = 512
WINDOW = 128
BLOCK = 128
IN_COLS = 3336
SHARD_COLS = IN_COLS // N_DEV
OUT_SHARD_ROWS = D_MODEL // N_DEV
CONV_SHARD_COLS = 3 * A_WIDTH // N_DEV
DEEPNORM_ALPHA = (2 * DEPTH) ** 0.25
LN_EPS = 1e-5
RMS_EPS = 1e-6
L2_EPS = 1e-6
ADAM_LR, ADAM_B1, ADAM_B2, ADAM_EPS, ADAM_WD, ADAM_STEP = 0.001, 0.9, 0.999, 1e-08, 0.01, 10

LANE = 128
L_QB, L_ZB, L_KB, L_VB, L_ZA, L_BA, L_QKV = 0, 512, 1024, 1152, 1280, 1792, 1920
L_SWA = 1280
L_GATE = 640
L_COLS = 3456
SMALL_SIZES = (("a_log", 4), ("dt_bias", 4), ("norm_w", 128), ("sinks", 8), ("ln_g", 1024), ("ln_b", 1024))
CS_CONV = CONV_K * CONV_SHARD_COLS
CS_ROWS = 24
VMEM_LIMIT = 48 * 1024 * 1024


def _cparams(sem=None):
    return pltpu.CompilerParams(dimension_semantics=sem, vmem_limit_bytes=VMEM_LIMIT)


def _mm(a, b):
    return jnp.dot(a.astype(MM_DTYPE), b.astype(MM_DTYPE), preferred_element_type=F32)


def _mm_nt(a, b):
    return lax.dot_general(a.astype(MM_DTYPE), b.astype(MM_DTYPE), (((1,), (1,)), ((), ())),
                           preferred_element_type=F32)


def _mm_tn(a, b):
    return lax.dot_general(a.astype(MM_DTYPE), b.astype(MM_DTYPE), (((0,), (0,)), ((), ())),
                           preferred_element_type=F32)


def _split(a):
    hi = a.astype(BF16)
    return hi, (a - hi.astype(F32)).astype(BF16)


def _silu(x):
    return x * jax.nn.sigmoid(x)


@jax.custom_vjp
def _stack(parts):
    return jnp.stack(parts)


_stack.defvjp(lambda parts: (jnp.stack(parts), None), lambda _, g: (tuple(g[i] for i in range(g.shape[0])),))


def _softplus(x):
    return jnp.maximum(x, 0.0) + jnp.log1p(jnp.exp(-jnp.abs(x)))


_ANY = pl.BlockSpec(memory_space=pl.ANY)


def _me():
    return lax.axis_index("x"), lax.axis_index("y"), lax.axis_index("c")


def _flat_id(pos):
    return 4 * pos[0] + 2 * pos[1] + pos[2]


def _remote(src, dst, send_sem, recv_sem, to):
    return pltpu.make_async_remote_copy(src_ref=src, dst_ref=dst, send_sem=send_sem, recv_sem=recv_sem,
                                        device_id=to, device_id_type=pl.DeviceIdType.MESH)


class _Direct:
    def __init__(self, items, bufs):
        self.items, self.bufs = list(items), list(bufs)
        self.n_src, self.n_buf = len(self.items), len(self.bufs)
        self.old = [j for j, b in enumerate(self.bufs) if not isinstance(b, jax.ShapeDtypeStruct)]
        self.args = [it[0] for it in self.items] + [self.bufs[j] for j in self.old]
        self.out_shape = [jax.ShapeDtypeStruct(b.shape, b.dtype) for b in self.bufs]
        self.scratch = [pltpu.SemaphoreType.DMA((self.n_src, N_DEV - 1)),
                        pltpu.SemaphoreType.DMA((self.n_src, N_DEV - 1)), pltpu.SemaphoreType.DMA((self.n_src,))]

    def aliases(self, in_base, out_base):
        return {in_base + self.n_src + pos: out_base + j for pos, j in enumerate(self.old)}

    def copies(self, in_refs, out_refs, sems):
        send_sems, recv_sems, local_sems = sems
        x, y, c = _me()
        me = _flat_id((x, y, c))
        peers = [(x ^ ((rel >> 2) & 1), y ^ ((rel >> 1) & 1), c ^ (rel & 1)) for rel in range(1, N_DEV)]
        local, sends, recvs = [], [], []
        for a, (_, per_dest, j, prefix, *rest) in enumerate(self.items):
            src = lambda d: in_refs[a].at[d] if per_dest else in_refs[a]
            dst = lambda s: out_refs[j].at[tuple(prefix) + (s,) + tuple(rest[0] if rest else ())]
            local.append(pltpu.make_async_copy(src(me), dst(me), local_sems.at[a]))
            for k, peer in enumerate(peers):
                pid = _flat_id(peer)
                sends.append(_remote(src(pid), dst(me), send_sems.at[a, k], recv_sems.at[a, k], peer))
                recvs.append(_remote(src(pid), dst(pid), send_sems.at[a, k], recv_sems.at[a, k], peer))
        return local, sends, recvs

    def start(self, in_refs, out_refs, sems):
        local, sends, _ = self.copies(in_refs, out_refs, sems)
        for cp in local + sends:
            cp.start()

    def wait(self, in_refs, out_refs, sems):
        local, sends, recvs = self.copies(in_refs, out_refs, sems)
        for cp in recvs:
            cp.wait_recv()
        for cp in sends:
            cp.wait_send()
        for cp in local:
            cp.wait()


def _pcall(core, *, name, grid, in_specs, out_specs, out_shape, args, sem, scratch_shapes=(), aliases=None,
           rider=None):
    n_in, n_out, n_scr = len(in_specs), len(out_specs), len(scratch_shapes)
    n_rin, n_rout = (len(rider.args), rider.n_buf) if rider else (0, 0)

    def body(*refs):
        ins, outs = refs[:n_in], refs[n_in + n_rin:n_in + n_rin + n_out]
        scr = refs[n_in + n_rin + n_out + n_rout:n_in + n_rin + n_out + n_rout + n_scr]
        if rider:
            r_refs = (refs[n_in:n_in + rider.n_src], refs[n_in + n_rin + n_out:n_in + n_rin + n_out + n_rout],
                      refs[n_in + n_rin + n_out + n_rout + n_scr:])
            ids = [pl.program_id(d) for d in range(len(grid))]
            first = functools.reduce(lambda p, q: p & q, [i == 0 for i in ids])
            last = functools.reduce(lambda p, q: p & q, [i == g - 1 for i, g in zip(ids, grid)])
            pl.when(first)(lambda: rider.start(*r_refs))
        core(ins, outs, scr)
        if rider:
            pl.when(last)(lambda: rider.wait(*r_refs))

    aliases = dict(aliases or {})
    if rider:
        sem = ("arbitrary",) * len(grid)
        aliases.update(rider.aliases(n_in, n_out))
    return pl.pallas_call(
        body, name=name, grid=grid, in_specs=list(in_specs) + [_ANY] * n_rin,
        out_specs=list(out_specs) + [_ANY] * n_rout,
        out_shape=list(out_shape) + (rider.out_shape if rider else []),
        scratch_shapes=list(scratch_shapes) + (rider.scratch if rider else []),
        input_output_aliases=aliases, compiler_params=_cparams(sem),
    )(*args, *(rider.args if rider else []))


def _exchange(direct, *, name):
    n_in = len(direct.args)

    def body(*refs):
        r_refs = refs[:direct.n_src], refs[n_in:n_in + direct.n_buf], refs[n_in + direct.n_buf:]
        direct.start(*r_refs)
        direct.wait(*r_refs)

    return pl.pallas_call(
        body, name=name, in_specs=[_ANY] * n_in, out_specs=[_ANY] * direct.n_buf, out_shape=direct.out_shape,
        input_output_aliases=direct.aliases(0, 0), scratch_shapes=direct.scratch,
    )(*direct.args)


def _matmul(a, b, *, form, tm, tn, tk, name, add=None, add_scale=1.0, rider=None, a_cols=None):
    if form == "nn":
        (m, kk), n = a.shape, b.shape[1]
        a_spec = pl.BlockSpec((tm, tk), lambda i, j, k: (i, k))
        b_spec = pl.BlockSpec((tk, tn), lambda i, j, k: (k, j))
        dn = (((1,), (0,)), ((), ()))
    elif form == "nt":
        (m, kk), n = a.shape, b.shape[0]
        a_spec = pl.BlockSpec((tm, tk), lambda i, j, k: (i, k))
        b_spec = pl.BlockSpec((tn, tk), lambda i, j, k: (j, k))
        dn = (((1,), (1,)), ((), ()))
    else:
        kk, n = a.shape[0], b.shape[1]
        m0, m = a_cols or (0, a.shape[1])
        assert m0 % tm == 0
        a_spec = pl.BlockSpec((tk, tm), lambda i, j, k: (k, i + m0 // tm))
        b_spec = pl.BlockSpec((tk, tn), lambda i, j, k: (k, j))
        dn = (((0,), (0,)), ((), ()))
    assert m % tm == 0 and n % tn == 0 and kk % tk == 0, (name, m, n, kk)
    has_add = add is not None

    def core(ins, outs, _):
        a_ref, b_ref = ins[:2]
        o_ref = outs[0]
        k = pl.program_id(2)
        p = lax.dot_general(a_ref[...].astype(MM_DTYPE), b_ref[...].astype(MM_DTYPE), dn,
                            preferred_element_type=F32)

        @pl.when(k == 0)
        def _():
            o_ref[...] = p + add_scale * ins[2][...] if has_add else p

        @pl.when(k > 0)
        def _():
            o_ref[...] += p

    in_specs = [a_spec, b_spec]
    args = [a, b]
    if has_add:
        in_specs.append(pl.BlockSpec((tm, tn), lambda i, j, k: (i, j)))
        args.append(add)
    res = _pcall(core, name=name, grid=(m // tm, n // tn, kk // tk), in_specs=in_specs,
                 out_specs=[pl.BlockSpec((tm, tn), lambda i, j, k: (i, j))],
                 out_shape=[jax.ShapeDtypeStruct((m, n), F32)], args=args,
                 sem=("parallel", "parallel", "arbitrary"), rider=rider)
    return res if rider else res[0]


ZERO_TAIL = 8


def _with_tail(x):
    return jnp.concatenate([x, jnp.zeros((ZERO_TAIL,) + x.shape[1:], x.dtype)], axis=0)


def _shift_down(x, k):
    return pltpu.roll(x, k, 0)


def _shift_up(x, k):
    return pltpu.roll(x, x.shape[0] - k, 0)


def _conv_slab(x, w):
    return w[3:4] * x + w[2:3] * _shift_down(x, 1) + w[1:2] * _shift_down(x, 2) + w[0:1] * _shift_down(x, 3)


def _prep_fwd(h, conv_w, *, name):
    t_len = h.shape[0]

    def body(x_ref, w_ref, o_ref):
        s = pl.program_id(0)
        y = _silu(_conv_slab(_with_tail(x_ref[...]), w_ref[...])[:t_len])
        rs = lax.rsqrt(jnp.sum(y * y, axis=-1, keepdims=True) + L2_EPS)
        scale = jnp.where(s < A_HEADS, A_HEAD_DIM ** -0.5, 1.0)
        o_ref[...] = jnp.where(s < 2 * A_HEADS, y * rs * scale, y)

    return pl.pallas_call(
        body, name=name, grid=(12,),
        in_specs=[pl.BlockSpec((t_len, LANE), lambda s: (0, L_QKV // LANE + s)),
                  pl.BlockSpec((8, LANE), lambda s: (0, s))],
        out_specs=pl.BlockSpec((t_len, LANE), lambda s: (0, s)),
        out_shape=jax.ShapeDtypeStruct((t_len, 3 * A_WIDTH), F32),
        compiler_params=_cparams(("parallel",)),
    )(h, conv_w)


def _prep_bwd(h, conv_w, d_out, dh, *, name):
    t_len = h.shape[0]

    def body(x_ref, w_ref, g_ref, dh_in, dx_ref, dw_ref):
        del dh_in
        s = pl.program_id(0)
        x = _with_tail(x_ref[...])
        g = _with_tail(g_ref[0])
        w = w_ref[...]
        xs = [_shift_down(x, 3), _shift_down(x, 2), _shift_down(x, 1), x]
        c = w[0:1] * xs[0] + w[1:2] * xs[1] + w[2:3] * xs[2] + w[3:4] * xs[3]
        sg = jax.nn.sigmoid(c)
        y = c * sg
        rs = lax.rsqrt(jnp.sum(y * y, axis=-1, keepdims=True) + L2_EPS)
        scale = jnp.where(s < A_HEADS, A_HEAD_DIM ** -0.5, 1.0)
        dy_n = scale * (rs * g - y * (rs * rs * rs) * jnp.sum(g * y, axis=-1, keepdims=True))
        dy = jnp.where(s < 2 * A_HEADS, dy_n, g)
        dc = dy * (sg * (1.0 + c * (1.0 - sg)))
        dx = w[3:4] * dc + w[2:3] * _shift_up(dc, 1) + w[1:2] * _shift_up(dc, 2) + w[0:1] * _shift_up(dc, 3)
        dx_ref[...] = dx[:t_len]
        dws = [jnp.sum(dc * xs[j], axis=0, keepdims=True) for j in range(CONV_K)]
        dw_ref[...] = jnp.concatenate(dws + [jnp.zeros((8 - CONV_K, LANE), F32)], axis=0)

    slab = pl.BlockSpec((t_len, LANE), lambda s: (0, L_QKV // LANE + s))
    return pl.pallas_call(
        body, name=name, grid=(12,),
        in_specs=[slab, pl.BlockSpec((8, LANE), lambda s: (0, s)),
                  pl.BlockSpec((1, t_len, LANE), lambda s: (s // A_HEADS, 0, s % A_HEADS)), _ANY],
        out_specs=[slab, pl.BlockSpec((8, LANE), lambda s: (0, s))],
        out_shape=[jax.ShapeDtypeStruct((t_len, L_COLS), F32), jax.ShapeDtypeStruct((8, 3 * A_WIDTH), F32)],
        input_output_aliases={3: 0},
        compiler_params=_cparams(("parallel",)),
    )(h, conv_w, d_out, dh)


N_LEVELS = 5
MF_TRIL, MF_STRIL, MF_DIAG8, MF_LOW16, MF_EYE = 0, 1, 2, 3, 3 + N_LEVELS
MB_CUM, MB_CUM_T, MB_TOT = 0, 1, 2


def _gdn_masks():
    r = lax.broadcasted_iota(jnp.int32, (SUPER, SUPER), 0)
    c = lax.broadcasted_iota(jnp.int32, (SUPER, SUPER), 1)
    same = lambda shift: (r >> shift) == (c >> shift)
    ninf = lambda m: jnp.where(m, 0.0, -jnp.inf).astype(F32)
    one = lambda m: m.astype(F32)
    mf = jnp.stack([ninf(r >= c), ninf(r > c), one(same(3))]
                   + [one(same(4 + lv) & jnp.logical_not(same(3 + lv))) for lv in range(N_LEVELS)] + [one(r == c)])
    mb = jnp.stack([one(r >= c), one(r <= c), jnp.ones((SUPER, SUPER), F32)]).astype(BF16)
    return mf, mb


def _tri_inv_impl(a, mf):
    d = lambda p, q: jnp.dot(p.astype(BF16), q.astype(BF16), preferred_element_type=F32)
    dd = lambda p, q: jnp.dot(p, q, preferred_element_type=F32)
    eye = mf[MF_EYE]
    a0 = a * mf[MF_DIAG8]
    a2 = d(a0, a0)
    a4 = d(a2, a2)
    t = d(d(eye - a0, eye + a2), eye + a4)
    for level in range(N_LEVELS):
        t = t - d(d(t, a * mf[MF_LOW16 + level]), t)
    a_hi, a_lo = _split(a)
    for _ in range(NEWTON_STEPS):
        t0 = t.astype(BF16)
        t0f = t0.astype(F32)
        resid = (eye - t0f) - (dd(a_hi, t0) + dd(a_lo, t0))
        r_hi, r_lo = _split(resid)
        t = t0f + (dd(t0, r_hi) + dd(t0, r_lo))
    return t


@jax.custom_vjp
def _wy_apply(a, rhs, t):
    return _mm(t, rhs)


def _wy_apply_fwd(a, rhs, t):
    x = _mm(t, rhs)
    return x, (t, x)


def _wy_apply_bwd(res, dx):
    t, x = res
    d_rhs = _mm_tn(t, dx)
    return -_mm_nt(d_rhs, x), d_rhs, jnp.zeros_like(t)


_wy_apply.defvjp(_wy_apply_fwd, _wy_apply_bwd)


@functools.partial(jax.custom_vjp, nondiff_argnums=(1,))
def _lane_roll(x, shift):
    return pltpu.roll(x, shift % LANE, 1)


_lane_roll.defvjp(lambda x, shift: (_lane_roll(x, shift), None), lambda shift, _, g: (_lane_roll(g, -shift),))


def _mask_times_lanes(x, mask):
    lane = lax.broadcasted_iota(jnp.int32, (1, LANE), 1)
    x = jnp.where(lane < A_HEADS, x, 0.0)
    x1 = x.astype(BF16).astype(F32)
    x2 = (x - x1).astype(BF16).astype(F32)
    x3 = (x - x1 - x2).astype(BF16).astype(F32)
    pieces = x1 + pltpu.roll(x2, A_HEADS, 1) + pltpu.roll(x3, 2 * A_HEADS, 1)
    res = jnp.dot(mask, pieces.astype(BF16), preferred_element_type=F32)
    return res + pltpu.roll(res, LANE - A_HEADS, 1) + pltpu.roll(res, LANE - 2 * A_HEADS, 1)


@jax.custom_vjp
def _chunk_sums(g, mb):
    return _mask_times_lanes(g, mb[MB_CUM]), _mask_times_lanes(g, mb[MB_TOT])


def _chunk_sums_fwd(g, mb):
    return _chunk_sums(g, mb), mb


def _chunk_sums_bwd(mb, d):
    lane = lax.broadcasted_iota(jnp.int32, (1, LANE), 1)
    dg = _mask_times_lanes(d[0], mb[MB_CUM_T]) + _mask_times_lanes(d[1], mb[MB_TOT])
    return jnp.where(lane < A_HEADS, dg, 0.0), jnp.zeros_like(mb)


_chunk_sums.defvjp(_chunk_sums_fwd, _chunk_sums_bwd)


def _gdn_gates(ba, alog, dtb, mb):
    beta = jax.nn.sigmoid(ba)
    g = -jnp.exp(alog) * _softplus(_lane_roll(ba, -A_HEADS) + dtb)
    gc, gl = _chunk_sums(g, mb)
    return beta, gc, gl, gc.T


def _gdn_block(s, q, k, v, z, gates, nw, h, t_known, mf):
    n = q.shape[0]
    beta_all, gc_all, gl_all, gct_all = gates
    lane = lax.broadcasted_iota(jnp.int32, (1, LANE), 1)
    sub = lax.broadcasted_iota(jnp.int32, (LANE, 1), 0)
    col = lambda x: jnp.sum(jnp.where(lane == h, x, 0.0), axis=1, keepdims=True)
    wide = lambda c: jnp.broadcast_to(c, (n, LANE))
    gc, gl = col(gc_all), col(gl_all)
    gc_row = jnp.sum(jnp.where(sub == h, gct_all, 0.0), axis=0, keepdims=True)
    beta_w, eg_w = wide(col(beta_all)), wide(jnp.exp(gc))
    diff = gc - gc_row
    decay = jnp.exp(diff + mf[MF_TRIL])
    kb = k * beta_w
    a_mat = _mm_nt(kb, k) * jnp.exp(diff + mf[MF_STRIL])
    rhs = jnp.concatenate([v * beta_w, kb * eg_w], axis=1)
    if t_known is None:
        t_mat = _tri_inv_impl(a_mat, mf)
        uw = _mm(t_mat, rhs)
    else:
        t_mat = t_known
        uw = _wy_apply(a_mat, rhs, t_known)
    u, w = uw[:, :LANE], uw[:, LANE:]
    qk = _mm_nt(q, k) * decay
    q_dec = q * eg_w
    k_dec = k * wide(jnp.exp(gl - gc))
    v_new = u - _mm(w, s)
    o = _mm(q_dec, s) + _mm(qk, v_new)
    s = s * jnp.exp(gl[0:1]) + _mm_tn(k_dec, v_new)
    o = o * lax.rsqrt(jnp.mean(o * o, axis=-1, keepdims=True) + RMS_EPS) * nw
    return o * _silu(z), s, t_mat


def _gdn_fwd(qkv, h, alog, dtb, nw, ycat, *, name, rider=None):
    t_len = qkv.shape[0]
    nsc = t_len // SUPER

    def core(ins, outs, scr):
        q_ref, k_ref, v_ref, gate_ref, al_ref, dt_ref, nw_ref, mf_ref, mb_ref, _ = ins
        y_ref, sin_ref, t_ref = outs
        s_scr, = scr

        @pl.when(pl.program_id(0) == 0)
        def _():
            s_scr[...] = jnp.zeros_like(s_scr)

        per_head = lambda ref: jnp.stack([ref[:, hh * LANE:(hh + 1) * LANE] for hh in range(A_HEADS)])
        states = s_scr[...]
        gates = _gdn_gates(gate_ref[:, A_WIDTH:], al_ref[...], dt_ref[...], mb_ref[...])
        fn = jax.vmap(_gdn_block, in_axes=(0, 0, 0, 0, 0, None, None, 0, None, None))
        y, s_new, t_mat = fn(states, per_head(q_ref), per_head(k_ref), per_head(v_ref), per_head(gate_ref),
                             gates, nw_ref[...], jnp.arange(A_HEADS), None, mf_ref[...])
        sin_ref[0] = states
        t_ref[0] = t_mat
        s_scr[...] = s_new
        for hh in range(A_HEADS):
            y_ref[:, hh * LANE:(hh + 1) * LANE] = y[hh]

    blk = lambda j: pl.BlockSpec((SUPER, A_WIDTH), lambda sc: (sc, j))
    row = pl.BlockSpec((1, LANE), lambda sc: (0, 0))
    mf, mb = _gdn_masks()
    whole = lambda a: pl.BlockSpec(a.shape, lambda sc: (0, 0, 0))
    return _pcall(
        core, name=name, grid=(nsc,),
        in_specs=[blk(0), blk(1), blk(2), pl.BlockSpec((SUPER, L_GATE), lambda sc: (sc, L_ZA // L_GATE)),
                  row, row, row, whole(mf), whole(mb), _ANY],
        out_specs=[blk(0),
                   pl.BlockSpec((1, A_HEADS, A_HEAD_DIM, A_HEAD_DIM), lambda sc: (sc, 0, 0, 0)),
                   pl.BlockSpec((1, A_HEADS, SUPER, SUPER), lambda sc: (sc, 0, 0, 0))],
        out_shape=[jax.ShapeDtypeStruct((t_len, D_MODEL), F32),
                   jax.ShapeDtypeStruct((nsc, A_HEADS, A_HEAD_DIM, A_HEAD_DIM), F32),
                   jax.ShapeDtypeStruct((nsc, A_HEADS, SUPER, SUPER), F32)],
        scratch_shapes=[pltpu.VMEM((A_HEADS, A_HEAD_DIM, A_HEAD_DIM), F32)],
        aliases={9: 0}, sem=("arbitrary",), rider=rider,
        args=(qkv, qkv, qkv, h, alog, dtb, nw, mf, mb, ycat))


def _gdn_bwd(qkv, h, alog, dtb, nw, s_in, t_in, dycat, dh, *, name, rider=None):
    t_len = qkv.shape[0]
    nsc = t_len // SUPER

    def core(ins, outs, scr):
        q_ref, k_ref, v_ref, gate_ref, al_ref, dt_ref, nw_ref, sin_ref, t_ref, dy_ref, mf_ref, mb_ref, _ = ins
        dgate_ref, dqkv_ref, dal_ref, ddt_ref, dnw_ref = outs
        ds_scr, = scr

        @pl.when(pl.program_id(0) == 0)
        def _():
            ds_scr[...] = jnp.zeros_like(ds_scr)
            dal_ref[...] = jnp.zeros_like(dal_ref)
            ddt_ref[...] = jnp.zeros_like(ddt_ref)
            dnw_ref[...] = jnp.zeros_like(dnw_ref)

        per_head = lambda ref: jnp.stack([ref[:, hh * LANE:(hh + 1) * LANE] for hh in range(A_HEADS)])
        head_ids = jnp.arange(A_HEADS)
        t_known, mf, mb = t_ref[0], mf_ref[...], mb_ref[...]

        def fn(s, q, k, v, z, ba, alog, dtb, nw):
            gates = _gdn_gates(ba, alog, dtb, mb)
            one = lambda s, q, k, v, z, t, h: _gdn_block(s, q, k, v, z, gates, nw, h, t, mf)[:2]
            return jax.vmap(one)(s, q, k, v, z, t_known, head_ids)

        _, vjp = jax.vjp(fn, sin_ref[0], per_head(q_ref), per_head(k_ref), per_head(v_ref), per_head(gate_ref),
                         gate_ref[:, A_WIDTH:], al_ref[...], dt_ref[...], nw_ref[...])
        ds, dq, dk, dv, dz, dba, dal, ddt, dnw = vjp((per_head(dy_ref), ds_scr[...]))
        ds_scr[...] = ds
        for hh in range(A_HEADS):
            cols = slice(hh * LANE, (hh + 1) * LANE)
            dqkv_ref[0, :, cols] = dq[hh]
            dqkv_ref[1, :, cols] = dk[hh]
            dqkv_ref[2, :, cols] = dv[hh]
            dgate_ref[:, cols] = dz[hh]
        dgate_ref[:, A_WIDTH:] = dba
        dal_ref[...] += dal
        ddt_ref[...] += ddt
        dnw_ref[...] += dnw

    rev = lambda i: nsc - 1 - i
    blk = lambda j: pl.BlockSpec((SUPER, A_WIDTH), lambda i: (rev(i), j))
    gate = pl.BlockSpec((SUPER, L_GATE), lambda i: (rev(i), L_ZA // L_GATE))
    row = pl.BlockSpec((1, LANE), lambda i: (0, 0))
    mf, mb = _gdn_masks()
    whole = lambda a: pl.BlockSpec(a.shape, lambda i: (0, 0, 0))
    return _pcall(
        core, name=name, grid=(nsc,),
        in_specs=[blk(0), blk(1), blk(2), gate, row, row, row,
                  pl.BlockSpec((1, A_HEADS, A_HEAD_DIM, A_HEAD_DIM), lambda i: (rev(i), 0, 0, 0)),
                  pl.BlockSpec((1, A_HEADS, SUPER, SUPER), lambda i: (rev(i), 0, 0, 0)),
                  blk(0), whole(mf), whole(mb), _ANY],
        out_specs=[gate, pl.BlockSpec((3, SUPER, A_WIDTH), lambda i: (0, rev(i), 0)), row, row, row],
        out_shape=[jax.ShapeDtypeStruct((t_len, L_COLS), F32), jax.ShapeDtypeStruct((3, t_len, A_WIDTH), F32)]
        + [jax.ShapeDtypeStruct((1, LANE), F32)] * 3,
        scratch_shapes=[pltpu.VMEM((A_HEADS, A_HEAD_DIM, A_HEAD_DIM), F32)],
        aliases={12: 0}, sem=("arbitrary",), rider=rider,
        args=(qkv, qkv, qkv, h, alog, dtb, nw, s_in, t_in, dycat, mf, mb, dh))


Q_BLOCKS = 4
Q_ROWS = Q_BLOCKS * BLOCK


def _swa_block(q, kp, kc, vp, vc, z, sinks, first):
    rows = B_GROUP * BLOCK
    ri = lax.broadcasted_iota(jnp.int32, (rows, 2 * BLOCK), 0)
    si = lax.broadcasted_iota(jnp.int32, (rows, 2 * BLOCK), 1)
    dist = (ri & (BLOCK - 1)) + BLOCK - si
    bias = jnp.where((dist >= 0) & (dist < WINDOW), 0.0, -jnp.inf)
    no_prev = jnp.where(first & (si[:1] < BLOCK), -jnp.inf, 0.0)
    dist_f = dist.astype(F32)
    head_of_row = lax.broadcasted_iota(jnp.int32, (rows, 1), 0) >> 7
    keys = jnp.concatenate([kp, kc], axis=0)
    vals = jnp.concatenate([vp, vc], axis=0)

    def item(b, j):
        cs = slice(j * B_HEAD_DIM, (j + 1) * B_HEAD_DIM)
        rs = slice(b * BLOCK, (b + 1) * BLOCK)
        heads = range(j * B_GROUP, (j + 1) * B_GROUP)
        qs = jnp.concatenate([q[rs, hq * B_HEAD_DIM:(hq + 1) * B_HEAD_DIM] for hq in heads], axis=0) * (
            B_HEAD_DIM ** -0.5)
        kk = keys[b * BLOCK:(b + 2) * BLOCK, cs]
        vv = vals[b * BLOCK:(b + 2) * BLOCK, cs]
        sink = jnp.concatenate([jnp.broadcast_to(sinks[:, hq:hq + 1], (BLOCK, 1)) for hq in heads], axis=0)
        slope = sum(jnp.where(head_of_row == gi, 2.0 ** (-8.0 * (hq + 1) / B_Q_HEADS), 0.0)
                    for gi, hq in enumerate(heads))
        return qs, kk, vv, sink, slope, (no_prev if b == 0 else jnp.zeros_like(no_prev))

    def attend(qs, kk, vv, sink, slope, hide):
        sc = _mm_nt(qs, kk) - slope * dist_f + (bias + hide)
        m = lax.stop_gradient(jnp.maximum(jnp.max(sc, axis=-1, keepdims=True), sink))
        p = jnp.exp(sc - m)
        inv = 1.0 / (jnp.sum(p, axis=-1, keepdims=True) + jnp.exp(sink - m))
        return _mm(p * inv, vv)

    items = [(b, j) for b in range(Q_BLOCKS) for j in range(B_KV_HEADS)]
    o = jax.vmap(attend)(*[_stack(t) for t in zip(*[item(b, j) for b, j in items])])
    rows_out = [jnp.concatenate([o[b * B_KV_HEADS + j, gi * BLOCK:(gi + 1) * BLOCK]
                                 for j in range(B_KV_HEADS) for gi in range(B_GROUP)], axis=1)
                for b in range(Q_BLOCKS)]
    return jnp.concatenate(rows_out, axis=0) * _silu(z)


def _swa_specs(idx):
    wide = lambda off: pl.BlockSpec((Q_ROWS, B_WIDTH), lambda n: (idx(n), off))
    cur = lambda off: pl.BlockSpec((Q_ROWS, LANE), lambda n: (idx(n), off))
    prev = lambda off: pl.BlockSpec((BLOCK, LANE), lambda n: (jnp.maximum(idx(n) * Q_BLOCKS - 1, 0), off))
    return [wide(L_QB // B_WIDTH), prev(L_KB // LANE), cur(L_KB // LANE), prev(L_VB // LANE), cur(L_VB // LANE),
            wide(L_ZB // B_WIDTH), pl.BlockSpec((1, LANE), lambda n: (0, 0))]


def _swa_fwd(h, sinks, *, name, rider=None):
    t_len = h.shape[0]
    nb = t_len // Q_ROWS

    def core(ins, outs, _):
        q_ref, kp_ref, kc_ref, vp_ref, vc_ref, z_ref, s_ref = ins
        outs[0][...] = _swa_block(q_ref[...], kp_ref[...], kc_ref[...], vp_ref[...], vc_ref[...], z_ref[...],
                                  s_ref[...], pl.program_id(0) == 0)

    res = _pcall(core, name=name, grid=(nb,), in_specs=_swa_specs(lambda n: n),
                 out_specs=[pl.BlockSpec((Q_ROWS, B_WIDTH), lambda n: (n, 1))],
                 out_shape=[jax.ShapeDtypeStruct((t_len, D_MODEL), F32)], sem=("parallel",), rider=rider,
                 args=(h, h, h, h, h, h, sinks))
    return res if rider else res[0]


def _swa_bwd(h, sinks, dycat, *, name, rider=None):
    t_len = h.shape[0]
    nb = t_len // Q_ROWS
    last = slice(Q_ROWS - BLOCK, Q_ROWS)

    def core(ins, outs, scr):
        q_ref, kp_ref, kc_ref, vp_ref, vc_ref, z_ref, s_ref, dy_ref = ins
        dh_ref, dsk_ref = outs
        ck_scr, cv_scr = scr
        i = pl.program_id(0)
        n = nb - 1 - i

        @pl.when(i == 0)
        def _():
            ck_scr[...] = jnp.zeros_like(ck_scr)
            cv_scr[...] = jnp.zeros_like(cv_scr)
            dsk_ref[...] = jnp.zeros_like(dsk_ref)

        fn = functools.partial(_swa_block, first=(n == 0))
        _, vjp = jax.vjp(fn, q_ref[...], kp_ref[...], kc_ref[...], vp_ref[...], vc_ref[...], z_ref[...], s_ref[...])
        dq, dkp, dkc, dvp, dvc, dz, dsk = vjp(dy_ref[...])
        dh_ref[:, L_QB:L_QB + B_WIDTH] = dq
        dh_ref[:, L_ZB:L_ZB + B_WIDTH] = dz
        dh_ref[:, L_KB:L_KB + LANE] = dkc
        dh_ref[:, L_VB:L_VB + LANE] = dvc
        dh_ref[last, L_KB:L_KB + LANE] += ck_scr[...]
        dh_ref[last, L_VB:L_VB + LANE] += cv_scr[...]
        ck_scr[...] = dkp
        cv_scr[...] = dvp
        dsk_ref[...] += dsk

    rev = lambda i: nb - 1 - i
    return _pcall(
        core, name=name, grid=(nb,),
        in_specs=_swa_specs(rev) + [pl.BlockSpec((Q_ROWS, B_WIDTH), lambda i: (rev(i), 1))],
        out_specs=[pl.BlockSpec((Q_ROWS, L_SWA), lambda i: (rev(i), 0)), pl.BlockSpec((1, LANE), lambda i: (0, 0))],
        out_shape=[jax.ShapeDtypeStruct((t_len, L_COLS), F32), jax.ShapeDtypeStruct((1, LANE), F32)],
        scratch_shapes=[pltpu.VMEM((BLOCK, LANE), F32), pltpu.VMEM((BLOCK, LANE), F32)],
        sem=("arbitrary",), rider=rider, args=(h, h, h, h, h, h, sinks, dycat))


def _out_ln_fwd(ycat, w_out, x, ln_g, ln_b, *, name, tm=512, last=False):
    t_len = x.shape[0]

    def body(y_ref, w_ref, x_ref, g_ref, b_ref, r_ref, *o_ref):
        r = DEEPNORM_ALPHA * x_ref[...] + _mm(y_ref[...], w_ref[...])
        r_ref[...] = r
        if not last:
            mu = jnp.mean(r, axis=-1, keepdims=True)
            d = r - mu
            var = jnp.mean(d * d, axis=-1, keepdims=True)
            o_ref[0][...] = d * lax.rsqrt(var + LN_EPS) * g_ref[...] + b_ref[...]

    tile = pl.BlockSpec((tm, D_MODEL), lambda i: (i, 0))
    vec = pl.BlockSpec((1, D_MODEL), lambda i: (0, 0))
    n_out = 1 if last else 2
    res = pl.pallas_call(
        body, name=name, grid=(t_len // tm,),
        in_specs=[tile, pl.BlockSpec((D_MODEL, D_MODEL), lambda i: (0, 0)), tile, vec, vec],
        out_specs=[tile] * n_out,
        out_shape=[jax.ShapeDtypeStruct((t_len, D_MODEL), F32)] * n_out,
        compiler_params=_cparams(("parallel",)),
    )(ycat, w_out, x, ln_g, ln_b)
    return (res[0], None) if last else res


def _ln_bwd(dxn, r, ln_g, *, name, tm=512, loss=None):
    t_len = r.shape[0]

    def body(*refs):
        if loss:
            t_ref, r_ref, g_ref, b_ref, dr_ref, dg_ref, db_ref, l_ref = refs
        else:
            dx_ref, r_ref, g_ref, dr_ref, dg_ref, db_ref = refs

        @pl.when(pl.program_id(0) == 0)
        def _():
            dg_ref[...] = jnp.zeros_like(dg_ref)
            db_ref[...] = jnp.zeros_like(db_ref)
            if loss:
                l_ref[...] = jnp.zeros_like(l_ref)

        rr = r_ref[...]
        mu = jnp.mean(rr, axis=-1, keepdims=True)
        d = rr - mu
        rstd = lax.rsqrt(jnp.mean(d * d, axis=-1, keepdims=True) + LN_EPS)
        xh = d * rstd
        if loss:
            e = (xh * g_ref[...] + b_ref[...]) - t_ref[...]
            dx = e * (1.0 / D_MODEL)
            l_ref[...] += jnp.sum(e * e, axis=0, keepdims=True)
        else:
            dx = dx_ref[...]
        dxh = dx * g_ref[...]
        dr_ref[...] = rstd * (dxh - jnp.mean(dxh, axis=-1, keepdims=True)
                              - xh * jnp.mean(dxh * xh, axis=-1, keepdims=True))
        dg_ref[...] += jnp.sum(dx * xh, axis=0, keepdims=True)
        db_ref[...] += jnp.sum(dx, axis=0, keepdims=True)

    tile = pl.BlockSpec((tm, D_MODEL), lambda i: (i, 0))
    vec = pl.BlockSpec((1, D_MODEL), lambda i: (0, 0))
    vec_shape = jax.ShapeDtypeStruct((1, D_MODEL), F32)
    args = (loss[0], r, ln_g, loss[1]) if loss else (dxn, r, ln_g)
    return pl.pallas_call(
        body, name=name, grid=(t_len // tm,),
        in_specs=[tile, tile, vec] + ([vec] if loss else []), out_specs=[tile, vec, vec] + ([vec] if loss else []),
        out_shape=[jax.ShapeDtypeStruct((t_len, D_MODEL), F32), vec_shape, vec_shape] + ([vec_shape] if loss else []),
        compiler_params=_cparams(("arbitrary",)),
    )(*args)


def _pad_row(v):
    return jnp.zeros((1, LANE), F32).at[0, :v.shape[0]].set(v)


_REGIONS = ((0, 1536, L_QKV), (1536, 2048, L_ZA), (2048, 2056, L_BA), (2056, 2568, L_QB), (2568, 2696, L_KB),
            (2696, 2824, L_VB), (2824, 3336, L_ZB))


def _shard_pieces(regions):
    for a, b, off in regions:
        for d in range(N_DEV):
            lo, hi = max(a, d * SHARD_COLS), min(b, (d + 1) * SHARD_COLS)
            if lo < hi:
                yield d, lo - d * SHARD_COLS, hi - d * SHARD_COLS, off + lo - a


def _as_list(r):
    return list(r) if isinstance(r, (list, tuple)) else [r]


def _gathered(shard):
    return jax.ShapeDtypeStruct((N_DEV,) + shard.shape, shard.dtype)


def _full_w_in(g_in, name):
    by_offset = sorted(_shard_pieces(_REGIONS), key=lambda p: p[3])
    tr = 256

    def body(g_ref, o_ref):
        pieces, col = [], 0
        for d, lo, hi, off in by_offset + [(None, 0, 0, L_COLS)]:
            if off > col:
                pieces.append(jnp.zeros((tr, off - col), g_ref.dtype))
            if d is not None:
                pieces.append(g_ref[d, :, lo:hi])
            col = off + hi - lo
        o_ref[...] = jnp.concatenate(pieces, axis=1)

    return pl.pallas_call(
        body, name=name, grid=(D_MODEL // tr,),
        in_specs=[pl.BlockSpec((N_DEV, tr, SHARD_COLS), lambda i: (0, i, 0))],
        out_specs=pl.BlockSpec((tr, L_COLS), lambda i: (i, 0)),
        out_shape=jax.ShapeDtypeStruct((D_MODEL, L_COLS), g_in.dtype),
        compiler_params=_cparams(("parallel",)),
    )(g_in)


def _full_conv(g_conv):
    return jnp.pad(g_conv.transpose(1, 0, 2).reshape(CONV_K, 3 * A_WIDTH), ((0, 8 - CONV_K), (0, 0)))


def _forward(x, weights, shards, small):
    a_log, dt_bias, norm_w, sinks, ln_g, ln_b = small
    tm = min(512, x.shape[0])
    saved, weights = [], [list(w) for w in weights]
    whole = lambda arrs: _Direct([(a, False, j, ()) for j, a in enumerate(arrs)], [_gathered(a) for a in arrs])
    for l in range(DEPTH):
        rider = whole(shards[l][1:]) if weights[l][1] is None else None
        h, *got = _as_list(_matmul(x, weights[l][0], form="nn", tm=tm, tn=L_COLS, tk=D_MODEL, name=f"in_proj_{l}",
                                   rider=rider))
        if rider:
            weights[l][1:] = [got[0].reshape(D_MODEL, D_MODEL), _full_conv(got[1])]
        w_in_l, w_out_l, conv_l = weights[l]
        qkv = _prep_fwd(h, conv_l, name=f"prep_fwd_{l}")
        al, dt, nw, sk = _pad_row(a_log[l]), _pad_row(dt_bias[l]), norm_w[l][None, :], _pad_row(sinks[l])
        ahead = l + 1 < DEPTH and weights[l + 1][0] is None
        rider = whole(shards[l + 1][1:]) if ahead else None
        ycat, *got = _as_list(_swa_fwd(h, sk, name=f"swa_fwd_{l}", rider=rider))
        if ahead:
            weights[l + 1][1:] = [got[0].reshape(D_MODEL, D_MODEL), _full_conv(got[1])]
        rider = whole(shards[l + 1][:1]) if ahead else None
        ycat, s_in, t_in, *got = _gdn_fwd(qkv, h, al, dt, nw, ycat, name=f"gdn_fwd_{l}", rider=rider)
        if ahead:
            weights[l + 1][0] = _full_w_in(got[0], f"w_in_columns_{l + 1}")
        r, xn = _out_ln_fwd(ycat, w_out_l, x, ln_g[l][None, :], ln_b[l][None, :], name=f"out_ln_{l}",
                            last=(l == DEPTH - 1))
        saved.append((x, h, qkv, s_in, t_in, ycat, r, al, dt, nw, sk))
        x = xn
    return x, saved, weights


def _w_in_blocks(g, name):
    rows, tr = g.shape[0], 128
    pieces = list(_shard_pieces(_REGIONS))

    def body(g_ref, o_ref):
        blocks = [[] for _ in range(N_DEV)]
        for d, lo, hi, off in pieces:
            blocks[d].append(g_ref[:, off:off + hi - lo])
        for d in range(N_DEV):
            o_ref[d] = jnp.concatenate(blocks[d], axis=1).astype(BF16)

    return pl.pallas_call(
        body, name=name, grid=(rows // tr,),
        in_specs=[pl.BlockSpec((tr, L_COLS), lambda i: (i, 0))],
        out_specs=pl.BlockSpec((N_DEV, tr, SHARD_COLS), lambda i: (0, i, 0)),
        out_shape=jax.ShapeDtypeStruct((N_DEV, rows, SHARD_COLS), BF16),
        compiler_params=_cparams(("parallel",)),
    )(g)


def _small_blocks(g):
    c_conv = g["conv_w"].reshape(CONV_K, N_DEV, CONV_SHARD_COLS).transpose(1, 0, 2)
    c_small = [jnp.broadcast_to(g[n][None], (N_DEV,) + g[n].shape) for n, _ in SMALL_SIZES]
    return _pack_small(c_conv, c_small)


def _contributions(g):
    c_out = g["w_out"].astype(BF16).reshape(N_DEV, OUT_SHARD_ROWS, D_MODEL)
    return _w_in_blocks(g["w_in_cols"], name="w_in_grad_blocks_above"), c_out, _small_blocks(g)


def _backward_layer(l, dx, saved_l, weights_l, ln_g_l, above=None, loss=None):
    x_in, h, qkv, s_in, t_in, ycat, r, al, dt, nw, sk = saved_l
    w_in_l, w_out_l, conv_l = weights_l
    tm = min(512, x_in.shape[0])
    dr, d_lng, d_lnb, *loss_lanes = _ln_bwd(dx, r, ln_g_l[None, :], name=f"ln_bwd_{l}", loss=loss)
    big = min(1024, x_in.shape[0])
    dycat = _matmul(dr, w_out_l, form="nt", tm=big, tn=D_MODEL, tk=D_MODEL, name=f"out_proj_dx_{l}")
    d_wout = _matmul(ycat, dr, form="tn", tm=D_MODEL, tn=D_MODEL, tk=big, name=f"out_proj_dw_{l}")
    rider, p_in, p_out, p_small = None, None, None, None
    recv = lambda c: jax.ShapeDtypeStruct((DEPTH,) + c.shape, c.dtype)
    if above:
        c_out = d_wout.astype(BF16).reshape(N_DEV, OUT_SHARD_ROWS, D_MODEL)
        rider = _Direct([(above[1], True, 0, (l + 1,)), (above[2], True, 1, (l + 1,)), (c_out, True, 0, (l,))],
                        [recv(above[1]), recv(above[2])])
    dh, d_sk, *got = _swa_bwd(h, sk, dycat, name=f"swa_bwd_{l}", rider=rider)
    if above:
        p_out, p_small = got
        rider = _Direct([(above[0], True, 0, (l + 1,))], [recv(above[0])])
    dh, dqkv_n, d_al, d_dt, d_nw, *got = _gdn_bwd(qkv, h, al, dt, nw, s_in, t_in, dycat, dh,
                                                  name=f"gdn_bwd_{l}", rider=rider)
    dh, d_conv = _prep_bwd(h, conv_l, dqkv_n, dh, name=f"prep_bwd_{l}")
    grads = dict(w_out=d_wout, conv_w=d_conv[:CONV_K], a_log=d_al[0, :A_HEADS], dt_bias=d_dt[0, :A_HEADS],
                 norm_w=d_nw[0], sinks=d_sk[0, :B_Q_HEADS], ln_g=d_lng[0], ln_b=d_lnb[0])
    dw = functools.partial(_matmul, x_in, dh, form="tn", tn=L_COLS, tk=tm)
    if not above:
        grads["w_in_cols"] = dw(name=f"in_proj_dw_{l}", tm=512)
    else:
        p_in, = got
        cut = D_MODEL // 2
        rest = D_MODEL - cut
        top = dw(name=f"in_proj_dw_top_{l}", tm=cut, a_cols=(0, cut))
        blocks = _w_in_blocks(top, name=f"w_in_grad_blocks_top_{l}")
        rider = _Direct([(blocks, True, 0, (l,), (pl.ds(0, cut),))], [p_in])
        bottom, p_in = dw(name=f"in_proj_dw_bottom_{l}", tm=cut, a_cols=(cut, rest), rider=rider)
        blocks = _w_in_blocks(bottom, name=f"w_in_grad_blocks_bottom_{l}")
        rider = _Direct([(blocks, True, 0, (l,), (pl.ds(cut, rest),)),
                         (_small_blocks(grads), True, 1, (l,))], [p_in, p_small])
    dx, *got = _as_list(_matmul(dh, w_in_l, form="nt", tm=tm, tn=D_MODEL, tk=L_COLS, name=f"in_proj_dx_{l}",
                                add=dr, add_scale=DEEPNORM_ALPHA, rider=rider))
    bufs = (got[0], p_out, got[1]) if above else None
    return dx, grads, bufs, (loss_lanes[0] if loss else None)


def _all_gather(shards, *, name):
    n_arr = len(shards)

    def body(*refs):
        x_refs, out_refs = refs[:n_arr], refs[n_arr:2 * n_arr]
        send_sems, recv_sems, local_sems = refs[2 * n_arr:]
        x, y, c = _me()
        me, sibling = (x, y, c), (x, y, 1 - c)
        chips = [(1 - x, y), (x, 1 - y), (1 - x, 1 - y)]

        def copy(a, k, block, to, src=None):
            dst = out_refs[a].at[_flat_id(block)]
            return _remote(dst if src is None else src, dst, send_sems.at[a, k], recv_sems.at[a, k], to)

        mine = [pltpu.make_async_copy(x_refs[a], out_refs[a].at[_flat_id(me)], local_sems.at[a])
                for a in range(n_arr)]
        for cp in mine:
            cp.start()
        first = []
        for a in range(n_arr):
            first.append(copy(a, 0, me, sibling, src=x_refs[a]))
            first += [copy(a, 1 + j, me, (*chip, c), src=x_refs[a]) for j, chip in enumerate(chips)]
        for cp in first:
            cp.start()
        passed = []
        for j, chip in enumerate(chips):
            for a in range(n_arr):
                copy(a, 1 + j, (*chip, c), me).wait_recv()
                fwd = copy(a, 4 + j, (*chip, c), sibling)
                fwd.start()
                passed.append(fwd)
        for a in range(n_arr):
            copy(a, 0, sibling, me).wait_recv()
            for j, chip in enumerate(chips):
                copy(a, 4 + j, (*chip, 1 - c), me).wait_recv()
        for cp in first + passed:
            cp.wait_send()
        for cp in mine:
            cp.wait()

    return pl.pallas_call(
        body, name=name, in_specs=[_ANY] * n_arr, out_specs=[_ANY] * n_arr,
        out_shape=[jax.ShapeDtypeStruct((N_DEV,) + s.shape, s.dtype) for s in shards],
        scratch_shapes=[pltpu.SemaphoreType.DMA((n_arr, N_DEV - 1)), pltpu.SemaphoreType.DMA((n_arr, N_DEV - 1)),
                        pltpu.SemaphoreType.DMA((n_arr,))],
    )(*shards)


def _adamw(parts, w, m, v, *, tr, name):
    depth, rows, cols = w.shape
    c1 = 1.0 - ADAM_B1 ** ADAM_STEP
    c2 = 1.0 - ADAM_B2 ** ADAM_STEP

    def body(g_ref, w_ref, m_ref, v_ref, go_ref, d_ref, mo_ref, vo_ref):
        g = g_ref[0, 0].astype(F32)
        for s in range(1, N_DEV):
            g = g + g_ref[0, s].astype(F32)
        m_new = ADAM_B1 * m_ref[0] + (1.0 - ADAM_B1) * g
        v_new = ADAM_B2 * v_ref[0] + (1.0 - ADAM_B2) * (g * g)
        go_ref[0] = g
        mo_ref[0] = m_new
        vo_ref[0] = v_new
        d_ref[0] = -ADAM_LR * ((m_new / c1) / (jnp.sqrt(v_new / c2) + ADAM_EPS) + ADAM_WD * w_ref[0])

    tile = pl.BlockSpec((1, tr, cols), lambda l, i: (l, i, 0))
    return pl.pallas_call(
        body, name=name, grid=(depth, rows // tr),
        in_specs=[pl.BlockSpec((1, N_DEV, tr, cols), lambda l, i: (l, 0, i, 0)), tile, tile, tile],
        out_specs=[tile] * 4, out_shape=[jax.ShapeDtypeStruct(w.shape, F32)] * 4,
        compiler_params=_cparams(("parallel", "parallel")),
    )(parts, w, m, v)


def _pack_small(conv, small):
    lead = conv.shape[:-2]
    flat = jnp.concatenate([conv.reshape(lead + (CS_CONV,))] + list(small), axis=-1)
    pad = CS_ROWS * LANE - flat.shape[-1]
    flat = jnp.concatenate([flat, jnp.zeros(lead + (pad,), F32)], axis=-1)
    return flat.reshape(lead + (CS_ROWS, LANE))


def _unpack_small(p):
    flat = p.reshape(DEPTH, CS_ROWS * LANE)
    conv = flat[:, :CS_CONV].reshape(DEPTH, CONV_K, CONV_SHARD_COLS)
    small, off = [], CS_CONV
    for _, n in SMALL_SIZES:
        small.append(flat[:, off:off + n])
        off += n
    return conv, small


def kernel(x, w_in, conv_w, a_log, dt_bias, norm_w, sinks, w_out, ln_g, ln_b, loss_target, m_w_in, m_conv_w, m_a_log, m_dt_bias, m_norm_w, m_sinks, m_w_out, m_ln_g, m_ln_b, v_w_in, v_conv_w, v_a_log, v_dt_bias, v_norm_w, v_sinks, v_w_out, v_ln_g, v_ln_b):
    small = [a_log, dt_bias, norm_w, sinks, ln_g, ln_b]
    shards = [[w_in[l].astype(BF16), w_out[l].astype(BF16), conv_w[l]] for l in range(DEPTH)]
    g_in0, = _all_gather(shards[0][:1], name="weights_all_gather_0")
    weights = [[_full_w_in(g_in0, "w_in_columns_0"), None, None]] + [[None, None, None]] * (DEPTH - 1)

    _, saved, weights = _forward(x[0], weights, shards, small)
    dx, g1, _, loss_lanes = _backward_layer(1, None, saved[1], weights[1], ln_g[1],
                                            loss=(loss_target[0], ln_b[1][None, :]))
    loss = lax.psum(0.5 * jnp.sum(loss_lanes) * (1.0 / D_MODEL), ("x", "y", "c"))
    dx, _, (p_in, p_out, p_small), _ = _backward_layer(0, dx, saved[0], weights[0], ln_g[0],
                                                       above=_contributions(g1))

    o_in = _adamw(p_in, w_in, m_w_in, v_w_in, tr=256, name="adamw_w_in")
    o_out = _adamw(p_out, w_out, m_w_out, v_w_out, tr=OUT_SHARD_ROWS, name="adamw_w_out")
    o_small = _adamw(p_small, _pack_small(conv_w, small),
                     _pack_small(m_conv_w, [m_a_log, m_dt_bias, m_norm_w, m_sinks, m_ln_g, m_ln_b]),
                     _pack_small(v_conv_w, [v_a_log, v_dt_bias, v_norm_w, v_sinks, v_ln_g, v_ln_b]),
                     tr=CS_ROWS, name="adamw_small")
    outs = []
    for k in range(4):
        cv, sm = _unpack_small(o_small[k])
        outs += [o_in[k], cv, sm[0], sm[1], sm[2], sm[3], o_out[k], sm[4], sm[5]]
    return (loss, dx[None], *outs)
```

```python
import functools

import jax
import jax.numpy as jnp
from jax import lax
from jax.experimental import pallas as pl
from jax.experimental.pallas import tpu as pltpu

F32 = jnp.float32
BF16 = jnp.bfloat16
MM_DTYPE = BF16

N_DEV = 8
D_MODEL = 1024
DEPTH = 2
A_HEADS = 4
A_HEAD_DIM = 128
A_WIDTH = 512
CONV_K = 4
SUPER = 256
NEWTON_STEPS = 1
B_Q_HEADS = 8
B_KV_HEADS = 2
B_HEAD_DIM = 64
B_GROUP = 4
B_WIDTH = 512
WINDOW = 128
BLOCK = 128
IN_COLS = 3336
SHARD_COLS = IN_COLS // N_DEV
OUT_SHARD_ROWS = D_MODEL // N_DEV
CONV_SHARD_COLS = 3 * A_WIDTH // N_DEV
DEEPNORM_ALPHA = (2 * DEPTH) ** 0.25
LN_EPS = 1e-5
RMS_EPS = 1e-6
L2_EPS = 1e-6
ADAM_LR, ADAM_B1, ADAM_B2, ADAM_EPS, ADAM_WD, ADAM_STEP = 0.001, 0.9, 0.999, 1e-08, 0.01, 10

LANE = 128
L_QB, L_ZB, L_KB, L_VB, L_ZA, L_BA, L_QKV = 0, 512, 1024, 1152, 1280, 1792, 1920
L_SWA = 1280
L_GATE = 640
L_COLS = 3456
SMALL_SIZES = (("a_log", 4), ("dt_bias", 4), ("norm_w", 128), ("sinks", 8), ("ln_g", 1024), ("ln_b", 1024))
CS_CONV = CONV_K * CONV_SHARD_COLS
CS_ROWS = 24
VMEM_LIMIT = 48 * 1024 * 1024


def _cparams(sem=None):
    return pltpu.CompilerParams(dimension_semantics=sem, vmem_limit_bytes=VMEM_LIMIT)


def _mm(a, b):
    return jnp.dot(a.astype(MM_DTYPE), b.astype(MM_DTYPE), preferred_element_type=F32)


def _mm_nt(a, b):
    return lax.dot_general(a.astype(MM_DTYPE), b.astype(MM_DTYPE), (((1,), (1,)), ((), ())),
                           preferred_element_type=F32)


def _mm_tn(a, b):
    return lax.dot_general(a.astype(MM_DTYPE), b.astype(MM_DTYPE), (((0,), (0,)), ((), ())),
                           preferred_element_type=F32)


def _split(a):
    hi = a.astype(BF16)
    return hi, (a - hi.astype(F32)).astype(BF16)


def _silu(x):
    return x * jax.nn.sigmoid(x)


@jax.custom_vjp
def _stack(parts):
    return jnp.stack(parts)


_stack.defvjp(lambda parts: (jnp.stack(parts), None), lambda _, g: (tuple(g[i] for i in range(g.shape[0])),))


def _softplus(x):
    return jnp.maximum(x, 0.0) + jnp.log1p(jnp.exp(-jnp.abs(x)))


_ANY = pl.BlockSpec(memory_space=pl.ANY)


def _me():
    return lax.axis_index("x"), lax.axis_index("y"), lax.axis_index("c")


def _flat_id(pos):
    return 4 * pos[0] + 2 * pos[1] + pos[2]


def _remote(src, dst, send_sem, recv_sem, to):
    return pltpu.make_async_remote_copy(src_ref=src, dst_ref=dst, send_sem=send_sem, recv_sem=recv_sem,
                                        device_id=to, device_id_type=pl.DeviceIdType.MESH)


class _Direct:
    def __init__(self, items, bufs):
        self.items, self.bufs = list(items), list(bufs)
        self.n_src, self.n_buf = len(self.items), len(self.bufs)
        self.old = [j for j, b in enumerate(self.bufs) if not isinstance(b, jax.ShapeDtypeStruct)]
        self.args = [it[0] for it in self.items] + [self.bufs[j] for j in self.old]
        self.out_shape = [jax.ShapeDtypeStruct(b.shape, b.dtype) for b in self.bufs]
        self.scratch = [pltpu.SemaphoreType.DMA((self.n_src, N_DEV - 1)),
                        pltpu.SemaphoreType.DMA((self.n_src, N_DEV - 1)), pltpu.SemaphoreType.DMA((self.n_src,))]

    def aliases(self, in_base, out_base):
        return {in_base + self.n_src + pos: out_base + j for pos, j in enumerate(self.old)}

    def copies(self, in_refs, out_refs, sems):
        send_sems, recv_sems, local_sems = sems
        x, y, c = _me()
        me = _flat_id((x, y, c))
        peers = [(x ^ ((rel >> 2) & 1), y ^ ((rel >> 1) & 1), c ^ (rel & 1)) for rel in range(1, N_DEV)]
        local, sends, recvs = [], [], []
        for a, (_, per_dest, j, prefix, *rest) in enumerate(self.items):
            src = lambda d: in_refs[a].at[d] if per_dest else in_refs[a]
            dst = lambda s: out_refs[j].at[tuple(prefix) + (s,) + tuple(rest[0] if rest else ())]
            local.append(pltpu.make_async_copy(src(me), dst(me), local_sems.at[a]))
            for k, peer in enumerate(peers):
                pid = _flat_id(peer)
                sends.append(_remote(src(pid), dst(me), send_sems.at[a, k], recv_sems.at[a, k], peer))
                recvs.append(_remote(src(pid), dst(pid), send_sems.at[a, k], recv_sems.at[a, k], peer))
        return local, sends, recvs

    def start(self, in_refs, out_refs, sems):
        local, sends, _ = self.copies(in_refs, out_refs, sems)
        for cp in local + sends:
            cp.start()

    def wait(self, in_refs, out_refs, sems):
        local, sends, recvs = self.copies(in_refs, out_refs, sems)
        for cp in recvs:
            cp.wait_recv()
        for cp in sends:
            cp.wait_send()
        for cp in local:
            cp.wait()


def _pcall(core, *, name, grid, in_specs, out_specs, out_shape, args, sem, scratch_shapes=(), aliases=None,
           rider=None):
    n_in, n_out, n_scr = len(in_specs), len(out_specs), len(scratch_shapes)
    n_rin, n_rout = (len(rider.args), rider.n_buf) if rider else (0, 0)

    def body(*refs):
        ins, outs = refs[:n_in], refs[n_in + n_rin:n_in + n_rin + n_out]
        scr = refs[n_in + n_rin + n_out + n_rout:n_in + n_rin + n_out + n_rout + n_scr]
        if rider:
            r_refs = (refs[n_in:n_in + rider.n_src], refs[n_in + n_rin + n_out:n_in + n_rin + n_out + n_rout],
                      refs[n_in + n_rin + n_out + n_rout + n_scr:])
            ids = [pl.program_id(d) for d in range(len(grid))]
            first = functools.reduce(lambda p, q: p & q, [i == 0 for i in ids])
            last = functools.reduce(lambda p, q: p & q, [i == g - 1 for i, g in zip(ids, grid)])
            pl.when(first)(lambda: rider.start(*r_refs))
        core(ins, outs, scr)
        if rider:
            pl.when(last)(lambda: rider.wait(*r_refs))

    aliases = dict(aliases or {})
    if rider:
        sem = ("arbitrary",) * len(grid)
        aliases.update(rider.aliases(n_in, n_out))
    return pl.pallas_call(
        body, name=name, grid=grid, in_specs=list(in_specs) + [_ANY] * n_rin,
        out_specs=list(out_specs) + [_ANY] * n_rout,
        out_shape=list(out_shape) + (rider.out_shape if rider else []),
        scratch_shapes=list(scratch_shapes) + (rider.scratch if rider else []),
        input_output_aliases=aliases, compiler_params=_cparams(sem),
    )(*args, *(rider.args if rider else []))


def _exchange(direct, *, name):
    n_in = len(direct.args)

    def body(*refs):
        r_refs = refs[:direct.n_src], refs[n_in:n_in + direct.n_buf], refs[n_in + direct.n_buf:]
        direct.start(*r_refs)
        direct.wait(*r_refs)

    return pl.pallas_call(
        body, name=name, in_specs=[_ANY] * n_in, out_specs=[_ANY] * direct.n_buf, out_shape=direct.out_shape,
        input_output_aliases=direct.aliases(0, 0), scratch_shapes=direct.scratch,
    )(*direct.args)


def _matmul(a, b, *, form, tm, tn, tk, name, add=None, add_scale=1.0, rider=None, b_cols=None):
    if form == "nn":
        (m, kk), n = a.shape, b.shape[1]
        a_spec = pl.BlockSpec((tm, tk), lambda i, j, k: (i, k))
        b_spec = pl.BlockSpec((tk, tn), lambda i, j, k: (k, j))
        dn = (((1,), (0,)), ((), ()))
    elif form == "nt":
        (m, kk), n = a.shape, b.shape[0]
        a_spec = pl.BlockSpec((tm, tk), lambda i, j, k: (i, k))
        b_spec = pl.BlockSpec((tn, tk), lambda i, j, k: (j, k))
        dn = (((1,), (1,)), ((), ()))
    else:
        kk, m = a.shape
        n0, n = b_cols or (0, b.shape[1])
        assert n0 % tn == 0
        a_spec = pl.BlockSpec((tk, tm), lambda i, j, k: (k, i))
        b_spec = pl.BlockSpec((tk, tn), lambda i, j, k: (k, j + n0 // tn))
        dn = (((0,), (0,)), ((), ()))
    assert m % tm == 0 and n % tn == 0 and kk % tk == 0, (name, m, n, kk)
    has_add = add is not None

    def core(ins, outs, _):
        a_ref, b_ref = ins[:2]
        o_ref = outs[0]
        k = pl.program_id(2)
        p = lax.dot_general(a_ref[...].astype(MM_DTYPE), b_ref[...].astype(MM_DTYPE), dn,
                            preferred_element_type=F32)

        @pl.when(k == 0)
        def _():
            o_ref[...] = p + add_scale * ins[2][...] if has_add else p

        @pl.when(k > 0)
        def _():
            o_ref[...] += p

    in_specs = [a_spec, b_spec]
    args = [a, b]
    if has_add:
        in_specs.append(pl.BlockSpec((tm, tn), lambda i, j, k: (i, j)))
        args.append(add)
    res = _pcall(core, name=name, grid=(m // tm, n // tn, kk // tk), in_specs=in_specs,
                 out_specs=[pl.BlockSpec((tm, tn), lambda i, j, k: (i, j))],
                 out_shape=[jax.ShapeDtypeStruct((m, n), F32)], args=args,
                 sem=("parallel", "parallel", "arbitrary"), rider=rider)
    return res if rider else res[0]


ZERO_TAIL = 8


def _with_tail(x):
    return jnp.concatenate([x, jnp.zeros((ZERO_TAIL,) + x.shape[1:], x.dtype)], axis=0)


def _shift_down(x, k):
    return pltpu.roll(x, k, 0)


def _shift_up(x, k):
    return pltpu.roll(x, x.shape[0] - k, 0)


def _conv_slab(x, w):
    return w[3:4] * x + w[2:3] * _shift_down(x, 1) + w[1:2] * _shift_down(x, 2) + w[0:1] * _shift_down(x, 3)


def _prep_fwd(h, conv_w, *, name):
    t_len = h.shape[0]

    def body(x_ref, w_ref, o_ref):
        s = pl.program_id(0)
        y = _silu(_conv_slab(_with_tail(x_ref[...]), w_ref[...])[:t_len])
        rs = lax.rsqrt(jnp.sum(y * y, axis=-1, keepdims=True) + L2_EPS)
        scale = jnp.where(s < A_HEADS, A_HEAD_DIM ** -0.5, 1.0)
        o_ref[...] = jnp.where(s < 2 * A_HEADS, y * rs * scale, y)

    return pl.pallas_call(
        body, name=name, grid=(12,),
        in_specs=[pl.BlockSpec((t_len, LANE), lambda s: (0, L_QKV // LANE + s)),
                  pl.BlockSpec((8, LANE), lambda s: (0, s))],
        out_specs=pl.BlockSpec((t_len, LANE), lambda s: (0, s)),
        out_shape=jax.ShapeDtypeStruct((t_len, 3 * A_WIDTH), F32),
        compiler_params=_cparams(("parallel",)),
    )(h, conv_w)


def _prep_bwd(h, conv_w, d_out, dh, *, name):
    t_len = h.shape[0]

    def body(x_ref, w_ref, g_ref, dh_in, dx_ref, dw_ref):
        del dh_in
        s = pl.program_id(0)
        x = _with_tail(x_ref[...])
        g = _with_tail(g_ref[0])
        w = w_ref[...]
        xs = [_shift_down(x, 3), _shift_down(x, 2), _shift_down(x, 1), x]
        c = w[0:1] * xs[0] + w[1:2] * xs[1] + w[2:3] * xs[2] + w[3:4] * xs[3]
        sg = jax.nn.sigmoid(c)
        y = c * sg
        rs = lax.rsqrt(jnp.sum(y * y, axis=-1, keepdims=True) + L2_EPS)
        scale = jnp.where(s < A_HEADS, A_HEAD_DIM ** -0.5, 1.0)
        dy_n = scale * (rs * g - y * (rs * rs * rs) * jnp.sum(g * y, axis=-1, keepdims=True))
        dy = jnp.where(s < 2 * A_HEADS, dy_n, g)
        dc = dy * (sg * (1.0 + c * (1.0 - sg)))
        dx = w[3:4] * dc + w[2:3] * _shift_up(dc, 1) + w[1:2] * _shift_up(dc, 2) + w[0:1] * _shift_up(dc, 3)
        dx_ref[...] = dx[:t_len]
        dws = [jnp.sum(dc * xs[j], axis=0, keepdims=True) for j in range(CONV_K)]
        dw_ref[...] = jnp.concatenate(dws + [jnp.zeros((8 - CONV_K, LANE), F32)], axis=0)

    slab = pl.BlockSpec((t_len, LANE), lambda s: (0, L_QKV // LANE + s))
    return pl.pallas_call(
        body, name=name, grid=(12,),
        in_specs=[slab, pl.BlockSpec((8, LANE), lambda s: (0, s)),
                  pl.BlockSpec((1, t_len, LANE), lambda s: (s // A_HEADS, 0, s % A_HEADS)), _ANY],
        out_specs=[slab, pl.BlockSpec((8, LANE), lambda s: (0, s))],
        out_shape=[jax.ShapeDtypeStruct((t_len, L_COLS), F32), jax.ShapeDtypeStruct((8, 3 * A_WIDTH), F32)],
        input_output_aliases={3: 0},
        compiler_params=_cparams(("parallel",)),
    )(h, conv_w, d_out, dh)


N_LEVELS = 5
MF_TRIL, MF_STRIL, MF_DIAG8, MF_LOW16, MF_EYE = 0, 1, 2, 3, 3 + N_LEVELS
MB_CUM, MB_CUM_T, MB_TOT = 0, 1, 2


def _gdn_masks():
    r = lax.broadcasted_iota(jnp.int32, (SUPER, SUPER), 0)
    c = lax.broadcasted_iota(jnp.int32, (SUPER, SUPER), 1)
    same = lambda shift: (r >> shift) == (c >> shift)
    ninf = lambda m: jnp.where(m, 0.0, -jnp.inf).astype(F32)
    one = lambda m: m.astype(F32)
    mf = jnp.stack([ninf(r >= c), ninf(r > c), one(same(3))]
                   + [one(same(4 + lv) & jnp.logical_not(same(3 + lv))) for lv in range(N_LEVELS)] + [one(r == c)])
    mb = jnp.stack([one(r >= c), one(r <= c), jnp.ones((SUPER, SUPER), F32)]).astype(BF16)
    return mf, mb


def _tri_inv_impl(a, mf):
    d = lambda p, q: jnp.dot(p.astype(BF16), q.astype(BF16), preferred_element_type=F32)
    dd = lambda p, q: jnp.dot(p, q, preferred_element_type=F32)
    eye = mf[MF_EYE]
    a0 = a * mf[MF_DIAG8]
    a2 = d(a0, a0)
    a4 = d(a2, a2)
    t = d(d(eye - a0, eye + a2), eye + a4)
    for level in range(N_LEVELS):
        t = t - d(d(t, a * mf[MF_LOW16 + level]), t)
    a_hi, a_lo = _split(a)
    for _ in range(NEWTON_STEPS):
        t0 = t.astype(BF16)
        t0f = t0.astype(F32)
        resid = (eye - t0f) - (dd(a_hi, t0) + dd(a_lo, t0))
        r_hi, r_lo = _split(resid)
        t = t0f + (dd(t0, r_hi) + dd(t0, r_lo))
    return t


@jax.custom_vjp
def _wy_apply(a, rhs, t):
    return _mm(t, rhs)


def _wy_apply_fwd(a, rhs, t):
    x = _mm(t, rhs)
    return x, (t, x)


def _wy_apply_bwd(res, dx):
    t, x = res
    d_rhs = _mm_tn(t, dx)
    return -_mm_nt(d_rhs, x), d_rhs, jnp.zeros_like(t)


_wy_apply.defvjp(_wy_apply_fwd, _wy_apply_bwd)


@functools.partial(jax.custom_vjp, nondiff_argnums=(1,))
def _lane_roll(x, shift):
    return pltpu.roll(x, shift % LANE, 1)


_lane_roll.defvjp(lambda x, shift: (_lane_roll(x, shift), None), lambda shift, _, g: (_lane_roll(g, -shift),))


def _mask_times_lanes(x, mask):
    lane = lax.broadcasted_iota(jnp.int32, (1, LANE), 1)
    x = jnp.where(lane < A_HEADS, x, 0.0)
    x1 = x.astype(BF16).astype(F32)
    x2 = (x - x1).astype(BF16).astype(F32)
    x3 = (x - x1 - x2).astype(BF16).astype(F32)
    pieces = x1 + pltpu.roll(x2, A_HEADS, 1) + pltpu.roll(x3, 2 * A_HEADS, 1)
    res = jnp.dot(mask, pieces.astype(BF16), preferred_element_type=F32)
    return res + pltpu.roll(res, LANE - A_HEADS, 1) + pltpu.roll(res, LANE - 2 * A_HEADS, 1)


@jax.custom_vjp
def _chunk_sums(g, mb):
    return _mask_times_lanes(g, mb[MB_CUM]), _mask_times_lanes(g, mb[MB_TOT])


def _chunk_sums_fwd(g, mb):
    return _chunk_sums(g, mb), mb


def _chunk_sums_bwd(mb, d):
    lane = lax.broadcasted_iota(jnp.int32, (1, LANE), 1)
    dg = _mask_times_lanes(d[0], mb[MB_CUM_T]) + _mask_times_lanes(d[1], mb[MB_TOT])
    return jnp.where(lane < A_HEADS, dg, 0.0), jnp.zeros_like(mb)


_chunk_sums.defvjp(_chunk_sums_fwd, _chunk_sums_bwd)


def _gdn_gates(ba, alog, dtb, mb):
    beta = jax.nn.sigmoid(ba)
    g = -jnp.exp(alog) * _softplus(_lane_roll(ba, -A_HEADS) + dtb)
    gc, gl = _chunk_sums(g, mb)
    return beta, gc, gl, gc.T


def _gdn_block(s, q, k, v, z, gates, nw, h, t_known, mf):
    n = q.shape[0]
    beta_all, gc_all, gl_all, gct_all = gates
    lane = lax.broadcasted_iota(jnp.int32, (1, LANE), 1)
    sub = lax.broadcasted_iota(jnp.int32, (LANE, 1), 0)
    col = lambda x: jnp.sum(jnp.where(lane == h, x, 0.0), axis=1, keepdims=True)
    wide = lambda c: jnp.broadcast_to(c, (n, LANE))
    gc, gl = col(gc_all), col(gl_all)
    gc_row = jnp.sum(jnp.where(sub == h, gct_all, 0.0), axis=0, keepdims=True)
    beta_w, eg_w = wide(col(beta_all)), wide(jnp.exp(gc))
    diff = gc - gc_row
    decay = jnp.exp(diff + mf[MF_TRIL])
    kb = k * beta_w
    a_mat = _mm_nt(kb, k) * jnp.exp(diff + mf[MF_STRIL])
    rhs = jnp.concatenate([v * beta_w, kb * eg_w], axis=1)
    if t_known is None:
        t_mat = _tri_inv_impl(a_mat, mf)
        uw = _mm(t_mat, rhs)
    else:
        t_mat = t_known
        uw = _wy_apply(a_mat, rhs, t_known)
    u, w = uw[:, :LANE], uw[:, LANE:]
    qk = _mm_nt(q, k) * decay
    q_dec = q * eg_w
    k_dec = k * wide(jnp.exp(gl - gc))
    v_new = u - _mm(w, s)
    o = _mm(q_dec, s) + _mm(qk, v_new)
    s = s * jnp.exp(gl[0:1]) + _mm_tn(k_dec, v_new)
    o = o * lax.rsqrt(jnp.mean(o * o, axis=-1, keepdims=True) + RMS_EPS) * nw
    return o * _silu(z), s, t_mat


def _gdn_fwd(qkv, h, alog, dtb, nw, ycat, *, name, rider=None):
    t_len = qkv.shape[0]
    nsc = t_len // SUPER

    def core(ins, outs, scr):
        q_ref, k_ref, v_ref, gate_ref, al_ref, dt_ref, nw_ref, mf_ref, mb_ref, _ = ins
        y_ref, sin_ref, t_ref = outs
        s_scr, = scr

        @pl.when(pl.program_id(0) == 0)
        def _():
            s_scr[...] = jnp.zeros_like(s_scr)

        per_head = lambda ref: jnp.stack([ref[:, hh * LANE:(hh + 1) * LANE] for hh in range(A_HEADS)])
        states = s_scr[...]
        gates = _gdn_gates(gate_ref[:, A_WIDTH:], al_ref[...], dt_ref[...], mb_ref[...])
        fn = jax.vmap(_gdn_block, in_axes=(0, 0, 0, 0, 0, None, None, 0, None, None))
        y, s_new, t_mat = fn(states, per_head(q_ref), per_head(k_ref), per_head(v_ref), per_head(gate_ref),
                             gates, nw_ref[...], jnp.arange(A_HEADS), None, mf_ref[...])
        sin_ref[0] = states
        t_ref[0] = t_mat
        s_scr[...] = s_new
        for hh in range(A_HEADS):
            y_ref[:, hh * LANE:(hh + 1) * LANE] = y[hh]

    blk = lambda j: pl.BlockSpec((SUPER, A_WIDTH), lambda sc: (sc, j))
    row = pl.BlockSpec((1, LANE), lambda sc: (0, 0))
    mf, mb = _gdn_masks()
    whole = lambda a: pl.BlockSpec(a.shape, lambda sc: (0, 0, 0))
    return _pcall(
        core, name=name, grid=(nsc,),
        in_specs=[blk(0), blk(1), blk(2), pl.BlockSpec((SUPER, L_GATE), lambda sc: (sc, L_ZA // L_GATE)),
                  row, row, row, whole(mf), whole(mb), _ANY],
        out_specs=[blk(0),
                   pl.BlockSpec((1, A_HEADS, A_HEAD_DIM, A_HEAD_DIM), lambda sc: (sc, 0, 0, 0)),
                   pl.BlockSpec((1, A_HEADS, SUPER, SUPER), lambda sc: (sc, 0, 0, 0))],
        out_shape=[jax.ShapeDtypeStruct((t_len, D_MODEL), F32),
                   jax.ShapeDtypeStruct((nsc, A_HEADS, A_HEAD_DIM, A_HEAD_DIM), F32),
                   jax.ShapeDtypeStruct((nsc, A_HEADS, SUPER, SUPER), F32)],
        scratch_shapes=[pltpu.VMEM((A_HEADS, A_HEAD_DIM, A_HEAD_DIM), F32)],
        aliases={9: 0}, sem=("arbitrary",), rider=rider,
        args=(qkv, qkv, qkv, h, alog, dtb, nw, mf, mb, ycat))


def _gdn_bwd(qkv, h, alog, dtb, nw, s_in, t_in, dycat, dh, *, name, rider=None):
    t_len = qkv.shape[0]
    nsc = t_len // SUPER

    def core(ins, outs, scr):
        q_ref, k_ref, v_ref, gate_ref, al_ref, dt_ref, nw_ref, sin_ref, t_ref, dy_ref, mf_ref, mb_ref, _ = ins
        dgate_ref, dqkv_ref, dal_ref, ddt_ref, dnw_ref = outs
        ds_scr, = scr

        @pl.when(pl.program_id(0) == 0)
        def _():
            ds_scr[...] = jnp.zeros_like(ds_scr)
            dal_ref[...] = jnp.zeros_like(dal_ref)
            ddt_ref[...] = jnp.zeros_like(ddt_ref)
            dnw_ref[...] = jnp.zeros_like(dnw_ref)

        per_head = lambda ref: jnp.stack([ref[:, hh * LANE:(hh + 1) * LANE] for hh in range(A_HEADS)])
        head_ids = jnp.arange(A_HEADS)
        t_known, mf, mb = t_ref[0], mf_ref[...], mb_ref[...]

        def fn(s, q, k, v, z, ba, alog, dtb, nw):
            gates = _gdn_gates(ba, alog, dtb, mb)
            one = lambda s, q, k, v, z, t, h: _gdn_block(s, q, k, v, z, gates, nw, h, t, mf)[:2]
            return jax.vmap(one)(s, q, k, v, z, t_known, head_ids)

        _, vjp = jax.vjp(fn, sin_ref[0], per_head(q_ref), per_head(k_ref), per_head(v_ref), per_head(gate_ref),
                         gate_ref[:, A_WIDTH:], al_ref[...], dt_ref[...], nw_ref[...])
        ds, dq, dk, dv, dz, dba, dal, ddt, dnw = vjp((per_head(dy_ref), ds_scr[...]))
        ds_scr[...] = ds
        for hh in range(A_HEADS):
            cols = slice(hh * LANE, (hh + 1) * LANE)
            dqkv_ref[0, :, cols] = dq[hh]
            dqkv_ref[1, :, cols] = dk[hh]
            dqkv_ref[2, :, cols] = dv[hh]
            dgate_ref[:, cols] = dz[hh]
        dgate_ref[:, A_WIDTH:] = dba
        dal_ref[...] += dal
        ddt_ref[...] += ddt
        dnw_ref[...] += dnw

    rev = lambda i: nsc - 1 - i
    blk = lambda j: pl.BlockSpec((SUPER, A_WIDTH), lambda i: (rev(i), j))
    gate = pl.BlockSpec((SUPER, L_GATE), lambda i: (rev(i), L_ZA // L_GATE))
    row = pl.BlockSpec((1, LANE), lambda i: (0, 0))
    mf, mb = _gdn_masks()
    whole = lambda a: pl.BlockSpec(a.shape, lambda i: (0, 0, 0))
    return _pcall(
        core, name=name, grid=(nsc,),
        in_specs=[blk(0), blk(1), blk(2), gate, row, row, row,
                  pl.BlockSpec((1, A_HEADS, A_HEAD_DIM, A_HEAD_DIM), lambda i: (rev(i), 0, 0, 0)),
                  pl.BlockSpec((1, A_HEADS, SUPER, SUPER), lambda i: (rev(i), 0, 0, 0)),
                  blk(0), whole(mf), whole(mb), _ANY],
        out_specs=[gate, pl.BlockSpec((3, SUPER, A_WIDTH), lambda i: (0, rev(i), 0)), row, row, row],
        out_shape=[jax.ShapeDtypeStruct((t_len, L_COLS), F32), jax.ShapeDtypeStruct((3, t_len, A_WIDTH), F32)]
        + [jax.ShapeDtypeStruct((1, LANE), F32)] * 3,
        scratch_shapes=[pltpu.VMEM((A_HEADS, A_HEAD_DIM, A_HEAD_DIM), F32)],
        aliases={12: 0}, sem=("arbitrary",), rider=rider,
        args=(qkv, qkv, qkv, h, alog, dtb, nw, s_in, t_in, dycat, mf, mb, dh))


Q_BLOCKS = 4
Q_ROWS = Q_BLOCKS * BLOCK


def _swa_block(q, kp, kc, vp, vc, z, sinks, first):
    rows = B_GROUP * BLOCK
    ri = lax.broadcasted_iota(jnp.int32, (rows, 2 * BLOCK), 0)
    si = lax.broadcasted_iota(jnp.int32, (rows, 2 * BLOCK), 1)
    dist = (ri & (BLOCK - 1)) + BLOCK - si
    bias = jnp.where((dist >= 0) & (dist < WINDOW), 0.0, -jnp.inf)
    no_prev = jnp.where(first & (si[:1] < BLOCK), -jnp.inf, 0.0)
    dist_f = dist.astype(F32)
    head_of_row = lax.broadcasted_iota(jnp.int32, (rows, 1), 0) >> 7
    keys = jnp.concatenate([kp, kc], axis=0)
    vals = jnp.concatenate([vp, vc], axis=0)

    def item(b, j):
        cs = slice(j * B_HEAD_DIM, (j + 1) * B_HEAD_DIM)
        rs = slice(b * BLOCK, (b + 1) * BLOCK)
        heads = range(j * B_GROUP, (j + 1) * B_GROUP)
        qs = jnp.concatenate([q[rs, hq * B_HEAD_DIM:(hq + 1) * B_HEAD_DIM] for hq in heads], axis=0) * (
            B_HEAD_DIM ** -0.5)
        kk = keys[b * BLOCK:(b + 2) * BLOCK, cs]
        vv = vals[b * BLOCK:(b + 2) * BLOCK, cs]
        sink = jnp.concatenate([jnp.broadcast_to(sinks[:, hq:hq + 1], (BLOCK, 1)) for hq in heads], axis=0)
        slope = sum(jnp.where(head_of_row == gi, 2.0 ** (-8.0 * (hq + 1) / B_Q_HEADS), 0.0)
                    for gi, hq in enumerate(heads))
        return qs, kk, vv, sink, slope, (no_prev if b == 0 else jnp.zeros_like(no_prev))

    def attend(qs, kk, vv, sink, slope, hide):
        sc = _mm_nt(qs, kk) - slope * dist_f + (bias + hide)
        m = lax.stop_gradient(jnp.maximum(jnp.max(sc, axis=-1, keepdims=True), sink))
        p = jnp.exp(sc - m)
        inv = 1.0 / (jnp.sum(p, axis=-1, keepdims=True) + jnp.exp(sink - m))
        return _mm(p * inv, vv)

    items = [(b, j) for b in range(Q_BLOCKS) for j in range(B_KV_HEADS)]
    o = jax.vmap(attend)(*[_stack(t) for t in zip(*[item(b, j) for b, j in items])])
    rows_out = [jnp.concatenate([o[b * B_KV_HEADS + j, gi * BLOCK:(gi + 1) * BLOCK]
                                 for j in range(B_KV_HEADS) for gi in range(B_GROUP)], axis=1)
                for b in range(Q_BLOCKS)]
    return jnp.concatenate(rows_out, axis=0) * _silu(z)


def _swa_specs(idx):
    wide = lambda off: pl.BlockSpec((Q_ROWS, B_WIDTH), lambda n: (idx(n), off))
    cur = lambda off: pl.BlockSpec((Q_ROWS, LANE), lambda n: (idx(n), off))
    prev = lambda off: pl.BlockSpec((BLOCK, LANE), lambda n: (jnp.maximum(idx(n) * Q_BLOCKS - 1, 0), off))
    return [wide(L_QB // B_WIDTH), prev(L_KB // LANE), cur(L_KB // LANE), prev(L_VB // LANE), cur(L_VB // LANE),
            wide(L_ZB // B_WIDTH), pl.BlockSpec((1, LANE), lambda n: (0, 0))]


def _swa_fwd(h, sinks, *, name, rider=None):
    t_len = h.shape[0]
    nb = t_len // Q_ROWS

    def core(ins, outs, _):
        q_ref, kp_ref, kc_ref, vp_ref, vc_ref, z_ref, s_ref = ins
        outs[0][...] = _swa_block(q_ref[...], kp_ref[...], kc_ref[...], vp_ref[...], vc_ref[...], z_ref[...],
                                  s_ref[...], pl.program_id(0) == 0)

    res = _pcall(core, name=name, grid=(nb,), in_specs=_swa_specs(lambda n: n),
                 out_specs=[pl.BlockSpec((Q_ROWS, B_WIDTH), lambda n: (n, 1))],
                 out_shape=[jax.ShapeDtypeStruct((t_len, D_MODEL), F32)], sem=("parallel",), rider=rider,
                 args=(h, h, h, h, h, h, sinks))
    return res if rider else res[0]


def _swa_bwd(h, sinks, dycat, *, name, rider=None):
    t_len = h.shape[0]
    nb = t_len // Q_ROWS
    last = slice(Q_ROWS - BLOCK, Q_ROWS)

    def core(ins, outs, scr):
        q_ref, kp_ref, kc_ref, vp_ref, vc_ref, z_ref, s_ref, dy_ref = ins
        dh_ref, dsk_ref = outs
        ck_scr, cv_scr = scr
        i = pl.program_id(0)
        n = nb - 1 - i

        @pl.when(i == 0)
        def _():
            ck_scr[...] = jnp.zeros_like(ck_scr)
            cv_scr[...] = jnp.zeros_like(cv_scr)
            dsk_ref[...] = jnp.zeros_like(dsk_ref)

        fn = functools.partial(_swa_block, first=(n == 0))
        _, vjp = jax.vjp(fn, q_ref[...], kp_ref[...], kc_ref[...], vp_ref[...], vc_ref[...], z_ref[...], s_ref[...])
        dq, dkp, dkc, dvp, dvc, dz, dsk = vjp(dy_ref[...])
        dh_ref[:, L_QB:L_QB + B_WIDTH] = dq
        dh_ref[:, L_ZB:L_ZB + B_WIDTH] = dz
        dh_ref[:, L_KB:L_KB + LANE] = dkc
        dh_ref[:, L_VB:L_VB + LANE] = dvc
        dh_ref[last, L_KB:L_KB + LANE] += ck_scr[...]
        dh_ref[last, L_VB:L_VB + LANE] += cv_scr[...]
        ck_scr[...] = dkp
        cv_scr[...] = dvp
        dsk_ref[...] += dsk

    rev = lambda i: nb - 1 - i
    return _pcall(
        core, name=name, grid=(nb,),
        in_specs=_swa_specs(rev) + [pl.BlockSpec((Q_ROWS, B_WIDTH), lambda i: (rev(i), 1))],
        out_specs=[pl.BlockSpec((Q_ROWS, L_SWA), lambda i: (rev(i), 0)), pl.BlockSpec((1, LANE), lambda i: (0, 0))],
        out_shape=[jax.ShapeDtypeStruct((t_len, L_COLS), F32), jax.ShapeDtypeStruct((1, LANE), F32)],
        scratch_shapes=[pltpu.VMEM((BLOCK, LANE), F32), pltpu.VMEM((BLOCK, LANE), F32)],
        sem=("arbitrary",), rider=rider, args=(h, h, h, h, h, h, sinks, dycat))


def _out_ln_fwd(ycat, w_out, x, ln_g, ln_b, *, name, tm=512, last=False):
    t_len = x.shape[0]

    def body(y_ref, w_ref, x_ref, g_ref, b_ref, r_ref, *o_ref):
        r = DEEPNORM_ALPHA * x_ref[...] + _mm(y_ref[...], w_ref[...])
        r_ref[...] = r
        if not last:
            mu = jnp.mean(r, axis=-1, keepdims=True)
            d = r - mu
            var = jnp.mean(d * d, axis=-1, keepdims=True)
            o_ref[0][...] = d * lax.rsqrt(var + LN_EPS) * g_ref[...] + b_ref[...]

    tile = pl.BlockSpec((tm, D_MODEL), lambda i: (i, 0))
    vec = pl.BlockSpec((1, D_MODEL), lambda i: (0, 0))
    n_out = 1 if last else 2
    res = pl.pallas_call(
        body, name=name, grid=(t_len // tm,),
        in_specs=[tile, pl.BlockSpec((D_MODEL, D_MODEL), lambda i: (0, 0)), tile, vec, vec],
        out_specs=[tile] * n_out,
        out_shape=[jax.ShapeDtypeStruct((t_len, D_MODEL), F32)] * n_out,
        compiler_params=_cparams(("parallel",)),
    )(ycat, w_out, x, ln_g, ln_b)
    return (res[0], None) if last else res


def _ln_bwd(dxn, r, ln_g, *, name, tm=512, loss=None):
    t_len = r.shape[0]

    def body(*refs):
        if loss:
            t_ref, r_ref, g_ref, b_ref, dr_ref, dg_ref, db_ref, l_ref = refs
        else:
            dx_ref, r_ref, g_ref, dr_ref, dg_ref, db_ref = refs

        @pl.when(pl.program_id(0) == 0)
        def _():
            dg_ref[...] = jnp.zeros_like(dg_ref)
            db_ref[...] = jnp.zeros_like(db_ref)
            if loss:
                l_ref[...] = jnp.zeros_like(l_ref)

        rr = r_ref[...]
        mu = jnp.mean(rr, axis=-1, keepdims=True)
        d = rr - mu
        rstd = lax.rsqrt(jnp.mean(d * d, axis=-1, keepdims=True) + LN_EPS)
        xh = d * rstd
        if loss:
            e = (xh * g_ref[...] + b_ref[...]) - t_ref[...]
            dx = e * (1.0 / D_MODEL)
            l_ref[...] += jnp.sum(e * e, axis=0, keepdims=True)
        else:
            dx = dx_ref[...]
        dxh = dx * g_ref[...]
        dr_ref[...] = rstd * (dxh - jnp.mean(dxh, axis=-1, keepdims=True)
                              - xh * jnp.mean(dxh * xh, axis=-1, keepdims=True))
        dg_ref[...] += jnp.sum(dx * xh, axis=0, keepdims=True)
        db_ref[...] += jnp.sum(dx, axis=0, keepdims=True)

    tile = pl.BlockSpec((tm, D_MODEL), lambda i: (i, 0))
    vec = pl.BlockSpec((1, D_MODEL), lambda i: (0, 0))
    vec_shape = jax.ShapeDtypeStruct((1, D_MODEL), F32)
    args = (loss[0], r, ln_g, loss[1]) if loss else (dxn, r, ln_g)
    return pl.pallas_call(
        body, name=name, grid=(t_len // tm,),
        in_specs=[tile, tile, vec] + ([vec] if loss else []), out_specs=[tile, vec, vec] + ([vec] if loss else []),
        out_shape=[jax.ShapeDtypeStruct((t_len, D_MODEL), F32), vec_shape, vec_shape] + ([vec_shape] if loss else []),
        compiler_params=_cparams(("arbitrary",)),
    )(*args)


def _pad_row(v):
    return jnp.zeros((1, LANE), F32).at[0, :v.shape[0]].set(v)


_REGIONS = ((0, 1536, L_QKV), (1536, 2048, L_ZA), (2048, 2056, L_BA), (2056, 2568, L_QB), (2568, 2696, L_KB),
            (2696, 2824, L_VB), (2824, 3336, L_ZB))


def _shard_pieces(regions):
    for a, b, off in regions:
        for d in range(N_DEV):
            lo, hi = max(a, d * SHARD_COLS), min(b, (d + 1) * SHARD_COLS)
            if lo < hi:
                yield d, lo - d * SHARD_COLS, hi - d * SHARD_COLS, off + lo - a


def _as_list(r):
    return list(r) if isinstance(r, (list, tuple)) else [r]


def _gathered(shard):
    return jax.ShapeDtypeStruct((N_DEV,) + shard.shape, shard.dtype)


def _full_w_in(g_in, name):
    by_offset = sorted(_shard_pieces(_REGIONS), key=lambda p: p[3])
    tc = 256

    def body(g_ref, o_ref):
        pieces, row = [], 0
        for d, lo, hi, off in by_offset + [(None, 0, 0, L_COLS)]:
            if off > row:
                pieces.append(jnp.zeros((off - row, tc), g_ref.dtype))
            if d is not None:
                pieces.append(g_ref[d, lo:hi, :])
            row = off + hi - lo
        o_ref[...] = jnp.concatenate(pieces, axis=0)

    return pl.pallas_call(
        body, name=name, grid=(D_MODEL // tc,),
        in_specs=[pl.BlockSpec((N_DEV, SHARD_COLS, tc), lambda i: (0, 0, i))],
        out_specs=pl.BlockSpec((L_COLS, tc), lambda i: (0, i)),
        out_shape=jax.ShapeDtypeStruct((L_COLS, D_MODEL), g_in.dtype),
        compiler_params=_cparams(("parallel",)),
    )(g_in)


def _full_conv(g_conv):
    return jnp.pad(g_conv.transpose(1, 0, 2).reshape(CONV_K, 3 * A_WIDTH), ((0, 8 - CONV_K), (0, 0)))


def _forward(x, weights, shards, small):
    a_log, dt_bias, norm_w, sinks, ln_g, ln_b = small
    tm = min(512, x.shape[0])
    saved, weights = [], [list(w) for w in weights]
    whole = lambda arrs: _Direct([(a, False, j, ()) for j, a in enumerate(arrs)], [_gathered(a) for a in arrs])
    for l in range(DEPTH):
        rider = whole(shards[l][1:]) if weights[l][1] is None else None
        h, *got = _as_list(_matmul(x, weights[l][0], form="nt", tm=tm, tn=L_COLS, tk=D_MODEL, name=f"in_proj_{l}",
                                   rider=rider))
        if rider:
            weights[l][1:] = [got[0].reshape(D_MODEL, D_MODEL), _full_conv(got[1])]
        w_in_l, w_out_l, conv_l = weights[l]
        qkv = _prep_fwd(h, conv_l, name=f"prep_fwd_{l}")
        al, dt, nw, sk = _pad_row(a_log[l]), _pad_row(dt_bias[l]), norm_w[l][None, :], _pad_row(sinks[l])
        ahead = l + 1 < DEPTH and weights[l + 1][0] is None
        rider = whole(shards[l + 1][1:]) if ahead else None
        ycat, *got = _as_list(_swa_fwd(h, sk, name=f"swa_fwd_{l}", rider=rider))
        if ahead:
            weights[l + 1][1:] = [got[0].reshape(D_MODEL, D_MODEL), _full_conv(got[1])]
        rider = whole(shards[l + 1][:1]) if ahead else None
        ycat, s_in, t_in, *got = _gdn_fwd(qkv, h, al, dt, nw, ycat, name=f"gdn_fwd_{l}", rider=rider)
        if ahead:
            weights[l + 1][0] = _full_w_in(got[0], f"w_in_rows_{l + 1}")
        r, xn = _out_ln_fwd(ycat, w_out_l, x, ln_g[l][None, :], ln_b[l][None, :], name=f"out_ln_{l}",
                            last=(l == DEPTH - 1))
        saved.append((x, h, qkv, s_in, t_in, ycat, r, al, dt, nw, sk))
        x = xn
    return x, saved, weights


def _w_in_blocks(g, name):
    cols, tc = g.shape[1], 256
    pieces = list(_shard_pieces(_REGIONS))

    def body(g_ref, o_ref):
        blocks = [[] for _ in range(N_DEV)]
        for d, lo, hi, off in pieces:
            blocks[d].append(g_ref[off:off + hi - lo, :])
        for d in range(N_DEV):
            o_ref[d] = jnp.concatenate(blocks[d], axis=0).astype(BF16)

    return pl.pallas_call(
        body, name=name, grid=(cols // tc,),
        in_specs=[pl.BlockSpec((L_COLS, tc), lambda i: (0, i))],
        out_specs=pl.BlockSpec((N_DEV, SHARD_COLS, tc), lambda i: (0, 0, i)),
        out_shape=jax.ShapeDtypeStruct((N_DEV, SHARD_COLS, cols), BF16),
        compiler_params=_cparams(("parallel",)),
    )(g)


def _small_blocks(g):
    c_conv = g["conv_w"].reshape(CONV_K, N_DEV, CONV_SHARD_COLS).transpose(1, 0, 2)
    c_small = [jnp.broadcast_to(g[n][None], (N_DEV,) + g[n].shape) for n, _ in SMALL_SIZES]
    return _pack_small(c_conv, c_small)


def _contributions(g):
    c_out = g["w_out"].astype(BF16).reshape(N_DEV, OUT_SHARD_ROWS, D_MODEL)
    return _w_in_blocks(g["w_in_rows"], name="w_in_grad_blocks_above"), c_out, _small_blocks(g)


def _backward_layer(l, dx, saved_l, weights_l, ln_g_l, above=None, loss=None):
    x_in, h, qkv, s_in, t_in, ycat, r, al, dt, nw, sk = saved_l
    w_in_l, w_out_l, conv_l = weights_l
    tm = min(512, x_in.shape[0])
    dr, d_lng, d_lnb, *loss_lanes = _ln_bwd(dx, r, ln_g_l[None, :], name=f"ln_bwd_{l}", loss=loss)
    big = min(1024, x_in.shape[0])
    dycat = _matmul(dr, w_out_l, form="nt", tm=big, tn=D_MODEL, tk=D_MODEL, name=f"out_proj_dx_{l}")
    d_wout = _matmul(ycat, dr, form="tn", tm=D_MODEL, tn=D_MODEL, tk=big, name=f"out_proj_dw_{l}")
    rider, p_in, p_out, p_small = None, None, None, None
    recv = lambda c: jax.ShapeDtypeStruct((DEPTH,) + c.shape, c.dtype)
    if above:
        c_out = d_wout.astype(BF16).reshape(N_DEV, OUT_SHARD_ROWS, D_MODEL)
        rider = _Direct([(above[1], True, 0, (l + 1,)), (above[2], True, 1, (l + 1,)), (c_out, True, 0, (l,))],
                        [recv(above[1]), recv(above[2])])
    dh, d_sk, *got = _swa_bwd(h, sk, dycat, name=f"swa_bwd_{l}", rider=rider)
    if above:
        p_out, p_small = got
        rider = _Direct([(above[0], True, 0, (l + 1,))], [recv(above[0])])
    dh, dqkv_n, d_al, d_dt, d_nw, *got = _gdn_bwd(qkv, h, al, dt, nw, s_in, t_in, dycat, dh,
                                                  name=f"gdn_bwd_{l}", rider=rider)
    dh, d_conv = _prep_bwd(h, conv_l, dqkv_n, dh, name=f"prep_bwd_{l}")
    grads = dict(w_out=d_wout, conv_w=d_conv[:CONV_K], a_log=d_al[0, :A_HEADS], dt_bias=d_dt[0, :A_HEADS],
                 norm_w=d_nw[0], sinks=d_sk[0, :B_Q_HEADS], ln_g=d_lng[0], ln_b=d_lnb[0])
    half = D_MODEL // 2
    dw = functools.partial(_matmul, dh, x_in, form="tn", tm=L_COLS, tn=half, tk=tm)
    if not above:
        grads["w_in_rows"] = dw(name=f"in_proj_dw_{l}")
    else:
        p_in, = got
        first = dw(name=f"in_proj_dw_first_{l}", b_cols=(0, half))
        blocks = _w_in_blocks(first, name=f"w_in_grad_blocks_first_{l}")
        rider = _Direct([(blocks, True, 0, (l,), (slice(None), pl.ds(0, half)))], [p_in])
        second, p_in = dw(name=f"in_proj_dw_second_{l}", b_cols=(half, half), rider=rider)
        blocks = _w_in_blocks(second, name=f"w_in_grad_blocks_second_{l}")
        rider = _Direct([(blocks, True, 0, (l,), (slice(None), pl.ds(half, half))),
                         (_small_blocks(grads), True, 1, (l,))], [p_in, p_small])
    dx, *got = _as_list(_matmul(dh, w_in_l, form="nn", tm=tm, tn=D_MODEL, tk=L_COLS, name=f"in_proj_dx_{l}",
                                add=dr, add_scale=DEEPNORM_ALPHA, rider=rider))
    bufs = (got[0], p_out, got[1]) if above else None
    return dx, grads, bufs, (loss_lanes[0] if loss else None)


def _all_gather(shards, *, name):
    n_arr = len(shards)

    def body(*refs):
        x_refs, out_refs = refs[:n_arr], refs[n_arr:2 * n_arr]
        send_sems, recv_sems, local_sems = refs[2 * n_arr:]
        x, y, c = _me()
        me, sibling = (x, y, c), (x, y, 1 - c)
        chips = [(1 - x, y), (x, 1 - y), (1 - x, 1 - y)]

        def copy(a, k, block, to, src=None):
            dst = out_refs[a].at[_flat_id(block)]
            return _remote(dst if src is None else src, dst, send_sems.at[a, k], recv_sems.at[a, k], to)

        mine = [pltpu.make_async_copy(x_refs[a], out_refs[a].at[_flat_id(me)], local_sems.at[a])
                for a in range(n_arr)]
        for cp in mine:
            cp.start()
        first = []
        for a in range(n_arr):
            first.append(copy(a, 0, me, sibling, src=x_refs[a]))
            first += [copy(a, 1 + j, me, (*chip, c), src=x_refs[a]) for j, chip in enumerate(chips)]
        for cp in first:
            cp.start()
        passed = []
        for j, chip in enumerate(chips):
            for a in range(n_arr):
                copy(a, 1 + j, (*chip, c), me).wait_recv()
                fwd = copy(a, 4 + j, (*chip, c), sibling)
                fwd.start()
                passed.append(fwd)
        for a in range(n_arr):
            copy(a, 0, sibling, me).wait_recv()
            for j, chip in enumerate(chips):
                copy(a, 4 + j, (*chip, 1 - c), me).wait_recv()
        for cp in first + passed:
            cp.wait_send()
        for cp in mine:
            cp.wait()

    return pl.pallas_call(
        body, name=name, in_specs=[_ANY] * n_arr, out_specs=[_ANY] * n_arr,
        out_shape=[jax.ShapeDtypeStruct((N_DEV,) + s.shape, s.dtype) for s in shards],
        scratch_shapes=[pltpu.SemaphoreType.DMA((n_arr, N_DEV - 1)), pltpu.SemaphoreType.DMA((n_arr, N_DEV - 1)),
                        pltpu.SemaphoreType.DMA((n_arr,))],
    )(*shards)


def _adamw(parts, w, m, v, *, tr, name):
    depth, rows, cols = w.shape
    c1 = 1.0 - ADAM_B1 ** ADAM_STEP
    c2 = 1.0 - ADAM_B2 ** ADAM_STEP

    def body(g_ref, w_ref, m_ref, v_ref, go_ref, d_ref, mo_ref, vo_ref):
        g = g_ref[0, 0].astype(F32)
        for s in range(1, N_DEV):
            g = g + g_ref[0, s].astype(F32)
        m_new = ADAM_B1 * m_ref[0] + (1.0 - ADAM_B1) * g
        v_new = ADAM_B2 * v_ref[0] + (1.0 - ADAM_B2) * (g * g)
        go_ref[0] = g
        mo_ref[0] = m_new
        vo_ref[0] = v_new
        d_ref[0] = -ADAM_LR * ((m_new / c1) / (jnp.sqrt(v_new / c2) + ADAM_EPS) + ADAM_WD * w_ref[0])

    tile = pl.BlockSpec((1, tr, cols), lambda l, i: (l, i, 0))
    return pl.pallas_call(
        body, name=name, grid=(depth, rows // tr),
        in_specs=[pl.BlockSpec((1, N_DEV, tr, cols), lambda l, i: (l, 0, i, 0)), tile, tile, tile],
        out_specs=[tile] * 4, out_shape=[jax.ShapeDtypeStruct(w.shape, F32)] * 4,
        compiler_params=_cparams(("parallel", "parallel")),
    )(parts, w, m, v)


def _adamw_w_in(parts, w, m, v, *, name):
    c1 = 1.0 - ADAM_B1 ** ADAM_STEP
    c2 = 1.0 - ADAM_B2 ** ADAM_STEP

    def body(g_ref, w_ref, m_ref, v_ref, go_ref, d_ref, mo_ref, vo_ref):
        for l in range(DEPTH):
            g = g_ref[l, 0].astype(F32)
            for s in range(1, N_DEV):
                g = g + g_ref[l, s].astype(F32)
            m_new = ADAM_B1 * m_ref[:, l, :] + (1.0 - ADAM_B1) * g
            v_new = ADAM_B2 * v_ref[:, l, :] + (1.0 - ADAM_B2) * (g * g)
            go_ref[:, l, :] = g
            mo_ref[:, l, :] = m_new
            vo_ref[:, l, :] = v_new
            d_ref[:, l, :] = -ADAM_LR * ((m_new / c1) / (jnp.sqrt(v_new / c2) + ADAM_EPS) + ADAM_WD * w_ref[:, l, :])

    tile = pl.BlockSpec((SHARD_COLS, DEPTH, LANE), lambda i: (0, 0, i))
    return pl.pallas_call(
        body, name=name, grid=(D_MODEL // LANE,),
        in_specs=[pl.BlockSpec((DEPTH, N_DEV, SHARD_COLS, LANE), lambda i: (0, 0, 0, i)), tile, tile, tile],
        out_specs=[tile] * 4, out_shape=[jax.ShapeDtypeStruct(w.shape, F32)] * 4,
        compiler_params=_cparams(("parallel",)),
    )(parts, w, m, v)


def _pack_small(conv, small):
    lead = conv.shape[:-2]
    flat = jnp.concatenate([conv.reshape(lead + (CS_CONV,))] + list(small), axis=-1)
    pad = CS_ROWS * LANE - flat.shape[-1]
    flat = jnp.concatenate([flat, jnp.zeros(lead + (pad,), F32)], axis=-1)
    return flat.reshape(lead + (CS_ROWS, LANE))


def _unpack_small(p):
    flat = p.reshape(DEPTH, CS_ROWS * LANE)
    conv = flat[:, :CS_CONV].reshape(DEPTH, CONV_K, CONV_SHARD_COLS)
    small, off = [], CS_CONV
    for _, n in SMALL_SIZES:
        small.append(flat[:, off:off + n])
        off += n
    return conv, small


def kernel(x, w_in, conv_w, a_log, dt_bias, norm_w, sinks, w_out, ln_g, ln_b, loss_target, m_w_in, m_conv_w, m_a_log, m_dt_bias, m_norm_w, m_sinks, m_w_out, m_ln_g, m_ln_b, v_w_in, v_conv_w, v_a_log, v_dt_bias, v_norm_w, v_sinks, v_w_out, v_ln_g, v_ln_b):
    small = [a_log, dt_bias, norm_w, sinks, ln_g, ln_b]
    w_t, m_t, v_t = (a.transpose(2, 0, 1) for a in (w_in, m_w_in, v_w_in))
    shards = [[w_t[:, l].astype(BF16), w_out[l].astype(BF16), conv_w[l]] for l in range(DEPTH)]
    g_in0, = _all_gather(shards[0][:1], name="weights_all_gather_0")
    weights = [[_full_w_in(g_in0, "w_in_rows_0"), None, None]] + [[None, None, None]] * (DEPTH - 1)

    _, saved, weights = _forward(x[0], weights, shards, small)
    dx, g1, _, loss_lanes = _backward_layer(1, None, saved[1], weights[1], ln_g[1],
                                            loss=(loss_target[0], ln_b[1][None, :]))
    loss = lax.psum(0.5 * jnp.sum(loss_lanes) * (1.0 / D_MODEL), ("x", "y", "c"))
    dx, _, (p_in, p_out, p_small), _ = _backward_layer(0, dx, saved[0], weights[0], ln_g[0],
                                                       above=_contributions(g1))

    o_in = [o.transpose(1, 2, 0) for o in _adamw_w_in(p_in, w_t, m_t, v_t, name="adamw_w_in")]
    o_out = _adamw(p_out, w_out, m_w_out, v_w_out, tr=OUT_SHARD_ROWS, name="adamw_w_out")
    o_small = _adamw(p_small, _pack_small(conv_w, small),
                     _pack_small(m_conv_w, [m_a_log, m_dt_bias, m_norm_w, m_sinks, m_ln_g, m_ln_b]),
                     _pack_small(v_conv_w, [v_a_log, v_dt_bias, v_norm_w, v_sinks, v_ln_g, v_ln_b]),
                     tr=CS_ROWS, name="adamw_small")
    outs = []
    for k in range(4):
        cv, sm = _unpack_small(o_small[k])
        outs += [o_in[k], cv, sm[0], sm[1], sm[2], sm[3], o_out[k], sm[4], sm[5]]
    return (loss, dx[None], *outs)
```

```python
import functools

import jax
import jax.numpy as jnp
from jax import lax
from jax.experimental import pallas as pl
from jax.experimental.pallas import tpu as pltpu

F32 = jnp.float32
BF16 = jnp.bfloat16
MM_DTYPE = BF16

N_DEV = 8
D_MODEL = 1024
DEPTH = 2
A_HEADS = 4
A_HEAD_DIM = 128
A_WIDTH = 512
CONV_K = 4
SUPER = 256
NEWTON_STEPS = 1
B_Q_HEADS = 8
B_KV_HEADS = 2
B_HEAD_DIM = 64
B_GROUP = 4
B_WIDTH = 512
WINDOW = 128
BLOCK = 128
IN_COLS = 3336
SHARD_COLS = IN_COLS // N_DEV
OUT_SHARD_ROWS = D_MODEL // N_DEV
CONV_SHARD_COLS = 3 * A_WIDTH // N_DEV
DEEPNORM_ALPHA = (2 * DEPTH) ** 0.25
LN_EPS = 1e-5
RMS_EPS = 1e-6
L2_EPS = 1e-6
ADAM_LR, ADAM_B1, ADAM_B2, ADAM_EPS, ADAM_WD, ADAM_STEP = 0.001, 0.9, 0.999, 1e-08, 0.01, 10

LANE = 128
L_QB, L_ZB, L_KB, L_VB, L_ZA, L_BA, L_QKV = 0, 512, 1024, 1152, 1280, 1792, 1920
L_SWA = 1280
L_GATE = 640
L_COLS = 3456
SMALL_SIZES = (("a_log", 4), ("dt_bias", 4), ("norm_w", 128), ("sinks", 8), ("ln_g", 1024), ("ln_b", 1024))
CS_CONV = CONV_K * CONV_SHARD_COLS
CS_ROWS = 24
VMEM_LIMIT = 48 * 1024 * 1024


def _cparams(sem=None):
    return pltpu.CompilerParams(dimension_semantics=sem, vmem_limit_bytes=VMEM_LIMIT)


def _mm(a, b):
    return jnp.dot(a.astype(MM_DTYPE), b.astype(MM_DTYPE), preferred_element_type=F32)


def _mm_nt(a, b):
    return lax.dot_general(a.astype(MM_DTYPE), b.astype(MM_DTYPE), (((1,), (1,)), ((), ())),
                           preferred_element_type=F32)


def _mm_tn(a, b):
    return lax.dot_general(a.astype(MM_DTYPE), b.astype(MM_DTYPE), (((0,), (0,)), ((), ())),
                           preferred_element_type=F32)


def _split(a):
    hi = a.astype(BF16)
    return hi, (a - hi.astype(F32)).astype(BF16)


def _silu(x):
    return x * jax.nn.sigmoid(x)


@jax.custom_vjp
def _stack(parts):
    return jnp.stack(parts)


_stack.defvjp(lambda parts: (jnp.stack(parts), None), lambda _, g: (tuple(g[i] for i in range(g.shape[0])),))


def _softplus(x):
    return jnp.maximum(x, 0.0) + jnp.log1p(jnp.exp(-jnp.abs(x)))


_ANY = pl.BlockSpec(memory_space=pl.ANY)


def _me():
    return lax.axis_index("x"), lax.axis_index("y"), lax.axis_index("c")


def _flat_id(pos):
    return 4 * pos[0] + 2 * pos[1] + pos[2]


def _remote(src, dst, send_sem, recv_sem, to):
    return pltpu.make_async_remote_copy(src_ref=src, dst_ref=dst, send_sem=send_sem, recv_sem=recv_sem,
                                        device_id=to, device_id_type=pl.DeviceIdType.MESH)


class _Direct:
    def __init__(self, items, bufs):
        self.items, self.bufs = list(items), list(bufs)
        self.n_src, self.n_buf = len(self.items), len(self.bufs)
        self.old = [j for j, b in enumerate(self.bufs) if not isinstance(b, jax.ShapeDtypeStruct)]
        self.args = [it[0] for it in self.items] + [self.bufs[j] for j in self.old]
        self.out_shape = [jax.ShapeDtypeStruct(b.shape, b.dtype) for b in self.bufs]
        self.scratch = [pltpu.SemaphoreType.DMA((self.n_src, N_DEV - 1)),
                        pltpu.SemaphoreType.DMA((self.n_src, N_DEV - 1)), pltpu.SemaphoreType.DMA((self.n_src,))]

    def aliases(self, in_base, out_base):
        return {in_base + self.n_src + pos: out_base + j for pos, j in enumerate(self.old)}

    def copies(self, in_refs, out_refs, sems):
        send_sems, recv_sems, local_sems = sems
        x, y, c = _me()
        me = _flat_id((x, y, c))
        peers = [(x ^ ((rel >> 2) & 1), y ^ ((rel >> 1) & 1), c ^ (rel & 1)) for rel in range(1, N_DEV)]
        local, sends, recvs = [], [], []
        for a, (_, per_dest, j, prefix, *rest) in enumerate(self.items):
            src = lambda d: in_refs[a].at[d] if per_dest else in_refs[a]
            dst = lambda s: out_refs[j].at[tuple(prefix) + (s,) + tuple(rest[0] if rest else ())]
            local.append(pltpu.make_async_copy(src(me), dst(me), local_sems.at[a]))
            for k, peer in enumerate(peers):
                pid = _flat_id(peer)
                sends.append(_remote(src(pid), dst(me), send_sems.at[a, k], recv_sems.at[a, k], peer))
                recvs.append(_remote(src(pid), dst(pid), send_sems.at[a, k], recv_sems.at[a, k], peer))
        return local, sends, recvs

    def start(self, in_refs, out_refs, sems):
        local, sends, _ = self.copies(in_refs, out_refs, sems)
        for cp in local + sends:
            cp.start()

    def wait(self, in_refs, out_refs, sems):
        local, sends, recvs = self.copies(in_refs, out_refs, sems)
        for cp in recvs:
            cp.wait_recv()
        for cp in sends:
            cp.wait_send()
        for cp in local:
            cp.wait()


def _pcall(core, *, name, grid, in_specs, out_specs, out_shape, args, sem, scratch_shapes=(), aliases=None,
           rider=None):
    n_in, n_out, n_scr = len(in_specs), len(out_specs), len(scratch_shapes)
    n_rin, n_rout = (len(rider.args), rider.n_buf) if rider else (0, 0)

    def body(*refs):
        ins, outs = refs[:n_in], refs[n_in + n_rin:n_in + n_rin + n_out]
        scr = refs[n_in + n_rin + n_out + n_rout:n_in + n_rin + n_out + n_rout + n_scr]
        if rider:
            r_refs = (refs[n_in:n_in + rider.n_src], refs[n_in + n_rin + n_out:n_in + n_rin + n_out + n_rout],
                      refs[n_in + n_rin + n_out + n_rout + n_scr:])
            ids = [pl.program_id(d) for d in range(len(grid))]
            first = functools.reduce(lambda p, q: p & q, [i == 0 for i in ids])
            last = functools.reduce(lambda p, q: p & q, [i == g - 1 for i, g in zip(ids, grid)])
            pl.when(first)(lambda: rider.start(*r_refs))
        core(ins, outs, scr)
        if rider:
            pl.when(last)(lambda: rider.wait(*r_refs))

    aliases = dict(aliases or {})
    if rider:
        sem = ("arbitrary",) * len(grid)
        aliases.update(rider.aliases(n_in, n_out))
    return pl.pallas_call(
        body, name=name, grid=grid, in_specs=list(in_specs) + [_ANY] * n_rin,
        out_specs=list(out_specs) + [_ANY] * n_rout,
        out_shape=list(out_shape) + (rider.out_shape if rider else []),
        scratch_shapes=list(scratch_shapes) + (rider.scratch if rider else []),
        input_output_aliases=aliases, compiler_params=_cparams(sem),
    )(*args, *(rider.args if rider else []))


def _exchange(direct, *, name):
    n_in = len(direct.args)

    def body(*refs):
        r_refs = refs[:direct.n_src], refs[n_in:n_in + direct.n_buf], refs[n_in + direct.n_buf:]
        direct.start(*r_refs)
        direct.wait(*r_refs)

    return pl.pallas_call(
        body, name=name, in_specs=[_ANY] * n_in, out_specs=[_ANY] * direct.n_buf, out_shape=direct.out_shape,
        input_output_aliases=direct.aliases(0, 0), scratch_shapes=direct.scratch,
    )(*direct.args)


def _matmul(a, b, *, form, tm, tn, tk, name, add=None, add_scale=1.0, rider=None, b_cols=None):
    if form == "nn":
        (m, kk), n = a.shape, b.shape[1]
        a_spec = pl.BlockSpec((tm, tk), lambda i, j, k: (i, k))
        b_spec = pl.BlockSpec((tk, tn), lambda i, j, k: (k, j))
        dn = (((1,), (0,)), ((), ()))
    elif form == "nt":
        (m, kk), n = a.shape, b.shape[0]
        a_spec = pl.BlockSpec((tm, tk), lambda i, j, k: (i, k))
        b_spec = pl.BlockSpec((tn, tk), lambda i, j, k: (j, k))
        dn = (((1,), (1,)), ((), ()))
    else:
        kk, m = a.shape
        n0, n = b_cols or (0, b.shape[1])
        assert n0 % tn == 0
        a_spec = pl.BlockSpec((tk, tm), lambda i, j, k: (k, i))
        b_spec = pl.BlockSpec((tk, tn), lambda i, j, k: (k, j + n0 // tn))
        dn = (((0,), (0,)), ((), ()))
    assert m % tm == 0 and n % tn == 0 and kk % tk == 0, (name, m, n, kk)
    has_add = add is not None

    def core(ins, outs, _):
        a_ref, b_ref = ins[:2]
        o_ref = outs[0]
        k = pl.program_id(2)
        p = lax.dot_general(a_ref[...].astype(MM_DTYPE), b_ref[...].astype(MM_DTYPE), dn,
                            preferred_element_type=F32)

        @pl.when(k == 0)
        def _():
            o_ref[...] = p + add_scale * ins[2][...] if has_add else p

        @pl.when(k > 0)
        def _():
            o_ref[...] += p

    in_specs = [a_spec, b_spec]
    args = [a, b]
    if has_add:
        in_specs.append(pl.BlockSpec((tm, tn), lambda i, j, k: (i, j)))
        args.append(add)
    res = _pcall(core, name=name, grid=(m // tm, n // tn, kk // tk), in_specs=in_specs,
                 out_specs=[pl.BlockSpec((tm, tn), lambda i, j, k: (i, j))],
                 out_shape=[jax.ShapeDtypeStruct((m, n), F32)], args=args,
                 sem=("parallel", "parallel", "arbitrary"), rider=rider)
    return res if rider else res[0]


ZERO_TAIL = 8


def _with_tail(x):
    return jnp.concatenate([x, jnp.zeros((ZERO_TAIL,) + x.shape[1:], x.dtype)], axis=0)


def _shift_down(x, k):
    return pltpu.roll(x, k, 0)


def _shift_up(x, k):
    return pltpu.roll(x, x.shape[0] - k, 0)


def _conv_slab(x, w):
    return w[3:4] * x + w[2:3] * _shift_down(x, 1) + w[1:2] * _shift_down(x, 2) + w[0:1] * _shift_down(x, 3)


def _prep_fwd(h, conv_w, *, name):
    t_len = h.shape[0]

    def body(x_ref, w_ref, o_ref):
        s = pl.program_id(0)
        y = _silu(_conv_slab(_with_tail(x_ref[...]), w_ref[...])[:t_len])
        rs = lax.rsqrt(jnp.sum(y * y, axis=-1, keepdims=True) + L2_EPS)
        scale = jnp.where(s < A_HEADS, A_HEAD_DIM ** -0.5, 1.0)
        o_ref[...] = jnp.where(s < 2 * A_HEADS, y * rs * scale, y)

    return pl.pallas_call(
        body, name=name, grid=(12,),
        in_specs=[pl.BlockSpec((t_len, LANE), lambda s: (0, L_QKV // LANE + s)),
                  pl.BlockSpec((8, LANE), lambda s: (0, s))],
        out_specs=pl.BlockSpec((t_len, LANE), lambda s: (0, s)),
        out_shape=jax.ShapeDtypeStruct((t_len, 3 * A_WIDTH), F32),
        compiler_params=_cparams(("parallel",)),
    )(h, conv_w)


def _prep_bwd(h, conv_w, d_out, dh, *, name):
    t_len = h.shape[0]

    def body(x_ref, w_ref, g_ref, dh_in, dx_ref, dw_ref):
        del dh_in
        s = pl.program_id(0)
        x = _with_tail(x_ref[...])
        g = _with_tail(g_ref[0])
        w = w_ref[...]
        xs = [_shift_down(x, 3), _shift_down(x, 2), _shift_down(x, 1), x]
        c = w[0:1] * xs[0] + w[1:2] * xs[1] + w[2:3] * xs[2] + w[3:4] * xs[3]
        sg = jax.nn.sigmoid(c)
        y = c * sg
        rs = lax.rsqrt(jnp.sum(y * y, axis=-1, keepdims=True) + L2_EPS)
        scale = jnp.where(s < A_HEADS, A_HEAD_DIM ** -0.5, 1.0)
        dy_n = scale * (rs * g - y * (rs * rs * rs) * jnp.sum(g * y, axis=-1, keepdims=True))
        dy = jnp.where(s < 2 * A_HEADS, dy_n, g)
        dc = dy * (sg * (1.0 + c * (1.0 - sg)))
        dx = w[3:4] * dc + w[2:3] * _shift_up(dc, 1) + w[1:2] * _shift_up(dc, 2) + w[0:1] * _shift_up(dc, 3)
        dx_ref[...] = dx[:t_len]
        dws = [jnp.sum(dc * xs[j], axis=0, keepdims=True) for j in range(CONV_K)]
        dw_ref[...] = jnp.concatenate(dws + [jnp.zeros((8 - CONV_K, LANE), F32)], axis=0)

    slab = pl.BlockSpec((t_len, LANE), lambda s: (0, L_QKV // LANE + s))
    return pl.pallas_call(
        body, name=name, grid=(12,),
        in_specs=[slab, pl.BlockSpec((8, LANE), lambda s: (0, s)),
                  pl.BlockSpec((1, t_len, LANE), lambda s: (s // A_HEADS, 0, s % A_HEADS)), _ANY],
        out_specs=[slab, pl.BlockSpec((8, LANE), lambda s: (0, s))],
        out_shape=[jax.ShapeDtypeStruct((t_len, L_COLS), F32), jax.ShapeDtypeStruct((8, 3 * A_WIDTH), F32)],
        input_output_aliases={3: 0},
        compiler_params=_cparams(("parallel",)),
    )(h, conv_w, d_out, dh)


N_LEVELS = 5
MF_TRIL, MF_STRIL, MF_DIAG8, MF_LOW16, MF_EYE = 0, 1, 2, 3, 3 + N_LEVELS
MB_CUM, MB_CUM_T, MB_TOT = 0, 1, 2


def _gdn_masks():
    r = lax.broadcasted_iota(jnp.int32, (SUPER, SUPER), 0)
    c = lax.broadcasted_iota(jnp.int32, (SUPER, SUPER), 1)
    same = lambda shift: (r >> shift) == (c >> shift)
    ninf = lambda m: jnp.where(m, 0.0, -jnp.inf).astype(F32)
    one = lambda m: m.astype(F32)
    mf = jnp.stack([ninf(r >= c), ninf(r > c), one(same(3))]
                   + [one(same(4 + lv) & jnp.logical_not(same(3 + lv))) for lv in range(N_LEVELS)] + [one(r == c)])
    mb = jnp.stack([one(r >= c), one(r <= c), jnp.ones((SUPER, SUPER), F32)]).astype(BF16)
    return mf, mb


def _tri_inv_impl(a, mf):
    d = lambda p, q: jnp.dot(p.astype(BF16), q.astype(BF16), preferred_element_type=F32)
    dd = lambda p, q: jnp.dot(p, q, preferred_element_type=F32)
    eye = mf[MF_EYE]
    a0 = a * mf[MF_DIAG8]
    a2 = d(a0, a0)
    a4 = d(a2, a2)
    t = d(d(eye - a0, eye + a2), eye + a4)
    for level in range(N_LEVELS):
        t = t - d(d(t, a * mf[MF_LOW16 + level]), t)
    a_hi, a_lo = _split(a)
    for _ in range(NEWTON_STEPS):
        t0 = t.astype(BF16)
        t0f = t0.astype(F32)
        resid = (eye - t0f) - (dd(a_hi, t0) + dd(a_lo, t0))
        r_hi, r_lo = _split(resid)
        t = t0f + (dd(t0, r_hi) + dd(t0, r_lo))
    return t


@jax.custom_vjp
def _wy_apply(a, rhs, t):
    return _mm(t, rhs)


def _wy_apply_fwd(a, rhs, t):
    x = _mm(t, rhs)
    return x, (t, x)


def _wy_apply_bwd(res, dx):
    t, x = res
    d_rhs = _mm_tn(t, dx)
    return -_mm_nt(d_rhs, x), d_rhs, jnp.zeros_like(t)


_wy_apply.defvjp(_wy_apply_fwd, _wy_apply_bwd)


@functools.partial(jax.custom_vjp, nondiff_argnums=(1,))
def _lane_roll(x, shift):
    return pltpu.roll(x, shift % LANE, 1)


_lane_roll.defvjp(lambda x, shift: (_lane_roll(x, shift), None), lambda shift, _, g: (_lane_roll(g, -shift),))


def _mask_times_lanes(x, mask):
    lane = lax.broadcasted_iota(jnp.int32, (1, LANE), 1)
    x = jnp.where(lane < A_HEADS, x, 0.0)
    x1 = x.astype(BF16).astype(F32)
    x2 = (x - x1).astype(BF16).astype(F32)
    x3 = (x - x1 - x2).astype(BF16).astype(F32)
    pieces = x1 + pltpu.roll(x2, A_HEADS, 1) + pltpu.roll(x3, 2 * A_HEADS, 1)
    res = jnp.dot(mask, pieces.astype(BF16), preferred_element_type=F32)
    return res + pltpu.roll(res, LANE - A_HEADS, 1) + pltpu.roll(res, LANE - 2 * A_HEADS, 1)


@jax.custom_vjp
def _chunk_sums(g, mb):
    return _mask_times_lanes(g, mb[MB_CUM]), _mask_times_lanes(g, mb[MB_TOT])


def _chunk_sums_fwd(g, mb):
    return _chunk_sums(g, mb), mb


def _chunk_sums_bwd(mb, d):
    lane = lax.broadcasted_iota(jnp.int32, (1, LANE), 1)
    dg = _mask_times_lanes(d[0], mb[MB_CUM_T]) + _mask_times_lanes(d[1], mb[MB_TOT])
    return jnp.where(lane < A_HEADS, dg, 0.0), jnp.zeros_like(mb)


_chunk_sums.defvjp(_chunk_sums_fwd, _chunk_sums_bwd)


def _gdn_gates(ba, alog, dtb, mb):
    beta = jax.nn.sigmoid(ba)
    g = -jnp.exp(alog) * _softplus(_lane_roll(ba, -A_HEADS) + dtb)
    gc, gl = _chunk_sums(g, mb)
    return beta, gc, gl, gc.T


def _gdn_block(s, q, k, v, z, gates, nw, h, t_known, mf):
    n = q.shape[0]
    beta_all, gc_all, gl_all, gct_all = gates
    lane = lax.broadcasted_iota(jnp.int32, (1, LANE), 1)
    sub = lax.broadcasted_iota(jnp.int32, (LANE, 1), 0)
    col = lambda x: jnp.sum(jnp.where(lane == h, x, 0.0), axis=1, keepdims=True)
    wide = lambda c: jnp.broadcast_to(c, (n, LANE))
    gc, gl = col(gc_all), col(gl_all)
    gc_row = jnp.sum(jnp.where(sub == h, gct_all, 0.0), axis=0, keepdims=True)
    beta_w, eg_w = wide(col(beta_all)), wide(jnp.exp(gc))
    diff = gc - gc_row
    decay = jnp.exp(diff + mf[MF_TRIL])
    kb = k * beta_w
    a_mat = _mm_nt(kb, k) * jnp.exp(diff + mf[MF_STRIL])
    rhs = jnp.concatenate([v * beta_w, kb * eg_w], axis=1)
    if t_known is None:
        t_mat = _tri_inv_impl(a_mat, mf)
        uw = _mm(t_mat, rhs)
    else:
        t_mat = t_known
        uw = _wy_apply(a_mat, rhs, t_known)
    u, w = uw[:, :LANE], uw[:, LANE:]
    qk = _mm_nt(q, k) * decay
    q_dec = q * eg_w
    k_dec = k * wide(jnp.exp(gl - gc))
    v_new = u - _mm(w, s)
    o = _mm(q_dec, s) + _mm(qk, v_new)
    s = s * jnp.exp(gl[0:1]) + _mm_tn(k_dec, v_new)
    o = o * lax.rsqrt(jnp.mean(o * o, axis=-1, keepdims=True) + RMS_EPS) * nw
    return o * _silu(z), s, t_mat


def _gdn_fwd(qkv, h, alog, dtb, nw, ycat, *, name, rider=None):
    t_len = qkv.shape[0]
    nsc = t_len // SUPER

    def core(ins, outs, scr):
        q_ref, k_ref, v_ref, gate_ref, al_ref, dt_ref, nw_ref, mf_ref, mb_ref, _ = ins
        y_ref, sin_ref, t_ref = outs
        s_scr, = scr

        @pl.when(pl.program_id(0) == 0)
        def _():
            s_scr[...] = jnp.zeros_like(s_scr)

        per_head = lambda ref: jnp.stack([ref[:, hh * LANE:(hh + 1) * LANE] for hh in range(A_HEADS)])
        states = s_scr[...]
        gates = _gdn_gates(gate_ref[:, A_WIDTH:], al_ref[...], dt_ref[...], mb_ref[...])
        fn = jax.vmap(_gdn_block, in_axes=(0, 0, 0, 0, 0, None, None, 0, None, None))
        y, s_new, t_mat = fn(states, per_head(q_ref), per_head(k_ref), per_head(v_ref), per_head(gate_ref),
                             gates, nw_ref[...], jnp.arange(A_HEADS), None, mf_ref[...])
        sin_ref[0] = states
        t_ref[0] = t_mat
        s_scr[...] = s_new
        for hh in range(A_HEADS):
            y_ref[:, hh * LANE:(hh + 1) * LANE] = y[hh]

    blk = lambda j: pl.BlockSpec((SUPER, A_WIDTH), lambda sc: (sc, j))
    row = pl.BlockSpec((1, LANE), lambda sc: (0, 0))
    mf, mb = _gdn_masks()
    whole = lambda a: pl.BlockSpec(a.shape, lambda sc: (0, 0, 0))
    return _pcall(
        core, name=name, grid=(nsc,),
        in_specs=[blk(0), blk(1), blk(2), pl.BlockSpec((SUPER, L_GATE), lambda sc: (sc, L_ZA // L_GATE)),
                  row, row, row, whole(mf), whole(mb), _ANY],
        out_specs=[blk(0),
                   pl.BlockSpec((1, A_HEADS, A_HEAD_DIM, A_HEAD_DIM), lambda sc: (sc, 0, 0, 0)),
                   pl.BlockSpec((1, A_HEADS, SUPER, SUPER), lambda sc: (sc, 0, 0, 0))],
        out_shape=[jax.ShapeDtypeStruct((t_len, D_MODEL), F32),
                   jax.ShapeDtypeStruct((nsc, A_HEADS, A_HEAD_DIM, A_HEAD_DIM), F32),
                   jax.ShapeDtypeStruct((nsc, A_HEADS, SUPER, SUPER), F32)],
        scratch_shapes=[pltpu.VMEM((A_HEADS, A_HEAD_DIM, A_HEAD_DIM), F32)],
        aliases={9: 0}, sem=("arbitrary",), rider=rider,
        args=(qkv, qkv, qkv, h, alog, dtb, nw, mf, mb, ycat))


def _gdn_bwd(qkv, h, alog, dtb, nw, s_in, t_in, dycat, dh, *, name, rider=None):
    t_len = qkv.shape[0]
    nsc = t_len // SUPER

    def core(ins, outs, scr):
        q_ref, k_ref, v_ref, gate_ref, al_ref, dt_ref, nw_ref, sin_ref, t_ref, dy_ref, mf_ref, mb_ref, _ = ins
        dgate_ref, dqkv_ref, dal_ref, ddt_ref, dnw_ref = outs
        ds_scr, = scr

        @pl.when(pl.program_id(0) == 0)
        def _():
            ds_scr[...] = jnp.zeros_like(ds_scr)
            dal_ref[...] = jnp.zeros_like(dal_ref)
            ddt_ref[...] = jnp.zeros_like(ddt_ref)
            dnw_ref[...] = jnp.zeros_like(dnw_ref)

        per_head = lambda ref: jnp.stack([ref[:, hh * LANE:(hh + 1) * LANE] for hh in range(A_HEADS)])
        head_ids = jnp.arange(A_HEADS)
        t_known, mf, mb = t_ref[0], mf_ref[...], mb_ref[...]

        def fn(s, q, k, v, z, ba, alog, dtb, nw):
            gates = _gdn_gates(ba, alog, dtb, mb)
            one = lambda s, q, k, v, z, t, h: _gdn_block(s, q, k, v, z, gates, nw, h, t, mf)[:2]
            return jax.vmap(one)(s, q, k, v, z, t_known, head_ids)

        _, vjp = jax.vjp(fn, sin_ref[0], per_head(q_ref), per_head(k_ref), per_head(v_ref), per_head(gate_ref),
                         gate_ref[:, A_WIDTH:], al_ref[...], dt_ref[...], nw_ref[...])
        ds, dq, dk, dv, dz, dba, dal, ddt, dnw = vjp((per_head(dy_ref), ds_scr[...]))
        ds_scr[...] = ds
        for hh in range(A_HEADS):
            cols = slice(hh * LANE, (hh + 1) * LANE)
            dqkv_ref[0, :, cols] = dq[hh]
            dqkv_ref[1, :, cols] = dk[hh]
            dqkv_ref[2, :, cols] = dv[hh]
            dgate_ref[:, cols] = dz[hh]
        dgate_ref[:, A_WIDTH:] = dba
        dal_ref[...] += dal
        ddt_ref[...] += ddt
        dnw_ref[...] += dnw

    rev = lambda i: nsc - 1 - i
    blk = lambda j: pl.BlockSpec((SUPER, A_WIDTH), lambda i: (rev(i), j))
    gate = pl.BlockSpec((SUPER, L_GATE), lambda i: (rev(i), L_ZA // L_GATE))
    row = pl.BlockSpec((1, LANE), lambda i: (0, 0))
    mf, mb = _gdn_masks()
    whole = lambda a: pl.BlockSpec(a.shape, lambda i: (0, 0, 0))
    return _pcall(
        core, name=name, grid=(nsc,),
        in_specs=[blk(0), blk(1), blk(2), gate, row, row, row,
                  pl.BlockSpec((1, A_HEADS, A_HEAD_DIM, A_HEAD_DIM), lambda i: (rev(i), 0, 0, 0)),
                  pl.BlockSpec((1, A_HEADS, SUPER, SUPER), lambda i: (rev(i), 0, 0, 0)),
                  blk(0), whole(mf), whole(mb), _ANY],
        out_specs=[gate, pl.BlockSpec((3, SUPER, A_WIDTH), lambda i: (0, rev(i), 0)), row, row, row],
        out_shape=[jax.ShapeDtypeStruct((t_len, L_COLS), F32), jax.ShapeDtypeStruct((3, t_len, A_WIDTH), F32)]
        + [jax.ShapeDtypeStruct((1, LANE), F32)] * 3,
        scratch_shapes=[pltpu.VMEM((A_HEADS, A_HEAD_DIM, A_HEAD_DIM), F32)],
        aliases={12: 0}, sem=("arbitrary",), rider=rider,
        args=(qkv, qkv, qkv, h, alog, dtb, nw, s_in, t_in, dycat, mf, mb, dh))


Q_BLOCKS = 4
Q_ROWS = Q_BLOCKS * BLOCK


def _swa_block(q, kp, kc, vp, vc, z, sinks, first):
    rows = B_GROUP * BLOCK
    ri = lax.broadcasted_iota(jnp.int32, (rows, 2 * BLOCK), 0)
    si = lax.broadcasted_iota(jnp.int32, (rows, 2 * BLOCK), 1)
    dist = (ri & (BLOCK - 1)) + BLOCK - si
    bias = jnp.where((dist >= 0) & (dist < WINDOW), 0.0, -jnp.inf)
    no_prev = jnp.where(first & (si[:1] < BLOCK), -jnp.inf, 0.0)
    dist_f = dist.astype(F32)
    head_of_row = lax.broadcasted_iota(jnp.int32, (rows, 1), 0) >> 7
    keys = jnp.concatenate([kp, kc], axis=0)
    vals = jnp.concatenate([vp, vc], axis=0)

    def item(b, j):
        cs = slice(j * B_HEAD_DIM, (j + 1) * B_HEAD_DIM)
        rs = slice(b * BLOCK, (b + 1) * BLOCK)
        heads = range(j * B_GROUP, (j + 1) * B_GROUP)
        qs = jnp.concatenate([q[rs, hq * B_HEAD_DIM:(hq + 1) * B_HEAD_DIM] for hq in heads], axis=0) * (
            B_HEAD_DIM ** -0.5)
        kk = keys[b * BLOCK:(b + 2) * BLOCK, cs]
        vv = vals[b * BLOCK:(b + 2) * BLOCK, cs]
        sink = jnp.concatenate([jnp.broadcast_to(sinks[:, hq:hq + 1], (BLOCK, 1)) for hq in heads], axis=0)
        slope = sum(jnp.where(head_of_row == gi, 2.0 ** (-8.0 * (hq + 1) / B_Q_HEADS), 0.0)
                    for gi, hq in enumerate(heads))
        return qs, kk, vv, sink, slope, (no_prev if b == 0 else jnp.zeros_like(no_prev))

    def attend(qs, kk, vv, sink, slope, hide):
        sc = _mm_nt(qs, kk) - slope * dist_f + (bias + hide)
        m = lax.stop_gradient(jnp.maximum(jnp.max(sc, axis=-1, keepdims=True), sink))
        p = jnp.exp(sc - m)
        inv = 1.0 / (jnp.sum(p, axis=-1, keepdims=True) + jnp.exp(sink - m))
        return _mm(p * inv, vv)

    items = [(b, j) for b in range(Q_BLOCKS) for j in range(B_KV_HEADS)]
    o = jax.vmap(attend)(*[_stack(t) for t in zip(*[item(b, j) for b, j in items])])
    rows_out = [jnp.concatenate([o[b * B_KV_HEADS + j, gi * BLOCK:(gi + 1) * BLOCK]
                                 for j in range(B_KV_HEADS) for gi in range(B_GROUP)], axis=1)
                for b in range(Q_BLOCKS)]
    return jnp.concatenate(rows_out, axis=0) * _silu(z)


def _swa_specs(idx):
    wide = lambda off: pl.BlockSpec((Q_ROWS, B_WIDTH), lambda n: (idx(n), off))
    cur = lambda off: pl.BlockSpec((Q_ROWS, LANE), lambda n: (idx(n), off))
    prev = lambda off: pl.BlockSpec((BLOCK, LANE), lambda n: (jnp.maximum(idx(n) * Q_BLOCKS - 1, 0), off))
    return [wide(L_QB // B_WIDTH), prev(L_KB // LANE), cur(L_KB // LANE), prev(L_VB // LANE), cur(L_VB // LANE),
            wide(L_ZB // B_WIDTH), pl.BlockSpec((1, LANE), lambda n: (0, 0))]


def _swa_fwd(h, sinks, *, name, rider=None):
    t_len = h.shape[0]
    nb = t_len // Q_ROWS

    def core(ins, outs, _):
        q_ref, kp_ref, kc_ref, vp_ref, vc_ref, z_ref, s_ref = ins
        outs[0][...] = _swa_block(q_ref[...], kp_ref[...], kc_ref[...], vp_ref[...], vc_ref[...], z_ref[...],
                                  s_ref[...], pl.program_id(0) == 0)

    res = _pcall(core, name=name, grid=(nb,), in_specs=_swa_specs(lambda n: n),
                 out_specs=[pl.BlockSpec((Q_ROWS, B_WIDTH), lambda n: (n, 1))],
                 out_shape=[jax.ShapeDtypeStruct((t_len, D_MODEL), F32)], sem=("parallel",), rider=rider,
                 args=(h, h, h, h, h, h, sinks))
    return res if rider else res[0]


def _swa_bwd(h, sinks, dycat, *, name, rider=None):
    t_len = h.shape[0]
    nb = t_len // Q_ROWS
    last = slice(Q_ROWS - BLOCK, Q_ROWS)

    def core(ins, outs, scr):
        q_ref, kp_ref, kc_ref, vp_ref, vc_ref, z_ref, s_ref, dy_ref = ins
        dh_ref, dsk_ref = outs
        ck_scr, cv_scr = scr
        i = pl.program_id(0)
        n = nb - 1 - i

        @pl.when(i == 0)
        def _():
            ck_scr[...] = jnp.zeros_like(ck_scr)
            cv_scr[...] = jnp.zeros_like(cv_scr)
            dsk_ref[...] = jnp.zeros_like(dsk_ref)

        fn = functools.partial(_swa_block, first=(n == 0))
        _, vjp = jax.vjp(fn, q_ref[...], kp_ref[...], kc_ref[...], vp_ref[...], vc_ref[...], z_ref[...], s_ref[...])
        dq, dkp, dkc, dvp, dvc, dz, dsk = vjp(dy_ref[...])
        dh_ref[:, L_QB:L_QB + B_WIDTH] = dq
        dh_ref[:, L_ZB:L_ZB + B_WIDTH] = dz
        dh_ref[:, L_KB:L_KB + LANE] = dkc
        dh_ref[:, L_VB:L_VB + LANE] = dvc
        dh_ref[last, L_KB:L_KB + LANE] += ck_scr[...]
        dh_ref[last, L_VB:L_VB + LANE] += cv_scr[...]
        ck_scr[...] = dkp
        cv_scr[...] = dvp
        dsk_ref[...] += dsk

    rev = lambda i: nb - 1 - i
    return _pcall(
        core, name=name, grid=(nb,),
        in_specs=_swa_specs(rev) + [pl.BlockSpec((Q_ROWS, B_WIDTH), lambda i: (rev(i), 1))],
        out_specs=[pl.BlockSpec((Q_ROWS, L_SWA), lambda i: (rev(i), 0)), pl.BlockSpec((1, LANE), lambda i: (0, 0))],
        out_shape=[jax.ShapeDtypeStruct((t_len, L_COLS), F32), jax.ShapeDtypeStruct((1, LANE), F32)],
        scratch_shapes=[pltpu.VMEM((BLOCK, LANE), F32), pltpu.VMEM((BLOCK, LANE), F32)],
        sem=("arbitrary",), rider=rider, args=(h, h, h, h, h, h, sinks, dycat))


def _out_ln_fwd(ycat, w_out, x, ln_g, ln_b, *, name, tm=512, last=False):
    t_len = x.shape[0]

    def body(y_ref, w_ref, x_ref, g_ref, b_ref, r_ref, *o_ref):
        r = DEEPNORM_ALPHA * x_ref[...] + _mm(y_ref[...], w_ref[...])
        r_ref[...] = r
        if not last:
            mu = jnp.mean(r, axis=-1, keepdims=True)
            d = r - mu
            var = jnp.mean(d * d, axis=-1, keepdims=True)
            o_ref[0][...] = d * lax.rsqrt(var + LN_EPS) * g_ref[...] + b_ref[...]

    tile = pl.BlockSpec((tm, D_MODEL), lambda i: (i, 0))
    vec = pl.BlockSpec((1, D_MODEL), lambda i: (0, 0))
    n_out = 1 if last else 2
    res = pl.pallas_call(
        body, name=name, grid=(t_len // tm,),
        in_specs=[tile, pl.BlockSpec((D_MODEL, D_MODEL), lambda i: (0, 0)), tile, vec, vec],
        out_specs=[tile] * n_out,
        out_shape=[jax.ShapeDtypeStruct((t_len, D_MODEL), F32)] * n_out,
        compiler_params=_cparams(("parallel",)),
    )(ycat, w_out, x, ln_g, ln_b)
    return (res[0], None) if last else res


def _ln_bwd(dxn, r, ln_g, *, name, tm=512, loss=None):
    t_len = r.shape[0]

    def body(*refs):
        if loss:
            t_ref, r_ref, g_ref, b_ref, dr_ref, dg_ref, db_ref, l_ref = refs
        else:
            dx_ref, r_ref, g_ref, dr_ref, dg_ref, db_ref = refs

        @pl.when(pl.program_id(0) == 0)
        def _():
            dg_ref[...] = jnp.zeros_like(dg_ref)
            db_ref[...] = jnp.zeros_like(db_ref)
            if loss:
                l_ref[...] = jnp.zeros_like(l_ref)

        rr = r_ref[...]
        mu = jnp.mean(rr, axis=-1, keepdims=True)
        d = rr - mu
        rstd = lax.rsqrt(jnp.mean(d * d, axis=-1, keepdims=True) + LN_EPS)
        xh = d * rstd
        if loss:
            e = (xh * g_ref[...] + b_ref[...]) - t_ref[...]
            dx = e * (1.0 / D_MODEL)
            l_ref[...] += jnp.sum(e * e, axis=0, keepdims=True)
        else:
            dx = dx_ref[...]
        dxh = dx * g_ref[...]
        dr_ref[...] = rstd * (dxh - jnp.mean(dxh, axis=-1, keepdims=True)
                              - xh * jnp.mean(dxh * xh, axis=-1, keepdims=True))
        dg_ref[...] += jnp.sum(dx * xh, axis=0, keepdims=True)
        db_ref[...] += jnp.sum(dx, axis=0, keepdims=True)

    tile = pl.BlockSpec((tm, D_MODEL), lambda i: (i, 0))
    vec = pl.BlockSpec((1, D_MODEL), lambda i: (0, 0))
    vec_shape = jax.ShapeDtypeStruct((1, D_MODEL), F32)
    args = (loss[0], r, ln_g, loss[1]) if loss else (dxn, r, ln_g)
    return pl.pallas_call(
        body, name=name, grid=(t_len // tm,),
        in_specs=[tile, tile, vec] + ([vec] if loss else []), out_specs=[tile, vec, vec] + ([vec] if loss else []),
        out_shape=[jax.ShapeDtypeStruct((t_len, D_MODEL), F32), vec_shape, vec_shape] + ([vec_shape] if loss else []),
        compiler_params=_cparams(("arbitrary",)),
    )(*args)


def _pad_row(v):
    return jnp.zeros((1, LANE), F32).at[0, :v.shape[0]].set(v)


_REGIONS = ((0, 1536, L_QKV), (1536, 2048, L_ZA), (2048, 2056, L_BA), (2056, 2568, L_QB), (2568, 2696, L_KB),
            (2696, 2824, L_VB), (2824, 3336, L_ZB))


def _shard_pieces(regions):
    for a, b, off in regions:
        for d in range(N_DEV):
            lo, hi = max(a, d * SHARD_COLS), min(b, (d + 1) * SHARD_COLS)
            if lo < hi:
                yield d, lo - d * SHARD_COLS, hi - d * SHARD_COLS, off + lo - a


def _as_list(r):
    return list(r) if isinstance(r, (list, tuple)) else [r]


def _gathered(shard):
    return jax.ShapeDtypeStruct((N_DEV,) + shard.shape, shard.dtype)


def _full_w_in(g_in, name):
    by_offset = sorted(_shard_pieces(_REGIONS), key=lambda p: p[3])
    tc = 256

    def body(g_ref, o_ref):
        pieces, row = [], 0
        for d, lo, hi, off in by_offset + [(None, 0, 0, L_COLS)]:
            if off > row:
                pieces.append(jnp.zeros((off - row, tc), g_ref.dtype))
            if d is not None:
                pieces.append(g_ref[d, lo:hi, :])
            row = off + hi - lo
        o_ref[...] = jnp.concatenate(pieces, axis=0)

    return pl.pallas_call(
        body, name=name, grid=(D_MODEL // tc,),
        in_specs=[pl.BlockSpec((N_DEV, SHARD_COLS, tc), lambda i: (0, 0, i))],
        out_specs=pl.BlockSpec((L_COLS, tc), lambda i: (0, i)),
        out_shape=jax.ShapeDtypeStruct((L_COLS, D_MODEL), g_in.dtype),
        compiler_params=_cparams(("parallel",)),
    )(g_in)


def _full_conv(g_conv):
    return jnp.pad(g_conv.transpose(1, 0, 2).reshape(CONV_K, 3 * A_WIDTH), ((0, 8 - CONV_K), (0, 0)))


def _forward(x, weights, shards, small):
    a_log, dt_bias, norm_w, sinks, ln_g, ln_b = small
    tm = min(512, x.shape[0])
    saved, weights = [], [list(w) for w in weights]
    whole = lambda arrs: _Direct([(a, False, j, ()) for j, a in enumerate(arrs)], [_gathered(a) for a in arrs])
    for l in range(DEPTH):
        rider = whole(shards[l][1:]) if weights[l][1] is None else None
        h, *got = _as_list(_matmul(x, weights[l][0], form="nt", tm=tm, tn=L_COLS, tk=D_MODEL, name=f"in_proj_{l}",
                                   rider=rider))
        if rider:
            weights[l][1:] = [got[0].reshape(D_MODEL, D_MODEL), _full_conv(got[1])]
        w_in_l, w_out_l, conv_l = weights[l]
        qkv = _prep_fwd(h, conv_l, name=f"prep_fwd_{l}")
        al, dt, nw, sk = _pad_row(a_log[l]), _pad_row(dt_bias[l]), norm_w[l][None, :], _pad_row(sinks[l])
        ahead = l + 1 < DEPTH and weights[l + 1][0] is None
        rider = whole(shards[l + 1][1:]) if ahead else None
        ycat, *got = _as_list(_swa_fwd(h, sk, name=f"swa_fwd_{l}", rider=rider))
        if ahead:
            weights[l + 1][1:] = [got[0].reshape(D_MODEL, D_MODEL), _full_conv(got[1])]
        rider = whole(shards[l + 1][:1]) if ahead else None
        ycat, s_in, t_in, *got = _gdn_fwd(qkv, h, al, dt, nw, ycat, name=f"gdn_fwd_{l}", rider=rider)
        if ahead:
            weights[l + 1][0] = _full_w_in(got[0], f"w_in_rows_{l + 1}")
        r, xn = _out_ln_fwd(ycat, w_out_l, x, ln_g[l][None, :], ln_b[l][None, :], name=f"out_ln_{l}",
                            last=(l == DEPTH - 1))
        saved.append((x, h, qkv, s_in, t_in, ycat, r, al, dt, nw, sk))
        x = xn
    return x, saved, weights


def _w_in_blocks(g, name):
    cols, tc = g.shape[1], 256
    pieces = list(_shard_pieces(_REGIONS))

    def body(g_ref, o_ref):
        blocks = [[] for _ in range(N_DEV)]
        for d, lo, hi, off in pieces:
            blocks[d].append(g_ref[off:off + hi - lo, :])
        for d in range(N_DEV):
            o_ref[d] = jnp.concatenate(blocks[d], axis=0).astype(BF16)

    return pl.pallas_call(
        body, name=name, grid=(cols // tc,),
        in_specs=[pl.BlockSpec((L_COLS, tc), lambda i: (0, i))],
        out_specs=pl.BlockSpec((N_DEV, SHARD_COLS, tc), lambda i: (0, 0, i)),
        out_shape=jax.ShapeDtypeStruct((N_DEV, SHARD_COLS, cols), BF16),
        compiler_params=_cparams(("parallel",)),
    )(g)


def _small_blocks(g):
    c_conv = g["conv_w"].reshape(CONV_K, N_DEV, CONV_SHARD_COLS).transpose(1, 0, 2)
    c_small = [jnp.broadcast_to(g[n][None], (N_DEV,) + g[n].shape) for n, _ in SMALL_SIZES]
    return _pack_small(c_conv, c_small)


def _contributions(g):
    c_out = g["w_out"].astype(BF16).reshape(N_DEV, OUT_SHARD_ROWS, D_MODEL)
    return _w_in_blocks(g["w_in_rows"], name="w_in_grad_blocks_above"), c_out, _small_blocks(g)


def _backward_layer(l, dx, saved_l, weights_l, ln_g_l, above=None, loss=None):
    x_in, h, qkv, s_in, t_in, ycat, r, al, dt, nw, sk = saved_l
    w_in_l, w_out_l, conv_l = weights_l
    tm = min(512, x_in.shape[0])
    dr, d_lng, d_lnb, *loss_lanes = _ln_bwd(dx, r, ln_g_l[None, :], name=f"ln_bwd_{l}", loss=loss)
    big = min(1024, x_in.shape[0])
    dycat = _matmul(dr, w_out_l, form="nt", tm=big, tn=D_MODEL, tk=D_MODEL, name=f"out_proj_dx_{l}")
    d_wout = _matmul(ycat, dr, form="tn", tm=D_MODEL, tn=D_MODEL, tk=big, name=f"out_proj_dw_{l}")
    rider, p_in, p_out, p_small = None, None, None, None
    recv = lambda c: jax.ShapeDtypeStruct((DEPTH,) + c.shape, c.dtype)
    if above:
        c_out = d_wout.astype(BF16).reshape(N_DEV, OUT_SHARD_ROWS, D_MODEL)
        rider = _Direct([(above[1], True, 0, (l + 1,)), (above[2], True, 1, (l + 1,)), (c_out, True, 0, (l,))],
                        [recv(above[1]), recv(above[2])])
    dh, d_sk, *got = _swa_bwd(h, sk, dycat, name=f"swa_bwd_{l}", rider=rider)
    if above:
        p_out, p_small = got
        rider = _Direct([(above[0], True, 0, (l + 1,))], [recv(above[0])])
    dh, dqkv_n, d_al, d_dt, d_nw, *got = _gdn_bwd(qkv, h, al, dt, nw, s_in, t_in, dycat, dh,
                                                  name=f"gdn_bwd_{l}", rider=rider)
    dh, d_conv = _prep_bwd(h, conv_l, dqkv_n, dh, name=f"prep_bwd_{l}")
    grads = dict(w_out=d_wout, conv_w=d_conv[:CONV_K], a_log=d_al[0, :A_HEADS], dt_bias=d_dt[0, :A_HEADS],
                 norm_w=d_nw[0], sinks=d_sk[0, :B_Q_HEADS], ln_g=d_lng[0], ln_b=d_lnb[0])
    dw = functools.partial(_matmul, dh, x_in, form="tn", tm=L_COLS, tk=tm)
    if not above:
        grads["w_in_rows"] = dw(name=f"in_proj_dw_{l}", tn=D_MODEL // 2)
    else:
        p_in, = got
        cut = D_MODEL // 4
        rest = D_MODEL - cut
        first = dw(name=f"in_proj_dw_first_{l}", tn=cut, b_cols=(0, cut))
        blocks = _w_in_blocks(first, name=f"w_in_grad_blocks_first_{l}")
        rider = _Direct([(blocks, True, 0, (l,), (slice(None), pl.ds(0, cut)))], [p_in])
        second, p_in = dw(name=f"in_proj_dw_second_{l}", tn=cut, b_cols=(cut, rest), rider=rider)
        blocks = _w_in_blocks(second, name=f"w_in_grad_blocks_second_{l}")
        rider = _Direct([(blocks, True, 0, (l,), (slice(None), pl.ds(cut, rest))),
                         (_small_blocks(grads), True, 1, (l,))], [p_in, p_small])
    dx, *got = _as_list(_matmul(dh, w_in_l, form="nn", tm=tm, tn=D_MODEL, tk=L_COLS, name=f"in_proj_dx_{l}",
                                add=dr, add_scale=DEEPNORM_ALPHA, rider=rider))
    bufs = (got[0], p_out, got[1]) if above else None
    return dx, grads, bufs, (loss_lanes[0] if loss else None)


def _all_gather(shards, *, name):
    n_arr = len(shards)

    def body(*refs):
        x_refs, out_refs = refs[:n_arr], refs[n_arr:2 * n_arr]
        send_sems, recv_sems, local_sems = refs[2 * n_arr:]
        x, y, c = _me()
        me, sibling = (x, y, c), (x, y, 1 - c)
        chips = [(1 - x, y), (x, 1 - y), (1 - x, 1 - y)]

        def copy(a, k, block, to, src=None):
            dst = out_refs[a].at[_flat_id(block)]
            return _remote(dst if src is None else src, dst, send_sems.at[a, k], recv_sems.at[a, k], to)

        mine = [pltpu.make_async_copy(x_refs[a], out_refs[a].at[_flat_id(me)], local_sems.at[a])
                for a in range(n_arr)]
        for cp in mine:
            cp.start()
        first = []
        for a in range(n_arr):
            first.append(copy(a, 0, me, sibling, src=x_refs[a]))
            first += [copy(a, 1 + j, me, (*chip, c), src=x_refs[a]) for j, chip in enumerate(chips)]
        for cp in first:
            cp.start()
        passed = []
        for j, chip in enumerate(chips):
            for a in range(n_arr):
                copy(a, 1 + j, (*chip, c), me).wait_recv()
                fwd = copy(a, 4 + j, (*chip, c), sibling)
                fwd.start()
                passed.append(fwd)
        for a in range(n_arr):
            copy(a, 0, sibling, me).wait_recv()
            for j, chip in enumerate(chips):
                copy(a, 4 + j, (*chip, 1 - c), me).wait_recv()
        for cp in first + passed:
            cp.wait_send()
        for cp in mine:
            cp.wait()

    return pl.pallas_call(
        body, name=name, in_specs=[_ANY] * n_arr, out_specs=[_ANY] * n_arr,
        out_shape=[jax.ShapeDtypeStruct((N_DEV,) + s.shape, s.dtype) for s in shards],
        scratch_shapes=[pltpu.SemaphoreType.DMA((n_arr, N_DEV - 1)), pltpu.SemaphoreType.DMA((n_arr, N_DEV - 1)),
                        pltpu.SemaphoreType.DMA((n_arr,))],
    )(*shards)


def _adamw(parts, w, m, v, *, tr, name):
    depth, rows, cols = w.shape
    c1 = 1.0 - ADAM_B1 ** ADAM_STEP
    c2 = 1.0 - ADAM_B2 ** ADAM_STEP

    def body(g_ref, w_ref, m_ref, v_ref, go_ref, d_ref, mo_ref, vo_ref):
        g = g_ref[0, 0].astype(F32)
        for s in range(1, N_DEV):
            g = g + g_ref[0, s].astype(F32)
        m_new = ADAM_B1 * m_ref[0] + (1.0 - ADAM_B1) * g
        v_new = ADAM_B2 * v_ref[0] + (1.0 - ADAM_B2) * (g * g)
        go_ref[0] = g
        mo_ref[0] = m_new
        vo_ref[0] = v_new
        d_ref[0] = -ADAM_LR * ((m_new / c1) / (jnp.sqrt(v_new / c2) + ADAM_EPS) + ADAM_WD * w_ref[0])

    tile = pl.BlockSpec((1, tr, cols), lambda l, i: (l, i, 0))
    return pl.pallas_call(
        body, name=name, grid=(depth, rows // tr),
        in_specs=[pl.BlockSpec((1, N_DEV, tr, cols), lambda l, i: (l, 0, i, 0)), tile, tile, tile],
        out_specs=[tile] * 4, out_shape=[jax.ShapeDtypeStruct(w.shape, F32)] * 4,
        compiler_params=_cparams(("parallel", "parallel")),
    )(parts, w, m, v)


def _adamw_w_in(parts, w, m, v, *, name):
    c1 = 1.0 - ADAM_B1 ** ADAM_STEP
    c2 = 1.0 - ADAM_B2 ** ADAM_STEP

    def body(g_ref, w_ref, m_ref, v_ref, go_ref, d_ref, mo_ref, vo_ref):
        for l in range(DEPTH):
            g = g_ref[l, 0].astype(F32)
            for s in range(1, N_DEV):
                g = g + g_ref[l, s].astype(F32)
            m_new = ADAM_B1 * m_ref[:, l, :] + (1.0 - ADAM_B1) * g
            v_new = ADAM_B2 * v_ref[:, l, :] + (1.0 - ADAM_B2) * (g * g)
            go_ref[:, l, :] = g
            mo_ref[:, l, :] = m_new
            vo_ref[:, l, :] = v_new
            d_ref[:, l, :] = -ADAM_LR * ((m_new / c1) / (jnp.sqrt(v_new / c2) + ADAM_EPS) + ADAM_WD * w_ref[:, l, :])

    tile = pl.BlockSpec((SHARD_COLS, DEPTH, LANE), lambda i: (0, 0, i))
    return pl.pallas_call(
        body, name=name, grid=(D_MODEL // LANE,),
        in_specs=[pl.BlockSpec((DEPTH, N_DEV, SHARD_COLS, LANE), lambda i: (0, 0, 0, i)), tile, tile, tile],
        out_specs=[tile] * 4, out_shape=[jax.ShapeDtypeStruct(w.shape, F32)] * 4,
        compiler_params=_cparams(("parallel",)),
    )(parts, w, m, v)


def _pack_small(conv, small):
    lead = conv.shape[:-2]
    flat = jnp.concatenate([conv.reshape(lead + (CS_CONV,))] + list(small), axis=-1)
    pad = CS_ROWS * LANE - flat.shape[-1]
    flat = jnp.concatenate([flat, jnp.zeros(lead + (pad,), F32)], axis=-1)
    return flat.reshape(lead + (CS_ROWS, LANE))


def _unpack_small(p):
    flat = p.reshape(DEPTH, CS_ROWS * LANE)
    conv = flat[:, :CS_CONV].reshape(DEPTH, CONV_K, CONV_SHARD_COLS)
    small, off = [], CS_CONV
    for _, n in SMALL_SIZES:
        small.append(flat[:, off:off + n])
        off += n
    return conv, small


def kernel(x, w_in, conv_w, a_log, dt_bias, norm_w, sinks, w_out, ln_g, ln_b, loss_target, m_w_in, m_conv_w, m_a_log, m_dt_bias, m_norm_w, m_sinks, m_w_out, m_ln_g, m_ln_b, v_w_in, v_conv_w, v_a_log, v_dt_bias, v_norm_w, v_sinks, v_w_out, v_ln_g, v_ln_b):
    small = [a_log, dt_bias, norm_w, sinks, ln_g, ln_b]
    w_t, m_t, v_t = (a.transpose(2, 0, 1) for a in (w_in, m_w_in, v_w_in))
    shards = [[w_t[:, l].astype(BF16), w_out[l].astype(BF16), conv_w[l]] for l in range(DEPTH)]
    g_in0, = _all_gather(shards[0][:1], name="weights_all_gather_0")
    weights = [[_full_w_in(g_in0, "w_in_rows_0"), None, None]] + [[None, None, None]] * (DEPTH - 1)

    _, saved, weights = _forward(x[0], weights, shards, small)
    dx, g1, _, loss_lanes = _backward_layer(1, None, saved[1], weights[1], ln_g[1],
                                            loss=(loss_target[0], ln_b[1][None, :]))
    loss = lax.psum(0.5 * jnp.sum(loss_lanes) * (1.0 / D_MODEL), ("x", "y", "c"))
    dx, _, (p_in, p_out, p_small), _ = _backward_layer(0, dx, saved[0], weights[0], ln_g[0],
                                                       above=_contributions(g1))

    o_in = [o.transpose(1, 2, 0) for o in _adamw_w_in(p_in, w_t, m_t, v_t, name="adamw_w_in")]
    o_out = _adamw(p_out, w_out, m_w_out, v_w_out, tr=OUT_SHARD_ROWS, name="adamw_w_out")
    o_small = _adamw(p_small, _pack_small(conv_w, small),
                     _pack_small(m_conv_w, [m_a_log, m_dt_bias, m_norm_w, m_sinks, m_ln_g, m_ln_b]),
                     _pack_small(v_conv_w, [v_a_log, v_dt_bias, v_norm_w, v_sinks, v_ln_g, v_ln_b]),
                     tr=CS_ROWS, name="adamw_small")
    outs = []
    for k in range(4):
        cv, sm = _unpack_small(o_small[k])
        outs += [o_in[k], cv, sm[0], sm[1], sm[2], sm[3], o_out[k], sm[4], sm[5]]
    return (loss, dx[None], *outs)
```

```python
import functools

import jax
import jax.numpy as jnp
from jax import lax
from jax.experimental import pallas as pl
from jax.experimental.pallas import tpu as pltpu

F32 = jnp.float32
BF16 = jnp.bfloat16
MM_DTYPE = BF16

N_DEV = 8
D_MODEL = 1024
DEPTH = 2
A_HEADS = 4
A_HEAD_DIM = 128
A_WIDTH = 512
CONV_K = 4
SUPER = 256
NEWTON_STEPS = 1
B_Q_HEADS = 8
B_KV_HEADS = 2
B_HEAD_DIM = 64
B_GROUP = 4
B_WIDTH = 512
WINDOW = 128
BLOCK = 128
IN_COLS = 3336
SHARD_COLS = IN_COLS // N_DEV
OUT_SHARD_ROWS = D_MODEL // N_DEV
CONV_SHARD_COLS = 3 * A_WIDTH // N_DEV
DEEPNORM_ALPHA = (2 * DEPTH) ** 0.25
LN_EPS = 1e-5
RMS_EPS = 1e-6
L2_EPS = 1e-6
ADAM_LR, ADAM_B1, ADAM_B2, ADAM_EPS, ADAM_WD, ADAM_STEP = 0.001, 0.9, 0.999, 1e-08, 0.01, 10

LANE = 128
L_QB, L_ZB, L_KB, L_VB, L_ZA, L_BA, L_QKV = 0, 512, 1024, 1152, 1280, 1792, 1920
L_SWA = 1280
L_GATE = 640
L_COLS = 3456
SMALL_SIZES = (("a_log", 4), ("dt_bias", 4), ("norm_w", 128), ("sinks", 8), ("ln_g", 1024), ("ln_b", 1024))
CS_CONV = CONV_K * CONV_SHARD_COLS
CS_ROWS = 24
VMEM_LIMIT = 48 * 1024 * 1024


def _cparams(sem=None):
    return pltpu.CompilerParams(dimension_semantics=sem, vmem_limit_bytes=VMEM_LIMIT)


def _mm(a, b):
    return jnp.dot(a.astype(MM_DTYPE), b.astype(MM_DTYPE), preferred_element_type=F32)


def _mm_nt(a, b):
    return lax.dot_general(a.astype(MM_DTYPE), b.astype(MM_DTYPE), (((1,), (1,)), ((), ())),
                           preferred_element_type=F32)


def _mm_tn(a, b):
    return lax.dot_general(a.astype(MM_DTYPE), b.astype(MM_DTYPE), (((0,), (0,)), ((), ())),
                           preferred_element_type=F32)


def _split(a):
    hi = a.astype(BF16)
    return hi, (a - hi.astype(F32)).astype(BF16)


def _silu(x):
    return x * jax.nn.sigmoid(x)


@jax.custom_vjp
def _stack(parts):
    return jnp.stack(parts)


_stack.defvjp(lambda parts: (jnp.stack(parts), None), lambda _, g: (tuple(g[i] for i in range(g.shape[0])),))


def _softplus(x):
    return jnp.maximum(x, 0.0) + jnp.log1p(jnp.exp(-jnp.abs(x)))


_ANY = pl.BlockSpec(memory_space=pl.ANY)


def _me():
    return lax.axis_index("x"), lax.axis_index("y"), lax.axis_index("c")


def _flat_id(pos):
    return 4 * pos[0] + 2 * pos[1] + pos[2]


def _remote(src, dst, send_sem, recv_sem, to):
    return pltpu.make_async_remote_copy(src_ref=src, dst_ref=dst, send_sem=send_sem, recv_sem=recv_sem,
                                        device_id=to, device_id_type=pl.DeviceIdType.MESH)


class _Direct:
    def __init__(self, items, bufs):
        self.items, self.bufs = list(items), list(bufs)
        self.n_src, self.n_buf = len(self.items), len(self.bufs)
        self.old = [j for j, b in enumerate(self.bufs) if not isinstance(b, jax.ShapeDtypeStruct)]
        self.args = [it[0] for it in self.items] + [self.bufs[j] for j in self.old]
        self.out_shape = [jax.ShapeDtypeStruct(b.shape, b.dtype) for b in self.bufs]
        self.scratch = [pltpu.SemaphoreType.DMA((self.n_src, N_DEV - 1)),
                        pltpu.SemaphoreType.DMA((self.n_src, N_DEV - 1)), pltpu.SemaphoreType.DMA((self.n_src,))]

    def aliases(self, in_base, out_base):
        return {in_base + self.n_src + pos: out_base + j for pos, j in enumerate(self.old)}

    def copies(self, in_refs, out_refs, sems):
        send_sems, recv_sems, local_sems = sems
        x, y, c = _me()
        me = _flat_id((x, y, c))
        peers = [(x ^ ((rel >> 2) & 1), y ^ ((rel >> 1) & 1), c ^ (rel & 1)) for rel in range(1, N_DEV)]
        local, sends, recvs = [], [], []
        for a, (_, per_dest, j, prefix, *rest) in enumerate(self.items):
            src = lambda d: in_refs[a].at[d] if per_dest else in_refs[a]
            dst = lambda s: out_refs[j].at[tuple(prefix) + (s,) + tuple(rest[0] if rest else ())]
            local.append(pltpu.make_async_copy(src(me), dst(me), local_sems.at[a]))
            for k, peer in enumerate(peers):
                pid = _flat_id(peer)
                sends.append(_remote(src(pid), dst(me), send_sems.at[a, k], recv_sems.at[a, k], peer))
                recvs.append(_remote(src(pid), dst(pid), send_sems.at[a, k], recv_sems.at[a, k], peer))
        return local, sends, recvs

    def start(self, in_refs, out_refs, sems):
        local, sends, _ = self.copies(in_refs, out_refs, sems)
        for cp in local + sends:
            cp.start()

    def wait(self, in_refs, out_refs, sems):
        local, sends, recvs = self.copies(in_refs, out_refs, sems)
        for cp in recvs:
            cp.wait_recv()
        for cp in sends:
            cp.wait_send()
        for cp in local:
            cp.wait()


def _pcall(core, *, name, grid, in_specs, out_specs, out_shape, args, sem, scratch_shapes=(), aliases=None,
           rider=None):
    n_in, n_out, n_scr = len(in_specs), len(out_specs), len(scratch_shapes)
    n_rin, n_rout = (len(rider.args), rider.n_buf) if rider else (0, 0)

    def body(*refs):
        ins, outs = refs[:n_in], refs[n_in + n_rin:n_in + n_rin + n_out]
        scr = refs[n_in + n_rin + n_out + n_rout:n_in + n_rin + n_out + n_rout + n_scr]
        if rider:
            r_refs = (refs[n_in:n_in + rider.n_src], refs[n_in + n_rin + n_out:n_in + n_rin + n_out + n_rout],
                      refs[n_in + n_rin + n_out + n_rout + n_scr:])
            ids = [pl.program_id(d) for d in range(len(grid))]
            first = functools.reduce(lambda p, q: p & q, [i == 0 for i in ids])
            last = functools.reduce(lambda p, q: p & q, [i == g - 1 for i, g in zip(ids, grid)])
            pl.when(first)(lambda: rider.start(*r_refs))
        core(ins, outs, scr)
        if rider:
            pl.when(last)(lambda: rider.wait(*r_refs))

    aliases = dict(aliases or {})
    if rider:
        sem = ("arbitrary",) * len(grid)
        aliases.update(rider.aliases(n_in, n_out))
    return pl.pallas_call(
        body, name=name, grid=grid, in_specs=list(in_specs) + [_ANY] * n_rin,
        out_specs=list(out_specs) + [_ANY] * n_rout,
        out_shape=list(out_shape) + (rider.out_shape if rider else []),
        scratch_shapes=list(scratch_shapes) + (rider.scratch if rider else []),
        input_output_aliases=aliases, compiler_params=_cparams(sem),
    )(*args, *(rider.args if rider else []))


def _exchange(direct, *, name):
    n_in = len(direct.args)

    def body(*refs):
        r_refs = refs[:direct.n_src], refs[n_in:n_in + direct.n_buf], refs[n_in + direct.n_buf:]
        direct.start(*r_refs)
        direct.wait(*r_refs)

    return pl.pallas_call(
        body, name=name, in_specs=[_ANY] * n_in, out_specs=[_ANY] * direct.n_buf, out_shape=direct.out_shape,
        input_output_aliases=direct.aliases(0, 0), scratch_shapes=direct.scratch,
    )(*direct.args)


def _matmul(a, b, *, form, tm, tn, tk, name, add=None, add_scale=1.0, rider=None, b_cols=None):
    if form == "nn":
        (m, kk), n = a.shape, b.shape[1]
        a_spec = pl.BlockSpec((tm, tk), lambda i, j, k: (i, k))
        b_spec = pl.BlockSpec((tk, tn), lambda i, j, k: (k, j))
        dn = (((1,), (0,)), ((), ()))
    elif form == "nt":
        (m, kk), n = a.shape, b.shape[0]
        a_spec = pl.BlockSpec((tm, tk), lambda i, j, k: (i, k))
        b_spec = pl.BlockSpec((tn, tk), lambda i, j, k: (j, k))
        dn = (((1,), (1,)), ((), ()))
    else:
        kk, m = a.shape
        n0, n = b_cols or (0, b.shape[1])
        assert n0 % tn == 0
        a_spec = pl.BlockSpec((tk, tm), lambda i, j, k: (k, i))
        b_spec = pl.BlockSpec((tk, tn), lambda i, j, k: (k, j + n0 // tn))
        dn = (((0,), (0,)), ((), ()))
    assert m % tm == 0 and n % tn == 0 and kk % tk == 0, (name, m, n, kk)
    has_add = add is not None

    def core(ins, outs, _):
        a_ref, b_ref = ins[:2]
        o_ref = outs[0]
        k = pl.program_id(2)
        p = lax.dot_general(a_ref[...].astype(MM_DTYPE), b_ref[...].astype(MM_DTYPE), dn,
                            preferred_element_type=F32)

        @pl.when(k == 0)
        def _():
            o_ref[...] = p + add_scale * ins[2][...] if has_add else p

        @pl.when(k > 0)
        def _():
            o_ref[...] += p

    in_specs = [a_spec, b_spec]
    args = [a, b]
    if has_add:
        in_specs.append(pl.BlockSpec((tm, tn), lambda i, j, k: (i, j)))
        args.append(add)
    res = _pcall(core, name=name, grid=(m // tm, n // tn, kk // tk), in_specs=in_specs,
                 out_specs=[pl.BlockSpec((tm, tn), lambda i, j, k: (i, j))],
                 out_shape=[jax.ShapeDtypeStruct((m, n), F32)], args=args,
                 sem=("parallel", "parallel", "arbitrary"), rider=rider)
    return res if rider else res[0]


ZERO_TAIL = 8


def _with_tail(x):
    return jnp.concatenate([x, jnp.zeros((ZERO_TAIL,) + x.shape[1:], x.dtype)], axis=0)


def _shift_down(x, k):
    return pltpu.roll(x, k, 0)


def _shift_up(x, k):
    return pltpu.roll(x, x.shape[0] - k, 0)


def _conv_slab(x, w):
    return w[3:4] * x + w[2:3] * _shift_down(x, 1) + w[1:2] * _shift_down(x, 2) + w[0:1] * _shift_down(x, 3)


def _prep_fwd(h, conv_w, *, name):
    t_len = h.shape[0]

    def body(x_ref, w_ref, o_ref):
        s = pl.program_id(0)
        y = _silu(_conv_slab(_with_tail(x_ref[...]), w_ref[...])[:t_len])
        rs = lax.rsqrt(jnp.sum(y * y, axis=-1, keepdims=True) + L2_EPS)
        scale = jnp.where(s < A_HEADS, A_HEAD_DIM ** -0.5, 1.0)
        o_ref[...] = jnp.where(s < 2 * A_HEADS, y * rs * scale, y)

    return pl.pallas_call(
        body, name=name, grid=(12,),
        in_specs=[pl.BlockSpec((t_len, LANE), lambda s: (0, L_QKV // LANE + s)),
                  pl.BlockSpec((8, LANE), lambda s: (0, s))],
        out_specs=pl.BlockSpec((t_len, LANE), lambda s: (0, s)),
        out_shape=jax.ShapeDtypeStruct((t_len, 3 * A_WIDTH), F32),
        compiler_params=_cparams(("parallel",)),
    )(h, conv_w)


def _prep_bwd(h, conv_w, d_out, dh, *, name):
    t_len = h.shape[0]

    def body(x_ref, w_ref, g_ref, dh_in, dx_ref, dw_ref):
        del dh_in
        s = pl.program_id(0)
        x = _with_tail(x_ref[...])
        g = _with_tail(g_ref[0])
        w = w_ref[...]
        xs = [_shift_down(x, 3), _shift_down(x, 2), _shift_down(x, 1), x]
        c = w[0:1] * xs[0] + w[1:2] * xs[1] + w[2:3] * xs[2] + w[3:4] * xs[3]
        sg = jax.nn.sigmoid(c)
        y = c * sg
        rs = lax.rsqrt(jnp.sum(y * y, axis=-1, keepdims=True) + L2_EPS)
        scale = jnp.where(s < A_HEADS, A_HEAD_DIM ** -0.5, 1.0)
        dy_n = scale * (rs * g - y * (rs * rs * rs) * jnp.sum(g * y, axis=-1, keepdims=True))
        dy = jnp.where(s < 2 * A_HEADS, dy_n, g)
        dc = dy * (sg * (1.0 + c * (1.0 - sg)))
        dx = w[3:4] * dc + w[2:3] * _shift_up(dc, 1) + w[1:2] * _shift_up(dc, 2) + w[0:1] * _shift_up(dc, 3)
        dx_ref[...] = dx[:t_len]
        dws = [jnp.sum(dc * xs[j], axis=0, keepdims=True) for j in range(CONV_K)]
        dw_ref[...] = jnp.concatenate(dws + [jnp.zeros((8 - CONV_K, LANE), F32)], axis=0)

    slab = pl.BlockSpec((t_len, LANE), lambda s: (0, L_QKV // LANE + s))
    return pl.pallas_call(
        body, name=name, grid=(12,),
        in_specs=[slab, pl.BlockSpec((8, LANE), lambda s: (0, s)),
                  pl.BlockSpec((1, t_len, LANE), lambda s: (s // A_HEADS, 0, s % A_HEADS)), _ANY],
        out_specs=[slab, pl.BlockSpec((8, LANE), lambda s: (0, s))],
        out_shape=[jax.ShapeDtypeStruct((t_len, L_COLS), F32), jax.ShapeDtypeStruct((8, 3 * A_WIDTH), F32)],
        input_output_aliases={3: 0},
        compiler_params=_cparams(("parallel",)),
    )(h, conv_w, d_out, dh)


N_LEVELS = 5
MF_TRIL, MF_STRIL, MF_DIAG8, MF_LOW16, MF_EYE = 0, 1, 2, 3, 3 + N_LEVELS
MB_CUM, MB_CUM_T, MB_TOT = 0, 1, 2


def _gdn_masks():
    r = lax.broadcasted_iota(jnp.int32, (SUPER, SUPER), 0)
    c = lax.broadcasted_iota(jnp.int32, (SUPER, SUPER), 1)
    same = lambda shift: (r >> shift) == (c >> shift)
    ninf = lambda m: jnp.where(m, 0.0, -jnp.inf).astype(F32)
    one = lambda m: m.astype(F32)
    mf = jnp.stack([ninf(r >= c), ninf(r > c), one(same(3))]
                   + [one(same(4 + lv) & jnp.logical_not(same(3 + lv))) for lv in range(N_LEVELS)] + [one(r == c)])
    mb = jnp.stack([one(r >= c), one(r <= c), jnp.ones((SUPER, SUPER), F32)]).astype(BF16)
    return mf, mb


def _tri_inv_impl(a, mf):
    d = lambda p, q: jnp.dot(p.astype(BF16), q.astype(BF16), preferred_element_type=F32)
    dd = lambda p, q: jnp.dot(p, q, preferred_element_type=F32)
    eye = mf[MF_EYE]
    a0 = a * mf[MF_DIAG8]
    a2 = d(a0, a0)
    a4 = d(a2, a2)
    t = d(d(eye - a0, eye + a2), eye + a4)
    for level in range(N_LEVELS):
        t = t - d(d(t, a * mf[MF_LOW16 + level]), t)
    a_hi, a_lo = _split(a)
    for _ in range(NEWTON_STEPS):
        t0 = t.astype(BF16)
        t0f = t0.astype(F32)
        resid = (eye - t0f) - (dd(a_hi, t0) + dd(a_lo, t0))
        r_hi, r_lo = _split(resid)
        t = t0f + (dd(t0, r_hi) + dd(t0, r_lo))
    return t


@jax.custom_vjp
def _wy_apply(a, rhs, t):
    return _mm(t, rhs)


def _wy_apply_fwd(a, rhs, t):
    x = _mm(t, rhs)
    return x, (t, x)


def _wy_apply_bwd(res, dx):
    t, x = res
    d_rhs = _mm_tn(t, dx)
    return -_mm_nt(d_rhs, x), d_rhs, jnp.zeros_like(t)


_wy_apply.defvjp(_wy_apply_fwd, _wy_apply_bwd)


@functools.partial(jax.custom_vjp, nondiff_argnums=(1,))
def _lane_roll(x, shift):
    return pltpu.roll(x, shift % LANE, 1)


_lane_roll.defvjp(lambda x, shift: (_lane_roll(x, shift), None), lambda shift, _, g: (_lane_roll(g, -shift),))


def _mask_times_lanes(x, mask):
    lane = lax.broadcasted_iota(jnp.int32, (1, LANE), 1)
    x = jnp.where(lane < A_HEADS, x, 0.0)
    x1 = x.astype(BF16).astype(F32)
    x2 = (x - x1).astype(BF16).astype(F32)
    x3 = (x - x1 - x2).astype(BF16).astype(F32)
    pieces = x1 + pltpu.roll(x2, A_HEADS, 1) + pltpu.roll(x3, 2 * A_HEADS, 1)
    res = jnp.dot(mask, pieces.astype(BF16), preferred_element_type=F32)
    return res + pltpu.roll(res, LANE - A_HEADS, 1) + pltpu.roll(res, LANE - 2 * A_HEADS, 1)


@jax.custom_vjp
def _chunk_sums(g, mb):
    return _mask_times_lanes(g, mb[MB_CUM]), _mask_times_lanes(g, mb[MB_TOT])


def _chunk_sums_fwd(g, mb):
    return _chunk_sums(g, mb), mb


def _chunk_sums_bwd(mb, d):
    lane = lax.broadcasted_iota(jnp.int32, (1, LANE), 1)
    dg = _mask_times_lanes(d[0], mb[MB_CUM_T]) + _mask_times_lanes(d[1], mb[MB_TOT])
    return jnp.where(lane < A_HEADS, dg, 0.0), jnp.zeros_like(mb)


_chunk_sums.defvjp(_chunk_sums_fwd, _chunk_sums_bwd)


def _gdn_gates(ba, alog, dtb, mb):
    beta = jax.nn.sigmoid(ba)
    g = -jnp.exp(alog) * _softplus(_lane_roll(ba, -A_HEADS) + dtb)
    gc, gl = _chunk_sums(g, mb)
    return beta, gc, gl, gc.T


def _gdn_block(s, q, k, v, z, gates, nw, h, t_known, mf):
    n = q.shape[0]
    beta_all, gc_all, gl_all, gct_all = gates
    lane = lax.broadcasted_iota(jnp.int32, (1, LANE), 1)
    sub = lax.broadcasted_iota(jnp.int32, (LANE, 1), 0)
    col = lambda x: jnp.sum(jnp.where(lane == h, x, 0.0), axis=1, keepdims=True)
    wide = lambda c: jnp.broadcast_to(c, (n, LANE))
    gc, gl = col(gc_all), col(gl_all)
    gc_row = jnp.sum(jnp.where(sub == h, gct_all, 0.0), axis=0, keepdims=True)
    beta_w, eg_w = wide(col(beta_all)), wide(jnp.exp(gc))
    diff = gc - gc_row
    decay = jnp.exp(diff + mf[MF_TRIL])
    kb = k * beta_w
    a_mat = _mm_nt(kb, k) * jnp.exp(diff + mf[MF_STRIL])
    rhs = jnp.concatenate([v * beta_w, kb * eg_w], axis=1)
    if t_known is None:
        t_mat = _tri_inv_impl(a_mat, mf)
        uw = _mm(t_mat, rhs)
    else:
        t_mat = t_known
        uw = _wy_apply(a_mat, rhs, t_known)
    u, w = uw[:, :LANE], uw[:, LANE:]
    qk = _mm_nt(q, k) * decay
    q_dec = q * eg_w
    k_dec = k * wide(jnp.exp(gl - gc))
    v_new = u - _mm(w, s)
    o = _mm(q_dec, s) + _mm(qk, v_new)
    s = s * jnp.exp(gl[0:1]) + _mm_tn(k_dec, v_new)
    o = o * lax.rsqrt(jnp.mean(o * o, axis=-1, keepdims=True) + RMS_EPS) * nw
    return o * _silu(z), s, t_mat


def _gdn_fwd(qkv, h, alog, dtb, nw, ycat, *, name, rider=None):
    t_len = qkv.shape[0]
    nsc = t_len // SUPER

    def core(ins, outs, scr):
        q_ref, k_ref, v_ref, gate_ref, al_ref, dt_ref, nw_ref, mf_ref, mb_ref, _ = ins
        y_ref, sin_ref, t_ref = outs
        s_scr, = scr

        @pl.when(pl.program_id(0) == 0)
        def _():
            s_scr[...] = jnp.zeros_like(s_scr)

        per_head = lambda ref: jnp.stack([ref[:, hh * LANE:(hh + 1) * LANE] for hh in range(A_HEADS)])
        states = s_scr[...]
        gates = _gdn_gates(gate_ref[:, A_WIDTH:], al_ref[...], dt_ref[...], mb_ref[...])
        fn = jax.vmap(_gdn_block, in_axes=(0, 0, 0, 0, 0, None, None, 0, None, None))
        y, s_new, t_mat = fn(states, per_head(q_ref), per_head(k_ref), per_head(v_ref), per_head(gate_ref),
                             gates, nw_ref[...], jnp.arange(A_HEADS), None, mf_ref[...])
        sin_ref[0] = states
        t_ref[0] = t_mat
        s_scr[...] = s_new
        for hh in range(A_HEADS):
            y_ref[:, hh * LANE:(hh + 1) * LANE] = y[hh]

    blk = lambda j: pl.BlockSpec((SUPER, A_WIDTH), lambda sc: (sc, j))
    row = pl.BlockSpec((1, LANE), lambda sc: (0, 0))
    mf, mb = _gdn_masks()
    whole = lambda a: pl.BlockSpec(a.shape, lambda sc: (0, 0, 0))
    return _pcall(
        core, name=name, grid=(nsc,),
        in_specs=[blk(0), blk(1), blk(2), pl.BlockSpec((SUPER, L_GATE), lambda sc: (sc, L_ZA // L_GATE)),
                  row, row, row, whole(mf), whole(mb), _ANY],
        out_specs=[blk(0),
                   pl.BlockSpec((1, A_HEADS, A_HEAD_DIM, A_HEAD_DIM), lambda sc: (sc, 0, 0, 0)),
                   pl.BlockSpec((1, A_HEADS, SUPER, SUPER), lambda sc: (sc, 0, 0, 0))],
        out_shape=[jax.ShapeDtypeStruct((t_len, D_MODEL), F32),
                   jax.ShapeDtypeStruct((nsc, A_HEADS, A_HEAD_DIM, A_HEAD_DIM), F32),
                   jax.ShapeDtypeStruct((nsc, A_HEADS, SUPER, SUPER), F32)],
        scratch_shapes=[pltpu.VMEM((A_HEADS, A_HEAD_DIM, A_HEAD_DIM), F32)],
        aliases={9: 0}, sem=("arbitrary",), rider=rider,
        args=(qkv, qkv, qkv, h, alog, dtb, nw, mf, mb, ycat))


def _gdn_bwd(qkv, h, alog, dtb, nw, s_in, t_in, dycat, dh, *, name, rider=None):
    t_len = qkv.shape[0]
    nsc = t_len // SUPER

    def core(ins, outs, scr):
        q_ref, k_ref, v_ref, gate_ref, al_ref, dt_ref, nw_ref, sin_ref, t_ref, dy_ref, mf_ref, mb_ref, _ = ins
        dgate_ref, dqkv_ref, dal_ref, ddt_ref, dnw_ref = outs
        ds_scr, = scr

        @pl.when(pl.program_id(0) == 0)
        def _():
            ds_scr[...] = jnp.zeros_like(ds_scr)
            dal_ref[...] = jnp.zeros_like(dal_ref)
            ddt_ref[...] = jnp.zeros_like(ddt_ref)
            dnw_ref[...] = jnp.zeros_like(dnw_ref)

        per_head = lambda ref: jnp.stack([ref[:, hh * LANE:(hh + 1) * LANE] for hh in range(A_HEADS)])
        head_ids = jnp.arange(A_HEADS)
        t_known, mf, mb = t_ref[0], mf_ref[...], mb_ref[...]

        def fn(s, q, k, v, z, ba, alog, dtb, nw):
            gates = _gdn_gates(ba, alog, dtb, mb)
            one = lambda s, q, k, v, z, t, h: _gdn_block(s, q, k, v, z, gates, nw, h, t, mf)[:2]
            return jax.vmap(one)(s, q, k, v, z, t_known, head_ids)

        _, vjp = jax.vjp(fn, sin_ref[0], per_head(q_ref), per_head(k_ref), per_head(v_ref), per_head(gate_ref),
                         gate_ref[:, A_WIDTH:], al_ref[...], dt_ref[...], nw_ref[...])
        ds, dq, dk, dv, dz, dba, dal, ddt, dnw = vjp((per_head(dy_ref), ds_scr[...]))
        ds_scr[...] = ds
        for hh in range(A_HEADS):
            cols = slice(hh * LANE, (hh + 1) * LANE)
            dqkv_ref[0, :, cols] = dq[hh]
            dqkv_ref[1, :, cols] = dk[hh]
            dqkv_ref[2, :, cols] = dv[hh]
            dgate_ref[:, cols] = dz[hh]
        dgate_ref[:, A_WIDTH:] = dba
        dal_ref[...] += dal
        ddt_ref[...] += ddt
        dnw_ref[...] += dnw

    rev = lambda i: nsc - 1 - i
    blk = lambda j: pl.BlockSpec((SUPER, A_WIDTH), lambda i: (rev(i), j))
    gate = pl.BlockSpec((SUPER, L_GATE), lambda i: (rev(i), L_ZA // L_GATE))
    row = pl.BlockSpec((1, LANE), lambda i: (0, 0))
    mf, mb = _gdn_masks()
    whole = lambda a: pl.BlockSpec(a.shape, lambda i: (0, 0, 0))
    return _pcall(
        core, name=name, grid=(nsc,),
        in_specs=[blk(0), blk(1), blk(2), gate, row, row, row,
                  pl.BlockSpec((1, A_HEADS, A_HEAD_DIM, A_HEAD_DIM), lambda i: (rev(i), 0, 0, 0)),
                  pl.BlockSpec((1, A_HEADS, SUPER, SUPER), lambda i: (rev(i), 0, 0, 0)),
                  blk(0), whole(mf), whole(mb), _ANY],
        out_specs=[gate, pl.BlockSpec((3, SUPER, A_WIDTH), lambda i: (0, rev(i), 0)), row, row, row],
        out_shape=[jax.ShapeDtypeStruct((t_len, L_COLS), F32), jax.ShapeDtypeStruct((3, t_len, A_WIDTH), F32)]
        + [jax.ShapeDtypeStruct((1, LANE), F32)] * 3,
        scratch_shapes=[pltpu.VMEM((A_HEADS, A_HEAD_DIM, A_HEAD_DIM), F32)],
        aliases={12: 0}, sem=("arbitrary",), rider=rider,
        args=(qkv, qkv, qkv, h, alog, dtb, nw, s_in, t_in, dycat, mf, mb, dh))


Q_BLOCKS = 4
Q_ROWS = Q_BLOCKS * BLOCK


def _swa_block(q, kp, kc, vp, vc, z, sinks, first):
    rows = B_GROUP * BLOCK
    ri = lax.broadcasted_iota(jnp.int32, (rows, 2 * BLOCK), 0)
    si = lax.broadcasted_iota(jnp.int32, (rows, 2 * BLOCK), 1)
    dist = (ri & (BLOCK - 1)) + BLOCK - si
    bias = jnp.where((dist >= 0) & (dist < WINDOW), 0.0, -jnp.inf)
    no_prev = jnp.where(first & (si[:1] < BLOCK), -jnp.inf, 0.0)
    dist_f = dist.astype(F32)
    head_of_row = lax.broadcasted_iota(jnp.int32, (rows, 1), 0) >> 7
    keys = jnp.concatenate([kp, kc], axis=0)
    vals = jnp.concatenate([vp, vc], axis=0)

    def item(b, j):
        cs = slice(j * B_HEAD_DIM, (j + 1) * B_HEAD_DIM)
        rs = slice(b * BLOCK, (b + 1) * BLOCK)
        heads = range(j * B_GROUP, (j + 1) * B_GROUP)
        qs = jnp.concatenate([q[rs, hq * B_HEAD_DIM:(hq + 1) * B_HEAD_DIM] for hq in heads], axis=0) * (
            B_HEAD_DIM ** -0.5)
        kk = keys[b * BLOCK:(b + 2) * BLOCK, cs]
        vv = vals[b * BLOCK:(b + 2) * BLOCK, cs]
        sink = jnp.concatenate([jnp.broadcast_to(sinks[:, hq:hq + 1], (BLOCK, 1)) for hq in heads], axis=0)
        slope = sum(jnp.where(head_of_row == gi, 2.0 ** (-8.0 * (hq + 1) / B_Q_HEADS), 0.0)
                    for gi, hq in enumerate(heads))
        return qs, kk, vv, sink, slope, (no_prev if b == 0 else jnp.zeros_like(no_prev))

    def attend(qs, kk, vv, sink, slope, hide):
        sc = _mm_nt(qs, kk) - slope * dist_f + (bias + hide)
        m = lax.stop_gradient(jnp.maximum(jnp.max(sc, axis=-1, keepdims=True), sink))
        p = jnp.exp(sc - m)
        inv = 1.0 / (jnp.sum(p, axis=-1, keepdims=True) + jnp.exp(sink - m))
        return _mm(p * inv, vv)

    items = [(b, j) for b in range(Q_BLOCKS) for j in range(B_KV_HEADS)]
    o = jax.vmap(attend)(*[_stack(t) for t in zip(*[item(b, j) for b, j in items])])
    rows_out = [jnp.concatenate([o[b * B_KV_HEADS + j, gi * BLOCK:(gi + 1) * BLOCK]
                                 for j in range(B_KV_HEADS) for gi in range(B_GROUP)], axis=1)
                for b in range(Q_BLOCKS)]
    return jnp.concatenate(rows_out, axis=0) * _silu(z)


def _swa_specs(idx):
    wide = lambda off: pl.BlockSpec((Q_ROWS, B_WIDTH), lambda n: (idx(n), off))
    cur = lambda off: pl.BlockSpec((Q_ROWS, LANE), lambda n: (idx(n), off))
    prev = lambda off: pl.BlockSpec((BLOCK, LANE), lambda n: (jnp.maximum(idx(n) * Q_BLOCKS - 1, 0), off))
    return [wide(L_QB // B_WIDTH), prev(L_KB // LANE), cur(L_KB // LANE), prev(L_VB // LANE), cur(L_VB // LANE),
            wide(L_ZB // B_WIDTH), pl.BlockSpec((1, LANE), lambda n: (0, 0))]


def _swa_fwd(h, sinks, *, name, rider=None):
    t_len = h.shape[0]
    nb = t_len // Q_ROWS

    def core(ins, outs, _):
        q_ref, kp_ref, kc_ref, vp_ref, vc_ref, z_ref, s_ref = ins
        outs[0][...] = _swa_block(q_ref[...], kp_ref[...], kc_ref[...], vp_ref[...], vc_ref[...], z_ref[...],
                                  s_ref[...], pl.program_id(0) == 0)

    res = _pcall(core, name=name, grid=(nb,), in_specs=_swa_specs(lambda n: n),
                 out_specs=[pl.BlockSpec((Q_ROWS, B_WIDTH), lambda n: (n, 1))],
                 out_shape=[jax.ShapeDtypeStruct((t_len, D_MODEL), F32)], sem=("parallel",), rider=rider,
                 args=(h, h, h, h, h, h, sinks))
    return res if rider else res[0]


def _swa_bwd(h, sinks, dycat, *, name, rider=None):
    t_len = h.shape[0]
    nb = t_len // Q_ROWS
    last = slice(Q_ROWS - BLOCK, Q_ROWS)

    def core(ins, outs, scr):
        q_ref, kp_ref, kc_ref, vp_ref, vc_ref, z_ref, s_ref, dy_ref = ins
        dh_ref, dsk_ref = outs
        ck_scr, cv_scr = scr
        i = pl.program_id(0)
        n = nb - 1 - i

        @pl.when(i == 0)
        def _():
            ck_scr[...] = jnp.zeros_like(ck_scr)
            cv_scr[...] = jnp.zeros_like(cv_scr)
            dsk_ref[...] = jnp.zeros_like(dsk_ref)

        fn = functools.partial(_swa_block, first=(n == 0))
        _, vjp = jax.vjp(fn, q_ref[...], kp_ref[...], kc_ref[...], vp_ref[...], vc_ref[...], z_ref[...], s_ref[...])
        dq, dkp, dkc, dvp, dvc, dz, dsk = vjp(dy_ref[...])
        dh_ref[:, L_QB:L_QB + B_WIDTH] = dq
        dh_ref[:, L_ZB:L_ZB + B_WIDTH] = dz
        dh_ref[:, L_KB:L_KB + LANE] = dkc
        dh_ref[:, L_VB:L_VB + LANE] = dvc
        dh_ref[last, L_KB:L_KB + LANE] += ck_scr[...]
        dh_ref[last, L_VB:L_VB + LANE] += cv_scr[...]
        ck_scr[...] = dkp
        cv_scr[...] = dvp
        dsk_ref[...] += dsk

    rev = lambda i: nb - 1 - i
    return _pcall(
        core, name=name, grid=(nb,),
        in_specs=_swa_specs(rev) + [pl.BlockSpec((Q_ROWS, B_WIDTH), lambda i: (rev(i), 1))],
        out_specs=[pl.BlockSpec((Q_ROWS, L_SWA), lambda i: (rev(i), 0)), pl.BlockSpec((1, LANE), lambda i: (0, 0))],
        out_shape=[jax.ShapeDtypeStruct((t_len, L_COLS), F32), jax.ShapeDtypeStruct((1, LANE), F32)],
        scratch_shapes=[pltpu.VMEM((BLOCK, LANE), F32), pltpu.VMEM((BLOCK, LANE), F32)],
        sem=("arbitrary",), rider=rider, args=(h, h, h, h, h, h, sinks, dycat))


def _out_ln_fwd(ycat, w_out, x, ln_g, ln_b, *, name, tm=512, last=False):
    t_len = x.shape[0]

    def body(y_ref, w_ref, x_ref, g_ref, b_ref, r_ref, *o_ref):
        r = DEEPNORM_ALPHA * x_ref[...] + _mm(y_ref[...], w_ref[...])
        r_ref[...] = r
        if not last:
            mu = jnp.mean(r, axis=-1, keepdims=True)
            d = r - mu
            var = jnp.mean(d * d, axis=-1, keepdims=True)
            o_ref[0][...] = d * lax.rsqrt(var + LN_EPS) * g_ref[...] + b_ref[...]

    tile = pl.BlockSpec((tm, D_MODEL), lambda i: (i, 0))
    vec = pl.BlockSpec((1, D_MODEL), lambda i: (0, 0))
    n_out = 1 if last else 2
    res = pl.pallas_call(
        body, name=name, grid=(t_len // tm,),
        in_specs=[tile, pl.BlockSpec((D_MODEL, D_MODEL), lambda i: (0, 0)), tile, vec, vec],
        out_specs=[tile] * n_out,
        out_shape=[jax.ShapeDtypeStruct((t_len, D_MODEL), F32)] * n_out,
        compiler_params=_cparams(("parallel",)),
    )(ycat, w_out, x, ln_g, ln_b)
    return (res[0], None) if last else res


def _ln_bwd(dxn, r, ln_g, *, name, tm=512, loss=None):
    t_len = r.shape[0]

    def body(*refs):
        if loss:
            t_ref, r_ref, g_ref, b_ref, dr_ref, dg_ref, db_ref, l_ref = refs
        else:
            dx_ref, r_ref, g_ref, dr_ref, dg_ref, db_ref = refs

        @pl.when(pl.program_id(0) == 0)
        def _():
            dg_ref[...] = jnp.zeros_like(dg_ref)
            db_ref[...] = jnp.zeros_like(db_ref)
            if loss:
                l_ref[...] = jnp.zeros_like(l_ref)

        rr = r_ref[...]
        mu = jnp.mean(rr, axis=-1, keepdims=True)
        d = rr - mu
        rstd = lax.rsqrt(jnp.mean(d * d, axis=-1, keepdims=True) + LN_EPS)
        xh = d * rstd
        if loss:
            e = (xh * g_ref[...] + b_ref[...]) - t_ref[...]
            dx = e * (1.0 / D_MODEL)
            l_ref[...] += jnp.sum(e * e, axis=0, keepdims=True)
        else:
            dx = dx_ref[...]
        dxh = dx * g_ref[...]
        dr_ref[...] = rstd * (dxh - jnp.mean(dxh, axis=-1, keepdims=True)
                              - xh * jnp.mean(dxh * xh, axis=-1, keepdims=True))
        dg_ref[...] += jnp.sum(dx * xh, axis=0, keepdims=True)
        db_ref[...] += jnp.sum(dx, axis=0, keepdims=True)

    tile = pl.BlockSpec((tm, D_MODEL), lambda i: (i, 0))
    vec = pl.BlockSpec((1, D_MODEL), lambda i: (0, 0))
    vec_shape = jax.ShapeDtypeStruct((1, D_MODEL), F32)
    args = (loss[0], r, ln_g, loss[1]) if loss else (dxn, r, ln_g)
    return pl.pallas_call(
        body, name=name, grid=(t_len // tm,),
        in_specs=[tile, tile, vec] + ([vec] if loss else []), out_specs=[tile, vec, vec] + ([vec] if loss else []),
        out_shape=[jax.ShapeDtypeStruct((t_len, D_MODEL), F32), vec_shape, vec_shape] + ([vec_shape] if loss else []),
        compiler_params=_cparams(("arbitrary",)),
    )(*args)


def _pad_row(v):
    return jnp.zeros((1, LANE), F32).at[0, :v.shape[0]].set(v)


_REGIONS = ((0, 1536, L_QKV), (1536, 2048, L_ZA), (2048, 2056, L_BA), (2056, 2568, L_QB), (2568, 2696, L_KB),
            (2696, 2824, L_VB), (2824, 3336, L_ZB))


def _shard_pieces(regions):
    for a, b, off in regions:
        for d in range(N_DEV):
            lo, hi = max(a, d * SHARD_COLS), min(b, (d + 1) * SHARD_COLS)
            if lo < hi:
                yield d, lo - d * SHARD_COLS, hi - d * SHARD_COLS, off + lo - a


def _as_list(r):
    return list(r) if isinstance(r, (list, tuple)) else [r]


def _gathered(shard):
    return jax.ShapeDtypeStruct((N_DEV,) + shard.shape, shard.dtype)


def _full_w_in(g_in, name):
    by_offset = sorted(_shard_pieces(_REGIONS), key=lambda p: p[3])
    tc = 256

    def body(g_ref, o_ref):
        pieces, row = [], 0
        for d, lo, hi, off in by_offset + [(None, 0, 0, L_COLS)]:
            if off > row:
                pieces.append(jnp.zeros((off - row, tc), g_ref.dtype))
            if d is not None:
                pieces.append(g_ref[d, lo:hi, :])
            row = off + hi - lo
        o_ref[...] = jnp.concatenate(pieces, axis=0)

    return pl.pallas_call(
        body, name=name, grid=(D_MODEL // tc,),
        in_specs=[pl.BlockSpec((N_DEV, SHARD_COLS, tc), lambda i: (0, 0, i))],
        out_specs=pl.BlockSpec((L_COLS, tc), lambda i: (0, i)),
        out_shape=jax.ShapeDtypeStruct((L_COLS, D_MODEL), g_in.dtype),
        compiler_params=_cparams(("parallel",)),
    )(g_in)


def _full_conv(g_conv):
    return jnp.pad(g_conv.transpose(1, 0, 2).reshape(CONV_K, 3 * A_WIDTH), ((0, 8 - CONV_K), (0, 0)))


def _forward(x, weights, shards, small):
    a_log, dt_bias, norm_w, sinks, ln_g, ln_b = small
    tm = min(512, x.shape[0])
    saved, weights = [], [list(w) for w in weights]
    whole = lambda arrs: _Direct([(a, False, j, ()) for j, a in enumerate(arrs)], [_gathered(a) for a in arrs])
    for l in range(DEPTH):
        rider = whole(shards[l][1:]) if weights[l][1] is None else None
        h, *got = _as_list(_matmul(x, weights[l][0], form="nt", tm=tm, tn=L_COLS, tk=D_MODEL, name=f"in_proj_{l}",
                                   rider=rider))
        if rider:
            weights[l][1:] = [got[0].reshape(D_MODEL, D_MODEL), _full_conv(got[1])]
        w_in_l, w_out_l, conv_l = weights[l]
        qkv = _prep_fwd(h, conv_l, name=f"prep_fwd_{l}")
        al, dt, nw, sk = _pad_row(a_log[l]), _pad_row(dt_bias[l]), norm_w[l][None, :], _pad_row(sinks[l])
        ahead = l + 1 < DEPTH and weights[l + 1][0] is None
        rider = whole(shards[l + 1][1:]) if ahead else None
        ycat, *got = _as_list(_swa_fwd(h, sk, name=f"swa_fwd_{l}", rider=rider))
        if ahead:
            weights[l + 1][1:] = [got[0].reshape(D_MODEL, D_MODEL), _full_conv(got[1])]
        rider = whole(shards[l + 1][:1]) if ahead else None
        ycat, s_in, t_in, *got = _gdn_fwd(qkv, h, al, dt, nw, ycat, name=f"gdn_fwd_{l}", rider=rider)
        if ahead:
            weights[l + 1][0] = _full_w_in(got[0], f"w_in_rows_{l + 1}")
        r, xn = _out_ln_fwd(ycat, w_out_l, x, ln_g[l][None, :], ln_b[l][None, :], name=f"out_ln_{l}",
                            last=(l == DEPTH - 1))
        saved.append((x, h, qkv, s_in, t_in, ycat, r, al, dt, nw, sk))
        x = xn
    return x, saved, weights


def _w_in_blocks(g, name):
    cols, tc = g.shape[1], 256
    pieces = list(_shard_pieces(_REGIONS))

    def body(g_ref, o_ref):
        blocks = [[] for _ in range(N_DEV)]
        for d, lo, hi, off in pieces:
            blocks[d].append(g_ref[off:off + hi - lo, :])
        for d in range(N_DEV):
            o_ref[d] = jnp.concatenate(blocks[d], axis=0).astype(BF16)

    return pl.pallas_call(
        body, name=name, grid=(cols // tc,),
        in_specs=[pl.BlockSpec((L_COLS, tc), lambda i: (0, i))],
        out_specs=pl.BlockSpec((N_DEV, SHARD_COLS, tc), lambda i: (0, 0, i)),
        out_shape=jax.ShapeDtypeStruct((N_DEV, SHARD_COLS, cols), BF16),
        compiler_params=_cparams(("parallel",)),
    )(g)


def _small_blocks(g):
    c_conv = g["conv_w"].reshape(CONV_K, N_DEV, CONV_SHARD_COLS).transpose(1, 0, 2)
    c_small = [jnp.broadcast_to(g[n][None], (N_DEV,) + g[n].shape) for n, _ in SMALL_SIZES]
    return _pack_small(c_conv, c_small)


def _contributions(g):
    c_out = g["w_out"].astype(BF16).reshape(N_DEV, OUT_SHARD_ROWS, D_MODEL)
    return _w_in_blocks(g["w_in_rows"], name="w_in_grad_blocks_above"), c_out, _small_blocks(g)


def _backward_layer(l, dx, saved_l, weights_l, ln_g_l, above=None, loss=None):
    x_in, h, qkv, s_in, t_in, ycat, r, al, dt, nw, sk = saved_l
    w_in_l, w_out_l, conv_l = weights_l
    tm = min(512, x_in.shape[0])
    dr, d_lng, d_lnb, *loss_lanes = _ln_bwd(dx, r, ln_g_l[None, :], name=f"ln_bwd_{l}", loss=loss)
    big = min(1024, x_in.shape[0])
    dycat = _matmul(dr, w_out_l, form="nt", tm=big, tn=D_MODEL, tk=D_MODEL, name=f"out_proj_dx_{l}")
    d_wout = _matmul(ycat, dr, form="tn", tm=D_MODEL, tn=D_MODEL, tk=big, name=f"out_proj_dw_{l}")
    rider, p_in, p_out, p_small = None, None, None, None
    recv = lambda c: jax.ShapeDtypeStruct((DEPTH,) + c.shape, c.dtype)
    if above:
        c_out = d_wout.astype(BF16).reshape(N_DEV, OUT_SHARD_ROWS, D_MODEL)
        rider = _Direct([(above[1], True, 0, (l + 1,)), (above[2], True, 1, (l + 1,)), (c_out, True, 0, (l,))],
                        [recv(above[1]), recv(above[2])])
    dh, d_sk, *got = _swa_bwd(h, sk, dycat, name=f"swa_bwd_{l}", rider=rider)
    if above:
        p_out, p_small = got
        rider = _Direct([(above[0], True, 0, (l + 1,))], [recv(above[0])])
    dh, dqkv_n, d_al, d_dt, d_nw, *got = _gdn_bwd(qkv, h, al, dt, nw, s_in, t_in, dycat, dh,
                                                  name=f"gdn_bwd_{l}", rider=rider)
    dh, d_conv = _prep_bwd(h, conv_l, dqkv_n, dh, name=f"prep_bwd_{l}")
    grads = dict(w_out=d_wout, conv_w=d_conv[:CONV_K], a_log=d_al[0, :A_HEADS], dt_bias=d_dt[0, :A_HEADS],
                 norm_w=d_nw[0], sinks=d_sk[0, :B_Q_HEADS], ln_g=d_lng[0], ln_b=d_lnb[0])
    dw = functools.partial(_matmul, dh, x_in, form="tn", tm=L_COLS, tk=tm)
    if not above:
        grads["w_in_rows"] = dw(name=f"in_proj_dw_{l}", tn=D_MODEL)
    else:
        p_in, = got
        cut = D_MODEL // 2
        rest = D_MODEL - cut
        first = dw(name=f"in_proj_dw_first_{l}", tn=cut, b_cols=(0, cut))
        blocks = _w_in_blocks(first, name=f"w_in_grad_blocks_first_{l}")
        rider = _Direct([(blocks, True, 0, (l,), (slice(None), pl.ds(0, cut)))], [p_in])
        second, p_in = dw(name=f"in_proj_dw_second_{l}", tn=cut, b_cols=(cut, rest), rider=rider)
        blocks = _w_in_blocks(second, name=f"w_in_grad_blocks_second_{l}")
        rider = _Direct([(blocks, True, 0, (l,), (slice(None), pl.ds(cut, rest))),
                         (_small_blocks(grads), True, 1, (l,))], [p_in, p_small])
    dx, *got = _as_list(_matmul(dh, w_in_l, form="nn", tm=tm, tn=D_MODEL, tk=L_COLS, name=f"in_proj_dx_{l}",
                                add=dr, add_scale=DEEPNORM_ALPHA, rider=rider))
    bufs = (got[0], p_out, got[1]) if above else None
    return dx, grads, bufs, (loss_lanes[0] if loss else None)


def _all_gather(shards, *, name):
    n_arr = len(shards)

    def body(*refs):
        x_refs, out_refs = refs[:n_arr], refs[n_arr:2 * n_arr]
        send_sems, recv_sems, local_sems = refs[2 * n_arr:]
        x, y, c = _me()
        me, sibling = (x, y, c), (x, y, 1 - c)
        chips = [(1 - x, y), (x, 1 - y), (1 - x, 1 - y)]

        def copy(a, k, block, to, src=None):
            dst = out_refs[a].at[_flat_id(block)]
            return _remote(dst if src is None else src, dst, send_sems.at[a, k], recv_sems.at[a, k], to)

        mine = [pltpu.make_async_copy(x_refs[a], out_refs[a].at[_flat_id(me)], local_sems.at[a])
                for a in range(n_arr)]
        for cp in mine:
            cp.start()
        first = []
        for a in range(n_arr):
            first.append(copy(a, 0, me, sibling, src=x_refs[a]))
            first += [copy(a, 1 + j, me, (*chip, c), src=x_refs[a]) for j, chip in enumerate(chips)]
        for cp in first:
            cp.start()
        passed = []
        for j, chip in enumerate(chips):
            for a in range(n_arr):
                copy(a, 1 + j, (*chip, c), me).wait_recv()
                fwd = copy(a, 4 + j, (*chip, c), sibling)
                fwd.start()
                passed.append(fwd)
        for a in range(n_arr):
            copy(a, 0, sibling, me).wait_recv()
            for j, chip in enumerate(chips):
                copy(a, 4 + j, (*chip, 1 - c), me).wait_recv()
        for cp in first + passed:
            cp.wait_send()
        for cp in mine:
            cp.wait()

    return pl.pallas_call(
        body, name=name, in_specs=[_ANY] * n_arr, out_specs=[_ANY] * n_arr,
        out_shape=[jax.ShapeDtypeStruct((N_DEV,) + s.shape, s.dtype) for s in shards],
        scratch_shapes=[pltpu.SemaphoreType.DMA((n_arr, N_DEV - 1)), pltpu.SemaphoreType.DMA((n_arr, N_DEV - 1)),
                        pltpu.SemaphoreType.DMA((n_arr,))],
    )(*shards)


def _adamw(parts, w, m, v, *, tr, name):
    depth, rows, cols = w.shape
    c1 = 1.0 - ADAM_B1 ** ADAM_STEP
    c2 = 1.0 - ADAM_B2 ** ADAM_STEP

    def body(g_ref, w_ref, m_ref, v_ref, go_ref, d_ref, mo_ref, vo_ref):
        g = g_ref[0, 0].astype(F32)
        for s in range(1, N_DEV):
            g = g + g_ref[0, s].astype(F32)
        m_new = ADAM_B1 * m_ref[0] + (1.0 - ADAM_B1) * g
        v_new = ADAM_B2 * v_ref[0] + (1.0 - ADAM_B2) * (g * g)
        go_ref[0] = g
        mo_ref[0] = m_new
        vo_ref[0] = v_new
        d_ref[0] = -ADAM_LR * ((m_new / c1) / (jnp.sqrt(v_new / c2) + ADAM_EPS) + ADAM_WD * w_ref[0])

    tile = pl.BlockSpec((1, tr, cols), lambda l, i: (l, i, 0))
    return pl.pallas_call(
        body, name=name, grid=(depth, rows // tr),
        in_specs=[pl.BlockSpec((1, N_DEV, tr, cols), lambda l, i: (l, 0, i, 0)), tile, tile, tile],
        out_specs=[tile] * 4, out_shape=[jax.ShapeDtypeStruct(w.shape, F32)] * 4,
        compiler_params=_cparams(("parallel", "parallel")),
    )(parts, w, m, v)


def _adamw_w_in(parts, w, m, v, *, name):
    c1 = 1.0 - ADAM_B1 ** ADAM_STEP
    c2 = 1.0 - ADAM_B2 ** ADAM_STEP

    def body(g_ref, w_ref, m_ref, v_ref, go_ref, d_ref, mo_ref, vo_ref):
        for l in range(DEPTH):
            g = g_ref[l, 0].astype(F32)
            for s in range(1, N_DEV):
                g = g + g_ref[l, s].astype(F32)
            m_new = ADAM_B1 * m_ref[:, l, :] + (1.0 - ADAM_B1) * g
            v_new = ADAM_B2 * v_ref[:, l, :] + (1.0 - ADAM_B2) * (g * g)
            go_ref[:, l, :] = g
            mo_ref[:, l, :] = m_new
            vo_ref[:, l, :] = v_new
            d_ref[:, l, :] = -ADAM_LR * ((m_new / c1) / (jnp.sqrt(v_new / c2) + ADAM_EPS) + ADAM_WD * w_ref[:, l, :])

    tile = pl.BlockSpec((SHARD_COLS, DEPTH, LANE), lambda i: (0, 0, i))
    return pl.pallas_call(
        body, name=name, grid=(D_MODEL // LANE,),
        in_specs=[pl.BlockSpec((DEPTH, N_DEV, SHARD_COLS, LANE), lambda i: (0, 0, 0, i)), tile, tile, tile],
        out_specs=[tile] * 4, out_shape=[jax.ShapeDtypeStruct(w.shape, F32)] * 4,
        compiler_params=_cparams(("parallel",)),
    )(parts, w, m, v)


def _pack_small(conv, small):
    lead = conv.shape[:-2]
    flat = jnp.concatenate([conv.reshape(lead + (CS_CONV,))] + list(small), axis=-1)
    pad = CS_ROWS * LANE - flat.shape[-1]
    flat = jnp.concatenate([flat, jnp.zeros(lead + (pad,), F32)], axis=-1)
    return flat.reshape(lead + (CS_ROWS, LANE))


def _unpack_small(p):
    flat = p.reshape(DEPTH, CS_ROWS * LANE)
    conv = flat[:, :CS_CONV].reshape(DEPTH, CONV_K, CONV_SHARD_COLS)
    small, off = [], CS_CONV
    for _, n in SMALL_SIZES:
        small.append(flat[:, off:off + n])
        off += n
    return conv, small


def kernel(x, w_in, conv_w, a_log, dt_bias, norm_w, sinks, w_out, ln_g, ln_b, loss_target, m_w_in, m_conv_w, m_a_log, m_dt_bias, m_norm_w, m_sinks, m_w_out, m_ln_g, m_ln_b, v_w_in, v_conv_w, v_a_log, v_dt_bias, v_norm_w, v_sinks, v_w_out, v_ln_g, v_ln_b):
    small = [a_log, dt_bias, norm_w, sinks, ln_g, ln_b]
    w_t, m_t, v_t = (a.transpose(2, 0, 1) for a in (w_in, m_w_in, v_w_in))
    shards = [[w_t[:, l].astype(BF16), w_out[l].astype(BF16), conv_w[l]] for l in range(DEPTH)]
    g_in0, = _all_gather(shards[0][:1], name="weights_all_gather_0")
    weights = [[_full_w_in(g_in0, "w_in_rows_0"), None, None]] + [[None, None, None]] * (DEPTH - 1)

    _, saved, weights = _forward(x[0], weights, shards, small)
    dx, g1, _, loss_lanes = _backward_layer(1, None, saved[1], weights[1], ln_g[1],
                                            loss=(loss_target[0], ln_b[1][None, :]))
    loss = lax.psum(0.5 * jnp.sum(loss_lanes) * (1.0 / D_MODEL), ("x", "y", "c"))
    dx, _, (p_in, p_out, p_small), _ = _backward_layer(0, dx, saved[0], weights[0], ln_g[0],
                                                       above=_contributions(g1))

    o_in = [o.transpose(1, 2, 0) for o in _adamw_w_in(p_in, w_t, m_t, v_t, name="adamw_w_in")]
    o_out = _adamw(p_out, w_out, m_w_out, v_w_out, tr=OUT_SHARD_ROWS, name="adamw_w_out")
    o_small = _adamw(p_small, _pack_small(conv_w, small),
                     _pack_small(m_conv_w, [m_a_log, m_dt_bias, m_norm_w, m_sinks, m_ln_g, m_ln_b]),
                     _pack_small(v_conv_w, [v_a_log, v_dt_bias, v_norm_w, v_sinks, v_ln_g, v_ln_b]),
                     tr=CS_ROWS, name="adamw_small")
    outs = []
    for k in range(4):
        cv, sm = _unpack_small(o_small[k])
        outs += [o_in[k], cv, sm[0], sm[1], sm[2], sm[3], o_out[k], sm[4], sm[5]]
    return (loss, dx[None], *outs)
```

```python
import functools

import jax
import jax.numpy as jnp
from jax import lax
from jax.experimental import pallas as pl
from jax.experimental.pallas import tpu as pltpu

F32 = jnp.float32
BF16 = jnp.bfloat16
MM_DTYPE = BF16

N_DEV = 8
D_MODEL = 1024
DEPTH = 2
A_HEADS = 4
A_HEAD_DIM = 128
A_WIDTH = 512
CONV_K = 4
SUPER = 256
NEWTON_STEPS = 1
B_Q_HEADS = 8
B_KV_HEADS = 2
B_HEAD_DIM = 64
B_GROUP = 4
B_WIDTH = 512
WINDOW = 128
BLOCK = 128
IN_COLS = 3336
SHARD_COLS = IN_COLS // N_DEV
OUT_SHARD_ROWS = D_MODEL // N_DEV
CONV_SHARD_COLS = 3 * A_WIDTH // N_DEV
DEEPNORM_ALPHA = (2 * DEPTH) ** 0.25
LN_EPS = 1e-5
RMS_EPS = 1e-6
L2_EPS = 1e-6
ADAM_LR, ADAM_B1, ADAM_B2, ADAM_EPS, ADAM_WD, ADAM_STEP = 0.001, 0.9, 0.999, 1e-08, 0.01, 10

LANE = 128
L_QB, L_ZB, L_KB, L_VB, L_ZA, L_BA, L_QKV = 0, 512, 1024, 1152, 1280, 1792, 1920
L_SWA = 1280
L_GATE = 640
L_COLS = 3456
SMALL_SIZES = (("a_log", 4), ("dt_bias", 4), ("norm_w", 128), ("sinks", 8), ("ln_g", 1024), ("ln_b", 1024))
CS_CONV = CONV_K * CONV_SHARD_COLS
CS_ROWS = 24
VMEM_LIMIT = 48 * 1024 * 1024


def _cparams(sem=None):
    return pltpu.CompilerParams(dimension_semantics=sem, vmem_limit_bytes=VMEM_LIMIT)


def _mm(a, b):
    return jnp.dot(a.astype(MM_DTYPE), b.astype(MM_DTYPE), preferred_element_type=F32)


def _mm_nt(a, b):
    return lax.dot_general(a.astype(MM_DTYPE), b.astype(MM_DTYPE), (((1,), (1,)), ((), ())),
                           preferred_element_type=F32)


def _mm_tn(a, b):
    return lax.dot_general(a.astype(MM_DTYPE), b.astype(MM_DTYPE), (((0,), (0,)), ((), ())),
                           preferred_element_type=F32)


def _split(a):
    hi = a.astype(BF16)
    return hi, (a - hi.astype(F32)).astype(BF16)


def _silu(x):
    return x * jax.nn.sigmoid(x)


@jax.custom_vjp
def _stack(parts):
    return jnp.stack(parts)


_stack.defvjp(lambda parts: (jnp.stack(parts), None), lambda _, g: (tuple(g[i] for i in range(g.shape[0])),))


def _softplus(x):
    return jnp.maximum(x, 0.0) + jnp.log1p(jnp.exp(-jnp.abs(x)))


_ANY = pl.BlockSpec(memory_space=pl.ANY)


def _me():
    return lax.axis_index("x"), lax.axis_index("y"), lax.axis_index("c")


def _flat_id(pos):
    return 4 * pos[0] + 2 * pos[1] + pos[2]


def _remote(src, dst, send_sem, recv_sem, to):
    return pltpu.make_async_remote_copy(src_ref=src, dst_ref=dst, send_sem=send_sem, recv_sem=recv_sem,
                                        device_id=to, device_id_type=pl.DeviceIdType.MESH)


class _Direct:
    def __init__(self, items, bufs):
        self.items, self.bufs = list(items), list(bufs)
        self.n_src, self.n_buf = len(self.items), len(self.bufs)
        self.old = [j for j, b in enumerate(self.bufs) if not isinstance(b, jax.ShapeDtypeStruct)]
        self.args = [it[0] for it in self.items] + [self.bufs[j] for j in self.old]
        self.out_shape = [jax.ShapeDtypeStruct(b.shape, b.dtype) for b in self.bufs]
        self.scratch = [pltpu.SemaphoreType.DMA((self.n_src, N_DEV - 1)),
                        pltpu.SemaphoreType.DMA((self.n_src, N_DEV - 1)), pltpu.SemaphoreType.DMA((self.n_src,))]

    def aliases(self, in_base, out_base):
        return {in_base + self.n_src + pos: out_base + j for pos, j in enumerate(self.old)}

    def copies(self, in_refs, out_refs, sems):
        send_sems, recv_sems, local_sems = sems
        x, y, c = _me()
        me = _flat_id((x, y, c))
        peers = [(x ^ ((rel >> 2) & 1), y ^ ((rel >> 1) & 1), c ^ (rel & 1)) for rel in range(1, N_DEV)]
        local, sends, recvs = [], [], []
        for a, (_, per_dest, j, prefix, *rest) in enumerate(self.items):
            src = lambda d: in_refs[a].at[d] if per_dest else in_refs[a]
            dst = lambda s: out_refs[j].at[tuple(prefix) + (s,) + tuple(rest[0] if rest else ())]
            local.append(pltpu.make_async_copy(src(me), dst(me), local_sems.at[a]))
            for k, peer in enumerate(peers):
                pid = _flat_id(peer)
                sends.append(_remote(src(pid), dst(me), send_sems.at[a, k], recv_sems.at[a, k], peer))
                recvs.append(_remote(src(pid), dst(pid), send_sems.at[a, k], recv_sems.at[a, k], peer))
        return local, sends, recvs

    def start(self, in_refs, out_refs, sems):
        local, sends, _ = self.copies(in_refs, out_refs, sems)
        for cp in local + sends:
            cp.start()

    def wait(self, in_refs, out_refs, sems):
        local, sends, recvs = self.copies(in_refs, out_refs, sems)
        for cp in recvs:
            cp.wait_recv()
        for cp in sends:
            cp.wait_send()
        for cp in local:
            cp.wait()


def _pcall(core, *, name, grid, in_specs, out_specs, out_shape, args, sem, scratch_shapes=(), aliases=None,
           rider=None):
    n_in, n_out, n_scr = len(in_specs), len(out_specs), len(scratch_shapes)
    n_rin, n_rout = (len(rider.args), rider.n_buf) if rider else (0, 0)

    def body(*refs):
        ins, outs = refs[:n_in], refs[n_in + n_rin:n_in + n_rin + n_out]
        scr = refs[n_in + n_rin + n_out + n_rout:n_in + n_rin + n_out + n_rout + n_scr]
        if rider:
            r_refs = (refs[n_in:n_in + rider.n_src], refs[n_in + n_rin + n_out:n_in + n_rin + n_out + n_rout],
                      refs[n_in + n_rin + n_out + n_rout + n_scr:])
            ids = [pl.program_id(d) for d in range(len(grid))]
            first = functools.reduce(lambda p, q: p & q, [i == 0 for i in ids])
            last = functools.reduce(lambda p, q: p & q, [i == g - 1 for i, g in zip(ids, grid)])
            pl.when(first)(lambda: rider.start(*r_refs))
        core(ins, outs, scr)
        if rider:
            pl.when(last)(lambda: rider.wait(*r_refs))

    aliases = dict(aliases or {})
    if rider:
        sem = ("arbitrary",) * len(grid)
        aliases.update(rider.aliases(n_in, n_out))
    return pl.pallas_call(
        body, name=name, grid=grid, in_specs=list(in_specs) + [_ANY] * n_rin,
        out_specs=list(out_specs) + [_ANY] * n_rout,
        out_shape=list(out_shape) + (rider.out_shape if rider else []),
        scratch_shapes=list(scratch_shapes) + (rider.scratch if rider else []),
        input_output_aliases=aliases, compiler_params=_cparams(sem),
    )(*args, *(rider.args if rider else []))


def _exchange(direct, *, name):
    n_in = len(direct.args)

    def body(*refs):
        r_refs = refs[:direct.n_src], refs[n_in:n_in + direct.n_buf], refs[n_in + direct.n_buf:]
        direct.start(*r_refs)
        direct.wait(*r_refs)

    return pl.pallas_call(
        body, name=name, in_specs=[_ANY] * n_in, out_specs=[_ANY] * direct.n_buf, out_shape=direct.out_shape,
        input_output_aliases=direct.aliases(0, 0), scratch_shapes=direct.scratch,
    )(*direct.args)


def _matmul(a, b, *, form, tm, tn, tk, name, add=None, add_scale=1.0, rider=None, b_cols=None):
    if form == "nn":
        (m, kk), n = a.shape, b.shape[1]
        a_spec = pl.BlockSpec((tm, tk), lambda i, j, k: (i, k))
        b_spec = pl.BlockSpec((tk, tn), lambda i, j, k: (k, j))
        dn = (((1,), (0,)), ((), ()))
    elif form == "nt":
        (m, kk), n = a.shape, b.shape[0]
        a_spec = pl.BlockSpec((tm, tk), lambda i, j, k: (i, k))
        b_spec = pl.BlockSpec((tn, tk), lambda i, j, k: (j, k))
        dn = (((1,), (1,)), ((), ()))
    else:
        kk, m = a.shape
        n0, n = b_cols or (0, b.shape[1])
        assert n0 % tn == 0
        a_spec = pl.BlockSpec((tk, tm), lambda i, j, k: (k, i))
        b_spec = pl.BlockSpec((tk, tn), lambda i, j, k: (k, j + n0 // tn))
        dn = (((0,), (0,)), ((), ()))
    assert m % tm == 0 and n % tn == 0 and kk % tk == 0, (name, m, n, kk)
    has_add = add is not None

    def core(ins, outs, _):
        a_ref, b_ref = ins[:2]
        o_ref = outs[0]
        k = pl.program_id(2)
        p = lax.dot_general(a_ref[...].astype(MM_DTYPE), b_ref[...].astype(MM_DTYPE), dn,
                            preferred_element_type=F32)

        @pl.when(k == 0)
        def _():
            o_ref[...] = p + add_scale * ins[2][...] if has_add else p

        @pl.when(k > 0)
        def _():
            o_ref[...] += p

    in_specs = [a_spec, b_spec]
    args = [a, b]
    if has_add:
        in_specs.append(pl.BlockSpec((tm, tn), lambda i, j, k: (i, j)))
        args.append(add)
    res = _pcall(core, name=name, grid=(m // tm, n // tn, kk // tk), in_specs=in_specs,
                 out_specs=[pl.BlockSpec((tm, tn), lambda i, j, k: (i, j))],
                 out_shape=[jax.ShapeDtypeStruct((m, n), F32)], args=args,
                 sem=("parallel", "parallel", "arbitrary"), rider=rider)
    return res if rider else res[0]


ZERO_TAIL = 8


def _with_tail(x):
    return jnp.concatenate([x, jnp.zeros((ZERO_TAIL,) + x.shape[1:], x.dtype)], axis=0)


def _shift_down(x, k):
    return pltpu.roll(x, k, 0)


def _shift_up(x, k):
    return pltpu.roll(x, x.shape[0] - k, 0)


def _conv_slab(x, w):
    return w[3:4] * x + w[2:3] * _shift_down(x, 1) + w[1:2] * _shift_down(x, 2) + w[0:1] * _shift_down(x, 3)


def _prep_fwd(h, conv_w, *, name):
    t_len = h.shape[0]

    def body(x_ref, w_ref, o_ref):
        s = pl.program_id(0)
        y = _silu(_conv_slab(_with_tail(x_ref[...]), w_ref[...])[:t_len])
        rs = lax.rsqrt(jnp.sum(y * y, axis=-1, keepdims=True) + L2_EPS)
        scale = jnp.where(s < A_HEADS, A_HEAD_DIM ** -0.5, 1.0)
        o_ref[...] = jnp.where(s < 2 * A_HEADS, y * rs * scale, y)

    return pl.pallas_call(
        body, name=name, grid=(12,),
        in_specs=[pl.BlockSpec((t_len, LANE), lambda s: (0, L_QKV // LANE + s)),
                  pl.BlockSpec((8, LANE), lambda s: (0, s))],
        out_specs=pl.BlockSpec((t_len, LANE), lambda s: (0, s)),
        out_shape=jax.ShapeDtypeStruct((t_len, 3 * A_WIDTH), F32),
        compiler_params=_cparams(("parallel",)),
    )(h, conv_w)


def _prep_bwd(h, conv_w, d_out, dh, *, name):
    t_len = h.shape[0]

    def body(x_ref, w_ref, g_ref, dh_in, dx_ref, dw_ref):
        del dh_in
        s = pl.program_id(0)
        x = _with_tail(x_ref[...])
        g = _with_tail(g_ref[0])
        w = w_ref[...]
        xs = [_shift_down(x, 3), _shift_down(x, 2), _shift_down(x, 1), x]
        c = w[0:1] * xs[0] + w[1:2] * xs[1] + w[2:3] * xs[2] + w[3:4] * xs[3]
        sg = jax.nn.sigmoid(c)
        y = c * sg
        rs = lax.rsqrt(jnp.sum(y * y, axis=-1, keepdims=True) + L2_EPS)
        scale = jnp.where(s < A_HEADS, A_HEAD_DIM ** -0.5, 1.0)
        dy_n = scale * (rs * g - y * (rs * rs * rs) * jnp.sum(g * y, axis=-1, keepdims=True))
        dy = jnp.where(s < 2 * A_HEADS, dy_n, g)
        dc = dy * (sg * (1.0 + c * (1.0 - sg)))
        dx = w[3:4] * dc + w[2:3] * _shift_up(dc, 1) + w[1:2] * _shift_up(dc, 2) + w[0:1] * _shift_up(dc, 3)
        dx_ref[...] = dx[:t_len].astype(dx_ref.dtype)
        dws = [jnp.sum(dc * xs[j], axis=0, keepdims=True) for j in range(CONV_K)]
        dw_ref[...] = jnp.concatenate(dws + [jnp.zeros((8 - CONV_K, LANE), F32)], axis=0)

    slab = pl.BlockSpec((t_len, LANE), lambda s: (0, L_QKV // LANE + s))
    return pl.pallas_call(
        body, name=name, grid=(12,),
        in_specs=[slab, pl.BlockSpec((8, LANE), lambda s: (0, s)),
                  pl.BlockSpec((1, t_len, LANE), lambda s: (s // A_HEADS, 0, s % A_HEADS)), _ANY],
        out_specs=[slab, pl.BlockSpec((8, LANE), lambda s: (0, s))],
        out_shape=[jax.ShapeDtypeStruct((t_len, L_COLS), MM_DTYPE), jax.ShapeDtypeStruct((8, 3 * A_WIDTH), F32)],
        input_output_aliases={3: 0},
        compiler_params=_cparams(("parallel",)),
    )(h, conv_w, d_out, dh)


N_LEVELS = 5
MF_TRIL, MF_STRIL, MF_DIAG8, MF_LOW16, MF_EYE = 0, 1, 2, 3, 3 + N_LEVELS
MB_CUM, MB_CUM_T, MB_TOT = 0, 1, 2


def _gdn_masks():
    r = lax.broadcasted_iota(jnp.int32, (SUPER, SUPER), 0)
    c = lax.broadcasted_iota(jnp.int32, (SUPER, SUPER), 1)
    same = lambda shift: (r >> shift) == (c >> shift)
    ninf = lambda m: jnp.where(m, 0.0, -jnp.inf).astype(F32)
    one = lambda m: m.astype(F32)
    mf = jnp.stack([ninf(r >= c), ninf(r > c), one(same(3))]
                   + [one(same(4 + lv) & jnp.logical_not(same(3 + lv))) for lv in range(N_LEVELS)] + [one(r == c)])
    mb = jnp.stack([one(r >= c), one(r <= c), jnp.ones((SUPER, SUPER), F32)]).astype(BF16)
    return mf, mb


def _tri_inv_impl(a, mf):
    d = lambda p, q: jnp.dot(p.astype(BF16), q.astype(BF16), preferred_element_type=F32)
    dd = lambda p, q: jnp.dot(p, q, preferred_element_type=F32)
    eye = mf[MF_EYE]
    a0 = a * mf[MF_DIAG8]
    a2 = d(a0, a0)
    a4 = d(a2, a2)
    t = d(d(eye - a0, eye + a2), eye + a4)
    for level in range(N_LEVELS):
        t = t - d(d(t, a * mf[MF_LOW16 + level]), t)
    a_hi, a_lo = _split(a)
    for _ in range(NEWTON_STEPS):
        t0 = t.astype(BF16)
        t0f = t0.astype(F32)
        resid = (eye - t0f) - (dd(a_hi, t0) + dd(a_lo, t0))
        r_hi, r_lo = _split(resid)
        t = t0f + (dd(t0, r_hi) + dd(t0, r_lo))
    return t


@jax.custom_vjp
def _wy_apply(a, rhs, t):
    return _mm(t, rhs)


def _wy_apply_fwd(a, rhs, t):
    x = _mm(t, rhs)
    return x, (t, x)


def _wy_apply_bwd(res, dx):
    t, x = res
    d_rhs = _mm_tn(t, dx)
    return -_mm_nt(d_rhs, x), d_rhs, jnp.zeros_like(t)


_wy_apply.defvjp(_wy_apply_fwd, _wy_apply_bwd)


@functools.partial(jax.custom_vjp, nondiff_argnums=(1,))
def _lane_roll(x, shift):
    return pltpu.roll(x, shift % LANE, 1)


_lane_roll.defvjp(lambda x, shift: (_lane_roll(x, shift), None), lambda shift, _, g: (_lane_roll(g, -shift),))


def _mask_times_lanes(x, mask):
    lane = lax.broadcasted_iota(jnp.int32, (1, LANE), 1)
    x = jnp.where(lane < A_HEADS, x, 0.0)
    x1 = x.astype(BF16).astype(F32)
    x2 = (x - x1).astype(BF16).astype(F32)
    x3 = (x - x1 - x2).astype(BF16).astype(F32)
    pieces = x1 + pltpu.roll(x2, A_HEADS, 1) + pltpu.roll(x3, 2 * A_HEADS, 1)
    res = jnp.dot(mask, pieces.astype(BF16), preferred_element_type=F32)
    return res + pltpu.roll(res, LANE - A_HEADS, 1) + pltpu.roll(res, LANE - 2 * A_HEADS, 1)


@jax.custom_vjp
def _chunk_sums(g, mb):
    return _mask_times_lanes(g, mb[MB_CUM]), _mask_times_lanes(g, mb[MB_TOT])


def _chunk_sums_fwd(g, mb):
    return _chunk_sums(g, mb), mb


def _chunk_sums_bwd(mb, d):
    lane = lax.broadcasted_iota(jnp.int32, (1, LANE), 1)
    dg = _mask_times_lanes(d[0], mb[MB_CUM_T]) + _mask_times_lanes(d[1], mb[MB_TOT])
    return jnp.where(lane < A_HEADS, dg, 0.0), jnp.zeros_like(mb)


_chunk_sums.defvjp(_chunk_sums_fwd, _chunk_sums_bwd)


def _gdn_gates(ba, alog, dtb, mb):
    beta = jax.nn.sigmoid(ba)
    g = -jnp.exp(alog) * _softplus(_lane_roll(ba, -A_HEADS) + dtb)
    gc, gl = _chunk_sums(g, mb)
    return beta, gc, gl, gc.T


def _gdn_block(s, q, k, v, z, gates, nw, h, t_known, mf):
    n = q.shape[0]
    beta_all, gc_all, gl_all, gct_all = gates
    lane = lax.broadcasted_iota(jnp.int32, (1, LANE), 1)
    sub = lax.broadcasted_iota(jnp.int32, (LANE, 1), 0)
    col = lambda x: jnp.sum(jnp.where(lane == h, x, 0.0), axis=1, keepdims=True)
    wide = lambda c: jnp.broadcast_to(c, (n, LANE))
    gc, gl = col(gc_all), col(gl_all)
    gc_row = jnp.sum(jnp.where(sub == h, gct_all, 0.0), axis=0, keepdims=True)
    beta_w, eg_w = wide(col(beta_all)), wide(jnp.exp(gc))
    diff = gc - gc_row
    decay = jnp.exp(diff + mf[MF_TRIL])
    kb = k * beta_w
    a_mat = _mm_nt(kb, k) * jnp.exp(diff + mf[MF_STRIL])
    rhs = jnp.concatenate([v * beta_w, kb * eg_w], axis=1)
    if t_known is None:
        t_mat = _tri_inv_impl(a_mat, mf)
        uw = _mm(t_mat, rhs)
    else:
        t_mat = t_known
        uw = _wy_apply(a_mat, rhs, t_known)
    u, w = uw[:, :LANE], uw[:, LANE:]
    qk = _mm_nt(q, k) * decay
    q_dec = q * eg_w
    k_dec = k * wide(jnp.exp(gl - gc))
    v_new = u - _mm(w, s)
    o = _mm(q_dec, s) + _mm(qk, v_new)
    s = s * jnp.exp(gl[0:1]) + _mm_tn(k_dec, v_new)
    o = o * lax.rsqrt(jnp.mean(o * o, axis=-1, keepdims=True) + RMS_EPS) * nw
    return o * _silu(z), s, t_mat


def _gdn_fwd(qkv, h, alog, dtb, nw, ycat, *, name, rider=None):
    t_len = qkv.shape[0]
    nsc = t_len // SUPER

    def core(ins, outs, scr):
        q_ref, k_ref, v_ref, gate_ref, al_ref, dt_ref, nw_ref, mf_ref, mb_ref, _ = ins
        y_ref, sin_ref, t_ref = outs
        s_scr, = scr

        @pl.when(pl.program_id(0) == 0)
        def _():
            s_scr[...] = jnp.zeros_like(s_scr)

        per_head = lambda ref: jnp.stack([ref[:, hh * LANE:(hh + 1) * LANE] for hh in range(A_HEADS)])
        states = s_scr[...]
        gates = _gdn_gates(gate_ref[:, A_WIDTH:], al_ref[...], dt_ref[...], mb_ref[...])
        fn = jax.vmap(_gdn_block, in_axes=(0, 0, 0, 0, 0, None, None, 0, None, None))
        y, s_new, t_mat = fn(states, per_head(q_ref), per_head(k_ref), per_head(v_ref), per_head(gate_ref),
                             gates, nw_ref[...], jnp.arange(A_HEADS), None, mf_ref[...])
        sin_ref[0] = states
        t_ref[0] = t_mat
        s_scr[...] = s_new
        for hh in range(A_HEADS):
            y_ref[:, hh * LANE:(hh + 1) * LANE] = y[hh]

    blk = lambda j: pl.BlockSpec((SUPER, A_WIDTH), lambda sc: (sc, j))
    row = pl.BlockSpec((1, LANE), lambda sc: (0, 0))
    mf, mb = _gdn_masks()
    whole = lambda a: pl.BlockSpec(a.shape, lambda sc: (0, 0, 0))
    return _pcall(
        core, name=name, grid=(nsc,),
        in_specs=[blk(0), blk(1), blk(2), pl.BlockSpec((SUPER, L_GATE), lambda sc: (sc, L_ZA // L_GATE)),
                  row, row, row, whole(mf), whole(mb), _ANY],
        out_specs=[blk(0),
                   pl.BlockSpec((1, A_HEADS, A_HEAD_DIM, A_HEAD_DIM), lambda sc: (sc, 0, 0, 0)),
                   pl.BlockSpec((1, A_HEADS, SUPER, SUPER), lambda sc: (sc, 0, 0, 0))],
        out_shape=[jax.ShapeDtypeStruct((t_len, D_MODEL), F32),
                   jax.ShapeDtypeStruct((nsc, A_HEADS, A_HEAD_DIM, A_HEAD_DIM), F32),
                   jax.ShapeDtypeStruct((nsc, A_HEADS, SUPER, SUPER), F32)],
        scratch_shapes=[pltpu.VMEM((A_HEADS, A_HEAD_DIM, A_HEAD_DIM), F32)],
        aliases={9: 0}, sem=("arbitrary",), rider=rider,
        args=(qkv, qkv, qkv, h, alog, dtb, nw, mf, mb, ycat))


def _gdn_bwd(qkv, h, alog, dtb, nw, s_in, t_in, dycat, dh, *, name, rider=None):
    t_len = qkv.shape[0]
    nsc = t_len // SUPER

    def core(ins, outs, scr):
        q_ref, k_ref, v_ref, gate_ref, al_ref, dt_ref, nw_ref, sin_ref, t_ref, dy_ref, mf_ref, mb_ref, _ = ins
        dgate_ref, dqkv_ref, dal_ref, ddt_ref, dnw_ref = outs
        ds_scr, = scr

        @pl.when(pl.program_id(0) == 0)
        def _():
            ds_scr[...] = jnp.zeros_like(ds_scr)
            dal_ref[...] = jnp.zeros_like(dal_ref)
            ddt_ref[...] = jnp.zeros_like(ddt_ref)
            dnw_ref[...] = jnp.zeros_like(dnw_ref)

        per_head = lambda ref: jnp.stack([ref[:, hh * LANE:(hh + 1) * LANE] for hh in range(A_HEADS)])
        head_ids = jnp.arange(A_HEADS)
        t_known, mf, mb = t_ref[0], mf_ref[...], mb_ref[...]

        def fn(s, q, k, v, z, ba, alog, dtb, nw):
            gates = _gdn_gates(ba, alog, dtb, mb)
            one = lambda s, q, k, v, z, t, h: _gdn_block(s, q, k, v, z, gates, nw, h, t, mf)[:2]
            return jax.vmap(one)(s, q, k, v, z, t_known, head_ids)

        _, vjp = jax.vjp(fn, sin_ref[0], per_head(q_ref), per_head(k_ref), per_head(v_ref), per_head(gate_ref),
                         gate_ref[:, A_WIDTH:], al_ref[...], dt_ref[...], nw_ref[...])
        ds, dq, dk, dv, dz, dba, dal, ddt, dnw = vjp((per_head(dy_ref), ds_scr[...]))
        ds_scr[...] = ds
        for hh in range(A_HEADS):
            cols = slice(hh * LANE, (hh + 1) * LANE)
            dqkv_ref[0, :, cols] = dq[hh]
            dqkv_ref[1, :, cols] = dk[hh]
            dqkv_ref[2, :, cols] = dv[hh]
            dgate_ref[:, cols] = dz[hh].astype(dgate_ref.dtype)
        dgate_ref[:, A_WIDTH:] = dba.astype(dgate_ref.dtype)
        dal_ref[...] += dal
        ddt_ref[...] += ddt
        dnw_ref[...] += dnw

    rev = lambda i: nsc - 1 - i
    blk = lambda j: pl.BlockSpec((SUPER, A_WIDTH), lambda i: (rev(i), j))
    gate = pl.BlockSpec((SUPER, L_GATE), lambda i: (rev(i), L_ZA // L_GATE))
    row = pl.BlockSpec((1, LANE), lambda i: (0, 0))
    mf, mb = _gdn_masks()
    whole = lambda a: pl.BlockSpec(a.shape, lambda i: (0, 0, 0))
    return _pcall(
        core, name=name, grid=(nsc,),
        in_specs=[blk(0), blk(1), blk(2), gate, row, row, row,
                  pl.BlockSpec((1, A_HEADS, A_HEAD_DIM, A_HEAD_DIM), lambda i: (rev(i), 0, 0, 0)),
                  pl.BlockSpec((1, A_HEADS, SUPER, SUPER), lambda i: (rev(i), 0, 0, 0)),
                  blk(0), whole(mf), whole(mb), _ANY],
        out_specs=[gate, pl.BlockSpec((3, SUPER, A_WIDTH), lambda i: (0, rev(i), 0)), row, row, row],
        out_shape=[jax.ShapeDtypeStruct((t_len, L_COLS), MM_DTYPE), jax.ShapeDtypeStruct((3, t_len, A_WIDTH), F32)]
        + [jax.ShapeDtypeStruct((1, LANE), F32)] * 3,
        scratch_shapes=[pltpu.VMEM((A_HEADS, A_HEAD_DIM, A_HEAD_DIM), F32)],
        aliases={12: 0}, sem=("arbitrary",), rider=rider,
        args=(qkv, qkv, qkv, h, alog, dtb, nw, s_in, t_in, dycat, mf, mb, dh))


Q_BLOCKS = 4
Q_ROWS = Q_BLOCKS * BLOCK


def _swa_block(q, kp, kc, vp, vc, z, sinks, first):
    rows = B_GROUP * BLOCK
    ri = lax.broadcasted_iota(jnp.int32, (rows, 2 * BLOCK), 0)
    si = lax.broadcasted_iota(jnp.int32, (rows, 2 * BLOCK), 1)
    dist = (ri & (BLOCK - 1)) + BLOCK - si
    bias = jnp.where((dist >= 0) & (dist < WINDOW), 0.0, -jnp.inf)
    no_prev = jnp.where(first & (si[:1] < BLOCK), -jnp.inf, 0.0)
    dist_f = dist.astype(F32)
    head_of_row = lax.broadcasted_iota(jnp.int32, (rows, 1), 0) >> 7
    keys = jnp.concatenate([kp, kc], axis=0)
    vals = jnp.concatenate([vp, vc], axis=0)

    def item(b, j):
        cs = slice(j * B_HEAD_DIM, (j + 1) * B_HEAD_DIM)
        rs = slice(b * BLOCK, (b + 1) * BLOCK)
        heads = range(j * B_GROUP, (j + 1) * B_GROUP)
        qs = jnp.concatenate([q[rs, hq * B_HEAD_DIM:(hq + 1) * B_HEAD_DIM] for hq in heads], axis=0) * (
            B_HEAD_DIM ** -0.5)
        kk = keys[b * BLOCK:(b + 2) * BLOCK, cs]
        vv = vals[b * BLOCK:(b + 2) * BLOCK, cs]
        sink = jnp.concatenate([jnp.broadcast_to(sinks[:, hq:hq + 1], (BLOCK, 1)) for hq in heads], axis=0)
        slope = sum(jnp.where(head_of_row == gi, 2.0 ** (-8.0 * (hq + 1) / B_Q_HEADS), 0.0)
                    for gi, hq in enumerate(heads))
        return qs, kk, vv, sink, slope, (no_prev if b == 0 else jnp.zeros_like(no_prev))

    def attend(qs, kk, vv, sink, slope, hide):
        sc = _mm_nt(qs, kk) - slope * dist_f + (bias + hide)
        m = lax.stop_gradient(jnp.maximum(jnp.max(sc, axis=-1, keepdims=True), sink))
        p = jnp.exp(sc - m)
        inv = 1.0 / (jnp.sum(p, axis=-1, keepdims=True) + jnp.exp(sink - m))
        return _mm(p * inv, vv)

    items = [(b, j) for b in range(Q_BLOCKS) for j in range(B_KV_HEADS)]
    o = jax.vmap(attend)(*[_stack(t) for t in zip(*[item(b, j) for b, j in items])])
    rows_out = [jnp.concatenate([o[b * B_KV_HEADS + j, gi * BLOCK:(gi + 1) * BLOCK]
                                 for j in range(B_KV_HEADS) for gi in range(B_GROUP)], axis=1)
                for b in range(Q_BLOCKS)]
    return jnp.concatenate(rows_out, axis=0) * _silu(z)


def _swa_specs(idx):
    wide = lambda off: pl.BlockSpec((Q_ROWS, B_WIDTH), lambda n: (idx(n), off))
    cur = lambda off: pl.BlockSpec((Q_ROWS, LANE), lambda n: (idx(n), off))
    prev = lambda off: pl.BlockSpec((BLOCK, LANE), lambda n: (jnp.maximum(idx(n) * Q_BLOCKS - 1, 0), off))
    return [wide(L_QB // B_WIDTH), prev(L_KB // LANE), cur(L_KB // LANE), prev(L_VB // LANE), cur(L_VB // LANE),
            wide(L_ZB // B_WIDTH), pl.BlockSpec((1, LANE), lambda n: (0, 0))]


def _swa_fwd(h, sinks, *, name, rider=None):
    t_len = h.shape[0]
    nb = t_len // Q_ROWS

    def core(ins, outs, _):
        q_ref, kp_ref, kc_ref, vp_ref, vc_ref, z_ref, s_ref = ins
        outs[0][...] = _swa_block(q_ref[...], kp_ref[...], kc_ref[...], vp_ref[...], vc_ref[...], z_ref[...],
                                  s_ref[...], pl.program_id(0) == 0)

    res = _pcall(core, name=name, grid=(nb,), in_specs=_swa_specs(lambda n: n),
                 out_specs=[pl.BlockSpec((Q_ROWS, B_WIDTH), lambda n: (n, 1))],
                 out_shape=[jax.ShapeDtypeStruct((t_len, D_MODEL), F32)], sem=("parallel",), rider=rider,
                 args=(h, h, h, h, h, h, sinks))
    return res if rider else res[0]


def _swa_bwd(h, sinks, dycat, *, name, rider=None):
    t_len = h.shape[0]
    nb = t_len // Q_ROWS
    early = slice(0, Q_ROWS - BLOCK)
    last = slice(Q_ROWS - BLOCK, Q_ROWS)

    def core(ins, outs, scr):
        q_ref, kp_ref, kc_ref, vp_ref, vc_ref, z_ref, s_ref, dy_ref = ins
        dh_ref, dsk_ref = outs
        ck_scr, cv_scr = scr
        i = pl.program_id(0)
        n = nb - 1 - i

        @pl.when(i == 0)
        def _():
            ck_scr[...] = jnp.zeros_like(ck_scr)
            cv_scr[...] = jnp.zeros_like(cv_scr)
            dsk_ref[...] = jnp.zeros_like(dsk_ref)

        fn = functools.partial(_swa_block, first=(n == 0))
        _, vjp = jax.vjp(fn, q_ref[...], kp_ref[...], kc_ref[...], vp_ref[...], vc_ref[...], z_ref[...], s_ref[...])
        dq, dkp, dkc, dvp, dvc, dz, dsk = vjp(dy_ref[...])
        def put(rows, col, val):
            dh_ref[rows, col:col + val.shape[1]] = val.astype(dh_ref.dtype)

        put(slice(None), L_QB, dq)
        put(slice(None), L_ZB, dz)
        put(early, L_KB, dkc[early])
        put(early, L_VB, dvc[early])
        put(last, L_KB, dkc[last] + ck_scr[...])
        put(last, L_VB, dvc[last] + cv_scr[...])
        ck_scr[...] = dkp
        cv_scr[...] = dvp
        dsk_ref[...] += dsk

    rev = lambda i: nb - 1 - i
    return _pcall(
        core, name=name, grid=(nb,),
        in_specs=_swa_specs(rev) + [pl.BlockSpec((Q_ROWS, B_WIDTH), lambda i: (rev(i), 1))],
        out_specs=[pl.BlockSpec((Q_ROWS, L_SWA), lambda i: (rev(i), 0)), pl.BlockSpec((1, LANE), lambda i: (0, 0))],
        out_shape=[jax.ShapeDtypeStruct((t_len, L_COLS), MM_DTYPE), jax.ShapeDtypeStruct((1, LANE), F32)],
        scratch_shapes=[pltpu.VMEM((BLOCK, LANE), F32), pltpu.VMEM((BLOCK, LANE), F32)],
        sem=("arbitrary",), rider=rider, args=(h, h, h, h, h, h, sinks, dycat))


def _out_ln_fwd(ycat, w_out, x, ln_g, ln_b, *, name, tm=512, last=False):
    t_len = x.shape[0]

    def body(y_ref, w_ref, x_ref, g_ref, b_ref, r_ref, *o_ref):
        r = DEEPNORM_ALPHA * x_ref[...] + _mm(y_ref[...], w_ref[...])
        r_ref[...] = r
        if not last:
            mu = jnp.mean(r, axis=-1, keepdims=True)
            d = r - mu
            var = jnp.mean(d * d, axis=-1, keepdims=True)
            o_ref[0][...] = d * lax.rsqrt(var + LN_EPS) * g_ref[...] + b_ref[...]

    tile = pl.BlockSpec((tm, D_MODEL), lambda i: (i, 0))
    vec = pl.BlockSpec((1, D_MODEL), lambda i: (0, 0))
    n_out = 1 if last else 2
    res = pl.pallas_call(
        body, name=name, grid=(t_len // tm,),
        in_specs=[tile, pl.BlockSpec((D_MODEL, D_MODEL), lambda i: (0, 0)), tile, vec, vec],
        out_specs=[tile] * n_out,
        out_shape=[jax.ShapeDtypeStruct((t_len, D_MODEL), F32)] * n_out,
        compiler_params=_cparams(("parallel",)),
    )(ycat, w_out, x, ln_g, ln_b)
    return (res[0], None) if last else res


def _ln_bwd(dxn, r, ln_g, *, name, tm=512, loss=None):
    t_len = r.shape[0]

    def body(*refs):
        if loss:
            t_ref, r_ref, g_ref, b_ref, dr_ref, dg_ref, db_ref, l_ref = refs
        else:
            dx_ref, r_ref, g_ref, dr_ref, dg_ref, db_ref = refs

        @pl.when(pl.program_id(0) == 0)
        def _():
            dg_ref[...] = jnp.zeros_like(dg_ref)
            db_ref[...] = jnp.zeros_like(db_ref)
            if loss:
                l_ref[...] = jnp.zeros_like(l_ref)

        rr = r_ref[...]
        mu = jnp.mean(rr, axis=-1, keepdims=True)
        d = rr - mu
        rstd = lax.rsqrt(jnp.mean(d * d, axis=-1, keepdims=True) + LN_EPS)
        xh = d * rstd
        if loss:
            e = (xh * g_ref[...] + b_ref[...]) - t_ref[...]
            dx = e * (1.0 / D_MODEL)
            l_ref[...] += jnp.sum(e * e, axis=0, keepdims=True)
        else:
            dx = dx_ref[...]
        dxh = dx * g_ref[...]
        dr_ref[...] = rstd * (dxh - jnp.mean(dxh, axis=-1, keepdims=True)
                              - xh * jnp.mean(dxh * xh, axis=-1, keepdims=True))
        dg_ref[...] += jnp.sum(dx * xh, axis=0, keepdims=True)
        db_ref[...] += jnp.sum(dx, axis=0, keepdims=True)

    tile = pl.BlockSpec((tm, D_MODEL), lambda i: (i, 0))
    vec = pl.BlockSpec((1, D_MODEL), lambda i: (0, 0))
    vec_shape = jax.ShapeDtypeStruct((1, D_MODEL), F32)
    args = (loss[0], r, ln_g, loss[1]) if loss else (dxn, r, ln_g)
    return pl.pallas_call(
        body, name=name, grid=(t_len // tm,),
        in_specs=[tile, tile, vec] + ([vec] if loss else []), out_specs=[tile, vec, vec] + ([vec] if loss else []),
        out_shape=[jax.ShapeDtypeStruct((t_len, D_MODEL), F32), vec_shape, vec_shape] + ([vec_shape] if loss else []),
        compiler_params=_cparams(("arbitrary",)),
    )(*args)


def _pad_row(v):
    return jnp.zeros((1, LANE), F32).at[0, :v.shape[0]].set(v)


_REGIONS = ((0, 1536, L_QKV), (1536, 2048, L_ZA), (2048, 2056, L_BA), (2056, 2568, L_QB), (2568, 2696, L_KB),
            (2696, 2824, L_VB), (2824, 3336, L_ZB))


def _shard_pieces(regions):
    for a, b, off in regions:
        for d in range(N_DEV):
            lo, hi = max(a, d * SHARD_COLS), min(b, (d + 1) * SHARD_COLS)
            if lo < hi:
                yield d, lo - d * SHARD_COLS, hi - d * SHARD_COLS, off + lo - a


def _as_list(r):
    return list(r) if isinstance(r, (list, tuple)) else [r]


def _gathered(shard):
    return jax.ShapeDtypeStruct((N_DEV,) + shard.shape, shard.dtype)


def _full_w_in(g_in, name):
    by_offset = sorted(_shard_pieces(_REGIONS), key=lambda p: p[3])
    tc = 256

    def body(g_ref, o_ref):
        pieces, row = [], 0
        for d, lo, hi, off in by_offset + [(None, 0, 0, L_COLS)]:
            if off > row:
                pieces.append(jnp.zeros((off - row, tc), g_ref.dtype))
            if d is not None:
                pieces.append(g_ref[d, lo:hi, :])
            row = off + hi - lo
        o_ref[...] = jnp.concatenate(pieces, axis=0)

    return pl.pallas_call(
        body, name=name, grid=(D_MODEL // tc,),
        in_specs=[pl.BlockSpec((N_DEV, SHARD_COLS, tc), lambda i: (0, 0, i))],
        out_specs=pl.BlockSpec((L_COLS, tc), lambda i: (0, i)),
        out_shape=jax.ShapeDtypeStruct((L_COLS, D_MODEL), g_in.dtype),
        compiler_params=_cparams(("parallel",)),
    )(g_in)


def _full_conv(g_conv):
    return jnp.pad(g_conv.transpose(1, 0, 2).reshape(CONV_K, 3 * A_WIDTH), ((0, 8 - CONV_K), (0, 0)))


def _forward(x, weights, shards, small):
    a_log, dt_bias, norm_w, sinks, ln_g, ln_b = small
    tm = min(512, x.shape[0])
    saved, weights = [], [list(w) for w in weights]
    whole = lambda arrs: _Direct([(a, False, j, ()) for j, a in enumerate(arrs)], [_gathered(a) for a in arrs])
    for l in range(DEPTH):
        rider = whole(shards[l][1:]) if weights[l][1] is None else None
        h, *got = _as_list(_matmul(x, weights[l][0], form="nt", tm=tm, tn=L_COLS, tk=D_MODEL, name=f"in_proj_{l}",
                                   rider=rider))
        if rider:
            weights[l][1:] = [got[0].reshape(D_MODEL, D_MODEL), _full_conv(got[1])]
        w_in_l, w_out_l, conv_l = weights[l]
        qkv = _prep_fwd(h, conv_l, name=f"prep_fwd_{l}")
        al, dt, nw, sk = _pad_row(a_log[l]), _pad_row(dt_bias[l]), norm_w[l][None, :], _pad_row(sinks[l])
        ahead = l + 1 < DEPTH and weights[l + 1][0] is None
        rider = whole(shards[l + 1][1:]) if ahead else None
        ycat, *got = _as_list(_swa_fwd(h, sk, name=f"swa_fwd_{l}", rider=rider))
        if ahead:
            weights[l + 1][1:] = [got[0].reshape(D_MODEL, D_MODEL), _full_conv(got[1])]
        rider = whole(shards[l + 1][:1]) if ahead else None
        ycat, s_in, t_in, *got = _gdn_fwd(qkv, h, al, dt, nw, ycat, name=f"gdn_fwd_{l}", rider=rider)
        if ahead:
            weights[l + 1][0] = _full_w_in(got[0], f"w_in_rows_{l + 1}")
        r, xn = _out_ln_fwd(ycat, w_out_l, x, ln_g[l][None, :], ln_b[l][None, :], name=f"out_ln_{l}",
                            last=(l == DEPTH - 1))
        saved.append((x, h, qkv, s_in, t_in, ycat, r, al, dt, nw, sk))
        x = xn
    return x, saved, weights


def _w_in_blocks(g, name):
    cols, tc = g.shape[1], 256
    pieces = list(_shard_pieces(_REGIONS))

    def body(g_ref, o_ref):
        blocks = [[] for _ in range(N_DEV)]
        for d, lo, hi, off in pieces:
            blocks[d].append(g_ref[off:off + hi - lo, :])
        for d in range(N_DEV):
            o_ref[d] = jnp.concatenate(blocks[d], axis=0).astype(BF16)

    return pl.pallas_call(
        body, name=name, grid=(cols // tc,),
        in_specs=[pl.BlockSpec((L_COLS, tc), lambda i: (0, i))],
        out_specs=pl.BlockSpec((N_DEV, SHARD_COLS, tc), lambda i: (0, 0, i)),
        out_shape=jax.ShapeDtypeStruct((N_DEV, SHARD_COLS, cols), BF16),
        compiler_params=_cparams(("parallel",)),
    )(g)


def _small_blocks(g):
    c_conv = g["conv_w"].reshape(CONV_K, N_DEV, CONV_SHARD_COLS).transpose(1, 0, 2)
    c_small = [jnp.broadcast_to(g[n][None], (N_DEV,) + g[n].shape) for n, _ in SMALL_SIZES]
    return _pack_small(c_conv, c_small)


def _contributions(g):
    c_out = g["w_out"].astype(BF16).reshape(N_DEV, OUT_SHARD_ROWS, D_MODEL)
    return _w_in_blocks(g["w_in_rows"], name="w_in_grad_blocks_above"), c_out, _small_blocks(g)


def _backward_layer(l, dx, saved_l, weights_l, ln_g_l, above=None, loss=None):
    x_in, h, qkv, s_in, t_in, ycat, r, al, dt, nw, sk = saved_l
    w_in_l, w_out_l, conv_l = weights_l
    tm = min(512, x_in.shape[0])
    dr, d_lng, d_lnb, *loss_lanes = _ln_bwd(dx, r, ln_g_l[None, :], name=f"ln_bwd_{l}", loss=loss)
    big = min(1024, x_in.shape[0])
    dycat = _matmul(dr, w_out_l, form="nt", tm=big, tn=D_MODEL, tk=D_MODEL, name=f"out_proj_dx_{l}")
    d_wout = _matmul(ycat, dr, form="tn", tm=D_MODEL, tn=D_MODEL, tk=big, name=f"out_proj_dw_{l}")
    rider, p_in, p_out, p_small = None, None, None, None
    recv = lambda c: jax.ShapeDtypeStruct((DEPTH,) + c.shape, c.dtype)
    if above:
        c_out = d_wout.astype(BF16).reshape(N_DEV, OUT_SHARD_ROWS, D_MODEL)
        rider = _Direct([(above[1], True, 0, (l + 1,)), (above[2], True, 1, (l + 1,)), (c_out, True, 0, (l,))],
                        [recv(above[1]), recv(above[2])])
    dh, d_sk, *got = _swa_bwd(h, sk, dycat, name=f"swa_bwd_{l}", rider=rider)
    if above:
        p_out, p_small = got
        rider = _Direct([(above[0], True, 0, (l + 1,))], [recv(above[0])])
    dh, dqkv_n, d_al, d_dt, d_nw, *got = _gdn_bwd(qkv, h, al, dt, nw, s_in, t_in, dycat, dh,
                                                  name=f"gdn_bwd_{l}", rider=rider)
    dh, d_conv = _prep_bwd(h, conv_l, dqkv_n, dh, name=f"prep_bwd_{l}")
    grads = dict(w_out=d_wout, conv_w=d_conv[:CONV_K], a_log=d_al[0, :A_HEADS], dt_bias=d_dt[0, :A_HEADS],
                 norm_w=d_nw[0], sinks=d_sk[0, :B_Q_HEADS], ln_g=d_lng[0], ln_b=d_lnb[0])
    dw = functools.partial(_matmul, dh, x_in, form="tn", tm=L_COLS, tk=tm)
    if not above:
        grads["w_in_rows"] = dw(name=f"in_proj_dw_{l}", tn=D_MODEL)
    else:
        p_in, = got
        cut = D_MODEL // 2
        rest = D_MODEL - cut
        first = dw(name=f"in_proj_dw_first_{l}", tn=cut, b_cols=(0, cut))
        blocks = _w_in_blocks(first, name=f"w_in_grad_blocks_first_{l}")
        rider = _Direct([(blocks, True, 0, (l,), (slice(None), pl.ds(0, cut)))], [p_in])
        second, p_in = dw(name=f"in_proj_dw_second_{l}", tn=cut, b_cols=(cut, rest), rider=rider)
        blocks = _w_in_blocks(second, name=f"w_in_grad_blocks_second_{l}")
        rider = _Direct([(blocks, True, 0, (l,), (slice(None), pl.ds(cut, rest))),
                         (_small_blocks(grads), True, 1, (l,))], [p_in, p_small])
    dx, *got = _as_list(_matmul(dh, w_in_l, form="nn", tm=tm, tn=D_MODEL, tk=L_COLS, name=f"in_proj_dx_{l}",
                                add=dr, add_scale=DEEPNORM_ALPHA, rider=rider))
    bufs = (got[0], p_out, got[1]) if above else None
    return dx, grads, bufs, (loss_lanes[0] if loss else None)


def _all_gather(shards, *, name):
    n_arr = len(shards)

    def body(*refs):
        x_refs, out_refs = refs[:n_arr], refs[n_arr:2 * n_arr]
        send_sems, recv_sems, local_sems = refs[2 * n_arr:]
        x, y, c = _me()
        me, sibling = (x, y, c), (x, y, 1 - c)
        chips = [(1 - x, y), (x, 1 - y), (1 - x, 1 - y)]

        def copy(a, k, block, to, src=None):
            dst = out_refs[a].at[_flat_id(block)]
            return _remote(dst if src is None else src, dst, send_sems.at[a, k], recv_sems.at[a, k], to)

        mine = [pltpu.make_async_copy(x_refs[a], out_refs[a].at[_flat_id(me)], local_sems.at[a])
                for a in range(n_arr)]
        for cp in mine:
            cp.start()
        first = []
        for a in range(n_arr):
            first.append(copy(a, 0, me, sibling, src=x_refs[a]))
            first += [copy(a, 1 + j, me, (*chip, c), src=x_refs[a]) for j, chip in enumerate(chips)]
        for cp in first:
            cp.start()
        passed = []
        for j, chip in enumerate(chips):
            for a in range(n_arr):
                copy(a, 1 + j, (*chip, c), me).wait_recv()
                fwd = copy(a, 4 + j, (*chip, c), sibling)
                fwd.start()
                passed.append(fwd)
        for a in range(n_arr):
            copy(a, 0, sibling, me).wait_recv()
            for j, chip in enumerate(chips):
                copy(a, 4 + j, (*chip, 1 - c), me).wait_recv()
        for cp in first + passed:
            cp.wait_send()
        for cp in mine:
            cp.wait()

    return pl.pallas_call(
        body, name=name, in_specs=[_ANY] * n_arr, out_specs=[_ANY] * n_arr,
        out_shape=[jax.ShapeDtypeStruct((N_DEV,) + s.shape, s.dtype) for s in shards],
        scratch_shapes=[pltpu.SemaphoreType.DMA((n_arr, N_DEV - 1)), pltpu.SemaphoreType.DMA((n_arr, N_DEV - 1)),
                        pltpu.SemaphoreType.DMA((n_arr,))],
    )(*shards)


def _adamw(parts, w, m, v, *, tr, name):
    depth, rows, cols = w.shape
    c1 = 1.0 - ADAM_B1 ** ADAM_STEP
    c2 = 1.0 - ADAM_B2 ** ADAM_STEP

    def body(g_ref, w_ref, m_ref, v_ref, go_ref, d_ref, mo_ref, vo_ref):
        g = g_ref[0, 0].astype(F32)
        for s in range(1, N_DEV):
            g = g + g_ref[0, s].astype(F32)
        m_new = ADAM_B1 * m_ref[0] + (1.0 - ADAM_B1) * g
        v_new = ADAM_B2 * v_ref[0] + (1.0 - ADAM_B2) * (g * g)
        go_ref[0] = g
        mo_ref[0] = m_new
        vo_ref[0] = v_new
        d_ref[0] = -ADAM_LR * ((m_new / c1) / (jnp.sqrt(v_new / c2) + ADAM_EPS) + ADAM_WD * w_ref[0])

    tile = pl.BlockSpec((1, tr, cols), lambda l, i: (l, i, 0))
    return pl.pallas_call(
        body, name=name, grid=(depth, rows // tr),
        in_specs=[pl.BlockSpec((1, N_DEV, tr, cols), lambda l, i: (l, 0, i, 0)), tile, tile, tile],
        out_specs=[tile] * 4, out_shape=[jax.ShapeDtypeStruct(w.shape, F32)] * 4,
        compiler_params=_cparams(("parallel", "parallel")),
    )(parts, w, m, v)


def _adamw_w_in(parts, w, m, v, *, name):
    c1 = 1.0 - ADAM_B1 ** ADAM_STEP
    c2 = 1.0 - ADAM_B2 ** ADAM_STEP

    def body(g_ref, w_ref, m_ref, v_ref, go_ref, d_ref, mo_ref, vo_ref):
        for l in range(DEPTH):
            g = g_ref[l, 0].astype(F32)
            for s in range(1, N_DEV):
                g = g + g_ref[l, s].astype(F32)
            m_new = ADAM_B1 * m_ref[:, l, :] + (1.0 - ADAM_B1) * g
            v_new = ADAM_B2 * v_ref[:, l, :] + (1.0 - ADAM_B2) * (g * g)
            go_ref[:, l, :] = g
            mo_ref[:, l, :] = m_new
            vo_ref[:, l, :] = v_new
            d_ref[:, l, :] = -ADAM_LR * ((m_new / c1) / (jnp.sqrt(v_new / c2) + ADAM_EPS) + ADAM_WD * w_ref[:, l, :])

    tile = pl.BlockSpec((SHARD_COLS, DEPTH, LANE), lambda i: (0, 0, i))
    return pl.pallas_call(
        body, name=name, grid=(D_MODEL // LANE,),
        in_specs=[pl.BlockSpec((DEPTH, N_DEV, SHARD_COLS, LANE), lambda i: (0, 0, 0, i)), tile, tile, tile],
        out_specs=[tile] * 4, out_shape=[jax.ShapeDtypeStruct(w.shape, F32)] * 4,
        compiler_params=_cparams(("parallel",)),
    )(parts, w, m, v)


def _pack_small(conv, small):
    lead = conv.shape[:-2]
    flat = jnp.concatenate([conv.reshape(lead + (CS_CONV,))] + list(small), axis=-1)
    pad = CS_ROWS * LANE - flat.shape[-1]
    flat = jnp.concatenate([flat, jnp.zeros(lead + (pad,), F32)], axis=-1)
    return flat.reshape(lead + (CS_ROWS, LANE))


def _unpack_small(p):
    flat = p.reshape(DEPTH, CS_ROWS * LANE)
    conv = flat[:, :CS_CONV].reshape(DEPTH, CONV_K, CONV_SHARD_COLS)
    small, off = [], CS_CONV
    for _, n in SMALL_SIZES:
        small.append(flat[:, off:off + n])
        off += n
    return conv, small


def kernel(x, w_in, conv_w, a_log, dt_bias, norm_w, sinks, w_out, ln_g, ln_b, loss_target, m_w_in, m_conv_w, m_a_log, m_dt_bias, m_norm_w, m_sinks, m_w_out, m_ln_g, m_ln_b, v_w_in, v_conv_w, v_a_log, v_dt_bias, v_norm_w, v_sinks, v_w_out, v_ln_g, v_ln_b):
    small = [a_log, dt_bias, norm_w, sinks, ln_g, ln_b]
    w_t, m_t, v_t = (a.transpose(2, 0, 1) for a in (w_in, m_w_in, v_w_in))
    shards = [[w_t[:, l].astype(BF16), w_out[l].astype(BF16), conv_w[l]] for l in range(DEPTH)]
    g_in0, = _all_gather(shards[0][:1], name="weights_all_gather_0")
    weights = [[_full_w_in(g_in0, "w_in_rows_0"), None, None]] + [[None, None, None]] * (DEPTH - 1)

    _, saved, weights = _forward(x[0], weights, shards, small)
    dx, g1, _, loss_lanes = _backward_layer(1, None, saved[1], weights[1], ln_g[1],
                                            loss=(loss_target[0], ln_b[1][None, :]))
    loss = lax.psum(0.5 * jnp.sum(loss_lanes) * (1.0 / D_MODEL), ("x", "y", "c"))
    dx, _, (p_in, p_out, p_small), _ = _backward_layer(0, dx, saved[0], weights[0], ln_g[0],
                                                       above=_contributions(g1))

    o_in = [o.transpose(1, 2, 0) for o in _adamw_w_in(p_in, w_t, m_t, v_t, name="adamw_w_in")]
    o_out = _adamw(p_out, w_out, m_w_out, v_w_out, tr=OUT_SHARD_ROWS, name="adamw_w_out")
    o_small = _adamw(p_small, _pack_small(conv_w, small),
                     _pack_small(m_conv_w, [m_a_log, m_dt_bias, m_norm_w, m_sinks, m_ln_g, m_ln_b]),
                     _pack_small(v_conv_w, [v_a_log, v_dt_bias, v_norm_w, v_sinks, v_ln_g, v_ln_b]),
                     tr=CS_ROWS, name="adamw_small")
    outs = []
    for k in range(4):
        cv, sm = _unpack_small(o_small[k])
        outs += [o_in[k], cv, sm[0], sm[1], sm[2], sm[3], o_out[k], sm[4], sm[5]]
    return (loss, dx[None], *outs)
```

```python
import functools

import jax
import jax.numpy as jnp
from jax import lax
from jax.experimental import pallas as pl
from jax.experimental.pallas import tpu as pltpu

F32 = jnp.float32
BF16 = jnp.bfloat16
MM_DTYPE = BF16

N_DEV = 8
D_MODEL = 1024
DEPTH = 2
A_HEADS = 4
A_HEAD_DIM = 128
A_WIDTH = 512
CONV_K = 4
SUPER = 256
NEWTON_STEPS = 1
B_Q_HEADS = 8
B_KV_HEADS = 2
B_HEAD_DIM = 64
B_GROUP = 4
B_WIDTH = 512
WINDOW = 128
BLOCK = 128
IN_COLS = 3336
SHARD_COLS = IN_COLS // N_DEV
OUT_SHARD_ROWS = D_MODEL // N_DEV
CONV_SHARD_COLS = 3 * A_WIDTH // N_DEV
DEEPNORM_ALPHA = (2 * DEPTH) ** 0.25
LN_EPS = 1e-5
RMS_EPS = 1e-6
L2_EPS = 1e-6
ADAM_LR, ADAM_B1, ADAM_B2, ADAM_EPS, ADAM_WD, ADAM_STEP = 0.001, 0.9, 0.999, 1e-08, 0.01, 10

LANE = 128
L_QB, L_ZB, L_KB, L_VB, L_ZA, L_BA, L_QKV = 0, 512, 1024, 1152, 1280, 1792, 1920
L_SWA = 1280
L_GATE = 640
L_COLS = 3456
SMALL_SIZES = (("a_log", 4), ("dt_bias", 4), ("norm_w", 128), ("sinks", 8), ("ln_g", 1024), ("ln_b", 1024))
CS_CONV = CONV_K * CONV_SHARD_COLS
CS_ROWS = 24
VMEM_LIMIT = 48 * 1024 * 1024


def _cparams(sem=None):
    return pltpu.CompilerParams(dimension_semantics=sem, vmem_limit_bytes=VMEM_LIMIT)


def _mm(a, b):
    return jnp.dot(a.astype(MM_DTYPE), b.astype(MM_DTYPE), preferred_element_type=F32)


def _mm_nt(a, b):
    return lax.dot_general(a.astype(MM_DTYPE), b.astype(MM_DTYPE), (((1,), (1,)), ((), ())),
                           preferred_element_type=F32)


def _mm_tn(a, b):
    return lax.dot_general(a.astype(MM_DTYPE), b.astype(MM_DTYPE), (((0,), (0,)), ((), ())),
                           preferred_element_type=F32)


def _split(a):
    hi = a.astype(BF16)
    return hi, (a - hi.astype(F32)).astype(BF16)


def _silu(x):
    return x * jax.nn.sigmoid(x)


@jax.custom_vjp
def _stack(parts):
    return jnp.stack(parts)


_stack.defvjp(lambda parts: (jnp.stack(parts), None), lambda _, g: (tuple(g[i] for i in range(g.shape[0])),))


def _softplus(x):
    return jnp.maximum(x, 0.0) + jnp.log1p(jnp.exp(-jnp.abs(x)))


_ANY = pl.BlockSpec(memory_space=pl.ANY)


def _me():
    return lax.axis_index("x"), lax.axis_index("y"), lax.axis_index("c")


def _flat_id(pos):
    return 4 * pos[0] + 2 * pos[1] + pos[2]


def _remote(src, dst, send_sem, recv_sem, to):
    return pltpu.make_async_remote_copy(src_ref=src, dst_ref=dst, send_sem=send_sem, recv_sem=recv_sem,
                                        device_id=to, device_id_type=pl.DeviceIdType.MESH)


class _Direct:
    def __init__(self, items, bufs):
        self.items, self.bufs = list(items), list(bufs)
        self.n_src, self.n_buf = len(self.items), len(self.bufs)
        self.old = [j for j, b in enumerate(self.bufs) if not isinstance(b, jax.ShapeDtypeStruct)]
        self.args = [it[0] for it in self.items] + [self.bufs[j] for j in self.old]
        self.out_shape = [jax.ShapeDtypeStruct(b.shape, b.dtype) for b in self.bufs]
        self.scratch = [pltpu.SemaphoreType.DMA((self.n_src, N_DEV - 1)),
                        pltpu.SemaphoreType.DMA((self.n_src, N_DEV - 1)), pltpu.SemaphoreType.DMA((self.n_src,))]

    def aliases(self, in_base, out_base):
        return {in_base + self.n_src + pos: out_base + j for pos, j in enumerate(self.old)}

    def copies(self, in_refs, out_refs, sems):
        send_sems, recv_sems, local_sems = sems
        x, y, c = _me()
        me = _flat_id((x, y, c))
        peers = [(x ^ ((rel >> 2) & 1), y ^ ((rel >> 1) & 1), c ^ (rel & 1)) for rel in range(1, N_DEV)]
        local, sends, recvs = [], [], []
        for a, (_, per_dest, j, prefix, *rest) in enumerate(self.items):
            src = lambda d: in_refs[a].at[d] if per_dest else in_refs[a]
            dst = lambda s: out_refs[j].at[tuple(prefix) + (s,) + tuple(rest[0] if rest else ())]
            local.append(pltpu.make_async_copy(src(me), dst(me), local_sems.at[a]))
            for k, peer in enumerate(peers):
                pid = _flat_id(peer)
                sends.append(_remote(src(pid), dst(me), send_sems.at[a, k], recv_sems.at[a, k], peer))
                recvs.append(_remote(src(pid), dst(pid), send_sems.at[a, k], recv_sems.at[a, k], peer))
        return local, sends, recvs

    def start(self, in_refs, out_refs, sems):
        local, sends, _ = self.copies(in_refs, out_refs, sems)
        for cp in local + sends:
            cp.start()

    def wait(self, in_refs, out_refs, sems):
        local, sends, recvs = self.copies(in_refs, out_refs, sems)
        for cp in recvs:
            cp.wait_recv()
        for cp in sends:
            cp.wait_send()
        for cp in local:
            cp.wait()


def _pcall(core, *, name, grid, in_specs, out_specs, out_shape, args, sem, scratch_shapes=(), aliases=None,
           rider=None):
    n_in, n_out, n_scr = len(in_specs), len(out_specs), len(scratch_shapes)
    n_rin, n_rout = (len(rider.args), rider.n_buf) if rider else (0, 0)

    def body(*refs):
        ins, outs = refs[:n_in], refs[n_in + n_rin:n_in + n_rin + n_out]
        scr = refs[n_in + n_rin + n_out + n_rout:n_in + n_rin + n_out + n_rout + n_scr]
        if rider:
            r_refs = (refs[n_in:n_in + rider.n_src], refs[n_in + n_rin + n_out:n_in + n_rin + n_out + n_rout],
                      refs[n_in + n_rin + n_out + n_rout + n_scr:])
            ids = [pl.program_id(d) for d in range(len(grid))]
            first = functools.reduce(lambda p, q: p & q, [i == 0 for i in ids])
            last = functools.reduce(lambda p, q: p & q, [i == g - 1 for i, g in zip(ids, grid)])
            pl.when(first)(lambda: rider.start(*r_refs))
        core(ins, outs, scr)
        if rider:
            pl.when(last)(lambda: rider.wait(*r_refs))

    aliases = dict(aliases or {})
    if rider:
        sem = ("arbitrary",) * len(grid)
        aliases.update(rider.aliases(n_in, n_out))
    return pl.pallas_call(
        body, name=name, grid=grid, in_specs=list(in_specs) + [_ANY] * n_rin,
        out_specs=list(out_specs) + [_ANY] * n_rout,
        out_shape=list(out_shape) + (rider.out_shape if rider else []),
        scratch_shapes=list(scratch_shapes) + (rider.scratch if rider else []),
        input_output_aliases=aliases, compiler_params=_cparams(sem),
    )(*args, *(rider.args if rider else []))


def _exchange(direct, *, name):
    n_in = len(direct.args)

    def body(*refs):
        r_refs = refs[:direct.n_src], refs[n_in:n_in + direct.n_buf], refs[n_in + direct.n_buf:]
        direct.start(*r_refs)
        direct.wait(*r_refs)

    return pl.pallas_call(
        body, name=name, in_specs=[_ANY] * n_in, out_specs=[_ANY] * direct.n_buf, out_shape=direct.out_shape,
        input_output_aliases=direct.aliases(0, 0), scratch_shapes=direct.scratch,
    )(*direct.args)


def _matmul(a, b, *, form, tm, tn, tk, name, add=None, add_scale=1.0, rider=None, b_cols=None):
    if form == "nn":
        (m, kk), n = a.shape, b.shape[1]
        a_spec = pl.BlockSpec((tm, tk), lambda i, j, k: (i, k))
        b_spec = pl.BlockSpec((tk, tn), lambda i, j, k: (k, j))
        dn = (((1,), (0,)), ((), ()))
    elif form == "nt":
        (m, kk), n = a.shape, b.shape[0]
        a_spec = pl.BlockSpec((tm, tk), lambda i, j, k: (i, k))
        b_spec = pl.BlockSpec((tn, tk), lambda i, j, k: (j, k))
        dn = (((1,), (1,)), ((), ()))
    else:
        kk, m = a.shape
        n0, n = b_cols or (0, b.shape[1])
        assert n0 % tn == 0
        a_spec = pl.BlockSpec((tk, tm), lambda i, j, k: (k, i))
        b_spec = pl.BlockSpec((tk, tn), lambda i, j, k: (k, j + n0 // tn))
        dn = (((0,), (0,)), ((), ()))
    assert m % tm == 0 and n % tn == 0 and kk % tk == 0, (name, m, n, kk)
    has_add = add is not None

    def core(ins, outs, _):
        a_ref, b_ref = ins[:2]
        o_ref = outs[0]
        k = pl.program_id(2)
        p = lax.dot_general(a_ref[...].astype(MM_DTYPE), b_ref[...].astype(MM_DTYPE), dn,
                            preferred_element_type=F32)

        @pl.when(k == 0)
        def _():
            o_ref[...] = p + add_scale * ins[2][...] if has_add else p

        @pl.when(k > 0)
        def _():
            o_ref[...] += p

    in_specs = [a_spec, b_spec]
    args = [a, b]
    if has_add:
        in_specs.append(pl.BlockSpec((tm, tn), lambda i, j, k: (i, j)))
        args.append(add)
    res = _pcall(core, name=name, grid=(m // tm, n // tn, kk // tk), in_specs=in_specs,
                 out_specs=[pl.BlockSpec((tm, tn), lambda i, j, k: (i, j))],
                 out_shape=[jax.ShapeDtypeStruct((m, n), F32)], args=args,
                 sem=("parallel", "parallel", "arbitrary"), rider=rider)
    return res if rider else res[0]


ZERO_TAIL = 8


def _with_tail(x):
    return jnp.concatenate([x, jnp.zeros((ZERO_TAIL,) + x.shape[1:], x.dtype)], axis=0)


def _shift_down(x, k):
    return pltpu.roll(x, k, 0)


def _shift_up(x, k):
    return pltpu.roll(x, x.shape[0] - k, 0)


def _conv_slab(x, w):
    return w[3:4] * x + w[2:3] * _shift_down(x, 1) + w[1:2] * _shift_down(x, 2) + w[0:1] * _shift_down(x, 3)


def _prep_fwd(h, conv_w, *, name):
    t_len = h.shape[0]

    def body(x_ref, w_ref, o_ref):
        s = pl.program_id(0)
        y = _silu(_conv_slab(_with_tail(x_ref[...]), w_ref[...])[:t_len])
        rs = lax.rsqrt(jnp.sum(y * y, axis=-1, keepdims=True) + L2_EPS)
        scale = jnp.where(s < A_HEADS, A_HEAD_DIM ** -0.5, 1.0)
        o_ref[...] = jnp.where(s < 2 * A_HEADS, y * rs * scale, y)

    return pl.pallas_call(
        body, name=name, grid=(12,),
        in_specs=[pl.BlockSpec((t_len, LANE), lambda s: (0, L_QKV // LANE + s)),
                  pl.BlockSpec((8, LANE), lambda s: (0, s))],
        out_specs=pl.BlockSpec((t_len, LANE), lambda s: (0, s)),
        out_shape=jax.ShapeDtypeStruct((t_len, 3 * A_WIDTH), F32),
        compiler_params=_cparams(("parallel",)),
    )(h, conv_w)


def _prep_bwd(h, conv_w, d_out, dh, *, name):
    t_len = h.shape[0]

    def body(x_ref, w_ref, g_ref, dh_in, dx_ref, dw_ref):
        del dh_in
        s = pl.program_id(0)
        x = _with_tail(x_ref[...])
        g = _with_tail(g_ref[0])
        w = w_ref[...]
        xs = [_shift_down(x, 3), _shift_down(x, 2), _shift_down(x, 1), x]
        c = w[0:1] * xs[0] + w[1:2] * xs[1] + w[2:3] * xs[2] + w[3:4] * xs[3]
        sg = jax.nn.sigmoid(c)
        y = c * sg
        rs = lax.rsqrt(jnp.sum(y * y, axis=-1, keepdims=True) + L2_EPS)
        scale = jnp.where(s < A_HEADS, A_HEAD_DIM ** -0.5, 1.0)
        dy_n = scale * (rs * g - y * (rs * rs * rs) * jnp.sum(g * y, axis=-1, keepdims=True))
        dy = jnp.where(s < 2 * A_HEADS, dy_n, g)
        dc = dy * (sg * (1.0 + c * (1.0 - sg)))
        dx = w[3:4] * dc + w[2:3] * _shift_up(dc, 1) + w[1:2] * _shift_up(dc, 2) + w[0:1] * _shift_up(dc, 3)
        dx_ref[...] = dx[:t_len].astype(dx_ref.dtype)
        dws = [jnp.sum(dc * xs[j], axis=0, keepdims=True) for j in range(CONV_K)]
        dw_ref[...] = jnp.concatenate(dws + [jnp.zeros((8 - CONV_K, LANE), F32)], axis=0)

    slab = pl.BlockSpec((t_len, LANE), lambda s: (0, L_QKV // LANE + s))
    return pl.pallas_call(
        body, name=name, grid=(12,),
        in_specs=[slab, pl.BlockSpec((8, LANE), lambda s: (0, s)),
                  pl.BlockSpec((1, t_len, LANE), lambda s: (s // A_HEADS, 0, s % A_HEADS)), _ANY],
        out_specs=[slab, pl.BlockSpec((8, LANE), lambda s: (0, s))],
        out_shape=[jax.ShapeDtypeStruct((t_len, L_COLS), MM_DTYPE), jax.ShapeDtypeStruct((8, 3 * A_WIDTH), F32)],
        input_output_aliases={3: 0},
        compiler_params=_cparams(("parallel",)),
    )(h, conv_w, d_out, dh)


N_LEVELS = 5
MF_TRIL, MF_STRIL, MF_DIAG8, MF_LOW16, MF_EYE = 0, 1, 2, 3, 3 + N_LEVELS
MB_CUM, MB_CUM_T, MB_TOT = 0, 1, 2


def _gdn_masks():
    r = lax.broadcasted_iota(jnp.int32, (SUPER, SUPER), 0)
    c = lax.broadcasted_iota(jnp.int32, (SUPER, SUPER), 1)
    same = lambda shift: (r >> shift) == (c >> shift)
    ninf = lambda m: jnp.where(m, 0.0, -jnp.inf).astype(F32)
    one = lambda m: m.astype(F32)
    mf = jnp.stack([ninf(r >= c), ninf(r > c), one(same(3))]
                   + [one(same(4 + lv) & jnp.logical_not(same(3 + lv))) for lv in range(N_LEVELS)] + [one(r == c)])
    mb = jnp.stack([one(r >= c), one(r <= c), jnp.ones((SUPER, SUPER), F32)]).astype(BF16)
    return mf, mb


def _tri_inv_impl(a, mf):
    d = lambda p, q: jnp.dot(p.astype(BF16), q.astype(BF16), preferred_element_type=F32)
    dd = lambda p, q: jnp.dot(p, q, preferred_element_type=F32)
    eye = mf[MF_EYE]
    a0 = a * mf[MF_DIAG8]
    a2 = d(a0, a0)
    a4 = d(a2, a2)
    t = d(d(eye - a0, eye + a2), eye + a4)
    for level in range(N_LEVELS):
        t = t - d(d(t, a * mf[MF_LOW16 + level]), t)
    a_hi, a_lo = _split(a)
    for _ in range(NEWTON_STEPS):
        t0 = t.astype(BF16)
        t0f = t0.astype(F32)
        resid = (eye - t0f) - (dd(a_hi, t0) + dd(a_lo, t0))
        r_hi, r_lo = _split(resid)
        t = t0f + (dd(t0, r_hi) + dd(t0, r_lo))
    return t


@jax.custom_vjp
def _wy_apply(a, rhs, t):
    return _mm(t, rhs)


def _wy_apply_fwd(a, rhs, t):
    x = _mm(t, rhs)
    return x, (t, x)


def _wy_apply_bwd(res, dx):
    t, x = res
    d_rhs = _mm_tn(t, dx)
    return -_mm_nt(d_rhs, x), d_rhs, jnp.zeros_like(t)


_wy_apply.defvjp(_wy_apply_fwd, _wy_apply_bwd)


@functools.partial(jax.custom_vjp, nondiff_argnums=(1,))
def _lane_roll(x, shift):
    return pltpu.roll(x, shift % LANE, 1)


_lane_roll.defvjp(lambda x, shift: (_lane_roll(x, shift), None), lambda shift, _, g: (_lane_roll(g, -shift),))


def _mask_times_lanes(x, mask):
    lane = lax.broadcasted_iota(jnp.int32, (1, LANE), 1)
    x = jnp.where(lane < A_HEADS, x, 0.0)
    x1 = x.astype(BF16).astype(F32)
    x2 = (x - x1).astype(BF16).astype(F32)
    x3 = (x - x1 - x2).astype(BF16).astype(F32)
    pieces = x1 + pltpu.roll(x2, A_HEADS, 1) + pltpu.roll(x3, 2 * A_HEADS, 1)
    res = jnp.dot(mask, pieces.astype(BF16), preferred_element_type=F32)
    return res + pltpu.roll(res, LANE - A_HEADS, 1) + pltpu.roll(res, LANE - 2 * A_HEADS, 1)


@jax.custom_vjp
def _chunk_sums(g, mb):
    return _mask_times_lanes(g, mb[MB_CUM]), _mask_times_lanes(g, mb[MB_TOT])


def _chunk_sums_fwd(g, mb):
    return _chunk_sums(g, mb), mb


def _chunk_sums_bwd(mb, d):
    lane = lax.broadcasted_iota(jnp.int32, (1, LANE), 1)
    dg = _mask_times_lanes(d[0], mb[MB_CUM_T]) + _mask_times_lanes(d[1], mb[MB_TOT])
    return jnp.where(lane < A_HEADS, dg, 0.0), jnp.zeros_like(mb)


_chunk_sums.defvjp(_chunk_sums_fwd, _chunk_sums_bwd)


def _gdn_gates(ba, alog, dtb, mb):
    beta = jax.nn.sigmoid(ba)
    g = -jnp.exp(alog) * _softplus(_lane_roll(ba, -A_HEADS) + dtb)
    gc, gl = _chunk_sums(g, mb)
    return beta, gc, gl, gc.T


def _gdn_block(s, q, k, v, z, gates, nw, h, t_known, mf):
    n = q.shape[0]
    beta_all, gc_all, gl_all, gct_all = gates
    lane = lax.broadcasted_iota(jnp.int32, (1, LANE), 1)
    sub = lax.broadcasted_iota(jnp.int32, (LANE, 1), 0)
    col = lambda x: jnp.sum(jnp.where(lane == h, x, 0.0), axis=1, keepdims=True)
    wide = lambda c: jnp.broadcast_to(c, (n, LANE))
    gc, gl = col(gc_all), col(gl_all)
    gc_row = jnp.sum(jnp.where(sub == h, gct_all, 0.0), axis=0, keepdims=True)
    beta_w, eg_w = wide(col(beta_all)), wide(jnp.exp(gc))
    diff = gc - gc_row
    decay = jnp.exp(diff + mf[MF_TRIL])
    kb = k * beta_w
    a_mat = _mm_nt(kb, k) * jnp.exp(diff + mf[MF_STRIL])
    rhs = jnp.concatenate([v * beta_w, kb * eg_w], axis=1)
    if t_known is None:
        t_mat = _tri_inv_impl(a_mat, mf)
        uw = _mm(t_mat, rhs)
    else:
        t_mat = t_known
        uw = _wy_apply(a_mat, rhs, t_known)
    u, w = uw[:, :LANE], uw[:, LANE:]
    qk = _mm_nt(q, k) * decay
    q_dec = q * eg_w
    k_dec = k * wide(jnp.exp(gl - gc))
    v_new = u - _mm(w, s)
    o = _mm(q_dec, s) + _mm(qk, v_new)
    s = s * jnp.exp(gl[0:1]) + _mm_tn(k_dec, v_new)
    o = o * lax.rsqrt(jnp.mean(o * o, axis=-1, keepdims=True) + RMS_EPS) * nw
    return o * _silu(z), s, t_mat


def _gdn_fwd(qkv, h, alog, dtb, nw, ycat, *, name, rider=None):
    t_len = qkv.shape[0]
    nsc = t_len // SUPER

    def core(ins, outs, scr):
        q_ref, k_ref, v_ref, gate_ref, al_ref, dt_ref, nw_ref, mf_ref, mb_ref, _ = ins
        y_ref, sin_ref, t_ref = outs
        s_scr, = scr

        @pl.when(pl.program_id(0) == 0)
        def _():
            s_scr[...] = jnp.zeros_like(s_scr)

        per_head = lambda ref: jnp.stack([ref[:, hh * LANE:(hh + 1) * LANE] for hh in range(A_HEADS)])
        states = s_scr[...]
        gates = _gdn_gates(gate_ref[:, A_WIDTH:], al_ref[...], dt_ref[...], mb_ref[...])
        fn = jax.vmap(_gdn_block, in_axes=(0, 0, 0, 0, 0, None, None, 0, None, None))
        y, s_new, t_mat = fn(states, per_head(q_ref), per_head(k_ref), per_head(v_ref), per_head(gate_ref),
                             gates, nw_ref[...], jnp.arange(A_HEADS), None, mf_ref[...])
        sin_ref[0] = states
        t_ref[0] = t_mat
        s_scr[...] = s_new
        for hh in range(A_HEADS):
            y_ref[:, hh * LANE:(hh + 1) * LANE] = y[hh]

    blk = lambda j: pl.BlockSpec((SUPER, A_WIDTH), lambda sc: (sc, j))
    row = pl.BlockSpec((1, LANE), lambda sc: (0, 0))
    mf, mb = _gdn_masks()
    whole = lambda a: pl.BlockSpec(a.shape, lambda sc: (0, 0, 0))
    return _pcall(
        core, name=name, grid=(nsc,),
        in_specs=[blk(0), blk(1), blk(2), pl.BlockSpec((SUPER, L_GATE), lambda sc: (sc, L_ZA // L_GATE)),
                  row, row, row, whole(mf), whole(mb), _ANY],
        out_specs=[blk(0),
                   pl.BlockSpec((1, A_HEADS, A_HEAD_DIM, A_HEAD_DIM), lambda sc: (sc, 0, 0, 0)),
                   pl.BlockSpec((1, A_HEADS, SUPER, SUPER), lambda sc: (sc, 0, 0, 0))],
        out_shape=[jax.ShapeDtypeStruct((t_len, D_MODEL), F32),
                   jax.ShapeDtypeStruct((nsc, A_HEADS, A_HEAD_DIM, A_HEAD_DIM), F32),
                   jax.ShapeDtypeStruct((nsc, A_HEADS, SUPER, SUPER), F32)],
        scratch_shapes=[pltpu.VMEM((A_HEADS, A_HEAD_DIM, A_HEAD_DIM), F32)],
        aliases={9: 0}, sem=("arbitrary",), rider=rider,
        args=(qkv, qkv, qkv, h, alog, dtb, nw, mf, mb, ycat))


def _gdn_bwd(qkv, h, alog, dtb, nw, s_in, t_in, dycat, dh, *, name, rider=None):
    t_len = qkv.shape[0]
    nsc = t_len // SUPER

    def core(ins, outs, scr):
        q_ref, k_ref, v_ref, gate_ref, al_ref, dt_ref, nw_ref, sin_ref, t_ref, dy_ref, mf_ref, mb_ref, _ = ins
        dgate_ref, dqkv_ref, dal_ref, ddt_ref, dnw_ref = outs
        ds_scr, = scr

        @pl.when(pl.program_id(0) == 0)
        def _():
            ds_scr[...] = jnp.zeros_like(ds_scr)
            dal_ref[...] = jnp.zeros_like(dal_ref)
            ddt_ref[...] = jnp.zeros_like(ddt_ref)
            dnw_ref[...] = jnp.zeros_like(dnw_ref)

        per_head = lambda ref: jnp.stack([ref[:, hh * LANE:(hh + 1) * LANE] for hh in range(A_HEADS)])
        head_ids = jnp.arange(A_HEADS)
        t_known, mf, mb = t_ref[0], mf_ref[...], mb_ref[...]

        def fn(s, q, k, v, z, ba, alog, dtb, nw):
            gates = _gdn_gates(ba, alog, dtb, mb)
            one = lambda s, q, k, v, z, t, h: _gdn_block(s, q, k, v, z, gates, nw, h, t, mf)[:2]
            return jax.vmap(one)(s, q, k, v, z, t_known, head_ids)

        _, vjp = jax.vjp(fn, sin_ref[0], per_head(q_ref), per_head(k_ref), per_head(v_ref), per_head(gate_ref),
                         gate_ref[:, A_WIDTH:], al_ref[...], dt_ref[...], nw_ref[...])
        ds, dq, dk, dv, dz, dba, dal, ddt, dnw = vjp((per_head(dy_ref), ds_scr[...]))
        ds_scr[...] = ds
        for hh in range(A_HEADS):
            cols = slice(hh * LANE, (hh + 1) * LANE)
            dqkv_ref[0, :, cols] = dq[hh]
            dqkv_ref[1, :, cols] = dk[hh]
            dqkv_ref[2, :, cols] = dv[hh]
            dgate_ref[:, cols] = dz[hh].astype(dgate_ref.dtype)
        dgate_ref[:, A_WIDTH:] = dba.astype(dgate_ref.dtype)
        dal_ref[...] += dal
        ddt_ref[...] += ddt
        dnw_ref[...] += dnw

    rev = lambda i: nsc - 1 - i
    blk = lambda j: pl.BlockSpec((SUPER, A_WIDTH), lambda i: (rev(i), j))
    gate = pl.BlockSpec((SUPER, L_GATE), lambda i: (rev(i), L_ZA // L_GATE))
    row = pl.BlockSpec((1, LANE), lambda i: (0, 0))
    mf, mb = _gdn_masks()
    whole = lambda a: pl.BlockSpec(a.shape, lambda i: (0, 0, 0))
    return _pcall(
        core, name=name, grid=(nsc,),
        in_specs=[blk(0), blk(1), blk(2), gate, row, row, row,
                  pl.BlockSpec((1, A_HEADS, A_HEAD_DIM, A_HEAD_DIM), lambda i: (rev(i), 0, 0, 0)),
                  pl.BlockSpec((1, A_HEADS, SUPER, SUPER), lambda i: (rev(i), 0, 0, 0)),
                  blk(0), whole(mf), whole(mb), _ANY],
        out_specs=[gate, pl.BlockSpec((3, SUPER, A_WIDTH), lambda i: (0, rev(i), 0)), row, row, row],
        out_shape=[jax.ShapeDtypeStruct((t_len, L_COLS), MM_DTYPE), jax.ShapeDtypeStruct((3, t_len, A_WIDTH), F32)]
        + [jax.ShapeDtypeStruct((1, LANE), F32)] * 3,
        scratch_shapes=[pltpu.VMEM((A_HEADS, A_HEAD_DIM, A_HEAD_DIM), F32)],
        aliases={12: 0}, sem=("arbitrary",), rider=rider,
        args=(qkv, qkv, qkv, h, alog, dtb, nw, s_in, t_in, dycat, mf, mb, dh))


Q_BLOCKS = 4
Q_ROWS = Q_BLOCKS * BLOCK


def _swa_block(q, kp, kc, vp, vc, z, sinks, first):
    rows = B_GROUP * BLOCK
    ri = lax.broadcasted_iota(jnp.int32, (rows, 2 * BLOCK), 0)
    si = lax.broadcasted_iota(jnp.int32, (rows, 2 * BLOCK), 1)
    dist = (ri & (BLOCK - 1)) + BLOCK - si
    bias = jnp.where((dist >= 0) & (dist < WINDOW), 0.0, -jnp.inf)
    no_prev = jnp.where(first & (si[:1] < BLOCK), -jnp.inf, 0.0)
    dist_f = dist.astype(F32)
    head_of_row = lax.broadcasted_iota(jnp.int32, (rows, 1), 0) >> 7
    keys = jnp.concatenate([kp, kc], axis=0)
    vals = jnp.concatenate([vp, vc], axis=0)

    def item(b, j):
        cs = slice(j * B_HEAD_DIM, (j + 1) * B_HEAD_DIM)
        rs = slice(b * BLOCK, (b + 1) * BLOCK)
        heads = range(j * B_GROUP, (j + 1) * B_GROUP)
        qs = jnp.concatenate([q[rs, hq * B_HEAD_DIM:(hq + 1) * B_HEAD_DIM] for hq in heads], axis=0) * (
            B_HEAD_DIM ** -0.5)
        kk = keys[b * BLOCK:(b + 2) * BLOCK, cs]
        vv = vals[b * BLOCK:(b + 2) * BLOCK, cs]
        sink = jnp.concatenate([jnp.broadcast_to(sinks[:, hq:hq + 1], (BLOCK, 1)) for hq in heads], axis=0)
        slope = sum(jnp.where(head_of_row == gi, 2.0 ** (-8.0 * (hq + 1) / B_Q_HEADS), 0.0)
                    for gi, hq in enumerate(heads))
        return qs, kk, vv, sink, slope, (no_prev if b == 0 else jnp.zeros_like(no_prev))

    def attend(qs, kk, vv, sink, slope, hide):
        sc = _mm_nt(qs, kk) - slope * dist_f + (bias + hide)
        m = lax.stop_gradient(jnp.maximum(jnp.max(sc, axis=-1, keepdims=True), sink))
        p = jnp.exp(sc - m)
        inv = 1.0 / (jnp.sum(p, axis=-1, keepdims=True) + jnp.exp(sink - m))
        return _mm(p * inv, vv)

    items = [(b, j) for b in range(Q_BLOCKS) for j in range(B_KV_HEADS)]
    o = jax.vmap(attend)(*[_stack(t) for t in zip(*[item(b, j) for b, j in items])])
    rows_out = [jnp.concatenate([o[b * B_KV_HEADS + j, gi * BLOCK:(gi + 1) * BLOCK]
                                 for j in range(B_KV_HEADS) for gi in range(B_GROUP)], axis=1)
                for b in range(Q_BLOCKS)]
    return jnp.concatenate(rows_out, axis=0) * _silu(z)


def _swa_specs(idx):
    wide = lambda off: pl.BlockSpec((Q_ROWS, B_WIDTH), lambda n: (idx(n), off))
    cur = lambda off: pl.BlockSpec((Q_ROWS, LANE), lambda n: (idx(n), off))
    prev = lambda off: pl.BlockSpec((BLOCK, LANE), lambda n: (jnp.maximum(idx(n) * Q_BLOCKS - 1, 0), off))
    return [wide(L_QB // B_WIDTH), prev(L_KB // LANE), cur(L_KB // LANE), prev(L_VB // LANE), cur(L_VB // LANE),
            wide(L_ZB // B_WIDTH), pl.BlockSpec((1, LANE), lambda n: (0, 0))]


def _swa_fwd(h, sinks, *, name, rider=None):
    t_len = h.shape[0]
    nb = t_len // Q_ROWS

    def core(ins, outs, _):
        q_ref, kp_ref, kc_ref, vp_ref, vc_ref, z_ref, s_ref = ins
        outs[0][...] = _swa_block(q_ref[...], kp_ref[...], kc_ref[...], vp_ref[...], vc_ref[...], z_ref[...],
                                  s_ref[...], pl.program_id(0) == 0)

    res = _pcall(core, name=name, grid=(nb,), in_specs=_swa_specs(lambda n: n),
                 out_specs=[pl.BlockSpec((Q_ROWS, B_WIDTH), lambda n: (n, 1))],
                 out_shape=[jax.ShapeDtypeStruct((t_len, D_MODEL), F32)], sem=("parallel",), rider=rider,
                 args=(h, h, h, h, h, h, sinks))
    return res if rider else res[0]


def _swa_bwd(h, sinks, dycat, *, name, rider=None):
    t_len = h.shape[0]
    nb = t_len // Q_ROWS
    early = slice(0, Q_ROWS - BLOCK)
    last = slice(Q_ROWS - BLOCK, Q_ROWS)

    def core(ins, outs, scr):
        q_ref, kp_ref, kc_ref, vp_ref, vc_ref, z_ref, s_ref, dy_ref = ins
        dh_ref, dsk_ref = outs
        ck_scr, cv_scr = scr
        i = pl.program_id(0)
        n = nb - 1 - i

        @pl.when(i == 0)
        def _():
            ck_scr[...] = jnp.zeros_like(ck_scr)
            cv_scr[...] = jnp.zeros_like(cv_scr)
            dsk_ref[...] = jnp.zeros_like(dsk_ref)

        fn = functools.partial(_swa_block, first=(n == 0))
        _, vjp = jax.vjp(fn, q_ref[...], kp_ref[...], kc_ref[...], vp_ref[...], vc_ref[...], z_ref[...], s_ref[...])
        dq, dkp, dkc, dvp, dvc, dz, dsk = vjp(dy_ref[...])
        def put(rows, col, val):
            dh_ref[rows, col:col + val.shape[1]] = val.astype(dh_ref.dtype)

        put(slice(None), L_QB, dq)
        put(slice(None), L_ZB, dz)
        put(early, L_KB, dkc[early])
        put(early, L_VB, dvc[early])
        put(last, L_KB, dkc[last] + ck_scr[...])
        put(last, L_VB, dvc[last] + cv_scr[...])
        ck_scr[...] = dkp
        cv_scr[...] = dvp
        dsk_ref[...] += dsk

    rev = lambda i: nb - 1 - i
    return _pcall(
        core, name=name, grid=(nb,),
        in_specs=_swa_specs(rev) + [pl.BlockSpec((Q_ROWS, B_WIDTH), lambda i: (rev(i), 1))],
        out_specs=[pl.BlockSpec((Q_ROWS, L_SWA), lambda i: (rev(i), 0)), pl.BlockSpec((1, LANE), lambda i: (0, 0))],
        out_shape=[jax.ShapeDtypeStruct((t_len, L_COLS), MM_DTYPE), jax.ShapeDtypeStruct((1, LANE), F32)],
        scratch_shapes=[pltpu.VMEM((BLOCK, LANE), F32), pltpu.VMEM((BLOCK, LANE), F32)],
        sem=("arbitrary",), rider=rider, args=(h, h, h, h, h, h, sinks, dycat))


def _out_ln_fwd(ycat, w_out, x, ln_g, ln_b, *, name, tm=512, last=False):
    t_len = x.shape[0]

    def body(y_ref, w_ref, x_ref, g_ref, b_ref, r_ref, *o_ref):
        r = DEEPNORM_ALPHA * x_ref[...] + _mm(y_ref[...], w_ref[...])
        r_ref[...] = r
        if not last:
            mu = jnp.mean(r, axis=-1, keepdims=True)
            d = r - mu
            var = jnp.mean(d * d, axis=-1, keepdims=True)
            o_ref[0][...] = d * lax.rsqrt(var + LN_EPS) * g_ref[...] + b_ref[...]

    tile = pl.BlockSpec((tm, D_MODEL), lambda i: (i, 0))
    vec = pl.BlockSpec((1, D_MODEL), lambda i: (0, 0))
    n_out = 1 if last else 2
    res = pl.pallas_call(
        body, name=name, grid=(t_len // tm,),
        in_specs=[tile, pl.BlockSpec((D_MODEL, D_MODEL), lambda i: (0, 0)), tile, vec, vec],
        out_specs=[tile] * n_out,
        out_shape=[jax.ShapeDtypeStruct((t_len, D_MODEL), F32)] * n_out,
        compiler_params=_cparams(("parallel",)),
    )(ycat, w_out, x, ln_g, ln_b)
    return (res[0], None) if last else res


def _ln_bwd(dxn, r, ln_g, *, name, tm=512, loss=None):
    t_len = r.shape[0]

    def body(*refs):
        if loss:
            t_ref, r_ref, g_ref, b_ref, dr_ref, dg_ref, db_ref, l_ref = refs
        else:
            dx_ref, r_ref, g_ref, dr_ref, dg_ref, db_ref = refs

        @pl.when(pl.program_id(0) == 0)
        def _():
            dg_ref[...] = jnp.zeros_like(dg_ref)
            db_ref[...] = jnp.zeros_like(db_ref)
            if loss:
                l_ref[...] = jnp.zeros_like(l_ref)

        rr = r_ref[...]
        mu = jnp.mean(rr, axis=-1, keepdims=True)
        d = rr - mu
        rstd = lax.rsqrt(jnp.mean(d * d, axis=-1, keepdims=True) + LN_EPS)
        xh = d * rstd
        if loss:
            e = (xh * g_ref[...] + b_ref[...]) - t_ref[...]
            dx = e * (1.0 / D_MODEL)
            l_ref[...] += jnp.sum(e * e, axis=0, keepdims=True)
        else:
            dx = dx_ref[...]
        dxh = dx * g_ref[...]
        dr_ref[...] = rstd * (dxh - jnp.mean(dxh, axis=-1, keepdims=True)
                              - xh * jnp.mean(dxh * xh, axis=-1, keepdims=True))
        dg_ref[...] += jnp.sum(dx * xh, axis=0, keepdims=True)
        db_ref[...] += jnp.sum(dx, axis=0, keepdims=True)

    tile = pl.BlockSpec((tm, D_MODEL), lambda i: (i, 0))
    vec = pl.BlockSpec((1, D_MODEL), lambda i: (0, 0))
    vec_shape = jax.ShapeDtypeStruct((1, D_MODEL), F32)
    args = (loss[0], r, ln_g, loss[1]) if loss else (dxn, r, ln_g)
    return pl.pallas_call(
        body, name=name, grid=(t_len // tm,),
        in_specs=[tile, tile, vec] + ([vec] if loss else []), out_specs=[tile, vec, vec] + ([vec] if loss else []),
        out_shape=[jax.ShapeDtypeStruct((t_len, D_MODEL), F32), vec_shape, vec_shape] + ([vec_shape] if loss else []),
        compiler_params=_cparams(("arbitrary",)),
    )(*args)


def _pad_row(v):
    return jnp.zeros((1, LANE), F32).at[0, :v.shape[0]].set(v)


_REGIONS = ((0, 1536, L_QKV), (1536, 2048, L_ZA), (2048, 2056, L_BA), (2056, 2568, L_QB), (2568, 2696, L_KB),
            (2696, 2824, L_VB), (2824, 3336, L_ZB))


def _shard_pieces(regions):
    for a, b, off in regions:
        for d in range(N_DEV):
            lo, hi = max(a, d * SHARD_COLS), min(b, (d + 1) * SHARD_COLS)
            if lo < hi:
                yield d, lo - d * SHARD_COLS, hi - d * SHARD_COLS, off + lo - a


def _as_list(r):
    return list(r) if isinstance(r, (list, tuple)) else [r]


def _gathered(shard):
    return jax.ShapeDtypeStruct((N_DEV,) + shard.shape, shard.dtype)


def _full_w_in(g_in, name):
    by_offset = sorted(_shard_pieces(_REGIONS), key=lambda p: p[3])
    tc = 256

    def body(g_ref, o_ref):
        pieces, row = [], 0
        for d, lo, hi, off in by_offset + [(None, 0, 0, L_COLS)]:
            if off > row:
                pieces.append(jnp.zeros((off - row, tc), g_ref.dtype))
            if d is not None:
                pieces.append(g_ref[d, lo:hi, :])
            row = off + hi - lo
        o_ref[...] = jnp.concatenate(pieces, axis=0)

    return pl.pallas_call(
        body, name=name, grid=(D_MODEL // tc,),
        in_specs=[pl.BlockSpec((N_DEV, SHARD_COLS, tc), lambda i: (0, 0, i))],
        out_specs=pl.BlockSpec((L_COLS, tc), lambda i: (0, i)),
        out_shape=jax.ShapeDtypeStruct((L_COLS, D_MODEL), g_in.dtype),
        compiler_params=_cparams(("parallel",)),
    )(g_in)


def _full_conv(g_conv):
    return jnp.pad(g_conv.transpose(1, 0, 2).reshape(CONV_K, 3 * A_WIDTH), ((0, 8 - CONV_K), (0, 0)))


def _forward(x, weights, shards, small):
    a_log, dt_bias, norm_w, sinks, ln_g, ln_b = small
    tm = min(512, x.shape[0])
    saved, weights = [], [list(w) for w in weights]
    whole = lambda arrs: _Direct([(a, False, j, ()) for j, a in enumerate(arrs)], [_gathered(a) for a in arrs])
    for l in range(DEPTH):
        rider = whole(shards[l][1:]) if weights[l][1] is None else None
        h, *got = _as_list(_matmul(x, weights[l][0], form="nt", tm=tm, tn=L_COLS, tk=D_MODEL, name=f"in_proj_{l}",
                                   rider=rider))
        if rider:
            weights[l][1:] = [got[0].reshape(D_MODEL, D_MODEL), _full_conv(got[1])]
        w_in_l, w_out_l, conv_l = weights[l]
        qkv = _prep_fwd(h, conv_l, name=f"prep_fwd_{l}")
        al, dt, nw, sk = _pad_row(a_log[l]), _pad_row(dt_bias[l]), norm_w[l][None, :], _pad_row(sinks[l])
        ahead = l + 1 < DEPTH and weights[l + 1][0] is None
        rider = whole(shards[l + 1][1:]) if ahead else None
        ycat, *got = _as_list(_swa_fwd(h, sk, name=f"swa_fwd_{l}", rider=rider))
        if ahead:
            weights[l + 1][1:] = [got[0].reshape(D_MODEL, D_MODEL), _full_conv(got[1])]
        rider = whole(shards[l + 1][:1]) if ahead else None
        ycat, s_in, t_in, *got = _gdn_fwd(qkv, h, al, dt, nw, ycat, name=f"gdn_fwd_{l}", rider=rider)
        if ahead:
            weights[l + 1][0] = _full_w_in(got[0], f"w_in_rows_{l + 1}")
        r, xn = _out_ln_fwd(ycat, w_out_l, x, ln_g[l][None, :], ln_b[l][None, :], name=f"out_ln_{l}",
                            last=(l == DEPTH - 1))
        saved.append((x, h, qkv, s_in, t_in, ycat, r, al, dt, nw, sk))
        x = xn
    return x, saved, weights


def _w_in_blocks(g, name):
    cols, tc = g.shape[1], 256
    pieces = list(_shard_pieces(_REGIONS))

    def body(g_ref, o_ref):
        blocks = [[] for _ in range(N_DEV)]
        for d, lo, hi, off in pieces:
            blocks[d].append(g_ref[off:off + hi - lo, :])
        for d in range(N_DEV):
            o_ref[d] = jnp.concatenate(blocks[d], axis=0).astype(BF16)

    return pl.pallas_call(
        body, name=name, grid=(cols // tc,),
        in_specs=[pl.BlockSpec((L_COLS, tc), lambda i: (0, i))],
        out_specs=pl.BlockSpec((N_DEV, SHARD_COLS, tc), lambda i: (0, 0, i)),
        out_shape=jax.ShapeDtypeStruct((N_DEV, SHARD_COLS, cols), BF16),
        compiler_params=_cparams(("parallel",)),
    )(g)


def _small_blocks(g):
    c_conv = g["conv_w"].reshape(CONV_K, N_DEV, CONV_SHARD_COLS).transpose(1, 0, 2)
    c_small = [jnp.broadcast_to(g[n][None], (N_DEV,) + g[n].shape) for n, _ in SMALL_SIZES]
    return _pack_small(c_conv, c_small)


def _contributions(g):
    c_out = g["w_out"].astype(BF16).reshape(N_DEV, OUT_SHARD_ROWS, D_MODEL)
    return _w_in_blocks(g["w_in_rows"], name="w_in_grad_blocks_above"), c_out, _small_blocks(g)


def _backward_layer(l, dx, saved_l, weights_l, ln_g_l, above=None, loss=None):
    x_in, h, qkv, s_in, t_in, ycat, r, al, dt, nw, sk = saved_l
    w_in_l, w_out_l, conv_l = weights_l
    tm = min(512, x_in.shape[0])
    dr, d_lng, d_lnb, *loss_lanes = _ln_bwd(dx, r, ln_g_l[None, :], name=f"ln_bwd_{l}", loss=loss)
    big = min(1024, x_in.shape[0])
    dycat = _matmul(dr, w_out_l, form="nt", tm=big, tn=D_MODEL, tk=D_MODEL, name=f"out_proj_dx_{l}")
    d_wout = _matmul(ycat, dr, form="tn", tm=D_MODEL, tn=D_MODEL, tk=big, name=f"out_proj_dw_{l}")
    rider, p_in, p_out, p_small = None, None, None, None
    recv = lambda c: jax.ShapeDtypeStruct((DEPTH,) + c.shape, c.dtype)
    if above:
        c_out = d_wout.astype(BF16).reshape(N_DEV, OUT_SHARD_ROWS, D_MODEL)
        rider = _Direct([(above[1], True, 0, (l + 1,)), (above[2], True, 1, (l + 1,)), (c_out, True, 0, (l,))],
                        [recv(above[1]), recv(above[2])])
    dh, d_sk, *got = _swa_bwd(h, sk, dycat, name=f"swa_bwd_{l}", rider=rider)
    if above:
        p_out, p_small = got
        rider = _Direct([(above[0], True, 0, (l + 1,))], [recv(above[0])])
    dh, dqkv_n, d_al, d_dt, d_nw, *got = _gdn_bwd(qkv, h, al, dt, nw, s_in, t_in, dycat, dh,
                                                  name=f"gdn_bwd_{l}", rider=rider)
    dh, d_conv = _prep_bwd(h, conv_l, dqkv_n, dh, name=f"prep_bwd_{l}")
    grads = dict(w_out=d_wout, conv_w=d_conv[:CONV_K], a_log=d_al[0, :A_HEADS], dt_bias=d_dt[0, :A_HEADS],
                 norm_w=d_nw[0], sinks=d_sk[0, :B_Q_HEADS], ln_g=d_lng[0], ln_b=d_lnb[0])
    dw = functools.partial(_matmul, dh, x_in, form="tn")
    if not above:
        grads["w_in_rows"] = dw(name=f"in_proj_dw_{l}", tm=L_COLS // 3, tn=D_MODEL, tk=min(2048, x_in.shape[0]))
    else:
        p_in, = got
        cut = D_MODEL // 2
        rest = D_MODEL - cut
        first = dw(name=f"in_proj_dw_first_{l}", tm=L_COLS, tn=cut, tk=big, b_cols=(0, cut))
        blocks = _w_in_blocks(first, name=f"w_in_grad_blocks_first_{l}")
        rider = _Direct([(blocks, True, 0, (l,), (slice(None), pl.ds(0, cut)))], [p_in])
        second, p_in = dw(name=f"in_proj_dw_second_{l}", tm=L_COLS, tn=cut, tk=big, b_cols=(cut, rest), rider=rider)
        blocks = _w_in_blocks(second, name=f"w_in_grad_blocks_second_{l}")
        rider = _Direct([(blocks, True, 0, (l,), (slice(None), pl.ds(cut, rest))),
                         (_small_blocks(grads), True, 1, (l,))], [p_in, p_small])
    dx, *got = _as_list(_matmul(dh, w_in_l, form="nn", tm=tm, tn=D_MODEL, tk=L_COLS, name=f"in_proj_dx_{l}",
                                add=dr, add_scale=DEEPNORM_ALPHA, rider=rider))
    bufs = (got[0], p_out, got[1]) if above else None
    return dx, grads, bufs, (loss_lanes[0] if loss else None)


def _all_gather(shards, *, name):
    n_arr = len(shards)

    def body(*refs):
        x_refs, out_refs = refs[:n_arr], refs[n_arr:2 * n_arr]
        send_sems, recv_sems, local_sems = refs[2 * n_arr:]
        x, y, c = _me()
        me, sibling = (x, y, c), (x, y, 1 - c)
        chips = [(1 - x, y), (x, 1 - y), (1 - x, 1 - y)]

        def copy(a, k, block, to, src=None):
            dst = out_refs[a].at[_flat_id(block)]
            return _remote(dst if src is None else src, dst, send_sems.at[a, k], recv_sems.at[a, k], to)

        mine = [pltpu.make_async_copy(x_refs[a], out_refs[a].at[_flat_id(me)], local_sems.at[a])
                for a in range(n_arr)]
        for cp in mine:
            cp.start()
        first = []
        for a in range(n_arr):
            first.append(copy(a, 0, me, sibling, src=x_refs[a]))
            first += [copy(a, 1 + j, me, (*chip, c), src=x_refs[a]) for j, chip in enumerate(chips)]
        for cp in first:
            cp.start()
        passed = []
        for j, chip in enumerate(chips):
            for a in range(n_arr):
                copy(a, 1 + j, (*chip, c), me).wait_recv()
                fwd = copy(a, 4 + j, (*chip, c), sibling)
                fwd.start()
                passed.append(fwd)
        for a in range(n_arr):
            copy(a, 0, sibling, me).wait_recv()
            for j, chip in enumerate(chips):
                copy(a, 4 + j, (*chip, 1 - c), me).wait_recv()
        for cp in first + passed:
            cp.wait_send()
        for cp in mine:
            cp.wait()

    return pl.pallas_call(
        body, name=name, in_specs=[_ANY] * n_arr, out_specs=[_ANY] * n_arr,
        out_shape=[jax.ShapeDtypeStruct((N_DEV,) + s.shape, s.dtype) for s in shards],
        scratch_shapes=[pltpu.SemaphoreType.DMA((n_arr, N_DEV - 1)), pltpu.SemaphoreType.DMA((n_arr, N_DEV - 1)),
                        pltpu.SemaphoreType.DMA((n_arr,))],
    )(*shards)


def _adamw(parts, w, m, v, *, tr, name):
    depth, rows, cols = w.shape
    c1 = 1.0 - ADAM_B1 ** ADAM_STEP
    c2 = 1.0 - ADAM_B2 ** ADAM_STEP

    def body(g_ref, w_ref, m_ref, v_ref, go_ref, d_ref, mo_ref, vo_ref):
        g = g_ref[0, 0].astype(F32)
        for s in range(1, N_DEV):
            g = g + g_ref[0, s].astype(F32)
        m_new = ADAM_B1 * m_ref[0] + (1.0 - ADAM_B1) * g
        v_new = ADAM_B2 * v_ref[0] + (1.0 - ADAM_B2) * (g * g)
        go_ref[0] = g
        mo_ref[0] = m_new
        vo_ref[0] = v_new
        d_ref[0] = -ADAM_LR * ((m_new / c1) / (jnp.sqrt(v_new / c2) + ADAM_EPS) + ADAM_WD * w_ref[0])

    tile = pl.BlockSpec((1, tr, cols), lambda l, i: (l, i, 0))
    return pl.pallas_call(
        body, name=name, grid=(depth, rows // tr),
        in_specs=[pl.BlockSpec((1, N_DEV, tr, cols), lambda l, i: (l, 0, i, 0)), tile, tile, tile],
        out_specs=[tile] * 4, out_shape=[jax.ShapeDtypeStruct(w.shape, F32)] * 4,
        compiler_params=_cparams(("parallel", "parallel")),
    )(parts, w, m, v)


def _adamw_w_in(parts, w, m, v, *, name):
    c1 = 1.0 - ADAM_B1 ** ADAM_STEP
    c2 = 1.0 - ADAM_B2 ** ADAM_STEP

    def body(g_ref, w_ref, m_ref, v_ref, go_ref, d_ref, mo_ref, vo_ref):
        for l in range(DEPTH):
            g = g_ref[l, 0].astype(F32)
            for s in range(1, N_DEV):
                g = g + g_ref[l, s].astype(F32)
            m_new = ADAM_B1 * m_ref[:, l, :] + (1.0 - ADAM_B1) * g
            v_new = ADAM_B2 * v_ref[:, l, :] + (1.0 - ADAM_B2) * (g * g)
            go_ref[:, l, :] = g
            mo_ref[:, l, :] = m_new
            vo_ref[:, l, :] = v_new
            d_ref[:, l, :] = -ADAM_LR * ((m_new / c1) / (jnp.sqrt(v_new / c2) + ADAM_EPS) + ADAM_WD * w_ref[:, l, :])

    tile = pl.BlockSpec((SHARD_COLS, DEPTH, LANE), lambda i: (0, 0, i))
    return pl.pallas_call(
        body, name=name, grid=(D_MODEL // LANE,),
        in_specs=[pl.BlockSpec((DEPTH, N_DEV, SHARD_COLS, LANE), lambda i: (0, 0, 0, i)), tile, tile, tile],
        out_specs=[tile] * 4, out_shape=[jax.ShapeDtypeStruct(w.shape, F32)] * 4,
        compiler_params=_cparams(("parallel",)),
    )(parts, w, m, v)


def _pack_small(conv, small):
    lead = conv.shape[:-2]
    flat = jnp.concatenate([conv.reshape(lead + (CS_CONV,))] + list(small), axis=-1)
    pad = CS_ROWS * LANE - flat.shape[-1]
    flat = jnp.concatenate([flat, jnp.zeros(lead + (pad,), F32)], axis=-1)
    return flat.reshape(lead + (CS_ROWS, LANE))


def _unpack_small(p):
    flat = p.reshape(DEPTH, CS_ROWS * LANE)
    conv = flat[:, :CS_CONV].reshape(DEPTH, CONV_K, CONV_SHARD_COLS)
    small, off = [], CS_CONV
    for _, n in SMALL_SIZES:
        small.append(flat[:, off:off + n])
        off += n
    return conv, small


def kernel(x, w_in, conv_w, a_log, dt_bias, norm_w, sinks, w_out, ln_g, ln_b, loss_target, m_w_in, m_conv_w, m_a_log, m_dt_bias, m_norm_w, m_sinks, m_w_out, m_ln_g, m_ln_b, v_w_in, v_conv_w, v_a_log, v_dt_bias, v_norm_w, v_sinks, v_w_out, v_ln_g, v_ln_b):
    small = [a_log, dt_bias, norm_w, sinks, ln_g, ln_b]
    w_t, m_t, v_t = (a.transpose(2, 0, 1) for a in (w_in, m_w_in, v_w_in))
    shards = [[w_t[:, l].astype(BF16), w_out[l].astype(BF16), conv_w[l]] for l in range(DEPTH)]
    g_in0, = _all_gather(shards[0][:1], name="weights_all_gather_0")
    weights = [[_full_w_in(g_in0, "w_in_rows_0"), None, None]] + [[None, None, None]] * (DEPTH - 1)

    _, saved, weights = _forward(x[0], weights, shards, small)
    dx, g1, _, loss_lanes = _backward_layer(1, None, saved[1], weights[1], ln_g[1],
                                            loss=(loss_target[0], ln_b[1][None, :]))
    loss = lax.psum(0.5 * jnp.sum(loss_lanes) * (1.0 / D_MODEL), ("x", "y", "c"))
    dx, _, (p_in, p_out, p_small), _ = _backward_layer(0, dx, saved[0], weights[0], ln_g[0],
                                                       above=_contributions(g1))

    o_in = [o.transpose(1, 2, 0) for o in _adamw_w_in(p_in, w_t, m_t, v_t, name="adamw_w_in")]
    o_out = _adamw(p_out, w_out, m_w_out, v_w_out, tr=OUT_SHARD_ROWS, name="adamw_w_out")
    o_small = _adamw(p_small, _pack_small(conv_w, small),
                     _pack_small(m_conv_w, [m_a_log, m_dt_bias, m_norm_w, m_sinks, m_ln_g, m_ln_b]),
                     _pack_small(v_conv_w, [v_a_log, v_dt_bias, v_norm_w, v_sinks, v_ln_g, v_ln_b]),
                     tr=CS_ROWS, name="adamw_small")
    outs = []
    for k in range(4):
        cv, sm = _unpack_small(o_small[k])
        outs += [o_in[k], cv, sm[0], sm[1], sm[2], sm[3], o_out[k], sm[4], sm[5]]
    return (loss, dx[None], *outs)
```

```python
import functools

import jax
import jax.numpy as jnp
from jax import lax
from jax.experimental import pallas as pl
from jax.experimental.pallas import tpu as pltpu

F32 = jnp.float32
BF16 = jnp.bfloat16
MM_DTYPE = BF16

N_DEV = 8
D_MODEL = 1024
DEPTH = 2
A_HEADS = 4
A_HEAD_DIM = 128
A_WIDTH = 512
CONV_K = 4
SUPER = 256
NEWTON_STEPS = 1
B_Q_HEADS = 8
B_KV_HEADS = 2
B_HEAD_DIM = 64
B_GROUP = 4
B_WIDTH = 512
WINDOW = 128
BLOCK = 128
IN_COLS = 3336
SHARD_COLS = IN_COLS // N_DEV
OUT_SHARD_ROWS = D_MODEL // N_DEV
CONV_SHARD_COLS = 3 * A_WIDTH // N_DEV
DEEPNORM_ALPHA = (2 * DEPTH) ** 0.25
LN_EPS = 1e-5
RMS_EPS = 1e-6
L2_EPS = 1e-6
ADAM_LR, ADAM_B1, ADAM_B2, ADAM_EPS, ADAM_WD, ADAM_STEP = 0.001, 0.9, 0.999, 1e-08, 0.01, 10

LANE = 128
L_QB, L_ZB, L_KB, L_VB, L_ZA, L_BA, L_QKV = 0, 512, 1024, 1152, 1280, 1792, 1920
L_SWA = 1280
L_GATE = 640
L_COLS = 3456
SMALL_SIZES = (("a_log", 4), ("dt_bias", 4), ("norm_w", 128), ("sinks", 8), ("ln_g", 1024), ("ln_b", 1024))
CS_CONV = CONV_K * CONV_SHARD_COLS
CS_ROWS = 24
assert CS_CONV + sum(n for _, n in SMALL_SIZES) < CS_ROWS * LANE
VMEM_LIMIT = 48 * 1024 * 1024


def _cparams(sem=None):
    return pltpu.CompilerParams(dimension_semantics=sem, vmem_limit_bytes=VMEM_LIMIT)


def _mm(a, b):
    return jnp.dot(a.astype(MM_DTYPE), b.astype(MM_DTYPE), preferred_element_type=F32)


def _mm_nt(a, b):
    return lax.dot_general(a.astype(MM_DTYPE), b.astype(MM_DTYPE), (((1,), (1,)), ((), ())),
                           preferred_element_type=F32)


def _mm_tn(a, b):
    return lax.dot_general(a.astype(MM_DTYPE), b.astype(MM_DTYPE), (((0,), (0,)), ((), ())),
                           preferred_element_type=F32)


def _split(a):
    hi = a.astype(BF16)
    return hi, (a - hi.astype(F32)).astype(BF16)


def _silu(x):
    return x * jax.nn.sigmoid(x)


@jax.custom_vjp
def _stack(parts):
    return jnp.stack(parts)


_stack.defvjp(lambda parts: (jnp.stack(parts), None), lambda _, g: (tuple(g[i] for i in range(g.shape[0])),))


def _softplus(x):
    return jnp.maximum(x, 0.0) + jnp.log1p(jnp.exp(-jnp.abs(x)))


_ANY = pl.BlockSpec(memory_space=pl.ANY)


def _me():
    return lax.axis_index("x"), lax.axis_index("y"), lax.axis_index("c")


def _flat_id(pos):
    return 4 * pos[0] + 2 * pos[1] + pos[2]


def _remote(src, dst, send_sem, recv_sem, to):
    return pltpu.make_async_remote_copy(src_ref=src, dst_ref=dst, send_sem=send_sem, recv_sem=recv_sem,
                                        device_id=to, device_id_type=pl.DeviceIdType.MESH)


class _Direct:
    def __init__(self, items, bufs):
        self.items, self.bufs = list(items), list(bufs)
        self.n_src, self.n_buf = len(self.items), len(self.bufs)
        self.old = [j for j, b in enumerate(self.bufs) if not isinstance(b, jax.ShapeDtypeStruct)]
        self.args = [it[0] for it in self.items] + [self.bufs[j] for j in self.old]
        self.out_shape = [jax.ShapeDtypeStruct(b.shape, b.dtype) for b in self.bufs]
        self.scratch = [pltpu.SemaphoreType.DMA((self.n_src, N_DEV - 1)),
                        pltpu.SemaphoreType.DMA((self.n_src, N_DEV - 1)), pltpu.SemaphoreType.DMA((self.n_src,))]

    def aliases(self, in_base, out_base):
        return {in_base + self.n_src + pos: out_base + j for pos, j in enumerate(self.old)}

    def copies(self, in_refs, out_refs, sems):
        send_sems, recv_sems, local_sems = sems
        x, y, c = _me()
        me = _flat_id((x, y, c))
        peers = [(x ^ ((rel >> 2) & 1), y ^ ((rel >> 1) & 1), c ^ (rel & 1)) for rel in range(1, N_DEV)]
        local, sends, recvs = [], [], []
        for a, (_, per_dest, j, prefix, *rest) in enumerate(self.items):
            src = lambda d: in_refs[a].at[d] if per_dest else in_refs[a]
            dst = lambda s: out_refs[j].at[tuple(prefix) + (s,) + tuple(rest[0] if rest else ())]
            local.append(pltpu.make_async_copy(src(me), dst(me), local_sems.at[a]))
            for k, peer in enumerate(peers):
                pid = _flat_id(peer)
                sends.append(_remote(src(pid), dst(me), send_sems.at[a, k], recv_sems.at[a, k], peer))
                recvs.append(_remote(src(pid), dst(pid), send_sems.at[a, k], recv_sems.at[a, k], peer))
        return local, sends, recvs

    def start(self, in_refs, out_refs, sems):
        local, sends, _ = self.copies(in_refs, out_refs, sems)
        for cp in local + sends:
            cp.start()

    def wait(self, in_refs, out_refs, sems):
        local, sends, recvs = self.copies(in_refs, out_refs, sems)
        for cp in recvs:
            cp.wait_recv()
        for cp in sends:
            cp.wait_send()
        for cp in local:
            cp.wait()


def _pcall(core, *, name, grid, in_specs, out_specs, out_shape, args, sem, scratch_shapes=(), aliases=None,
           rider=None):
    n_in, n_out, n_scr = len(in_specs), len(out_specs), len(scratch_shapes)
    n_rin, n_rout = (len(rider.args), rider.n_buf) if rider else (0, 0)

    def body(*refs):
        ins, outs = refs[:n_in], refs[n_in + n_rin:n_in + n_rin + n_out]
        scr = refs[n_in + n_rin + n_out + n_rout:n_in + n_rin + n_out + n_rout + n_scr]
        if rider:
            r_refs = (refs[n_in:n_in + rider.n_src], refs[n_in + n_rin + n_out:n_in + n_rin + n_out + n_rout],
                      refs[n_in + n_rin + n_out + n_rout + n_scr:])
            ids = [pl.program_id(d) for d in range(len(grid))]
            first = functools.reduce(lambda p, q: p & q, [i == 0 for i in ids])
            last = functools.reduce(lambda p, q: p & q, [i == g - 1 for i, g in zip(ids, grid)])
            pl.when(first)(lambda: rider.start(*r_refs))
        core(ins, outs, scr)
        if rider:
            pl.when(last)(lambda: rider.wait(*r_refs))

    aliases = dict(aliases or {})
    if rider:
        sem = ("arbitrary",) * len(grid)
        aliases.update(rider.aliases(n_in, n_out))
    return pl.pallas_call(
        body, name=name, grid=grid, in_specs=list(in_specs) + [_ANY] * n_rin,
        out_specs=list(out_specs) + [_ANY] * n_rout,
        out_shape=list(out_shape) + (rider.out_shape if rider else []),
        scratch_shapes=list(scratch_shapes) + (rider.scratch if rider else []),
        input_output_aliases=aliases, compiler_params=_cparams(sem),
    )(*args, *(rider.args if rider else []))


def _exchange(direct, *, name):
    n_in = len(direct.args)

    def body(*refs):
        r_refs = refs[:direct.n_src], refs[n_in:n_in + direct.n_buf], refs[n_in + direct.n_buf:]
        direct.start(*r_refs)
        direct.wait(*r_refs)

    return pl.pallas_call(
        body, name=name, in_specs=[_ANY] * n_in, out_specs=[_ANY] * direct.n_buf, out_shape=direct.out_shape,
        input_output_aliases=direct.aliases(0, 0), scratch_shapes=direct.scratch,
    )(*direct.args)


def _matmul(a, b, *, form, tm, tn, tk, name, add=None, add_scale=1.0, rider=None, b_cols=None):
    if form == "nn":
        (m, kk), n = a.shape, b.shape[1]
        a_spec = pl.BlockSpec((tm, tk), lambda i, j, k: (i, k))
        b_spec = pl.BlockSpec((tk, tn), lambda i, j, k: (k, j))
        dn = (((1,), (0,)), ((), ()))
    elif form == "nt":
        (m, kk), n = a.shape, b.shape[0]
        a_spec = pl.BlockSpec((tm, tk), lambda i, j, k: (i, k))
        b_spec = pl.BlockSpec((tn, tk), lambda i, j, k: (j, k))
        dn = (((1,), (1,)), ((), ()))
    else:
        kk, m = a.shape
        n0, n = b_cols or (0, b.shape[1])
        assert n0 % tn == 0
        a_spec = pl.BlockSpec((tk, tm), lambda i, j, k: (k, i))
        b_spec = pl.BlockSpec((tk, tn), lambda i, j, k: (k, j + n0 // tn))
        dn = (((0,), (0,)), ((), ()))
    assert m % tm == 0 and n % tn == 0 and kk % tk == 0, (name, m, n, kk)
    has_add = add is not None

    def core(ins, outs, _):
        a_ref, b_ref = ins[:2]
        o_ref = outs[0]
        k = pl.program_id(2)
        p = lax.dot_general(a_ref[...].astype(MM_DTYPE), b_ref[...].astype(MM_DTYPE), dn,
                            preferred_element_type=F32)

        @pl.when(k == 0)
        def _():
            o_ref[...] = p + add_scale * ins[2][...] if has_add else p

        @pl.when(k > 0)
        def _():
            o_ref[...] += p

    in_specs = [a_spec, b_spec]
    args = [a, b]
    if has_add:
        in_specs.append(pl.BlockSpec((tm, tn), lambda i, j, k: (i, j)))
        args.append(add)
    res = _pcall(core, name=name, grid=(m // tm, n // tn, kk // tk), in_specs=in_specs,
                 out_specs=[pl.BlockSpec((tm, tn), lambda i, j, k: (i, j))],
                 out_shape=[jax.ShapeDtypeStruct((m, n), F32)], args=args,
                 sem=("parallel", "parallel", "arbitrary"), rider=rider)
    return res if rider else res[0]


ZERO_TAIL = 8


def _with_tail(x):
    return jnp.concatenate([x, jnp.zeros((ZERO_TAIL,) + x.shape[1:], x.dtype)], axis=0)


def _shift_down(x, k):
    return pltpu.roll(x, k, 0)


def _shift_up(x, k):
    return pltpu.roll(x, x.shape[0] - k, 0)


def _conv_slab(x, w):
    return w[3:4] * x + w[2:3] * _shift_down(x, 1) + w[1:2] * _shift_down(x, 2) + w[0:1] * _shift_down(x, 3)


def _prep_fwd(h, conv_w, *, name):
    t_len = h.shape[0]

    def body(x_ref, w_ref, o_ref):
        s = pl.program_id(0)
        y = _silu(_conv_slab(_with_tail(x_ref[...]), w_ref[...])[:t_len])
        rs = lax.rsqrt(jnp.sum(y * y, axis=-1, keepdims=True) + L2_EPS)
        scale = jnp.where(s < A_HEADS, A_HEAD_DIM ** -0.5, 1.0)
        o_ref[...] = jnp.where(s < 2 * A_HEADS, y * rs * scale, y)

    return pl.pallas_call(
        body, name=name, grid=(12,),
        in_specs=[pl.BlockSpec((t_len, LANE), lambda s: (0, L_QKV // LANE + s)),
                  pl.BlockSpec((8, LANE), lambda s: (0, s))],
        out_specs=pl.BlockSpec((t_len, LANE), lambda s: (0, s)),
        out_shape=jax.ShapeDtypeStruct((t_len, 3 * A_WIDTH), F32),
        compiler_params=_cparams(("parallel",)),
    )(h, conv_w)


def _prep_bwd(h, conv_w, d_out, dh, *, name):
    t_len = h.shape[0]

    def body(x_ref, w_ref, g_ref, dh_in, dx_ref, dw_ref):
        del dh_in
        s = pl.program_id(0)
        x = _with_tail(x_ref[...])
        g = _with_tail(g_ref[0])
        w = w_ref[...]
        xs = [_shift_down(x, 3), _shift_down(x, 2), _shift_down(x, 1), x]
        c = w[0:1] * xs[0] + w[1:2] * xs[1] + w[2:3] * xs[2] + w[3:4] * xs[3]
        sg = jax.nn.sigmoid(c)
        y = c * sg
        rs = lax.rsqrt(jnp.sum(y * y, axis=-1, keepdims=True) + L2_EPS)
        scale = jnp.where(s < A_HEADS, A_HEAD_DIM ** -0.5, 1.0)
        dy_n = scale * (rs * g - y * (rs * rs * rs) * jnp.sum(g * y, axis=-1, keepdims=True))
        dy = jnp.where(s < 2 * A_HEADS, dy_n, g)
        dc = dy * (sg * (1.0 + c * (1.0 - sg)))
        dx = w[3:4] * dc + w[2:3] * _shift_up(dc, 1) + w[1:2] * _shift_up(dc, 2) + w[0:1] * _shift_up(dc, 3)
        dx_ref[...] = dx[:t_len].astype(dx_ref.dtype)
        dws = [jnp.sum(dc * xs[j], axis=0, keepdims=True) for j in range(CONV_K)]
        dw_ref[...] = jnp.concatenate(dws + [jnp.zeros((8 - CONV_K, LANE), F32)], axis=0)

    slab = pl.BlockSpec((t_len, LANE), lambda s: (0, L_QKV // LANE + s))
    return pl.pallas_call(
        body, name=name, grid=(12,),
        in_specs=[slab, pl.BlockSpec((8, LANE), lambda s: (0, s)),
                  pl.BlockSpec((1, t_len, LANE), lambda s: (s // A_HEADS, 0, s % A_HEADS)), _ANY],
        out_specs=[slab, pl.BlockSpec((8, LANE), lambda s: (0, s))],
        out_shape=[jax.ShapeDtypeStruct((t_len, L_COLS), MM_DTYPE), jax.ShapeDtypeStruct((8, 3 * A_WIDTH), F32)],
        input_output_aliases={3: 0},
        compiler_params=_cparams(("parallel",)),
    )(h, conv_w, d_out, dh)


N_LEVELS = 5
MF_TRIL, MF_STRIL, MF_DIAG8, MF_LOW16, MF_EYE = 0, 1, 2, 3, 3 + N_LEVELS
MB_CUM, MB_CUM_T, MB_TOT = 0, 1, 2


def _gdn_masks():
    r = lax.broadcasted_iota(jnp.int32, (SUPER, SUPER), 0)
    c = lax.broadcasted_iota(jnp.int32, (SUPER, SUPER), 1)
    same = lambda shift: (r >> shift) == (c >> shift)
    ninf = lambda m: jnp.where(m, 0.0, -jnp.inf).astype(F32)
    one = lambda m: m.astype(F32)
    mf = jnp.stack([ninf(r >= c), ninf(r > c), one(same(3))]
                   + [one(same(4 + lv) & jnp.logical_not(same(3 + lv))) for lv in range(N_LEVELS)] + [one(r == c)])
    mb = jnp.stack([one(r >= c), one(r <= c), jnp.ones((SUPER, SUPER), F32)]).astype(BF16)
    return mf, mb


def _tri_inv_impl(a, mf):
    d = lambda p, q: jnp.dot(p.astype(BF16), q.astype(BF16), preferred_element_type=F32)
    dd = lambda p, q: jnp.dot(p, q, preferred_element_type=F32)
    eye = mf[MF_EYE]
    a0 = a * mf[MF_DIAG8]
    a2 = d(a0, a0)
    a4 = d(a2, a2)
    t = d(d(eye - a0, eye + a2), eye + a4)
    for level in range(N_LEVELS):
        t = t - d(d(t, a * mf[MF_LOW16 + level]), t)
    a_hi, a_lo = _split(a)
    for _ in range(NEWTON_STEPS):
        t0 = t.astype(BF16)
        t0f = t0.astype(F32)
        resid = (eye - t0f) - (dd(a_hi, t0) + dd(a_lo, t0))
        r_hi, r_lo = _split(resid)
        t = t0f + (dd(t0, r_hi) + dd(t0, r_lo))
    return t


@jax.custom_vjp
def _wy_apply(a, rhs, t):
    return _mm(t, rhs)


def _wy_apply_fwd(a, rhs, t):
    x = _mm(t, rhs)
    return x, (t, x)


def _wy_apply_bwd(res, dx):
    t, x = res
    d_rhs = _mm_tn(t, dx)
    return -_mm_nt(d_rhs, x), d_rhs, jnp.zeros_like(t)


_wy_apply.defvjp(_wy_apply_fwd, _wy_apply_bwd)


@functools.partial(jax.custom_vjp, nondiff_argnums=(1,))
def _lane_roll(x, shift):
    return pltpu.roll(x, shift % LANE, 1)


_lane_roll.defvjp(lambda x, shift: (_lane_roll(x, shift), None), lambda shift, _, g: (_lane_roll(g, -shift),))


def _mask_times_lanes(x, mask):
    lane = lax.broadcasted_iota(jnp.int32, (1, LANE), 1)
    x = jnp.where(lane < A_HEADS, x, 0.0)
    x1 = x.astype(BF16).astype(F32)
    x2 = (x - x1).astype(BF16).astype(F32)
    x3 = (x - x1 - x2).astype(BF16).astype(F32)
    pieces = x1 + pltpu.roll(x2, A_HEADS, 1) + pltpu.roll(x3, 2 * A_HEADS, 1)
    res = jnp.dot(mask, pieces.astype(BF16), preferred_element_type=F32)
    return res + pltpu.roll(res, LANE - A_HEADS, 1) + pltpu.roll(res, LANE - 2 * A_HEADS, 1)


@jax.custom_vjp
def _chunk_sums(g, mb):
    return _mask_times_lanes(g, mb[MB_CUM]), _mask_times_lanes(g, mb[MB_TOT])


def _chunk_sums_fwd(g, mb):
    return _chunk_sums(g, mb), mb


def _chunk_sums_bwd(mb, d):
    lane = lax.broadcasted_iota(jnp.int32, (1, LANE), 1)
    dg = _mask_times_lanes(d[0], mb[MB_CUM_T]) + _mask_times_lanes(d[1], mb[MB_TOT])
    return jnp.where(lane < A_HEADS, dg, 0.0), jnp.zeros_like(mb)


_chunk_sums.defvjp(_chunk_sums_fwd, _chunk_sums_bwd)


def _gdn_gates(ba, alog, dtb, mb):
    beta = jax.nn.sigmoid(ba)
    g = -jnp.exp(alog) * _softplus(_lane_roll(ba, -A_HEADS) + dtb)
    gc, gl = _chunk_sums(g, mb)
    return beta, gc, gl, gc.T


def _gdn_block(s, q, k, v, z, gates, nw, h, t_known, mf):
    n = q.shape[0]
    beta_all, gc_all, gl_all, gct_all = gates
    lane = lax.broadcasted_iota(jnp.int32, (1, LANE), 1)
    sub = lax.broadcasted_iota(jnp.int32, (LANE, 1), 0)
    col = lambda x: jnp.sum(jnp.where(lane == h, x, 0.0), axis=1, keepdims=True)
    wide = lambda c: jnp.broadcast_to(c, (n, LANE))
    gc, gl = col(gc_all), col(gl_all)
    gc_row = jnp.sum(jnp.where(sub == h, gct_all, 0.0), axis=0, keepdims=True)
    beta_w, eg_w = wide(col(beta_all)), wide(jnp.exp(gc))
    diff = gc - gc_row
    decay = jnp.exp(diff + mf[MF_TRIL])
    kb = k * beta_w
    a_mat = _mm_nt(kb, k) * jnp.exp(diff + mf[MF_STRIL])
    rhs = jnp.concatenate([v * beta_w, kb * eg_w], axis=1)
    if t_known is None:
        t_mat = _tri_inv_impl(a_mat, mf)
        uw = _mm(t_mat, rhs)
    else:
        t_mat = t_known
        uw = _wy_apply(a_mat, rhs, t_known)
    u, w = uw[:, :LANE], uw[:, LANE:]
    qk = _mm_nt(q, k) * decay
    q_dec = q * eg_w
    k_dec = k * wide(jnp.exp(gl - gc))
    v_new = u - _mm(w, s)
    o = _mm(q_dec, s) + _mm(qk, v_new)
    s = s * jnp.exp(gl[0:1]) + _mm_tn(k_dec, v_new)
    o = o * lax.rsqrt(jnp.mean(o * o, axis=-1, keepdims=True) + RMS_EPS) * nw
    return o * _silu(z), s, t_mat


def _gdn_fwd(qkv, h, alog, dtb, nw, ycat, *, name, rider=None):
    t_len = qkv.shape[0]
    nsc = t_len // SUPER

    def core(ins, outs, scr):
        q_ref, k_ref, v_ref, gate_ref, al_ref, dt_ref, nw_ref, mf_ref, mb_ref, _ = ins
        y_ref, sin_ref, t_ref = outs
        s_scr, = scr

        @pl.when(pl.program_id(0) == 0)
        def _():
            s_scr[...] = jnp.zeros_like(s_scr)

        per_head = lambda ref: jnp.stack([ref[:, hh * LANE:(hh + 1) * LANE] for hh in range(A_HEADS)])
        states = s_scr[...]
        gates = _gdn_gates(gate_ref[:, A_WIDTH:], al_ref[...], dt_ref[...], mb_ref[...])
        fn = jax.vmap(_gdn_block, in_axes=(0, 0, 0, 0, 0, None, None, 0, None, None))
        y, s_new, t_mat = fn(states, per_head(q_ref), per_head(k_ref), per_head(v_ref), per_head(gate_ref),
                             gates, nw_ref[...], jnp.arange(A_HEADS), None, mf_ref[...])
        sin_ref[0] = states
        t_ref[0] = t_mat
        s_scr[...] = s_new
        for hh in range(A_HEADS):
            y_ref[:, hh * LANE:(hh + 1) * LANE] = y[hh]

    blk = lambda j: pl.BlockSpec((SUPER, A_WIDTH), lambda sc: (sc, j))
    row = pl.BlockSpec((1, LANE), lambda sc: (0, 0))
    mf, mb = _gdn_masks()
    whole = lambda a: pl.BlockSpec(a.shape, lambda sc: (0, 0, 0))
    return _pcall(
        core, name=name, grid=(nsc,),
        in_specs=[blk(0), blk(1), blk(2), pl.BlockSpec((SUPER, L_GATE), lambda sc: (sc, L_ZA // L_GATE)),
                  row, row, row, whole(mf), whole(mb), _ANY],
        out_specs=[blk(0),
                   pl.BlockSpec((1, A_HEADS, A_HEAD_DIM, A_HEAD_DIM), lambda sc: (sc, 0, 0, 0)),
                   pl.BlockSpec((1, A_HEADS, SUPER, SUPER), lambda sc: (sc, 0, 0, 0))],
        out_shape=[jax.ShapeDtypeStruct((t_len, D_MODEL), F32),
                   jax.ShapeDtypeStruct((nsc, A_HEADS, A_HEAD_DIM, A_HEAD_DIM), F32),
                   jax.ShapeDtypeStruct((nsc, A_HEADS, SUPER, SUPER), F32)],
        scratch_shapes=[pltpu.VMEM((A_HEADS, A_HEAD_DIM, A_HEAD_DIM), F32)],
        aliases={9: 0}, sem=("arbitrary",), rider=rider,
        args=(qkv, qkv, qkv, h, alog, dtb, nw, mf, mb, ycat))


def _gdn_bwd(qkv, h, alog, dtb, nw, s_in, t_in, dycat, dh, *, name, rider=None):
    t_len = qkv.shape[0]
    nsc = t_len // SUPER

    def core(ins, outs, scr):
        q_ref, k_ref, v_ref, gate_ref, al_ref, dt_ref, nw_ref, sin_ref, t_ref, dy_ref, mf_ref, mb_ref, _ = ins
        dgate_ref, dqkv_ref, dal_ref, ddt_ref, dnw_ref = outs
        ds_scr, = scr

        @pl.when(pl.program_id(0) == 0)
        def _():
            ds_scr[...] = jnp.zeros_like(ds_scr)
            dal_ref[...] = jnp.zeros_like(dal_ref)
            ddt_ref[...] = jnp.zeros_like(ddt_ref)
            dnw_ref[...] = jnp.zeros_like(dnw_ref)

        per_head = lambda ref: jnp.stack([ref[:, hh * LANE:(hh + 1) * LANE] for hh in range(A_HEADS)])
        head_ids = jnp.arange(A_HEADS)
        t_known, mf, mb = t_ref[0], mf_ref[...], mb_ref[...]

        def fn(s, q, k, v, z, ba, alog, dtb, nw):
            gates = _gdn_gates(ba, alog, dtb, mb)
            one = lambda s, q, k, v, z, t, h: _gdn_block(s, q, k, v, z, gates, nw, h, t, mf)[:2]
            return jax.vmap(one)(s, q, k, v, z, t_known, head_ids)

        _, vjp = jax.vjp(fn, sin_ref[0], per_head(q_ref), per_head(k_ref), per_head(v_ref), per_head(gate_ref),
                         gate_ref[:, A_WIDTH:], al_ref[...], dt_ref[...], nw_ref[...])
        ds, dq, dk, dv, dz, dba, dal, ddt, dnw = vjp((per_head(dy_ref), ds_scr[...]))
        ds_scr[...] = ds
        for hh in range(A_HEADS):
            cols = slice(hh * LANE, (hh + 1) * LANE)
            dqkv_ref[0, :, cols] = dq[hh]
            dqkv_ref[1, :, cols] = dk[hh]
            dqkv_ref[2, :, cols] = dv[hh]
            dgate_ref[:, cols] = dz[hh].astype(dgate_ref.dtype)
        dgate_ref[:, A_WIDTH:] = dba.astype(dgate_ref.dtype)
        dal_ref[...] += dal
        ddt_ref[...] += ddt
        dnw_ref[...] += dnw

    rev = lambda i: nsc - 1 - i
    blk = lambda j: pl.BlockSpec((SUPER, A_WIDTH), lambda i: (rev(i), j))
    gate = pl.BlockSpec((SUPER, L_GATE), lambda i: (rev(i), L_ZA // L_GATE))
    row = pl.BlockSpec((1, LANE), lambda i: (0, 0))
    mf, mb = _gdn_masks()
    whole = lambda a: pl.BlockSpec(a.shape, lambda i: (0, 0, 0))
    return _pcall(
        core, name=name, grid=(nsc,),
        in_specs=[blk(0), blk(1), blk(2), gate, row, row, row,
                  pl.BlockSpec((1, A_HEADS, A_HEAD_DIM, A_HEAD_DIM), lambda i: (rev(i), 0, 0, 0)),
                  pl.BlockSpec((1, A_HEADS, SUPER, SUPER), lambda i: (rev(i), 0, 0, 0)),
                  blk(0), whole(mf), whole(mb), _ANY],
        out_specs=[gate, pl.BlockSpec((3, SUPER, A_WIDTH), lambda i: (0, rev(i), 0)), row, row, row],
        out_shape=[jax.ShapeDtypeStruct((t_len, L_COLS), MM_DTYPE), jax.ShapeDtypeStruct((3, t_len, A_WIDTH), F32)]
        + [jax.ShapeDtypeStruct((1, LANE), F32)] * 3,
        scratch_shapes=[pltpu.VMEM((A_HEADS, A_HEAD_DIM, A_HEAD_DIM), F32)],
        aliases={12: 0}, sem=("arbitrary",), rider=rider,
        args=(qkv, qkv, qkv, h, alog, dtb, nw, s_in, t_in, dycat, mf, mb, dh))


Q_BLOCKS = 4
Q_ROWS = Q_BLOCKS * BLOCK


def _swa_block(q, kp, kc, vp, vc, z, sinks, first):
    rows = B_GROUP * BLOCK
    ri = lax.broadcasted_iota(jnp.int32, (rows, 2 * BLOCK), 0)
    si = lax.broadcasted_iota(jnp.int32, (rows, 2 * BLOCK), 1)
    dist = (ri & (BLOCK - 1)) + BLOCK - si
    bias = jnp.where((dist >= 0) & (dist < WINDOW), 0.0, -jnp.inf)
    no_prev = jnp.where(first & (si[:1] < BLOCK), -jnp.inf, 0.0)
    dist_f = dist.astype(F32)
    head_of_row = lax.broadcasted_iota(jnp.int32, (rows, 1), 0) >> 7
    keys = jnp.concatenate([kp, kc], axis=0)
    vals = jnp.concatenate([vp, vc], axis=0)

    def item(b, j):
        cs = slice(j * B_HEAD_DIM, (j + 1) * B_HEAD_DIM)
        rs = slice(b * BLOCK, (b + 1) * BLOCK)
        heads = range(j * B_GROUP, (j + 1) * B_GROUP)
        qs = jnp.concatenate([q[rs, hq * B_HEAD_DIM:(hq + 1) * B_HEAD_DIM] for hq in heads], axis=0) * (
            B_HEAD_DIM ** -0.5)
        kk = keys[b * BLOCK:(b + 2) * BLOCK, cs]
        vv = vals[b * BLOCK:(b + 2) * BLOCK, cs]
        sink = jnp.concatenate([jnp.broadcast_to(sinks[:, hq:hq + 1], (BLOCK, 1)) for hq in heads], axis=0)
        slope = sum(jnp.where(head_of_row == gi, 2.0 ** (-8.0 * (hq + 1) / B_Q_HEADS), 0.0)
                    for gi, hq in enumerate(heads))
        return qs, kk, vv, sink, slope, (no_prev if b == 0 else jnp.zeros_like(no_prev))

    def attend(qs, kk, vv, sink, slope, hide):
        sc = _mm_nt(qs, kk) - slope * dist_f + (bias + hide)
        m = lax.stop_gradient(jnp.maximum(jnp.max(sc, axis=-1, keepdims=True), sink))
        p = jnp.exp(sc - m)
        inv = 1.0 / (jnp.sum(p, axis=-1, keepdims=True) + jnp.exp(sink - m))
        return _mm(p * inv, vv)

    items = [(b, j) for b in range(Q_BLOCKS) for j in range(B_KV_HEADS)]
    o = jax.vmap(attend)(*[_stack(t) for t in zip(*[item(b, j) for b, j in items])])
    rows_out = [jnp.concatenate([o[b * B_KV_HEADS + j, gi * BLOCK:(gi + 1) * BLOCK]
                                 for j in range(B_KV_HEADS) for gi in range(B_GROUP)], axis=1)
                for b in range(Q_BLOCKS)]
    return jnp.concatenate(rows_out, axis=0) * _silu(z)


def _swa_specs(idx):
    wide = lambda off: pl.BlockSpec((Q_ROWS, B_WIDTH), lambda n: (idx(n), off))
    cur = lambda off: pl.BlockSpec((Q_ROWS, LANE), lambda n: (idx(n), off))
    prev = lambda off: pl.BlockSpec((BLOCK, LANE), lambda n: (jnp.maximum(idx(n) * Q_BLOCKS - 1, 0), off))
    return [wide(L_QB // B_WIDTH), prev(L_KB // LANE), cur(L_KB // LANE), prev(L_VB // LANE), cur(L_VB // LANE),
            wide(L_ZB // B_WIDTH), pl.BlockSpec((1, LANE), lambda n: (0, 0))]


def _swa_fwd(h, sinks, *, name, rider=None):
    t_len = h.shape[0]
    nb = t_len // Q_ROWS

    def core(ins, outs, _):
        q_ref, kp_ref, kc_ref, vp_ref, vc_ref, z_ref, s_ref = ins
        outs[0][...] = _swa_block(q_ref[...], kp_ref[...], kc_ref[...], vp_ref[...], vc_ref[...], z_ref[...],
                                  s_ref[...], pl.program_id(0) == 0)

    res = _pcall(core, name=name, grid=(nb,), in_specs=_swa_specs(lambda n: n),
                 out_specs=[pl.BlockSpec((Q_ROWS, B_WIDTH), lambda n: (n, 1))],
                 out_shape=[jax.ShapeDtypeStruct((t_len, D_MODEL), F32)], sem=("parallel",), rider=rider,
                 args=(h, h, h, h, h, h, sinks))
    return res if rider else res[0]


def _swa_bwd(h, sinks, dycat, *, name, rider=None):
    t_len = h.shape[0]
    nb = t_len // Q_ROWS
    early = slice(0, Q_ROWS - BLOCK)
    last = slice(Q_ROWS - BLOCK, Q_ROWS)

    def core(ins, outs, scr):
        q_ref, kp_ref, kc_ref, vp_ref, vc_ref, z_ref, s_ref, dy_ref = ins
        dh_ref, dsk_ref = outs
        ck_scr, cv_scr = scr
        i = pl.program_id(0)
        n = nb - 1 - i

        @pl.when(i == 0)
        def _():
            ck_scr[...] = jnp.zeros_like(ck_scr)
            cv_scr[...] = jnp.zeros_like(cv_scr)
            dsk_ref[...] = jnp.zeros_like(dsk_ref)

        fn = functools.partial(_swa_block, first=(n == 0))
        _, vjp = jax.vjp(fn, q_ref[...], kp_ref[...], kc_ref[...], vp_ref[...], vc_ref[...], z_ref[...], s_ref[...])
        dq, dkp, dkc, dvp, dvc, dz, dsk = vjp(dy_ref[...])
        def put(rows, col, val):
            dh_ref[rows, col:col + val.shape[1]] = val.astype(dh_ref.dtype)

        put(slice(None), L_QB, dq)
        put(slice(None), L_ZB, dz)
        put(early, L_KB, dkc[early])
        put(early, L_VB, dvc[early])
        put(last, L_KB, dkc[last] + ck_scr[...])
        put(last, L_VB, dvc[last] + cv_scr[...])
        ck_scr[...] = dkp
        cv_scr[...] = dvp
        dsk_ref[...] += dsk

    rev = lambda i: nb - 1 - i
    return _pcall(
        core, name=name, grid=(nb,),
        in_specs=_swa_specs(rev) + [pl.BlockSpec((Q_ROWS, B_WIDTH), lambda i: (rev(i), 1))],
        out_specs=[pl.BlockSpec((Q_ROWS, L_SWA), lambda i: (rev(i), 0)), pl.BlockSpec((1, LANE), lambda i: (0, 0))],
        out_shape=[jax.ShapeDtypeStruct((t_len, L_COLS), MM_DTYPE), jax.ShapeDtypeStruct((1, LANE), F32)],
        scratch_shapes=[pltpu.VMEM((BLOCK, LANE), F32), pltpu.VMEM((BLOCK, LANE), F32)],
        sem=("arbitrary",), rider=rider, args=(h, h, h, h, h, h, sinks, dycat))


def _out_ln_fwd(ycat, w_out, x, ln_g, ln_b, *, name, tm=512, last=False):
    t_len = x.shape[0]

    def body(y_ref, w_ref, x_ref, g_ref, b_ref, r_ref, *o_ref):
        r = DEEPNORM_ALPHA * x_ref[...] + _mm(y_ref[...], w_ref[...])
        r_ref[...] = r
        if not last:
            mu = jnp.mean(r, axis=-1, keepdims=True)
            d = r - mu
            var = jnp.mean(d * d, axis=-1, keepdims=True)
            o_ref[0][...] = d * lax.rsqrt(var + LN_EPS) * g_ref[...] + b_ref[...]

    tile = pl.BlockSpec((tm, D_MODEL), lambda i: (i, 0))
    vec = pl.BlockSpec((1, D_MODEL), lambda i: (0, 0))
    n_out = 1 if last else 2
    res = pl.pallas_call(
        body, name=name, grid=(t_len // tm,),
        in_specs=[tile, pl.BlockSpec((D_MODEL, D_MODEL), lambda i: (0, 0)), tile, vec, vec],
        out_specs=[tile] * n_out,
        out_shape=[jax.ShapeDtypeStruct((t_len, D_MODEL), F32)] * n_out,
        compiler_params=_cparams(("parallel",)),
    )(ycat, w_out, x, ln_g, ln_b)
    return (res[0], None) if last else res


def _ln_bwd(dxn, r, ln_g, *, name, tm=512, loss=None):
    t_len = r.shape[0]

    def body(*refs):
        if loss:
            t_ref, r_ref, g_ref, b_ref, dr_ref, dg_ref, db_ref, l_ref = refs
        else:
            dx_ref, r_ref, g_ref, dr_ref, dg_ref, db_ref = refs

        @pl.when(pl.program_id(0) == 0)
        def _():
            dg_ref[...] = jnp.zeros_like(dg_ref)
            db_ref[...] = jnp.zeros_like(db_ref)
            if loss:
                l_ref[...] = jnp.zeros_like(l_ref)

        rr = r_ref[...]
        mu = jnp.mean(rr, axis=-1, keepdims=True)
        d = rr - mu
        rstd = lax.rsqrt(jnp.mean(d * d, axis=-1, keepdims=True) + LN_EPS)
        xh = d * rstd
        if loss:
            e = (xh * g_ref[...] + b_ref[...]) - t_ref[...]
            dx = e * (1.0 / D_MODEL)
            l_ref[...] += jnp.sum(e * e, axis=0, keepdims=True)
        else:
            dx = dx_ref[...]
        dxh = dx * g_ref[...]
        dr_ref[...] = rstd * (dxh - jnp.mean(dxh, axis=-1, keepdims=True)
                              - xh * jnp.mean(dxh * xh, axis=-1, keepdims=True))
        dg_ref[...] += jnp.sum(dx * xh, axis=0, keepdims=True)
        db_ref[...] += jnp.sum(dx, axis=0, keepdims=True)

    tile = pl.BlockSpec((tm, D_MODEL), lambda i: (i, 0))
    vec = pl.BlockSpec((1, D_MODEL), lambda i: (0, 0))
    vec_shape = jax.ShapeDtypeStruct((1, D_MODEL), F32)
    args = (loss[0], r, ln_g, loss[1]) if loss else (dxn, r, ln_g)
    return pl.pallas_call(
        body, name=name, grid=(t_len // tm,),
        in_specs=[tile, tile, vec] + ([vec] if loss else []), out_specs=[tile, vec, vec] + ([vec] if loss else []),
        out_shape=[jax.ShapeDtypeStruct((t_len, D_MODEL), F32), vec_shape, vec_shape] + ([vec_shape] if loss else []),
        compiler_params=_cparams(("arbitrary",)),
    )(*args)


def _pad_row(v):
    return jnp.zeros((1, LANE), F32).at[0, :v.shape[0]].set(v)


_REGIONS = ((0, 1536, L_QKV), (1536, 2048, L_ZA), (2048, 2056, L_BA), (2056, 2568, L_QB), (2568, 2696, L_KB),
            (2696, 2824, L_VB), (2824, 3336, L_ZB))


def _shard_pieces(regions):
    for a, b, off in regions:
        for d in range(N_DEV):
            lo, hi = max(a, d * SHARD_COLS), min(b, (d + 1) * SHARD_COLS)
            if lo < hi:
                yield d, lo - d * SHARD_COLS, hi - d * SHARD_COLS, off + lo - a


def _as_list(r):
    return list(r) if isinstance(r, (list, tuple)) else [r]


def _gathered(shard):
    return jax.ShapeDtypeStruct((N_DEV,) + shard.shape, shard.dtype)


def _full_w_in(g_in, name):
    by_offset = sorted(_shard_pieces(_REGIONS), key=lambda p: p[3])
    tc = 256

    def body(g_ref, o_ref):
        pieces, row = [], 0
        for d, lo, hi, off in by_offset + [(None, 0, 0, L_COLS)]:
            if off > row:
                pieces.append(jnp.zeros((off - row, tc), g_ref.dtype))
            if d is not None:
                pieces.append(g_ref[d, lo:hi, :])
            row = off + hi - lo
        o_ref[...] = jnp.concatenate(pieces, axis=0)

    return pl.pallas_call(
        body, name=name, grid=(D_MODEL // tc,),
        in_specs=[pl.BlockSpec((N_DEV, SHARD_COLS, tc), lambda i: (0, 0, i))],
        out_specs=pl.BlockSpec((L_COLS, tc), lambda i: (0, i)),
        out_shape=jax.ShapeDtypeStruct((L_COLS, D_MODEL), g_in.dtype),
        compiler_params=_cparams(("parallel",)),
    )(g_in)


def _full_conv(g_conv):
    return jnp.pad(g_conv.transpose(1, 0, 2).reshape(CONV_K, 3 * A_WIDTH), ((0, 8 - CONV_K), (0, 0)))


def _forward(x, weights, shards, small):
    a_log, dt_bias, norm_w, sinks, ln_g, ln_b = small
    tm = min(512, x.shape[0])
    saved, weights = [], [list(w) for w in weights]
    whole = lambda arrs: _Direct([(a, False, j, ()) for j, a in enumerate(arrs)], [_gathered(a) for a in arrs])
    for l in range(DEPTH):
        rider = whole(shards[l][1:]) if weights[l][1] is None else None
        h, *got = _as_list(_matmul(x, weights[l][0], form="nt", tm=tm, tn=L_COLS, tk=D_MODEL, name=f"in_proj_{l}",
                                   rider=rider))
        if rider:
            weights[l][1:] = [got[0].reshape(D_MODEL, D_MODEL), _full_conv(got[1])]
        w_in_l, w_out_l, conv_l = weights[l]
        qkv = _prep_fwd(h, conv_l, name=f"prep_fwd_{l}")
        al, dt, nw, sk = _pad_row(a_log[l]), _pad_row(dt_bias[l]), norm_w[l][None, :], _pad_row(sinks[l])
        ahead = l + 1 < DEPTH and weights[l + 1][0] is None
        rider = whole(shards[l + 1][1:]) if ahead else None
        ycat, *got = _as_list(_swa_fwd(h, sk, name=f"swa_fwd_{l}", rider=rider))
        if ahead:
            weights[l + 1][1:] = [got[0].reshape(D_MODEL, D_MODEL), _full_conv(got[1])]
        rider = whole(shards[l + 1][:1]) if ahead else None
        ycat, s_in, t_in, *got = _gdn_fwd(qkv, h, al, dt, nw, ycat, name=f"gdn_fwd_{l}", rider=rider)
        if ahead:
            weights[l + 1][0] = _full_w_in(got[0], f"w_in_rows_{l + 1}")
        r, xn = _out_ln_fwd(ycat, w_out_l, x, ln_g[l][None, :], ln_b[l][None, :], name=f"out_ln_{l}",
                            last=(l == DEPTH - 1))
        saved.append((x, h, qkv, s_in, t_in, ycat, r, al, dt, nw, sk))
        x = xn
    return x, saved, weights


def _w_in_blocks(g, name):
    cols, tc = g.shape[1], 256
    pieces = list(_shard_pieces(_REGIONS))

    def body(g_ref, o_ref):
        blocks = [[] for _ in range(N_DEV)]
        for d, lo, hi, off in pieces:
            blocks[d].append(g_ref[off:off + hi - lo, :])
        for d in range(N_DEV):
            o_ref[d] = jnp.concatenate(blocks[d], axis=0).astype(BF16)

    return pl.pallas_call(
        body, name=name, grid=(cols // tc,),
        in_specs=[pl.BlockSpec((L_COLS, tc), lambda i: (0, i))],
        out_specs=pl.BlockSpec((N_DEV, SHARD_COLS, tc), lambda i: (0, 0, i)),
        out_shape=jax.ShapeDtypeStruct((N_DEV, SHARD_COLS, cols), BF16),
        compiler_params=_cparams(("parallel",)),
    )(g)


def _small_blocks(g):
    c_conv = g["conv_w"].reshape(CONV_K, N_DEV, CONV_SHARD_COLS).transpose(1, 0, 2)
    c_small = [jnp.broadcast_to(g[n][None], (N_DEV,) + g[n].shape) for n, _ in SMALL_SIZES]
    return _pack_small(c_conv, c_small)


def _contributions(g, loss_part):
    c_out = g["w_out"].astype(BF16).reshape(N_DEV, OUT_SHARD_ROWS, D_MODEL)
    c_small = _small_blocks(g).at[:, CS_ROWS - 1, LANE - 1].set(loss_part)
    return _w_in_blocks(g["w_in_rows"], name="w_in_grad_blocks_above"), c_out, c_small


def _backward_layer(l, dx, saved_l, weights_l, ln_g_l, above=None, loss=None):
    x_in, h, qkv, s_in, t_in, ycat, r, al, dt, nw, sk = saved_l
    w_in_l, w_out_l, conv_l = weights_l
    tm = min(512, x_in.shape[0])
    dr, d_lng, d_lnb, *loss_lanes = _ln_bwd(dx, r, ln_g_l[None, :], name=f"ln_bwd_{l}", loss=loss)
    big = min(1024, x_in.shape[0])
    dycat = _matmul(dr, w_out_l, form="nt", tm=big, tn=D_MODEL, tk=D_MODEL, name=f"out_proj_dx_{l}")
    d_wout = _matmul(ycat, dr, form="tn", tm=D_MODEL, tn=D_MODEL, tk=big, name=f"out_proj_dw_{l}")
    rider, p_in, p_out, p_small = None, None, None, None
    recv = lambda c: jax.ShapeDtypeStruct((DEPTH,) + c.shape, c.dtype)
    if above:
        c_out = d_wout.astype(BF16).reshape(N_DEV, OUT_SHARD_ROWS, D_MODEL)
        rider = _Direct([(above[1], True, 0, (l + 1,)), (above[2], True, 1, (l + 1,)), (c_out, True, 0, (l,))],
                        [recv(above[1]), recv(above[2])])
    dh, d_sk, *got = _swa_bwd(h, sk, dycat, name=f"swa_bwd_{l}", rider=rider)
    if above:
        p_out, p_small = got
        rider = _Direct([(above[0], True, 0, (l + 1,))], [recv(above[0])])
    dh, dqkv_n, d_al, d_dt, d_nw, *got = _gdn_bwd(qkv, h, al, dt, nw, s_in, t_in, dycat, dh,
                                                  name=f"gdn_bwd_{l}", rider=rider)
    dh, d_conv = _prep_bwd(h, conv_l, dqkv_n, dh, name=f"prep_bwd_{l}")
    grads = dict(w_out=d_wout, conv_w=d_conv[:CONV_K], a_log=d_al[0, :A_HEADS], dt_bias=d_dt[0, :A_HEADS],
                 norm_w=d_nw[0], sinks=d_sk[0, :B_Q_HEADS], ln_g=d_lng[0], ln_b=d_lnb[0])
    dw = functools.partial(_matmul, dh, x_in, form="tn")
    if not above:
        grads["w_in_rows"] = dw(name=f"in_proj_dw_{l}", tm=L_COLS // 3, tn=D_MODEL, tk=min(2048, x_in.shape[0]))
    else:
        p_in, = got
        cut = D_MODEL // 2
        rest = D_MODEL - cut
        first = dw(name=f"in_proj_dw_first_{l}", tm=L_COLS, tn=cut, tk=big, b_cols=(0, cut))
        blocks = _w_in_blocks(first, name=f"w_in_grad_blocks_first_{l}")
        rider = _Direct([(blocks, True, 0, (l,), (slice(None), pl.ds(0, cut)))], [p_in])
        second, p_in = dw(name=f"in_proj_dw_second_{l}", tm=L_COLS, tn=cut, tk=big, b_cols=(cut, rest), rider=rider)
        blocks = _w_in_blocks(second, name=f"w_in_grad_blocks_second_{l}")
        rider = _Direct([(blocks, True, 0, (l,), (slice(None), pl.ds(cut, rest))),
                         (_small_blocks(grads), True, 1, (l,))], [p_in, p_small])
    dx, *got = _as_list(_matmul(dh, w_in_l, form="nn", tm=tm, tn=D_MODEL, tk=L_COLS, name=f"in_proj_dx_{l}",
                                add=dr, add_scale=DEEPNORM_ALPHA, rider=rider))
    bufs = (got[0], p_out, got[1]) if above else None
    return dx, grads, bufs, (loss_lanes[0] if loss else None)


def _all_gather(shards, *, name):
    n_arr = len(shards)

    def body(*refs):
        x_refs, out_refs = refs[:n_arr], refs[n_arr:2 * n_arr]
        send_sems, recv_sems, local_sems = refs[2 * n_arr:]
        x, y, c = _me()
        me, sibling = (x, y, c), (x, y, 1 - c)
        chips = [(1 - x, y), (x, 1 - y), (1 - x, 1 - y)]

        def copy(a, k, block, to, src=None):
            dst = out_refs[a].at[_flat_id(block)]
            return _remote(dst if src is None else src, dst, send_sems.at[a, k], recv_sems.at[a, k], to)

        mine = [pltpu.make_async_copy(x_refs[a], out_refs[a].at[_flat_id(me)], local_sems.at[a])
                for a in range(n_arr)]
        for cp in mine:
            cp.start()
        first = []
        for a in range(n_arr):
            first.append(copy(a, 0, me, sibling, src=x_refs[a]))
            first += [copy(a, 1 + j, me, (*chip, c), src=x_refs[a]) for j, chip in enumerate(chips)]
        for cp in first:
            cp.start()
        passed = []
        for j, chip in enumerate(chips):
            for a in range(n_arr):
                copy(a, 1 + j, (*chip, c), me).wait_recv()
                fwd = copy(a, 4 + j, (*chip, c), sibling)
                fwd.start()
                passed.append(fwd)
        for a in range(n_arr):
            copy(a, 0, sibling, me).wait_recv()
            for j, chip in enumerate(chips):
                copy(a, 4 + j, (*chip, 1 - c), me).wait_recv()
        for cp in first + passed:
            cp.wait_send()
        for cp in mine:
            cp.wait()

    return pl.pallas_call(
        body, name=name, in_specs=[_ANY] * n_arr, out_specs=[_ANY] * n_arr,
        out_shape=[jax.ShapeDtypeStruct((N_DEV,) + s.shape, s.dtype) for s in shards],
        scratch_shapes=[pltpu.SemaphoreType.DMA((n_arr, N_DEV - 1)), pltpu.SemaphoreType.DMA((n_arr, N_DEV - 1)),
                        pltpu.SemaphoreType.DMA((n_arr,))],
    )(*shards)


def _adamw(parts, w, m, v, *, tr, name):
    depth, rows, cols = w.shape
    c1 = 1.0 - ADAM_B1 ** ADAM_STEP
    c2 = 1.0 - ADAM_B2 ** ADAM_STEP

    def body(g_ref, w_ref, m_ref, v_ref, go_ref, d_ref, mo_ref, vo_ref):
        g = g_ref[0, 0].astype(F32)
        for s in range(1, N_DEV):
            g = g + g_ref[0, s].astype(F32)
        m_new = ADAM_B1 * m_ref[0] + (1.0 - ADAM_B1) * g
        v_new = ADAM_B2 * v_ref[0] + (1.0 - ADAM_B2) * (g * g)
        go_ref[0] = g
        mo_ref[0] = m_new
        vo_ref[0] = v_new
        d_ref[0] = -ADAM_LR * ((m_new / c1) / (jnp.sqrt(v_new / c2) + ADAM_EPS) + ADAM_WD * w_ref[0])

    tile = pl.BlockSpec((1, tr, cols), lambda l, i: (l, i, 0))
    return pl.pallas_call(
        body, name=name, grid=(depth, rows // tr),
        in_specs=[pl.BlockSpec((1, N_DEV, tr, cols), lambda l, i: (l, 0, i, 0)), tile, tile, tile],
        out_specs=[tile] * 4, out_shape=[jax.ShapeDtypeStruct(w.shape, F32)] * 4,
        compiler_params=_cparams(("parallel", "parallel")),
    )(parts, w, m, v)


def _adamw_w_in(parts, w, m, v, *, name):
    c1 = 1.0 - ADAM_B1 ** ADAM_STEP
    c2 = 1.0 - ADAM_B2 ** ADAM_STEP

    def body(g_ref, w_ref, m_ref, v_ref, go_ref, d_ref, mo_ref, vo_ref):
        for l in range(DEPTH):
            g = g_ref[l, 0].astype(F32)
            for s in range(1, N_DEV):
                g = g + g_ref[l, s].astype(F32)
            m_new = ADAM_B1 * m_ref[:, l, :] + (1.0 - ADAM_B1) * g
            v_new = ADAM_B2 * v_ref[:, l, :] + (1.0 - ADAM_B2) * (g * g)
            go_ref[:, l, :] = g
            mo_ref[:, l, :] = m_new
            vo_ref[:, l, :] = v_new
            d_ref[:, l, :] = -ADAM_LR * ((m_new / c1) / (jnp.sqrt(v_new / c2) + ADAM_EPS) + ADAM_WD * w_ref[:, l, :])

    tile = pl.BlockSpec((SHARD_COLS, DEPTH, LANE), lambda i: (0, 0, i))
    return pl.pallas_call(
        body, name=name, grid=(D_MODEL // LANE,),
        in_specs=[pl.BlockSpec((DEPTH, N_DEV, SHARD_COLS, LANE), lambda i: (0, 0, 0, i)), tile, tile, tile],
        out_specs=[tile] * 4, out_shape=[jax.ShapeDtypeStruct(w.shape, F32)] * 4,
        compiler_params=_cparams(("parallel",)),
    )(parts, w, m, v)


def _pack_small(conv, small):
    lead = conv.shape[:-2]
    flat = jnp.concatenate([conv.reshape(lead + (CS_CONV,))] + list(small), axis=-1)
    pad = CS_ROWS * LANE - flat.shape[-1]
    flat = jnp.concatenate([flat, jnp.zeros(lead + (pad,), F32)], axis=-1)
    return flat.reshape(lead + (CS_ROWS, LANE))


def _unpack_small(p):
    flat = p.reshape(DEPTH, CS_ROWS * LANE)
    conv = flat[:, :CS_CONV].reshape(DEPTH, CONV_K, CONV_SHARD_COLS)
    small, off = [], CS_CONV
    for _, n in SMALL_SIZES:
        small.append(flat[:, off:off + n])
        off += n
    return conv, small


def kernel(x, w_in, conv_w, a_log, dt_bias, norm_w, sinks, w_out, ln_g, ln_b, loss_target, m_w_in, m_conv_w, m_a_log, m_dt_bias, m_norm_w, m_sinks, m_w_out, m_ln_g, m_ln_b, v_w_in, v_conv_w, v_a_log, v_dt_bias, v_norm_w, v_sinks, v_w_out, v_ln_g, v_ln_b):
    small = [a_log, dt_bias, norm_w, sinks, ln_g, ln_b]
    w_t, m_t, v_t = (a.transpose(2, 0, 1) for a in (w_in, m_w_in, v_w_in))
    shards = [[w_t[:, l].astype(BF16), w_out[l].astype(BF16), conv_w[l]] for l in range(DEPTH)]
    g_in0, = _all_gather(shards[0][:1], name="weights_all_gather_0")
    weights = [[_full_w_in(g_in0, "w_in_rows_0"), None, None]] + [[None, None, None]] * (DEPTH - 1)

    _, saved, weights = _forward(x[0], weights, shards, small)
    dx, g1, _, loss_lanes = _backward_layer(1, None, saved[1], weights[1], ln_g[1],
                                            loss=(loss_target[0], ln_b[1][None, :]))
    loss_part = 0.5 * jnp.sum(loss_lanes) * (1.0 / D_MODEL)
    dx, _, (p_in, p_out, p_small), _ = _backward_layer(0, dx, saved[0], weights[0], ln_g[0],
                                                       above=_contributions(g1, loss_part))
    loss = functools.reduce(lambda a, b: a + b, [p_small[1, s, CS_ROWS - 1, LANE - 1] for s in range(N_DEV)])

    o_in = [o.transpose(1, 2, 0) for o in _adamw_w_in(p_in, w_t, m_t, v_t, name="adamw_w_in")]
    o_out = _adamw(p_out, w_out, m_w_out, v_w_out, tr=OUT_SHARD_ROWS, name="adamw_w_out")
    o_small = _adamw(p_small, _pack_small(conv_w, small),
                     _pack_small(m_conv_w, [m_a_log, m_dt_bias, m_norm_w, m_sinks, m_ln_g, m_ln_b]),
                     _pack_small(v_conv_w, [v_a_log, v_dt_bias, v_norm_w, v_sinks, v_ln_g, v_ln_b]),
                     tr=CS_ROWS, name="adamw_small")
    outs = []
    for k in range(4):
        cv, sm = _unpack_small(o_small[k])
        outs += [o_in[k], cv, sm[0], sm[1], sm[2], sm[3], o_out[k], sm[4], sm[5]]
    return (loss, dx[None], *outs)
```

```python
import functools

import jax
import jax.numpy as jnp
from jax import lax
from jax.experimental import pallas as pl
from jax.experimental.pallas import tpu as pltpu

F32 = jnp.float32
BF16 = jnp.bfloat16
MM_DTYPE = BF16

N_DEV = 8
D_MODEL = 1024
DEPTH = 2
A_HEADS = 4
A_HEAD_DIM = 128
A_WIDTH = 512
CONV_K = 4
SUPER = 256
NEWTON_STEPS = 1
B_Q_HEADS = 8
B_KV_HEADS = 2
B_HEAD_DIM = 64
B_GROUP = 4
B_WIDTH = 512
WINDOW = 128
BLOCK = 128
IN_COLS = 3336
SHARD_COLS = IN_COLS // N_DEV
OUT_SHARD_ROWS = D_MODEL // N_DEV
CONV_SHARD_COLS = 3 * A_WIDTH // N_DEV
DEEPNORM_ALPHA = (2 * DEPTH) ** 0.25
LN_EPS = 1e-5
RMS_EPS = 1e-6
L2_EPS = 1e-6
ADAM_LR, ADAM_B1, ADAM_B2, ADAM_EPS, ADAM_WD, ADAM_STEP = 0.001, 0.9, 0.999, 1e-08, 0.01, 10

LANE = 128
L_QB, L_ZB, L_KB, L_VB, L_ZA, L_BA, L_QKV = 0, 512, 1024, 1152, 1280, 1792, 1920
L_SWA = 1280
L_GATE = 640
L_COLS = 3456
SMALL_SIZES = (("a_log", 4), ("dt_bias", 4), ("norm_w", 128), ("sinks", 8), ("ln_g", 1024), ("ln_b", 1024))
CS_CONV = CONV_K * CONV_SHARD_COLS
CS_ROWS = 24
VMEM_LIMIT = 48 * 1024 * 1024


def _cparams(sem=None):
    return pltpu.CompilerParams(dimension_semantics=sem, vmem_limit_bytes=VMEM_LIMIT)


def _mm(a, b):
    return jnp.dot(a.astype(MM_DTYPE), b.astype(MM_DTYPE), preferred_element_type=F32)


def _mm_nt(a, b):
    return lax.dot_general(a.astype(MM_DTYPE), b.astype(MM_DTYPE), (((1,), (1,)), ((), ())),
                           preferred_element_type=F32)


def _mm_tn(a, b):
    return lax.dot_general(a.astype(MM_DTYPE), b.astype(MM_DTYPE), (((0,), (0,)), ((), ())),
                           preferred_element_type=F32)


def _split(a):
    hi = a.astype(BF16)
    return hi, (a - hi.astype(F32)).astype(BF16)


def _silu(x):
    return x * jax.nn.sigmoid(x)


@jax.custom_vjp
def _stack(parts):
    return jnp.stack(parts)


_stack.defvjp(lambda parts: (jnp.stack(parts), None), lambda _, g: (tuple(g[i] for i in range(g.shape[0])),))


def _softplus(x):
    return jnp.maximum(x, 0.0) + jnp.log1p(jnp.exp(-jnp.abs(x)))


_ANY = pl.BlockSpec(memory_space=pl.ANY)


def _me():
    return lax.axis_index("x"), lax.axis_index("y"), lax.axis_index("c")


def _flat_id(pos):
    return 4 * pos[0] + 2 * pos[1] + pos[2]


def _remote(src, dst, send_sem, recv_sem, to):
    return pltpu.make_async_remote_copy(src_ref=src, dst_ref=dst, send_sem=send_sem, recv_sem=recv_sem,
                                        device_id=to, device_id_type=pl.DeviceIdType.MESH)


class _Direct:
    def __init__(self, items, bufs):
        self.items, self.bufs = list(items), list(bufs)
        self.n_src, self.n_buf = len(self.items), len(self.bufs)
        self.old = [j for j, b in enumerate(self.bufs) if not isinstance(b, jax.ShapeDtypeStruct)]
        self.args = [it[0] for it in self.items] + [self.bufs[j] for j in self.old]
        self.out_shape = [jax.ShapeDtypeStruct(b.shape, b.dtype) for b in self.bufs]
        self.scratch = [pltpu.SemaphoreType.DMA((self.n_src, N_DEV - 1)),
                        pltpu.SemaphoreType.DMA((self.n_src, N_DEV - 1)), pltpu.SemaphoreType.DMA((self.n_src,))]

    def aliases(self, in_base, out_base):
        return {in_base + self.n_src + pos: out_base + j for pos, j in enumerate(self.old)}

    def copies(self, in_refs, out_refs, sems):
        send_sems, recv_sems, local_sems = sems
        x, y, c = _me()
        me = _flat_id((x, y, c))
        peers = [(x ^ ((rel >> 2) & 1), y ^ ((rel >> 1) & 1), c ^ (rel & 1)) for rel in range(1, N_DEV)]
        local, sends, recvs = [], [], []
        for a, (_, per_dest, j, prefix, *rest) in enumerate(self.items):
            src = lambda d: in_refs[a].at[d] if per_dest else in_refs[a]
            dst = lambda s: out_refs[j].at[tuple(prefix) + (s,) + tuple(rest[0] if rest else ())]
            local.append(pltpu.make_async_copy(src(me), dst(me), local_sems.at[a]))
            for k, peer in enumerate(peers):
                pid = _flat_id(peer)
                sends.append(_remote(src(pid), dst(me), send_sems.at[a, k], recv_sems.at[a, k], peer))
                recvs.append(_remote(src(pid), dst(pid), send_sems.at[a, k], recv_sems.at[a, k], peer))
        return local, sends, recvs

    def start(self, in_refs, out_refs, sems):
        local, sends, _ = self.copies(in_refs, out_refs, sems)
        for cp in local + sends:
            cp.start()

    def wait(self, in_refs, out_refs, sems):
        local, sends, recvs = self.copies(in_refs, out_refs, sems)
        for cp in recvs:
            cp.wait_recv()
        for cp in sends:
            cp.wait_send()
        for cp in local:
            cp.wait()


def _pcall(core, *, name, grid, in_specs, out_specs, out_shape, args, sem, scratch_shapes=(), aliases=None,
           rider=None):
    n_in, n_out, n_scr = len(in_specs), len(out_specs), len(scratch_shapes)
    n_rin, n_rout = (len(rider.args), rider.n_buf) if rider else (0, 0)

    def body(*refs):
        ins, outs = refs[:n_in], refs[n_in + n_rin:n_in + n_rin + n_out]
        scr = refs[n_in + n_rin + n_out + n_rout:n_in + n_rin + n_out + n_rout + n_scr]
        if rider:
            r_refs = (refs[n_in:n_in + rider.n_src], refs[n_in + n_rin + n_out:n_in + n_rin + n_out + n_rout],
                      refs[n_in + n_rin + n_out + n_rout + n_scr:])
            ids = [pl.program_id(d) for d in range(len(grid))]
            first = functools.reduce(lambda p, q: p & q, [i == 0 for i in ids])
            last = functools.reduce(lambda p, q: p & q, [i == g - 1 for i, g in zip(ids, grid)])
            pl.when(first)(lambda: rider.start(*r_refs))
        core(ins, outs, scr)
        if rider:
            pl.when(last)(lambda: rider.wait(*r_refs))

    aliases = dict(aliases or {})
    if rider:
        sem = ("arbitrary",) * len(grid)
        aliases.update(rider.aliases(n_in, n_out))
    return pl.pallas_call(
        body, name=name, grid=grid, in_specs=list(in_specs) + [_ANY] * n_rin,
        out_specs=list(out_specs) + [_ANY] * n_rout,
        out_shape=list(out_shape) + (rider.out_shape if rider else []),
        scratch_shapes=list(scratch_shapes) + (rider.scratch if rider else []),
        input_output_aliases=aliases, compiler_params=_cparams(sem),
    )(*args, *(rider.args if rider else []))


def _exchange(direct, *, name):
    n_in = len(direct.args)

    def body(*refs):
        r_refs = refs[:direct.n_src], refs[n_in:n_in + direct.n_buf], refs[n_in + direct.n_buf:]
        direct.start(*r_refs)
        direct.wait(*r_refs)

    return pl.pallas_call(
        body, name=name, in_specs=[_ANY] * n_in, out_specs=[_ANY] * direct.n_buf, out_shape=direct.out_shape,
        input_output_aliases=direct.aliases(0, 0), scratch_shapes=direct.scratch,
    )(*direct.args)


def _matmul(a, b, *, form, tm, tn, tk, name, add=None, add_scale=1.0, rider=None, b_cols=None):
    if form == "nn":
        (m, kk), n = a.shape, b.shape[1]
        a_spec = pl.BlockSpec((tm, tk), lambda i, j, k: (i, k))
        b_spec = pl.BlockSpec((tk, tn), lambda i, j, k: (k, j))
        dn = (((1,), (0,)), ((), ()))
    elif form == "nt":
        (m, kk), n = a.shape, b.shape[0]
        a_spec = pl.BlockSpec((tm, tk), lambda i, j, k: (i, k))
        b_spec = pl.BlockSpec((tn, tk), lambda i, j, k: (j, k))
        dn = (((1,), (1,)), ((), ()))
    else:
        kk, m = a.shape
        n0, n = b_cols or (0, b.shape[1])
        assert n0 % tn == 0
        a_spec = pl.BlockSpec((tk, tm), lambda i, j, k: (k, i))
        b_spec = pl.BlockSpec((tk, tn), lambda i, j, k: (k, j + n0 // tn))
        dn = (((0,), (0,)), ((), ()))
    assert m % tm == 0 and n % tn == 0 and kk % tk == 0, (name, m, n, kk)
    has_add = add is not None

    def core(ins, outs, _):
        a_ref, b_ref = ins[:2]
        o_ref = outs[0]
        k = pl.program_id(2)
        p = lax.dot_general(a_ref[...].astype(MM_DTYPE), b_ref[...].astype(MM_DTYPE), dn,
                            preferred_element_type=F32)

        @pl.when(k == 0)
        def _():
            o_ref[...] = p + add_scale * ins[2][...] if has_add else p

        @pl.when(k > 0)
        def _():
            o_ref[...] += p

    in_specs = [a_spec, b_spec]
    args = [a, b]
    if has_add:
        in_specs.append(pl.BlockSpec((tm, tn), lambda i, j, k: (i, j)))
        args.append(add)
    res = _pcall(core, name=name, grid=(m // tm, n // tn, kk // tk), in_specs=in_specs,
                 out_specs=[pl.BlockSpec((tm, tn), lambda i, j, k: (i, j))],
                 out_shape=[jax.ShapeDtypeStruct((m, n), F32)], args=args,
                 sem=("parallel", "parallel", "arbitrary"), rider=rider)
    return res if rider else res[0]


ZERO_TAIL = 8


def _with_tail(x):
    return jnp.concatenate([x, jnp.zeros((ZERO_TAIL,) + x.shape[1:], x.dtype)], axis=0)


def _shift_down(x, k):
    return pltpu.roll(x, k, 0)


def _shift_up(x, k):
    return pltpu.roll(x, x.shape[0] - k, 0)


def _conv_slab(x, w):
    return w[3:4] * x + w[2:3] * _shift_down(x, 1) + w[1:2] * _shift_down(x, 2) + w[0:1] * _shift_down(x, 3)


def _prep_fwd(h, conv_w, *, name):
    t_len = h.shape[0]

    def body(x_ref, w_ref, o_ref):
        s = pl.program_id(0)
        y = _silu(_conv_slab(_with_tail(x_ref[...]), w_ref[...])[:t_len])
        rs = lax.rsqrt(jnp.sum(y * y, axis=-1, keepdims=True) + L2_EPS)
        scale = jnp.where(s < A_HEADS, A_HEAD_DIM ** -0.5, 1.0)
        o_ref[...] = jnp.where(s < 2 * A_HEADS, y * rs * scale, y)

    return pl.pallas_call(
        body, name=name, grid=(12,),
        in_specs=[pl.BlockSpec((t_len, LANE), lambda s: (0, L_QKV // LANE + s)),
                  pl.BlockSpec((8, LANE), lambda s: (0, s))],
        out_specs=pl.BlockSpec((t_len, LANE), lambda s: (0, s)),
        out_shape=jax.ShapeDtypeStruct((t_len, 3 * A_WIDTH), F32),
        compiler_params=_cparams(("parallel",)),
    )(h, conv_w)


def _prep_bwd(h, conv_w, d_out, dh, *, name):
    t_len = h.shape[0]

    def body(x_ref, w_ref, g_ref, dh_in, dx_ref, dw_ref):
        del dh_in
        s = pl.program_id(0)
        x = _with_tail(x_ref[...])
        g = _with_tail(g_ref[0])
        w = w_ref[...]
        xs = [_shift_down(x, 3), _shift_down(x, 2), _shift_down(x, 1), x]
        c = w[0:1] * xs[0] + w[1:2] * xs[1] + w[2:3] * xs[2] + w[3:4] * xs[3]
        sg = jax.nn.sigmoid(c)
        y = c * sg
        rs = lax.rsqrt(jnp.sum(y * y, axis=-1, keepdims=True) + L2_EPS)
        scale = jnp.where(s < A_HEADS, A_HEAD_DIM ** -0.5, 1.0)
        dy_n = scale * (rs * g - y * (rs * rs * rs) * jnp.sum(g * y, axis=-1, keepdims=True))
        dy = jnp.where(s < 2 * A_HEADS, dy_n, g)
        dc = dy * (sg * (1.0 + c * (1.0 - sg)))
        dx = w[3:4] * dc + w[2:3] * _shift_up(dc, 1) + w[1:2] * _shift_up(dc, 2) + w[0:1] * _shift_up(dc, 3)
        dx_ref[...] = dx[:t_len].astype(dx_ref.dtype)
        dws = [jnp.sum(dc * xs[j], axis=0, keepdims=True) for j in range(CONV_K)]
        dw_ref[...] = jnp.concatenate(dws + [jnp.zeros((8 - CONV_K, LANE), F32)], axis=0)

    slab = pl.BlockSpec((t_len, LANE), lambda s: (0, L_QKV // LANE + s))
    return pl.pallas_call(
        body, name=name, grid=(12,),
        in_specs=[slab, pl.BlockSpec((8, LANE), lambda s: (0, s)),
                  pl.BlockSpec((1, t_len, LANE), lambda s: (s // A_HEADS, 0, s % A_HEADS)), _ANY],
        out_specs=[slab, pl.BlockSpec((8, LANE), lambda s: (0, s))],
        out_shape=[jax.ShapeDtypeStruct((t_len, L_COLS), MM_DTYPE), jax.ShapeDtypeStruct((8, 3 * A_WIDTH), F32)],
        input_output_aliases={3: 0},
        compiler_params=_cparams(("parallel",)),
    )(h, conv_w, d_out, dh)


N_LEVELS = 5
MF_TRIL, MF_STRIL, MF_DIAG8, MF_LOW16, MF_EYE = 0, 1, 2, 3, 3 + N_LEVELS
MB_CUM, MB_CUM_T, MB_TOT = 0, 1, 2


def _gdn_masks():
    r = lax.broadcasted_iota(jnp.int32, (SUPER, SUPER), 0)
    c = lax.broadcasted_iota(jnp.int32, (SUPER, SUPER), 1)
    same = lambda shift: (r >> shift) == (c >> shift)
    ninf = lambda m: jnp.where(m, 0.0, -jnp.inf).astype(F32)
    one = lambda m: m.astype(F32)
    mf = jnp.stack([ninf(r >= c), ninf(r > c), one(same(3))]
                   + [one(same(4 + lv) & jnp.logical_not(same(3 + lv))) for lv in range(N_LEVELS)] + [one(r == c)])
    mb = jnp.stack([one(r >= c), one(r <= c), jnp.ones((SUPER, SUPER), F32)]).astype(BF16)
    return mf, mb


def _tri_inv_impl(a, mf):
    d = lambda p, q: jnp.dot(p.astype(BF16), q.astype(BF16), preferred_element_type=F32)
    dd = lambda p, q: jnp.dot(p, q, preferred_element_type=F32)
    eye = mf[MF_EYE]
    a0 = a * mf[MF_DIAG8]
    a2 = d(a0, a0)
    a4 = d(a2, a2)
    t = d(d(eye - a0, eye + a2), eye + a4)
    for level in range(N_LEVELS):
        t = t - d(d(t, a * mf[MF_LOW16 + level]), t)
    a_hi, a_lo = _split(a)
    for _ in range(NEWTON_STEPS):
        t0 = t.astype(BF16)
        t0f = t0.astype(F32)
        resid = (eye - t0f) - (dd(a_hi, t0) + dd(a_lo, t0))
        r_hi, r_lo = _split(resid)
        t = t0f + (dd(t0, r_hi) + dd(t0, r_lo))
    return t


@jax.custom_vjp
def _wy_apply(a, rhs, t):
    return _mm(t, rhs)


def _wy_apply_fwd(a, rhs, t):
    x = _mm(t, rhs)
    return x, (t, x)


def _wy_apply_bwd(res, dx):
    t, x = res
    d_rhs = _mm_tn(t, dx)
    return -_mm_nt(d_rhs, x), d_rhs, jnp.zeros_like(t)


_wy_apply.defvjp(_wy_apply_fwd, _wy_apply_bwd)


@functools.partial(jax.custom_vjp, nondiff_argnums=(1,))
def _lane_roll(x, shift):
    return pltpu.roll(x, shift % LANE, 1)


_lane_roll.defvjp(lambda x, shift: (_lane_roll(x, shift), None), lambda shift, _, g: (_lane_roll(g, -shift),))


def _mask_times_lanes(x, mask):
    lane = lax.broadcasted_iota(jnp.int32, (1, LANE), 1)
    x = jnp.where(lane < A_HEADS, x, 0.0)
    x1 = x.astype(BF16).astype(F32)
    x2 = (x - x1).astype(BF16).astype(F32)
    x3 = (x - x1 - x2).astype(BF16).astype(F32)
    pieces = x1 + pltpu.roll(x2, A_HEADS, 1) + pltpu.roll(x3, 2 * A_HEADS, 1)
    res = jnp.dot(mask, pieces.astype(BF16), preferred_element_type=F32)
    return res + pltpu.roll(res, LANE - A_HEADS, 1) + pltpu.roll(res, LANE - 2 * A_HEADS, 1)


@jax.custom_vjp
def _chunk_sums(g, mb):
    return _mask_times_lanes(g, mb[MB_CUM]), _mask_times_lanes(g, mb[MB_TOT])


def _chunk_sums_fwd(g, mb):
    return _chunk_sums(g, mb), mb


def _chunk_sums_bwd(mb, d):
    lane = lax.broadcasted_iota(jnp.int32, (1, LANE), 1)
    dg = _mask_times_lanes(d[0], mb[MB_CUM_T]) + _mask_times_lanes(d[1], mb[MB_TOT])
    return jnp.where(lane < A_HEADS, dg, 0.0), jnp.zeros_like(mb)


_chunk_sums.defvjp(_chunk_sums_fwd, _chunk_sums_bwd)


def _gdn_gates(ba, alog, dtb, mb):
    beta = jax.nn.sigmoid(ba)
    g = -jnp.exp(alog) * _softplus(_lane_roll(ba, -A_HEADS) + dtb)
    gc, gl = _chunk_sums(g, mb)
    return beta, gc, gl, gc.T


def _gdn_block(s, q, k, v, z, gates, nw, h, t_known, mf):
    n = q.shape[0]
    beta_all, gc_all, gl_all, gct_all = gates
    lane = lax.broadcasted_iota(jnp.int32, (1, LANE), 1)
    sub = lax.broadcasted_iota(jnp.int32, (LANE, 1), 0)
    col = lambda x: jnp.sum(jnp.where(lane == h, x, 0.0), axis=1, keepdims=True)
    wide = lambda c: jnp.broadcast_to(c, (n, LANE))
    gc, gl = col(gc_all), col(gl_all)
    gc_row = jnp.sum(jnp.where(sub == h, gct_all, 0.0), axis=0, keepdims=True)
    beta_w, eg_w = wide(col(beta_all)), wide(jnp.exp(gc))
    diff = gc - gc_row
    decay = jnp.exp(diff + mf[MF_TRIL])
    kb = k * beta_w
    a_mat = _mm_nt(kb, k) * jnp.exp(diff + mf[MF_STRIL])
    rhs = jnp.concatenate([v * beta_w, kb * eg_w], axis=1)
    if t_known is None:
        t_mat = _tri_inv_impl(a_mat, mf)
        uw = _mm(t_mat, rhs)
    else:
        t_mat = t_known
        uw = _wy_apply(a_mat, rhs, t_known)
    u, w = uw[:, :LANE], uw[:, LANE:]
    qk = _mm_nt(q, k) * decay
    q_dec = q * eg_w
    k_dec = k * wide(jnp.exp(gl - gc))
    v_new = u - _mm(w, s)
    o = _mm(q_dec, s) + _mm(qk, v_new)
    s = s * jnp.exp(gl[0:1]) + _mm_tn(k_dec, v_new)
    o = o * lax.rsqrt(jnp.mean(o * o, axis=-1, keepdims=True) + RMS_EPS) * nw
    return o * _silu(z), s, t_mat


def _gdn_fwd(qkv, h, alog, dtb, nw, ycat, *, name, rider=None):
    t_len = qkv.shape[0]
    nsc = t_len // SUPER

    def core(ins, outs, scr):
        q_ref, k_ref, v_ref, gate_ref, al_ref, dt_ref, nw_ref, mf_ref, mb_ref, _ = ins
        y_ref, sin_ref, t_ref = outs
        s_scr, = scr

        @pl.when(pl.program_id(0) == 0)
        def _():
            s_scr[...] = jnp.zeros_like(s_scr)

        per_head = lambda ref: jnp.stack([ref[:, hh * LANE:(hh + 1) * LANE] for hh in range(A_HEADS)])
        states = s_scr[...]
        gates = _gdn_gates(gate_ref[:, A_WIDTH:], al_ref[...], dt_ref[...], mb_ref[...])
        fn = jax.vmap(_gdn_block, in_axes=(0, 0, 0, 0, 0, None, None, 0, None, None))
        y, s_new, t_mat = fn(states, per_head(q_ref), per_head(k_ref), per_head(v_ref), per_head(gate_ref),
                             gates, nw_ref[...], jnp.arange(A_HEADS), None, mf_ref[...])
        sin_ref[0] = states
        t_ref[0] = t_mat
        s_scr[...] = s_new
        for hh in range(A_HEADS):
            y_ref[:, hh * LANE:(hh + 1) * LANE] = y[hh]

    blk = lambda j: pl.BlockSpec((SUPER, A_WIDTH), lambda sc: (sc, j))
    row = pl.BlockSpec((1, LANE), lambda sc: (0, 0))
    mf, mb = _gdn_masks()
    whole = lambda a: pl.BlockSpec(a.shape, lambda sc: (0, 0, 0))
    return _pcall(
        core, name=name, grid=(nsc,),
        in_specs=[blk(0), blk(1), blk(2), pl.BlockSpec((SUPER, L_GATE), lambda sc: (sc, L_ZA // L_GATE)),
                  row, row, row, whole(mf), whole(mb), _ANY],
        out_specs=[blk(0),
                   pl.BlockSpec((1, A_HEADS, A_HEAD_DIM, A_HEAD_DIM), lambda sc: (sc, 0, 0, 0)),
                   pl.BlockSpec((1, A_HEADS, SUPER, SUPER), lambda sc: (sc, 0, 0, 0))],
        out_shape=[jax.ShapeDtypeStruct((t_len, D_MODEL), F32),
                   jax.ShapeDtypeStruct((nsc, A_HEADS, A_HEAD_DIM, A_HEAD_DIM), F32),
                   jax.ShapeDtypeStruct((nsc, A_HEADS, SUPER, SUPER), F32)],
        scratch_shapes=[pltpu.VMEM((A_HEADS, A_HEAD_DIM, A_HEAD_DIM), F32)],
        aliases={9: 0}, sem=("arbitrary",), rider=rider,
        args=(qkv, qkv, qkv, h, alog, dtb, nw, mf, mb, ycat))


def _gdn_bwd(qkv, h, alog, dtb, nw, s_in, t_in, dycat, dh, *, name, rider=None):
    t_len = qkv.shape[0]
    nsc = t_len // SUPER

    def core(ins, outs, scr):
        q_ref, k_ref, v_ref, gate_ref, al_ref, dt_ref, nw_ref, sin_ref, t_ref, dy_ref, mf_ref, mb_ref, _ = ins
        dgate_ref, dqkv_ref, dal_ref, ddt_ref, dnw_ref = outs
        ds_scr, = scr

        @pl.when(pl.program_id(0) == 0)
        def _():
            ds_scr[...] = jnp.zeros_like(ds_scr)
            dal_ref[...] = jnp.zeros_like(dal_ref)
            ddt_ref[...] = jnp.zeros_like(ddt_ref)
            dnw_ref[...] = jnp.zeros_like(dnw_ref)

        per_head = lambda ref: jnp.stack([ref[:, hh * LANE:(hh + 1) * LANE] for hh in range(A_HEADS)])
        head_ids = jnp.arange(A_HEADS)
        t_known, mf, mb = t_ref[0], mf_ref[...], mb_ref[...]

        def fn(s, q, k, v, z, ba, alog, dtb, nw):
            gates = _gdn_gates(ba, alog, dtb, mb)
            one = lambda s, q, k, v, z, t, h: _gdn_block(s, q, k, v, z, gates, nw, h, t, mf)[:2]
            return jax.vmap(one)(s, q, k, v, z, t_known, head_ids)

        _, vjp = jax.vjp(fn, sin_ref[0], per_head(q_ref), per_head(k_ref), per_head(v_ref), per_head(gate_ref),
                         gate_ref[:, A_WIDTH:], al_ref[...], dt_ref[...], nw_ref[...])
        ds, dq, dk, dv, dz, dba, dal, ddt, dnw = vjp((per_head(dy_ref), ds_scr[...]))
        ds_scr[...] = ds
        for hh in range(A_HEADS):
            cols = slice(hh * LANE, (hh + 1) * LANE)
            dqkv_ref[0, :, cols] = dq[hh]
            dqkv_ref[1, :, cols] = dk[hh]
            dqkv_ref[2, :, cols] = dv[hh]
            dgate_ref[:, cols] = dz[hh].astype(dgate_ref.dtype)
        dgate_ref[:, A_WIDTH:] = dba.astype(dgate_ref.dtype)
        dal_ref[...] += dal
        ddt_ref[...] += ddt
        dnw_ref[...] += dnw

    rev = lambda i: nsc - 1 - i
    blk = lambda j: pl.BlockSpec((SUPER, A_WIDTH), lambda i: (rev(i), j))
    gate = pl.BlockSpec((SUPER, L_GATE), lambda i: (rev(i), L_ZA // L_GATE))
    row = pl.BlockSpec((1, LANE), lambda i: (0, 0))
    mf, mb = _gdn_masks()
    whole = lambda a: pl.BlockSpec(a.shape, lambda i: (0, 0, 0))
    return _pcall(
        core, name=name, grid=(nsc,),
        in_specs=[blk(0), blk(1), blk(2), gate, row, row, row,
                  pl.BlockSpec((1, A_HEADS, A_HEAD_DIM, A_HEAD_DIM), lambda i: (rev(i), 0, 0, 0)),
                  pl.BlockSpec((1, A_HEADS, SUPER, SUPER), lambda i: (rev(i), 0, 0, 0)),
                  blk(0), whole(mf), whole(mb), _ANY],
        out_specs=[gate, pl.BlockSpec((3, SUPER, A_WIDTH), lambda i: (0, rev(i), 0)), row, row, row],
        out_shape=[jax.ShapeDtypeStruct((t_len, L_COLS), MM_DTYPE), jax.ShapeDtypeStruct((3, t_len, A_WIDTH), F32)]
        + [jax.ShapeDtypeStruct((1, LANE), F32)] * 3,
        scratch_shapes=[pltpu.VMEM((A_HEADS, A_HEAD_DIM, A_HEAD_DIM), F32)],
        aliases={12: 0}, sem=("arbitrary",), rider=rider,
        args=(qkv, qkv, qkv, h, alog, dtb, nw, s_in, t_in, dycat, mf, mb, dh))


Q_BLOCKS = 4
Q_ROWS = Q_BLOCKS * BLOCK


def _swa_block(q, kp, kc, vp, vc, z, sinks, first):
    rows = B_GROUP * BLOCK
    ri = lax.broadcasted_iota(jnp.int32, (rows, 2 * BLOCK), 0)
    si = lax.broadcasted_iota(jnp.int32, (rows, 2 * BLOCK), 1)
    dist = (ri & (BLOCK - 1)) + BLOCK - si
    bias = jnp.where((dist >= 0) & (dist < WINDOW), 0.0, -jnp.inf)
    no_prev = jnp.where(first & (si[:1] < BLOCK), -jnp.inf, 0.0)
    dist_f = dist.astype(F32)
    head_of_row = lax.broadcasted_iota(jnp.int32, (rows, 1), 0) >> 7
    keys = jnp.concatenate([kp, kc], axis=0)
    vals = jnp.concatenate([vp, vc], axis=0)

    def item(b, j):
        cs = slice(j * B_HEAD_DIM, (j + 1) * B_HEAD_DIM)
        rs = slice(b * BLOCK, (b + 1) * BLOCK)
        heads = range(j * B_GROUP, (j + 1) * B_GROUP)
        qs = jnp.concatenate([q[rs, hq * B_HEAD_DIM:(hq + 1) * B_HEAD_DIM] for hq in heads], axis=0) * (
            B_HEAD_DIM ** -0.5)
        kk = keys[b * BLOCK:(b + 2) * BLOCK, cs]
        vv = vals[b * BLOCK:(b + 2) * BLOCK, cs]
        sink = jnp.concatenate([jnp.broadcast_to(sinks[:, hq:hq + 1], (BLOCK, 1)) for hq in heads], axis=0)
        slope = sum(jnp.where(head_of_row == gi, 2.0 ** (-8.0 * (hq + 1) / B_Q_HEADS), 0.0)
                    for gi, hq in enumerate(heads))
        return qs, kk, vv, sink, slope, (no_prev if b == 0 else jnp.zeros_like(no_prev))

    def attend(qs, kk, vv, sink, slope, hide):
        sc = _mm_nt(qs, kk) - slope * dist_f + (bias + hide)
        m = lax.stop_gradient(jnp.maximum(jnp.max(sc, axis=-1, keepdims=True), sink))
        p = jnp.exp(sc - m)
        inv = 1.0 / (jnp.sum(p, axis=-1, keepdims=True) + jnp.exp(sink - m))
        return _mm(p * inv, vv)

    items = [(b, j) for b in range(Q_BLOCKS) for j in range(B_KV_HEADS)]
    o = jax.vmap(attend)(*[_stack(t) for t in zip(*[item(b, j) for b, j in items])])
    rows_out = [jnp.concatenate([o[b * B_KV_HEADS + j, gi * BLOCK:(gi + 1) * BLOCK]
                                 for j in range(B_KV_HEADS) for gi in range(B_GROUP)], axis=1)
                for b in range(Q_BLOCKS)]
    return jnp.concatenate(rows_out, axis=0) * _silu(z)


def _swa_specs(idx):
    wide = lambda off: pl.BlockSpec((Q_ROWS, B_WIDTH), lambda n: (idx(n), off))
    cur = lambda off: pl.BlockSpec((Q_ROWS, LANE), lambda n: (idx(n), off))
    prev = lambda off: pl.BlockSpec((BLOCK, LANE), lambda n: (jnp.maximum(idx(n) * Q_BLOCKS - 1, 0), off))
    return [wide(L_QB // B_WIDTH), prev(L_KB // LANE), cur(L_KB // LANE), prev(L_VB // LANE), cur(L_VB // LANE),
            wide(L_ZB // B_WIDTH), pl.BlockSpec((1, LANE), lambda n: (0, 0))]


def _swa_fwd(h, sinks, *, name, rider=None):
    t_len = h.shape[0]
    nb = t_len // Q_ROWS

    def core(ins, outs, _):
        q_ref, kp_ref, kc_ref, vp_ref, vc_ref, z_ref, s_ref = ins
        outs[0][...] = _swa_block(q_ref[...], kp_ref[...], kc_ref[...], vp_ref[...], vc_ref[...], z_ref[...],
                                  s_ref[...], pl.program_id(0) == 0)

    res = _pcall(core, name=name, grid=(nb,), in_specs=_swa_specs(lambda n: n),
                 out_specs=[pl.BlockSpec((Q_ROWS, B_WIDTH), lambda n: (n, 1))],
                 out_shape=[jax.ShapeDtypeStruct((t_len, D_MODEL), F32)], sem=("parallel",), rider=rider,
                 args=(h, h, h, h, h, h, sinks))
    return res if rider else res[0]


def _swa_bwd(h, sinks, dycat, *, name, rider=None):
    t_len = h.shape[0]
    nb = t_len // Q_ROWS
    early = slice(0, Q_ROWS - BLOCK)
    last = slice(Q_ROWS - BLOCK, Q_ROWS)

    def core(ins, outs, scr):
        q_ref, kp_ref, kc_ref, vp_ref, vc_ref, z_ref, s_ref, dy_ref = ins
        dh_ref, dsk_ref = outs
        ck_scr, cv_scr = scr
        i = pl.program_id(0)
        n = nb - 1 - i

        @pl.when(i == 0)
        def _():
            ck_scr[...] = jnp.zeros_like(ck_scr)
            cv_scr[...] = jnp.zeros_like(cv_scr)
            dsk_ref[...] = jnp.zeros_like(dsk_ref)

        fn = functools.partial(_swa_block, first=(n == 0))
        _, vjp = jax.vjp(fn, q_ref[...], kp_ref[...], kc_ref[...], vp_ref[...], vc_ref[...], z_ref[...], s_ref[...])
        dq, dkp, dkc, dvp, dvc, dz, dsk = vjp(dy_ref[...])
        def put(rows, col, val):
            dh_ref[rows, col:col + val.shape[1]] = val.astype(dh_ref.dtype)

        put(slice(None), L_QB, dq)
        put(slice(None), L_ZB, dz)
        put(early, L_KB, dkc[early])
        put(early, L_VB, dvc[early])
        put(last, L_KB, dkc[last] + ck_scr[...])
        put(last, L_VB, dvc[last] + cv_scr[...])
        ck_scr[...] = dkp
        cv_scr[...] = dvp
        dsk_ref[...] += dsk

    rev = lambda i: nb - 1 - i
    return _pcall(
        core, name=name, grid=(nb,),
        in_specs=_swa_specs(rev) + [pl.BlockSpec((Q_ROWS, B_WIDTH), lambda i: (rev(i), 1))],
        out_specs=[pl.BlockSpec((Q_ROWS, L_SWA), lambda i: (rev(i), 0)), pl.BlockSpec((1, LANE), lambda i: (0, 0))],
        out_shape=[jax.ShapeDtypeStruct((t_len, L_COLS), MM_DTYPE), jax.ShapeDtypeStruct((1, LANE), F32)],
        scratch_shapes=[pltpu.VMEM((BLOCK, LANE), F32), pltpu.VMEM((BLOCK, LANE), F32)],
        sem=("arbitrary",), rider=rider, args=(h, h, h, h, h, h, sinks, dycat))


def _out_ln_fwd(ycat, w_out, x, ln_g, ln_b, *, name, tm=512, last=False):
    t_len = x.shape[0]

    def body(y_ref, w_ref, x_ref, g_ref, b_ref, r_ref, *o_ref):
        r = DEEPNORM_ALPHA * x_ref[...] + _mm(y_ref[...], w_ref[...])
        r_ref[...] = r
        if not last:
            mu = jnp.mean(r, axis=-1, keepdims=True)
            d = r - mu
            var = jnp.mean(d * d, axis=-1, keepdims=True)
            o_ref[0][...] = d * lax.rsqrt(var + LN_EPS) * g_ref[...] + b_ref[...]

    tile = pl.BlockSpec((tm, D_MODEL), lambda i: (i, 0))
    vec = pl.BlockSpec((1, D_MODEL), lambda i: (0, 0))
    n_out = 1 if last else 2
    res = pl.pallas_call(
        body, name=name, grid=(t_len // tm,),
        in_specs=[tile, pl.BlockSpec((D_MODEL, D_MODEL), lambda i: (0, 0)), tile, vec, vec],
        out_specs=[tile] * n_out,
        out_shape=[jax.ShapeDtypeStruct((t_len, D_MODEL), F32)] * n_out,
        compiler_params=_cparams(("parallel",)),
    )(ycat, w_out, x, ln_g, ln_b)
    return (res[0], None) if last else res


def _ln_bwd(dxn, r, ln_g, *, name, tm=512, loss=None):
    t_len = r.shape[0]

    def body(*refs):
        if loss:
            t_ref, r_ref, g_ref, b_ref, dr_ref, dg_ref, db_ref, l_ref = refs
        else:
            dx_ref, r_ref, g_ref, dr_ref, dg_ref, db_ref = refs

        @pl.when(pl.program_id(0) == 0)
        def _():
            dg_ref[...] = jnp.zeros_like(dg_ref)
            db_ref[...] = jnp.zeros_like(db_ref)
            if loss:
                l_ref[...] = jnp.zeros_like(l_ref)

        rr = r_ref[...]
        mu = jnp.mean(rr, axis=-1, keepdims=True)
        d = rr - mu
        rstd = lax.rsqrt(jnp.mean(d * d, axis=-1, keepdims=True) + LN_EPS)
        xh = d * rstd
        if loss:
            e = (xh * g_ref[...] + b_ref[...]) - t_ref[...]
            dx = e * (1.0 / D_MODEL)
            l_ref[...] += jnp.sum(e * e, axis=0, keepdims=True)
        else:
            dx = dx_ref[...]
        dxh = dx * g_ref[...]
        dr_ref[...] = rstd * (dxh - jnp.mean(dxh, axis=-1, keepdims=True)
                              - xh * jnp.mean(dxh * xh, axis=-1, keepdims=True))
        dg_ref[...] += jnp.sum(dx * xh, axis=0, keepdims=True)
        db_ref[...] += jnp.sum(dx, axis=0, keepdims=True)

    tile = pl.BlockSpec((tm, D_MODEL), lambda i: (i, 0))
    vec = pl.BlockSpec((1, D_MODEL), lambda i: (0, 0))
    vec_shape = jax.ShapeDtypeStruct((1, D_MODEL), F32)
    args = (loss[0], r, ln_g, loss[1]) if loss else (dxn, r, ln_g)
    return pl.pallas_call(
        body, name=name, grid=(t_len // tm,),
        in_specs=[tile, tile, vec] + ([vec] if loss else []), out_specs=[tile, vec, vec] + ([vec] if loss else []),
        out_shape=[jax.ShapeDtypeStruct((t_len, D_MODEL), F32), vec_shape, vec_shape] + ([vec_shape] if loss else []),
        compiler_params=_cparams(("arbitrary",)),
    )(*args)


def _pad_row(v):
    return jnp.zeros((1, LANE), F32).at[0, :v.shape[0]].set(v)


_REGIONS = ((0, 1536, L_QKV), (1536, 2048, L_ZA), (2048, 2056, L_BA), (2056, 2568, L_QB), (2568, 2696, L_KB),
            (2696, 2824, L_VB), (2824, 3336, L_ZB))


def _shard_pieces(regions):
    for a, b, off in regions:
        for d in range(N_DEV):
            lo, hi = max(a, d * SHARD_COLS), min(b, (d + 1) * SHARD_COLS)
            if lo < hi:
                yield d, lo - d * SHARD_COLS, hi - d * SHARD_COLS, off + lo - a


def _as_list(r):
    return list(r) if isinstance(r, (list, tuple)) else [r]


def _gathered(shard):
    return jax.ShapeDtypeStruct((N_DEV,) + shard.shape, shard.dtype)


def _full_w_in(g_in, name):
    by_offset = sorted(_shard_pieces(_REGIONS), key=lambda p: p[3])
    tc = 256

    def body(g_ref, o_ref):
        pieces, row = [], 0
        for d, lo, hi, off in by_offset + [(None, 0, 0, L_COLS)]:
            if off > row:
                pieces.append(jnp.zeros((off - row, tc), g_ref.dtype))
            if d is not None:
                pieces.append(g_ref[d, lo:hi, :])
            row = off + hi - lo
        o_ref[...] = jnp.concatenate(pieces, axis=0)

    return pl.pallas_call(
        body, name=name, grid=(D_MODEL // tc,),
        in_specs=[pl.BlockSpec((N_DEV, SHARD_COLS, tc), lambda i: (0, 0, i))],
        out_specs=pl.BlockSpec((L_COLS, tc), lambda i: (0, i)),
        out_shape=jax.ShapeDtypeStruct((L_COLS, D_MODEL), g_in.dtype),
        compiler_params=_cparams(("parallel",)),
    )(g_in)


def _full_conv(g_conv):
    return jnp.pad(g_conv.transpose(1, 0, 2).reshape(CONV_K, 3 * A_WIDTH), ((0, 8 - CONV_K), (0, 0)))


def _forward(x, weights, shards, small):
    a_log, dt_bias, norm_w, sinks, ln_g, ln_b = small
    tm = min(512, x.shape[0])
    saved, weights = [], [list(w) for w in weights]
    whole = lambda arrs: _Direct([(a, False, j, ()) for j, a in enumerate(arrs)], [_gathered(a) for a in arrs])
    for l in range(DEPTH):
        rider = whole(shards[l][1:]) if weights[l][1] is None else None
        h, *got = _as_list(_matmul(x, weights[l][0], form="nt", tm=tm, tn=L_COLS, tk=D_MODEL, name=f"in_proj_{l}",
                                   rider=rider))
        if rider:
            weights[l][1:] = [got[0].reshape(D_MODEL, D_MODEL), _full_conv(got[1])]
        w_in_l, w_out_l, conv_l = weights[l]
        qkv = _prep_fwd(h, conv_l, name=f"prep_fwd_{l}")
        al, dt, nw, sk = _pad_row(a_log[l]), _pad_row(dt_bias[l]), norm_w[l][None, :], _pad_row(sinks[l])
        ahead = l + 1 < DEPTH and weights[l + 1][0] is None
        rider = whole(shards[l + 1][1:]) if ahead else None
        ycat, *got = _as_list(_swa_fwd(h, sk, name=f"swa_fwd_{l}", rider=rider))
        if ahead:
            weights[l + 1][1:] = [got[0].reshape(D_MODEL, D_MODEL), _full_conv(got[1])]
        rider = whole(shards[l + 1][:1]) if ahead else None
        ycat, s_in, t_in, *got = _gdn_fwd(qkv, h, al, dt, nw, ycat, name=f"gdn_fwd_{l}", rider=rider)
        if ahead:
            weights[l + 1][0] = _full_w_in(got[0], f"w_in_rows_{l + 1}")
        r, xn = _out_ln_fwd(ycat, w_out_l, x, ln_g[l][None, :], ln_b[l][None, :], name=f"out_ln_{l}",
                            last=(l == DEPTH - 1))
        saved.append((x, h, qkv, s_in, t_in, ycat, r, al, dt, nw, sk))
        x = xn
    return x, saved, weights


def _w_in_blocks(g, name):
    cols, tc = g.shape[1], 256
    pieces = list(_shard_pieces(_REGIONS))

    def body(g_ref, o_ref):
        blocks = [[] for _ in range(N_DEV)]
        for d, lo, hi, off in pieces:
            blocks[d].append(g_ref[off:off + hi - lo, :])
        for d in range(N_DEV):
            o_ref[d] = jnp.concatenate(blocks[d], axis=0).astype(BF16)

    return pl.pallas_call(
        body, name=name, grid=(cols // tc,),
        in_specs=[pl.BlockSpec((L_COLS, tc), lambda i: (0, i))],
        out_specs=pl.BlockSpec((N_DEV, SHARD_COLS, tc), lambda i: (0, 0, i)),
        out_shape=jax.ShapeDtypeStruct((N_DEV, SHARD_COLS, cols), BF16),
        compiler_params=_cparams(("parallel",)),
    )(g)


def _small_blocks(g):
    c_conv = g["conv_w"].reshape(CONV_K, N_DEV, CONV_SHARD_COLS).transpose(1, 0, 2)
    c_small = [jnp.broadcast_to(g[n][None], (N_DEV,) + g[n].shape) for n, _ in SMALL_SIZES]
    return _pack_small(c_conv, c_small)


def _contributions(g):
    c_out = g["w_out"].astype(BF16).reshape(N_DEV, OUT_SHARD_ROWS, D_MODEL)
    return _w_in_blocks(g["w_in_rows"], name="w_in_grad_blocks_above"), c_out, _small_blocks(g)


def _backward_layer(l, dx, saved_l, weights_l, ln_g_l, above=None, loss=None):
    x_in, h, qkv, s_in, t_in, ycat, r, al, dt, nw, sk = saved_l
    w_in_l, w_out_l, conv_l = weights_l
    tm = min(512, x_in.shape[0])
    dr, d_lng, d_lnb, *loss_lanes = _ln_bwd(dx, r, ln_g_l[None, :], name=f"ln_bwd_{l}", loss=loss)
    big = min(1024, x_in.shape[0])
    dycat = _matmul(dr, w_out_l, form="nt", tm=big, tn=D_MODEL, tk=D_MODEL, name=f"out_proj_dx_{l}")
    d_wout = _matmul(ycat, dr, form="tn", tm=D_MODEL, tn=D_MODEL, tk=big, name=f"out_proj_dw_{l}")
    rider, p_in, p_out, p_small = None, None, None, None
    recv = lambda c: jax.ShapeDtypeStruct((DEPTH,) + c.shape, c.dtype)
    if above:
        c_out = d_wout.astype(BF16).reshape(N_DEV, OUT_SHARD_ROWS, D_MODEL)
        rider = _Direct([(above[1], True, 0, (l + 1,)), (above[2], True, 1, (l + 1,)), (c_out, True, 0, (l,))],
                        [recv(above[1]), recv(above[2])])
    dh, d_sk, *got = _swa_bwd(h, sk, dycat, name=f"swa_bwd_{l}", rider=rider)
    if above:
        p_out, p_small = got
        rider = _Direct([(above[0], True, 0, (l + 1,))], [recv(above[0])])
    dh, dqkv_n, d_al, d_dt, d_nw, *got = _gdn_bwd(qkv, h, al, dt, nw, s_in, t_in, dycat, dh,
                                                  name=f"gdn_bwd_{l}", rider=rider)
    dh, d_conv = _prep_bwd(h, conv_l, dqkv_n, dh, name=f"prep_bwd_{l}")
    grads = dict(w_out=d_wout, conv_w=d_conv[:CONV_K], a_log=d_al[0, :A_HEADS], dt_bias=d_dt[0, :A_HEADS],
                 norm_w=d_nw[0], sinks=d_sk[0, :B_Q_HEADS], ln_g=d_lng[0], ln_b=d_lnb[0])
    dw = functools.partial(_matmul, dh, x_in, form="tn")
    if not above:
        grads["w_in_rows"] = dw(name=f"in_proj_dw_{l}", tm=L_COLS // 3, tn=D_MODEL, tk=min(2048, x_in.shape[0]))
    else:
        p_in, = got
        cut = D_MODEL // 2
        rest = D_MODEL - cut
        first = dw(name=f"in_proj_dw_first_{l}", tm=L_COLS, tn=cut, tk=big, b_cols=(0, cut))
        blocks = _w_in_blocks(first, name=f"w_in_grad_blocks_first_{l}")
        rider = _Direct([(blocks, True, 0, (l,), (slice(None), pl.ds(0, cut)))], [p_in])
        second, p_in = dw(name=f"in_proj_dw_second_{l}", tm=L_COLS, tn=cut, tk=big, b_cols=(cut, rest), rider=rider)
        blocks = _w_in_blocks(second, name=f"w_in_grad_blocks_second_{l}")
        rider = _Direct([(blocks, True, 0, (l,), (slice(None), pl.ds(cut, rest))),
                         (_small_blocks(grads), True, 1, (l,))], [p_in, p_small])
    dx, *got = _as_list(_matmul(dh, w_in_l, form="nn", tm=tm, tn=D_MODEL, tk=L_COLS, name=f"in_proj_dx_{l}",
                                add=dr, add_scale=DEEPNORM_ALPHA, rider=rider))
    bufs = (got[0], p_out, got[1]) if above else None
    return dx, grads, bufs, (loss_lanes[0] if loss else None)


def _all_gather(shards, *, name):
    n_arr = len(shards)

    def body(*refs):
        x_refs, out_refs = refs[:n_arr], refs[n_arr:2 * n_arr]
        send_sems, recv_sems, local_sems = refs[2 * n_arr:]
        x, y, c = _me()
        me, sibling = (x, y, c), (x, y, 1 - c)
        chips = [(1 - x, y), (x, 1 - y), (1 - x, 1 - y)]

        def copy(a, k, block, to, src=None):
            dst = out_refs[a].at[_flat_id(block)]
            return _remote(dst if src is None else src, dst, send_sems.at[a, k], recv_sems.at[a, k], to)

        mine = [pltpu.make_async_copy(x_refs[a], out_refs[a].at[_flat_id(me)], local_sems.at[a])
                for a in range(n_arr)]
        for cp in mine:
            cp.start()
        first = []
        for a in range(n_arr):
            first.append(copy(a, 0, me, sibling, src=x_refs[a]))
            first += [copy(a, 1 + j, me, (*chip, c), src=x_refs[a]) for j, chip in enumerate(chips)]
        for cp in first:
            cp.start()
        passed = []
        for j, chip in enumerate(chips):
            for a in range(n_arr):
                copy(a, 1 + j, (*chip, c), me).wait_recv()
                fwd = copy(a, 4 + j, (*chip, c), sibling)
                fwd.start()
                passed.append(fwd)
        for a in range(n_arr):
            copy(a, 0, sibling, me).wait_recv()
            for j, chip in enumerate(chips):
                copy(a, 4 + j, (*chip, 1 - c), me).wait_recv()
        for cp in first + passed:
            cp.wait_send()
        for cp in mine:
            cp.wait()

    return pl.pallas_call(
        body, name=name, in_specs=[_ANY] * n_arr, out_specs=[_ANY] * n_arr,
        out_shape=[jax.ShapeDtypeStruct((N_DEV,) + s.shape, s.dtype) for s in shards],
        scratch_shapes=[pltpu.SemaphoreType.DMA((n_arr, N_DEV - 1)), pltpu.SemaphoreType.DMA((n_arr, N_DEV - 1)),
                        pltpu.SemaphoreType.DMA((n_arr,))],
    )(*shards)


def _adamw(parts, w, m, v, *, tr, name):
    depth, rows, cols = w.shape
    c1 = 1.0 - ADAM_B1 ** ADAM_STEP
    c2 = 1.0 - ADAM_B2 ** ADAM_STEP

    def body(g_ref, w_ref, m_ref, v_ref, go_ref, d_ref, mo_ref, vo_ref):
        g = g_ref[0, 0].astype(F32)
        for s in range(1, N_DEV):
            g = g + g_ref[0, s].astype(F32)
        m_new = ADAM_B1 * m_ref[0] + (1.0 - ADAM_B1) * g
        v_new = ADAM_B2 * v_ref[0] + (1.0 - ADAM_B2) * (g * g)
        go_ref[0] = g
        mo_ref[0] = m_new
        vo_ref[0] = v_new
        d_ref[0] = -ADAM_LR * ((m_new / c1) / (jnp.sqrt(v_new / c2) + ADAM_EPS) + ADAM_WD * w_ref[0])

    tile = pl.BlockSpec((1, tr, cols), lambda l, i: (l, i, 0))
    return pl.pallas_call(
        body, name=name, grid=(depth, rows // tr),
        in_specs=[pl.BlockSpec((1, N_DEV, tr, cols), lambda l, i: (l, 0, i, 0)), tile, tile, tile],
        out_specs=[tile] * 4, out_shape=[jax.ShapeDtypeStruct(w.shape, F32)] * 4,
        compiler_params=_cparams(("parallel", "parallel")),
    )(parts, w, m, v)


def _adamw_w_in(parts, w, m, v, *, name):
    c1 = 1.0 - ADAM_B1 ** ADAM_STEP
    c2 = 1.0 - ADAM_B2 ** ADAM_STEP

    def body(g_ref, w_ref, m_ref, v_ref, go_ref, d_ref, mo_ref, vo_ref):
        gs = []
        for l in range(DEPTH):
            g = g_ref[l, 0].astype(F32)
            for s in range(1, N_DEV):
                g = g + g_ref[l, s].astype(F32)
            gs.append(g)
        g = jnp.stack(gs, axis=1)
        m_new = ADAM_B1 * m_ref[...] + (1.0 - ADAM_B1) * g
        v_new = ADAM_B2 * v_ref[...] + (1.0 - ADAM_B2) * (g * g)
        go_ref[...] = g
        mo_ref[...] = m_new
        vo_ref[...] = v_new
        d_ref[...] = -ADAM_LR * ((m_new / c1) / (jnp.sqrt(v_new / c2) + ADAM_EPS) + ADAM_WD * w_ref[...])

    tile = pl.BlockSpec((SHARD_COLS, DEPTH, LANE), lambda i: (0, 0, i))
    return pl.pallas_call(
        body, name=name, grid=(D_MODEL // LANE,),
        in_specs=[pl.BlockSpec((DEPTH, N_DEV, SHARD_COLS, LANE), lambda i: (0, 0, 0, i)), tile, tile, tile],
        out_specs=[tile] * 4, out_shape=[jax.ShapeDtypeStruct(w.shape, F32)] * 4,
        compiler_params=_cparams(("parallel",)),
    )(parts, w, m, v)


def _pack_small(conv, small):
    lead = conv.shape[:-2]
    flat = jnp.concatenate([conv.reshape(lead + (CS_CONV,))] + list(small), axis=-1)
    pad = CS_ROWS * LANE - flat.shape[-1]
    flat = jnp.concatenate([flat, jnp.zeros(lead + (pad,), F32)], axis=-1)
    return flat.reshape(lead + (CS_ROWS, LANE))


def _unpack_small(p):
    flat = p.reshape(DEPTH, CS_ROWS * LANE)
    conv = flat[:, :CS_CONV].reshape(DEPTH, CONV_K, CONV_SHARD_COLS)
    small, off = [], CS_CONV
    for _, n in SMALL_SIZES:
        small.append(flat[:, off:off + n])
        off += n
    return conv, small


def kernel(x, w_in, conv_w, a_log, dt_bias, norm_w, sinks, w_out, ln_g, ln_b, loss_target, m_w_in, m_conv_w, m_a_log, m_dt_bias, m_norm_w, m_sinks, m_w_out, m_ln_g, m_ln_b, v_w_in, v_conv_w, v_a_log, v_dt_bias, v_norm_w, v_sinks, v_w_out, v_ln_g, v_ln_b):
    small = [a_log, dt_bias, norm_w, sinks, ln_g, ln_b]
    w_t, m_t, v_t = (a.transpose(2, 0, 1) for a in (w_in, m_w_in, v_w_in))
    shards = [[w_t[:, l].astype(BF16), w_out[l].astype(BF16), conv_w[l]] for l in range(DEPTH)]
    g_in0, = _all_gather(shards[0][:1], name="weights_all_gather_0")
    weights = [[_full_w_in(g_in0, "w_in_rows_0"), None, None]] + [[None, None, None]] * (DEPTH - 1)

    _, saved, weights = _forward(x[0], weights, shards, small)
    dx, g1, _, loss_lanes = _backward_layer(1, None, saved[1], weights[1], ln_g[1],
                                            loss=(loss_target[0], ln_b[1][None, :]))
    loss = lax.psum(0.5 * jnp.sum(loss_lanes) * (1.0 / D_MODEL), ("x", "y", "c"))
    dx, _, (p_in, p_out, p_small), _ = _backward_layer(0, dx, saved[0], weights[0], ln_g[0],
                                                       above=_contributions(g1))

    o_in = [o.transpose(1, 2, 0) for o in _adamw_w_in(p_in, w_t, m_t, v_t, name="adamw_w_in")]
    o_out = _adamw(p_out, w_out, m_w_out, v_w_out, tr=OUT_SHARD_ROWS, name="adamw_w_out")
    o_small = _adamw(p_small, _pack_small(conv_w, small),
                     _pack_small(m_conv_w, [m_a_log, m_dt_bias, m_norm_w, m_sinks, m_ln_g, m_ln_b]),
                     _pack_small(v_conv_w, [v_a_log, v_dt_bias, v_norm_w, v_sinks, v_ln_g, v_ln_b]),
                     tr=CS_ROWS, name="adamw_small")
    outs = []
    for k in range(4):
        cv, sm = _unpack_small(o_small[k])
        outs += [o_in[k], cv, sm[0], sm[1], sm[2], sm[3], o_out[k], sm[4], sm[5]]
    return (loss, dx[None], *outs)
```

```python
import functools

import jax
import jax.numpy as jnp
from jax import lax
from jax.experimental import pallas as pl
from jax.experimental.pallas import tpu as pltpu

F32 = jnp.float32
BF16 = jnp.bfloat16
MM_DTYPE = BF16

N_DEV = 8
D_MODEL = 1024
DEPTH = 2
A_HEADS = 4
A_HEAD_DIM = 128
A_WIDTH = 512
CONV_K = 4
SUPER = 256
NEWTON_STEPS = 1
B_Q_HEADS = 8
B_KV_HEADS = 2
B_HEAD_DIM = 64
B_GROUP = 4
B_WIDTH = 512
WINDOW = 128
BLOCK = 128
IN_COLS = 3336
SHARD_COLS = IN_COLS // N_DEV
OUT_SHARD_ROWS = D_MODEL // N_DEV
CONV_SHARD_COLS = 3 * A_WIDTH // N_DEV
DEEPNORM_ALPHA = (2 * DEPTH) ** 0.25
LN_EPS = 1e-5
RMS_EPS = 1e-6
L2_EPS = 1e-6
ADAM_LR, ADAM_B1, ADAM_B2, ADAM_EPS, ADAM_WD, ADAM_STEP = 0.001, 0.9, 0.999, 1e-08, 0.01, 10

LANE = 128
L_QB, L_ZB, L_KB, L_VB, L_ZA, L_BA, L_QKV = 0, 512, 1024, 1152, 1280, 1792, 1920
L_SWA = 1280
L_GATE = 640
L_COLS = 3456
SMALL_SIZES = (("a_log", 4), ("dt_bias", 4), ("norm_w", 128), ("sinks", 8), ("ln_g", 1024), ("ln_b", 1024))
CS_CONV = CONV_K * CONV_SHARD_COLS
CS_ROWS = 24
VMEM_LIMIT = 48 * 1024 * 1024


def _cparams(sem=None):
    return pltpu.CompilerParams(dimension_semantics=sem, vmem_limit_bytes=VMEM_LIMIT)


def _mm(a, b):
    return jnp.dot(a.astype(MM_DTYPE), b.astype(MM_DTYPE), preferred_element_type=F32)


def _mm_nt(a, b):
    return lax.dot_general(a.astype(MM_DTYPE), b.astype(MM_DTYPE), (((1,), (1,)), ((), ())),
                           preferred_element_type=F32)


def _mm_tn(a, b):
    return lax.dot_general(a.astype(MM_DTYPE), b.astype(MM_DTYPE), (((0,), (0,)), ((), ())),
                           preferred_element_type=F32)


def _split(a):
    hi = a.astype(BF16)
    return hi, (a - hi.astype(F32)).astype(BF16)


def _silu(x):
    return x * jax.nn.sigmoid(x)


@jax.custom_vjp
def _stack(parts):
    return jnp.stack(parts)


_stack.defvjp(lambda parts: (jnp.stack(parts), None), lambda _, g: (tuple(g[i] for i in range(g.shape[0])),))


def _softplus(x):
    return jnp.maximum(x, 0.0) + jnp.log1p(jnp.exp(-jnp.abs(x)))


_ANY = pl.BlockSpec(memory_space=pl.ANY)


def _me():
    return lax.axis_index("x"), lax.axis_index("y"), lax.axis_index("c")


def _flat_id(pos):
    return 4 * pos[0] + 2 * pos[1] + pos[2]


def _remote(src, dst, send_sem, recv_sem, to):
    return pltpu.make_async_remote_copy(src_ref=src, dst_ref=dst, send_sem=send_sem, recv_sem=recv_sem,
                                        device_id=to, device_id_type=pl.DeviceIdType.MESH)


class _Direct:
    def __init__(self, items, bufs):
        self.items, self.bufs = list(items), list(bufs)
        self.n_src, self.n_buf = len(self.items), len(self.bufs)
        self.old = [j for j, b in enumerate(self.bufs) if not isinstance(b, jax.ShapeDtypeStruct)]
        self.args = [it[0] for it in self.items] + [self.bufs[j] for j in self.old]
        self.out_shape = [jax.ShapeDtypeStruct(b.shape, b.dtype) for b in self.bufs]
        self.scratch = [pltpu.SemaphoreType.DMA((self.n_src, N_DEV - 1)),
                        pltpu.SemaphoreType.DMA((self.n_src, N_DEV - 1)), pltpu.SemaphoreType.DMA((self.n_src,))]

    def aliases(self, in_base, out_base):
        return {in_base + self.n_src + pos: out_base + j for pos, j in enumerate(self.old)}

    def copies(self, in_refs, out_refs, sems):
        send_sems, recv_sems, local_sems = sems
        x, y, c = _me()
        me = _flat_id((x, y, c))
        peers = [(x ^ ((rel >> 2) & 1), y ^ ((rel >> 1) & 1), c ^ (rel & 1)) for rel in range(1, N_DEV)]
        local, sends, recvs = [], [], []
        for a, (_, per_dest, j, prefix, *rest) in enumerate(self.items):
            src = lambda d: in_refs[a].at[d] if per_dest else in_refs[a]
            dst = lambda s: out_refs[j].at[tuple(prefix) + (s,) + tuple(rest[0] if rest else ())]
            local.append(pltpu.make_async_copy(src(me), dst(me), local_sems.at[a]))
            for k, peer in enumerate(peers):
                pid = _flat_id(peer)
                sends.append(_remote(src(pid), dst(me), send_sems.at[a, k], recv_sems.at[a, k], peer))
                recvs.append(_remote(src(pid), dst(pid), send_sems.at[a, k], recv_sems.at[a, k], peer))
        return local, sends, recvs

    def start(self, in_refs, out_refs, sems):
        local, sends, _ = self.copies(in_refs, out_refs, sems)
        for cp in local + sends:
            cp.start()

    def wait(self, in_refs, out_refs, sems):
        local, sends, recvs = self.copies(in_refs, out_refs, sems)
        for cp in recvs:
            cp.wait_recv()
        for cp in sends:
            cp.wait_send()
        for cp in local:
            cp.wait()


def _pcall(core, *, name, grid, in_specs, out_specs, out_shape, args, sem, scratch_shapes=(), aliases=None,
           rider=None):
    n_in, n_out, n_scr = len(in_specs), len(out_specs), len(scratch_shapes)
    n_rin, n_rout = (len(rider.args), rider.n_buf) if rider else (0, 0)

    def body(*refs):
        ins, outs = refs[:n_in], refs[n_in + n_rin:n_in + n_rin + n_out]
        scr = refs[n_in + n_rin + n_out + n_rout:n_in + n_rin + n_out + n_rout + n_scr]
        if rider:
            r_refs = (refs[n_in:n_in + rider.n_src], refs[n_in + n_rin + n_out:n_in + n_rin + n_out + n_rout],
                      refs[n_in + n_rin + n_out + n_rout + n_scr:])
            ids = [pl.program_id(d) for d in range(len(grid))]
            first = functools.reduce(lambda p, q: p & q, [i == 0 for i in ids])
            last = functools.reduce(lambda p, q: p & q, [i == g - 1 for i, g in zip(ids, grid)])
            pl.when(first)(lambda: rider.start(*r_refs))
        core(ins, outs, scr)
        if rider:
            pl.when(last)(lambda: rider.wait(*r_refs))

    aliases = dict(aliases or {})
    if rider:
        sem = ("arbitrary",) * len(grid)
        aliases.update(rider.aliases(n_in, n_out))
    return pl.pallas_call(
        body, name=name, grid=grid, in_specs=list(in_specs) + [_ANY] * n_rin,
        out_specs=list(out_specs) + [_ANY] * n_rout,
        out_shape=list(out_shape) + (rider.out_shape if rider else []),
        scratch_shapes=list(scratch_shapes) + (rider.scratch if rider else []),
        input_output_aliases=aliases, compiler_params=_cparams(sem),
    )(*args, *(rider.args if rider else []))


def _exchange(direct, *, name):
    n_in = len(direct.args)

    def body(*refs):
        r_refs = refs[:direct.n_src], refs[n_in:n_in + direct.n_buf], refs[n_in + direct.n_buf:]
        direct.start(*r_refs)
        direct.wait(*r_refs)

    return pl.pallas_call(
        body, name=name, in_specs=[_ANY] * n_in, out_specs=[_ANY] * direct.n_buf, out_shape=direct.out_shape,
        input_output_aliases=direct.aliases(0, 0), scratch_shapes=direct.scratch,
    )(*direct.args)


def _matmul(a, b, *, form, tm, tn, tk, name, add=None, add_scale=1.0, rider=None, b_cols=None):
    if form == "nn":
        (m, kk), n = a.shape, b.shape[1]
        a_spec = pl.BlockSpec((tm, tk), lambda i, j, k: (i, k))
        b_spec = pl.BlockSpec((tk, tn), lambda i, j, k: (k, j))
        dn = (((1,), (0,)), ((), ()))
    elif form == "nt":
        (m, kk), n = a.shape, b.shape[0]
        a_spec = pl.BlockSpec((tm, tk), lambda i, j, k: (i, k))
        b_spec = pl.BlockSpec((tn, tk), lambda i, j, k: (j, k))
        dn = (((1,), (1,)), ((), ()))
    else:
        kk, m = a.shape
        n0, n = b_cols or (0, b.shape[1])
        assert n0 % tn == 0
        a_spec = pl.BlockSpec((tk, tm), lambda i, j, k: (k, i))
        b_spec = pl.BlockSpec((tk, tn), lambda i, j, k: (k, j + n0 // tn))
        dn = (((0,), (0,)), ((), ()))
    assert m % tm == 0 and n % tn == 0 and kk % tk == 0, (name, m, n, kk)
    has_add = add is not None

    def core(ins, outs, _):
        a_ref, b_ref = ins[:2]
        o_ref = outs[0]
        k = pl.program_id(2)
        p = lax.dot_general(a_ref[...].astype(MM_DTYPE), b_ref[...].astype(MM_DTYPE), dn,
                            preferred_element_type=F32)

        @pl.when(k == 0)
        def _():
            o_ref[...] = p + add_scale * ins[2][...] if has_add else p

        @pl.when(k > 0)
        def _():
            o_ref[...] += p

    in_specs = [a_spec, b_spec]
    args = [a, b]
    if has_add:
        in_specs.append(pl.BlockSpec((tm, tn), lambda i, j, k: (i, j)))
        args.append(add)
    res = _pcall(core, name=name, grid=(m // tm, n // tn, kk // tk), in_specs=in_specs,
                 out_specs=[pl.BlockSpec((tm, tn), lambda i, j, k: (i, j))],
                 out_shape=[jax.ShapeDtypeStruct((m, n), F32)], args=args,
                 sem=("parallel", "parallel", "arbitrary"), rider=rider)
    return res if rider else res[0]


ZERO_TAIL = 8


def _with_tail(x):
    return jnp.concatenate([x, jnp.zeros((ZERO_TAIL,) + x.shape[1:], x.dtype)], axis=0)


def _shift_down(x, k):
    return pltpu.roll(x, k, 0)


def _shift_up(x, k):
    return pltpu.roll(x, x.shape[0] - k, 0)


def _conv_slab(x, w):
    return w[3:4] * x + w[2:3] * _shift_down(x, 1) + w[1:2] * _shift_down(x, 2) + w[0:1] * _shift_down(x, 3)


def _prep_fwd(h, conv_w, *, name):
    t_len = h.shape[0]

    def body(x_ref, w_ref, o_ref):
        s = pl.program_id(0)
        y = _silu(_conv_slab(_with_tail(x_ref[...]), w_ref[...])[:t_len])
        rs = lax.rsqrt(jnp.sum(y * y, axis=-1, keepdims=True) + L2_EPS)
        scale = jnp.where(s < A_HEADS, A_HEAD_DIM ** -0.5, 1.0)
        o_ref[...] = jnp.where(s < 2 * A_HEADS, y * rs * scale, y)

    return pl.pallas_call(
        body, name=name, grid=(12,),
        in_specs=[pl.BlockSpec((t_len, LANE), lambda s: (0, L_QKV // LANE + s)),
                  pl.BlockSpec((8, LANE), lambda s: (0, s))],
        out_specs=pl.BlockSpec((t_len, LANE), lambda s: (0, s)),
        out_shape=jax.ShapeDtypeStruct((t_len, 3 * A_WIDTH), F32),
        compiler_params=_cparams(("parallel",)),
    )(h, conv_w)


def _prep_bwd(h, conv_w, d_out, dh, *, name):
    t_len = h.shape[0]

    def body(x_ref, w_ref, g_ref, dh_in, dx_ref, dw_ref):
        del dh_in
        s = pl.program_id(0)
        x = _with_tail(x_ref[...])
        g = _with_tail(g_ref[0])
        w = w_ref[...]
        xs = [_shift_down(x, 3), _shift_down(x, 2), _shift_down(x, 1), x]
        c = w[0:1] * xs[0] + w[1:2] * xs[1] + w[2:3] * xs[2] + w[3:4] * xs[3]
        sg = jax.nn.sigmoid(c)
        y = c * sg
        rs = lax.rsqrt(jnp.sum(y * y, axis=-1, keepdims=True) + L2_EPS)
        scale = jnp.where(s < A_HEADS, A_HEAD_DIM ** -0.5, 1.0)
        dy_n = scale * (rs * g - y * (rs * rs * rs) * jnp.sum(g * y, axis=-1, keepdims=True))
        dy = jnp.where(s < 2 * A_HEADS, dy_n, g)
        dc = dy * (sg * (1.0 + c * (1.0 - sg)))
        dx = w[3:4] * dc + w[2:3] * _shift_up(dc, 1) + w[1:2] * _shift_up(dc, 2) + w[0:1] * _shift_up(dc, 3)
        dx_ref[...] = dx[:t_len].astype(dx_ref.dtype)
        dws = [jnp.sum(dc * xs[j], axis=0, keepdims=True) for j in range(CONV_K)]
        dw_ref[...] = jnp.concatenate(dws + [jnp.zeros((8 - CONV_K, LANE), F32)], axis=0)

    slab = pl.BlockSpec((t_len, LANE), lambda s: (0, L_QKV // LANE + s))
    return pl.pallas_call(
        body, name=name, grid=(12,),
        in_specs=[slab, pl.BlockSpec((8, LANE), lambda s: (0, s)),
                  pl.BlockSpec((1, t_len, LANE), lambda s: (s // A_HEADS, 0, s % A_HEADS)), _ANY],
        out_specs=[slab, pl.BlockSpec((8, LANE), lambda s: (0, s))],
        out_shape=[jax.ShapeDtypeStruct((t_len, L_COLS), MM_DTYPE), jax.ShapeDtypeStruct((8, 3 * A_WIDTH), F32)],
        input_output_aliases={3: 0},
        compiler_params=_cparams(("parallel",)),
    )(h, conv_w, d_out, dh)


N_LEVELS = 5
MF_TRIL, MF_STRIL, MF_DIAG8, MF_LOW16, MF_EYE = 0, 1, 2, 3, 3 + N_LEVELS
MB_CUM, MB_CUM_T, MB_TOT = 0, 1, 2


def _gdn_masks():
    r = lax.broadcasted_iota(jnp.int32, (SUPER, SUPER), 0)
    c = lax.broadcasted_iota(jnp.int32, (SUPER, SUPER), 1)
    same = lambda shift: (r >> shift) == (c >> shift)
    ninf = lambda m: jnp.where(m, 0.0, -jnp.inf).astype(F32)
    one = lambda m: m.astype(F32)
    mf = jnp.stack([ninf(r >= c), ninf(r > c), one(same(3))]
                   + [one(same(4 + lv) & jnp.logical_not(same(3 + lv))) for lv in range(N_LEVELS)] + [one(r == c)])
    mb = jnp.stack([one(r >= c), one(r <= c), jnp.ones((SUPER, SUPER), F32)]).astype(BF16)
    return mf, mb


def _tri_inv_impl(a, mf):
    d = lambda p, q: jnp.dot(p.astype(BF16), q.astype(BF16), preferred_element_type=F32)
    dd = lambda p, q: jnp.dot(p, q, preferred_element_type=F32)
    eye = mf[MF_EYE]
    a0 = a * mf[MF_DIAG8]
    a2 = d(a0, a0)
    a4 = d(a2, a2)
    t = d(d(eye - a0, eye + a2), eye + a4)
    for level in range(N_LEVELS):
        t = t - d(d(t, a * mf[MF_LOW16 + level]), t)
    a_hi, a_lo = _split(a)
    for _ in range(NEWTON_STEPS):
        t0 = t.astype(BF16)
        t0f = t0.astype(F32)
        resid = (eye - t0f) - (dd(a_hi, t0) + dd(a_lo, t0))
        r_hi, r_lo = _split(resid)
        t = t0f + (dd(t0, r_hi) + dd(t0, r_lo))
    return t


@jax.custom_vjp
def _wy_apply(a, rhs, t):
    return _mm(t, rhs)


def _wy_apply_fwd(a, rhs, t):
    x = _mm(t, rhs)
    return x, (t, x)


def _wy_apply_bwd(res, dx):
    t, x = res
    d_rhs = _mm_tn(t, dx)
    return -_mm_nt(d_rhs, x), d_rhs, jnp.zeros_like(t)


_wy_apply.defvjp(_wy_apply_fwd, _wy_apply_bwd)


@functools.partial(jax.custom_vjp, nondiff_argnums=(1,))
def _lane_roll(x, shift):
    return pltpu.roll(x, shift % LANE, 1)


_lane_roll.defvjp(lambda x, shift: (_lane_roll(x, shift), None), lambda shift, _, g: (_lane_roll(g, -shift),))


def _mask_times_lanes(x, mask):
    lane = lax.broadcasted_iota(jnp.int32, (1, LANE), 1)
    x = jnp.where(lane < A_HEADS, x, 0.0)
    x1 = x.astype(BF16).astype(F32)
    x2 = (x - x1).astype(BF16).astype(F32)
    x3 = (x - x1 - x2).astype(BF16).astype(F32)
    pieces = x1 + pltpu.roll(x2, A_HEADS, 1) + pltpu.roll(x3, 2 * A_HEADS, 1)
    res = jnp.dot(mask, pieces.astype(BF16), preferred_element_type=F32)
    return res + pltpu.roll(res, LANE - A_HEADS, 1) + pltpu.roll(res, LANE - 2 * A_HEADS, 1)


@jax.custom_vjp
def _chunk_sums(g, mb):
    return _mask_times_lanes(g, mb[MB_CUM]), _mask_times_lanes(g, mb[MB_TOT])


def _chunk_sums_fwd(g, mb):
    return _chunk_sums(g, mb), mb


def _chunk_sums_bwd(mb, d):
    lane = lax.broadcasted_iota(jnp.int32, (1, LANE), 1)
    dg = _mask_times_lanes(d[0], mb[MB_CUM_T]) + _mask_times_lanes(d[1], mb[MB_TOT])
    return jnp.where(lane < A_HEADS, dg, 0.0), jnp.zeros_like(mb)


_chunk_sums.defvjp(_chunk_sums_fwd, _chunk_sums_bwd)


def _gdn_gates(ba, alog, dtb, mb):
    beta = jax.nn.sigmoid(ba)
    g = -jnp.exp(alog) * _softplus(_lane_roll(ba, -A_HEADS) + dtb)
    gc, gl = _chunk_sums(g, mb)
    return beta, gc, gl, gc.T


def _gdn_block(s, q, k, v, z, gates, nw, h, t_known, mf):
    n = q.shape[0]
    beta_all, gc_all, gl_all, gct_all = gates
    lane = lax.broadcasted_iota(jnp.int32, (1, LANE), 1)
    sub = lax.broadcasted_iota(jnp.int32, (LANE, 1), 0)
    col = lambda x: jnp.sum(jnp.where(lane == h, x, 0.0), axis=1, keepdims=True)
    wide = lambda c: jnp.broadcast_to(c, (n, LANE))
    gc, gl = col(gc_all), col(gl_all)
    gc_row = jnp.sum(jnp.where(sub == h, gct_all, 0.0), axis=0, keepdims=True)
    beta_w, eg_w = wide(col(beta_all)), wide(jnp.exp(gc))
    diff = gc - gc_row
    decay = jnp.exp(diff + mf[MF_TRIL])
    kb = k * beta_w
    a_mat = _mm_nt(kb, k) * jnp.exp(diff + mf[MF_STRIL])
    rhs = jnp.concatenate([v * beta_w, kb * eg_w], axis=1)
    if t_known is None:
        t_mat = _tri_inv_impl(a_mat, mf)
        uw = _mm(t_mat, rhs)
    else:
        t_mat = t_known
        uw = _wy_apply(a_mat, rhs, t_known)
    u, w = uw[:, :LANE], uw[:, LANE:]
    qk = _mm_nt(q, k) * decay
    q_dec = q * eg_w
    k_dec = k * wide(jnp.exp(gl - gc))
    v_new = u - _mm(w, s)
    o = _mm(q_dec, s) + _mm(qk, v_new)
    s = s * jnp.exp(gl[0:1]) + _mm_tn(k_dec, v_new)
    o = o * lax.rsqrt(jnp.mean(o * o, axis=-1, keepdims=True) + RMS_EPS) * nw
    return o * _silu(z), s, t_mat


def _gdn_fwd(qkv, h, alog, dtb, nw, ycat, *, name, rider=None):
    t_len = qkv.shape[0]
    nsc = t_len // SUPER

    def core(ins, outs, scr):
        q_ref, k_ref, v_ref, gate_ref, al_ref, dt_ref, nw_ref, mf_ref, mb_ref, _ = ins
        y_ref, sin_ref, t_ref = outs
        s_scr, = scr

        @pl.when(pl.program_id(0) == 0)
        def _():
            s_scr[...] = jnp.zeros_like(s_scr)

        per_head = lambda ref: jnp.stack([ref[:, hh * LANE:(hh + 1) * LANE] for hh in range(A_HEADS)])
        states = s_scr[...]
        gates = _gdn_gates(gate_ref[:, A_WIDTH:], al_ref[...], dt_ref[...], mb_ref[...])
        fn = jax.vmap(_gdn_block, in_axes=(0, 0, 0, 0, 0, None, None, 0, None, None))
        y, s_new, t_mat = fn(states, per_head(q_ref), per_head(k_ref), per_head(v_ref), per_head(gate_ref),
                             gates, nw_ref[...], jnp.arange(A_HEADS), None, mf_ref[...])
        sin_ref[0] = states
        t_ref[0] = t_mat
        s_scr[...] = s_new
        for hh in range(A_HEADS):
            y_ref[:, hh * LANE:(hh + 1) * LANE] = y[hh]

    blk = lambda j: pl.BlockSpec((SUPER, A_WIDTH), lambda sc: (sc, j))
    row = pl.BlockSpec((1, LANE), lambda sc: (0, 0))
    mf, mb = _gdn_masks()
    whole = lambda a: pl.BlockSpec(a.shape, lambda sc: (0, 0, 0))
    return _pcall(
        core, name=name, grid=(nsc,),
        in_specs=[blk(0), blk(1), blk(2), pl.BlockSpec((SUPER, L_GATE), lambda sc: (sc, L_ZA // L_GATE)),
                  row, row, row, whole(mf), whole(mb), _ANY],
        out_specs=[blk(0),
                   pl.BlockSpec((1, A_HEADS, A_HEAD_DIM, A_HEAD_DIM), lambda sc: (sc, 0, 0, 0)),
                   pl.BlockSpec((1, A_HEADS, SUPER, SUPER), lambda sc: (sc, 0, 0, 0))],
        out_shape=[jax.ShapeDtypeStruct((t_len, D_MODEL), F32),
                   jax.ShapeDtypeStruct((nsc, A_HEADS, A_HEAD_DIM, A_HEAD_DIM), F32),
                   jax.ShapeDtypeStruct((nsc, A_HEADS, SUPER, SUPER), F32)],
        scratch_shapes=[pltpu.VMEM((A_HEADS, A_HEAD_DIM, A_HEAD_DIM), F32)],
        aliases={9: 0}, sem=("arbitrary",), rider=rider,
        args=(qkv, qkv, qkv, h, alog, dtb, nw, mf, mb, ycat))


def _gdn_bwd(qkv, h, alog, dtb, nw, s_in, t_in, dycat, dh, *, name, rider=None):
    t_len = qkv.shape[0]
    nsc = t_len // SUPER

    def core(ins, outs, scr):
        q_ref, k_ref, v_ref, gate_ref, al_ref, dt_ref, nw_ref, sin_ref, t_ref, dy_ref, mf_ref, mb_ref, _ = ins
        dgate_ref, dqkv_ref, dal_ref, ddt_ref, dnw_ref = outs
        ds_scr, = scr

        @pl.when(pl.program_id(0) == 0)
        def _():
            ds_scr[...] = jnp.zeros_like(ds_scr)
            dal_ref[...] = jnp.zeros_like(dal_ref)
            ddt_ref[...] = jnp.zeros_like(ddt_ref)
            dnw_ref[...] = jnp.zeros_like(dnw_ref)

        per_head = lambda ref: jnp.stack([ref[:, hh * LANE:(hh + 1) * LANE] for hh in range(A_HEADS)])
        head_ids = jnp.arange(A_HEADS)
        t_known, mf, mb = t_ref[0], mf_ref[...], mb_ref[...]

        def fn(s, q, k, v, z, ba, alog, dtb, nw):
            gates = _gdn_gates(ba, alog, dtb, mb)
            one = lambda s, q, k, v, z, t, h: _gdn_block(s, q, k, v, z, gates, nw, h, t, mf)[:2]
            return jax.vmap(one)(s, q, k, v, z, t_known, head_ids)

        _, vjp = jax.vjp(fn, sin_ref[0], per_head(q_ref), per_head(k_ref), per_head(v_ref), per_head(gate_ref),
                         gate_ref[:, A_WIDTH:], al_ref[...], dt_ref[...], nw_ref[...])
        ds, dq, dk, dv, dz, dba, dal, ddt, dnw = vjp((per_head(dy_ref), ds_scr[...]))
        ds_scr[...] = ds
        for hh in range(A_HEADS):
            cols = slice(hh * LANE, (hh + 1) * LANE)
            dqkv_ref[0, :, cols] = dq[hh]
            dqkv_ref[1, :, cols] = dk[hh]
            dqkv_ref[2, :, cols] = dv[hh]
            dgate_ref[:, cols] = dz[hh].astype(dgate_ref.dtype)
        dgate_ref[:, A_WIDTH:] = dba.astype(dgate_ref.dtype)
        dal_ref[...] += dal
        ddt_ref[...] += ddt
        dnw_ref[...] += dnw

    rev = lambda i: nsc - 1 - i
    blk = lambda j: pl.BlockSpec((SUPER, A_WIDTH), lambda i: (rev(i), j))
    gate = pl.BlockSpec((SUPER, L_GATE), lambda i: (rev(i), L_ZA // L_GATE))
    row = pl.BlockSpec((1, LANE), lambda i: (0, 0))
    mf, mb = _gdn_masks()
    whole = lambda a: pl.BlockSpec(a.shape, lambda i: (0, 0, 0))
    return _pcall(
        core, name=name, grid=(nsc,),
        in_specs=[blk(0), blk(1), blk(2), gate, row, row, row,
                  pl.BlockSpec((1, A_HEADS, A_HEAD_DIM, A_HEAD_DIM), lambda i: (rev(i), 0, 0, 0)),
                  pl.BlockSpec((1, A_HEADS, SUPER, SUPER), lambda i: (rev(i), 0, 0, 0)),
                  blk(0), whole(mf), whole(mb), _ANY],
        out_specs=[gate, pl.BlockSpec((3, SUPER, A_WIDTH), lambda i: (0, rev(i), 0)), row, row, row],
        out_shape=[jax.ShapeDtypeStruct((t_len, L_COLS), MM_DTYPE), jax.ShapeDtypeStruct((3, t_len, A_WIDTH), F32)]
        + [jax.ShapeDtypeStruct((1, LANE), F32)] * 3,
        scratch_shapes=[pltpu.VMEM((A_HEADS, A_HEAD_DIM, A_HEAD_DIM), F32)],
        aliases={12: 0}, sem=("arbitrary",), rider=rider,
        args=(qkv, qkv, qkv, h, alog, dtb, nw, s_in, t_in, dycat, mf, mb, dh))


Q_BLOCKS = 4
Q_ROWS = Q_BLOCKS * BLOCK


def _swa_block(q, kp, kc, vp, vc, z, sinks, first):
    rows = B_GROUP * BLOCK
    ri = lax.broadcasted_iota(jnp.int32, (rows, 2 * BLOCK), 0)
    si = lax.broadcasted_iota(jnp.int32, (rows, 2 * BLOCK), 1)
    dist = (ri & (BLOCK - 1)) + BLOCK - si
    bias = jnp.where((dist >= 0) & (dist < WINDOW), 0.0, -jnp.inf)
    no_prev = jnp.where(first & (si[:1] < BLOCK), -jnp.inf, 0.0)
    dist_f = dist.astype(F32)
    head_of_row = lax.broadcasted_iota(jnp.int32, (rows, 1), 0) >> 7
    keys = jnp.concatenate([kp, kc], axis=0)
    vals = jnp.concatenate([vp, vc], axis=0)

    def item(b, j):
        cs = slice(j * B_HEAD_DIM, (j + 1) * B_HEAD_DIM)
        rs = slice(b * BLOCK, (b + 1) * BLOCK)
        heads = range(j * B_GROUP, (j + 1) * B_GROUP)
        qs = jnp.concatenate([q[rs, hq * B_HEAD_DIM:(hq + 1) * B_HEAD_DIM] for hq in heads], axis=0) * (
            B_HEAD_DIM ** -0.5)
        kk = keys[b * BLOCK:(b + 2) * BLOCK, cs]
        vv = vals[b * BLOCK:(b + 2) * BLOCK, cs]
        sink = jnp.concatenate([jnp.broadcast_to(sinks[:, hq:hq + 1], (BLOCK, 1)) for hq in heads], axis=0)
        slope = sum(jnp.where(head_of_row == gi, 2.0 ** (-8.0 * (hq + 1) / B_Q_HEADS), 0.0)
                    for gi, hq in enumerate(heads))
        return qs, kk, vv, sink, slope, (no_prev if b == 0 else jnp.zeros_like(no_prev))

    def attend(qs, kk, vv, sink, slope, hide):
        sc = _mm_nt(qs, kk) - slope * dist_f + (bias + hide)
        m = lax.stop_gradient(jnp.maximum(jnp.max(sc, axis=-1, keepdims=True), sink))
        p = jnp.exp(sc - m)
        inv = 1.0 / (jnp.sum(p, axis=-1, keepdims=True) + jnp.exp(sink - m))
        return _mm(p * inv, vv)

    items = [(b, j) for b in range(Q_BLOCKS) for j in range(B_KV_HEADS)]
    o = jax.vmap(attend)(*[_stack(t) for t in zip(*[item(b, j) for b, j in items])])
    rows_out = [jnp.concatenate([o[b * B_KV_HEADS + j, gi * BLOCK:(gi + 1) * BLOCK]
                                 for j in range(B_KV_HEADS) for gi in range(B_GROUP)], axis=1)
                for b in range(Q_BLOCKS)]
    return jnp.concatenate(rows_out, axis=0) * _silu(z)


def _swa_specs(idx):
    wide = lambda off: pl.BlockSpec((Q_ROWS, B_WIDTH), lambda n: (idx(n), off))
    cur = lambda off: pl.BlockSpec((Q_ROWS, LANE), lambda n: (idx(n), off))
    prev = lambda off: pl.BlockSpec((BLOCK, LANE), lambda n: (jnp.maximum(idx(n) * Q_BLOCKS - 1, 0), off))
    return [wide(L_QB // B_WIDTH), prev(L_KB // LANE), cur(L_KB // LANE), prev(L_VB // LANE), cur(L_VB // LANE),
            wide(L_ZB // B_WIDTH), pl.BlockSpec((1, LANE), lambda n: (0, 0))]


def _swa_fwd(h, sinks, *, name, rider=None):
    t_len = h.shape[0]
    nb = t_len // Q_ROWS

    def core(ins, outs, _):
        q_ref, kp_ref, kc_ref, vp_ref, vc_ref, z_ref, s_ref = ins
        outs[0][...] = _swa_block(q_ref[...], kp_ref[...], kc_ref[...], vp_ref[...], vc_ref[...], z_ref[...],
                                  s_ref[...], pl.program_id(0) == 0)

    res = _pcall(core, name=name, grid=(nb,), in_specs=_swa_specs(lambda n: n),
                 out_specs=[pl.BlockSpec((Q_ROWS, B_WIDTH), lambda n: (n, 1))],
                 out_shape=[jax.ShapeDtypeStruct((t_len, D_MODEL), F32)], sem=("parallel",), rider=rider,
                 args=(h, h, h, h, h, h, sinks))
    return res if rider else res[0]


def _swa_bwd(h, sinks, dycat, *, name, rider=None):
    t_len = h.shape[0]
    nb = t_len // Q_ROWS
    early = slice(0, Q_ROWS - BLOCK)
    last = slice(Q_ROWS - BLOCK, Q_ROWS)

    def core(ins, outs, scr):
        q_ref, kp_ref, kc_ref, vp_ref, vc_ref, z_ref, s_ref, dy_ref = ins
        dh_ref, dsk_ref = outs
        ck_scr, cv_scr = scr
        i = pl.program_id(0)
        n = nb - 1 - i

        @pl.when(i == 0)
        def _():
            ck_scr[...] = jnp.zeros_like(ck_scr)
            cv_scr[...] = jnp.zeros_like(cv_scr)
            dsk_ref[...] = jnp.zeros_like(dsk_ref)

        fn = functools.partial(_swa_block, first=(n == 0))
        _, vjp = jax.vjp(fn, q_ref[...], kp_ref[...], kc_ref[...], vp_ref[...], vc_ref[...], z_ref[...], s_ref[...])
        dq, dkp, dkc, dvp, dvc, dz, dsk = vjp(dy_ref[...])
        def put(rows, col, val):
            dh_ref[rows, col:col + val.shape[1]] = val.astype(dh_ref.dtype)

        put(slice(None), L_QB, dq)
        put(slice(None), L_ZB, dz)
        put(early, L_KB, dkc[early])
        put(early, L_VB, dvc[early])
        put(last, L_KB, dkc[last] + ck_scr[...])
        put(last, L_VB, dvc[last] + cv_scr[...])
        ck_scr[...] = dkp
        cv_scr[...] = dvp
        dsk_ref[...] += dsk

    rev = lambda i: nb - 1 - i
    return _pcall(
        core, name=name, grid=(nb,),
        in_specs=_swa_specs(rev) + [pl.BlockSpec((Q_ROWS, B_WIDTH), lambda i: (rev(i), 1))],
        out_specs=[pl.BlockSpec((Q_ROWS, L_SWA), lambda i: (rev(i), 0)), pl.BlockSpec((1, LANE), lambda i: (0, 0))],
        out_shape=[jax.ShapeDtypeStruct((t_len, L_COLS), MM_DTYPE), jax.ShapeDtypeStruct((1, LANE), F32)],
        scratch_shapes=[pltpu.VMEM((BLOCK, LANE), F32), pltpu.VMEM((BLOCK, LANE), F32)],
        sem=("arbitrary",), rider=rider, args=(h, h, h, h, h, h, sinks, dycat))


def _out_ln_fwd(ycat, w_out, x, ln_g, ln_b, *, name, tm=512, last=False):
    t_len = x.shape[0]

    def body(y_ref, w_ref, x_ref, g_ref, b_ref, r_ref, *o_ref):
        r = DEEPNORM_ALPHA * x_ref[...] + _mm(y_ref[...], w_ref[...])
        r_ref[...] = r
        if not last:
            mu = jnp.mean(r, axis=-1, keepdims=True)
            d = r - mu
            var = jnp.mean(d * d, axis=-1, keepdims=True)
            o_ref[0][...] = d * lax.rsqrt(var + LN_EPS) * g_ref[...] + b_ref[...]

    tile = pl.BlockSpec((tm, D_MODEL), lambda i: (i, 0))
    vec = pl.BlockSpec((1, D_MODEL), lambda i: (0, 0))
    n_out = 1 if last else 2
    res = pl.pallas_call(
        body, name=name, grid=(t_len // tm,),
        in_specs=[tile, pl.BlockSpec((D_MODEL, D_MODEL), lambda i: (0, 0)), tile, vec, vec],
        out_specs=[tile] * n_out,
        out_shape=[jax.ShapeDtypeStruct((t_len, D_MODEL), F32)] * n_out,
        compiler_params=_cparams(("parallel",)),
    )(ycat, w_out, x, ln_g, ln_b)
    return (res[0], None) if last else res


def _ln_out_bwd(dxn, r, ln_g, ycat, w_out, *, name, tm=512, loss=None):
    t_len = r.shape[0]

    def body(*refs):
        if loss:
            t_ref, r_ref, g_ref, b_ref, y_ref, w_ref, dr_ref, dg_ref, db_ref, l_ref, dy_ref, dw_ref = refs
        else:
            dx_ref, r_ref, g_ref, y_ref, w_ref, dr_ref, dg_ref, db_ref, dy_ref, dw_ref = refs

        @pl.when(pl.program_id(0) == 0)
        def _():
            dg_ref[...] = jnp.zeros_like(dg_ref)
            db_ref[...] = jnp.zeros_like(db_ref)
            dw_ref[...] = jnp.zeros_like(dw_ref)
            if loss:
                l_ref[...] = jnp.zeros_like(l_ref)

        rr = r_ref[...]
        mu = jnp.mean(rr, axis=-1, keepdims=True)
        d = rr - mu
        rstd = lax.rsqrt(jnp.mean(d * d, axis=-1, keepdims=True) + LN_EPS)
        xh = d * rstd
        if loss:
            e = (xh * g_ref[...] + b_ref[...]) - t_ref[...]
            dx = e * (1.0 / D_MODEL)
            l_ref[...] += jnp.sum(e * e, axis=0, keepdims=True)
        else:
            dx = dx_ref[...]
        dxh = dx * g_ref[...]
        dr = rstd * (dxh - jnp.mean(dxh, axis=-1, keepdims=True) - xh * jnp.mean(dxh * xh, axis=-1, keepdims=True))
        dr_ref[...] = dr
        dg_ref[...] += jnp.sum(dx * xh, axis=0, keepdims=True)
        db_ref[...] += jnp.sum(dx, axis=0, keepdims=True)
        dy_ref[...] = _mm_nt(dr, w_ref[...])
        dw_ref[...] += _mm_tn(y_ref[...], dr)

    tile = pl.BlockSpec((tm, D_MODEL), lambda i: (i, 0))
    vec = pl.BlockSpec((1, D_MODEL), lambda i: (0, 0))
    square = pl.BlockSpec((D_MODEL, D_MODEL), lambda i: (0, 0))
    tile_shape = jax.ShapeDtypeStruct((t_len, D_MODEL), F32)
    vec_shape = jax.ShapeDtypeStruct((1, D_MODEL), F32)
    args = (loss[0], r, ln_g, loss[1]) if loss else (dxn, r, ln_g)
    return pl.pallas_call(
        body, name=name, grid=(t_len // tm,),
        in_specs=[tile, tile, vec] + ([vec] if loss else []) + [tile, square],
        out_specs=[tile, vec, vec] + ([vec] if loss else []) + [tile, square],
        out_shape=[tile_shape, vec_shape, vec_shape] + ([vec_shape] if loss else [])
        + [tile_shape, jax.ShapeDtypeStruct((D_MODEL, D_MODEL), F32)],
        compiler_params=_cparams(("arbitrary",)),
    )(*args, ycat, w_out)


def _pad_row(v):
    return jnp.zeros((1, LANE), F32).at[0, :v.shape[0]].set(v)


_REGIONS = ((0, 1536, L_QKV), (1536, 2048, L_ZA), (2048, 2056, L_BA), (2056, 2568, L_QB), (2568, 2696, L_KB),
            (2696, 2824, L_VB), (2824, 3336, L_ZB))


def _shard_pieces(regions):
    for a, b, off in regions:
        for d in range(N_DEV):
            lo, hi = max(a, d * SHARD_COLS), min(b, (d + 1) * SHARD_COLS)
            if lo < hi:
                yield d, lo - d * SHARD_COLS, hi - d * SHARD_COLS, off + lo - a


def _as_list(r):
    return list(r) if isinstance(r, (list, tuple)) else [r]


def _gathered(shard):
    return jax.ShapeDtypeStruct((N_DEV,) + shard.shape, shard.dtype)


def _full_w_in(g_in, name):
    by_offset = sorted(_shard_pieces(_REGIONS), key=lambda p: p[3])
    tc = 256

    def body(g_ref, o_ref):
        pieces, row = [], 0
        for d, lo, hi, off in by_offset + [(None, 0, 0, L_COLS)]:
            if off > row:
                pieces.append(jnp.zeros((off - row, tc), g_ref.dtype))
            if d is not None:
                pieces.append(g_ref[d, lo:hi, :])
            row = off + hi - lo
        o_ref[...] = jnp.concatenate(pieces, axis=0)

    return pl.pallas_call(
        body, name=name, grid=(D_MODEL // tc,),
        in_specs=[pl.BlockSpec((N_DEV, SHARD_COLS, tc), lambda i: (0, 0, i))],
        out_specs=pl.BlockSpec((L_COLS, tc), lambda i: (0, i)),
        out_shape=jax.ShapeDtypeStruct((L_COLS, D_MODEL), g_in.dtype),
        compiler_params=_cparams(("parallel",)),
    )(g_in)


def _full_conv(g_conv):
    return jnp.pad(g_conv.transpose(1, 0, 2).reshape(CONV_K, 3 * A_WIDTH), ((0, 8 - CONV_K), (0, 0)))


def _forward(x, weights, shards, small):
    a_log, dt_bias, norm_w, sinks, ln_g, ln_b = small
    tm = min(512, x.shape[0])
    saved, weights = [], [list(w) for w in weights]
    whole = lambda arrs: _Direct([(a, False, j, ()) for j, a in enumerate(arrs)], [_gathered(a) for a in arrs])
    for l in range(DEPTH):
        rider = whole(shards[l][1:]) if weights[l][1] is None else None
        h, *got = _as_list(_matmul(x, weights[l][0], form="nt", tm=tm, tn=L_COLS, tk=D_MODEL, name=f"in_proj_{l}",
                                   rider=rider))
        if rider:
            weights[l][1:] = [got[0].reshape(D_MODEL, D_MODEL), _full_conv(got[1])]
        w_in_l, w_out_l, conv_l = weights[l]
        qkv = _prep_fwd(h, conv_l, name=f"prep_fwd_{l}")
        al, dt, nw, sk = _pad_row(a_log[l]), _pad_row(dt_bias[l]), norm_w[l][None, :], _pad_row(sinks[l])
        ahead = l + 1 < DEPTH and weights[l + 1][0] is None
        rider = whole(shards[l + 1][1:]) if ahead else None
        ycat, *got = _as_list(_swa_fwd(h, sk, name=f"swa_fwd_{l}", rider=rider))
        if ahead:
            weights[l + 1][1:] = [got[0].reshape(D_MODEL, D_MODEL), _full_conv(got[1])]
        rider = whole(shards[l + 1][:1]) if ahead else None
        ycat, s_in, t_in, *got = _gdn_fwd(qkv, h, al, dt, nw, ycat, name=f"gdn_fwd_{l}", rider=rider)
        if ahead:
            weights[l + 1][0] = _full_w_in(got[0], f"w_in_rows_{l + 1}")
        r, xn = _out_ln_fwd(ycat, w_out_l, x, ln_g[l][None, :], ln_b[l][None, :], name=f"out_ln_{l}",
                            last=(l == DEPTH - 1))
        saved.append((x, h, qkv, s_in, t_in, ycat, r, al, dt, nw, sk))
        x = xn
    return x, saved, weights


def _w_in_blocks(g, name):
    cols, tc = g.shape[1], 256
    pieces = list(_shard_pieces(_REGIONS))

    def body(g_ref, o_ref):
        blocks = [[] for _ in range(N_DEV)]
        for d, lo, hi, off in pieces:
            blocks[d].append(g_ref[off:off + hi - lo, :])
        for d in range(N_DEV):
            o_ref[d] = jnp.concatenate(blocks[d], axis=0).astype(BF16)

    return pl.pallas_call(
        body, name=name, grid=(cols // tc,),
        in_specs=[pl.BlockSpec((L_COLS, tc), lambda i: (0, i))],
        out_specs=pl.BlockSpec((N_DEV, SHARD_COLS, tc), lambda i: (0, 0, i)),
        out_shape=jax.ShapeDtypeStruct((N_DEV, SHARD_COLS, cols), BF16),
        compiler_params=_cparams(("parallel",)),
    )(g)


def _small_blocks(g):
    c_conv = g["conv_w"].reshape(CONV_K, N_DEV, CONV_SHARD_COLS).transpose(1, 0, 2)
    c_small = [jnp.broadcast_to(g[n][None], (N_DEV,) + g[n].shape) for n, _ in SMALL_SIZES]
    return _pack_small(c_conv, c_small)


def _contributions(g):
    c_out = g["w_out"].astype(BF16).reshape(N_DEV, OUT_SHARD_ROWS, D_MODEL)
    return _w_in_blocks(g["w_in_rows"], name="w_in_grad_blocks_above"), c_out, _small_blocks(g)


def _backward_layer(l, dx, saved_l, weights_l, ln_g_l, above=None, loss=None):
    x_in, h, qkv, s_in, t_in, ycat, r, al, dt, nw, sk = saved_l
    w_in_l, w_out_l, conv_l = weights_l
    tm = min(512, x_in.shape[0])
    dr, d_lng, d_lnb, *loss_lanes, dycat, d_wout = _ln_out_bwd(dx, r, ln_g_l[None, :], ycat, w_out_l,
                                                               name=f"ln_out_bwd_{l}", tm=tm, loss=loss)
    big = min(1024, x_in.shape[0])
    rider, p_in, p_out, p_small = None, None, None, None
    recv = lambda c: jax.ShapeDtypeStruct((DEPTH,) + c.shape, c.dtype)
    if above:
        c_out = d_wout.astype(BF16).reshape(N_DEV, OUT_SHARD_ROWS, D_MODEL)
        rider = _Direct([(above[1], True, 0, (l + 1,)), (above[2], True, 1, (l + 1,)), (c_out, True, 0, (l,))],
                        [recv(above[1]), recv(above[2])])
    dh, d_sk, *got = _swa_bwd(h, sk, dycat, name=f"swa_bwd_{l}", rider=rider)
    if above:
        p_out, p_small = got
        rider = _Direct([(above[0], True, 0, (l + 1,))], [recv(above[0])])
    dh, dqkv_n, d_al, d_dt, d_nw, *got = _gdn_bwd(qkv, h, al, dt, nw, s_in, t_in, dycat, dh,
                                                  name=f"gdn_bwd_{l}", rider=rider)
    dh, d_conv = _prep_bwd(h, conv_l, dqkv_n, dh, name=f"prep_bwd_{l}")
    grads = dict(w_out=d_wout, conv_w=d_conv[:CONV_K], a_log=d_al[0, :A_HEADS], dt_bias=d_dt[0, :A_HEADS],
                 norm_w=d_nw[0], sinks=d_sk[0, :B_Q_HEADS], ln_g=d_lng[0], ln_b=d_lnb[0])
    dw = functools.partial(_matmul, dh, x_in, form="tn")
    if not above:
        grads["w_in_rows"] = dw(name=f"in_proj_dw_{l}", tm=L_COLS // 3, tn=D_MODEL, tk=min(2048, x_in.shape[0]))
    else:
        p_in, = got
        cut = D_MODEL // 2
        rest = D_MODEL - cut
        first = dw(name=f"in_proj_dw_first_{l}", tm=L_COLS, tn=cut, tk=big, b_cols=(0, cut))
        blocks = _w_in_blocks(first, name=f"w_in_grad_blocks_first_{l}")
        rider = _Direct([(blocks, True, 0, (l,), (slice(None), pl.ds(0, cut)))], [p_in])
        second, p_in = dw(name=f"in_proj_dw_second_{l}", tm=L_COLS, tn=cut, tk=big, b_cols=(cut, rest), rider=rider)
        blocks = _w_in_blocks(second, name=f"w_in_grad_blocks_second_{l}")
        rider = _Direct([(blocks, True, 0, (l,), (slice(None), pl.ds(cut, rest))),
                         (_small_blocks(grads), True, 1, (l,))], [p_in, p_small])
    dx, *got = _as_list(_matmul(dh, w_in_l, form="nn", tm=tm, tn=D_MODEL, tk=L_COLS, name=f"in_proj_dx_{l}",
                                add=dr, add_scale=DEEPNORM_ALPHA, rider=rider))
    bufs = (got[0], p_out, got[1]) if above else None
    return dx, grads, bufs, (loss_lanes[0] if loss else None)


def _all_gather(shards, *, name):
    n_arr = len(shards)

    def body(*refs):
        x_refs, out_refs = refs[:n_arr], refs[n_arr:2 * n_arr]
        send_sems, recv_sems, local_sems = refs[2 * n_arr:]
        x, y, c = _me()
        me, sibling = (x, y, c), (x, y, 1 - c)
        chips = [(1 - x, y), (x, 1 - y), (1 - x, 1 - y)]

        def copy(a, k, block, to, src=None):
            dst = out_refs[a].at[_flat_id(block)]
            return _remote(dst if src is None else src, dst, send_sems.at[a, k], recv_sems.at[a, k], to)

        mine = [pltpu.make_async_copy(x_refs[a], out_refs[a].at[_flat_id(me)], local_sems.at[a])
                for a in range(n_arr)]
        for cp in mine:
            cp.start()
        first = []
        for a in range(n_arr):
            first.append(copy(a, 0, me, sibling, src=x_refs[a]))
            first += [copy(a, 1 + j, me, (*chip, c), src=x_refs[a]) for j, chip in enumerate(chips)]
        for cp in first:
            cp.start()
        passed = []
        for j, chip in enumerate(chips):
            for a in range(n_arr):
                copy(a, 1 + j, (*chip, c), me).wait_recv()
                fwd = copy(a, 4 + j, (*chip, c), sibling)
                fwd.start()
                passed.append(fwd)
        for a in range(n_arr):
            copy(a, 0, sibling, me).wait_recv()
            for j, chip in enumerate(chips):
                copy(a, 4 + j, (*chip, 1 - c), me).wait_recv()
        for cp in first + passed:
            cp.wait_send()
        for cp in mine:
            cp.wait()

    return pl.pallas_call(
        body, name=name, in_specs=[_ANY] * n_arr, out_specs=[_ANY] * n_arr,
        out_shape=[jax.ShapeDtypeStruct((N_DEV,) + s.shape, s.dtype) for s in shards],
        scratch_shapes=[pltpu.SemaphoreType.DMA((n_arr, N_DEV - 1)), pltpu.SemaphoreType.DMA((n_arr, N_DEV - 1)),
                        pltpu.SemaphoreType.DMA((n_arr,))],
    )(*shards)


def _adamw(parts, w, m, v, *, tr, name):
    depth, rows, cols = w.shape
    c1 = 1.0 - ADAM_B1 ** ADAM_STEP
    c2 = 1.0 - ADAM_B2 ** ADAM_STEP

    def body(g_ref, w_ref, m_ref, v_ref, go_ref, d_ref, mo_ref, vo_ref):
        g = g_ref[0, 0].astype(F32)
        for s in range(1, N_DEV):
            g = g + g_ref[0, s].astype(F32)
        m_new = ADAM_B1 * m_ref[0] + (1.0 - ADAM_B1) * g
        v_new = ADAM_B2 * v_ref[0] + (1.0 - ADAM_B2) * (g * g)
        go_ref[0] = g
        mo_ref[0] = m_new
        vo_ref[0] = v_new
        d_ref[0] = -ADAM_LR * ((m_new / c1) / (jnp.sqrt(v_new / c2) + ADAM_EPS) + ADAM_WD * w_ref[0])

    tile = pl.BlockSpec((1, tr, cols), lambda l, i: (l, i, 0))
    return pl.pallas_call(
        body, name=name, grid=(depth, rows // tr),
        in_specs=[pl.BlockSpec((1, N_DEV, tr, cols), lambda l, i: (l, 0, i, 0)), tile, tile, tile],
        out_specs=[tile] * 4, out_shape=[jax.ShapeDtypeStruct(w.shape, F32)] * 4,
        compiler_params=_cparams(("parallel", "parallel")),
    )(parts, w, m, v)


def _adamw_w_in(parts, w, m, v, *, name):
    c1 = 1.0 - ADAM_B1 ** ADAM_STEP
    c2 = 1.0 - ADAM_B2 ** ADAM_STEP

    def body(g_ref, w_ref, m_ref, v_ref, go_ref, d_ref, mo_ref, vo_ref):
        gs = []
        for l in range(DEPTH):
            g = g_ref[l, 0].astype(F32)
            for s in range(1, N_DEV):
                g = g + g_ref[l, s].astype(F32)
            gs.append(g)
        g = jnp.stack(gs, axis=1)
        m_new = ADAM_B1 * m_ref[...] + (1.0 - ADAM_B1) * g
        v_new = ADAM_B2 * v_ref[...] + (1.0 - ADAM_B2) * (g * g)
        go_ref[...] = g
        mo_ref[...] = m_new
        vo_ref[...] = v_new
        d_ref[...] = -ADAM_LR * ((m_new / c1) / (jnp.sqrt(v_new / c2) + ADAM_EPS) + ADAM_WD * w_ref[...])

    tile = pl.BlockSpec((SHARD_COLS, DEPTH, LANE), lambda i: (0, 0, i))
    return pl.pallas_call(
        body, name=name, grid=(D_MODEL // LANE,),
        in_specs=[pl.BlockSpec((DEPTH, N_DEV, SHARD_COLS, LANE), lambda i: (0, 0, 0, i)), tile, tile, tile],
        out_specs=[tile] * 4, out_shape=[jax.ShapeDtypeStruct(w.shape, F32)] * 4,
        compiler_params=_cparams(("parallel",)),
    )(parts, w, m, v)


def _pack_small(conv, small):
    lead = conv.shape[:-2]
    flat = jnp.concatenate([conv.reshape(lead + (CS_CONV,))] + list(small), axis=-1)
    pad = CS_ROWS * LANE - flat.shape[-1]
    flat = jnp.concatenate([flat, jnp.zeros(lead + (pad,), F32)], axis=-1)
    return flat.reshape(lead + (CS_ROWS, LANE))


def _unpack_small(p):
    flat = p.reshape(DEPTH, CS_ROWS * LANE)
    conv = flat[:, :CS_CONV].reshape(DEPTH, CONV_K, CONV_SHARD_COLS)
    small, off = [], CS_CONV
    for _, n in SMALL_SIZES:
        small.append(flat[:, off:off + n])
        off += n
    return conv, small


def kernel(x, w_in, conv_w, a_log, dt_bias, norm_w, sinks, w_out, ln_g, ln_b, loss_target, m_w_in, m_conv_w, m_a_log, m_dt_bias, m_norm_w, m_sinks, m_w_out, m_ln_g, m_ln_b, v_w_in, v_conv_w, v_a_log, v_dt_bias, v_norm_w, v_sinks, v_w_out, v_ln_g, v_ln_b):
    small = [a_log, dt_bias, norm_w, sinks, ln_g, ln_b]
    w_t, m_t, v_t = (a.transpose(2, 0, 1) for a in (w_in, m_w_in, v_w_in))
    shards = [[w_t[:, l].astype(BF16), w_out[l].astype(BF16), conv_w[l]] for l in range(DEPTH)]
    g_in0, = _all_gather(shards[0][:1], name="weights_all_gather_0")
    weights = [[_full_w_in(g_in0, "w_in_rows_0"), None, None]] + [[None, None, None]] * (DEPTH - 1)

    _, saved, weights = _forward(x[0], weights, shards, small)
    dx, g1, _, loss_lanes = _backward_layer(1, None, saved[1], weights[1], ln_g[1],
                                            loss=(loss_target[0], ln_b[1][None, :]))
    loss = lax.psum(0.5 * jnp.sum(loss_lanes) * (1.0 / D_MODEL), ("x", "y", "c"))
    dx, _, (p_in, p_out, p_small), _ = _backward_layer(0, dx, saved[0], weights[0], ln_g[0],
                                                       above=_contributions(g1))

    o_in = [o.transpose(1, 2, 0) for o in _adamw_w_in(p_in, w_t, m_t, v_t, name="adamw_w_in")]
    o_out = _adamw(p_out, w_out, m_w_out, v_w_out, tr=OUT_SHARD_ROWS, name="adamw_w_out")
    o_small = _adamw(p_small, _pack_small(conv_w, small),
                     _pack_small(m_conv_w, [m_a_log, m_dt_bias, m_norm_w, m_sinks, m_ln_g, m_ln_b]),
                     _pack_small(v_conv_w, [v_a_log, v_dt_bias, v_norm_w, v_sinks, v_ln_g, v_ln_b]),
                     tr=CS_ROWS, name="adamw_small")
    outs = []
    for k in range(4):
        cv, sm = _unpack_small(o_small[k])
        outs += [o_in[k], cv, sm[0], sm[1], sm[2], sm[3], o_out[k], sm[4], sm[5]]
    return (loss, dx[None], *outs)
```

```python
import functools

import jax
import jax.numpy as jnp
from jax import lax
from jax.experimental import pallas as pl
from jax.experimental.pallas import tpu as pltpu

F32 = jnp.float32
BF16 = jnp.bfloat16
MM_DTYPE = BF16

N_DEV = 8
D_MODEL = 1024
DEPTH = 2
A_HEADS = 4
A_HEAD_DIM = 128
A_WIDTH = 512
CONV_K = 4
SUPER = 256
NEWTON_STEPS = 1
B_Q_HEADS = 8
B_KV_HEADS = 2
B_HEAD_DIM = 64
B_GROUP = 4
B_WIDTH = 512
WINDOW = 128
BLOCK = 128
IN_COLS = 3336
SHARD_COLS = IN_COLS // N_DEV
OUT_SHARD_ROWS = D_MODEL // N_DEV
CONV_SHARD_COLS = 3 * A_WIDTH // N_DEV
DEEPNORM_ALPHA = (2 * DEPTH) ** 0.25
LN_EPS = 1e-5
RMS_EPS = 1e-6
L2_EPS = 1e-6
ADAM_LR, ADAM_B1, ADAM_B2, ADAM_EPS, ADAM_WD, ADAM_STEP = 0.001, 0.9, 0.999, 1e-08, 0.01, 10

LANE = 128
L_QB, L_ZB, L_KB, L_VB, L_ZA, L_BA, L_QKV = 0, 512, 1024, 1152, 1280, 1792, 1920
L_SWA = 1280
L_GATE = 640
L_COLS = 3456
SMALL_SIZES = (("a_log", 4), ("dt_bias", 4), ("norm_w", 128), ("sinks", 8), ("ln_g", 1024), ("ln_b", 1024))
CS_CONV = CONV_K * CONV_SHARD_COLS
CS_ROWS = 24
VMEM_LIMIT = 48 * 1024 * 1024


def _cparams(sem=None):
    return pltpu.CompilerParams(dimension_semantics=sem, vmem_limit_bytes=VMEM_LIMIT)


def _mm(a, b):
    return jnp.dot(a.astype(MM_DTYPE), b.astype(MM_DTYPE), preferred_element_type=F32)


def _mm_nt(a, b):
    return lax.dot_general(a.astype(MM_DTYPE), b.astype(MM_DTYPE), (((1,), (1,)), ((), ())),
                           preferred_element_type=F32)


def _mm_tn(a, b):
    return lax.dot_general(a.astype(MM_DTYPE), b.astype(MM_DTYPE), (((0,), (0,)), ((), ())),
                           preferred_element_type=F32)


def _split(a):
    hi = a.astype(BF16)
    return hi, (a - hi.astype(F32)).astype(BF16)


def _silu(x):
    return x * jax.nn.sigmoid(x)


@jax.custom_vjp
def _stack(parts):
    return jnp.stack(parts)


_stack.defvjp(lambda parts: (jnp.stack(parts), None), lambda _, g: (tuple(g[i] for i in range(g.shape[0])),))


def _softplus(x):
    return jnp.maximum(x, 0.0) + jnp.log1p(jnp.exp(-jnp.abs(x)))


_ANY = pl.BlockSpec(memory_space=pl.ANY)


def _me():
    return lax.axis_index("x"), lax.axis_index("y"), lax.axis_index("c")


def _flat_id(pos):
    return 4 * pos[0] + 2 * pos[1] + pos[2]


def _remote(src, dst, send_sem, recv_sem, to):
    return pltpu.make_async_remote_copy(src_ref=src, dst_ref=dst, send_sem=send_sem, recv_sem=recv_sem,
                                        device_id=to, device_id_type=pl.DeviceIdType.MESH)


class _Direct:
    def __init__(self, items, bufs):
        self.items, self.bufs = list(items), list(bufs)
        self.n_src, self.n_buf = len(self.items), len(self.bufs)
        self.old = [j for j, b in enumerate(self.bufs) if not isinstance(b, jax.ShapeDtypeStruct)]
        self.args = [it[0] for it in self.items] + [self.bufs[j] for j in self.old]
        self.out_shape = [jax.ShapeDtypeStruct(b.shape, b.dtype) for b in self.bufs]
        self.scratch = [pltpu.SemaphoreType.DMA((self.n_src, N_DEV - 1)),
                        pltpu.SemaphoreType.DMA((self.n_src, N_DEV - 1)), pltpu.SemaphoreType.DMA((self.n_src,))]

    def aliases(self, in_base, out_base):
        return {in_base + self.n_src + pos: out_base + j for pos, j in enumerate(self.old)}

    def copies(self, in_refs, out_refs, sems):
        send_sems, recv_sems, local_sems = sems
        x, y, c = _me()
        me = _flat_id((x, y, c))
        peers = [(x ^ ((rel >> 2) & 1), y ^ ((rel >> 1) & 1), c ^ (rel & 1)) for rel in range(1, N_DEV)]
        local, sends, recvs = [], [], []
        for a, (_, per_dest, j, prefix, *rest) in enumerate(self.items):
            src = lambda d: in_refs[a].at[d] if per_dest else in_refs[a]
            dst = lambda s: out_refs[j].at[tuple(prefix) + (s,) + tuple(rest[0] if rest else ())]
            local.append(pltpu.make_async_copy(src(me), dst(me), local_sems.at[a]))
            for k, peer in enumerate(peers):
                pid = _flat_id(peer)
                sends.append(_remote(src(pid), dst(me), send_sems.at[a, k], recv_sems.at[a, k], peer))
                recvs.append(_remote(src(pid), dst(pid), send_sems.at[a, k], recv_sems.at[a, k], peer))
        return local, sends, recvs

    def start(self, in_refs, out_refs, sems):
        local, sends, _ = self.copies(in_refs, out_refs, sems)
        for cp in local + sends:
            cp.start()

    def wait(self, in_refs, out_refs, sems):
        local, sends, recvs = self.copies(in_refs, out_refs, sems)
        for cp in recvs:
            cp.wait_recv()
        for cp in sends:
            cp.wait_send()
        for cp in local:
            cp.wait()


def _pcall(core, *, name, grid, in_specs, out_specs, out_shape, args, sem, scratch_shapes=(), aliases=None,
           rider=None):
    n_in, n_out, n_scr = len(in_specs), len(out_specs), len(scratch_shapes)
    n_rin, n_rout = (len(rider.args), rider.n_buf) if rider else (0, 0)

    def body(*refs):
        ins, outs = refs[:n_in], refs[n_in + n_rin:n_in + n_rin + n_out]
        scr = refs[n_in + n_rin + n_out + n_rout:n_in + n_rin + n_out + n_rout + n_scr]
        if rider:
            r_refs = (refs[n_in:n_in + rider.n_src], refs[n_in + n_rin + n_out:n_in + n_rin + n_out + n_rout],
                      refs[n_in + n_rin + n_out + n_rout + n_scr:])
            ids = [pl.program_id(d) for d in range(len(grid))]
            first = functools.reduce(lambda p, q: p & q, [i == 0 for i in ids])
            last = functools.reduce(lambda p, q: p & q, [i == g - 1 for i, g in zip(ids, grid)])
            pl.when(first)(lambda: rider.start(*r_refs))
        core(ins, outs, scr)
        if rider:
            pl.when(last)(lambda: rider.wait(*r_refs))

    aliases = dict(aliases or {})
    if rider:
        sem = ("arbitrary",) * len(grid)
        aliases.update(rider.aliases(n_in, n_out))
    return pl.pallas_call(
        body, name=name, grid=grid, in_specs=list(in_specs) + [_ANY] * n_rin,
        out_specs=list(out_specs) + [_ANY] * n_rout,
        out_shape=list(out_shape) + (rider.out_shape if rider else []),
        scratch_shapes=list(scratch_shapes) + (rider.scratch if rider else []),
        input_output_aliases=aliases, compiler_params=_cparams(sem),
    )(*args, *(rider.args if rider else []))


def _exchange(direct, *, name):
    n_in = len(direct.args)

    def body(*refs):
        r_refs = refs[:direct.n_src], refs[n_in:n_in + direct.n_buf], refs[n_in + direct.n_buf:]
        direct.start(*r_refs)
        direct.wait(*r_refs)

    return pl.pallas_call(
        body, name=name, in_specs=[_ANY] * n_in, out_specs=[_ANY] * direct.n_buf, out_shape=direct.out_shape,
        input_output_aliases=direct.aliases(0, 0), scratch_shapes=direct.scratch,
    )(*direct.args)


def _matmul(a, b, *, form, tm, tn, tk, name, add=None, add_scale=1.0, rider=None, b_cols=None):
    if form == "nn":
        (m, kk), n = a.shape, b.shape[1]
        a_spec = pl.BlockSpec((tm, tk), lambda i, j, k: (i, k))
        b_spec = pl.BlockSpec((tk, tn), lambda i, j, k: (k, j))
        dn = (((1,), (0,)), ((), ()))
    elif form == "nt":
        (m, kk), n = a.shape, b.shape[0]
        a_spec = pl.BlockSpec((tm, tk), lambda i, j, k: (i, k))
        b_spec = pl.BlockSpec((tn, tk), lambda i, j, k: (j, k))
        dn = (((1,), (1,)), ((), ()))
    else:
        kk, m = a.shape
        n0, n = b_cols or (0, b.shape[1])
        assert n0 % tn == 0
        a_spec = pl.BlockSpec((tk, tm), lambda i, j, k: (k, i))
        b_spec = pl.BlockSpec((tk, tn), lambda i, j, k: (k, j + n0 // tn))
        dn = (((0,), (0,)), ((), ()))
    assert m % tm == 0 and n % tn == 0 and kk % tk == 0, (name, m, n, kk)
    has_add = add is not None

    def core(ins, outs, _):
        a_ref, b_ref = ins[:2]
        o_ref = outs[0]
        k = pl.program_id(2)
        p = lax.dot_general(a_ref[...].astype(MM_DTYPE), b_ref[...].astype(MM_DTYPE), dn,
                            preferred_element_type=F32)

        @pl.when(k == 0)
        def _():
            o_ref[...] = p + add_scale * ins[2][...] if has_add else p

        @pl.when(k > 0)
        def _():
            o_ref[...] += p

    in_specs = [a_spec, b_spec]
    args = [a, b]
    if has_add:
        in_specs.append(pl.BlockSpec((tm, tn), lambda i, j, k: (i, j)))
        args.append(add)
    res = _pcall(core, name=name, grid=(m // tm, n // tn, kk // tk), in_specs=in_specs,
                 out_specs=[pl.BlockSpec((tm, tn), lambda i, j, k: (i, j))],
                 out_shape=[jax.ShapeDtypeStruct((m, n), F32)], args=args,
                 sem=("parallel", "parallel", "arbitrary"), rider=rider)
    return res if rider else res[0]


ZERO_TAIL = 8


def _with_tail(x):
    return jnp.concatenate([x, jnp.zeros((ZERO_TAIL,) + x.shape[1:], x.dtype)], axis=0)


def _shift_down(x, k):
    return pltpu.roll(x, k, 0)


def _shift_up(x, k):
    return pltpu.roll(x, x.shape[0] - k, 0)


def _conv_slab(x, w):
    return w[3:4] * x + w[2:3] * _shift_down(x, 1) + w[1:2] * _shift_down(x, 2) + w[0:1] * _shift_down(x, 3)


def _prep_fwd(h, conv_w, *, name):
    t_len = h.shape[0]

    def body(x_ref, w_ref, o_ref):
        s = pl.program_id(0)
        y = _silu(_conv_slab(_with_tail(x_ref[...]), w_ref[...])[:t_len])
        rs = lax.rsqrt(jnp.sum(y * y, axis=-1, keepdims=True) + L2_EPS)
        scale = jnp.where(s < A_HEADS, A_HEAD_DIM ** -0.5, 1.0)
        o_ref[...] = jnp.where(s < 2 * A_HEADS, y * rs * scale, y)

    return pl.pallas_call(
        body, name=name, grid=(12,),
        in_specs=[pl.BlockSpec((t_len, LANE), lambda s: (0, L_QKV // LANE + s)),
                  pl.BlockSpec((8, LANE), lambda s: (0, s))],
        out_specs=pl.BlockSpec((t_len, LANE), lambda s: (0, s)),
        out_shape=jax.ShapeDtypeStruct((t_len, 3 * A_WIDTH), F32),
        compiler_params=_cparams(("parallel",)),
    )(h, conv_w)


def _prep_bwd(h, conv_w, d_out, dh, *, name):
    t_len = h.shape[0]

    def body(x_ref, w_ref, g_ref, dh_in, dx_ref, dw_ref):
        del dh_in
        s = pl.program_id(0)
        x = _with_tail(x_ref[...])
        g = _with_tail(g_ref[0])
        w = w_ref[...]
        xs = [_shift_down(x, 3), _shift_down(x, 2), _shift_down(x, 1), x]
        c = w[0:1] * xs[0] + w[1:2] * xs[1] + w[2:3] * xs[2] + w[3:4] * xs[3]
        sg = jax.nn.sigmoid(c)
        y = c * sg
        rs = lax.rsqrt(jnp.sum(y * y, axis=-1, keepdims=True) + L2_EPS)
        scale = jnp.where(s < A_HEADS, A_HEAD_DIM ** -0.5, 1.0)
        dy_n = scale * (rs * g - y * (rs * rs * rs) * jnp.sum(g * y, axis=-1, keepdims=True))
        dy = jnp.where(s < 2 * A_HEADS, dy_n, g)
        dc = dy * (sg * (1.0 + c * (1.0 - sg)))
        dx = w[3:4] * dc + w[2:3] * _shift_up(dc, 1) + w[1:2] * _shift_up(dc, 2) + w[0:1] * _shift_up(dc, 3)
        dx_ref[...] = dx[:t_len].astype(dx_ref.dtype)
        dws = [jnp.sum(dc * xs[j], axis=0, keepdims=True) for j in range(CONV_K)]
        dw_ref[...] = jnp.concatenate(dws + [jnp.zeros((8 - CONV_K, LANE), F32)], axis=0)

    slab = pl.BlockSpec((t_len, LANE), lambda s: (0, L_QKV // LANE + s))
    return pl.pallas_call(
        body, name=name, grid=(12,),
        in_specs=[slab, pl.BlockSpec((8, LANE), lambda s: (0, s)),
                  pl.BlockSpec((1, t_len, LANE), lambda s: (s // A_HEADS, 0, s % A_HEADS)), _ANY],
        out_specs=[slab, pl.BlockSpec((8, LANE), lambda s: (0, s))],
        out_shape=[jax.ShapeDtypeStruct((t_len, L_COLS), MM_DTYPE), jax.ShapeDtypeStruct((8, 3 * A_WIDTH), F32)],
        input_output_aliases={3: 0},
        compiler_params=_cparams(("parallel",)),
    )(h, conv_w, d_out, dh)


N_LEVELS = 5
MF_TRIL, MF_STRIL, MF_DIAG8, MF_LOW16, MF_EYE = 0, 1, 2, 3, 3 + N_LEVELS
MB_CUM, MB_CUM_T, MB_TOT = 0, 1, 2


def _gdn_masks():
    r = lax.broadcasted_iota(jnp.int32, (SUPER, SUPER), 0)
    c = lax.broadcasted_iota(jnp.int32, (SUPER, SUPER), 1)
    same = lambda shift: (r >> shift) == (c >> shift)
    ninf = lambda m: jnp.where(m, 0.0, -jnp.inf).astype(F32)
    one = lambda m: m.astype(F32)
    mf = jnp.stack([ninf(r >= c), ninf(r > c), one(same(3))]
                   + [one(same(4 + lv) & jnp.logical_not(same(3 + lv))) for lv in range(N_LEVELS)] + [one(r == c)])
    mb = jnp.stack([one(r >= c), one(r <= c), jnp.ones((SUPER, SUPER), F32)]).astype(BF16)
    return mf, mb


def _tri_inv_impl(a, mf):
    d = lambda p, q: jnp.dot(p.astype(BF16), q.astype(BF16), preferred_element_type=F32)
    dd = lambda p, q: jnp.dot(p, q, preferred_element_type=F32)
    eye = mf[MF_EYE]
    a0 = a * mf[MF_DIAG8]
    a2 = d(a0, a0)
    a4 = d(a2, a2)
    t = d(d(eye - a0, eye + a2), eye + a4)
    for level in range(N_LEVELS):
        t = t - d(d(t, a * mf[MF_LOW16 + level]), t)
    a_hi, a_lo = _split(a)
    for _ in range(NEWTON_STEPS):
        t0 = t.astype(BF16)
        t0f = t0.astype(F32)
        resid = (eye - t0f) - (dd(a_hi, t0) + dd(a_lo, t0))
        r_hi, r_lo = _split(resid)
        t = t0f + (dd(t0, r_hi) + dd(t0, r_lo))
    return t


@jax.custom_vjp
def _wy_apply(a, rhs, t):
    return _mm(t, rhs)


def _wy_apply_fwd(a, rhs, t):
    x = _mm(t, rhs)
    return x, (t, x)


def _wy_apply_bwd(res, dx):
    t, x = res
    d_rhs = _mm_tn(t, dx)
    return -_mm_nt(d_rhs, x), d_rhs, jnp.zeros_like(t)


_wy_apply.defvjp(_wy_apply_fwd, _wy_apply_bwd)


@functools.partial(jax.custom_vjp, nondiff_argnums=(1,))
def _lane_roll(x, shift):
    return pltpu.roll(x, shift % LANE, 1)


_lane_roll.defvjp(lambda x, shift: (_lane_roll(x, shift), None), lambda shift, _, g: (_lane_roll(g, -shift),))


def _mask_times_lanes(x, mask):
    lane = lax.broadcasted_iota(jnp.int32, (1, LANE), 1)
    x = jnp.where(lane < A_HEADS, x, 0.0)
    x1 = x.astype(BF16).astype(F32)
    x2 = (x - x1).astype(BF16).astype(F32)
    x3 = (x - x1 - x2).astype(BF16).astype(F32)
    pieces = x1 + pltpu.roll(x2, A_HEADS, 1) + pltpu.roll(x3, 2 * A_HEADS, 1)
    res = jnp.dot(mask, pieces.astype(BF16), preferred_element_type=F32)
    return res + pltpu.roll(res, LANE - A_HEADS, 1) + pltpu.roll(res, LANE - 2 * A_HEADS, 1)


@jax.custom_vjp
def _chunk_sums(g, mb):
    return _mask_times_lanes(g, mb[MB_CUM]), _mask_times_lanes(g, mb[MB_TOT])


def _chunk_sums_fwd(g, mb):
    return _chunk_sums(g, mb), mb


def _chunk_sums_bwd(mb, d):
    lane = lax.broadcasted_iota(jnp.int32, (1, LANE), 1)
    dg = _mask_times_lanes(d[0], mb[MB_CUM_T]) + _mask_times_lanes(d[1], mb[MB_TOT])
    return jnp.where(lane < A_HEADS, dg, 0.0), jnp.zeros_like(mb)


_chunk_sums.defvjp(_chunk_sums_fwd, _chunk_sums_bwd)


def _gdn_gates(ba, alog, dtb, mb):
    beta = jax.nn.sigmoid(ba)
    g = -jnp.exp(alog) * _softplus(_lane_roll(ba, -A_HEADS) + dtb)
    gc, gl = _chunk_sums(g, mb)
    return beta, gc, gl, gc.T


def _gdn_block(s, q, k, v, z, gates, nw, h, t_known, mf):
    n = q.shape[0]
    beta_all, gc_all, gl_all, gct_all = gates
    lane = lax.broadcasted_iota(jnp.int32, (1, LANE), 1)
    sub = lax.broadcasted_iota(jnp.int32, (LANE, 1), 0)
    col = lambda x: jnp.sum(jnp.where(lane == h, x, 0.0), axis=1, keepdims=True)
    wide = lambda c: jnp.broadcast_to(c, (n, LANE))
    gc, gl = col(gc_all), col(gl_all)
    gc_row = jnp.sum(jnp.where(sub == h, gct_all, 0.0), axis=0, keepdims=True)
    beta_w, eg_w = wide(col(beta_all)), wide(jnp.exp(gc))
    diff = gc - gc_row
    decay = jnp.exp(diff + mf[MF_TRIL])
    kb = k * beta_w
    a_mat = _mm_nt(kb, k) * jnp.exp(diff + mf[MF_STRIL])
    rhs = jnp.concatenate([v * beta_w, kb * eg_w], axis=1)
    if t_known is None:
        t_mat = _tri_inv_impl(a_mat, mf)
        uw = _mm(t_mat, rhs)
    else:
        t_mat = t_known
        uw = _wy_apply(a_mat, rhs, t_known)
    u, w = uw[:, :LANE], uw[:, LANE:]
    qk = _mm_nt(q, k) * decay
    q_dec = q * eg_w
    k_dec = k * wide(jnp.exp(gl - gc))
    v_new = u - _mm(w, s)
    o = _mm(q_dec, s) + _mm(qk, v_new)
    s = s * jnp.exp(gl[0:1]) + _mm_tn(k_dec, v_new)
    o = o * lax.rsqrt(jnp.mean(o * o, axis=-1, keepdims=True) + RMS_EPS) * nw
    return o * _silu(z), s, t_mat


def _gdn_fwd(qkv, h, alog, dtb, nw, ycat, *, name, rider=None):
    t_len = qkv.shape[0]
    nsc = t_len // SUPER

    def core(ins, outs, scr):
        q_ref, k_ref, v_ref, gate_ref, al_ref, dt_ref, nw_ref, mf_ref, mb_ref, _ = ins
        y_ref, sin_ref, t_ref = outs
        s_scr, = scr

        @pl.when(pl.program_id(0) == 0)
        def _():
            s_scr[...] = jnp.zeros_like(s_scr)

        per_head = lambda ref: jnp.stack([ref[:, hh * LANE:(hh + 1) * LANE] for hh in range(A_HEADS)])
        states = s_scr[...]
        gates = _gdn_gates(gate_ref[:, A_WIDTH:], al_ref[...], dt_ref[...], mb_ref[...])
        fn = jax.vmap(_gdn_block, in_axes=(0, 0, 0, 0, 0, None, None, 0, None, None))
        y, s_new, t_mat = fn(states, per_head(q_ref), per_head(k_ref), per_head(v_ref), per_head(gate_ref),
                             gates, nw_ref[...], jnp.arange(A_HEADS), None, mf_ref[...])
        sin_ref[0] = states
        t_ref[0] = t_mat
        s_scr[...] = s_new
        for hh in range(A_HEADS):
            y_ref[:, hh * LANE:(hh + 1) * LANE] = y[hh].astype(y_ref.dtype)

    blk = lambda j: pl.BlockSpec((SUPER, A_WIDTH), lambda sc: (sc, j))
    row = pl.BlockSpec((1, LANE), lambda sc: (0, 0))
    mf, mb = _gdn_masks()
    whole = lambda a: pl.BlockSpec(a.shape, lambda sc: (0, 0, 0))
    return _pcall(
        core, name=name, grid=(nsc,),
        in_specs=[blk(0), blk(1), blk(2), pl.BlockSpec((SUPER, L_GATE), lambda sc: (sc, L_ZA // L_GATE)),
                  row, row, row, whole(mf), whole(mb), _ANY],
        out_specs=[blk(0),
                   pl.BlockSpec((1, A_HEADS, A_HEAD_DIM, A_HEAD_DIM), lambda sc: (sc, 0, 0, 0)),
                   pl.BlockSpec((1, A_HEADS, SUPER, SUPER), lambda sc: (sc, 0, 0, 0))],
        out_shape=[jax.ShapeDtypeStruct((t_len, D_MODEL), MM_DTYPE),
                   jax.ShapeDtypeStruct((nsc, A_HEADS, A_HEAD_DIM, A_HEAD_DIM), F32),
                   jax.ShapeDtypeStruct((nsc, A_HEADS, SUPER, SUPER), F32)],
        scratch_shapes=[pltpu.VMEM((A_HEADS, A_HEAD_DIM, A_HEAD_DIM), F32)],
        aliases={9: 0}, sem=("arbitrary",), rider=rider,
        args=(qkv, qkv, qkv, h, alog, dtb, nw, mf, mb, ycat))


def _gdn_bwd(qkv, h, alog, dtb, nw, s_in, t_in, dycat, dh, *, name, rider=None):
    t_len = qkv.shape[0]
    nsc = t_len // SUPER

    def core(ins, outs, scr):
        q_ref, k_ref, v_ref, gate_ref, al_ref, dt_ref, nw_ref, sin_ref, t_ref, dy_ref, mf_ref, mb_ref, _ = ins
        dgate_ref, dqkv_ref, dal_ref, ddt_ref, dnw_ref = outs
        ds_scr, = scr

        @pl.when(pl.program_id(0) == 0)
        def _():
            ds_scr[...] = jnp.zeros_like(ds_scr)
            dal_ref[...] = jnp.zeros_like(dal_ref)
            ddt_ref[...] = jnp.zeros_like(ddt_ref)
            dnw_ref[...] = jnp.zeros_like(dnw_ref)

        per_head = lambda ref: jnp.stack([ref[:, hh * LANE:(hh + 1) * LANE] for hh in range(A_HEADS)])
        head_ids = jnp.arange(A_HEADS)
        t_known, mf, mb = t_ref[0], mf_ref[...], mb_ref[...]

        def fn(s, q, k, v, z, ba, alog, dtb, nw):
            gates = _gdn_gates(ba, alog, dtb, mb)
            one = lambda s, q, k, v, z, t, h: _gdn_block(s, q, k, v, z, gates, nw, h, t, mf)[:2]
            return jax.vmap(one)(s, q, k, v, z, t_known, head_ids)

        _, vjp = jax.vjp(fn, sin_ref[0], per_head(q_ref), per_head(k_ref), per_head(v_ref), per_head(gate_ref),
                         gate_ref[:, A_WIDTH:], al_ref[...], dt_ref[...], nw_ref[...])
        ds, dq, dk, dv, dz, dba, dal, ddt, dnw = vjp((per_head(dy_ref), ds_scr[...]))
        ds_scr[...] = ds
        for hh in range(A_HEADS):
            cols = slice(hh * LANE, (hh + 1) * LANE)
            dqkv_ref[0, :, cols] = dq[hh]
            dqkv_ref[1, :, cols] = dk[hh]
            dqkv_ref[2, :, cols] = dv[hh]
            dgate_ref[:, cols] = dz[hh].astype(dgate_ref.dtype)
        dgate_ref[:, A_WIDTH:] = dba.astype(dgate_ref.dtype)
        dal_ref[...] += dal
        ddt_ref[...] += ddt
        dnw_ref[...] += dnw

    rev = lambda i: nsc - 1 - i
    blk = lambda j: pl.BlockSpec((SUPER, A_WIDTH), lambda i: (rev(i), j))
    gate = pl.BlockSpec((SUPER, L_GATE), lambda i: (rev(i), L_ZA // L_GATE))
    row = pl.BlockSpec((1, LANE), lambda i: (0, 0))
    mf, mb = _gdn_masks()
    whole = lambda a: pl.BlockSpec(a.shape, lambda i: (0, 0, 0))
    return _pcall(
        core, name=name, grid=(nsc,),
        in_specs=[blk(0), blk(1), blk(2), gate, row, row, row,
                  pl.BlockSpec((1, A_HEADS, A_HEAD_DIM, A_HEAD_DIM), lambda i: (rev(i), 0, 0, 0)),
                  pl.BlockSpec((1, A_HEADS, SUPER, SUPER), lambda i: (rev(i), 0, 0, 0)),
                  blk(0), whole(mf), whole(mb), _ANY],
        out_specs=[gate, pl.BlockSpec((3, SUPER, A_WIDTH), lambda i: (0, rev(i), 0)), row, row, row],
        out_shape=[jax.ShapeDtypeStruct((t_len, L_COLS), MM_DTYPE), jax.ShapeDtypeStruct((3, t_len, A_WIDTH), F32)]
        + [jax.ShapeDtypeStruct((1, LANE), F32)] * 3,
        scratch_shapes=[pltpu.VMEM((A_HEADS, A_HEAD_DIM, A_HEAD_DIM), F32)],
        aliases={12: 0}, sem=("arbitrary",), rider=rider,
        args=(qkv, qkv, qkv, h, alog, dtb, nw, s_in, t_in, dycat, mf, mb, dh))


Q_BLOCKS = 4
Q_ROWS = Q_BLOCKS * BLOCK


def _swa_block(q, kp, kc, vp, vc, z, sinks, first):
    rows = B_GROUP * BLOCK
    ri = lax.broadcasted_iota(jnp.int32, (rows, 2 * BLOCK), 0)
    si = lax.broadcasted_iota(jnp.int32, (rows, 2 * BLOCK), 1)
    dist = (ri & (BLOCK - 1)) + BLOCK - si
    bias = jnp.where((dist >= 0) & (dist < WINDOW), 0.0, -jnp.inf)
    no_prev = jnp.where(first & (si[:1] < BLOCK), -jnp.inf, 0.0)
    dist_f = dist.astype(F32)
    head_of_row = lax.broadcasted_iota(jnp.int32, (rows, 1), 0) >> 7
    keys = jnp.concatenate([kp, kc], axis=0)
    vals = jnp.concatenate([vp, vc], axis=0)

    def item(b, j):
        cs = slice(j * B_HEAD_DIM, (j + 1) * B_HEAD_DIM)
        rs = slice(b * BLOCK, (b + 1) * BLOCK)
        heads = range(j * B_GROUP, (j + 1) * B_GROUP)
        qs = jnp.concatenate([q[rs, hq * B_HEAD_DIM:(hq + 1) * B_HEAD_DIM] for hq in heads], axis=0) * (
            B_HEAD_DIM ** -0.5)
        kk = keys[b * BLOCK:(b + 2) * BLOCK, cs]
        vv = vals[b * BLOCK:(b + 2) * BLOCK, cs]
        sink = jnp.concatenate([jnp.broadcast_to(sinks[:, hq:hq + 1], (BLOCK, 1)) for hq in heads], axis=0)
        slope = sum(jnp.where(head_of_row == gi, 2.0 ** (-8.0 * (hq + 1) / B_Q_HEADS), 0.0)
                    for gi, hq in enumerate(heads))
        return qs, kk, vv, sink, slope, (no_prev if b == 0 else jnp.zeros_like(no_prev))

    def attend(qs, kk, vv, sink, slope, hide):
        sc = _mm_nt(qs, kk) - slope * dist_f + (bias + hide)
        m = lax.stop_gradient(jnp.maximum(jnp.max(sc, axis=-1, keepdims=True), sink))
        p = jnp.exp(sc - m)
        inv = 1.0 / (jnp.sum(p, axis=-1, keepdims=True) + jnp.exp(sink - m))
        return _mm(p * inv, vv)

    items = [(b, j) for b in range(Q_BLOCKS) for j in range(B_KV_HEADS)]
    o = jax.vmap(attend)(*[_stack(t) for t in zip(*[item(b, j) for b, j in items])])
    rows_out = [jnp.concatenate([o[b * B_KV_HEADS + j, gi * BLOCK:(gi + 1) * BLOCK]
                                 for j in range(B_KV_HEADS) for gi in range(B_GROUP)], axis=1)
                for b in range(Q_BLOCKS)]
    return jnp.concatenate(rows_out, axis=0) * _silu(z)


def _swa_specs(idx):
    wide = lambda off: pl.BlockSpec((Q_ROWS, B_WIDTH), lambda n: (idx(n), off))
    cur = lambda off: pl.BlockSpec((Q_ROWS, LANE), lambda n: (idx(n), off))
    prev = lambda off: pl.BlockSpec((BLOCK, LANE), lambda n: (jnp.maximum(idx(n) * Q_BLOCKS - 1, 0), off))
    return [wide(L_QB // B_WIDTH), prev(L_KB // LANE), cur(L_KB // LANE), prev(L_VB // LANE), cur(L_VB // LANE),
            wide(L_ZB // B_WIDTH), pl.BlockSpec((1, LANE), lambda n: (0, 0))]


def _swa_fwd(h, sinks, *, name, rider=None):
    t_len = h.shape[0]
    nb = t_len // Q_ROWS

    def core(ins, outs, _):
        q_ref, kp_ref, kc_ref, vp_ref, vc_ref, z_ref, s_ref = ins
        outs[0][...] = _swa_block(q_ref[...], kp_ref[...], kc_ref[...], vp_ref[...], vc_ref[...], z_ref[...],
                                  s_ref[...], pl.program_id(0) == 0).astype(outs[0].dtype)

    res = _pcall(core, name=name, grid=(nb,), in_specs=_swa_specs(lambda n: n),
                 out_specs=[pl.BlockSpec((Q_ROWS, B_WIDTH), lambda n: (n, 1))],
                 out_shape=[jax.ShapeDtypeStruct((t_len, D_MODEL), MM_DTYPE)], sem=("parallel",), rider=rider,
                 args=(h, h, h, h, h, h, sinks))
    return res if rider else res[0]


def _swa_bwd(h, sinks, dycat, *, name, rider=None):
    t_len = h.shape[0]
    nb = t_len // Q_ROWS
    early = slice(0, Q_ROWS - BLOCK)
    last = slice(Q_ROWS - BLOCK, Q_ROWS)

    def core(ins, outs, scr):
        q_ref, kp_ref, kc_ref, vp_ref, vc_ref, z_ref, s_ref, dy_ref = ins
        dh_ref, dsk_ref = outs
        ck_scr, cv_scr = scr
        i = pl.program_id(0)
        n = nb - 1 - i

        @pl.when(i == 0)
        def _():
            ck_scr[...] = jnp.zeros_like(ck_scr)
            cv_scr[...] = jnp.zeros_like(cv_scr)
            dsk_ref[...] = jnp.zeros_like(dsk_ref)

        fn = functools.partial(_swa_block, first=(n == 0))
        _, vjp = jax.vjp(fn, q_ref[...], kp_ref[...], kc_ref[...], vp_ref[...], vc_ref[...], z_ref[...], s_ref[...])
        dq, dkp, dkc, dvp, dvc, dz, dsk = vjp(dy_ref[...])
        def put(rows, col, val):
            dh_ref[rows, col:col + val.shape[1]] = val.astype(dh_ref.dtype)

        put(slice(None), L_QB, dq)
        put(slice(None), L_ZB, dz)
        put(early, L_KB, dkc[early])
        put(early, L_VB, dvc[early])
        put(last, L_KB, dkc[last] + ck_scr[...])
        put(last, L_VB, dvc[last] + cv_scr[...])
        ck_scr[...] = dkp
        cv_scr[...] = dvp
        dsk_ref[...] += dsk

    rev = lambda i: nb - 1 - i
    return _pcall(
        core, name=name, grid=(nb,),
        in_specs=_swa_specs(rev) + [pl.BlockSpec((Q_ROWS, B_WIDTH), lambda i: (rev(i), 1))],
        out_specs=[pl.BlockSpec((Q_ROWS, L_SWA), lambda i: (rev(i), 0)), pl.BlockSpec((1, LANE), lambda i: (0, 0))],
        out_shape=[jax.ShapeDtypeStruct((t_len, L_COLS), MM_DTYPE), jax.ShapeDtypeStruct((1, LANE), F32)],
        scratch_shapes=[pltpu.VMEM((BLOCK, LANE), F32), pltpu.VMEM((BLOCK, LANE), F32)],
        sem=("arbitrary",), rider=rider, args=(h, h, h, h, h, h, sinks, dycat))


def _out_ln_fwd(ycat, w_out, x, ln_g, ln_b, *, name, tm=512, last=False):
    t_len = x.shape[0]

    def body(y_ref, w_ref, x_ref, g_ref, b_ref, r_ref, *o_ref):
        r = DEEPNORM_ALPHA * x_ref[...] + _mm(y_ref[...], w_ref[...])
        r_ref[...] = r
        if not last:
            mu = jnp.mean(r, axis=-1, keepdims=True)
            d = r - mu
            var = jnp.mean(d * d, axis=-1, keepdims=True)
            o_ref[0][...] = d * lax.rsqrt(var + LN_EPS) * g_ref[...] + b_ref[...]

    tile = pl.BlockSpec((tm, D_MODEL), lambda i: (i, 0))
    vec = pl.BlockSpec((1, D_MODEL), lambda i: (0, 0))
    n_out = 1 if last else 2
    res = pl.pallas_call(
        body, name=name, grid=(t_len // tm,),
        in_specs=[tile, pl.BlockSpec((D_MODEL, D_MODEL), lambda i: (0, 0)), tile, vec, vec],
        out_specs=[tile] * n_out,
        out_shape=[jax.ShapeDtypeStruct((t_len, D_MODEL), F32)] * n_out,
        compiler_params=_cparams(("parallel",)),
    )(ycat, w_out, x, ln_g, ln_b)
    return (res[0], None) if last else res


def _ln_out_bwd(dxn, r, ln_g, ycat, w_out, *, name, tm=512, loss=None):
    t_len = r.shape[0]

    def body(*refs):
        if loss:
            t_ref, r_ref, g_ref, b_ref, y_ref, w_ref, dr_ref, dg_ref, db_ref, l_ref, dy_ref, dw_ref = refs
        else:
            dx_ref, r_ref, g_ref, y_ref, w_ref, dr_ref, dg_ref, db_ref, dy_ref, dw_ref = refs

        @pl.when(pl.program_id(0) == 0)
        def _():
            dg_ref[...] = jnp.zeros_like(dg_ref)
            db_ref[...] = jnp.zeros_like(db_ref)
            dw_ref[...] = jnp.zeros_like(dw_ref)
            if loss:
                l_ref[...] = jnp.zeros_like(l_ref)

        rr = r_ref[...]
        mu = jnp.mean(rr, axis=-1, keepdims=True)
        d = rr - mu
        rstd = lax.rsqrt(jnp.mean(d * d, axis=-1, keepdims=True) + LN_EPS)
        xh = d * rstd
        if loss:
            e = (xh * g_ref[...] + b_ref[...]) - t_ref[...]
            dx = e * (1.0 / D_MODEL)
            l_ref[...] += jnp.sum(e * e, axis=0, keepdims=True)
        else:
            dx = dx_ref[...]
        dxh = dx * g_ref[...]
        dr = rstd * (dxh - jnp.mean(dxh, axis=-1, keepdims=True) - xh * jnp.mean(dxh * xh, axis=-1, keepdims=True))
        dr_ref[...] = dr
        dg_ref[...] += jnp.sum(dx * xh, axis=0, keepdims=True)
        db_ref[...] += jnp.sum(dx, axis=0, keepdims=True)
        dy_ref[...] = _mm_nt(dr, w_ref[...])
        dw_ref[...] += _mm_tn(y_ref[...], dr)

    tile = pl.BlockSpec((tm, D_MODEL), lambda i: (i, 0))
    vec = pl.BlockSpec((1, D_MODEL), lambda i: (0, 0))
    square = pl.BlockSpec((D_MODEL, D_MODEL), lambda i: (0, 0))
    tile_shape = jax.ShapeDtypeStruct((t_len, D_MODEL), F32)
    vec_shape = jax.ShapeDtypeStruct((1, D_MODEL), F32)
    args = (loss[0], r, ln_g, loss[1]) if loss else (dxn, r, ln_g)
    return pl.pallas_call(
        body, name=name, grid=(t_len // tm,),
        in_specs=[tile, tile, vec] + ([vec] if loss else []) + [tile, square],
        out_specs=[tile, vec, vec] + ([vec] if loss else []) + [tile, square],
        out_shape=[tile_shape, vec_shape, vec_shape] + ([vec_shape] if loss else [])
        + [tile_shape, jax.ShapeDtypeStruct((D_MODEL, D_MODEL), F32)],
        compiler_params=_cparams(("arbitrary",)),
    )(*args, ycat, w_out)


def _pad_row(v):
    return jnp.zeros((1, LANE), F32).at[0, :v.shape[0]].set(v)


_REGIONS = ((0, 1536, L_QKV), (1536, 2048, L_ZA), (2048, 2056, L_BA), (2056, 2568, L_QB), (2568, 2696, L_KB),
            (2696, 2824, L_VB), (2824, 3336, L_ZB))


def _shard_pieces(regions):
    for a, b, off in regions:
        for d in range(N_DEV):
            lo, hi = max(a, d * SHARD_COLS), min(b, (d + 1) * SHARD_COLS)
            if lo < hi:
                yield d, lo - d * SHARD_COLS, hi - d * SHARD_COLS, off + lo - a


def _as_list(r):
    return list(r) if isinstance(r, (list, tuple)) else [r]


def _gathered(shard):
    return jax.ShapeDtypeStruct((N_DEV,) + shard.shape, shard.dtype)


def _full_w_in(g_in, name):
    by_offset = sorted(_shard_pieces(_REGIONS), key=lambda p: p[3])
    tc = 256

    def body(g_ref, o_ref):
        pieces, row = [], 0
        for d, lo, hi, off in by_offset + [(None, 0, 0, L_COLS)]:
            if off > row:
                pieces.append(jnp.zeros((off - row, tc), g_ref.dtype))
            if d is not None:
                pieces.append(g_ref[d, lo:hi, :])
            row = off + hi - lo
        o_ref[...] = jnp.concatenate(pieces, axis=0)

    return pl.pallas_call(
        body, name=name, grid=(D_MODEL // tc,),
        in_specs=[pl.BlockSpec((N_DEV, SHARD_COLS, tc), lambda i: (0, 0, i))],
        out_specs=pl.BlockSpec((L_COLS, tc), lambda i: (0, i)),
        out_shape=jax.ShapeDtypeStruct((L_COLS, D_MODEL), g_in.dtype),
        compiler_params=_cparams(("parallel",)),
    )(g_in)


def _full_conv(g_conv):
    return jnp.pad(g_conv.transpose(1, 0, 2).reshape(CONV_K, 3 * A_WIDTH), ((0, 8 - CONV_K), (0, 0)))


def _forward(x, weights, shards, small):
    a_log, dt_bias, norm_w, sinks, ln_g, ln_b = small
    tm = min(512, x.shape[0])
    saved, weights = [], [list(w) for w in weights]
    whole = lambda arrs: _Direct([(a, False, j, ()) for j, a in enumerate(arrs)], [_gathered(a) for a in arrs])
    for l in range(DEPTH):
        rider = whole(shards[l][1:]) if weights[l][1] is None else None
        h, *got = _as_list(_matmul(x, weights[l][0], form="nt", tm=tm, tn=L_COLS, tk=D_MODEL, name=f"in_proj_{l}",
                                   rider=rider))
        if rider:
            weights[l][1:] = [got[0].reshape(D_MODEL, D_MODEL), _full_conv(got[1])]
        w_in_l, w_out_l, conv_l = weights[l]
        qkv = _prep_fwd(h, conv_l, name=f"prep_fwd_{l}")
        al, dt, nw, sk = _pad_row(a_log[l]), _pad_row(dt_bias[l]), norm_w[l][None, :], _pad_row(sinks[l])
        ahead = l + 1 < DEPTH and weights[l + 1][0] is None
        rider = whole(shards[l + 1][1:]) if ahead else None
        ycat, *got = _as_list(_swa_fwd(h, sk, name=f"swa_fwd_{l}", rider=rider))
        if ahead:
            weights[l + 1][1:] = [got[0].reshape(D_MODEL, D_MODEL), _full_conv(got[1])]
        rider = whole(shards[l + 1][:1]) if ahead else None
        ycat, s_in, t_in, *got = _gdn_fwd(qkv, h, al, dt, nw, ycat, name=f"gdn_fwd_{l}", rider=rider)
        if ahead:
            weights[l + 1][0] = _full_w_in(got[0], f"w_in_rows_{l + 1}")
        r, xn = _out_ln_fwd(ycat, w_out_l, x, ln_g[l][None, :], ln_b[l][None, :], name=f"out_ln_{l}",
                            last=(l == DEPTH - 1))
        saved.append((x, h, qkv, s_in, t_in, ycat, r, al, dt, nw, sk))
        x = xn
    return x, saved, weights


def _w_in_blocks(g, name):
    cols, tc = g.shape[1], 256
    pieces = list(_shard_pieces(_REGIONS))

    def body(g_ref, o_ref):
        blocks = [[] for _ in range(N_DEV)]
        for d, lo, hi, off in pieces:
            blocks[d].append(g_ref[off:off + hi - lo, :])
        for d in range(N_DEV):
            o_ref[d] = jnp.concatenate(blocks[d], axis=0).astype(BF16)

    return pl.pallas_call(
        body, name=name, grid=(cols // tc,),
        in_specs=[pl.BlockSpec((L_COLS, tc), lambda i: (0, i))],
        out_specs=pl.BlockSpec((N_DEV, SHARD_COLS, tc), lambda i: (0, 0, i)),
        out_shape=jax.ShapeDtypeStruct((N_DEV, SHARD_COLS, cols), BF16),
        compiler_params=_cparams(("parallel",)),
    )(g)


def _small_blocks(g):
    c_conv = g["conv_w"].reshape(CONV_K, N_DEV, CONV_SHARD_COLS).transpose(1, 0, 2)
    c_small = [jnp.broadcast_to(g[n][None], (N_DEV,) + g[n].shape) for n, _ in SMALL_SIZES]
    return _pack_small(c_conv, c_small)


def _contributions(g):
    c_out = g["w_out"].astype(BF16).reshape(N_DEV, OUT_SHARD_ROWS, D_MODEL)
    return _w_in_blocks(g["w_in_rows"], name="w_in_grad_blocks_above"), c_out, _small_blocks(g)


def _backward_layer(l, dx, saved_l, weights_l, ln_g_l, above=None, loss=None):
    x_in, h, qkv, s_in, t_in, ycat, r, al, dt, nw, sk = saved_l
    w_in_l, w_out_l, conv_l = weights_l
    tm = min(512, x_in.shape[0])
    dr, d_lng, d_lnb, *loss_lanes, dycat, d_wout = _ln_out_bwd(dx, r, ln_g_l[None, :], ycat, w_out_l,
                                                               name=f"ln_out_bwd_{l}", tm=tm, loss=loss)
    big = min(1024, x_in.shape[0])
    rider, p_in, p_out, p_small = None, None, None, None
    recv = lambda c: jax.ShapeDtypeStruct((DEPTH,) + c.shape, c.dtype)
    if above:
        c_out = d_wout.astype(BF16).reshape(N_DEV, OUT_SHARD_ROWS, D_MODEL)
        rider = _Direct([(above[1], True, 0, (l + 1,)), (above[2], True, 1, (l + 1,)), (c_out, True, 0, (l,))],
                        [recv(above[1]), recv(above[2])])
    dh, d_sk, *got = _swa_bwd(h, sk, dycat, name=f"swa_bwd_{l}", rider=rider)
    if above:
        p_out, p_small = got
        rider = _Direct([(above[0], True, 0, (l + 1,))], [recv(above[0])])
    dh, dqkv_n, d_al, d_dt, d_nw, *got = _gdn_bwd(qkv, h, al, dt, nw, s_in, t_in, dycat, dh,
                                                  name=f"gdn_bwd_{l}", rider=rider)
    dh, d_conv = _prep_bwd(h, conv_l, dqkv_n, dh, name=f"prep_bwd_{l}")
    grads = dict(w_out=d_wout, conv_w=d_conv[:CONV_K], a_log=d_al[0, :A_HEADS], dt_bias=d_dt[0, :A_HEADS],
                 norm_w=d_nw[0], sinks=d_sk[0, :B_Q_HEADS], ln_g=d_lng[0], ln_b=d_lnb[0])
    dw = functools.partial(_matmul, dh, x_in, form="tn")
    if not above:
        grads["w_in_rows"] = dw(name=f"in_proj_dw_{l}", tm=L_COLS // 3, tn=D_MODEL, tk=min(2048, x_in.shape[0]))
    else:
        p_in, = got
        cut = D_MODEL // 2
        rest = D_MODEL - cut
        first = dw(name=f"in_proj_dw_first_{l}", tm=L_COLS, tn=cut, tk=big, b_cols=(0, cut))
        blocks = _w_in_blocks(first, name=f"w_in_grad_blocks_first_{l}")
        rider = _Direct([(blocks, True, 0, (l,), (slice(None), pl.ds(0, cut)))], [p_in])
        second, p_in = dw(name=f"in_proj_dw_second_{l}", tm=L_COLS, tn=cut, tk=big, b_cols=(cut, rest), rider=rider)
        blocks = _w_in_blocks(second, name=f"w_in_grad_blocks_second_{l}")
        rider = _Direct([(blocks, True, 0, (l,), (slice(None), pl.ds(cut, rest))),
                         (_small_blocks(grads), True, 1, (l,))], [p_in, p_small])
    dx, *got = _as_list(_matmul(dh, w_in_l, form="nn", tm=tm, tn=D_MODEL, tk=L_COLS, name=f"in_proj_dx_{l}",
                                add=dr, add_scale=DEEPNORM_ALPHA, rider=rider))
    bufs = (got[0], p_out, got[1]) if above else None
    return dx, grads, bufs, (loss_lanes[0] if loss else None)


def _all_gather(shards, *, name):
    n_arr = len(shards)

    def body(*refs):
        x_refs, out_refs = refs[:n_arr], refs[n_arr:2 * n_arr]
        send_sems, recv_sems, local_sems = refs[2 * n_arr:]
        x, y, c = _me()
        me, sibling = (x, y, c), (x, y, 1 - c)
        chips = [(1 - x, y), (x, 1 - y), (1 - x, 1 - y)]

        def copy(a, k, block, to, src=None):
            dst = out_refs[a].at[_flat_id(block)]
            return _remote(dst if src is None else src, dst, send_sems.at[a, k], recv_sems.at[a, k], to)

        mine = [pltpu.make_async_copy(x_refs[a], out_refs[a].at[_flat_id(me)], local_sems.at[a])
                for a in range(n_arr)]
        for cp in mine:
            cp.start()
        first = []
        for a in range(n_arr):
            first.append(copy(a, 0, me, sibling, src=x_refs[a]))
            first += [copy(a, 1 + j, me, (*chip, c), src=x_refs[a]) for j, chip in enumerate(chips)]
        for cp in first:
            cp.start()
        passed = []
        for j, chip in enumerate(chips):
            for a in range(n_arr):
                copy(a, 1 + j, (*chip, c), me).wait_recv()
                fwd = copy(a, 4 + j, (*chip, c), sibling)
                fwd.start()
                passed.append(fwd)
        for a in range(n_arr):
            copy(a, 0, sibling, me).wait_recv()
            for j, chip in enumerate(chips):
                copy(a, 4 + j, (*chip, 1 - c), me).wait_recv()
        for cp in first + passed:
            cp.wait_send()
        for cp in mine:
            cp.wait()

    return pl.pallas_call(
        body, name=name, in_specs=[_ANY] * n_arr, out_specs=[_ANY] * n_arr,
        out_shape=[jax.ShapeDtypeStruct((N_DEV,) + s.shape, s.dtype) for s in shards],
        scratch_shapes=[pltpu.SemaphoreType.DMA((n_arr, N_DEV - 1)), pltpu.SemaphoreType.DMA((n_arr, N_DEV - 1)),
                        pltpu.SemaphoreType.DMA((n_arr,))],
    )(*shards)


def _adamw(parts, w, m, v, *, tr, name):
    depth, rows, cols = w.shape
    c1 = 1.0 - ADAM_B1 ** ADAM_STEP
    c2 = 1.0 - ADAM_B2 ** ADAM_STEP

    def body(g_ref, w_ref, m_ref, v_ref, go_ref, d_ref, mo_ref, vo_ref):
        g = g_ref[0, 0].astype(F32)
        for s in range(1, N_DEV):
            g = g + g_ref[0, s].astype(F32)
        m_new = ADAM_B1 * m_ref[0] + (1.0 - ADAM_B1) * g
        v_new = ADAM_B2 * v_ref[0] + (1.0 - ADAM_B2) * (g * g)
        go_ref[0] = g
        mo_ref[0] = m_new
        vo_ref[0] = v_new
        d_ref[0] = -ADAM_LR * ((m_new / c1) / (jnp.sqrt(v_new / c2) + ADAM_EPS) + ADAM_WD * w_ref[0])

    tile = pl.BlockSpec((1, tr, cols), lambda l, i: (l, i, 0))
    return pl.pallas_call(
        body, name=name, grid=(depth, rows // tr),
        in_specs=[pl.BlockSpec((1, N_DEV, tr, cols), lambda l, i: (l, 0, i, 0)), tile, tile, tile],
        out_specs=[tile] * 4, out_shape=[jax.ShapeDtypeStruct(w.shape, F32)] * 4,
        compiler_params=_cparams(("parallel", "parallel")),
    )(parts, w, m, v)


def _adamw_w_in(parts, w, m, v, *, name):
    c1 = 1.0 - ADAM_B1 ** ADAM_STEP
    c2 = 1.0 - ADAM_B2 ** ADAM_STEP

    def body(g_ref, w_ref, m_ref, v_ref, go_ref, d_ref, mo_ref, vo_ref):
        gs = []
        for l in range(DEPTH):
            g = g_ref[l, 0].astype(F32)
            for s in range(1, N_DEV):
                g = g + g_ref[l, s].astype(F32)
            gs.append(g)
        g = jnp.stack(gs, axis=1)
        m_new = ADAM_B1 * m_ref[...] + (1.0 - ADAM_B1) * g
        v_new = ADAM_B2 * v_ref[...] + (1.0 - ADAM_B2) * (g * g)
        go_ref[...] = g
        mo_ref[...] = m_new
        vo_ref[...] = v_new
        d_ref[...] = -ADAM_LR * ((m_new / c1) / (jnp.sqrt(v_new / c2) + ADAM_EPS) + ADAM_WD * w_ref[...])

    tile = pl.BlockSpec((SHARD_COLS, DEPTH, LANE), lambda i: (0, 0, i))
    return pl.pallas_call(
        body, name=name, grid=(D_MODEL // LANE,),
        in_specs=[pl.BlockSpec((DEPTH, N_DEV, SHARD_COLS, LANE), lambda i: (0, 0, 0, i)), tile, tile, tile],
        out_specs=[tile] * 4, out_shape=[jax.ShapeDtypeStruct(w.shape, F32)] * 4,
        compiler_params=_cparams(("parallel",)),
    )(parts, w, m, v)


def _pack_small(conv, small):
    lead = conv.shape[:-2]
    flat = jnp.concatenate([conv.reshape(lead + (CS_CONV,))] + list(small), axis=-1)
    pad = CS_ROWS * LANE - flat.shape[-1]
    flat = jnp.concatenate([flat, jnp.zeros(lead + (pad,), F32)], axis=-1)
    return flat.reshape(lead + (CS_ROWS, LANE))


def _unpack_small(p):
    flat = p.reshape(DEPTH, CS_ROWS * LANE)
    conv = flat[:, :CS_CONV].reshape(DEPTH, CONV_K, CONV_SHARD_COLS)
    small, off = [], CS_CONV
    for _, n in SMALL_SIZES:
        small.append(flat[:, off:off + n])
        off += n
    return conv, small


def kernel(x, w_in, conv_w, a_log, dt_bias, norm_w, sinks, w_out, ln_g, ln_b, loss_target, m_w_in, m_conv_w, m_a_log, m_dt_bias, m_norm_w, m_sinks, m_w_out, m_ln_g, m_ln_b, v_w_in, v_conv_w, v_a_log, v_dt_bias, v_norm_w, v_sinks, v_w_out, v_ln_g, v_ln_b):
    small = [a_log, dt_bias, norm_w, sinks, ln_g, ln_b]
    w_t, m_t, v_t = (a.transpose(2, 0, 1) for a in (w_in, m_w_in, v_w_in))
    shards = [[w_t[:, l].astype(BF16), w_out[l].astype(BF16), conv_w[l]] for l in range(DEPTH)]
    g_in0, = _all_gather(shards[0][:1], name="weights_all_gather_0")
    weights = [[_full_w_in(g_in0, "w_in_rows_0"), None, None]] + [[None, None, None]] * (DEPTH - 1)

    _, saved, weights = _forward(x[0], weights, shards, small)
    dx, g1, _, loss_lanes = _backward_layer(1, None, saved[1], weights[1], ln_g[1],
                                            loss=(loss_target[0], ln_b[1][None, :]))
    loss = lax.psum(0.5 * jnp.sum(loss_lanes) * (1.0 / D_MODEL), ("x", "y", "c"))
    dx, _, (p_in, p_out, p_small), _ = _backward_layer(0, dx, saved[0], weights[0], ln_g[0],
                                                       above=_contributions(g1))

    o_in = [o.transpose(1, 2, 0) for o in _adamw_w_in(p_in, w_t, m_t, v_t, name="adamw_w_in")]
    o_out = _adamw(p_out, w_out, m_w_out, v_w_out, tr=OUT_SHARD_ROWS, name="adamw_w_out")
    o_small = _adamw(p_small, _pack_small(conv_w, small),
                     _pack_small(m_conv_w, [m_a_log, m_dt_bias, m_norm_w, m_sinks, m_ln_g, m_ln_b]),
                     _pack_small(v_conv_w, [v_a_log, v_dt_bias, v_norm_w, v_sinks, v_ln_g, v_ln_b]),
                     tr=CS_ROWS, name="adamw_small")
    outs = []
    for k in range(4):
        cv, sm = _unpack_small(o_small[k])
        outs += [o_in[k], cv, sm[0], sm[1], sm[2], sm[3], o_out[k], sm[4], sm[5]]
    return (loss, dx[None], *outs)
```

```python
import functools

import jax
import jax.numpy as jnp
from jax import lax
from jax.experimental import pallas as pl
from jax.experimental.pallas import tpu as pltpu

F32 = jnp.float32
BF16 = jnp.bfloat16
MM_DTYPE = BF16

N_DEV = 8
D_MODEL = 1024
DEPTH = 2
A_HEADS = 4
A_HEAD_DIM = 128
A_WIDTH = 512
CONV_K = 4
SUPER = 256
NEWTON_STEPS = 1
B_Q_HEADS = 8
B_KV_HEADS = 2
B_HEAD_DIM = 64
B_GROUP = 4
B_WIDTH = 512
WINDOW = 128
BLOCK = 128
IN_COLS = 3336
SHARD_COLS = IN_COLS // N_DEV
OUT_SHARD_ROWS = D_MODEL // N_DEV
CONV_SHARD_COLS = 3 * A_WIDTH // N_DEV
DEEPNORM_ALPHA = (2 * DEPTH) ** 0.25
LN_EPS = 1e-5
RMS_EPS = 1e-6
L2_EPS = 1e-6
ADAM_LR, ADAM_B1, ADAM_B2, ADAM_EPS, ADAM_WD, ADAM_STEP = 0.001, 0.9, 0.999, 1e-08, 0.01, 10

LANE = 128
L_QB, L_ZB, L_KB, L_VB, L_ZA, L_BA, L_QKV = 0, 512, 1024, 1152, 1280, 1792, 1920
L_SWA = 1280
L_GATE = 640
L_COLS = 3456
SMALL_SIZES = (("a_log", 4), ("dt_bias", 4), ("norm_w", 128), ("sinks", 8), ("ln_g", 1024), ("ln_b", 1024))
CS_CONV = CONV_K * CONV_SHARD_COLS
CS_ROWS = 24
VMEM_LIMIT = 48 * 1024 * 1024


def _cparams(sem=None):
    return pltpu.CompilerParams(dimension_semantics=sem, vmem_limit_bytes=VMEM_LIMIT)


def _mm(a, b):
    return jnp.dot(a.astype(MM_DTYPE), b.astype(MM_DTYPE), preferred_element_type=F32)


def _mm_nt(a, b):
    return lax.dot_general(a.astype(MM_DTYPE), b.astype(MM_DTYPE), (((1,), (1,)), ((), ())),
                           preferred_element_type=F32)


def _mm_tn(a, b):
    return lax.dot_general(a.astype(MM_DTYPE), b.astype(MM_DTYPE), (((0,), (0,)), ((), ())),
                           preferred_element_type=F32)


def _split(a):
    hi = a.astype(BF16)
    return hi, (a - hi.astype(F32)).astype(BF16)


def _silu(x):
    return x * jax.nn.sigmoid(x)


@jax.custom_vjp
def _stack(parts):
    return jnp.stack(parts)


_stack.defvjp(lambda parts: (jnp.stack(parts), None), lambda _, g: (tuple(g[i] for i in range(g.shape[0])),))


def _softplus(x):
    return jnp.maximum(x, 0.0) + jnp.log1p(jnp.exp(-jnp.abs(x)))


_ANY = pl.BlockSpec(memory_space=pl.ANY)


def _me():
    return lax.axis_index("x"), lax.axis_index("y"), lax.axis_index("c")


def _flat_id(pos):
    return 4 * pos[0] + 2 * pos[1] + pos[2]


def _remote(src, dst, send_sem, recv_sem, to):
    return pltpu.make_async_remote_copy(src_ref=src, dst_ref=dst, send_sem=send_sem, recv_sem=recv_sem,
                                        device_id=to, device_id_type=pl.DeviceIdType.MESH)


class _Direct:
    def __init__(self, items, bufs):
        self.items, self.bufs = list(items), list(bufs)
        self.n_src, self.n_buf = len(self.items), len(self.bufs)
        self.old = [j for j, b in enumerate(self.bufs) if not isinstance(b, jax.ShapeDtypeStruct)]
        self.args = [it[0] for it in self.items] + [self.bufs[j] for j in self.old]
        self.out_shape = [jax.ShapeDtypeStruct(b.shape, b.dtype) for b in self.bufs]
        self.scratch = [pltpu.SemaphoreType.DMA((self.n_src, N_DEV - 1)),
                        pltpu.SemaphoreType.DMA((self.n_src, N_DEV - 1)), pltpu.SemaphoreType.DMA((self.n_src,))]

    def aliases(self, in_base, out_base):
        return {in_base + self.n_src + pos: out_base + j for pos, j in enumerate(self.old)}

    def copies(self, in_refs, out_refs, sems):
        send_sems, recv_sems, local_sems = sems
        x, y, c = _me()
        me = _flat_id((x, y, c))
        peers = [(x ^ ((rel >> 2) & 1), y ^ ((rel >> 1) & 1), c ^ (rel & 1)) for rel in range(1, N_DEV)]
        local, sends, recvs = [], [], []
        for a, (_, per_dest, j, prefix, *rest) in enumerate(self.items):
            src = lambda d: in_refs[a].at[d] if per_dest else in_refs[a]
            dst = lambda s: out_refs[j].at[tuple(prefix) + (s,) + tuple(rest[0] if rest else ())]
            local.append(pltpu.make_async_copy(src(me), dst(me), local_sems.at[a]))
            for k, peer in enumerate(peers):
                pid = _flat_id(peer)
                sends.append(_remote(src(pid), dst(me), send_sems.at[a, k], recv_sems.at[a, k], peer))
                recvs.append(_remote(src(pid), dst(pid), send_sems.at[a, k], recv_sems.at[a, k], peer))
        return local, sends, recvs

    def start(self, in_refs, out_refs, sems):
        local, sends, _ = self.copies(in_refs, out_refs, sems)
        for cp in local + sends:
            cp.start()

    def wait(self, in_refs, out_refs, sems):
        local, sends, recvs = self.copies(in_refs, out_refs, sems)
        for cp in recvs:
            cp.wait_recv()
        for cp in sends:
            cp.wait_send()
        for cp in local:
            cp.wait()


def _pcall(core, *, name, grid, in_specs, out_specs, out_shape, args, sem, scratch_shapes=(), aliases=None,
           rider=None):
    n_in, n_out, n_scr = len(in_specs), len(out_specs), len(scratch_shapes)
    n_rin, n_rout = (len(rider.args), rider.n_buf) if rider else (0, 0)

    def body(*refs):
        ins, outs = refs[:n_in], refs[n_in + n_rin:n_in + n_rin + n_out]
        scr = refs[n_in + n_rin + n_out + n_rout:n_in + n_rin + n_out + n_rout + n_scr]
        if rider:
            r_refs = (refs[n_in:n_in + rider.n_src], refs[n_in + n_rin + n_out:n_in + n_rin + n_out + n_rout],
                      refs[n_in + n_rin + n_out + n_rout + n_scr:])
            ids = [pl.program_id(d) for d in range(len(grid))]
            first = functools.reduce(lambda p, q: p & q, [i == 0 for i in ids])
            last = functools.reduce(lambda p, q: p & q, [i == g - 1 for i, g in zip(ids, grid)])
            pl.when(first)(lambda: rider.start(*r_refs))
        core(ins, outs, scr)
        if rider:
            pl.when(last)(lambda: rider.wait(*r_refs))

    aliases = dict(aliases or {})
    if rider:
        sem = ("arbitrary",) * len(grid)
        aliases.update(rider.aliases(n_in, n_out))
    return pl.pallas_call(
        body, name=name, grid=grid, in_specs=list(in_specs) + [_ANY] * n_rin,
        out_specs=list(out_specs) + [_ANY] * n_rout,
        out_shape=list(out_shape) + (rider.out_shape if rider else []),
        scratch_shapes=list(scratch_shapes) + (rider.scratch if rider else []),
        input_output_aliases=aliases, compiler_params=_cparams(sem),
    )(*args, *(rider.args if rider else []))


def _exchange(direct, *, name):
    n_in = len(direct.args)

    def body(*refs):
        r_refs = refs[:direct.n_src], refs[n_in:n_in + direct.n_buf], refs[n_in + direct.n_buf:]
        direct.start(*r_refs)
        direct.wait(*r_refs)

    return pl.pallas_call(
        body, name=name, in_specs=[_ANY] * n_in, out_specs=[_ANY] * direct.n_buf, out_shape=direct.out_shape,
        input_output_aliases=direct.aliases(0, 0), scratch_shapes=direct.scratch,
    )(*direct.args)


def _matmul(a, b, *, form, tm, tn, tk, name, add=None, add_scale=1.0, rider=None, b_cols=None):
    if form == "nn":
        (m, kk), n = a.shape, b.shape[1]
        a_spec = pl.BlockSpec((tm, tk), lambda i, j, k: (i, k))
        b_spec = pl.BlockSpec((tk, tn), lambda i, j, k: (k, j))
        dn = (((1,), (0,)), ((), ()))
    elif form == "nt":
        (m, kk), n = a.shape, b.shape[0]
        a_spec = pl.BlockSpec((tm, tk), lambda i, j, k: (i, k))
        b_spec = pl.BlockSpec((tn, tk), lambda i, j, k: (j, k))
        dn = (((1,), (1,)), ((), ()))
    else:
        kk, m = a.shape
        n0, n = b_cols or (0, b.shape[1])
        assert n0 % tn == 0
        a_spec = pl.BlockSpec((tk, tm), lambda i, j, k: (k, i))
        b_spec = pl.BlockSpec((tk, tn), lambda i, j, k: (k, j + n0 // tn))
        dn = (((0,), (0,)), ((), ()))
    assert m % tm == 0 and n % tn == 0 and kk % tk == 0, (name, m, n, kk)
    has_add = add is not None

    def core(ins, outs, _):
        a_ref, b_ref = ins[:2]
        o_ref = outs[0]
        k = pl.program_id(2)
        p = lax.dot_general(a_ref[...].astype(MM_DTYPE), b_ref[...].astype(MM_DTYPE), dn,
                            preferred_element_type=F32)

        @pl.when(k == 0)
        def _():
            o_ref[...] = p + add_scale * ins[2][...] if has_add else p

        @pl.when(k > 0)
        def _():
            o_ref[...] += p

    in_specs = [a_spec, b_spec]
    args = [a, b]
    if has_add:
        in_specs.append(pl.BlockSpec((tm, tn), lambda i, j, k: (i, j)))
        args.append(add)
    res = _pcall(core, name=name, grid=(m // tm, n // tn, kk // tk), in_specs=in_specs,
                 out_specs=[pl.BlockSpec((tm, tn), lambda i, j, k: (i, j))],
                 out_shape=[jax.ShapeDtypeStruct((m, n), F32)], args=args,
                 sem=("parallel", "parallel", "arbitrary"), rider=rider)
    return res if rider else res[0]


ZERO_TAIL = 8


def _with_tail(x):
    return jnp.concatenate([x, jnp.zeros((ZERO_TAIL,) + x.shape[1:], x.dtype)], axis=0)


def _shift_down(x, k):
    return pltpu.roll(x, k, 0)


def _shift_up(x, k):
    return pltpu.roll(x, x.shape[0] - k, 0)


def _conv_slab(x, w):
    return w[3:4] * x + w[2:3] * _shift_down(x, 1) + w[1:2] * _shift_down(x, 2) + w[0:1] * _shift_down(x, 3)


def _prep_fwd(h, conv_w, *, name):
    t_len = h.shape[0]

    def body(x_ref, w_ref, o_ref):
        s = pl.program_id(0)
        y = _silu(_conv_slab(_with_tail(x_ref[...]), w_ref[...])[:t_len])
        rs = lax.rsqrt(jnp.sum(y * y, axis=-1, keepdims=True) + L2_EPS)
        scale = jnp.where(s < A_HEADS, A_HEAD_DIM ** -0.5, 1.0)
        o_ref[...] = jnp.where(s < 2 * A_HEADS, y * rs * scale, y)

    return pl.pallas_call(
        body, name=name, grid=(12,),
        in_specs=[pl.BlockSpec((t_len, LANE), lambda s: (0, L_QKV // LANE + s)),
                  pl.BlockSpec((8, LANE), lambda s: (0, s))],
        out_specs=pl.BlockSpec((t_len, LANE), lambda s: (0, s)),
        out_shape=jax.ShapeDtypeStruct((t_len, 3 * A_WIDTH), F32),
        compiler_params=_cparams(("parallel",)),
    )(h, conv_w)


def _prep_bwd(h, conv_w, d_out, dh, *, name):
    t_len = h.shape[0]

    def body(x_ref, w_ref, g_ref, dh_in, dx_ref, dw_ref):
        del dh_in
        s = pl.program_id(0)
        x = _with_tail(x_ref[...])
        g = _with_tail(g_ref[0])
        w = w_ref[...]
        xs = [_shift_down(x, 3), _shift_down(x, 2), _shift_down(x, 1), x]
        c = w[0:1] * xs[0] + w[1:2] * xs[1] + w[2:3] * xs[2] + w[3:4] * xs[3]
        sg = jax.nn.sigmoid(c)
        y = c * sg
        rs = lax.rsqrt(jnp.sum(y * y, axis=-1, keepdims=True) + L2_EPS)
        scale = jnp.where(s < A_HEADS, A_HEAD_DIM ** -0.5, 1.0)
        dy_n = scale * (rs * g - y * (rs * rs * rs) * jnp.sum(g * y, axis=-1, keepdims=True))
        dy = jnp.where(s < 2 * A_HEADS, dy_n, g)
        dc = dy * (sg * (1.0 + c * (1.0 - sg)))
        dx = w[3:4] * dc + w[2:3] * _shift_up(dc, 1) + w[1:2] * _shift_up(dc, 2) + w[0:1] * _shift_up(dc, 3)
        dx_ref[...] = dx[:t_len].astype(dx_ref.dtype)
        dws = [jnp.sum(dc * xs[j], axis=0, keepdims=True) for j in range(CONV_K)]
        dw_ref[...] = jnp.concatenate(dws + [jnp.zeros((8 - CONV_K, LANE), F32)], axis=0)

    slab = pl.BlockSpec((t_len, LANE), lambda s: (0, L_QKV // LANE + s))
    return pl.pallas_call(
        body, name=name, grid=(12,),
        in_specs=[slab, pl.BlockSpec((8, LANE), lambda s: (0, s)),
                  pl.BlockSpec((1, t_len, LANE), lambda s: (s // A_HEADS, 0, s % A_HEADS)), _ANY],
        out_specs=[slab, pl.BlockSpec((8, LANE), lambda s: (0, s))],
        out_shape=[jax.ShapeDtypeStruct((t_len, L_COLS), MM_DTYPE), jax.ShapeDtypeStruct((8, 3 * A_WIDTH), F32)],
        input_output_aliases={3: 0},
        compiler_params=_cparams(("parallel",)),
    )(h, conv_w, d_out, dh)


N_LEVELS = 5
MF_TRIL, MF_STRIL, MF_DIAG8, MF_LOW16, MF_EYE = 0, 1, 2, 3, 3 + N_LEVELS
MB_CUM, MB_CUM_T, MB_TOT = 0, 1, 2


def _gdn_masks():
    r = lax.broadcasted_iota(jnp.int32, (SUPER, SUPER), 0)
    c = lax.broadcasted_iota(jnp.int32, (SUPER, SUPER), 1)
    same = lambda shift: (r >> shift) == (c >> shift)
    ninf = lambda m: jnp.where(m, 0.0, -jnp.inf).astype(F32)
    one = lambda m: m.astype(F32)
    mf = jnp.stack([ninf(r >= c), ninf(r > c), one(same(3))]
                   + [one(same(4 + lv) & jnp.logical_not(same(3 + lv))) for lv in range(N_LEVELS)] + [one(r == c)])
    mb = jnp.stack([one(r >= c), one(r <= c), jnp.ones((SUPER, SUPER), F32)]).astype(BF16)
    return mf, mb


def _tri_inv_impl(a, mf):
    d = lambda p, q: jnp.dot(p.astype(BF16), q.astype(BF16), preferred_element_type=F32)
    dd = lambda p, q: jnp.dot(p, q, preferred_element_type=F32)
    eye = mf[MF_EYE]
    a0 = a * mf[MF_DIAG8]
    a2 = d(a0, a0)
    a4 = d(a2, a2)
    t = d(d(eye - a0, eye + a2), eye + a4)
    for level in range(N_LEVELS):
        t = t - d(d(t, a * mf[MF_LOW16 + level]), t)
    a_hi, a_lo = _split(a)
    for _ in range(NEWTON_STEPS):
        t0 = t.astype(BF16)
        t0f = t0.astype(F32)
        resid = (eye - t0f) - (dd(a_hi, t0) + dd(a_lo, t0))
        r_hi, r_lo = _split(resid)
        t = t0f + (dd(t0, r_hi) + dd(t0, r_lo))
    return t


@jax.custom_vjp
def _wy_apply(a, rhs, t):
    return _mm(t, rhs)


def _wy_apply_fwd(a, rhs, t):
    x = _mm(t, rhs)
    return x, (t, x)


def _wy_apply_bwd(res, dx):
    t, x = res
    d_rhs = _mm_tn(t, dx)
    return -_mm_nt(d_rhs, x), d_rhs, jnp.zeros_like(t)


_wy_apply.defvjp(_wy_apply_fwd, _wy_apply_bwd)


@functools.partial(jax.custom_vjp, nondiff_argnums=(1,))
def _lane_roll(x, shift):
    return pltpu.roll(x, shift % LANE, 1)


_lane_roll.defvjp(lambda x, shift: (_lane_roll(x, shift), None), lambda shift, _, g: (_lane_roll(g, -shift),))


def _mask_times_lanes(x, mask):
    lane = lax.broadcasted_iota(jnp.int32, (1, LANE), 1)
    x = jnp.where(lane < A_HEADS, x, 0.0)
    x1 = x.astype(BF16).astype(F32)
    x2 = (x - x1).astype(BF16).astype(F32)
    x3 = (x - x1 - x2).astype(BF16).astype(F32)
    pieces = x1 + pltpu.roll(x2, A_HEADS, 1) + pltpu.roll(x3, 2 * A_HEADS, 1)
    res = jnp.dot(mask, pieces.astype(BF16), preferred_element_type=F32)
    return res + pltpu.roll(res, LANE - A_HEADS, 1) + pltpu.roll(res, LANE - 2 * A_HEADS, 1)


@jax.custom_vjp
def _chunk_sums(g, mb):
    return _mask_times_lanes(g, mb[MB_CUM]), _mask_times_lanes(g, mb[MB_TOT])


def _chunk_sums_fwd(g, mb):
    return _chunk_sums(g, mb), mb


def _chunk_sums_bwd(mb, d):
    lane = lax.broadcasted_iota(jnp.int32, (1, LANE), 1)
    dg = _mask_times_lanes(d[0], mb[MB_CUM_T]) + _mask_times_lanes(d[1], mb[MB_TOT])
    return jnp.where(lane < A_HEADS, dg, 0.0), jnp.zeros_like(mb)


_chunk_sums.defvjp(_chunk_sums_fwd, _chunk_sums_bwd)


def _gdn_gates(ba, alog, dtb, mb):
    beta = jax.nn.sigmoid(ba)
    g = -jnp.exp(alog) * _softplus(_lane_roll(ba, -A_HEADS) + dtb)
    gc, gl = _chunk_sums(g, mb)
    return beta, gc, gl, gc.T


def _gdn_block(s, q, k, v, z, gates, nw, h, t_known, mf):
    n = q.shape[0]
    beta_all, gc_all, gl_all, gct_all = gates
    lane = lax.broadcasted_iota(jnp.int32, (1, LANE), 1)
    sub = lax.broadcasted_iota(jnp.int32, (LANE, 1), 0)
    col = lambda x: jnp.sum(jnp.where(lane == h, x, 0.0), axis=1, keepdims=True)
    wide = lambda c: jnp.broadcast_to(c, (n, LANE))
    gc, gl = col(gc_all), col(gl_all)
    gc_row = jnp.sum(jnp.where(sub == h, gct_all, 0.0), axis=0, keepdims=True)
    beta_w, eg_w = wide(col(beta_all)), wide(jnp.exp(gc))
    diff = gc - gc_row
    decay = jnp.exp(diff + mf[MF_TRIL])
    kb = k * beta_w
    a_mat = _mm_nt(kb, k) * jnp.exp(diff + mf[MF_STRIL])
    rhs = jnp.concatenate([v * beta_w, kb * eg_w], axis=1)
    if t_known is None:
        t_mat = _tri_inv_impl(a_mat, mf)
        uw = _mm(t_mat, rhs)
    else:
        t_mat = t_known
        uw = _wy_apply(a_mat, rhs, t_known)
    u, w = uw[:, :LANE], uw[:, LANE:]
    qk = _mm_nt(q, k) * decay
    q_dec = q * eg_w
    k_dec = k * wide(jnp.exp(gl - gc))
    v_new = u - _mm(w, s)
    o = _mm(q_dec, s) + _mm(qk, v_new)
    s = s * jnp.exp(gl[0:1]) + _mm_tn(k_dec, v_new)
    o = o * lax.rsqrt(jnp.mean(o * o, axis=-1, keepdims=True) + RMS_EPS) * nw
    return o * _silu(z), s, t_mat


def _gdn_fwd(qkv, h, alog, dtb, nw, ycat, *, name, rider=None):
    t_len = qkv.shape[0]
    nsc = t_len // SUPER

    def core(ins, outs, scr):
        q_ref, k_ref, v_ref, gate_ref, al_ref, dt_ref, nw_ref, mf_ref, mb_ref, _ = ins
        y_ref, sin_ref, t_ref = outs
        s_scr, = scr

        @pl.when(pl.program_id(0) == 0)
        def _():
            s_scr[...] = jnp.zeros_like(s_scr)

        per_head = lambda ref: jnp.stack([ref[:, hh * LANE:(hh + 1) * LANE] for hh in range(A_HEADS)])
        states = s_scr[...]
        gates = _gdn_gates(gate_ref[:, A_WIDTH:], al_ref[...], dt_ref[...], mb_ref[...])
        fn = jax.vmap(_gdn_block, in_axes=(0, 0, 0, 0, 0, None, None, 0, None, None))
        y, s_new, t_mat = fn(states, per_head(q_ref), per_head(k_ref), per_head(v_ref), per_head(gate_ref),
                             gates, nw_ref[...], jnp.arange(A_HEADS), None, mf_ref[...])
        sin_ref[0] = states
        t_ref[0] = t_mat.astype(t_ref.dtype)
        s_scr[...] = s_new
        for hh in range(A_HEADS):
            y_ref[:, hh * LANE:(hh + 1) * LANE] = y[hh].astype(y_ref.dtype)

    blk = lambda j: pl.BlockSpec((SUPER, A_WIDTH), lambda sc: (sc, j))
    row = pl.BlockSpec((1, LANE), lambda sc: (0, 0))
    mf, mb = _gdn_masks()
    whole = lambda a: pl.BlockSpec(a.shape, lambda sc: (0, 0, 0))
    return _pcall(
        core, name=name, grid=(nsc,),
        in_specs=[blk(0), blk(1), blk(2), pl.BlockSpec((SUPER, L_GATE), lambda sc: (sc, L_ZA // L_GATE)),
                  row, row, row, whole(mf), whole(mb), _ANY],
        out_specs=[blk(0),
                   pl.BlockSpec((1, A_HEADS, A_HEAD_DIM, A_HEAD_DIM), lambda sc: (sc, 0, 0, 0)),
                   pl.BlockSpec((1, A_HEADS, SUPER, SUPER), lambda sc: (sc, 0, 0, 0))],
        out_shape=[jax.ShapeDtypeStruct((t_len, D_MODEL), MM_DTYPE),
                   jax.ShapeDtypeStruct((nsc, A_HEADS, A_HEAD_DIM, A_HEAD_DIM), F32),
                   jax.ShapeDtypeStruct((nsc, A_HEADS, SUPER, SUPER), MM_DTYPE)],
        scratch_shapes=[pltpu.VMEM((A_HEADS, A_HEAD_DIM, A_HEAD_DIM), F32)],
        aliases={9: 0}, sem=("arbitrary",), rider=rider,
        args=(qkv, qkv, qkv, h, alog, dtb, nw, mf, mb, ycat))


def _gdn_bwd(qkv, h, alog, dtb, nw, s_in, t_in, dycat, dh, *, name, rider=None):
    t_len = qkv.shape[0]
    nsc = t_len // SUPER

    def core(ins, outs, scr):
        q_ref, k_ref, v_ref, gate_ref, al_ref, dt_ref, nw_ref, sin_ref, t_ref, dy_ref, mf_ref, mb_ref, _ = ins
        dgate_ref, dqkv_ref, dal_ref, ddt_ref, dnw_ref = outs
        ds_scr, = scr

        @pl.when(pl.program_id(0) == 0)
        def _():
            ds_scr[...] = jnp.zeros_like(ds_scr)
            dal_ref[...] = jnp.zeros_like(dal_ref)
            ddt_ref[...] = jnp.zeros_like(ddt_ref)
            dnw_ref[...] = jnp.zeros_like(dnw_ref)

        per_head = lambda ref: jnp.stack([ref[:, hh * LANE:(hh + 1) * LANE] for hh in range(A_HEADS)])
        head_ids = jnp.arange(A_HEADS)
        t_known, mf, mb = t_ref[0], mf_ref[...], mb_ref[...]

        def fn(s, q, k, v, z, ba, alog, dtb, nw):
            gates = _gdn_gates(ba, alog, dtb, mb)
            one = lambda s, q, k, v, z, t, h: _gdn_block(s, q, k, v, z, gates, nw, h, t, mf)[:2]
            return jax.vmap(one)(s, q, k, v, z, t_known, head_ids)

        _, vjp = jax.vjp(fn, sin_ref[0], per_head(q_ref), per_head(k_ref), per_head(v_ref), per_head(gate_ref),
                         gate_ref[:, A_WIDTH:], al_ref[...], dt_ref[...], nw_ref[...])
        ds, dq, dk, dv, dz, dba, dal, ddt, dnw = vjp((per_head(dy_ref), ds_scr[...]))
        ds_scr[...] = ds
        for hh in range(A_HEADS):
            cols = slice(hh * LANE, (hh + 1) * LANE)
            dqkv_ref[0, :, cols] = dq[hh]
            dqkv_ref[1, :, cols] = dk[hh]
            dqkv_ref[2, :, cols] = dv[hh]
            dgate_ref[:, cols] = dz[hh].astype(dgate_ref.dtype)
        dgate_ref[:, A_WIDTH:] = dba.astype(dgate_ref.dtype)
        dal_ref[...] += dal
        ddt_ref[...] += ddt
        dnw_ref[...] += dnw

    rev = lambda i: nsc - 1 - i
    blk = lambda j: pl.BlockSpec((SUPER, A_WIDTH), lambda i: (rev(i), j))
    gate = pl.BlockSpec((SUPER, L_GATE), lambda i: (rev(i), L_ZA // L_GATE))
    row = pl.BlockSpec((1, LANE), lambda i: (0, 0))
    mf, mb = _gdn_masks()
    whole = lambda a: pl.BlockSpec(a.shape, lambda i: (0, 0, 0))
    return _pcall(
        core, name=name, grid=(nsc,),
        in_specs=[blk(0), blk(1), blk(2), gate, row, row, row,
                  pl.BlockSpec((1, A_HEADS, A_HEAD_DIM, A_HEAD_DIM), lambda i: (rev(i), 0, 0, 0)),
                  pl.BlockSpec((1, A_HEADS, SUPER, SUPER), lambda i: (rev(i), 0, 0, 0)),
                  blk(0), whole(mf), whole(mb), _ANY],
        out_specs=[gate, pl.BlockSpec((3, SUPER, A_WIDTH), lambda i: (0, rev(i), 0)), row, row, row],
        out_shape=[jax.ShapeDtypeStruct((t_len, L_COLS), MM_DTYPE), jax.ShapeDtypeStruct((3, t_len, A_WIDTH), F32)]
        + [jax.ShapeDtypeStruct((1, LANE), F32)] * 3,
        scratch_shapes=[pltpu.VMEM((A_HEADS, A_HEAD_DIM, A_HEAD_DIM), F32)],
        aliases={12: 0}, sem=("arbitrary",), rider=rider,
        args=(qkv, qkv, qkv, h, alog, dtb, nw, s_in, t_in, dycat, mf, mb, dh))


Q_BLOCKS = 4
Q_ROWS = Q_BLOCKS * BLOCK


def _swa_block(q, kp, kc, vp, vc, z, sinks, first):
    rows = B_GROUP * BLOCK
    ri = lax.broadcasted_iota(jnp.int32, (rows, 2 * BLOCK), 0)
    si = lax.broadcasted_iota(jnp.int32, (rows, 2 * BLOCK), 1)
    dist = (ri & (BLOCK - 1)) + BLOCK - si
    bias = jnp.where((dist >= 0) & (dist < WINDOW), 0.0, -jnp.inf)
    no_prev = jnp.where(first & (si[:1] < BLOCK), -jnp.inf, 0.0)
    dist_f = dist.astype(F32)
    head_of_row = lax.broadcasted_iota(jnp.int32, (rows, 1), 0) >> 7
    keys = jnp.concatenate([kp, kc], axis=0)
    vals = jnp.concatenate([vp, vc], axis=0)

    def item(b, j):
        cs = slice(j * B_HEAD_DIM, (j + 1) * B_HEAD_DIM)
        rs = slice(b * BLOCK, (b + 1) * BLOCK)
        heads = range(j * B_GROUP, (j + 1) * B_GROUP)
        qs = jnp.concatenate([q[rs, hq * B_HEAD_DIM:(hq + 1) * B_HEAD_DIM] for hq in heads], axis=0) * (
            B_HEAD_DIM ** -0.5)
        kk = keys[b * BLOCK:(b + 2) * BLOCK, cs]
        vv = vals[b * BLOCK:(b + 2) * BLOCK, cs]
        sink = jnp.concatenate([jnp.broadcast_to(sinks[:, hq:hq + 1], (BLOCK, 1)) for hq in heads], axis=0)
        slope = sum(jnp.where(head_of_row == gi, 2.0 ** (-8.0 * (hq + 1) / B_Q_HEADS), 0.0)
                    for gi, hq in enumerate(heads))
        return qs, kk, vv, sink, slope, (no_prev if b == 0 else jnp.zeros_like(no_prev))

    def attend(qs, kk, vv, sink, slope, hide):
        sc = _mm_nt(qs, kk) - slope * dist_f + (bias + hide)
        m = lax.stop_gradient(jnp.maximum(jnp.max(sc, axis=-1, keepdims=True), sink))
        p = jnp.exp(sc - m)
        inv = 1.0 / (jnp.sum(p, axis=-1, keepdims=True) + jnp.exp(sink - m))
        return _mm(p * inv, vv)

    items = [(b, j) for b in range(Q_BLOCKS) for j in range(B_KV_HEADS)]
    o = jax.vmap(attend)(*[_stack(t) for t in zip(*[item(b, j) for b, j in items])])
    rows_out = [jnp.concatenate([o[b * B_KV_HEADS + j, gi * BLOCK:(gi + 1) * BLOCK]
                                 for j in range(B_KV_HEADS) for gi in range(B_GROUP)], axis=1)
                for b in range(Q_BLOCKS)]
    return jnp.concatenate(rows_out, axis=0) * _silu(z)


def _swa_specs(idx):
    wide = lambda off: pl.BlockSpec((Q_ROWS, B_WIDTH), lambda n: (idx(n), off))
    cur = lambda off: pl.BlockSpec((Q_ROWS, LANE), lambda n: (idx(n), off))
    prev = lambda off: pl.BlockSpec((BLOCK, LANE), lambda n: (jnp.maximum(idx(n) * Q_BLOCKS - 1, 0), off))
    return [wide(L_QB // B_WIDTH), prev(L_KB // LANE), cur(L_KB // LANE), prev(L_VB // LANE), cur(L_VB // LANE),
            wide(L_ZB // B_WIDTH), pl.BlockSpec((1, LANE), lambda n: (0, 0))]


def _swa_fwd(h, sinks, *, name, rider=None):
    t_len = h.shape[0]
    nb = t_len // Q_ROWS

    def core(ins, outs, _):
        q_ref, kp_ref, kc_ref, vp_ref, vc_ref, z_ref, s_ref = ins
        outs[0][...] = _swa_block(q_ref[...], kp_ref[...], kc_ref[...], vp_ref[...], vc_ref[...], z_ref[...],
                                  s_ref[...], pl.program_id(0) == 0).astype(outs[0].dtype)

    res = _pcall(core, name=name, grid=(nb,), in_specs=_swa_specs(lambda n: n),
                 out_specs=[pl.BlockSpec((Q_ROWS, B_WIDTH), lambda n: (n, 1))],
                 out_shape=[jax.ShapeDtypeStruct((t_len, D_MODEL), MM_DTYPE)], sem=("parallel",), rider=rider,
                 args=(h, h, h, h, h, h, sinks))
    return res if rider else res[0]


def _swa_bwd(h, sinks, dycat, *, name, rider=None):
    t_len = h.shape[0]
    nb = t_len // Q_ROWS
    early = slice(0, Q_ROWS - BLOCK)
    last = slice(Q_ROWS - BLOCK, Q_ROWS)

    def core(ins, outs, scr):
        q_ref, kp_ref, kc_ref, vp_ref, vc_ref, z_ref, s_ref, dy_ref = ins
        dh_ref, dsk_ref = outs
        ck_scr, cv_scr = scr
        i = pl.program_id(0)
        n = nb - 1 - i

        @pl.when(i == 0)
        def _():
            ck_scr[...] = jnp.zeros_like(ck_scr)
            cv_scr[...] = jnp.zeros_like(cv_scr)
            dsk_ref[...] = jnp.zeros_like(dsk_ref)

        fn = functools.partial(_swa_block, first=(n == 0))
        _, vjp = jax.vjp(fn, q_ref[...], kp_ref[...], kc_ref[...], vp_ref[...], vc_ref[...], z_ref[...], s_ref[...])
        dq, dkp, dkc, dvp, dvc, dz, dsk = vjp(dy_ref[...])
        def put(rows, col, val):
            dh_ref[rows, col:col + val.shape[1]] = val.astype(dh_ref.dtype)

        put(slice(None), L_QB, dq)
        put(slice(None), L_ZB, dz)
        put(early, L_KB, dkc[early])
        put(early, L_VB, dvc[early])
        put(last, L_KB, dkc[last] + ck_scr[...])
        put(last, L_VB, dvc[last] + cv_scr[...])
        ck_scr[...] = dkp
        cv_scr[...] = dvp
        dsk_ref[...] += dsk

    rev = lambda i: nb - 1 - i
    return _pcall(
        core, name=name, grid=(nb,),
        in_specs=_swa_specs(rev) + [pl.BlockSpec((Q_ROWS, B_WIDTH), lambda i: (rev(i), 1))],
        out_specs=[pl.BlockSpec((Q_ROWS, L_SWA), lambda i: (rev(i), 0)), pl.BlockSpec((1, LANE), lambda i: (0, 0))],
        out_shape=[jax.ShapeDtypeStruct((t_len, L_COLS), MM_DTYPE), jax.ShapeDtypeStruct((1, LANE), F32)],
        scratch_shapes=[pltpu.VMEM((BLOCK, LANE), F32), pltpu.VMEM((BLOCK, LANE), F32)],
        sem=("arbitrary",), rider=rider, args=(h, h, h, h, h, h, sinks, dycat))


def _out_ln_fwd(ycat, w_out, x, ln_g, ln_b, *, name, tm=512, last=False):
    t_len = x.shape[0]

    def body(y_ref, w_ref, x_ref, g_ref, b_ref, r_ref, *o_ref):
        r = DEEPNORM_ALPHA * x_ref[...] + _mm(y_ref[...], w_ref[...])
        r_ref[...] = r
        if not last:
            mu = jnp.mean(r, axis=-1, keepdims=True)
            d = r - mu
            var = jnp.mean(d * d, axis=-1, keepdims=True)
            o_ref[0][...] = d * lax.rsqrt(var + LN_EPS) * g_ref[...] + b_ref[...]

    tile = pl.BlockSpec((tm, D_MODEL), lambda i: (i, 0))
    vec = pl.BlockSpec((1, D_MODEL), lambda i: (0, 0))
    n_out = 1 if last else 2
    res = pl.pallas_call(
        body, name=name, grid=(t_len // tm,),
        in_specs=[tile, pl.BlockSpec((D_MODEL, D_MODEL), lambda i: (0, 0)), tile, vec, vec],
        out_specs=[tile] * n_out,
        out_shape=[jax.ShapeDtypeStruct((t_len, D_MODEL), F32)] * n_out,
        compiler_params=_cparams(("parallel",)),
    )(ycat, w_out, x, ln_g, ln_b)
    return (res[0], None) if last else res


def _ln_out_bwd(dxn, r, ln_g, ycat, w_out, *, name, tm=512, loss=None):
    t_len = r.shape[0]

    def body(*refs):
        if loss:
            t_ref, r_ref, g_ref, b_ref, y_ref, w_ref, dr_ref, dg_ref, db_ref, l_ref, dy_ref, dw_ref = refs
        else:
            dx_ref, r_ref, g_ref, y_ref, w_ref, dr_ref, dg_ref, db_ref, dy_ref, dw_ref = refs

        @pl.when(pl.program_id(0) == 0)
        def _():
            dg_ref[...] = jnp.zeros_like(dg_ref)
            db_ref[...] = jnp.zeros_like(db_ref)
            dw_ref[...] = jnp.zeros_like(dw_ref)
            if loss:
                l_ref[...] = jnp.zeros_like(l_ref)

        rr = r_ref[...]
        mu = jnp.mean(rr, axis=-1, keepdims=True)
        d = rr - mu
        rstd = lax.rsqrt(jnp.mean(d * d, axis=-1, keepdims=True) + LN_EPS)
        xh = d * rstd
        if loss:
            e = (xh * g_ref[...] + b_ref[...]) - t_ref[...]
            dx = e * (1.0 / D_MODEL)
            l_ref[...] += jnp.sum(e * e, axis=0, keepdims=True)
        else:
            dx = dx_ref[...]
        dxh = dx * g_ref[...]
        dr = rstd * (dxh - jnp.mean(dxh, axis=-1, keepdims=True) - xh * jnp.mean(dxh * xh, axis=-1, keepdims=True))
        dr_ref[...] = dr
        dg_ref[...] += jnp.sum(dx * xh, axis=0, keepdims=True)
        db_ref[...] += jnp.sum(dx, axis=0, keepdims=True)
        dy_ref[...] = _mm_nt(dr, w_ref[...])
        dw_ref[...] += _mm_tn(y_ref[...], dr)

    tile = pl.BlockSpec((tm, D_MODEL), lambda i: (i, 0))
    vec = pl.BlockSpec((1, D_MODEL), lambda i: (0, 0))
    square = pl.BlockSpec((D_MODEL, D_MODEL), lambda i: (0, 0))
    tile_shape = jax.ShapeDtypeStruct((t_len, D_MODEL), F32)
    vec_shape = jax.ShapeDtypeStruct((1, D_MODEL), F32)
    args = (loss[0], r, ln_g, loss[1]) if loss else (dxn, r, ln_g)
    return pl.pallas_call(
        body, name=name, grid=(t_len // tm,),
        in_specs=[tile, tile, vec] + ([vec] if loss else []) + [tile, square],
        out_specs=[tile, vec, vec] + ([vec] if loss else []) + [tile, square],
        out_shape=[tile_shape, vec_shape, vec_shape] + ([vec_shape] if loss else [])
        + [tile_shape, jax.ShapeDtypeStruct((D_MODEL, D_MODEL), F32)],
        compiler_params=_cparams(("arbitrary",)),
    )(*args, ycat, w_out)


def _pad_row(v):
    return jnp.zeros((1, LANE), F32).at[0, :v.shape[0]].set(v)


_REGIONS = ((0, 1536, L_QKV), (1536, 2048, L_ZA), (2048, 2056, L_BA), (2056, 2568, L_QB), (2568, 2696, L_KB),
            (2696, 2824, L_VB), (2824, 3336, L_ZB))


def _shard_pieces(regions):
    for a, b, off in regions:
        for d in range(N_DEV):
            lo, hi = max(a, d * SHARD_COLS), min(b, (d + 1) * SHARD_COLS)
            if lo < hi:
                yield d, lo - d * SHARD_COLS, hi - d * SHARD_COLS, off + lo - a


def _as_list(r):
    return list(r) if isinstance(r, (list, tuple)) else [r]


def _gathered(shard):
    return jax.ShapeDtypeStruct((N_DEV,) + shard.shape, shard.dtype)


def _full_w_in(g_in, name):
    by_offset = sorted(_shard_pieces(_REGIONS), key=lambda p: p[3])
    tc = 256

    def body(g_ref, o_ref):
        pieces, row = [], 0
        for d, lo, hi, off in by_offset + [(None, 0, 0, L_COLS)]:
            if off > row:
                pieces.append(jnp.zeros((off - row, tc), g_ref.dtype))
            if d is not None:
                pieces.append(g_ref[d, lo:hi, :])
            row = off + hi - lo
        o_ref[...] = jnp.concatenate(pieces, axis=0)

    return pl.pallas_call(
        body, name=name, grid=(D_MODEL // tc,),
        in_specs=[pl.BlockSpec((N_DEV, SHARD_COLS, tc), lambda i: (0, 0, i))],
        out_specs=pl.BlockSpec((L_COLS, tc), lambda i: (0, i)),
        out_shape=jax.ShapeDtypeStruct((L_COLS, D_MODEL), g_in.dtype),
        compiler_params=_cparams(("parallel",)),
    )(g_in)


def _full_conv(g_conv):
    return jnp.pad(g_conv.transpose(1, 0, 2).reshape(CONV_K, 3 * A_WIDTH), ((0, 8 - CONV_K), (0, 0)))


def _forward(x, weights, shards, small):
    a_log, dt_bias, norm_w, sinks, ln_g, ln_b = small
    tm = min(512, x.shape[0])
    saved, weights = [], [list(w) for w in weights]
    whole = lambda arrs: _Direct([(a, False, j, ()) for j, a in enumerate(arrs)], [_gathered(a) for a in arrs])
    for l in range(DEPTH):
        rider = whole(shards[l][1:]) if weights[l][1] is None else None
        h, *got = _as_list(_matmul(x, weights[l][0], form="nt", tm=tm, tn=L_COLS, tk=D_MODEL, name=f"in_proj_{l}",
                                   rider=rider))
        if rider:
            weights[l][1:] = [got[0].reshape(D_MODEL, D_MODEL), _full_conv(got[1])]
        w_in_l, w_out_l, conv_l = weights[l]
        qkv = _prep_fwd(h, conv_l, name=f"prep_fwd_{l}")
        al, dt, nw, sk = _pad_row(a_log[l]), _pad_row(dt_bias[l]), norm_w[l][None, :], _pad_row(sinks[l])
        ahead = l + 1 < DEPTH and weights[l + 1][0] is None
        rider = whole(shards[l + 1][1:]) if ahead else None
        ycat, *got = _as_list(_swa_fwd(h, sk, name=f"swa_fwd_{l}", rider=rider))
        if ahead:
            weights[l + 1][1:] = [got[0].reshape(D_MODEL, D_MODEL), _full_conv(got[1])]
        rider = whole(shards[l + 1][:1]) if ahead else None
        ycat, s_in, t_in, *got = _gdn_fwd(qkv, h, al, dt, nw, ycat, name=f"gdn_fwd_{l}", rider=rider)
        if ahead:
            weights[l + 1][0] = _full_w_in(got[0], f"w_in_rows_{l + 1}")
        r, xn = _out_ln_fwd(ycat, w_out_l, x, ln_g[l][None, :], ln_b[l][None, :], name=f"out_ln_{l}",
                            last=(l == DEPTH - 1))
        saved.append((x, h, qkv, s_in, t_in, ycat, r, al, dt, nw, sk))
        x = xn
    return x, saved, weights


def _w_in_blocks(g, name):
    cols, tc = g.shape[1], 256
    pieces = list(_shard_pieces(_REGIONS))

    def body(g_ref, o_ref):
        blocks = [[] for _ in range(N_DEV)]
        for d, lo, hi, off in pieces:
            blocks[d].append(g_ref[off:off + hi - lo, :])
        for d in range(N_DEV):
            o_ref[d] = jnp.concatenate(blocks[d], axis=0).astype(BF16)

    return pl.pallas_call(
        body, name=name, grid=(cols // tc,),
        in_specs=[pl.BlockSpec((L_COLS, tc), lambda i: (0, i))],
        out_specs=pl.BlockSpec((N_DEV, SHARD_COLS, tc), lambda i: (0, 0, i)),
        out_shape=jax.ShapeDtypeStruct((N_DEV, SHARD_COLS, cols), BF16),
        compiler_params=_cparams(("parallel",)),
    )(g)


def _small_blocks(g):
    c_conv = g["conv_w"].reshape(CONV_K, N_DEV, CONV_SHARD_COLS).transpose(1, 0, 2)
    c_small = [jnp.broadcast_to(g[n][None], (N_DEV,) + g[n].shape) for n, _ in SMALL_SIZES]
    return _pack_small(c_conv, c_small)


def _contributions(g):
    c_out = g["w_out"].astype(BF16).reshape(N_DEV, OUT_SHARD_ROWS, D_MODEL)
    return _w_in_blocks(g["w_in_rows"], name="w_in_grad_blocks_above"), c_out, _small_blocks(g)


def _backward_layer(l, dx, saved_l, weights_l, ln_g_l, above=None, loss=None):
    x_in, h, qkv, s_in, t_in, ycat, r, al, dt, nw, sk = saved_l
    w_in_l, w_out_l, conv_l = weights_l
    tm = min(512, x_in.shape[0])
    dr, d_lng, d_lnb, *loss_lanes, dycat, d_wout = _ln_out_bwd(dx, r, ln_g_l[None, :], ycat, w_out_l,
                                                               name=f"ln_out_bwd_{l}", tm=tm, loss=loss)
    big = min(1024, x_in.shape[0])
    rider, p_in, p_out, p_small = None, None, None, None
    recv = lambda c: jax.ShapeDtypeStruct((DEPTH,) + c.shape, c.dtype)
    if above:
        c_out = d_wout.astype(BF16).reshape(N_DEV, OUT_SHARD_ROWS, D_MODEL)
        rider = _Direct([(above[1], True, 0, (l + 1,)), (above[2], True, 1, (l + 1,)), (c_out, True, 0, (l,))],
                        [recv(above[1]), recv(above[2])])
    dh, d_sk, *got = _swa_bwd(h, sk, dycat, name=f"swa_bwd_{l}", rider=rider)
    if above:
        p_out, p_small = got
        rider = _Direct([(above[0], True, 0, (l + 1,))], [recv(above[0])])
    dh, dqkv_n, d_al, d_dt, d_nw, *got = _gdn_bwd(qkv, h, al, dt, nw, s_in, t_in, dycat, dh,
                                                  name=f"gdn_bwd_{l}", rider=rider)
    dh, d_conv = _prep_bwd(h, conv_l, dqkv_n, dh, name=f"prep_bwd_{l}")
    grads = dict(w_out=d_wout, conv_w=d_conv[:CONV_K], a_log=d_al[0, :A_HEADS], dt_bias=d_dt[0, :A_HEADS],
                 norm_w=d_nw[0], sinks=d_sk[0, :B_Q_HEADS], ln_g=d_lng[0], ln_b=d_lnb[0])
    dw = functools.partial(_matmul, dh, x_in, form="tn")
    if not above:
        grads["w_in_rows"] = dw(name=f"in_proj_dw_{l}", tm=L_COLS // 3, tn=D_MODEL, tk=min(2048, x_in.shape[0]))
    else:
        p_in, = got
        cut = D_MODEL // 2
        rest = D_MODEL - cut
        first = dw(name=f"in_proj_dw_first_{l}", tm=L_COLS, tn=cut, tk=big, b_cols=(0, cut))
        blocks = _w_in_blocks(first, name=f"w_in_grad_blocks_first_{l}")
        rider = _Direct([(blocks, True, 0, (l,), (slice(None), pl.ds(0, cut)))], [p_in])
        second, p_in = dw(name=f"in_proj_dw_second_{l}", tm=L_COLS, tn=cut, tk=big, b_cols=(cut, rest), rider=rider)
        blocks = _w_in_blocks(second, name=f"w_in_grad_blocks_second_{l}")
        rider = _Direct([(blocks, True, 0, (l,), (slice(None), pl.ds(cut, rest))),
                         (_small_blocks(grads), True, 1, (l,))], [p_in, p_small])
    dx, *got = _as_list(_matmul(dh, w_in_l, form="nn", tm=tm, tn=D_MODEL, tk=L_COLS, name=f"in_proj_dx_{l}",
                                add=dr, add_scale=DEEPNORM_ALPHA, rider=rider))
    bufs = (got[0], p_out, got[1]) if above else None
    return dx, grads, bufs, (loss_lanes[0] if loss else None)


def _all_gather(shards, *, name):
    n_arr = len(shards)

    def body(*refs):
        x_refs, out_refs = refs[:n_arr], refs[n_arr:2 * n_arr]
        send_sems, recv_sems, local_sems = refs[2 * n_arr:]
        x, y, c = _me()
        me, sibling = (x, y, c), (x, y, 1 - c)
        chips = [(1 - x, y), (x, 1 - y), (1 - x, 1 - y)]

        def copy(a, k, block, to, src=None):
            dst = out_refs[a].at[_flat_id(block)]
            return _remote(dst if src is None else src, dst, send_sems.at[a, k], recv_sems.at[a, k], to)

        mine = [pltpu.make_async_copy(x_refs[a], out_refs[a].at[_flat_id(me)], local_sems.at[a])
                for a in range(n_arr)]
        for cp in mine:
            cp.start()
        first = []
        for a in range(n_arr):
            first.append(copy(a, 0, me, sibling, src=x_refs[a]))
            first += [copy(a, 1 + j, me, (*chip, c), src=x_refs[a]) for j, chip in enumerate(chips)]
        for cp in first:
            cp.start()
        passed = []
        for j, chip in enumerate(chips):
            for a in range(n_arr):
                copy(a, 1 + j, (*chip, c), me).wait_recv()
                fwd = copy(a, 4 + j, (*chip, c), sibling)
                fwd.start()
                passed.append(fwd)
        for a in range(n_arr):
            copy(a, 0, sibling, me).wait_recv()
            for j, chip in enumerate(chips):
                copy(a, 4 + j, (*chip, 1 - c), me).wait_recv()
        for cp in first + passed:
            cp.wait_send()
        for cp in mine:
            cp.wait()

    return pl.pallas_call(
        body, name=name, in_specs=[_ANY] * n_arr, out_specs=[_ANY] * n_arr,
        out_shape=[jax.ShapeDtypeStruct((N_DEV,) + s.shape, s.dtype) for s in shards],
        scratch_shapes=[pltpu.SemaphoreType.DMA((n_arr, N_DEV - 1)), pltpu.SemaphoreType.DMA((n_arr, N_DEV - 1)),
                        pltpu.SemaphoreType.DMA((n_arr,))],
    )(*shards)


def _adamw(parts, w, m, v, *, tr, name):
    depth, rows, cols = w.shape
    c1 = 1.0 - ADAM_B1 ** ADAM_STEP
    c2 = 1.0 - ADAM_B2 ** ADAM_STEP

    def body(g_ref, w_ref, m_ref, v_ref, go_ref, d_ref, mo_ref, vo_ref):
        g = g_ref[0, 0].astype(F32)
        for s in range(1, N_DEV):
            g = g + g_ref[0, s].astype(F32)
        m_new = ADAM_B1 * m_ref[0] + (1.0 - ADAM_B1) * g
        v_new = ADAM_B2 * v_ref[0] + (1.0 - ADAM_B2) * (g * g)
        go_ref[0] = g
        mo_ref[0] = m_new
        vo_ref[0] = v_new
        d_ref[0] = -ADAM_LR * ((m_new / c1) / (jnp.sqrt(v_new / c2) + ADAM_EPS) + ADAM_WD * w_ref[0])

    tile = pl.BlockSpec((1, tr, cols), lambda l, i: (l, i, 0))
    return pl.pallas_call(
        body, name=name, grid=(depth, rows // tr),
        in_specs=[pl.BlockSpec((1, N_DEV, tr, cols), lambda l, i: (l, 0, i, 0)), tile, tile, tile],
        out_specs=[tile] * 4, out_shape=[jax.ShapeDtypeStruct(w.shape, F32)] * 4,
        compiler_params=_cparams(("parallel", "parallel")),
    )(parts, w, m, v)


def _adamw_w_in(parts, w, m, v, *, name):
    c1 = 1.0 - ADAM_B1 ** ADAM_STEP
    c2 = 1.0 - ADAM_B2 ** ADAM_STEP

    def body(g_ref, w_ref, m_ref, v_ref, go_ref, d_ref, mo_ref, vo_ref):
        gs = []
        for l in range(DEPTH):
            g = g_ref[l, 0].astype(F32)
            for s in range(1, N_DEV):
                g = g + g_ref[l, s].astype(F32)
            gs.append(g)
        g = jnp.stack(gs, axis=1)
        m_new = ADAM_B1 * m_ref[...] + (1.0 - ADAM_B1) * g
        v_new = ADAM_B2 * v_ref[...] + (1.0 - ADAM_B2) * (g * g)
        go_ref[...] = g
        mo_ref[...] = m_new
        vo_ref[...] = v_new
        d_ref[...] = -ADAM_LR * ((m_new / c1) / (jnp.sqrt(v_new / c2) + ADAM_EPS) + ADAM_WD * w_ref[...])

    tile = pl.BlockSpec((SHARD_COLS, DEPTH, LANE), lambda i: (0, 0, i))
    return pl.pallas_call(
        body, name=name, grid=(D_MODEL // LANE,),
        in_specs=[pl.BlockSpec((DEPTH, N_DEV, SHARD_COLS, LANE), lambda i: (0, 0, 0, i)), tile, tile, tile],
        out_specs=[tile] * 4, out_shape=[jax.ShapeDtypeStruct(w.shape, F32)] * 4,
        compiler_params=_cparams(("parallel",)),
    )(parts, w, m, v)


def _pack_small(conv, small):
    lead = conv.shape[:-2]
    flat = jnp.concatenate([conv.reshape(lead + (CS_CONV,))] + list(small), axis=-1)
    pad = CS_ROWS * LANE - flat.shape[-1]
    flat = jnp.concatenate([flat, jnp.zeros(lead + (pad,), F32)], axis=-1)
    return flat.reshape(lead + (CS_ROWS, LANE))


def _unpack_small(p):
    flat = p.reshape(DEPTH, CS_ROWS * LANE)
    conv = flat[:, :CS_CONV].reshape(DEPTH, CONV_K, CONV_SHARD_COLS)
    small, off = [], CS_CONV
    for _, n in SMALL_SIZES:
        small.append(flat[:, off:off + n])
        off += n
    return conv, small


def kernel(x, w_in, conv_w, a_log, dt_bias, norm_w, sinks, w_out, ln_g, ln_b, loss_target, m_w_in, m_conv_w, m_a_log, m_dt_bias, m_norm_w, m_sinks, m_w_out, m_ln_g, m_ln_b, v_w_in, v_conv_w, v_a_log, v_dt_bias, v_norm_w, v_sinks, v_w_out, v_ln_g, v_ln_b):
    small = [a_log, dt_bias, norm_w, sinks, ln_g, ln_b]
    w_t, m_t, v_t = (a.transpose(2, 0, 1) for a in (w_in, m_w_in, v_w_in))
    shards = [[w_t[:, l].astype(BF16), w_out[l].astype(BF16), conv_w[l]] for l in range(DEPTH)]
    g_in0, = _all_gather(shards[0][:1], name="weights_all_gather_0")
    weights = [[_full_w_in(g_in0, "w_in_rows_0"), None, None]] + [[None, None, None]] * (DEPTH - 1)

    _, saved, weights = _forward(x[0], weights, shards, small)
    dx, g1, _, loss_lanes = _backward_layer(1, None, saved[1], weights[1], ln_g[1],
                                            loss=(loss_target[0], ln_b[1][None, :]))
    loss = lax.psum(0.5 * jnp.sum(loss_lanes) * (1.0 / D_MODEL), ("x", "y", "c"))
    dx, _, (p_in, p_out, p_small), _ = _backward_layer(0, dx, saved[0], weights[0], ln_g[0],
                                                       above=_contributions(g1))

    o_in = [o.transpose(1, 2, 0) for o in _adamw_w_in(p_in, w_t, m_t, v_t, name="adamw_w_in")]
    o_out = _adamw(p_out, w_out, m_w_out, v_w_out, tr=OUT_SHARD_ROWS, name="adamw_w_out")
    o_small = _adamw(p_small, _pack_small(conv_w, small),
                     _pack_small(m_conv_w, [m_a_log, m_dt_bias, m_norm_w, m_sinks, m_ln_g, m_ln_b]),
                     _pack_small(v_conv_w, [v_a_log, v_dt_bias, v_norm_w, v_sinks, v_ln_g, v_ln_b]),
                     tr=CS_ROWS, name="adamw_small")
    outs = []
    for k in range(4):
        cv, sm = _unpack_small(o_small[k])
        outs += [o_in[k], cv, sm[0], sm[1], sm[2], sm[3], o_out[k], sm[4], sm[5]]
    return (loss, dx[None], *outs)
```

```python
import functools

import jax
import jax.numpy as jnp
from jax import lax
from jax.experimental import pallas as pl
from jax.experimental.pallas import tpu as pltpu

F32 = jnp.float32
BF16 = jnp.bfloat16
MM_DTYPE = BF16

N_DEV = 8
D_MODEL = 1024
DEPTH = 2
A_HEADS = 4
A_HEAD_DIM = 128
A_WIDTH = 512
CONV_K = 4
SUPER = 256
NEWTON_STEPS = 1
B_Q_HEADS = 8
B_KV_HEADS = 2
B_HEAD_DIM = 64
B_GROUP = 4
B_WIDTH = 512
WINDOW = 128
BLOCK = 128
IN_COLS = 3336
SHARD_COLS = IN_COLS // N_DEV
OUT_SHARD_ROWS = D_MODEL // N_DEV
CONV_SHARD_COLS = 3 * A_WIDTH // N_DEV
DEEPNORM_ALPHA = (2 * DEPTH) ** 0.25
LN_EPS = 1e-5
RMS_EPS = 1e-6
L2_EPS = 1e-6
ADAM_LR, ADAM_B1, ADAM_B2, ADAM_EPS, ADAM_WD, ADAM_STEP = 0.001, 0.9, 0.999, 1e-08, 0.01, 10

LANE = 128
L_QB, L_ZB, L_KB, L_VB, L_ZA, L_BA, L_QKV = 0, 512, 1024, 1152, 1280, 1792, 1920
L_SWA = 1280
L_GATE = 640
L_COLS = 3456
SMALL_SIZES = (("a_log", 4), ("dt_bias", 4), ("norm_w", 128), ("sinks", 8), ("ln_g", 1024), ("ln_b", 1024))
CS_CONV = CONV_K * CONV_SHARD_COLS
CS_ROWS = 24
VMEM_LIMIT = 48 * 1024 * 1024


def _cparams(sem=None):
    return pltpu.CompilerParams(dimension_semantics=sem, vmem_limit_bytes=VMEM_LIMIT)


def _mm(a, b):
    return jnp.dot(a.astype(MM_DTYPE), b.astype(MM_DTYPE), preferred_element_type=F32)


def _mm_nt(a, b):
    return lax.dot_general(a.astype(MM_DTYPE), b.astype(MM_DTYPE), (((1,), (1,)), ((), ())),
                           preferred_element_type=F32)


def _mm_tn(a, b):
    return lax.dot_general(a.astype(MM_DTYPE), b.astype(MM_DTYPE), (((0,), (0,)), ((), ())),
                           preferred_element_type=F32)


def _split(a):
    hi = a.astype(BF16)
    return hi, (a - hi.astype(F32)).astype(BF16)


def _silu(x):
    return x * jax.nn.sigmoid(x)


@jax.custom_vjp
def _stack(parts):
    return jnp.stack(parts)


_stack.defvjp(lambda parts: (jnp.stack(parts), None), lambda _, g: (tuple(g[i] for i in range(g.shape[0])),))


def _softplus(x):
    return jnp.maximum(x, 0.0) + jnp.log1p(jnp.exp(-jnp.abs(x)))


_ANY = pl.BlockSpec(memory_space=pl.ANY)


def _me():
    return lax.axis_index("x"), lax.axis_index("y"), lax.axis_index("c")


def _flat_id(pos):
    return 4 * pos[0] + 2 * pos[1] + pos[2]


def _remote(src, dst, send_sem, recv_sem, to):
    return pltpu.make_async_remote_copy(src_ref=src, dst_ref=dst, send_sem=send_sem, recv_sem=recv_sem,
                                        device_id=to, device_id_type=pl.DeviceIdType.MESH)


class _Direct:
    def __init__(self, items, bufs):
        self.items, self.bufs = list(items), list(bufs)
        self.n_src, self.n_buf = len(self.items), len(self.bufs)
        self.old = [j for j, b in enumerate(self.bufs) if not isinstance(b, jax.ShapeDtypeStruct)]
        self.args = [it[0] for it in self.items] + [self.bufs[j] for j in self.old]
        self.out_shape = [jax.ShapeDtypeStruct(b.shape, b.dtype) for b in self.bufs]
        self.scratch = [pltpu.SemaphoreType.DMA((self.n_src, N_DEV - 1)),
                        pltpu.SemaphoreType.DMA((self.n_src, N_DEV - 1)), pltpu.SemaphoreType.DMA((self.n_src,))]

    def aliases(self, in_base, out_base):
        return {in_base + self.n_src + pos: out_base + j for pos, j in enumerate(self.old)}

    def copies(self, in_refs, out_refs, sems):
        send_sems, recv_sems, local_sems = sems
        x, y, c = _me()
        me = _flat_id((x, y, c))
        peers = [(x ^ ((rel >> 2) & 1), y ^ ((rel >> 1) & 1), c ^ (rel & 1)) for rel in range(1, N_DEV)]
        local, sends, recvs = [], [], []
        for a, (_, per_dest, j, prefix, *rest) in enumerate(self.items):
            src = lambda d: in_refs[a].at[d] if per_dest else in_refs[a]
            dst = lambda s: out_refs[j].at[tuple(prefix) + (s,) + tuple(rest[0] if rest else ())]
            local.append(pltpu.make_async_copy(src(me), dst(me), local_sems.at[a]))
            for k, peer in enumerate(peers):
                pid = _flat_id(peer)
                sends.append(_remote(src(pid), dst(me), send_sems.at[a, k], recv_sems.at[a, k], peer))
                recvs.append(_remote(src(pid), dst(pid), send_sems.at[a, k], recv_sems.at[a, k], peer))
        return local, sends, recvs

    def start(self, in_refs, out_refs, sems):
        local, sends, _ = self.copies(in_refs, out_refs, sems)
        for cp in local + sends:
            cp.start()

    def wait(self, in_refs, out_refs, sems):
        local, sends, recvs = self.copies(in_refs, out_refs, sems)
        for cp in recvs:
            cp.wait_recv()
        for cp in sends:
            cp.wait_send()
        for cp in local:
            cp.wait()


def _pcall(core, *, name, grid, in_specs, out_specs, out_shape, args, sem, scratch_shapes=(), aliases=None,
           rider=None):
    n_in, n_out, n_scr = len(in_specs), len(out_specs), len(scratch_shapes)
    n_rin, n_rout = (len(rider.args), rider.n_buf) if rider else (0, 0)

    def body(*refs):
        ins, outs = refs[:n_in], refs[n_in + n_rin:n_in + n_rin + n_out]
        scr = refs[n_in + n_rin + n_out + n_rout:n_in + n_rin + n_out + n_rout + n_scr]
        if rider:
            r_refs = (refs[n_in:n_in + rider.n_src], refs[n_in + n_rin + n_out:n_in + n_rin + n_out + n_rout],
                      refs[n_in + n_rin + n_out + n_rout + n_scr:])
            ids = [pl.program_id(d) for d in range(len(grid))]
            first = functools.reduce(lambda p, q: p & q, [i == 0 for i in ids])
            last = functools.reduce(lambda p, q: p & q, [i == g - 1 for i, g in zip(ids, grid)])
            pl.when(first)(lambda: rider.start(*r_refs))
        core(ins, outs, scr)
        if rider:
            pl.when(last)(lambda: rider.wait(*r_refs))

    aliases = dict(aliases or {})
    if rider:
        sem = ("arbitrary",) * len(grid)
        aliases.update(rider.aliases(n_in, n_out))
    return pl.pallas_call(
        body, name=name, grid=grid, in_specs=list(in_specs) + [_ANY] * n_rin,
        out_specs=list(out_specs) + [_ANY] * n_rout,
        out_shape=list(out_shape) + (rider.out_shape if rider else []),
        scratch_shapes=list(scratch_shapes) + (rider.scratch if rider else []),
        input_output_aliases=aliases, compiler_params=_cparams(sem),
    )(*args, *(rider.args if rider else []))


def _exchange(direct, *, name):
    n_in = len(direct.args)

    def body(*refs):
        r_refs = refs[:direct.n_src], refs[n_in:n_in + direct.n_buf], refs[n_in + direct.n_buf:]
        direct.start(*r_refs)
        direct.wait(*r_refs)

    return pl.pallas_call(
        body, name=name, in_specs=[_ANY] * n_in, out_specs=[_ANY] * direct.n_buf, out_shape=direct.out_shape,
        input_output_aliases=direct.aliases(0, 0), scratch_shapes=direct.scratch,
    )(*direct.args)


def _matmul(a, b, *, form, tm, tn, tk, name, add=None, add_scale=1.0, rider=None, b_cols=None):
    if form == "nn":
        (m, kk), n = a.shape, b.shape[1]
        a_spec = pl.BlockSpec((tm, tk), lambda i, j, k: (i, k))
        b_spec = pl.BlockSpec((tk, tn), lambda i, j, k: (k, j))
        dn = (((1,), (0,)), ((), ()))
    elif form == "nt":
        (m, kk), n = a.shape, b.shape[0]
        a_spec = pl.BlockSpec((tm, tk), lambda i, j, k: (i, k))
        b_spec = pl.BlockSpec((tn, tk), lambda i, j, k: (j, k))
        dn = (((1,), (1,)), ((), ()))
    else:
        kk, m = a.shape
        n0, n = b_cols or (0, b.shape[1])
        assert n0 % tn == 0
        a_spec = pl.BlockSpec((tk, tm), lambda i, j, k: (k, i))
        b_spec = pl.BlockSpec((tk, tn), lambda i, j, k: (k, j + n0 // tn))
        dn = (((0,), (0,)), ((), ()))
    assert m % tm == 0 and n % tn == 0 and kk % tk == 0, (name, m, n, kk)
    has_add = add is not None

    def core(ins, outs, _):
        a_ref, b_ref = ins[:2]
        o_ref = outs[0]
        k = pl.program_id(2)
        p = lax.dot_general(a_ref[...].astype(MM_DTYPE), b_ref[...].astype(MM_DTYPE), dn,
                            preferred_element_type=F32)

        @pl.when(k == 0)
        def _():
            o_ref[...] = p + add_scale * ins[2][...] if has_add else p

        @pl.when(k > 0)
        def _():
            o_ref[...] += p

    in_specs = [a_spec, b_spec]
    args = [a, b]
    if has_add:
        in_specs.append(pl.BlockSpec((tm, tn), lambda i, j, k: (i, j)))
        args.append(add)
    res = _pcall(core, name=name, grid=(m // tm, n // tn, kk // tk), in_specs=in_specs,
                 out_specs=[pl.BlockSpec((tm, tn), lambda i, j, k: (i, j))],
                 out_shape=[jax.ShapeDtypeStruct((m, n), F32)], args=args,
                 sem=("parallel", "parallel", "arbitrary"), rider=rider)
    return res if rider else res[0]


ZERO_TAIL = 8


def _with_tail(x):
    return jnp.concatenate([x, jnp.zeros((ZERO_TAIL,) + x.shape[1:], x.dtype)], axis=0)


def _shift_down(x, k):
    return pltpu.roll(x, k, 0)


def _shift_up(x, k):
    return pltpu.roll(x, x.shape[0] - k, 0)


def _conv_slab(x, w):
    return w[3:4] * x + w[2:3] * _shift_down(x, 1) + w[1:2] * _shift_down(x, 2) + w[0:1] * _shift_down(x, 3)


def _prep_fwd(h, conv_w, *, name):
    t_len = h.shape[0]

    def body(x_ref, w_ref, o_ref):
        s = pl.program_id(0)
        y = _silu(_conv_slab(_with_tail(x_ref[...]), w_ref[...])[:t_len])
        rs = lax.rsqrt(jnp.sum(y * y, axis=-1, keepdims=True) + L2_EPS)
        scale = jnp.where(s < A_HEADS, A_HEAD_DIM ** -0.5, 1.0)
        o_ref[...] = jnp.where(s < 2 * A_HEADS, y * rs * scale, y)

    return pl.pallas_call(
        body, name=name, grid=(12,),
        in_specs=[pl.BlockSpec((t_len, LANE), lambda s: (0, L_QKV // LANE + s)),
                  pl.BlockSpec((8, LANE), lambda s: (0, s))],
        out_specs=pl.BlockSpec((t_len, LANE), lambda s: (0, s)),
        out_shape=jax.ShapeDtypeStruct((t_len, 3 * A_WIDTH), F32),
        compiler_params=_cparams(("parallel",)),
    )(h, conv_w)


def _prep_bwd(h, conv_w, d_out, dh, *, name):
    t_len = h.shape[0]

    def body(x_ref, w_ref, g_ref, dh_in, dx_ref, dw_ref):
        del dh_in
        s = pl.program_id(0)
        x = _with_tail(x_ref[...])
        g = _with_tail(g_ref[0])
        w = w_ref[...]
        xs = [_shift_down(x, 3), _shift_down(x, 2), _shift_down(x, 1), x]
        c = w[0:1] * xs[0] + w[1:2] * xs[1] + w[2:3] * xs[2] + w[3:4] * xs[3]
        sg = jax.nn.sigmoid(c)
        y = c * sg
        rs = lax.rsqrt(jnp.sum(y * y, axis=-1, keepdims=True) + L2_EPS)
        scale = jnp.where(s < A_HEADS, A_HEAD_DIM ** -0.5, 1.0)
        dy_n = scale * (rs * g - y * (rs * rs * rs) * jnp.sum(g * y, axis=-1, keepdims=True))
        dy = jnp.where(s < 2 * A_HEADS, dy_n, g)
        dc = dy * (sg * (1.0 + c * (1.0 - sg)))
        dx = w[3:4] * dc + w[2:3] * _shift_up(dc, 1) + w[1:2] * _shift_up(dc, 2) + w[0:1] * _shift_up(dc, 3)
        dx_ref[...] = dx[:t_len].astype(dx_ref.dtype)
        dws = [jnp.sum(dc * xs[j], axis=0, keepdims=True) for j in range(CONV_K)]
        dw_ref[...] = jnp.concatenate(dws + [jnp.zeros((8 - CONV_K, LANE), F32)], axis=0)

    slab = pl.BlockSpec((t_len, LANE), lambda s: (0, L_QKV // LANE + s))
    return pl.pallas_call(
        body, name=name, grid=(12,),
        in_specs=[slab, pl.BlockSpec((8, LANE), lambda s: (0, s)),
                  pl.BlockSpec((1, t_len, LANE), lambda s: (s // A_HEADS, 0, s % A_HEADS)), _ANY],
        out_specs=[slab, pl.BlockSpec((8, LANE), lambda s: (0, s))],
        out_shape=[jax.ShapeDtypeStruct((t_len, L_COLS), MM_DTYPE), jax.ShapeDtypeStruct((8, 3 * A_WIDTH), F32)],
        input_output_aliases={3: 0},
        compiler_params=_cparams(("parallel",)),
    )(h, conv_w, d_out, dh)


N_LEVELS = 5
MF_TRIL, MF_STRIL, MF_DIAG8, MF_LOW16, MF_EYE = 0, 1, 2, 3, 3 + N_LEVELS
MB_CUM, MB_CUM_T, MB_TOT = 0, 1, 2


def _gdn_masks():
    r = lax.broadcasted_iota(jnp.int32, (SUPER, SUPER), 0)
    c = lax.broadcasted_iota(jnp.int32, (SUPER, SUPER), 1)
    same = lambda shift: (r >> shift) == (c >> shift)
    ninf = lambda m: jnp.where(m, 0.0, -jnp.inf).astype(F32)
    one = lambda m: m.astype(F32)
    mf = jnp.stack([ninf(r >= c), ninf(r > c), one(same(3))]
                   + [one(same(4 + lv) & jnp.logical_not(same(3 + lv))) for lv in range(N_LEVELS)] + [one(r == c)])
    mb = jnp.stack([one(r >= c), one(r <= c), jnp.ones((SUPER, SUPER), F32)]).astype(BF16)
    return mf, mb


def _tri_inv_impl(a, mf):
    d = lambda p, q: jnp.dot(p.astype(BF16), q.astype(BF16), preferred_element_type=F32)
    dd = lambda p, q: jnp.dot(p, q, preferred_element_type=F32)
    eye = mf[MF_EYE]
    a0 = a * mf[MF_DIAG8]
    a2 = d(a0, a0)
    a4 = d(a2, a2)
    t = d(d(eye - a0, eye + a2), eye + a4)
    for level in range(N_LEVELS):
        t = t - d(d(t, a * mf[MF_LOW16 + level]), t)
    a_hi, a_lo = _split(a)
    for _ in range(NEWTON_STEPS):
        t0 = t.astype(BF16)
        t0f = t0.astype(F32)
        resid = (eye - t0f) - (dd(a_hi, t0) + dd(a_lo, t0))
        r_hi, r_lo = _split(resid)
        t = t0f + (dd(t0, r_hi) + dd(t0, r_lo))
    return t


@jax.custom_vjp
def _wy_apply(a, rhs, t):
    return _mm(t, rhs)


def _wy_apply_fwd(a, rhs, t):
    x = _mm(t, rhs)
    return x, (t, x)


def _wy_apply_bwd(res, dx):
    t, x = res
    d_rhs = _mm_tn(t, dx)
    return -_mm_nt(d_rhs, x), d_rhs, jnp.zeros_like(t)


_wy_apply.defvjp(_wy_apply_fwd, _wy_apply_bwd)


@functools.partial(jax.custom_vjp, nondiff_argnums=(1,))
def _lane_roll(x, shift):
    return pltpu.roll(x, shift % LANE, 1)


_lane_roll.defvjp(lambda x, shift: (_lane_roll(x, shift), None), lambda shift, _, g: (_lane_roll(g, -shift),))


def _mask_times_lanes(x, mask):
    lane = lax.broadcasted_iota(jnp.int32, (1, LANE), 1)
    x = jnp.where(lane < A_HEADS, x, 0.0)
    x1 = x.astype(BF16).astype(F32)
    x2 = (x - x1).astype(BF16).astype(F32)
    x3 = (x - x1 - x2).astype(BF16).astype(F32)
    pieces = x1 + pltpu.roll(x2, A_HEADS, 1) + pltpu.roll(x3, 2 * A_HEADS, 1)
    res = jnp.dot(mask, pieces.astype(BF16), preferred_element_type=F32)
    return res + pltpu.roll(res, LANE - A_HEADS, 1) + pltpu.roll(res, LANE - 2 * A_HEADS, 1)


@jax.custom_vjp
def _chunk_sums(g, mb):
    return _mask_times_lanes(g, mb[MB_CUM]), _mask_times_lanes(g, mb[MB_TOT])


def _chunk_sums_fwd(g, mb):
    return _chunk_sums(g, mb), mb


def _chunk_sums_bwd(mb, d):
    lane = lax.broadcasted_iota(jnp.int32, (1, LANE), 1)
    dg = _mask_times_lanes(d[0], mb[MB_CUM_T]) + _mask_times_lanes(d[1], mb[MB_TOT])
    return jnp.where(lane < A_HEADS, dg, 0.0), jnp.zeros_like(mb)


_chunk_sums.defvjp(_chunk_sums_fwd, _chunk_sums_bwd)


def _gdn_gates(ba, alog, dtb, mb):
    beta = jax.nn.sigmoid(ba)
    g = -jnp.exp(alog) * _softplus(_lane_roll(ba, -A_HEADS) + dtb)
    gc, gl = _chunk_sums(g, mb)
    return beta, gc, gl, gc.T


def _gdn_block(s, q, k, v, z, gates, nw, h, t_known, mf):
    n = q.shape[0]
    beta_all, gc_all, gl_all, gct_all = gates
    lane = lax.broadcasted_iota(jnp.int32, (1, LANE), 1)
    sub = lax.broadcasted_iota(jnp.int32, (LANE, 1), 0)
    col = lambda x: jnp.sum(jnp.where(lane == h, x, 0.0), axis=1, keepdims=True)
    wide = lambda c: jnp.broadcast_to(c, (n, LANE))
    gc, gl = col(gc_all), col(gl_all)
    gc_row = jnp.sum(jnp.where(sub == h, gct_all, 0.0), axis=0, keepdims=True)
    beta_w, eg_w = wide(col(beta_all)), wide(jnp.exp(gc))
    diff = gc - gc_row
    decay = jnp.exp(diff + mf[MF_TRIL])
    kb = k * beta_w
    a_mat = _mm_nt(kb, k) * jnp.exp(diff + mf[MF_STRIL])
    rhs = jnp.concatenate([v * beta_w, kb * eg_w], axis=1)
    if t_known is None:
        t_mat = _tri_inv_impl(a_mat, mf)
        uw = _mm(t_mat, rhs)
    else:
        t_mat = t_known
        uw = _wy_apply(a_mat, rhs, t_known)
    u, w = uw[:, :LANE], uw[:, LANE:]
    qk = _mm_nt(q, k) * decay
    q_dec = q * eg_w
    k_dec = k * wide(jnp.exp(gl - gc))
    v_new = u - _mm(w, s)
    o = _mm(q_dec, s) + _mm(qk, v_new)
    s = s * jnp.exp(gl[0:1]) + _mm_tn(k_dec, v_new)
    o = o * lax.rsqrt(jnp.mean(o * o, axis=-1, keepdims=True) + RMS_EPS) * nw
    return o * _silu(z), s, t_mat


def _gdn_fwd(qkv, h, alog, dtb, nw, ycat, *, name, rider=None):
    t_len = qkv.shape[0]
    nsc = t_len // SUPER

    def core(ins, outs, scr):
        q_ref, k_ref, v_ref, gate_ref, al_ref, dt_ref, nw_ref, mf_ref, mb_ref, _ = ins
        y_ref, sin_ref, t_ref = outs
        s_scr, = scr

        @pl.when(pl.program_id(0) == 0)
        def _():
            s_scr[...] = jnp.zeros_like(s_scr)

        per_head = lambda ref: jnp.stack([ref[:, hh * LANE:(hh + 1) * LANE] for hh in range(A_HEADS)])
        states = s_scr[...]
        gates = _gdn_gates(gate_ref[:, A_WIDTH:], al_ref[...], dt_ref[...], mb_ref[...])
        fn = jax.vmap(_gdn_block, in_axes=(0, 0, 0, 0, 0, None, None, 0, None, None))
        y, s_new, t_mat = fn(states, per_head(q_ref), per_head(k_ref), per_head(v_ref), per_head(gate_ref),
                             gates, nw_ref[...], jnp.arange(A_HEADS), None, mf_ref[...])
        sin_ref[0] = states
        t_ref[0] = t_mat
        s_scr[...] = s_new
        for hh in range(A_HEADS):
            y_ref[:, hh * LANE:(hh + 1) * LANE] = y[hh].astype(y_ref.dtype)

    blk = lambda j: pl.BlockSpec((SUPER, A_WIDTH), lambda sc: (sc, j))
    row = pl.BlockSpec((1, LANE), lambda sc: (0, 0))
    mf, mb = _gdn_masks()
    whole = lambda a: pl.BlockSpec(a.shape, lambda sc: (0, 0, 0))
    return _pcall(
        core, name=name, grid=(nsc,),
        in_specs=[blk(0), blk(1), blk(2), pl.BlockSpec((SUPER, L_GATE), lambda sc: (sc, L_ZA // L_GATE)),
                  row, row, row, whole(mf), whole(mb), _ANY],
        out_specs=[blk(0),
                   pl.BlockSpec((1, A_HEADS, A_HEAD_DIM, A_HEAD_DIM), lambda sc: (sc, 0, 0, 0)),
                   pl.BlockSpec((1, A_HEADS, SUPER, SUPER), lambda sc: (sc, 0, 0, 0))],
        out_shape=[jax.ShapeDtypeStruct((t_len, D_MODEL), MM_DTYPE),
                   jax.ShapeDtypeStruct((nsc, A_HEADS, A_HEAD_DIM, A_HEAD_DIM), F32),
                   jax.ShapeDtypeStruct((nsc, A_HEADS, SUPER, SUPER), F32)],
        scratch_shapes=[pltpu.VMEM((A_HEADS, A_HEAD_DIM, A_HEAD_DIM), F32)],
        aliases={9: 0}, sem=("arbitrary",), rider=rider,
        args=(qkv, qkv, qkv, h, alog, dtb, nw, mf, mb, ycat))


def _gdn_bwd(qkv, h, alog, dtb, nw, s_in, t_in, dycat, dh, *, name, rider=None):
    t_len = qkv.shape[0]
    nsc = t_len // SUPER

    def core(ins, outs, scr):
        q_ref, k_ref, v_ref, gate_ref, al_ref, dt_ref, nw_ref, sin_ref, t_ref, dy_ref, mf_ref, mb_ref, _ = ins
        dgate_ref, dqkv_ref, dal_ref, ddt_ref, dnw_ref = outs
        ds_scr, = scr

        @pl.when(pl.program_id(0) == 0)
        def _():
            ds_scr[...] = jnp.zeros_like(ds_scr)
            dal_ref[...] = jnp.zeros_like(dal_ref)
            ddt_ref[...] = jnp.zeros_like(ddt_ref)
            dnw_ref[...] = jnp.zeros_like(dnw_ref)

        per_head = lambda ref: jnp.stack([ref[:, hh * LANE:(hh + 1) * LANE] for hh in range(A_HEADS)])
        head_ids = jnp.arange(A_HEADS)
        t_known, mf, mb = t_ref[0], mf_ref[...], mb_ref[...]

        def fn(s, q, k, v, z, ba, alog, dtb, nw):
            gates = _gdn_gates(ba, alog, dtb, mb)
            one = lambda s, q, k, v, z, t, h: _gdn_block(s, q, k, v, z, gates, nw, h, t, mf)[:2]
            return jax.vmap(one)(s, q, k, v, z, t_known, head_ids)

        _, vjp = jax.vjp(fn, sin_ref[0], per_head(q_ref), per_head(k_ref), per_head(v_ref), per_head(gate_ref),
                         gate_ref[:, A_WIDTH:], al_ref[...], dt_ref[...], nw_ref[...])
        ds, dq, dk, dv, dz, dba, dal, ddt, dnw = vjp((per_head(dy_ref), ds_scr[...]))
        ds_scr[...] = ds
        for hh in range(A_HEADS):
            cols = slice(hh * LANE, (hh + 1) * LANE)
            dqkv_ref[0, :, cols] = dq[hh]
            dqkv_ref[1, :, cols] = dk[hh]
            dqkv_ref[2, :, cols] = dv[hh]
            dgate_ref[:, cols] = dz[hh].astype(dgate_ref.dtype)
        dgate_ref[:, A_WIDTH:] = dba.astype(dgate_ref.dtype)
        dal_ref[...] += dal
        ddt_ref[...] += ddt
        dnw_ref[...] += dnw

    rev = lambda i: nsc - 1 - i
    blk = lambda j: pl.BlockSpec((SUPER, A_WIDTH), lambda i: (rev(i), j))
    gate = pl.BlockSpec((SUPER, L_GATE), lambda i: (rev(i), L_ZA // L_GATE))
    row = pl.BlockSpec((1, LANE), lambda i: (0, 0))
    mf, mb = _gdn_masks()
    whole = lambda a: pl.BlockSpec(a.shape, lambda i: (0, 0, 0))
    return _pcall(
        core, name=name, grid=(nsc,),
        in_specs=[blk(0), blk(1), blk(2), gate, row, row, row,
                  pl.BlockSpec((1, A_HEADS, A_HEAD_DIM, A_HEAD_DIM), lambda i: (rev(i), 0, 0, 0)),
                  pl.BlockSpec((1, A_HEADS, SUPER, SUPER), lambda i: (rev(i), 0, 0, 0)),
                  blk(0), whole(mf), whole(mb), _ANY],
        out_specs=[gate, pl.BlockSpec((3, SUPER, A_WIDTH), lambda i: (0, rev(i), 0)), row, row, row],
        out_shape=[jax.ShapeDtypeStruct((t_len, L_COLS), MM_DTYPE), jax.ShapeDtypeStruct((3, t_len, A_WIDTH), F32)]
        + [jax.ShapeDtypeStruct((1, LANE), F32)] * 3,
        scratch_shapes=[pltpu.VMEM((A_HEADS, A_HEAD_DIM, A_HEAD_DIM), F32)],
        aliases={12: 0}, sem=("arbitrary",), rider=rider,
        args=(qkv, qkv, qkv, h, alog, dtb, nw, s_in, t_in, dycat, mf, mb, dh))


Q_BLOCKS = 4
Q_ROWS = Q_BLOCKS * BLOCK


def _swa_block(q, kp, kc, vp, vc, z, sinks, first):
    rows = B_GROUP * BLOCK
    ri = lax.broadcasted_iota(jnp.int32, (rows, 2 * BLOCK), 0)
    si = lax.broadcasted_iota(jnp.int32, (rows, 2 * BLOCK), 1)
    dist = (ri & (BLOCK - 1)) + BLOCK - si
    bias = jnp.where((dist >= 0) & (dist < WINDOW), 0.0, -jnp.inf)
    no_prev = jnp.where(first & (si[:1] < BLOCK), -jnp.inf, 0.0)
    dist_f = dist.astype(F32)
    head_of_row = lax.broadcasted_iota(jnp.int32, (rows, 1), 0) >> 7
    keys = jnp.concatenate([kp, kc], axis=0)
    vals = jnp.concatenate([vp, vc], axis=0)

    def item(b, j):
        cs = slice(j * B_HEAD_DIM, (j + 1) * B_HEAD_DIM)
        rs = slice(b * BLOCK, (b + 1) * BLOCK)
        heads = range(j * B_GROUP, (j + 1) * B_GROUP)
        qs = jnp.concatenate([q[rs, hq * B_HEAD_DIM:(hq + 1) * B_HEAD_DIM] for hq in heads], axis=0) * (
            B_HEAD_DIM ** -0.5)
        kk = keys[b * BLOCK:(b + 2) * BLOCK, cs]
        vv = vals[b * BLOCK:(b + 2) * BLOCK, cs]
        sink = jnp.concatenate([jnp.broadcast_to(sinks[:, hq:hq + 1], (BLOCK, 1)) for hq in heads], axis=0)
        slope = sum(jnp.where(head_of_row == gi, 2.0 ** (-8.0 * (hq + 1) / B_Q_HEADS), 0.0)
                    for gi, hq in enumerate(heads))
        return qs, kk, vv, sink, slope, (no_prev if b == 0 else jnp.zeros_like(no_prev))

    def attend(qs, kk, vv, sink, slope, hide):
        sc = _mm_nt(qs, kk) - slope * dist_f + (bias + hide)
        m = lax.stop_gradient(jnp.maximum(jnp.max(sc, axis=-1, keepdims=True), sink))
        p = jnp.exp(sc - m)
        inv = 1.0 / (jnp.sum(p, axis=-1, keepdims=True) + jnp.exp(sink - m))
        return _mm(p * inv, vv)

    items = [(b, j) for b in range(Q_BLOCKS) for j in range(B_KV_HEADS)]
    o = jax.vmap(attend)(*[_stack(t) for t in zip(*[item(b, j) for b, j in items])])
    rows_out = [jnp.concatenate([o[b * B_KV_HEADS + j, gi * BLOCK:(gi + 1) * BLOCK]
                                 for j in range(B_KV_HEADS) for gi in range(B_GROUP)], axis=1)
                for b in range(Q_BLOCKS)]
    return jnp.concatenate(rows_out, axis=0) * _silu(z)


def _swa_specs(idx):
    wide = lambda off: pl.BlockSpec((Q_ROWS, B_WIDTH), lambda n: (idx(n), off))
    cur = lambda off: pl.BlockSpec((Q_ROWS, LANE), lambda n: (idx(n), off))
    prev = lambda off: pl.BlockSpec((BLOCK, LANE), lambda n: (jnp.maximum(idx(n) * Q_BLOCKS - 1, 0), off))
    return [wide(L_QB // B_WIDTH), prev(L_KB // LANE), cur(L_KB // LANE), prev(L_VB // LANE), cur(L_VB // LANE),
            wide(L_ZB // B_WIDTH), pl.BlockSpec((1, LANE), lambda n: (0, 0))]


def _swa_fwd(h, sinks, *, name, rider=None):
    t_len = h.shape[0]
    nb = t_len // Q_ROWS

    def core(ins, outs, _):
        q_ref, kp_ref, kc_ref, vp_ref, vc_ref, z_ref, s_ref = ins
        outs[0][...] = _swa_block(q_ref[...], kp_ref[...], kc_ref[...], vp_ref[...], vc_ref[...], z_ref[...],
                                  s_ref[...], pl.program_id(0) == 0).astype(outs[0].dtype)

    res = _pcall(core, name=name, grid=(nb,), in_specs=_swa_specs(lambda n: n),
                 out_specs=[pl.BlockSpec((Q_ROWS, B_WIDTH), lambda n: (n, 1))],
                 out_shape=[jax.ShapeDtypeStruct((t_len, D_MODEL), MM_DTYPE)], sem=("parallel",), rider=rider,
                 args=(h, h, h, h, h, h, sinks))
    return res if rider else res[0]


def _swa_bwd(h, sinks, dycat, *, name, rider=None):
    t_len = h.shape[0]
    nb = t_len // Q_ROWS
    early = slice(0, Q_ROWS - BLOCK)
    last = slice(Q_ROWS - BLOCK, Q_ROWS)

    def core(ins, outs, scr):
        q_ref, kp_ref, kc_ref, vp_ref, vc_ref, z_ref, s_ref, dy_ref = ins
        dh_ref, dsk_ref = outs
        ck_scr, cv_scr = scr
        i = pl.program_id(0)
        n = nb - 1 - i

        @pl.when(i == 0)
        def _():
            ck_scr[...] = jnp.zeros_like(ck_scr)
            cv_scr[...] = jnp.zeros_like(cv_scr)
            dsk_ref[...] = jnp.zeros_like(dsk_ref)

        fn = functools.partial(_swa_block, first=(n == 0))
        _, vjp = jax.vjp(fn, q_ref[...], kp_ref[...], kc_ref[...], vp_ref[...], vc_ref[...], z_ref[...], s_ref[...])
        dq, dkp, dkc, dvp, dvc, dz, dsk = vjp(dy_ref[...])
        def put(rows, col, val):
            dh_ref[rows, col:col + val.shape[1]] = val.astype(dh_ref.dtype)

        put(slice(None), L_QB, dq)
        put(slice(None), L_ZB, dz)
        put(early, L_KB, dkc[early])
        put(early, L_VB, dvc[early])
        put(last, L_KB, dkc[last] + ck_scr[...])
        put(last, L_VB, dvc[last] + cv_scr[...])
        ck_scr[...] = dkp
        cv_scr[...] = dvp
        dsk_ref[...] += dsk

    rev = lambda i: nb - 1 - i
    return _pcall(
        core, name=name, grid=(nb,),
        in_specs=_swa_specs(rev) + [pl.BlockSpec((Q_ROWS, B_WIDTH), lambda i: (rev(i), 1))],
        out_specs=[pl.BlockSpec((Q_ROWS, L_SWA), lambda i: (rev(i), 0)), pl.BlockSpec((1, LANE), lambda i: (0, 0))],
        out_shape=[jax.ShapeDtypeStruct((t_len, L_COLS), MM_DTYPE), jax.ShapeDtypeStruct((1, LANE), F32)],
        scratch_shapes=[pltpu.VMEM((BLOCK, LANE), F32), pltpu.VMEM((BLOCK, LANE), F32)],
        sem=("arbitrary",), rider=rider, args=(h, h, h, h, h, h, sinks, dycat))


def _out_ln_fwd(ycat, w_out, x, ln_g, ln_b, *, name, tm=512, w_in_next=None):
    t_len = x.shape[0]
    last = w_in_next is None

    def body(y_ref, w_ref, x_ref, g_ref, b_ref, *rest):
        r = DEEPNORM_ALPHA * x_ref[...] + _mm(y_ref[...], w_ref[...])
        if last:
            rest[0][...] = r
            return
        win_ref, r_ref, o_ref, h_ref = rest
        r_ref[...] = r
        mu = jnp.mean(r, axis=-1, keepdims=True)
        d = r - mu
        var = jnp.mean(d * d, axis=-1, keepdims=True)
        xn = d * lax.rsqrt(var + LN_EPS) * g_ref[...] + b_ref[...]
        o_ref[...] = xn
        h_ref[...] = _mm_nt(xn, win_ref[...])

    tile = pl.BlockSpec((tm, D_MODEL), lambda i: (i, 0))
    vec = pl.BlockSpec((1, D_MODEL), lambda i: (0, 0))
    tile_shape = jax.ShapeDtypeStruct((t_len, D_MODEL), F32)
    in_specs = [tile, pl.BlockSpec((D_MODEL, D_MODEL), lambda i: (0, 0)), tile, vec, vec]
    if last:
        args, out_specs, out_shape = (), [tile], [tile_shape]
    else:
        args = (w_in_next,)
        in_specs.append(pl.BlockSpec((L_COLS, D_MODEL), lambda i: (0, 0)))
        out_specs = [tile, tile, pl.BlockSpec((tm, L_COLS), lambda i: (i, 0))]
        out_shape = [tile_shape, tile_shape, jax.ShapeDtypeStruct((t_len, L_COLS), F32)]
    res = pl.pallas_call(
        body, name=name, grid=(t_len // tm,), in_specs=in_specs, out_specs=out_specs, out_shape=out_shape,
        compiler_params=_cparams(("parallel",)),
    )(ycat, w_out, x, ln_g, ln_b, *args)
    return (res[0], None, None) if last else res


def _ln_out_bwd(dxn, r, ln_g, ycat, w_out, *, name, tm=512, loss=None):
    t_len = r.shape[0]

    def body(*refs):
        if loss:
            t_ref, r_ref, g_ref, b_ref, y_ref, w_ref, dr_ref, dg_ref, db_ref, l_ref, dy_ref, dw_ref = refs
        else:
            dx_ref, r_ref, g_ref, y_ref, w_ref, dr_ref, dg_ref, db_ref, dy_ref, dw_ref = refs

        @pl.when(pl.program_id(0) == 0)
        def _():
            dg_ref[...] = jnp.zeros_like(dg_ref)
            db_ref[...] = jnp.zeros_like(db_ref)
            dw_ref[...] = jnp.zeros_like(dw_ref)
            if loss:
                l_ref[...] = jnp.zeros_like(l_ref)

        rr = r_ref[...]
        mu = jnp.mean(rr, axis=-1, keepdims=True)
        d = rr - mu
        rstd = lax.rsqrt(jnp.mean(d * d, axis=-1, keepdims=True) + LN_EPS)
        xh = d * rstd
        if loss:
            e = (xh * g_ref[...] + b_ref[...]) - t_ref[...]
            dx = e * (1.0 / D_MODEL)
            l_ref[...] += jnp.sum(e * e, axis=0, keepdims=True)
        else:
            dx = dx_ref[...]
        dxh = dx * g_ref[...]
        dr = rstd * (dxh - jnp.mean(dxh, axis=-1, keepdims=True) - xh * jnp.mean(dxh * xh, axis=-1, keepdims=True))
        dr_ref[...] = dr
        dg_ref[...] += jnp.sum(dx * xh, axis=0, keepdims=True)
        db_ref[...] += jnp.sum(dx, axis=0, keepdims=True)
        dy_ref[...] = _mm_nt(dr, w_ref[...])
        dw_ref[...] += _mm_tn(y_ref[...], dr)

    tile = pl.BlockSpec((tm, D_MODEL), lambda i: (i, 0))
    vec = pl.BlockSpec((1, D_MODEL), lambda i: (0, 0))
    square = pl.BlockSpec((D_MODEL, D_MODEL), lambda i: (0, 0))
    tile_shape = jax.ShapeDtypeStruct((t_len, D_MODEL), F32)
    vec_shape = jax.ShapeDtypeStruct((1, D_MODEL), F32)
    args = (loss[0], r, ln_g, loss[1]) if loss else (dxn, r, ln_g)
    return pl.pallas_call(
        body, name=name, grid=(t_len // tm,),
        in_specs=[tile, tile, vec] + ([vec] if loss else []) + [tile, square],
        out_specs=[tile, vec, vec] + ([vec] if loss else []) + [tile, square],
        out_shape=[tile_shape, vec_shape, vec_shape] + ([vec_shape] if loss else [])
        + [tile_shape, jax.ShapeDtypeStruct((D_MODEL, D_MODEL), F32)],
        compiler_params=_cparams(("arbitrary",)),
    )(*args, ycat, w_out)


def _pad_row(v):
    return jnp.zeros((1, LANE), F32).at[0, :v.shape[0]].set(v)


_REGIONS = ((0, 1536, L_QKV), (1536, 2048, L_ZA), (2048, 2056, L_BA), (2056, 2568, L_QB), (2568, 2696, L_KB),
            (2696, 2824, L_VB), (2824, 3336, L_ZB))


def _shard_pieces(regions):
    for a, b, off in regions:
        for d in range(N_DEV):
            lo, hi = max(a, d * SHARD_COLS), min(b, (d + 1) * SHARD_COLS)
            if lo < hi:
                yield d, lo - d * SHARD_COLS, hi - d * SHARD_COLS, off + lo - a


def _as_list(r):
    return list(r) if isinstance(r, (list, tuple)) else [r]


def _gathered(shard):
    return jax.ShapeDtypeStruct((N_DEV,) + shard.shape, shard.dtype)


def _full_w_in(g_in, name):
    by_offset = sorted(_shard_pieces(_REGIONS), key=lambda p: p[3])
    tc = 256

    def body(g_ref, o_ref):
        pieces, row = [], 0
        for d, lo, hi, off in by_offset + [(None, 0, 0, L_COLS)]:
            if off > row:
                pieces.append(jnp.zeros((off - row, tc), g_ref.dtype))
            if d is not None:
                pieces.append(g_ref[d, lo:hi, :])
            row = off + hi - lo
        o_ref[...] = jnp.concatenate(pieces, axis=0)

    return pl.pallas_call(
        body, name=name, grid=(D_MODEL // tc,),
        in_specs=[pl.BlockSpec((N_DEV, SHARD_COLS, tc), lambda i: (0, 0, i))],
        out_specs=pl.BlockSpec((L_COLS, tc), lambda i: (0, i)),
        out_shape=jax.ShapeDtypeStruct((L_COLS, D_MODEL), g_in.dtype),
        compiler_params=_cparams(("parallel",)),
    )(g_in)


def _full_conv(g_conv):
    return jnp.pad(g_conv.transpose(1, 0, 2).reshape(CONV_K, 3 * A_WIDTH), ((0, 8 - CONV_K), (0, 0)))


def _forward(x, weights, shards, small):
    a_log, dt_bias, norm_w, sinks, ln_g, ln_b = small
    tm = min(512, x.shape[0])
    saved, weights = [], [list(w) for w in weights]
    whole = lambda arrs: _Direct([(a, False, j, ()) for j, a in enumerate(arrs)], [_gathered(a) for a in arrs])
    h = None
    for l in range(DEPTH):
        if h is None:
            rider = whole(shards[l][1:]) if weights[l][1] is None else None
            h, *got = _as_list(_matmul(x, weights[l][0], form="nt", tm=tm, tn=L_COLS, tk=D_MODEL,
                                       name=f"in_proj_{l}", rider=rider))
            if rider:
                weights[l][1:] = [got[0].reshape(D_MODEL, D_MODEL), _full_conv(got[1])]
        w_in_l, w_out_l, conv_l = weights[l]
        qkv = _prep_fwd(h, conv_l, name=f"prep_fwd_{l}")
        al, dt, nw, sk = _pad_row(a_log[l]), _pad_row(dt_bias[l]), norm_w[l][None, :], _pad_row(sinks[l])
        ahead = l + 1 < DEPTH and weights[l + 1][0] is None
        rider = whole(shards[l + 1][1:]) if ahead else None
        ycat, *got = _as_list(_swa_fwd(h, sk, name=f"swa_fwd_{l}", rider=rider))
        if ahead:
            weights[l + 1][1:] = [got[0].reshape(D_MODEL, D_MODEL), _full_conv(got[1])]
        rider = whole(shards[l + 1][:1]) if ahead else None
        ycat, s_in, t_in, *got = _gdn_fwd(qkv, h, al, dt, nw, ycat, name=f"gdn_fwd_{l}", rider=rider)
        if ahead:
            weights[l + 1][0] = _full_w_in(got[0], f"w_in_rows_{l + 1}")
        r, xn, h_next = _out_ln_fwd(ycat, w_out_l, x, ln_g[l][None, :], ln_b[l][None, :], name=f"out_ln_{l}",
                                    w_in_next=weights[l + 1][0] if l + 1 < DEPTH else None)
        saved.append((x, h, qkv, s_in, t_in, ycat, r, al, dt, nw, sk))
        x, h = xn, h_next
    return x, saved, weights


def _w_in_blocks(g, name):
    cols, tc = g.shape[1], 256
    pieces = list(_shard_pieces(_REGIONS))

    def body(g_ref, o_ref):
        blocks = [[] for _ in range(N_DEV)]
        for d, lo, hi, off in pieces:
            blocks[d].append(g_ref[off:off + hi - lo, :])
        for d in range(N_DEV):
            o_ref[d] = jnp.concatenate(blocks[d], axis=0).astype(BF16)

    return pl.pallas_call(
        body, name=name, grid=(cols // tc,),
        in_specs=[pl.BlockSpec((L_COLS, tc), lambda i: (0, i))],
        out_specs=pl.BlockSpec((N_DEV, SHARD_COLS, tc), lambda i: (0, 0, i)),
        out_shape=jax.ShapeDtypeStruct((N_DEV, SHARD_COLS, cols), BF16),
        compiler_params=_cparams(("parallel",)),
    )(g)


def _small_blocks(g):
    c_conv = g["conv_w"].reshape(CONV_K, N_DEV, CONV_SHARD_COLS).transpose(1, 0, 2)
    c_small = [jnp.broadcast_to(g[n][None], (N_DEV,) + g[n].shape) for n, _ in SMALL_SIZES]
    return _pack_small(c_conv, c_small)


def _contributions(g):
    c_out = g["w_out"].astype(BF16).reshape(N_DEV, OUT_SHARD_ROWS, D_MODEL)
    return _w_in_blocks(g["w_in_rows"], name="w_in_grad_blocks_above"), c_out, _small_blocks(g)


def _backward_layer(l, dx, saved_l, weights_l, ln_g_l, above=None, loss=None):
    x_in, h, qkv, s_in, t_in, ycat, r, al, dt, nw, sk = saved_l
    w_in_l, w_out_l, conv_l = weights_l
    tm = min(512, x_in.shape[0])
    dr, d_lng, d_lnb, *loss_lanes, dycat, d_wout = _ln_out_bwd(dx, r, ln_g_l[None, :], ycat, w_out_l,
                                                               name=f"ln_out_bwd_{l}", tm=tm, loss=loss)
    big = min(1024, x_in.shape[0])
    rider, p_in, p_out, p_small = None, None, None, None
    recv = lambda c: jax.ShapeDtypeStruct((DEPTH,) + c.shape, c.dtype)
    if above:
        c_out = d_wout.astype(BF16).reshape(N_DEV, OUT_SHARD_ROWS, D_MODEL)
        rider = _Direct([(above[1], True, 0, (l + 1,)), (above[2], True, 1, (l + 1,)), (c_out, True, 0, (l,))],
                        [recv(above[1]), recv(above[2])])
    dh, d_sk, *got = _swa_bwd(h, sk, dycat, name=f"swa_bwd_{l}", rider=rider)
    if above:
        p_out, p_small = got
        rider = _Direct([(above[0], True, 0, (l + 1,))], [recv(above[0])])
    dh, dqkv_n, d_al, d_dt, d_nw, *got = _gdn_bwd(qkv, h, al, dt, nw, s_in, t_in, dycat, dh,
                                                  name=f"gdn_bwd_{l}", rider=rider)
    dh, d_conv = _prep_bwd(h, conv_l, dqkv_n, dh, name=f"prep_bwd_{l}")
    grads = dict(w_out=d_wout, conv_w=d_conv[:CONV_K], a_log=d_al[0, :A_HEADS], dt_bias=d_dt[0, :A_HEADS],
                 norm_w=d_nw[0], sinks=d_sk[0, :B_Q_HEADS], ln_g=d_lng[0], ln_b=d_lnb[0])
    dw = functools.partial(_matmul, dh, x_in, form="tn")
    if not above:
        grads["w_in_rows"] = dw(name=f"in_proj_dw_{l}", tm=L_COLS // 3, tn=D_MODEL, tk=min(2048, x_in.shape[0]))
    else:
        p_in, = got
        cut = D_MODEL // 2
        rest = D_MODEL - cut
        first = dw(name=f"in_proj_dw_first_{l}", tm=L_COLS, tn=cut, tk=big, b_cols=(0, cut))
        blocks = _w_in_blocks(first, name=f"w_in_grad_blocks_first_{l}")
        rider = _Direct([(blocks, True, 0, (l,), (slice(None), pl.ds(0, cut)))], [p_in])
        second, p_in = dw(name=f"in_proj_dw_second_{l}", tm=L_COLS, tn=cut, tk=big, b_cols=(cut, rest), rider=rider)
        blocks = _w_in_blocks(second, name=f"w_in_grad_blocks_second_{l}")
        rider = _Direct([(blocks, True, 0, (l,), (slice(None), pl.ds(cut, rest))),
                         (_small_blocks(grads), True, 1, (l,))], [p_in, p_small])
    dx, *got = _as_list(_matmul(dh, w_in_l, form="nn", tm=tm, tn=D_MODEL, tk=L_COLS, name=f"in_proj_dx_{l}",
                                add=dr, add_scale=DEEPNORM_ALPHA, rider=rider))
    bufs = (got[0], p_out, got[1]) if above else None
    return dx, grads, bufs, (loss_lanes[0] if loss else None)


def _all_gather(shards, *, name):
    n_arr = len(shards)

    def body(*refs):
        x_refs, out_refs = refs[:n_arr], refs[n_arr:2 * n_arr]
        send_sems, recv_sems, local_sems = refs[2 * n_arr:]
        x, y, c = _me()
        me, sibling = (x, y, c), (x, y, 1 - c)
        chips = [(1 - x, y), (x, 1 - y), (1 - x, 1 - y)]

        def copy(a, k, block, to, src=None):
            dst = out_refs[a].at[_flat_id(block)]
            return _remote(dst if src is None else src, dst, send_sems.at[a, k], recv_sems.at[a, k], to)

        mine = [pltpu.make_async_copy(x_refs[a], out_refs[a].at[_flat_id(me)], local_sems.at[a])
                for a in range(n_arr)]
        for cp in mine:
            cp.start()
        first = []
        for a in range(n_arr):
            first.append(copy(a, 0, me, sibling, src=x_refs[a]))
            first += [copy(a, 1 + j, me, (*chip, c), src=x_refs[a]) for j, chip in enumerate(chips)]
        for cp in first:
            cp.start()
        passed = []
        for j, chip in enumerate(chips):
            for a in range(n_arr):
                copy(a, 1 + j, (*chip, c), me).wait_recv()
                fwd = copy(a, 4 + j, (*chip, c), sibling)
                fwd.start()
                passed.append(fwd)
        for a in range(n_arr):
            copy(a, 0, sibling, me).wait_recv()
            for j, chip in enumerate(chips):
                copy(a, 4 + j, (*chip, 1 - c), me).wait_recv()
        for cp in first + passed:
            cp.wait_send()
        for cp in mine:
            cp.wait()

    return pl.pallas_call(
        body, name=name, in_specs=[_ANY] * n_arr, out_specs=[_ANY] * n_arr,
        out_shape=[jax.ShapeDtypeStruct((N_DEV,) + s.shape, s.dtype) for s in shards],
        scratch_shapes=[pltpu.SemaphoreType.DMA((n_arr, N_DEV - 1)), pltpu.SemaphoreType.DMA((n_arr, N_DEV - 1)),
                        pltpu.SemaphoreType.DMA((n_arr,))],
    )(*shards)


def _adamw(parts, w, m, v, *, tr, name):
    depth, rows, cols = w.shape
    c1 = 1.0 - ADAM_B1 ** ADAM_STEP
    c2 = 1.0 - ADAM_B2 ** ADAM_STEP

    def body(g_ref, w_ref, m_ref, v_ref, go_ref, d_ref, mo_ref, vo_ref):
        g = g_ref[0, 0].astype(F32)
        for s in range(1, N_DEV):
            g = g + g_ref[0, s].astype(F32)
        m_new = ADAM_B1 * m_ref[0] + (1.0 - ADAM_B1) * g
        v_new = ADAM_B2 * v_ref[0] + (1.0 - ADAM_B2) * (g * g)
        go_ref[0] = g
        mo_ref[0] = m_new
        vo_ref[0] = v_new
        d_ref[0] = -ADAM_LR * ((m_new / c1) / (jnp.sqrt(v_new / c2) + ADAM_EPS) + ADAM_WD * w_ref[0])

    tile = pl.BlockSpec((1, tr, cols), lambda l, i: (l, i, 0))
    return pl.pallas_call(
        body, name=name, grid=(depth, rows // tr),
        in_specs=[pl.BlockSpec((1, N_DEV, tr, cols), lambda l, i: (l, 0, i, 0)), tile, tile, tile],
        out_specs=[tile] * 4, out_shape=[jax.ShapeDtypeStruct(w.shape, F32)] * 4,
        compiler_params=_cparams(("parallel", "parallel")),
    )(parts, w, m, v)


def _adamw_w_in(parts, w, m, v, *, name):
    c1 = 1.0 - ADAM_B1 ** ADAM_STEP
    c2 = 1.0 - ADAM_B2 ** ADAM_STEP

    def body(g_ref, w_ref, m_ref, v_ref, go_ref, d_ref, mo_ref, vo_ref):
        gs = []
        for l in range(DEPTH):
            g = g_ref[l, 0].astype(F32)
            for s in range(1, N_DEV):
                g = g + g_ref[l, s].astype(F32)
            gs.append(g)
        g = jnp.stack(gs, axis=1)
        m_new = ADAM_B1 * m_ref[...] + (1.0 - ADAM_B1) * g
        v_new = ADAM_B2 * v_ref[...] + (1.0 - ADAM_B2) * (g * g)
        go_ref[...] = g
        mo_ref[...] = m_new
        vo_ref[...] = v_new
        d_ref[...] = -ADAM_LR * ((m_new / c1) / (jnp.sqrt(v_new / c2) + ADAM_EPS) + ADAM_WD * w_ref[...])

    tile = pl.BlockSpec((SHARD_COLS, DEPTH, LANE), lambda i: (0, 0, i))
    return pl.pallas_call(
        body, name=name, grid=(D_MODEL // LANE,),
        in_specs=[pl.BlockSpec((DEPTH, N_DEV, SHARD_COLS, LANE), lambda i: (0, 0, 0, i)), tile, tile, tile],
        out_specs=[tile] * 4, out_shape=[jax.ShapeDtypeStruct(w.shape, F32)] * 4,
        compiler_params=_cparams(("parallel",)),
    )(parts, w, m, v)


def _pack_small(conv, small):
    lead = conv.shape[:-2]
    flat = jnp.concatenate([conv.reshape(lead + (CS_CONV,))] + list(small), axis=-1)
    pad = CS_ROWS * LANE - flat.shape[-1]
    flat = jnp.concatenate([flat, jnp.zeros(lead + (pad,), F32)], axis=-1)
    return flat.reshape(lead + (CS_ROWS, LANE))


def _unpack_small(p):
    flat = p.reshape(DEPTH, CS_ROWS * LANE)
    conv = flat[:, :CS_CONV].reshape(DEPTH, CONV_K, CONV_SHARD_COLS)
    small, off = [], CS_CONV
    for _, n in SMALL_SIZES:
        small.append(flat[:, off:off + n])
        off += n
    return conv, small


def kernel(x, w_in, conv_w, a_log, dt_bias, norm_w, sinks, w_out, ln_g, ln_b, loss_target, m_w_in, m_conv_w, m_a_log, m_dt_bias, m_norm_w, m_sinks, m_w_out, m_ln_g, m_ln_b, v_w_in, v_conv_w, v_a_log, v_dt_bias, v_norm_w, v_sinks, v_w_out, v_ln_g, v_ln_b):
    small = [a_log, dt_bias, norm_w, sinks, ln_g, ln_b]
    w_t, m_t, v_t = (a.transpose(2, 0, 1) for a in (w_in, m_w_in, v_w_in))
    shards = [[w_t[:, l].astype(BF16), w_out[l].astype(BF16), conv_w[l]] for l in range(DEPTH)]
    g_in0, = _all_gather(shards[0][:1], name="weights_all_gather_0")
    weights = [[_full_w_in(g_in0, "w_in_rows_0"), None, None]] + [[None, None, None]] * (DEPTH - 1)

    _, saved, weights = _forward(x[0], weights, shards, small)
    dx, g1, _, loss_lanes = _backward_layer(1, None, saved[1], weights[1], ln_g[1],
                                            loss=(loss_target[0], ln_b[1][None, :]))
    loss = lax.psum(0.5 * jnp.sum(loss_lanes) * (1.0 / D_MODEL), ("x", "y", "c"))
    dx, _, (p_in, p_out, p_small), _ = _backward_layer(0, dx, saved[0], weights[0], ln_g[0],
                                                       above=_contributions(g1))

    o_in = [o.transpose(1, 2, 0) for o in _adamw_w_in(p_in, w_t, m_t, v_t, name="adamw_w_in")]
    o_out = _adamw(p_out, w_out, m_w_out, v_w_out, tr=OUT_SHARD_ROWS, name="adamw_w_out")
    o_small = _adamw(p_small, _pack_small(conv_w, small),
                     _pack_small(m_conv_w, [m_a_log, m_dt_bias, m_norm_w, m_sinks, m_ln_g, m_ln_b]),
                     _pack_small(v_conv_w, [v_a_log, v_dt_bias, v_norm_w, v_sinks, v_ln_g, v_ln_b]),
                     tr=CS_ROWS, name="adamw_small")
    outs = []
    for k in range(4):
        cv, sm = _unpack_small(o_small[k])
        outs += [o_in[k], cv, sm[0], sm[1], sm[2], sm[3], o_out[k], sm[4], sm[5]]
    return (loss, dx[None], *outs)
```

```python
import functools

import jax
import jax.numpy as jnp
from jax import lax
from jax.experimental import pallas as pl
from jax.experimental.pallas import tpu as pltpu

F32 = jnp.float32
BF16 = jnp.bfloat16
MM_DTYPE = BF16

N_DEV = 8
D_MODEL = 1024
DEPTH = 2
A_HEADS = 4
A_HEAD_DIM = 128
A_WIDTH = 512
CONV_K = 4
SUPER = 256
NEWTON_STEPS = 1
B_Q_HEADS = 8
B_KV_HEADS = 2
B_HEAD_DIM = 64
B_GROUP = 4
B_WIDTH = 512
WINDOW = 128
BLOCK = 128
IN_COLS = 3336
SHARD_COLS = IN_COLS // N_DEV
OUT_SHARD_ROWS = D_MODEL // N_DEV
CONV_SHARD_COLS = 3 * A_WIDTH // N_DEV
DEEPNORM_ALPHA = (2 * DEPTH) ** 0.25
LN_EPS = 1e-5
RMS_EPS = 1e-6
L2_EPS = 1e-6
ADAM_LR, ADAM_B1, ADAM_B2, ADAM_EPS, ADAM_WD, ADAM_STEP = 0.001, 0.9, 0.999, 1e-08, 0.01, 10

LANE = 128
L_QB, L_ZB, L_KB, L_VB, L_ZA, L_BA, L_QKV = 0, 512, 1024, 1152, 1280, 1792, 1920
L_SWA = 1280
L_GATE = 640
L_COLS = 3456
SMALL_SIZES = (("a_log", 4), ("dt_bias", 4), ("norm_w", 128), ("sinks", 8), ("ln_g", 1024), ("ln_b", 1024))
CS_CONV = CONV_K * CONV_SHARD_COLS
CS_ROWS = 24
VMEM_LIMIT = 48 * 1024 * 1024


def _cparams(sem=None):
    return pltpu.CompilerParams(dimension_semantics=sem, vmem_limit_bytes=VMEM_LIMIT)


def _mm(a, b):
    return jnp.dot(a.astype(MM_DTYPE), b.astype(MM_DTYPE), preferred_element_type=F32)


def _mm_nt(a, b):
    return lax.dot_general(a.astype(MM_DTYPE), b.astype(MM_DTYPE), (((1,), (1,)), ((), ())),
                           preferred_element_type=F32)


def _mm_tn(a, b):
    return lax.dot_general(a.astype(MM_DTYPE), b.astype(MM_DTYPE), (((0,), (0,)), ((), ())),
                           preferred_element_type=F32)


def _split(a):
    hi = a.astype(BF16)
    return hi, (a - hi.astype(F32)).astype(BF16)


def _silu(x):
    return x * jax.nn.sigmoid(x)


@jax.custom_vjp
def _stack(parts):
    return jnp.stack(parts)


_stack.defvjp(lambda parts: (jnp.stack(parts), None), lambda _, g: (tuple(g[i] for i in range(g.shape[0])),))


def _softplus(x):
    return jnp.maximum(x, 0.0) + jnp.log1p(jnp.exp(-jnp.abs(x)))


_ANY = pl.BlockSpec(memory_space=pl.ANY)


def _me():
    return lax.axis_index("x"), lax.axis_index("y"), lax.axis_index("c")


def _flat_id(pos):
    return 4 * pos[0] + 2 * pos[1] + pos[2]


def _remote(src, dst, send_sem, recv_sem, to):
    return pltpu.make_async_remote_copy(src_ref=src, dst_ref=dst, send_sem=send_sem, recv_sem=recv_sem,
                                        device_id=to, device_id_type=pl.DeviceIdType.MESH)


class _Direct:
    def __init__(self, items, bufs):
        self.items, self.bufs = list(items), list(bufs)
        self.n_src, self.n_buf = len(self.items), len(self.bufs)
        self.old = [j for j, b in enumerate(self.bufs) if not isinstance(b, jax.ShapeDtypeStruct)]
        self.args = [it[0] for it in self.items] + [self.bufs[j] for j in self.old]
        self.out_shape = [jax.ShapeDtypeStruct(b.shape, b.dtype) for b in self.bufs]
        self.scratch = [pltpu.SemaphoreType.DMA((self.n_src, N_DEV - 1)),
                        pltpu.SemaphoreType.DMA((self.n_src, N_DEV - 1)), pltpu.SemaphoreType.DMA((self.n_src,))]

    def aliases(self, in_base, out_base):
        return {in_base + self.n_src + pos: out_base + j for pos, j in enumerate(self.old)}

    def copies(self, in_refs, out_refs, sems):
        send_sems, recv_sems, local_sems = sems
        x, y, c = _me()
        me = _flat_id((x, y, c))
        peers = [(x ^ ((rel >> 2) & 1), y ^ ((rel >> 1) & 1), c ^ (rel & 1)) for rel in range(1, N_DEV)]
        local, sends, recvs = [], [], []
        for a, (_, per_dest, j, prefix, *rest) in enumerate(self.items):
            src = lambda d: in_refs[a].at[d] if per_dest else in_refs[a]
            dst = lambda s: out_refs[j].at[tuple(prefix) + (s,) + tuple(rest[0] if rest else ())]
            local.append(pltpu.make_async_copy(src(me), dst(me), local_sems.at[a]))
            for k, peer in enumerate(peers):
                pid = _flat_id(peer)
                sends.append(_remote(src(pid), dst(me), send_sems.at[a, k], recv_sems.at[a, k], peer))
                recvs.append(_remote(src(pid), dst(pid), send_sems.at[a, k], recv_sems.at[a, k], peer))
        return local, sends, recvs

    def start(self, in_refs, out_refs, sems):
        local, sends, _ = self.copies(in_refs, out_refs, sems)
        for cp in local + sends:
            cp.start()

    def wait(self, in_refs, out_refs, sems):
        local, sends, recvs = self.copies(in_refs, out_refs, sems)
        for cp in recvs:
            cp.wait_recv()
        for cp in sends:
            cp.wait_send()
        for cp in local:
            cp.wait()


def _pcall(core, *, name, grid, in_specs, out_specs, out_shape, args, sem, scratch_shapes=(), aliases=None,
           rider=None):
    n_in, n_out, n_scr = len(in_specs), len(out_specs), len(scratch_shapes)
    n_rin, n_rout = (len(rider.args), rider.n_buf) if rider else (0, 0)

    def body(*refs):
        ins, outs = refs[:n_in], refs[n_in + n_rin:n_in + n_rin + n_out]
        scr = refs[n_in + n_rin + n_out + n_rout:n_in + n_rin + n_out + n_rout + n_scr]
        if rider:
            r_refs = (refs[n_in:n_in + rider.n_src], refs[n_in + n_rin + n_out:n_in + n_rin + n_out + n_rout],
                      refs[n_in + n_rin + n_out + n_rout + n_scr:])
            ids = [pl.program_id(d) for d in range(len(grid))]
            first = functools.reduce(lambda p, q: p & q, [i == 0 for i in ids])
            last = functools.reduce(lambda p, q: p & q, [i == g - 1 for i, g in zip(ids, grid)])
            pl.when(first)(lambda: rider.start(*r_refs))
        core(ins, outs, scr)
        if rider:
            pl.when(last)(lambda: rider.wait(*r_refs))

    aliases = dict(aliases or {})
    if rider:
        sem = ("arbitrary",) * len(grid)
        aliases.update(rider.aliases(n_in, n_out))
    return pl.pallas_call(
        body, name=name, grid=grid, in_specs=list(in_specs) + [_ANY] * n_rin,
        out_specs=list(out_specs) + [_ANY] * n_rout,
        out_shape=list(out_shape) + (rider.out_shape if rider else []),
        scratch_shapes=list(scratch_shapes) + (rider.scratch if rider else []),
        input_output_aliases=aliases, compiler_params=_cparams(sem),
    )(*args, *(rider.args if rider else []))


def _exchange(direct, *, name):
    n_in = len(direct.args)

    def body(*refs):
        r_refs = refs[:direct.n_src], refs[n_in:n_in + direct.n_buf], refs[n_in + direct.n_buf:]
        direct.start(*r_refs)
        direct.wait(*r_refs)

    return pl.pallas_call(
        body, name=name, in_specs=[_ANY] * n_in, out_specs=[_ANY] * direct.n_buf, out_shape=direct.out_shape,
        input_output_aliases=direct.aliases(0, 0), scratch_shapes=direct.scratch,
    )(*direct.args)


def _matmul(a, b, *, form, tm, tn, tk, name, add=None, add_scale=1.0, rider=None, b_cols=None):
    if form == "nn":
        (m, kk), n = a.shape, b.shape[1]
        a_spec = pl.BlockSpec((tm, tk), lambda i, j, k: (i, k))
        b_spec = pl.BlockSpec((tk, tn), lambda i, j, k: (k, j))
        dn = (((1,), (0,)), ((), ()))
    elif form == "nt":
        (m, kk), n = a.shape, b.shape[0]
        a_spec = pl.BlockSpec((tm, tk), lambda i, j, k: (i, k))
        b_spec = pl.BlockSpec((tn, tk), lambda i, j, k: (j, k))
        dn = (((1,), (1,)), ((), ()))
    else:
        kk, m = a.shape
        n0, n = b_cols or (0, b.shape[1])
        assert n0 % tn == 0
        a_spec = pl.BlockSpec((tk, tm), lambda i, j, k: (k, i))
        b_spec = pl.BlockSpec((tk, tn), lambda i, j, k: (k, j + n0 // tn))
        dn = (((0,), (0,)), ((), ()))
    assert m % tm == 0 and n % tn == 0 and kk % tk == 0, (name, m, n, kk)
    has_add = add is not None

    def core(ins, outs, _):
        a_ref, b_ref = ins[:2]
        o_ref = outs[0]
        k = pl.program_id(2)
        p = lax.dot_general(a_ref[...].astype(MM_DTYPE), b_ref[...].astype(MM_DTYPE), dn,
                            preferred_element_type=F32)

        @pl.when(k == 0)
        def _():
            o_ref[...] = p + add_scale * ins[2][...] if has_add else p

        @pl.when(k > 0)
        def _():
            o_ref[...] += p

    in_specs = [a_spec, b_spec]
    args = [a, b]
    if has_add:
        in_specs.append(pl.BlockSpec((tm, tn), lambda i, j, k: (i, j)))
        args.append(add)
    res = _pcall(core, name=name, grid=(m // tm, n // tn, kk // tk), in_specs=in_specs,
                 out_specs=[pl.BlockSpec((tm, tn), lambda i, j, k: (i, j))],
                 out_shape=[jax.ShapeDtypeStruct((m, n), F32)], args=args,
                 sem=("parallel", "parallel", "arbitrary"), rider=rider)
    return res if rider else res[0]


ZERO_TAIL = 8


def _with_tail(x):
    return jnp.concatenate([x, jnp.zeros((ZERO_TAIL,) + x.shape[1:], x.dtype)], axis=0)


def _shift_down(x, k):
    return pltpu.roll(x, k, 0)


def _shift_up(x, k):
    return pltpu.roll(x, x.shape[0] - k, 0)


def _conv_slab(x, w):
    return w[3:4] * x + w[2:3] * _shift_down(x, 1) + w[1:2] * _shift_down(x, 2) + w[0:1] * _shift_down(x, 3)


def _prep_fwd(h, conv_w, *, name):
    t_len = h.shape[0]

    def body(x_ref, w_ref, o_ref):
        s = pl.program_id(0)
        y = _silu(_conv_slab(_with_tail(x_ref[...]), w_ref[...])[:t_len])
        rs = lax.rsqrt(jnp.sum(y * y, axis=-1, keepdims=True) + L2_EPS)
        scale = jnp.where(s < A_HEADS, A_HEAD_DIM ** -0.5, 1.0)
        o_ref[...] = jnp.where(s < 2 * A_HEADS, y * rs * scale, y)

    return pl.pallas_call(
        body, name=name, grid=(12,),
        in_specs=[pl.BlockSpec((t_len, LANE), lambda s: (0, L_QKV // LANE + s)),
                  pl.BlockSpec((8, LANE), lambda s: (0, s))],
        out_specs=pl.BlockSpec((t_len, LANE), lambda s: (0, s)),
        out_shape=jax.ShapeDtypeStruct((t_len, 3 * A_WIDTH), F32),
        compiler_params=_cparams(("parallel",)),
    )(h, conv_w)


def _prep_bwd(h, conv_w, d_out, dh, *, name):
    t_len = h.shape[0]

    def body(x_ref, w_ref, g_ref, dh_in, dx_ref, dw_ref):
        del dh_in
        s = pl.program_id(0)
        x = _with_tail(x_ref[...])
        g = _with_tail(g_ref[0])
        w = w_ref[...]
        xs = [_shift_down(x, 3), _shift_down(x, 2), _shift_down(x, 1), x]
        c = w[0:1] * xs[0] + w[1:2] * xs[1] + w[2:3] * xs[2] + w[3:4] * xs[3]
        sg = jax.nn.sigmoid(c)
        y = c * sg
        rs = lax.rsqrt(jnp.sum(y * y, axis=-1, keepdims=True) + L2_EPS)
        scale = jnp.where(s < A_HEADS, A_HEAD_DIM ** -0.5, 1.0)
        dy_n = scale * (rs * g - y * (rs * rs * rs) * jnp.sum(g * y, axis=-1, keepdims=True))
        dy = jnp.where(s < 2 * A_HEADS, dy_n, g)
        dc = dy * (sg * (1.0 + c * (1.0 - sg)))
        dx = w[3:4] * dc + w[2:3] * _shift_up(dc, 1) + w[1:2] * _shift_up(dc, 2) + w[0:1] * _shift_up(dc, 3)
        dx_ref[...] = dx[:t_len].astype(dx_ref.dtype)
        dws = [jnp.sum(dc * xs[j], axis=0, keepdims=True) for j in range(CONV_K)]
        dw_ref[...] = jnp.concatenate(dws + [jnp.zeros((8 - CONV_K, LANE), F32)], axis=0)

    slab = pl.BlockSpec((t_len, LANE), lambda s: (0, L_QKV // LANE + s))
    return pl.pallas_call(
        body, name=name, grid=(12,),
        in_specs=[slab, pl.BlockSpec((8, LANE), lambda s: (0, s)),
                  pl.BlockSpec((1, t_len, LANE), lambda s: (s // A_HEADS, 0, s % A_HEADS)), _ANY],
        out_specs=[slab, pl.BlockSpec((8, LANE), lambda s: (0, s))],
        out_shape=[jax.ShapeDtypeStruct((t_len, L_COLS), MM_DTYPE), jax.ShapeDtypeStruct((8, 3 * A_WIDTH), F32)],
        input_output_aliases={3: 0},
        compiler_params=_cparams(("parallel",)),
    )(h, conv_w, d_out, dh)


N_LEVELS = 5
MF_TRIL, MF_STRIL, MF_DIAG8, MF_LOW16, MF_EYE = 0, 1, 2, 3, 3 + N_LEVELS
MB_CUM, MB_CUM_T, MB_TOT = 0, 1, 2


def _gdn_masks():
    r = lax.broadcasted_iota(jnp.int32, (SUPER, SUPER), 0)
    c = lax.broadcasted_iota(jnp.int32, (SUPER, SUPER), 1)
    same = lambda shift: (r >> shift) == (c >> shift)
    ninf = lambda m: jnp.where(m, 0.0, -jnp.inf).astype(F32)
    one = lambda m: m.astype(F32)
    mf = jnp.stack([ninf(r >= c), ninf(r > c), one(same(3))]
                   + [one(same(4 + lv) & jnp.logical_not(same(3 + lv))) for lv in range(N_LEVELS)] + [one(r == c)])
    mb = jnp.stack([one(r >= c), one(r <= c), jnp.ones((SUPER, SUPER), F32)]).astype(BF16)
    return mf, mb


def _tri_inv_impl(a, mf):
    d = lambda p, q: jnp.dot(p.astype(BF16), q.astype(BF16), preferred_element_type=F32)
    dd = lambda p, q: jnp.dot(p, q, preferred_element_type=F32)
    eye = mf[MF_EYE]
    a0 = a * mf[MF_DIAG8]
    a2 = d(a0, a0)
    a4 = d(a2, a2)
    t = d(d(eye - a0, eye + a2), eye + a4)
    for level in range(N_LEVELS):
        t = t - d(d(t, a * mf[MF_LOW16 + level]), t)
    a_hi, a_lo = _split(a)
    for _ in range(NEWTON_STEPS):
        t0 = t.astype(BF16)
        t0f = t0.astype(F32)
        resid = (eye - t0f) - (dd(a_hi, t0) + dd(a_lo, t0))
        r_hi, r_lo = _split(resid)
        t = t0f + (dd(t0, r_hi) + dd(t0, r_lo))
    return t


@jax.custom_vjp
def _wy_apply(a, rhs, t):
    return _mm(t, rhs)


def _wy_apply_fwd(a, rhs, t):
    x = _mm(t, rhs)
    return x, (t, x)


def _wy_apply_bwd(res, dx):
    t, x = res
    d_rhs = _mm_tn(t, dx)
    return -_mm_nt(d_rhs, x), d_rhs, jnp.zeros_like(t)


_wy_apply.defvjp(_wy_apply_fwd, _wy_apply_bwd)


@functools.partial(jax.custom_vjp, nondiff_argnums=(1,))
def _lane_roll(x, shift):
    return pltpu.roll(x, shift % LANE, 1)


_lane_roll.defvjp(lambda x, shift: (_lane_roll(x, shift), None), lambda shift, _, g: (_lane_roll(g, -shift),))


def _mask_times_lanes(x, mask):
    lane = lax.broadcasted_iota(jnp.int32, (1, LANE), 1)
    x = jnp.where(lane < A_HEADS, x, 0.0)
    x1 = x.astype(BF16).astype(F32)
    x2 = (x - x1).astype(BF16).astype(F32)
    x3 = (x - x1 - x2).astype(BF16).astype(F32)
    pieces = x1 + pltpu.roll(x2, A_HEADS, 1) + pltpu.roll(x3, 2 * A_HEADS, 1)
    res = jnp.dot(mask, pieces.astype(BF16), preferred_element_type=F32)
    return res + pltpu.roll(res, LANE - A_HEADS, 1) + pltpu.roll(res, LANE - 2 * A_HEADS, 1)


@jax.custom_vjp
def _chunk_sums(g, mb):
    return _mask_times_lanes(g, mb[MB_CUM]), _mask_times_lanes(g, mb[MB_TOT])


def _chunk_sums_fwd(g, mb):
    return _chunk_sums(g, mb), mb


def _chunk_sums_bwd(mb, d):
    lane = lax.broadcasted_iota(jnp.int32, (1, LANE), 1)
    dg = _mask_times_lanes(d[0], mb[MB_CUM_T]) + _mask_times_lanes(d[1], mb[MB_TOT])
    return jnp.where(lane < A_HEADS, dg, 0.0), jnp.zeros_like(mb)


_chunk_sums.defvjp(_chunk_sums_fwd, _chunk_sums_bwd)


def _gdn_gates(ba, alog, dtb, mb):
    beta = jax.nn.sigmoid(ba)
    g = -jnp.exp(alog) * _softplus(_lane_roll(ba, -A_HEADS) + dtb)
    gc, gl = _chunk_sums(g, mb)
    return beta, gc, gl, gc.T


def _gdn_block(s, q, k, v, z, gates, nw, h, t_known, mf):
    n = q.shape[0]
    beta_all, gc_all, gl_all, gct_all = gates
    lane = lax.broadcasted_iota(jnp.int32, (1, LANE), 1)
    sub = lax.broadcasted_iota(jnp.int32, (LANE, 1), 0)
    col = lambda x: jnp.sum(jnp.where(lane == h, x, 0.0), axis=1, keepdims=True)
    wide = lambda c: jnp.broadcast_to(c, (n, LANE))
    gc, gl = col(gc_all), col(gl_all)
    gc_row = jnp.sum(jnp.where(sub == h, gct_all, 0.0), axis=0, keepdims=True)
    beta_w, eg_w = wide(col(beta_all)), wide(jnp.exp(gc))
    diff = gc - gc_row
    decay = jnp.exp(diff + mf[MF_TRIL])
    kb = k * beta_w
    a_mat = _mm_nt(kb, k) * jnp.exp(diff + mf[MF_STRIL])
    rhs = jnp.concatenate([v * beta_w, kb * eg_w], axis=1)
    if t_known is None:
        t_mat = _tri_inv_impl(a_mat, mf)
        uw = _mm(t_mat, rhs)
    else:
        t_mat = t_known
        uw = _wy_apply(a_mat, rhs, t_known)
    u, w = uw[:, :LANE], uw[:, LANE:]
    qk = _mm_nt(q, k) * decay
    q_dec = q * eg_w
    k_dec = k * wide(jnp.exp(gl - gc))
    v_new = u - _mm(w, s)
    o = _mm(q_dec, s) + _mm(qk, v_new)
    s = s * jnp.exp(gl[0:1]) + _mm_tn(k_dec, v_new)
    o = o * lax.rsqrt(jnp.mean(o * o, axis=-1, keepdims=True) + RMS_EPS) * nw
    return o * _silu(z), s, t_mat


def _gdn_fwd(qkv, h, alog, dtb, nw, ycat, *, name, rider=None):
    t_len = qkv.shape[0]
    nsc = t_len // SUPER

    def core(ins, outs, scr):
        q_ref, k_ref, v_ref, gate_ref, al_ref, dt_ref, nw_ref, mf_ref, mb_ref, _ = ins
        y_ref, sin_ref, t_ref = outs
        s_scr, = scr

        @pl.when(pl.program_id(0) == 0)
        def _():
            s_scr[...] = jnp.zeros_like(s_scr)

        per_head = lambda ref: jnp.stack([ref[:, hh * LANE:(hh + 1) * LANE] for hh in range(A_HEADS)])
        states = s_scr[...]
        gates = _gdn_gates(gate_ref[:, A_WIDTH:], al_ref[...], dt_ref[...], mb_ref[...])
        fn = jax.vmap(_gdn_block, in_axes=(0, 0, 0, 0, 0, None, None, 0, None, None))
        y, s_new, t_mat = fn(states, per_head(q_ref), per_head(k_ref), per_head(v_ref), per_head(gate_ref),
                             gates, nw_ref[...], jnp.arange(A_HEADS), None, mf_ref[...])
        sin_ref[0] = states
        t_ref[0] = t_mat
        s_scr[...] = s_new
        for hh in range(A_HEADS):
            y_ref[:, hh * LANE:(hh + 1) * LANE] = y[hh].astype(y_ref.dtype)

    blk = lambda j: pl.BlockSpec((SUPER, A_WIDTH), lambda sc: (sc, j))
    row = pl.BlockSpec((1, LANE), lambda sc: (0, 0))
    mf, mb = _gdn_masks()
    whole = lambda a: pl.BlockSpec(a.shape, lambda sc: (0, 0, 0))
    return _pcall(
        core, name=name, grid=(nsc,),
        in_specs=[blk(0), blk(1), blk(2), pl.BlockSpec((SUPER, L_GATE), lambda sc: (sc, L_ZA // L_GATE)),
                  row, row, row, whole(mf), whole(mb), _ANY],
        out_specs=[blk(0),
                   pl.BlockSpec((1, A_HEADS, A_HEAD_DIM, A_HEAD_DIM), lambda sc: (sc, 0, 0, 0)),
                   pl.BlockSpec((1, A_HEADS, SUPER, SUPER), lambda sc: (sc, 0, 0, 0))],
        out_shape=[jax.ShapeDtypeStruct((t_len, D_MODEL), MM_DTYPE),
                   jax.ShapeDtypeStruct((nsc, A_HEADS, A_HEAD_DIM, A_HEAD_DIM), F32),
                   jax.ShapeDtypeStruct((nsc, A_HEADS, SUPER, SUPER), F32)],
        scratch_shapes=[pltpu.VMEM((A_HEADS, A_HEAD_DIM, A_HEAD_DIM), F32)],
        aliases={9: 0}, sem=("arbitrary",), rider=rider,
        args=(qkv, qkv, qkv, h, alog, dtb, nw, mf, mb, ycat))


def _gdn_bwd(qkv, h, alog, dtb, nw, s_in, t_in, dycat, dh, *, name, rider=None):
    t_len = qkv.shape[0]
    nsc = t_len // SUPER

    def core(ins, outs, scr):
        q_ref, k_ref, v_ref, gate_ref, al_ref, dt_ref, nw_ref, sin_ref, t_ref, dy_ref, mf_ref, mb_ref, _ = ins
        dgate_ref, dqkv_ref, dal_ref, ddt_ref, dnw_ref = outs
        ds_scr, = scr

        @pl.when(pl.program_id(0) == 0)
        def _():
            ds_scr[...] = jnp.zeros_like(ds_scr)
            dal_ref[...] = jnp.zeros_like(dal_ref)
            ddt_ref[...] = jnp.zeros_like(ddt_ref)
            dnw_ref[...] = jnp.zeros_like(dnw_ref)

        per_head = lambda ref: jnp.stack([ref[:, hh * LANE:(hh + 1) * LANE] for hh in range(A_HEADS)])
        head_ids = jnp.arange(A_HEADS)
        t_known, mf, mb = t_ref[0], mf_ref[...], mb_ref[...]

        def fn(s, q, k, v, z, ba, alog, dtb, nw):
            gates = _gdn_gates(ba, alog, dtb, mb)
            one = lambda s, q, k, v, z, t, h: _gdn_block(s, q, k, v, z, gates, nw, h, t, mf)[:2]
            return jax.vmap(one)(s, q, k, v, z, t_known, head_ids)

        _, vjp = jax.vjp(fn, sin_ref[0], per_head(q_ref), per_head(k_ref), per_head(v_ref), per_head(gate_ref),
                         gate_ref[:, A_WIDTH:], al_ref[...], dt_ref[...], nw_ref[...])
        ds, dq, dk, dv, dz, dba, dal, ddt, dnw = vjp((per_head(dy_ref), ds_scr[...]))
        ds_scr[...] = ds
        for hh in range(A_HEADS):
            cols = slice(hh * LANE, (hh + 1) * LANE)
            dqkv_ref[0, :, cols] = dq[hh]
            dqkv_ref[1, :, cols] = dk[hh]
            dqkv_ref[2, :, cols] = dv[hh]
            dgate_ref[:, cols] = dz[hh].astype(dgate_ref.dtype)
        dgate_ref[:, A_WIDTH:] = dba.astype(dgate_ref.dtype)
        dal_ref[...] += dal
        ddt_ref[...] += ddt
        dnw_ref[...] += dnw

    rev = lambda i: nsc - 1 - i
    blk = lambda j: pl.BlockSpec((SUPER, A_WIDTH), lambda i: (rev(i), j))
    gate = pl.BlockSpec((SUPER, L_GATE), lambda i: (rev(i), L_ZA // L_GATE))
    row = pl.BlockSpec((1, LANE), lambda i: (0, 0))
    mf, mb = _gdn_masks()
    whole = lambda a: pl.BlockSpec(a.shape, lambda i: (0, 0, 0))
    return _pcall(
        core, name=name, grid=(nsc,),
        in_specs=[blk(0), blk(1), blk(2), gate, row, row, row,
                  pl.BlockSpec((1, A_HEADS, A_HEAD_DIM, A_HEAD_DIM), lambda i: (rev(i), 0, 0, 0)),
                  pl.BlockSpec((1, A_HEADS, SUPER, SUPER), lambda i: (rev(i), 0, 0, 0)),
                  blk(0), whole(mf), whole(mb), _ANY],
        out_specs=[gate, pl.BlockSpec((3, SUPER, A_WIDTH), lambda i: (0, rev(i), 0)), row, row, row],
        out_shape=[jax.ShapeDtypeStruct((t_len, L_COLS), MM_DTYPE), jax.ShapeDtypeStruct((3, t_len, A_WIDTH), F32)]
        + [jax.ShapeDtypeStruct((1, LANE), F32)] * 3,
        scratch_shapes=[pltpu.VMEM((A_HEADS, A_HEAD_DIM, A_HEAD_DIM), F32)],
        aliases={12: 0}, sem=("arbitrary",), rider=rider,
        args=(qkv, qkv, qkv, h, alog, dtb, nw, s_in, t_in, dycat, mf, mb, dh))


Q_BLOCKS = 4
Q_ROWS = Q_BLOCKS * BLOCK


def _swa_block(q, kp, kc, vp, vc, z, sinks, first):
    rows = B_GROUP * BLOCK
    ri = lax.broadcasted_iota(jnp.int32, (rows, 2 * BLOCK), 0)
    si = lax.broadcasted_iota(jnp.int32, (rows, 2 * BLOCK), 1)
    dist = (ri & (BLOCK - 1)) + BLOCK - si
    bias = jnp.where((dist >= 0) & (dist < WINDOW), 0.0, -jnp.inf)
    no_prev = jnp.where(first & (si[:1] < BLOCK), -jnp.inf, 0.0)
    dist_f = dist.astype(F32)
    head_of_row = lax.broadcasted_iota(jnp.int32, (rows, 1), 0) >> 7
    keys = jnp.concatenate([kp, kc], axis=0)
    vals = jnp.concatenate([vp, vc], axis=0)

    def item(b, j):
        cs = slice(j * B_HEAD_DIM, (j + 1) * B_HEAD_DIM)
        rs = slice(b * BLOCK, (b + 1) * BLOCK)
        heads = range(j * B_GROUP, (j + 1) * B_GROUP)
        qs = jnp.concatenate([q[rs, hq * B_HEAD_DIM:(hq + 1) * B_HEAD_DIM] for hq in heads], axis=0) * (
            B_HEAD_DIM ** -0.5)
        kk = keys[b * BLOCK:(b + 2) * BLOCK, cs]
        vv = vals[b * BLOCK:(b + 2) * BLOCK, cs]
        sink = jnp.concatenate([jnp.broadcast_to(sinks[:, hq:hq + 1], (BLOCK, 1)) for hq in heads], axis=0)
        slope = sum(jnp.where(head_of_row == gi, 2.0 ** (-8.0 * (hq + 1) / B_Q_HEADS), 0.0)
                    for gi, hq in enumerate(heads))
        return qs, kk, vv, sink, slope, (no_prev if b == 0 else jnp.zeros_like(no_prev))

    def attend(qs, kk, vv, sink, slope, hide):
        sc = _mm_nt(qs, kk) - slope * dist_f + (bias + hide)
        m = lax.stop_gradient(jnp.maximum(jnp.max(sc, axis=-1, keepdims=True), sink))
        p = jnp.exp(sc - m)
        inv = 1.0 / (jnp.sum(p, axis=-1, keepdims=True) + jnp.exp(sink - m))
        return _mm(p * inv, vv)

    items = [(b, j) for b in range(Q_BLOCKS) for j in range(B_KV_HEADS)]
    o = jax.vmap(attend)(*[_stack(t) for t in zip(*[item(b, j) for b, j in items])])
    rows_out = [jnp.concatenate([o[b * B_KV_HEADS + j, gi * BLOCK:(gi + 1) * BLOCK]
                                 for j in range(B_KV_HEADS) for gi in range(B_GROUP)], axis=1)
                for b in range(Q_BLOCKS)]
    return jnp.concatenate(rows_out, axis=0) * _silu(z)


def _swa_specs(idx):
    wide = lambda off: pl.BlockSpec((Q_ROWS, B_WIDTH), lambda n: (idx(n), off))
    cur = lambda off: pl.BlockSpec((Q_ROWS, LANE), lambda n: (idx(n), off))
    prev = lambda off: pl.BlockSpec((BLOCK, LANE), lambda n: (jnp.maximum(idx(n) * Q_BLOCKS - 1, 0), off))
    return [wide(L_QB // B_WIDTH), prev(L_KB // LANE), cur(L_KB // LANE), prev(L_VB // LANE), cur(L_VB // LANE),
            wide(L_ZB // B_WIDTH), pl.BlockSpec((1, LANE), lambda n: (0, 0))]


def _swa_fwd(h, sinks, *, name, rider=None):
    t_len = h.shape[0]
    nb = t_len // Q_ROWS

    def core(ins, outs, _):
        q_ref, kp_ref, kc_ref, vp_ref, vc_ref, z_ref, s_ref = ins
        outs[0][...] = _swa_block(q_ref[...], kp_ref[...], kc_ref[...], vp_ref[...], vc_ref[...], z_ref[...],
                                  s_ref[...], pl.program_id(0) == 0).astype(outs[0].dtype)

    res = _pcall(core, name=name, grid=(nb,), in_specs=_swa_specs(lambda n: n),
                 out_specs=[pl.BlockSpec((Q_ROWS, B_WIDTH), lambda n: (n, 1))],
                 out_shape=[jax.ShapeDtypeStruct((t_len, D_MODEL), MM_DTYPE)], sem=("parallel",), rider=rider,
                 args=(h, h, h, h, h, h, sinks))
    return res if rider else res[0]


def _swa_bwd(h, sinks, dycat, *, name, rider=None):
    t_len = h.shape[0]
    nb = t_len // Q_ROWS
    early = slice(0, Q_ROWS - BLOCK)
    last = slice(Q_ROWS - BLOCK, Q_ROWS)

    def core(ins, outs, scr):
        q_ref, kp_ref, kc_ref, vp_ref, vc_ref, z_ref, s_ref, dy_ref = ins
        dh_ref, dsk_ref = outs
        ck_scr, cv_scr = scr
        i = pl.program_id(0)
        n = nb - 1 - i

        @pl.when(i == 0)
        def _():
            ck_scr[...] = jnp.zeros_like(ck_scr)
            cv_scr[...] = jnp.zeros_like(cv_scr)
            dsk_ref[...] = jnp.zeros_like(dsk_ref)

        fn = functools.partial(_swa_block, first=(n == 0))
        _, vjp = jax.vjp(fn, q_ref[...], kp_ref[...], kc_ref[...], vp_ref[...], vc_ref[...], z_ref[...], s_ref[...])
        dq, dkp, dkc, dvp, dvc, dz, dsk = vjp(dy_ref[...])
        def put(rows, col, val):
            dh_ref[rows, col:col + val.shape[1]] = val.astype(dh_ref.dtype)

        put(slice(None), L_QB, dq)
        put(slice(None), L_ZB, dz)
        put(early, L_KB, dkc[early])
        put(early, L_VB, dvc[early])
        put(last, L_KB, dkc[last] + ck_scr[...])
        put(last, L_VB, dvc[last] + cv_scr[...])
        ck_scr[...] = dkp
        cv_scr[...] = dvp
        dsk_ref[...] += dsk

    rev = lambda i: nb - 1 - i
    return _pcall(
        core, name=name, grid=(nb,),
        in_specs=_swa_specs(rev) + [pl.BlockSpec((Q_ROWS, B_WIDTH), lambda i: (rev(i), 1))],
        out_specs=[pl.BlockSpec((Q_ROWS, L_SWA), lambda i: (rev(i), 0)), pl.BlockSpec((1, LANE), lambda i: (0, 0))],
        out_shape=[jax.ShapeDtypeStruct((t_len, L_COLS), MM_DTYPE), jax.ShapeDtypeStruct((1, LANE), F32)],
        scratch_shapes=[pltpu.VMEM((BLOCK, LANE), F32), pltpu.VMEM((BLOCK, LANE), F32)],
        sem=("arbitrary",), rider=rider, args=(h, h, h, h, h, h, sinks, dycat))


def _out_ln_fwd(ycat, w_out, x, ln_g, ln_b, *, name, tm=512, w_in_next=None):
    t_len = x.shape[0]
    last = w_in_next is None

    def body(y_ref, w_ref, x_ref, g_ref, b_ref, *rest):
        r = DEEPNORM_ALPHA * x_ref[...] + _mm(y_ref[...], w_ref[...])
        if last:
            rest[0][...] = r
            return
        win_ref, r_ref, o_ref, h_ref = rest
        r_ref[...] = r
        mu = jnp.mean(r, axis=-1, keepdims=True)
        d = r - mu
        var = jnp.mean(d * d, axis=-1, keepdims=True)
        xn = d * lax.rsqrt(var + LN_EPS) * g_ref[...] + b_ref[...]
        o_ref[...] = xn
        h_ref[...] = _mm_nt(xn, win_ref[...])

    tile = pl.BlockSpec((tm, D_MODEL), lambda i: (i, 0))
    vec = pl.BlockSpec((1, D_MODEL), lambda i: (0, 0))
    tile_shape = jax.ShapeDtypeStruct((t_len, D_MODEL), F32)
    in_specs = [tile, pl.BlockSpec((D_MODEL, D_MODEL), lambda i: (0, 0)), tile, vec, vec]
    if last:
        args, out_specs, out_shape = (), [tile], [tile_shape]
    else:
        args = (w_in_next,)
        in_specs.append(pl.BlockSpec((L_COLS, D_MODEL), lambda i: (0, 0)))
        out_specs = [tile, tile, pl.BlockSpec((tm, L_COLS), lambda i: (i, 0))]
        out_shape = [tile_shape, tile_shape, jax.ShapeDtypeStruct((t_len, L_COLS), F32)]
    res = pl.pallas_call(
        body, name=name, grid=(t_len // tm,), in_specs=in_specs, out_specs=out_specs, out_shape=out_shape,
        compiler_params=_cparams(("parallel",)),
    )(ycat, w_out, x, ln_g, ln_b, *args)
    return (res[0], None, None) if last else res


def _ln_out_bwd(dxn, r, ln_g, ycat, w_out, *, name, tm=512, loss=None):
    t_len = r.shape[0]
    above = isinstance(dxn, tuple)
    n_lead = 4 if loss else 5 if above else 3

    def body(*refs):
        lead, (y_ref, w_ref), outs = refs[:n_lead], refs[n_lead:n_lead + 2], refs[n_lead + 2:]
        if loss:
            t_ref, r_ref, g_ref, b_ref = lead
            dr_ref, dg_ref, db_ref, l_ref, dy_ref, dw_ref = outs
        else:
            *dx_refs, r_ref, g_ref = lead
            dr_ref, dg_ref, db_ref, dy_ref, dw_ref = outs

        @pl.when(pl.program_id(0) == 0)
        def _():
            dg_ref[...] = jnp.zeros_like(dg_ref)
            db_ref[...] = jnp.zeros_like(db_ref)
            dw_ref[...] = jnp.zeros_like(dw_ref)
            if loss:
                l_ref[...] = jnp.zeros_like(l_ref)

        rr = r_ref[...]
        mu = jnp.mean(rr, axis=-1, keepdims=True)
        d = rr - mu
        rstd = lax.rsqrt(jnp.mean(d * d, axis=-1, keepdims=True) + LN_EPS)
        xh = d * rstd
        if loss:
            e = (xh * g_ref[...] + b_ref[...]) - t_ref[...]
            dx = e * (1.0 / D_MODEL)
            l_ref[...] += jnp.sum(e * e, axis=0, keepdims=True)
        elif above:
            dh_ref, win_ref, add_ref = dx_refs
            dx = _mm(dh_ref[...], win_ref[...]) + DEEPNORM_ALPHA * add_ref[...]
        else:
            dx = dx_refs[0][...]
        dxh = dx * g_ref[...]
        dr = rstd * (dxh - jnp.mean(dxh, axis=-1, keepdims=True) - xh * jnp.mean(dxh * xh, axis=-1, keepdims=True))
        dr_ref[...] = dr
        dg_ref[...] += jnp.sum(dx * xh, axis=0, keepdims=True)
        db_ref[...] += jnp.sum(dx, axis=0, keepdims=True)
        dy_ref[...] = _mm_nt(dr, w_ref[...])
        dw_ref[...] += _mm_tn(y_ref[...], dr)

    tile = pl.BlockSpec((tm, D_MODEL), lambda i: (i, 0))
    vec = pl.BlockSpec((1, D_MODEL), lambda i: (0, 0))
    square = pl.BlockSpec((D_MODEL, D_MODEL), lambda i: (0, 0))
    tile_shape = jax.ShapeDtypeStruct((t_len, D_MODEL), F32)
    vec_shape = jax.ShapeDtypeStruct((1, D_MODEL), F32)
    if loss:
        args, lead_specs = (loss[0], r, ln_g, loss[1]), [tile, tile, vec, vec]
    elif above:
        args = (*dxn, r, ln_g)
        lead_specs = [pl.BlockSpec((tm, L_COLS), lambda i: (i, 0)), pl.BlockSpec((L_COLS, D_MODEL), lambda i: (0, 0)),
                      tile, tile, vec]
    else:
        args, lead_specs = (dxn, r, ln_g), [tile, tile, vec]
    return pl.pallas_call(
        body, name=name, grid=(t_len // tm,),
        in_specs=lead_specs + [tile, square],
        out_specs=[tile, vec, vec] + ([vec] if loss else []) + [tile, square],
        out_shape=[tile_shape, vec_shape, vec_shape] + ([vec_shape] if loss else [])
        + [tile_shape, jax.ShapeDtypeStruct((D_MODEL, D_MODEL), F32)],
        compiler_params=_cparams(("arbitrary",)),
    )(*args, ycat, w_out)


def _pad_row(v):
    return jnp.zeros((1, LANE), F32).at[0, :v.shape[0]].set(v)


_REGIONS = ((0, 1536, L_QKV), (1536, 2048, L_ZA), (2048, 2056, L_BA), (2056, 2568, L_QB), (2568, 2696, L_KB),
            (2696, 2824, L_VB), (2824, 3336, L_ZB))


def _shard_pieces(regions):
    for a, b, off in regions:
        for d in range(N_DEV):
            lo, hi = max(a, d * SHARD_COLS), min(b, (d + 1) * SHARD_COLS)
            if lo < hi:
                yield d, lo - d * SHARD_COLS, hi - d * SHARD_COLS, off + lo - a


def _as_list(r):
    return list(r) if isinstance(r, (list, tuple)) else [r]


def _gathered(shard):
    return jax.ShapeDtypeStruct((N_DEV,) + shard.shape, shard.dtype)


def _full_w_in(g_in, name):
    by_offset = sorted(_shard_pieces(_REGIONS), key=lambda p: p[3])
    tc = 256

    def body(g_ref, o_ref):
        pieces, row = [], 0
        for d, lo, hi, off in by_offset + [(None, 0, 0, L_COLS)]:
            if off > row:
                pieces.append(jnp.zeros((off - row, tc), g_ref.dtype))
            if d is not None:
                pieces.append(g_ref[d, lo:hi, :])
            row = off + hi - lo
        o_ref[...] = jnp.concatenate(pieces, axis=0)

    return pl.pallas_call(
        body, name=name, grid=(D_MODEL // tc,),
        in_specs=[pl.BlockSpec((N_DEV, SHARD_COLS, tc), lambda i: (0, 0, i))],
        out_specs=pl.BlockSpec((L_COLS, tc), lambda i: (0, i)),
        out_shape=jax.ShapeDtypeStruct((L_COLS, D_MODEL), g_in.dtype),
        compiler_params=_cparams(("parallel",)),
    )(g_in)


def _full_conv(g_conv):
    return jnp.pad(g_conv.transpose(1, 0, 2).reshape(CONV_K, 3 * A_WIDTH), ((0, 8 - CONV_K), (0, 0)))


def _forward(x, weights, shards, small):
    a_log, dt_bias, norm_w, sinks, ln_g, ln_b = small
    tm = min(512, x.shape[0])
    saved, weights = [], [list(w) for w in weights]
    whole = lambda arrs: _Direct([(a, False, j, ()) for j, a in enumerate(arrs)], [_gathered(a) for a in arrs])
    h = None
    for l in range(DEPTH):
        if h is None:
            rider = whole(shards[l][1:]) if weights[l][1] is None else None
            h, *got = _as_list(_matmul(x, weights[l][0], form="nt", tm=tm, tn=L_COLS, tk=D_MODEL,
                                       name=f"in_proj_{l}", rider=rider))
            if rider:
                weights[l][1:] = [got[0].reshape(D_MODEL, D_MODEL), _full_conv(got[1])]
        w_in_l, w_out_l, conv_l = weights[l]
        qkv = _prep_fwd(h, conv_l, name=f"prep_fwd_{l}")
        al, dt, nw, sk = _pad_row(a_log[l]), _pad_row(dt_bias[l]), norm_w[l][None, :], _pad_row(sinks[l])
        ahead = l + 1 < DEPTH and weights[l + 1][0] is None
        rider = whole(shards[l + 1][1:]) if ahead else None
        ycat, *got = _as_list(_swa_fwd(h, sk, name=f"swa_fwd_{l}", rider=rider))
        if ahead:
            weights[l + 1][1:] = [got[0].reshape(D_MODEL, D_MODEL), _full_conv(got[1])]
        rider = whole(shards[l + 1][:1]) if ahead else None
        ycat, s_in, t_in, *got = _gdn_fwd(qkv, h, al, dt, nw, ycat, name=f"gdn_fwd_{l}", rider=rider)
        if ahead:
            weights[l + 1][0] = _full_w_in(got[0], f"w_in_rows_{l + 1}")
        r, xn, h_next = _out_ln_fwd(ycat, w_out_l, x, ln_g[l][None, :], ln_b[l][None, :], name=f"out_ln_{l}",
                                    w_in_next=weights[l + 1][0] if l + 1 < DEPTH else None)
        saved.append((x, h, qkv, s_in, t_in, ycat, r, al, dt, nw, sk))
        x, h = xn, h_next
    return x, saved, weights


def _w_in_blocks(g, name):
    cols, tc = g.shape[1], 256
    pieces = list(_shard_pieces(_REGIONS))

    def body(g_ref, o_ref):
        blocks = [[] for _ in range(N_DEV)]
        for d, lo, hi, off in pieces:
            blocks[d].append(g_ref[off:off + hi - lo, :])
        for d in range(N_DEV):
            o_ref[d] = jnp.concatenate(blocks[d], axis=0).astype(BF16)

    return pl.pallas_call(
        body, name=name, grid=(cols // tc,),
        in_specs=[pl.BlockSpec((L_COLS, tc), lambda i: (0, i))],
        out_specs=pl.BlockSpec((N_DEV, SHARD_COLS, tc), lambda i: (0, 0, i)),
        out_shape=jax.ShapeDtypeStruct((N_DEV, SHARD_COLS, cols), BF16),
        compiler_params=_cparams(("parallel",)),
    )(g)


def _small_blocks(g):
    c_conv = g["conv_w"].reshape(CONV_K, N_DEV, CONV_SHARD_COLS).transpose(1, 0, 2)
    c_small = [jnp.broadcast_to(g[n][None], (N_DEV,) + g[n].shape) for n, _ in SMALL_SIZES]
    return _pack_small(c_conv, c_small)


def _contributions(g):
    c_out = g["w_out"].astype(BF16).reshape(N_DEV, OUT_SHARD_ROWS, D_MODEL)
    return _w_in_blocks(g["w_in_rows"], name="w_in_grad_blocks_above"), c_out, _small_blocks(g)


def _backward_layer(l, dx, saved_l, weights_l, ln_g_l, above=None, loss=None):
    x_in, h, qkv, s_in, t_in, ycat, r, al, dt, nw, sk = saved_l
    w_in_l, w_out_l, conv_l = weights_l
    tm = min(512, x_in.shape[0])
    dr, d_lng, d_lnb, *loss_lanes, dycat, d_wout = _ln_out_bwd(dx, r, ln_g_l[None, :], ycat, w_out_l,
                                                               name=f"ln_out_bwd_{l}", tm=tm, loss=loss)
    big = min(1024, x_in.shape[0])
    rider, p_in, p_out, p_small = None, None, None, None
    recv = lambda c: jax.ShapeDtypeStruct((DEPTH,) + c.shape, c.dtype)
    if above:
        c_out = d_wout.astype(BF16).reshape(N_DEV, OUT_SHARD_ROWS, D_MODEL)
        rider = _Direct([(above[1], True, 0, (l + 1,)), (above[2], True, 1, (l + 1,)), (c_out, True, 0, (l,))],
                        [recv(above[1]), recv(above[2])])
    dh, d_sk, *got = _swa_bwd(h, sk, dycat, name=f"swa_bwd_{l}", rider=rider)
    if above:
        p_out, p_small = got
        rider = _Direct([(above[0], True, 0, (l + 1,))], [recv(above[0])])
    dh, dqkv_n, d_al, d_dt, d_nw, *got = _gdn_bwd(qkv, h, al, dt, nw, s_in, t_in, dycat, dh,
                                                  name=f"gdn_bwd_{l}", rider=rider)
    dh, d_conv = _prep_bwd(h, conv_l, dqkv_n, dh, name=f"prep_bwd_{l}")
    grads = dict(w_out=d_wout, conv_w=d_conv[:CONV_K], a_log=d_al[0, :A_HEADS], dt_bias=d_dt[0, :A_HEADS],
                 norm_w=d_nw[0], sinks=d_sk[0, :B_Q_HEADS], ln_g=d_lng[0], ln_b=d_lnb[0])
    dw = functools.partial(_matmul, dh, x_in, form="tn")
    if not above:
        grads["w_in_rows"] = dw(name=f"in_proj_dw_{l}", tm=L_COLS // 3, tn=D_MODEL, tk=min(2048, x_in.shape[0]))
    else:
        p_in, = got
        cut = D_MODEL // 2
        rest = D_MODEL - cut
        first = dw(name=f"in_proj_dw_first_{l}", tm=L_COLS, tn=cut, tk=big, b_cols=(0, cut))
        blocks = _w_in_blocks(first, name=f"w_in_grad_blocks_first_{l}")
        rider = _Direct([(blocks, True, 0, (l,), (slice(None), pl.ds(0, cut)))], [p_in])
        second, p_in = dw(name=f"in_proj_dw_second_{l}", tm=L_COLS, tn=cut, tk=big, b_cols=(cut, rest), rider=rider)
        blocks = _w_in_blocks(second, name=f"w_in_grad_blocks_second_{l}")
        rider = _Direct([(blocks, True, 0, (l,), (slice(None), pl.ds(cut, rest))),
                         (_small_blocks(grads), True, 1, (l,))], [p_in, p_small])
    if l > 0 and not rider:
        return (dh, w_in_l, dr), grads, None, (loss_lanes[0] if loss else None)
    dx, *got = _as_list(_matmul(dh, w_in_l, form="nn", tm=tm, tn=D_MODEL, tk=L_COLS, name=f"in_proj_dx_{l}",
                                add=dr, add_scale=DEEPNORM_ALPHA, rider=rider))
    bufs = (got[0], p_out, got[1]) if above else None
    return dx, grads, bufs, (loss_lanes[0] if loss else None)


def _all_gather(shards, *, name):
    n_arr = len(shards)

    def body(*refs):
        x_refs, out_refs = refs[:n_arr], refs[n_arr:2 * n_arr]
        send_sems, recv_sems, local_sems = refs[2 * n_arr:]
        x, y, c = _me()
        me, sibling = (x, y, c), (x, y, 1 - c)
        chips = [(1 - x, y), (x, 1 - y), (1 - x, 1 - y)]

        def copy(a, k, block, to, src=None):
            dst = out_refs[a].at[_flat_id(block)]
            return _remote(dst if src is None else src, dst, send_sems.at[a, k], recv_sems.at[a, k], to)

        mine = [pltpu.make_async_copy(x_refs[a], out_refs[a].at[_flat_id(me)], local_sems.at[a])
                for a in range(n_arr)]
        for cp in mine:
            cp.start()
        first = []
        for a in range(n_arr):
            first.append(copy(a, 0, me, sibling, src=x_refs[a]))
            first += [copy(a, 1 + j, me, (*chip, c), src=x_refs[a]) for j, chip in enumerate(chips)]
        for cp in first:
            cp.start()
        passed = []
        for j, chip in enumerate(chips):
            for a in range(n_arr):
                copy(a, 1 + j, (*chip, c), me).wait_recv()
                fwd = copy(a, 4 + j, (*chip, c), sibling)
                fwd.start()
                passed.append(fwd)
        for a in range(n_arr):
            copy(a, 0, sibling, me).wait_recv()
            for j, chip in enumerate(chips):
                copy(a, 4 + j, (*chip, 1 - c), me).wait_recv()
        for cp in first + passed:
            cp.wait_send()
        for cp in mine:
            cp.wait()

    return pl.pallas_call(
        body, name=name, in_specs=[_ANY] * n_arr, out_specs=[_ANY] * n_arr,
        out_shape=[jax.ShapeDtypeStruct((N_DEV,) + s.shape, s.dtype) for s in shards],
        scratch_shapes=[pltpu.SemaphoreType.DMA((n_arr, N_DEV - 1)), pltpu.SemaphoreType.DMA((n_arr, N_DEV - 1)),
                        pltpu.SemaphoreType.DMA((n_arr,))],
    )(*shards)


def _adamw(parts, w, m, v, *, tr, name):
    depth, rows, cols = w.shape
    c1 = 1.0 - ADAM_B1 ** ADAM_STEP
    c2 = 1.0 - ADAM_B2 ** ADAM_STEP

    def body(g_ref, w_ref, m_ref, v_ref, go_ref, d_ref, mo_ref, vo_ref):
        g = g_ref[0, 0].astype(F32)
        for s in range(1, N_DEV):
            g = g + g_ref[0, s].astype(F32)
        m_new = ADAM_B1 * m_ref[0] + (1.0 - ADAM_B1) * g
        v_new = ADAM_B2 * v_ref[0] + (1.0 - ADAM_B2) * (g * g)
        go_ref[0] = g
        mo_ref[0] = m_new
        vo_ref[0] = v_new
        d_ref[0] = -ADAM_LR * ((m_new / c1) / (jnp.sqrt(v_new / c2) + ADAM_EPS) + ADAM_WD * w_ref[0])

    tile = pl.BlockSpec((1, tr, cols), lambda l, i: (l, i, 0))
    return pl.pallas_call(
        body, name=name, grid=(depth, rows // tr),
        in_specs=[pl.BlockSpec((1, N_DEV, tr, cols), lambda l, i: (l, 0, i, 0)), tile, tile, tile],
        out_specs=[tile] * 4, out_shape=[jax.ShapeDtypeStruct(w.shape, F32)] * 4,
        compiler_params=_cparams(("parallel", "parallel")),
    )(parts, w, m, v)


def _adamw_w_in(parts, w, m, v, *, name):
    c1 = 1.0 - ADAM_B1 ** ADAM_STEP
    c2 = 1.0 - ADAM_B2 ** ADAM_STEP

    def body(g_ref, w_ref, m_ref, v_ref, go_ref, d_ref, mo_ref, vo_ref):
        gs = []
        for l in range(DEPTH):
            g = g_ref[l, 0].astype(F32)
            for s in range(1, N_DEV):
                g = g + g_ref[l, s].astype(F32)
            gs.append(g)
        g = jnp.stack(gs, axis=1)
        m_new = ADAM_B1 * m_ref[...] + (1.0 - ADAM_B1) * g
        v_new = ADAM_B2 * v_ref[...] + (1.0 - ADAM_B2) * (g * g)
        go_ref[...] = g
        mo_ref[...] = m_new
        vo_ref[...] = v_new
        d_ref[...] = -ADAM_LR * ((m_new / c1) / (jnp.sqrt(v_new / c2) + ADAM_EPS) + ADAM_WD * w_ref[...])

    tile = pl.BlockSpec((SHARD_COLS, DEPTH, LANE), lambda i: (0, 0, i))
    return pl.pallas_call(
        body, name=name, grid=(D_MODEL // LANE,),
        in_specs=[pl.BlockSpec((DEPTH, N_DEV, SHARD_COLS, LANE), lambda i: (0, 0, 0, i)), tile, tile, tile],
        out_specs=[tile] * 4, out_shape=[jax.ShapeDtypeStruct(w.shape, F32)] * 4,
        compiler_params=_cparams(("parallel",)),
    )(parts, w, m, v)


def _pack_small(conv, small):
    lead = conv.shape[:-2]
    flat = jnp.concatenate([conv.reshape(lead + (CS_CONV,))] + list(small), axis=-1)
    pad = CS_ROWS * LANE - flat.shape[-1]
    flat = jnp.concatenate([flat, jnp.zeros(lead + (pad,), F32)], axis=-1)
    return flat.reshape(lead + (CS_ROWS, LANE))


def _unpack_small(p):
    flat = p.reshape(DEPTH, CS_ROWS * LANE)
    conv = flat[:, :CS_CONV].reshape(DEPTH, CONV_K, CONV_SHARD_COLS)
    small, off = [], CS_CONV
    for _, n in SMALL_SIZES:
        small.append(flat[:, off:off + n])
        off += n
    return conv, small


def kernel(x, w_in, conv_w, a_log, dt_bias, norm_w, sinks, w_out, ln_g, ln_b, loss_target, m_w_in, m_conv_w, m_a_log, m_dt_bias, m_norm_w, m_sinks, m_w_out, m_ln_g, m_ln_b, v_w_in, v_conv_w, v_a_log, v_dt_bias, v_norm_w, v_sinks, v_w_out, v_ln_g, v_ln_b):
    small = [a_log, dt_bias, norm_w, sinks, ln_g, ln_b]
    w_t, m_t, v_t = (a.transpose(2, 0, 1) for a in (w_in, m_w_in, v_w_in))
    shards = [[w_t[:, l].astype(BF16), w_out[l].astype(BF16), conv_w[l]] for l in range(DEPTH)]
    g_in0, = _all_gather(shards[0][:1], name="weights_all_gather_0")
    weights = [[_full_w_in(g_in0, "w_in_rows_0"), None, None]] + [[None, None, None]] * (DEPTH - 1)

    _, saved, weights = _forward(x[0], weights, shards, small)
    dx, g1, _, loss_lanes = _backward_layer(1, None, saved[1], weights[1], ln_g[1],
                                            loss=(loss_target[0], ln_b[1][None, :]))
    loss = lax.psum(0.5 * jnp.sum(loss_lanes) * (1.0 / D_MODEL), ("x", "y", "c"))
    dx, _, (p_in, p_out, p_small), _ = _backward_layer(0, dx, saved[0], weights[0], ln_g[0],
                                                       above=_contributions(g1))

    o_in = [o.transpose(1, 2, 0) for o in _adamw_w_in(p_in, w_t, m_t, v_t, name="adamw_w_in")]
    o_out = _adamw(p_out, w_out, m_w_out, v_w_out, tr=OUT_SHARD_ROWS, name="adamw_w_out")
    o_small = _adamw(p_small, _pack_small(conv_w, small),
                     _pack_small(m_conv_w, [m_a_log, m_dt_bias, m_norm_w, m_sinks, m_ln_g, m_ln_b]),
                     _pack_small(v_conv_w, [v_a_log, v_dt_bias, v_norm_w, v_sinks, v_ln_g, v_ln_b]),
                     tr=CS_ROWS, name="adamw_small")
    outs = []
    for k in range(4):
        cv, sm = _unpack_small(o_small[k])
        outs += [o_in[k], cv, sm[0], sm[1], sm[2], sm[3], o_out[k], sm[4], sm[5]]
    return (loss, dx[None], *outs)
```

```python
import functools

import jax
import jax.numpy as jnp
from jax import lax
from jax.experimental import pallas as pl
from jax.experimental.pallas import tpu as pltpu

F32 = jnp.float32
BF16 = jnp.bfloat16
MM_DTYPE = BF16

N_DEV = 8
D_MODEL = 1024
DEPTH = 2
A_HEADS = 4
A_HEAD_DIM = 128
A_WIDTH = 512
CONV_K = 4
SUPER = 256
NEWTON_STEPS = 1
B_Q_HEADS = 8
B_KV_HEADS = 2
B_HEAD_DIM = 64
B_GROUP = 4
B_WIDTH = 512
WINDOW = 128
BLOCK = 128
IN_COLS = 3336
SHARD_COLS = IN_COLS // N_DEV
OUT_SHARD_ROWS = D_MODEL // N_DEV
CONV_SHARD_COLS = 3 * A_WIDTH // N_DEV
DEEPNORM_ALPHA = (2 * DEPTH) ** 0.25
LN_EPS = 1e-5
RMS_EPS = 1e-6
L2_EPS = 1e-6
ADAM_LR, ADAM_B1, ADAM_B2, ADAM_EPS, ADAM_WD, ADAM_STEP = 0.001, 0.9, 0.999, 1e-08, 0.01, 10

LANE = 128
L_QB, L_ZB, L_KB, L_VB, L_ZA, L_BA, L_QKV = 0, 512, 1024, 1152, 1280, 1792, 1920
L_SWA = 1280
L_GATE = 640
L_COLS = 3456
SMALL_SIZES = (("a_log", 4), ("dt_bias", 4), ("norm_w", 128), ("sinks", 8), ("ln_g", 1024), ("ln_b", 1024))
CS_CONV = CONV_K * CONV_SHARD_COLS
CS_ROWS = 24
VMEM_LIMIT = 48 * 1024 * 1024


def _cparams(sem=None):
    return pltpu.CompilerParams(dimension_semantics=sem, vmem_limit_bytes=VMEM_LIMIT)


def _mm(a, b):
    return jnp.dot(a.astype(MM_DTYPE), b.astype(MM_DTYPE), preferred_element_type=F32)


def _mm_nt(a, b):
    return lax.dot_general(a.astype(MM_DTYPE), b.astype(MM_DTYPE), (((1,), (1,)), ((), ())),
                           preferred_element_type=F32)


def _mm_tn(a, b):
    return lax.dot_general(a.astype(MM_DTYPE), b.astype(MM_DTYPE), (((0,), (0,)), ((), ())),
                           preferred_element_type=F32)


def _split(a):
    hi = a.astype(BF16)
    return hi, (a - hi.astype(F32)).astype(BF16)


def _silu(x):
    return x * jax.nn.sigmoid(x)


@jax.custom_vjp
def _stack(parts):
    return jnp.stack(parts)


_stack.defvjp(lambda parts: (jnp.stack(parts), None), lambda _, g: (tuple(g[i] for i in range(g.shape[0])),))


def _softplus(x):
    return jnp.maximum(x, 0.0) + jnp.log1p(jnp.exp(-jnp.abs(x)))


_ANY = pl.BlockSpec(memory_space=pl.ANY)


def _me():
    return lax.axis_index("x"), lax.axis_index("y"), lax.axis_index("c")


def _flat_id(pos):
    return 4 * pos[0] + 2 * pos[1] + pos[2]


def _remote(src, dst, send_sem, recv_sem, to):
    return pltpu.make_async_remote_copy(src_ref=src, dst_ref=dst, send_sem=send_sem, recv_sem=recv_sem,
                                        device_id=to, device_id_type=pl.DeviceIdType.MESH)


class _Direct:
    def __init__(self, items, bufs):
        self.items, self.bufs = list(items), list(bufs)
        self.n_src, self.n_buf = len(self.items), len(self.bufs)
        self.old = [j for j, b in enumerate(self.bufs) if not isinstance(b, jax.ShapeDtypeStruct)]
        self.args = [it[0] for it in self.items] + [self.bufs[j] for j in self.old]
        self.out_shape = [jax.ShapeDtypeStruct(b.shape, b.dtype) for b in self.bufs]
        self.scratch = [pltpu.SemaphoreType.DMA((self.n_src, N_DEV - 1)),
                        pltpu.SemaphoreType.DMA((self.n_src, N_DEV - 1)), pltpu.SemaphoreType.DMA((self.n_src,))]

    def aliases(self, in_base, out_base):
        return {in_base + self.n_src + pos: out_base + j for pos, j in enumerate(self.old)}

    def copies(self, in_refs, out_refs, sems):
        send_sems, recv_sems, local_sems = sems
        x, y, c = _me()
        me = _flat_id((x, y, c))
        peers = [(x ^ ((rel >> 2) & 1), y ^ ((rel >> 1) & 1), c ^ (rel & 1)) for rel in range(1, N_DEV)]
        local, sends, recvs = [], [], []
        for a, (_, per_dest, j, prefix, *rest) in enumerate(self.items):
            src = lambda d: in_refs[a].at[d] if per_dest else in_refs[a]
            dst = lambda s: out_refs[j].at[tuple(prefix) + (s,) + tuple(rest[0] if rest else ())]
            local.append(pltpu.make_async_copy(src(me), dst(me), local_sems.at[a]))
            for k, peer in enumerate(peers):
                pid = _flat_id(peer)
                sends.append(_remote(src(pid), dst(me), send_sems.at[a, k], recv_sems.at[a, k], peer))
                recvs.append(_remote(src(pid), dst(pid), send_sems.at[a, k], recv_sems.at[a, k], peer))
        return local, sends, recvs

    def start(self, in_refs, out_refs, sems):
        local, sends, _ = self.copies(in_refs, out_refs, sems)
        for cp in local + sends:
            cp.start()

    def wait(self, in_refs, out_refs, sems):
        local, sends, recvs = self.copies(in_refs, out_refs, sems)
        for cp in recvs:
            cp.wait_recv()
        for cp in sends:
            cp.wait_send()
        for cp in local:
            cp.wait()


def _pcall(core, *, name, grid, in_specs, out_specs, out_shape, args, sem, scratch_shapes=(), aliases=None,
           rider=None):
    n_in, n_out, n_scr = len(in_specs), len(out_specs), len(scratch_shapes)
    n_rin, n_rout = (len(rider.args), rider.n_buf) if rider else (0, 0)

    def body(*refs):
        ins, outs = refs[:n_in], refs[n_in + n_rin:n_in + n_rin + n_out]
        scr = refs[n_in + n_rin + n_out + n_rout:n_in + n_rin + n_out + n_rout + n_scr]
        if rider:
            r_refs = (refs[n_in:n_in + rider.n_src], refs[n_in + n_rin + n_out:n_in + n_rin + n_out + n_rout],
                      refs[n_in + n_rin + n_out + n_rout + n_scr:])
            ids = [pl.program_id(d) for d in range(len(grid))]
            first = functools.reduce(lambda p, q: p & q, [i == 0 for i in ids])
            last = functools.reduce(lambda p, q: p & q, [i == g - 1 for i, g in zip(ids, grid)])
            pl.when(first)(lambda: rider.start(*r_refs))
        core(ins, outs, scr)
        if rider:
            pl.when(last)(lambda: rider.wait(*r_refs))

    aliases = dict(aliases or {})
    if rider:
        sem = ("arbitrary",) * len(grid)
        aliases.update(rider.aliases(n_in, n_out))
    return pl.pallas_call(
        body, name=name, grid=grid, in_specs=list(in_specs) + [_ANY] * n_rin,
        out_specs=list(out_specs) + [_ANY] * n_rout,
        out_shape=list(out_shape) + (rider.out_shape if rider else []),
        scratch_shapes=list(scratch_shapes) + (rider.scratch if rider else []),
        input_output_aliases=aliases, compiler_params=_cparams(sem),
    )(*args, *(rider.args if rider else []))


def _exchange(direct, *, name):
    n_in = len(direct.args)

    def body(*refs):
        r_refs = refs[:direct.n_src], refs[n_in:n_in + direct.n_buf], refs[n_in + direct.n_buf:]
        direct.start(*r_refs)
        direct.wait(*r_refs)

    return pl.pallas_call(
        body, name=name, in_specs=[_ANY] * n_in, out_specs=[_ANY] * direct.n_buf, out_shape=direct.out_shape,
        input_output_aliases=direct.aliases(0, 0), scratch_shapes=direct.scratch,
    )(*direct.args)


def _matmul(a, b, *, form, tm, tn, tk, name, add=None, add_scale=1.0, rider=None, b_cols=None):
    if form == "nn":
        (m, kk), n = a.shape, b.shape[1]
        a_spec = pl.BlockSpec((tm, tk), lambda i, j, k: (i, k))
        b_spec = pl.BlockSpec((tk, tn), lambda i, j, k: (k, j))
        dn = (((1,), (0,)), ((), ()))
    elif form == "nt":
        (m, kk), n = a.shape, b.shape[0]
        a_spec = pl.BlockSpec((tm, tk), lambda i, j, k: (i, k))
        b_spec = pl.BlockSpec((tn, tk), lambda i, j, k: (j, k))
        dn = (((1,), (1,)), ((), ()))
    else:
        kk, m = a.shape
        n0, n = b_cols or (0, b.shape[1])
        assert n0 % tn == 0
        a_spec = pl.BlockSpec((tk, tm), lambda i, j, k: (k, i))
        b_spec = pl.BlockSpec((tk, tn), lambda i, j, k: (k, j + n0 // tn))
        dn = (((0,), (0,)), ((), ()))
    assert m % tm == 0 and n % tn == 0 and kk % tk == 0, (name, m, n, kk)
    has_add = add is not None

    def core(ins, outs, _):
        a_ref, b_ref = ins[:2]
        o_ref = outs[0]
        k = pl.program_id(2)
        p = lax.dot_general(a_ref[...].astype(MM_DTYPE), b_ref[...].astype(MM_DTYPE), dn,
                            preferred_element_type=F32)

        @pl.when(k == 0)
        def _():
            o_ref[...] = p + add_scale * ins[2][...] if has_add else p

        @pl.when(k > 0)
        def _():
            o_ref[...] += p

    in_specs = [a_spec, b_spec]
    args = [a, b]
    if has_add:
        in_specs.append(pl.BlockSpec((tm, tn), lambda i, j, k: (i, j)))
        args.append(add)
    res = _pcall(core, name=name, grid=(m // tm, n // tn, kk // tk), in_specs=in_specs,
                 out_specs=[pl.BlockSpec((tm, tn), lambda i, j, k: (i, j))],
                 out_shape=[jax.ShapeDtypeStruct((m, n), F32)], args=args,
                 sem=("parallel", "parallel", "arbitrary"), rider=rider)
    return res if rider else res[0]


ZERO_TAIL = 8


def _with_tail(x):
    return jnp.concatenate([x, jnp.zeros((ZERO_TAIL,) + x.shape[1:], x.dtype)], axis=0)


def _shift_down(x, k):
    return pltpu.roll(x, k, 0)


def _shift_up(x, k):
    return pltpu.roll(x, x.shape[0] - k, 0)


def _conv_slab(x, w):
    return w[3:4] * x + w[2:3] * _shift_down(x, 1) + w[1:2] * _shift_down(x, 2) + w[0:1] * _shift_down(x, 3)


def _prep_fwd(h, conv_w, *, name):
    t_len = h.shape[0]

    def body(x_ref, w_ref, o_ref):
        s = pl.program_id(0)
        y = _silu(_conv_slab(_with_tail(x_ref[...]), w_ref[...])[:t_len])
        rs = lax.rsqrt(jnp.sum(y * y, axis=-1, keepdims=True) + L2_EPS)
        scale = jnp.where(s < A_HEADS, A_HEAD_DIM ** -0.5, 1.0)
        o_ref[...] = jnp.where(s < 2 * A_HEADS, y * rs * scale, y)

    return pl.pallas_call(
        body, name=name, grid=(12,),
        in_specs=[pl.BlockSpec((t_len, LANE), lambda s: (0, L_QKV // LANE + s)),
                  pl.BlockSpec((8, LANE), lambda s: (0, s))],
        out_specs=pl.BlockSpec((t_len, LANE), lambda s: (0, s)),
        out_shape=jax.ShapeDtypeStruct((t_len, 3 * A_WIDTH), F32),
        compiler_params=_cparams(("parallel",)),
    )(h, conv_w)


def _prep_bwd(h, conv_w, d_out, dh, *, name):
    t_len = h.shape[0]

    def body(x_ref, w_ref, g_ref, dh_in, dx_ref, dw_ref):
        del dh_in
        s = pl.program_id(0)
        x = _with_tail(x_ref[...])
        g = _with_tail(g_ref[0])
        w = w_ref[...]
        xs = [_shift_down(x, 3), _shift_down(x, 2), _shift_down(x, 1), x]
        c = w[0:1] * xs[0] + w[1:2] * xs[1] + w[2:3] * xs[2] + w[3:4] * xs[3]
        sg = jax.nn.sigmoid(c)
        y = c * sg
        rs = lax.rsqrt(jnp.sum(y * y, axis=-1, keepdims=True) + L2_EPS)
        scale = jnp.where(s < A_HEADS, A_HEAD_DIM ** -0.5, 1.0)
        dy_n = scale * (rs * g - y * (rs * rs * rs) * jnp.sum(g * y, axis=-1, keepdims=True))
        dy = jnp.where(s < 2 * A_HEADS, dy_n, g)
        dc = dy * (sg * (1.0 + c * (1.0 - sg)))
        dx = w[3:4] * dc + w[2:3] * _shift_up(dc, 1) + w[1:2] * _shift_up(dc, 2) + w[0:1] * _shift_up(dc, 3)
        dx_ref[...] = dx[:t_len].astype(dx_ref.dtype)
        dws = [jnp.sum(dc * xs[j], axis=0, keepdims=True) for j in range(CONV_K)]
        dw_ref[...] = jnp.concatenate(dws + [jnp.zeros((8 - CONV_K, LANE), F32)], axis=0)

    slab = pl.BlockSpec((t_len, LANE), lambda s: (0, L_QKV // LANE + s))
    return pl.pallas_call(
        body, name=name, grid=(12,),
        in_specs=[slab, pl.BlockSpec((8, LANE), lambda s: (0, s)),
                  pl.BlockSpec((1, t_len, LANE), lambda s: (s // A_HEADS, 0, s % A_HEADS)), _ANY],
        out_specs=[slab, pl.BlockSpec((8, LANE), lambda s: (0, s))],
        out_shape=[jax.ShapeDtypeStruct((t_len, L_COLS), MM_DTYPE), jax.ShapeDtypeStruct((8, 3 * A_WIDTH), F32)],
        input_output_aliases={3: 0},
        compiler_params=_cparams(("parallel",)),
    )(h, conv_w, d_out, dh)


N_LEVELS = 5
MF_TRIL, MF_STRIL, MF_DIAG8, MF_LOW16, MF_EYE = 0, 1, 2, 3, 3 + N_LEVELS
MB_CUM, MB_CUM_T, MB_TOT = 0, 1, 2


def _gdn_masks():
    r = lax.broadcasted_iota(jnp.int32, (SUPER, SUPER), 0)
    c = lax.broadcasted_iota(jnp.int32, (SUPER, SUPER), 1)
    same = lambda shift: (r >> shift) == (c >> shift)
    ninf = lambda m: jnp.where(m, 0.0, -jnp.inf).astype(F32)
    one = lambda m: m.astype(F32)
    mf = jnp.stack([ninf(r >= c), ninf(r > c), one(same(3))]
                   + [one(same(4 + lv) & jnp.logical_not(same(3 + lv))) for lv in range(N_LEVELS)] + [one(r == c)])
    mb = jnp.stack([one(r >= c), one(r <= c), jnp.ones((SUPER, SUPER), F32)]).astype(BF16)
    return mf, mb


def _tri_inv_impl(a, mf):
    d = lambda p, q: jnp.dot(p.astype(BF16), q.astype(BF16), preferred_element_type=F32)
    dd = lambda p, q: jnp.dot(p, q, preferred_element_type=F32)
    eye = mf[MF_EYE]
    a0 = a * mf[MF_DIAG8]
    a2 = d(a0, a0)
    a4 = d(a2, a2)
    t = d(d(eye - a0, eye + a2), eye + a4)
    for level in range(N_LEVELS):
        t = t - d(d(t, a * mf[MF_LOW16 + level]), t)
    a_hi, a_lo = _split(a)
    for _ in range(NEWTON_STEPS):
        t0 = t.astype(BF16)
        t0f = t0.astype(F32)
        resid = (eye - t0f) - (dd(a_hi, t0) + dd(a_lo, t0))
        r_hi, r_lo = _split(resid)
        t = t0f + (dd(t0, r_hi) + dd(t0, r_lo))
    return t


@jax.custom_vjp
def _wy_apply(a, rhs, t):
    return _mm(t, rhs)


def _wy_apply_fwd(a, rhs, t):
    x = _mm(t, rhs)
    return x, (t, x)


def _wy_apply_bwd(res, dx):
    t, x = res
    d_rhs = _mm_tn(t, dx)
    return -_mm_nt(d_rhs, x), d_rhs, jnp.zeros_like(t)


_wy_apply.defvjp(_wy_apply_fwd, _wy_apply_bwd)


@functools.partial(jax.custom_vjp, nondiff_argnums=(1,))
def _lane_roll(x, shift):
    return pltpu.roll(x, shift % LANE, 1)


_lane_roll.defvjp(lambda x, shift: (_lane_roll(x, shift), None), lambda shift, _, g: (_lane_roll(g, -shift),))


def _mask_times_lanes(x, mask):
    lane = lax.broadcasted_iota(jnp.int32, (1, LANE), 1)
    x = jnp.where(lane < A_HEADS, x, 0.0)
    x1 = x.astype(BF16).astype(F32)
    x2 = (x - x1).astype(BF16).astype(F32)
    x3 = (x - x1 - x2).astype(BF16).astype(F32)
    pieces = x1 + pltpu.roll(x2, A_HEADS, 1) + pltpu.roll(x3, 2 * A_HEADS, 1)
    res = jnp.dot(mask, pieces.astype(BF16), preferred_element_type=F32)
    return res + pltpu.roll(res, LANE - A_HEADS, 1) + pltpu.roll(res, LANE - 2 * A_HEADS, 1)


@jax.custom_vjp
def _chunk_sums(g, mb):
    return _mask_times_lanes(g, mb[MB_CUM]), _mask_times_lanes(g, mb[MB_TOT])


def _chunk_sums_fwd(g, mb):
    return _chunk_sums(g, mb), mb


def _chunk_sums_bwd(mb, d):
    lane = lax.broadcasted_iota(jnp.int32, (1, LANE), 1)
    dg = _mask_times_lanes(d[0], mb[MB_CUM_T]) + _mask_times_lanes(d[1], mb[MB_TOT])
    return jnp.where(lane < A_HEADS, dg, 0.0), jnp.zeros_like(mb)


_chunk_sums.defvjp(_chunk_sums_fwd, _chunk_sums_bwd)


def _gdn_gates(ba, alog, dtb, mb):
    beta = jax.nn.sigmoid(ba)
    g = -jnp.exp(alog) * _softplus(_lane_roll(ba, -A_HEADS) + dtb)
    gc, gl = _chunk_sums(g, mb)
    return beta, gc, gl, gc.T


def _gdn_block(s, q, k, v, z, gates, nw, h, t_known, mf):
    n = q.shape[0]
    beta_all, gc_all, gl_all, gct_all = gates
    lane = lax.broadcasted_iota(jnp.int32, (1, LANE), 1)
    sub = lax.broadcasted_iota(jnp.int32, (LANE, 1), 0)
    col = lambda x: jnp.sum(jnp.where(lane == h, x, 0.0), axis=1, keepdims=True)
    wide = lambda c: jnp.broadcast_to(c, (n, LANE))
    gc, gl = col(gc_all), col(gl_all)
    gc_row = jnp.sum(jnp.where(sub == h, gct_all, 0.0), axis=0, keepdims=True)
    beta_w, eg_w = wide(col(beta_all)), wide(jnp.exp(gc))
    diff = gc - gc_row
    decay = jnp.exp(diff + mf[MF_TRIL])
    kb = k * beta_w
    a_mat = _mm_nt(kb, k) * jnp.exp(diff + mf[MF_STRIL])
    rhs = jnp.concatenate([v * beta_w, kb * eg_w], axis=1)
    if t_known is None:
        t_mat = _tri_inv_impl(a_mat, mf)
        uw = _mm(t_mat, rhs)
    else:
        t_mat = t_known
        uw = _wy_apply(a_mat, rhs, t_known)
    u, w = uw[:, :LANE], uw[:, LANE:]
    qk = _mm_nt(q, k) * decay
    q_dec = q * eg_w
    k_dec = k * wide(jnp.exp(gl - gc))
    v_new = u - _mm(w, s)
    o = _mm(q_dec, s) + _mm(qk, v_new)
    s = s * jnp.exp(gl[0:1]) + _mm_tn(k_dec, v_new)
    o = o * lax.rsqrt(jnp.mean(o * o, axis=-1, keepdims=True) + RMS_EPS) * nw
    return o * _silu(z), s, t_mat


def _gdn_fwd(qkv, h, alog, dtb, nw, ycat, *, name, rider=None):
    t_len = qkv.shape[0]
    nsc = t_len // SUPER

    def core(ins, outs, scr):
        q_ref, k_ref, v_ref, gate_ref, al_ref, dt_ref, nw_ref, mf_ref, mb_ref, _ = ins
        y_ref, sin_ref, t_ref = outs
        s_scr, = scr

        @pl.when(pl.program_id(0) == 0)
        def _():
            s_scr[...] = jnp.zeros_like(s_scr)

        per_head = lambda ref: jnp.stack([ref[:, hh * LANE:(hh + 1) * LANE] for hh in range(A_HEADS)])
        states = s_scr[...]
        gates = _gdn_gates(gate_ref[:, A_WIDTH:], al_ref[...], dt_ref[...], mb_ref[...])
        fn = jax.vmap(_gdn_block, in_axes=(0, 0, 0, 0, 0, None, None, 0, None, None))
        y, s_new, t_mat = fn(states, per_head(q_ref), per_head(k_ref), per_head(v_ref), per_head(gate_ref),
                             gates, nw_ref[...], jnp.arange(A_HEADS), None, mf_ref[...])
        sin_ref[0] = states
        t_ref[0] = t_mat
        s_scr[...] = s_new
        for hh in range(A_HEADS):
            y_ref[:, hh * LANE:(hh + 1) * LANE] = y[hh].astype(y_ref.dtype)

    blk = lambda j: pl.BlockSpec((SUPER, A_WIDTH), lambda sc: (sc, j))
    row = pl.BlockSpec((1, LANE), lambda sc: (0, 0))
    mf, mb = _gdn_masks()
    whole = lambda a: pl.BlockSpec(a.shape, lambda sc: (0, 0, 0))
    return _pcall(
        core, name=name, grid=(nsc,),
        in_specs=[blk(0), blk(1), blk(2), pl.BlockSpec((SUPER, L_GATE), lambda sc: (sc, L_ZA // L_GATE)),
                  row, row, row, whole(mf), whole(mb), _ANY],
        out_specs=[blk(0),
                   pl.BlockSpec((1, A_HEADS, A_HEAD_DIM, A_HEAD_DIM), lambda sc: (sc, 0, 0, 0)),
                   pl.BlockSpec((1, A_HEADS, SUPER, SUPER), lambda sc: (sc, 0, 0, 0))],
        out_shape=[jax.ShapeDtypeStruct((t_len, D_MODEL), MM_DTYPE),
                   jax.ShapeDtypeStruct((nsc, A_HEADS, A_HEAD_DIM, A_HEAD_DIM), F32),
                   jax.ShapeDtypeStruct((nsc, A_HEADS, SUPER, SUPER), F32)],
        scratch_shapes=[pltpu.VMEM((A_HEADS, A_HEAD_DIM, A_HEAD_DIM), F32)],
        aliases={9: 0}, sem=("arbitrary",), rider=rider,
        args=(qkv, qkv, qkv, h, alog, dtb, nw, mf, mb, ycat))


def _gdn_bwd(qkv, h, alog, dtb, nw, s_in, t_in, dycat, dh, *, name, rider=None):
    t_len = qkv.shape[0]
    nsc = t_len // SUPER

    def core(ins, outs, scr):
        q_ref, k_ref, v_ref, gate_ref, al_ref, dt_ref, nw_ref, sin_ref, t_ref, dy_ref, mf_ref, mb_ref, _ = ins
        dgate_ref, dqkv_ref, dal_ref, ddt_ref, dnw_ref = outs
        ds_scr, = scr

        @pl.when(pl.program_id(0) == 0)
        def _():
            ds_scr[...] = jnp.zeros_like(ds_scr)
            dal_ref[...] = jnp.zeros_like(dal_ref)
            ddt_ref[...] = jnp.zeros_like(ddt_ref)
            dnw_ref[...] = jnp.zeros_like(dnw_ref)

        per_head = lambda ref: jnp.stack([ref[:, hh * LANE:(hh + 1) * LANE] for hh in range(A_HEADS)])
        head_ids = jnp.arange(A_HEADS)
        t_known, mf, mb = t_ref[0], mf_ref[...], mb_ref[...]

        def fn(s, q, k, v, z, ba, alog, dtb, nw):
            gates = _gdn_gates(ba, alog, dtb, mb)
            one = lambda s, q, k, v, z, t, h: _gdn_block(s, q, k, v, z, gates, nw, h, t, mf)[:2]
            return jax.vmap(one)(s, q, k, v, z, t_known, head_ids)

        _, vjp = jax.vjp(fn, sin_ref[0], per_head(q_ref), per_head(k_ref), per_head(v_ref), per_head(gate_ref),
                         gate_ref[:, A_WIDTH:], al_ref[...], dt_ref[...], nw_ref[...])
        ds, dq, dk, dv, dz, dba, dal, ddt, dnw = vjp((per_head(dy_ref), ds_scr[...]))
        ds_scr[...] = ds
        for hh in range(A_HEADS):
            cols = slice(hh * LANE, (hh + 1) * LANE)
            dqkv_ref[0, :, cols] = dq[hh]
            dqkv_ref[1, :, cols] = dk[hh]
            dqkv_ref[2, :, cols] = dv[hh]
            dgate_ref[:, cols] = dz[hh].astype(dgate_ref.dtype)
        dgate_ref[:, A_WIDTH:] = dba.astype(dgate_ref.dtype)
        dal_ref[...] += dal
        ddt_ref[...] += ddt
        dnw_ref[...] += dnw

    rev = lambda i: nsc - 1 - i
    blk = lambda j: pl.BlockSpec((SUPER, A_WIDTH), lambda i: (rev(i), j))
    gate = pl.BlockSpec((SUPER, L_GATE), lambda i: (rev(i), L_ZA // L_GATE))
    row = pl.BlockSpec((1, LANE), lambda i: (0, 0))
    mf, mb = _gdn_masks()
    whole = lambda a: pl.BlockSpec(a.shape, lambda i: (0, 0, 0))
    return _pcall(
        core, name=name, grid=(nsc,),
        in_specs=[blk(0), blk(1), blk(2), gate, row, row, row,
                  pl.BlockSpec((1, A_HEADS, A_HEAD_DIM, A_HEAD_DIM), lambda i: (rev(i), 0, 0, 0)),
                  pl.BlockSpec((1, A_HEADS, SUPER, SUPER), lambda i: (rev(i), 0, 0, 0)),
                  blk(0), whole(mf), whole(mb), _ANY],
        out_specs=[gate, pl.BlockSpec((3, SUPER, A_WIDTH), lambda i: (0, rev(i), 0)), row, row, row],
        out_shape=[jax.ShapeDtypeStruct((t_len, L_COLS), MM_DTYPE), jax.ShapeDtypeStruct((3, t_len, A_WIDTH), F32)]
        + [jax.ShapeDtypeStruct((1, LANE), F32)] * 3,
        scratch_shapes=[pltpu.VMEM((A_HEADS, A_HEAD_DIM, A_HEAD_DIM), F32)],
        aliases={12: 0}, sem=("arbitrary",), rider=rider,
        args=(qkv, qkv, qkv, h, alog, dtb, nw, s_in, t_in, dycat, mf, mb, dh))


Q_BLOCKS = 4
Q_ROWS = Q_BLOCKS * BLOCK


def _swa_block(q, kp, kc, vp, vc, z, sinks, first):
    rows = B_GROUP * BLOCK
    ri = lax.broadcasted_iota(jnp.int32, (rows, 2 * BLOCK), 0)
    si = lax.broadcasted_iota(jnp.int32, (rows, 2 * BLOCK), 1)
    dist = (ri & (BLOCK - 1)) + BLOCK - si
    bias = jnp.where((dist >= 0) & (dist < WINDOW), 0.0, -jnp.inf)
    no_prev = jnp.where(first & (si[:1] < BLOCK), -jnp.inf, 0.0)
    dist_f = dist.astype(F32)
    head_of_row = lax.broadcasted_iota(jnp.int32, (rows, 1), 0) >> 7
    keys = jnp.concatenate([kp, kc], axis=0)
    vals = jnp.concatenate([vp, vc], axis=0)

    def item(b, j):
        cs = slice(j * B_HEAD_DIM, (j + 1) * B_HEAD_DIM)
        rs = slice(b * BLOCK, (b + 1) * BLOCK)
        heads = range(j * B_GROUP, (j + 1) * B_GROUP)
        qs = jnp.concatenate([q[rs, hq * B_HEAD_DIM:(hq + 1) * B_HEAD_DIM] for hq in heads], axis=0) * (
            B_HEAD_DIM ** -0.5)
        kk = keys[b * BLOCK:(b + 2) * BLOCK, cs]
        vv = vals[b * BLOCK:(b + 2) * BLOCK, cs]
        sink = jnp.concatenate([jnp.broadcast_to(sinks[:, hq:hq + 1], (BLOCK, 1)) for hq in heads], axis=0)
        slope = sum(jnp.where(head_of_row == gi, 2.0 ** (-8.0 * (hq + 1) / B_Q_HEADS), 0.0)
                    for gi, hq in enumerate(heads))
        return qs, kk, vv, sink, slope, (no_prev if b == 0 else jnp.zeros_like(no_prev))

    def attend(qs, kk, vv, sink, slope, hide):
        sc = _mm_nt(qs, kk) - slope * dist_f + (bias + hide)
        m = lax.stop_gradient(jnp.maximum(jnp.max(sc, axis=-1, keepdims=True), sink))
        p = jnp.exp(sc - m)
        inv = 1.0 / (jnp.sum(p, axis=-1, keepdims=True) + jnp.exp(sink - m))
        return _mm(p * inv, vv)

    items = [(b, j) for b in range(Q_BLOCKS) for j in range(B_KV_HEADS)]
    o = jax.vmap(attend)(*[_stack(t) for t in zip(*[item(b, j) for b, j in items])])
    rows_out = [jnp.concatenate([o[b * B_KV_HEADS + j, gi * BLOCK:(gi + 1) * BLOCK]
                                 for j in range(B_KV_HEADS) for gi in range(B_GROUP)], axis=1)
                for b in range(Q_BLOCKS)]
    return jnp.concatenate(rows_out, axis=0) * _silu(z)


def _swa_specs(idx):
    wide = lambda off: pl.BlockSpec((Q_ROWS, B_WIDTH), lambda n: (idx(n), off))
    cur = lambda off: pl.BlockSpec((Q_ROWS, LANE), lambda n: (idx(n), off))
    prev = lambda off: pl.BlockSpec((BLOCK, LANE), lambda n: (jnp.maximum(idx(n) * Q_BLOCKS - 1, 0), off))
    return [wide(L_QB // B_WIDTH), prev(L_KB // LANE), cur(L_KB // LANE), prev(L_VB // LANE), cur(L_VB // LANE),
            wide(L_ZB // B_WIDTH), pl.BlockSpec((1, LANE), lambda n: (0, 0))]


def _swa_fwd(h, sinks, *, name, rider=None):
    t_len = h.shape[0]
    nb = t_len // Q_ROWS

    def core(ins, outs, _):
        q_ref, kp_ref, kc_ref, vp_ref, vc_ref, z_ref, s_ref = ins
        outs[0][...] = _swa_block(q_ref[...], kp_ref[...], kc_ref[...], vp_ref[...], vc_ref[...], z_ref[...],
                                  s_ref[...], pl.program_id(0) == 0).astype(outs[0].dtype)

    res = _pcall(core, name=name, grid=(nb,), in_specs=_swa_specs(lambda n: n),
                 out_specs=[pl.BlockSpec((Q_ROWS, B_WIDTH), lambda n: (n, 1))],
                 out_shape=[jax.ShapeDtypeStruct((t_len, D_MODEL), MM_DTYPE)], sem=("parallel",), rider=rider,
                 args=(h, h, h, h, h, h, sinks))
    return res if rider else res[0]


def _swa_bwd(h, sinks, dycat, *, name, rider=None):
    t_len = h.shape[0]
    nb = t_len // Q_ROWS
    early = slice(0, Q_ROWS - BLOCK)
    last = slice(Q_ROWS - BLOCK, Q_ROWS)

    def core(ins, outs, scr):
        q_ref, kp_ref, kc_ref, vp_ref, vc_ref, z_ref, s_ref, dy_ref = ins
        dh_ref, dsk_ref = outs
        ck_scr, cv_scr = scr
        i = pl.program_id(0)
        n = nb - 1 - i

        @pl.when(i == 0)
        def _():
            ck_scr[...] = jnp.zeros_like(ck_scr)
            cv_scr[...] = jnp.zeros_like(cv_scr)
            dsk_ref[...] = jnp.zeros_like(dsk_ref)

        fn = functools.partial(_swa_block, first=(n == 0))
        _, vjp = jax.vjp(fn, q_ref[...], kp_ref[...], kc_ref[...], vp_ref[...], vc_ref[...], z_ref[...], s_ref[...])
        dq, dkp, dkc, dvp, dvc, dz, dsk = vjp(dy_ref[...])
        def put(rows, col, val):
            dh_ref[rows, col:col + val.shape[1]] = val.astype(dh_ref.dtype)

        put(slice(None), L_QB, dq)
        put(slice(None), L_ZB, dz)
        put(early, L_KB, dkc[early])
        put(early, L_VB, dvc[early])
        put(last, L_KB, dkc[last] + ck_scr[...])
        put(last, L_VB, dvc[last] + cv_scr[...])
        ck_scr[...] = dkp
        cv_scr[...] = dvp
        dsk_ref[...] += dsk

    rev = lambda i: nb - 1 - i
    return _pcall(
        core, name=name, grid=(nb,),
        in_specs=_swa_specs(rev) + [pl.BlockSpec((Q_ROWS, B_WIDTH), lambda i: (rev(i), 1))],
        out_specs=[pl.BlockSpec((Q_ROWS, L_SWA), lambda i: (rev(i), 0)), pl.BlockSpec((1, LANE), lambda i: (0, 0))],
        out_shape=[jax.ShapeDtypeStruct((t_len, L_COLS), MM_DTYPE), jax.ShapeDtypeStruct((1, LANE), F32)],
        scratch_shapes=[pltpu.VMEM((BLOCK, LANE), F32), pltpu.VMEM((BLOCK, LANE), F32)],
        sem=("arbitrary",), rider=rider, args=(h, h, h, h, h, h, sinks, dycat))


def _out_ln_fwd(ycat, w_out, x, ln_g, ln_b, *, name, tm=512, w_in_next=None):
    t_len = x.shape[0]
    last = w_in_next is None

    def body(y_ref, w_ref, x_ref, g_ref, b_ref, *rest):
        r = DEEPNORM_ALPHA * x_ref[...] + _mm(y_ref[...], w_ref[...])
        if last:
            rest[0][...] = r
            return
        win_ref, r_ref, o_ref, h_ref = rest
        r_ref[...] = r
        mu = jnp.mean(r, axis=-1, keepdims=True)
        d = r - mu
        var = jnp.mean(d * d, axis=-1, keepdims=True)
        xn = d * lax.rsqrt(var + LN_EPS) * g_ref[...] + b_ref[...]
        o_ref[...] = xn
        h_ref[...] = _mm_nt(xn, win_ref[...])

    tile = pl.BlockSpec((tm, D_MODEL), lambda i: (i, 0))
    vec = pl.BlockSpec((1, D_MODEL), lambda i: (0, 0))
    tile_shape = jax.ShapeDtypeStruct((t_len, D_MODEL), F32)
    in_specs = [tile, pl.BlockSpec((D_MODEL, D_MODEL), lambda i: (0, 0)), tile, vec, vec]
    if last:
        args, out_specs, out_shape = (), [tile], [tile_shape]
    else:
        args = (w_in_next,)
        in_specs.append(pl.BlockSpec((L_COLS, D_MODEL), lambda i: (0, 0)))
        out_specs = [tile, tile, pl.BlockSpec((tm, L_COLS), lambda i: (i, 0))]
        out_shape = [tile_shape, tile_shape, jax.ShapeDtypeStruct((t_len, L_COLS), F32)]
    res = pl.pallas_call(
        body, name=name, grid=(t_len // tm,), in_specs=in_specs, out_specs=out_specs, out_shape=out_shape,
        compiler_params=_cparams(("parallel",)),
    )(ycat, w_out, x, ln_g, ln_b, *args)
    return (res[0], None, None) if last else res


def _ln_out_bwd(dxn, r, ln_g, ycat, w_out, *, name, tm=512, loss=None):
    t_len = r.shape[0]

    def body(*refs):
        if loss:
            t_ref, r_ref, g_ref, b_ref, y_ref, w_ref, dr_ref, dg_ref, db_ref, l_ref, dy_ref, dw_ref = refs
        else:
            dx_ref, r_ref, g_ref, y_ref, w_ref, dr_ref, dg_ref, db_ref, dy_ref, dw_ref = refs

        @pl.when(pl.program_id(0) == 0)
        def _():
            dg_ref[...] = jnp.zeros_like(dg_ref)
            db_ref[...] = jnp.zeros_like(db_ref)
            dw_ref[...] = jnp.zeros_like(dw_ref)
            if loss:
                l_ref[...] = jnp.zeros_like(l_ref)

        rr = r_ref[...]
        if loss:
            rr = DEEPNORM_ALPHA * rr + _mm(y_ref[...], w_ref[...])
        mu = jnp.mean(rr, axis=-1, keepdims=True)
        d = rr - mu
        rstd = lax.rsqrt(jnp.mean(d * d, axis=-1, keepdims=True) + LN_EPS)
        xh = d * rstd
        if loss:
            e = (xh * g_ref[...] + b_ref[...]) - t_ref[...]
            dx = e * (1.0 / D_MODEL)
            l_ref[...] += jnp.sum(e * e, axis=0, keepdims=True)
        else:
            dx = dx_ref[...]
        dxh = dx * g_ref[...]
        dr = rstd * (dxh - jnp.mean(dxh, axis=-1, keepdims=True) - xh * jnp.mean(dxh * xh, axis=-1, keepdims=True))
        dr_ref[...] = dr
        dg_ref[...] += jnp.sum(dx * xh, axis=0, keepdims=True)
        db_ref[...] += jnp.sum(dx, axis=0, keepdims=True)
        dy_ref[...] = _mm_nt(dr, w_ref[...])
        dw_ref[...] += _mm_tn(y_ref[...], dr)

    tile = pl.BlockSpec((tm, D_MODEL), lambda i: (i, 0))
    vec = pl.BlockSpec((1, D_MODEL), lambda i: (0, 0))
    square = pl.BlockSpec((D_MODEL, D_MODEL), lambda i: (0, 0))
    tile_shape = jax.ShapeDtypeStruct((t_len, D_MODEL), F32)
    vec_shape = jax.ShapeDtypeStruct((1, D_MODEL), F32)
    args = (loss[0], r, ln_g, loss[1]) if loss else (dxn, r, ln_g)
    return pl.pallas_call(
        body, name=name, grid=(t_len // tm,),
        in_specs=[tile, tile, vec] + ([vec] if loss else []) + [tile, square],
        out_specs=[tile, vec, vec] + ([vec] if loss else []) + [tile, square],
        out_shape=[tile_shape, vec_shape, vec_shape] + ([vec_shape] if loss else [])
        + [tile_shape, jax.ShapeDtypeStruct((D_MODEL, D_MODEL), F32)],
        compiler_params=_cparams(("arbitrary",)),
    )(*args, ycat, w_out)


def _pad_row(v):
    return jnp.zeros((1, LANE), F32).at[0, :v.shape[0]].set(v)


_REGIONS = ((0, 1536, L_QKV), (1536, 2048, L_ZA), (2048, 2056, L_BA), (2056, 2568, L_QB), (2568, 2696, L_KB),
            (2696, 2824, L_VB), (2824, 3336, L_ZB))


def _shard_pieces(regions):
    for a, b, off in regions:
        for d in range(N_DEV):
            lo, hi = max(a, d * SHARD_COLS), min(b, (d + 1) * SHARD_COLS)
            if lo < hi:
                yield d, lo - d * SHARD_COLS, hi - d * SHARD_COLS, off + lo - a


def _as_list(r):
    return list(r) if isinstance(r, (list, tuple)) else [r]


def _gathered(shard):
    return jax.ShapeDtypeStruct((N_DEV,) + shard.shape, shard.dtype)


def _full_w_in(g_in, name):
    by_offset = sorted(_shard_pieces(_REGIONS), key=lambda p: p[3])
    tc = 256

    def body(g_ref, o_ref):
        pieces, row = [], 0
        for d, lo, hi, off in by_offset + [(None, 0, 0, L_COLS)]:
            if off > row:
                pieces.append(jnp.zeros((off - row, tc), g_ref.dtype))
            if d is not None:
                pieces.append(g_ref[d, lo:hi, :])
            row = off + hi - lo
        o_ref[...] = jnp.concatenate(pieces, axis=0)

    return pl.pallas_call(
        body, name=name, grid=(D_MODEL // tc,),
        in_specs=[pl.BlockSpec((N_DEV, SHARD_COLS, tc), lambda i: (0, 0, i))],
        out_specs=pl.BlockSpec((L_COLS, tc), lambda i: (0, i)),
        out_shape=jax.ShapeDtypeStruct((L_COLS, D_MODEL), g_in.dtype),
        compiler_params=_cparams(("parallel",)),
    )(g_in)


def _full_conv(g_conv):
    return jnp.pad(g_conv.transpose(1, 0, 2).reshape(CONV_K, 3 * A_WIDTH), ((0, 8 - CONV_K), (0, 0)))


def _forward(x, weights, shards, small):
    a_log, dt_bias, norm_w, sinks, ln_g, ln_b = small
    tm = min(512, x.shape[0])
    saved, weights = [], [list(w) for w in weights]
    whole = lambda arrs: _Direct([(a, False, j, ()) for j, a in enumerate(arrs)], [_gathered(a) for a in arrs])
    h = None
    for l in range(DEPTH):
        if h is None:
            rider = whole(shards[l][1:]) if weights[l][1] is None else None
            h, *got = _as_list(_matmul(x, weights[l][0], form="nt", tm=tm, tn=L_COLS, tk=D_MODEL,
                                       name=f"in_proj_{l}", rider=rider))
            if rider:
                weights[l][1:] = [got[0].reshape(D_MODEL, D_MODEL), _full_conv(got[1])]
        w_in_l, w_out_l, conv_l = weights[l]
        qkv = _prep_fwd(h, conv_l, name=f"prep_fwd_{l}")
        al, dt, nw, sk = _pad_row(a_log[l]), _pad_row(dt_bias[l]), norm_w[l][None, :], _pad_row(sinks[l])
        ahead = l + 1 < DEPTH and weights[l + 1][0] is None
        rider = whole(shards[l + 1][1:]) if ahead else None
        ycat, *got = _as_list(_swa_fwd(h, sk, name=f"swa_fwd_{l}", rider=rider))
        if ahead:
            weights[l + 1][1:] = [got[0].reshape(D_MODEL, D_MODEL), _full_conv(got[1])]
        rider = whole(shards[l + 1][:1]) if ahead else None
        ycat, s_in, t_in, *got = _gdn_fwd(qkv, h, al, dt, nw, ycat, name=f"gdn_fwd_{l}", rider=rider)
        if ahead:
            weights[l + 1][0] = _full_w_in(got[0], f"w_in_rows_{l + 1}")
        r, xn, h_next = None, None, None
        if l + 1 < DEPTH:
            r, xn, h_next = _out_ln_fwd(ycat, w_out_l, x, ln_g[l][None, :], ln_b[l][None, :], name=f"out_ln_{l}",
                                        w_in_next=weights[l + 1][0])
        saved.append((x, h, qkv, s_in, t_in, ycat, r, al, dt, nw, sk))
        x, h = xn, h_next
    return x, saved, weights


def _w_in_blocks(g, name):
    cols, tc = g.shape[1], 256
    pieces = list(_shard_pieces(_REGIONS))

    def body(g_ref, o_ref):
        blocks = [[] for _ in range(N_DEV)]
        for d, lo, hi, off in pieces:
            blocks[d].append(g_ref[off:off + hi - lo, :])
        for d in range(N_DEV):
            o_ref[d] = jnp.concatenate(blocks[d], axis=0).astype(BF16)

    return pl.pallas_call(
        body, name=name, grid=(cols // tc,),
        in_specs=[pl.BlockSpec((L_COLS, tc), lambda i: (0, i))],
        out_specs=pl.BlockSpec((N_DEV, SHARD_COLS, tc), lambda i: (0, 0, i)),
        out_shape=jax.ShapeDtypeStruct((N_DEV, SHARD_COLS, cols), BF16),
        compiler_params=_cparams(("parallel",)),
    )(g)


def _small_blocks(g):
    c_conv = g["conv_w"].reshape(CONV_K, N_DEV, CONV_SHARD_COLS).transpose(1, 0, 2)
    c_small = [jnp.broadcast_to(g[n][None], (N_DEV,) + g[n].shape) for n, _ in SMALL_SIZES]
    return _pack_small(c_conv, c_small)


def _contributions(g):
    c_out = g["w_out"].astype(BF16).reshape(N_DEV, OUT_SHARD_ROWS, D_MODEL)
    return _w_in_blocks(g["w_in_rows"], name="w_in_grad_blocks_above"), c_out, _small_blocks(g)


def _backward_layer(l, dx, saved_l, weights_l, ln_g_l, above=None, loss=None):
    x_in, h, qkv, s_in, t_in, ycat, r, al, dt, nw, sk = saved_l
    w_in_l, w_out_l, conv_l = weights_l
    tm = min(512, x_in.shape[0])
    dr, d_lng, d_lnb, *loss_lanes, dycat, d_wout = _ln_out_bwd(dx, x_in if loss else r, ln_g_l[None, :], ycat, w_out_l,
                                                               name=f"ln_out_bwd_{l}", tm=tm, loss=loss)
    big = min(1024, x_in.shape[0])
    rider, p_in, p_out, p_small = None, None, None, None
    recv = lambda c: jax.ShapeDtypeStruct((DEPTH,) + c.shape, c.dtype)
    if above:
        c_out = d_wout.astype(BF16).reshape(N_DEV, OUT_SHARD_ROWS, D_MODEL)
        rider = _Direct([(above[1], True, 0, (l + 1,)), (above[2], True, 1, (l + 1,)), (c_out, True, 0, (l,))],
                        [recv(above[1]), recv(above[2])])
    dh, d_sk, *got = _swa_bwd(h, sk, dycat, name=f"swa_bwd_{l}", rider=rider)
    if above:
        p_out, p_small = got
        rider = _Direct([(above[0], True, 0, (l + 1,))], [recv(above[0])])
    dh, dqkv_n, d_al, d_dt, d_nw, *got = _gdn_bwd(qkv, h, al, dt, nw, s_in, t_in, dycat, dh,
                                                  name=f"gdn_bwd_{l}", rider=rider)
    dh, d_conv = _prep_bwd(h, conv_l, dqkv_n, dh, name=f"prep_bwd_{l}")
    grads = dict(w_out=d_wout, conv_w=d_conv[:CONV_K], a_log=d_al[0, :A_HEADS], dt_bias=d_dt[0, :A_HEADS],
                 norm_w=d_nw[0], sinks=d_sk[0, :B_Q_HEADS], ln_g=d_lng[0], ln_b=d_lnb[0])
    dw = functools.partial(_matmul, dh, x_in, form="tn")
    if not above:
        grads["w_in_rows"] = dw(name=f"in_proj_dw_{l}", tm=L_COLS // 3, tn=D_MODEL, tk=min(2048, x_in.shape[0]))
    else:
        p_in, = got
        cut = D_MODEL // 2
        rest = D_MODEL - cut
        first = dw(name=f"in_proj_dw_first_{l}", tm=L_COLS, tn=cut, tk=big, b_cols=(0, cut))
        blocks = _w_in_blocks(first, name=f"w_in_grad_blocks_first_{l}")
        rider = _Direct([(blocks, True, 0, (l,), (slice(None), pl.ds(0, cut)))], [p_in])
        second, p_in = dw(name=f"in_proj_dw_second_{l}", tm=L_COLS, tn=cut, tk=big, b_cols=(cut, rest), rider=rider)
        blocks = _w_in_blocks(second, name=f"w_in_grad_blocks_second_{l}")
        rider = _Direct([(blocks, True, 0, (l,), (slice(None), pl.ds(cut, rest))),
                         (_small_blocks(grads), True, 1, (l,))], [p_in, p_small])
    dx, *got = _as_list(_matmul(dh, w_in_l, form="nn", tm=tm, tn=D_MODEL, tk=L_COLS, name=f"in_proj_dx_{l}",
                                add=dr, add_scale=DEEPNORM_ALPHA, rider=rider))
    bufs = (got[0], p_out, got[1]) if above else None
    return dx, grads, bufs, (loss_lanes[0] if loss else None)


def _all_gather(shards, *, name):
    n_arr = len(shards)

    def body(*refs):
        x_refs, out_refs = refs[:n_arr], refs[n_arr:2 * n_arr]
        send_sems, recv_sems, local_sems = refs[2 * n_arr:]
        x, y, c = _me()
        me, sibling = (x, y, c), (x, y, 1 - c)
        chips = [(1 - x, y), (x, 1 - y), (1 - x, 1 - y)]

        def copy(a, k, block, to, src=None):
            dst = out_refs[a].at[_flat_id(block)]
            return _remote(dst if src is None else src, dst, send_sems.at[a, k], recv_sems.at[a, k], to)

        mine = [pltpu.make_async_copy(x_refs[a], out_refs[a].at[_flat_id(me)], local_sems.at[a])
                for a in range(n_arr)]
        for cp in mine:
            cp.start()
        first = []
        for a in range(n_arr):
            first.append(copy(a, 0, me, sibling, src=x_refs[a]))
            first += [copy(a, 1 + j, me, (*chip, c), src=x_refs[a]) for j, chip in enumerate(chips)]
        for cp in first:
            cp.start()
        passed = []
        for j, chip in enumerate(chips):
            for a in range(n_arr):
                copy(a, 1 + j, (*chip, c), me).wait_recv()
                fwd = copy(a, 4 + j, (*chip, c), sibling)
                fwd.start()
                passed.append(fwd)
        for a in range(n_arr):
            copy(a, 0, sibling, me).wait_recv()
            for j, chip in enumerate(chips):
                copy(a, 4 + j, (*chip, 1 - c), me).wait_recv()
        for cp in first + passed:
            cp.wait_send()
        for cp in mine:
            cp.wait()

    return pl.pallas_call(
        body, name=name, in_specs=[_ANY] * n_arr, out_specs=[_ANY] * n_arr,
        out_shape=[jax.ShapeDtypeStruct((N_DEV,) + s.shape, s.dtype) for s in shards],
        scratch_shapes=[pltpu.SemaphoreType.DMA((n_arr, N_DEV - 1)), pltpu.SemaphoreType.DMA((n_arr, N_DEV - 1)),
                        pltpu.SemaphoreType.DMA((n_arr,))],
    )(*shards)


def _adamw(parts, w, m, v, *, tr, name):
    depth, rows, cols = w.shape
    c1 = 1.0 - ADAM_B1 ** ADAM_STEP
    c2 = 1.0 - ADAM_B2 ** ADAM_STEP

    def body(g_ref, w_ref, m_ref, v_ref, go_ref, d_ref, mo_ref, vo_ref):
        g = g_ref[0, 0].astype(F32)
        for s in range(1, N_DEV):
            g = g + g_ref[0, s].astype(F32)
        m_new = ADAM_B1 * m_ref[0] + (1.0 - ADAM_B1) * g
        v_new = ADAM_B2 * v_ref[0] + (1.0 - ADAM_B2) * (g * g)
        go_ref[0] = g
        mo_ref[0] = m_new
        vo_ref[0] = v_new
        d_ref[0] = -ADAM_LR * ((m_new / c1) / (jnp.sqrt(v_new / c2) + ADAM_EPS) + ADAM_WD * w_ref[0])

    tile = pl.BlockSpec((1, tr, cols), lambda l, i: (l, i, 0))
    return pl.pallas_call(
        body, name=name, grid=(depth, rows // tr),
        in_specs=[pl.BlockSpec((1, N_DEV, tr, cols), lambda l, i: (l, 0, i, 0)), tile, tile, tile],
        out_specs=[tile] * 4, out_shape=[jax.ShapeDtypeStruct(w.shape, F32)] * 4,
        compiler_params=_cparams(("parallel", "parallel")),
    )(parts, w, m, v)


def _adamw_w_in(parts, w, m, v, *, name):
    c1 = 1.0 - ADAM_B1 ** ADAM_STEP
    c2 = 1.0 - ADAM_B2 ** ADAM_STEP

    def body(g_ref, w_ref, m_ref, v_ref, go_ref, d_ref, mo_ref, vo_ref):
        gs = []
        for l in range(DEPTH):
            g = g_ref[l, 0].astype(F32)
            for s in range(1, N_DEV):
                g = g + g_ref[l, s].astype(F32)
            gs.append(g)
        g = jnp.stack(gs, axis=1)
        m_new = ADAM_B1 * m_ref[...] + (1.0 - ADAM_B1) * g
        v_new = ADAM_B2 * v_ref[...] + (1.0 - ADAM_B2) * (g * g)
        go_ref[...] = g
        mo_ref[...] = m_new
        vo_ref[...] = v_new
        d_ref[...] = -ADAM_LR * ((m_new / c1) / (jnp.sqrt(v_new / c2) + ADAM_EPS) + ADAM_WD * w_ref[...])

    tile = pl.BlockSpec((SHARD_COLS, DEPTH, LANE), lambda i: (0, 0, i))
    return pl.pallas_call(
        body, name=name, grid=(D_MODEL // LANE,),
        in_specs=[pl.BlockSpec((DEPTH, N_DEV, SHARD_COLS, LANE), lambda i: (0, 0, 0, i)), tile, tile, tile],
        out_specs=[tile] * 4, out_shape=[jax.ShapeDtypeStruct(w.shape, F32)] * 4,
        compiler_params=_cparams(("parallel",)),
    )(parts, w, m, v)


def _pack_small(conv, small):
    lead = conv.shape[:-2]
    flat = jnp.concatenate([conv.reshape(lead + (CS_CONV,))] + list(small), axis=-1)
    pad = CS_ROWS * LANE - flat.shape[-1]
    flat = jnp.concatenate([flat, jnp.zeros(lead + (pad,), F32)], axis=-1)
    return flat.reshape(lead + (CS_ROWS, LANE))


def _unpack_small(p):
    flat = p.reshape(DEPTH, CS_ROWS * LANE)
    conv = flat[:, :CS_CONV].reshape(DEPTH, CONV_K, CONV_SHARD_COLS)
    small, off = [], CS_CONV
    for _, n in SMALL_SIZES:
        small.append(flat[:, off:off + n])
        off += n
    return conv, small


def kernel(x, w_in, conv_w, a_log, dt_bias, norm_w, sinks, w_out, ln_g, ln_b, loss_target, m_w_in, m_conv_w, m_a_log, m_dt_bias, m_norm_w, m_sinks, m_w_out, m_ln_g, m_ln_b, v_w_in, v_conv_w, v_a_log, v_dt_bias, v_norm_w, v_sinks, v_w_out, v_ln_g, v_ln_b):
    small = [a_log, dt_bias, norm_w, sinks, ln_g, ln_b]
    w_t, m_t, v_t = (a.transpose(2, 0, 1) for a in (w_in, m_w_in, v_w_in))
    shards = [[w_t[:, l].astype(BF16), w_out[l].astype(BF16), conv_w[l]] for l in range(DEPTH)]
    g_in0, = _all_gather(shards[0][:1], name="weights_all_gather_0")
    weights = [[_full_w_in(g_in0, "w_in_rows_0"), None, None]] + [[None, None, None]] * (DEPTH - 1)

    _, saved, weights = _forward(x[0], weights, shards, small)
    dx, g1, _, loss_lanes = _backward_layer(1, None, saved[1], weights[1], ln_g[1],
                                            loss=(loss_target[0], ln_b[1][None, :]))
    loss = lax.psum(0.5 * jnp.sum(loss_lanes) * (1.0 / D_MODEL), ("x", "y", "c"))
    dx, _, (p_in, p_out, p_small), _ = _backward_layer(0, dx, saved[0], weights[0], ln_g[0],
                                                       above=_contributions(g1))

    o_in = [o.transpose(1, 2, 0) for o in _adamw_w_in(p_in, w_t, m_t, v_t, name="adamw_w_in")]
    o_out = _adamw(p_out, w_out, m_w_out, v_w_out, tr=OUT_SHARD_ROWS, name="adamw_w_out")
    o_small = _adamw(p_small, _pack_small(conv_w, small),
                     _pack_small(m_conv_w, [m_a_log, m_dt_bias, m_norm_w, m_sinks, m_ln_g, m_ln_b]),
                     _pack_small(v_conv_w, [v_a_log, v_dt_bias, v_norm_w, v_sinks, v_ln_g, v_ln_b]),
                     tr=CS_ROWS, name="adamw_small")
    outs = []
    for k in range(4):
        cv, sm = _unpack_small(o_small[k])
        outs += [o_in[k], cv, sm[0], sm[1], sm[2], sm[3], o_out[k], sm[4], sm[5]]
    return (loss, dx[None], *outs)
```

```python
import functools

import jax
import jax.numpy as jnp
from jax import lax
from jax.experimental import pallas as pl
from jax.experimental.pallas import tpu as pltpu

F32 = jnp.float32
BF16 = jnp.bfloat16
MM_DTYPE = BF16

N_DEV = 8
D_MODEL = 1024
DEPTH = 2
A_HEADS = 4
A_HEAD_DIM = 128
A_WIDTH = 512
CONV_K = 4
SUPER = 256
NEWTON_STEPS = 1
B_Q_HEADS = 8
B_KV_HEADS = 2
B_HEAD_DIM = 64
B_GROUP = 4
B_WIDTH = 512
WINDOW = 128
BLOCK = 128
IN_COLS = 3336
SHARD_COLS = IN_COLS // N_DEV
OUT_SHARD_ROWS = D_MODEL // N_DEV
CONV_SHARD_COLS = 3 * A_WIDTH // N_DEV
DEEPNORM_ALPHA = (2 * DEPTH) ** 0.25
LN_EPS = 1e-5
RMS_EPS = 1e-6
L2_EPS = 1e-6
ADAM_LR, ADAM_B1, ADAM_B2, ADAM_EPS, ADAM_WD, ADAM_STEP = 0.001, 0.9, 0.999, 1e-08, 0.01, 10

LANE = 128
L_QB, L_ZB, L_KB, L_VB, L_ZA, L_BA, L_QKV = 0, 512, 1024, 1152, 1280, 1792, 1920
L_SWA = 1280
L_GATE = 640
L_COLS = 3456
SMALL_SIZES = (("a_log", 4), ("dt_bias", 4), ("norm_w", 128), ("sinks", 8), ("ln_g", 1024), ("ln_b", 1024))
CS_CONV = CONV_K * CONV_SHARD_COLS
CS_ROWS = 24
VMEM_LIMIT = 48 * 1024 * 1024


def _cparams(sem=None):
    return pltpu.CompilerParams(dimension_semantics=sem, vmem_limit_bytes=VMEM_LIMIT)


def _mm(a, b):
    return jnp.dot(a.astype(MM_DTYPE), b.astype(MM_DTYPE), preferred_element_type=F32)


def _mm_nt(a, b):
    return lax.dot_general(a.astype(MM_DTYPE), b.astype(MM_DTYPE), (((1,), (1,)), ((), ())),
                           preferred_element_type=F32)


def _mm_tn(a, b):
    return lax.dot_general(a.astype(MM_DTYPE), b.astype(MM_DTYPE), (((0,), (0,)), ((), ())),
                           preferred_element_type=F32)


def _split(a):
    hi = a.astype(BF16)
    return hi, (a - hi.astype(F32)).astype(BF16)


def _silu(x):
    return x * jax.nn.sigmoid(x)


@jax.custom_vjp
def _stack(parts):
    return jnp.stack(parts)


_stack.defvjp(lambda parts: (jnp.stack(parts), None), lambda _, g: (tuple(g[i] for i in range(g.shape[0])),))


def _softplus(x):
    return jnp.maximum(x, 0.0) + jnp.log1p(jnp.exp(-jnp.abs(x)))


_ANY = pl.BlockSpec(memory_space=pl.ANY)


def _me():
    return lax.axis_index("x"), lax.axis_index("y"), lax.axis_index("c")


def _flat_id(pos):
    return 4 * pos[0] + 2 * pos[1] + pos[2]


def _remote(src, dst, send_sem, recv_sem, to):
    return pltpu.make_async_remote_copy(src_ref=src, dst_ref=dst, send_sem=send_sem, recv_sem=recv_sem,
                                        device_id=to, device_id_type=pl.DeviceIdType.MESH)


class _Direct:
    def __init__(self, items, bufs):
        self.items, self.bufs = list(items), list(bufs)
        self.n_src, self.n_buf = len(self.items), len(self.bufs)
        self.old = [j for j, b in enumerate(self.bufs) if not isinstance(b, jax.ShapeDtypeStruct)]
        self.args = [it[0] for it in self.items] + [self.bufs[j] for j in self.old]
        self.out_shape = [jax.ShapeDtypeStruct(b.shape, b.dtype) for b in self.bufs]
        self.scratch = [pltpu.SemaphoreType.DMA((self.n_src, N_DEV - 1)),
                        pltpu.SemaphoreType.DMA((self.n_src, N_DEV - 1)), pltpu.SemaphoreType.DMA((self.n_src,))]

    def aliases(self, in_base, out_base):
        return {in_base + self.n_src + pos: out_base + j for pos, j in enumerate(self.old)}

    def copies(self, in_refs, out_refs, sems):
        send_sems, recv_sems, local_sems = sems
        x, y, c = _me()
        me = _flat_id((x, y, c))
        peers = [(x ^ ((rel >> 2) & 1), y ^ ((rel >> 1) & 1), c ^ (rel & 1)) for rel in range(1, N_DEV)]
        local, sends, recvs = [], [], []
        for a, (_, per_dest, j, prefix, *rest) in enumerate(self.items):
            src = lambda d: in_refs[a].at[d] if per_dest else in_refs[a]
            dst = lambda s: out_refs[j].at[tuple(prefix) + (s,) + tuple(rest[0] if rest else ())]
            local.append(pltpu.make_async_copy(src(me), dst(me), local_sems.at[a]))
            for k, peer in enumerate(peers):
                pid = _flat_id(peer)
                sends.append(_remote(src(pid), dst(me), send_sems.at[a, k], recv_sems.at[a, k], peer))
                recvs.append(_remote(src(pid), dst(pid), send_sems.at[a, k], recv_sems.at[a, k], peer))
        return local, sends, recvs

    def start(self, in_refs, out_refs, sems):
        local, sends, _ = self.copies(in_refs, out_refs, sems)
        for cp in local + sends:
            cp.start()

    def wait(self, in_refs, out_refs, sems):
        local, sends, recvs = self.copies(in_refs, out_refs, sems)
        for cp in recvs:
            cp.wait_recv()
        for cp in sends:
            cp.wait_send()
        for cp in local:
            cp.wait()


def _pcall(core, *, name, grid, in_specs, out_specs, out_shape, args, sem, scratch_shapes=(), aliases=None,
           rider=None):
    n_in, n_out, n_scr = len(in_specs), len(out_specs), len(scratch_shapes)
    n_rin, n_rout = (len(rider.args), rider.n_buf) if rider else (0, 0)

    def body(*refs):
        ins, outs = refs[:n_in], refs[n_in + n_rin:n_in + n_rin + n_out]
        scr = refs[n_in + n_rin + n_out + n_rout:n_in + n_rin + n_out + n_rout + n_scr]
        if rider:
            r_refs = (refs[n_in:n_in + rider.n_src], refs[n_in + n_rin + n_out:n_in + n_rin + n_out + n_rout],
                      refs[n_in + n_rin + n_out + n_rout + n_scr:])
            ids = [pl.program_id(d) for d in range(len(grid))]
            first = functools.reduce(lambda p, q: p & q, [i == 0 for i in ids])
            last = functools.reduce(lambda p, q: p & q, [i == g - 1 for i, g in zip(ids, grid)])
            pl.when(first)(lambda: rider.start(*r_refs))
        core(ins, outs, scr)
        if rider:
            pl.when(last)(lambda: rider.wait(*r_refs))

    aliases = dict(aliases or {})
    if rider:
        sem = ("arbitrary",) * len(grid)
        aliases.update(rider.aliases(n_in, n_out))
    return pl.pallas_call(
        body, name=name, grid=grid, in_specs=list(in_specs) + [_ANY] * n_rin,
        out_specs=list(out_specs) + [_ANY] * n_rout,
        out_shape=list(out_shape) + (rider.out_shape if rider else []),
        scratch_shapes=list(scratch_shapes) + (rider.scratch if rider else []),
        input_output_aliases=aliases, compiler_params=_cparams(sem),
    )(*args, *(rider.args if rider else []))


def _exchange(direct, *, name):
    n_in = len(direct.args)

    def body(*refs):
        r_refs = refs[:direct.n_src], refs[n_in:n_in + direct.n_buf], refs[n_in + direct.n_buf:]
        direct.start(*r_refs)
        direct.wait(*r_refs)

    return pl.pallas_call(
        body, name=name, in_specs=[_ANY] * n_in, out_specs=[_ANY] * direct.n_buf, out_shape=direct.out_shape,
        input_output_aliases=direct.aliases(0, 0), scratch_shapes=direct.scratch,
    )(*direct.args)


def _matmul(a, b, *, form, tm, tn, tk, name, add=None, add_scale=1.0, rider=None, b_cols=None):
    if form == "nn":
        (m, kk), n = a.shape, b.shape[1]
        a_spec = pl.BlockSpec((tm, tk), lambda i, j, k: (i, k))
        b_spec = pl.BlockSpec((tk, tn), lambda i, j, k: (k, j))
        dn = (((1,), (0,)), ((), ()))
    elif form == "nt":
        (m, kk), n = a.shape, b.shape[0]
        a_spec = pl.BlockSpec((tm, tk), lambda i, j, k: (i, k))
        b_spec = pl.BlockSpec((tn, tk), lambda i, j, k: (j, k))
        dn = (((1,), (1,)), ((), ()))
    else:
        kk, m = a.shape
        n0, n = b_cols or (0, b.shape[1])
        assert n0 % tn == 0
        a_spec = pl.BlockSpec((tk, tm), lambda i, j, k: (k, i))
        b_spec = pl.BlockSpec((tk, tn), lambda i, j, k: (k, j + n0 // tn))
        dn = (((0,), (0,)), ((), ()))
    assert m % tm == 0 and n % tn == 0 and kk % tk == 0, (name, m, n, kk)
    has_add = add is not None

    def core(ins, outs, _):
        a_ref, b_ref = ins[:2]
        o_ref = outs[0]
        k = pl.program_id(2)
        p = lax.dot_general(a_ref[...].astype(MM_DTYPE), b_ref[...].astype(MM_DTYPE), dn,
                            preferred_element_type=F32)

        @pl.when(k == 0)
        def _():
            o_ref[...] = p + add_scale * ins[2][...] if has_add else p

        @pl.when(k > 0)
        def _():
            o_ref[...] += p

    in_specs = [a_spec, b_spec]
    args = [a, b]
    if has_add:
        in_specs.append(pl.BlockSpec((tm, tn), lambda i, j, k: (i, j)))
        args.append(add)
    res = _pcall(core, name=name, grid=(m // tm, n // tn, kk // tk), in_specs=in_specs,
                 out_specs=[pl.BlockSpec((tm, tn), lambda i, j, k: (i, j))],
                 out_shape=[jax.ShapeDtypeStruct((m, n), F32)], args=args,
                 sem=("parallel", "parallel", "arbitrary"), rider=rider)
    return res if rider else res[0]


ZERO_TAIL = 8


def _with_tail(x):
    return jnp.concatenate([x, jnp.zeros((ZERO_TAIL,) + x.shape[1:], x.dtype)], axis=0)


def _shift_down(x, k):
    return pltpu.roll(x, k, 0)


def _shift_up(x, k):
    return pltpu.roll(x, x.shape[0] - k, 0)


def _conv_slab(x, w):
    return w[3:4] * x + w[2:3] * _shift_down(x, 1) + w[1:2] * _shift_down(x, 2) + w[0:1] * _shift_down(x, 3)


def _prep_fwd(h, conv_w, *, name):
    t_len = h.shape[0]

    def body(x_ref, w_ref, o_ref):
        s = pl.program_id(0)
        y = _silu(_conv_slab(_with_tail(x_ref[...]), w_ref[...])[:t_len])
        rs = lax.rsqrt(jnp.sum(y * y, axis=-1, keepdims=True) + L2_EPS)
        scale = jnp.where(s < A_HEADS, A_HEAD_DIM ** -0.5, 1.0)
        o_ref[...] = jnp.where(s < 2 * A_HEADS, y * rs * scale, y)

    return pl.pallas_call(
        body, name=name, grid=(12,),
        in_specs=[pl.BlockSpec((t_len, LANE), lambda s: (0, L_QKV // LANE + s)),
                  pl.BlockSpec((8, LANE), lambda s: (0, s))],
        out_specs=pl.BlockSpec((t_len, LANE), lambda s: (0, s)),
        out_shape=jax.ShapeDtypeStruct((t_len, 3 * A_WIDTH), F32),
        compiler_params=_cparams(("parallel",)),
    )(h, conv_w)


def _prep_bwd(h, conv_w, d_out, dh, *, name):
    t_len = h.shape[0]

    def body(x_ref, w_ref, g_ref, dh_in, dx_ref, dw_ref):
        del dh_in
        s = pl.program_id(0)
        x = _with_tail(x_ref[...])
        g = _with_tail(g_ref[0])
        w = w_ref[...]
        xs = [_shift_down(x, 3), _shift_down(x, 2), _shift_down(x, 1), x]
        c = w[0:1] * xs[0] + w[1:2] * xs[1] + w[2:3] * xs[2] + w[3:4] * xs[3]
        sg = jax.nn.sigmoid(c)
        y = c * sg
        rs = lax.rsqrt(jnp.sum(y * y, axis=-1, keepdims=True) + L2_EPS)
        scale = jnp.where(s < A_HEADS, A_HEAD_DIM ** -0.5, 1.0)
        dy_n = scale * (rs * g - y * (rs * rs * rs) * jnp.sum(g * y, axis=-1, keepdims=True))
        dy = jnp.where(s < 2 * A_HEADS, dy_n, g)
        dc = dy * (sg * (1.0 + c * (1.0 - sg)))
        dx = w[3:4] * dc + w[2:3] * _shift_up(dc, 1) + w[1:2] * _shift_up(dc, 2) + w[0:1] * _shift_up(dc, 3)
        dx_ref[...] = dx[:t_len].astype(dx_ref.dtype)
        dws = [jnp.sum(dc * xs[j], axis=0, keepdims=True) for j in range(CONV_K)]
        dw_ref[...] = jnp.concatenate(dws + [jnp.zeros((8 - CONV_K, LANE), F32)], axis=0)

    slab = pl.BlockSpec((t_len, LANE), lambda s: (0, L_QKV // LANE + s))
    return pl.pallas_call(
        body, name=name, grid=(12,),
        in_specs=[slab, pl.BlockSpec((8, LANE), lambda s: (0, s)),
                  pl.BlockSpec((1, t_len, LANE), lambda s: (s // A_HEADS, 0, s % A_HEADS)), _ANY],
        out_specs=[slab, pl.BlockSpec((8, LANE), lambda s: (0, s))],
        out_shape=[jax.ShapeDtypeStruct((t_len, L_COLS), MM_DTYPE), jax.ShapeDtypeStruct((8, 3 * A_WIDTH), F32)],
        input_output_aliases={3: 0},
        compiler_params=_cparams(("parallel",)),
    )(h, conv_w, d_out, dh)


N_LEVELS = 5
MF_TRIL, MF_STRIL, MF_DIAG8, MF_LOW16, MF_EYE = 0, 1, 2, 3, 3 + N_LEVELS
MB_CUM, MB_CUM_T, MB_TOT = 0, 1, 2


def _gdn_masks():
    r = lax.broadcasted_iota(jnp.int32, (SUPER, SUPER), 0)
    c = lax.broadcasted_iota(jnp.int32, (SUPER, SUPER), 1)
    same = lambda shift: (r >> shift) == (c >> shift)
    ninf = lambda m: jnp.where(m, 0.0, -jnp.inf).astype(F32)
    one = lambda m: m.astype(F32)
    mf = jnp.stack([ninf(r >= c), ninf(r > c), one(same(3))]
                   + [one(same(4 + lv) & jnp.logical_not(same(3 + lv))) for lv in range(N_LEVELS)] + [one(r == c)])
    mb = jnp.stack([one(r >= c), one(r <= c), jnp.ones((SUPER, SUPER), F32)]).astype(BF16)
    return mf, mb


def _tri_inv_impl(a, mf):
    d = lambda p, q: jnp.dot(p.astype(BF16), q.astype(BF16), preferred_element_type=F32)
    dd = lambda p, q: jnp.dot(p, q, preferred_element_type=F32)
    eye = mf[MF_EYE]
    a0 = a * mf[MF_DIAG8]
    a2 = d(a0, a0)
    a4 = d(a2, a2)
    t = d(d(eye - a0, eye + a2), eye + a4)
    for level in range(N_LEVELS):
        t = t - d(d(t, a * mf[MF_LOW16 + level]), t)
    a_hi, a_lo = _split(a)
    for _ in range(NEWTON_STEPS):
        t0 = t.astype(BF16)
        t0f = t0.astype(F32)
        resid = (eye - t0f) - (dd(a_hi, t0) + dd(a_lo, t0))
        r_hi, r_lo = _split(resid)
        t = t0f + (dd(t0, r_hi) + dd(t0, r_lo))
    return t


@jax.custom_vjp
def _wy_apply(a, rhs, t):
    return _mm(t, rhs)


def _wy_apply_fwd(a, rhs, t):
    x = _mm(t, rhs)
    return x, (t, x)


def _wy_apply_bwd(res, dx):
    t, x = res
    d_rhs = _mm_tn(t, dx)
    return -_mm_nt(d_rhs, x), d_rhs, jnp.zeros_like(t)


_wy_apply.defvjp(_wy_apply_fwd, _wy_apply_bwd)


@functools.partial(jax.custom_vjp, nondiff_argnums=(1,))
def _lane_roll(x, shift):
    return pltpu.roll(x, shift % LANE, 1)


_lane_roll.defvjp(lambda x, shift: (_lane_roll(x, shift), None), lambda shift, _, g: (_lane_roll(g, -shift),))


def _mask_times_lanes(x, mask):
    lane = lax.broadcasted_iota(jnp.int32, (1, LANE), 1)
    x = jnp.where(lane < A_HEADS, x, 0.0)
    x1 = x.astype(BF16).astype(F32)
    x2 = (x - x1).astype(BF16).astype(F32)
    x3 = (x - x1 - x2).astype(BF16).astype(F32)
    pieces = x1 + pltpu.roll(x2, A_HEADS, 1) + pltpu.roll(x3, 2 * A_HEADS, 1)
    res = jnp.dot(mask, pieces.astype(BF16), preferred_element_type=F32)
    return res + pltpu.roll(res, LANE - A_HEADS, 1) + pltpu.roll(res, LANE - 2 * A_HEADS, 1)


@jax.custom_vjp
def _chunk_sums(g, mb):
    return _mask_times_lanes(g, mb[MB_CUM]), _mask_times_lanes(g, mb[MB_TOT])


def _chunk_sums_fwd(g, mb):
    return _chunk_sums(g, mb), mb


def _chunk_sums_bwd(mb, d):
    lane = lax.broadcasted_iota(jnp.int32, (1, LANE), 1)
    dg = _mask_times_lanes(d[0], mb[MB_CUM_T]) + _mask_times_lanes(d[1], mb[MB_TOT])
    return jnp.where(lane < A_HEADS, dg, 0.0), jnp.zeros_like(mb)


_chunk_sums.defvjp(_chunk_sums_fwd, _chunk_sums_bwd)


def _gdn_gates(ba, alog, dtb, mb):
    beta = jax.nn.sigmoid(ba)
    g = -jnp.exp(alog) * _softplus(_lane_roll(ba, -A_HEADS) + dtb)
    gc, gl = _chunk_sums(g, mb)
    return beta, gc, gl, gc.T


def _gdn_block(s, q, k, v, z, gates, nw, h, t_known, mf):
    n = q.shape[0]
    beta_all, gc_all, gl_all, gct_all = gates
    lane = lax.broadcasted_iota(jnp.int32, (1, LANE), 1)
    sub = lax.broadcasted_iota(jnp.int32, (LANE, 1), 0)
    col = lambda x: jnp.sum(jnp.where(lane == h, x, 0.0), axis=1, keepdims=True)
    wide = lambda c: jnp.broadcast_to(c, (n, LANE))
    gc, gl = col(gc_all), col(gl_all)
    gc_row = jnp.sum(jnp.where(sub == h, gct_all, 0.0), axis=0, keepdims=True)
    beta_w, eg_w = wide(col(beta_all)), wide(jnp.exp(gc))
    diff = gc - gc_row
    decay = jnp.exp(diff + mf[MF_TRIL])
    kb = k * beta_w
    a_mat = _mm_nt(kb, k) * jnp.exp(diff + mf[MF_STRIL])
    rhs = jnp.concatenate([v * beta_w, kb * eg_w], axis=1)
    if t_known is None:
        t_mat = _tri_inv_impl(a_mat, mf)
        uw = _mm(t_mat, rhs)
    else:
        t_mat = t_known
        uw = _wy_apply(a_mat, rhs, t_known)
    u, w = uw[:, :LANE], uw[:, LANE:]
    qk = _mm_nt(q, k) * decay
    q_dec = q * eg_w
    k_dec = k * wide(jnp.exp(gl - gc))
    v_new = u - _mm(w, s)
    o = _mm(q_dec, s) + _mm(qk, v_new)
    s = s * jnp.exp(gl[0:1]) + _mm_tn(k_dec, v_new)
    o = o * lax.rsqrt(jnp.mean(o * o, axis=-1, keepdims=True) + RMS_EPS) * nw
    return o * _silu(z), s, t_mat


def _gdn_fwd(qkv, h, alog, dtb, nw, ycat, *, name, rider=None):
    t_len = qkv.shape[0]
    nsc = t_len // SUPER

    def core(ins, outs, scr):
        q_ref, k_ref, v_ref, gate_ref, al_ref, dt_ref, nw_ref, mf_ref, mb_ref, _ = ins
        y_ref, sin_ref, t_ref = outs
        s_scr, = scr

        @pl.when(pl.program_id(0) == 0)
        def _():
            s_scr[...] = jnp.zeros_like(s_scr)

        per_head = lambda ref: jnp.stack([ref[:, hh * LANE:(hh + 1) * LANE] for hh in range(A_HEADS)])
        states = s_scr[...]
        gates = _gdn_gates(gate_ref[:, A_WIDTH:], al_ref[...], dt_ref[...], mb_ref[...])
        fn = jax.vmap(_gdn_block, in_axes=(0, 0, 0, 0, 0, None, None, 0, None, None))
        y, s_new, t_mat = fn(states, per_head(q_ref), per_head(k_ref), per_head(v_ref), per_head(gate_ref),
                             gates, nw_ref[...], jnp.arange(A_HEADS), None, mf_ref[...])
        sin_ref[0] = states
        t_ref[0] = t_mat.astype(t_ref.dtype)
        s_scr[...] = s_new
        for hh in range(A_HEADS):
            y_ref[:, hh * LANE:(hh + 1) * LANE] = y[hh].astype(y_ref.dtype)

    blk = lambda j: pl.BlockSpec((SUPER, A_WIDTH), lambda sc: (sc, j))
    row = pl.BlockSpec((1, LANE), lambda sc: (0, 0))
    mf, mb = _gdn_masks()
    whole = lambda a: pl.BlockSpec(a.shape, lambda sc: (0, 0, 0))
    return _pcall(
        core, name=name, grid=(nsc,),
        in_specs=[blk(0), blk(1), blk(2), pl.BlockSpec((SUPER, L_GATE), lambda sc: (sc, L_ZA // L_GATE)),
                  row, row, row, whole(mf), whole(mb), _ANY],
        out_specs=[blk(0),
                   pl.BlockSpec((1, A_HEADS, A_HEAD_DIM, A_HEAD_DIM), lambda sc: (sc, 0, 0, 0)),
                   pl.BlockSpec((1, A_HEADS, SUPER, SUPER), lambda sc: (sc, 0, 0, 0))],
        out_shape=[jax.ShapeDtypeStruct((t_len, D_MODEL), MM_DTYPE),
                   jax.ShapeDtypeStruct((nsc, A_HEADS, A_HEAD_DIM, A_HEAD_DIM), F32),
                   jax.ShapeDtypeStruct((nsc, A_HEADS, SUPER, SUPER), MM_DTYPE)],
        scratch_shapes=[pltpu.VMEM((A_HEADS, A_HEAD_DIM, A_HEAD_DIM), F32)],
        aliases={9: 0}, sem=("arbitrary",), rider=rider,
        args=(qkv, qkv, qkv, h, alog, dtb, nw, mf, mb, ycat))


def _gdn_bwd(qkv, h, alog, dtb, nw, s_in, t_in, dycat, dh, *, name, rider=None):
    t_len = qkv.shape[0]
    nsc = t_len // SUPER

    def core(ins, outs, scr):
        q_ref, k_ref, v_ref, gate_ref, al_ref, dt_ref, nw_ref, sin_ref, t_ref, dy_ref, mf_ref, mb_ref, _ = ins
        dgate_ref, dqkv_ref, dal_ref, ddt_ref, dnw_ref = outs
        ds_scr, = scr

        @pl.when(pl.program_id(0) == 0)
        def _():
            ds_scr[...] = jnp.zeros_like(ds_scr)
            dal_ref[...] = jnp.zeros_like(dal_ref)
            ddt_ref[...] = jnp.zeros_like(ddt_ref)
            dnw_ref[...] = jnp.zeros_like(dnw_ref)

        per_head = lambda ref: jnp.stack([ref[:, hh * LANE:(hh + 1) * LANE] for hh in range(A_HEADS)])
        head_ids = jnp.arange(A_HEADS)
        t_known, mf, mb = t_ref[0], mf_ref[...], mb_ref[...]

        def fn(s, q, k, v, z, ba, alog, dtb, nw):
            gates = _gdn_gates(ba, alog, dtb, mb)
            one = lambda s, q, k, v, z, t, h: _gdn_block(s, q, k, v, z, gates, nw, h, t, mf)[:2]
            return jax.vmap(one)(s, q, k, v, z, t_known, head_ids)

        _, vjp = jax.vjp(fn, sin_ref[0], per_head(q_ref), per_head(k_ref), per_head(v_ref), per_head(gate_ref),
                         gate_ref[:, A_WIDTH:], al_ref[...], dt_ref[...], nw_ref[...])
        ds, dq, dk, dv, dz, dba, dal, ddt, dnw = vjp((per_head(dy_ref), ds_scr[...]))
        ds_scr[...] = ds
        for hh in range(A_HEADS):
            cols = slice(hh * LANE, (hh + 1) * LANE)
            dqkv_ref[0, :, cols] = dq[hh]
            dqkv_ref[1, :, cols] = dk[hh]
            dqkv_ref[2, :, cols] = dv[hh]
            dgate_ref[:, cols] = dz[hh].astype(dgate_ref.dtype)
        dgate_ref[:, A_WIDTH:] = dba.astype(dgate_ref.dtype)
        dal_ref[...] += dal
        ddt_ref[...] += ddt
        dnw_ref[...] += dnw

    rev = lambda i: nsc - 1 - i
    blk = lambda j: pl.BlockSpec((SUPER, A_WIDTH), lambda i: (rev(i), j))
    gate = pl.BlockSpec((SUPER, L_GATE), lambda i: (rev(i), L_ZA // L_GATE))
    row = pl.BlockSpec((1, LANE), lambda i: (0, 0))
    mf, mb = _gdn_masks()
    whole = lambda a: pl.BlockSpec(a.shape, lambda i: (0, 0, 0))
    return _pcall(
        core, name=name, grid=(nsc,),
        in_specs=[blk(0), blk(1), blk(2), gate, row, row, row,
                  pl.BlockSpec((1, A_HEADS, A_HEAD_DIM, A_HEAD_DIM), lambda i: (rev(i), 0, 0, 0)),
                  pl.BlockSpec((1, A_HEADS, SUPER, SUPER), lambda i: (rev(i), 0, 0, 0)),
                  blk(0), whole(mf), whole(mb), _ANY],
        out_specs=[gate, pl.BlockSpec((3, SUPER, A_WIDTH), lambda i: (0, rev(i), 0)), row, row, row],
        out_shape=[jax.ShapeDtypeStruct((t_len, L_COLS), MM_DTYPE), jax.ShapeDtypeStruct((3, t_len, A_WIDTH), F32)]
        + [jax.ShapeDtypeStruct((1, LANE), F32)] * 3,
        scratch_shapes=[pltpu.VMEM((A_HEADS, A_HEAD_DIM, A_HEAD_DIM), F32)],
        aliases={12: 0}, sem=("arbitrary",), rider=rider,
        args=(qkv, qkv, qkv, h, alog, dtb, nw, s_in, t_in, dycat, mf, mb, dh))


Q_BLOCKS = 4
Q_ROWS = Q_BLOCKS * BLOCK


def _swa_block(q, kp, kc, vp, vc, z, sinks, first):
    rows = B_GROUP * BLOCK
    ri = lax.broadcasted_iota(jnp.int32, (rows, 2 * BLOCK), 0)
    si = lax.broadcasted_iota(jnp.int32, (rows, 2 * BLOCK), 1)
    dist = (ri & (BLOCK - 1)) + BLOCK - si
    bias = jnp.where((dist >= 0) & (dist < WINDOW), 0.0, -jnp.inf)
    no_prev = jnp.where(first & (si[:1] < BLOCK), -jnp.inf, 0.0)
    dist_f = dist.astype(F32)
    head_of_row = lax.broadcasted_iota(jnp.int32, (rows, 1), 0) >> 7
    keys = jnp.concatenate([kp, kc], axis=0)
    vals = jnp.concatenate([vp, vc], axis=0)

    def item(b, j):
        cs = slice(j * B_HEAD_DIM, (j + 1) * B_HEAD_DIM)
        rs = slice(b * BLOCK, (b + 1) * BLOCK)
        heads = range(j * B_GROUP, (j + 1) * B_GROUP)
        qs = jnp.concatenate([q[rs, hq * B_HEAD_DIM:(hq + 1) * B_HEAD_DIM] for hq in heads], axis=0) * (
            B_HEAD_DIM ** -0.5)
        kk = keys[b * BLOCK:(b + 2) * BLOCK, cs]
        vv = vals[b * BLOCK:(b + 2) * BLOCK, cs]
        sink = jnp.concatenate([jnp.broadcast_to(sinks[:, hq:hq + 1], (BLOCK, 1)) for hq in heads], axis=0)
        slope = sum(jnp.where(head_of_row == gi, 2.0 ** (-8.0 * (hq + 1) / B_Q_HEADS), 0.0)
                    for gi, hq in enumerate(heads))
        return qs, kk, vv, sink, slope, (no_prev if b == 0 else jnp.zeros_like(no_prev))

    def attend(qs, kk, vv, sink, slope, hide):
        sc = _mm_nt(qs, kk) - slope * dist_f + (bias + hide)
        m = lax.stop_gradient(jnp.maximum(jnp.max(sc, axis=-1, keepdims=True), sink))
        p = jnp.exp(sc - m)
        inv = 1.0 / (jnp.sum(p, axis=-1, keepdims=True) + jnp.exp(sink - m))
        return _mm(p * inv, vv)

    items = [(b, j) for b in range(Q_BLOCKS) for j in range(B_KV_HEADS)]
    o = jax.vmap(attend)(*[_stack(t) for t in zip(*[item(b, j) for b, j in items])])
    rows_out = [jnp.concatenate([o[b * B_KV_HEADS + j, gi * BLOCK:(gi + 1) * BLOCK]
                                 for j in range(B_KV_HEADS) for gi in range(B_GROUP)], axis=1)
                for b in range(Q_BLOCKS)]
    return jnp.concatenate(rows_out, axis=0) * _silu(z)


def _swa_specs(idx):
    wide = lambda off: pl.BlockSpec((Q_ROWS, B_WIDTH), lambda n: (idx(n), off))
    cur = lambda off: pl.BlockSpec((Q_ROWS, LANE), lambda n: (idx(n), off))
    prev = lambda off: pl.BlockSpec((BLOCK, LANE), lambda n: (jnp.maximum(idx(n) * Q_BLOCKS - 1, 0), off))
    return [wide(L_QB // B_WIDTH), prev(L_KB // LANE), cur(L_KB // LANE), prev(L_VB // LANE), cur(L_VB // LANE),
            wide(L_ZB // B_WIDTH), pl.BlockSpec((1, LANE), lambda n: (0, 0))]


def _swa_fwd(h, sinks, *, name, rider=None):
    t_len = h.shape[0]
    nb = t_len // Q_ROWS

    def core(ins, outs, _):
        q_ref, kp_ref, kc_ref, vp_ref, vc_ref, z_ref, s_ref = ins
        outs[0][...] = _swa_block(q_ref[...], kp_ref[...], kc_ref[...], vp_ref[...], vc_ref[...], z_ref[...],
                                  s_ref[...], pl.program_id(0) == 0).astype(outs[0].dtype)

    res = _pcall(core, name=name, grid=(nb,), in_specs=_swa_specs(lambda n: n),
                 out_specs=[pl.BlockSpec((Q_ROWS, B_WIDTH), lambda n: (n, 1))],
                 out_shape=[jax.ShapeDtypeStruct((t_len, D_MODEL), MM_DTYPE)], sem=("parallel",), rider=rider,
                 args=(h, h, h, h, h, h, sinks))
    return res if rider else res[0]


def _swa_bwd(h, sinks, dycat, *, name, rider=None):
    t_len = h.shape[0]
    nb = t_len // Q_ROWS
    early = slice(0, Q_ROWS - BLOCK)
    last = slice(Q_ROWS - BLOCK, Q_ROWS)

    def core(ins, outs, scr):
        q_ref, kp_ref, kc_ref, vp_ref, vc_ref, z_ref, s_ref, dy_ref = ins
        dh_ref, dsk_ref = outs
        ck_scr, cv_scr = scr
        i = pl.program_id(0)
        n = nb - 1 - i

        @pl.when(i == 0)
        def _():
            ck_scr[...] = jnp.zeros_like(ck_scr)
            cv_scr[...] = jnp.zeros_like(cv_scr)
            dsk_ref[...] = jnp.zeros_like(dsk_ref)

        fn = functools.partial(_swa_block, first=(n == 0))
        _, vjp = jax.vjp(fn, q_ref[...], kp_ref[...], kc_ref[...], vp_ref[...], vc_ref[...], z_ref[...], s_ref[...])
        dq, dkp, dkc, dvp, dvc, dz, dsk = vjp(dy_ref[...])
        def put(rows, col, val):
            dh_ref[rows, col:col + val.shape[1]] = val.astype(dh_ref.dtype)

        put(slice(None), L_QB, dq)
        put(slice(None), L_ZB, dz)
        put(early, L_KB, dkc[early])
        put(early, L_VB, dvc[early])
        put(last, L_KB, dkc[last] + ck_scr[...])
        put(last, L_VB, dvc[last] + cv_scr[...])
        ck_scr[...] = dkp
        cv_scr[...] = dvp
        dsk_ref[...] += dsk

    rev = lambda i: nb - 1 - i
    return _pcall(
        core, name=name, grid=(nb,),
        in_specs=_swa_specs(rev) + [pl.BlockSpec((Q_ROWS, B_WIDTH), lambda i: (rev(i), 1))],
        out_specs=[pl.BlockSpec((Q_ROWS, L_SWA), lambda i: (rev(i), 0)), pl.BlockSpec((1, LANE), lambda i: (0, 0))],
        out_shape=[jax.ShapeDtypeStruct((t_len, L_COLS), MM_DTYPE), jax.ShapeDtypeStruct((1, LANE), F32)],
        scratch_shapes=[pltpu.VMEM((BLOCK, LANE), F32), pltpu.VMEM((BLOCK, LANE), F32)],
        sem=("arbitrary",), rider=rider, args=(h, h, h, h, h, h, sinks, dycat))


def _out_ln_fwd(ycat, w_out, x, ln_g, ln_b, w_in_next, *, name, tm=512):
    t_len = x.shape[0]

    def body(y_ref, w_ref, x_ref, g_ref, b_ref, win_ref, r_ref, o_ref, h_ref):
        r = DEEPNORM_ALPHA * x_ref[...] + _mm(y_ref[...], w_ref[...])
        r_ref[...] = r
        mu = jnp.mean(r, axis=-1, keepdims=True)
        d = r - mu
        var = jnp.mean(d * d, axis=-1, keepdims=True)
        xn = d * lax.rsqrt(var + LN_EPS) * g_ref[...] + b_ref[...]
        o_ref[...] = xn
        h_ref[...] = _mm_nt(xn, win_ref[...])

    tile = pl.BlockSpec((tm, D_MODEL), lambda i: (i, 0))
    vec = pl.BlockSpec((1, D_MODEL), lambda i: (0, 0))
    tile_shape = jax.ShapeDtypeStruct((t_len, D_MODEL), F32)
    return pl.pallas_call(
        body, name=name, grid=(t_len // tm,),
        in_specs=[tile, pl.BlockSpec((D_MODEL, D_MODEL), lambda i: (0, 0)), tile, vec, vec,
                  pl.BlockSpec((L_COLS, D_MODEL), lambda i: (0, 0))],
        out_specs=[tile, tile, pl.BlockSpec((tm, L_COLS), lambda i: (i, 0))],
        out_shape=[tile_shape, tile_shape, jax.ShapeDtypeStruct((t_len, L_COLS), F32)],
        compiler_params=_cparams(("parallel",)),
    )(ycat, w_out, x, ln_g, ln_b, w_in_next)


def _ln_out_bwd(dxn, r, ln_g, ycat, w_out, *, name, tm=512, loss=None):
    t_len = r.shape[0]
    above = isinstance(dxn, tuple)
    n_lead = 4 if loss else 5 if above else 3

    def body(*refs):
        lead, (y_ref, w_ref), outs = refs[:n_lead], refs[n_lead:n_lead + 2], refs[n_lead + 2:]
        if loss:
            t_ref, r_ref, g_ref, b_ref = lead
            dr_ref, dg_ref, db_ref, l_ref, dy_ref, dw_ref = outs
        else:
            *dx_refs, r_ref, g_ref = lead
            dr_ref, dg_ref, db_ref, dy_ref, dw_ref = outs

        @pl.when(pl.program_id(0) == 0)
        def _():
            dg_ref[...] = jnp.zeros_like(dg_ref)
            db_ref[...] = jnp.zeros_like(db_ref)
            dw_ref[...] = jnp.zeros_like(dw_ref)
            if loss:
                l_ref[...] = jnp.zeros_like(l_ref)

        rr = r_ref[...]
        if loss:
            rr = DEEPNORM_ALPHA * rr + _mm(y_ref[...], w_ref[...])
        mu = jnp.mean(rr, axis=-1, keepdims=True)
        d = rr - mu
        rstd = lax.rsqrt(jnp.mean(d * d, axis=-1, keepdims=True) + LN_EPS)
        xh = d * rstd
        if loss:
            e = (xh * g_ref[...] + b_ref[...]) - t_ref[...]
            dx = e * (1.0 / D_MODEL)
            l_ref[...] += jnp.sum(e * e, axis=0, keepdims=True)
        elif above:
            dh_ref, win_ref, add_ref = dx_refs
            dx = _mm(dh_ref[...], win_ref[...]) + DEEPNORM_ALPHA * add_ref[...]
        else:
            dx = dx_refs[0][...]
        dxh = dx * g_ref[...]
        dr = rstd * (dxh - jnp.mean(dxh, axis=-1, keepdims=True) - xh * jnp.mean(dxh * xh, axis=-1, keepdims=True))
        dr_ref[...] = dr
        dg_ref[...] += jnp.sum(dx * xh, axis=0, keepdims=True)
        db_ref[...] += jnp.sum(dx, axis=0, keepdims=True)
        dy_ref[...] = _mm_nt(dr, w_ref[...])
        dw_ref[...] += _mm_tn(y_ref[...], dr)

    tile = pl.BlockSpec((tm, D_MODEL), lambda i: (i, 0))
    vec = pl.BlockSpec((1, D_MODEL), lambda i: (0, 0))
    square = pl.BlockSpec((D_MODEL, D_MODEL), lambda i: (0, 0))
    tile_shape = jax.ShapeDtypeStruct((t_len, D_MODEL), F32)
    vec_shape = jax.ShapeDtypeStruct((1, D_MODEL), F32)
    if loss:
        args, lead_specs = (loss[0], r, ln_g, loss[1]), [tile, tile, vec, vec]
    elif above:
        args = (*dxn, r, ln_g)
        lead_specs = [pl.BlockSpec((tm, L_COLS), lambda i: (i, 0)), pl.BlockSpec((L_COLS, D_MODEL), lambda i: (0, 0)),
                      tile, tile, vec]
    else:
        args, lead_specs = (dxn, r, ln_g), [tile, tile, vec]
    return pl.pallas_call(
        body, name=name, grid=(t_len // tm,),
        in_specs=lead_specs + [tile, square],
        out_specs=[tile, vec, vec] + ([vec] if loss else []) + [tile, square],
        out_shape=[tile_shape, vec_shape, vec_shape] + ([vec_shape] if loss else [])
        + [tile_shape, jax.ShapeDtypeStruct((D_MODEL, D_MODEL), F32)],
        compiler_params=_cparams(("arbitrary",)),
    )(*args, ycat, w_out)


def _pad_row(v):
    return jnp.zeros((1, LANE), F32).at[0, :v.shape[0]].set(v)


_REGIONS = ((0, 1536, L_QKV), (1536, 2048, L_ZA), (2048, 2056, L_BA), (2056, 2568, L_QB), (2568, 2696, L_KB),
            (2696, 2824, L_VB), (2824, 3336, L_ZB))


def _shard_pieces(regions):
    for a, b, off in regions:
        for d in range(N_DEV):
            lo, hi = max(a, d * SHARD_COLS), min(b, (d + 1) * SHARD_COLS)
            if lo < hi:
                yield d, lo - d * SHARD_COLS, hi - d * SHARD_COLS, off + lo - a


def _as_list(r):
    return list(r) if isinstance(r, (list, tuple)) else [r]


def _gathered(shard):
    return jax.ShapeDtypeStruct((N_DEV,) + shard.shape, shard.dtype)


def _full_w_in(g_in, name):
    by_offset = sorted(_shard_pieces(_REGIONS), key=lambda p: p[3])
    tc = 256

    def body(g_ref, o_ref):
        pieces, row = [], 0
        for d, lo, hi, off in by_offset + [(None, 0, 0, L_COLS)]:
            if off > row:
                pieces.append(jnp.zeros((off - row, tc), g_ref.dtype))
            if d is not None:
                pieces.append(g_ref[d, lo:hi, :])
            row = off + hi - lo
        o_ref[...] = jnp.concatenate(pieces, axis=0)

    return pl.pallas_call(
        body, name=name, grid=(D_MODEL // tc,),
        in_specs=[pl.BlockSpec((N_DEV, SHARD_COLS, tc), lambda i: (0, 0, i))],
        out_specs=pl.BlockSpec((L_COLS, tc), lambda i: (0, i)),
        out_shape=jax.ShapeDtypeStruct((L_COLS, D_MODEL), g_in.dtype),
        compiler_params=_cparams(("parallel",)),
    )(g_in)


def _full_conv(g_conv):
    return jnp.pad(g_conv.transpose(1, 0, 2).reshape(CONV_K, 3 * A_WIDTH), ((0, 8 - CONV_K), (0, 0)))


def _forward(x, weights, shards, small):
    a_log, dt_bias, norm_w, sinks, ln_g, ln_b = small
    tm = min(512, x.shape[0])
    saved, weights = [], [list(w) for w in weights]
    whole = lambda arrs: _Direct([(a, False, j, ()) for j, a in enumerate(arrs)], [_gathered(a) for a in arrs])
    h = None
    for l in range(DEPTH):
        if h is None:
            rider = whole(shards[l][1:]) if weights[l][1] is None else None
            h, *got = _as_list(_matmul(x, weights[l][0], form="nt", tm=tm, tn=L_COLS, tk=D_MODEL,
                                       name=f"in_proj_{l}", rider=rider))
            if rider:
                weights[l][1:] = [got[0].reshape(D_MODEL, D_MODEL), _full_conv(got[1])]
        w_in_l, w_out_l, conv_l = weights[l]
        qkv = _prep_fwd(h, conv_l, name=f"prep_fwd_{l}")
        al, dt, nw, sk = _pad_row(a_log[l]), _pad_row(dt_bias[l]), norm_w[l][None, :], _pad_row(sinks[l])
        ahead = l + 1 < DEPTH and weights[l + 1][0] is None
        rider = whole(shards[l + 1][1:]) if ahead else None
        ycat, *got = _as_list(_swa_fwd(h, sk, name=f"swa_fwd_{l}", rider=rider))
        if ahead:
            weights[l + 1][1:] = [got[0].reshape(D_MODEL, D_MODEL), _full_conv(got[1])]
        rider = whole(shards[l + 1][:1]) if ahead else None
        ycat, s_in, t_in, *got = _gdn_fwd(qkv, h, al, dt, nw, ycat, name=f"gdn_fwd_{l}", rider=rider)
        if ahead:
            weights[l + 1][0] = _full_w_in(got[0], f"w_in_rows_{l + 1}")
        r, xn, h_next = None, None, None
        if l + 1 < DEPTH:
            r, xn, h_next = _out_ln_fwd(ycat, w_out_l, x, ln_g[l][None, :], ln_b[l][None, :], weights[l + 1][0],
                                        name=f"out_ln_{l}")
        saved.append((x, h, qkv, s_in, t_in, ycat, r, al, dt, nw, sk))
        x, h = xn, h_next
    return x, saved, weights


def _w_in_blocks(g, name):
    cols, tc = g.shape[1], 256
    pieces = list(_shard_pieces(_REGIONS))

    def body(g_ref, o_ref):
        blocks = [[] for _ in range(N_DEV)]
        for d, lo, hi, off in pieces:
            blocks[d].append(g_ref[off:off + hi - lo, :])
        for d in range(N_DEV):
            o_ref[d] = jnp.concatenate(blocks[d], axis=0).astype(BF16)

    return pl.pallas_call(
        body, name=name, grid=(cols // tc,),
        in_specs=[pl.BlockSpec((L_COLS, tc), lambda i: (0, i))],
        out_specs=pl.BlockSpec((N_DEV, SHARD_COLS, tc), lambda i: (0, 0, i)),
        out_shape=jax.ShapeDtypeStruct((N_DEV, SHARD_COLS, cols), BF16),
        compiler_params=_cparams(("parallel",)),
    )(g)


def _small_blocks(g):
    c_conv = g["conv_w"].reshape(CONV_K, N_DEV, CONV_SHARD_COLS).transpose(1, 0, 2)
    c_small = [jnp.broadcast_to(g[n][None], (N_DEV,) + g[n].shape) for n, _ in SMALL_SIZES]
    return _pack_small(c_conv, c_small)


def _contributions(g):
    c_out = g["w_out"].astype(BF16).reshape(N_DEV, OUT_SHARD_ROWS, D_MODEL)
    return _w_in_blocks(g["w_in_rows"], name="w_in_grad_blocks_above"), c_out, _small_blocks(g)


def _backward_layer(l, dx, saved_l, weights_l, ln_g_l, above=None, loss=None):
    x_in, h, qkv, s_in, t_in, ycat, r, al, dt, nw, sk = saved_l
    w_in_l, w_out_l, conv_l = weights_l
    tm = min(512, x_in.shape[0])
    dr, d_lng, d_lnb, *loss_lanes, dycat, d_wout = _ln_out_bwd(dx, x_in if loss else r, ln_g_l[None, :], ycat, w_out_l,
                                                               name=f"ln_out_bwd_{l}", tm=tm, loss=loss)
    big = min(1024, x_in.shape[0])
    rider, p_in, p_out, p_small = None, None, None, None
    recv = lambda c: jax.ShapeDtypeStruct((DEPTH,) + c.shape, c.dtype)
    if above:
        c_out = d_wout.astype(BF16).reshape(N_DEV, OUT_SHARD_ROWS, D_MODEL)
        rider = _Direct([(above[1], True, 0, (l + 1,)), (above[2], True, 1, (l + 1,)), (c_out, True, 0, (l,))],
                        [recv(above[1]), recv(above[2])])
    dh, d_sk, *got = _swa_bwd(h, sk, dycat, name=f"swa_bwd_{l}", rider=rider)
    if above:
        p_out, p_small = got
        rider = _Direct([(above[0], True, 0, (l + 1,))], [recv(above[0])])
    dh, dqkv_n, d_al, d_dt, d_nw, *got = _gdn_bwd(qkv, h, al, dt, nw, s_in, t_in, dycat, dh,
                                                  name=f"gdn_bwd_{l}", rider=rider)
    dh, d_conv = _prep_bwd(h, conv_l, dqkv_n, dh, name=f"prep_bwd_{l}")
    grads = dict(w_out=d_wout, conv_w=d_conv[:CONV_K], a_log=d_al[0, :A_HEADS], dt_bias=d_dt[0, :A_HEADS],
                 norm_w=d_nw[0], sinks=d_sk[0, :B_Q_HEADS], ln_g=d_lng[0], ln_b=d_lnb[0])
    dw = functools.partial(_matmul, dh, x_in, form="tn")
    if not above:
        grads["w_in_rows"] = dw(name=f"in_proj_dw_{l}", tm=L_COLS // 3, tn=D_MODEL, tk=min(2048, x_in.shape[0]))
    else:
        p_in, = got
        cut = D_MODEL // 2
        rest = D_MODEL - cut
        first = dw(name=f"in_proj_dw_first_{l}", tm=L_COLS, tn=cut, tk=big, b_cols=(0, cut))
        blocks = _w_in_blocks(first, name=f"w_in_grad_blocks_first_{l}")
        rider = _Direct([(blocks, True, 0, (l,), (slice(None), pl.ds(0, cut)))], [p_in])
        second, p_in = dw(name=f"in_proj_dw_second_{l}", tm=L_COLS, tn=cut, tk=big, b_cols=(cut, rest), rider=rider)
        blocks = _w_in_blocks(second, name=f"w_in_grad_blocks_second_{l}")
        rider = _Direct([(blocks, True, 0, (l,), (slice(None), pl.ds(cut, rest))),
                         (_small_blocks(grads), True, 1, (l,))], [p_in, p_small])
    if l > 0 and not rider:
        return (dh, w_in_l, dr), grads, None, (loss_lanes[0] if loss else None)
    dx, *got = _as_list(_matmul(dh, w_in_l, form="nn", tm=tm, tn=D_MODEL, tk=L_COLS, name=f"in_proj_dx_{l}",
                                add=dr, add_scale=DEEPNORM_ALPHA, rider=rider))
    bufs = (got[0], p_out, got[1]) if above else None
    return dx, grads, bufs, (loss_lanes[0] if loss else None)


def _all_gather(shards, *, name):
    n_arr = len(shards)

    def body(*refs):
        x_refs, out_refs = refs[:n_arr], refs[n_arr:2 * n_arr]
        send_sems, recv_sems, local_sems = refs[2 * n_arr:]
        x, y, c = _me()
        me, sibling = (x, y, c), (x, y, 1 - c)
        chips = [(1 - x, y), (x, 1 - y), (1 - x, 1 - y)]

        def copy(a, k, block, to, src=None):
            dst = out_refs[a].at[_flat_id(block)]
            return _remote(dst if src is None else src, dst, send_sems.at[a, k], recv_sems.at[a, k], to)

        mine = [pltpu.make_async_copy(x_refs[a], out_refs[a].at[_flat_id(me)], local_sems.at[a])
                for a in range(n_arr)]
        for cp in mine:
            cp.start()
        first = []
        for a in range(n_arr):
            first.append(copy(a, 0, me, sibling, src=x_refs[a]))
            first += [copy(a, 1 + j, me, (*chip, c), src=x_refs[a]) for j, chip in enumerate(chips)]
        for cp in first:
            cp.start()
        passed = []
        for j, chip in enumerate(chips):
            for a in range(n_arr):
                copy(a, 1 + j, (*chip, c), me).wait_recv()
                fwd = copy(a, 4 + j, (*chip, c), sibling)
                fwd.start()
                passed.append(fwd)
        for a in range(n_arr):
            copy(a, 0, sibling, me).wait_recv()
            for j, chip in enumerate(chips):
                copy(a, 4 + j, (*chip, 1 - c), me).wait_recv()
        for cp in first + passed:
            cp.wait_send()
        for cp in mine:
            cp.wait()

    return pl.pallas_call(
        body, name=name, in_specs=[_ANY] * n_arr, out_specs=[_ANY] * n_arr,
        out_shape=[jax.ShapeDtypeStruct((N_DEV,) + s.shape, s.dtype) for s in shards],
        scratch_shapes=[pltpu.SemaphoreType.DMA((n_arr, N_DEV - 1)), pltpu.SemaphoreType.DMA((n_arr, N_DEV - 1)),
                        pltpu.SemaphoreType.DMA((n_arr,))],
    )(*shards)


def _adamw(parts, w, m, v, *, tr, name):
    depth, rows, cols = w.shape
    c1 = 1.0 - ADAM_B1 ** ADAM_STEP
    c2 = 1.0 - ADAM_B2 ** ADAM_STEP

    def body(g_ref, w_ref, m_ref, v_ref, go_ref, d_ref, mo_ref, vo_ref):
        g = g_ref[0, 0].astype(F32)
        for s in range(1, N_DEV):
            g = g + g_ref[0, s].astype(F32)
        m_new = ADAM_B1 * m_ref[0] + (1.0 - ADAM_B1) * g
        v_new = ADAM_B2 * v_ref[0] + (1.0 - ADAM_B2) * (g * g)
        go_ref[0] = g
        mo_ref[0] = m_new
        vo_ref[0] = v_new
        d_ref[0] = -ADAM_LR * ((m_new / c1) / (jnp.sqrt(v_new / c2) + ADAM_EPS) + ADAM_WD * w_ref[0])

    tile = pl.BlockSpec((1, tr, cols), lambda l, i: (l, i, 0))
    return pl.pallas_call(
        body, name=name, grid=(depth, rows // tr),
        in_specs=[pl.BlockSpec((1, N_DEV, tr, cols), lambda l, i: (l, 0, i, 0)), tile, tile, tile],
        out_specs=[tile] * 4, out_shape=[jax.ShapeDtypeStruct(w.shape, F32)] * 4,
        compiler_params=_cparams(("parallel", "parallel")),
    )(parts, w, m, v)


def _adamw_w_in(parts, w, m, v, *, name):
    c1 = 1.0 - ADAM_B1 ** ADAM_STEP
    c2 = 1.0 - ADAM_B2 ** ADAM_STEP

    def body(g_ref, w_ref, m_ref, v_ref, go_ref, d_ref, mo_ref, vo_ref):
        gs = []
        for l in range(DEPTH):
            g = g_ref[l, 0].astype(F32)
            for s in range(1, N_DEV):
                g = g + g_ref[l, s].astype(F32)
            gs.append(g)
        g = jnp.stack(gs, axis=1)
        m_new = ADAM_B1 * m_ref[...] + (1.0 - ADAM_B1) * g
        v_new = ADAM_B2 * v_ref[...] + (1.0 - ADAM_B2) * (g * g)
        go_ref[...] = g
        mo_ref[...] = m_new
        vo_ref[...] = v_new
        d_ref[...] = -ADAM_LR * ((m_new / c1) / (jnp.sqrt(v_new / c2) + ADAM_EPS) + ADAM_WD * w_ref[...])

    tile = pl.BlockSpec((SHARD_COLS, DEPTH, LANE), lambda i: (0, 0, i))
    return pl.pallas_call(
        body, name=name, grid=(D_MODEL // LANE,),
        in_specs=[pl.BlockSpec((DEPTH, N_DEV, SHARD_COLS, LANE), lambda i: (0, 0, 0, i)), tile, tile, tile],
        out_specs=[tile] * 4, out_shape=[jax.ShapeDtypeStruct(w.shape, F32)] * 4,
        compiler_params=_cparams(("parallel",)),
    )(parts, w, m, v)


def _pack_small(conv, small):
    lead = conv.shape[:-2]
    flat = jnp.concatenate([conv.reshape(lead + (CS_CONV,))] + list(small), axis=-1)
    pad = CS_ROWS * LANE - flat.shape[-1]
    flat = jnp.concatenate([flat, jnp.zeros(lead + (pad,), F32)], axis=-1)
    return flat.reshape(lead + (CS_ROWS, LANE))


def _unpack_small(p):
    flat = p.reshape(DEPTH, CS_ROWS * LANE)
    conv = flat[:, :CS_CONV].reshape(DEPTH, CONV_K, CONV_SHARD_COLS)
    small, off = [], CS_CONV
    for _, n in SMALL_SIZES:
        small.append(flat[:, off:off + n])
        off += n
    return conv, small


def kernel(x, w_in, conv_w, a_log, dt_bias, norm_w, sinks, w_out, ln_g, ln_b, loss_target, m_w_in, m_conv_w, m_a_log, m_dt_bias, m_norm_w, m_sinks, m_w_out, m_ln_g, m_ln_b, v_w_in, v_conv_w, v_a_log, v_dt_bias, v_norm_w, v_sinks, v_w_out, v_ln_g, v_ln_b):
    small = [a_log, dt_bias, norm_w, sinks, ln_g, ln_b]
    w_t, m_t, v_t = (a.transpose(2, 0, 1) for a in (w_in, m_w_in, v_w_in))
    shards = [[w_t[:, l].astype(BF16), w_out[l].astype(BF16), conv_w[l]] for l in range(DEPTH)]
    g_in0, = _all_gather(shards[0][:1], name="weights_all_gather_0")
    weights = [[_full_w_in(g_in0, "w_in_rows_0"), None, None]] + [[None, None, None]] * (DEPTH - 1)

    _, saved, weights = _forward(x[0], weights, shards, small)
    dx, g1, _, loss_lanes = _backward_layer(1, None, saved[1], weights[1], ln_g[1],
                                            loss=(loss_target[0], ln_b[1][None, :]))
    loss = lax.psum(0.5 * jnp.sum(loss_lanes) * (1.0 / D_MODEL), ("x", "y", "c"))
    dx, _, (p_in, p_out, p_small), _ = _backward_layer(0, dx, saved[0], weights[0], ln_g[0],
                                                       above=_contributions(g1))

    o_in = [o.transpose(1, 2, 0) for o in _adamw_w_in(p_in, w_t, m_t, v_t, name="adamw_w_in")]
    o_out = _adamw(p_out, w_out, m_w_out, v_w_out, tr=OUT_SHARD_ROWS, name="adamw_w_out")
    o_small = _adamw(p_small, _pack_small(conv_w, small),
                     _pack_small(m_conv_w, [m_a_log, m_dt_bias, m_norm_w, m_sinks, m_ln_g, m_ln_b]),
                     _pack_small(v_conv_w, [v_a_log, v_dt_bias, v_norm_w, v_sinks, v_ln_g, v_ln_b]),
                     tr=CS_ROWS, name="adamw_small")
    outs = []
    for k in range(4):
        cv, sm = _unpack_small(o_small[k])
        outs += [o_in[k], cv, sm[0], sm[1], sm[2], sm[3], o_out[k], sm[4], sm[5]]
    return (loss, dx[None], *outs)
```

```python
import functools

import jax
import jax.numpy as jnp
from jax import lax
from jax.experimental import pallas as pl
from jax.experimental.pallas import tpu as pltpu

F32 = jnp.float32
BF16 = jnp.bfloat16
MM_DTYPE = BF16

N_DEV = 8
D_MODEL = 1024
DEPTH = 2
A_HEADS = 4
A_HEAD_DIM = 128
A_WIDTH = 512
CONV_K = 4
SUPER = 256
NEWTON_STEPS = 1
B_Q_HEADS = 8
B_KV_HEADS = 2
B_HEAD_DIM = 64
B_GROUP = 4
B_WIDTH = 512
WINDOW = 128
BLOCK = 128
IN_COLS = 3336
SHARD_COLS = IN_COLS // N_DEV
OUT_SHARD_ROWS = D_MODEL // N_DEV
CONV_SHARD_COLS = 3 * A_WIDTH // N_DEV
DEEPNORM_ALPHA = (2 * DEPTH) ** 0.25
LN_EPS = 1e-5
RMS_EPS = 1e-6
L2_EPS = 1e-6
ADAM_LR, ADAM_B1, ADAM_B2, ADAM_EPS, ADAM_WD, ADAM_STEP = 0.001, 0.9, 0.999, 1e-08, 0.01, 10

LANE = 128
L_QB, L_ZB, L_KB, L_VB, L_ZA, L_BA, L_QKV = 0, 512, 1024, 1152, 1280, 1792, 1920
L_SWA = 1280
L_GATE = 640
L_COLS = 3456
SMALL_SIZES = (("a_log", 4), ("dt_bias", 4), ("norm_w", 128), ("sinks", 8), ("ln_g", 1024), ("ln_b", 1024))
CS_CONV = CONV_K * CONV_SHARD_COLS
CS_ROWS = 24
assert CS_CONV + sum(n for _, n in SMALL_SIZES) < CS_ROWS * LANE
VMEM_LIMIT = 48 * 1024 * 1024


def _cparams(sem=None):
    return pltpu.CompilerParams(dimension_semantics=sem, vmem_limit_bytes=VMEM_LIMIT)


def _mm(a, b):
    return jnp.dot(a.astype(MM_DTYPE), b.astype(MM_DTYPE), preferred_element_type=F32)


def _mm_nt(a, b):
    return lax.dot_general(a.astype(MM_DTYPE), b.astype(MM_DTYPE), (((1,), (1,)), ((), ())),
                           preferred_element_type=F32)


def _mm_tn(a, b):
    return lax.dot_general(a.astype(MM_DTYPE), b.astype(MM_DTYPE), (((0,), (0,)), ((), ())),
                           preferred_element_type=F32)


def _split(a):
    hi = a.astype(BF16)
    return hi, (a - hi.astype(F32)).astype(BF16)


def _silu(x):
    return x * jax.nn.sigmoid(x)


@jax.custom_vjp
def _stack(parts):
    return jnp.stack(parts)


_stack.defvjp(lambda parts: (jnp.stack(parts), None), lambda _, g: (tuple(g[i] for i in range(g.shape[0])),))


def _softplus(x):
    return jnp.maximum(x, 0.0) + jnp.log1p(jnp.exp(-jnp.abs(x)))


_ANY = pl.BlockSpec(memory_space=pl.ANY)


def _me():
    return lax.axis_index("x"), lax.axis_index("y"), lax.axis_index("c")


def _flat_id(pos):
    return 4 * pos[0] + 2 * pos[1] + pos[2]


def _remote(src, dst, send_sem, recv_sem, to):
    return pltpu.make_async_remote_copy(src_ref=src, dst_ref=dst, send_sem=send_sem, recv_sem=recv_sem,
                                        device_id=to, device_id_type=pl.DeviceIdType.MESH)


class _Direct:
    def __init__(self, items, bufs):
        self.items, self.bufs = list(items), list(bufs)
        self.n_src, self.n_buf = len(self.items), len(self.bufs)
        self.old = [j for j, b in enumerate(self.bufs) if not isinstance(b, jax.ShapeDtypeStruct)]
        self.args = [it[0] for it in self.items] + [self.bufs[j] for j in self.old]
        self.out_shape = [jax.ShapeDtypeStruct(b.shape, b.dtype) for b in self.bufs]
        self.scratch = [pltpu.SemaphoreType.DMA((self.n_src, N_DEV - 1)),
                        pltpu.SemaphoreType.DMA((self.n_src, N_DEV - 1)), pltpu.SemaphoreType.DMA((self.n_src,))]

    def aliases(self, in_base, out_base):
        return {in_base + self.n_src + pos: out_base + j for pos, j in enumerate(self.old)}

    def copies(self, in_refs, out_refs, sems):
        send_sems, recv_sems, local_sems = sems
        x, y, c = _me()
        me = _flat_id((x, y, c))
        peers = [(x ^ ((rel >> 2) & 1), y ^ ((rel >> 1) & 1), c ^ (rel & 1)) for rel in range(1, N_DEV)]
        local, sends, recvs = [], [], []
        for a, (_, per_dest, j, prefix, *rest) in enumerate(self.items):
            src = lambda d: in_refs[a].at[d] if per_dest else in_refs[a]
            dst = lambda s: out_refs[j].at[tuple(prefix) + (s,) + tuple(rest[0] if rest else ())]
            local.append(pltpu.make_async_copy(src(me), dst(me), local_sems.at[a]))
            for k, peer in enumerate(peers):
                pid = _flat_id(peer)
                sends.append(_remote(src(pid), dst(me), send_sems.at[a, k], recv_sems.at[a, k], peer))
                recvs.append(_remote(src(pid), dst(pid), send_sems.at[a, k], recv_sems.at[a, k], peer))
        return local, sends, recvs

    def start(self, in_refs, out_refs, sems):
        local, sends, _ = self.copies(in_refs, out_refs, sems)
        for cp in local + sends:
            cp.start()

    def wait(self, in_refs, out_refs, sems):
        local, sends, recvs = self.copies(in_refs, out_refs, sems)
        for cp in recvs:
            cp.wait_recv()
        for cp in sends:
            cp.wait_send()
        for cp in local:
            cp.wait()


def _pcall(core, *, name, grid, in_specs, out_specs, out_shape, args, sem, scratch_shapes=(), aliases=None,
           rider=None):
    n_in, n_out, n_scr = len(in_specs), len(out_specs), len(scratch_shapes)
    n_rin, n_rout = (len(rider.args), rider.n_buf) if rider else (0, 0)

    def body(*refs):
        ins, outs = refs[:n_in], refs[n_in + n_rin:n_in + n_rin + n_out]
        scr = refs[n_in + n_rin + n_out + n_rout:n_in + n_rin + n_out + n_rout + n_scr]
        if rider:
            r_refs = (refs[n_in:n_in + rider.n_src], refs[n_in + n_rin + n_out:n_in + n_rin + n_out + n_rout],
                      refs[n_in + n_rin + n_out + n_rout + n_scr:])
            ids = [pl.program_id(d) for d in range(len(grid))]
            first = functools.reduce(lambda p, q: p & q, [i == 0 for i in ids])
            last = functools.reduce(lambda p, q: p & q, [i == g - 1 for i, g in zip(ids, grid)])
            pl.when(first)(lambda: rider.start(*r_refs))
        core(ins, outs, scr)
        if rider:
            pl.when(last)(lambda: rider.wait(*r_refs))

    aliases = dict(aliases or {})
    if rider:
        sem = ("arbitrary",) * len(grid)
        aliases.update(rider.aliases(n_in, n_out))
    return pl.pallas_call(
        body, name=name, grid=grid, in_specs=list(in_specs) + [_ANY] * n_rin,
        out_specs=list(out_specs) + [_ANY] * n_rout,
        out_shape=list(out_shape) + (rider.out_shape if rider else []),
        scratch_shapes=list(scratch_shapes) + (rider.scratch if rider else []),
        input_output_aliases=aliases, compiler_params=_cparams(sem),
    )(*args, *(rider.args if rider else []))


def _exchange(direct, *, name):
    n_in = len(direct.args)

    def body(*refs):
        r_refs = refs[:direct.n_src], refs[n_in:n_in + direct.n_buf], refs[n_in + direct.n_buf:]
        direct.start(*r_refs)
        direct.wait(*r_refs)

    return pl.pallas_call(
        body, name=name, in_specs=[_ANY] * n_in, out_specs=[_ANY] * direct.n_buf, out_shape=direct.out_shape,
        input_output_aliases=direct.aliases(0, 0), scratch_shapes=direct.scratch,
    )(*direct.args)


def _matmul(a, b, *, form, tm, tn, tk, name, add=None, add_scale=1.0, rider=None, b_cols=None):
    if form == "nn":
        (m, kk), n = a.shape, b.shape[1]
        a_spec = pl.BlockSpec((tm, tk), lambda i, j, k: (i, k))
        b_spec = pl.BlockSpec((tk, tn), lambda i, j, k: (k, j))
        dn = (((1,), (0,)), ((), ()))
    elif form == "nt":
        (m, kk), n = a.shape, b.shape[0]
        a_spec = pl.BlockSpec((tm, tk), lambda i, j, k: (i, k))
        b_spec = pl.BlockSpec((tn, tk), lambda i, j, k: (j, k))
        dn = (((1,), (1,)), ((), ()))
    else:
        kk, m = a.shape
        n0, n = b_cols or (0, b.shape[1])
        assert n0 % tn == 0
        a_spec = pl.BlockSpec((tk, tm), lambda i, j, k: (k, i))
        b_spec = pl.BlockSpec((tk, tn), lambda i, j, k: (k, j + n0 // tn))
        dn = (((0,), (0,)), ((), ()))
    assert m % tm == 0 and n % tn == 0 and kk % tk == 0, (name, m, n, kk)
    has_add = add is not None

    def core(ins, outs, _):
        a_ref, b_ref = ins[:2]
        o_ref = outs[0]
        k = pl.program_id(2)
        p = lax.dot_general(a_ref[...].astype(MM_DTYPE), b_ref[...].astype(MM_DTYPE), dn,
                            preferred_element_type=F32)

        @pl.when(k == 0)
        def _():
            o_ref[...] = p + add_scale * ins[2][...] if has_add else p

        @pl.when(k > 0)
        def _():
            o_ref[...] += p

    in_specs = [a_spec, b_spec]
    args = [a, b]
    if has_add:
        in_specs.append(pl.BlockSpec((tm, tn), lambda i, j, k: (i, j)))
        args.append(add)
    res = _pcall(core, name=name, grid=(m // tm, n // tn, kk // tk), in_specs=in_specs,
                 out_specs=[pl.BlockSpec((tm, tn), lambda i, j, k: (i, j))],
                 out_shape=[jax.ShapeDtypeStruct((m, n), F32)], args=args,
                 sem=("parallel", "parallel", "arbitrary"), rider=rider)
    return res if rider else res[0]


ZERO_TAIL = 8


def _with_tail(x):
    return jnp.concatenate([x, jnp.zeros((ZERO_TAIL,) + x.shape[1:], x.dtype)], axis=0)


def _shift_down(x, k):
    return pltpu.roll(x, k, 0)


def _shift_up(x, k):
    return pltpu.roll(x, x.shape[0] - k, 0)


def _conv_slab(x, w):
    return w[3:4] * x + w[2:3] * _shift_down(x, 1) + w[1:2] * _shift_down(x, 2) + w[0:1] * _shift_down(x, 3)


def _prep_fwd(h, conv_w, *, name):
    t_len = h.shape[0]

    def body(x_ref, w_ref, o_ref):
        s = pl.program_id(0)
        y = _silu(_conv_slab(_with_tail(x_ref[...]), w_ref[...])[:t_len])
        rs = lax.rsqrt(jnp.sum(y * y, axis=-1, keepdims=True) + L2_EPS)
        scale = jnp.where(s < A_HEADS, A_HEAD_DIM ** -0.5, 1.0)
        o_ref[...] = jnp.where(s < 2 * A_HEADS, y * rs * scale, y)

    return pl.pallas_call(
        body, name=name, grid=(12,),
        in_specs=[pl.BlockSpec((t_len, LANE), lambda s: (0, L_QKV // LANE + s)),
                  pl.BlockSpec((8, LANE), lambda s: (0, s))],
        out_specs=pl.BlockSpec((t_len, LANE), lambda s: (0, s)),
        out_shape=jax.ShapeDtypeStruct((t_len, 3 * A_WIDTH), F32),
        compiler_params=_cparams(("parallel",)),
    )(h, conv_w)


def _prep_bwd(h, conv_w, d_out, dh, *, name):
    t_len = h.shape[0]

    def body(x_ref, w_ref, g_ref, dh_in, dx_ref, dw_ref):
        del dh_in
        s = pl.program_id(0)
        x = _with_tail(x_ref[...])
        g = _with_tail(g_ref[0])
        w = w_ref[...]
        xs = [_shift_down(x, 3), _shift_down(x, 2), _shift_down(x, 1), x]
        c = w[0:1] * xs[0] + w[1:2] * xs[1] + w[2:3] * xs[2] + w[3:4] * xs[3]
        sg = jax.nn.sigmoid(c)
        y = c * sg
        rs = lax.rsqrt(jnp.sum(y * y, axis=-1, keepdims=True) + L2_EPS)
        scale = jnp.where(s < A_HEADS, A_HEAD_DIM ** -0.5, 1.0)
        dy_n = scale * (rs * g - y * (rs * rs * rs) * jnp.sum(g * y, axis=-1, keepdims=True))
        dy = jnp.where(s < 2 * A_HEADS, dy_n, g)
        dc = dy * (sg * (1.0 + c * (1.0 - sg)))
        dx = w[3:4] * dc + w[2:3] * _shift_up(dc, 1) + w[1:2] * _shift_up(dc, 2) + w[0:1] * _shift_up(dc, 3)
        dx_ref[...] = dx[:t_len].astype(dx_ref.dtype)
        dws = [jnp.sum(dc * xs[j], axis=0, keepdims=True) for j in range(CONV_K)]
        dw_ref[...] = jnp.concatenate(dws + [jnp.zeros((8 - CONV_K, LANE), F32)], axis=0)

    slab = pl.BlockSpec((t_len, LANE), lambda s: (0, L_QKV // LANE + s))
    return pl.pallas_call(
        body, name=name, grid=(12,),
        in_specs=[slab, pl.BlockSpec((8, LANE), lambda s: (0, s)),
                  pl.BlockSpec((1, t_len, LANE), lambda s: (s // A_HEADS, 0, s % A_HEADS)), _ANY],
        out_specs=[slab, pl.BlockSpec((8, LANE), lambda s: (0, s))],
        out_shape=[jax.ShapeDtypeStruct((t_len, L_COLS), MM_DTYPE), jax.ShapeDtypeStruct((8, 3 * A_WIDTH), F32)],
        input_output_aliases={3: 0},
        compiler_params=_cparams(("parallel",)),
    )(h, conv_w, d_out, dh)


N_LEVELS = 5
MF_TRIL, MF_STRIL, MF_DIAG8, MF_LOW16, MF_EYE = 0, 1, 2, 3, 3 + N_LEVELS
MB_CUM, MB_CUM_T, MB_TOT = 0, 1, 2


def _gdn_masks():
    r = lax.broadcasted_iota(jnp.int32, (SUPER, SUPER), 0)
    c = lax.broadcasted_iota(jnp.int32, (SUPER, SUPER), 1)
    same = lambda shift: (r >> shift) == (c >> shift)
    ninf = lambda m: jnp.where(m, 0.0, -jnp.inf).astype(F32)
    one = lambda m: m.astype(F32)
    mf = jnp.stack([ninf(r >= c), ninf(r > c), one(same(3))]
                   + [one(same(4 + lv) & jnp.logical_not(same(3 + lv))) for lv in range(N_LEVELS)] + [one(r == c)])
    mb = jnp.stack([one(r >= c), one(r <= c), jnp.ones((SUPER, SUPER), F32)]).astype(BF16)
    return mf, mb


def _tri_inv_impl(a, mf):
    d = lambda p, q: jnp.dot(p.astype(BF16), q.astype(BF16), preferred_element_type=F32)
    dd = lambda p, q: jnp.dot(p, q, preferred_element_type=F32)
    eye = mf[MF_EYE]
    a0 = a * mf[MF_DIAG8]
    a2 = d(a0, a0)
    a4 = d(a2, a2)
    t = d(d(eye - a0, eye + a2), eye + a4)
    for level in range(N_LEVELS):
        t = t - d(d(t, a * mf[MF_LOW16 + level]), t)
    a_hi, a_lo = _split(a)
    for _ in range(NEWTON_STEPS):
        t0 = t.astype(BF16)
        t0f = t0.astype(F32)
        resid = (eye - t0f) - (dd(a_hi, t0) + dd(a_lo, t0))
        r_hi, r_lo = _split(resid)
        t = t0f + (dd(t0, r_hi) + dd(t0, r_lo))
    return t


@jax.custom_vjp
def _wy_apply(a, rhs, t):
    return _mm(t, rhs)


def _wy_apply_fwd(a, rhs, t):
    x = _mm(t, rhs)
    return x, (t, x)


def _wy_apply_bwd(res, dx):
    t, x = res
    d_rhs = _mm_tn(t, dx)
    return -_mm_nt(d_rhs, x), d_rhs, jnp.zeros_like(t)


_wy_apply.defvjp(_wy_apply_fwd, _wy_apply_bwd)


@functools.partial(jax.custom_vjp, nondiff_argnums=(1,))
def _lane_roll(x, shift):
    return pltpu.roll(x, shift % LANE, 1)


_lane_roll.defvjp(lambda x, shift: (_lane_roll(x, shift), None), lambda shift, _, g: (_lane_roll(g, -shift),))


def _mask_times_lanes(x, mask):
    lane = lax.broadcasted_iota(jnp.int32, (1, LANE), 1)
    x = jnp.where(lane < A_HEADS, x, 0.0)
    x1 = x.astype(BF16).astype(F32)
    x2 = (x - x1).astype(BF16).astype(F32)
    x3 = (x - x1 - x2).astype(BF16).astype(F32)
    pieces = x1 + pltpu.roll(x2, A_HEADS, 1) + pltpu.roll(x3, 2 * A_HEADS, 1)
    res = jnp.dot(mask, pieces.astype(BF16), preferred_element_type=F32)
    return res + pltpu.roll(res, LANE - A_HEADS, 1) + pltpu.roll(res, LANE - 2 * A_HEADS, 1)


@jax.custom_vjp
def _chunk_sums(g, mb):
    return _mask_times_lanes(g, mb[MB_CUM]), _mask_times_lanes(g, mb[MB_TOT])


def _chunk_sums_fwd(g, mb):
    return _chunk_sums(g, mb), mb


def _chunk_sums_bwd(mb, d):
    lane = lax.broadcasted_iota(jnp.int32, (1, LANE), 1)
    dg = _mask_times_lanes(d[0], mb[MB_CUM_T]) + _mask_times_lanes(d[1], mb[MB_TOT])
    return jnp.where(lane < A_HEADS, dg, 0.0), jnp.zeros_like(mb)


_chunk_sums.defvjp(_chunk_sums_fwd, _chunk_sums_bwd)


def _gdn_gates(ba, alog, dtb, mb):
    beta = jax.nn.sigmoid(ba)
    g = -jnp.exp(alog) * _softplus(_lane_roll(ba, -A_HEADS) + dtb)
    gc, gl = _chunk_sums(g, mb)
    return beta, gc, gl, gc.T


def _gdn_block(s, q, k, v, z, gates, nw, h, t_known, mf):
    n = q.shape[0]
    beta_all, gc_all, gl_all, gct_all = gates
    lane = lax.broadcasted_iota(jnp.int32, (1, LANE), 1)
    sub = lax.broadcasted_iota(jnp.int32, (LANE, 1), 0)
    col = lambda x: jnp.sum(jnp.where(lane == h, x, 0.0), axis=1, keepdims=True)
    wide = lambda c: jnp.broadcast_to(c, (n, LANE))
    gc, gl = col(gc_all), col(gl_all)
    gc_row = jnp.sum(jnp.where(sub == h, gct_all, 0.0), axis=0, keepdims=True)
    beta_w, eg_w = wide(col(beta_all)), wide(jnp.exp(gc))
    diff = gc - gc_row
    decay = jnp.exp(diff + mf[MF_TRIL])
    kb = k * beta_w
    a_mat = _mm_nt(kb, k) * jnp.exp(diff + mf[MF_STRIL])
    rhs = jnp.concatenate([v * beta_w, kb * eg_w], axis=1)
    if t_known is None:
        t_mat = _tri_inv_impl(a_mat, mf)
        uw = _mm(t_mat, rhs)
    else:
        t_mat = t_known
        uw = _wy_apply(a_mat, rhs, t_known)
    u, w = uw[:, :LANE], uw[:, LANE:]
    qk = _mm_nt(q, k) * decay
    q_dec = q * eg_w
    k_dec = k * wide(jnp.exp(gl - gc))
    v_new = u - _mm(w, s)
    o = _mm(q_dec, s) + _mm(qk, v_new)
    s = s * jnp.exp(gl[0:1]) + _mm_tn(k_dec, v_new)
    o = o * lax.rsqrt(jnp.mean(o * o, axis=-1, keepdims=True) + RMS_EPS) * nw
    return o * _silu(z), s, t_mat


def _gdn_fwd(qkv, h, alog, dtb, nw, ycat, *, name, rider=None):
    t_len = qkv.shape[0]
    nsc = t_len // SUPER

    def core(ins, outs, scr):
        q_ref, k_ref, v_ref, gate_ref, al_ref, dt_ref, nw_ref, mf_ref, mb_ref, _ = ins
        y_ref, sin_ref, t_ref = outs
        s_scr, = scr

        @pl.when(pl.program_id(0) == 0)
        def _():
            s_scr[...] = jnp.zeros_like(s_scr)

        per_head = lambda ref: jnp.stack([ref[:, hh * LANE:(hh + 1) * LANE] for hh in range(A_HEADS)])
        states = s_scr[...]
        gates = _gdn_gates(gate_ref[:, A_WIDTH:], al_ref[...], dt_ref[...], mb_ref[...])
        fn = jax.vmap(_gdn_block, in_axes=(0, 0, 0, 0, 0, None, None, 0, None, None))
        y, s_new, t_mat = fn(states, per_head(q_ref), per_head(k_ref), per_head(v_ref), per_head(gate_ref),
                             gates, nw_ref[...], jnp.arange(A_HEADS), None, mf_ref[...])
        sin_ref[0] = states
        t_ref[0] = t_mat.astype(t_ref.dtype)
        s_scr[...] = s_new
        for hh in range(A_HEADS):
            y_ref[:, hh * LANE:(hh + 1) * LANE] = y[hh].astype(y_ref.dtype)

    blk = lambda j: pl.BlockSpec((SUPER, A_WIDTH), lambda sc: (sc, j))
    row = pl.BlockSpec((1, LANE), lambda sc: (0, 0))
    mf, mb = _gdn_masks()
    whole = lambda a: pl.BlockSpec(a.shape, lambda sc: (0, 0, 0))
    return _pcall(
        core, name=name, grid=(nsc,),
        in_specs=[blk(0), blk(1), blk(2), pl.BlockSpec((SUPER, L_GATE), lambda sc: (sc, L_ZA // L_GATE)),
                  row, row, row, whole(mf), whole(mb), _ANY],
        out_specs=[blk(0),
                   pl.BlockSpec((1, A_HEADS, A_HEAD_DIM, A_HEAD_DIM), lambda sc: (sc, 0, 0, 0)),
                   pl.BlockSpec((1, A_HEADS, SUPER, SUPER), lambda sc: (sc, 0, 0, 0))],
        out_shape=[jax.ShapeDtypeStruct((t_len, D_MODEL), MM_DTYPE),
                   jax.ShapeDtypeStruct((nsc, A_HEADS, A_HEAD_DIM, A_HEAD_DIM), F32),
                   jax.ShapeDtypeStruct((nsc, A_HEADS, SUPER, SUPER), MM_DTYPE)],
        scratch_shapes=[pltpu.VMEM((A_HEADS, A_HEAD_DIM, A_HEAD_DIM), F32)],
        aliases={9: 0}, sem=("arbitrary",), rider=rider,
        args=(qkv, qkv, qkv, h, alog, dtb, nw, mf, mb, ycat))


def _gdn_bwd(qkv, h, alog, dtb, nw, s_in, t_in, dycat, dh, *, name, rider=None):
    t_len = qkv.shape[0]
    nsc = t_len // SUPER

    def core(ins, outs, scr):
        q_ref, k_ref, v_ref, gate_ref, al_ref, dt_ref, nw_ref, sin_ref, t_ref, dy_ref, mf_ref, mb_ref, _ = ins
        dgate_ref, dqkv_ref, dal_ref, ddt_ref, dnw_ref = outs
        ds_scr, = scr

        @pl.when(pl.program_id(0) == 0)
        def _():
            ds_scr[...] = jnp.zeros_like(ds_scr)
            dal_ref[...] = jnp.zeros_like(dal_ref)
            ddt_ref[...] = jnp.zeros_like(ddt_ref)
            dnw_ref[...] = jnp.zeros_like(dnw_ref)

        per_head = lambda ref: jnp.stack([ref[:, hh * LANE:(hh + 1) * LANE] for hh in range(A_HEADS)])
        head_ids = jnp.arange(A_HEADS)
        t_known, mf, mb = t_ref[0], mf_ref[...], mb_ref[...]

        def fn(s, q, k, v, z, ba, alog, dtb, nw):
            gates = _gdn_gates(ba, alog, dtb, mb)
            one = lambda s, q, k, v, z, t, h: _gdn_block(s, q, k, v, z, gates, nw, h, t, mf)[:2]
            return jax.vmap(one)(s, q, k, v, z, t_known, head_ids)

        _, vjp = jax.vjp(fn, sin_ref[0], per_head(q_ref), per_head(k_ref), per_head(v_ref), per_head(gate_ref),
                         gate_ref[:, A_WIDTH:], al_ref[...], dt_ref[...], nw_ref[...])
        ds, dq, dk, dv, dz, dba, dal, ddt, dnw = vjp((per_head(dy_ref), ds_scr[...]))
        ds_scr[...] = ds
        for hh in range(A_HEADS):
            cols = slice(hh * LANE, (hh + 1) * LANE)
            dqkv_ref[0, :, cols] = dq[hh]
            dqkv_ref[1, :, cols] = dk[hh]
            dqkv_ref[2, :, cols] = dv[hh]
            dgate_ref[:, cols] = dz[hh].astype(dgate_ref.dtype)
        dgate_ref[:, A_WIDTH:] = dba.astype(dgate_ref.dtype)
        dal_ref[...] += dal
        ddt_ref[...] += ddt
        dnw_ref[...] += dnw

    rev = lambda i: nsc - 1 - i
    blk = lambda j: pl.BlockSpec((SUPER, A_WIDTH), lambda i: (rev(i), j))
    gate = pl.BlockSpec((SUPER, L_GATE), lambda i: (rev(i), L_ZA // L_GATE))
    row = pl.BlockSpec((1, LANE), lambda i: (0, 0))
    mf, mb = _gdn_masks()
    whole = lambda a: pl.BlockSpec(a.shape, lambda i: (0, 0, 0))
    return _pcall(
        core, name=name, grid=(nsc,),
        in_specs=[blk(0), blk(1), blk(2), gate, row, row, row,
                  pl.BlockSpec((1, A_HEADS, A_HEAD_DIM, A_HEAD_DIM), lambda i: (rev(i), 0, 0, 0)),
                  pl.BlockSpec((1, A_HEADS, SUPER, SUPER), lambda i: (rev(i), 0, 0, 0)),
                  blk(0), whole(mf), whole(mb), _ANY],
        out_specs=[gate, pl.BlockSpec((3, SUPER, A_WIDTH), lambda i: (0, rev(i), 0)), row, row, row],
        out_shape=[jax.ShapeDtypeStruct((t_len, L_COLS), MM_DTYPE), jax.ShapeDtypeStruct((3, t_len, A_WIDTH), F32)]
        + [jax.ShapeDtypeStruct((1, LANE), F32)] * 3,
        scratch_shapes=[pltpu.VMEM((A_HEADS, A_HEAD_DIM, A_HEAD_DIM), F32)],
        aliases={12: 0}, sem=("arbitrary",), rider=rider,
        args=(qkv, qkv, qkv, h, alog, dtb, nw, s_in, t_in, dycat, mf, mb, dh))


Q_BLOCKS = 4
Q_ROWS = Q_BLOCKS * BLOCK


def _swa_block(q, kp, kc, vp, vc, z, sinks, first):
    rows = B_GROUP * BLOCK
    ri = lax.broadcasted_iota(jnp.int32, (rows, 2 * BLOCK), 0)
    si = lax.broadcasted_iota(jnp.int32, (rows, 2 * BLOCK), 1)
    dist = (ri & (BLOCK - 1)) + BLOCK - si
    bias = jnp.where((dist >= 0) & (dist < WINDOW), 0.0, -jnp.inf)
    no_prev = jnp.where(first & (si[:1] < BLOCK), -jnp.inf, 0.0)
    dist_f = dist.astype(F32)
    head_of_row = lax.broadcasted_iota(jnp.int32, (rows, 1), 0) >> 7
    keys = jnp.concatenate([kp, kc], axis=0)
    vals = jnp.concatenate([vp, vc], axis=0)

    def item(b, j):
        cs = slice(j * B_HEAD_DIM, (j + 1) * B_HEAD_DIM)
        rs = slice(b * BLOCK, (b + 1) * BLOCK)
        heads = range(j * B_GROUP, (j + 1) * B_GROUP)
        qs = jnp.concatenate([q[rs, hq * B_HEAD_DIM:(hq + 1) * B_HEAD_DIM] for hq in heads], axis=0) * (
            B_HEAD_DIM ** -0.5)
        kk = keys[b * BLOCK:(b + 2) * BLOCK, cs]
        vv = vals[b * BLOCK:(b + 2) * BLOCK, cs]
        sink = jnp.concatenate([jnp.broadcast_to(sinks[:, hq:hq + 1], (BLOCK, 1)) for hq in heads], axis=0)
        slope = sum(jnp.where(head_of_row == gi, 2.0 ** (-8.0 * (hq + 1) / B_Q_HEADS), 0.0)
                    for gi, hq in enumerate(heads))
        return qs, kk, vv, sink, slope, (no_prev if b == 0 else jnp.zeros_like(no_prev))

    def attend(qs, kk, vv, sink, slope, hide):
        sc = _mm_nt(qs, kk) - slope * dist_f + (bias + hide)
        m = lax.stop_gradient(jnp.maximum(jnp.max(sc, axis=-1, keepdims=True), sink))
        p = jnp.exp(sc - m)
        inv = 1.0 / (jnp.sum(p, axis=-1, keepdims=True) + jnp.exp(sink - m))
        return _mm(p * inv, vv)

    items = [(b, j) for b in range(Q_BLOCKS) for j in range(B_KV_HEADS)]
    o = jax.vmap(attend)(*[_stack(t) for t in zip(*[item(b, j) for b, j in items])])
    rows_out = [jnp.concatenate([o[b * B_KV_HEADS + j, gi * BLOCK:(gi + 1) * BLOCK]
                                 for j in range(B_KV_HEADS) for gi in range(B_GROUP)], axis=1)
                for b in range(Q_BLOCKS)]
    return jnp.concatenate(rows_out, axis=0) * _silu(z)


def _swa_specs(idx):
    wide = lambda off: pl.BlockSpec((Q_ROWS, B_WIDTH), lambda n: (idx(n), off))
    cur = lambda off: pl.BlockSpec((Q_ROWS, LANE), lambda n: (idx(n), off))
    prev = lambda off: pl.BlockSpec((BLOCK, LANE), lambda n: (jnp.maximum(idx(n) * Q_BLOCKS - 1, 0), off))
    return [wide(L_QB // B_WIDTH), prev(L_KB // LANE), cur(L_KB // LANE), prev(L_VB // LANE), cur(L_VB // LANE),
            wide(L_ZB // B_WIDTH), pl.BlockSpec((1, LANE), lambda n: (0, 0))]


def _swa_fwd(h, sinks, *, name, rider=None):
    t_len = h.shape[0]
    nb = t_len // Q_ROWS

    def core(ins, outs, _):
        q_ref, kp_ref, kc_ref, vp_ref, vc_ref, z_ref, s_ref = ins
        outs[0][...] = _swa_block(q_ref[...], kp_ref[...], kc_ref[...], vp_ref[...], vc_ref[...], z_ref[...],
                                  s_ref[...], pl.program_id(0) == 0).astype(outs[0].dtype)

    res = _pcall(core, name=name, grid=(nb,), in_specs=_swa_specs(lambda n: n),
                 out_specs=[pl.BlockSpec((Q_ROWS, B_WIDTH), lambda n: (n, 1))],
                 out_shape=[jax.ShapeDtypeStruct((t_len, D_MODEL), MM_DTYPE)], sem=("parallel",), rider=rider,
                 args=(h, h, h, h, h, h, sinks))
    return res if rider else res[0]


def _swa_bwd(h, sinks, dycat, *, name, rider=None):
    t_len = h.shape[0]
    nb = t_len // Q_ROWS
    early = slice(0, Q_ROWS - BLOCK)
    last = slice(Q_ROWS - BLOCK, Q_ROWS)

    def core(ins, outs, scr):
        q_ref, kp_ref, kc_ref, vp_ref, vc_ref, z_ref, s_ref, dy_ref = ins
        dh_ref, dsk_ref = outs
        ck_scr, cv_scr = scr
        i = pl.program_id(0)
        n = nb - 1 - i

        @pl.when(i == 0)
        def _():
            ck_scr[...] = jnp.zeros_like(ck_scr)
            cv_scr[...] = jnp.zeros_like(cv_scr)
            dsk_ref[...] = jnp.zeros_like(dsk_ref)

        fn = functools.partial(_swa_block, first=(n == 0))
        _, vjp = jax.vjp(fn, q_ref[...], kp_ref[...], kc_ref[...], vp_ref[...], vc_ref[...], z_ref[...], s_ref[...])
        dq, dkp, dkc, dvp, dvc, dz, dsk = vjp(dy_ref[...])
        def put(rows, col, val):
            dh_ref[rows, col:col + val.shape[1]] = val.astype(dh_ref.dtype)

        put(slice(None), L_QB, dq)
        put(slice(None), L_ZB, dz)
        put(early, L_KB, dkc[early])
        put(early, L_VB, dvc[early])
        put(last, L_KB, dkc[last] + ck_scr[...])
        put(last, L_VB, dvc[last] + cv_scr[...])
        ck_scr[...] = dkp
        cv_scr[...] = dvp
        dsk_ref[...] += dsk

    rev = lambda i: nb - 1 - i
    return _pcall(
        core, name=name, grid=(nb,),
        in_specs=_swa_specs(rev) + [pl.BlockSpec((Q_ROWS, B_WIDTH), lambda i: (rev(i), 1))],
        out_specs=[pl.BlockSpec((Q_ROWS, L_SWA), lambda i: (rev(i), 0)), pl.BlockSpec((1, LANE), lambda i: (0, 0))],
        out_shape=[jax.ShapeDtypeStruct((t_len, L_COLS), MM_DTYPE), jax.ShapeDtypeStruct((1, LANE), F32)],
        scratch_shapes=[pltpu.VMEM((BLOCK, LANE), F32), pltpu.VMEM((BLOCK, LANE), F32)],
        sem=("arbitrary",), rider=rider, args=(h, h, h, h, h, h, sinks, dycat))


def _out_ln_fwd(ycat, w_out, x, ln_g, ln_b, w_in_next, *, name, tm=512):
    t_len = x.shape[0]

    def body(y_ref, w_ref, x_ref, g_ref, b_ref, win_ref, r_ref, o_ref, h_ref):
        r = DEEPNORM_ALPHA * x_ref[...] + _mm(y_ref[...], w_ref[...])
        r_ref[...] = r
        mu = jnp.mean(r, axis=-1, keepdims=True)
        d = r - mu
        var = jnp.mean(d * d, axis=-1, keepdims=True)
        xn = d * lax.rsqrt(var + LN_EPS) * g_ref[...] + b_ref[...]
        o_ref[...] = xn
        h_ref[...] = _mm_nt(xn, win_ref[...])

    tile = pl.BlockSpec((tm, D_MODEL), lambda i: (i, 0))
    vec = pl.BlockSpec((1, D_MODEL), lambda i: (0, 0))
    tile_shape = jax.ShapeDtypeStruct((t_len, D_MODEL), F32)
    return pl.pallas_call(
        body, name=name, grid=(t_len // tm,),
        in_specs=[tile, pl.BlockSpec((D_MODEL, D_MODEL), lambda i: (0, 0)), tile, vec, vec,
                  pl.BlockSpec((L_COLS, D_MODEL), lambda i: (0, 0))],
        out_specs=[tile, tile, pl.BlockSpec((tm, L_COLS), lambda i: (i, 0))],
        out_shape=[tile_shape, tile_shape, jax.ShapeDtypeStruct((t_len, L_COLS), F32)],
        compiler_params=_cparams(("parallel",)),
    )(ycat, w_out, x, ln_g, ln_b, w_in_next)


def _ln_out_bwd(dxn, r, ln_g, ycat, w_out, *, name, tm=512, loss=None):
    t_len = r.shape[0]
    above = isinstance(dxn, tuple)
    n_lead = 4 if loss else 5 if above else 3

    def body(*refs):
        lead, (y_ref, w_ref), outs = refs[:n_lead], refs[n_lead:n_lead + 2], refs[n_lead + 2:]
        if loss:
            t_ref, r_ref, g_ref, b_ref = lead
            dr_ref, dg_ref, db_ref, l_ref, dy_ref, dw_ref = outs
        else:
            *dx_refs, r_ref, g_ref = lead
            dr_ref, dg_ref, db_ref, dy_ref, dw_ref = outs

        @pl.when(pl.program_id(0) == 0)
        def _():
            dg_ref[...] = jnp.zeros_like(dg_ref)
            db_ref[...] = jnp.zeros_like(db_ref)
            dw_ref[...] = jnp.zeros_like(dw_ref)
            if loss:
                l_ref[...] = jnp.zeros_like(l_ref)

        rr = r_ref[...]
        if loss:
            rr = DEEPNORM_ALPHA * rr + _mm(y_ref[...], w_ref[...])
        mu = jnp.mean(rr, axis=-1, keepdims=True)
        d = rr - mu
        rstd = lax.rsqrt(jnp.mean(d * d, axis=-1, keepdims=True) + LN_EPS)
        xh = d * rstd
        if loss:
            e = (xh * g_ref[...] + b_ref[...]) - t_ref[...]
            dx = e * (1.0 / D_MODEL)
            l_ref[...] += jnp.sum(e * e, axis=0, keepdims=True)
        elif above:
            dh_ref, win_ref, add_ref = dx_refs
            dx = _mm(dh_ref[...], win_ref[...]) + DEEPNORM_ALPHA * add_ref[...]
        else:
            dx = dx_refs[0][...]
        dxh = dx * g_ref[...]
        dr = rstd * (dxh - jnp.mean(dxh, axis=-1, keepdims=True) - xh * jnp.mean(dxh * xh, axis=-1, keepdims=True))
        dr_ref[...] = dr
        dg_ref[...] += jnp.sum(dx * xh, axis=0, keepdims=True)
        db_ref[...] += jnp.sum(dx, axis=0, keepdims=True)
        dy_ref[...] = _mm_nt(dr, w_ref[...])
        dw_ref[...] += _mm_tn(y_ref[...], dr)

    tile = pl.BlockSpec((tm, D_MODEL), lambda i: (i, 0))
    vec = pl.BlockSpec((1, D_MODEL), lambda i: (0, 0))
    square = pl.BlockSpec((D_MODEL, D_MODEL), lambda i: (0, 0))
    tile_shape = jax.ShapeDtypeStruct((t_len, D_MODEL), F32)
    vec_shape = jax.ShapeDtypeStruct((1, D_MODEL), F32)
    if loss:
        args, lead_specs = (loss[0], r, ln_g, loss[1]), [tile, tile, vec, vec]
    elif above:
        args = (*dxn, r, ln_g)
        lead_specs = [pl.BlockSpec((tm, L_COLS), lambda i: (i, 0)), pl.BlockSpec((L_COLS, D_MODEL), lambda i: (0, 0)),
                      tile, tile, vec]
    else:
        args, lead_specs = (dxn, r, ln_g), [tile, tile, vec]
    return pl.pallas_call(
        body, name=name, grid=(t_len // tm,),
        in_specs=lead_specs + [tile, square],
        out_specs=[tile, vec, vec] + ([vec] if loss else []) + [tile, square],
        out_shape=[tile_shape, vec_shape, vec_shape] + ([vec_shape] if loss else [])
        + [tile_shape, jax.ShapeDtypeStruct((D_MODEL, D_MODEL), F32)],
        compiler_params=_cparams(("arbitrary",)),
    )(*args, ycat, w_out)


def _pad_row(v):
    return jnp.zeros((1, LANE), F32).at[0, :v.shape[0]].set(v)


_REGIONS = ((0, 1536, L_QKV), (1536, 2048, L_ZA), (2048, 2056, L_BA), (2056, 2568, L_QB), (2568, 2696, L_KB),
            (2696, 2824, L_VB), (2824, 3336, L_ZB))


def _shard_pieces(regions):
    for a, b, off in regions:
        for d in range(N_DEV):
            lo, hi = max(a, d * SHARD_COLS), min(b, (d + 1) * SHARD_COLS)
            if lo < hi:
                yield d, lo - d * SHARD_COLS, hi - d * SHARD_COLS, off + lo - a


def _as_list(r):
    return list(r) if isinstance(r, (list, tuple)) else [r]


def _gathered(shard):
    return jax.ShapeDtypeStruct((N_DEV,) + shard.shape, shard.dtype)


def _full_w_in(g_in, name):
    by_offset = sorted(_shard_pieces(_REGIONS), key=lambda p: p[3])
    tc = 256

    def body(g_ref, o_ref):
        pieces, row = [], 0
        for d, lo, hi, off in by_offset + [(None, 0, 0, L_COLS)]:
            if off > row:
                pieces.append(jnp.zeros((off - row, tc), g_ref.dtype))
            if d is not None:
                pieces.append(g_ref[d, lo:hi, :])
            row = off + hi - lo
        o_ref[...] = jnp.concatenate(pieces, axis=0)

    return pl.pallas_call(
        body, name=name, grid=(D_MODEL // tc,),
        in_specs=[pl.BlockSpec((N_DEV, SHARD_COLS, tc), lambda i: (0, 0, i))],
        out_specs=pl.BlockSpec((L_COLS, tc), lambda i: (0, i)),
        out_shape=jax.ShapeDtypeStruct((L_COLS, D_MODEL), g_in.dtype),
        compiler_params=_cparams(("parallel",)),
    )(g_in)


def _full_conv(g_conv):
    return jnp.pad(g_conv.transpose(1, 0, 2).reshape(CONV_K, 3 * A_WIDTH), ((0, 8 - CONV_K), (0, 0)))


def _forward(x, weights, shards, small):
    a_log, dt_bias, norm_w, sinks, ln_g, ln_b = small
    tm = min(512, x.shape[0])
    saved, weights = [], [list(w) for w in weights]
    whole = lambda arrs: _Direct([(a, False, j, ()) for j, a in enumerate(arrs)], [_gathered(a) for a in arrs])
    h = None
    for l in range(DEPTH):
        if h is None:
            rider = whole(shards[l][1:]) if weights[l][1] is None else None
            h, *got = _as_list(_matmul(x, weights[l][0], form="nt", tm=tm, tn=L_COLS, tk=D_MODEL,
                                       name=f"in_proj_{l}", rider=rider))
            if rider:
                weights[l][1:] = [got[0].reshape(D_MODEL, D_MODEL), _full_conv(got[1])]
        w_in_l, w_out_l, conv_l = weights[l]
        qkv = _prep_fwd(h, conv_l, name=f"prep_fwd_{l}")
        al, dt, nw, sk = _pad_row(a_log[l]), _pad_row(dt_bias[l]), norm_w[l][None, :], _pad_row(sinks[l])
        ahead = l + 1 < DEPTH and weights[l + 1][0] is None
        rider = whole(shards[l + 1][1:]) if ahead else None
        ycat, *got = _as_list(_swa_fwd(h, sk, name=f"swa_fwd_{l}", rider=rider))
        if ahead:
            weights[l + 1][1:] = [got[0].reshape(D_MODEL, D_MODEL), _full_conv(got[1])]
        rider = whole(shards[l + 1][:1]) if ahead else None
        ycat, s_in, t_in, *got = _gdn_fwd(qkv, h, al, dt, nw, ycat, name=f"gdn_fwd_{l}", rider=rider)
        if ahead:
            weights[l + 1][0] = _full_w_in(got[0], f"w_in_rows_{l + 1}")
        r, xn, h_next = None, None, None
        if l + 1 < DEPTH:
            r, xn, h_next = _out_ln_fwd(ycat, w_out_l, x, ln_g[l][None, :], ln_b[l][None, :], weights[l + 1][0],
                                        name=f"out_ln_{l}")
        saved.append((x, h, qkv, s_in, t_in, ycat, r, al, dt, nw, sk))
        x, h = xn, h_next
    return x, saved, weights


def _w_in_blocks(g, name):
    cols, tc = g.shape[1], 256
    pieces = list(_shard_pieces(_REGIONS))

    def body(g_ref, o_ref):
        blocks = [[] for _ in range(N_DEV)]
        for d, lo, hi, off in pieces:
            blocks[d].append(g_ref[off:off + hi - lo, :])
        for d in range(N_DEV):
            o_ref[d] = jnp.concatenate(blocks[d], axis=0).astype(BF16)

    return pl.pallas_call(
        body, name=name, grid=(cols // tc,),
        in_specs=[pl.BlockSpec((L_COLS, tc), lambda i: (0, i))],
        out_specs=pl.BlockSpec((N_DEV, SHARD_COLS, tc), lambda i: (0, 0, i)),
        out_shape=jax.ShapeDtypeStruct((N_DEV, SHARD_COLS, cols), BF16),
        compiler_params=_cparams(("parallel",)),
    )(g)


def _small_blocks(g):
    c_conv = g["conv_w"].reshape(CONV_K, N_DEV, CONV_SHARD_COLS).transpose(1, 0, 2)
    c_small = [jnp.broadcast_to(g[n][None], (N_DEV,) + g[n].shape) for n, _ in SMALL_SIZES]
    return _pack_small(c_conv, c_small)


def _contributions(g, loss_part):
    c_out = g["w_out"].astype(BF16).reshape(N_DEV, OUT_SHARD_ROWS, D_MODEL)
    c_small = _small_blocks(g).at[:, CS_ROWS - 1, LANE - 1].set(loss_part)
    return _w_in_blocks(g["w_in_rows"], name="w_in_grad_blocks_above"), c_out, c_small


def _backward_layer(l, dx, saved_l, weights_l, ln_g_l, above=None, loss=None):
    x_in, h, qkv, s_in, t_in, ycat, r, al, dt, nw, sk = saved_l
    w_in_l, w_out_l, conv_l = weights_l
    tm = min(512, x_in.shape[0])
    dr, d_lng, d_lnb, *loss_lanes, dycat, d_wout = _ln_out_bwd(dx, x_in if loss else r, ln_g_l[None, :], ycat, w_out_l,
                                                               name=f"ln_out_bwd_{l}", tm=tm, loss=loss)
    big = min(1024, x_in.shape[0])
    rider, p_in, p_out, p_small = None, None, None, None
    recv = lambda c: jax.ShapeDtypeStruct((DEPTH,) + c.shape, c.dtype)
    if above:
        c_out = d_wout.astype(BF16).reshape(N_DEV, OUT_SHARD_ROWS, D_MODEL)
        rider = _Direct([(above[1], True, 0, (l + 1,)), (above[2], True, 1, (l + 1,)), (c_out, True, 0, (l,))],
                        [recv(above[1]), recv(above[2])])
    dh, d_sk, *got = _swa_bwd(h, sk, dycat, name=f"swa_bwd_{l}", rider=rider)
    if above:
        p_out, p_small = got
        rider = _Direct([(above[0], True, 0, (l + 1,))], [recv(above[0])])
    dh, dqkv_n, d_al, d_dt, d_nw, *got = _gdn_bwd(qkv, h, al, dt, nw, s_in, t_in, dycat, dh,
                                                  name=f"gdn_bwd_{l}", rider=rider)
    dh, d_conv = _prep_bwd(h, conv_l, dqkv_n, dh, name=f"prep_bwd_{l}")
    grads = dict(w_out=d_wout, conv_w=d_conv[:CONV_K], a_log=d_al[0, :A_HEADS], dt_bias=d_dt[0, :A_HEADS],
                 norm_w=d_nw[0], sinks=d_sk[0, :B_Q_HEADS], ln_g=d_lng[0], ln_b=d_lnb[0])
    dw = functools.partial(_matmul, dh, x_in, form="tn")
    if not above:
        grads["w_in_rows"] = dw(name=f"in_proj_dw_{l}", tm=L_COLS // 3, tn=D_MODEL, tk=min(2048, x_in.shape[0]))
    else:
        p_in, = got
        cut = D_MODEL // 2
        rest = D_MODEL - cut
        first = dw(name=f"in_proj_dw_first_{l}", tm=L_COLS, tn=cut, tk=big, b_cols=(0, cut))
        blocks = _w_in_blocks(first, name=f"w_in_grad_blocks_first_{l}")
        rider = _Direct([(blocks, True, 0, (l,), (slice(None), pl.ds(0, cut)))], [p_in])
        second, p_in = dw(name=f"in_proj_dw_second_{l}", tm=L_COLS, tn=cut, tk=big, b_cols=(cut, rest), rider=rider)
        blocks = _w_in_blocks(second, name=f"w_in_grad_blocks_second_{l}")
        rider = _Direct([(blocks, True, 0, (l,), (slice(None), pl.ds(cut, rest))),
                         (_small_blocks(grads), True, 1, (l,))], [p_in, p_small])
    if l > 0 and not rider:
        return (dh, w_in_l, dr), grads, None, (loss_lanes[0] if loss else None)
    dx, *got = _as_list(_matmul(dh, w_in_l, form="nn", tm=tm, tn=D_MODEL, tk=L_COLS, name=f"in_proj_dx_{l}",
                                add=dr, add_scale=DEEPNORM_ALPHA, rider=rider))
    bufs = (got[0], p_out, got[1]) if above else None
    return dx, grads, bufs, (loss_lanes[0] if loss else None)


def _all_gather(shards, *, name):
    n_arr = len(shards)

    def body(*refs):
        x_refs, out_refs = refs[:n_arr], refs[n_arr:2 * n_arr]
        send_sems, recv_sems, local_sems = refs[2 * n_arr:]
        x, y, c = _me()
        me, sibling = (x, y, c), (x, y, 1 - c)
        chips = [(1 - x, y), (x, 1 - y), (1 - x, 1 - y)]

        def copy(a, k, block, to, src=None):
            dst = out_refs[a].at[_flat_id(block)]
            return _remote(dst if src is None else src, dst, send_sems.at[a, k], recv_sems.at[a, k], to)

        mine = [pltpu.make_async_copy(x_refs[a], out_refs[a].at[_flat_id(me)], local_sems.at[a])
                for a in range(n_arr)]
        for cp in mine:
            cp.start()
        first = []
        for a in range(n_arr):
            first.append(copy(a, 0, me, sibling, src=x_refs[a]))
            first += [copy(a, 1 + j, me, (*chip, c), src=x_refs[a]) for j, chip in enumerate(chips)]
        for cp in first:
            cp.start()
        passed = []
        for j, chip in enumerate(chips):
            for a in range(n_arr):
                copy(a, 1 + j, (*chip, c), me).wait_recv()
                fwd = copy(a, 4 + j, (*chip, c), sibling)
                fwd.start()
                passed.append(fwd)
        for a in range(n_arr):
            copy(a, 0, sibling, me).wait_recv()
            for j, chip in enumerate(chips):
                copy(a, 4 + j, (*chip, 1 - c), me).wait_recv()
        for cp in first + passed:
            cp.wait_send()
        for cp in mine:
            cp.wait()

    return pl.pallas_call(
        body, name=name, in_specs=[_ANY] * n_arr, out_specs=[_ANY] * n_arr,
        out_shape=[jax.ShapeDtypeStruct((N_DEV,) + s.shape, s.dtype) for s in shards],
        scratch_shapes=[pltpu.SemaphoreType.DMA((n_arr, N_DEV - 1)), pltpu.SemaphoreType.DMA((n_arr, N_DEV - 1)),
                        pltpu.SemaphoreType.DMA((n_arr,))],
    )(*shards)


def _adamw(parts, w, m, v, *, tr, name):
    depth, rows, cols = w.shape
    c1 = 1.0 - ADAM_B1 ** ADAM_STEP
    c2 = 1.0 - ADAM_B2 ** ADAM_STEP

    def body(g_ref, w_ref, m_ref, v_ref, go_ref, d_ref, mo_ref, vo_ref):
        g = g_ref[0, 0].astype(F32)
        for s in range(1, N_DEV):
            g = g + g_ref[0, s].astype(F32)
        m_new = ADAM_B1 * m_ref[0] + (1.0 - ADAM_B1) * g
        v_new = ADAM_B2 * v_ref[0] + (1.0 - ADAM_B2) * (g * g)
        go_ref[0] = g
        mo_ref[0] = m_new
        vo_ref[0] = v_new
        d_ref[0] = -ADAM_LR * ((m_new / c1) / (jnp.sqrt(v_new / c2) + ADAM_EPS) + ADAM_WD * w_ref[0])

    tile = pl.BlockSpec((1, tr, cols), lambda l, i: (l, i, 0))
    return pl.pallas_call(
        body, name=name, grid=(depth, rows // tr),
        in_specs=[pl.BlockSpec((1, N_DEV, tr, cols), lambda l, i: (l, 0, i, 0)), tile, tile, tile],
        out_specs=[tile] * 4, out_shape=[jax.ShapeDtypeStruct(w.shape, F32)] * 4,
        compiler_params=_cparams(("parallel", "parallel")),
    )(parts, w, m, v)


def _adamw_w_in(parts, w, m, v, *, name):
    c1 = 1.0 - ADAM_B1 ** ADAM_STEP
    c2 = 1.0 - ADAM_B2 ** ADAM_STEP

    def body(g_ref, w_ref, m_ref, v_ref, go_ref, d_ref, mo_ref, vo_ref):
        gs = []
        for l in range(DEPTH):
            g = g_ref[l, 0].astype(F32)
            for s in range(1, N_DEV):
                g = g + g_ref[l, s].astype(F32)
            gs.append(g)
        g = jnp.stack(gs, axis=1)
        m_new = ADAM_B1 * m_ref[...] + (1.0 - ADAM_B1) * g
        v_new = ADAM_B2 * v_ref[...] + (1.0 - ADAM_B2) * (g * g)
        go_ref[...] = g
        mo_ref[...] = m_new
        vo_ref[...] = v_new
        d_ref[...] = -ADAM_LR * ((m_new / c1) / (jnp.sqrt(v_new / c2) + ADAM_EPS) + ADAM_WD * w_ref[...])

    tile = pl.BlockSpec((SHARD_COLS, DEPTH, LANE), lambda i: (0, 0, i))
    return pl.pallas_call(
        body, name=name, grid=(D_MODEL // LANE,),
        in_specs=[pl.BlockSpec((DEPTH, N_DEV, SHARD_COLS, LANE), lambda i: (0, 0, 0, i)), tile, tile, tile],
        out_specs=[tile] * 4, out_shape=[jax.ShapeDtypeStruct(w.shape, F32)] * 4,
        compiler_params=_cparams(("parallel",)),
    )(parts, w, m, v)


def _pack_small(conv, small):
    lead = conv.shape[:-2]
    flat = jnp.concatenate([conv.reshape(lead + (CS_CONV,))] + list(small), axis=-1)
    pad = CS_ROWS * LANE - flat.shape[-1]
    flat = jnp.concatenate([flat, jnp.zeros(lead + (pad,), F32)], axis=-1)
    return flat.reshape(lead + (CS_ROWS, LANE))


def _unpack_small(p):
    flat = p.reshape(DEPTH, CS_ROWS * LANE)
    conv = flat[:, :CS_CONV].reshape(DEPTH, CONV_K, CONV_SHARD_COLS)
    small, off = [], CS_CONV
    for _, n in SMALL_SIZES:
        small.append(flat[:, off:off + n])
        off += n
    return conv, small


def kernel(x, w_in, conv_w, a_log, dt_bias, norm_w, sinks, w_out, ln_g, ln_b, loss_target, m_w_in, m_conv_w, m_a_log, m_dt_bias, m_norm_w, m_sinks, m_w_out, m_ln_g, m_ln_b, v_w_in, v_conv_w, v_a_log, v_dt_bias, v_norm_w, v_sinks, v_w_out, v_ln_g, v_ln_b):
    small = [a_log, dt_bias, norm_w, sinks, ln_g, ln_b]
    w_t, m_t, v_t = (a.transpose(2, 0, 1) for a in (w_in, m_w_in, v_w_in))
    shards = [[w_t[:, l].astype(BF16), w_out[l].astype(BF16), conv_w[l]] for l in range(DEPTH)]
    g_in0, = _all_gather(shards[0][:1], name="weights_all_gather_0")
    weights = [[_full_w_in(g_in0, "w_in_rows_0"), None, None]] + [[None, None, None]] * (DEPTH - 1)

    _, saved, weights = _forward(x[0], weights, shards, small)
    dx, g1, _, loss_lanes = _backward_layer(1, None, saved[1], weights[1], ln_g[1],
                                            loss=(loss_target[0], ln_b[1][None, :]))
    loss_part = 0.5 * jnp.sum(loss_lanes) * (1.0 / D_MODEL)
    dx, _, (p_in, p_out, p_small), _ = _backward_layer(0, dx, saved[0], weights[0], ln_g[0],
                                                       above=_contributions(g1, loss_part))
    loss = functools.reduce(lambda a, b: a + b, [p_small[1, s, CS_ROWS - 1, LANE - 1] for s in range(N_DEV)])

    o_in = [o.transpose(1, 2, 0) for o in _adamw_w_in(p_in, w_t, m_t, v_t, name="adamw_w_in")]
    o_out = _adamw(p_out, w_out, m_w_out, v_w_out, tr=OUT_SHARD_ROWS, name="adamw_w_out")
    o_small = _adamw(p_small, _pack_small(conv_w, small),
                     _pack_small(m_conv_w, [m_a_log, m_dt_bias, m_norm_w, m_sinks, m_ln_g, m_ln_b]),
                     _pack_small(v_conv_w, [v_a_log, v_dt_bias, v_norm_w, v_sinks, v_ln_g, v_ln_b]),
                     tr=CS_ROWS, name="adamw_small")
    outs = []
    for k in range(4):
        cv, sm = _unpack_small(o_small[k])
        outs += [o_in[k], cv, sm[0], sm[1], sm[2], sm[3], o_out[k], sm[4], sm[5]]
    return (loss, dx[None], *outs)
```
